```python
import jax, jax.numpy as jnp
from jax import lax
import numpy as np

D_MODEL = 1024
BATCH = 8
SEQ = 2048
DEPTH = 1

CHUNK = 64
D_CONV = 512
CONV_WIDTH = 31
N_HEADS = 8
HEAD_DIM = 64
D_ATTN = N_HEADS * HEAD_DIM
D_MIX = D_CONV + D_ATTN
Q_BLOCK = 128
D_FF = 2816
EPS = 1e-6
N_IN = 2 * D_CONV + 3 * D_ATTN + N_HEADS

kernel_name = "hybrid_conv_fox_macaron_block"


def rms_norm(x, g):
    xf = x.astype(jnp.float32)
    y = xf * lax.rsqrt(jnp.mean(xf * xf, axis=-1, keepdims=True) + EPS)
    return (y * g.astype(jnp.float32)).astype(x.dtype)


def layer_norm(x, g, b):
    xf = x.astype(jnp.float32)
    mu = jnp.mean(xf, axis=-1, keepdims=True)
    xc = xf - mu
    y = xc * lax.rsqrt(jnp.mean(xc * xc, axis=-1, keepdims=True) + EPS)
    return (y * g.astype(jnp.float32) + b.astype(jnp.float32)).astype(x.dtype)


def swiglu_ffn(h, w13, w2):
    gate, up = jnp.split(h @ w13, 2, axis=-1)
    return (jax.nn.silu(gate) * up) @ w2


def conv_module(a, g, conv_w, conv_b, ln_g, ln_b):
    u = a * jax.nn.sigmoid(g)
    u_pad = jnp.pad(u, ((0, 0), (CONV_WIDTH - 1, 0), (0, 0)))
    y = lax.conv_general_dilated(
        u_pad, conv_w[:, None, :].astype(u.dtype), window_strides=(1,), padding="VALID",
        dimension_numbers=("NWC", "WIO", "NWC"), feature_group_count=D_CONV)
    y = y + conv_b.astype(y.dtype)
    y = layer_norm(y, ln_g, ln_b)
    return jax.nn.silu(y)


def forgetting_attention(q, k, v, f_logit):
    seq = q.shape[1]
    log_f = jax.nn.log_sigmoid(f_logit.astype(jnp.float32))
    cum = jnp.cumsum(log_f, axis=1).transpose(0, 2, 1)
    scale = HEAD_DIM ** -0.5
    outs = []
    for i in range(seq // Q_BLOCK):
        q0, q1 = i * Q_BLOCK, (i + 1) * Q_BLOCK
        qb, kb, vb = q[:, q0:q1], k[:, :q1], v[:, :q1]
        s = jnp.einsum("bqhd,bkhd->bhqk", qb, kb, preferred_element_type=jnp.float32) * scale
        s = s + cum[:, :, q0:q1, None] - cum[:, :, None, :q1]
        qpos = jnp.arange(q0, q1)
        kpos = jnp.arange(q1)
        s = jnp.where(kpos[None, :] <= qpos[:, None], s, -jnp.inf)
        p = jax.nn.softmax(s, axis=-1)
        outs.append(jnp.einsum("bhqk,bkhd->bqhd", p.astype(vb.dtype), vb))
    return jnp.concatenate(outs, axis=1)


def _fwd_setup_inputs(seed: int = 0) -> dict:
    key = jax.random.key(seed)
    ks = jax.random.split(key, 24)
    f32 = jnp.float32

    def nrm(k, shape, scale):
        return jax.random.normal(k, shape, f32) * scale

    def gain(k, shape):
        return 1.0 + 0.05 * jax.random.normal(k, shape, f32)

    L = DEPTH
    return {
        "x": jax.random.normal(ks[0], (BATCH, SEQ, D_MODEL), f32),
        "ffn1_norm": gain(ks[1], (L, D_MODEL)),
        "ffn1_w13": nrm(ks[2], (L, D_MODEL, 2 * D_FF), D_MODEL ** -0.5),
        "ffn1_w2": nrm(ks[3], (L, D_FF, D_MODEL), D_FF ** -0.5),
        "mix_norm": gain(ks[4], (L, D_MODEL)),
        "w_in": nrm(ks[5], (L, D_MODEL, N_IN), D_MODEL ** -0.5),
        "conv_w": nrm(ks[6], (L, CONV_WIDTH, D_CONV), CONV_WIDTH ** -0.5),
        "conv_b": nrm(ks[7], (L, D_CONV), 0.02),
        "conv_ln_g": gain(ks[8], (L, D_CONV)),
        "conv_ln_b": nrm(ks[9], (L, D_CONV), 0.02),
        "forget_b": jax.random.uniform(ks[10], (L, N_HEADS), f32, minval=1.0, maxval=4.0),
        "out_norm_conv": gain(ks[11], (L, D_CONV)),
        "out_norm_attn": gain(ks[12], (L, D_ATTN)),
        "w_out": nrm(ks[13], (L, D_MIX, D_MODEL), D_MIX ** -0.5),
        "ffn2_norm": gain(ks[14], (L, D_MODEL)),
        "ffn2_w13": nrm(ks[15], (L, D_MODEL, 2 * D_FF), D_MODEL ** -0.5),
        "ffn2_w2": nrm(ks[16], (L, D_FF, D_MODEL), D_FF ** -0.5),
        "final_norm": gain(ks[17], (D_MODEL,)),
    }


def _fwd_reference(x, ffn1_norm, ffn1_w13, ffn1_w2, mix_norm, w_in, conv_w, conv_b, conv_ln_g,
              conv_ln_b, forget_b, out_norm_conv, out_norm_attn, w_out, ffn2_norm, ffn2_w13,
              ffn2_w2, final_norm):
    bsz, seq, _ = x.shape
    splits = [D_CONV, 2 * D_CONV, 2 * D_CONV + D_ATTN, 2 * D_CONV + 2 * D_ATTN,
              2 * D_CONV + 3 * D_ATTN]
    for l in range(DEPTH):
        x = x + 0.5 * swiglu_ffn(rms_norm(x, ffn1_norm[l]), ffn1_w13[l], ffn1_w2[l])

        h = rms_norm(x, mix_norm[l])
        proj = h @ w_in[l]
        a, g, q, k, v, fl = jnp.split(proj, splits, axis=-1)

        y_conv = conv_module(a, g, conv_w[l], conv_b[l], conv_ln_g[l], conv_ln_b[l])

        heads = (bsz, seq, N_HEADS, HEAD_DIM)
        y_attn = forgetting_attention(q.reshape(heads), k.reshape(heads), v.reshape(heads),
                                      fl + forget_b[l].astype(fl.dtype))
        y_attn = y_attn.reshape(bsz, seq, D_ATTN)

        y = jnp.concatenate([rms_norm(y_conv, out_norm_conv[l]),
                             rms_norm(y_attn, out_norm_attn[l])], axis=-1)
        x = x + y @ w_out[l]

        x = x + 0.5 * swiglu_ffn(rms_norm(x, ffn2_norm[l]), ffn2_w13[l], ffn2_w2[l])
    return rms_norm(x, final_norm)


import jax as _jax
import jax.numpy as _jnp

TWIN_FORMAT = 'train_step'
FWD_PARAMS = ['x', 'ffn1_norm', 'ffn1_w13', 'ffn1_w2', 'mix_norm', 'w_in', 'conv_w', 'conv_b', 'conv_ln_g', 'conv_ln_b', 'forget_b', 'out_norm_conv', 'out_norm_attn', 'w_out', 'ffn2_norm', 'ffn2_w13', 'ffn2_w2', 'final_norm']
TWIN_WEIGHTS = ['ffn1_norm', 'ffn1_w13', 'ffn1_w2', 'mix_norm', 'w_in', 'conv_w', 'conv_b', 'conv_ln_g', 'conv_ln_b', 'forget_b', 'out_norm_conv', 'out_norm_attn', 'w_out', 'ffn2_norm', 'ffn2_w13', 'ffn2_w2', 'final_norm']
TWIN_DIFF_INPUT = 'x'
TWIN_INPUTS = ['x', 'ffn1_norm', 'ffn1_w13', 'ffn1_w2', 'mix_norm', 'w_in', 'conv_w', 'conv_b', 'conv_ln_g', 'conv_ln_b', 'forget_b', 'out_norm_conv', 'out_norm_attn', 'w_out', 'ffn2_norm', 'ffn2_w13', 'ffn2_w2', 'final_norm', 'loss_target', 'm_ffn1_norm', 'm_ffn1_w13', 'm_ffn1_w2', 'm_mix_norm', 'm_w_in', 'm_conv_w', 'm_conv_b', 'm_conv_ln_g', 'm_conv_ln_b', 'm_forget_b', 'm_out_norm_conv', 'm_out_norm_attn', 'm_w_out', 'm_ffn2_norm', 'm_ffn2_w13', 'm_ffn2_w2', 'm_final_norm', 'v_ffn1_norm', 'v_ffn1_w13', 'v_ffn1_w2', 'v_mix_norm', 'v_w_in', 'v_conv_w', 'v_conv_b', 'v_conv_ln_g', 'v_conv_ln_b', 'v_forget_b', 'v_out_norm_conv', 'v_out_norm_attn', 'v_w_out', 'v_ffn2_norm', 'v_ffn2_w13', 'v_ffn2_w2', 'v_final_norm']
TWIN_OUTPUTS = ['loss', 'grad_x', 'grad_ffn1_norm', 'grad_ffn1_w13', 'grad_ffn1_w2', 'grad_mix_norm', 'grad_w_in', 'grad_conv_w', 'grad_conv_b', 'grad_conv_ln_g', 'grad_conv_ln_b', 'grad_forget_b', 'grad_out_norm_conv', 'grad_out_norm_attn', 'grad_w_out', 'grad_ffn2_norm', 'grad_ffn2_w13', 'grad_ffn2_w2', 'grad_final_norm', 'delta_ffn1_norm', 'delta_ffn1_w13', 'delta_ffn1_w2', 'delta_mix_norm', 'delta_w_in', 'delta_conv_w', 'delta_conv_b', 'delta_conv_ln_g', 'delta_conv_ln_b', 'delta_forget_b', 'delta_out_norm_conv', 'delta_out_norm_attn', 'delta_w_out', 'delta_ffn2_norm', 'delta_ffn2_w13', 'delta_ffn2_w2', 'delta_final_norm', 'new_m_ffn1_norm', 'new_m_ffn1_w13', 'new_m_ffn1_w2', 'new_m_mix_norm', 'new_m_w_in', 'new_m_conv_w', 'new_m_conv_b', 'new_m_conv_ln_g', 'new_m_conv_ln_b', 'new_m_forget_b', 'new_m_out_norm_conv', 'new_m_out_norm_attn', 'new_m_w_out', 'new_m_ffn2_norm', 'new_m_ffn2_w13', 'new_m_ffn2_w2', 'new_m_final_norm', 'new_v_ffn1_norm', 'new_v_ffn1_w13', 'new_v_ffn1_w2', 'new_v_mix_norm', 'new_v_w_in', 'new_v_conv_w', 'new_v_conv_b', 'new_v_conv_ln_g', 'new_v_conv_ln_b', 'new_v_forget_b', 'new_v_out_norm_conv', 'new_v_out_norm_attn', 'new_v_w_out', 'new_v_ffn2_norm', 'new_v_ffn2_w13', 'new_v_ffn2_w2', 'new_v_final_norm']
TWIN_LEAF_KINDS = {'loss': 'loss', 'grad_x': 'grad_x', 'grad_ffn1_norm': 'grad_w', 'grad_ffn1_w13': 'grad_w', 'grad_ffn1_w2': 'grad_w', 'grad_mix_norm': 'grad_w', 'grad_w_in': 'grad_w', 'grad_conv_w': 'grad_w', 'grad_conv_b': 'grad_w', 'grad_conv_ln_g': 'grad_w', 'grad_conv_ln_b': 'grad_w', 'grad_forget_b': 'grad_w', 'grad_out_norm_conv': 'grad_w', 'grad_out_norm_attn': 'grad_w', 'grad_w_out': 'grad_w', 'grad_ffn2_norm': 'grad_w', 'grad_ffn2_w13': 'grad_w', 'grad_ffn2_w2': 'grad_w', 'grad_final_norm': 'grad_w', 'delta_ffn1_norm': 'delta_w', 'delta_ffn1_w13': 'delta_w', 'delta_ffn1_w2': 'delta_w', 'delta_mix_norm': 'delta_w', 'delta_w_in': 'delta_w', 'delta_conv_w': 'delta_w', 'delta_conv_b': 'delta_w', 'delta_conv_ln_g': 'delta_w', 'delta_conv_ln_b': 'delta_w', 'delta_forget_b': 'delta_w', 'delta_out_norm_conv': 'delta_w', 'delta_out_norm_attn': 'delta_w', 'delta_w_out': 'delta_w', 'delta_ffn2_norm': 'delta_w', 'delta_ffn2_w13': 'delta_w', 'delta_ffn2_w2': 'delta_w', 'delta_final_norm': 'delta_w', 'new_m_ffn1_norm': 'new_m', 'new_m_ffn1_w13': 'new_m', 'new_m_ffn1_w2': 'new_m', 'new_m_mix_norm': 'new_m', 'new_m_w_in': 'new_m', 'new_m_conv_w': 'new_m', 'new_m_conv_b': 'new_m', 'new_m_conv_ln_g': 'new_m', 'new_m_conv_ln_b': 'new_m', 'new_m_forget_b': 'new_m', 'new_m_out_norm_conv': 'new_m', 'new_m_out_norm_attn': 'new_m', 'new_m_w_out': 'new_m', 'new_m_ffn2_norm': 'new_m', 'new_m_ffn2_w13': 'new_m', 'new_m_ffn2_w2': 'new_m', 'new_m_final_norm': 'new_m', 'new_v_ffn1_norm': 'new_v', 'new_v_ffn1_w13': 'new_v', 'new_v_ffn1_w2': 'new_v', 'new_v_mix_norm': 'new_v', 'new_v_w_in': 'new_v', 'new_v_conv_w': 'new_v', 'new_v_conv_b': 'new_v', 'new_v_conv_ln_g': 'new_v', 'new_v_conv_ln_b': 'new_v', 'new_v_forget_b': 'new_v', 'new_v_out_norm_conv': 'new_v', 'new_v_out_norm_attn': 'new_v', 'new_v_w_out': 'new_v', 'new_v_ffn2_norm': 'new_v', 'new_v_ffn2_w13': 'new_v', 'new_v_ffn2_w2': 'new_v', 'new_v_final_norm': 'new_v'}


def _forward(args):
    return _fwd_reference(*[args[k] for k in FWD_PARAMS])


def _output_shape():
    out = _jax.eval_shape(lambda: _forward(_fwd_setup_inputs(0)))
    return out.shape, out.dtype

N_MICROBATCH = 1
ADAM_LR = 0.001
ADAM_B1 = 0.9
ADAM_B2 = 0.999
ADAM_EPS = 1e-08
ADAM_WD = 0.01
ADAM_STEP = 10
PER_EXAMPLE_BATCH_AXIS = {'x': 0, 'loss_target': 0}
SHARED_INPUTS = []
_WEIGHT_DTYPES = {'ffn1_norm': _jnp.float32, 'ffn1_w13': _jnp.float32, 'ffn1_w2': _jnp.float32, 'mix_norm': _jnp.float32, 'w_in': _jnp.float32, 'conv_w': _jnp.float32, 'conv_b': _jnp.float32, 'conv_ln_g': _jnp.float32, 'conv_ln_b': _jnp.float32, 'forget_b': _jnp.float32, 'out_norm_conv': _jnp.float32, 'out_norm_attn': _jnp.float32, 'w_out': _jnp.float32, 'ffn2_norm': _jnp.float32, 'ffn2_w13': _jnp.float32, 'ffn2_w2': _jnp.float32, 'final_norm': _jnp.float32}
MOMENT_SCALE = {'ffn1_norm': 6.458663e-02, 'ffn1_w13': 2.648451e-02, 'ffn1_w2': 4.314522e-02, 'mix_norm': 1.147383e-01, 'w_in': 7.542637e-02, 'conv_w': 9.437324e-02, 'conv_b': 2.788769e-01, 'conv_ln_g': 1.452993e-01, 'conv_ln_b': 1.608622e-01, 'forget_b': 6.566169e-01, 'out_norm_conv': 1.074243e-01, 'out_norm_attn': 9.425836e-02, 'w_out': 9.627538e-02, 'ffn2_norm': 3.712195e-02, 'ffn2_w13': 1.576388e-02, 'ffn2_w2': 2.584905e-02, 'final_norm': 1.605448e+01}


def _to_microbatches(a, axis):
    t = _jnp.moveaxis(a, axis, 0)
    t = t.reshape((N_MICROBATCH, t.shape[0] // N_MICROBATCH) + t.shape[1:])
    return _jnp.moveaxis(t, 1, axis + 1)


def setup_inputs(seed: int = 0) -> dict:
    inp = _fwd_setup_inputs(seed)
    key = _jax.random.fold_in(_jax.random.key(seed), 7919)
    shape, _ = _output_shape()
    out = dict(inp)
    out["loss_target"] = _jax.random.normal(_jax.random.fold_in(key, 0), shape, _jnp.float32)
    for i, name in enumerate(TWIN_WEIGHTS):
        w = inp[name].astype(_jnp.float32)
        if MOMENT_SCALE is None:
            s = _jnp.sqrt(_jnp.mean(_jnp.square(w)) + 1e-30)
        else:
            s = MOMENT_SCALE[name]
        km, kv = _jax.random.split(_jax.random.fold_in(key, i + 1))
        out[name] = w
        out["m_" + name] = s * _jax.random.normal(km, w.shape, _jnp.float32)
        out["v_" + name] = (s * s) * _jax.random.uniform(kv, w.shape, _jnp.float32, 0.5, 1.5)
    if N_MICROBATCH > 1:
        for name, axis in PER_EXAMPLE_BATCH_AXIS.items():
            out[name] = _to_microbatches(out[name], axis)
    return {'x': out['x'], 'ffn1_norm': out['ffn1_norm'], 'ffn1_w13': out['ffn1_w13'], 'ffn1_w2': out['ffn1_w2'], 'mix_norm': out['mix_norm'], 'w_in': out['w_in'], 'conv_w': out['conv_w'], 'conv_b': out['conv_b'], 'conv_ln_g': out['conv_ln_g'], 'conv_ln_b': out['conv_ln_b'], 'forget_b': out['forget_b'], 'out_norm_conv': out['out_norm_conv'], 'out_norm_attn': out['out_norm_attn'], 'w_out': out['w_out'], 'ffn2_norm': out['ffn2_norm'], 'ffn2_w13': out['ffn2_w13'], 'ffn2_w2': out['ffn2_w2'], 'final_norm': out['final_norm'], 'loss_target': out['loss_target'], 'm_ffn1_norm': out['m_ffn1_norm'], 'm_ffn1_w13': out['m_ffn1_w13'], 'm_ffn1_w2': out['m_ffn1_w2'], 'm_mix_norm': out['m_mix_norm'], 'm_w_in': out['m_w_in'], 'm_conv_w': out['m_conv_w'], 'm_conv_b': out['m_conv_b'], 'm_conv_ln_g': out['m_conv_ln_g'], 'm_conv_ln_b': out['m_conv_ln_b'], 'm_forget_b': out['m_forget_b'], 'm_out_norm_conv': out['m_out_norm_conv'], 'm_out_norm_attn': out['m_out_norm_attn'], 'm_w_out': out['m_w_out'], 'm_ffn2_norm': out['m_ffn2_norm'], 'm_ffn2_w13': out['m_ffn2_w13'], 'm_ffn2_w2': out['m_ffn2_w2'], 'm_final_norm': out['m_final_norm'], 'v_ffn1_norm': out['v_ffn1_norm'], 'v_ffn1_w13': out['v_ffn1_w13'], 'v_ffn1_w2': out['v_ffn1_w2'], 'v_mix_norm': out['v_mix_norm'], 'v_w_in': out['v_w_in'], 'v_conv_w': out['v_conv_w'], 'v_conv_b': out['v_conv_b'], 'v_conv_ln_g': out['v_conv_ln_g'], 'v_conv_ln_b': out['v_conv_ln_b'], 'v_forget_b': out['v_forget_b'], 'v_out_norm_conv': out['v_out_norm_conv'], 'v_out_norm_attn': out['v_out_norm_attn'], 'v_w_out': out['v_w_out'], 'v_ffn2_norm': out['v_ffn2_norm'], 'v_ffn2_w13': out['v_ffn2_w13'], 'v_ffn2_w2': out['v_ffn2_w2'], 'v_final_norm': out['v_final_norm']}


def _loss(weights, diff, rest, loss_target):
    with _jax.named_scope("forward"):
        args = {**rest, TWIN_DIFF_INPUT: diff, **{k: w.astype(_WEIGHT_DTYPES[k]) for k, w in weights.items()}}
        y = _forward(args)
    with _jax.named_scope("loss_head"):
        err = _jnp.square(y.astype(_jnp.float32) - loss_target)
        return 0.5 * _jnp.sum(_jnp.mean(err, axis=-1)) if err.ndim else 0.5 * err


def _adamw(w, g, m, v):
    m = ADAM_B1 * m + (1.0 - ADAM_B1) * g
    v = ADAM_B2 * v + (1.0 - ADAM_B2) * _jnp.square(g)
    m_hat = m / (1.0 - ADAM_B1 ** ADAM_STEP)
    v_hat = v / (1.0 - ADAM_B2 ** ADAM_STEP)
    delta = -ADAM_LR * (m_hat / (_jnp.sqrt(v_hat) + ADAM_EPS) + ADAM_WD * w)
    return delta, m, v


def reference(x, ffn1_norm, ffn1_w13, ffn1_w2, mix_norm, w_in, conv_w, conv_b, conv_ln_g, conv_ln_b, forget_b, out_norm_conv, out_norm_attn, w_out, ffn2_norm, ffn2_w13, ffn2_w2, final_norm, loss_target, m_ffn1_norm, m_ffn1_w13, m_ffn1_w2, m_mix_norm, m_w_in, m_conv_w, m_conv_b, m_conv_ln_g, m_conv_ln_b, m_forget_b, m_out_norm_conv, m_out_norm_attn, m_w_out, m_ffn2_norm, m_ffn2_w13, m_ffn2_w2, m_final_norm, v_ffn1_norm, v_ffn1_w13, v_ffn1_w2, v_mix_norm, v_w_in, v_conv_w, v_conv_b, v_conv_ln_g, v_conv_ln_b, v_forget_b, v_out_norm_conv, v_out_norm_attn, v_w_out, v_ffn2_norm, v_ffn2_w13, v_ffn2_w2, v_final_norm):
    given = dict(x=x, ffn1_norm=ffn1_norm, ffn1_w13=ffn1_w13, ffn1_w2=ffn1_w2, mix_norm=mix_norm, w_in=w_in, conv_w=conv_w, conv_b=conv_b, conv_ln_g=conv_ln_g, conv_ln_b=conv_ln_b, forget_b=forget_b, out_norm_conv=out_norm_conv, out_norm_attn=out_norm_attn, w_out=w_out, ffn2_norm=ffn2_norm, ffn2_w13=ffn2_w13, ffn2_w2=ffn2_w2, final_norm=final_norm, loss_target=loss_target, m_ffn1_norm=m_ffn1_norm, m_ffn1_w13=m_ffn1_w13, m_ffn1_w2=m_ffn1_w2, m_mix_norm=m_mix_norm, m_w_in=m_w_in, m_conv_w=m_conv_w, m_conv_b=m_conv_b, m_conv_ln_g=m_conv_ln_g, m_conv_ln_b=m_conv_ln_b, m_forget_b=m_forget_b, m_out_norm_conv=m_out_norm_conv, m_out_norm_attn=m_out_norm_attn, m_w_out=m_w_out, m_ffn2_norm=m_ffn2_norm, m_ffn2_w13=m_ffn2_w13, m_ffn2_w2=m_ffn2_w2, m_final_norm=m_final_norm, v_ffn1_norm=v_ffn1_norm, v_ffn1_w13=v_ffn1_w13, v_ffn1_w2=v_ffn1_w2, v_mix_norm=v_mix_norm, v_w_in=v_w_in, v_conv_w=v_conv_w, v_conv_b=v_conv_b, v_conv_ln_g=v_conv_ln_g, v_conv_ln_b=v_conv_ln_b, v_forget_b=v_forget_b, v_out_norm_conv=v_out_norm_conv, v_out_norm_attn=v_out_norm_attn, v_w_out=v_w_out, v_ffn2_norm=v_ffn2_norm, v_ffn2_w13=v_ffn2_w13, v_ffn2_w2=v_ffn2_w2, v_final_norm=v_final_norm)
    weights = {n: given[n] for n in TWIN_WEIGHTS}
    shared = {n: given[n] for n in SHARED_INPUTS}
    per_example = {n: given[n] for n in ['x']}
    grad_fn = _jax.value_and_grad(_loss, argnums=(0, 1))

    def one_microbatch(ex, loss_target):
        ex = dict(ex)
        diff = ex.pop(TWIN_DIFF_INPUT)
        return grad_fn(weights, diff, {**shared, **ex}, loss_target)

    if N_MICROBATCH == 1:
        loss, (grad_w, grad_x) = one_microbatch(per_example, given["loss_target"])
    else:
        def body(carry, xs):
            loss_sum, grad_sum = carry
            l_k, (gw_k, gx_k) = one_microbatch(xs[0], xs[1])
            with _jax.named_scope("update"):
                return (loss_sum + l_k, _jax.tree.map(_jnp.add, grad_sum, gw_k)), gx_k

        init = (_jnp.zeros((), _jnp.float32), _jax.tree.map(_jnp.zeros_like, weights))
        (loss, grad_w), grad_x = _jax.lax.scan(body, init, (per_example, given["loss_target"]))
    with _jax.named_scope("update"):
        delta_w, new_m, new_v = {}, {}, {}
        for n in TWIN_WEIGHTS:
            delta_w[n], new_m[n], new_v[n] = _adamw(weights[n], grad_w[n], given["m_" + n], given["v_" + n])
    return (loss, grad_x, *[grad_w[n] for n in TWIN_WEIGHTS], *[delta_w[n] for n in TWIN_WEIGHTS],
            *[new_m[n] for n in TWIN_WEIGHTS], *[new_v[n] for n in TWIN_WEIGHTS])
```

```python
import functools

import jax
import jax.numpy as jnp
from jax import lax
from jax.experimental import pallas as pl
from jax.experimental.pallas import tpu as pltpu

F32 = jnp.float32
MXU = jnp.bfloat16
EPS = 1e-6
N_HEADS = 8
HEAD_DIM = 64
D_CONV = 512
D_ATTN = N_HEADS * HEAD_DIM
CONV_TAPS = 31
CONV_HALO = 32
SCALE = HEAD_DIM ** -0.5
NEG = -1e30
LANES = 128
N_DEV = 8
MESH = pl.DeviceIdType.MESH
MIB = 1 << 20

ADAM_LR = 0.001
ADAM_B1 = 0.9
ADAM_B2 = 0.999
ADAM_EPS = 1e-08
ADAM_WD = 0.01
ADAM_STEP = 10


def _params(vmem_mib, n_axes):
    return pltpu.CompilerParams(dimension_semantics=("arbitrary",) * n_axes, vmem_limit_bytes=vmem_mib * MIB)


def _mm(a, b):
    return jnp.dot(a, b, preferred_element_type=F32)


def _mm_nt(a, b):
    return lax.dot_general(a, b, (((1,), (1,)), ((), ())), preferred_element_type=F32)


def _mm_tn(a, b):
    return lax.dot_general(a, b, (((0,), (0,)), ((), ())), preferred_element_type=F32)


def _rms_fwd(x, g):
    r = lax.rsqrt(jnp.mean(x * x, axis=-1, keepdims=True) + EPS)
    return x * r * g, r


def _rms_bwd(x, r, g, dy):
    gdy = dy * g
    dx = r * gdy - x * (r * r * r) * jnp.mean(x * gdy, axis=-1, keepdims=True)
    dg = jnp.sum(dy * x * r, axis=0, keepdims=True)
    return dx, dg


def _silu_grad(z, sz):
    return sz * (1.0 + z * (1.0 - sz))


def _exact_tri_dot(tri, x):
    x1 = x.astype(jnp.bfloat16)
    r1 = x - x1.astype(F32)
    x2 = r1.astype(jnp.bfloat16)
    x3 = (r1 - x2.astype(F32)).astype(jnp.bfloat16)
    return _mm(tri, x1) + _mm(tri, x2) + _mm(tri, x3)


def _lanes_from_cols(cols, rows):
    lane = lax.broadcasted_iota(jnp.int32, (rows, LANES), 1)
    out = jnp.zeros((rows, LANES), F32)
    for h, col in enumerate(cols):
        out = jnp.where(lane == h, col, out)
    return out


def _tile(n, want):
    t = min(n, want)
    assert n % t == 0
    return t


def _ffn_fwd(x, g, w13, w2, name):
    T, D = x.shape
    _, J, _, bf = w13.shape
    tm = _tile(T, 512)
    I = T // tm

    def body(x_ref, g_ref, w13_ref, w2_ref, xo_ref, G_ref, U_ref, acc_s):
        j = pl.program_id(0)
        i = pl.program_id(1)
        rows = pl.ds(pl.multiple_of(i * tm, tm), tm)
        xv = x_ref[...]
        h, _ = _rms_fwd(xv, g_ref[...])
        hb = h.astype(MXU)
        G = _mm(hb, w13_ref[0])
        U = _mm(hb, w13_ref[1])
        G_ref[...] = G.astype(MXU)
        U_ref[...] = U.astype(MXU)
        A = (G * jax.nn.sigmoid(G) * U).astype(MXU)
        F = _mm(A, w2_ref[...])

        @pl.when(j == 0)
        def _():
            acc_s[rows, :] = xv + 0.5 * F

        @pl.when(j > 0)
        def _():
            acc_s[rows, :] += 0.5 * F

        @pl.when(j == J - 1)
        def _():
            xo_ref[...] = acc_s[rows, :]

    return pl.pallas_call(
        body, name=name, grid=(J, I),
        in_specs=[
            pl.BlockSpec((tm, D), lambda j, i: (i, 0)),
            pl.BlockSpec((1, D), lambda j, i: (0, 0)),
            pl.BlockSpec((2, None, D, bf), lambda j, i: (0, j, 0, 0)),
            pl.BlockSpec((bf, D), lambda j, i: (j, 0)),
        ],
        out_specs=[
            pl.BlockSpec((tm, D), lambda j, i: (jnp.where(j == J - 1, i, 0), 0)),
            pl.BlockSpec((None, tm, bf), lambda j, i: (j, i, 0)),
            pl.BlockSpec((None, tm, bf), lambda j, i: (j, i, 0)),
        ],
        out_shape=[
            jax.ShapeDtypeStruct((T, D), F32),
            jax.ShapeDtypeStruct((J, T, bf), MXU),
            jax.ShapeDtypeStruct((J, T, bf), MXU),
        ],
        scratch_shapes=[pltpu.VMEM((T, D), F32)],
        compiler_params=_params(48, 2),
    )(x, g, w13, w2)


def _ffn_bwd(x, g, dy, Gs, Us, w13, w2, name):
    T, D = x.shape
    _, J, _, bf = w13.shape
    tm = _tile(T, 256)
    I = T // tm

    def body(x_ref, g_ref, dy_ref, G_ref, U_ref, w13_ref, w2_ref,
             dx_ref, dw13_ref, dw2_ref, dg_ref, dh_s, a13_s, a2_s):
        j = pl.program_id(0)
        i = pl.program_id(1)
        rows = pl.ds(pl.multiple_of(i * tm, tm), tm)
        xv = x_ref[...]
        gv = g_ref[...]
        h, r = _rms_fwd(xv, gv)
        hb = h.astype(MXU)
        dyv = dy_ref[...]
        dFb = (0.5 * dyv).astype(MXU)
        G = G_ref[...].astype(F32)
        U = U_ref[...].astype(F32)
        sg = jax.nn.sigmoid(G)
        s = G * sg
        A = (s * U).astype(MXU)
        dA = _mm_nt(dFb, w2_ref[...])
        dUb = (dA * s).astype(MXU)
        dGb = (dA * U * _silu_grad(G, sg)).astype(MXU)
        dw2 = _mm_tn(A, dFb)
        dwg = _mm_tn(hb, dGb)
        dwu = _mm_tn(hb, dUb)
        dh = _mm_nt(dGb, w13_ref[0]) + _mm_nt(dUb, w13_ref[1])

        @pl.when(i == 0)
        def _():
            a13_s[0] = dwg
            a13_s[1] = dwu
            a2_s[...] = dw2

        @pl.when(i > 0)
        def _():
            a13_s[0] += dwg
            a13_s[1] += dwu
            a2_s[...] += dw2

        @pl.when(i == I - 1)
        def _():
            dw13_ref[...] = a13_s[...].astype(dw13_ref.dtype)
            dw2_ref[...] = a2_s[...].astype(dw2_ref.dtype)

        @pl.when(j == 0)
        def _():
            dh_s[rows, :] = dh

        @pl.when(j > 0)
        def _():
            dh_s[rows, :] += dh

        @pl.when(j == J - 1)
        def _():
            dxn, dgp = _rms_bwd(xv, r, gv, dh_s[rows, :])
            dx_ref[...] = dyv + dxn

            @pl.when(i == 0)
            def _():
                dg_ref[...] = dgp

            @pl.when(i > 0)
            def _():
                dg_ref[...] += dgp

    return pl.pallas_call(
        body, name=name, grid=(J, I),
        in_specs=[
            pl.BlockSpec((tm, D), lambda j, i: (i, 0)),
            pl.BlockSpec((1, D), lambda j, i: (0, 0)),
            pl.BlockSpec((tm, D), lambda j, i: (i, 0)),
            pl.BlockSpec((None, tm, bf), lambda j, i: (j, i, 0)),
            pl.BlockSpec((None, tm, bf), lambda j, i: (j, i, 0)),
            pl.BlockSpec((2, None, D, bf), lambda j, i: (0, j, 0, 0)),
            pl.BlockSpec((bf, D), lambda j, i: (j, 0)),
        ],
        out_specs=[
            pl.BlockSpec((tm, D), lambda j, i: (jnp.where(j == J - 1, i, 0), 0)),
            pl.BlockSpec((2, None, D, bf), lambda j, i: (0, j, 0, 0)),
            pl.BlockSpec((bf, D), lambda j, i: (j, 0)),
            pl.BlockSpec((1, D), lambda j, i: (0, 0)),
        ],
        out_shape=[
            jax.ShapeDtypeStruct((T, D), F32),
            jax.ShapeDtypeStruct(w13.shape, MXU),
            jax.ShapeDtypeStruct(w2.shape, MXU),
            jax.ShapeDtypeStruct((1, D), F32),
        ],
        scratch_shapes=[pltpu.VMEM((T, D), F32), pltpu.VMEM((2, D, bf), F32), pltpu.VMEM((bf, D), F32)],
        compiler_params=_params(58, 2),
    )(x, g, dy, Gs, Us, w13, w2)


_AG0, _Q0, _K0, _V0, _F0 = 0, 2 * D_CONV, 2 * D_CONV + D_ATTN, 2 * D_CONV + 2 * D_ATTN, 2 * D_CONV + 3 * D_ATTN
N_IN = _F0 + N_HEADS
N_IN_PAD = _F0 + LANES


def _inproj_fwd(x1, gm, winp, name):
    T, D = x1.shape
    tm = _tile(T, 256)

    def body(x_ref, g_ref, w_ref, ag_ref, q_ref, k_ref, v_ref, fl_ref):
        h, _ = _rms_fwd(x_ref[...], g_ref[...])
        hb = h.astype(MXU)
        ag_ref[...] = _mm(hb, w_ref[:, _AG0:_Q0])
        q_ref[...] = _mm(hb, w_ref[:, _Q0:_K0]).astype(MXU)
        k_ref[...] = _mm(hb, w_ref[:, _K0:_V0]).astype(MXU)
        v_ref[...] = _mm(hb, w_ref[:, _V0:_F0]).astype(MXU)
        fl_ref[...] = _mm(hb, w_ref[:, _F0:N_IN_PAD])

    row = lambda w: pl.BlockSpec((tm, w), lambda i: (i, 0))
    return pl.pallas_call(
        body, name=name, grid=(T // tm,),
        in_specs=[row(D), pl.BlockSpec((1, D), lambda i: (0, 0)), pl.BlockSpec((D, N_IN_PAD), lambda i: (0, 0))],
        out_specs=[row(2 * D_CONV), row(D_ATTN), row(D_ATTN), row(D_ATTN), row(LANES)],
        out_shape=[
            jax.ShapeDtypeStruct((T, 2 * D_CONV), F32),
            jax.ShapeDtypeStruct((T, D_ATTN), MXU),
            jax.ShapeDtypeStruct((T, D_ATTN), MXU),
            jax.ShapeDtypeStruct((T, D_ATTN), MXU),
            jax.ShapeDtypeStruct((T, LANES), F32),
        ],
        compiler_params=_params(40, 1),
    )(x1, gm, winp)


def _inproj_bwd(x1, gm, dx2, dag, dq, dk, dv, dfl, winp, name):
    T, D = x1.shape
    tm = _tile(T, 256)
    I = T // tm
    pieces = ((_AG0, _Q0), (_Q0, _K0), (_K0, _V0), (_V0, _F0), (_F0, N_IN_PAD))

    def body(x_ref, g_ref, dx2_ref, dag_ref, dq_ref, dk_ref, dv_ref, dfl_ref, w_ref,
             dx1_ref, dw_ref, dg_ref, acc_s):
        i = pl.program_id(0)
        xv = x_ref[...]
        gv = g_ref[...]
        h, r = _rms_fwd(xv, gv)
        hb = h.astype(MXU)
        dps = (dag_ref[...], dq_ref[...], dk_ref[...].astype(MXU), dv_ref[...].astype(MXU), dfl_ref[...].astype(MXU))
        dh = jnp.zeros((tm, D), F32)
        for (c0, c1), dp in zip(pieces, dps):
            dh = dh + _mm_nt(dp, w_ref[:, c0:c1])
            dwp = _mm_tn(hb, dp)

            @pl.when(i == 0)
            def _():
                acc_s[:, c0:c1] = dwp

            @pl.when(i > 0)
            def _():
                acc_s[:, c0:c1] += dwp

        dxn, dgp = _rms_bwd(xv, r, gv, dh)
        dx1_ref[...] = dx2_ref[...] + dxn

        @pl.when(i == 0)
        def _():
            dg_ref[...] = dgp

        @pl.when(i > 0)
        def _():
            dg_ref[...] += dgp

        @pl.when(i == I - 1)
        def _():
            dw_ref[...] = acc_s[...].astype(dw_ref.dtype)

    row = lambda w: pl.BlockSpec((tm, w), lambda i: (i, 0))
    full = lambda a, b: pl.BlockSpec((a, b), lambda i: (0, 0))
    return pl.pallas_call(
        body, name=name, grid=(I,),
        in_specs=[row(D), full(1, D), row(D), row(2 * D_CONV), row(D_ATTN), row(D_ATTN), row(D_ATTN), row(LANES),
                  full(D, N_IN_PAD)],
        out_specs=[row(D), full(D, N_IN_PAD), full(1, D)],
        out_shape=[
            jax.ShapeDtypeStruct((T, D), F32),
            jax.ShapeDtypeStruct((D, N_IN_PAD), MXU),
            jax.ShapeDtypeStruct((1, D), F32),
        ],
        scratch_shapes=[pltpu.VMEM((D, N_IN_PAD), F32)],
        compiler_params=_params(56, 1),
    )(x1, gm, dx2, dag, dq, dk, dv, dfl, winp)


def _forget_fwd(fl, fbp, name):
    T = fl.shape[0]
    tb = _tile(T, 256)

    def body(fl_ref, fb_ref, cum_ref, cumT_ref):
        ri = lax.broadcasted_iota(jnp.int32, (tb, tb), 0)
        ci = lax.broadcasted_iota(jnp.int32, (tb, tb), 1)
        tri = (ri >= ci).astype(jnp.bfloat16)
        carry = jnp.zeros((1, LANES), F32)
        for b in range(T // tb):
            z = fl_ref[b * tb:(b + 1) * tb, :] + fb_ref[...]
            lf = jnp.minimum(z, 0.0) - jnp.log1p(jnp.exp(-jnp.abs(z)))
            c = _exact_tri_dot(tri, lf) + carry
            cum_ref[b * tb:(b + 1) * tb, :] = c
            carry = c[tb - 1:tb, :]
        cumT_ref[...] = cum_ref[...].T[:N_HEADS, :]

    return pl.pallas_call(
        body, name=name,
        out_shape=[jax.ShapeDtypeStruct((T, LANES), F32), jax.ShapeDtypeStruct((N_HEADS, T), F32)],
        compiler_params=pltpu.CompilerParams(vmem_limit_bytes=32 * MIB),
    )(fl, fbp)


def _forget_bwd(dcum, fl, fbp, name):
    T = fl.shape[0]
    tb = _tile(T, 256)

    def body(dc_ref, fl_ref, fb_ref, dfl_ref, dfb_ref):
        ri = lax.broadcasted_iota(jnp.int32, (tb, tb), 0)
        ci = lax.broadcasted_iota(jnp.int32, (tb, tb), 1)
        tri = (ri <= ci).astype(jnp.bfloat16)
        carry = jnp.zeros((1, LANES), F32)
        dfb = jnp.zeros((1, LANES), F32)
        for b in reversed(range(T // tb)):
            sl = slice(b * tb, (b + 1) * tb)
            dl = _exact_tri_dot(tri, dc_ref[sl, :]) + carry
            carry = dl[0:1, :]
            z = fl_ref[sl, :] + fb_ref[...]
            dfl = dl * jax.nn.sigmoid(-z)
            dfl_ref[sl, :] = dfl
            dfb = dfb + jnp.sum(dfl, axis=0, keepdims=True)
        dfb_ref[...] = dfb

    return pl.pallas_call(
        body, name=name,
        out_shape=[jax.ShapeDtypeStruct((T, LANES), F32), jax.ShapeDtypeStruct((1, LANES), F32)],
        compiler_params=pltpu.CompilerParams(vmem_limit_bytes=32 * MIB),
    )(dcum, fl, fbp)


def _attn_fwd(q, k, v, cum, cumT, name):
    T = q.shape[0]
    tq = _tile(T, 256)
    NQ = T // tq

    def body(q_ref, k_ref, v_ref, cum_ref, cumT_ref, o_ref, lse_ref):
        i = pl.program_id(0)
        ri = lax.broadcasted_iota(jnp.int32, (tq, tq), 0)
        ci = lax.broadcasted_iota(jnp.int32, (tq, tq), 1)
        causal = ci <= ri
        lse_cols = []
        for h in range(N_HEADS):
            hs = slice(HEAD_DIM * h, HEAD_DIM * (h + 1))
            qh = q_ref[:, hs]
            cq = cum_ref[:, h:h + 1]

            def block(j, masked, carry, hs=hs, qh=qh, cq=cq, h=h):
                m, l, acc = carry
                cols = pl.ds(pl.multiple_of(j * tq, tq), tq)
                kj = k_ref[cols, hs]
                vj = v_ref[cols, hs]
                ck = cumT_ref[h:h + 1, cols]
                s = _mm_nt(qh, kj) * SCALE + (cq - ck)
                if masked:
                    s = jnp.where(causal, s, NEG)
                m_new = jnp.maximum(m, jnp.max(s, axis=1, keepdims=True))
                alpha = jnp.exp(m - m_new)
                p = jnp.exp(s - m_new)
                l = alpha * l + jnp.sum(p, axis=1, keepdims=True)
                acc = alpha * acc + _mm(p.astype(MXU), vj)
                return m_new, l, acc

            carry = (jnp.full((tq, 1), NEG, F32), jnp.zeros((tq, 1), F32), jnp.zeros((tq, HEAD_DIM), F32))
            carry = lax.fori_loop(0, i, lambda j, c: block(j, False, c), carry)
            m, l, acc = block(i, True, carry)
            o_ref[:, hs] = acc / l
            lse_cols.append(m + jnp.log(l))
        lse_ref[...] = _lanes_from_cols(lse_cols, tq)

    row = lambda w: pl.BlockSpec((tq, w), lambda i: (i, 0))
    full = lambda a, b: pl.BlockSpec((a, b), lambda i: (0, 0))
    return pl.pallas_call(
        body, name=name, grid=(NQ,),
        in_specs=[row(D_ATTN), full(T, D_ATTN), full(T, D_ATTN), row(LANES), full(N_HEADS, T)],
        out_specs=[row(D_ATTN), row(LANES)],
        out_shape=[jax.ShapeDtypeStruct((T, D_ATTN), F32), jax.ShapeDtypeStruct((T, LANES), F32)],
        compiler_params=_params(40, 1),
    )(q, k, v, cum, cumT)


def _attn_bwd(q, k, v, dob, lse, cum, cumT, name):
    T = q.shape[0]
    tq = _tile(T, 256)
    NQ = T // tq

    def body(q_ref, k_ref, v_ref, do_ref, lse_ref, cum_ref, cumT_ref, dq_ref, dk_ref, dv_ref, dcT_ref, p_s, dp_s):
        i = pl.program_id(0)
        ri = lax.broadcasted_iota(jnp.int32, (tq, tq), 0)
        ci = lax.broadcasted_iota(jnp.int32, (tq, tq), 1)
        causal = ci <= ri

        @pl.when(i == 0)
        def _():
            dk_ref[...] = jnp.zeros_like(dk_ref)
            dv_ref[...] = jnp.zeros_like(dv_ref)
            dcT_ref[...] = jnp.zeros_like(dcT_ref)

        for h in range(N_HEADS):
            hs = slice(HEAD_DIM * h, HEAD_DIM * (h + 1))
            qh = q_ref[:, hs]
            doh = do_ref[:, hs]
            cq = cum_ref[:, h:h + 1]
            lse = lse_ref[:, h:h + 1]

            def sweep1(j, masked, dl, hs=hs, qh=qh, doh=doh, cq=cq, lse=lse, h=h):
                cols = pl.ds(pl.multiple_of(j * tq, tq), tq)
                s = _mm_nt(qh, k_ref[cols, hs]) * SCALE + (cq - cumT_ref[h:h + 1, cols])
                if masked:
                    s = jnp.where(causal, s, NEG)
                p = jnp.exp(s - lse)
                dp = _mm_nt(doh, v_ref[cols, hs])
                p_s[j] = p
                dp_s[j] = dp
                dv_ref[cols, hs] += _mm_tn(p.astype(MXU), doh)
                return dl + jnp.sum(p * dp, axis=1, keepdims=True)

            dl = lax.fori_loop(0, i, lambda j, c: sweep1(j, False, c), jnp.zeros((tq, 1), F32))
            dl = sweep1(i, True, dl)

            def sweep2(j, dq, hs=hs, qh=qh, dl=dl, h=h):
                cols = pl.ds(pl.multiple_of(j * tq, tq), tq)
                ds = p_s[j] * (dp_s[j] - dl)
                dcT_ref[h:h + 1, cols] += -jnp.sum(ds, axis=0, keepdims=True)
                dsb = ds.astype(MXU)
                dk_ref[cols, hs] += _mm_tn(dsb, qh) * SCALE
                return dq + _mm(dsb, k_ref[cols, hs]) * SCALE

            dq = lax.fori_loop(0, i + 1, sweep2, jnp.zeros((tq, HEAD_DIM), F32))
            dq_ref[:, hs] = dq.astype(dq_ref.dtype)

    row = lambda w: pl.BlockSpec((tq, w), lambda i: (i, 0))
    full = lambda a, b: pl.BlockSpec((a, b), lambda i: (0, 0))
    return pl.pallas_call(
        body, name=name, grid=(NQ,),
        in_specs=[row(D_ATTN), full(T, D_ATTN), full(T, D_ATTN), row(D_ATTN), row(LANES), row(LANES),
                  full(N_HEADS, T)],
        out_specs=[row(D_ATTN), full(T, D_ATTN), full(T, D_ATTN), full(N_HEADS, T)],
        out_shape=[
            jax.ShapeDtypeStruct((T, D_ATTN), MXU),
            jax.ShapeDtypeStruct((T, D_ATTN), F32),
            jax.ShapeDtypeStruct((T, D_ATTN), F32),
            jax.ShapeDtypeStruct((N_HEADS, T), F32),
        ],
        scratch_shapes=[pltpu.VMEM((NQ, tq, tq), F32), pltpu.VMEM((NQ, tq, tq), F32)],
        compiler_params=_params(48, 1),
    )(q, k, v, dob, lse, cum, cumT)


_ROWS_PER_CHUNK = 64


def _glu_halo(ag_ref, agh_ref, uext_s, tm, first):
    a = ag_ref[:, :D_CONV]
    sg = jax.nn.sigmoid(ag_ref[:, D_CONV:])
    uh = agh_ref[:, :D_CONV] * jax.nn.sigmoid(agh_ref[:, D_CONV:])
    uext_s[0:CONV_HALO, :] = jnp.where(first, 0.0, uh)
    uext_s[CONV_HALO:CONV_HALO + tm, :] = a * sg
    return a, sg


def _layer_norm_stats(y):
    mu = jnp.mean(y, axis=-1, keepdims=True)
    xc = y - mu
    rs = lax.rsqrt(jnp.mean(xc * xc, axis=-1, keepdims=True) + EPS)
    return xc * rs, rs


def _conv_fwd(ag, w32, cb, lg, lb, name):
    T = ag.shape[0]
    tm = _tile(T, 256)
    rc = _tile(tm, _ROWS_PER_CHUNK)
    hb = tm // CONV_HALO

    def body(ag_ref, agh_ref, w_ref, cb_ref, lg_ref, lb_ref, yc_ref, c_ref, uext_s):
        i = pl.program_id(0)
        _glu_halo(ag_ref, agh_ref, uext_s, tm, i == 0)
        for r0 in range(0, tm, rc):
            acc = jnp.zeros((rc, D_CONV), F32)
            for t in range(CONV_TAPS):
                acc = acc + uext_s[pl.ds(r0 + CONV_HALO - (CONV_TAPS - 1) + t, rc), :] * w_ref[t:t + 1, :]
            y = acc + cb_ref[...]
            yc_ref[r0:r0 + rc, :] = y
            n, _ = _layer_norm_stats(y)
            z = n * lg_ref[...] + lb_ref[...]
            c_ref[r0:r0 + rc, :] = z * jax.nn.sigmoid(z)

    row = lambda w: pl.BlockSpec((tm, w), lambda i: (i, 0))
    full = lambda a, b: pl.BlockSpec((a, b), lambda i: (0, 0))
    return pl.pallas_call(
        body, name=name, grid=(T // tm,),
        in_specs=[row(2 * D_CONV),
                  pl.BlockSpec((CONV_HALO, 2 * D_CONV), lambda i: (jnp.maximum(i * hb - 1, 0), 0)),
                  full(CONV_HALO, D_CONV), full(1, D_CONV), full(1, D_CONV), full(1, D_CONV)],
        out_specs=[row(D_CONV), row(D_CONV)],
        out_shape=[jax.ShapeDtypeStruct((T, D_CONV), F32), jax.ShapeDtypeStruct((T, D_CONV), F32)],
        scratch_shapes=[pltpu.VMEM((CONV_HALO + tm, D_CONV), F32)],
        compiler_params=_params(32, 1),
    )(ag, ag, w32, cb, lg, lb)


def _conv_bwd(dc, yc, ag, w32, lg, lb, name):
    T = ag.shape[0]
    tm = _tile(T, 256)
    rc = _tile(tm, _ROWS_PER_CHUNK)
    I = T // tm
    hb = tm // CONV_HALO
    n_halo_blocks = T // CONV_HALO

    def body(dc_ref, yc_ref, dch_ref, ych_ref, ag_ref, agh_ref, w_ref, lg_ref, lb_ref,
             dag_ref, dw_ref, dcb_ref, dlg_ref, dlb_ref, uext_s, dext_s):
        i = pl.program_id(0)
        lgv = lg_ref[...]
        lbv = lb_ref[...]

        def ln_bwd(dcv, ycv):
            n, rs = _layer_norm_stats(ycv)
            z = n * lgv + lbv
            dz = dcv * _silu_grad(z, jax.nn.sigmoid(z))
            dn = dz * lgv
            dy = rs * (dn - jnp.mean(dn, axis=-1, keepdims=True) - n * jnp.mean(dn * n, axis=-1, keepdims=True))
            return dy, dz, n

        dy, dz, n = ln_bwd(dc_ref[...], yc_ref[...])
        dyh, _, _ = ln_bwd(dch_ref[...], ych_ref[...])
        dext_s[0:tm, :] = dy
        dext_s[tm:tm + CONV_HALO, :] = jnp.where(i == I - 1, 0.0, dyh)
        a, sg = _glu_halo(ag_ref, agh_ref, uext_s, tm, i == 0)

        @pl.when(i == 0)
        def _():
            dw_ref[...] = jnp.zeros_like(dw_ref)
            dcb_ref[...] = jnp.zeros_like(dcb_ref)
            dlg_ref[...] = jnp.zeros_like(dlg_ref)
            dlb_ref[...] = jnp.zeros_like(dlb_ref)

        dcb_ref[...] += jnp.sum(dy, axis=0, keepdims=True)
        dlg_ref[...] += jnp.sum(dz * n, axis=0, keepdims=True)
        dlb_ref[...] += jnp.sum(dz, axis=0, keepdims=True)
        for t in range(CONV_TAPS):
            u_t = uext_s[pl.ds(CONV_HALO - (CONV_TAPS - 1) + t, tm), :]
            dw_ref[t:t + 1, :] += jnp.sum(dy * u_t, axis=0, keepdims=True)
        for r0 in range(0, tm, rc):
            acc = jnp.zeros((rc, D_CONV), F32)
            for t in range(CONV_TAPS):
                acc = acc + dext_s[pl.ds(r0 + (CONV_TAPS - 1) - t, rc), :] * w_ref[t:t + 1, :]
            a_c = a[r0:r0 + rc, :]
            sg_c = sg[r0:r0 + rc, :]
            dag_ref[r0:r0 + rc, :D_CONV] = (acc * sg_c).astype(dag_ref.dtype)
            dag_ref[r0:r0 + rc, D_CONV:] = (acc * a_c * sg_c * (1.0 - sg_c)).astype(dag_ref.dtype)

    row = lambda w: pl.BlockSpec((tm, w), lambda i: (i, 0))
    full = lambda a, b: pl.BlockSpec((a, b), lambda i: (0, 0))
    nxt = pl.BlockSpec((CONV_HALO, D_CONV), lambda i: (jnp.minimum((i + 1) * hb, n_halo_blocks - 1), 0))
    return pl.pallas_call(
        body, name=name, grid=(I,),
        in_specs=[row(D_CONV), row(D_CONV), nxt, nxt, row(2 * D_CONV),
                  pl.BlockSpec((CONV_HALO, 2 * D_CONV), lambda i: (jnp.maximum(i * hb - 1, 0), 0)),
                  full(CONV_HALO, D_CONV), full(1, D_CONV), full(1, D_CONV)],
        out_specs=[row(2 * D_CONV), full(CONV_HALO, D_CONV), full(1, D_CONV), full(1, D_CONV), full(1, D_CONV)],
        out_shape=[
            jax.ShapeDtypeStruct((T, 2 * D_CONV), MXU),
            jax.ShapeDtypeStruct((CONV_HALO, D_CONV), F32),
            jax.ShapeDtypeStruct((1, D_CONV), F32),
            jax.ShapeDtypeStruct((1, D_CONV), F32),
            jax.ShapeDtypeStruct((1, D_CONV), F32),
        ],
        scratch_shapes=[pltpu.VMEM((CONV_HALO + tm, D_CONV), F32), pltpu.VMEM((tm + CONV_HALO, D_CONV), F32)],
        compiler_params=_params(32, 1),
    )(dc, yc, dc, yc, ag, ag, w32, lg, lb)


def _outproj_fwd(x1, c, o, gc, ga, wout, name):
    T, D = x1.shape
    tm = _tile(T, 512)

    def body(x_ref, c_ref, o_ref, gc_ref, ga_ref, w_ref, x2_ref):
        yc, _ = _rms_fwd(c_ref[...], gc_ref[...])
        ya, _ = _rms_fwd(o_ref[...], ga_ref[...])
        x2_ref[...] = (x_ref[...] + _mm(yc.astype(MXU), w_ref[:D_CONV, :])
                       + _mm(ya.astype(MXU), w_ref[D_CONV:, :]))

    row = lambda w: pl.BlockSpec((tm, w), lambda i: (i, 0))
    full = lambda a, b: pl.BlockSpec((a, b), lambda i: (0, 0))
    return pl.pallas_call(
        body, name=name, grid=(T // tm,),
        in_specs=[row(D), row(D_CONV), row(D_ATTN), full(1, D_CONV), full(1, D_ATTN), full(D_CONV + D_ATTN, D)],
        out_specs=row(D),
        out_shape=jax.ShapeDtypeStruct((T, D), F32),
        compiler_params=_params(32, 1),
    )(x1, c, o, gc, ga, wout)


def _outproj_bwd(dx2, c, o, gc, ga, wout, name):
    T, D = dx2.shape
    tm = _tile(T, 256)
    I = T // tm

    def body(dx_ref, c_ref, o_ref, gc_ref, ga_ref, w_ref, dc_ref, do_ref, dw_ref, dgc_ref, dga_ref, acc_s):
        i = pl.program_id(0)
        dxb = dx_ref[...].astype(MXU)
        cv = c_ref[...]
        ov = o_ref[...]
        yc, rcn = _rms_fwd(cv, gc_ref[...])
        ya, ra = _rms_fwd(ov, ga_ref[...])
        dyc = _mm_nt(dxb, w_ref[:D_CONV, :])
        dya = _mm_nt(dxb, w_ref[D_CONV:, :])
        dwc = _mm_tn(yc.astype(MXU), dxb)
        dwa = _mm_tn(ya.astype(MXU), dxb)
        dcv, dgc = _rms_bwd(cv, rcn, gc_ref[...], dyc)
        dov, dga = _rms_bwd(ov, ra, ga_ref[...], dya)
        dc_ref[...] = dcv
        do_ref[...] = dov.astype(do_ref.dtype)

        @pl.when(i == 0)
        def _():
            acc_s[:D_CONV, :] = dwc
            acc_s[D_CONV:, :] = dwa
            dgc_ref[...] = dgc
            dga_ref[...] = dga

        @pl.when(i > 0)
        def _():
            acc_s[:D_CONV, :] += dwc
            acc_s[D_CONV:, :] += dwa
            dgc_ref[...] += dgc
            dga_ref[...] += dga

        @pl.when(i == I - 1)
        def _():
            dw_ref[...] = acc_s[...].astype(dw_ref.dtype)

    row = lambda w: pl.BlockSpec((tm, w), lambda i: (i, 0))
    full = lambda a, b: pl.BlockSpec((a, b), lambda i: (0, 0))
    return pl.pallas_call(
        body, name=name, grid=(I,),
        in_specs=[row(D), row(D_CONV), row(D_ATTN), full(1, D_CONV), full(1, D_ATTN), full(D_CONV + D_ATTN, D)],
        out_specs=[row(D_CONV), row(D_ATTN), full(D_CONV + D_ATTN, D), full(1, D_CONV), full(1, D_ATTN)],
        out_shape=[
            jax.ShapeDtypeStruct((T, D_CONV), F32),
            jax.ShapeDtypeStruct((T, D_ATTN), MXU),
            jax.ShapeDtypeStruct((D_CONV + D_ATTN, D), MXU),
            jax.ShapeDtypeStruct((1, D_CONV), F32),
            jax.ShapeDtypeStruct((1, D_ATTN), F32),
        ],
        scratch_shapes=[pltpu.VMEM((D_CONV + D_ATTN, D), F32)],
        compiler_params=_params(40, 1),
    )(dx2, c, o, gc, ga, wout)


def _loss_head(x3, gf, target, name):
    T, D = x3.shape
    tm = _tile(T, 512)

    def body(x_ref, g_ref, t_ref, loss_ref, dx_ref, dg_ref):
        i = pl.program_id(0)
        xv = x_ref[...]
        gv = g_ref[...]
        out, r = _rms_fwd(xv, gv)
        err = out - t_ref[...]
        part = jnp.full((1, LANES), 0.5 / D, F32) * jnp.sum(err * err)
        dxn, dgp = _rms_bwd(xv, r, gv, err * (1.0 / D))
        dx_ref[...] = dxn

        @pl.when(i == 0)
        def _():
            loss_ref[...] = part
            dg_ref[...] = dgp

        @pl.when(i > 0)
        def _():
            loss_ref[...] += part
            dg_ref[...] += dgp

    row = lambda w: pl.BlockSpec((tm, w), lambda i: (i, 0))
    full = lambda a, b: pl.BlockSpec((a, b), lambda i: (0, 0))
    return pl.pallas_call(
        body, name=name, grid=(T // tm,),
        in_specs=[row(D), full(1, D), row(D)],
        out_specs=[full(1, LANES), row(D), full(1, D)],
        out_shape=[jax.ShapeDtypeStruct((1, LANES), F32), jax.ShapeDtypeStruct((T, D), F32),
                   jax.ShapeDtypeStruct((1, D), F32)],
        compiler_params=_params(32, 1),
    )(x3, gf, target)


def _adamw(w, m, v, parts, name):
    R, C = w.shape
    P = parts.shape[0]
    tr = R
    for cand in (256, 176, 128, 64, 32, 16):
        if R % cand == 0:
            tr = cand
            break
    c1 = 1.0 - ADAM_B1 ** ADAM_STEP
    c2 = 1.0 - ADAM_B2 ** ADAM_STEP

    def body(w_ref, m_ref, v_ref, p_ref, g_ref, d_ref, nm_ref, nv_ref):
        g = p_ref[0].astype(F32)
        for s in range(1, P):
            g = g + p_ref[s].astype(F32)
        wv = w_ref[...]
        mn = ADAM_B1 * m_ref[...] + (1.0 - ADAM_B1) * g
        vn = ADAM_B2 * v_ref[...] + (1.0 - ADAM_B2) * (g * g)
        g_ref[...] = g
        nm_ref[...] = mn
        nv_ref[...] = vn
        d_ref[...] = -ADAM_LR * ((mn / c1) / (jnp.sqrt(vn / c2) + ADAM_EPS) + ADAM_WD * wv)

    blk = pl.BlockSpec((tr, C), lambda i: (i, 0))
    out = jax.ShapeDtypeStruct((R, C), F32)
    return pl.pallas_call(
        body, name=name, grid=(R // tr,),
        in_specs=[blk, blk, blk, pl.BlockSpec((P, tr, C), lambda i: (0, i, 0))],
        out_specs=[blk, blk, blk, blk],
        out_shape=[out, out, out, out],
        compiler_params=_params(32, 1),
    )(w, m, v, parts)


def _position():
    return lax.axis_index("x"), lax.axis_index("y"), lax.axis_index("c")


def _flat(px, py, pc):
    return 4 * px + 2 * py + pc


def _all_gather(shards, name):
    n = len(shards)

    def body(*refs):
        ins, outs = refs[:n], refs[n:2 * n]
        send_sems, recv_sems, local_sems = refs[2 * n:]
        x, y, c = _position()
        me, sibling = (x, y, c), (x, y, 1 - c)
        chips = [(1 - x, y), (x, 1 - y), (1 - x, 1 - y)]

        def copy(a, k, block, to, src=None):
            dst = outs[a].at[_flat(*block)]
            return pltpu.make_async_remote_copy(
                src_ref=dst if src is None else src, dst_ref=dst,
                send_sem=send_sems.at[a, k], recv_sem=recv_sems.at[a, k],
                device_id=to, device_id_type=MESH)

        mine = [pltpu.make_async_copy(ins[a], outs[a].at[_flat(*me)], local_sems.at[a]) for a in range(n)]
        for cp in mine:
            cp.start()
        first = []
        for a in range(n):
            first.append(copy(a, 0, me, sibling, src=ins[a]))
            first += [copy(a, 1 + j, me, (*chip, c), src=ins[a]) for j, chip in enumerate(chips)]
        for cp in first:
            cp.start()
        passed = []
        for a in range(n):
            for j, chip in enumerate(chips):
                copy(a, 1 + j, (*chip, c), me).wait_recv()
                fwd = copy(a, 4 + j, (*chip, c), sibling)
                fwd.start()
                passed.append(fwd)
        for a in range(n):
            copy(a, 0, sibling, me).wait_recv()
            for j, chip in enumerate(chips):
                copy(a, 4 + j, (*chip, 1 - c), me).wait_recv()
        for cp in first + passed:
            cp.wait_send()
        for cp in mine:
            cp.wait()

    hbm = pl.BlockSpec(memory_space=pltpu.HBM)
    return pl.pallas_call(
        body, name=name,
        in_specs=[hbm] * n, out_specs=[hbm] * n,
        out_shape=[jax.ShapeDtypeStruct((N_DEV,) + s.shape, s.dtype) for s in shards],
        scratch_shapes=[pltpu.SemaphoreType.DMA((n, 7)), pltpu.SemaphoreType.DMA((n, 7)),
                        pltpu.SemaphoreType.DMA((n,))],
    )(*shards)


def _exchange_blocks(parts, name):
    n = len(parts)

    def body(*refs):
        ins, outs = refs[:n], refs[n:2 * n]
        send_sems, recv_sems, local_sems = refs[2 * n:]
        x, y, c = _position()
        me = _flat(x, y, c)
        mine = [pltpu.make_async_copy(ins[a].at[me], outs[a].at[me], local_sems.at[a]) for a in range(n)]
        for cp in mine:
            cp.start()
        sent = []
        for k in range(1, N_DEV):
            kx, ky, kc = (k >> 2) & 1, (k >> 1) & 1, k & 1
            peer = (1 - x if kx else x, 1 - y if ky else y, 1 - c if kc else c)
            for a in range(n):
                cp = pltpu.make_async_remote_copy(
                    src_ref=ins[a].at[_flat(*peer)], dst_ref=outs[a].at[me],
                    send_sem=send_sems.at[a, k - 1], recv_sem=recv_sems.at[a, k - 1],
                    device_id=peer, device_id_type=MESH)
                cp.start()
                sent.append(cp)
        for cp in sent:
            cp.wait_recv()
        for cp in sent:
            cp.wait_send()
        for cp in mine:
            cp.wait()

    hbm = pl.BlockSpec(memory_space=pltpu.HBM)
    return pl.pallas_call(
        body, name=name,
        in_specs=[hbm] * n, out_specs=[hbm] * n,
        out_shape=[jax.ShapeDtypeStruct(p.shape, p.dtype) for p in parts],
        scratch_shapes=[pltpu.SemaphoreType.DMA((n, 7)), pltpu.SemaphoreType.DMA((n, 7)),
                        pltpu.SemaphoreType.DMA((n,))],
    )(*parts)


def _pack(arrays):
    flat = []
    for a in arrays:
        v = a.reshape(-1)
        pad = (-v.shape[0]) % LANES
        flat.append(jnp.pad(v, (0, pad)) if pad else v)
    v = jnp.concatenate(flat)
    pad = (-v.shape[0]) % (8 * LANES)
    return jnp.pad(v, (0, pad)).reshape(-1, LANES)


def _unpack(packed, shapes):
    v = packed.reshape(-1)
    out, off = [], 0
    for s in shapes:
        size = 1
        for d in s:
            size *= d
        out.append(v[off:off + size].reshape(s))
        off += size + (-size) % LANES
    return out


def _local_step(x, target, norms, w13_1, w2_1, winp, conv_w32, wout, w13_2, w2_2):
    x1, G1, U1 = _ffn_fwd(x, norms["ffn1_norm"], w13_1, w2_1, "ffn1_fwd")
    ag, q, k, v, fl = _inproj_fwd(x1, norms["mix_norm"], winp, "inproj_fwd")
    cum, cumT = _forget_fwd(fl, norms["forget_b"], "forget_fwd")
    yc, c = _conv_fwd(ag, conv_w32, norms["conv_b"], norms["conv_ln_g"], norms["conv_ln_b"], "conv_fwd")
    o, lse = _attn_fwd(q, k, v, cum, cumT, "attn_fwd")
    x2 = _outproj_fwd(x1, c, o, norms["out_norm_conv"], norms["out_norm_attn"], wout, "outproj_fwd")
    x3, G2, U2 = _ffn_fwd(x2, norms["ffn2_norm"], w13_2, w2_2, "ffn2_fwd")
    loss, dx3, d_final = _loss_head(x3, norms["final_norm"], target, "loss_head")

    dx2, dw13_2, dw2_2, d_ffn2n = _ffn_bwd(x2, norms["ffn2_norm"], dx3, G2, U2, w13_2, w2_2, "ffn2_bwd")
    dc, dob, dwout, d_onc, d_ona = _outproj_bwd(
        dx2, c, o, norms["out_norm_conv"], norms["out_norm_attn"], wout, "outproj_bwd")
    dq, dk, dv, dcumT = _attn_bwd(q, k, v, dob, lse, cum, cumT, "attn_bwd")
    dcum = jnp.pad(dcumT.T, ((0, 0), (0, LANES - N_HEADS)))
    dfl, d_fb = _forget_bwd(dcum, fl, norms["forget_b"], "forget_bwd")
    dag, d_convw, d_cb, d_lg, d_lb = _conv_bwd(dc, yc, ag, conv_w32, norms["conv_ln_g"], norms["conv_ln_b"], "conv_bwd")
    dx1, dwinp, d_mixn = _inproj_bwd(x1, norms["mix_norm"], dx2, dag, dq, dk, dv, dfl, winp, "inproj_bwd")
    dx, dw13_1, dw2_1, d_ffn1n = _ffn_bwd(x, norms["ffn1_norm"], dx1, G1, U1, w13_1, w2_1, "ffn1_bwd")

    small = dict(ffn1_norm=d_ffn1n, mix_norm=d_mixn, conv_b=d_cb, conv_ln_g=d_lg, conv_ln_b=d_lb,
                 forget_b=d_fb, out_norm_conv=d_onc, out_norm_attn=d_ona, ffn2_norm=d_ffn2n,
                 final_norm=d_final, conv_w=d_convw)
    big = dict(ffn1_w13=dw13_1, ffn1_w2=dw2_1, w_in=dwinp, w_out=dwout, ffn2_w13=dw13_2, ffn2_w2=dw2_2)
    return loss[0, 0], dx, small, big


_SMALL = ("ffn1_norm", "mix_norm", "conv_b", "conv_ln_g", "conv_ln_b", "forget_b", "out_norm_conv",
          "out_norm_attn", "ffn2_norm", "final_norm")
_BIG = ("ffn1_w13", "ffn1_w2", "w_in", "w_out", "ffn2_w13", "ffn2_w2")
_ORDER = ("ffn1_norm", "ffn1_w13", "ffn1_w2", "mix_norm", "w_in", "conv_w", "conv_b", "conv_ln_g", "conv_ln_b",
          "forget_b", "out_norm_conv", "out_norm_attn", "w_out", "ffn2_norm", "ffn2_w13", "ffn2_w2", "final_norm")


def kernel(x, ffn1_norm, ffn1_w13, ffn1_w2, mix_norm, w_in, conv_w, conv_b, conv_ln_g, conv_ln_b, forget_b, out_norm_conv, out_norm_attn, w_out, ffn2_norm, ffn2_w13, ffn2_w2, final_norm, loss_target, m_ffn1_norm, m_ffn1_w13, m_ffn1_w2, m_mix_norm, m_w_in, m_conv_w, m_conv_b, m_conv_ln_g, m_conv_ln_b, m_forget_b, m_out_norm_conv, m_out_norm_attn, m_w_out, m_ffn2_norm, m_ffn2_w13, m_ffn2_w2, m_final_norm, v_ffn1_norm, v_ffn1_w13, v_ffn1_w2, v_mix_norm, v_w_in, v_conv_w, v_conv_b, v_conv_ln_g, v_conv_ln_b, v_forget_b, v_out_norm_conv, v_out_norm_attn, v_w_out, v_ffn2_norm, v_ffn2_w13, v_ffn2_w2, v_final_norm):
    w = dict(ffn1_norm=ffn1_norm, ffn1_w13=ffn1_w13, ffn1_w2=ffn1_w2, mix_norm=mix_norm, w_in=w_in, conv_w=conv_w,
             conv_b=conv_b, conv_ln_g=conv_ln_g, conv_ln_b=conv_ln_b, forget_b=forget_b, out_norm_conv=out_norm_conv,
             out_norm_attn=out_norm_attn, w_out=w_out, ffn2_norm=ffn2_norm, ffn2_w13=ffn2_w13, ffn2_w2=ffn2_w2,
             final_norm=final_norm)
    m = dict(ffn1_norm=m_ffn1_norm, ffn1_w13=m_ffn1_w13, ffn1_w2=m_ffn1_w2, mix_norm=m_mix_norm, w_in=m_w_in,
             conv_w=m_conv_w, conv_b=m_conv_b, conv_ln_g=m_conv_ln_g, conv_ln_b=m_conv_ln_b, forget_b=m_forget_b,
             out_norm_conv=m_out_norm_conv, out_norm_attn=m_out_norm_attn, w_out=m_w_out, ffn2_norm=m_ffn2_norm,
             ffn2_w13=m_ffn2_w13, ffn2_w2=m_ffn2_w2, final_norm=m_final_norm)
    v = dict(ffn1_norm=v_ffn1_norm, ffn1_w13=v_ffn1_w13, ffn1_w2=v_ffn1_w2, mix_norm=v_mix_norm, w_in=v_w_in,
             conv_w=v_conv_w, conv_b=v_conv_b, conv_ln_g=v_conv_ln_g, conv_ln_b=v_conv_ln_b, forget_b=v_forget_b,
             out_norm_conv=v_out_norm_conv, out_norm_attn=v_out_norm_attn, w_out=v_w_out, ffn2_norm=v_ffn2_norm,
             ffn2_w13=v_ffn2_w13, ffn2_w2=v_ffn2_w2, final_norm=v_final_norm)
    shapes = {n: a.shape for n, a in w.items()}
    T, D = x.shape[1], x.shape[2]
    two = lambda a: a.reshape(a.shape[-2], a.shape[-1]) if a.ndim == 3 else a.reshape(1, -1)
    w2d = {n: two(a) for n, a in w.items()}
    m2d = {n: two(a) for n, a in m.items()}
    v2d = {n: two(a) for n, a in v.items()}

    shards = [w2d[n].astype(MXU) for n in _BIG] + [w2d["conv_w"]]
    g13_1, g2_1, gin, gout, g13_2, g2_2, gconv = _all_gather(shards, "gather_weights")
    bf = g13_1.shape[-1]
    J = N_DEV // 2
    w13_1 = g13_1.reshape(2, J, D, bf)
    w13_2 = g13_2.reshape(2, J, D, bf)
    w2_1 = g2_1.reshape(-1, D)
    w2_2 = g2_2.reshape(-1, D)
    wout = gout.reshape(-1, D)
    winp = jnp.pad(gin.transpose(1, 0, 2).reshape(D, N_IN), ((0, 0), (0, N_IN_PAD - N_IN)))
    conv_full = gconv.transpose(1, 0, 2).reshape(CONV_TAPS, D_CONV)
    conv_w32 = jnp.pad(conv_full, ((0, CONV_HALO - CONV_TAPS), (0, 0)))

    norms = {n: w2d[n] for n in _SMALL}
    norms["forget_b"] = jnp.pad(w2d["forget_b"], ((0, 0), (0, LANES - N_HEADS)))
    loss_part, dx, small, big = _local_step(x[0], loss_target[0], norms, w13_1, w2_1, winp, conv_w32, wout,
                                            w13_2, w2_2)
    loss = lax.psum(loss_part, ("x", "y", "c"))

    cin = shapes["w_in"][-1]
    parts = [
        big["ffn1_w13"].reshape(N_DEV, D, bf),
        big["ffn1_w2"].reshape(N_DEV, -1, D),
        big["w_in"][:, :N_IN].reshape(D, N_DEV, cin).transpose(1, 0, 2),
        big["w_out"].reshape(N_DEV, -1, D),
        big["ffn2_w13"].reshape(N_DEV, D, bf),
        big["ffn2_w2"].reshape(N_DEV, -1, D),
    ]
    recv = _exchange_blocks(parts, "exchange_grads")

    grads, deltas, new_m, new_v = {}, {}, {}, {}
    for n, r in zip(_BIG, recv):
        g, d, nm, nv = _adamw(w2d[n], m2d[n], v2d[n], r, "adamw_" + n)
        grads[n], deltas[n], new_m[n], new_v[n] = g, d, nm, nv

    small["forget_b"] = small["forget_b"][:, :N_HEADS]
    small_shapes = [tuple(small[n].shape) for n in _SMALL] + [(CONV_HALO, D_CONV)]
    packed_g = _pack([small[n] for n in _SMALL] + [small["conv_w"]])
    (gathered_small,) = _all_gather([packed_g], "gather_small_grads")
    zero_conv = jnp.zeros((CONV_HALO, D_CONV), F32)
    pw = _pack([w2d[n] for n in _SMALL] + [zero_conv])
    pm = _pack([m2d[n] for n in _SMALL] + [zero_conv])
    pv = _pack([v2d[n] for n in _SMALL] + [zero_conv + 1.0])
    sg, sd, sm, sv = _adamw(pw, pm, pv, gathered_small, "adamw_small")
    for name_, g, d, nm, nv in zip(_SMALL, *[_unpack(t, small_shapes)[:len(_SMALL)] for t in (sg, sd, sm, sv)]):
        grads[name_], deltas[name_], new_m[name_], new_v[name_] = g, d, nm, nv
    conv_g_full = _unpack(sg, small_shapes)[-1][:CONV_TAPS]
    xi, yi, ci = _position()
    cw = shapes["conv_w"][-1]
    conv_g_mine = lax.dynamic_slice_in_dim(conv_g_full, _flat(xi, yi, ci) * cw, cw, axis=1)
    g, d, nm, nv = _adamw(w2d["conv_w"], m2d["conv_w"], v2d["conv_w"], conv_g_mine[None], "adamw_conv_w")
    grads["conv_w"], deltas["conv_w"], new_m["conv_w"], new_v["conv_w"] = g, d, nm, nv

    shaped = lambda dct: [dct[n].reshape(shapes[n]) for n in _ORDER]
    return (loss, dx[None], *shaped(grads), *shaped(deltas), *shaped(new_m), *shaped(new_v))
```

```python
from typing import NamedTuple

import jax
import jax.numpy as jnp
from jax import lax
from jax.experimental import pallas as pl
from jax.experimental.pallas import tpu as pltpu
from jax.experimental.pallas import tpu_sc as plsc

F32 = jnp.float32
MXU = jnp.bfloat16
EPS = 1e-6
N_HEADS = 8
HEAD_DIM = 64
D_CONV = 512
D_ATTN = N_HEADS * HEAD_DIM
CONV_TAPS = 31
CONV_HALO = 32
SCALE = HEAD_DIM ** -0.5
NEG = -1e30
LANES = 128
N_DEV = 8
N_CHIPS = N_DEV // 2
MESH = pl.DeviceIdType.MESH
MIB = 1 << 20

ADAM_LR = 0.001
ADAM_B1 = 0.9
ADAM_B2 = 0.999
ADAM_EPS = 1e-08
ADAM_WD = 0.01
ADAM_STEP = 10


_UNREAD = pl.BlockSpec(memory_space=pl.ANY)


def _params(vmem_mib, n_axes):
    return pltpu.CompilerParams(dimension_semantics=("arbitrary",) * n_axes, vmem_limit_bytes=vmem_mib * MIB)


def _mm(a, b):
    return jnp.dot(a, b, preferred_element_type=F32)


def _mm_nt(a, b):
    return lax.dot_general(a, b, (((1,), (1,)), ((), ())), preferred_element_type=F32)


def _mm_tn(a, b):
    return lax.dot_general(a, b, (((0,), (0,)), ((), ())), preferred_element_type=F32)


def _rms_fwd(x, g):
    r = lax.rsqrt(jnp.mean(x * x, axis=-1, keepdims=True) + EPS)
    return x * r * g, r


def _rms_bwd(x, r, g, dy):
    gdy = dy * g
    dx = r * gdy - x * (r * r * r) * jnp.mean(x * gdy, axis=-1, keepdims=True)
    dg = jnp.sum(dy * x * r, axis=0, keepdims=True)
    return dx, dg


def _silu_grad(z, sz):
    return sz * (1.0 + z * (1.0 - sz))


def _three_terms(x):
    x1 = x.astype(jnp.bfloat16)
    r1 = x - x1.astype(F32)
    x2 = r1.astype(jnp.bfloat16)
    x3 = (r1 - x2.astype(F32)).astype(jnp.bfloat16)
    return x1, x2, x3


def _exact_tri_dot(tri, x):
    x1, x2, x3 = _three_terms(x)
    return _mm(tri, x1) + _mm(tri, x2) + _mm(tri, x3)


def _exact_dot_01(x, sel):
    x1, x2, x3 = _three_terms(x)
    return _mm(x1, sel) + _mm(x2, sel) + _mm(x3, sel)


def _tile(n, want):
    t = min(n, want)
    assert n % t == 0
    return t


def _ffn_fwd(x, g, w13, w2, name):
    T, D = x.shape
    _, J, _, bf = w13.shape
    tm = _tile(T, 512)
    I = T // tm

    def body(x_ref, g_ref, w13_ref, w2_ref, xo_ref, G_ref, U_ref, acc_s):
        j = pl.program_id(0)
        i = pl.program_id(1)
        rows = pl.ds(pl.multiple_of(i * tm, tm), tm)
        xv = x_ref[...]
        h, _ = _rms_fwd(xv, g_ref[...])
        hb = h.astype(MXU)
        G = _mm(hb, w13_ref[0])
        U = _mm(hb, w13_ref[1])
        G_ref[...] = G.astype(MXU)
        U_ref[...] = U.astype(MXU)
        A = (G * jax.nn.sigmoid(G) * U).astype(MXU)
        F = _mm(A, w2_ref[...])

        @pl.when(j == 0)
        def _():
            acc_s[rows, :] = xv + 0.5 * F

        @pl.when(j > 0)
        def _():
            acc_s[rows, :] += 0.5 * F

        @pl.when(j == J - 1)
        def _():
            xo_ref[...] = acc_s[rows, :]

    return pl.pallas_call(
        body, name=name, grid=(J, I),
        in_specs=[
            pl.BlockSpec((tm, D), lambda j, i: (i, 0)),
            pl.BlockSpec((1, D), lambda j, i: (0, 0)),
            pl.BlockSpec((2, None, D, bf), lambda j, i: (0, j, 0, 0)),
            pl.BlockSpec((bf, D), lambda j, i: (j, 0)),
        ],
        out_specs=[
            pl.BlockSpec((tm, D), lambda j, i: (jnp.where(j == J - 1, i, 0), 0)),
            pl.BlockSpec((None, tm, bf), lambda j, i: (j, i, 0)),
            pl.BlockSpec((None, tm, bf), lambda j, i: (j, i, 0)),
        ],
        out_shape=[
            jax.ShapeDtypeStruct((T, D), F32),
            jax.ShapeDtypeStruct((J, T, bf), MXU),
            jax.ShapeDtypeStruct((J, T, bf), MXU),
        ],
        scratch_shapes=[pltpu.VMEM((T, D), F32)],
        compiler_params=_params(48, 2),
    )(x, g, w13, w2)


def _ffn_bwd(x, g, dy, Gs, Us, w13, w2, name, after=()):
    T, D = x.shape
    _, J, _, bf = w13.shape
    tm = _tile(T, 256)
    I = T // tm

    def body(x_ref, g_ref, dy_ref, G_ref, U_ref, w13_ref, w2_ref, *rest):
        dx_ref, dw13_ref, dw2_ref, dg_ref, dh_s, a13_s, a2_s = rest[len(after):]
        j = pl.program_id(0)
        i = pl.program_id(1)
        rows = pl.ds(pl.multiple_of(i * tm, tm), tm)
        xv = x_ref[...]
        gv = g_ref[...]
        h, r = _rms_fwd(xv, gv)
        hb = h.astype(MXU)
        dyv = dy_ref[...]
        dFb = (0.5 * dyv).astype(MXU)
        G = G_ref[...].astype(F32)
        U = U_ref[...].astype(F32)
        sg = jax.nn.sigmoid(G)
        s = G * sg
        A = (s * U).astype(MXU)
        dA = _mm_nt(dFb, w2_ref[...])
        dUb = (dA * s).astype(MXU)
        dGb = (dA * U * _silu_grad(G, sg)).astype(MXU)
        dw2 = _mm_tn(A, dFb)
        dwg = _mm_tn(hb, dGb)
        dwu = _mm_tn(hb, dUb)
        dh = _mm_nt(dGb, w13_ref[0]) + _mm_nt(dUb, w13_ref[1])

        @pl.when(i == 0)
        def _():
            a13_s[0] = dwg
            a13_s[1] = dwu
            a2_s[...] = dw2

        @pl.when(i > 0)
        def _():
            a13_s[0] += dwg
            a13_s[1] += dwu
            a2_s[...] += dw2

        @pl.when(i == I - 1)
        def _():
            dw13_ref[...] = a13_s[...].astype(dw13_ref.dtype)
            dw2_ref[...] = a2_s[...].astype(dw2_ref.dtype)

        @pl.when(j == 0)
        def _():
            dh_s[rows, :] = dh

        @pl.when(j > 0)
        def _():
            dh_s[rows, :] += dh

        @pl.when(j == J - 1)
        def _():
            dxn, dgp = _rms_bwd(xv, r, gv, dh_s[rows, :])
            dx_ref[...] = dyv + dxn

            @pl.when(i == 0)
            def _():
                dg_ref[...] = dgp

            @pl.when(i > 0)
            def _():
                dg_ref[...] += dgp

    return pl.pallas_call(
        body, name=name, grid=(J, I),
        in_specs=[
            pl.BlockSpec((tm, D), lambda j, i: (i, 0)),
            pl.BlockSpec((1, D), lambda j, i: (0, 0)),
            pl.BlockSpec((tm, D), lambda j, i: (i, 0)),
            pl.BlockSpec((None, tm, bf), lambda j, i: (j, i, 0)),
            pl.BlockSpec((None, tm, bf), lambda j, i: (j, i, 0)),
            pl.BlockSpec((2, None, D, bf), lambda j, i: (0, j, 0, 0)),
            pl.BlockSpec((bf, D), lambda j, i: (j, 0)),
        ] + [_UNREAD] * len(after),
        out_specs=[
            pl.BlockSpec((tm, D), lambda j, i: (jnp.where(j == J - 1, i, 0), 0)),
            pl.BlockSpec((2, None, D, bf), lambda j, i: (0, j, 0, 0)),
            pl.BlockSpec((bf, D), lambda j, i: (j, 0)),
            pl.BlockSpec((1, D), lambda j, i: (0, 0)),
        ],
        out_shape=[
            jax.ShapeDtypeStruct((T, D), F32),
            jax.ShapeDtypeStruct(w13.shape, MXU),
            jax.ShapeDtypeStruct(w2.shape, MXU),
            jax.ShapeDtypeStruct((1, D), F32),
        ],
        scratch_shapes=[pltpu.VMEM((T, D), F32), pltpu.VMEM((2, D, bf), F32), pltpu.VMEM((bf, D), F32)],
        compiler_params=_params(58, 2),
    )(x, g, dy, Gs, Us, w13, w2, *after)


_AG0, _Q0, _K0, _V0, _F0 = 0, 2 * D_CONV, 2 * D_CONV + D_ATTN, 2 * D_CONV + 2 * D_ATTN, 2 * D_CONV + 3 * D_ATTN
N_IN = _F0 + N_HEADS
N_IN_PAD = _F0 + LANES


def _inproj_fwd(x1, gm, winp, name):
    T, D = x1.shape
    tm = _tile(T, 256)

    def body(x_ref, g_ref, w_ref, ag_ref, q_ref, k_ref, v_ref, qT_ref, kT_ref, vT_ref, fl_ref):
        h, _ = _rms_fwd(x_ref[...], g_ref[...])
        hb = h.astype(MXU)
        ag_ref[...] = _mm(hb, w_ref[:, _AG0:_Q0])
        for c0, ref, refT in ((_Q0, q_ref, qT_ref), (_K0, k_ref, kT_ref), (_V0, v_ref, vT_ref)):
            y = _mm(hb, w_ref[:, c0:c0 + D_ATTN])
            ref[...] = y.astype(MXU)
            refT[...] = y.T.astype(MXU)
        fl_ref[...] = _mm(hb, w_ref[:, _F0:N_IN_PAD])

    row = lambda w: pl.BlockSpec((tm, w), lambda i: (i, 0))
    col = pl.BlockSpec((D_ATTN, tm), lambda i: (0, i))
    std = jax.ShapeDtypeStruct((T, D_ATTN), MXU)
    trn = jax.ShapeDtypeStruct((D_ATTN, T), MXU)
    return pl.pallas_call(
        body, name=name, grid=(T // tm,),
        in_specs=[row(D), pl.BlockSpec((1, D), lambda i: (0, 0)), pl.BlockSpec((D, N_IN_PAD), lambda i: (0, 0))],
        out_specs=[row(2 * D_CONV), row(D_ATTN), row(D_ATTN), row(D_ATTN), col, col, col, row(LANES)],
        out_shape=[jax.ShapeDtypeStruct((T, 2 * D_CONV), F32), std, std, std, trn, trn, trn,
                   jax.ShapeDtypeStruct((T, LANES), F32)],
        compiler_params=_params(40, 1),
    )(x1, gm, winp)


def _inproj_bwd(x1, gm, dx2, dag, dq, dk, dv, dfl, winp, name):
    T, D = x1.shape
    tm = _tile(T, 256)
    I = T // tm
    pieces = ((_AG0, _Q0), (_Q0, _K0), (_K0, _V0), (_V0, _F0), (_F0, N_IN_PAD))

    def body(x_ref, g_ref, dx2_ref, dag_ref, dq_ref, dk_ref, dv_ref, dfl_ref, w_ref,
             dx1_ref, dw_ref, dg_ref, acc_s):
        i = pl.program_id(0)
        xv = x_ref[...]
        gv = g_ref[...]
        h, r = _rms_fwd(xv, gv)
        hb = h.astype(MXU)
        dps = (dag_ref[...], dq_ref[...], dk_ref[...].astype(MXU), dv_ref[...].astype(MXU), dfl_ref[...].astype(MXU))
        dh = jnp.zeros((tm, D), F32)
        for (c0, c1), dp in zip(pieces, dps):
            dh = dh + _mm_nt(dp, w_ref[:, c0:c1])
            dwp = _mm_tn(hb, dp)

            @pl.when(i == 0)
            def _():
                acc_s[:, c0:c1] = dwp

            @pl.when(i > 0)
            def _():
                acc_s[:, c0:c1] += dwp

        dxn, dgp = _rms_bwd(xv, r, gv, dh)
        dx1_ref[...] = dx2_ref[...] + dxn

        @pl.when(i == 0)
        def _():
            dg_ref[...] = dgp

        @pl.when(i > 0)
        def _():
            dg_ref[...] += dgp

        @pl.when(i == I - 1)
        def _():
            dw_ref[...] = acc_s[...].astype(dw_ref.dtype)

    row = lambda w: pl.BlockSpec((tm, w), lambda i: (i, 0))
    full = lambda a, b: pl.BlockSpec((a, b), lambda i: (0, 0))
    return pl.pallas_call(
        body, name=name, grid=(I,),
        in_specs=[row(D), full(1, D), row(D), row(2 * D_CONV), row(D_ATTN), row(D_ATTN), row(D_ATTN), row(LANES),
                  full(D, N_IN_PAD)],
        out_specs=[row(D), full(D, N_IN_PAD), full(1, D)],
        out_shape=[
            jax.ShapeDtypeStruct((T, D), F32),
            jax.ShapeDtypeStruct((D, N_IN_PAD), MXU),
            jax.ShapeDtypeStruct((1, D), F32),
        ],
        scratch_shapes=[pltpu.VMEM((D, N_IN_PAD), F32)],
        compiler_params=_params(56, 1),
    )(x1, gm, dx2, dag, dq, dk, dv, dfl, winp)


def _forget_fwd(fl, fbp, name):
    T = fl.shape[0]
    tb = _tile(T, 256)

    def body(fl_ref, fb_ref, cum_ref, cumT_ref):
        ri = lax.broadcasted_iota(jnp.int32, (tb, tb), 0)
        ci = lax.broadcasted_iota(jnp.int32, (tb, tb), 1)
        tri = (ri >= ci).astype(jnp.bfloat16)
        carry = jnp.zeros((1, LANES), F32)
        for b in range(T // tb):
            z = fl_ref[b * tb:(b + 1) * tb, :] + fb_ref[...]
            lf = jnp.minimum(z, 0.0) - jnp.log1p(jnp.exp(-jnp.abs(z)))
            c = _exact_tri_dot(tri, lf) + carry
            cum_ref[b * tb:(b + 1) * tb, :] = c
            carry = c[tb - 1:tb, :]
        cumT_ref[...] = cum_ref[...].T[:N_HEADS, :]

    return pl.pallas_call(
        body, name=name,
        out_shape=[jax.ShapeDtypeStruct((T, LANES), F32), jax.ShapeDtypeStruct((N_HEADS, T), F32)],
        compiler_params=pltpu.CompilerParams(vmem_limit_bytes=32 * MIB),
    )(fl, fbp)


def _forget_bwd(dcum, fl, fbp, name):
    T = fl.shape[0]
    tb = _tile(T, 256)

    def body(dc_ref, fl_ref, fb_ref, dfl_ref, dfb_ref):
        ri = lax.broadcasted_iota(jnp.int32, (tb, tb), 0)
        ci = lax.broadcasted_iota(jnp.int32, (tb, tb), 1)
        tri = (ri <= ci).astype(jnp.bfloat16)
        carry = jnp.zeros((1, LANES), F32)
        dfb = jnp.zeros((1, LANES), F32)
        for b in reversed(range(T // tb)):
            sl = slice(b * tb, (b + 1) * tb)
            dl = _exact_tri_dot(tri, dc_ref[sl, :]) + carry
            carry = dl[0:1, :]
            z = fl_ref[sl, :] + fb_ref[...]
            dfl = dl * jax.nn.sigmoid(-z)
            dfl_ref[sl, :] = dfl
            dfb = dfb + jnp.sum(dfl, axis=0, keepdims=True)
        dfb_ref[...] = dfb

    return pl.pallas_call(
        body, name=name,
        out_shape=[jax.ShapeDtypeStruct((T, LANES), F32), jax.ShapeDtypeStruct((1, LANES), F32)],
        compiler_params=pltpu.CompilerParams(vmem_limit_bytes=32 * MIB),
    )(dcum, fl, fbp)


def _causal_keep(i, j, tq, tk):
    key = j * tk + lax.broadcasted_iota(jnp.int32, (tk, tq), 0)
    qry = i * tq + lax.broadcasted_iota(jnp.int32, (tk, tq), 1)
    return key <= qry


def _split_hi_lo(x):
    hi = x.astype(MXU)
    lo = (x - hi.astype(F32)).astype(MXU)
    return hi, lo


def _attn_fwd(qT, k, vT, cum, cumT, name):
    T = k.shape[0]
    tq = _tile(T, 256)
    tk = _tile(tq, 128)
    kpq = tq // tk
    heads = [slice(HEAD_DIM * h, HEAD_DIM * (h + 1)) for h in range(N_HEADS)]

    def body(qT_ref, k_ref, vT_ref, cum_ref, cumT_ref, o_ref, lseT_ref, acc_s, m_s, l_s):
        i = pl.program_id(0)
        acc_s[...] = jnp.zeros_like(acc_s)
        m_s[...] = jnp.full_like(m_s, NEG)
        l_s[...] = jnp.zeros_like(l_s)

        def kblock(j, masked):
            rows = pl.ds(pl.multiple_of(j * tk, tk), tk)
            keep = _causal_keep(i, j, tq, tk) if masked else None
            qk = [_mm(k_ref[rows, hs], qT_ref[hs, :]) for hs in heads]
            for h, hs in enumerate(heads):
                sT = qk[h] * SCALE + (cumT_ref[h:h + 1, :] - cum_ref[rows, h:h + 1])
                if masked:
                    sT = jnp.where(keep, sT, NEG)
                m_old = m_s[h:h + 1, :]
                m_new = jnp.maximum(m_old, jnp.max(sT, axis=0, keepdims=True))
                alpha = jnp.exp(m_old - m_new)
                pT = jnp.exp(sT - m_new)
                l_s[h:h + 1, :] = alpha * l_s[h:h + 1, :] + jnp.sum(pT, axis=0, keepdims=True)
                p_hi, p_lo = _split_hi_lo(pT)
                vh = vT_ref[hs, rows]
                acc_s[hs, :] = alpha * acc_s[hs, :] + (_mm(vh, p_hi) + _mm(vh, p_lo))
                m_s[h:h + 1, :] = m_new

        def unmasked(j, c):
            kblock(j, False)
            return c

        lax.fori_loop(0, kpq * i, unmasked, 0)
        for d in range(kpq):
            kblock(kpq * i + d, True)
        for h, hs in enumerate(heads):
            acc_s[hs, :] = acc_s[hs, :] / l_s[h:h + 1, :]
        o_ref[...] = acc_s[...].T
        lseT_ref[...] = m_s[...] + jnp.log(l_s[...])

    full = lambda a, b: pl.BlockSpec((a, b), lambda i: (0, 0))
    colblk = lambda r: pl.BlockSpec((r, tq), lambda i: (0, i))
    return pl.pallas_call(
        body, name=name, grid=(T // tq,),
        in_specs=[colblk(D_ATTN), full(T, D_ATTN), full(D_ATTN, T), full(T, LANES), colblk(N_HEADS)],
        out_specs=[pl.BlockSpec((tq, D_ATTN), lambda i: (i, 0)), colblk(N_HEADS)],
        out_shape=[jax.ShapeDtypeStruct((T, D_ATTN), F32), jax.ShapeDtypeStruct((N_HEADS, T), F32)],
        scratch_shapes=[pltpu.VMEM((D_ATTN, tq), F32), pltpu.VMEM((N_HEADS, tq), F32),
                        pltpu.VMEM((N_HEADS, tq), F32)],
        compiler_params=_params(40, 1),
    )(qT, k, vT, cum, cumT)


def _attn_bwd(q, qT, k, kT, v, do, doT, lseT, deltaT, cum, cumT, name, after=()):
    T = k.shape[0]
    tq = _tile(T, 256)
    tk = _tile(tq, 128)
    kpq = tq // tk
    heads = [slice(HEAD_DIM * h, HEAD_DIM * (h + 1)) for h in range(N_HEADS)]

    def body(q_ref, qT_ref, k_ref, kT_ref, v_ref, do_ref, doT_ref, lseT_ref, dlT_ref, cum_ref, cumT_ref, *rest):
        dq_ref, dk_ref, dv_ref, dcum_ref, dq_s = rest[len(after):]
        i = pl.program_id(0)

        @pl.when(i == 0)
        def _():
            dk_ref[...] = jnp.zeros_like(dk_ref)
            dv_ref[...] = jnp.zeros_like(dv_ref)
            dcum_ref[...] = jnp.zeros_like(dcum_ref)

        dq_s[...] = jnp.zeros_like(dq_s)

        def kblock(j, masked):
            rows = pl.ds(pl.multiple_of(j * tk, tk), tk)
            keep = _causal_keep(i, j, tq, tk) if masked else None
            qk = [_mm(k_ref[rows, hs], qT_ref[hs, :]) for hs in heads]
            dps = [_mm(v_ref[rows, hs], doT_ref[hs, :]) for hs in heads]
            for h, hs in enumerate(heads):
                sT = qk[h] * SCALE + (cumT_ref[h:h + 1, :] - cum_ref[rows, h:h + 1])
                if masked:
                    sT = jnp.where(keep, sT, NEG)
                pT = jnp.exp(sT - lseT_ref[h:h + 1, :])
                dsT = pT * (dps[h] - dlT_ref[h:h + 1, :])
                dcum_ref[rows, h:h + 1] += -jnp.sum(dsT, axis=1, keepdims=True)
                dsb = dsT.astype(MXU)
                dv_ref[rows, hs] += _mm(pT.astype(MXU), do_ref[:, hs])
                dk_ref[rows, hs] += _mm(dsb, q_ref[:, hs]) * SCALE
                dq_s[hs, :] += _mm(kT_ref[hs, rows], dsb)

        def unmasked(j, c):
            kblock(j, False)
            return c

        lax.fori_loop(0, kpq * i, unmasked, 0)
        for d in range(kpq):
            kblock(kpq * i + d, True)
        dq_ref[...] = (dq_s[...] * SCALE).T.astype(dq_ref.dtype)

    row = pl.BlockSpec((tq, D_ATTN), lambda i: (i, 0))
    full = lambda a, b: pl.BlockSpec((a, b), lambda i: (0, 0))
    colblk = lambda r: pl.BlockSpec((r, tq), lambda i: (0, i))
    return pl.pallas_call(
        body, name=name, grid=(T // tq,),
        in_specs=[row, colblk(D_ATTN), full(T, D_ATTN), full(D_ATTN, T), full(T, D_ATTN), row, colblk(D_ATTN),
                  colblk(N_HEADS), colblk(N_HEADS), full(T, LANES), colblk(N_HEADS)] + [_UNREAD] * len(after),
        out_specs=[row, full(T, D_ATTN), full(T, D_ATTN), full(T, LANES)],
        out_shape=[
            jax.ShapeDtypeStruct((T, D_ATTN), MXU),
            jax.ShapeDtypeStruct((T, D_ATTN), F32),
            jax.ShapeDtypeStruct((T, D_ATTN), F32),
            jax.ShapeDtypeStruct((T, LANES), F32),
        ],
        scratch_shapes=[pltpu.VMEM((D_ATTN, tq), F32)],
        compiler_params=_params(48, 1),
    )(q, qT, k, kT, v, do, doT, lseT, deltaT, cum, cumT, *after)


_ROWS_PER_CHUNK = 64


def _glu_halo(ag_ref, agh_ref, uext_s, tm, first):
    a = ag_ref[:, :D_CONV]
    sg = jax.nn.sigmoid(ag_ref[:, D_CONV:])
    uh = agh_ref[:, :D_CONV] * jax.nn.sigmoid(agh_ref[:, D_CONV:])
    uext_s[0:CONV_HALO, :] = jnp.where(first, 0.0, uh)
    uext_s[CONV_HALO:CONV_HALO + tm, :] = a * sg
    return a, sg


def _layer_norm_stats(y):
    mu = jnp.mean(y, axis=-1, keepdims=True)
    xc = y - mu
    rs = lax.rsqrt(jnp.mean(xc * xc, axis=-1, keepdims=True) + EPS)
    return xc * rs, rs


def _conv_fwd(ag, w32, cb, lg, lb, name):
    T = ag.shape[0]
    tm = _tile(T, 256)
    rc = _tile(tm, _ROWS_PER_CHUNK)
    hb = tm // CONV_HALO

    def body(ag_ref, agh_ref, w_ref, cb_ref, lg_ref, lb_ref, yc_ref, c_ref, uext_s):
        i = pl.program_id(0)
        _glu_halo(ag_ref, agh_ref, uext_s, tm, i == 0)
        for r0 in range(0, tm, rc):
            acc = jnp.zeros((rc, D_CONV), F32)
            for t in range(CONV_TAPS):
                acc = acc + uext_s[pl.ds(r0 + CONV_HALO - (CONV_TAPS - 1) + t, rc), :] * w_ref[t:t + 1, :]
            y = acc + cb_ref[...]
            yc_ref[r0:r0 + rc, :] = y
            n, _ = _layer_norm_stats(y)
            z = n * lg_ref[...] + lb_ref[...]
            c_ref[r0:r0 + rc, :] = z * jax.nn.sigmoid(z)

    row = lambda w: pl.BlockSpec((tm, w), lambda i: (i, 0))
    full = lambda a, b: pl.BlockSpec((a, b), lambda i: (0, 0))
    return pl.pallas_call(
        body, name=name, grid=(T // tm,),
        in_specs=[row(2 * D_CONV),
                  pl.BlockSpec((CONV_HALO, 2 * D_CONV), lambda i: (jnp.maximum(i * hb - 1, 0), 0)),
                  full(CONV_HALO, D_CONV), full(1, D_CONV), full(1, D_CONV), full(1, D_CONV)],
        out_specs=[row(D_CONV), row(D_CONV)],
        out_shape=[jax.ShapeDtypeStruct((T, D_CONV), F32), jax.ShapeDtypeStruct((T, D_CONV), F32)],
        scratch_shapes=[pltpu.VMEM((CONV_HALO + tm, D_CONV), F32)],
        compiler_params=_params(32, 1),
    )(ag, ag, w32, cb, lg, lb)


def _conv_bwd(dc, yc, ag, w32, lg, lb, name):
    T = ag.shape[0]
    tm = _tile(T, 256)
    rc = _tile(tm, _ROWS_PER_CHUNK)
    I = T // tm
    hb = tm // CONV_HALO
    n_halo_blocks = T // CONV_HALO

    def body(dc_ref, yc_ref, dch_ref, ych_ref, ag_ref, agh_ref, w_ref, lg_ref, lb_ref,
             dag_ref, dw_ref, dcb_ref, dlg_ref, dlb_ref, uext_s, dext_s):
        i = pl.program_id(0)
        lgv = lg_ref[...]
        lbv = lb_ref[...]

        def ln_bwd(dcv, ycv):
            n, rs = _layer_norm_stats(ycv)
            z = n * lgv + lbv
            dz = dcv * _silu_grad(z, jax.nn.sigmoid(z))
            dn = dz * lgv
            dy = rs * (dn - jnp.mean(dn, axis=-1, keepdims=True) - n * jnp.mean(dn * n, axis=-1, keepdims=True))
            return dy, dz, n

        dy, dz, n = ln_bwd(dc_ref[...], yc_ref[...])
        dyh, _, _ = ln_bwd(dch_ref[...], ych_ref[...])
        dext_s[0:tm, :] = dy
        dext_s[tm:tm + CONV_HALO, :] = jnp.where(i == I - 1, 0.0, dyh)
        a, sg = _glu_halo(ag_ref, agh_ref, uext_s, tm, i == 0)

        @pl.when(i == 0)
        def _():
            dw_ref[...] = jnp.zeros_like(dw_ref)
            dcb_ref[...] = jnp.zeros_like(dcb_ref)
            dlg_ref[...] = jnp.zeros_like(dlg_ref)
            dlb_ref[...] = jnp.zeros_like(dlb_ref)

        dcb_ref[...] += jnp.sum(dy, axis=0, keepdims=True)
        dlg_ref[...] += jnp.sum(dz * n, axis=0, keepdims=True)
        dlb_ref[...] += jnp.sum(dz, axis=0, keepdims=True)
        for t in range(CONV_TAPS):
            u_t = uext_s[pl.ds(CONV_HALO - (CONV_TAPS - 1) + t, tm), :]
            dw_ref[t:t + 1, :] += jnp.sum(dy * u_t, axis=0, keepdims=True)
        for r0 in range(0, tm, rc):
            acc = jnp.zeros((rc, D_CONV), F32)
            for t in range(CONV_TAPS):
                acc = acc + dext_s[pl.ds(r0 + (CONV_TAPS - 1) - t, rc), :] * w_ref[t:t + 1, :]
            a_c = a[r0:r0 + rc, :]
            sg_c = sg[r0:r0 + rc, :]
            dag_ref[r0:r0 + rc, :D_CONV] = (acc * sg_c).astype(dag_ref.dtype)
            dag_ref[r0:r0 + rc, D_CONV:] = (acc * a_c * sg_c * (1.0 - sg_c)).astype(dag_ref.dtype)

    row = lambda w: pl.BlockSpec((tm, w), lambda i: (i, 0))
    full = lambda a, b: pl.BlockSpec((a, b), lambda i: (0, 0))
    nxt = pl.BlockSpec((CONV_HALO, D_CONV), lambda i: (jnp.minimum((i + 1) * hb, n_halo_blocks - 1), 0))
    return pl.pallas_call(
        body, name=name, grid=(I,),
        in_specs=[row(D_CONV), row(D_CONV), nxt, nxt, row(2 * D_CONV),
                  pl.BlockSpec((CONV_HALO, 2 * D_CONV), lambda i: (jnp.maximum(i * hb - 1, 0), 0)),
                  full(CONV_HALO, D_CONV), full(1, D_CONV), full(1, D_CONV)],
        out_specs=[row(2 * D_CONV), full(CONV_HALO, D_CONV), full(1, D_CONV), full(1, D_CONV), full(1, D_CONV)],
        out_shape=[
            jax.ShapeDtypeStruct((T, 2 * D_CONV), MXU),
            jax.ShapeDtypeStruct((CONV_HALO, D_CONV), F32),
            jax.ShapeDtypeStruct((1, D_CONV), F32),
            jax.ShapeDtypeStruct((1, D_CONV), F32),
            jax.ShapeDtypeStruct((1, D_CONV), F32),
        ],
        scratch_shapes=[pltpu.VMEM((CONV_HALO + tm, D_CONV), F32), pltpu.VMEM((tm + CONV_HALO, D_CONV), F32)],
        compiler_params=_params(32, 1),
    )(dc, yc, dc, yc, ag, ag, w32, lg, lb)


def _outproj_fwd(x1, c, o, gc, ga, wout, name):
    T, D = x1.shape
    tm = _tile(T, 512)

    def body(x_ref, c_ref, o_ref, gc_ref, ga_ref, w_ref, x2_ref):
        yc, _ = _rms_fwd(c_ref[...], gc_ref[...])
        ya, _ = _rms_fwd(o_ref[...], ga_ref[...])
        x2_ref[...] = (x_ref[...] + _mm(yc.astype(MXU), w_ref[:D_CONV, :])
                       + _mm(ya.astype(MXU), w_ref[D_CONV:, :]))

    row = lambda w: pl.BlockSpec((tm, w), lambda i: (i, 0))
    full = lambda a, b: pl.BlockSpec((a, b), lambda i: (0, 0))
    return pl.pallas_call(
        body, name=name, grid=(T // tm,),
        in_specs=[row(D), row(D_CONV), row(D_ATTN), full(1, D_CONV), full(1, D_ATTN), full(D_CONV + D_ATTN, D)],
        out_specs=row(D),
        out_shape=jax.ShapeDtypeStruct((T, D), F32),
        compiler_params=_params(32, 1),
    )(x1, c, o, gc, ga, wout)


def _outproj_bwd(dx2, c, o, gc, ga, wout, name):
    T, D = dx2.shape
    tm = _tile(T, 256)
    I = T // tm

    def body(dx_ref, c_ref, o_ref, gc_ref, ga_ref, w_ref,
             dc_ref, do_ref, doT_ref, dlT_ref, dw_ref, dgc_ref, dga_ref, acc_s):
        i = pl.program_id(0)
        dxb = dx_ref[...].astype(MXU)
        cv = c_ref[...]
        ov = o_ref[...]
        yc, rcn = _rms_fwd(cv, gc_ref[...])
        ya, ra = _rms_fwd(ov, ga_ref[...])
        dyc = _mm_nt(dxb, w_ref[:D_CONV, :])
        dya = _mm_nt(dxb, w_ref[D_CONV:, :])
        dwc = _mm_tn(yc.astype(MXU), dxb)
        dwa = _mm_tn(ya.astype(MXU), dxb)
        dcv, dgc = _rms_bwd(cv, rcn, gc_ref[...], dyc)
        dov, dga = _rms_bwd(ov, ra, ga_ref[...], dya)
        dc_ref[...] = dcv
        dob = dov.astype(do_ref.dtype)
        do_ref[...] = dob
        doT_ref[...] = dov.T.astype(doT_ref.dtype)
        chan = lax.broadcasted_iota(jnp.int32, (D_ATTN, LANES), 0)
        head = lax.broadcasted_iota(jnp.int32, (D_ATTN, LANES), 1)
        in_head = ((chan >= head * HEAD_DIM) & (chan < (head + 1) * HEAD_DIM)).astype(jnp.bfloat16)
        dlT_ref[...] = _exact_dot_01(dob.astype(F32) * ov, in_head).T[:N_HEADS, :]

        @pl.when(i == 0)
        def _():
            acc_s[:D_CONV, :] = dwc
            acc_s[D_CONV:, :] = dwa
            dgc_ref[...] = dgc
            dga_ref[...] = dga

        @pl.when(i > 0)
        def _():
            acc_s[:D_CONV, :] += dwc
            acc_s[D_CONV:, :] += dwa
            dgc_ref[...] += dgc
            dga_ref[...] += dga

        @pl.when(i == I - 1)
        def _():
            dw_ref[...] = acc_s[...].astype(dw_ref.dtype)

    row = lambda w: pl.BlockSpec((tm, w), lambda i: (i, 0))
    full = lambda a, b: pl.BlockSpec((a, b), lambda i: (0, 0))
    return pl.pallas_call(
        body, name=name, grid=(I,),
        in_specs=[row(D), row(D_CONV), row(D_ATTN), full(1, D_CONV), full(1, D_ATTN), full(D_CONV + D_ATTN, D)],
        out_specs=[row(D_CONV), row(D_ATTN), pl.BlockSpec((D_ATTN, tm), lambda i: (0, i)),
                   pl.BlockSpec((N_HEADS, tm), lambda i: (0, i)),
                   full(D_CONV + D_ATTN, D), full(1, D_CONV), full(1, D_ATTN)],
        out_shape=[
            jax.ShapeDtypeStruct((T, D_CONV), F32),
            jax.ShapeDtypeStruct((T, D_ATTN), MXU),
            jax.ShapeDtypeStruct((D_ATTN, T), MXU),
            jax.ShapeDtypeStruct((N_HEADS, T), F32),
            jax.ShapeDtypeStruct((D_CONV + D_ATTN, D), MXU),
            jax.ShapeDtypeStruct((1, D_CONV), F32),
            jax.ShapeDtypeStruct((1, D_ATTN), F32),
        ],
        scratch_shapes=[pltpu.VMEM((D_CONV + D_ATTN, D), F32)],
        compiler_params=_params(40, 1),
    )(dx2, c, o, gc, ga, wout)


def _loss_head(x3, gf, target, name):
    T, D = x3.shape
    tm = _tile(T, 512)

    def body(x_ref, g_ref, t_ref, loss_ref, dx_ref, dg_ref):
        i = pl.program_id(0)
        xv = x_ref[...]
        gv = g_ref[...]
        out, r = _rms_fwd(xv, gv)
        err = out - t_ref[...]
        part = jnp.full((1, LANES), 0.5 / D, F32) * jnp.sum(err * err)
        dxn, dgp = _rms_bwd(xv, r, gv, err * (1.0 / D))
        dx_ref[...] = dxn

        @pl.when(i == 0)
        def _():
            loss_ref[...] = part
            dg_ref[...] = dgp

        @pl.when(i > 0)
        def _():
            loss_ref[...] += part
            dg_ref[...] += dgp

    row = lambda w: pl.BlockSpec((tm, w), lambda i: (i, 0))
    full = lambda a, b: pl.BlockSpec((a, b), lambda i: (0, 0))
    return pl.pallas_call(
        body, name=name, grid=(T // tm,),
        in_specs=[row(D), full(1, D), row(D)],
        out_specs=[full(1, LANES), row(D), full(1, D)],
        out_shape=[jax.ShapeDtypeStruct((1, LANES), F32), jax.ShapeDtypeStruct((T, D), F32),
                   jax.ShapeDtypeStruct((1, D), F32)],
        compiler_params=_params(32, 1),
    )(x3, gf, target)


def _row_tile(rows):
    for cand in (256, 176, 128, 64, 32, 16):
        if rows % cand == 0:
            return cand
    return rows


def _adamw(w, m, v, parts, name):
    R, C = w.shape
    P = parts.shape[0]
    tr = _row_tile(R)
    c1 = 1.0 - ADAM_B1 ** ADAM_STEP
    c2 = 1.0 - ADAM_B2 ** ADAM_STEP

    def body(w_ref, m_ref, v_ref, p_ref, g_ref, d_ref, nm_ref, nv_ref):
        g = p_ref[0].astype(F32)
        for s in range(1, P):
            g = g + p_ref[s].astype(F32)
        wv = w_ref[...]
        mn = ADAM_B1 * m_ref[...] + (1.0 - ADAM_B1) * g
        vn = ADAM_B2 * v_ref[...] + (1.0 - ADAM_B2) * (g * g)
        g_ref[...] = g
        nm_ref[...] = mn
        nv_ref[...] = vn
        d_ref[...] = -ADAM_LR * ((mn / c1) / (jnp.sqrt(vn / c2) + ADAM_EPS) + ADAM_WD * wv)

    blk = pl.BlockSpec((tr, C), lambda i: (i, 0))
    out = jax.ShapeDtypeStruct((R, C), F32)
    return pl.pallas_call(
        body, name=name, grid=(R // tr,),
        in_specs=[blk, blk, blk, pl.BlockSpec((P, tr, C), lambda i: (0, i, 0))],
        out_specs=[blk, blk, blk, blk],
        out_shape=[out, out, out, out],
        compiler_params=_params(32, 1),
    )(w, m, v, parts)


def _position():
    return lax.axis_index("x"), lax.axis_index("y"), lax.axis_index("c")


def _flat(px, py, pc):
    return 4 * px + 2 * py + pc


def _gather_body(ins, outs, send_sems, recv_sems, local_sems, handshake):
    n = len(ins)
    x, y, c = _position()
    me, sibling = (x, y, c), (x, y, 1 - c)
    chips = [(1 - x, y), (x, 1 - y), (1 - x, 1 - y)]
    if handshake:
        _handshake([sibling] + [(*chip, cc) for chip in chips for cc in (c, 1 - c)])

    def copy(a, k, block, to, src=None):
        dst = outs[a].at[_flat(*block)]
        return pltpu.make_async_remote_copy(
            src_ref=dst if src is None else src, dst_ref=dst,
            send_sem=send_sems.at[a, k], recv_sem=recv_sems.at[a, k],
            device_id=to, device_id_type=MESH)

    mine = [pltpu.make_async_copy(ins[a], outs[a].at[_flat(*me)], local_sems.at[a]) for a in range(n)]
    for cp in mine:
        cp.start()
    first = []
    for a in range(n):
        first.append(copy(a, 0, me, sibling, src=ins[a]))
        first += [copy(a, 1 + j, me, (*chip, c), src=ins[a]) for j, chip in enumerate(chips)]
    for cp in first:
        cp.start()
    passed = []
    for a in range(n):
        for j, chip in enumerate(chips):
            copy(a, 1 + j, (*chip, c), me).wait_recv()
            fwd = copy(a, 4 + j, (*chip, c), sibling)
            fwd.start()
            passed.append(fwd)
    for a in range(n):
        copy(a, 0, sibling, me).wait_recv()
        for j, chip in enumerate(chips):
            copy(a, 4 + j, (*chip, 1 - c), me).wait_recv()
    for cp in first + passed:
        cp.wait_send()
    for cp in mine:
        cp.wait()


def _gather_scratch(n):
    return [pltpu.SemaphoreType.DMA((n, 7)), pltpu.SemaphoreType.DMA((n, 7)), pltpu.SemaphoreType.DMA((n,))]


def _all_gather(shards, name):
    n = len(shards)

    def body(*refs):
        _gather_body(refs[:n], refs[n:2 * n], *refs[2 * n:], handshake=False)

    hbm = pl.BlockSpec(memory_space=pltpu.HBM)
    return pl.pallas_call(
        body, name=name,
        in_specs=[hbm] * n, out_specs=[hbm] * n,
        out_shape=[jax.ShapeDtypeStruct((N_DEV,) + s.shape, s.dtype) for s in shards],
        scratch_shapes=_gather_scratch(n),
    )(*shards)


def _handshake(peers):
    barrier = pltpu.get_barrier_semaphore()
    for peer in peers:
        pl.semaphore_signal(barrier, inc=1, device_id=peer, device_id_type=MESH)
    pl.semaphore_wait(barrier, len(peers))


def _sequencer_call(body, name, collective_id, out_type, scratch_types, operands):
    return pl.kernel(
        body, name=name, out_type=out_type,
        mesh=plsc.ScalarSubcoreMesh(axis_name="sequencer", num_cores=1),
        scratch_types=scratch_types,
        compiler_params=pltpu.CompilerParams(collective_id=collective_id),
    )(*operands)


def _seq_all_gather(shards, name, collective_id, after):
    n = len(shards)

    def body(*refs):
        _gather_body(refs[:n], refs[n + 1:2 * n + 1], *refs[2 * n + 1:], handshake=True)

    return _sequencer_call(
        body, name, collective_id,
        [jax.ShapeDtypeStruct((N_DEV,) + s.shape, s.dtype) for s in shards],
        _gather_scratch(n), list(shards) + [after])


def _seq_to_sibling(parts, name, collective_id, after):
    n = len(parts)

    def body(*refs):
        ins, outs = refs[:n], refs[n + len(after):2 * n + len(after)]
        send_sems, recv_sems = refs[2 * n + len(after):]
        x, y, c = _position()
        sibling = (x, y, 1 - c)
        _handshake([sibling])
        sent = []
        for a in range(n):
            for q in range(N_CHIPS):
                cp = pltpu.make_async_remote_copy(
                    src_ref=ins[a].at[2 * q + (1 - c)], dst_ref=outs[a].at[q],
                    send_sem=send_sems.at[a, q], recv_sem=recv_sems.at[a, q],
                    device_id=sibling, device_id_type=MESH)
                cp.start()
                sent.append(cp)
        for cp in sent:
            cp.wait_recv()
        for cp in sent:
            cp.wait_send()

    return _sequencer_call(
        body, name, collective_id,
        [jax.ShapeDtypeStruct((N_CHIPS,) + p.shape[1:], p.dtype) for p in parts],
        [pltpu.SemaphoreType.DMA((n, N_CHIPS)), pltpu.SemaphoreType.DMA((n, N_CHIPS))],
        list(parts) + list(after))


def _seq_to_chips(partials, name, collective_id):
    n = len(partials)

    def body(*refs):
        ins, outs = refs[:n], refs[n:2 * n]
        send_sems, recv_sems, local_sems = refs[2 * n:]
        x, y, c = _position()
        my_chip = 2 * x + y
        chips = [(1 - x, y), (x, 1 - y), (1 - x, 1 - y)]
        _handshake([(*chip, c) for chip in chips])
        mine = [pltpu.make_async_copy(ins[a].at[my_chip], outs[a].at[my_chip], local_sems.at[a]) for a in range(n)]
        for cp in mine:
            cp.start()
        sent = []
        for a in range(n):
            for j, (px, py) in enumerate(chips):
                cp = pltpu.make_async_remote_copy(
                    src_ref=ins[a].at[2 * px + py], dst_ref=outs[a].at[my_chip],
                    send_sem=send_sems.at[a, j], recv_sem=recv_sems.at[a, j],
                    device_id=(px, py, c), device_id_type=MESH)
                cp.start()
                sent.append(cp)
        for cp in sent:
            cp.wait_recv()
        for cp in sent:
            cp.wait_send()
        for cp in mine:
            cp.wait()

    return _sequencer_call(
        body, name, collective_id,
        [jax.ShapeDtypeStruct(p.shape, p.dtype) for p in partials],
        [pltpu.SemaphoreType.DMA((n, 3)), pltpu.SemaphoreType.DMA((n, 3)), pltpu.SemaphoreType.DMA((n,))],
        partials)


def _pair_add(part, recv, name):
    _, R, C = part.shape
    tr = _row_tile(R)
    pairs = part.reshape(N_CHIPS, 2, R, C)

    def body(p_ref, r_ref, o_ref):
        c = lax.axis_index("c")
        o_ref[...] = (p_ref[c].astype(F32) + r_ref[...].astype(F32)).astype(o_ref.dtype)

    return pl.pallas_call(
        body, name=name, grid=(N_CHIPS, R // tr),
        in_specs=[pl.BlockSpec((None, 2, tr, C), lambda q, i: (q, 0, i, 0)),
                  pl.BlockSpec((None, tr, C), lambda q, i: (q, i, 0))],
        out_specs=pl.BlockSpec((None, tr, C), lambda q, i: (q, i, 0)),
        out_shape=jax.ShapeDtypeStruct((N_CHIPS, R, C), part.dtype),
        compiler_params=_params(32, 2),
    )(pairs, recv)


class _Reduced(NamedTuple):
    partials: list
    reduced: list


def _blocks(g):
    return g.reshape(N_DEV, -1, g.shape[-1])


def _reduce_scatter(parts, tag, ids, after=()):
    from_sibling = _seq_to_sibling(parts, "rs_sibling_" + tag, ids[0], after)
    partials = [_pair_add(p, r, "rs_add_%s_%d" % (tag, a)) for a, (p, r) in enumerate(zip(parts, from_sibling))]
    return _Reduced(partials, _seq_to_chips(partials, "rs_chips_" + tag, ids[1]))


def _pack(arrays):
    flat = []
    for a in arrays:
        v = a.reshape(-1)
        pad = (-v.shape[0]) % LANES
        flat.append(jnp.pad(v, (0, pad)) if pad else v)
    v = jnp.concatenate(flat)
    pad = (-v.shape[0]) % (8 * LANES)
    return jnp.pad(v, (0, pad)).reshape(-1, LANES)


def _unpack(packed, shapes):
    v = packed.reshape(-1)
    out, off = [], 0
    for s in shapes:
        size = 1
        for d in s:
            size *= d
        out.append(v[off:off + size].reshape(s))
        off += size + (-size) % LANES
    return out


def _local_step(x, target, norms, w13_1, w2_1, winp, conv_w32, wout, w13_2, w2_2):
    x1, G1, U1 = _ffn_fwd(x, norms["ffn1_norm"], w13_1, w2_1, "ffn1_fwd")
    ag, q, k, v, qT, kT, vT, fl = _inproj_fwd(x1, norms["mix_norm"], winp, "inproj_fwd")
    cum, cumT = _forget_fwd(fl, norms["forget_b"], "forget_fwd")
    yc, c = _conv_fwd(ag, conv_w32, norms["conv_b"], norms["conv_ln_g"], norms["conv_ln_b"], "conv_fwd")
    o, lseT = _attn_fwd(qT, k, vT, cum, cumT, "attn_fwd")
    x2 = _outproj_fwd(x1, c, o, norms["out_norm_conv"], norms["out_norm_attn"], wout, "outproj_fwd")
    x3, G2, U2 = _ffn_fwd(x2, norms["ffn2_norm"], w13_2, w2_2, "ffn2_fwd")
    loss, dx3, d_final = _loss_head(x3, norms["final_norm"], target, "loss_head")

    dx2, dw13_2, dw2_2, d_ffn2n = _ffn_bwd(x2, norms["ffn2_norm"], dx3, G2, U2, w13_2, w2_2, "ffn2_bwd")
    dc, dob, dobT, deltaT, dwout, d_onc, d_ona = _outproj_bwd(
        dx2, c, o, norms["out_norm_conv"], norms["out_norm_attn"], wout, "outproj_bwd")
    red_ffn2 = _reduce_scatter([_blocks(dw13_2), _blocks(dw2_2)], "ffn2", (3, 4))
    dq, dk, dv, dcum = _attn_bwd(q, qT, k, kT, v, dob, dobT, lseT, deltaT, cum, cumT, "attn_bwd",
                                 after=red_ffn2.partials)
    dfl, d_fb = _forget_bwd(dcum, fl, norms["forget_b"], "forget_bwd")
    dag, d_convw, d_cb, d_lg, d_lb = _conv_bwd(dc, yc, ag, conv_w32, norms["conv_ln_g"], norms["conv_ln_b"], "conv_bwd")
    dx1, dwinp, d_mixn = _inproj_bwd(x1, norms["mix_norm"], dx2, dag, dq, dk, dv, dfl, winp, "inproj_bwd")
    D = x.shape[1]
    dwin_blocks = dwinp[:, :N_IN].reshape(D, N_DEV, N_IN // N_DEV).transpose(1, 0, 2)
    red_mix = _reduce_scatter([dwin_blocks, _blocks(dwout)], "mix", (5, 6), after=red_ffn2.reduced[:1])
    dx, dw13_1, dw2_1, d_ffn1n = _ffn_bwd(x, norms["ffn1_norm"], dx1, G1, U1, w13_1, w2_1, "ffn1_bwd",
                                          after=red_mix.partials)
    red_ffn1 = _reduce_scatter([_blocks(dw13_1), _blocks(dw2_1)], "ffn1", (7, 8), after=red_mix.reduced[:1])

    small = dict(ffn1_norm=d_ffn1n, mix_norm=d_mixn, conv_b=d_cb, conv_ln_g=d_lg, conv_ln_b=d_lb,
                 forget_b=d_fb, out_norm_conv=d_onc, out_norm_attn=d_ona, ffn2_norm=d_ffn2n,
                 final_norm=d_final, conv_w=d_convw)
    big = dict(ffn1_w13=red_ffn1.reduced[0], ffn1_w2=red_ffn1.reduced[1], w_in=red_mix.reduced[0],
               w_out=red_mix.reduced[1], ffn2_w13=red_ffn2.reduced[0], ffn2_w2=red_ffn2.reduced[1])
    return loss[0, 0], dx, small, big


_SMALL = ("ffn1_norm", "mix_norm", "conv_b", "conv_ln_g", "conv_ln_b", "forget_b", "out_norm_conv",
          "out_norm_attn", "ffn2_norm", "final_norm")
_BIG = ("ffn1_w13", "ffn1_w2", "w_in", "w_out", "ffn2_w13", "ffn2_w2")
_ORDER = ("ffn1_norm", "ffn1_w13", "ffn1_w2", "mix_norm", "w_in", "conv_w", "conv_b", "conv_ln_g", "conv_ln_b",
          "forget_b", "out_norm_conv", "out_norm_attn", "w_out", "ffn2_norm", "ffn2_w13", "ffn2_w2", "final_norm")


def kernel(x, ffn1_norm, ffn1_w13, ffn1_w2, mix_norm, w_in, conv_w, conv_b, conv_ln_g, conv_ln_b, forget_b, out_norm_conv, out_norm_attn, w_out, ffn2_norm, ffn2_w13, ffn2_w2, final_norm, loss_target, m_ffn1_norm, m_ffn1_w13, m_ffn1_w2, m_mix_norm, m_w_in, m_conv_w, m_conv_b, m_conv_ln_g, m_conv_ln_b, m_forget_b, m_out_norm_conv, m_out_norm_attn, m_w_out, m_ffn2_norm, m_ffn2_w13, m_ffn2_w2, m_final_norm, v_ffn1_norm, v_ffn1_w13, v_ffn1_w2, v_mix_norm, v_w_in, v_conv_w, v_conv_b, v_conv_ln_g, v_conv_ln_b, v_forget_b, v_out_norm_conv, v_out_norm_attn, v_w_out, v_ffn2_norm, v_ffn2_w13, v_ffn2_w2, v_final_norm):
    w = dict(ffn1_norm=ffn1_norm, ffn1_w13=ffn1_w13, ffn1_w2=ffn1_w2, mix_norm=mix_norm, w_in=w_in, conv_w=conv_w,
             conv_b=conv_b, conv_ln_g=conv_ln_g, conv_ln_b=conv_ln_b, forget_b=forget_b, out_norm_conv=out_norm_conv,
             out_norm_attn=out_norm_attn, w_out=w_out, ffn2_norm=ffn2_norm, ffn2_w13=ffn2_w13, ffn2_w2=ffn2_w2,
             final_norm=final_norm)
    m = dict(ffn1_norm=m_ffn1_norm, ffn1_w13=m_ffn1_w13, ffn1_w2=m_ffn1_w2, mix_norm=m_mix_norm, w_in=m_w_in,
             conv_w=m_conv_w, conv_b=m_conv_b, conv_ln_g=m_conv_ln_g, conv_ln_b=m_conv_ln_b, forget_b=m_forget_b,
             out_norm_conv=m_out_norm_conv, out_norm_attn=m_out_norm_attn, w_out=m_w_out, ffn2_norm=m_ffn2_norm,
             ffn2_w13=m_ffn2_w13, ffn2_w2=m_ffn2_w2, final_norm=m_final_norm)
    v = dict(ffn1_norm=v_ffn1_norm, ffn1_w13=v_ffn1_w13, ffn1_w2=v_ffn1_w2, mix_norm=v_mix_norm, w_in=v_w_in,
             conv_w=v_conv_w, conv_b=v_conv_b, conv_ln_g=v_conv_ln_g, conv_ln_b=v_conv_ln_b, forget_b=v_forget_b,
             out_norm_conv=v_out_norm_conv, out_norm_attn=v_out_norm_attn, w_out=v_w_out, ffn2_norm=v_ffn2_norm,
             ffn2_w13=v_ffn2_w13, ffn2_w2=v_ffn2_w2, final_norm=v_final_norm)
    shapes = {n: a.shape for n, a in w.items()}
    T, D = x.shape[1], x.shape[2]
    two = lambda a: a.reshape(a.shape[-2], a.shape[-1]) if a.ndim == 3 else a.reshape(1, -1)
    w2d = {n: two(a) for n, a in w.items()}
    m2d = {n: two(a) for n, a in m.items()}
    v2d = {n: two(a) for n, a in v.items()}

    cast = lambda n: w2d[n].astype(MXU)
    g13_1, g2_1 = _all_gather([cast("ffn1_w13"), cast("ffn1_w2")], "gather_ffn1")
    gin, gout, gconv = _seq_all_gather([cast("w_in"), cast("w_out"), w2d["conv_w"]], "gather_mix", 1,
                                       after=g13_1)
    g13_2, g2_2 = _seq_all_gather([cast("ffn2_w13"), cast("ffn2_w2")], "gather_ffn2", 2, after=g2_1)
    bf = g13_1.shape[-1]
    J = N_DEV // 2
    w13_1 = g13_1.reshape(2, J, D, bf)
    w13_2 = g13_2.reshape(2, J, D, bf)
    w2_1 = g2_1.reshape(-1, D)
    w2_2 = g2_2.reshape(-1, D)
    wout = gout.reshape(-1, D)
    winp = jnp.pad(gin.transpose(1, 0, 2).reshape(D, N_IN), ((0, 0), (0, N_IN_PAD - N_IN)))
    conv_full = gconv.transpose(1, 0, 2).reshape(CONV_TAPS, D_CONV)
    conv_w32 = jnp.pad(conv_full, ((0, CONV_HALO - CONV_TAPS), (0, 0)))

    norms = {n: w2d[n] for n in _SMALL}
    norms["forget_b"] = jnp.pad(w2d["forget_b"], ((0, 0), (0, LANES - N_HEADS)))
    loss_part, dx, small, big = _local_step(x[0], loss_target[0], norms, w13_1, w2_1, winp, conv_w32, wout,
                                            w13_2, w2_2)
    loss = lax.psum(loss_part, ("x", "y", "c"))

    grads, deltas, new_m, new_v = {}, {}, {}, {}
    for n in _BIG:
        g, d, nm, nv = _adamw(w2d[n], m2d[n], v2d[n], big[n], "adamw_" + n)
        grads[n], deltas[n], new_m[n], new_v[n] = g, d, nm, nv

    small["forget_b"] = small["forget_b"][:, :N_HEADS]
    small_shapes = [tuple(small[n].shape) for n in _SMALL] + [(CONV_HALO, D_CONV)]
    packed_g = _pack([small[n] for n in _SMALL] + [small["conv_w"]])
    (gathered_small,) = _all_gather([packed_g], "gather_small_grads")
    zero_conv = jnp.zeros((CONV_HALO, D_CONV), F32)
    pw = _pack([w2d[n] for n in _SMALL] + [zero_conv])
    pm = _pack([m2d[n] for n in _SMALL] + [zero_conv])
    pv = _pack([v2d[n] for n in _SMALL] + [zero_conv + 1.0])
    sg, sd, sm, sv = _adamw(pw, pm, pv, gathered_small, "adamw_small")
    for name_, g, d, nm, nv in zip(_SMALL, *[_unpack(t, small_shapes)[:len(_SMALL)] for t in (sg, sd, sm, sv)]):
        grads[name_], deltas[name_], new_m[name_], new_v[name_] = g, d, nm, nv
    conv_g_full = _unpack(sg, small_shapes)[-1][:CONV_TAPS]
    xi, yi, ci = _position()
    cw = shapes["conv_w"][-1]
    conv_g_mine = lax.dynamic_slice_in_dim(conv_g_full, _flat(xi, yi, ci) * cw, cw, axis=1)
    g, d, nm, nv = _adamw(w2d["conv_w"], m2d["conv_w"], v2d["conv_w"], conv_g_mine[None], "adamw_conv_w")
    grads["conv_w"], deltas["conv_w"], new_m["conv_w"], new_v["conv_w"] = g, d, nm, nv

    shaped = lambda dct: [dct[n].reshape(shapes[n]) for n in _ORDER]
    return (loss, dx[None], *shaped(grads), *shaped(deltas), *shaped(new_m), *shaped(new_v))
```

```python
from typing import NamedTuple

import jax
import jax.numpy as jnp
from jax import lax
from jax.experimental import pallas as pl
from jax.experimental.pallas import tpu as pltpu
from jax.experimental.pallas import tpu_sc as plsc

F32 = jnp.float32
MXU = jnp.bfloat16
EPS = 1e-6
N_HEADS = 8
HEAD_DIM = 64
D_CONV = 512
D_ATTN = N_HEADS * HEAD_DIM
CONV_TAPS = 31
CONV_HALO = 32
SCALE = HEAD_DIM ** -0.5
NEG = -1e30
LANES = 128
N_DEV = 8
N_CHIPS = N_DEV // 2
MESH = pl.DeviceIdType.MESH
MIB = 1 << 20

ADAM_LR = 0.001
ADAM_B1 = 0.9
ADAM_B2 = 0.999
ADAM_EPS = 1e-08
ADAM_WD = 0.01
ADAM_STEP = 10


_UNREAD = pl.BlockSpec(memory_space=pl.ANY)


def _pallas_call(body, *, out_shape, **kwargs):
    in_hbm = lambda s: pltpu.HBM(s.shape, s.dtype)
    outs = [in_hbm(s) for s in out_shape] if isinstance(out_shape, (list, tuple)) else in_hbm(out_shape)
    call = pl.pallas_call(body, out_shape=outs, **kwargs)
    return lambda *operands: call(*[pltpu.with_memory_space_constraint(a, pltpu.HBM) for a in operands])


def _params(vmem_mib, n_axes):
    return pltpu.CompilerParams(dimension_semantics=("arbitrary",) * n_axes, vmem_limit_bytes=vmem_mib * MIB)


def _mm(a, b):
    return jnp.dot(a, b, preferred_element_type=F32)


def _mm_nt(a, b):
    return lax.dot_general(a, b, (((1,), (1,)), ((), ())), preferred_element_type=F32)


def _mm_tn(a, b):
    return lax.dot_general(a, b, (((0,), (0,)), ((), ())), preferred_element_type=F32)


def _rms_fwd(x, g):
    r = lax.rsqrt(jnp.mean(x * x, axis=-1, keepdims=True) + EPS)
    return x * r * g, r


def _rms_bwd(x, r, g, dy):
    gdy = dy * g
    dx = r * gdy - x * (r * r * r) * jnp.mean(x * gdy, axis=-1, keepdims=True)
    dg = jnp.sum(dy * x * r, axis=0, keepdims=True)
    return dx, dg


def _silu_grad(z, sz):
    return sz * (1.0 + z * (1.0 - sz))


def _three_terms(x):
    x1 = x.astype(jnp.bfloat16)
    r1 = x - x1.astype(F32)
    x2 = r1.astype(jnp.bfloat16)
    x3 = (r1 - x2.astype(F32)).astype(jnp.bfloat16)
    return x1, x2, x3


def _exact_tri_dot(tri, x):
    x1, x2, x3 = _three_terms(x)
    return _mm(tri, x1) + _mm(tri, x2) + _mm(tri, x3)


def _exact_dot_01(x, sel):
    x1, x2, x3 = _three_terms(x)
    return _mm(x1, sel) + _mm(x2, sel) + _mm(x3, sel)


def _tile(n, want):
    t = min(n, want)
    assert n % t == 0
    return t


def _ffn_fwd(x, g, w13, w2, name):
    T, D = x.shape
    _, J, bf, _ = w13.shape
    tm = _tile(T, 512)
    I = T // tm

    def body(x_ref, g_ref, w13_ref, w2_ref, xo_ref, G_ref, U_ref, acc_s):
        j = pl.program_id(0)
        i = pl.program_id(1)
        rows = pl.ds(pl.multiple_of(i * tm, tm), tm)
        xv = x_ref[...]
        h, _ = _rms_fwd(xv, g_ref[...])
        hb = h.astype(MXU)
        G = _mm_nt(hb, w13_ref[0])
        U = _mm_nt(hb, w13_ref[1])
        G_ref[...] = G.astype(MXU)
        U_ref[...] = U.astype(MXU)
        A = (G * jax.nn.sigmoid(G) * U).astype(MXU)
        F = _mm(A, w2_ref[...])

        @pl.when(j == 0)
        def _():
            acc_s[rows, :] = xv + 0.5 * F

        @pl.when(j > 0)
        def _():
            acc_s[rows, :] += 0.5 * F

        @pl.when(j == J - 1)
        def _():
            xo_ref[...] = acc_s[rows, :]

    return _pallas_call(
        body, name=name, grid=(J, I),
        in_specs=[
            pl.BlockSpec((tm, D), lambda j, i: (i, 0)),
            pl.BlockSpec((1, D), lambda j, i: (0, 0)),
            pl.BlockSpec((2, None, bf, D), lambda j, i: (0, j, 0, 0)),
            pl.BlockSpec((bf, D), lambda j, i: (j, 0)),
        ],
        out_specs=[
            pl.BlockSpec((tm, D), lambda j, i: (jnp.where(j == J - 1, i, 0), 0)),
            pl.BlockSpec((None, tm, bf), lambda j, i: (j, i, 0)),
            pl.BlockSpec((None, tm, bf), lambda j, i: (j, i, 0)),
        ],
        out_shape=[
            jax.ShapeDtypeStruct((T, D), F32),
            jax.ShapeDtypeStruct((J, T, bf), MXU),
            jax.ShapeDtypeStruct((J, T, bf), MXU),
        ],
        scratch_shapes=[pltpu.VMEM((T, D), F32)],
        compiler_params=_params(48, 2),
    )(x, g, w13, w2)


def _ffn_bwd(x, g, dy, Gs, Us, w13, w2, name, after=()):
    T, D = x.shape
    _, J, bf, _ = w13.shape
    tm = _tile(T, 256)
    I = T // tm

    def body(x_ref, g_ref, dy_ref, G_ref, U_ref, w13_ref, w2_ref, *rest):
        dx_ref, dw13_ref, dw2_ref, dg_ref, dh_s, a13_s, a2_s = rest[len(after):]
        j = pl.program_id(0)
        i = pl.program_id(1)
        rows = pl.ds(pl.multiple_of(i * tm, tm), tm)
        xv = x_ref[...]
        gv = g_ref[...]
        h, r = _rms_fwd(xv, gv)
        hb = h.astype(MXU)
        dyv = dy_ref[...]
        dFb = (0.5 * dyv).astype(MXU)
        G = G_ref[...].astype(F32)
        U = U_ref[...].astype(F32)
        sg = jax.nn.sigmoid(G)
        s = G * sg
        A = (s * U).astype(MXU)
        dA = _mm_nt(dFb, w2_ref[...])
        dUb = (dA * s).astype(MXU)
        dGb = (dA * U * _silu_grad(G, sg)).astype(MXU)
        dw2 = _mm_tn(A, dFb)
        dwg = _mm_tn(dGb, hb)
        dwu = _mm_tn(dUb, hb)
        dh = _mm(dGb, w13_ref[0]) + _mm(dUb, w13_ref[1])

        @pl.when(i == 0)
        def _():
            a13_s[0] = dwg
            a13_s[1] = dwu
            a2_s[...] = dw2

        @pl.when(i > 0)
        def _():
            a13_s[0] += dwg
            a13_s[1] += dwu
            a2_s[...] += dw2

        @pl.when(i == I - 1)
        def _():
            dw13_ref[...] = a13_s[...].astype(dw13_ref.dtype)
            dw2_ref[...] = a2_s[...].astype(dw2_ref.dtype)

        @pl.when(j == 0)
        def _():
            dh_s[rows, :] = dh

        @pl.when(j > 0)
        def _():
            dh_s[rows, :] += dh

        @pl.when(j == J - 1)
        def _():
            dxn, dgp = _rms_bwd(xv, r, gv, dh_s[rows, :])
            dx_ref[...] = dyv + dxn

            @pl.when(i == 0)
            def _():
                dg_ref[...] = dgp

            @pl.when(i > 0)
            def _():
                dg_ref[...] += dgp

    return _pallas_call(
        body, name=name, grid=(J, I),
        in_specs=[
            pl.BlockSpec((tm, D), lambda j, i: (i, 0)),
            pl.BlockSpec((1, D), lambda j, i: (0, 0)),
            pl.BlockSpec((tm, D), lambda j, i: (i, 0)),
            pl.BlockSpec((None, tm, bf), lambda j, i: (j, i, 0)),
            pl.BlockSpec((None, tm, bf), lambda j, i: (j, i, 0)),
            pl.BlockSpec((2, None, bf, D), lambda j, i: (0, j, 0, 0)),
            pl.BlockSpec((bf, D), lambda j, i: (j, 0)),
        ] + [_UNREAD] * len(after),
        out_specs=[
            pl.BlockSpec((tm, D), lambda j, i: (jnp.where(j == J - 1, i, 0), 0)),
            pl.BlockSpec((2, None, bf, D), lambda j, i: (0, j, 0, 0)),
            pl.BlockSpec((bf, D), lambda j, i: (j, 0)),
            pl.BlockSpec((1, D), lambda j, i: (0, 0)),
        ],
        out_shape=[
            jax.ShapeDtypeStruct((T, D), F32),
            jax.ShapeDtypeStruct(w13.shape, MXU),
            jax.ShapeDtypeStruct(w2.shape, MXU),
            jax.ShapeDtypeStruct((1, D), F32),
        ],
        scratch_shapes=[pltpu.VMEM((T, D), F32), pltpu.VMEM((2, bf, D), F32), pltpu.VMEM((bf, D), F32)],
        compiler_params=_params(58, 2),
    )(x, g, dy, Gs, Us, w13, w2, *after)


_AG0, _Q0, _K0, _V0, _F0 = 0, 2 * D_CONV, 2 * D_CONV + D_ATTN, 2 * D_CONV + 2 * D_ATTN, 2 * D_CONV + 3 * D_ATTN
N_IN = _F0 + N_HEADS
N_IN_PAD = _F0 + LANES


def _inproj_fwd(x1, gm, winp, name):
    T, D = x1.shape
    tm = _tile(T, 256)

    def body(x_ref, g_ref, w_ref, ag_ref, q_ref, k_ref, v_ref, qT_ref, kT_ref, vT_ref, fl_ref):
        h, _ = _rms_fwd(x_ref[...], g_ref[...])
        hb = h.astype(MXU)
        ag_ref[...] = _mm_nt(hb, w_ref[_AG0:_Q0, :])
        for c0, ref, refT in ((_Q0, q_ref, qT_ref), (_K0, k_ref, kT_ref), (_V0, v_ref, vT_ref)):
            y = _mm_nt(hb, w_ref[c0:c0 + D_ATTN, :])
            ref[...] = y.astype(MXU)
            refT[...] = y.T.astype(MXU)
        fl_ref[...] = _mm_nt(hb, w_ref[_F0:N_IN_PAD, :])

    row = lambda w: pl.BlockSpec((tm, w), lambda i: (i, 0))
    col = pl.BlockSpec((D_ATTN, tm), lambda i: (0, i))
    std = jax.ShapeDtypeStruct((T, D_ATTN), MXU)
    trn = jax.ShapeDtypeStruct((D_ATTN, T), MXU)
    return _pallas_call(
        body, name=name, grid=(T // tm,),
        in_specs=[row(D), pl.BlockSpec((1, D), lambda i: (0, 0)), pl.BlockSpec((N_IN_PAD, D), lambda i: (0, 0))],
        out_specs=[row(2 * D_CONV), row(D_ATTN), row(D_ATTN), row(D_ATTN), col, col, col, row(LANES)],
        out_shape=[jax.ShapeDtypeStruct((T, 2 * D_CONV), F32), std, std, std, trn, trn, trn,
                   jax.ShapeDtypeStruct((T, LANES), F32)],
        compiler_params=_params(40, 1),
    )(x1, gm, winp)


def _inproj_bwd(x1, gm, dx2, dag, dq, dk, dv, dfl, winp, name):
    T, D = x1.shape
    tm = _tile(T, 256)
    I = T // tm
    pieces = ((_AG0, _Q0), (_Q0, _K0), (_K0, _V0), (_V0, _F0), (_F0, N_IN_PAD))

    def body(x_ref, g_ref, dx2_ref, dag_ref, dq_ref, dk_ref, dv_ref, dfl_ref, w_ref,
             dx1_ref, dw_ref, dg_ref, acc_s):
        i = pl.program_id(0)
        xv = x_ref[...]
        gv = g_ref[...]
        h, r = _rms_fwd(xv, gv)
        hb = h.astype(MXU)
        dps = (dag_ref[...], dq_ref[...], dk_ref[...].astype(MXU), dv_ref[...].astype(MXU), dfl_ref[...].astype(MXU))
        dh = jnp.zeros((tm, D), F32)
        for (c0, c1), dp in zip(pieces, dps):
            dh = dh + _mm(dp, w_ref[c0:c1, :])
            dwp = _mm_tn(dp, hb)

            @pl.when(i == 0)
            def _():
                acc_s[c0:c1, :] = dwp

            @pl.when(i > 0)
            def _():
                acc_s[c0:c1, :] += dwp

        dxn, dgp = _rms_bwd(xv, r, gv, dh)
        dx1_ref[...] = dx2_ref[...] + dxn

        @pl.when(i == 0)
        def _():
            dg_ref[...] = dgp

        @pl.when(i > 0)
        def _():
            dg_ref[...] += dgp

        @pl.when(i == I - 1)
        def _():
            dw_ref[...] = acc_s[...].astype(dw_ref.dtype)

    row = lambda w: pl.BlockSpec((tm, w), lambda i: (i, 0))
    full = lambda a, b: pl.BlockSpec((a, b), lambda i: (0, 0))
    return _pallas_call(
        body, name=name, grid=(I,),
        in_specs=[row(D), full(1, D), row(D), row(2 * D_CONV), row(D_ATTN), row(D_ATTN), row(D_ATTN), row(LANES),
                  full(N_IN_PAD, D)],
        out_specs=[row(D), full(N_IN_PAD, D), full(1, D)],
        out_shape=[
            jax.ShapeDtypeStruct((T, D), F32),
            jax.ShapeDtypeStruct((N_IN_PAD, D), MXU),
            jax.ShapeDtypeStruct((1, D), F32),
        ],
        scratch_shapes=[pltpu.VMEM((N_IN_PAD, D), F32)],
        compiler_params=_params(56, 1),
    )(x1, gm, dx2, dag, dq, dk, dv, dfl, winp)


def _forget_fwd(fl, fbp, name):
    T = fl.shape[0]
    tb = _tile(T, 256)

    def body(fl_ref, fb_ref, cum_ref, cumT_ref):
        ri = lax.broadcasted_iota(jnp.int32, (tb, tb), 0)
        ci = lax.broadcasted_iota(jnp.int32, (tb, tb), 1)
        tri = (ri >= ci).astype(jnp.bfloat16)
        carry = jnp.zeros((1, LANES), F32)
        for b in range(T // tb):
            z = fl_ref[b * tb:(b + 1) * tb, :] + fb_ref[...]
            lf = jnp.minimum(z, 0.0) - jnp.log1p(jnp.exp(-jnp.abs(z)))
            c = _exact_tri_dot(tri, lf) + carry
            cum_ref[b * tb:(b + 1) * tb, :] = c
            carry = c[tb - 1:tb, :]
        cumT_ref[...] = cum_ref[...].T[:N_HEADS, :]

    return _pallas_call(
        body, name=name,
        out_shape=[jax.ShapeDtypeStruct((T, LANES), F32), jax.ShapeDtypeStruct((N_HEADS, T), F32)],
        compiler_params=pltpu.CompilerParams(vmem_limit_bytes=32 * MIB),
    )(fl, fbp)


def _forget_bwd(dcum, fl, fbp, name):
    T = fl.shape[0]
    tb = _tile(T, 256)

    def body(dc_ref, fl_ref, fb_ref, dfl_ref, dfb_ref):
        ri = lax.broadcasted_iota(jnp.int32, (tb, tb), 0)
        ci = lax.broadcasted_iota(jnp.int32, (tb, tb), 1)
        tri = (ri <= ci).astype(jnp.bfloat16)
        carry = jnp.zeros((1, LANES), F32)
        dfb = jnp.zeros((1, LANES), F32)
        for b in reversed(range(T // tb)):
            sl = slice(b * tb, (b + 1) * tb)
            dl = _exact_tri_dot(tri, dc_ref[sl, :]) + carry
            carry = dl[0:1, :]
            z = fl_ref[sl, :] + fb_ref[...]
            dfl = dl * jax.nn.sigmoid(-z)
            dfl_ref[sl, :] = dfl
            dfb = dfb + jnp.sum(dfl, axis=0, keepdims=True)
        dfb_ref[...] = dfb

    return _pallas_call(
        body, name=name,
        out_shape=[jax.ShapeDtypeStruct((T, LANES), F32), jax.ShapeDtypeStruct((1, LANES), F32)],
        compiler_params=pltpu.CompilerParams(vmem_limit_bytes=32 * MIB),
    )(dcum, fl, fbp)


def _causal_keep(i, j, tq, tk):
    key = j * tk + lax.broadcasted_iota(jnp.int32, (tk, tq), 0)
    qry = i * tq + lax.broadcasted_iota(jnp.int32, (tk, tq), 1)
    return key <= qry


def _split_hi_lo(x):
    hi = x.astype(MXU)
    lo = (x - hi.astype(F32)).astype(MXU)
    return hi, lo


def _attn_fwd(qT, k, vT, cum, cumT, name):
    T = k.shape[0]
    tq = _tile(T, 256)
    tk = _tile(tq, 128)
    kpq = tq // tk
    heads = [slice(HEAD_DIM * h, HEAD_DIM * (h + 1)) for h in range(N_HEADS)]

    def body(qT_ref, k_ref, vT_ref, cum_ref, cumT_ref, o_ref, lseT_ref, acc_s, m_s, l_s):
        i = pl.program_id(0)
        acc_s[...] = jnp.zeros_like(acc_s)
        m_s[...] = jnp.full_like(m_s, NEG)
        l_s[...] = jnp.zeros_like(l_s)

        def kblock(j, masked):
            rows = pl.ds(pl.multiple_of(j * tk, tk), tk)
            keep = _causal_keep(i, j, tq, tk) if masked else None
            qk = [_mm(k_ref[rows, hs], qT_ref[hs, :]) for hs in heads]
            for h, hs in enumerate(heads):
                sT = qk[h] * SCALE + (cumT_ref[h:h + 1, :] - cum_ref[rows, h:h + 1])
                if masked:
                    sT = jnp.where(keep, sT, NEG)
                m_old = m_s[h:h + 1, :]
                m_new = jnp.maximum(m_old, jnp.max(sT, axis=0, keepdims=True))
                alpha = jnp.exp(m_old - m_new)
                pT = jnp.exp(sT - m_new)
                l_s[h:h + 1, :] = alpha * l_s[h:h + 1, :] + jnp.sum(pT, axis=0, keepdims=True)
                p_hi, p_lo = _split_hi_lo(pT)
                vh = vT_ref[hs, rows]
                acc_s[hs, :] = alpha * acc_s[hs, :] + (_mm(vh, p_hi) + _mm(vh, p_lo))
                m_s[h:h + 1, :] = m_new

        def unmasked(j, c):
            kblock(j, False)
            return c

        lax.fori_loop(0, kpq * i, unmasked, 0)
        for d in range(kpq):
            kblock(kpq * i + d, True)
        for h, hs in enumerate(heads):
            acc_s[hs, :] = acc_s[hs, :] / l_s[h:h + 1, :]
        o_ref[...] = acc_s[...].T
        lseT_ref[...] = m_s[...] + jnp.log(l_s[...])

    full = lambda a, b: pl.BlockSpec((a, b), lambda i: (0, 0))
    colblk = lambda r: pl.BlockSpec((r, tq), lambda i: (0, i))
    return _pallas_call(
        body, name=name, grid=(T // tq,),
        in_specs=[colblk(D_ATTN), full(T, D_ATTN), full(D_ATTN, T), full(T, LANES), colblk(N_HEADS)],
        out_specs=[pl.BlockSpec((tq, D_ATTN), lambda i: (i, 0)), colblk(N_HEADS)],
        out_shape=[jax.ShapeDtypeStruct((T, D_ATTN), F32), jax.ShapeDtypeStruct((N_HEADS, T), F32)],
        scratch_shapes=[pltpu.VMEM((D_ATTN, tq), F32), pltpu.VMEM((N_HEADS, tq), F32),
                        pltpu.VMEM((N_HEADS, tq), F32)],
        compiler_params=_params(40, 1),
    )(qT, k, vT, cum, cumT)


def _attn_bwd(q, qT, k, kT, v, do, doT, lseT, deltaT, cum, cumT, name, after=()):
    T = k.shape[0]
    tq = _tile(T, 256)
    tk = _tile(tq, 128)
    kpq = tq // tk
    heads = [slice(HEAD_DIM * h, HEAD_DIM * (h + 1)) for h in range(N_HEADS)]

    def body(q_ref, qT_ref, k_ref, kT_ref, v_ref, do_ref, doT_ref, lseT_ref, dlT_ref, cum_ref, cumT_ref, *rest):
        dq_ref, dk_ref, dv_ref, dcum_ref, dq_s = rest[len(after):]
        i = pl.program_id(0)

        @pl.when(i == 0)
        def _():
            dk_ref[...] = jnp.zeros_like(dk_ref)
            dv_ref[...] = jnp.zeros_like(dv_ref)
            dcum_ref[...] = jnp.zeros_like(dcum_ref)

        dq_s[...] = jnp.zeros_like(dq_s)

        def kblock(j, masked):
            rows = pl.ds(pl.multiple_of(j * tk, tk), tk)
            keep = _causal_keep(i, j, tq, tk) if masked else None
            qk = [_mm(k_ref[rows, hs], qT_ref[hs, :]) for hs in heads]
            dps = [_mm(v_ref[rows, hs], doT_ref[hs, :]) for hs in heads]
            for h, hs in enumerate(heads):
                sT = qk[h] * SCALE + (cumT_ref[h:h + 1, :] - cum_ref[rows, h:h + 1])
                if masked:
                    sT = jnp.where(keep, sT, NEG)
                pT = jnp.exp(sT - lseT_ref[h:h + 1, :])
                dsT = pT * (dps[h] - dlT_ref[h:h + 1, :])
                dcum_ref[rows, h:h + 1] += -jnp.sum(dsT, axis=1, keepdims=True)
                dsb = dsT.astype(MXU)
                dv_ref[rows, hs] += _mm(pT.astype(MXU), do_ref[:, hs])
                dk_ref[rows, hs] += _mm(dsb, q_ref[:, hs]) * SCALE
                dq_s[hs, :] += _mm(kT_ref[hs, rows], dsb)

        def unmasked(j, c):
            kblock(j, False)
            return c

        lax.fori_loop(0, kpq * i, unmasked, 0)
        for d in range(kpq):
            kblock(kpq * i + d, True)
        dq_ref[...] = (dq_s[...] * SCALE).T.astype(dq_ref.dtype)

    row = pl.BlockSpec((tq, D_ATTN), lambda i: (i, 0))
    full = lambda a, b: pl.BlockSpec((a, b), lambda i: (0, 0))
    colblk = lambda r: pl.BlockSpec((r, tq), lambda i: (0, i))
    return _pallas_call(
        body, name=name, grid=(T // tq,),
        in_specs=[row, colblk(D_ATTN), full(T, D_ATTN), full(D_ATTN, T), full(T, D_ATTN), row, colblk(D_ATTN),
                  colblk(N_HEADS), colblk(N_HEADS), full(T, LANES), colblk(N_HEADS)] + [_UNREAD] * len(after),
        out_specs=[row, full(T, D_ATTN), full(T, D_ATTN), full(T, LANES)],
        out_shape=[
            jax.ShapeDtypeStruct((T, D_ATTN), MXU),
            jax.ShapeDtypeStruct((T, D_ATTN), F32),
            jax.ShapeDtypeStruct((T, D_ATTN), F32),
            jax.ShapeDtypeStruct((T, LANES), F32),
        ],
        scratch_shapes=[pltpu.VMEM((D_ATTN, tq), F32)],
        compiler_params=_params(48, 1),
    )(q, qT, k, kT, v, do, doT, lseT, deltaT, cum, cumT, *after)


_ROWS_PER_CHUNK = 64


def _glu_halo(ag_ref, agh_ref, uext_s, tm, first):
    a = ag_ref[:, :D_CONV]
    sg = jax.nn.sigmoid(ag_ref[:, D_CONV:])
    uh = agh_ref[:, :D_CONV] * jax.nn.sigmoid(agh_ref[:, D_CONV:])
    uext_s[0:CONV_HALO, :] = jnp.where(first, 0.0, uh)
    uext_s[CONV_HALO:CONV_HALO + tm, :] = a * sg
    return a, sg


def _layer_norm_stats(y):
    mu = jnp.mean(y, axis=-1, keepdims=True)
    xc = y - mu
    rs = lax.rsqrt(jnp.mean(xc * xc, axis=-1, keepdims=True) + EPS)
    return xc * rs, rs


def _conv_fwd(ag, w32, cb, lg, lb, name):
    T = ag.shape[0]
    tm = _tile(T, 256)
    rc = _tile(tm, _ROWS_PER_CHUNK)
    hb = tm // CONV_HALO

    def body(ag_ref, agh_ref, w_ref, cb_ref, lg_ref, lb_ref, yc_ref, c_ref, uext_s):
        i = pl.program_id(0)
        _glu_halo(ag_ref, agh_ref, uext_s, tm, i == 0)
        for r0 in range(0, tm, rc):
            acc = jnp.zeros((rc, D_CONV), F32)
            for t in range(CONV_TAPS):
                acc = acc + uext_s[pl.ds(r0 + CONV_HALO - (CONV_TAPS - 1) + t, rc), :] * w_ref[t:t + 1, :]
            y = acc + cb_ref[...]
            yc_ref[r0:r0 + rc, :] = y
            n, _ = _layer_norm_stats(y)
            z = n * lg_ref[...] + lb_ref[...]
            c_ref[r0:r0 + rc, :] = z * jax.nn.sigmoid(z)

    row = lambda w: pl.BlockSpec((tm, w), lambda i: (i, 0))
    full = lambda a, b: pl.BlockSpec((a, b), lambda i: (0, 0))
    return _pallas_call(
        body, name=name, grid=(T // tm,),
        in_specs=[row(2 * D_CONV),
                  pl.BlockSpec((CONV_HALO, 2 * D_CONV), lambda i: (jnp.maximum(i * hb - 1, 0), 0)),
                  full(CONV_HALO, D_CONV), full(1, D_CONV), full(1, D_CONV), full(1, D_CONV)],
        out_specs=[row(D_CONV), row(D_CONV)],
        out_shape=[jax.ShapeDtypeStruct((T, D_CONV), F32), jax.ShapeDtypeStruct((T, D_CONV), F32)],
        scratch_shapes=[pltpu.VMEM((CONV_HALO + tm, D_CONV), F32)],
        compiler_params=_params(32, 1),
    )(ag, ag, w32, cb, lg, lb)


def _conv_bwd(dc, yc, ag, w32, lg, lb, name):
    T = ag.shape[0]
    tm = _tile(T, 256)
    rc = _tile(tm, _ROWS_PER_CHUNK)
    I = T // tm
    hb = tm // CONV_HALO
    n_halo_blocks = T // CONV_HALO

    def body(dc_ref, yc_ref, dch_ref, ych_ref, ag_ref, agh_ref, w_ref, lg_ref, lb_ref,
             dag_ref, dw_ref, dcb_ref, dlg_ref, dlb_ref, uext_s, dext_s):
        i = pl.program_id(0)
        lgv = lg_ref[...]
        lbv = lb_ref[...]

        def ln_bwd(dcv, ycv):
            n, rs = _layer_norm_stats(ycv)
            z = n * lgv + lbv
            dz = dcv * _silu_grad(z, jax.nn.sigmoid(z))
            dn = dz * lgv
            dy = rs * (dn - jnp.mean(dn, axis=-1, keepdims=True) - n * jnp.mean(dn * n, axis=-1, keepdims=True))
            return dy, dz, n

        dy, dz, n = ln_bwd(dc_ref[...], yc_ref[...])
        dyh, _, _ = ln_bwd(dch_ref[...], ych_ref[...])
        dext_s[0:tm, :] = dy
        dext_s[tm:tm + CONV_HALO, :] = jnp.where(i == I - 1, 0.0, dyh)
        a, sg = _glu_halo(ag_ref, agh_ref, uext_s, tm, i == 0)

        @pl.when(i == 0)
        def _():
            dw_ref[...] = jnp.zeros_like(dw_ref)
            dcb_ref[...] = jnp.zeros_like(dcb_ref)
            dlg_ref[...] = jnp.zeros_like(dlg_ref)
            dlb_ref[...] = jnp.zeros_like(dlb_ref)

        dcb_ref[...] += jnp.sum(dy, axis=0, keepdims=True)
        dlg_ref[...] += jnp.sum(dz * n, axis=0, keepdims=True)
        dlb_ref[...] += jnp.sum(dz, axis=0, keepdims=True)
        for t in range(CONV_TAPS):
            u_t = uext_s[pl.ds(CONV_HALO - (CONV_TAPS - 1) + t, tm), :]
            dw_ref[t:t + 1, :] += jnp.sum(dy * u_t, axis=0, keepdims=True)
        for r0 in range(0, tm, rc):
            acc = jnp.zeros((rc, D_CONV), F32)
            for t in range(CONV_TAPS):
                acc = acc + dext_s[pl.ds(r0 + (CONV_TAPS - 1) - t, rc), :] * w_ref[t:t + 1, :]
            a_c = a[r0:r0 + rc, :]
            sg_c = sg[r0:r0 + rc, :]
            dag_ref[r0:r0 + rc, :D_CONV] = (acc * sg_c).astype(dag_ref.dtype)
            dag_ref[r0:r0 + rc, D_CONV:] = (acc * a_c * sg_c * (1.0 - sg_c)).astype(dag_ref.dtype)

    row = lambda w: pl.BlockSpec((tm, w), lambda i: (i, 0))
    full = lambda a, b: pl.BlockSpec((a, b), lambda i: (0, 0))
    nxt = pl.BlockSpec((CONV_HALO, D_CONV), lambda i: (jnp.minimum((i + 1) * hb, n_halo_blocks - 1), 0))
    return _pallas_call(
        body, name=name, grid=(I,),
        in_specs=[row(D_CONV), row(D_CONV), nxt, nxt, row(2 * D_CONV),
                  pl.BlockSpec((CONV_HALO, 2 * D_CONV), lambda i: (jnp.maximum(i * hb - 1, 0), 0)),
                  full(CONV_HALO, D_CONV), full(1, D_CONV), full(1, D_CONV)],
        out_specs=[row(2 * D_CONV), full(CONV_HALO, D_CONV), full(1, D_CONV), full(1, D_CONV), full(1, D_CONV)],
        out_shape=[
            jax.ShapeDtypeStruct((T, 2 * D_CONV), MXU),
            jax.ShapeDtypeStruct((CONV_HALO, D_CONV), F32),
            jax.ShapeDtypeStruct((1, D_CONV), F32),
            jax.ShapeDtypeStruct((1, D_CONV), F32),
            jax.ShapeDtypeStruct((1, D_CONV), F32),
        ],
        scratch_shapes=[pltpu.VMEM((CONV_HALO + tm, D_CONV), F32), pltpu.VMEM((tm + CONV_HALO, D_CONV), F32)],
        compiler_params=_params(32, 1),
    )(dc, yc, dc, yc, ag, ag, w32, lg, lb)


def _outproj_fwd(x1, c, o, gc, ga, wout, name):
    T, D = x1.shape
    tm = _tile(T, 512)

    def body(x_ref, c_ref, o_ref, gc_ref, ga_ref, w_ref, x2_ref):
        yc, _ = _rms_fwd(c_ref[...], gc_ref[...])
        ya, _ = _rms_fwd(o_ref[...], ga_ref[...])
        x2_ref[...] = (x_ref[...] + _mm(yc.astype(MXU), w_ref[:D_CONV, :])
                       + _mm(ya.astype(MXU), w_ref[D_CONV:, :]))

    row = lambda w: pl.BlockSpec((tm, w), lambda i: (i, 0))
    full = lambda a, b: pl.BlockSpec((a, b), lambda i: (0, 0))
    return _pallas_call(
        body, name=name, grid=(T // tm,),
        in_specs=[row(D), row(D_CONV), row(D_ATTN), full(1, D_CONV), full(1, D_ATTN), full(D_CONV + D_ATTN, D)],
        out_specs=row(D),
        out_shape=jax.ShapeDtypeStruct((T, D), F32),
        compiler_params=_params(32, 1),
    )(x1, c, o, gc, ga, wout)


def _outproj_bwd(dx2, c, o, gc, ga, wout, name):
    T, D = dx2.shape
    tm = _tile(T, 256)
    I = T // tm

    def body(dx_ref, c_ref, o_ref, gc_ref, ga_ref, w_ref,
             dc_ref, do_ref, doT_ref, dlT_ref, dw_ref, dgc_ref, dga_ref, acc_s):
        i = pl.program_id(0)
        dxb = dx_ref[...].astype(MXU)
        cv = c_ref[...]
        ov = o_ref[...]
        yc, rcn = _rms_fwd(cv, gc_ref[...])
        ya, ra = _rms_fwd(ov, ga_ref[...])
        dyc = _mm_nt(dxb, w_ref[:D_CONV, :])
        dya = _mm_nt(dxb, w_ref[D_CONV:, :])
        dwc = _mm_tn(yc.astype(MXU), dxb)
        dwa = _mm_tn(ya.astype(MXU), dxb)
        dcv, dgc = _rms_bwd(cv, rcn, gc_ref[...], dyc)
        dov, dga = _rms_bwd(ov, ra, ga_ref[...], dya)
        dc_ref[...] = dcv
        dob = dov.astype(do_ref.dtype)
        do_ref[...] = dob
        doT_ref[...] = dov.T.astype(doT_ref.dtype)
        chan = lax.broadcasted_iota(jnp.int32, (D_ATTN, LANES), 0)
        head = lax.broadcasted_iota(jnp.int32, (D_ATTN, LANES), 1)
        in_head = ((chan >= head * HEAD_DIM) & (chan < (head + 1) * HEAD_DIM)).astype(jnp.bfloat16)
        dlT_ref[...] = _exact_dot_01(dob.astype(F32) * ov, in_head).T[:N_HEADS, :]

        @pl.when(i == 0)
        def _():
            acc_s[:D_CONV, :] = dwc
            acc_s[D_CONV:, :] = dwa
            dgc_ref[...] = dgc
            dga_ref[...] = dga

        @pl.when(i > 0)
        def _():
            acc_s[:D_CONV, :] += dwc
            acc_s[D_CONV:, :] += dwa
            dgc_ref[...] += dgc
            dga_ref[...] += dga

        @pl.when(i == I - 1)
        def _():
            dw_ref[...] = acc_s[...].astype(dw_ref.dtype)

    row = lambda w: pl.BlockSpec((tm, w), lambda i: (i, 0))
    full = lambda a, b: pl.BlockSpec((a, b), lambda i: (0, 0))
    return _pallas_call(
        body, name=name, grid=(I,),
        in_specs=[row(D), row(D_CONV), row(D_ATTN), full(1, D_CONV), full(1, D_ATTN), full(D_CONV + D_ATTN, D)],
        out_specs=[row(D_CONV), row(D_ATTN), pl.BlockSpec((D_ATTN, tm), lambda i: (0, i)),
                   pl.BlockSpec((N_HEADS, tm), lambda i: (0, i)),
                   full(D_CONV + D_ATTN, D), full(1, D_CONV), full(1, D_ATTN)],
        out_shape=[
            jax.ShapeDtypeStruct((T, D_CONV), F32),
            jax.ShapeDtypeStruct((T, D_ATTN), MXU),
            jax.ShapeDtypeStruct((D_ATTN, T), MXU),
            jax.ShapeDtypeStruct((N_HEADS, T), F32),
            jax.ShapeDtypeStruct((D_CONV + D_ATTN, D), MXU),
            jax.ShapeDtypeStruct((1, D_CONV), F32),
            jax.ShapeDtypeStruct((1, D_ATTN), F32),
        ],
        scratch_shapes=[pltpu.VMEM((D_CONV + D_ATTN, D), F32)],
        compiler_params=_params(40, 1),
    )(dx2, c, o, gc, ga, wout)


def _loss_head(x3, gf, target, name):
    T, D = x3.shape
    tm = _tile(T, 512)

    def body(x_ref, g_ref, t_ref, loss_ref, dx_ref, dg_ref):
        i = pl.program_id(0)
        xv = x_ref[...]
        gv = g_ref[...]
        out, r = _rms_fwd(xv, gv)
        err = out - t_ref[...]
        part = jnp.full((1, LANES), 0.5 / D, F32) * jnp.sum(err * err)
        dxn, dgp = _rms_bwd(xv, r, gv, err * (1.0 / D))
        dx_ref[...] = dxn

        @pl.when(i == 0)
        def _():
            loss_ref[...] = part
            dg_ref[...] = dgp

        @pl.when(i > 0)
        def _():
            loss_ref[...] += part
            dg_ref[...] += dgp

    row = lambda w: pl.BlockSpec((tm, w), lambda i: (i, 0))
    full = lambda a, b: pl.BlockSpec((a, b), lambda i: (0, 0))
    return _pallas_call(
        body, name=name, grid=(T // tm,),
        in_specs=[row(D), full(1, D), row(D)],
        out_specs=[full(1, LANES), row(D), full(1, D)],
        out_shape=[jax.ShapeDtypeStruct((1, LANES), F32), jax.ShapeDtypeStruct((T, D), F32),
                   jax.ShapeDtypeStruct((1, D), F32)],
        compiler_params=_params(32, 1),
    )(x3, gf, target)


def _row_tile(rows):
    for cand in (256, 176, 128, 64, 32, 16):
        if rows % cand == 0:
            return cand
    return rows


def _adamw(w, m, v, parts, name):
    R, C = w.shape
    P = parts.shape[0]
    tr = _row_tile(R)
    c1 = 1.0 - ADAM_B1 ** ADAM_STEP
    c2 = 1.0 - ADAM_B2 ** ADAM_STEP

    def body(w_ref, m_ref, v_ref, p_ref, g_ref, d_ref, nm_ref, nv_ref):
        g = p_ref[0].astype(F32)
        for s in range(1, P):
            g = g + p_ref[s].astype(F32)
        wv = w_ref[...]
        mn = ADAM_B1 * m_ref[...] + (1.0 - ADAM_B1) * g
        vn = ADAM_B2 * v_ref[...] + (1.0 - ADAM_B2) * (g * g)
        g_ref[...] = g
        nm_ref[...] = mn
        nv_ref[...] = vn
        d_ref[...] = -ADAM_LR * ((mn / c1) / (jnp.sqrt(vn / c2) + ADAM_EPS) + ADAM_WD * wv)

    blk = pl.BlockSpec((tr, C), lambda i: (i, 0))
    out = jax.ShapeDtypeStruct((R, C), F32)
    return _pallas_call(
        body, name=name, grid=(R // tr,),
        in_specs=[blk, blk, blk, pl.BlockSpec((P, tr, C), lambda i: (0, i, 0))],
        out_specs=[blk, blk, blk, blk],
        out_shape=[out, out, out, out],
        compiler_params=_params(32, 1),
    )(w, m, v, parts)


def _position():
    return lax.axis_index("x"), lax.axis_index("y"), lax.axis_index("c")


def _flat(px, py, pc):
    return 4 * px + 2 * py + pc


def _gather_body(ins, outs, send_sems, recv_sems, local_sems, handshake):
    n = len(ins)
    x, y, c = _position()
    me, sibling = (x, y, c), (x, y, 1 - c)
    chips = [(1 - x, y), (x, 1 - y), (1 - x, 1 - y)]
    if handshake:
        _handshake([sibling] + [(*chip, cc) for chip in chips for cc in (c, 1 - c)])

    def copy(a, k, block, to, src=None):
        dst = outs[a].at[_flat(*block)]
        return pltpu.make_async_remote_copy(
            src_ref=dst if src is None else src, dst_ref=dst,
            send_sem=send_sems.at[a, k], recv_sem=recv_sems.at[a, k],
            device_id=to, device_id_type=MESH)

    mine = [pltpu.make_async_copy(ins[a], outs[a].at[_flat(*me)], local_sems.at[a]) for a in range(n)]
    for cp in mine:
        cp.start()
    first = []
    for a in range(n):
        first.append(copy(a, 0, me, sibling, src=ins[a]))
        first += [copy(a, 1 + j, me, (*chip, c), src=ins[a]) for j, chip in enumerate(chips)]
    for cp in first:
        cp.start()
    passed = []
    for a in range(n):
        for j, chip in enumerate(chips):
            copy(a, 1 + j, (*chip, c), me).wait_recv()
            fwd = copy(a, 4 + j, (*chip, c), sibling)
            fwd.start()
            passed.append(fwd)
    for a in range(n):
        copy(a, 0, sibling, me).wait_recv()
        for j, chip in enumerate(chips):
            copy(a, 4 + j, (*chip, 1 - c), me).wait_recv()
    for cp in first + passed:
        cp.wait_send()
    for cp in mine:
        cp.wait()


def _gather_scratch(n):
    return [pltpu.SemaphoreType.DMA((n, 7)), pltpu.SemaphoreType.DMA((n, 7)), pltpu.SemaphoreType.DMA((n,))]


def _all_gather(shards, name):
    n = len(shards)

    def body(*refs):
        _gather_body(refs[:n], refs[n:2 * n], *refs[2 * n:], handshake=False)

    hbm = pl.BlockSpec(memory_space=pltpu.HBM)
    return _pallas_call(
        body, name=name,
        in_specs=[hbm] * n, out_specs=[hbm] * n,
        out_shape=[jax.ShapeDtypeStruct((N_DEV,) + s.shape, s.dtype) for s in shards],
        scratch_shapes=_gather_scratch(n),
    )(*shards)


def _handshake(peers):
    barrier = pltpu.get_barrier_semaphore()
    for peer in peers:
        pl.semaphore_signal(barrier, inc=1, device_id=peer, device_id_type=MESH)
    pl.semaphore_wait(barrier, len(peers))


def _sequencer_call(body, name, collective_id, out_type, scratch_types, operands):
    return pl.kernel(
        body, name=name, out_type=out_type,
        mesh=plsc.ScalarSubcoreMesh(axis_name="sequencer", num_cores=1),
        scratch_types=scratch_types,
        compiler_params=pltpu.CompilerParams(collective_id=collective_id),
    )(*operands)


def _seq_all_gather(shards, name, collective_id, after):
    n = len(shards)

    def body(*refs):
        _gather_body(refs[:n], refs[n + 1:2 * n + 1], *refs[2 * n + 1:], handshake=True)

    return _sequencer_call(
        body, name, collective_id,
        [jax.ShapeDtypeStruct((N_DEV,) + s.shape, s.dtype) for s in shards],
        _gather_scratch(n), list(shards) + [after])


def _seq_to_sibling(parts, name, collective_id, after):
    n = len(parts)

    def body(*refs):
        ins, outs = refs[:n], refs[n + len(after):2 * n + len(after)]
        send_sems, recv_sems = refs[2 * n + len(after):]
        x, y, c = _position()
        sibling = (x, y, 1 - c)
        _handshake([sibling])
        sent = []
        for a in range(n):
            for q in range(N_CHIPS):
                cp = pltpu.make_async_remote_copy(
                    src_ref=ins[a].at[2 * q + (1 - c)], dst_ref=outs[a].at[q],
                    send_sem=send_sems.at[a, q], recv_sem=recv_sems.at[a, q],
                    device_id=sibling, device_id_type=MESH)
                cp.start()
                sent.append(cp)
        for cp in sent:
            cp.wait_recv()
        for cp in sent:
            cp.wait_send()

    return _sequencer_call(
        body, name, collective_id,
        [jax.ShapeDtypeStruct((N_CHIPS,) + p.shape[1:], p.dtype) for p in parts],
        [pltpu.SemaphoreType.DMA((n, N_CHIPS)), pltpu.SemaphoreType.DMA((n, N_CHIPS))],
        list(parts) + list(after))


def _seq_to_chips(partials, name, collective_id):
    n = len(partials)

    def body(*refs):
        ins, outs = refs[:n], refs[n:2 * n]
        send_sems, recv_sems, local_sems = refs[2 * n:]
        x, y, c = _position()
        my_chip = 2 * x + y
        chips = [(1 - x, y), (x, 1 - y), (1 - x, 1 - y)]
        _handshake([(*chip, c) for chip in chips])
        mine = [pltpu.make_async_copy(ins[a].at[my_chip], outs[a].at[my_chip], local_sems.at[a]) for a in range(n)]
        for cp in mine:
            cp.start()
        sent = []
        for a in range(n):
            for j, (px, py) in enumerate(chips):
                cp = pltpu.make_async_remote_copy(
                    src_ref=ins[a].at[2 * px + py], dst_ref=outs[a].at[my_chip],
                    send_sem=send_sems.at[a, j], recv_sem=recv_sems.at[a, j],
                    device_id=(px, py, c), device_id_type=MESH)
                cp.start()
                sent.append(cp)
        for cp in sent:
            cp.wait_recv()
        for cp in sent:
            cp.wait_send()
        for cp in mine:
            cp.wait()

    return _sequencer_call(
        body, name, collective_id,
        [jax.ShapeDtypeStruct(p.shape, p.dtype) for p in partials],
        [pltpu.SemaphoreType.DMA((n, 3)), pltpu.SemaphoreType.DMA((n, 3)), pltpu.SemaphoreType.DMA((n,))],
        partials)


def _pair_add(part, recv, name):
    _, R, C = part.shape
    tr = _row_tile(R)
    pairs = part.reshape(N_CHIPS, 2, R, C)

    def body(p_ref, r_ref, o_ref):
        c = lax.axis_index("c")
        o_ref[...] = (p_ref[c].astype(F32) + r_ref[...].astype(F32)).astype(o_ref.dtype)

    return _pallas_call(
        body, name=name, grid=(N_CHIPS, R // tr),
        in_specs=[pl.BlockSpec((None, 2, tr, C), lambda q, i: (q, 0, i, 0)),
                  pl.BlockSpec((None, tr, C), lambda q, i: (q, i, 0))],
        out_specs=pl.BlockSpec((None, tr, C), lambda q, i: (q, i, 0)),
        out_shape=jax.ShapeDtypeStruct((N_CHIPS, R, C), part.dtype),
        compiler_params=_params(32, 2),
    )(pairs, recv)


class _Reduced(NamedTuple):
    partials: list
    reduced: list


def _blocks(g):
    return g.reshape(N_DEV, -1, g.shape[-1])


def _reduce_scatter(parts, tag, ids, after=()):
    from_sibling = _seq_to_sibling(parts, "rs_sibling_" + tag, ids[0], after)
    partials = [_pair_add(p, r, "rs_add_%s_%d" % (tag, a)) for a, (p, r) in enumerate(zip(parts, from_sibling))]
    return _Reduced(partials, _seq_to_chips(partials, "rs_chips_" + tag, ids[1]))


_SMALL = ("ffn1_norm", "mix_norm", "conv_b", "conv_ln_g", "conv_ln_b", "forget_b", "out_norm_conv",
          "out_norm_attn", "ffn2_norm", "final_norm")
_PACK_WIDTH = 2 * D_CONV
_SLOT = dict(ffn1_norm=(0, 0), mix_norm=(1, 0), ffn2_norm=(2, 0), final_norm=(3, 0), conv_b=(4, 0),
             conv_ln_g=(4, D_CONV), conv_ln_b=(5, 0), out_norm_conv=(5, D_CONV), out_norm_attn=(6, 0),
             forget_b=(6, D_CONV))
_CONV_ROW0 = 8
_PACK_ROWS = _CONV_ROW0 + CONV_HALO


def _pack_small(small, name):
    arrays = [small[n] for n in _SMALL] + [small["conv_w"]]

    def body(*refs):
        out = refs[-1]
        out[...] = jnp.zeros_like(out)
        for n, ref in zip(_SMALL, refs):
            row, lane = _SLOT[n]
            out[row:row + 1, lane:lane + ref.shape[1]] = ref[...]
        out[_CONV_ROW0:, :D_CONV] = refs[len(_SMALL)][...]

    return _pallas_call(body, name=name, out_shape=jax.ShapeDtypeStruct((_PACK_ROWS, _PACK_WIDTH), F32))(*arrays)


def _adamw_small(gathered, w, m, v, name):
    c1 = 1.0 - ADAM_B1 ** ADAM_STEP
    c2 = 1.0 - ADAM_B2 ** ADAM_STEP
    k = len(_SMALL)

    def body(g_ref, *refs):
        ws, ms, vs = refs[:k], refs[k:2 * k], refs[2 * k:3 * k]
        outs = refs[3 * k:]
        total = g_ref[0]
        for s in range(1, N_DEV):
            total = total + g_ref[s]
        for i, n in enumerate(_SMALL):
            row, lane = _SLOT[n]
            width = ws[i].shape[1]
            g = total[row:row + 1, lane:lane + width]
            mn = ADAM_B1 * ms[i][...] + (1.0 - ADAM_B1) * g
            vn = ADAM_B2 * vs[i][...] + (1.0 - ADAM_B2) * (g * g)
            o_g, o_d, o_m, o_v = outs[4 * i:4 * i + 4]
            o_g[...] = g
            o_m[...] = mn
            o_v[...] = vn
            o_d[...] = -ADAM_LR * ((mn / c1) / (jnp.sqrt(vn / c2) + ADAM_EPS) + ADAM_WD * ws[i][...])
        outs[4 * k][...] = total[_CONV_ROW0:, :D_CONV]

    shapes = []
    for n in _SMALL:
        shapes += [jax.ShapeDtypeStruct(w[n].shape, F32)] * 4
    shapes.append(jax.ShapeDtypeStruct((CONV_HALO, D_CONV), F32))
    res = _pallas_call(body, name=name, out_shape=shapes)(
        gathered, *[w[n] for n in _SMALL], *[m[n] for n in _SMALL], *[v[n] for n in _SMALL])
    return {n: res[4 * i:4 * i + 4] for i, n in enumerate(_SMALL)}, res[4 * k]


def _local_step(x, target, norms, w13_1, w2_1, winp, conv_w32, wout, w13_2, w2_2):
    x1, G1, U1 = _ffn_fwd(x, norms["ffn1_norm"], w13_1, w2_1, "ffn1_fwd")
    ag, q, k, v, qT, kT, vT, fl = _inproj_fwd(x1, norms["mix_norm"], winp, "inproj_fwd")
    cum, cumT = _forget_fwd(fl, norms["forget_b"], "forget_fwd")
    yc, c = _conv_fwd(ag, conv_w32, norms["conv_b"], norms["conv_ln_g"], norms["conv_ln_b"], "conv_fwd")
    o, lseT = _attn_fwd(qT, k, vT, cum, cumT, "attn_fwd")
    x2 = _outproj_fwd(x1, c, o, norms["out_norm_conv"], norms["out_norm_attn"], wout, "outproj_fwd")
    x3, G2, U2 = _ffn_fwd(x2, norms["ffn2_norm"], w13_2, w2_2, "ffn2_fwd")
    loss, dx3, d_final = _loss_head(x3, norms["final_norm"], target, "loss_head")

    dx2, dw13_2, dw2_2, d_ffn2n = _ffn_bwd(x2, norms["ffn2_norm"], dx3, G2, U2, w13_2, w2_2, "ffn2_bwd")
    dc, dob, dobT, deltaT, dwout, d_onc, d_ona = _outproj_bwd(
        dx2, c, o, norms["out_norm_conv"], norms["out_norm_attn"], wout, "outproj_bwd")
    red_ffn2 = _reduce_scatter([_blocks(dw13_2), _blocks(dw2_2)], "ffn2", (3, 4))
    dq, dk, dv, dcum = _attn_bwd(q, qT, k, kT, v, dob, dobT, lseT, deltaT, cum, cumT, "attn_bwd",
                                 after=red_ffn2.partials)
    dfl, d_fb = _forget_bwd(dcum, fl, norms["forget_b"], "forget_bwd")
    dag, d_convw, d_cb, d_lg, d_lb = _conv_bwd(dc, yc, ag, conv_w32, norms["conv_ln_g"], norms["conv_ln_b"], "conv_bwd")
    dx1, dwinp, d_mixn = _inproj_bwd(x1, norms["mix_norm"], dx2, dag, dq, dk, dv, dfl, winp, "inproj_bwd")
    dwin_blocks = dwinp[:N_IN].reshape(N_DEV, N_IN // N_DEV, -1)
    red_mix = _reduce_scatter([dwin_blocks, _blocks(dwout)], "mix", (5, 6), after=red_ffn2.reduced[:1])
    dx, dw13_1, dw2_1, d_ffn1n = _ffn_bwd(x, norms["ffn1_norm"], dx1, G1, U1, w13_1, w2_1, "ffn1_bwd",
                                          after=red_mix.partials)
    red_ffn1 = _reduce_scatter([_blocks(dw13_1), _blocks(dw2_1)], "ffn1", (7, 8), after=red_mix.reduced[:1])

    small = dict(ffn1_norm=d_ffn1n, mix_norm=d_mixn, conv_b=d_cb, conv_ln_g=d_lg, conv_ln_b=d_lb,
                 forget_b=d_fb, out_norm_conv=d_onc, out_norm_attn=d_ona, ffn2_norm=d_ffn2n,
                 final_norm=d_final, conv_w=d_convw)
    big = dict(ffn1_w13=red_ffn1.reduced[0], ffn1_w2=red_ffn1.reduced[1], w_in=red_mix.reduced[0],
               w_out=red_mix.reduced[1], ffn2_w13=red_ffn2.reduced[0], ffn2_w2=red_ffn2.reduced[1])
    return loss[0, 0], dx, small, big


_BIG = ("ffn1_w13", "ffn1_w2", "w_in", "w_out", "ffn2_w13", "ffn2_w2")
_TRANSPOSED = ("ffn1_w13", "ffn2_w13", "w_in")
_ORDER = ("ffn1_norm", "ffn1_w13", "ffn1_w2", "mix_norm", "w_in", "conv_w", "conv_b", "conv_ln_g", "conv_ln_b",
          "forget_b", "out_norm_conv", "out_norm_attn", "w_out", "ffn2_norm", "ffn2_w13", "ffn2_w2", "final_norm")


def kernel(x, ffn1_norm, ffn1_w13, ffn1_w2, mix_norm, w_in, conv_w, conv_b, conv_ln_g, conv_ln_b, forget_b, out_norm_conv, out_norm_attn, w_out, ffn2_norm, ffn2_w13, ffn2_w2, final_norm, loss_target, m_ffn1_norm, m_ffn1_w13, m_ffn1_w2, m_mix_norm, m_w_in, m_conv_w, m_conv_b, m_conv_ln_g, m_conv_ln_b, m_forget_b, m_out_norm_conv, m_out_norm_attn, m_w_out, m_ffn2_norm, m_ffn2_w13, m_ffn2_w2, m_final_norm, v_ffn1_norm, v_ffn1_w13, v_ffn1_w2, v_mix_norm, v_w_in, v_conv_w, v_conv_b, v_conv_ln_g, v_conv_ln_b, v_forget_b, v_out_norm_conv, v_out_norm_attn, v_w_out, v_ffn2_norm, v_ffn2_w13, v_ffn2_w2, v_final_norm):
    w = dict(ffn1_norm=ffn1_norm, ffn1_w13=ffn1_w13, ffn1_w2=ffn1_w2, mix_norm=mix_norm, w_in=w_in, conv_w=conv_w,
             conv_b=conv_b, conv_ln_g=conv_ln_g, conv_ln_b=conv_ln_b, forget_b=forget_b, out_norm_conv=out_norm_conv,
             out_norm_attn=out_norm_attn, w_out=w_out, ffn2_norm=ffn2_norm, ffn2_w13=ffn2_w13, ffn2_w2=ffn2_w2,
             final_norm=final_norm)
    m = dict(ffn1_norm=m_ffn1_norm, ffn1_w13=m_ffn1_w13, ffn1_w2=m_ffn1_w2, mix_norm=m_mix_norm, w_in=m_w_in,
             conv_w=m_conv_w, conv_b=m_conv_b, conv_ln_g=m_conv_ln_g, conv_ln_b=m_conv_ln_b, forget_b=m_forget_b,
             out_norm_conv=m_out_norm_conv, out_norm_attn=m_out_norm_attn, w_out=m_w_out, ffn2_norm=m_ffn2_norm,
             ffn2_w13=m_ffn2_w13, ffn2_w2=m_ffn2_w2, final_norm=m_final_norm)
    v = dict(ffn1_norm=v_ffn1_norm, ffn1_w13=v_ffn1_w13, ffn1_w2=v_ffn1_w2, mix_norm=v_mix_norm, w_in=v_w_in,
             conv_w=v_conv_w, conv_b=v_conv_b, conv_ln_g=v_conv_ln_g, conv_ln_b=v_conv_ln_b, forget_b=v_forget_b,
             out_norm_conv=v_out_norm_conv, out_norm_attn=v_out_norm_attn, w_out=v_w_out, ffn2_norm=v_ffn2_norm,
             ffn2_w13=v_ffn2_w13, ffn2_w2=v_ffn2_w2, final_norm=v_final_norm)
    shapes = {n: a.shape for n, a in w.items()}
    T, D = x.shape[1], x.shape[2]
    def two(n, a):
        if a.ndim != 3:
            return a.reshape(1, -1)
        a = a.reshape(a.shape[-2], a.shape[-1])
        return a.T if n in _TRANSPOSED else a

    w2d = {n: two(n, a) for n, a in w.items()}
    m2d = {n: two(n, a) for n, a in m.items()}
    v2d = {n: two(n, a) for n, a in v.items()}

    cast = lambda n: w2d[n].astype(MXU)
    g13_1, g2_1 = _all_gather([cast("ffn1_w13"), cast("ffn1_w2")], "gather_ffn1")
    gin, gout, gconv = _seq_all_gather([cast("w_in"), cast("w_out"), w2d["conv_w"]], "gather_mix", 1,
                                       after=g13_1)
    g13_2, g2_2 = _seq_all_gather([cast("ffn2_w13"), cast("ffn2_w2")], "gather_ffn2", 2, after=g2_1)
    bf = g13_1.shape[1]
    J = N_DEV // 2
    w13_1 = g13_1.reshape(2, J, bf, D)
    w13_2 = g13_2.reshape(2, J, bf, D)
    w2_1 = g2_1.reshape(-1, D)
    w2_2 = g2_2.reshape(-1, D)
    wout = gout.reshape(-1, D)
    winp = jnp.pad(gin.reshape(N_IN, D), ((0, N_IN_PAD - N_IN), (0, 0)))
    conv_full = gconv.transpose(1, 0, 2).reshape(CONV_TAPS, D_CONV)
    conv_w32 = jnp.pad(conv_full, ((0, CONV_HALO - CONV_TAPS), (0, 0)))

    norms = {n: w2d[n] for n in _SMALL}
    norms["forget_b"] = jnp.pad(w2d["forget_b"], ((0, 0), (0, LANES - N_HEADS)))
    loss_part, dx, small, big = _local_step(x[0], loss_target[0], norms, w13_1, w2_1, winp, conv_w32, wout,
                                            w13_2, w2_2)
    loss = lax.psum(loss_part, ("x", "y", "c"))

    grads, deltas, new_m, new_v = {}, {}, {}, {}
    for n in _BIG:
        g, d, nm, nv = _adamw(w2d[n], m2d[n], v2d[n], big[n], "adamw_" + n)
        grads[n], deltas[n], new_m[n], new_v[n] = g, d, nm, nv

    (gathered_small,) = _all_gather([_pack_small(small, "pack_small_grads")], "gather_small_grads")
    small_out, conv_g_full = _adamw_small(gathered_small, w2d, m2d, v2d, "adamw_small")
    for n in _SMALL:
        grads[n], deltas[n], new_m[n], new_v[n] = small_out[n]
    conv_g_full = conv_g_full[:CONV_TAPS]
    xi, yi, ci = _position()
    cw = shapes["conv_w"][-1]
    conv_g_mine = lax.dynamic_slice_in_dim(conv_g_full, _flat(xi, yi, ci) * cw, cw, axis=1)
    g, d, nm, nv = _adamw(w2d["conv_w"], m2d["conv_w"], v2d["conv_w"], conv_g_mine[None], "adamw_conv_w")
    grads["conv_w"], deltas["conv_w"], new_m["conv_w"], new_v["conv_w"] = g, d, nm, nv

    shaped = lambda dct: [(dct[n].T if n in _TRANSPOSED else dct[n]).reshape(shapes[n]) for n in _ORDER]
    return (loss, dx[None], *shaped(grads), *shaped(deltas), *shaped(new_m), *shaped(new_v))
```

```python
from typing import NamedTuple

import jax
import jax.numpy as jnp
from jax import lax
from jax.experimental import pallas as pl
from jax.experimental.pallas import tpu as pltpu
from jax.experimental.pallas import tpu_sc as plsc

F32 = jnp.float32
MXU = jnp.bfloat16
EPS = 1e-6
N_HEADS = 8
HEAD_DIM = 64
D_CONV = 512
D_ATTN = N_HEADS * HEAD_DIM
CONV_TAPS = 31
CONV_HALO = 32
SCALE = HEAD_DIM ** -0.5
NEG = -1e30
LANES = 128
N_DEV = 8
N_CHIPS = N_DEV // 2
MESH = pl.DeviceIdType.MESH
MIB = 1 << 20

ADAM_LR = 0.001
ADAM_B1 = 0.9
ADAM_B2 = 0.999
ADAM_EPS = 1e-08
ADAM_WD = 0.01
ADAM_STEP = 10


_UNREAD = pl.BlockSpec(memory_space=pl.ANY)


def _pallas_call(body, *, out_shape, **kwargs):
    in_hbm = lambda s: pltpu.HBM(s.shape, s.dtype)
    outs = [in_hbm(s) for s in out_shape] if isinstance(out_shape, (list, tuple)) else in_hbm(out_shape)
    call = pl.pallas_call(body, out_shape=outs, **kwargs)
    return lambda *operands: call(*[pltpu.with_memory_space_constraint(a, pltpu.HBM) for a in operands])


def _params(vmem_mib, n_axes):
    return pltpu.CompilerParams(dimension_semantics=("arbitrary",) * n_axes, vmem_limit_bytes=vmem_mib * MIB)


def _mm(a, b):
    return jnp.dot(a, b, preferred_element_type=F32)


def _mm_nt(a, b):
    return lax.dot_general(a, b, (((1,), (1,)), ((), ())), preferred_element_type=F32)


def _mm_tn(a, b):
    return lax.dot_general(a, b, (((0,), (0,)), ((), ())), preferred_element_type=F32)


def _rms_fwd(x, g):
    r = lax.rsqrt(jnp.mean(x * x, axis=-1, keepdims=True) + EPS)
    return x * r * g, r


def _rms_bwd(x, r, g, dy):
    gdy = dy * g
    dx = r * gdy - x * (r * r * r) * jnp.mean(x * gdy, axis=-1, keepdims=True)
    dg = jnp.sum(dy * x * r, axis=0, keepdims=True)
    return dx, dg


def _silu_grad(z, sz):
    return sz * (1.0 + z * (1.0 - sz))


def _three_terms(x):
    x1 = x.astype(jnp.bfloat16)
    r1 = x - x1.astype(F32)
    x2 = r1.astype(jnp.bfloat16)
    x3 = (r1 - x2.astype(F32)).astype(jnp.bfloat16)
    return x1, x2, x3


def _exact_tri_dot(tri, x):
    x1, x2, x3 = _three_terms(x)
    return _mm(tri, x1) + _mm(tri, x2) + _mm(tri, x3)


def _exact_dot_01(x, sel):
    x1, x2, x3 = _three_terms(x)
    return _mm(x1, sel) + _mm(x2, sel) + _mm(x3, sel)


def _tile(n, want):
    t = min(n, want)
    assert n % t == 0
    return t


_FFN_CHUNK = 256


def _ffn_fwd(x, g, w13, w2, name):
    T, D = x.shape
    _, J, bf, _ = w13.shape
    tm = _tile(T, 512)
    I = T // tm

    def body(x_ref, g_ref, w13_ref, w2_ref, xo_ref, G_ref, U_ref, acc_s):
        j = pl.program_id(0)
        i = pl.program_id(1)
        rows = pl.ds(pl.multiple_of(i * tm, tm), tm)
        xv = x_ref[...]
        h, _ = _rms_fwd(xv, g_ref[...])
        hb = h.astype(MXU)

        @pl.when(j == 0)
        def _():
            acc_s[rows, :] = xv

        chunks = [slice(r0, r0 + _FFN_CHUNK) for r0 in range(0, tm, _FFN_CHUNK)]
        GU = [(_mm_nt(hb[rs, :], w13_ref[0]), _mm_nt(hb[rs, :], w13_ref[1])) for rs in chunks]
        for rs, (G, U) in zip(chunks, GU):
            G_ref[rs, :] = G.astype(MXU)
            U_ref[rs, :] = U.astype(MXU)
            A = (G * jax.nn.sigmoid(G) * U).astype(MXU)
            crows = pl.ds(pl.multiple_of(i * tm + rs.start, _FFN_CHUNK), _FFN_CHUNK)
            acc_s[crows, :] += 0.5 * _mm(A, w2_ref[...])

        @pl.when(j == J - 1)
        def _():
            xo_ref[...] = acc_s[rows, :]

    return _pallas_call(
        body, name=name, grid=(J, I),
        in_specs=[
            pl.BlockSpec((tm, D), lambda j, i: (i, 0)),
            pl.BlockSpec((1, D), lambda j, i: (0, 0)),
            pl.BlockSpec((2, None, bf, D), lambda j, i: (0, j, 0, 0)),
            pl.BlockSpec((bf, D), lambda j, i: (j, 0)),
        ],
        out_specs=[
            pl.BlockSpec((tm, D), lambda j, i: (jnp.where(j == J - 1, i, 0), 0)),
            pl.BlockSpec((None, tm, bf), lambda j, i: (j, i, 0)),
            pl.BlockSpec((None, tm, bf), lambda j, i: (j, i, 0)),
        ],
        out_shape=[
            jax.ShapeDtypeStruct((T, D), F32),
            jax.ShapeDtypeStruct((J, T, bf), MXU),
            jax.ShapeDtypeStruct((J, T, bf), MXU),
        ],
        scratch_shapes=[pltpu.VMEM((T, D), F32)],
        compiler_params=_params(48, 2),
    )(x, g, w13, w2)


def _ffn_bwd_act(x, g, dy, Gs, Us, w13, w2, name, after=()):
    T, D = x.shape
    _, J, bf, _ = w13.shape
    tm = _tile(T, 512)

    def body(x_ref, g_ref, dy_ref, G_ref, U_ref, w13_ref, w2_ref, *rest):
        dx_ref, dg_ref, h_ref, dF_ref, A_ref, dG_ref, dU_ref, dh_s = rest[len(after):]
        i = pl.program_id(0)
        j = pl.program_id(1)

        @pl.when(j == 0)
        def _():
            h, _ = _rms_fwd(x_ref[...], g_ref[...])
            h_ref[...] = h.astype(MXU)
            dF_ref[...] = (0.5 * dy_ref[...]).astype(MXU)
            dh_s[...] = jnp.zeros_like(dh_s)

        chunks = [slice(r0, r0 + _FFN_CHUNK) for r0 in range(0, tm, _FFN_CHUNK)]
        dAs = [_mm_nt(dF_ref[rs, :], w2_ref[...]) for rs in chunks]
        for rs, dA in zip(chunks, dAs):
            G = G_ref[rs, :].astype(F32)
            U = U_ref[rs, :].astype(F32)
            sg = jax.nn.sigmoid(G)
            s = G * sg
            A_ref[rs, :] = (s * U).astype(MXU)
            dUb = (dA * s).astype(MXU)
            dGb = (dA * U * _silu_grad(G, sg)).astype(MXU)
            dG_ref[rs, :] = dGb
            dU_ref[rs, :] = dUb
            dh_s[rs, :] += _mm(dGb, w13_ref[0]) + _mm(dUb, w13_ref[1])

        @pl.when(j == J - 1)
        def _():
            xv = x_ref[...]
            gv = g_ref[...]
            _, r = _rms_fwd(xv, gv)
            dxn, dgp = _rms_bwd(xv, r, gv, dh_s[...])
            dx_ref[...] = dy_ref[...] + dxn

            @pl.when(i == 0)
            def _():
                dg_ref[...] = dgp

            @pl.when(i > 0)
            def _():
                dg_ref[...] += dgp

    row = pl.BlockSpec((tm, D), lambda i, j: (i, 0))
    blk = pl.BlockSpec((None, tm, bf), lambda i, j: (j, i, 0))
    act = jax.ShapeDtypeStruct((T, D), MXU)
    hid = jax.ShapeDtypeStruct((J, T, bf), MXU)
    return _pallas_call(
        body, name=name, grid=(T // tm, J),
        in_specs=[row, pl.BlockSpec((1, D), lambda i, j: (0, 0)), row, blk, blk,
                  pl.BlockSpec((2, None, bf, D), lambda i, j: (0, j, 0, 0)),
                  pl.BlockSpec((bf, D), lambda i, j: (j, 0))] + [_UNREAD] * len(after),
        out_specs=[row, pl.BlockSpec((1, D), lambda i, j: (0, 0)), row, row, blk, blk, blk],
        out_shape=[jax.ShapeDtypeStruct((T, D), F32), jax.ShapeDtypeStruct((1, D), F32), act, act, hid, hid, hid],
        scratch_shapes=[pltpu.VMEM((tm, D), F32)],
        compiler_params=_params(56, 2),
    )(x, g, dy, Gs, Us, w13, w2, *after)


def _ffn_bwd_weights(h, dF, A, dG, dU, name):
    T, D = h.shape
    J, _, bf = A.shape

    def body(h_ref, dF_ref, A_ref, dG_ref, dU_ref, dw13_ref, dw2_ref):
        dw13_ref[0] = _mm_tn(dG_ref[...], h_ref[...]).astype(dw13_ref.dtype)
        dw13_ref[1] = _mm_tn(dU_ref[...], h_ref[...]).astype(dw13_ref.dtype)
        dw2_ref[...] = _mm_tn(A_ref[...], dF_ref[...]).astype(dw2_ref.dtype)

    full = pl.BlockSpec((T, D), lambda j: (0, 0))
    blk = pl.BlockSpec((None, T, bf), lambda j: (j, 0, 0))
    return _pallas_call(
        body, name=name, grid=(J,),
        in_specs=[full, full, blk, blk, blk],
        out_specs=[pl.BlockSpec((2, None, bf, D), lambda j: (0, j, 0, 0)), pl.BlockSpec((bf, D), lambda j: (j, 0))],
        out_shape=[jax.ShapeDtypeStruct((2, J, bf, D), MXU), jax.ShapeDtypeStruct((J * bf, D), MXU)],
        compiler_params=_params(56, 1),
    )(h, dF, A, dG, dU)


def _ffn_bwd(x, g, dy, Gs, Us, w13, w2, name, after=()):
    dx, dg, h, dF, A, dG, dU = _ffn_bwd_act(x, g, dy, Gs, Us, w13, w2, name + "_act", after)
    dw13, dw2 = _ffn_bwd_weights(h, dF, A, dG, dU, name + "_weights")
    return dx, dw13, dw2, dg


_AG0, _Q0, _K0, _V0, _F0 = 0, 2 * D_CONV, 2 * D_CONV + D_ATTN, 2 * D_CONV + 2 * D_ATTN, 2 * D_CONV + 3 * D_ATTN
N_IN = _F0 + N_HEADS
N_IN_PAD = _F0 + LANES


def _inproj_fwd(x1, gm, winp, name):
    T, D = x1.shape
    tm = _tile(T, 256)

    def body(x_ref, g_ref, w_ref, ag_ref, q_ref, k_ref, v_ref, qT_ref, kT_ref, vT_ref, fl_ref):
        h, _ = _rms_fwd(x_ref[...], g_ref[...])
        hb = h.astype(MXU)
        ag_ref[...] = _mm_nt(hb, w_ref[_AG0:_Q0, :])
        for c0, ref, refT in ((_Q0, q_ref, qT_ref), (_K0, k_ref, kT_ref), (_V0, v_ref, vT_ref)):
            y = _mm_nt(hb, w_ref[c0:c0 + D_ATTN, :])
            ref[...] = y.astype(MXU)
            refT[...] = y.T.astype(MXU)
        fl_ref[...] = _mm_nt(hb, w_ref[_F0:N_IN_PAD, :])

    row = lambda w: pl.BlockSpec((tm, w), lambda i: (i, 0))
    col = pl.BlockSpec((D_ATTN, tm), lambda i: (0, i))
    std = jax.ShapeDtypeStruct((T, D_ATTN), MXU)
    trn = jax.ShapeDtypeStruct((D_ATTN, T), MXU)
    return _pallas_call(
        body, name=name, grid=(T // tm,),
        in_specs=[row(D), pl.BlockSpec((1, D), lambda i: (0, 0)), pl.BlockSpec((N_IN_PAD, D), lambda i: (0, 0))],
        out_specs=[row(2 * D_CONV), row(D_ATTN), row(D_ATTN), row(D_ATTN), col, col, col, row(LANES)],
        out_shape=[jax.ShapeDtypeStruct((T, 2 * D_CONV), F32), std, std, std, trn, trn, trn,
                   jax.ShapeDtypeStruct((T, LANES), F32)],
        compiler_params=_params(40, 1),
    )(x1, gm, winp)


def _inproj_bwd(x1, gm, dx2, dag, dq, dk, dv, dfl, winp, name):
    T, D = x1.shape
    tm = _tile(T, 256)
    I = T // tm
    pieces = ((_AG0, _Q0), (_Q0, _K0), (_K0, _V0), (_V0, _F0), (_F0, N_IN_PAD))

    def body(x_ref, g_ref, dx2_ref, dag_ref, dq_ref, dk_ref, dv_ref, dfl_ref, w_ref,
             dx1_ref, dw_ref, dg_ref, acc_s):
        i = pl.program_id(0)
        xv = x_ref[...]
        gv = g_ref[...]
        h, r = _rms_fwd(xv, gv)
        hb = h.astype(MXU)
        dps = (dag_ref[...], dq_ref[...], dk_ref[...].astype(MXU), dv_ref[...].astype(MXU), dfl_ref[...].astype(MXU))
        dh = jnp.zeros((tm, D), F32)
        for (c0, c1), dp in zip(pieces, dps):
            dh = dh + _mm(dp, w_ref[c0:c1, :])
            dwp = _mm_tn(dp, hb)

            @pl.when(i == 0)
            def _():
                acc_s[c0:c1, :] = dwp

            @pl.when(i > 0)
            def _():
                acc_s[c0:c1, :] += dwp

        dxn, dgp = _rms_bwd(xv, r, gv, dh)
        dx1_ref[...] = dx2_ref[...] + dxn

        @pl.when(i == 0)
        def _():
            dg_ref[...] = dgp

        @pl.when(i > 0)
        def _():
            dg_ref[...] += dgp

        @pl.when(i == I - 1)
        def _():
            dw_ref[...] = acc_s[...].astype(dw_ref.dtype)

    row = lambda w: pl.BlockSpec((tm, w), lambda i: (i, 0))
    full = lambda a, b: pl.BlockSpec((a, b), lambda i: (0, 0))
    return _pallas_call(
        body, name=name, grid=(I,),
        in_specs=[row(D), full(1, D), row(D), row(2 * D_CONV), row(D_ATTN), row(D_ATTN), row(D_ATTN), row(LANES),
                  full(N_IN_PAD, D)],
        out_specs=[row(D), full(N_IN_PAD, D), full(1, D)],
        out_shape=[
            jax.ShapeDtypeStruct((T, D), F32),
            jax.ShapeDtypeStruct((N_IN_PAD, D), MXU),
            jax.ShapeDtypeStruct((1, D), F32),
        ],
        scratch_shapes=[pltpu.VMEM((N_IN_PAD, D), F32)],
        compiler_params=_params(56, 1),
    )(x1, gm, dx2, dag, dq, dk, dv, dfl, winp)


def _forget_fwd(fl, fbp, name):
    T = fl.shape[0]
    tb = _tile(T, 256)

    def body(fl_ref, fb_ref, cum_ref, cumT_ref):
        ri = lax.broadcasted_iota(jnp.int32, (tb, tb), 0)
        ci = lax.broadcasted_iota(jnp.int32, (tb, tb), 1)
        tri = (ri >= ci).astype(jnp.bfloat16)
        carry = jnp.zeros((1, LANES), F32)
        for b in range(T // tb):
            z = fl_ref[b * tb:(b + 1) * tb, :] + fb_ref[...]
            lf = jnp.minimum(z, 0.0) - jnp.log1p(jnp.exp(-jnp.abs(z)))
            c = _exact_tri_dot(tri, lf) + carry
            cum_ref[b * tb:(b + 1) * tb, :] = c
            carry = c[tb - 1:tb, :]
        cumT_ref[...] = cum_ref[...].T[:N_HEADS, :]

    return _pallas_call(
        body, name=name,
        out_shape=[jax.ShapeDtypeStruct((T, LANES), F32), jax.ShapeDtypeStruct((N_HEADS, T), F32)],
        compiler_params=pltpu.CompilerParams(vmem_limit_bytes=32 * MIB),
    )(fl, fbp)


def _forget_bwd(dcum, fl, fbp, name):
    T = fl.shape[0]
    tb = _tile(T, 256)

    def body(dc_ref, fl_ref, fb_ref, dfl_ref, dfb_ref):
        ri = lax.broadcasted_iota(jnp.int32, (tb, tb), 0)
        ci = lax.broadcasted_iota(jnp.int32, (tb, tb), 1)
        tri = (ri <= ci).astype(jnp.bfloat16)
        carry = jnp.zeros((1, LANES), F32)
        dfb = jnp.zeros((1, LANES), F32)
        for b in reversed(range(T // tb)):
            sl = slice(b * tb, (b + 1) * tb)
            dl = _exact_tri_dot(tri, dc_ref[sl, :]) + carry
            carry = dl[0:1, :]
            z = fl_ref[sl, :] + fb_ref[...]
            dfl = dl * jax.nn.sigmoid(-z)
            dfl_ref[sl, :] = dfl
            dfb = dfb + jnp.sum(dfl, axis=0, keepdims=True)
        dfb_ref[...] = dfb

    return _pallas_call(
        body, name=name,
        out_shape=[jax.ShapeDtypeStruct((T, LANES), F32), jax.ShapeDtypeStruct((1, LANES), F32)],
        compiler_params=pltpu.CompilerParams(vmem_limit_bytes=32 * MIB),
    )(dcum, fl, fbp)


def _causal_keep(i, j, tq, tk):
    key = j * tk + lax.broadcasted_iota(jnp.int32, (tk, tq), 0)
    qry = i * tq + lax.broadcasted_iota(jnp.int32, (tk, tq), 1)
    return key <= qry


def _split_hi_lo(x):
    hi = x.astype(MXU)
    lo = (x - hi.astype(F32)).astype(MXU)
    return hi, lo


def _attn_fwd(qT, k, vT, cum, cumT, name):
    T = k.shape[0]
    tq = _tile(T, 256)
    tk = _tile(tq, 128)
    kpq = tq // tk
    heads = [slice(HEAD_DIM * h, HEAD_DIM * (h + 1)) for h in range(N_HEADS)]

    def body(qT_ref, k_ref, vT_ref, cum_ref, cumT_ref, o_ref, lseT_ref, acc_s, m_s, l_s):
        i = pl.program_id(0)
        acc_s[...] = jnp.zeros_like(acc_s)
        m_s[...] = jnp.full_like(m_s, NEG)
        l_s[...] = jnp.zeros_like(l_s)

        def kblock(j, masked):
            rows = pl.ds(pl.multiple_of(j * tk, tk), tk)
            keep = _causal_keep(i, j, tq, tk) if masked else None
            qk = [_mm(k_ref[rows, hs], qT_ref[hs, :]) for hs in heads]
            for h, hs in enumerate(heads):
                sT = qk[h] * SCALE + (cumT_ref[h:h + 1, :] - cum_ref[rows, h:h + 1])
                if masked:
                    sT = jnp.where(keep, sT, NEG)
                m_old = m_s[h:h + 1, :]
                m_new = jnp.maximum(m_old, jnp.max(sT, axis=0, keepdims=True))
                alpha = jnp.exp(m_old - m_new)
                pT = jnp.exp(sT - m_new)
                l_s[h:h + 1, :] = alpha * l_s[h:h + 1, :] + jnp.sum(pT, axis=0, keepdims=True)
                p_hi, p_lo = _split_hi_lo(pT)
                vh = vT_ref[hs, rows]
                acc_s[hs, :] = alpha * acc_s[hs, :] + (_mm(vh, p_hi) + _mm(vh, p_lo))
                m_s[h:h + 1, :] = m_new

        def unmasked(j, c):
            kblock(j, False)
            return c

        lax.fori_loop(0, kpq * i, unmasked, 0)
        for d in range(kpq):
            kblock(kpq * i + d, True)
        for h, hs in enumerate(heads):
            acc_s[hs, :] = acc_s[hs, :] / l_s[h:h + 1, :]
        o_ref[...] = acc_s[...].T
        lseT_ref[...] = m_s[...] + jnp.log(l_s[...])

    full = lambda a, b: pl.BlockSpec((a, b), lambda i: (0, 0))
    colblk = lambda r: pl.BlockSpec((r, tq), lambda i: (0, i))
    return _pallas_call(
        body, name=name, grid=(T // tq,),
        in_specs=[colblk(D_ATTN), full(T, D_ATTN), full(D_ATTN, T), full(T, LANES), colblk(N_HEADS)],
        out_specs=[pl.BlockSpec((tq, D_ATTN), lambda i: (i, 0)), colblk(N_HEADS)],
        out_shape=[jax.ShapeDtypeStruct((T, D_ATTN), F32), jax.ShapeDtypeStruct((N_HEADS, T), F32)],
        scratch_shapes=[pltpu.VMEM((D_ATTN, tq), F32), pltpu.VMEM((N_HEADS, tq), F32),
                        pltpu.VMEM((N_HEADS, tq), F32)],
        compiler_params=_params(40, 1),
    )(qT, k, vT, cum, cumT)


def _attn_bwd(q, qT, k, kT, v, do, doT, lseT, deltaT, cum, cumT, name, after=()):
    T = k.shape[0]
    tq = _tile(T, 256)
    tk = _tile(tq, 128)
    kpq = tq // tk
    heads = [slice(HEAD_DIM * h, HEAD_DIM * (h + 1)) for h in range(N_HEADS)]

    def body(q_ref, qT_ref, k_ref, kT_ref, v_ref, do_ref, doT_ref, lseT_ref, dlT_ref, cum_ref, cumT_ref, *rest):
        dq_ref, dk_ref, dv_ref, dcum_ref, dq_s = rest[len(after):]
        i = pl.program_id(0)

        @pl.when(i == 0)
        def _():
            dk_ref[...] = jnp.zeros_like(dk_ref)
            dv_ref[...] = jnp.zeros_like(dv_ref)
            dcum_ref[...] = jnp.zeros_like(dcum_ref)

        dq_s[...] = jnp.zeros_like(dq_s)

        def kblock(j, masked):
            rows = pl.ds(pl.multiple_of(j * tk, tk), tk)
            keep = _causal_keep(i, j, tq, tk) if masked else None
            qk = [_mm(k_ref[rows, hs], qT_ref[hs, :]) for hs in heads]
            dps = [_mm(v_ref[rows, hs], doT_ref[hs, :]) for hs in heads]
            for h, hs in enumerate(heads):
                sT = qk[h] * SCALE + (cumT_ref[h:h + 1, :] - cum_ref[rows, h:h + 1])
                if masked:
                    sT = jnp.where(keep, sT, NEG)
                pT = jnp.exp(sT - lseT_ref[h:h + 1, :])
                dsT = pT * (dps[h] - dlT_ref[h:h + 1, :])
                dcum_ref[rows, h:h + 1] += -jnp.sum(dsT, axis=1, keepdims=True)
                dsb = dsT.astype(MXU)
                dv_ref[rows, hs] += _mm(pT.astype(MXU), do_ref[:, hs])
                dk_ref[rows, hs] += _mm(dsb, q_ref[:, hs]) * SCALE
                dq_s[hs, :] += _mm(kT_ref[hs, rows], dsb)

        def unmasked(j, c):
            kblock(j, False)
            return c

        lax.fori_loop(0, kpq * i, unmasked, 0)
        for d in range(kpq):
            kblock(kpq * i + d, True)
        dq_ref[...] = (dq_s[...] * SCALE).T.astype(dq_ref.dtype)

    row = pl.BlockSpec((tq, D_ATTN), lambda i: (i, 0))
    full = lambda a, b: pl.BlockSpec((a, b), lambda i: (0, 0))
    colblk = lambda r: pl.BlockSpec((r, tq), lambda i: (0, i))
    return _pallas_call(
        body, name=name, grid=(T // tq,),
        in_specs=[row, colblk(D_ATTN), full(T, D_ATTN), full(D_ATTN, T), full(T, D_ATTN), row, colblk(D_ATTN),
                  colblk(N_HEADS), colblk(N_HEADS), full(T, LANES), colblk(N_HEADS)] + [_UNREAD] * len(after),
        out_specs=[row, full(T, D_ATTN), full(T, D_ATTN), full(T, LANES)],
        out_shape=[
            jax.ShapeDtypeStruct((T, D_ATTN), MXU),
            jax.ShapeDtypeStruct((T, D_ATTN), F32),
            jax.ShapeDtypeStruct((T, D_ATTN), F32),
            jax.ShapeDtypeStruct((T, LANES), F32),
        ],
        scratch_shapes=[pltpu.VMEM((D_ATTN, tq), F32)],
        compiler_params=_params(48, 1),
    )(q, qT, k, kT, v, do, doT, lseT, deltaT, cum, cumT, *after)


_ROWS_PER_CHUNK = 64


def _glu_halo(ag_ref, agh_ref, uext_s, tm, first):
    a = ag_ref[:, :D_CONV]
    sg = jax.nn.sigmoid(ag_ref[:, D_CONV:])
    uh = agh_ref[:, :D_CONV] * jax.nn.sigmoid(agh_ref[:, D_CONV:])
    uext_s[0:CONV_HALO, :] = jnp.where(first, 0.0, uh)
    uext_s[CONV_HALO:CONV_HALO + tm, :] = a * sg
    return a, sg


def _layer_norm_stats(y):
    mu = jnp.mean(y, axis=-1, keepdims=True)
    xc = y - mu
    rs = lax.rsqrt(jnp.mean(xc * xc, axis=-1, keepdims=True) + EPS)
    return xc * rs, rs


def _conv_fwd(ag, w32, cb, lg, lb, name):
    T = ag.shape[0]
    tm = _tile(T, 256)
    rc = _tile(tm, _ROWS_PER_CHUNK)
    hb = tm // CONV_HALO

    def body(ag_ref, agh_ref, w_ref, cb_ref, lg_ref, lb_ref, yc_ref, c_ref, uext_s):
        i = pl.program_id(0)
        _glu_halo(ag_ref, agh_ref, uext_s, tm, i == 0)
        for r0 in range(0, tm, rc):
            acc = jnp.zeros((rc, D_CONV), F32)
            for t in range(CONV_TAPS):
                acc = acc + uext_s[pl.ds(r0 + CONV_HALO - (CONV_TAPS - 1) + t, rc), :] * w_ref[t:t + 1, :]
            y = acc + cb_ref[...]
            yc_ref[r0:r0 + rc, :] = y
            n, _ = _layer_norm_stats(y)
            z = n * lg_ref[...] + lb_ref[...]
            c_ref[r0:r0 + rc, :] = z * jax.nn.sigmoid(z)

    row = lambda w: pl.BlockSpec((tm, w), lambda i: (i, 0))
    full = lambda a, b: pl.BlockSpec((a, b), lambda i: (0, 0))
    return _pallas_call(
        body, name=name, grid=(T // tm,),
        in_specs=[row(2 * D_CONV),
                  pl.BlockSpec((CONV_HALO, 2 * D_CONV), lambda i: (jnp.maximum(i * hb - 1, 0), 0)),
                  full(CONV_HALO, D_CONV), full(1, D_CONV), full(1, D_CONV), full(1, D_CONV)],
        out_specs=[row(D_CONV), row(D_CONV)],
        out_shape=[jax.ShapeDtypeStruct((T, D_CONV), F32), jax.ShapeDtypeStruct((T, D_CONV), F32)],
        scratch_shapes=[pltpu.VMEM((CONV_HALO + tm, D_CONV), F32)],
        compiler_params=_params(32, 1),
    )(ag, ag, w32, cb, lg, lb)


def _conv_bwd(dc, yc, ag, w32, lg, lb, name):
    T = ag.shape[0]
    tm = _tile(T, 256)
    rc = _tile(tm, _ROWS_PER_CHUNK)
    I = T // tm
    hb = tm // CONV_HALO
    n_halo_blocks = T // CONV_HALO

    def body(dc_ref, yc_ref, dch_ref, ych_ref, ag_ref, agh_ref, w_ref, lg_ref, lb_ref,
             dag_ref, dw_ref, dcb_ref, dlg_ref, dlb_ref, uext_s, dext_s):
        i = pl.program_id(0)
        lgv = lg_ref[...]
        lbv = lb_ref[...]

        def ln_bwd(dcv, ycv):
            n, rs = _layer_norm_stats(ycv)
            z = n * lgv + lbv
            dz = dcv * _silu_grad(z, jax.nn.sigmoid(z))
            dn = dz * lgv
            dy = rs * (dn - jnp.mean(dn, axis=-1, keepdims=True) - n * jnp.mean(dn * n, axis=-1, keepdims=True))
            return dy, dz, n

        dy, dz, n = ln_bwd(dc_ref[...], yc_ref[...])
        dyh, _, _ = ln_bwd(dch_ref[...], ych_ref[...])
        dext_s[0:tm, :] = dy
        dext_s[tm:tm + CONV_HALO, :] = jnp.where(i == I - 1, 0.0, dyh)
        a, sg = _glu_halo(ag_ref, agh_ref, uext_s, tm, i == 0)

        @pl.when(i == 0)
        def _():
            dw_ref[...] = jnp.zeros_like(dw_ref)
            dcb_ref[...] = jnp.zeros_like(dcb_ref)
            dlg_ref[...] = jnp.zeros_like(dlg_ref)
            dlb_ref[...] = jnp.zeros_like(dlb_ref)

        dcb_ref[...] += jnp.sum(dy, axis=0, keepdims=True)
        dlg_ref[...] += jnp.sum(dz * n, axis=0, keepdims=True)
        dlb_ref[...] += jnp.sum(dz, axis=0, keepdims=True)
        for t in range(CONV_TAPS):
            u_t = uext_s[pl.ds(CONV_HALO - (CONV_TAPS - 1) + t, tm), :]
            dw_ref[t:t + 1, :] += jnp.sum(dy * u_t, axis=0, keepdims=True)
        for r0 in range(0, tm, rc):
            acc = jnp.zeros((rc, D_CONV), F32)
            for t in range(CONV_TAPS):
                acc = acc + dext_s[pl.ds(r0 + (CONV_TAPS - 1) - t, rc), :] * w_ref[t:t + 1, :]
            a_c = a[r0:r0 + rc, :]
            sg_c = sg[r0:r0 + rc, :]
            dag_ref[r0:r0 + rc, :D_CONV] = (acc * sg_c).astype(dag_ref.dtype)
            dag_ref[r0:r0 + rc, D_CONV:] = (acc * a_c * sg_c * (1.0 - sg_c)).astype(dag_ref.dtype)

    row = lambda w: pl.BlockSpec((tm, w), lambda i: (i, 0))
    full = lambda a, b: pl.BlockSpec((a, b), lambda i: (0, 0))
    nxt = pl.BlockSpec((CONV_HALO, D_CONV), lambda i: (jnp.minimum((i + 1) * hb, n_halo_blocks - 1), 0))
    return _pallas_call(
        body, name=name, grid=(I,),
        in_specs=[row(D_CONV), row(D_CONV), nxt, nxt, row(2 * D_CONV),
                  pl.BlockSpec((CONV_HALO, 2 * D_CONV), lambda i: (jnp.maximum(i * hb - 1, 0), 0)),
                  full(CONV_HALO, D_CONV), full(1, D_CONV), full(1, D_CONV)],
        out_specs=[row(2 * D_CONV), full(CONV_HALO, D_CONV), full(1, D_CONV), full(1, D_CONV), full(1, D_CONV)],
        out_shape=[
            jax.ShapeDtypeStruct((T, 2 * D_CONV), MXU),
            jax.ShapeDtypeStruct((CONV_HALO, D_CONV), F32),
            jax.ShapeDtypeStruct((1, D_CONV), F32),
            jax.ShapeDtypeStruct((1, D_CONV), F32),
            jax.ShapeDtypeStruct((1, D_CONV), F32),
        ],
        scratch_shapes=[pltpu.VMEM((CONV_HALO + tm, D_CONV), F32), pltpu.VMEM((tm + CONV_HALO, D_CONV), F32)],
        compiler_params=_params(32, 1),
    )(dc, yc, dc, yc, ag, ag, w32, lg, lb)


def _outproj_fwd(x1, c, o, gc, ga, wout, name):
    T, D = x1.shape
    tm = _tile(T, 512)

    def body(x_ref, c_ref, o_ref, gc_ref, ga_ref, w_ref, x2_ref):
        yc, _ = _rms_fwd(c_ref[...], gc_ref[...])
        ya, _ = _rms_fwd(o_ref[...], ga_ref[...])
        x2_ref[...] = (x_ref[...] + _mm(yc.astype(MXU), w_ref[:D_CONV, :])
                       + _mm(ya.astype(MXU), w_ref[D_CONV:, :]))

    row = lambda w: pl.BlockSpec((tm, w), lambda i: (i, 0))
    full = lambda a, b: pl.BlockSpec((a, b), lambda i: (0, 0))
    return _pallas_call(
        body, name=name, grid=(T // tm,),
        in_specs=[row(D), row(D_CONV), row(D_ATTN), full(1, D_CONV), full(1, D_ATTN), full(D_CONV + D_ATTN, D)],
        out_specs=row(D),
        out_shape=jax.ShapeDtypeStruct((T, D), F32),
        compiler_params=_params(32, 1),
    )(x1, c, o, gc, ga, wout)


def _outproj_bwd(dx2, c, o, gc, ga, wout, name):
    T, D = dx2.shape
    tm = _tile(T, 256)
    I = T // tm

    def body(dx_ref, c_ref, o_ref, gc_ref, ga_ref, w_ref,
             dc_ref, do_ref, doT_ref, dlT_ref, dw_ref, dgc_ref, dga_ref, acc_s):
        i = pl.program_id(0)
        dxb = dx_ref[...].astype(MXU)
        cv = c_ref[...]
        ov = o_ref[...]
        yc, rcn = _rms_fwd(cv, gc_ref[...])
        ya, ra = _rms_fwd(ov, ga_ref[...])
        dyc = _mm_nt(dxb, w_ref[:D_CONV, :])
        dya = _mm_nt(dxb, w_ref[D_CONV:, :])
        dwc = _mm_tn(yc.astype(MXU), dxb)
        dwa = _mm_tn(ya.astype(MXU), dxb)
        dcv, dgc = _rms_bwd(cv, rcn, gc_ref[...], dyc)
        dov, dga = _rms_bwd(ov, ra, ga_ref[...], dya)
        dc_ref[...] = dcv
        dob = dov.astype(do_ref.dtype)
        do_ref[...] = dob
        doT_ref[...] = dov.T.astype(doT_ref.dtype)
        chan = lax.broadcasted_iota(jnp.int32, (D_ATTN, LANES), 0)
        head = lax.broadcasted_iota(jnp.int32, (D_ATTN, LANES), 1)
        in_head = ((chan >= head * HEAD_DIM) & (chan < (head + 1) * HEAD_DIM)).astype(jnp.bfloat16)
        dlT_ref[...] = _exact_dot_01(dob.astype(F32) * ov, in_head).T[:N_HEADS, :]

        @pl.when(i == 0)
        def _():
            acc_s[:D_CONV, :] = dwc
            acc_s[D_CONV:, :] = dwa
            dgc_ref[...] = dgc
            dga_ref[...] = dga

        @pl.when(i > 0)
        def _():
            acc_s[:D_CONV, :] += dwc
            acc_s[D_CONV:, :] += dwa
            dgc_ref[...] += dgc
            dga_ref[...] += dga

        @pl.when(i == I - 1)
        def _():
            dw_ref[...] = acc_s[...].astype(dw_ref.dtype)

    row = lambda w: pl.BlockSpec((tm, w), lambda i: (i, 0))
    full = lambda a, b: pl.BlockSpec((a, b), lambda i: (0, 0))
    return _pallas_call(
        body, name=name, grid=(I,),
        in_specs=[row(D), row(D_CONV), row(D_ATTN), full(1, D_CONV), full(1, D_ATTN), full(D_CONV + D_ATTN, D)],
        out_specs=[row(D_CONV), row(D_ATTN), pl.BlockSpec((D_ATTN, tm), lambda i: (0, i)),
                   pl.BlockSpec((N_HEADS, tm), lambda i: (0, i)),
                   full(D_CONV + D_ATTN, D), full(1, D_CONV), full(1, D_ATTN)],
        out_shape=[
            jax.ShapeDtypeStruct((T, D_CONV), F32),
            jax.ShapeDtypeStruct((T, D_ATTN), MXU),
            jax.ShapeDtypeStruct((D_ATTN, T), MXU),
            jax.ShapeDtypeStruct((N_HEADS, T), F32),
            jax.ShapeDtypeStruct((D_CONV + D_ATTN, D), MXU),
            jax.ShapeDtypeStruct((1, D_CONV), F32),
            jax.ShapeDtypeStruct((1, D_ATTN), F32),
        ],
        scratch_shapes=[pltpu.VMEM((D_CONV + D_ATTN, D), F32)],
        compiler_params=_params(40, 1),
    )(dx2, c, o, gc, ga, wout)


def _loss_head(x3, gf, target, name):
    T, D = x3.shape
    tm = _tile(T, 512)

    def body(x_ref, g_ref, t_ref, loss_ref, dx_ref, dg_ref):
        i = pl.program_id(0)
        xv = x_ref[...]
        gv = g_ref[...]
        out, r = _rms_fwd(xv, gv)
        err = out - t_ref[...]
        part = jnp.full((1, LANES), 0.5 / D, F32) * jnp.sum(err * err)
        dxn, dgp = _rms_bwd(xv, r, gv, err * (1.0 / D))
        dx_ref[...] = dxn

        @pl.when(i == 0)
        def _():
            loss_ref[...] = part
            dg_ref[...] = dgp

        @pl.when(i > 0)
        def _():
            loss_ref[...] += part
            dg_ref[...] += dgp

    row = lambda w: pl.BlockSpec((tm, w), lambda i: (i, 0))
    full = lambda a, b: pl.BlockSpec((a, b), lambda i: (0, 0))
    return _pallas_call(
        body, name=name, grid=(T // tm,),
        in_specs=[row(D), full(1, D), row(D)],
        out_specs=[full(1, LANES), row(D), full(1, D)],
        out_shape=[jax.ShapeDtypeStruct((1, LANES), F32), jax.ShapeDtypeStruct((T, D), F32),
                   jax.ShapeDtypeStruct((1, D), F32)],
        compiler_params=_params(32, 1),
    )(x3, gf, target)


def _row_tile(rows):
    for cand in (256, 176, 128, 64, 32, 16):
        if rows % cand == 0:
            return cand
    return rows


def _adamw(w, m, v, parts, name):
    R, C = w.shape
    P = parts.shape[0]
    tr = _row_tile(R)
    c1 = 1.0 - ADAM_B1 ** ADAM_STEP
    c2 = 1.0 - ADAM_B2 ** ADAM_STEP

    def body(w_ref, m_ref, v_ref, p_ref, g_ref, d_ref, nm_ref, nv_ref):
        g = p_ref[0].astype(F32)
        for s in range(1, P):
            g = g + p_ref[s].astype(F32)
        wv = w_ref[...]
        mn = ADAM_B1 * m_ref[...] + (1.0 - ADAM_B1) * g
        vn = ADAM_B2 * v_ref[...] + (1.0 - ADAM_B2) * (g * g)
        g_ref[...] = g
        nm_ref[...] = mn
        nv_ref[...] = vn
        d_ref[...] = -ADAM_LR * ((mn / c1) / (jnp.sqrt(vn / c2) + ADAM_EPS) + ADAM_WD * wv)

    blk = pl.BlockSpec((tr, C), lambda i: (i, 0))
    out = jax.ShapeDtypeStruct((R, C), F32)
    return _pallas_call(
        body, name=name, grid=(R // tr,),
        in_specs=[blk, blk, blk, pl.BlockSpec((P, tr, C), lambda i: (0, i, 0))],
        out_specs=[blk, blk, blk, blk],
        out_shape=[out, out, out, out],
        compiler_params=_params(32, 1),
    )(w, m, v, parts)


def _position():
    return lax.axis_index("x"), lax.axis_index("y"), lax.axis_index("c")


def _flat(px, py, pc):
    return 4 * px + 2 * py + pc


def _gather_body(ins, outs, send_sems, recv_sems, local_sems, handshake):
    n = len(ins)
    x, y, c = _position()
    me, sibling = (x, y, c), (x, y, 1 - c)
    chips = [(1 - x, y), (x, 1 - y), (1 - x, 1 - y)]
    if handshake:
        _handshake([sibling] + [(*chip, cc) for chip in chips for cc in (c, 1 - c)])

    def copy(a, k, block, to, src=None):
        dst = outs[a].at[_flat(*block)]
        return pltpu.make_async_remote_copy(
            src_ref=dst if src is None else src, dst_ref=dst,
            send_sem=send_sems.at[a, k], recv_sem=recv_sems.at[a, k],
            device_id=to, device_id_type=MESH)

    mine = [pltpu.make_async_copy(ins[a], outs[a].at[_flat(*me)], local_sems.at[a]) for a in range(n)]
    for cp in mine:
        cp.start()
    first = []
    for a in range(n):
        first.append(copy(a, 0, me, sibling, src=ins[a]))
        first += [copy(a, 1 + j, me, (*chip, c), src=ins[a]) for j, chip in enumerate(chips)]
    for cp in first:
        cp.start()
    passed = []
    for a in range(n):
        for j, chip in enumerate(chips):
            copy(a, 1 + j, (*chip, c), me).wait_recv()
            fwd = copy(a, 4 + j, (*chip, c), sibling)
            fwd.start()
            passed.append(fwd)
    for a in range(n):
        copy(a, 0, sibling, me).wait_recv()
        for j, chip in enumerate(chips):
            copy(a, 4 + j, (*chip, 1 - c), me).wait_recv()
    for cp in first + passed:
        cp.wait_send()
    for cp in mine:
        cp.wait()


def _gather_scratch(n):
    return [pltpu.SemaphoreType.DMA((n, 7)), pltpu.SemaphoreType.DMA((n, 7)), pltpu.SemaphoreType.DMA((n,))]


def _all_gather(shards, name):
    n = len(shards)

    def body(*refs):
        _gather_body(refs[:n], refs[n:2 * n], *refs[2 * n:], handshake=False)

    hbm = pl.BlockSpec(memory_space=pltpu.HBM)
    return _pallas_call(
        body, name=name,
        in_specs=[hbm] * n, out_specs=[hbm] * n,
        out_shape=[jax.ShapeDtypeStruct((N_DEV,) + s.shape, s.dtype) for s in shards],
        scratch_shapes=_gather_scratch(n),
    )(*shards)


def _handshake(peers):
    barrier = pltpu.get_barrier_semaphore()
    for peer in peers:
        pl.semaphore_signal(barrier, inc=1, device_id=peer, device_id_type=MESH)
    pl.semaphore_wait(barrier, len(peers))


def _sequencer_call(body, name, collective_id, out_type, scratch_types, operands):
    return pl.kernel(
        body, name=name, out_type=out_type,
        mesh=plsc.ScalarSubcoreMesh(axis_name="sequencer", num_cores=1),
        scratch_types=scratch_types,
        compiler_params=pltpu.CompilerParams(collective_id=collective_id),
    )(*operands)


def _seq_all_gather(shards, name, collective_id, after):
    n = len(shards)

    def body(*refs):
        _gather_body(refs[:n], refs[n + 1:2 * n + 1], *refs[2 * n + 1:], handshake=True)

    return _sequencer_call(
        body, name, collective_id,
        [jax.ShapeDtypeStruct((N_DEV,) + s.shape, s.dtype) for s in shards],
        _gather_scratch(n), list(shards) + [after])


def _seq_to_sibling(parts, name, collective_id, after):
    n = len(parts)

    def body(*refs):
        ins, outs = refs[:n], refs[n + len(after):2 * n + len(after)]
        send_sems, recv_sems = refs[2 * n + len(after):]
        x, y, c = _position()
        sibling = (x, y, 1 - c)
        _handshake([sibling])
        sent = []
        for a in range(n):
            for q in range(N_CHIPS):
                cp = pltpu.make_async_remote_copy(
                    src_ref=ins[a].at[2 * q + (1 - c)], dst_ref=outs[a].at[q],
                    send_sem=send_sems.at[a, q], recv_sem=recv_sems.at[a, q],
                    device_id=sibling, device_id_type=MESH)
                cp.start()
                sent.append(cp)
        for cp in sent:
            cp.wait_recv()
        for cp in sent:
            cp.wait_send()

    return _sequencer_call(
        body, name, collective_id,
        [jax.ShapeDtypeStruct((N_CHIPS,) + p.shape[1:], p.dtype) for p in parts],
        [pltpu.SemaphoreType.DMA((n, N_CHIPS)), pltpu.SemaphoreType.DMA((n, N_CHIPS))],
        list(parts) + list(after))


def _seq_to_chips(partials, name, collective_id, after=()):
    n = len(partials)

    def body(*refs):
        ins, outs = refs[:n], refs[n + len(after):2 * n + len(after)]
        send_sems, recv_sems, local_sems = refs[2 * n + len(after):]
        x, y, c = _position()
        my_chip = 2 * x + y
        chips = [(1 - x, y), (x, 1 - y), (1 - x, 1 - y)]
        _handshake([(*chip, c) for chip in chips])
        mine = [pltpu.make_async_copy(ins[a].at[my_chip], outs[a].at[my_chip], local_sems.at[a]) for a in range(n)]
        for cp in mine:
            cp.start()
        sent = []
        for a in range(n):
            for j, (px, py) in enumerate(chips):
                cp = pltpu.make_async_remote_copy(
                    src_ref=ins[a].at[2 * px + py], dst_ref=outs[a].at[my_chip],
                    send_sem=send_sems.at[a, j], recv_sem=recv_sems.at[a, j],
                    device_id=(px, py, c), device_id_type=MESH)
                cp.start()
                sent.append(cp)
        for cp in sent:
            cp.wait_recv()
        for cp in sent:
            cp.wait_send()
        for cp in mine:
            cp.wait()

    return _sequencer_call(
        body, name, collective_id,
        [jax.ShapeDtypeStruct(p.shape, p.dtype) for p in partials],
        [pltpu.SemaphoreType.DMA((n, 3)), pltpu.SemaphoreType.DMA((n, 3)), pltpu.SemaphoreType.DMA((n,))],
        list(partials) + list(after))


def _pair_add(part, recv, name):
    _, R, C = part.shape
    tr = _row_tile(R)
    pairs = part.reshape(N_CHIPS, 2, R, C)

    def body(p_ref, r_ref, o_ref):
        c = lax.axis_index("c")
        o_ref[...] = (p_ref[c].astype(F32) + r_ref[...].astype(F32)).astype(o_ref.dtype)

    return _pallas_call(
        body, name=name, grid=(N_CHIPS, R // tr),
        in_specs=[pl.BlockSpec((None, 2, tr, C), lambda q, i: (q, 0, i, 0)),
                  pl.BlockSpec((None, tr, C), lambda q, i: (q, i, 0))],
        out_specs=pl.BlockSpec((None, tr, C), lambda q, i: (q, i, 0)),
        out_shape=jax.ShapeDtypeStruct((N_CHIPS, R, C), part.dtype),
        compiler_params=_params(32, 2),
    )(pairs, recv)


class _Reduced(NamedTuple):
    partials: list
    reduced: list


def _blocks(g):
    return g.reshape(N_DEV, -1, g.shape[-1])


def _reduce_scatter(parts, tag, ids, after=(), between=None):
    from_sibling = _seq_to_sibling(parts, "rs_sibling_" + tag, ids[0], after)
    mid = between(from_sibling[0]) if between else ()
    partials = [_pair_add(p, r, "rs_add_%s_%d" % (tag, a)) for a, (p, r) in enumerate(zip(parts, from_sibling))]
    return _Reduced(partials, _seq_to_chips(partials, "rs_chips_" + tag, ids[1], mid))


_SMALL = ("ffn1_norm", "mix_norm", "conv_b", "conv_ln_g", "conv_ln_b", "forget_b", "out_norm_conv",
          "out_norm_attn", "ffn2_norm", "final_norm")
_PACK_WIDTH = 2 * D_CONV
_SLOT = dict(ffn1_norm=(0, 0), mix_norm=(1, 0), ffn2_norm=(2, 0), final_norm=(3, 0), conv_b=(4, 0),
             conv_ln_g=(4, D_CONV), conv_ln_b=(5, 0), out_norm_conv=(5, D_CONV), out_norm_attn=(6, 0),
             forget_b=(6, D_CONV))
_CONV_ROW0 = 8
_PACK_ROWS = _CONV_ROW0 + CONV_HALO


def _pack_small(small, name):
    arrays = [small[n] for n in _SMALL] + [small["conv_w"]]

    def body(*refs):
        out = refs[-1]
        out[...] = jnp.zeros_like(out)
        for n, ref in zip(_SMALL, refs):
            row, lane = _SLOT[n]
            out[row:row + 1, lane:lane + ref.shape[1]] = ref[...]
        out[_CONV_ROW0:, :D_CONV] = refs[len(_SMALL)][...]

    return _pallas_call(body, name=name, out_shape=jax.ShapeDtypeStruct((_PACK_ROWS, _PACK_WIDTH), F32))(*arrays)


def _adamw_small(gathered, w, m, v, name):
    c1 = 1.0 - ADAM_B1 ** ADAM_STEP
    c2 = 1.0 - ADAM_B2 ** ADAM_STEP
    k = len(_SMALL)

    def body(g_ref, *refs):
        ws, ms, vs = refs[:k], refs[k:2 * k], refs[2 * k:3 * k]
        outs = refs[3 * k:]
        total = g_ref[0]
        for s in range(1, N_DEV):
            total = total + g_ref[s]
        for i, n in enumerate(_SMALL):
            row, lane = _SLOT[n]
            width = ws[i].shape[1]
            g = total[row:row + 1, lane:lane + width]
            mn = ADAM_B1 * ms[i][...] + (1.0 - ADAM_B1) * g
            vn = ADAM_B2 * vs[i][...] + (1.0 - ADAM_B2) * (g * g)
            o_g, o_d, o_m, o_v = outs[4 * i:4 * i + 4]
            o_g[...] = g
            o_m[...] = mn
            o_v[...] = vn
            o_d[...] = -ADAM_LR * ((mn / c1) / (jnp.sqrt(vn / c2) + ADAM_EPS) + ADAM_WD * ws[i][...])
        outs[4 * k][...] = total[_CONV_ROW0:, :D_CONV]

    shapes = []
    for n in _SMALL:
        shapes += [jax.ShapeDtypeStruct(w[n].shape, F32)] * 4
    shapes.append(jax.ShapeDtypeStruct((CONV_HALO, D_CONV), F32))
    res = _pallas_call(body, name=name, out_shape=shapes)(
        gathered, *[w[n] for n in _SMALL], *[m[n] for n in _SMALL], *[v[n] for n in _SMALL])
    return {n: res[4 * i:4 * i + 4] for i, n in enumerate(_SMALL)}, res[4 * k]


def _local_step(x, target, norms, w13_1, w2_1, winp, conv_w32, wout, w13_2, w2_2):
    x1, G1, U1 = _ffn_fwd(x, norms["ffn1_norm"], w13_1, w2_1, "ffn1_fwd")
    ag, q, k, v, qT, kT, vT, fl = _inproj_fwd(x1, norms["mix_norm"], winp, "inproj_fwd")
    cum, cumT = _forget_fwd(fl, norms["forget_b"], "forget_fwd")
    yc, c = _conv_fwd(ag, conv_w32, norms["conv_b"], norms["conv_ln_g"], norms["conv_ln_b"], "conv_fwd")
    o, lseT = _attn_fwd(qT, k, vT, cum, cumT, "attn_fwd")
    x2 = _outproj_fwd(x1, c, o, norms["out_norm_conv"], norms["out_norm_attn"], wout, "outproj_fwd")
    x3, G2, U2 = _ffn_fwd(x2, norms["ffn2_norm"], w13_2, w2_2, "ffn2_fwd")
    loss, dx3, d_final = _loss_head(x3, norms["final_norm"], target, "loss_head")

    dx2, dw13_2, dw2_2, d_ffn2n = _ffn_bwd(x2, norms["ffn2_norm"], dx3, G2, U2, w13_2, w2_2, "ffn2_bwd")
    dc, dob, dobT, deltaT, dwout, d_onc, d_ona = _outproj_bwd(
        dx2, c, o, norms["out_norm_conv"], norms["out_norm_attn"], wout, "outproj_bwd")
    red_ffn2 = _reduce_scatter([_blocks(dw13_2), _blocks(dw2_2)], "ffn2", (3, 4))
    dq, dk, dv, dcum = _attn_bwd(q, qT, k, kT, v, dob, dobT, lseT, deltaT, cum, cumT, "attn_bwd",
                                 after=red_ffn2.partials)
    dfl, d_fb = _forget_bwd(dcum, fl, norms["forget_b"], "forget_bwd")
    dag, d_convw, d_cb, d_lg, d_lb = _conv_bwd(dc, yc, ag, conv_w32, norms["conv_ln_g"], norms["conv_ln_b"], "conv_bwd")
    dx1, dwinp, d_mixn = _inproj_bwd(x1, norms["mix_norm"], dx2, dag, dq, dk, dv, dfl, winp, "inproj_bwd")
    dwin_blocks = dwinp[:N_IN].reshape(N_DEV, N_IN // N_DEV, -1)
    red_mix = _reduce_scatter([dwin_blocks, _blocks(dwout)], "mix", (5, 6), after=red_ffn2.reduced[:1])
    dx, dw13_1, dw2_1, d_ffn1n = _ffn_bwd(x, norms["ffn1_norm"], dx1, G1, U1, w13_1, w2_1, "ffn1_bwd",
                                          after=red_mix.partials)
    small = dict(ffn1_norm=d_ffn1n, mix_norm=d_mixn, conv_b=d_cb, conv_ln_g=d_lg, conv_ln_b=d_lb,
                 forget_b=d_fb, out_norm_conv=d_onc, out_norm_attn=d_ona, ffn2_norm=d_ffn2n,
                 final_norm=d_final, conv_w=d_convw)
    packed_small = _pack_small(small, "pack_small_grads")
    gathered_small = []

    def gather_small(behind):
        gathered_small.extend(_seq_all_gather([packed_small], "gather_small_grads", 9, after=behind))
        return gathered_small

    red_ffn1 = _reduce_scatter([_blocks(dw13_1), _blocks(dw2_1)], "ffn1", (7, 8), after=red_mix.reduced[:1],
                               between=gather_small)
    big = dict(ffn1_w13=red_ffn1.reduced[0], ffn1_w2=red_ffn1.reduced[1], w_in=red_mix.reduced[0],
               w_out=red_mix.reduced[1], ffn2_w13=red_ffn2.reduced[0], ffn2_w2=red_ffn2.reduced[1])
    return loss[0, 0], dx, gathered_small[0], big


_BIG = ("ffn1_w13", "ffn1_w2", "w_in", "w_out", "ffn2_w13", "ffn2_w2")
_TRANSPOSED = ("ffn1_w13", "ffn2_w13", "w_in")
_ORDER = ("ffn1_norm", "ffn1_w13", "ffn1_w2", "mix_norm", "w_in", "conv_w", "conv_b", "conv_ln_g", "conv_ln_b",
          "forget_b", "out_norm_conv", "out_norm_attn", "w_out", "ffn2_norm", "ffn2_w13", "ffn2_w2", "final_norm")


def kernel(x, ffn1_norm, ffn1_w13, ffn1_w2, mix_norm, w_in, conv_w, conv_b, conv_ln_g, conv_ln_b, forget_b, out_norm_conv, out_norm_attn, w_out, ffn2_norm, ffn2_w13, ffn2_w2, final_norm, loss_target, m_ffn1_norm, m_ffn1_w13, m_ffn1_w2, m_mix_norm, m_w_in, m_conv_w, m_conv_b, m_conv_ln_g, m_conv_ln_b, m_forget_b, m_out_norm_conv, m_out_norm_attn, m_w_out, m_ffn2_norm, m_ffn2_w13, m_ffn2_w2, m_final_norm, v_ffn1_norm, v_ffn1_w13, v_ffn1_w2, v_mix_norm, v_w_in, v_conv_w, v_conv_b, v_conv_ln_g, v_conv_ln_b, v_forget_b, v_out_norm_conv, v_out_norm_attn, v_w_out, v_ffn2_norm, v_ffn2_w13, v_ffn2_w2, v_final_norm):
    w = dict(ffn1_norm=ffn1_norm, ffn1_w13=ffn1_w13, ffn1_w2=ffn1_w2, mix_norm=mix_norm, w_in=w_in, conv_w=conv_w,
             conv_b=conv_b, conv_ln_g=conv_ln_g, conv_ln_b=conv_ln_b, forget_b=forget_b, out_norm_conv=out_norm_conv,
             out_norm_attn=out_norm_attn, w_out=w_out, ffn2_norm=ffn2_norm, ffn2_w13=ffn2_w13, ffn2_w2=ffn2_w2,
             final_norm=final_norm)
    m = dict(ffn1_norm=m_ffn1_norm, ffn1_w13=m_ffn1_w13, ffn1_w2=m_ffn1_w2, mix_norm=m_mix_norm, w_in=m_w_in,
             conv_w=m_conv_w, conv_b=m_conv_b, conv_ln_g=m_conv_ln_g, conv_ln_b=m_conv_ln_b, forget_b=m_forget_b,
             out_norm_conv=m_out_norm_conv, out_norm_attn=m_out_norm_attn, w_out=m_w_out, ffn2_norm=m_ffn2_norm,
             ffn2_w13=m_ffn2_w13, ffn2_w2=m_ffn2_w2, final_norm=m_final_norm)
    v = dict(ffn1_norm=v_ffn1_norm, ffn1_w13=v_ffn1_w13, ffn1_w2=v_ffn1_w2, mix_norm=v_mix_norm, w_in=v_w_in,
             conv_w=v_conv_w, conv_b=v_conv_b, conv_ln_g=v_conv_ln_g, conv_ln_b=v_conv_ln_b, forget_b=v_forget_b,
             out_norm_conv=v_out_norm_conv, out_norm_attn=v_out_norm_attn, w_out=v_w_out, ffn2_norm=v_ffn2_norm,
             ffn2_w13=v_ffn2_w13, ffn2_w2=v_ffn2_w2, final_norm=v_final_norm)
    shapes = {n: a.shape for n, a in w.items()}
    T, D = x.shape[1], x.shape[2]
    def two(n, a):
        if a.ndim != 3:
            return a.reshape(1, -1)
        a = a.reshape(a.shape[-2], a.shape[-1])
        return a.T if n in _TRANSPOSED else a

    w2d = {n: two(n, a) for n, a in w.items()}
    m2d = {n: two(n, a) for n, a in m.items()}
    v2d = {n: two(n, a) for n, a in v.items()}

    cast = lambda n: w2d[n].astype(MXU)
    g13_1, g2_1 = _all_gather([cast("ffn1_w13"), cast("ffn1_w2")], "gather_ffn1")
    gin, gout, gconv = _seq_all_gather([cast("w_in"), cast("w_out"), w2d["conv_w"]], "gather_mix", 1,
                                       after=g13_1)
    g13_2, g2_2 = _seq_all_gather([cast("ffn2_w13"), cast("ffn2_w2")], "gather_ffn2", 2, after=g2_1)
    bf = g13_1.shape[1]
    J = N_DEV // 2
    w13_1 = g13_1.reshape(2, J, bf, D)
    w13_2 = g13_2.reshape(2, J, bf, D)
    w2_1 = g2_1.reshape(-1, D)
    w2_2 = g2_2.reshape(-1, D)
    wout = gout.reshape(-1, D)
    winp = jnp.pad(gin.reshape(N_IN, D), ((0, N_IN_PAD - N_IN), (0, 0)))
    conv_full = gconv.transpose(1, 0, 2).reshape(CONV_TAPS, D_CONV)
    conv_w32 = jnp.pad(conv_full, ((0, CONV_HALO - CONV_TAPS), (0, 0)))

    norms = {n: w2d[n] for n in _SMALL}
    norms["forget_b"] = jnp.pad(w2d["forget_b"], ((0, 0), (0, LANES - N_HEADS)))
    loss_part, dx, gathered_small, big = _local_step(x[0], loss_target[0], norms, w13_1, w2_1, winp, conv_w32,
                                                     wout, w13_2, w2_2)
    loss = lax.psum(loss_part, ("x", "y", "c"))

    grads, deltas, new_m, new_v = {}, {}, {}, {}
    for n in _BIG:
        g, d, nm, nv = _adamw(w2d[n], m2d[n], v2d[n], big[n], "adamw_" + n)
        grads[n], deltas[n], new_m[n], new_v[n] = g, d, nm, nv

    small_out, conv_g_full = _adamw_small(gathered_small, w2d, m2d, v2d, "adamw_small")
    for n in _SMALL:
        grads[n], deltas[n], new_m[n], new_v[n] = small_out[n]
    conv_g_full = conv_g_full[:CONV_TAPS]
    xi, yi, ci = _position()
    cw = shapes["conv_w"][-1]
    conv_g_mine = lax.dynamic_slice_in_dim(conv_g_full, _flat(xi, yi, ci) * cw, cw, axis=1)
    g, d, nm, nv = _adamw(w2d["conv_w"], m2d["conv_w"], v2d["conv_w"], conv_g_mine[None], "adamw_conv_w")
    grads["conv_w"], deltas["conv_w"], new_m["conv_w"], new_v["conv_w"] = g, d, nm, nv

    shaped = lambda dct: [(dct[n].T if n in _TRANSPOSED else dct[n]).reshape(shapes[n]) for n in _ORDER]
    return (loss, dx[None], *shaped(grads), *shaped(deltas), *shaped(new_m), *shaped(new_v))
```

```python
from typing import NamedTuple

import jax
import jax.numpy as jnp
from jax import lax
from jax.experimental import pallas as pl
from jax.experimental.pallas import tpu as pltpu
from jax.experimental.pallas import tpu_sc as plsc

F32 = jnp.float32
MXU = jnp.bfloat16
EPS = 1e-6
N_HEADS = 8
HEAD_DIM = 64
D_CONV = 512
D_ATTN = N_HEADS * HEAD_DIM
CONV_TAPS = 31
CONV_HALO = 32
SCALE = HEAD_DIM ** -0.5
NEG = -1e30
LANES = 128
N_DEV = 8
N_CHIPS = N_DEV // 2
MESH = pl.DeviceIdType.MESH
MIB = 1 << 20

ADAM_LR = 0.001
ADAM_B1 = 0.9
ADAM_B2 = 0.999
ADAM_EPS = 1e-08
ADAM_WD = 0.01
ADAM_STEP = 10


_UNREAD = pl.BlockSpec(memory_space=pl.ANY)


def _pallas_call(body, *, out_shape, **kwargs):
    in_hbm = lambda s: pltpu.HBM(s.shape, s.dtype)
    outs = [in_hbm(s) for s in out_shape] if isinstance(out_shape, (list, tuple)) else in_hbm(out_shape)
    call = pl.pallas_call(body, out_shape=outs, **kwargs)
    return lambda *operands: call(*[pltpu.with_memory_space_constraint(a, pltpu.HBM) for a in operands])


def _params(vmem_mib, n_axes):
    return pltpu.CompilerParams(dimension_semantics=("arbitrary",) * n_axes, vmem_limit_bytes=vmem_mib * MIB)


def _mm(a, b):
    return jnp.dot(a, b, preferred_element_type=F32)


def _mm_nt(a, b):
    return lax.dot_general(a, b, (((1,), (1,)), ((), ())), preferred_element_type=F32)


def _mm_tn(a, b):
    return lax.dot_general(a, b, (((0,), (0,)), ((), ())), preferred_element_type=F32)


def _rms_fwd(x, g):
    r = lax.rsqrt(jnp.mean(x * x, axis=-1, keepdims=True) + EPS)
    return x * r * g, r


def _rms_bwd(x, r, g, dy):
    gdy = dy * g
    dx = r * gdy - x * (r * r * r) * jnp.mean(x * gdy, axis=-1, keepdims=True)
    dg = jnp.sum(dy * x * r, axis=0, keepdims=True)
    return dx, dg


def _silu_grad(z, sz):
    return sz * (1.0 + z * (1.0 - sz))


def _three_terms(x):
    x1 = x.astype(jnp.bfloat16)
    r1 = x - x1.astype(F32)
    x2 = r1.astype(jnp.bfloat16)
    x3 = (r1 - x2.astype(F32)).astype(jnp.bfloat16)
    return x1, x2, x3


def _exact_tri_dot(tri, x):
    x1, x2, x3 = _three_terms(x)
    return _mm(tri, x1) + _mm(tri, x2) + _mm(tri, x3)


def _exact_dot_01(x, sel):
    x1, x2, x3 = _three_terms(x)
    return _mm(x1, sel) + _mm(x2, sel) + _mm(x3, sel)


def _tile(n, want):
    t = min(n, want)
    assert n % t == 0
    return t


_FFN_CHUNK = 256


def _ffn_fwd(x, g, w13, w2, name):
    T, D = x.shape
    _, J, bf, _ = w13.shape
    tm = _tile(T, 512)
    I = T // tm

    def body(x_ref, g_ref, w13_ref, w2_ref, xo_ref, G_ref, U_ref, acc_s):
        j = pl.program_id(0)
        i = pl.program_id(1)
        rows = pl.ds(pl.multiple_of(i * tm, tm), tm)
        xv = x_ref[...]
        h, _ = _rms_fwd(xv, g_ref[...])
        hb = h.astype(MXU)

        @pl.when(j == 0)
        def _():
            acc_s[rows, :] = xv

        chunks = [slice(r0, r0 + _FFN_CHUNK) for r0 in range(0, tm, _FFN_CHUNK)]
        GU = [(_mm_nt(hb[rs, :], w13_ref[0]), _mm_nt(hb[rs, :], w13_ref[1])) for rs in chunks]
        for rs, (G, U) in zip(chunks, GU):
            G_ref[rs, :] = G.astype(MXU)
            U_ref[rs, :] = U.astype(MXU)
            A = (G * jax.nn.sigmoid(G) * U).astype(MXU)
            crows = pl.ds(pl.multiple_of(i * tm + rs.start, _FFN_CHUNK), _FFN_CHUNK)
            acc_s[crows, :] += 0.5 * _mm(A, w2_ref[...])

        @pl.when(j == J - 1)
        def _():
            xo_ref[...] = acc_s[rows, :]

    return _pallas_call(
        body, name=name, grid=(J, I),
        in_specs=[
            pl.BlockSpec((tm, D), lambda j, i: (i, 0)),
            pl.BlockSpec((1, D), lambda j, i: (0, 0)),
            pl.BlockSpec((2, None, bf, D), lambda j, i: (0, j, 0, 0)),
            pl.BlockSpec((bf, D), lambda j, i: (j, 0)),
        ],
        out_specs=[
            pl.BlockSpec((tm, D), lambda j, i: (jnp.where(j == J - 1, i, 0), 0)),
            pl.BlockSpec((None, tm, bf), lambda j, i: (j, i, 0)),
            pl.BlockSpec((None, tm, bf), lambda j, i: (j, i, 0)),
        ],
        out_shape=[
            jax.ShapeDtypeStruct((T, D), F32),
            jax.ShapeDtypeStruct((J, T, bf), MXU),
            jax.ShapeDtypeStruct((J, T, bf), MXU),
        ],
        scratch_shapes=[pltpu.VMEM((T, D), F32)],
        compiler_params=_params(48, 2),
    )(x, g, w13, w2)


def _ffn_bwd_act(x, g, dy, Gs, Us, w13, w2, name, after=()):
    T, D = x.shape
    _, J, bf, _ = w13.shape
    tm = _tile(T, 512)

    def body(x_ref, g_ref, dy_ref, G_ref, U_ref, w13_ref, w2_ref, *rest):
        dx_ref, dg_ref, h_ref, dF_ref, A_ref, dG_ref, dU_ref, dh_s = rest[len(after):]
        i = pl.program_id(0)
        j = pl.program_id(1)

        @pl.when(j == 0)
        def _():
            h, _ = _rms_fwd(x_ref[...], g_ref[...])
            h_ref[...] = h.astype(MXU)
            dF_ref[...] = (0.5 * dy_ref[...]).astype(MXU)
            dh_s[...] = jnp.zeros_like(dh_s)

        chunks = [slice(r0, r0 + _FFN_CHUNK) for r0 in range(0, tm, _FFN_CHUNK)]
        dAs = [_mm_nt(dF_ref[rs, :], w2_ref[...]) for rs in chunks]
        for rs, dA in zip(chunks, dAs):
            G = G_ref[rs, :].astype(F32)
            U = U_ref[rs, :].astype(F32)
            sg = jax.nn.sigmoid(G)
            s = G * sg
            A_ref[rs, :] = (s * U).astype(MXU)
            dUb = (dA * s).astype(MXU)
            dGb = (dA * U * _silu_grad(G, sg)).astype(MXU)
            dG_ref[rs, :] = dGb
            dU_ref[rs, :] = dUb
            dh_s[rs, :] += _mm(dGb, w13_ref[0]) + _mm(dUb, w13_ref[1])

        @pl.when(j == J - 1)
        def _():
            xv = x_ref[...]
            gv = g_ref[...]
            _, r = _rms_fwd(xv, gv)
            dxn, dgp = _rms_bwd(xv, r, gv, dh_s[...])
            dx_ref[...] = dy_ref[...] + dxn

            @pl.when(i == 0)
            def _():
                dg_ref[...] = dgp

            @pl.when(i > 0)
            def _():
                dg_ref[...] += dgp

    row = pl.BlockSpec((tm, D), lambda i, j: (i, 0))
    blk = pl.BlockSpec((None, tm, bf), lambda i, j: (j, i, 0))
    act = jax.ShapeDtypeStruct((T, D), MXU)
    hid = jax.ShapeDtypeStruct((J, T, bf), MXU)
    return _pallas_call(
        body, name=name, grid=(T // tm, J),
        in_specs=[row, pl.BlockSpec((1, D), lambda i, j: (0, 0)), row, blk, blk,
                  pl.BlockSpec((2, None, bf, D), lambda i, j: (0, j, 0, 0)),
                  pl.BlockSpec((bf, D), lambda i, j: (j, 0))] + [_UNREAD] * len(after),
        out_specs=[row, pl.BlockSpec((1, D), lambda i, j: (0, 0)), row, row, blk, blk, blk],
        out_shape=[jax.ShapeDtypeStruct((T, D), F32), jax.ShapeDtypeStruct((1, D), F32), act, act, hid, hid, hid],
        scratch_shapes=[pltpu.VMEM((tm, D), F32)],
        compiler_params=_params(56, 2),
    )(x, g, dy, Gs, Us, w13, w2, *after)


def _ffn_bwd_weights(h, dF, A, dG, dU, name):
    T, D = h.shape
    J, _, bf = A.shape

    def body(h_ref, dF_ref, A_ref, dG_ref, dU_ref, dw13_ref, dw2_ref):
        dw13_ref[0] = _mm_tn(dG_ref[...], h_ref[...]).astype(dw13_ref.dtype)
        dw13_ref[1] = _mm_tn(dU_ref[...], h_ref[...]).astype(dw13_ref.dtype)
        dw2_ref[...] = _mm_tn(A_ref[...], dF_ref[...]).astype(dw2_ref.dtype)

    full = pl.BlockSpec((T, D), lambda j: (0, 0))
    blk = pl.BlockSpec((None, T, bf), lambda j: (j, 0, 0))
    return _pallas_call(
        body, name=name, grid=(J,),
        in_specs=[full, full, blk, blk, blk],
        out_specs=[pl.BlockSpec((2, None, bf, D), lambda j: (0, j, 0, 0)), pl.BlockSpec((bf, D), lambda j: (j, 0))],
        out_shape=[jax.ShapeDtypeStruct((2, J, bf, D), MXU), jax.ShapeDtypeStruct((J * bf, D), MXU)],
        compiler_params=_params(56, 1),
    )(h, dF, A, dG, dU)


def _ffn_bwd(x, g, dy, Gs, Us, w13, w2, name, after=()):
    dx, dg, h, dF, A, dG, dU = _ffn_bwd_act(x, g, dy, Gs, Us, w13, w2, name + "_act", after)
    dw13, dw2 = _ffn_bwd_weights(h, dF, A, dG, dU, name + "_weights")
    return dx, dw13, dw2, dg


_AG0, _Q0, _K0, _V0, _F0 = 0, 2 * D_CONV, 2 * D_CONV + D_ATTN, 2 * D_CONV + 2 * D_ATTN, 2 * D_CONV + 3 * D_ATTN
N_IN = _F0 + N_HEADS
N_IN_PAD = _F0 + LANES


def _inproj_fwd(x1, gm, winp, name):
    T, D = x1.shape
    tm = _tile(T, 256)

    def body(x_ref, g_ref, w_ref, ag_ref, k_ref, v_ref, qT_ref, kT_ref, vT_ref, fl_ref):
        h, _ = _rms_fwd(x_ref[...], g_ref[...])
        hb = h.astype(MXU)
        ag_ref[...] = _mm_nt(hb, w_ref[_AG0:_Q0, :])
        for c0, ref, refT in ((_Q0, None, qT_ref), (_K0, k_ref, kT_ref), (_V0, v_ref, vT_ref)):
            y = _mm_nt(hb, w_ref[c0:c0 + D_ATTN, :])
            if ref is not None:
                ref[...] = y.astype(MXU)
            refT[...] = y.T.astype(MXU)
        fl_ref[...] = _mm_nt(hb, w_ref[_F0:N_IN_PAD, :])

    row = lambda w: pl.BlockSpec((tm, w), lambda i: (i, 0))
    col = pl.BlockSpec((D_ATTN, tm), lambda i: (0, i))
    std = jax.ShapeDtypeStruct((T, D_ATTN), MXU)
    trn = jax.ShapeDtypeStruct((D_ATTN, T), MXU)
    return _pallas_call(
        body, name=name, grid=(T // tm,),
        in_specs=[row(D), pl.BlockSpec((1, D), lambda i: (0, 0)), pl.BlockSpec((N_IN_PAD, D), lambda i: (0, 0))],
        out_specs=[row(2 * D_CONV), row(D_ATTN), row(D_ATTN), col, col, col, row(LANES)],
        out_shape=[jax.ShapeDtypeStruct((T, 2 * D_CONV), F32), std, std, trn, trn, trn,
                   jax.ShapeDtypeStruct((T, LANES), F32)],
        compiler_params=_params(40, 1),
    )(x1, gm, winp)


def _inproj_bwd_act(x1, gm, dx2, dag, dqT, dkT, dvT, dfl, winp, name):
    T, D = x1.shape
    tm = _tile(T, 256)

    def body(x_ref, g_ref, dx2_ref, dag_ref, dqT_ref, dkT_ref, dvT_ref, dfl_ref, w_ref, dx1_ref, dg_ref, h_ref):
        i = pl.program_id(0)
        xv = x_ref[...]
        gv = g_ref[...]
        h, r = _rms_fwd(xv, gv)
        h_ref[...] = h.astype(MXU)
        dh = _mm(dag_ref[...], w_ref[_AG0:_Q0, :])
        for c0, ref in ((_Q0, dqT_ref), (_K0, dkT_ref), (_V0, dvT_ref)):
            dh = dh + _mm_tn(ref[...].astype(MXU), w_ref[c0:c0 + D_ATTN, :])
        dh = dh + _mm(dfl_ref[...].astype(MXU), w_ref[_F0:N_IN_PAD, :])
        dxn, dgp = _rms_bwd(xv, r, gv, dh)
        dx1_ref[...] = dx2_ref[...] + dxn

        @pl.when(i == 0)
        def _():
            dg_ref[...] = dgp

        @pl.when(i > 0)
        def _():
            dg_ref[...] += dgp

    row = lambda w: pl.BlockSpec((tm, w), lambda i: (i, 0))
    col = pl.BlockSpec((D_ATTN, tm), lambda i: (0, i))
    full = lambda a, b: pl.BlockSpec((a, b), lambda i: (0, 0))
    return _pallas_call(
        body, name=name, grid=(T // tm,),
        in_specs=[row(D), full(1, D), row(D), row(2 * D_CONV), col, col, col, row(LANES), full(N_IN_PAD, D)],
        out_specs=[row(D), full(1, D), row(D)],
        out_shape=[jax.ShapeDtypeStruct((T, D), F32), jax.ShapeDtypeStruct((1, D), F32),
                   jax.ShapeDtypeStruct((T, D), MXU)],
        compiler_params=_params(40, 1),
    )(x1, gm, dx2, dag, dqT, dkT, dvT, dfl, winp)


def _inproj_bwd_weights(h, dag, dqT, dkT, dvT, dfl, name):
    T, D = h.shape

    def body(h_ref, dag_ref, dqT_ref, dkT_ref, dvT_ref, dfl_ref, dw_ref):
        hb = h_ref[...]
        dw_ref[_AG0:_Q0, :] = _mm_tn(dag_ref[...], hb).astype(dw_ref.dtype)
        for c0, ref in ((_Q0, dqT_ref), (_K0, dkT_ref), (_V0, dvT_ref)):
            dw_ref[c0:c0 + D_ATTN, :] = _mm(ref[...].astype(MXU), hb).astype(dw_ref.dtype)
        dw_ref[_F0:N_IN_PAD, :] = _mm_tn(dfl_ref[...].astype(MXU), hb).astype(dw_ref.dtype)

    return _pallas_call(
        body, name=name, out_shape=jax.ShapeDtypeStruct((N_IN_PAD, D), MXU),
        compiler_params=pltpu.CompilerParams(vmem_limit_bytes=56 * MIB),
    )(h, dag, dqT, dkT, dvT, dfl)


def _forget_fwd(fl, fbp, name):
    T = fl.shape[0]
    tb = _tile(T, 256)

    def body(fl_ref, fb_ref, cum_ref, cumT_ref):
        ri = lax.broadcasted_iota(jnp.int32, (tb, tb), 0)
        ci = lax.broadcasted_iota(jnp.int32, (tb, tb), 1)
        tri = (ri >= ci).astype(jnp.bfloat16)
        carry = jnp.zeros((1, LANES), F32)
        for b in range(T // tb):
            z = fl_ref[b * tb:(b + 1) * tb, :] + fb_ref[...]
            lf = jnp.minimum(z, 0.0) - jnp.log1p(jnp.exp(-jnp.abs(z)))
            c = _exact_tri_dot(tri, lf) + carry
            cum_ref[b * tb:(b + 1) * tb, :] = c
            carry = c[tb - 1:tb, :]
        cumT_ref[...] = cum_ref[...].T[:N_HEADS, :]

    return _pallas_call(
        body, name=name,
        out_shape=[jax.ShapeDtypeStruct((T, LANES), F32), jax.ShapeDtypeStruct((N_HEADS, T), F32)],
        compiler_params=pltpu.CompilerParams(vmem_limit_bytes=32 * MIB),
    )(fl, fbp)


def _forget_bwd(dcum, fl, fbp, name):
    T = fl.shape[0]
    tb = _tile(T, 256)

    def body(dc_ref, fl_ref, fb_ref, dfl_ref, dfb_ref):
        ri = lax.broadcasted_iota(jnp.int32, (tb, tb), 0)
        ci = lax.broadcasted_iota(jnp.int32, (tb, tb), 1)
        tri = (ri <= ci).astype(jnp.bfloat16)
        carry = jnp.zeros((1, LANES), F32)
        dfb = jnp.zeros((1, LANES), F32)
        for b in reversed(range(T // tb)):
            sl = slice(b * tb, (b + 1) * tb)
            dl = _exact_tri_dot(tri, dc_ref[sl, :]) + carry
            carry = dl[0:1, :]
            z = fl_ref[sl, :] + fb_ref[...]
            dfl = dl * jax.nn.sigmoid(-z)
            dfl_ref[sl, :] = dfl
            dfb = dfb + jnp.sum(dfl, axis=0, keepdims=True)
        dfb_ref[...] = dfb

    return _pallas_call(
        body, name=name,
        out_shape=[jax.ShapeDtypeStruct((T, LANES), F32), jax.ShapeDtypeStruct((1, LANES), F32)],
        compiler_params=pltpu.CompilerParams(vmem_limit_bytes=32 * MIB),
    )(dcum, fl, fbp)


def _causal_keep(i, j, tq, tk):
    key = j * tk + lax.broadcasted_iota(jnp.int32, (tk, tq), 0)
    qry = i * tq + lax.broadcasted_iota(jnp.int32, (tk, tq), 1)
    return key <= qry


def _split_hi_lo(x):
    hi = x.astype(MXU)
    lo = (x - hi.astype(F32)).astype(MXU)
    return hi, lo


def _attn_fwd(qT, k, vT, cum, cumT, name):
    T = k.shape[0]
    tq = _tile(T, 256)
    tk = _tile(tq, 128)
    kpq = tq // tk
    heads = [slice(HEAD_DIM * h, HEAD_DIM * (h + 1)) for h in range(N_HEADS)]

    def body(qT_ref, k_ref, vT_ref, cum_ref, cumT_ref, o_ref, lseT_ref, acc_s, m_s, l_s):
        i = pl.program_id(0)
        acc_s[...] = jnp.zeros_like(acc_s)
        m_s[...] = jnp.full_like(m_s, NEG)
        l_s[...] = jnp.zeros_like(l_s)

        def kblock(j, masked):
            rows = pl.ds(pl.multiple_of(j * tk, tk), tk)
            keep = _causal_keep(i, j, tq, tk) if masked else None
            qk = [_mm(k_ref[rows, hs], qT_ref[hs, :]) for hs in heads]
            for h, hs in enumerate(heads):
                sT = qk[h] * SCALE + (cumT_ref[h:h + 1, :] - cum_ref[rows, h:h + 1])
                if masked:
                    sT = jnp.where(keep, sT, NEG)
                m_old = m_s[h:h + 1, :]
                m_new = jnp.maximum(m_old, jnp.max(sT, axis=0, keepdims=True))
                alpha = jnp.exp(m_old - m_new)
                pT = jnp.exp(sT - m_new)
                l_s[h:h + 1, :] = alpha * l_s[h:h + 1, :] + jnp.sum(pT, axis=0, keepdims=True)
                p_hi, p_lo = _split_hi_lo(pT)
                vh = vT_ref[hs, rows]
                acc_s[hs, :] = alpha * acc_s[hs, :] + (_mm(vh, p_hi) + _mm(vh, p_lo))
                m_s[h:h + 1, :] = m_new

        def unmasked(j, c):
            kblock(j, False)
            return c

        lax.fori_loop(0, kpq * i, unmasked, 0)
        for d in range(kpq):
            kblock(kpq * i + d, True)
        for h, hs in enumerate(heads):
            acc_s[hs, :] = acc_s[hs, :] / l_s[h:h + 1, :]
        o_ref[...] = acc_s[...].T
        lseT_ref[...] = m_s[...] + jnp.log(l_s[...])

    full = lambda a, b: pl.BlockSpec((a, b), lambda i: (0, 0))
    colblk = lambda r: pl.BlockSpec((r, tq), lambda i: (0, i))
    return _pallas_call(
        body, name=name, grid=(T // tq,),
        in_specs=[colblk(D_ATTN), full(T, D_ATTN), full(D_ATTN, T), full(T, LANES), colblk(N_HEADS)],
        out_specs=[pl.BlockSpec((tq, D_ATTN), lambda i: (i, 0)), colblk(N_HEADS)],
        out_shape=[jax.ShapeDtypeStruct((T, D_ATTN), F32), jax.ShapeDtypeStruct((N_HEADS, T), F32)],
        scratch_shapes=[pltpu.VMEM((D_ATTN, tq), F32), pltpu.VMEM((N_HEADS, tq), F32),
                        pltpu.VMEM((N_HEADS, tq), F32)],
        compiler_params=_params(40, 1),
    )(qT, k, vT, cum, cumT)


def _attn_bwd(qT, k, kT, v, doT, lseT, deltaT, cum, cumT, name, after=()):
    T = k.shape[0]
    tq = _tile(T, 256)
    tk = _tile(tq, 128)
    kpq = tq // tk
    heads = [slice(HEAD_DIM * h, HEAD_DIM * (h + 1)) for h in range(N_HEADS)]

    def body(qT_ref, k_ref, kT_ref, v_ref, doT_ref, lseT_ref, dlT_ref, cum_ref, cumT_ref, *rest):
        dq_ref, dk_ref, dv_ref, dcum_ref, dq_s = rest[len(after):]
        i = pl.program_id(0)

        @pl.when(i == 0)
        def _():
            dk_ref[...] = jnp.zeros_like(dk_ref)
            dv_ref[...] = jnp.zeros_like(dv_ref)
            dcum_ref[...] = jnp.zeros_like(dcum_ref)

        dq_s[...] = jnp.zeros_like(dq_s)

        def kblock(j, masked):
            rows = pl.ds(pl.multiple_of(j * tk, tk), tk)
            keep = _causal_keep(i, j, tq, tk) if masked else None
            qk = [_mm(k_ref[rows, hs], qT_ref[hs, :]) for hs in heads]
            dps = [_mm(v_ref[rows, hs], doT_ref[hs, :]) for hs in heads]
            for h, hs in enumerate(heads):
                sT = qk[h] * SCALE + (cumT_ref[h:h + 1, :] - cum_ref[rows, h:h + 1])
                if masked:
                    sT = jnp.where(keep, sT, NEG)
                pT = jnp.exp(sT - lseT_ref[h:h + 1, :])
                dsT = pT * (dps[h] - dlT_ref[h:h + 1, :])
                dcum_ref[rows, h:h + 1] += -jnp.sum(dsT, axis=1, keepdims=True)
                dsb = dsT.astype(MXU)
                dv_ref[hs, rows] += _mm_nt(doT_ref[hs, :], pT.astype(MXU))
                dk_ref[hs, rows] += _mm_nt(qT_ref[hs, :], dsb) * SCALE
                dq_s[hs, :] += _mm(kT_ref[hs, rows], dsb)

        def unmasked(j, c):
            kblock(j, False)
            return c

        lax.fori_loop(0, kpq * i, unmasked, 0)
        for d in range(kpq):
            kblock(kpq * i + d, True)
        dq_ref[...] = (dq_s[...] * SCALE).astype(dq_ref.dtype)

    full = lambda a, b: pl.BlockSpec((a, b), lambda i: (0, 0))
    colblk = lambda r: pl.BlockSpec((r, tq), lambda i: (0, i))
    return _pallas_call(
        body, name=name, grid=(T // tq,),
        in_specs=[colblk(D_ATTN), full(T, D_ATTN), full(D_ATTN, T), full(T, D_ATTN), colblk(D_ATTN),
                  colblk(N_HEADS), colblk(N_HEADS), full(T, LANES), colblk(N_HEADS)] + [_UNREAD] * len(after),
        out_specs=[colblk(D_ATTN), full(D_ATTN, T), full(D_ATTN, T), full(T, LANES)],
        out_shape=[
            jax.ShapeDtypeStruct((D_ATTN, T), MXU),
            jax.ShapeDtypeStruct((D_ATTN, T), F32),
            jax.ShapeDtypeStruct((D_ATTN, T), F32),
            jax.ShapeDtypeStruct((T, LANES), F32),
        ],
        scratch_shapes=[pltpu.VMEM((D_ATTN, tq), F32)],
        compiler_params=_params(48, 1),
    )(qT, k, kT, v, doT, lseT, deltaT, cum, cumT, *after)


_ROWS_PER_CHUNK = 64


def _glu_halo(ag_ref, agh_ref, uext_s, tm, first):
    a = ag_ref[:, :D_CONV]
    sg = jax.nn.sigmoid(ag_ref[:, D_CONV:])
    uh = agh_ref[:, :D_CONV] * jax.nn.sigmoid(agh_ref[:, D_CONV:])
    uext_s[0:CONV_HALO, :] = jnp.where(first, 0.0, uh)
    uext_s[CONV_HALO:CONV_HALO + tm, :] = a * sg
    return a, sg


def _layer_norm_stats(y):
    mu = jnp.mean(y, axis=-1, keepdims=True)
    xc = y - mu
    rs = lax.rsqrt(jnp.mean(xc * xc, axis=-1, keepdims=True) + EPS)
    return xc * rs, rs


def _conv_fwd(ag, w32, cb, lg, lb, name):
    T = ag.shape[0]
    tm = _tile(T, 256)
    rc = _tile(tm, _ROWS_PER_CHUNK)
    hb = tm // CONV_HALO

    def body(ag_ref, agh_ref, w_ref, cb_ref, lg_ref, lb_ref, yc_ref, c_ref, uext_s):
        i = pl.program_id(0)
        _glu_halo(ag_ref, agh_ref, uext_s, tm, i == 0)
        for r0 in range(0, tm, rc):
            acc = jnp.zeros((rc, D_CONV), F32)
            for t in range(CONV_TAPS):
                acc = acc + uext_s[pl.ds(r0 + CONV_HALO - (CONV_TAPS - 1) + t, rc), :] * w_ref[t:t + 1, :]
            y = acc + cb_ref[...]
            yc_ref[r0:r0 + rc, :] = y
            n, _ = _layer_norm_stats(y)
            z = n * lg_ref[...] + lb_ref[...]
            c_ref[r0:r0 + rc, :] = z * jax.nn.sigmoid(z)

    row = lambda w: pl.BlockSpec((tm, w), lambda i: (i, 0))
    full = lambda a, b: pl.BlockSpec((a, b), lambda i: (0, 0))
    return _pallas_call(
        body, name=name, grid=(T // tm,),
        in_specs=[row(2 * D_CONV),
                  pl.BlockSpec((CONV_HALO, 2 * D_CONV), lambda i: (jnp.maximum(i * hb - 1, 0), 0)),
                  full(CONV_HALO, D_CONV), full(1, D_CONV), full(1, D_CONV), full(1, D_CONV)],
        out_specs=[row(D_CONV), row(D_CONV)],
        out_shape=[jax.ShapeDtypeStruct((T, D_CONV), F32), jax.ShapeDtypeStruct((T, D_CONV), F32)],
        scratch_shapes=[pltpu.VMEM((CONV_HALO + tm, D_CONV), F32)],
        compiler_params=_params(32, 1),
    )(ag, ag, w32, cb, lg, lb)


def _conv_bwd(dc, yc, ag, w32, lg, lb, name):
    T = ag.shape[0]
    tm = _tile(T, 256)
    rc = _tile(tm, _ROWS_PER_CHUNK)
    I = T // tm
    hb = tm // CONV_HALO
    n_halo_blocks = T // CONV_HALO

    def body(dc_ref, yc_ref, dch_ref, ych_ref, ag_ref, agh_ref, w_ref, lg_ref, lb_ref,
             dag_ref, dw_ref, dcb_ref, dlg_ref, dlb_ref, uext_s, dext_s):
        i = pl.program_id(0)
        lgv = lg_ref[...]
        lbv = lb_ref[...]

        def ln_bwd(dcv, ycv):
            n, rs = _layer_norm_stats(ycv)
            z = n * lgv + lbv
            dz = dcv * _silu_grad(z, jax.nn.sigmoid(z))
            dn = dz * lgv
            dy = rs * (dn - jnp.mean(dn, axis=-1, keepdims=True) - n * jnp.mean(dn * n, axis=-1, keepdims=True))
            return dy, dz, n

        dy, dz, n = ln_bwd(dc_ref[...], yc_ref[...])
        dyh, _, _ = ln_bwd(dch_ref[...], ych_ref[...])
        dext_s[0:tm, :] = dy
        dext_s[tm:tm + CONV_HALO, :] = jnp.where(i == I - 1, 0.0, dyh)
        a, sg = _glu_halo(ag_ref, agh_ref, uext_s, tm, i == 0)

        @pl.when(i == 0)
        def _():
            dw_ref[...] = jnp.zeros_like(dw_ref)
            dcb_ref[...] = jnp.zeros_like(dcb_ref)
            dlg_ref[...] = jnp.zeros_like(dlg_ref)
            dlb_ref[...] = jnp.zeros_like(dlb_ref)

        dcb_ref[...] += jnp.sum(dy, axis=0, keepdims=True)
        dlg_ref[...] += jnp.sum(dz * n, axis=0, keepdims=True)
        dlb_ref[...] += jnp.sum(dz, axis=0, keepdims=True)
        for t in range(CONV_TAPS):
            u_t = uext_s[pl.ds(CONV_HALO - (CONV_TAPS - 1) + t, tm), :]
            dw_ref[t:t + 1, :] += jnp.sum(dy * u_t, axis=0, keepdims=True)
        for r0 in range(0, tm, rc):
            acc = jnp.zeros((rc, D_CONV), F32)
            for t in range(CONV_TAPS):
                acc = acc + dext_s[pl.ds(r0 + (CONV_TAPS - 1) - t, rc), :] * w_ref[t:t + 1, :]
            a_c = a[r0:r0 + rc, :]
            sg_c = sg[r0:r0 + rc, :]
            dag_ref[r0:r0 + rc, :D_CONV] = (acc * sg_c).astype(dag_ref.dtype)
            dag_ref[r0:r0 + rc, D_CONV:] = (acc * a_c * sg_c * (1.0 - sg_c)).astype(dag_ref.dtype)

    row = lambda w: pl.BlockSpec((tm, w), lambda i: (i, 0))
    full = lambda a, b: pl.BlockSpec((a, b), lambda i: (0, 0))
    nxt = pl.BlockSpec((CONV_HALO, D_CONV), lambda i: (jnp.minimum((i + 1) * hb, n_halo_blocks - 1), 0))
    return _pallas_call(
        body, name=name, grid=(I,),
        in_specs=[row(D_CONV), row(D_CONV), nxt, nxt, row(2 * D_CONV),
                  pl.BlockSpec((CONV_HALO, 2 * D_CONV), lambda i: (jnp.maximum(i * hb - 1, 0), 0)),
                  full(CONV_HALO, D_CONV), full(1, D_CONV), full(1, D_CONV)],
        out_specs=[row(2 * D_CONV), full(CONV_HALO, D_CONV), full(1, D_CONV), full(1, D_CONV), full(1, D_CONV)],
        out_shape=[
            jax.ShapeDtypeStruct((T, 2 * D_CONV), MXU),
            jax.ShapeDtypeStruct((CONV_HALO, D_CONV), F32),
            jax.ShapeDtypeStruct((1, D_CONV), F32),
            jax.ShapeDtypeStruct((1, D_CONV), F32),
            jax.ShapeDtypeStruct((1, D_CONV), F32),
        ],
        scratch_shapes=[pltpu.VMEM((CONV_HALO + tm, D_CONV), F32), pltpu.VMEM((tm + CONV_HALO, D_CONV), F32)],
        compiler_params=_params(32, 1),
    )(dc, yc, dc, yc, ag, ag, w32, lg, lb)


def _outproj_fwd(x1, c, o, gc, ga, wout, name):
    T, D = x1.shape
    tm = _tile(T, 512)

    def body(x_ref, c_ref, o_ref, gc_ref, ga_ref, w_ref, x2_ref):
        yc, _ = _rms_fwd(c_ref[...], gc_ref[...])
        ya, _ = _rms_fwd(o_ref[...], ga_ref[...])
        x2_ref[...] = (x_ref[...] + _mm(yc.astype(MXU), w_ref[:D_CONV, :])
                       + _mm(ya.astype(MXU), w_ref[D_CONV:, :]))

    row = lambda w: pl.BlockSpec((tm, w), lambda i: (i, 0))
    full = lambda a, b: pl.BlockSpec((a, b), lambda i: (0, 0))
    return _pallas_call(
        body, name=name, grid=(T // tm,),
        in_specs=[row(D), row(D_CONV), row(D_ATTN), full(1, D_CONV), full(1, D_ATTN), full(D_CONV + D_ATTN, D)],
        out_specs=row(D),
        out_shape=jax.ShapeDtypeStruct((T, D), F32),
        compiler_params=_params(32, 1),
    )(x1, c, o, gc, ga, wout)


def _outproj_bwd(dx2, c, o, gc, ga, wout, name):
    T, D = dx2.shape
    tm = _tile(T, 256)
    I = T // tm

    def body(dx_ref, c_ref, o_ref, gc_ref, ga_ref, w_ref,
             dc_ref, doT_ref, dlT_ref, dw_ref, dgc_ref, dga_ref, acc_s):
        i = pl.program_id(0)
        dxb = dx_ref[...].astype(MXU)
        cv = c_ref[...]
        ov = o_ref[...]
        yc, rcn = _rms_fwd(cv, gc_ref[...])
        ya, ra = _rms_fwd(ov, ga_ref[...])
        dyc = _mm_nt(dxb, w_ref[:D_CONV, :])
        dya = _mm_nt(dxb, w_ref[D_CONV:, :])
        dwc = _mm_tn(yc.astype(MXU), dxb)
        dwa = _mm_tn(ya.astype(MXU), dxb)
        dcv, dgc = _rms_bwd(cv, rcn, gc_ref[...], dyc)
        dov, dga = _rms_bwd(ov, ra, ga_ref[...], dya)
        dc_ref[...] = dcv
        dob = dov.astype(doT_ref.dtype)
        doT_ref[...] = dov.T.astype(doT_ref.dtype)
        chan = lax.broadcasted_iota(jnp.int32, (D_ATTN, LANES), 0)
        head = lax.broadcasted_iota(jnp.int32, (D_ATTN, LANES), 1)
        in_head = ((chan >= head * HEAD_DIM) & (chan < (head + 1) * HEAD_DIM)).astype(jnp.bfloat16)
        dlT_ref[...] = _exact_dot_01(dob.astype(F32) * ov, in_head).T[:N_HEADS, :]

        @pl.when(i == 0)
        def _():
            acc_s[:D_CONV, :] = dwc
            acc_s[D_CONV:, :] = dwa
            dgc_ref[...] = dgc
            dga_ref[...] = dga

        @pl.when(i > 0)
        def _():
            acc_s[:D_CONV, :] += dwc
            acc_s[D_CONV:, :] += dwa
            dgc_ref[...] += dgc
            dga_ref[...] += dga

        @pl.when(i == I - 1)
        def _():
            dw_ref[...] = acc_s[...].astype(dw_ref.dtype)

    row = lambda w: pl.BlockSpec((tm, w), lambda i: (i, 0))
    full = lambda a, b: pl.BlockSpec((a, b), lambda i: (0, 0))
    return _pallas_call(
        body, name=name, grid=(I,),
        in_specs=[row(D), row(D_CONV), row(D_ATTN), full(1, D_CONV), full(1, D_ATTN), full(D_CONV + D_ATTN, D)],
        out_specs=[row(D_CONV), pl.BlockSpec((D_ATTN, tm), lambda i: (0, i)),
                   pl.BlockSpec((N_HEADS, tm), lambda i: (0, i)),
                   full(D_CONV + D_ATTN, D), full(1, D_CONV), full(1, D_ATTN)],
        out_shape=[
            jax.ShapeDtypeStruct((T, D_CONV), F32),
            jax.ShapeDtypeStruct((D_ATTN, T), MXU),
            jax.ShapeDtypeStruct((N_HEADS, T), F32),
            jax.ShapeDtypeStruct((D_CONV + D_ATTN, D), MXU),
            jax.ShapeDtypeStruct((1, D_CONV), F32),
            jax.ShapeDtypeStruct((1, D_ATTN), F32),
        ],
        scratch_shapes=[pltpu.VMEM((D_CONV + D_ATTN, D), F32)],
        compiler_params=_params(40, 1),
    )(dx2, c, o, gc, ga, wout)


def _loss_head(x3, gf, target, name):
    T, D = x3.shape
    tm = _tile(T, 512)

    def body(x_ref, g_ref, t_ref, loss_ref, dx_ref, dg_ref):
        i = pl.program_id(0)
        xv = x_ref[...]
        gv = g_ref[...]
        out, r = _rms_fwd(xv, gv)
        err = out - t_ref[...]
        part = jnp.full((1, LANES), 0.5 / D, F32) * jnp.sum(err * err)
        dxn, dgp = _rms_bwd(xv, r, gv, err * (1.0 / D))
        dx_ref[...] = dxn

        @pl.when(i == 0)
        def _():
            loss_ref[...] = part
            dg_ref[...] = dgp

        @pl.when(i > 0)
        def _():
            loss_ref[...] += part
            dg_ref[...] += dgp

    row = lambda w: pl.BlockSpec((tm, w), lambda i: (i, 0))
    full = lambda a, b: pl.BlockSpec((a, b), lambda i: (0, 0))
    return _pallas_call(
        body, name=name, grid=(T // tm,),
        in_specs=[row(D), full(1, D), row(D)],
        out_specs=[full(1, LANES), row(D), full(1, D)],
        out_shape=[jax.ShapeDtypeStruct((1, LANES), F32), jax.ShapeDtypeStruct((T, D), F32),
                   jax.ShapeDtypeStruct((1, D), F32)],
        compiler_params=_params(32, 1),
    )(x3, gf, target)


def _row_tile(rows):
    for cand in (256, 176, 128, 64, 32, 16):
        if rows % cand == 0:
            return cand
    return rows


def _adamw(w, m, v, parts, name):
    R, C = w.shape
    P = parts.shape[0]
    tr = _row_tile(R)
    c1 = 1.0 - ADAM_B1 ** ADAM_STEP
    c2 = 1.0 - ADAM_B2 ** ADAM_STEP

    def body(w_ref, m_ref, v_ref, p_ref, g_ref, d_ref, nm_ref, nv_ref):
        g = p_ref[0].astype(F32)
        for s in range(1, P):
            g = g + p_ref[s].astype(F32)
        wv = w_ref[...]
        mn = ADAM_B1 * m_ref[...] + (1.0 - ADAM_B1) * g
        vn = ADAM_B2 * v_ref[...] + (1.0 - ADAM_B2) * (g * g)
        g_ref[...] = g
        nm_ref[...] = mn
        nv_ref[...] = vn
        d_ref[...] = -ADAM_LR * ((mn / c1) / (jnp.sqrt(vn / c2) + ADAM_EPS) + ADAM_WD * wv)

    blk = pl.BlockSpec((tr, C), lambda i: (i, 0))
    out = jax.ShapeDtypeStruct((R, C), F32)
    return _pallas_call(
        body, name=name, grid=(R // tr,),
        in_specs=[blk, blk, blk, pl.BlockSpec((P, tr, C), lambda i: (0, i, 0))],
        out_specs=[blk, blk, blk, blk],
        out_shape=[out, out, out, out],
        compiler_params=_params(32, 1),
    )(w, m, v, parts)


def _position():
    return lax.axis_index("x"), lax.axis_index("y"), lax.axis_index("c")


def _flat(px, py, pc):
    return 4 * px + 2 * py + pc


def _gather_body(ins, outs, send_sems, recv_sems, local_sems, handshake):
    n = len(ins)
    x, y, c = _position()
    me, sibling = (x, y, c), (x, y, 1 - c)
    chips = [(1 - x, y), (x, 1 - y), (1 - x, 1 - y)]
    if handshake:
        _handshake([sibling] + [(*chip, cc) for chip in chips for cc in (c, 1 - c)])

    def copy(a, k, block, to, src=None):
        dst = outs[a].at[_flat(*block)]
        return pltpu.make_async_remote_copy(
            src_ref=dst if src is None else src, dst_ref=dst,
            send_sem=send_sems.at[a, k], recv_sem=recv_sems.at[a, k],
            device_id=to, device_id_type=MESH)

    mine = [pltpu.make_async_copy(ins[a], outs[a].at[_flat(*me)], local_sems.at[a]) for a in range(n)]
    for cp in mine:
        cp.start()
    first = []
    for a in range(n):
        first.append(copy(a, 0, me, sibling, src=ins[a]))
        first += [copy(a, 1 + j, me, (*chip, c), src=ins[a]) for j, chip in enumerate(chips)]
    for cp in first:
        cp.start()
    passed = []
    for a in range(n):
        for j, chip in enumerate(chips):
            copy(a, 1 + j, (*chip, c), me).wait_recv()
            fwd = copy(a, 4 + j, (*chip, c), sibling)
            fwd.start()
            passed.append(fwd)
    for a in range(n):
        copy(a, 0, sibling, me).wait_recv()
        for j, chip in enumerate(chips):
            copy(a, 4 + j, (*chip, 1 - c), me).wait_recv()
    for cp in first + passed:
        cp.wait_send()
    for cp in mine:
        cp.wait()


def _gather_scratch(n):
    return [pltpu.SemaphoreType.DMA((n, 7)), pltpu.SemaphoreType.DMA((n, 7)), pltpu.SemaphoreType.DMA((n,))]


def _all_gather(shards, name):
    n = len(shards)

    def body(*refs):
        _gather_body(refs[:n], refs[n:2 * n], *refs[2 * n:], handshake=False)

    hbm = pl.BlockSpec(memory_space=pltpu.HBM)
    return _pallas_call(
        body, name=name,
        in_specs=[hbm] * n, out_specs=[hbm] * n,
        out_shape=[jax.ShapeDtypeStruct((N_DEV,) + s.shape, s.dtype) for s in shards],
        scratch_shapes=_gather_scratch(n),
    )(*shards)


def _handshake(peers):
    barrier = pltpu.get_barrier_semaphore()
    for peer in peers:
        pl.semaphore_signal(barrier, inc=1, device_id=peer, device_id_type=MESH)
    pl.semaphore_wait(barrier, len(peers))


def _sequencer_call(body, name, collective_id, out_type, scratch_types, operands):
    return pl.kernel(
        body, name=name, out_type=out_type,
        mesh=plsc.ScalarSubcoreMesh(axis_name="sequencer", num_cores=1),
        scratch_types=scratch_types,
        compiler_params=pltpu.CompilerParams(collective_id=collective_id),
    )(*operands)


def _seq_all_gather(shards, name, collective_id, after):
    n = len(shards)

    def body(*refs):
        _gather_body(refs[:n], refs[n + 1:2 * n + 1], *refs[2 * n + 1:], handshake=True)

    return _sequencer_call(
        body, name, collective_id,
        [jax.ShapeDtypeStruct((N_DEV,) + s.shape, s.dtype) for s in shards],
        _gather_scratch(n), list(shards) + [after])


def _seq_to_sibling(parts, name, collective_id, after):
    n = len(parts)

    def body(*refs):
        ins, outs = refs[:n], refs[n + len(after):2 * n + len(after)]
        send_sems, recv_sems = refs[2 * n + len(after):]
        x, y, c = _position()
        sibling = (x, y, 1 - c)
        _handshake([sibling])
        sent = []
        for a in range(n):
            for q in range(N_CHIPS):
                cp = pltpu.make_async_remote_copy(
                    src_ref=ins[a].at[2 * q + (1 - c)], dst_ref=outs[a].at[q],
                    send_sem=send_sems.at[a, q], recv_sem=recv_sems.at[a, q],
                    device_id=sibling, device_id_type=MESH)
                cp.start()
                sent.append(cp)
        for cp in sent:
            cp.wait_recv()
        for cp in sent:
            cp.wait_send()

    return _sequencer_call(
        body, name, collective_id,
        [jax.ShapeDtypeStruct((N_CHIPS,) + p.shape[1:], p.dtype) for p in parts],
        [pltpu.SemaphoreType.DMA((n, N_CHIPS)), pltpu.SemaphoreType.DMA((n, N_CHIPS))],
        list(parts) + list(after))


def _seq_to_chips(partials, name, collective_id, after=()):
    n = len(partials)

    def body(*refs):
        ins, outs = refs[:n], refs[n + len(after):2 * n + len(after)]
        send_sems, recv_sems, local_sems = refs[2 * n + len(after):]
        x, y, c = _position()
        my_chip = 2 * x + y
        chips = [(1 - x, y), (x, 1 - y), (1 - x, 1 - y)]
        _handshake([(*chip, c) for chip in chips])
        mine = [pltpu.make_async_copy(ins[a].at[my_chip], outs[a].at[my_chip], local_sems.at[a]) for a in range(n)]
        for cp in mine:
            cp.start()
        sent = []
        for a in range(n):
            for j, (px, py) in enumerate(chips):
                cp = pltpu.make_async_remote_copy(
                    src_ref=ins[a].at[2 * px + py], dst_ref=outs[a].at[my_chip],
                    send_sem=send_sems.at[a, j], recv_sem=recv_sems.at[a, j],
                    device_id=(px, py, c), device_id_type=MESH)
                cp.start()
                sent.append(cp)
        for cp in sent:
            cp.wait_recv()
        for cp in sent:
            cp.wait_send()
        for cp in mine:
            cp.wait()

    return _sequencer_call(
        body, name, collective_id,
        [jax.ShapeDtypeStruct(p.shape, p.dtype) for p in partials],
        [pltpu.SemaphoreType.DMA((n, 3)), pltpu.SemaphoreType.DMA((n, 3)), pltpu.SemaphoreType.DMA((n,))],
        list(partials) + list(after))


def _pair_add(part, recv, name, after=()):
    _, R, C = part.shape
    pairs = part.reshape(N_CHIPS, 2, R, C)

    def body(p_ref, r_ref, *rest):
        o_ref = rest[len(after)]
        c = lax.axis_index("c")
        o_ref[...] = (p_ref[c].astype(F32) + r_ref[...].astype(F32)).astype(o_ref.dtype)

    return _pallas_call(
        body, name=name, grid=(N_CHIPS,),
        in_specs=[pl.BlockSpec((None, 2, R, C), lambda q: (q, 0, 0, 0)),
                  pl.BlockSpec((None, R, C), lambda q: (q, 0, 0))] + [_UNREAD] * len(after),
        out_specs=pl.BlockSpec((None, R, C), lambda q: (q, 0, 0)),
        out_shape=jax.ShapeDtypeStruct((N_CHIPS, R, C), part.dtype),
        compiler_params=_params(32, 1),
    )(pairs, recv, *after)


class _Reduced(NamedTuple):
    partials: list
    reduced: list


def _blocks(g):
    return g.reshape(N_DEV, -1, g.shape[-1])


def _reduce_scatter(parts, tag, ids, after=(), between=None, add_after=()):
    from_sibling = _seq_to_sibling(parts, "rs_sibling_" + tag, ids[0], after)
    mid = between(from_sibling[0]) if between else ()
    partials = [_pair_add(p, r, "rs_add_%s_%d" % (tag, a), add_after)
                for a, (p, r) in enumerate(zip(parts, from_sibling))]
    return _Reduced(partials, _seq_to_chips(partials, "rs_chips_" + tag, ids[1], mid))


_SMALL = ("ffn1_norm", "mix_norm", "conv_b", "conv_ln_g", "conv_ln_b", "forget_b", "out_norm_conv",
          "out_norm_attn", "ffn2_norm", "final_norm")
_PACK_WIDTH = 2 * D_CONV
_SLOT = dict(ffn1_norm=(0, 0), mix_norm=(1, 0), ffn2_norm=(2, 0), final_norm=(3, 0), conv_b=(4, 0),
             conv_ln_g=(4, D_CONV), conv_ln_b=(5, 0), out_norm_conv=(5, D_CONV), out_norm_attn=(6, 0),
             forget_b=(6, D_CONV))
_CONV_ROW0 = 8
_PACK_ROWS = _CONV_ROW0 + CONV_HALO


def _pack_small(small, name):
    arrays = [small[n] for n in _SMALL] + [small["conv_w"]]

    def body(*refs):
        out = refs[-1]
        out[...] = jnp.zeros_like(out)
        for n, ref in zip(_SMALL, refs):
            row, lane = _SLOT[n]
            out[row:row + 1, lane:lane + ref.shape[1]] = ref[...]
        out[_CONV_ROW0:, :D_CONV] = refs[len(_SMALL)][...]

    return _pallas_call(body, name=name, out_shape=jax.ShapeDtypeStruct((_PACK_ROWS, _PACK_WIDTH), F32))(*arrays)


def _adamw_small(gathered, w, m, v, name):
    c1 = 1.0 - ADAM_B1 ** ADAM_STEP
    c2 = 1.0 - ADAM_B2 ** ADAM_STEP
    k = len(_SMALL)

    def body(g_ref, *refs):
        ws, ms, vs = refs[:k], refs[k:2 * k], refs[2 * k:3 * k]
        outs = refs[3 * k:]
        total = g_ref[0]
        for s in range(1, N_DEV):
            total = total + g_ref[s]
        for i, n in enumerate(_SMALL):
            row, lane = _SLOT[n]
            width = ws[i].shape[1]
            g = total[row:row + 1, lane:lane + width]
            mn = ADAM_B1 * ms[i][...] + (1.0 - ADAM_B1) * g
            vn = ADAM_B2 * vs[i][...] + (1.0 - ADAM_B2) * (g * g)
            o_g, o_d, o_m, o_v = outs[4 * i:4 * i + 4]
            o_g[...] = g
            o_m[...] = mn
            o_v[...] = vn
            o_d[...] = -ADAM_LR * ((mn / c1) / (jnp.sqrt(vn / c2) + ADAM_EPS) + ADAM_WD * ws[i][...])
        outs[4 * k][...] = total[_CONV_ROW0:, :D_CONV]

    shapes = []
    for n in _SMALL:
        shapes += [jax.ShapeDtypeStruct(w[n].shape, F32)] * 4
    shapes.append(jax.ShapeDtypeStruct((CONV_HALO, D_CONV), F32))
    res = _pallas_call(body, name=name, out_shape=shapes)(
        gathered, *[w[n] for n in _SMALL], *[m[n] for n in _SMALL], *[v[n] for n in _SMALL])
    return {n: res[4 * i:4 * i + 4] for i, n in enumerate(_SMALL)}, res[4 * k]


def _local_step(x, target, norms, w13_1, w2_1, winp, conv_w32, wout, w13_2, w2_2):
    x1, G1, U1 = _ffn_fwd(x, norms["ffn1_norm"], w13_1, w2_1, "ffn1_fwd")
    ag, k, v, qT, kT, vT, fl = _inproj_fwd(x1, norms["mix_norm"], winp, "inproj_fwd")
    cum, cumT = _forget_fwd(fl, norms["forget_b"], "forget_fwd")
    yc, c = _conv_fwd(ag, conv_w32, norms["conv_b"], norms["conv_ln_g"], norms["conv_ln_b"], "conv_fwd")
    o, lseT = _attn_fwd(qT, k, vT, cum, cumT, "attn_fwd")
    x2 = _outproj_fwd(x1, c, o, norms["out_norm_conv"], norms["out_norm_attn"], wout, "outproj_fwd")
    x3, G2, U2 = _ffn_fwd(x2, norms["ffn2_norm"], w13_2, w2_2, "ffn2_fwd")
    loss, dx3, d_final = _loss_head(x3, norms["final_norm"], target, "loss_head")

    dx2, dw13_2, dw2_2, d_ffn2n = _ffn_bwd(x2, norms["ffn2_norm"], dx3, G2, U2, w13_2, w2_2, "ffn2_bwd")
    dc, dobT, deltaT, dwout, d_onc, d_ona = _outproj_bwd(
        dx2, c, o, norms["out_norm_conv"], norms["out_norm_attn"], wout, "outproj_bwd")
    red_ffn2 = _reduce_scatter([_blocks(dw13_2), _blocks(dw2_2)], "ffn2", (3, 4), add_after=(dc,))
    dqT, dkT, dvT, dcum = _attn_bwd(qT, k, kT, v, dobT, lseT, deltaT, cum, cumT, "attn_bwd",
                                    after=red_ffn2.partials)
    dfl, d_fb = _forget_bwd(dcum, fl, norms["forget_b"], "forget_bwd")
    dag, d_convw, d_cb, d_lg, d_lb = _conv_bwd(dc, yc, ag, conv_w32, norms["conv_ln_g"], norms["conv_ln_b"], "conv_bwd")
    dx1, d_mixn, h2 = _inproj_bwd_act(x1, norms["mix_norm"], dx2, dag, dqT, dkT, dvT, dfl, winp, "inproj_bwd_act")
    dwinp = _inproj_bwd_weights(h2, dag, dqT, dkT, dvT, dfl, "inproj_bwd_weights")
    dwin_blocks = dwinp[:N_IN].reshape(N_DEV, N_IN // N_DEV, -1)
    red_mix = _reduce_scatter([dwin_blocks, _blocks(dwout)], "mix", (5, 6), after=red_ffn2.reduced[:1])
    dx, dw13_1, dw2_1, d_ffn1n = _ffn_bwd(x, norms["ffn1_norm"], dx1, G1, U1, w13_1, w2_1, "ffn1_bwd",
                                          after=red_mix.partials)
    small = dict(ffn1_norm=d_ffn1n, mix_norm=d_mixn, conv_b=d_cb, conv_ln_g=d_lg, conv_ln_b=d_lb,
                 forget_b=d_fb, out_norm_conv=d_onc, out_norm_attn=d_ona, ffn2_norm=d_ffn2n,
                 final_norm=d_final, conv_w=d_convw)
    packed_small = _pack_small(small, "pack_small_grads")
    gathered_small = []

    def gather_small(behind):
        gathered_small.extend(_seq_all_gather([packed_small], "gather_small_grads", 9, after=behind))
        return gathered_small

    red_ffn1 = _reduce_scatter([_blocks(dw13_1), _blocks(dw2_1)], "ffn1", (7, 8), after=red_mix.reduced[:1],
                               between=gather_small)
    big = dict(ffn1_w13=red_ffn1.reduced[0], ffn1_w2=red_ffn1.reduced[1], w_in=red_mix.reduced[0],
               w_out=red_mix.reduced[1], ffn2_w13=red_ffn2.reduced[0], ffn2_w2=red_ffn2.reduced[1])
    return loss[0, 0], dx, gathered_small[0], big


_BIG = ("ffn1_w13", "ffn1_w2", "w_in", "w_out", "ffn2_w13", "ffn2_w2")
_TRANSPOSED = ("ffn1_w13", "ffn2_w13", "w_in")
_ORDER = ("ffn1_norm", "ffn1_w13", "ffn1_w2", "mix_norm", "w_in", "conv_w", "conv_b", "conv_ln_g", "conv_ln_b",
          "forget_b", "out_norm_conv", "out_norm_attn", "w_out", "ffn2_norm", "ffn2_w13", "ffn2_w2", "final_norm")


def kernel(x, ffn1_norm, ffn1_w13, ffn1_w2, mix_norm, w_in, conv_w, conv_b, conv_ln_g, conv_ln_b, forget_b, out_norm_conv, out_norm_attn, w_out, ffn2_norm, ffn2_w13, ffn2_w2, final_norm, loss_target, m_ffn1_norm, m_ffn1_w13, m_ffn1_w2, m_mix_norm, m_w_in, m_conv_w, m_conv_b, m_conv_ln_g, m_conv_ln_b, m_forget_b, m_out_norm_conv, m_out_norm_attn, m_w_out, m_ffn2_norm, m_ffn2_w13, m_ffn2_w2, m_final_norm, v_ffn1_norm, v_ffn1_w13, v_ffn1_w2, v_mix_norm, v_w_in, v_conv_w, v_conv_b, v_conv_ln_g, v_conv_ln_b, v_forget_b, v_out_norm_conv, v_out_norm_attn, v_w_out, v_ffn2_norm, v_ffn2_w13, v_ffn2_w2, v_final_norm):
    w = dict(ffn1_norm=ffn1_norm, ffn1_w13=ffn1_w13, ffn1_w2=ffn1_w2, mix_norm=mix_norm, w_in=w_in, conv_w=conv_w,
             conv_b=conv_b, conv_ln_g=conv_ln_g, conv_ln_b=conv_ln_b, forget_b=forget_b, out_norm_conv=out_norm_conv,
             out_norm_attn=out_norm_attn, w_out=w_out, ffn2_norm=ffn2_norm, ffn2_w13=ffn2_w13, ffn2_w2=ffn2_w2,
             final_norm=final_norm)
    m = dict(ffn1_norm=m_ffn1_norm, ffn1_w13=m_ffn1_w13, ffn1_w2=m_ffn1_w2, mix_norm=m_mix_norm, w_in=m_w_in,
             conv_w=m_conv_w, conv_b=m_conv_b, conv_ln_g=m_conv_ln_g, conv_ln_b=m_conv_ln_b, forget_b=m_forget_b,
             out_norm_conv=m_out_norm_conv, out_norm_attn=m_out_norm_attn, w_out=m_w_out, ffn2_norm=m_ffn2_norm,
             ffn2_w13=m_ffn2_w13, ffn2_w2=m_ffn2_w2, final_norm=m_final_norm)
    v = dict(ffn1_norm=v_ffn1_norm, ffn1_w13=v_ffn1_w13, ffn1_w2=v_ffn1_w2, mix_norm=v_mix_norm, w_in=v_w_in,
             conv_w=v_conv_w, conv_b=v_conv_b, conv_ln_g=v_conv_ln_g, conv_ln_b=v_conv_ln_b, forget_b=v_forget_b,
             out_norm_conv=v_out_norm_conv, out_norm_attn=v_out_norm_attn, w_out=v_w_out, ffn2_norm=v_ffn2_norm,
             ffn2_w13=v_ffn2_w13, ffn2_w2=v_ffn2_w2, final_norm=v_final_norm)
    shapes = {n: a.shape for n, a in w.items()}
    T, D = x.shape[1], x.shape[2]
    def two(n, a):
        if a.ndim != 3:
            return a.reshape(1, -1)
        a = a.reshape(a.shape[-2], a.shape[-1])
        return a.T if n in _TRANSPOSED else a

    w2d = {n: two(n, a) for n, a in w.items()}
    m2d = {n: two(n, a) for n, a in m.items()}
    v2d = {n: two(n, a) for n, a in v.items()}

    cast = lambda n: w2d[n].astype(MXU)
    g13_1, g2_1 = _all_gather([cast("ffn1_w13"), cast("ffn1_w2")], "gather_ffn1")
    gin, gout, gconv = _seq_all_gather([cast("w_in"), cast("w_out"), w2d["conv_w"]], "gather_mix", 1,
                                       after=g13_1)
    g13_2, g2_2 = _seq_all_gather([cast("ffn2_w13"), cast("ffn2_w2")], "gather_ffn2", 2, after=g2_1)
    bf = g13_1.shape[1]
    J = N_DEV // 2
    w13_1 = g13_1.reshape(2, J, bf, D)
    w13_2 = g13_2.reshape(2, J, bf, D)
    w2_1 = g2_1.reshape(-1, D)
    w2_2 = g2_2.reshape(-1, D)
    wout = gout.reshape(-1, D)
    winp = jnp.pad(gin.reshape(N_IN, D), ((0, N_IN_PAD - N_IN), (0, 0)))
    conv_full = gconv.transpose(1, 0, 2).reshape(CONV_TAPS, D_CONV)
    conv_w32 = jnp.pad(conv_full, ((0, CONV_HALO - CONV_TAPS), (0, 0)))

    norms = {n: w2d[n] for n in _SMALL}
    norms["forget_b"] = jnp.pad(w2d["forget_b"], ((0, 0), (0, LANES - N_HEADS)))
    loss_part, dx, gathered_small, big = _local_step(x[0], loss_target[0], norms, w13_1, w2_1, winp, conv_w32,
                                                     wout, w13_2, w2_2)
    loss = lax.psum(loss_part, ("x", "y", "c"))

    grads, deltas, new_m, new_v = {}, {}, {}, {}
    for n in _BIG:
        g, d, nm, nv = _adamw(w2d[n], m2d[n], v2d[n], big[n], "adamw_" + n)
        grads[n], deltas[n], new_m[n], new_v[n] = g, d, nm, nv

    small_out, conv_g_full = _adamw_small(gathered_small, w2d, m2d, v2d, "adamw_small")
    for n in _SMALL:
        grads[n], deltas[n], new_m[n], new_v[n] = small_out[n]
    conv_g_full = conv_g_full[:CONV_TAPS]
    xi, yi, ci = _position()
    cw = shapes["conv_w"][-1]
    conv_g_mine = lax.dynamic_slice_in_dim(conv_g_full, _flat(xi, yi, ci) * cw, cw, axis=1)
    g, d, nm, nv = _adamw(w2d["conv_w"], m2d["conv_w"], v2d["conv_w"], conv_g_mine[None], "adamw_conv_w")
    grads["conv_w"], deltas["conv_w"], new_m["conv_w"], new_v["conv_w"] = g, d, nm, nv

    shaped = lambda dct: [(dct[n].T if n in _TRANSPOSED else dct[n]).reshape(shapes[n]) for n in _ORDER]
    return (loss, dx[None], *shaped(grads), *shaped(deltas), *shaped(new_m), *shaped(new_v))
```

```python
from typing import NamedTuple

import jax
import jax.numpy as jnp
from jax import lax
from jax.experimental import pallas as pl
from jax.experimental.pallas import tpu as pltpu
from jax.experimental.pallas import tpu_sc as plsc

F32 = jnp.float32
MXU = jnp.bfloat16
EPS = 1e-6
N_HEADS = 8
HEAD_DIM = 64
D_CONV = 512
D_ATTN = N_HEADS * HEAD_DIM
CONV_TAPS = 31
CONV_HALO = 32
SCALE = HEAD_DIM ** -0.5
NEG = -1e30
LANES = 128
N_DEV = 8
N_CHIPS = N_DEV // 2
MESH = pl.DeviceIdType.MESH
MIB = 1 << 20

ADAM_LR = 0.001
ADAM_B1 = 0.9
ADAM_B2 = 0.999
ADAM_EPS = 1e-08
ADAM_WD = 0.01
ADAM_STEP = 10


_UNREAD = pl.BlockSpec(memory_space=pl.ANY)


def _pallas_call(body, *, out_shape, **kwargs):
    in_hbm = lambda s: pltpu.HBM(s.shape, s.dtype)
    outs = [in_hbm(s) for s in out_shape] if isinstance(out_shape, (list, tuple)) else in_hbm(out_shape)
    call = pl.pallas_call(body, out_shape=outs, **kwargs)
    return lambda *operands: call(*[pltpu.with_memory_space_constraint(a, pltpu.HBM) for a in operands])


def _params(vmem_mib, n_axes):
    return pltpu.CompilerParams(dimension_semantics=("arbitrary",) * n_axes, vmem_limit_bytes=vmem_mib * MIB)


def _mm(a, b):
    return jnp.dot(a, b, preferred_element_type=F32)


def _mm_nt(a, b):
    return lax.dot_general(a, b, (((1,), (1,)), ((), ())), preferred_element_type=F32)


def _mm_tn(a, b):
    return lax.dot_general(a, b, (((0,), (0,)), ((), ())), preferred_element_type=F32)


def _rms_fwd(x, g):
    r = lax.rsqrt(jnp.mean(x * x, axis=-1, keepdims=True) + EPS)
    return x * r * g, r


def _rms_bwd(x, r, g, dy):
    gdy = dy * g
    dx = r * gdy - x * (r * r * r) * jnp.mean(x * gdy, axis=-1, keepdims=True)
    dg = jnp.sum(dy * x * r, axis=0, keepdims=True)
    return dx, dg


def _silu_grad(z, sz):
    return sz * (1.0 + z * (1.0 - sz))


def _three_terms(x):
    x1 = x.astype(jnp.bfloat16)
    r1 = x - x1.astype(F32)
    x2 = r1.astype(jnp.bfloat16)
    x3 = (r1 - x2.astype(F32)).astype(jnp.bfloat16)
    return x1, x2, x3


def _exact_tri_dot(tri, x):
    x1, x2, x3 = _three_terms(x)
    return _mm(tri, x1) + _mm(tri, x2) + _mm(tri, x3)


def _exact_dot_01(x, sel):
    x1, x2, x3 = _three_terms(x)
    return _mm(x1, sel) + _mm(x2, sel) + _mm(x3, sel)


def _tile(n, want):
    t = min(n, want)
    assert n % t == 0
    return t


_FFN_CHUNK = 256


def _ffn_up(x, g, w13, name):
    T, D = x.shape
    _, J, bf, _ = w13.shape
    tm = _tile(T, 512)

    def body(x_ref, g_ref, w13_ref, G_ref, U_ref, A_ref):
        h, _ = _rms_fwd(x_ref[...], g_ref[...])
        hb = h.astype(MXU)
        chunks = [slice(r0, r0 + _FFN_CHUNK) for r0 in range(0, tm, _FFN_CHUNK)]
        GU = [(_mm_nt(hb[rs, :], w13_ref[0]), _mm_nt(hb[rs, :], w13_ref[1])) for rs in chunks]
        for rs, (G, U) in zip(chunks, GU):
            G_ref[rs, :] = G.astype(MXU)
            U_ref[rs, :] = U.astype(MXU)
            A_ref[rs, :] = (G * jax.nn.sigmoid(G) * U).astype(MXU)

    blk = pl.BlockSpec((None, tm, bf), lambda j, i: (j, i, 0))
    hid = jax.ShapeDtypeStruct((J, T, bf), MXU)
    return _pallas_call(
        body, name=name, grid=(J, T // tm),
        in_specs=[pl.BlockSpec((tm, D), lambda j, i: (i, 0)), pl.BlockSpec((1, D), lambda j, i: (0, 0)),
                  pl.BlockSpec((2, None, bf, D), lambda j, i: (0, j, 0, 0))],
        out_specs=[blk, blk, blk],
        out_shape=[hid, hid, hid],
        compiler_params=_params(40, 2),
    )(x, g, w13)


def _ffn_down(x, A, w2, name):
    T, D = x.shape
    J, _, bf = A.shape
    tm = _tile(T, 512)

    def body(x_ref, A_ref, w2_ref, xo_ref):
        j = pl.program_id(1)

        @pl.when(j == 0)
        def _():
            xo_ref[...] = x_ref[...]

        xo_ref[...] += 0.5 * _mm(A_ref[...], w2_ref[...])

    row = pl.BlockSpec((tm, D), lambda i, j: (i, 0))
    return _pallas_call(
        body, name=name, grid=(T // tm, J),
        in_specs=[row, pl.BlockSpec((None, tm, bf), lambda i, j: (j, i, 0)), pl.BlockSpec((bf, D), lambda i, j: (j, 0))],
        out_specs=row,
        out_shape=jax.ShapeDtypeStruct((T, D), F32),
        compiler_params=_params(40, 2),
    )(x, A, w2)


def _ffn_bwd_act(x, g, dy, Gs, Us, w13, w2, name, after=()):
    T, D = x.shape
    _, J, bf, _ = w13.shape
    tm = _tile(T, 512)

    def body(x_ref, g_ref, dy_ref, G_ref, U_ref, w13_ref, w2_ref, *rest):
        dx_ref, dg_ref, h_ref, dG_ref, dU_ref, dh_s, dF_s = rest[len(after):]
        i = pl.program_id(0)
        j = pl.program_id(1)

        @pl.when(j == 0)
        def _():
            h, _ = _rms_fwd(x_ref[...], g_ref[...])
            h_ref[...] = h.astype(MXU)
            dF_s[...] = (0.5 * dy_ref[...]).astype(MXU)
            dh_s[...] = jnp.zeros_like(dh_s)

        chunks = [slice(r0, r0 + _FFN_CHUNK) for r0 in range(0, tm, _FFN_CHUNK)]
        dAs = [_mm_nt(dF_s[rs, :], w2_ref[...]) for rs in chunks]
        for rs, dA in zip(chunks, dAs):
            G = G_ref[rs, :].astype(F32)
            U = U_ref[rs, :].astype(F32)
            sg = jax.nn.sigmoid(G)
            s = G * sg
            dUb = (dA * s).astype(MXU)
            dGb = (dA * U * _silu_grad(G, sg)).astype(MXU)
            dG_ref[rs, :] = dGb
            dU_ref[rs, :] = dUb
            dh_s[rs, :] += _mm(dGb, w13_ref[0]) + _mm(dUb, w13_ref[1])

        @pl.when(j == J - 1)
        def _():
            xv = x_ref[...]
            gv = g_ref[...]
            _, r = _rms_fwd(xv, gv)
            dxn, dgp = _rms_bwd(xv, r, gv, dh_s[...])
            dx_ref[...] = dy_ref[...] + dxn

            @pl.when(i == 0)
            def _():
                dg_ref[...] = dgp

            @pl.when(i > 0)
            def _():
                dg_ref[...] += dgp

    row = pl.BlockSpec((tm, D), lambda i, j: (i, 0))
    blk = pl.BlockSpec((None, tm, bf), lambda i, j: (j, i, 0))
    act = jax.ShapeDtypeStruct((T, D), MXU)
    hid = jax.ShapeDtypeStruct((J, T, bf), MXU)
    return _pallas_call(
        body, name=name, grid=(T // tm, J),
        in_specs=[row, pl.BlockSpec((1, D), lambda i, j: (0, 0)), row, blk, blk,
                  pl.BlockSpec((2, None, bf, D), lambda i, j: (0, j, 0, 0)),
                  pl.BlockSpec((bf, D), lambda i, j: (j, 0))] + [_UNREAD] * len(after),
        out_specs=[row, pl.BlockSpec((1, D), lambda i, j: (0, 0)), row, blk, blk],
        out_shape=[jax.ShapeDtypeStruct((T, D), F32), jax.ShapeDtypeStruct((1, D), F32), act, hid, hid],
        scratch_shapes=[pltpu.VMEM((tm, D), F32), pltpu.VMEM((tm, D), MXU)],
        compiler_params=_params(56, 2),
    )(x, g, dy, Gs, Us, w13, w2, *after)


def _ffn_w13_grad(h, dG, dU, name):
    T, D = h.shape
    J, _, bf = dG.shape

    def body(h_ref, dG_ref, dU_ref, dw13_ref):
        dw13_ref[0] = _mm_tn(dG_ref[...], h_ref[...]).astype(dw13_ref.dtype)
        dw13_ref[1] = _mm_tn(dU_ref[...], h_ref[...]).astype(dw13_ref.dtype)

    blk = pl.BlockSpec((None, T, bf), lambda j: (j, 0, 0))
    return _pallas_call(
        body, name=name, grid=(J,),
        in_specs=[pl.BlockSpec((T, D), lambda j: (0, 0)), blk, blk],
        out_specs=pl.BlockSpec((2, None, bf, D), lambda j: (0, j, 0, 0)),
        out_shape=jax.ShapeDtypeStruct((2, J, bf, D), MXU),
        compiler_params=_params(48, 1),
    )(h, dG, dU)


def _ffn_w2_grad(dy, A, name, after=()):
    T, D = dy.shape
    J, _, bf = A.shape

    def body(dy_ref, A_ref, *rest):
        dw2_ref, dF_s = rest[len(after):]

        @pl.when(pl.program_id(0) == 0)
        def _():
            dF_s[...] = (0.5 * dy_ref[...]).astype(MXU)

        dw2_ref[...] = _mm_tn(A_ref[...], dF_s[...]).astype(dw2_ref.dtype)

    return _pallas_call(
        body, name=name, grid=(J,),
        in_specs=[pl.BlockSpec((T, D), lambda j: (0, 0)), pl.BlockSpec((None, T, bf), lambda j: (j, 0, 0))]
        + [_UNREAD] * len(after),
        out_specs=pl.BlockSpec((bf, D), lambda j: (j, 0)),
        out_shape=jax.ShapeDtypeStruct((J * bf, D), MXU),
        scratch_shapes=[pltpu.VMEM((T, D), MXU)],
        compiler_params=_params(48, 1),
    )(dy, A, *after)


_AG0, _Q0, _K0, _V0, _F0 = 0, 2 * D_CONV, 2 * D_CONV + D_ATTN, 2 * D_CONV + 2 * D_ATTN, 2 * D_CONV + 3 * D_ATTN
N_IN = _F0 + N_HEADS
N_IN_PAD = _F0 + LANES


def _inproj_fwd(x1, gm, winp, name):
    T, D = x1.shape
    tm = _tile(T, 256)

    def body(x_ref, g_ref, w_ref, ag_ref, k_ref, v_ref, qT_ref, kT_ref, vT_ref, fl_ref):
        h, _ = _rms_fwd(x_ref[...], g_ref[...])
        hb = h.astype(MXU)
        ag_ref[...] = _mm_nt(hb, w_ref[_AG0:_Q0, :])
        for c0, ref, refT in ((_Q0, None, qT_ref), (_K0, k_ref, kT_ref), (_V0, v_ref, vT_ref)):
            y = _mm_nt(hb, w_ref[c0:c0 + D_ATTN, :])
            if ref is not None:
                ref[...] = y.astype(MXU)
            refT[...] = y.T.astype(MXU)
        fl_ref[...] = _mm_nt(hb, w_ref[_F0:N_IN_PAD, :])

    row = lambda w: pl.BlockSpec((tm, w), lambda i: (i, 0))
    col = pl.BlockSpec((D_ATTN, tm), lambda i: (0, i))
    std = jax.ShapeDtypeStruct((T, D_ATTN), MXU)
    trn = jax.ShapeDtypeStruct((D_ATTN, T), MXU)
    return _pallas_call(
        body, name=name, grid=(T // tm,),
        in_specs=[row(D), pl.BlockSpec((1, D), lambda i: (0, 0)), pl.BlockSpec((N_IN_PAD, D), lambda i: (0, 0))],
        out_specs=[row(2 * D_CONV), row(D_ATTN), row(D_ATTN), col, col, col, row(LANES)],
        out_shape=[jax.ShapeDtypeStruct((T, 2 * D_CONV), F32), std, std, trn, trn, trn,
                   jax.ShapeDtypeStruct((T, LANES), F32)],
        compiler_params=_params(40, 1),
    )(x1, gm, winp)


def _inproj_bwd_act(x1, gm, dx2, dag, dqT, dkT, dvT, dfl, winp, name):
    T, D = x1.shape
    tm = _tile(T, 256)

    def body(x_ref, g_ref, dx2_ref, dag_ref, dqT_ref, dkT_ref, dvT_ref, dfl_ref, w_ref, dx1_ref, dg_ref, h_ref):
        i = pl.program_id(0)
        xv = x_ref[...]
        gv = g_ref[...]
        h, r = _rms_fwd(xv, gv)
        h_ref[...] = h.astype(MXU)
        dh = _mm(dag_ref[...], w_ref[_AG0:_Q0, :])
        for c0, ref in ((_Q0, dqT_ref), (_K0, dkT_ref), (_V0, dvT_ref)):
            dh = dh + _mm_tn(ref[...].astype(MXU), w_ref[c0:c0 + D_ATTN, :])
        dh = dh + _mm(dfl_ref[...].astype(MXU), w_ref[_F0:N_IN_PAD, :])
        dxn, dgp = _rms_bwd(xv, r, gv, dh)
        dx1_ref[...] = dx2_ref[...] + dxn

        @pl.when(i == 0)
        def _():
            dg_ref[...] = dgp

        @pl.when(i > 0)
        def _():
            dg_ref[...] += dgp

    row = lambda w: pl.BlockSpec((tm, w), lambda i: (i, 0))
    col = pl.BlockSpec((D_ATTN, tm), lambda i: (0, i))
    full = lambda a, b: pl.BlockSpec((a, b), lambda i: (0, 0))
    return _pallas_call(
        body, name=name, grid=(T // tm,),
        in_specs=[row(D), full(1, D), row(D), row(2 * D_CONV), col, col, col, row(LANES), full(N_IN_PAD, D)],
        out_specs=[row(D), full(1, D), row(D)],
        out_shape=[jax.ShapeDtypeStruct((T, D), F32), jax.ShapeDtypeStruct((1, D), F32),
                   jax.ShapeDtypeStruct((T, D), MXU)],
        compiler_params=_params(40, 1),
    )(x1, gm, dx2, dag, dqT, dkT, dvT, dfl, winp)


def _inproj_bwd_weights(h, dag, dqT, dkT, dvT, dfl, name, after=()):
    T, D = h.shape

    def body(h_ref, dag_ref, dqT_ref, dkT_ref, dvT_ref, dfl_ref, *rest):
        dw_ref = rest[len(after)]
        hb = h_ref[...]
        dw_ref[_AG0:_Q0, :] = _mm_tn(dag_ref[...], hb).astype(dw_ref.dtype)
        for c0, ref in ((_Q0, dqT_ref), (_K0, dkT_ref), (_V0, dvT_ref)):
            dw_ref[c0:c0 + D_ATTN, :] = _mm(ref[...].astype(MXU), hb).astype(dw_ref.dtype)
        dw_ref[_F0:N_IN_PAD, :] = _mm_tn(dfl_ref[...].astype(MXU), hb).astype(dw_ref.dtype)

    vmem = pl.BlockSpec(memory_space=pltpu.VMEM)
    return _pallas_call(
        body, name=name, in_specs=[vmem] * 6 + [_UNREAD] * len(after), out_specs=vmem,
        out_shape=jax.ShapeDtypeStruct((N_IN_PAD, D), MXU),
        compiler_params=pltpu.CompilerParams(vmem_limit_bytes=56 * MIB),
    )(h, dag, dqT, dkT, dvT, dfl, *after)


def _forget_fwd(fl, fbp, name):
    T = fl.shape[0]
    tb = _tile(T, 256)

    def body(fl_ref, fb_ref, cum_ref, cumT_ref):
        ri = lax.broadcasted_iota(jnp.int32, (tb, tb), 0)
        ci = lax.broadcasted_iota(jnp.int32, (tb, tb), 1)
        tri = (ri >= ci).astype(jnp.bfloat16)
        carry = jnp.zeros((1, LANES), F32)
        for b in range(T // tb):
            z = fl_ref[b * tb:(b + 1) * tb, :] + fb_ref[...]
            lf = jnp.minimum(z, 0.0) - jnp.log1p(jnp.exp(-jnp.abs(z)))
            c = _exact_tri_dot(tri, lf) + carry
            cum_ref[b * tb:(b + 1) * tb, :] = c
            carry = c[tb - 1:tb, :]
        cumT_ref[...] = cum_ref[...].T[:N_HEADS, :]

    return _pallas_call(
        body, name=name,
        out_shape=[jax.ShapeDtypeStruct((T, LANES), F32), jax.ShapeDtypeStruct((N_HEADS, T), F32)],
        compiler_params=pltpu.CompilerParams(vmem_limit_bytes=32 * MIB),
    )(fl, fbp)


def _forget_bwd(dcum, fl, fbp, name):
    T = fl.shape[0]
    tb = _tile(T, 256)

    def body(dc_ref, fl_ref, fb_ref, dfl_ref, dfb_ref):
        ri = lax.broadcasted_iota(jnp.int32, (tb, tb), 0)
        ci = lax.broadcasted_iota(jnp.int32, (tb, tb), 1)
        tri = (ri <= ci).astype(jnp.bfloat16)
        carry = jnp.zeros((1, LANES), F32)
        dfb = jnp.zeros((1, LANES), F32)
        for b in reversed(range(T // tb)):
            sl = slice(b * tb, (b + 1) * tb)
            dl = _exact_tri_dot(tri, dc_ref[sl, :]) + carry
            carry = dl[0:1, :]
            z = fl_ref[sl, :] + fb_ref[...]
            dfl = dl * jax.nn.sigmoid(-z)
            dfl_ref[sl, :] = dfl
            dfb = dfb + jnp.sum(dfl, axis=0, keepdims=True)
        dfb_ref[...] = dfb

    return _pallas_call(
        body, name=name,
        out_shape=[jax.ShapeDtypeStruct((T, LANES), F32), jax.ShapeDtypeStruct((1, LANES), F32)],
        compiler_params=pltpu.CompilerParams(vmem_limit_bytes=32 * MIB),
    )(dcum, fl, fbp)


def _causal_keep(i, j, tq, tk):
    key = j * tk + lax.broadcasted_iota(jnp.int32, (tk, tq), 0)
    qry = i * tq + lax.broadcasted_iota(jnp.int32, (tk, tq), 1)
    return key <= qry


def _split_hi_lo(x):
    hi = x.astype(MXU)
    lo = (x - hi.astype(F32)).astype(MXU)
    return hi, lo


def _attn_fwd(qT, k, vT, cum, cumT, name):
    T = k.shape[0]
    tq = _tile(T, 256)
    tk = _tile(tq, 128)
    kpq = tq // tk
    heads = [slice(HEAD_DIM * h, HEAD_DIM * (h + 1)) for h in range(N_HEADS)]

    def body(qT_ref, k_ref, vT_ref, cum_ref, cumT_ref, o_ref, lseT_ref, acc_s, m_s, l_s):
        i = pl.program_id(0)
        acc_s[...] = jnp.zeros_like(acc_s)
        m_s[...] = jnp.full_like(m_s, NEG)
        l_s[...] = jnp.zeros_like(l_s)

        def kblock(j, masked):
            rows = pl.ds(pl.multiple_of(j * tk, tk), tk)
            keep = _causal_keep(i, j, tq, tk) if masked else None
            qk = [_mm(k_ref[rows, hs], qT_ref[hs, :]) for hs in heads]
            for h, hs in enumerate(heads):
                sT = qk[h] * SCALE + (cumT_ref[h:h + 1, :] - cum_ref[rows, h:h + 1])
                if masked:
                    sT = jnp.where(keep, sT, NEG)
                m_old = m_s[h:h + 1, :]
                m_new = jnp.maximum(m_old, jnp.max(sT, axis=0, keepdims=True))
                alpha = jnp.exp(m_old - m_new)
                pT = jnp.exp(sT - m_new)
                l_s[h:h + 1, :] = alpha * l_s[h:h + 1, :] + jnp.sum(pT, axis=0, keepdims=True)
                p_hi, p_lo = _split_hi_lo(pT)
                vh = vT_ref[hs, rows]
                acc_s[hs, :] = alpha * acc_s[hs, :] + (_mm(vh, p_hi) + _mm(vh, p_lo))
                m_s[h:h + 1, :] = m_new

        def unmasked(j, c):
            kblock(j, False)
            return c

        lax.fori_loop(0, kpq * i, unmasked, 0)
        for d in range(kpq):
            kblock(kpq * i + d, True)
        for h, hs in enumerate(heads):
            acc_s[hs, :] = acc_s[hs, :] / l_s[h:h + 1, :]
        o_ref[...] = acc_s[...].T
        lseT_ref[...] = m_s[...] + jnp.log(l_s[...])

    full = lambda a, b: pl.BlockSpec((a, b), lambda i: (0, 0))
    colblk = lambda r: pl.BlockSpec((r, tq), lambda i: (0, i))
    return _pallas_call(
        body, name=name, grid=(T // tq,),
        in_specs=[colblk(D_ATTN), full(T, D_ATTN), full(D_ATTN, T), full(T, LANES), colblk(N_HEADS)],
        out_specs=[pl.BlockSpec((tq, D_ATTN), lambda i: (i, 0)), colblk(N_HEADS)],
        out_shape=[jax.ShapeDtypeStruct((T, D_ATTN), F32), jax.ShapeDtypeStruct((N_HEADS, T), F32)],
        scratch_shapes=[pltpu.VMEM((D_ATTN, tq), F32), pltpu.VMEM((N_HEADS, tq), F32),
                        pltpu.VMEM((N_HEADS, tq), F32)],
        compiler_params=_params(40, 1),
    )(qT, k, vT, cum, cumT)


def _attn_bwd(qT, k, kT, v, doT, lseT, deltaT, cum, cumT, name, after=()):
    T = k.shape[0]
    tq = _tile(T, 256)
    tk = _tile(tq, 128)
    kpq = tq // tk
    heads = [slice(HEAD_DIM * h, HEAD_DIM * (h + 1)) for h in range(N_HEADS)]

    def body(qT_ref, k_ref, kT_ref, v_ref, doT_ref, lseT_ref, dlT_ref, cum_ref, cumT_ref, *rest):
        dq_ref, dk_ref, dv_ref, dcum_ref, dq_s = rest[len(after):]
        i = pl.program_id(0)

        @pl.when(i == 0)
        def _():
            dk_ref[...] = jnp.zeros_like(dk_ref)
            dv_ref[...] = jnp.zeros_like(dv_ref)
            dcum_ref[...] = jnp.zeros_like(dcum_ref)

        dq_s[...] = jnp.zeros_like(dq_s)

        def kblock(j, masked):
            rows = pl.ds(pl.multiple_of(j * tk, tk), tk)
            keep = _causal_keep(i, j, tq, tk) if masked else None
            qk = [_mm(k_ref[rows, hs], qT_ref[hs, :]) for hs in heads]
            dps = [_mm(v_ref[rows, hs], doT_ref[hs, :]) for hs in heads]
            for h, hs in enumerate(heads):
                sT = qk[h] * SCALE + (cumT_ref[h:h + 1, :] - cum_ref[rows, h:h + 1])
                if masked:
                    sT = jnp.where(keep, sT, NEG)
                pT = jnp.exp(sT - lseT_ref[h:h + 1, :])
                dsT = pT * (dps[h] - dlT_ref[h:h + 1, :])
                dcum_ref[rows, h:h + 1] += -jnp.sum(dsT, axis=1, keepdims=True)
                dsb = dsT.astype(MXU)
                dv_ref[hs, rows] += _mm_nt(doT_ref[hs, :], pT.astype(MXU))
                dk_ref[hs, rows] += _mm_nt(qT_ref[hs, :], dsb) * SCALE
                dq_s[hs, :] += _mm(kT_ref[hs, rows], dsb)

        def unmasked(j, c):
            kblock(j, False)
            return c

        lax.fori_loop(0, kpq * i, unmasked, 0)
        for d in range(kpq):
            kblock(kpq * i + d, True)
        dq_ref[...] = (dq_s[...] * SCALE).astype(dq_ref.dtype)

    full = lambda a, b: pl.BlockSpec((a, b), lambda i: (0, 0))
    colblk = lambda r: pl.BlockSpec((r, tq), lambda i: (0, i))
    return _pallas_call(
        body, name=name, grid=(T // tq,),
        in_specs=[colblk(D_ATTN), full(T, D_ATTN), full(D_ATTN, T), full(T, D_ATTN), colblk(D_ATTN),
                  colblk(N_HEADS), colblk(N_HEADS), full(T, LANES), colblk(N_HEADS)] + [_UNREAD] * len(after),
        out_specs=[colblk(D_ATTN), full(D_ATTN, T), full(D_ATTN, T), full(T, LANES)],
        out_shape=[
            jax.ShapeDtypeStruct((D_ATTN, T), MXU),
            jax.ShapeDtypeStruct((D_ATTN, T), F32),
            jax.ShapeDtypeStruct((D_ATTN, T), F32),
            jax.ShapeDtypeStruct((T, LANES), F32),
        ],
        scratch_shapes=[pltpu.VMEM((D_ATTN, tq), F32)],
        compiler_params=_params(48, 1),
    )(qT, k, kT, v, doT, lseT, deltaT, cum, cumT, *after)


_ROWS_PER_CHUNK = 64


def _glu_halo(ag_ref, agh_ref, uext_s, tm, first):
    a = ag_ref[:, :D_CONV]
    sg = jax.nn.sigmoid(ag_ref[:, D_CONV:])
    uh = agh_ref[:, :D_CONV] * jax.nn.sigmoid(agh_ref[:, D_CONV:])
    uext_s[0:CONV_HALO, :] = jnp.where(first, 0.0, uh)
    uext_s[CONV_HALO:CONV_HALO + tm, :] = a * sg
    return a, sg


def _layer_norm_stats(y):
    mu = jnp.mean(y, axis=-1, keepdims=True)
    xc = y - mu
    rs = lax.rsqrt(jnp.mean(xc * xc, axis=-1, keepdims=True) + EPS)
    return xc * rs, rs


def _conv_fwd(ag, w32, cb, lg, lb, name):
    T = ag.shape[0]
    tm = _tile(T, 256)
    rc = _tile(tm, _ROWS_PER_CHUNK)
    hb = tm // CONV_HALO

    def body(ag_ref, agh_ref, w_ref, cb_ref, lg_ref, lb_ref, yc_ref, c_ref, uext_s):
        i = pl.program_id(0)
        _glu_halo(ag_ref, agh_ref, uext_s, tm, i == 0)
        for r0 in range(0, tm, rc):
            acc = jnp.zeros((rc, D_CONV), F32)
            for t in range(CONV_TAPS):
                acc = acc + uext_s[pl.ds(r0 + CONV_HALO - (CONV_TAPS - 1) + t, rc), :] * w_ref[t:t + 1, :]
            y = acc + cb_ref[...]
            yc_ref[r0:r0 + rc, :] = y
            n, _ = _layer_norm_stats(y)
            z = n * lg_ref[...] + lb_ref[...]
            c_ref[r0:r0 + rc, :] = z * jax.nn.sigmoid(z)

    row = lambda w: pl.BlockSpec((tm, w), lambda i: (i, 0))
    full = lambda a, b: pl.BlockSpec((a, b), lambda i: (0, 0))
    return _pallas_call(
        body, name=name, grid=(T // tm,),
        in_specs=[row(2 * D_CONV),
                  pl.BlockSpec((CONV_HALO, 2 * D_CONV), lambda i: (jnp.maximum(i * hb - 1, 0), 0)),
                  full(CONV_HALO, D_CONV), full(1, D_CONV), full(1, D_CONV), full(1, D_CONV)],
        out_specs=[row(D_CONV), row(D_CONV)],
        out_shape=[jax.ShapeDtypeStruct((T, D_CONV), F32), jax.ShapeDtypeStruct((T, D_CONV), F32)],
        scratch_shapes=[pltpu.VMEM((CONV_HALO + tm, D_CONV), F32)],
        compiler_params=_params(32, 1),
    )(ag, ag, w32, cb, lg, lb)


def _conv_bwd(dc, yc, ag, w32, lg, lb, name):
    T = ag.shape[0]
    tm = _tile(T, 256)
    rc = _tile(tm, _ROWS_PER_CHUNK)
    I = T // tm
    hb = tm // CONV_HALO
    n_halo_blocks = T // CONV_HALO

    def body(dc_ref, yc_ref, dch_ref, ych_ref, ag_ref, agh_ref, w_ref, lg_ref, lb_ref,
             dag_ref, dw_ref, dcb_ref, dlg_ref, dlb_ref, uext_s, dext_s):
        i = pl.program_id(0)
        lgv = lg_ref[...]
        lbv = lb_ref[...]

        def ln_bwd(dcv, ycv):
            n, rs = _layer_norm_stats(ycv)
            z = n * lgv + lbv
            dz = dcv * _silu_grad(z, jax.nn.sigmoid(z))
            dn = dz * lgv
            dy = rs * (dn - jnp.mean(dn, axis=-1, keepdims=True) - n * jnp.mean(dn * n, axis=-1, keepdims=True))
            return dy, dz, n

        dy, dz, n = ln_bwd(dc_ref[...], yc_ref[...])
        dyh, _, _ = ln_bwd(dch_ref[...], ych_ref[...])
        dext_s[0:tm, :] = dy
        dext_s[tm:tm + CONV_HALO, :] = jnp.where(i == I - 1, 0.0, dyh)
        a, sg = _glu_halo(ag_ref, agh_ref, uext_s, tm, i == 0)

        @pl.when(i == 0)
        def _():
            dw_ref[...] = jnp.zeros_like(dw_ref)
            dcb_ref[...] = jnp.zeros_like(dcb_ref)
            dlg_ref[...] = jnp.zeros_like(dlg_ref)
            dlb_ref[...] = jnp.zeros_like(dlb_ref)

        dcb_ref[...] += jnp.sum(dy, axis=0, keepdims=True)
        dlg_ref[...] += jnp.sum(dz * n, axis=0, keepdims=True)
        dlb_ref[...] += jnp.sum(dz, axis=0, keepdims=True)
        for t in range(CONV_TAPS):
            u_t = uext_s[pl.ds(CONV_HALO - (CONV_TAPS - 1) + t, tm), :]
            dw_ref[t:t + 1, :] += jnp.sum(dy * u_t, axis=0, keepdims=True)
        for r0 in range(0, tm, rc):
            acc = jnp.zeros((rc, D_CONV), F32)
            for t in range(CONV_TAPS):
                acc = acc + dext_s[pl.ds(r0 + (CONV_TAPS - 1) - t, rc), :] * w_ref[t:t + 1, :]
            a_c = a[r0:r0 + rc, :]
            sg_c = sg[r0:r0 + rc, :]
            dag_ref[r0:r0 + rc, :D_CONV] = (acc * sg_c).astype(dag_ref.dtype)
            dag_ref[r0:r0 + rc, D_CONV:] = (acc * a_c * sg_c * (1.0 - sg_c)).astype(dag_ref.dtype)

    row = lambda w: pl.BlockSpec((tm, w), lambda i: (i, 0))
    full = lambda a, b: pl.BlockSpec((a, b), lambda i: (0, 0))
    nxt = pl.BlockSpec((CONV_HALO, D_CONV), lambda i: (jnp.minimum((i + 1) * hb, n_halo_blocks - 1), 0))
    return _pallas_call(
        body, name=name, grid=(I,),
        in_specs=[row(D_CONV), row(D_CONV), nxt, nxt, row(2 * D_CONV),
                  pl.BlockSpec((CONV_HALO, 2 * D_CONV), lambda i: (jnp.maximum(i * hb - 1, 0), 0)),
                  full(CONV_HALO, D_CONV), full(1, D_CONV), full(1, D_CONV)],
        out_specs=[row(2 * D_CONV), full(CONV_HALO, D_CONV), full(1, D_CONV), full(1, D_CONV), full(1, D_CONV)],
        out_shape=[
            jax.ShapeDtypeStruct((T, 2 * D_CONV), MXU),
            jax.ShapeDtypeStruct((CONV_HALO, D_CONV), F32),
            jax.ShapeDtypeStruct((1, D_CONV), F32),
            jax.ShapeDtypeStruct((1, D_CONV), F32),
            jax.ShapeDtypeStruct((1, D_CONV), F32),
        ],
        scratch_shapes=[pltpu.VMEM((CONV_HALO + tm, D_CONV), F32), pltpu.VMEM((tm + CONV_HALO, D_CONV), F32)],
        compiler_params=_params(32, 1),
    )(dc, yc, dc, yc, ag, ag, w32, lg, lb)


def _outproj_fwd(x1, c, o, gc, ga, wout, name):
    T, D = x1.shape
    tm = _tile(T, 512)

    def body(x_ref, c_ref, o_ref, gc_ref, ga_ref, w_ref, x2_ref):
        yc, _ = _rms_fwd(c_ref[...], gc_ref[...])
        ya, _ = _rms_fwd(o_ref[...], ga_ref[...])
        x2_ref[...] = (x_ref[...] + _mm(yc.astype(MXU), w_ref[:D_CONV, :])
                       + _mm(ya.astype(MXU), w_ref[D_CONV:, :]))

    row = lambda w: pl.BlockSpec((tm, w), lambda i: (i, 0))
    full = lambda a, b: pl.BlockSpec((a, b), lambda i: (0, 0))
    return _pallas_call(
        body, name=name, grid=(T // tm,),
        in_specs=[row(D), row(D_CONV), row(D_ATTN), full(1, D_CONV), full(1, D_ATTN), full(D_CONV + D_ATTN, D)],
        out_specs=row(D),
        out_shape=jax.ShapeDtypeStruct((T, D), F32),
        compiler_params=_params(32, 1),
    )(x1, c, o, gc, ga, wout)


def _outproj_bwd(dx2, c, o, gc, ga, wout, name):
    T, D = dx2.shape
    tm = _tile(T, 256)
    I = T // tm

    def body(dx_ref, c_ref, o_ref, gc_ref, ga_ref, w_ref,
             dc_ref, doT_ref, dlT_ref, dw_ref, dgc_ref, dga_ref, acc_s):
        i = pl.program_id(0)
        dxb = dx_ref[...].astype(MXU)
        cv = c_ref[...]
        ov = o_ref[...]
        yc, rcn = _rms_fwd(cv, gc_ref[...])
        ya, ra = _rms_fwd(ov, ga_ref[...])
        dyc = _mm_nt(dxb, w_ref[:D_CONV, :])
        dya = _mm_nt(dxb, w_ref[D_CONV:, :])
        dwc = _mm_tn(yc.astype(MXU), dxb)
        dwa = _mm_tn(ya.astype(MXU), dxb)
        dcv, dgc = _rms_bwd(cv, rcn, gc_ref[...], dyc)
        dov, dga = _rms_bwd(ov, ra, ga_ref[...], dya)
        dc_ref[...] = dcv
        dob = dov.astype(doT_ref.dtype)
        doT_ref[...] = dov.T.astype(doT_ref.dtype)
        chan = lax.broadcasted_iota(jnp.int32, (D_ATTN, LANES), 0)
        head = lax.broadcasted_iota(jnp.int32, (D_ATTN, LANES), 1)
        in_head = ((chan >= head * HEAD_DIM) & (chan < (head + 1) * HEAD_DIM)).astype(jnp.bfloat16)
        dlT_ref[...] = _exact_dot_01(dob.astype(F32) * ov, in_head).T[:N_HEADS, :]

        @pl.when(i == 0)
        def _():
            acc_s[:D_CONV, :] = dwc
            acc_s[D_CONV:, :] = dwa
            dgc_ref[...] = dgc
            dga_ref[...] = dga

        @pl.when(i > 0)
        def _():
            acc_s[:D_CONV, :] += dwc
            acc_s[D_CONV:, :] += dwa
            dgc_ref[...] += dgc
            dga_ref[...] += dga

        @pl.when(i == I - 1)
        def _():
            dw_ref[...] = acc_s[...].astype(dw_ref.dtype)

    row = lambda w: pl.BlockSpec((tm, w), lambda i: (i, 0))
    full = lambda a, b: pl.BlockSpec((a, b), lambda i: (0, 0))
    return _pallas_call(
        body, name=name, grid=(I,),
        in_specs=[row(D), row(D_CONV), row(D_ATTN), full(1, D_CONV), full(1, D_ATTN), full(D_CONV + D_ATTN, D)],
        out_specs=[row(D_CONV), pl.BlockSpec((D_ATTN, tm), lambda i: (0, i)),
                   pl.BlockSpec((N_HEADS, tm), lambda i: (0, i)),
                   full(D_CONV + D_ATTN, D), full(1, D_CONV), full(1, D_ATTN)],
        out_shape=[
            jax.ShapeDtypeStruct((T, D_CONV), F32),
            jax.ShapeDtypeStruct((D_ATTN, T), MXU),
            jax.ShapeDtypeStruct((N_HEADS, T), F32),
            jax.ShapeDtypeStruct((D_CONV + D_ATTN, D), MXU),
            jax.ShapeDtypeStruct((1, D_CONV), F32),
            jax.ShapeDtypeStruct((1, D_ATTN), F32),
        ],
        scratch_shapes=[pltpu.VMEM((D_CONV + D_ATTN, D), F32)],
        compiler_params=_params(40, 1),
    )(dx2, c, o, gc, ga, wout)


def _loss_head(x3, gf, target, name):
    T, D = x3.shape
    tm = _tile(T, 512)

    def body(x_ref, g_ref, t_ref, loss_ref, dx_ref, dg_ref):
        i = pl.program_id(0)
        xv = x_ref[...]
        gv = g_ref[...]
        out, r = _rms_fwd(xv, gv)
        err = out - t_ref[...]
        part = jnp.full((1, LANES), 0.5 / D, F32) * jnp.sum(err * err)
        dxn, dgp = _rms_bwd(xv, r, gv, err * (1.0 / D))
        dx_ref[...] = dxn

        @pl.when(i == 0)
        def _():
            loss_ref[...] = part
            dg_ref[...] = dgp

        @pl.when(i > 0)
        def _():
            loss_ref[...] += part
            dg_ref[...] += dgp

    row = lambda w: pl.BlockSpec((tm, w), lambda i: (i, 0))
    full = lambda a, b: pl.BlockSpec((a, b), lambda i: (0, 0))
    return _pallas_call(
        body, name=name, grid=(T // tm,),
        in_specs=[row(D), full(1, D), row(D)],
        out_specs=[full(1, LANES), row(D), full(1, D)],
        out_shape=[jax.ShapeDtypeStruct((1, LANES), F32), jax.ShapeDtypeStruct((T, D), F32),
                   jax.ShapeDtypeStruct((1, D), F32)],
        compiler_params=_params(32, 1),
    )(x3, gf, target)


def _row_tile(rows):
    for cand in (256, 176, 128, 64, 32, 16):
        if rows % cand == 0:
            return cand
    return rows


def _adamw(w, m, v, parts, name):
    R, C = w.shape
    P = parts.shape[0]
    tr = _row_tile(R)
    c1 = 1.0 - ADAM_B1 ** ADAM_STEP
    c2 = 1.0 - ADAM_B2 ** ADAM_STEP

    def body(w_ref, m_ref, v_ref, p_ref, g_ref, d_ref, nm_ref, nv_ref):
        g = p_ref[0].astype(F32)
        for s in range(1, P):
            g = g + p_ref[s].astype(F32)
        wv = w_ref[...]
        mn = ADAM_B1 * m_ref[...] + (1.0 - ADAM_B1) * g
        vn = ADAM_B2 * v_ref[...] + (1.0 - ADAM_B2) * (g * g)
        g_ref[...] = g
        nm_ref[...] = mn
        nv_ref[...] = vn
        d_ref[...] = -ADAM_LR * ((mn / c1) / (jnp.sqrt(vn / c2) + ADAM_EPS) + ADAM_WD * wv)

    blk = pl.BlockSpec((tr, C), lambda i: (i, 0))
    out = jax.ShapeDtypeStruct((R, C), F32)
    return _pallas_call(
        body, name=name, grid=(R // tr,),
        in_specs=[blk, blk, blk, pl.BlockSpec((P, tr, C), lambda i: (0, i, 0))],
        out_specs=[blk, blk, blk, blk],
        out_shape=[out, out, out, out],
        compiler_params=_params(32, 1),
    )(w, m, v, parts)


def _position():
    return lax.axis_index("x"), lax.axis_index("y"), lax.axis_index("c")


def _flat(px, py, pc):
    return 4 * px + 2 * py + pc


def _gather_body(ins, outs, send_sems, recv_sems, local_sems, handshake):
    n = len(ins)
    x, y, c = _position()
    me, sibling = (x, y, c), (x, y, 1 - c)
    chips = [(1 - x, y), (x, 1 - y), (1 - x, 1 - y)]
    if handshake:
        _handshake([sibling] + [(*chip, cc) for chip in chips for cc in (c, 1 - c)])

    def copy(a, k, block, to, src=None):
        dst = outs[a].at[_flat(*block)]
        return pltpu.make_async_remote_copy(
            src_ref=dst if src is None else src, dst_ref=dst,
            send_sem=send_sems.at[a, k], recv_sem=recv_sems.at[a, k],
            device_id=to, device_id_type=MESH)

    mine = [pltpu.make_async_copy(ins[a], outs[a].at[_flat(*me)], local_sems.at[a]) for a in range(n)]
    for cp in mine:
        cp.start()
    first = []
    for a in range(n):
        first.append(copy(a, 0, me, sibling, src=ins[a]))
        first += [copy(a, 1 + j, me, (*chip, c), src=ins[a]) for j, chip in enumerate(chips)]
    for cp in first:
        cp.start()
    passed = []
    for a in range(n):
        for j, chip in enumerate(chips):
            copy(a, 1 + j, (*chip, c), me).wait_recv()
            fwd = copy(a, 4 + j, (*chip, c), sibling)
            fwd.start()
            passed.append(fwd)
    for a in range(n):
        copy(a, 0, sibling, me).wait_recv()
        for j, chip in enumerate(chips):
            copy(a, 4 + j, (*chip, 1 - c), me).wait_recv()
    for cp in first + passed:
        cp.wait_send()
    for cp in mine:
        cp.wait()


def _gather_scratch(n):
    return [pltpu.SemaphoreType.DMA((n, 7)), pltpu.SemaphoreType.DMA((n, 7)), pltpu.SemaphoreType.DMA((n,))]


def _all_gather(shards, name):
    n = len(shards)

    def body(*refs):
        _gather_body(refs[:n], refs[n:2 * n], *refs[2 * n:], handshake=False)

    hbm = pl.BlockSpec(memory_space=pltpu.HBM)
    return _pallas_call(
        body, name=name,
        in_specs=[hbm] * n, out_specs=[hbm] * n,
        out_shape=[jax.ShapeDtypeStruct((N_DEV,) + s.shape, s.dtype) for s in shards],
        scratch_shapes=_gather_scratch(n),
    )(*shards)


def _handshake(peers):
    barrier = pltpu.get_barrier_semaphore()
    for peer in peers:
        pl.semaphore_signal(barrier, inc=1, device_id=peer, device_id_type=MESH)
    pl.semaphore_wait(barrier, len(peers))


def _sequencer_call(body, name, collective_id, out_type, scratch_types, operands):
    return pl.kernel(
        body, name=name, out_type=out_type,
        mesh=plsc.ScalarSubcoreMesh(axis_name="sequencer", num_cores=1),
        scratch_types=scratch_types,
        compiler_params=pltpu.CompilerParams(collective_id=collective_id),
    )(*operands)


def _seq_all_gather(shards, name, collective_id, after):
    n = len(shards)

    def body(*refs):
        _gather_body(refs[:n], refs[n + 1:2 * n + 1], *refs[2 * n + 1:], handshake=True)

    return _sequencer_call(
        body, name, collective_id,
        [jax.ShapeDtypeStruct((N_DEV,) + s.shape, s.dtype) for s in shards],
        _gather_scratch(n), list(shards) + [after])


def _seq_to_sibling(parts, name, collective_id, after):
    n = len(parts)

    def body(*refs):
        ins, outs = refs[:n], refs[n + len(after):2 * n + len(after)]
        send_sems, recv_sems = refs[2 * n + len(after):]
        x, y, c = _position()
        sibling = (x, y, 1 - c)
        _handshake([sibling])
        sent = []
        for a in range(n):
            for q in range(N_CHIPS):
                cp = pltpu.make_async_remote_copy(
                    src_ref=ins[a].at[2 * q + (1 - c)], dst_ref=outs[a].at[q],
                    send_sem=send_sems.at[a, q], recv_sem=recv_sems.at[a, q],
                    device_id=sibling, device_id_type=MESH)
                cp.start()
                sent.append(cp)
        for cp in sent:
            cp.wait_recv()
        for cp in sent:
            cp.wait_send()

    return _sequencer_call(
        body, name, collective_id,
        [jax.ShapeDtypeStruct((N_CHIPS,) + p.shape[1:], p.dtype) for p in parts],
        [pltpu.SemaphoreType.DMA((n, N_CHIPS)), pltpu.SemaphoreType.DMA((n, N_CHIPS))],
        list(parts) + list(after))


def _seq_to_chips(partials, name, collective_id, after=()):
    n = len(partials)

    def body(*refs):
        ins, outs = refs[:n], refs[n + len(after):2 * n + len(after)]
        send_sems, recv_sems, local_sems = refs[2 * n + len(after):]
        x, y, c = _position()
        my_chip = 2 * x + y
        chips = [(1 - x, y), (x, 1 - y), (1 - x, 1 - y)]
        _handshake([(*chip, c) for chip in chips])
        mine = [pltpu.make_async_copy(ins[a].at[my_chip], outs[a].at[my_chip], local_sems.at[a]) for a in range(n)]
        for cp in mine:
            cp.start()
        sent = []
        for a in range(n):
            for j, (px, py) in enumerate(chips):
                cp = pltpu.make_async_remote_copy(
                    src_ref=ins[a].at[2 * px + py], dst_ref=outs[a].at[my_chip],
                    send_sem=send_sems.at[a, j], recv_sem=recv_sems.at[a, j],
                    device_id=(px, py, c), device_id_type=MESH)
                cp.start()
                sent.append(cp)
        for cp in sent:
            cp.wait_recv()
        for cp in sent:
            cp.wait_send()
        for cp in mine:
            cp.wait()

    return _sequencer_call(
        body, name, collective_id,
        [jax.ShapeDtypeStruct(p.shape, p.dtype) for p in partials],
        [pltpu.SemaphoreType.DMA((n, 3)), pltpu.SemaphoreType.DMA((n, 3)), pltpu.SemaphoreType.DMA((n,))],
        list(partials) + list(after))


def _pair_add(part, recv, name, after=()):
    _, R, C = part.shape
    pairs = part.reshape(N_CHIPS, 2, R, C)

    def body(p_ref, r_ref, *rest):
        o_ref = rest[len(after)]
        c = lax.axis_index("c")
        o_ref[...] = (p_ref[c].astype(F32) + r_ref[...].astype(F32)).astype(o_ref.dtype)

    return _pallas_call(
        body, name=name, grid=(N_CHIPS,),
        in_specs=[pl.BlockSpec((None, 2, R, C), lambda q: (q, 0, 0, 0)),
                  pl.BlockSpec((None, R, C), lambda q: (q, 0, 0))] + [_UNREAD] * len(after),
        out_specs=pl.BlockSpec((None, R, C), lambda q: (q, 0, 0)),
        out_shape=jax.ShapeDtypeStruct((N_CHIPS, R, C), part.dtype),
        compiler_params=_params(32, 1),
    )(pairs, recv, *after)


class _Reduced(NamedTuple):
    partials: list
    reduced: list


def _blocks(g):
    return g.reshape(N_DEV, -1, g.shape[-1])


def _reduce_scatter(parts, tag, ids, after=(), between=None, add_after=()):
    from_sibling = _seq_to_sibling(parts, "rs_sibling_" + tag, ids[0], after)
    mid = between(from_sibling[0]) if between else ()
    partials = [_pair_add(p, r, "rs_add_%s_%d" % (tag, a), add_after)
                for a, (p, r) in enumerate(zip(parts, from_sibling))]
    return _Reduced(partials, _seq_to_chips(partials, "rs_chips_" + tag, ids[1], mid))


_SMALL = ("ffn1_norm", "mix_norm", "conv_b", "conv_ln_g", "conv_ln_b", "forget_b", "out_norm_conv",
          "out_norm_attn", "ffn2_norm", "final_norm")
_PACK_WIDTH = 2 * D_CONV
_SLOT = dict(ffn1_norm=(0, 0), mix_norm=(1, 0), ffn2_norm=(2, 0), final_norm=(3, 0), conv_b=(4, 0),
             conv_ln_g=(4, D_CONV), conv_ln_b=(5, 0), out_norm_conv=(5, D_CONV), out_norm_attn=(6, 0),
             forget_b=(6, D_CONV))
_CONV_ROW0 = 8
_PACK_ROWS = _CONV_ROW0 + CONV_HALO


def _pack_small(small, name):
    arrays = [small[n] for n in _SMALL] + [small["conv_w"]]

    def body(*refs):
        out = refs[-1]
        out[...] = jnp.zeros_like(out)
        for n, ref in zip(_SMALL, refs):
            row, lane = _SLOT[n]
            out[row:row + 1, lane:lane + ref.shape[1]] = ref[...]
        out[_CONV_ROW0:, :D_CONV] = refs[len(_SMALL)][...]

    return _pallas_call(body, name=name, out_shape=jax.ShapeDtypeStruct((_PACK_ROWS, _PACK_WIDTH), F32))(*arrays)


def _adamw_small(gathered, w, m, v, name):
    c1 = 1.0 - ADAM_B1 ** ADAM_STEP
    c2 = 1.0 - ADAM_B2 ** ADAM_STEP
    k = len(_SMALL)

    def body(g_ref, *refs):
        ws, ms, vs = refs[:k], refs[k:2 * k], refs[2 * k:3 * k]
        outs = refs[3 * k:]
        total = g_ref[0]
        for s in range(1, N_DEV):
            total = total + g_ref[s]
        for i, n in enumerate(_SMALL):
            row, lane = _SLOT[n]
            width = ws[i].shape[1]
            g = total[row:row + 1, lane:lane + width]
            mn = ADAM_B1 * ms[i][...] + (1.0 - ADAM_B1) * g
            vn = ADAM_B2 * vs[i][...] + (1.0 - ADAM_B2) * (g * g)
            o_g, o_d, o_m, o_v = outs[4 * i:4 * i + 4]
            o_g[...] = g
            o_m[...] = mn
            o_v[...] = vn
            o_d[...] = -ADAM_LR * ((mn / c1) / (jnp.sqrt(vn / c2) + ADAM_EPS) + ADAM_WD * ws[i][...])
        outs[4 * k][...] = total[_CONV_ROW0:, :D_CONV]

    shapes = []
    for n in _SMALL:
        shapes += [jax.ShapeDtypeStruct(w[n].shape, F32)] * 4
    shapes.append(jax.ShapeDtypeStruct((CONV_HALO, D_CONV), F32))
    res = _pallas_call(body, name=name, out_shape=shapes)(
        gathered, *[w[n] for n in _SMALL], *[m[n] for n in _SMALL], *[v[n] for n in _SMALL])
    return {n: res[4 * i:4 * i + 4] for i, n in enumerate(_SMALL)}, res[4 * k]


def _local_step(x, target, norms, shard):
    D = x.shape[1]
    J = N_DEV // 2
    as13 = lambda g: g.reshape(2, J, g.shape[1], D)

    (g13_1,) = _all_gather([shard["ffn1_w13"]], "gather_ffn1_w13")
    (g2_1,) = _seq_all_gather([shard["ffn1_w2"]], "gather_ffn1_w2", 10, after=g13_1)
    w13_1 = as13(g13_1)
    G1, U1, A1 = _ffn_up(x, norms["ffn1_norm"], w13_1, "ffn1_up")
    gin, gout, gconv = _seq_all_gather([shard["w_in"], shard["w_out"], shard["conv_w"]], "gather_mix", 1, after=G1)
    w2_1 = g2_1.reshape(-1, D)
    x1 = _ffn_down(x, A1, w2_1, "ffn1_down")
    g13_2, g2_2 = _seq_all_gather([shard["ffn2_w13"], shard["ffn2_w2"]], "gather_ffn2", 2, after=x1)
    winp = jnp.pad(gin.reshape(N_IN, D), ((0, N_IN_PAD - N_IN), (0, 0)))
    wout = gout.reshape(-1, D)
    conv_w32 = jnp.pad(gconv.transpose(1, 0, 2).reshape(CONV_TAPS, D_CONV), ((0, CONV_HALO - CONV_TAPS), (0, 0)))

    ag, k, v, qT, kT, vT, fl = _inproj_fwd(x1, norms["mix_norm"], winp, "inproj_fwd")
    cum, cumT = _forget_fwd(fl, norms["forget_b"], "forget_fwd")
    yc, c = _conv_fwd(ag, conv_w32, norms["conv_b"], norms["conv_ln_g"], norms["conv_ln_b"], "conv_fwd")
    o, lseT = _attn_fwd(qT, k, vT, cum, cumT, "attn_fwd")
    x2 = _outproj_fwd(x1, c, o, norms["out_norm_conv"], norms["out_norm_attn"], wout, "outproj_fwd")
    w13_2, w2_2 = as13(g13_2), g2_2.reshape(-1, D)
    G2, U2, A2 = _ffn_up(x2, norms["ffn2_norm"], w13_2, "ffn2_up")
    x3 = _ffn_down(x2, A2, w2_2, "ffn2_down")
    loss, dx3, d_final = _loss_head(x3, norms["final_norm"], target, "loss_head")

    dw2_2 = _ffn_w2_grad(dx3, A2, "ffn2_w2_grad")
    dx2, d_ffn2n, h3, dG2, dU2 = _ffn_bwd_act(x2, norms["ffn2_norm"], dx3, G2, U2, w13_2, w2_2, "ffn2_bwd_act")
    dw13_2 = _ffn_w13_grad(h3, dG2, dU2, "ffn2_w13_grad")
    dc, dobT, deltaT, dwout, d_onc, d_ona = _outproj_bwd(
        dx2, c, o, norms["out_norm_conv"], norms["out_norm_attn"], wout, "outproj_bwd")
    red_ffn2 = _reduce_scatter([_blocks(dw13_2), _blocks(dw2_2)], "ffn2", (3, 4), add_after=(dc,))
    dqT, dkT, dvT, dcum = _attn_bwd(qT, k, kT, v, dobT, lseT, deltaT, cum, cumT, "attn_bwd",
                                    after=red_ffn2.partials)
    dfl, d_fb = _forget_bwd(dcum, fl, norms["forget_b"], "forget_bwd")
    dag, d_convw, d_cb, d_lg, d_lb = _conv_bwd(dc, yc, ag, conv_w32, norms["conv_ln_g"], norms["conv_ln_b"], "conv_bwd")
    dx1, d_mixn, h2 = _inproj_bwd_act(x1, norms["mix_norm"], dx2, dag, dqT, dkT, dvT, dfl, winp, "inproj_bwd_act")
    dw2_1 = _ffn_w2_grad(dx1, A1, "ffn1_w2_grad")
    red_w2_1 = _reduce_scatter([_blocks(dw2_1)], "ffn1_w2", (11, 12), after=red_ffn2.reduced[:1])
    dwinp = _inproj_bwd_weights(h2, dag, dqT, dkT, dvT, dfl, "inproj_bwd_weights", after=red_w2_1.partials)
    dwin_blocks = dwinp[:N_IN].reshape(N_DEV, N_IN // N_DEV, -1)
    red_mix = _reduce_scatter([dwin_blocks, _blocks(dwout)], "mix", (5, 6), after=red_w2_1.reduced[:1])
    dx, d_ffn1n, h1, dG1, dU1 = _ffn_bwd_act(x, norms["ffn1_norm"], dx1, G1, U1, w13_1, w2_1, "ffn1_bwd_act",
                                             after=red_mix.partials)
    dw13_1 = _ffn_w13_grad(h1, dG1, dU1, "ffn1_w13_grad")

    small = dict(ffn1_norm=d_ffn1n, mix_norm=d_mixn, conv_b=d_cb, conv_ln_g=d_lg, conv_ln_b=d_lb,
                 forget_b=d_fb, out_norm_conv=d_onc, out_norm_attn=d_ona, ffn2_norm=d_ffn2n,
                 final_norm=d_final, conv_w=d_convw)
    packed_small = _pack_small(small, "pack_small_grads")
    gathered_small = []

    def gather_small(behind):
        gathered_small.extend(_seq_all_gather([packed_small], "gather_small_grads", 9, after=behind))
        return gathered_small

    red_w13_1 = _reduce_scatter([_blocks(dw13_1)], "ffn1_w13", (7, 8), after=red_mix.reduced[:1],
                                between=gather_small)
    big = dict(ffn1_w13=red_w13_1.reduced[0], ffn1_w2=red_w2_1.reduced[0], w_in=red_mix.reduced[0],
               w_out=red_mix.reduced[1], ffn2_w13=red_ffn2.reduced[0], ffn2_w2=red_ffn2.reduced[1])
    return loss[0, 0], dx, gathered_small[0], big


_BIG = ("ffn1_w13", "ffn1_w2", "w_in", "w_out", "ffn2_w13", "ffn2_w2")
_TRANSPOSED = ("ffn1_w13", "ffn2_w13", "w_in")
_ORDER = ("ffn1_norm", "ffn1_w13", "ffn1_w2", "mix_norm", "w_in", "conv_w", "conv_b", "conv_ln_g", "conv_ln_b",
          "forget_b", "out_norm_conv", "out_norm_attn", "w_out", "ffn2_norm", "ffn2_w13", "ffn2_w2", "final_norm")


def kernel(x, ffn1_norm, ffn1_w13, ffn1_w2, mix_norm, w_in, conv_w, conv_b, conv_ln_g, conv_ln_b, forget_b, out_norm_conv, out_norm_attn, w_out, ffn2_norm, ffn2_w13, ffn2_w2, final_norm, loss_target, m_ffn1_norm, m_ffn1_w13, m_ffn1_w2, m_mix_norm, m_w_in, m_conv_w, m_conv_b, m_conv_ln_g, m_conv_ln_b, m_forget_b, m_out_norm_conv, m_out_norm_attn, m_w_out, m_ffn2_norm, m_ffn2_w13, m_ffn2_w2, m_final_norm, v_ffn1_norm, v_ffn1_w13, v_ffn1_w2, v_mix_norm, v_w_in, v_conv_w, v_conv_b, v_conv_ln_g, v_conv_ln_b, v_forget_b, v_out_norm_conv, v_out_norm_attn, v_w_out, v_ffn2_norm, v_ffn2_w13, v_ffn2_w2, v_final_norm):
    w = dict(ffn1_norm=ffn1_norm, ffn1_w13=ffn1_w13, ffn1_w2=ffn1_w2, mix_norm=mix_norm, w_in=w_in, conv_w=conv_w,
             conv_b=conv_b, conv_ln_g=conv_ln_g, conv_ln_b=conv_ln_b, forget_b=forget_b, out_norm_conv=out_norm_conv,
             out_norm_attn=out_norm_attn, w_out=w_out, ffn2_norm=ffn2_norm, ffn2_w13=ffn2_w13, ffn2_w2=ffn2_w2,
             final_norm=final_norm)
    m = dict(ffn1_norm=m_ffn1_norm, ffn1_w13=m_ffn1_w13, ffn1_w2=m_ffn1_w2, mix_norm=m_mix_norm, w_in=m_w_in,
             conv_w=m_conv_w, conv_b=m_conv_b, conv_ln_g=m_conv_ln_g, conv_ln_b=m_conv_ln_b, forget_b=m_forget_b,
             out_norm_conv=m_out_norm_conv, out_norm_attn=m_out_norm_attn, w_out=m_w_out, ffn2_norm=m_ffn2_norm,
             ffn2_w13=m_ffn2_w13, ffn2_w2=m_ffn2_w2, final_norm=m_final_norm)
    v = dict(ffn1_norm=v_ffn1_norm, ffn1_w13=v_ffn1_w13, ffn1_w2=v_ffn1_w2, mix_norm=v_mix_norm, w_in=v_w_in,
             conv_w=v_conv_w, conv_b=v_conv_b, conv_ln_g=v_conv_ln_g, conv_ln_b=v_conv_ln_b, forget_b=v_forget_b,
             out_norm_conv=v_out_norm_conv, out_norm_attn=v_out_norm_attn, w_out=v_w_out, ffn2_norm=v_ffn2_norm,
             ffn2_w13=v_ffn2_w13, ffn2_w2=v_ffn2_w2, final_norm=v_final_norm)
    shapes = {n: a.shape for n, a in w.items()}
    T, D = x.shape[1], x.shape[2]
    def two(n, a):
        if a.ndim != 3:
            return a.reshape(1, -1)
        a = a.reshape(a.shape[-2], a.shape[-1])
        return a.T if n in _TRANSPOSED else a

    w2d = {n: two(n, a) for n, a in w.items()}
    m2d = {n: two(n, a) for n, a in m.items()}
    v2d = {n: two(n, a) for n, a in v.items()}

    shard = {n: w2d[n].astype(MXU) for n in _BIG}
    shard["conv_w"] = w2d["conv_w"]
    norms = {n: w2d[n] for n in _SMALL}
    norms["forget_b"] = jnp.pad(w2d["forget_b"], ((0, 0), (0, LANES - N_HEADS)))
    loss_part, dx, gathered_small, big = _local_step(x[0], loss_target[0], norms, shard)
    loss = lax.psum(loss_part, ("x", "y", "c"))

    grads, deltas, new_m, new_v = {}, {}, {}, {}
    for n in _BIG:
        g, d, nm, nv = _adamw(w2d[n], m2d[n], v2d[n], big[n], "adamw_" + n)
        grads[n], deltas[n], new_m[n], new_v[n] = g, d, nm, nv

    small_out, conv_g_full = _adamw_small(gathered_small, w2d, m2d, v2d, "adamw_small")
    for n in _SMALL:
        grads[n], deltas[n], new_m[n], new_v[n] = small_out[n]
    conv_g_full = conv_g_full[:CONV_TAPS]
    xi, yi, ci = _position()
    cw = shapes["conv_w"][-1]
    conv_g_mine = lax.dynamic_slice_in_dim(conv_g_full, _flat(xi, yi, ci) * cw, cw, axis=1)
    g, d, nm, nv = _adamw(w2d["conv_w"], m2d["conv_w"], v2d["conv_w"], conv_g_mine[None], "adamw_conv_w")
    grads["conv_w"], deltas["conv_w"], new_m["conv_w"], new_v["conv_w"] = g, d, nm, nv

    shaped = lambda dct: [(dct[n].T if n in _TRANSPOSED else dct[n]).reshape(shapes[n]) for n in _ORDER]
    return (loss, dx[None], *shaped(grads), *shaped(deltas), *shaped(new_m), *shaped(new_v))
```

```python
from typing import NamedTuple

import jax
import jax.numpy as jnp
from jax import lax
from jax.experimental import pallas as pl
from jax.experimental.pallas import tpu as pltpu
from jax.experimental.pallas import tpu_sc as plsc

F32 = jnp.float32
MXU = jnp.bfloat16
EPS = 1e-6
N_HEADS = 8
HEAD_DIM = 64
D_CONV = 512
D_ATTN = N_HEADS * HEAD_DIM
CONV_TAPS = 31
CONV_HALO = 32
SCALE = HEAD_DIM ** -0.5
NEG = -1e30
LANES = 128
N_DEV = 8
N_CHIPS = N_DEV // 2
MESH = pl.DeviceIdType.MESH
MIB = 1 << 20

ADAM_LR = 0.001
ADAM_B1 = 0.9
ADAM_B2 = 0.999
ADAM_EPS = 1e-08
ADAM_WD = 0.01
ADAM_STEP = 10


_UNREAD = pl.BlockSpec(memory_space=pl.ANY)


def _pallas_call(body, *, out_shape, **kwargs):
    in_hbm = lambda s: pltpu.HBM(s.shape, s.dtype)
    outs = [in_hbm(s) for s in out_shape] if isinstance(out_shape, (list, tuple)) else in_hbm(out_shape)
    call = pl.pallas_call(body, out_shape=outs, **kwargs)
    return lambda *operands: call(*[pltpu.with_memory_space_constraint(a, pltpu.HBM) for a in operands])


def _params(vmem_mib, n_axes):
    return pltpu.CompilerParams(dimension_semantics=("arbitrary",) * n_axes, vmem_limit_bytes=vmem_mib * MIB)


def _mm(a, b):
    return jnp.dot(a, b, preferred_element_type=F32)


def _mm_nt(a, b):
    return lax.dot_general(a, b, (((1,), (1,)), ((), ())), preferred_element_type=F32)


def _mm_tn(a, b):
    return lax.dot_general(a, b, (((0,), (0,)), ((), ())), preferred_element_type=F32)


def _rms_fwd(x, g):
    r = lax.rsqrt(jnp.mean(x * x, axis=-1, keepdims=True) + EPS)
    return x * r * g, r


def _rms_bwd(x, r, g, dy):
    gdy = dy * g
    dx = r * gdy - x * (r * r * r) * jnp.mean(x * gdy, axis=-1, keepdims=True)
    dg = jnp.sum(dy * x * r, axis=0, keepdims=True)
    return dx, dg


def _silu_grad(z, sz):
    return sz * (1.0 + z * (1.0 - sz))


def _three_terms(x):
    x1 = x.astype(jnp.bfloat16)
    r1 = x - x1.astype(F32)
    x2 = r1.astype(jnp.bfloat16)
    x3 = (r1 - x2.astype(F32)).astype(jnp.bfloat16)
    return x1, x2, x3


def _exact_tri_dot(tri, x):
    x1, x2, x3 = _three_terms(x)
    return _mm(tri, x1) + _mm(tri, x2) + _mm(tri, x3)


def _exact_dot_01(x, sel):
    x1, x2, x3 = _three_terms(x)
    return _mm(x1, sel) + _mm(x2, sel) + _mm(x3, sel)


def _tile(n, want):
    t = min(n, want)
    assert n % t == 0
    return t


_FFN_CHUNK = 256


def _ffn_up(x, g, w13, name):
    T, D = x.shape
    _, J, bf, _ = w13.shape
    tm = _tile(T, 512)
    I = T // tm

    def body(x_ref, g_ref, w13_ref, G_ref, U_ref, A_ref, h_s):
        j = pl.program_id(0)
        i = pl.program_id(1)
        rows = pl.ds(pl.multiple_of(i * tm, tm), tm)

        @pl.when(j == 0)
        def _():
            h, _ = _rms_fwd(x_ref[...], g_ref[...])
            h_s[rows, :] = h.astype(MXU)

        chunks = [slice(r0, r0 + _FFN_CHUNK) for r0 in range(0, tm, _FFN_CHUNK)]
        hbs = [h_s[pl.ds(pl.multiple_of(i * tm + rs.start, _FFN_CHUNK), _FFN_CHUNK), :] for rs in chunks]
        GU = [(_mm_nt(hb, w13_ref[0]), _mm_nt(hb, w13_ref[1])) for hb in hbs]
        for rs, (G, U) in zip(chunks, GU):
            G_ref[rs, :] = G.astype(MXU)
            U_ref[rs, :] = U.astype(MXU)
            A_ref[rs, :] = (G * jax.nn.sigmoid(G) * U).astype(MXU)

    blk = pl.BlockSpec((None, tm, bf), lambda j, i: (j, i, 0))
    hid = jax.ShapeDtypeStruct((J, T, bf), MXU)
    return _pallas_call(
        body, name=name, grid=(J, I),
        in_specs=[pl.BlockSpec((tm, D), lambda j, i: (jnp.where(j == 0, i, I - 1), 0)),
                  pl.BlockSpec((1, D), lambda j, i: (0, 0)),
                  pl.BlockSpec((2, None, bf, D), lambda j, i: (0, j, 0, 0))],
        out_specs=[blk, blk, blk],
        out_shape=[hid, hid, hid],
        scratch_shapes=[pltpu.VMEM((T, D), MXU)],
        compiler_params=_params(40, 2),
    )(x, g, w13)


def _ffn_down(x, A, w2, name):
    T, D = x.shape
    J, _, bf = A.shape
    tm = _tile(T, 512)

    def body(x_ref, A_ref, w2_ref, xo_ref):
        f = _mm(A_ref[0], w2_ref[0:bf, :])
        for j in range(1, J):
            f = f + _mm(A_ref[j], w2_ref[j * bf:(j + 1) * bf, :])
        xo_ref[...] = x_ref[...] + 0.5 * f

    row = pl.BlockSpec((tm, D), lambda i: (i, 0))
    return _pallas_call(
        body, name=name, grid=(T // tm,),
        in_specs=[row, pl.BlockSpec((J, tm, bf), lambda i: (0, i, 0)), pl.BlockSpec((J * bf, D), lambda i: (0, 0))],
        out_specs=row,
        out_shape=jax.ShapeDtypeStruct((T, D), F32),
        compiler_params=_params(48, 1),
    )(x, A, w2)


def _ffn_bwd_act(x, g, dy, Gs, Us, w13, w2, name, after=()):
    T, D = x.shape
    _, J, bf, _ = w13.shape
    tm = _tile(T, 512)
    I = T // tm

    def body(x_ref, g_ref, dy_ref, G_ref, U_ref, w13_ref, w2_ref, *rest):
        dx_ref, dg_ref, h_ref, dG_ref, dU_ref, dh_s, dF_s, h_s = rest[len(after):]
        j = pl.program_id(0)
        i = pl.program_id(1)
        rows = pl.ds(pl.multiple_of(i * tm, tm), tm)

        @pl.when(j == 0)
        def _():
            h, _ = _rms_fwd(x_ref[...], g_ref[...])
            hb = h.astype(MXU)
            h_s[rows, :] = hb
            h_ref[...] = hb
            dF_s[rows, :] = (0.5 * dy_ref[...]).astype(MXU)
            dh_s[rows, :] = jnp.zeros((tm, D), F32)

        chunks = [slice(r0, r0 + _FFN_CHUNK) for r0 in range(0, tm, _FFN_CHUNK)]
        crows = [pl.ds(pl.multiple_of(i * tm + rs.start, _FFN_CHUNK), _FFN_CHUNK) for rs in chunks]
        dAs = [_mm_nt(dF_s[cr, :], w2_ref[...]) for cr in crows]
        for rs, cr, dA in zip(chunks, crows, dAs):
            G = G_ref[rs, :].astype(F32)
            U = U_ref[rs, :].astype(F32)
            sg = jax.nn.sigmoid(G)
            s = G * sg
            dUb = (dA * s).astype(MXU)
            dGb = (dA * U * _silu_grad(G, sg)).astype(MXU)
            dG_ref[rs, :] = dGb
            dU_ref[rs, :] = dUb
            dh_s[cr, :] += _mm(dGb, w13_ref[0]) + _mm(dUb, w13_ref[1])

        @pl.when(j == J - 1)
        def _():
            xv = x_ref[...]
            gv = g_ref[...]
            _, r = _rms_fwd(xv, gv)
            dxn, dgp = _rms_bwd(xv, r, gv, dh_s[rows, :])
            dx_ref[...] = dy_ref[...] + dxn

            @pl.when(i == 0)
            def _():
                dg_ref[...] = dgp

            @pl.when(i > 0)
            def _():
                dg_ref[...] += dgp

    ends = lambda j, i: (jnp.where((j == 0) | (j == J - 1), i, I - 1), 0)
    blk = pl.BlockSpec((None, tm, bf), lambda j, i: (j, i, 0))
    hid = jax.ShapeDtypeStruct((J, T, bf), MXU)
    return _pallas_call(
        body, name=name, grid=(J, I),
        in_specs=[pl.BlockSpec((tm, D), ends), pl.BlockSpec((1, D), lambda j, i: (0, 0)), pl.BlockSpec((tm, D), ends),
                  blk, blk, pl.BlockSpec((2, None, bf, D), lambda j, i: (0, j, 0, 0)),
                  pl.BlockSpec((bf, D), lambda j, i: (j, 0))] + [_UNREAD] * len(after),
        out_specs=[pl.BlockSpec((tm, D), lambda j, i: (jnp.where(j == J - 1, i, 0), 0)),
                   pl.BlockSpec((1, D), lambda j, i: (0, 0)),
                   pl.BlockSpec((tm, D), lambda j, i: (jnp.where(j == 0, i, I - 1), 0)), blk, blk],
        out_shape=[jax.ShapeDtypeStruct((T, D), F32), jax.ShapeDtypeStruct((1, D), F32),
                   jax.ShapeDtypeStruct((T, D), MXU), hid, hid],
        scratch_shapes=[pltpu.VMEM((T, D), F32), pltpu.VMEM((T, D), MXU), pltpu.VMEM((T, D), MXU)],
        compiler_params=_params(58, 2),
    )(x, g, dy, Gs, Us, w13, w2, *after)


def _ffn_w13_grad(h, dG, dU, name):
    T, D = h.shape
    J, _, bf = dG.shape

    def body(h_ref, dG_ref, dU_ref, dw13_ref):
        dw13_ref[0] = _mm_tn(dG_ref[...], h_ref[...]).astype(dw13_ref.dtype)
        dw13_ref[1] = _mm_tn(dU_ref[...], h_ref[...]).astype(dw13_ref.dtype)

    blk = pl.BlockSpec((None, T, bf), lambda j: (j, 0, 0))
    return _pallas_call(
        body, name=name, grid=(J,),
        in_specs=[pl.BlockSpec((T, D), lambda j: (0, 0)), blk, blk],
        out_specs=pl.BlockSpec((2, None, bf, D), lambda j: (0, j, 0, 0)),
        out_shape=jax.ShapeDtypeStruct((2, J, bf, D), MXU),
        compiler_params=_params(48, 1),
    )(h, dG, dU)


def _ffn_w2_grad(dy, A, name, after=()):
    T, D = dy.shape
    J, _, bf = A.shape

    def body(dy_ref, A_ref, *rest):
        dw2_ref, dF_s = rest[len(after):]

        @pl.when(pl.program_id(0) == 0)
        def _():
            dF_s[...] = (0.5 * dy_ref[...]).astype(MXU)

        dw2_ref[...] = _mm_tn(A_ref[...], dF_s[...]).astype(dw2_ref.dtype)

    return _pallas_call(
        body, name=name, grid=(J,),
        in_specs=[pl.BlockSpec((T, D), lambda j: (0, 0)), pl.BlockSpec((None, T, bf), lambda j: (j, 0, 0))]
        + [_UNREAD] * len(after),
        out_specs=pl.BlockSpec((bf, D), lambda j: (j, 0)),
        out_shape=jax.ShapeDtypeStruct((J * bf, D), MXU),
        scratch_shapes=[pltpu.VMEM((T, D), MXU)],
        compiler_params=_params(48, 1),
    )(dy, A, *after)


_AG0, _Q0, _K0, _V0, _F0 = 0, 2 * D_CONV, 2 * D_CONV + D_ATTN, 2 * D_CONV + 2 * D_ATTN, 2 * D_CONV + 3 * D_ATTN
N_IN = _F0 + N_HEADS
N_IN_PAD = _F0 + LANES


def _inproj_fwd(x1, gm, winp, name):
    T, D = x1.shape
    tm = _tile(T, 256)

    def body(x_ref, g_ref, w_ref, ag_ref, k_ref, v_ref, qT_ref, kT_ref, vT_ref, fl_ref):
        h, _ = _rms_fwd(x_ref[...], g_ref[...])
        hb = h.astype(MXU)
        ag_ref[...] = _mm_nt(hb, w_ref[_AG0:_Q0, :])
        for c0, ref, refT in ((_Q0, None, qT_ref), (_K0, k_ref, kT_ref), (_V0, v_ref, vT_ref)):
            y = _mm_nt(hb, w_ref[c0:c0 + D_ATTN, :])
            if ref is not None:
                ref[...] = y.astype(MXU)
            refT[...] = y.T.astype(MXU)
        fl_ref[...] = _mm_nt(hb, w_ref[_F0:N_IN_PAD, :])

    row = lambda w: pl.BlockSpec((tm, w), lambda i: (i, 0))
    col = pl.BlockSpec((D_ATTN, tm), lambda i: (0, i))
    std = jax.ShapeDtypeStruct((T, D_ATTN), MXU)
    trn = jax.ShapeDtypeStruct((D_ATTN, T), MXU)
    return _pallas_call(
        body, name=name, grid=(T // tm,),
        in_specs=[row(D), pl.BlockSpec((1, D), lambda i: (0, 0)), pl.BlockSpec((N_IN_PAD, D), lambda i: (0, 0))],
        out_specs=[row(2 * D_CONV), row(D_ATTN), row(D_ATTN), col, col, col, row(LANES)],
        out_shape=[jax.ShapeDtypeStruct((T, 2 * D_CONV), F32), std, std, trn, trn, trn,
                   jax.ShapeDtypeStruct((T, LANES), F32)],
        compiler_params=_params(40, 1),
    )(x1, gm, winp)


def _inproj_bwd_act(x1, gm, dx2, dag, dqT, dkT, dvT, dfl, winp, name):
    T, D = x1.shape
    tm = _tile(T, 256)

    def body(x_ref, g_ref, dx2_ref, dag_ref, dqT_ref, dkT_ref, dvT_ref, dfl_ref, w_ref, dx1_ref, dg_ref, h_ref):
        i = pl.program_id(0)
        xv = x_ref[...]
        gv = g_ref[...]
        h, r = _rms_fwd(xv, gv)
        h_ref[...] = h.astype(MXU)
        dh = _mm(dag_ref[...], w_ref[_AG0:_Q0, :])
        for c0, ref in ((_Q0, dqT_ref), (_K0, dkT_ref), (_V0, dvT_ref)):
            dh = dh + _mm_tn(ref[...].astype(MXU), w_ref[c0:c0 + D_ATTN, :])
        dh = dh + _mm(dfl_ref[...].astype(MXU), w_ref[_F0:N_IN_PAD, :])
        dxn, dgp = _rms_bwd(xv, r, gv, dh)
        dx1_ref[...] = dx2_ref[...] + dxn

        @pl.when(i == 0)
        def _():
            dg_ref[...] = dgp

        @pl.when(i > 0)
        def _():
            dg_ref[...] += dgp

    row = lambda w: pl.BlockSpec((tm, w), lambda i: (i, 0))
    col = pl.BlockSpec((D_ATTN, tm), lambda i: (0, i))
    full = lambda a, b: pl.BlockSpec((a, b), lambda i: (0, 0))
    return _pallas_call(
        body, name=name, grid=(T // tm,),
        in_specs=[row(D), full(1, D), row(D), row(2 * D_CONV), col, col, col, row(LANES), full(N_IN_PAD, D)],
        out_specs=[row(D), full(1, D), row(D)],
        out_shape=[jax.ShapeDtypeStruct((T, D), F32), jax.ShapeDtypeStruct((1, D), F32),
                   jax.ShapeDtypeStruct((T, D), MXU)],
        compiler_params=_params(40, 1),
    )(x1, gm, dx2, dag, dqT, dkT, dvT, dfl, winp)


def _inproj_bwd_weights(h, dag, dqT, dkT, dvT, dfl, name, after=()):
    T, D = h.shape

    def body(h_ref, dag_ref, dqT_ref, dkT_ref, dvT_ref, dfl_ref, *rest):
        dw_ref = rest[len(after)]
        hb = h_ref[...]
        dw_ref[_AG0:_Q0, :] = _mm_tn(dag_ref[...], hb).astype(dw_ref.dtype)
        for c0, ref in ((_Q0, dqT_ref), (_K0, dkT_ref), (_V0, dvT_ref)):
            dw_ref[c0:c0 + D_ATTN, :] = _mm(ref[...].astype(MXU), hb).astype(dw_ref.dtype)
        dw_ref[_F0:N_IN_PAD, :] = _mm_tn(dfl_ref[...].astype(MXU), hb).astype(dw_ref.dtype)

    vmem = pl.BlockSpec(memory_space=pltpu.VMEM)
    return _pallas_call(
        body, name=name, in_specs=[vmem] * 6 + [_UNREAD] * len(after), out_specs=vmem,
        out_shape=jax.ShapeDtypeStruct((N_IN_PAD, D), MXU),
        compiler_params=pltpu.CompilerParams(vmem_limit_bytes=56 * MIB),
    )(h, dag, dqT, dkT, dvT, dfl, *after)


def _forget_fwd(fl, fbp, name):
    T = fl.shape[0]
    tb = _tile(T, 256)

    def body(fl_ref, fb_ref, cum_ref, cumT_ref):
        ri = lax.broadcasted_iota(jnp.int32, (tb, tb), 0)
        ci = lax.broadcasted_iota(jnp.int32, (tb, tb), 1)
        tri = (ri >= ci).astype(jnp.bfloat16)
        carry = jnp.zeros((1, LANES), F32)
        for b in range(T // tb):
            z = fl_ref[b * tb:(b + 1) * tb, :] + fb_ref[...]
            lf = jnp.minimum(z, 0.0) - jnp.log1p(jnp.exp(-jnp.abs(z)))
            c = _exact_tri_dot(tri, lf) + carry
            cum_ref[b * tb:(b + 1) * tb, :] = c
            carry = c[tb - 1:tb, :]
        cumT_ref[...] = cum_ref[...].T[:N_HEADS, :]

    return _pallas_call(
        body, name=name,
        out_shape=[jax.ShapeDtypeStruct((T, LANES), F32), jax.ShapeDtypeStruct((N_HEADS, T), F32)],
        compiler_params=pltpu.CompilerParams(vmem_limit_bytes=32 * MIB),
    )(fl, fbp)


def _forget_bwd(dcum, fl, fbp, name):
    T = fl.shape[0]
    tb = _tile(T, 256)

    def body(dc_ref, fl_ref, fb_ref, dfl_ref, dfb_ref):
        ri = lax.broadcasted_iota(jnp.int32, (tb, tb), 0)
        ci = lax.broadcasted_iota(jnp.int32, (tb, tb), 1)
        tri = (ri <= ci).astype(jnp.bfloat16)
        carry = jnp.zeros((1, LANES), F32)
        dfb = jnp.zeros((1, LANES), F32)
        for b in reversed(range(T // tb)):
            sl = slice(b * tb, (b + 1) * tb)
            dl = _exact_tri_dot(tri, dc_ref[sl, :]) + carry
            carry = dl[0:1, :]
            z = fl_ref[sl, :] + fb_ref[...]
            dfl = dl * jax.nn.sigmoid(-z)
            dfl_ref[sl, :] = dfl
            dfb = dfb + jnp.sum(dfl, axis=0, keepdims=True)
        dfb_ref[...] = dfb

    return _pallas_call(
        body, name=name,
        out_shape=[jax.ShapeDtypeStruct((T, LANES), F32), jax.ShapeDtypeStruct((1, LANES), F32)],
        compiler_params=pltpu.CompilerParams(vmem_limit_bytes=32 * MIB),
    )(dcum, fl, fbp)


def _causal_keep(i, j, tq, tk):
    key = j * tk + lax.broadcasted_iota(jnp.int32, (tk, tq), 0)
    qry = i * tq + lax.broadcasted_iota(jnp.int32, (tk, tq), 1)
    return key <= qry


def _split_hi_lo(x):
    hi = x.astype(MXU)
    lo = (x - hi.astype(F32)).astype(MXU)
    return hi, lo


def _attn_fwd(qT, k, vT, cum, cumT, name):
    T = k.shape[0]
    tq = _tile(T, 256)
    tk = _tile(tq, 128)
    kpq = tq // tk
    heads = [slice(HEAD_DIM * h, HEAD_DIM * (h + 1)) for h in range(N_HEADS)]

    def body(qT_ref, k_ref, vT_ref, cum_ref, cumT_ref, o_ref, lseT_ref, acc_s, m_s, l_s):
        i = pl.program_id(0)
        acc_s[...] = jnp.zeros_like(acc_s)
        m_s[...] = jnp.full_like(m_s, NEG)
        l_s[...] = jnp.zeros_like(l_s)

        def kblock(j, masked):
            rows = pl.ds(pl.multiple_of(j * tk, tk), tk)
            keep = _causal_keep(i, j, tq, tk) if masked else None
            qk = [_mm(k_ref[rows, hs], qT_ref[hs, :]) for hs in heads]
            for h, hs in enumerate(heads):
                sT = qk[h] * SCALE + (cumT_ref[h:h + 1, :] - cum_ref[rows, h:h + 1])
                if masked:
                    sT = jnp.where(keep, sT, NEG)
                m_old = m_s[h:h + 1, :]
                m_new = jnp.maximum(m_old, jnp.max(sT, axis=0, keepdims=True))
                alpha = jnp.exp(m_old - m_new)
                pT = jnp.exp(sT - m_new)
                l_s[h:h + 1, :] = alpha * l_s[h:h + 1, :] + jnp.sum(pT, axis=0, keepdims=True)
                p_hi, p_lo = _split_hi_lo(pT)
                vh = vT_ref[hs, rows]
                acc_s[hs, :] = alpha * acc_s[hs, :] + (_mm(vh, p_hi) + _mm(vh, p_lo))
                m_s[h:h + 1, :] = m_new

        def unmasked(j, c):
            kblock(j, False)
            return c

        lax.fori_loop(0, kpq * i, unmasked, 0)
        for d in range(kpq):
            kblock(kpq * i + d, True)
        for h, hs in enumerate(heads):
            acc_s[hs, :] = acc_s[hs, :] / l_s[h:h + 1, :]
        o_ref[...] = acc_s[...].T
        lseT_ref[...] = m_s[...] + jnp.log(l_s[...])

    full = lambda a, b: pl.BlockSpec((a, b), lambda i: (0, 0))
    colblk = lambda r: pl.BlockSpec((r, tq), lambda i: (0, i))
    return _pallas_call(
        body, name=name, grid=(T // tq,),
        in_specs=[colblk(D_ATTN), full(T, D_ATTN), full(D_ATTN, T), full(T, LANES), colblk(N_HEADS)],
        out_specs=[pl.BlockSpec((tq, D_ATTN), lambda i: (i, 0)), colblk(N_HEADS)],
        out_shape=[jax.ShapeDtypeStruct((T, D_ATTN), F32), jax.ShapeDtypeStruct((N_HEADS, T), F32)],
        scratch_shapes=[pltpu.VMEM((D_ATTN, tq), F32), pltpu.VMEM((N_HEADS, tq), F32),
                        pltpu.VMEM((N_HEADS, tq), F32)],
        compiler_params=_params(40, 1),
    )(qT, k, vT, cum, cumT)


def _attn_bwd(qT, k, kT, v, doT, lseT, deltaT, cum, cumT, name, after=()):
    T = k.shape[0]
    tq = _tile(T, 256)
    tk = _tile(tq, 128)
    kpq = tq // tk
    heads = [slice(HEAD_DIM * h, HEAD_DIM * (h + 1)) for h in range(N_HEADS)]

    def body(qT_ref, k_ref, kT_ref, v_ref, doT_ref, lseT_ref, dlT_ref, cum_ref, cumT_ref, *rest):
        dq_ref, dk_ref, dv_ref, dcum_ref, dq_s = rest[len(after):]
        i = pl.program_id(0)

        @pl.when(i == 0)
        def _():
            dk_ref[...] = jnp.zeros_like(dk_ref)
            dv_ref[...] = jnp.zeros_like(dv_ref)
            dcum_ref[...] = jnp.zeros_like(dcum_ref)

        dq_s[...] = jnp.zeros_like(dq_s)

        def kblock(j, masked):
            rows = pl.ds(pl.multiple_of(j * tk, tk), tk)
            keep = _causal_keep(i, j, tq, tk) if masked else None
            qk = [_mm(k_ref[rows, hs], qT_ref[hs, :]) for hs in heads]
            dps = [_mm(v_ref[rows, hs], doT_ref[hs, :]) for hs in heads]
            for h, hs in enumerate(heads):
                sT = qk[h] * SCALE + (cumT_ref[h:h + 1, :] - cum_ref[rows, h:h + 1])
                if masked:
                    sT = jnp.where(keep, sT, NEG)
                pT = jnp.exp(sT - lseT_ref[h:h + 1, :])
                dsT = pT * (dps[h] - dlT_ref[h:h + 1, :])
                dcum_ref[rows, h:h + 1] += -jnp.sum(dsT, axis=1, keepdims=True)
                dsb = dsT.astype(MXU)
                dv_ref[hs, rows] += _mm_nt(doT_ref[hs, :], pT.astype(MXU))
                dk_ref[hs, rows] += _mm_nt(qT_ref[hs, :], dsb) * SCALE
                dq_s[hs, :] += _mm(kT_ref[hs, rows], dsb)

        def unmasked(j, c):
            kblock(j, False)
            return c

        lax.fori_loop(0, kpq * i, unmasked, 0)
        for d in range(kpq):
            kblock(kpq * i + d, True)
        dq_ref[...] = (dq_s[...] * SCALE).astype(dq_ref.dtype)

    full = lambda a, b: pl.BlockSpec((a, b), lambda i: (0, 0))
    colblk = lambda r: pl.BlockSpec((r, tq), lambda i: (0, i))
    return _pallas_call(
        body, name=name, grid=(T // tq,),
        in_specs=[colblk(D_ATTN), full(T, D_ATTN), full(D_ATTN, T), full(T, D_ATTN), colblk(D_ATTN),
                  colblk(N_HEADS), colblk(N_HEADS), full(T, LANES), colblk(N_HEADS)] + [_UNREAD] * len(after),
        out_specs=[colblk(D_ATTN), full(D_ATTN, T), full(D_ATTN, T), full(T, LANES)],
        out_shape=[
            jax.ShapeDtypeStruct((D_ATTN, T), MXU),
            jax.ShapeDtypeStruct((D_ATTN, T), F32),
            jax.ShapeDtypeStruct((D_ATTN, T), F32),
            jax.ShapeDtypeStruct((T, LANES), F32),
        ],
        scratch_shapes=[pltpu.VMEM((D_ATTN, tq), F32)],
        compiler_params=_params(48, 1),
    )(qT, k, kT, v, doT, lseT, deltaT, cum, cumT, *after)


_ROWS_PER_CHUNK = 64


def _glu_halo(ag_ref, agh_ref, uext_s, tm, first):
    a = ag_ref[:, :D_CONV]
    sg = jax.nn.sigmoid(ag_ref[:, D_CONV:])
    uh = agh_ref[:, :D_CONV] * jax.nn.sigmoid(agh_ref[:, D_CONV:])
    uext_s[0:CONV_HALO, :] = jnp.where(first, 0.0, uh)
    uext_s[CONV_HALO:CONV_HALO + tm, :] = a * sg
    return a, sg


def _layer_norm_stats(y):
    mu = jnp.mean(y, axis=-1, keepdims=True)
    xc = y - mu
    rs = lax.rsqrt(jnp.mean(xc * xc, axis=-1, keepdims=True) + EPS)
    return xc * rs, rs


def _conv_fwd(ag, w32, cb, lg, lb, name):
    T = ag.shape[0]
    tm = _tile(T, 256)
    rc = _tile(tm, _ROWS_PER_CHUNK)
    hb = tm // CONV_HALO

    def body(ag_ref, agh_ref, w_ref, cb_ref, lg_ref, lb_ref, yc_ref, c_ref, uext_s):
        i = pl.program_id(0)
        _glu_halo(ag_ref, agh_ref, uext_s, tm, i == 0)
        for r0 in range(0, tm, rc):
            acc = jnp.zeros((rc, D_CONV), F32)
            for t in range(CONV_TAPS):
                acc = acc + uext_s[pl.ds(r0 + CONV_HALO - (CONV_TAPS - 1) + t, rc), :] * w_ref[t:t + 1, :]
            y = acc + cb_ref[...]
            yc_ref[r0:r0 + rc, :] = y
            n, _ = _layer_norm_stats(y)
            z = n * lg_ref[...] + lb_ref[...]
            c_ref[r0:r0 + rc, :] = z * jax.nn.sigmoid(z)

    row = lambda w: pl.BlockSpec((tm, w), lambda i: (i, 0))
    full = lambda a, b: pl.BlockSpec((a, b), lambda i: (0, 0))
    return _pallas_call(
        body, name=name, grid=(T // tm,),
        in_specs=[row(2 * D_CONV),
                  pl.BlockSpec((CONV_HALO, 2 * D_CONV), lambda i: (jnp.maximum(i * hb - 1, 0), 0)),
                  full(CONV_HALO, D_CONV), full(1, D_CONV), full(1, D_CONV), full(1, D_CONV)],
        out_specs=[row(D_CONV), row(D_CONV)],
        out_shape=[jax.ShapeDtypeStruct((T, D_CONV), F32), jax.ShapeDtypeStruct((T, D_CONV), F32)],
        scratch_shapes=[pltpu.VMEM((CONV_HALO + tm, D_CONV), F32)],
        compiler_params=_params(32, 1),
    )(ag, ag, w32, cb, lg, lb)


def _conv_bwd(dc, yc, ag, w32, lg, lb, name):
    T = ag.shape[0]
    tm = _tile(T, 256)
    rc = _tile(tm, _ROWS_PER_CHUNK)
    I = T // tm
    hb = tm // CONV_HALO
    n_halo_blocks = T // CONV_HALO

    def body(dc_ref, yc_ref, dch_ref, ych_ref, ag_ref, agh_ref, w_ref, lg_ref, lb_ref,
             dag_ref, dw_ref, dcb_ref, dlg_ref, dlb_ref, uext_s, dext_s):
        i = pl.program_id(0)
        lgv = lg_ref[...]
        lbv = lb_ref[...]

        def ln_bwd(dcv, ycv):
            n, rs = _layer_norm_stats(ycv)
            z = n * lgv + lbv
            dz = dcv * _silu_grad(z, jax.nn.sigmoid(z))
            dn = dz * lgv
            dy = rs * (dn - jnp.mean(dn, axis=-1, keepdims=True) - n * jnp.mean(dn * n, axis=-1, keepdims=True))
            return dy, dz, n

        dy, dz, n = ln_bwd(dc_ref[...], yc_ref[...])
        dyh, _, _ = ln_bwd(dch_ref[...], ych_ref[...])
        dext_s[0:tm, :] = dy
        dext_s[tm:tm + CONV_HALO, :] = jnp.where(i == I - 1, 0.0, dyh)
        a, sg = _glu_halo(ag_ref, agh_ref, uext_s, tm, i == 0)

        @pl.when(i == 0)
        def _():
            dw_ref[...] = jnp.zeros_like(dw_ref)
            dcb_ref[...] = jnp.zeros_like(dcb_ref)
            dlg_ref[...] = jnp.zeros_like(dlg_ref)
            dlb_ref[...] = jnp.zeros_like(dlb_ref)

        dcb_ref[...] += jnp.sum(dy, axis=0, keepdims=True)
        dlg_ref[...] += jnp.sum(dz * n, axis=0, keepdims=True)
        dlb_ref[...] += jnp.sum(dz, axis=0, keepdims=True)
        for t in range(CONV_TAPS):
            u_t = uext_s[pl.ds(CONV_HALO - (CONV_TAPS - 1) + t, tm), :]
            dw_ref[t:t + 1, :] += jnp.sum(dy * u_t, axis=0, keepdims=True)
        for r0 in range(0, tm, rc):
            acc = jnp.zeros((rc, D_CONV), F32)
            for t in range(CONV_TAPS):
                acc = acc + dext_s[pl.ds(r0 + (CONV_TAPS - 1) - t, rc), :] * w_ref[t:t + 1, :]
            a_c = a[r0:r0 + rc, :]
            sg_c = sg[r0:r0 + rc, :]
            dag_ref[r0:r0 + rc, :D_CONV] = (acc * sg_c).astype(dag_ref.dtype)
            dag_ref[r0:r0 + rc, D_CONV:] = (acc * a_c * sg_c * (1.0 - sg_c)).astype(dag_ref.dtype)

    row = lambda w: pl.BlockSpec((tm, w), lambda i: (i, 0))
    full = lambda a, b: pl.BlockSpec((a, b), lambda i: (0, 0))
    nxt = pl.BlockSpec((CONV_HALO, D_CONV), lambda i: (jnp.minimum((i + 1) * hb, n_halo_blocks - 1), 0))
    return _pallas_call(
        body, name=name, grid=(I,),
        in_specs=[row(D_CONV), row(D_CONV), nxt, nxt, row(2 * D_CONV),
                  pl.BlockSpec((CONV_HALO, 2 * D_CONV), lambda i: (jnp.maximum(i * hb - 1, 0), 0)),
                  full(CONV_HALO, D_CONV), full(1, D_CONV), full(1, D_CONV)],
        out_specs=[row(2 * D_CONV), full(CONV_HALO, D_CONV), full(1, D_CONV), full(1, D_CONV), full(1, D_CONV)],
        out_shape=[
            jax.ShapeDtypeStruct((T, 2 * D_CONV), MXU),
            jax.ShapeDtypeStruct((CONV_HALO, D_CONV), F32),
            jax.ShapeDtypeStruct((1, D_CONV), F32),
            jax.ShapeDtypeStruct((1, D_CONV), F32),
            jax.ShapeDtypeStruct((1, D_CONV), F32),
        ],
        scratch_shapes=[pltpu.VMEM((CONV_HALO + tm, D_CONV), F32), pltpu.VMEM((tm + CONV_HALO, D_CONV), F32)],
        compiler_params=_params(32, 1),
    )(dc, yc, dc, yc, ag, ag, w32, lg, lb)


def _outproj_fwd(x1, c, o, gc, ga, wout, name):
    T, D = x1.shape
    tm = _tile(T, 512)

    def body(x_ref, c_ref, o_ref, gc_ref, ga_ref, w_ref, x2_ref):
        yc, _ = _rms_fwd(c_ref[...], gc_ref[...])
        ya, _ = _rms_fwd(o_ref[...], ga_ref[...])
        x2_ref[...] = (x_ref[...] + _mm(yc.astype(MXU), w_ref[:D_CONV, :])
                       + _mm(ya.astype(MXU), w_ref[D_CONV:, :]))

    row = lambda w: pl.BlockSpec((tm, w), lambda i: (i, 0))
    full = lambda a, b: pl.BlockSpec((a, b), lambda i: (0, 0))
    return _pallas_call(
        body, name=name, grid=(T // tm,),
        in_specs=[row(D), row(D_CONV), row(D_ATTN), full(1, D_CONV), full(1, D_ATTN), full(D_CONV + D_ATTN, D)],
        out_specs=row(D),
        out_shape=jax.ShapeDtypeStruct((T, D), F32),
        compiler_params=_params(32, 1),
    )(x1, c, o, gc, ga, wout)


def _outproj_bwd(dx2, c, o, gc, ga, wout, name):
    T, D = dx2.shape
    tm = _tile(T, 256)
    I = T // tm

    def body(dx_ref, c_ref, o_ref, gc_ref, ga_ref, w_ref,
             dc_ref, doT_ref, dlT_ref, dw_ref, dgc_ref, dga_ref, acc_s):
        i = pl.program_id(0)
        dxb = dx_ref[...].astype(MXU)
        cv = c_ref[...]
        ov = o_ref[...]
        yc, rcn = _rms_fwd(cv, gc_ref[...])
        ya, ra = _rms_fwd(ov, ga_ref[...])
        dyc = _mm_nt(dxb, w_ref[:D_CONV, :])
        dya = _mm_nt(dxb, w_ref[D_CONV:, :])
        dwc = _mm_tn(yc.astype(MXU), dxb)
        dwa = _mm_tn(ya.astype(MXU), dxb)
        dcv, dgc = _rms_bwd(cv, rcn, gc_ref[...], dyc)
        dov, dga = _rms_bwd(ov, ra, ga_ref[...], dya)
        dc_ref[...] = dcv
        dob = dov.astype(doT_ref.dtype)
        doT_ref[...] = dov.T.astype(doT_ref.dtype)
        chan = lax.broadcasted_iota(jnp.int32, (D_ATTN, LANES), 0)
        head = lax.broadcasted_iota(jnp.int32, (D_ATTN, LANES), 1)
        in_head = ((chan >= head * HEAD_DIM) & (chan < (head + 1) * HEAD_DIM)).astype(jnp.bfloat16)
        dlT_ref[...] = _exact_dot_01(dob.astype(F32) * ov, in_head).T[:N_HEADS, :]

        @pl.when(i == 0)
        def _():
            acc_s[:D_CONV, :] = dwc
            acc_s[D_CONV:, :] = dwa
            dgc_ref[...] = dgc
            dga_ref[...] = dga

        @pl.when(i > 0)
        def _():
            acc_s[:D_CONV, :] += dwc
            acc_s[D_CONV:, :] += dwa
            dgc_ref[...] += dgc
            dga_ref[...] += dga

        @pl.when(i == I - 1)
        def _():
            dw_ref[...] = acc_s[...].astype(dw_ref.dtype)

    row = lambda w: pl.BlockSpec((tm, w), lambda i: (i, 0))
    full = lambda a, b: pl.BlockSpec((a, b), lambda i: (0, 0))
    return _pallas_call(
        body, name=name, grid=(I,),
        in_specs=[row(D), row(D_CONV), row(D_ATTN), full(1, D_CONV), full(1, D_ATTN), full(D_CONV + D_ATTN, D)],
        out_specs=[row(D_CONV), pl.BlockSpec((D_ATTN, tm), lambda i: (0, i)),
                   pl.BlockSpec((N_HEADS, tm), lambda i: (0, i)),
                   full(D_CONV + D_ATTN, D), full(1, D_CONV), full(1, D_ATTN)],
        out_shape=[
            jax.ShapeDtypeStruct((T, D_CONV), F32),
            jax.ShapeDtypeStruct((D_ATTN, T), MXU),
            jax.ShapeDtypeStruct((N_HEADS, T), F32),
            jax.ShapeDtypeStruct((D_CONV + D_ATTN, D), MXU),
            jax.ShapeDtypeStruct((1, D_CONV), F32),
            jax.ShapeDtypeStruct((1, D_ATTN), F32),
        ],
        scratch_shapes=[pltpu.VMEM((D_CONV + D_ATTN, D), F32)],
        compiler_params=_params(40, 1),
    )(dx2, c, o, gc, ga, wout)


def _loss_head(x3, gf, target, name):
    T, D = x3.shape
    tm = _tile(T, 512)

    def body(x_ref, g_ref, t_ref, loss_ref, dx_ref, dg_ref):
        i = pl.program_id(0)
        xv = x_ref[...]
        gv = g_ref[...]
        out, r = _rms_fwd(xv, gv)
        err = out - t_ref[...]
        part = jnp.full((1, LANES), 0.5 / D, F32) * jnp.sum(err * err)
        dxn, dgp = _rms_bwd(xv, r, gv, err * (1.0 / D))
        dx_ref[...] = dxn

        @pl.when(i == 0)
        def _():
            loss_ref[...] = part
            dg_ref[...] = dgp

        @pl.when(i > 0)
        def _():
            loss_ref[...] += part
            dg_ref[...] += dgp

    row = lambda w: pl.BlockSpec((tm, w), lambda i: (i, 0))
    full = lambda a, b: pl.BlockSpec((a, b), lambda i: (0, 0))
    return _pallas_call(
        body, name=name, grid=(T // tm,),
        in_specs=[row(D), full(1, D), row(D)],
        out_specs=[full(1, LANES), row(D), full(1, D)],
        out_shape=[jax.ShapeDtypeStruct((1, LANES), F32), jax.ShapeDtypeStruct((T, D), F32),
                   jax.ShapeDtypeStruct((1, D), F32)],
        compiler_params=_params(32, 1),
    )(x3, gf, target)


def _row_tile(rows):
    for cand in (256, 176, 128, 64, 32, 16):
        if rows % cand == 0:
            return cand
    return rows


def _adamw(w, m, v, parts, name):
    R, C = w.shape
    P = parts.shape[0]
    tr = _row_tile(R)
    c1 = 1.0 - ADAM_B1 ** ADAM_STEP
    c2 = 1.0 - ADAM_B2 ** ADAM_STEP

    def body(w_ref, m_ref, v_ref, p_ref, g_ref, d_ref, nm_ref, nv_ref):
        g = p_ref[0].astype(F32)
        for s in range(1, P):
            g = g + p_ref[s].astype(F32)
        wv = w_ref[...]
        mn = ADAM_B1 * m_ref[...] + (1.0 - ADAM_B1) * g
        vn = ADAM_B2 * v_ref[...] + (1.0 - ADAM_B2) * (g * g)
        g_ref[...] = g
        nm_ref[...] = mn
        nv_ref[...] = vn
        d_ref[...] = -ADAM_LR * ((mn / c1) / (jnp.sqrt(vn / c2) + ADAM_EPS) + ADAM_WD * wv)

    blk = pl.BlockSpec((tr, C), lambda i: (i, 0))
    out = jax.ShapeDtypeStruct((R, C), F32)
    return _pallas_call(
        body, name=name, grid=(R // tr,),
        in_specs=[blk, blk, blk, pl.BlockSpec((P, tr, C), lambda i: (0, i, 0))],
        out_specs=[blk, blk, blk, blk],
        out_shape=[out, out, out, out],
        compiler_params=_params(32, 1),
    )(w, m, v, parts)


def _position():
    return lax.axis_index("x"), lax.axis_index("y"), lax.axis_index("c")


def _flat(px, py, pc):
    return 4 * px + 2 * py + pc


def _gather_body(ins, outs, send_sems, recv_sems, local_sems, handshake):
    n = len(ins)
    x, y, c = _position()
    me, sibling = (x, y, c), (x, y, 1 - c)
    chips = [(1 - x, y), (x, 1 - y), (1 - x, 1 - y)]
    if handshake:
        _handshake([sibling] + [(*chip, cc) for chip in chips for cc in (c, 1 - c)])

    def copy(a, k, block, to, src=None):
        dst = outs[a].at[_flat(*block)]
        return pltpu.make_async_remote_copy(
            src_ref=dst if src is None else src, dst_ref=dst,
            send_sem=send_sems.at[a, k], recv_sem=recv_sems.at[a, k],
            device_id=to, device_id_type=MESH)

    mine = [pltpu.make_async_copy(ins[a], outs[a].at[_flat(*me)], local_sems.at[a]) for a in range(n)]
    for cp in mine:
        cp.start()
    first = []
    for a in range(n):
        first.append(copy(a, 0, me, sibling, src=ins[a]))
        first += [copy(a, 1 + j, me, (*chip, c), src=ins[a]) for j, chip in enumerate(chips)]
    for cp in first:
        cp.start()
    passed = []
    for a in range(n):
        for j, chip in enumerate(chips):
            copy(a, 1 + j, (*chip, c), me).wait_recv()
            fwd = copy(a, 4 + j, (*chip, c), sibling)
            fwd.start()
            passed.append(fwd)
    for a in range(n):
        copy(a, 0, sibling, me).wait_recv()
        for j, chip in enumerate(chips):
            copy(a, 4 + j, (*chip, 1 - c), me).wait_recv()
    for cp in first + passed:
        cp.wait_send()
    for cp in mine:
        cp.wait()


def _gather_scratch(n):
    return [pltpu.SemaphoreType.DMA((n, 7)), pltpu.SemaphoreType.DMA((n, 7)), pltpu.SemaphoreType.DMA((n,))]


def _all_gather(shards, name):
    n = len(shards)

    def body(*refs):
        _gather_body(refs[:n], refs[n:2 * n], *refs[2 * n:], handshake=False)

    hbm = pl.BlockSpec(memory_space=pltpu.HBM)
    return _pallas_call(
        body, name=name,
        in_specs=[hbm] * n, out_specs=[hbm] * n,
        out_shape=[jax.ShapeDtypeStruct((N_DEV,) + s.shape, s.dtype) for s in shards],
        scratch_shapes=_gather_scratch(n),
    )(*shards)


def _handshake(peers):
    barrier = pltpu.get_barrier_semaphore()
    for peer in peers:
        pl.semaphore_signal(barrier, inc=1, device_id=peer, device_id_type=MESH)
    pl.semaphore_wait(barrier, len(peers))


def _sequencer_call(body, name, collective_id, out_type, scratch_types, operands):
    return pl.kernel(
        body, name=name, out_type=out_type,
        mesh=plsc.ScalarSubcoreMesh(axis_name="sequencer", num_cores=1),
        scratch_types=scratch_types,
        compiler_params=pltpu.CompilerParams(collective_id=collective_id),
    )(*operands)


def _seq_all_gather(shards, name, collective_id, after):
    n = len(shards)

    def body(*refs):
        _gather_body(refs[:n], refs[n + 1:2 * n + 1], *refs[2 * n + 1:], handshake=True)

    return _sequencer_call(
        body, name, collective_id,
        [jax.ShapeDtypeStruct((N_DEV,) + s.shape, s.dtype) for s in shards],
        _gather_scratch(n), list(shards) + [after])


def _seq_to_sibling(parts, name, collective_id, after):
    n = len(parts)

    def body(*refs):
        ins, outs = refs[:n], refs[n + len(after):2 * n + len(after)]
        send_sems, recv_sems = refs[2 * n + len(after):]
        x, y, c = _position()
        sibling = (x, y, 1 - c)
        _handshake([sibling])
        sent = []
        for a in range(n):
            for q in range(N_CHIPS):
                cp = pltpu.make_async_remote_copy(
                    src_ref=ins[a].at[2 * q + (1 - c)], dst_ref=outs[a].at[q],
                    send_sem=send_sems.at[a, q], recv_sem=recv_sems.at[a, q],
                    device_id=sibling, device_id_type=MESH)
                cp.start()
                sent.append(cp)
        for cp in sent:
            cp.wait_recv()
        for cp in sent:
            cp.wait_send()

    return _sequencer_call(
        body, name, collective_id,
        [jax.ShapeDtypeStruct((N_CHIPS,) + p.shape[1:], p.dtype) for p in parts],
        [pltpu.SemaphoreType.DMA((n, N_CHIPS)), pltpu.SemaphoreType.DMA((n, N_CHIPS))],
        list(parts) + list(after))


def _seq_to_chips(partials, name, collective_id, after=()):
    n = len(partials)

    def body(*refs):
        ins, outs = refs[:n], refs[n + len(after):2 * n + len(after)]
        send_sems, recv_sems, local_sems = refs[2 * n + len(after):]
        x, y, c = _position()
        my_chip = 2 * x + y
        chips = [(1 - x, y), (x, 1 - y), (1 - x, 1 - y)]
        _handshake([(*chip, c) for chip in chips])
        mine = [pltpu.make_async_copy(ins[a].at[my_chip], outs[a].at[my_chip], local_sems.at[a]) for a in range(n)]
        for cp in mine:
            cp.start()
        sent = []
        for a in range(n):
            for j, (px, py) in enumerate(chips):
                cp = pltpu.make_async_remote_copy(
                    src_ref=ins[a].at[2 * px + py], dst_ref=outs[a].at[my_chip],
                    send_sem=send_sems.at[a, j], recv_sem=recv_sems.at[a, j],
                    device_id=(px, py, c), device_id_type=MESH)
                cp.start()
                sent.append(cp)
        for cp in sent:
            cp.wait_recv()
        for cp in sent:
            cp.wait_send()
        for cp in mine:
            cp.wait()

    return _sequencer_call(
        body, name, collective_id,
        [jax.ShapeDtypeStruct(p.shape, p.dtype) for p in partials],
        [pltpu.SemaphoreType.DMA((n, 3)), pltpu.SemaphoreType.DMA((n, 3)), pltpu.SemaphoreType.DMA((n,))],
        list(partials) + list(after))


def _pair_add(part, recv, name, after=()):
    _, R, C = part.shape
    pairs = part.reshape(N_CHIPS, 2, R, C)

    def body(p_ref, r_ref, *rest):
        o_ref = rest[len(after)]
        c = lax.axis_index("c")
        o_ref[...] = (p_ref[c].astype(F32) + r_ref[...].astype(F32)).astype(o_ref.dtype)

    return _pallas_call(
        body, name=name, grid=(N_CHIPS,),
        in_specs=[pl.BlockSpec((None, 2, R, C), lambda q: (q, 0, 0, 0)),
                  pl.BlockSpec((None, R, C), lambda q: (q, 0, 0))] + [_UNREAD] * len(after),
        out_specs=pl.BlockSpec((None, R, C), lambda q: (q, 0, 0)),
        out_shape=jax.ShapeDtypeStruct((N_CHIPS, R, C), part.dtype),
        compiler_params=_params(32, 1),
    )(pairs, recv, *after)


class _Reduced(NamedTuple):
    partials: list
    reduced: list


def _blocks(g):
    return g.reshape(N_DEV, -1, g.shape[-1])


def _reduce_scatter(parts, tag, ids, after=(), between=None, add_after=()):
    from_sibling = _seq_to_sibling(parts, "rs_sibling_" + tag, ids[0], after)
    mid = between(from_sibling[0]) if between else ()
    partials = [_pair_add(p, r, "rs_add_%s_%d" % (tag, a), add_after)
                for a, (p, r) in enumerate(zip(parts, from_sibling))]
    return _Reduced(partials, _seq_to_chips(partials, "rs_chips_" + tag, ids[1], mid))


_SMALL = ("ffn1_norm", "mix_norm", "conv_b", "conv_ln_g", "conv_ln_b", "forget_b", "out_norm_conv",
          "out_norm_attn", "ffn2_norm", "final_norm")
_PACK_WIDTH = 2 * D_CONV
_SLOT = dict(ffn1_norm=(0, 0), mix_norm=(1, 0), ffn2_norm=(2, 0), final_norm=(3, 0), conv_b=(4, 0),
             conv_ln_g=(4, D_CONV), conv_ln_b=(5, 0), out_norm_conv=(5, D_CONV), out_norm_attn=(6, 0),
             forget_b=(6, D_CONV))
_CONV_ROW0 = 8
_PACK_ROWS = _CONV_ROW0 + CONV_HALO


def _pack_small(small, name):
    arrays = [small[n] for n in _SMALL] + [small["conv_w"]]

    def body(*refs):
        out = refs[-1]
        out[...] = jnp.zeros_like(out)
        for n, ref in zip(_SMALL, refs):
            row, lane = _SLOT[n]
            out[row:row + 1, lane:lane + ref.shape[1]] = ref[...]
        out[_CONV_ROW0:, :D_CONV] = refs[len(_SMALL)][...]

    return _pallas_call(body, name=name, out_shape=jax.ShapeDtypeStruct((_PACK_ROWS, _PACK_WIDTH), F32))(*arrays)


def _adamw_small(gathered, w, m, v, name):
    c1 = 1.0 - ADAM_B1 ** ADAM_STEP
    c2 = 1.0 - ADAM_B2 ** ADAM_STEP
    k = len(_SMALL)

    def body(g_ref, *refs):
        ws, ms, vs = refs[:k], refs[k:2 * k], refs[2 * k:3 * k]
        outs = refs[3 * k:]
        total = g_ref[0]
        for s in range(1, N_DEV):
            total = total + g_ref[s]
        for i, n in enumerate(_SMALL):
            row, lane = _SLOT[n]
            width = ws[i].shape[1]
            g = total[row:row + 1, lane:lane + width]
            mn = ADAM_B1 * ms[i][...] + (1.0 - ADAM_B1) * g
            vn = ADAM_B2 * vs[i][...] + (1.0 - ADAM_B2) * (g * g)
            o_g, o_d, o_m, o_v = outs[4 * i:4 * i + 4]
            o_g[...] = g
            o_m[...] = mn
            o_v[...] = vn
            o_d[...] = -ADAM_LR * ((mn / c1) / (jnp.sqrt(vn / c2) + ADAM_EPS) + ADAM_WD * ws[i][...])
        outs[4 * k][...] = total[_CONV_ROW0:, :D_CONV]

    shapes = []
    for n in _SMALL:
        shapes += [jax.ShapeDtypeStruct(w[n].shape, F32)] * 4
    shapes.append(jax.ShapeDtypeStruct((CONV_HALO, D_CONV), F32))
    res = _pallas_call(body, name=name, out_shape=shapes)(
        gathered, *[w[n] for n in _SMALL], *[m[n] for n in _SMALL], *[v[n] for n in _SMALL])
    return {n: res[4 * i:4 * i + 4] for i, n in enumerate(_SMALL)}, res[4 * k]


def _local_step(x, target, norms, shard):
    D = x.shape[1]
    J = N_DEV // 2
    as13 = lambda g: g.reshape(2, J, g.shape[1], D)

    (g13_1,) = _all_gather([shard["ffn1_w13"]], "gather_ffn1_w13")
    (g2_1,) = _seq_all_gather([shard["ffn1_w2"]], "gather_ffn1_w2", 10, after=g13_1)
    w13_1 = as13(g13_1)
    G1, U1, A1 = _ffn_up(x, norms["ffn1_norm"], w13_1, "ffn1_up")
    gin, gout, gconv = _seq_all_gather([shard["w_in"], shard["w_out"], shard["conv_w"]], "gather_mix", 1, after=G1)
    w2_1 = g2_1.reshape(-1, D)
    x1 = _ffn_down(x, A1, w2_1, "ffn1_down")
    g13_2, g2_2 = _seq_all_gather([shard["ffn2_w13"], shard["ffn2_w2"]], "gather_ffn2", 2, after=x1)
    winp = jnp.pad(gin.reshape(N_IN, D), ((0, N_IN_PAD - N_IN), (0, 0)))
    wout = gout.reshape(-1, D)
    conv_w32 = jnp.pad(gconv.transpose(1, 0, 2).reshape(CONV_TAPS, D_CONV), ((0, CONV_HALO - CONV_TAPS), (0, 0)))

    ag, k, v, qT, kT, vT, fl = _inproj_fwd(x1, norms["mix_norm"], winp, "inproj_fwd")
    cum, cumT = _forget_fwd(fl, norms["forget_b"], "forget_fwd")
    yc, c = _conv_fwd(ag, conv_w32, norms["conv_b"], norms["conv_ln_g"], norms["conv_ln_b"], "conv_fwd")
    o, lseT = _attn_fwd(qT, k, vT, cum, cumT, "attn_fwd")
    x2 = _outproj_fwd(x1, c, o, norms["out_norm_conv"], norms["out_norm_attn"], wout, "outproj_fwd")
    w13_2, w2_2 = as13(g13_2), g2_2.reshape(-1, D)
    G2, U2, A2 = _ffn_up(x2, norms["ffn2_norm"], w13_2, "ffn2_up")
    x3 = _ffn_down(x2, A2, w2_2, "ffn2_down")
    loss, dx3, d_final = _loss_head(x3, norms["final_norm"], target, "loss_head")

    dw2_2 = _ffn_w2_grad(dx3, A2, "ffn2_w2_grad")
    dx2, d_ffn2n, h3, dG2, dU2 = _ffn_bwd_act(x2, norms["ffn2_norm"], dx3, G2, U2, w13_2, w2_2, "ffn2_bwd_act")
    dw13_2 = _ffn_w13_grad(h3, dG2, dU2, "ffn2_w13_grad")
    dc, dobT, deltaT, dwout, d_onc, d_ona = _outproj_bwd(
        dx2, c, o, norms["out_norm_conv"], norms["out_norm_attn"], wout, "outproj_bwd")
    red_ffn2 = _reduce_scatter([_blocks(dw13_2), _blocks(dw2_2)], "ffn2", (3, 4), add_after=(dc,))
    dqT, dkT, dvT, dcum = _attn_bwd(qT, k, kT, v, dobT, lseT, deltaT, cum, cumT, "attn_bwd",
                                    after=red_ffn2.partials)
    dfl, d_fb = _forget_bwd(dcum, fl, norms["forget_b"], "forget_bwd")
    dag, d_convw, d_cb, d_lg, d_lb = _conv_bwd(dc, yc, ag, conv_w32, norms["conv_ln_g"], norms["conv_ln_b"], "conv_bwd")
    dx1, d_mixn, h2 = _inproj_bwd_act(x1, norms["mix_norm"], dx2, dag, dqT, dkT, dvT, dfl, winp, "inproj_bwd_act")
    dw2_1 = _ffn_w2_grad(dx1, A1, "ffn1_w2_grad")
    red_w2_1 = _reduce_scatter([_blocks(dw2_1)], "ffn1_w2", (11, 12), after=red_ffn2.reduced[:1])
    dwinp = _inproj_bwd_weights(h2, dag, dqT, dkT, dvT, dfl, "inproj_bwd_weights", after=red_w2_1.partials)
    dwin_blocks = dwinp[:N_IN].reshape(N_DEV, N_IN // N_DEV, -1)
    red_mix = _reduce_scatter([dwin_blocks, _blocks(dwout)], "mix", (5, 6), after=red_w2_1.reduced[:1])
    dx, d_ffn1n, h1, dG1, dU1 = _ffn_bwd_act(x, norms["ffn1_norm"], dx1, G1, U1, w13_1, w2_1, "ffn1_bwd_act",
                                             after=red_mix.partials)
    dw13_1 = _ffn_w13_grad(h1, dG1, dU1, "ffn1_w13_grad")

    small = dict(ffn1_norm=d_ffn1n, mix_norm=d_mixn, conv_b=d_cb, conv_ln_g=d_lg, conv_ln_b=d_lb,
                 forget_b=d_fb, out_norm_conv=d_onc, out_norm_attn=d_ona, ffn2_norm=d_ffn2n,
                 final_norm=d_final, conv_w=d_convw)
    packed_small = _pack_small(small, "pack_small_grads")
    gathered_small = []

    def gather_small(behind):
        gathered_small.extend(_seq_all_gather([packed_small], "gather_small_grads", 9, after=behind))
        return gathered_small

    red_w13_1 = _reduce_scatter([_blocks(dw13_1)], "ffn1_w13", (7, 8), after=red_mix.reduced[:1],
                                between=gather_small)
    big = dict(ffn1_w13=red_w13_1.reduced[0], ffn1_w2=red_w2_1.reduced[0], w_in=red_mix.reduced[0],
               w_out=red_mix.reduced[1], ffn2_w13=red_ffn2.reduced[0], ffn2_w2=red_ffn2.reduced[1])
    return loss[0, 0], dx, gathered_small[0], big


_BIG = ("ffn1_w13", "ffn1_w2", "w_in", "w_out", "ffn2_w13", "ffn2_w2")
_TRANSPOSED = ("ffn1_w13", "ffn2_w13", "w_in")
_ORDER = ("ffn1_norm", "ffn1_w13", "ffn1_w2", "mix_norm", "w_in", "conv_w", "conv_b", "conv_ln_g", "conv_ln_b",
          "forget_b", "out_norm_conv", "out_norm_attn", "w_out", "ffn2_norm", "ffn2_w13", "ffn2_w2", "final_norm")


def kernel(x, ffn1_norm, ffn1_w13, ffn1_w2, mix_norm, w_in, conv_w, conv_b, conv_ln_g, conv_ln_b, forget_b, out_norm_conv, out_norm_attn, w_out, ffn2_norm, ffn2_w13, ffn2_w2, final_norm, loss_target, m_ffn1_norm, m_ffn1_w13, m_ffn1_w2, m_mix_norm, m_w_in, m_conv_w, m_conv_b, m_conv_ln_g, m_conv_ln_b, m_forget_b, m_out_norm_conv, m_out_norm_attn, m_w_out, m_ffn2_norm, m_ffn2_w13, m_ffn2_w2, m_final_norm, v_ffn1_norm, v_ffn1_w13, v_ffn1_w2, v_mix_norm, v_w_in, v_conv_w, v_conv_b, v_conv_ln_g, v_conv_ln_b, v_forget_b, v_out_norm_conv, v_out_norm_attn, v_w_out, v_ffn2_norm, v_ffn2_w13, v_ffn2_w2, v_final_norm):
    w = dict(ffn1_norm=ffn1_norm, ffn1_w13=ffn1_w13, ffn1_w2=ffn1_w2, mix_norm=mix_norm, w_in=w_in, conv_w=conv_w,
             conv_b=conv_b, conv_ln_g=conv_ln_g, conv_ln_b=conv_ln_b, forget_b=forget_b, out_norm_conv=out_norm_conv,
             out_norm_attn=out_norm_attn, w_out=w_out, ffn2_norm=ffn2_norm, ffn2_w13=ffn2_w13, ffn2_w2=ffn2_w2,
             final_norm=final_norm)
    m = dict(ffn1_norm=m_ffn1_norm, ffn1_w13=m_ffn1_w13, ffn1_w2=m_ffn1_w2, mix_norm=m_mix_norm, w_in=m_w_in,
             conv_w=m_conv_w, conv_b=m_conv_b, conv_ln_g=m_conv_ln_g, conv_ln_b=m_conv_ln_b, forget_b=m_forget_b,
             out_norm_conv=m_out_norm_conv, out_norm_attn=m_out_norm_attn, w_out=m_w_out, ffn2_norm=m_ffn2_norm,
             ffn2_w13=m_ffn2_w13, ffn2_w2=m_ffn2_w2, final_norm=m_final_norm)
    v = dict(ffn1_norm=v_ffn1_norm, ffn1_w13=v_ffn1_w13, ffn1_w2=v_ffn1_w2, mix_norm=v_mix_norm, w_in=v_w_in,
             conv_w=v_conv_w, conv_b=v_conv_b, conv_ln_g=v_conv_ln_g, conv_ln_b=v_conv_ln_b, forget_b=v_forget_b,
             out_norm_conv=v_out_norm_conv, out_norm_attn=v_out_norm_attn, w_out=v_w_out, ffn2_norm=v_ffn2_norm,
             ffn2_w13=v_ffn2_w13, ffn2_w2=v_ffn2_w2, final_norm=v_final_norm)
    shapes = {n: a.shape for n, a in w.items()}
    T, D = x.shape[1], x.shape[2]
    def two(n, a):
        if a.ndim != 3:
            return a.reshape(1, -1)
        a = a.reshape(a.shape[-2], a.shape[-1])
        return a.T if n in _TRANSPOSED else a

    w2d = {n: two(n, a) for n, a in w.items()}
    m2d = {n: two(n, a) for n, a in m.items()}
    v2d = {n: two(n, a) for n, a in v.items()}

    shard = {n: w2d[n].astype(MXU) for n in _BIG}
    shard["conv_w"] = w2d["conv_w"]
    norms = {n: w2d[n] for n in _SMALL}
    norms["forget_b"] = jnp.pad(w2d["forget_b"], ((0, 0), (0, LANES - N_HEADS)))
    loss_part, dx, gathered_small, big = _local_step(x[0], loss_target[0], norms, shard)
    loss = lax.psum(loss_part, ("x", "y", "c"))

    grads, deltas, new_m, new_v = {}, {}, {}, {}
    for n in _BIG:
        g, d, nm, nv = _adamw(w2d[n], m2d[n], v2d[n], big[n], "adamw_" + n)
        grads[n], deltas[n], new_m[n], new_v[n] = g, d, nm, nv

    small_out, conv_g_full = _adamw_small(gathered_small, w2d, m2d, v2d, "adamw_small")
    for n in _SMALL:
        grads[n], deltas[n], new_m[n], new_v[n] = small_out[n]
    conv_g_full = conv_g_full[:CONV_TAPS]
    xi, yi, ci = _position()
    cw = shapes["conv_w"][-1]
    conv_g_mine = lax.dynamic_slice_in_dim(conv_g_full, _flat(xi, yi, ci) * cw, cw, axis=1)
    g, d, nm, nv = _adamw(w2d["conv_w"], m2d["conv_w"], v2d["conv_w"], conv_g_mine[None], "adamw_conv_w")
    grads["conv_w"], deltas["conv_w"], new_m["conv_w"], new_v["conv_w"] = g, d, nm, nv

    shaped = lambda dct: [(dct[n].T if n in _TRANSPOSED else dct[n]).reshape(shapes[n]) for n in _ORDER]
    return (loss, dx[None], *shaped(grads), *shaped(deltas), *shaped(new_m), *shaped(new_v))
```

```python
from typing import NamedTuple

import jax
import jax.numpy as jnp
from jax import lax
from jax.experimental import pallas as pl
from jax.experimental.pallas import tpu as pltpu
from jax.experimental.pallas import tpu_sc as plsc

F32 = jnp.float32
MXU = jnp.bfloat16
EPS = 1e-6
N_HEADS = 8
HEAD_DIM = 64
D_CONV = 512
D_ATTN = N_HEADS * HEAD_DIM
CONV_TAPS = 31
CONV_HALO = 32
SCALE = HEAD_DIM ** -0.5
NEG = -1e30
LANES = 128
N_DEV = 8
N_CHIPS = N_DEV // 2
MESH = pl.DeviceIdType.MESH
MIB = 1 << 20

ADAM_LR = 0.001
ADAM_B1 = 0.9
ADAM_B2 = 0.999
ADAM_EPS = 1e-08
ADAM_WD = 0.01
ADAM_STEP = 10


_UNREAD = pl.BlockSpec(memory_space=pl.ANY)


def _pallas_call(body, *, out_shape, **kwargs):
    in_hbm = lambda s: pltpu.HBM(s.shape, s.dtype)
    outs = [in_hbm(s) for s in out_shape] if isinstance(out_shape, (list, tuple)) else in_hbm(out_shape)
    call = pl.pallas_call(body, out_shape=outs, **kwargs)
    return lambda *operands: call(*[pltpu.with_memory_space_constraint(a, pltpu.HBM) for a in operands])


def _params(vmem_mib, n_axes):
    return pltpu.CompilerParams(dimension_semantics=("arbitrary",) * n_axes, vmem_limit_bytes=vmem_mib * MIB)


def _mm(a, b):
    return jnp.dot(a, b, preferred_element_type=F32)


def _mm_nt(a, b):
    return lax.dot_general(a, b, (((1,), (1,)), ((), ())), preferred_element_type=F32)


def _mm_tn(a, b):
    return lax.dot_general(a, b, (((0,), (0,)), ((), ())), preferred_element_type=F32)


def _rms_fwd(x, g):
    r = lax.rsqrt(jnp.mean(x * x, axis=-1, keepdims=True) + EPS)
    return x * r * g, r


def _rms_bwd(x, r, g, dy):
    gdy = dy * g
    dx = r * gdy - x * (r * r * r) * jnp.mean(x * gdy, axis=-1, keepdims=True)
    dg = jnp.sum(dy * x * r, axis=0, keepdims=True)
    return dx, dg


def _silu_grad(z, sz):
    return sz * (1.0 + z * (1.0 - sz))


def _three_terms(x):
    x1 = x.astype(jnp.bfloat16)
    r1 = x - x1.astype(F32)
    x2 = r1.astype(jnp.bfloat16)
    x3 = (r1 - x2.astype(F32)).astype(jnp.bfloat16)
    return x1, x2, x3


def _exact_tri_dot(tri, x):
    x1, x2, x3 = _three_terms(x)
    return _mm(tri, x1) + _mm(tri, x2) + _mm(tri, x3)


def _exact_dot_01(x, sel):
    x1, x2, x3 = _three_terms(x)
    return _mm(x1, sel) + _mm(x2, sel) + _mm(x3, sel)


def _tile(n, want):
    t = min(n, want)
    assert n % t == 0
    return t


_FFN_CHUNK = 256


def _ffn_up(x, g, w13, name):
    T, D = x.shape
    _, J, bf, _ = w13.shape
    tm = _tile(T, 512)
    I = T // tm

    def body(x_ref, g_ref, w13_ref, G_ref, U_ref, A_ref, h_s):
        j = pl.program_id(0)
        i = pl.program_id(1)
        rows = pl.ds(pl.multiple_of(i * tm, tm), tm)

        @pl.when(j == 0)
        def _():
            h, _ = _rms_fwd(x_ref[...], g_ref[...])
            h_s[rows, :] = h.astype(MXU)

        chunks = [slice(r0, r0 + _FFN_CHUNK) for r0 in range(0, tm, _FFN_CHUNK)]
        hbs = [h_s[pl.ds(pl.multiple_of(i * tm + rs.start, _FFN_CHUNK), _FFN_CHUNK), :] for rs in chunks]
        GU = [(_mm_nt(hb, w13_ref[0]), _mm_nt(hb, w13_ref[1])) for hb in hbs]
        for rs, (G, U) in zip(chunks, GU):
            G_ref[rs, :] = G.astype(MXU)
            U_ref[rs, :] = U.astype(MXU)
            A_ref[rs, :] = (G * jax.nn.sigmoid(G) * U).astype(MXU)

    blk = pl.BlockSpec((None, tm, bf), lambda j, i: (j, i, 0))
    hid = jax.ShapeDtypeStruct((J, T, bf), MXU)
    return _pallas_call(
        body, name=name, grid=(J, I),
        in_specs=[pl.BlockSpec((tm, D), lambda j, i: (jnp.where(j == 0, i, I - 1), 0)),
                  pl.BlockSpec((1, D), lambda j, i: (0, 0)),
                  pl.BlockSpec((2, None, bf, D), lambda j, i: (0, j, 0, 0))],
        out_specs=[blk, blk, blk],
        out_shape=[hid, hid, hid],
        scratch_shapes=[pltpu.VMEM((T, D), MXU)],
        compiler_params=_params(40, 2),
    )(x, g, w13)


def _ffn_down(x, A, w2, name):
    T, D = x.shape
    J, _, bf = A.shape
    tm = _tile(T, 512)

    def body(x_ref, A_ref, w2_ref, xo_ref):
        f = _mm(A_ref[0], w2_ref[0:bf, :])
        for j in range(1, J):
            f = f + _mm(A_ref[j], w2_ref[j * bf:(j + 1) * bf, :])
        xo_ref[...] = x_ref[...] + 0.5 * f

    row = pl.BlockSpec((tm, D), lambda i: (i, 0))
    return _pallas_call(
        body, name=name, grid=(T // tm,),
        in_specs=[row, pl.BlockSpec((J, tm, bf), lambda i: (0, i, 0)), pl.BlockSpec((J * bf, D), lambda i: (0, 0))],
        out_specs=row,
        out_shape=jax.ShapeDtypeStruct((T, D), F32),
        compiler_params=_params(48, 1),
    )(x, A, w2)


def _ffn_bwd_act(x, g, dy, Gs, Us, w13, w2, name, after=()):
    T, D = x.shape
    _, J, bf, _ = w13.shape
    tm = _tile(T, 512)
    I = T // tm

    def body(x_ref, g_ref, dy_ref, G_ref, U_ref, w13_ref, w2_ref, *rest):
        dx_ref, dg_ref, h_ref, dG_ref, dU_ref, dh_s, dF_s, h_s = rest[len(after):]
        j = pl.program_id(0)
        i = pl.program_id(1)
        rows = pl.ds(pl.multiple_of(i * tm, tm), tm)

        @pl.when(j == 0)
        def _():
            h, _ = _rms_fwd(x_ref[...], g_ref[...])
            hb = h.astype(MXU)
            h_s[rows, :] = hb
            h_ref[...] = hb
            dF_s[rows, :] = (0.5 * dy_ref[...]).astype(MXU)
            dh_s[rows, :] = jnp.zeros((tm, D), F32)

        chunks = [slice(r0, r0 + _FFN_CHUNK) for r0 in range(0, tm, _FFN_CHUNK)]
        crows = [pl.ds(pl.multiple_of(i * tm + rs.start, _FFN_CHUNK), _FFN_CHUNK) for rs in chunks]
        dAs = [_mm_nt(dF_s[cr, :], w2_ref[...]) for cr in crows]
        for rs, cr, dA in zip(chunks, crows, dAs):
            G = G_ref[rs, :].astype(F32)
            U = U_ref[rs, :].astype(F32)
            sg = jax.nn.sigmoid(G)
            s = G * sg
            dUb = (dA * s).astype(MXU)
            dGb = (dA * U * _silu_grad(G, sg)).astype(MXU)
            dG_ref[rs, :] = dGb
            dU_ref[rs, :] = dUb
            dh_s[cr, :] += _mm(dGb, w13_ref[0]) + _mm(dUb, w13_ref[1])

        @pl.when(j == J - 1)
        def _():
            xv = x_ref[...]
            gv = g_ref[...]
            _, r = _rms_fwd(xv, gv)
            dxn, dgp = _rms_bwd(xv, r, gv, dh_s[rows, :])
            dx_ref[...] = dy_ref[...] + dxn

            @pl.when(i == 0)
            def _():
                dg_ref[...] = dgp

            @pl.when(i > 0)
            def _():
                dg_ref[...] += dgp

    ends = lambda j, i: (jnp.where((j == 0) | (j == J - 1), i, I - 1), 0)
    blk = pl.BlockSpec((None, tm, bf), lambda j, i: (j, i, 0))
    hid = jax.ShapeDtypeStruct((J, T, bf), MXU)
    return _pallas_call(
        body, name=name, grid=(J, I),
        in_specs=[pl.BlockSpec((tm, D), ends), pl.BlockSpec((1, D), lambda j, i: (0, 0)), pl.BlockSpec((tm, D), ends),
                  blk, blk, pl.BlockSpec((2, None, bf, D), lambda j, i: (0, j, 0, 0)),
                  pl.BlockSpec((bf, D), lambda j, i: (j, 0))] + [_UNREAD] * len(after),
        out_specs=[pl.BlockSpec((tm, D), lambda j, i: (jnp.where(j == J - 1, i, 0), 0)),
                   pl.BlockSpec((1, D), lambda j, i: (0, 0)),
                   pl.BlockSpec((tm, D), lambda j, i: (jnp.where(j == 0, i, I - 1), 0)), blk, blk],
        out_shape=[jax.ShapeDtypeStruct((T, D), F32), jax.ShapeDtypeStruct((1, D), F32),
                   jax.ShapeDtypeStruct((T, D), MXU), hid, hid],
        scratch_shapes=[pltpu.VMEM((T, D), F32), pltpu.VMEM((T, D), MXU), pltpu.VMEM((T, D), MXU)],
        compiler_params=_params(58, 2),
    )(x, g, dy, Gs, Us, w13, w2, *after)


def _ffn_w13_grad(h, dG, dU, name):
    T, D = h.shape
    J, _, bf = dG.shape

    def body(h_ref, dG_ref, dU_ref, dw13_ref):
        dw13_ref[0] = _mm_tn(dG_ref[...], h_ref[...]).astype(dw13_ref.dtype)
        dw13_ref[1] = _mm_tn(dU_ref[...], h_ref[...]).astype(dw13_ref.dtype)

    blk = pl.BlockSpec((None, T, bf), lambda j: (j, 0, 0))
    return _pallas_call(
        body, name=name, grid=(J,),
        in_specs=[pl.BlockSpec((T, D), lambda j: (0, 0)), blk, blk],
        out_specs=pl.BlockSpec((2, None, bf, D), lambda j: (0, j, 0, 0)),
        out_shape=jax.ShapeDtypeStruct((2, J, bf, D), MXU),
        compiler_params=_params(48, 1),
    )(h, dG, dU)


def _ffn_w2_grad(dy, A, name, after=()):
    T, D = dy.shape
    J, _, bf = A.shape

    def body(dy_ref, A_ref, *rest):
        dw2_ref, dF_s = rest[len(after):]

        @pl.when(pl.program_id(0) == 0)
        def _():
            dF_s[...] = (0.5 * dy_ref[...]).astype(MXU)

        dw2_ref[...] = _mm_tn(A_ref[...], dF_s[...]).astype(dw2_ref.dtype)

    return _pallas_call(
        body, name=name, grid=(J,),
        in_specs=[pl.BlockSpec((T, D), lambda j: (0, 0)), pl.BlockSpec((None, T, bf), lambda j: (j, 0, 0))]
        + [_UNREAD] * len(after),
        out_specs=pl.BlockSpec((bf, D), lambda j: (j, 0)),
        out_shape=jax.ShapeDtypeStruct((J * bf, D), MXU),
        scratch_shapes=[pltpu.VMEM((T, D), MXU)],
        compiler_params=_params(48, 1),
    )(dy, A, *after)


_AG0, _Q0, _K0, _V0, _F0 = 0, 2 * D_CONV, 2 * D_CONV + D_ATTN, 2 * D_CONV + 2 * D_ATTN, 2 * D_CONV + 3 * D_ATTN
N_IN = _F0 + N_HEADS
N_IN_PAD = _F0 + LANES


def _inproj_fwd(x1, gm, winp, name):
    T, D = x1.shape
    tm = _tile(T, 256)

    def body(x_ref, g_ref, w_ref, ag_ref, k_ref, v_ref, qT_ref, kT_ref, vT_ref, fl_ref):
        h, _ = _rms_fwd(x_ref[...], g_ref[...])
        hb = h.astype(MXU)
        ag_ref[...] = _mm_nt(hb, w_ref[_AG0:_Q0, :])
        qT_ref[...] = (_mm_nt(hb, w_ref[_Q0:_K0, :]) * SCALE).T.astype(MXU)
        for c0, ref, refT in ((_K0, k_ref, kT_ref), (_V0, v_ref, vT_ref)):
            y = _mm_nt(hb, w_ref[c0:c0 + D_ATTN, :])
            ref[...] = y.astype(MXU)
            refT[...] = y.T.astype(MXU)
        fl_ref[...] = _mm_nt(hb, w_ref[_F0:N_IN_PAD, :])

    row = lambda w: pl.BlockSpec((tm, w), lambda i: (i, 0))
    col = pl.BlockSpec((D_ATTN, tm), lambda i: (0, i))
    std = jax.ShapeDtypeStruct((T, D_ATTN), MXU)
    trn = jax.ShapeDtypeStruct((D_ATTN, T), MXU)
    return _pallas_call(
        body, name=name, grid=(T // tm,),
        in_specs=[row(D), pl.BlockSpec((1, D), lambda i: (0, 0)), pl.BlockSpec((N_IN_PAD, D), lambda i: (0, 0))],
        out_specs=[row(2 * D_CONV), row(D_ATTN), row(D_ATTN), col, col, col, row(LANES)],
        out_shape=[jax.ShapeDtypeStruct((T, 2 * D_CONV), F32), std, std, trn, trn, trn,
                   jax.ShapeDtypeStruct((T, LANES), F32)],
        compiler_params=_params(40, 1),
    )(x1, gm, winp)


def _inproj_bwd_act(x1, gm, dx2, dag, dqT, dkT, dvT, dfl, winp, name):
    T, D = x1.shape
    tm = _tile(T, 256)

    def body(x_ref, g_ref, dx2_ref, dag_ref, dqT_ref, dkT_ref, dvT_ref, dfl_ref, w_ref, dx1_ref, dg_ref, h_ref):
        i = pl.program_id(0)
        xv = x_ref[...]
        gv = g_ref[...]
        h, r = _rms_fwd(xv, gv)
        h_ref[...] = h.astype(MXU)
        dh = _mm(dag_ref[...], w_ref[_AG0:_Q0, :])
        for c0, ref in ((_Q0, dqT_ref), (_K0, dkT_ref), (_V0, dvT_ref)):
            dh = dh + _mm_tn(ref[...].astype(MXU), w_ref[c0:c0 + D_ATTN, :])
        dh = dh + _mm(dfl_ref[...].astype(MXU), w_ref[_F0:N_IN_PAD, :])
        dxn, dgp = _rms_bwd(xv, r, gv, dh)
        dx1_ref[...] = dx2_ref[...] + dxn

        @pl.when(i == 0)
        def _():
            dg_ref[...] = dgp

        @pl.when(i > 0)
        def _():
            dg_ref[...] += dgp

    row = lambda w: pl.BlockSpec((tm, w), lambda i: (i, 0))
    col = pl.BlockSpec((D_ATTN, tm), lambda i: (0, i))
    full = lambda a, b: pl.BlockSpec((a, b), lambda i: (0, 0))
    return _pallas_call(
        body, name=name, grid=(T // tm,),
        in_specs=[row(D), full(1, D), row(D), row(2 * D_CONV), col, col, col, row(LANES), full(N_IN_PAD, D)],
        out_specs=[row(D), full(1, D), row(D)],
        out_shape=[jax.ShapeDtypeStruct((T, D), F32), jax.ShapeDtypeStruct((1, D), F32),
                   jax.ShapeDtypeStruct((T, D), MXU)],
        compiler_params=_params(40, 1),
    )(x1, gm, dx2, dag, dqT, dkT, dvT, dfl, winp)


def _inproj_bwd_weights(h, dag, dqT, dkT, dvT, dfl, name, after=()):
    T, D = h.shape

    def body(h_ref, dag_ref, dqT_ref, dkT_ref, dvT_ref, dfl_ref, *rest):
        dw_ref = rest[len(after)]
        hb = h_ref[...]
        dw_ref[_AG0:_Q0, :] = _mm_tn(dag_ref[...], hb).astype(dw_ref.dtype)
        for c0, ref in ((_Q0, dqT_ref), (_K0, dkT_ref), (_V0, dvT_ref)):
            dw_ref[c0:c0 + D_ATTN, :] = _mm(ref[...].astype(MXU), hb).astype(dw_ref.dtype)
        dw_ref[_F0:N_IN_PAD, :] = _mm_tn(dfl_ref[...].astype(MXU), hb).astype(dw_ref.dtype)

    vmem = pl.BlockSpec(memory_space=pltpu.VMEM)
    return _pallas_call(
        body, name=name, in_specs=[vmem] * 6 + [_UNREAD] * len(after), out_specs=vmem,
        out_shape=jax.ShapeDtypeStruct((N_IN_PAD, D), MXU),
        compiler_params=pltpu.CompilerParams(vmem_limit_bytes=56 * MIB),
    )(h, dag, dqT, dkT, dvT, dfl, *after)


def _forget_fwd(fl, fbp, name):
    T = fl.shape[0]
    tb = _tile(T, 256)

    def body(fl_ref, fb_ref, cum_ref, cumT_ref):
        ri = lax.broadcasted_iota(jnp.int32, (tb, tb), 0)
        ci = lax.broadcasted_iota(jnp.int32, (tb, tb), 1)
        tri = (ri >= ci).astype(jnp.bfloat16)
        carry = jnp.zeros((1, LANES), F32)
        for b in range(T // tb):
            z = fl_ref[b * tb:(b + 1) * tb, :] + fb_ref[...]
            lf = jnp.minimum(z, 0.0) - jnp.log1p(jnp.exp(-jnp.abs(z)))
            c = _exact_tri_dot(tri, lf) + carry
            cum_ref[b * tb:(b + 1) * tb, :] = c
            carry = c[tb - 1:tb, :]
        cumT_ref[...] = cum_ref[...].T[:N_HEADS, :]

    return _pallas_call(
        body, name=name,
        out_shape=[jax.ShapeDtypeStruct((T, LANES), F32), jax.ShapeDtypeStruct((N_HEADS, T), F32)],
        compiler_params=pltpu.CompilerParams(vmem_limit_bytes=32 * MIB),
    )(fl, fbp)


def _forget_bwd(dcum, fl, fbp, name):
    T = fl.shape[0]
    tb = _tile(T, 256)

    def body(dc_ref, fl_ref, fb_ref, dfl_ref, dfb_ref):
        ri = lax.broadcasted_iota(jnp.int32, (tb, tb), 0)
        ci = lax.broadcasted_iota(jnp.int32, (tb, tb), 1)
        tri = (ri <= ci).astype(jnp.bfloat16)
        carry = jnp.zeros((1, LANES), F32)
        dfb = jnp.zeros((1, LANES), F32)
        for b in reversed(range(T // tb)):
            sl = slice(b * tb, (b + 1) * tb)
            dl = _exact_tri_dot(tri, dc_ref[sl, :]) + carry
            carry = dl[0:1, :]
            z = fl_ref[sl, :] + fb_ref[...]
            dfl = dl * jax.nn.sigmoid(-z)
            dfl_ref[sl, :] = dfl
            dfb = dfb + jnp.sum(dfl, axis=0, keepdims=True)
        dfb_ref[...] = dfb

    return _pallas_call(
        body, name=name,
        out_shape=[jax.ShapeDtypeStruct((T, LANES), F32), jax.ShapeDtypeStruct((1, LANES), F32)],
        compiler_params=pltpu.CompilerParams(vmem_limit_bytes=32 * MIB),
    )(dcum, fl, fbp)


def _causal_keep(i, j, tq, tk):
    key = j * tk + lax.broadcasted_iota(jnp.int32, (tk, tq), 0)
    qry = i * tq + lax.broadcasted_iota(jnp.int32, (tk, tq), 1)
    return key <= qry


def _split_hi_lo(x):
    hi = x.astype(MXU)
    lo = (x - hi.astype(F32)).astype(MXU)
    return hi, lo


def _attn_fwd(qT, k, vT, cum, cumT, name):
    T = k.shape[0]
    tq = _tile(T, 256)
    tk = _tile(tq, 128)
    kpq = tq // tk
    heads = [slice(HEAD_DIM * h, HEAD_DIM * (h + 1)) for h in range(N_HEADS)]

    def body(qT_ref, k_ref, vT_ref, cum_ref, cumT_ref, o_ref, lseT_ref, acc_s, m_s, l_s):
        i = pl.program_id(0)
        acc_s[...] = jnp.zeros_like(acc_s)
        m_s[...] = jnp.full_like(m_s, NEG)
        l_s[...] = jnp.zeros_like(l_s)

        def kblock(j, masked):
            rows = pl.ds(pl.multiple_of(j * tk, tk), tk)
            keep = _causal_keep(i, j, tq, tk) if masked else None
            qk = [_mm(k_ref[rows, hs], qT_ref[hs, :]) for hs in heads]
            for h, hs in enumerate(heads):
                sT = qk[h] + (cumT_ref[h:h + 1, :] - cum_ref[rows, h:h + 1])
                if masked:
                    sT = jnp.where(keep, sT, NEG)
                m_old = m_s[h:h + 1, :]
                m_new = jnp.maximum(m_old, jnp.max(sT, axis=0, keepdims=True))
                alpha = jnp.exp(m_old - m_new)
                pT = jnp.exp(sT - m_new)
                l_s[h:h + 1, :] = alpha * l_s[h:h + 1, :] + jnp.sum(pT, axis=0, keepdims=True)
                p_hi, p_lo = _split_hi_lo(pT)
                vh = vT_ref[hs, rows]
                acc_s[hs, :] = alpha * acc_s[hs, :] + (_mm(vh, p_hi) + _mm(vh, p_lo))
                m_s[h:h + 1, :] = m_new

        def unmasked(j, c):
            kblock(j, False)
            return c

        lax.fori_loop(0, kpq * i, unmasked, 0)
        for d in range(kpq):
            kblock(kpq * i + d, True)
        for h, hs in enumerate(heads):
            acc_s[hs, :] = acc_s[hs, :] / l_s[h:h + 1, :]
        o_ref[...] = acc_s[...].T
        lseT_ref[...] = m_s[...] + jnp.log(l_s[...])

    full = lambda a, b: pl.BlockSpec((a, b), lambda i: (0, 0))
    colblk = lambda r: pl.BlockSpec((r, tq), lambda i: (0, i))
    return _pallas_call(
        body, name=name, grid=(T // tq,),
        in_specs=[colblk(D_ATTN), full(T, D_ATTN), full(D_ATTN, T), full(T, LANES), colblk(N_HEADS)],
        out_specs=[pl.BlockSpec((tq, D_ATTN), lambda i: (i, 0)), colblk(N_HEADS)],
        out_shape=[jax.ShapeDtypeStruct((T, D_ATTN), F32), jax.ShapeDtypeStruct((N_HEADS, T), F32)],
        scratch_shapes=[pltpu.VMEM((D_ATTN, tq), F32), pltpu.VMEM((N_HEADS, tq), F32),
                        pltpu.VMEM((N_HEADS, tq), F32)],
        compiler_params=_params(40, 1),
    )(qT, k, vT, cum, cumT)


def _attn_bwd(qT, k, kT, v, doT, lseT, deltaT, cum, cumT, name, after=()):
    T = k.shape[0]
    tq = _tile(T, 256)
    tk = _tile(tq, 128)
    kpq = tq // tk
    heads = [slice(HEAD_DIM * h, HEAD_DIM * (h + 1)) for h in range(N_HEADS)]

    def body(qT_ref, k_ref, kT_ref, v_ref, doT_ref, lseT_ref, dlT_ref, cum_ref, cumT_ref, *rest):
        dq_ref, dk_ref, dv_ref, dcum_ref, dq_s = rest[len(after):]
        i = pl.program_id(0)

        @pl.when(i == 0)
        def _():
            dk_ref[...] = jnp.zeros_like(dk_ref)
            dv_ref[...] = jnp.zeros_like(dv_ref)
            dcum_ref[...] = jnp.zeros_like(dcum_ref)

        dq_s[...] = jnp.zeros_like(dq_s)

        def kblock(j, masked):
            rows = pl.ds(pl.multiple_of(j * tk, tk), tk)
            keep = _causal_keep(i, j, tq, tk) if masked else None
            qk = [_mm(k_ref[rows, hs], qT_ref[hs, :]) for hs in heads]
            dps = [_mm(v_ref[rows, hs], doT_ref[hs, :]) for hs in heads]
            for h, hs in enumerate(heads):
                sT = qk[h] + (cumT_ref[h:h + 1, :] - cum_ref[rows, h:h + 1])
                if masked:
                    sT = jnp.where(keep, sT, NEG)
                pT = jnp.exp(sT - lseT_ref[h:h + 1, :])
                dsT = pT * (dps[h] - dlT_ref[h:h + 1, :])
                dcum_ref[rows, h:h + 1] += -jnp.sum(dsT, axis=1, keepdims=True)
                dsb = dsT.astype(MXU)
                dv_ref[hs, rows] += _mm_nt(doT_ref[hs, :], pT.astype(MXU))
                dk_ref[hs, rows] += _mm_nt(qT_ref[hs, :], dsb)
                dq_s[hs, :] += _mm(kT_ref[hs, rows], dsb)

        def unmasked(j, c):
            kblock(j, False)
            return c

        lax.fori_loop(0, kpq * i, unmasked, 0)
        for d in range(kpq):
            kblock(kpq * i + d, True)
        dq_ref[...] = (dq_s[...] * SCALE).astype(dq_ref.dtype)

    full = lambda a, b: pl.BlockSpec((a, b), lambda i: (0, 0))
    colblk = lambda r: pl.BlockSpec((r, tq), lambda i: (0, i))
    return _pallas_call(
        body, name=name, grid=(T // tq,),
        in_specs=[colblk(D_ATTN), full(T, D_ATTN), full(D_ATTN, T), full(T, D_ATTN), colblk(D_ATTN),
                  colblk(N_HEADS), colblk(N_HEADS), full(T, LANES), colblk(N_HEADS)] + [_UNREAD] * len(after),
        out_specs=[colblk(D_ATTN), full(D_ATTN, T), full(D_ATTN, T), full(T, LANES)],
        out_shape=[
            jax.ShapeDtypeStruct((D_ATTN, T), MXU),
            jax.ShapeDtypeStruct((D_ATTN, T), F32),
            jax.ShapeDtypeStruct((D_ATTN, T), F32),
            jax.ShapeDtypeStruct((T, LANES), F32),
        ],
        scratch_shapes=[pltpu.VMEM((D_ATTN, tq), F32)],
        compiler_params=_params(48, 1),
    )(qT, k, kT, v, doT, lseT, deltaT, cum, cumT, *after)


_ROWS_PER_CHUNK = 64


def _glu_halo(ag_ref, agh_ref, uext_s, tm, first):
    a = ag_ref[:, :D_CONV]
    sg = jax.nn.sigmoid(ag_ref[:, D_CONV:])
    uh = agh_ref[:, :D_CONV] * jax.nn.sigmoid(agh_ref[:, D_CONV:])
    uext_s[0:CONV_HALO, :] = jnp.where(first, 0.0, uh)
    uext_s[CONV_HALO:CONV_HALO + tm, :] = a * sg
    return a, sg


_SUBLANES = 8


def _shifted_copies(ext_s, sh_s, rows):
    for k in range(1, _SUBLANES):
        sh_s[k, 0:rows, :] = ext_s[pl.ds(k, rows), :]


def _window(ext_s, sh_s, start, rows):
    k = start % _SUBLANES
    if k == 0:
        return ext_s[pl.ds(start, rows), :]
    return sh_s[k, pl.ds(start - k, rows), :]


def _layer_norm_stats(y):
    mu = jnp.mean(y, axis=-1, keepdims=True)
    xc = y - mu
    rs = lax.rsqrt(jnp.mean(xc * xc, axis=-1, keepdims=True) + EPS)
    return xc * rs, rs


def _conv_fwd(ag, w32, cb, lg, lb, name):
    T = ag.shape[0]
    tm = _tile(T, 256)
    rc = _tile(tm, _ROWS_PER_CHUNK)
    hb = tm // CONV_HALO

    def body(ag_ref, agh_ref, w_ref, cb_ref, lg_ref, lb_ref, yc_ref, c_ref, uext_s, ush_s):
        i = pl.program_id(0)
        _glu_halo(ag_ref, agh_ref, uext_s, tm, i == 0)
        _shifted_copies(uext_s, ush_s, tm + CONV_HALO - _SUBLANES)
        for r0 in range(0, tm, rc):
            acc = jnp.zeros((rc, D_CONV), F32)
            for t in range(CONV_TAPS):
                acc = acc + _window(uext_s, ush_s, r0 + CONV_HALO - (CONV_TAPS - 1) + t, rc) * w_ref[t:t + 1, :]
            y = acc + cb_ref[...]
            yc_ref[r0:r0 + rc, :] = y
            n, _ = _layer_norm_stats(y)
            z = n * lg_ref[...] + lb_ref[...]
            c_ref[r0:r0 + rc, :] = z * jax.nn.sigmoid(z)

    row = lambda w: pl.BlockSpec((tm, w), lambda i: (i, 0))
    full = lambda a, b: pl.BlockSpec((a, b), lambda i: (0, 0))
    return _pallas_call(
        body, name=name, grid=(T // tm,),
        in_specs=[row(2 * D_CONV),
                  pl.BlockSpec((CONV_HALO, 2 * D_CONV), lambda i: (jnp.maximum(i * hb - 1, 0), 0)),
                  full(CONV_HALO, D_CONV), full(1, D_CONV), full(1, D_CONV), full(1, D_CONV)],
        out_specs=[row(D_CONV), row(D_CONV)],
        out_shape=[jax.ShapeDtypeStruct((T, D_CONV), F32), jax.ShapeDtypeStruct((T, D_CONV), F32)],
        scratch_shapes=[pltpu.VMEM((CONV_HALO + tm, D_CONV), F32),
                        pltpu.VMEM((_SUBLANES, CONV_HALO + tm, D_CONV), F32)],
        compiler_params=_params(32, 1),
    )(ag, ag, w32, cb, lg, lb)


def _conv_bwd(dc, yc, ag, w32, lg, lb, name):
    T = ag.shape[0]
    tm = _tile(T, 256)
    rc = _tile(tm, _ROWS_PER_CHUNK)
    I = T // tm
    hb = tm // CONV_HALO
    n_halo_blocks = T // CONV_HALO

    def body(dc_ref, yc_ref, dch_ref, ych_ref, ag_ref, agh_ref, w_ref, lg_ref, lb_ref,
             dag_ref, dw_ref, dcb_ref, dlg_ref, dlb_ref, uext_s, dext_s, ush_s, dsh_s):
        i = pl.program_id(0)
        lgv = lg_ref[...]
        lbv = lb_ref[...]

        def ln_bwd(dcv, ycv):
            n, rs = _layer_norm_stats(ycv)
            z = n * lgv + lbv
            dz = dcv * _silu_grad(z, jax.nn.sigmoid(z))
            dn = dz * lgv
            dy = rs * (dn - jnp.mean(dn, axis=-1, keepdims=True) - n * jnp.mean(dn * n, axis=-1, keepdims=True))
            return dy, dz, n

        dy, dz, n = ln_bwd(dc_ref[...], yc_ref[...])
        dyh, _, _ = ln_bwd(dch_ref[...], ych_ref[...])
        dext_s[0:tm, :] = dy
        dext_s[tm:tm + CONV_HALO, :] = jnp.where(i == I - 1, 0.0, dyh)
        a, sg = _glu_halo(ag_ref, agh_ref, uext_s, tm, i == 0)
        _shifted_copies(uext_s, ush_s, tm + CONV_HALO - _SUBLANES)
        _shifted_copies(dext_s, dsh_s, tm + CONV_HALO - _SUBLANES)

        @pl.when(i == 0)
        def _():
            dw_ref[...] = jnp.zeros_like(dw_ref)
            dcb_ref[...] = jnp.zeros_like(dcb_ref)
            dlg_ref[...] = jnp.zeros_like(dlg_ref)
            dlb_ref[...] = jnp.zeros_like(dlb_ref)

        dcb_ref[...] += jnp.sum(dy, axis=0, keepdims=True)
        dlg_ref[...] += jnp.sum(dz * n, axis=0, keepdims=True)
        dlb_ref[...] += jnp.sum(dz, axis=0, keepdims=True)
        for t in range(CONV_TAPS):
            u_t = _window(uext_s, ush_s, CONV_HALO - (CONV_TAPS - 1) + t, tm)
            dw_ref[t:t + 1, :] += jnp.sum(dy * u_t, axis=0, keepdims=True)
        for r0 in range(0, tm, rc):
            acc = jnp.zeros((rc, D_CONV), F32)
            for t in range(CONV_TAPS):
                acc = acc + _window(dext_s, dsh_s, r0 + (CONV_TAPS - 1) - t, rc) * w_ref[t:t + 1, :]
            a_c = a[r0:r0 + rc, :]
            sg_c = sg[r0:r0 + rc, :]
            dag_ref[r0:r0 + rc, :D_CONV] = (acc * sg_c).astype(dag_ref.dtype)
            dag_ref[r0:r0 + rc, D_CONV:] = (acc * a_c * sg_c * (1.0 - sg_c)).astype(dag_ref.dtype)

    row = lambda w: pl.BlockSpec((tm, w), lambda i: (i, 0))
    full = lambda a, b: pl.BlockSpec((a, b), lambda i: (0, 0))
    nxt = pl.BlockSpec((CONV_HALO, D_CONV), lambda i: (jnp.minimum((i + 1) * hb, n_halo_blocks - 1), 0))
    return _pallas_call(
        body, name=name, grid=(I,),
        in_specs=[row(D_CONV), row(D_CONV), nxt, nxt, row(2 * D_CONV),
                  pl.BlockSpec((CONV_HALO, 2 * D_CONV), lambda i: (jnp.maximum(i * hb - 1, 0), 0)),
                  full(CONV_HALO, D_CONV), full(1, D_CONV), full(1, D_CONV)],
        out_specs=[row(2 * D_CONV), full(CONV_HALO, D_CONV), full(1, D_CONV), full(1, D_CONV), full(1, D_CONV)],
        out_shape=[
            jax.ShapeDtypeStruct((T, 2 * D_CONV), MXU),
            jax.ShapeDtypeStruct((CONV_HALO, D_CONV), F32),
            jax.ShapeDtypeStruct((1, D_CONV), F32),
            jax.ShapeDtypeStruct((1, D_CONV), F32),
            jax.ShapeDtypeStruct((1, D_CONV), F32),
        ],
        scratch_shapes=[pltpu.VMEM((CONV_HALO + tm, D_CONV), F32), pltpu.VMEM((tm + CONV_HALO, D_CONV), F32),
                        pltpu.VMEM((_SUBLANES, CONV_HALO + tm, D_CONV), F32),
                        pltpu.VMEM((_SUBLANES, CONV_HALO + tm, D_CONV), F32)],
        compiler_params=_params(40, 1),
    )(dc, yc, dc, yc, ag, ag, w32, lg, lb)


def _outproj_fwd(x1, c, o, gc, ga, wout, name):
    T, D = x1.shape
    tm = _tile(T, 512)

    def body(x_ref, c_ref, o_ref, gc_ref, ga_ref, w_ref, x2_ref):
        yc, _ = _rms_fwd(c_ref[...], gc_ref[...])
        ya, _ = _rms_fwd(o_ref[...], ga_ref[...])
        x2_ref[...] = (x_ref[...] + _mm(yc.astype(MXU), w_ref[:D_CONV, :])
                       + _mm(ya.astype(MXU), w_ref[D_CONV:, :]))

    row = lambda w: pl.BlockSpec((tm, w), lambda i: (i, 0))
    full = lambda a, b: pl.BlockSpec((a, b), lambda i: (0, 0))
    return _pallas_call(
        body, name=name, grid=(T // tm,),
        in_specs=[row(D), row(D_CONV), row(D_ATTN), full(1, D_CONV), full(1, D_ATTN), full(D_CONV + D_ATTN, D)],
        out_specs=row(D),
        out_shape=jax.ShapeDtypeStruct((T, D), F32),
        compiler_params=_params(32, 1),
    )(x1, c, o, gc, ga, wout)


def _outproj_bwd(dx2, c, o, gc, ga, wout, name):
    T, D = dx2.shape
    tm = _tile(T, 256)
    I = T // tm

    def body(dx_ref, c_ref, o_ref, gc_ref, ga_ref, w_ref,
             dc_ref, doT_ref, dlT_ref, dw_ref, dgc_ref, dga_ref, acc_s):
        i = pl.program_id(0)
        dxb = dx_ref[...].astype(MXU)
        cv = c_ref[...]
        ov = o_ref[...]
        yc, rcn = _rms_fwd(cv, gc_ref[...])
        ya, ra = _rms_fwd(ov, ga_ref[...])
        dyc = _mm_nt(dxb, w_ref[:D_CONV, :])
        dya = _mm_nt(dxb, w_ref[D_CONV:, :])
        dwc = _mm_tn(yc.astype(MXU), dxb)
        dwa = _mm_tn(ya.astype(MXU), dxb)
        dcv, dgc = _rms_bwd(cv, rcn, gc_ref[...], dyc)
        dov, dga = _rms_bwd(ov, ra, ga_ref[...], dya)
        dc_ref[...] = dcv
        dob = dov.astype(doT_ref.dtype)
        doT_ref[...] = dov.T.astype(doT_ref.dtype)
        chan = lax.broadcasted_iota(jnp.int32, (D_ATTN, LANES), 0)
        head = lax.broadcasted_iota(jnp.int32, (D_ATTN, LANES), 1)
        in_head = ((chan >= head * HEAD_DIM) & (chan < (head + 1) * HEAD_DIM)).astype(jnp.bfloat16)
        dlT_ref[...] = _exact_dot_01(dob.astype(F32) * ov, in_head).T[:N_HEADS, :]

        @pl.when(i == 0)
        def _():
            acc_s[:D_CONV, :] = dwc
            acc_s[D_CONV:, :] = dwa
            dgc_ref[...] = dgc
            dga_ref[...] = dga

        @pl.when(i > 0)
        def _():
            acc_s[:D_CONV, :] += dwc
            acc_s[D_CONV:, :] += dwa
            dgc_ref[...] += dgc
            dga_ref[...] += dga

        @pl.when(i == I - 1)
        def _():
            dw_ref[...] = acc_s[...].astype(dw_ref.dtype)

    row = lambda w: pl.BlockSpec((tm, w), lambda i: (i, 0))
    full = lambda a, b: pl.BlockSpec((a, b), lambda i: (0, 0))
    return _pallas_call(
        body, name=name, grid=(I,),
        in_specs=[row(D), row(D_CONV), row(D_ATTN), full(1, D_CONV), full(1, D_ATTN), full(D_CONV + D_ATTN, D)],
        out_specs=[row(D_CONV), pl.BlockSpec((D_ATTN, tm), lambda i: (0, i)),
                   pl.BlockSpec((N_HEADS, tm), lambda i: (0, i)),
                   full(D_CONV + D_ATTN, D), full(1, D_CONV), full(1, D_ATTN)],
        out_shape=[
            jax.ShapeDtypeStruct((T, D_CONV), F32),
            jax.ShapeDtypeStruct((D_ATTN, T), MXU),
            jax.ShapeDtypeStruct((N_HEADS, T), F32),
            jax.ShapeDtypeStruct((D_CONV + D_ATTN, D), MXU),
            jax.ShapeDtypeStruct((1, D_CONV), F32),
            jax.ShapeDtypeStruct((1, D_ATTN), F32),
        ],
        scratch_shapes=[pltpu.VMEM((D_CONV + D_ATTN, D), F32)],
        compiler_params=_params(40, 1),
    )(dx2, c, o, gc, ga, wout)


def _loss_head(x3, gf, target, name):
    T, D = x3.shape
    tm = _tile(T, 512)

    def body(x_ref, g_ref, t_ref, loss_ref, dx_ref, dg_ref):
        i = pl.program_id(0)
        xv = x_ref[...]
        gv = g_ref[...]
        out, r = _rms_fwd(xv, gv)
        err = out - t_ref[...]
        part = jnp.full((1, LANES), 0.5 / D, F32) * jnp.sum(err * err)
        dxn, dgp = _rms_bwd(xv, r, gv, err * (1.0 / D))
        dx_ref[...] = dxn

        @pl.when(i == 0)
        def _():
            loss_ref[...] = part
            dg_ref[...] = dgp

        @pl.when(i > 0)
        def _():
            loss_ref[...] += part
            dg_ref[...] += dgp

    row = lambda w: pl.BlockSpec((tm, w), lambda i: (i, 0))
    full = lambda a, b: pl.BlockSpec((a, b), lambda i: (0, 0))
    return _pallas_call(
        body, name=name, grid=(T // tm,),
        in_specs=[row(D), full(1, D), row(D)],
        out_specs=[full(1, LANES), row(D), full(1, D)],
        out_shape=[jax.ShapeDtypeStruct((1, LANES), F32), jax.ShapeDtypeStruct((T, D), F32),
                   jax.ShapeDtypeStruct((1, D), F32)],
        compiler_params=_params(32, 1),
    )(x3, gf, target)


def _row_tile(rows):
    for cand in (256, 176, 128, 64, 32, 16):
        if rows % cand == 0:
            return cand
    return rows


def _adamw(w, m, v, parts, name):
    R, C = w.shape
    P = parts.shape[0]
    tr = _row_tile(R)
    c1 = 1.0 - ADAM_B1 ** ADAM_STEP
    c2 = 1.0 - ADAM_B2 ** ADAM_STEP

    def body(w_ref, m_ref, v_ref, p_ref, g_ref, d_ref, nm_ref, nv_ref):
        g = p_ref[0].astype(F32)
        for s in range(1, P):
            g = g + p_ref[s].astype(F32)
        wv = w_ref[...]
        mn = ADAM_B1 * m_ref[...] + (1.0 - ADAM_B1) * g
        vn = ADAM_B2 * v_ref[...] + (1.0 - ADAM_B2) * (g * g)
        g_ref[...] = g
        nm_ref[...] = mn
        nv_ref[...] = vn
        d_ref[...] = -ADAM_LR * ((mn / c1) / (jnp.sqrt(vn / c2) + ADAM_EPS) + ADAM_WD * wv)

    blk = pl.BlockSpec((tr, C), lambda i: (i, 0))
    out = jax.ShapeDtypeStruct((R, C), F32)
    return _pallas_call(
        body, name=name, grid=(R // tr,),
        in_specs=[blk, blk, blk, pl.BlockSpec((P, tr, C), lambda i: (0, i, 0))],
        out_specs=[blk, blk, blk, blk],
        out_shape=[out, out, out, out],
        compiler_params=_params(32, 1),
    )(w, m, v, parts)


def _position():
    return lax.axis_index("x"), lax.axis_index("y"), lax.axis_index("c")


def _flat(px, py, pc):
    return 4 * px + 2 * py + pc


def _gather_body(ins, outs, send_sems, recv_sems, local_sems, handshake):
    n = len(ins)
    x, y, c = _position()
    me, sibling = (x, y, c), (x, y, 1 - c)
    chips = [(1 - x, y), (x, 1 - y), (1 - x, 1 - y)]
    if handshake:
        _handshake([sibling] + [(*chip, cc) for chip in chips for cc in (c, 1 - c)])

    def copy(a, k, block, to, src=None):
        dst = outs[a].at[_flat(*block)]
        return pltpu.make_async_remote_copy(
            src_ref=dst if src is None else src, dst_ref=dst,
            send_sem=send_sems.at[a, k], recv_sem=recv_sems.at[a, k],
            device_id=to, device_id_type=MESH)

    mine = [pltpu.make_async_copy(ins[a], outs[a].at[_flat(*me)], local_sems.at[a]) for a in range(n)]
    for cp in mine:
        cp.start()
    first = []
    for a in range(n):
        first.append(copy(a, 0, me, sibling, src=ins[a]))
        first += [copy(a, 1 + j, me, (*chip, c), src=ins[a]) for j, chip in enumerate(chips)]
    for cp in first:
        cp.start()
    passed = []
    for a in range(n):
        for j, chip in enumerate(chips):
            copy(a, 1 + j, (*chip, c), me).wait_recv()
            fwd = copy(a, 4 + j, (*chip, c), sibling)
            fwd.start()
            passed.append(fwd)
    for a in range(n):
        copy(a, 0, sibling, me).wait_recv()
        for j, chip in enumerate(chips):
            copy(a, 4 + j, (*chip, 1 - c), me).wait_recv()
    for cp in first + passed:
        cp.wait_send()
    for cp in mine:
        cp.wait()


def _gather_scratch(n):
    return [pltpu.SemaphoreType.DMA((n, 7)), pltpu.SemaphoreType.DMA((n, 7)), pltpu.SemaphoreType.DMA((n,))]


def _all_gather(shards, name):
    n = len(shards)

    def body(*refs):
        _gather_body(refs[:n], refs[n:2 * n], *refs[2 * n:], handshake=False)

    hbm = pl.BlockSpec(memory_space=pltpu.HBM)
    return _pallas_call(
        body, name=name,
        in_specs=[hbm] * n, out_specs=[hbm] * n,
        out_shape=[jax.ShapeDtypeStruct((N_DEV,) + s.shape, s.dtype) for s in shards],
        scratch_shapes=_gather_scratch(n),
    )(*shards)


def _handshake(peers):
    barrier = pltpu.get_barrier_semaphore()
    for peer in peers:
        pl.semaphore_signal(barrier, inc=1, device_id=peer, device_id_type=MESH)
    pl.semaphore_wait(barrier, len(peers))


def _sequencer_call(body, name, collective_id, out_type, scratch_types, operands):
    return pl.kernel(
        body, name=name, out_type=out_type,
        mesh=plsc.ScalarSubcoreMesh(axis_name="sequencer", num_cores=1),
        scratch_types=scratch_types,
        compiler_params=pltpu.CompilerParams(collective_id=collective_id),
    )(*operands)


def _seq_all_gather(shards, name, collective_id, after):
    n = len(shards)

    def body(*refs):
        _gather_body(refs[:n], refs[n + 1:2 * n + 1], *refs[2 * n + 1:], handshake=True)

    return _sequencer_call(
        body, name, collective_id,
        [jax.ShapeDtypeStruct((N_DEV,) + s.shape, s.dtype) for s in shards],
        _gather_scratch(n), list(shards) + [after])


def _seq_to_sibling(parts, name, collective_id, after):
    n = len(parts)

    def body(*refs):
        ins, outs = refs[:n], refs[n + len(after):2 * n + len(after)]
        send_sems, recv_sems = refs[2 * n + len(after):]
        x, y, c = _position()
        sibling = (x, y, 1 - c)
        _handshake([sibling])
        sent = []
        for a in range(n):
            for q in range(N_CHIPS):
                cp = pltpu.make_async_remote_copy(
                    src_ref=ins[a].at[2 * q + (1 - c)], dst_ref=outs[a].at[q],
                    send_sem=send_sems.at[a, q], recv_sem=recv_sems.at[a, q],
                    device_id=sibling, device_id_type=MESH)
                cp.start()
                sent.append(cp)
        for cp in sent:
            cp.wait_recv()
        for cp in sent:
            cp.wait_send()

    return _sequencer_call(
        body, name, collective_id,
        [jax.ShapeDtypeStruct((N_CHIPS,) + p.shape[1:], p.dtype) for p in parts],
        [pltpu.SemaphoreType.DMA((n, N_CHIPS)), pltpu.SemaphoreType.DMA((n, N_CHIPS))],
        list(parts) + list(after))


def _seq_to_chips(partials, name, collective_id, after=()):
    n = len(partials)

    def body(*refs):
        ins, outs = refs[:n], refs[n + len(after):2 * n + len(after)]
        send_sems, recv_sems, local_sems = refs[2 * n + len(after):]
        x, y, c = _position()
        my_chip = 2 * x + y
        chips = [(1 - x, y), (x, 1 - y), (1 - x, 1 - y)]
        _handshake([(*chip, c) for chip in chips])
        mine = [pltpu.make_async_copy(ins[a].at[my_chip], outs[a].at[my_chip], local_sems.at[a]) for a in range(n)]
        for cp in mine:
            cp.start()
        sent = []
        for a in range(n):
            for j, (px, py) in enumerate(chips):
                cp = pltpu.make_async_remote_copy(
                    src_ref=ins[a].at[2 * px + py], dst_ref=outs[a].at[my_chip],
                    send_sem=send_sems.at[a, j], recv_sem=recv_sems.at[a, j],
                    device_id=(px, py, c), device_id_type=MESH)
                cp.start()
                sent.append(cp)
        for cp in sent:
            cp.wait_recv()
        for cp in sent:
            cp.wait_send()
        for cp in mine:
            cp.wait()

    return _sequencer_call(
        body, name, collective_id,
        [jax.ShapeDtypeStruct(p.shape, p.dtype) for p in partials],
        [pltpu.SemaphoreType.DMA((n, 3)), pltpu.SemaphoreType.DMA((n, 3)), pltpu.SemaphoreType.DMA((n,))],
        list(partials) + list(after))


def _pair_add(part, recv, name, after=()):
    _, R, C = part.shape
    core = lax.axis_index("c").astype(jnp.int32).reshape(1)

    def body(c_ref, p_ref, r_ref, *rest):
        o_ref = rest[len(after)]
        o_ref[...] = (p_ref[...].astype(F32) + r_ref[...].astype(F32)).astype(o_ref.dtype)

    blk = pl.BlockSpec((None, R, C), lambda q, c: (q, 0, 0))
    return pl.pallas_call(
        body, name=name,
        grid_spec=pltpu.PrefetchScalarGridSpec(
            num_scalar_prefetch=1, grid=(N_CHIPS,),
            in_specs=[pl.BlockSpec((None, R, C), lambda q, c: (2 * q + c[0], 0, 0)), blk] + [_UNREAD] * len(after),
            out_specs=blk),
        out_shape=pltpu.HBM((N_CHIPS, R, C), part.dtype),
        compiler_params=_params(32, 1),
    )(core, *[pltpu.with_memory_space_constraint(a, pltpu.HBM) for a in (part, recv, *after)])


class _Reduced(NamedTuple):
    partials: list
    reduced: list


def _blocks(g):
    return g.reshape(N_DEV, -1, g.shape[-1])


def _reduce_scatter(parts, tag, ids, after=(), between=None, add_after=()):
    from_sibling = _seq_to_sibling(parts, "rs_sibling_" + tag, ids[0], after)
    mid = between(from_sibling[0]) if between else ()
    partials = [_pair_add(p, r, "rs_add_%s_%d" % (tag, a), add_after)
                for a, (p, r) in enumerate(zip(parts, from_sibling))]
    return _Reduced(partials, _seq_to_chips(partials, "rs_chips_" + tag, ids[1], mid))


_SMALL = ("ffn1_norm", "mix_norm", "conv_b", "conv_ln_g", "conv_ln_b", "forget_b", "out_norm_conv",
          "out_norm_attn", "ffn2_norm", "final_norm")
_PACK_WIDTH = 2 * D_CONV
_SLOT = dict(ffn1_norm=(0, 0), mix_norm=(1, 0), ffn2_norm=(2, 0), final_norm=(3, 0), conv_b=(4, 0),
             conv_ln_g=(4, D_CONV), conv_ln_b=(5, 0), out_norm_conv=(5, D_CONV), out_norm_attn=(6, 0),
             forget_b=(6, D_CONV))
_CONV_ROW0 = 8
_PACK_ROWS = _CONV_ROW0 + CONV_HALO


def _pack_small(small, name):
    arrays = [small[n] for n in _SMALL] + [small["conv_w"]]

    def body(*refs):
        out = refs[-1]
        out[...] = jnp.zeros_like(out)
        for n, ref in zip(_SMALL, refs):
            row, lane = _SLOT[n]
            out[row:row + 1, lane:lane + ref.shape[1]] = ref[...]
        out[_CONV_ROW0:, :D_CONV] = refs[len(_SMALL)][...]

    return _pallas_call(body, name=name, out_shape=jax.ShapeDtypeStruct((_PACK_ROWS, _PACK_WIDTH), F32))(*arrays)


def _adamw_small(gathered, w, m, v, name):
    c1 = 1.0 - ADAM_B1 ** ADAM_STEP
    c2 = 1.0 - ADAM_B2 ** ADAM_STEP
    k = len(_SMALL)

    def body(g_ref, *refs):
        ws, ms, vs = refs[:k], refs[k:2 * k], refs[2 * k:3 * k]
        outs = refs[3 * k:]
        total = g_ref[0]
        for s in range(1, N_DEV):
            total = total + g_ref[s]
        for i, n in enumerate(_SMALL):
            row, lane = _SLOT[n]
            width = ws[i].shape[1]
            g = total[row:row + 1, lane:lane + width]
            mn = ADAM_B1 * ms[i][...] + (1.0 - ADAM_B1) * g
            vn = ADAM_B2 * vs[i][...] + (1.0 - ADAM_B2) * (g * g)
            o_g, o_d, o_m, o_v = outs[4 * i:4 * i + 4]
            o_g[...] = g
            o_m[...] = mn
            o_v[...] = vn
            o_d[...] = -ADAM_LR * ((mn / c1) / (jnp.sqrt(vn / c2) + ADAM_EPS) + ADAM_WD * ws[i][...])
        outs[4 * k][...] = total[_CONV_ROW0:, :D_CONV]

    shapes = []
    for n in _SMALL:
        shapes += [jax.ShapeDtypeStruct(w[n].shape, F32)] * 4
    shapes.append(jax.ShapeDtypeStruct((CONV_HALO, D_CONV), F32))
    res = _pallas_call(body, name=name, out_shape=shapes)(
        gathered, *[w[n] for n in _SMALL], *[m[n] for n in _SMALL], *[v[n] for n in _SMALL])
    return {n: res[4 * i:4 * i + 4] for i, n in enumerate(_SMALL)}, res[4 * k]


def _local_step(x, target, norms, shard):
    D = x.shape[1]
    J = N_DEV // 2
    as13 = lambda g: g.reshape(2, J, g.shape[1], D)

    (g13_1,) = _all_gather([shard["ffn1_w13"]], "gather_ffn1_w13")
    (g2_1,) = _seq_all_gather([shard["ffn1_w2"]], "gather_ffn1_w2", 10, after=g13_1)
    w13_1 = as13(g13_1)
    G1, U1, A1 = _ffn_up(x, norms["ffn1_norm"], w13_1, "ffn1_up")
    gin, gout, gconv = _seq_all_gather([shard["w_in"], shard["w_out"], shard["conv_w"]], "gather_mix", 1, after=G1)
    w2_1 = g2_1.reshape(-1, D)
    x1 = _ffn_down(x, A1, w2_1, "ffn1_down")
    g13_2, g2_2 = _seq_all_gather([shard["ffn2_w13"], shard["ffn2_w2"]], "gather_ffn2", 2, after=x1)
    winp = jnp.pad(gin.reshape(N_IN, D), ((0, N_IN_PAD - N_IN), (0, 0)))
    wout = gout.reshape(-1, D)
    conv_w32 = jnp.pad(gconv.transpose(1, 0, 2).reshape(CONV_TAPS, D_CONV), ((0, CONV_HALO - CONV_TAPS), (0, 0)))

    ag, k, v, qT, kT, vT, fl = _inproj_fwd(x1, norms["mix_norm"], winp, "inproj_fwd")
    cum, cumT = _forget_fwd(fl, norms["forget_b"], "forget_fwd")
    yc, c = _conv_fwd(ag, conv_w32, norms["conv_b"], norms["conv_ln_g"], norms["conv_ln_b"], "conv_fwd")
    o, lseT = _attn_fwd(qT, k, vT, cum, cumT, "attn_fwd")
    x2 = _outproj_fwd(x1, c, o, norms["out_norm_conv"], norms["out_norm_attn"], wout, "outproj_fwd")
    w13_2, w2_2 = as13(g13_2), g2_2.reshape(-1, D)
    G2, U2, A2 = _ffn_up(x2, norms["ffn2_norm"], w13_2, "ffn2_up")
    x3 = _ffn_down(x2, A2, w2_2, "ffn2_down")
    loss, dx3, d_final = _loss_head(x3, norms["final_norm"], target, "loss_head")

    dw2_2 = _ffn_w2_grad(dx3, A2, "ffn2_w2_grad")
    dx2, d_ffn2n, h3, dG2, dU2 = _ffn_bwd_act(x2, norms["ffn2_norm"], dx3, G2, U2, w13_2, w2_2, "ffn2_bwd_act")
    dw13_2 = _ffn_w13_grad(h3, dG2, dU2, "ffn2_w13_grad")
    dc, dobT, deltaT, dwout, d_onc, d_ona = _outproj_bwd(
        dx2, c, o, norms["out_norm_conv"], norms["out_norm_attn"], wout, "outproj_bwd")
    red_ffn2 = _reduce_scatter([_blocks(dw13_2), _blocks(dw2_2)], "ffn2", (3, 4), add_after=(dc,))
    dqT, dkT, dvT, dcum = _attn_bwd(qT, k, kT, v, dobT, lseT, deltaT, cum, cumT, "attn_bwd",
                                    after=red_ffn2.partials)
    dfl, d_fb = _forget_bwd(dcum, fl, norms["forget_b"], "forget_bwd")
    dag, d_convw, d_cb, d_lg, d_lb = _conv_bwd(dc, yc, ag, conv_w32, norms["conv_ln_g"], norms["conv_ln_b"], "conv_bwd")
    dx1, d_mixn, h2 = _inproj_bwd_act(x1, norms["mix_norm"], dx2, dag, dqT, dkT, dvT, dfl, winp, "inproj_bwd_act")
    dw2_1 = _ffn_w2_grad(dx1, A1, "ffn1_w2_grad")
    red_w2_1 = _reduce_scatter([_blocks(dw2_1)], "ffn1_w2", (11, 12), after=red_ffn2.reduced[:1])
    dwinp = _inproj_bwd_weights(h2, dag, dqT, dkT, dvT, dfl, "inproj_bwd_weights", after=red_w2_1.partials)
    dwin_blocks = dwinp[:N_IN].reshape(N_DEV, N_IN // N_DEV, -1)
    red_mix = _reduce_scatter([dwin_blocks, _blocks(dwout)], "mix", (5, 6), after=red_w2_1.reduced[:1])
    dx, d_ffn1n, h1, dG1, dU1 = _ffn_bwd_act(x, norms["ffn1_norm"], dx1, G1, U1, w13_1, w2_1, "ffn1_bwd_act",
                                             after=red_mix.partials)
    dw13_1 = _ffn_w13_grad(h1, dG1, dU1, "ffn1_w13_grad")

    small = dict(ffn1_norm=d_ffn1n, mix_norm=d_mixn, conv_b=d_cb, conv_ln_g=d_lg, conv_ln_b=d_lb,
                 forget_b=d_fb, out_norm_conv=d_onc, out_norm_attn=d_ona, ffn2_norm=d_ffn2n,
                 final_norm=d_final, conv_w=d_convw)
    packed_small = _pack_small(small, "pack_small_grads")
    gathered_small = []

    def gather_small(behind):
        gathered_small.extend(_seq_all_gather([packed_small], "gather_small_grads", 9, after=behind))
        return gathered_small

    red_w13_1 = _reduce_scatter([_blocks(dw13_1)], "ffn1_w13", (7, 8), after=red_mix.reduced[:1],
                                between=gather_small)
    big = dict(ffn1_w13=red_w13_1.reduced[0], ffn1_w2=red_w2_1.reduced[0], w_in=red_mix.reduced[0],
               w_out=red_mix.reduced[1], ffn2_w13=red_ffn2.reduced[0], ffn2_w2=red_ffn2.reduced[1])
    return loss[0, 0], dx, gathered_small[0], big


_BIG = ("ffn1_w13", "ffn1_w2", "w_in", "w_out", "ffn2_w13", "ffn2_w2")
_TRANSPOSED = ("ffn1_w13", "ffn2_w13", "w_in")
_ORDER = ("ffn1_norm", "ffn1_w13", "ffn1_w2", "mix_norm", "w_in", "conv_w", "conv_b", "conv_ln_g", "conv_ln_b",
          "forget_b", "out_norm_conv", "out_norm_attn", "w_out", "ffn2_norm", "ffn2_w13", "ffn2_w2", "final_norm")


def kernel(x, ffn1_norm, ffn1_w13, ffn1_w2, mix_norm, w_in, conv_w, conv_b, conv_ln_g, conv_ln_b, forget_b, out_norm_conv, out_norm_attn, w_out, ffn2_norm, ffn2_w13, ffn2_w2, final_norm, loss_target, m_ffn1_norm, m_ffn1_w13, m_ffn1_w2, m_mix_norm, m_w_in, m_conv_w, m_conv_b, m_conv_ln_g, m_conv_ln_b, m_forget_b, m_out_norm_conv, m_out_norm_attn, m_w_out, m_ffn2_norm, m_ffn2_w13, m_ffn2_w2, m_final_norm, v_ffn1_norm, v_ffn1_w13, v_ffn1_w2, v_mix_norm, v_w_in, v_conv_w, v_conv_b, v_conv_ln_g, v_conv_ln_b, v_forget_b, v_out_norm_conv, v_out_norm_attn, v_w_out, v_ffn2_norm, v_ffn2_w13, v_ffn2_w2, v_final_norm):
    w = dict(ffn1_norm=ffn1_norm, ffn1_w13=ffn1_w13, ffn1_w2=ffn1_w2, mix_norm=mix_norm, w_in=w_in, conv_w=conv_w,
             conv_b=conv_b, conv_ln_g=conv_ln_g, conv_ln_b=conv_ln_b, forget_b=forget_b, out_norm_conv=out_norm_conv,
             out_norm_attn=out_norm_attn, w_out=w_out, ffn2_norm=ffn2_norm, ffn2_w13=ffn2_w13, ffn2_w2=ffn2_w2,
             final_norm=final_norm)
    m = dict(ffn1_norm=m_ffn1_norm, ffn1_w13=m_ffn1_w13, ffn1_w2=m_ffn1_w2, mix_norm=m_mix_norm, w_in=m_w_in,
             conv_w=m_conv_w, conv_b=m_conv_b, conv_ln_g=m_conv_ln_g, conv_ln_b=m_conv_ln_b, forget_b=m_forget_b,
             out_norm_conv=m_out_norm_conv, out_norm_attn=m_out_norm_attn, w_out=m_w_out, ffn2_norm=m_ffn2_norm,
             ffn2_w13=m_ffn2_w13, ffn2_w2=m_ffn2_w2, final_norm=m_final_norm)
    v = dict(ffn1_norm=v_ffn1_norm, ffn1_w13=v_ffn1_w13, ffn1_w2=v_ffn1_w2, mix_norm=v_mix_norm, w_in=v_w_in,
             conv_w=v_conv_w, conv_b=v_conv_b, conv_ln_g=v_conv_ln_g, conv_ln_b=v_conv_ln_b, forget_b=v_forget_b,
             out_norm_conv=v_out_norm_conv, out_norm_attn=v_out_norm_attn, w_out=v_w_out, ffn2_norm=v_ffn2_norm,
             ffn2_w13=v_ffn2_w13, ffn2_w2=v_ffn2_w2, final_norm=v_final_norm)
    shapes = {n: a.shape for n, a in w.items()}
    T, D = x.shape[1], x.shape[2]
    def two(n, a):
        if a.ndim != 3:
            return a.reshape(1, -1)
        a = a.reshape(a.shape[-2], a.shape[-1])
        return a.T if n in _TRANSPOSED else a

    w2d = {n: two(n, a) for n, a in w.items()}
    m2d = {n: two(n, a) for n, a in m.items()}
    v2d = {n: two(n, a) for n, a in v.items()}

    shard = {n: w2d[n].astype(MXU) for n in _BIG}
    shard["conv_w"] = w2d["conv_w"]
    norms = {n: w2d[n] for n in _SMALL}
    norms["forget_b"] = jnp.pad(w2d["forget_b"], ((0, 0), (0, LANES - N_HEADS)))
    loss_part, dx, gathered_small, big = _local_step(x[0], loss_target[0], norms, shard)
    loss = lax.psum(loss_part, ("x", "y", "c"))

    grads, deltas, new_m, new_v = {}, {}, {}, {}
    for n in _BIG:
        g, d, nm, nv = _adamw(w2d[n], m2d[n], v2d[n], big[n], "adamw_" + n)
        grads[n], deltas[n], new_m[n], new_v[n] = g, d, nm, nv

    small_out, conv_g_full = _adamw_small(gathered_small, w2d, m2d, v2d, "adamw_small")
    for n in _SMALL:
        grads[n], deltas[n], new_m[n], new_v[n] = small_out[n]
    conv_g_full = conv_g_full[:CONV_TAPS]
    xi, yi, ci = _position()
    cw = shapes["conv_w"][-1]
    conv_g_mine = lax.dynamic_slice_in_dim(conv_g_full, _flat(xi, yi, ci) * cw, cw, axis=1)
    g, d, nm, nv = _adamw(w2d["conv_w"], m2d["conv_w"], v2d["conv_w"], conv_g_mine[None], "adamw_conv_w")
    grads["conv_w"], deltas["conv_w"], new_m["conv_w"], new_v["conv_w"] = g, d, nm, nv

    shaped = lambda dct: [(dct[n].T if n in _TRANSPOSED else dct[n]).reshape(shapes[n]) for n in _ORDER]
    return (loss, dx[None], *shaped(grads), *shaped(deltas), *shaped(new_m), *shaped(new_v))
```

```python
from typing import NamedTuple

import jax
import jax.numpy as jnp
from jax import lax
from jax.experimental import pallas as pl
from jax.experimental.pallas import tpu as pltpu
from jax.experimental.pallas import tpu_sc as plsc

F32 = jnp.float32
MXU = jnp.bfloat16
EPS = 1e-6
N_HEADS = 8
HEAD_DIM = 64
D_CONV = 512
D_ATTN = N_HEADS * HEAD_DIM
CONV_TAPS = 31
CONV_HALO = 32
SCALE = HEAD_DIM ** -0.5
NEG = -1e30
LANES = 128
N_DEV = 8
N_CHIPS = N_DEV // 2
MESH = pl.DeviceIdType.MESH
MIB = 1 << 20

ADAM_LR = 0.001
ADAM_B1 = 0.9
ADAM_B2 = 0.999
ADAM_EPS = 1e-08
ADAM_WD = 0.01
ADAM_STEP = 10


_UNREAD = pl.BlockSpec(memory_space=pl.ANY)


def _pallas_call(body, *, out_shape, **kwargs):
    in_hbm = lambda s: pltpu.HBM(s.shape, s.dtype)
    outs = [in_hbm(s) for s in out_shape] if isinstance(out_shape, (list, tuple)) else in_hbm(out_shape)
    call = pl.pallas_call(body, out_shape=outs, **kwargs)
    return lambda *operands: call(*[pltpu.with_memory_space_constraint(a, pltpu.HBM) for a in operands])


def _params(vmem_mib, n_axes):
    return pltpu.CompilerParams(dimension_semantics=("arbitrary",) * n_axes, vmem_limit_bytes=vmem_mib * MIB)


def _mm(a, b):
    return jnp.dot(a, b, preferred_element_type=F32)


def _mm_nt(a, b):
    return lax.dot_general(a, b, (((1,), (1,)), ((), ())), preferred_element_type=F32)


def _mm_tn(a, b):
    return lax.dot_general(a, b, (((0,), (0,)), ((), ())), preferred_element_type=F32)


def _rms_fwd(x, g):
    r = lax.rsqrt(jnp.mean(x * x, axis=-1, keepdims=True) + EPS)
    return x * r * g, r


def _rms_bwd(x, r, g, dy):
    gdy = dy * g
    dx = r * gdy - x * (r * r * r) * jnp.mean(x * gdy, axis=-1, keepdims=True)
    dg = jnp.sum(dy * x * r, axis=0, keepdims=True)
    return dx, dg


def _silu_grad(z, sz):
    return sz * (1.0 + z * (1.0 - sz))


def _three_terms(x):
    x1 = x.astype(jnp.bfloat16)
    r1 = x - x1.astype(F32)
    x2 = r1.astype(jnp.bfloat16)
    x3 = (r1 - x2.astype(F32)).astype(jnp.bfloat16)
    return x1, x2, x3


def _exact_tri_dot(tri, x):
    x1, x2, x3 = _three_terms(x)
    return _mm(tri, x1) + _mm(tri, x2) + _mm(tri, x3)


def _exact_dot_01(x, sel):
    x1, x2, x3 = _three_terms(x)
    return _mm(x1, sel) + _mm(x2, sel) + _mm(x3, sel)


def _tile(n, want):
    t = min(n, want)
    assert n % t == 0
    return t


_FFN_CHUNK = 256


def _ffn_up(x, g, w13, name):
    T, D = x.shape
    _, J, bf, _ = w13.shape
    tm = _tile(T, 512)
    I = T // tm

    def body(x_ref, g_ref, w13_ref, G_ref, U_ref, A_ref, h_s):
        j = pl.program_id(0)
        i = pl.program_id(1)
        rows = pl.ds(pl.multiple_of(i * tm, tm), tm)

        @pl.when(j == 0)
        def _():
            h, _ = _rms_fwd(x_ref[...], g_ref[...])
            h_s[rows, :] = h.astype(MXU)

        chunks = [slice(r0, r0 + _FFN_CHUNK) for r0 in range(0, tm, _FFN_CHUNK)]
        hbs = [h_s[pl.ds(pl.multiple_of(i * tm + rs.start, _FFN_CHUNK), _FFN_CHUNK), :] for rs in chunks]
        GU = [(_mm_nt(hb, w13_ref[0]), _mm_nt(hb, w13_ref[1])) for hb in hbs]
        for rs, (G, U) in zip(chunks, GU):
            G_ref[rs, :] = G.astype(MXU)
            U_ref[rs, :] = U.astype(MXU)
            A_ref[rs, :] = (G * jax.nn.sigmoid(G) * U).astype(MXU)

    blk = pl.BlockSpec((None, tm, bf), lambda j, i: (j, i, 0))
    hid = jax.ShapeDtypeStruct((J, T, bf), MXU)
    return _pallas_call(
        body, name=name, grid=(J, I),
        in_specs=[pl.BlockSpec((tm, D), lambda j, i: (jnp.where(j == 0, i, I - 1), 0)),
                  pl.BlockSpec((1, D), lambda j, i: (0, 0)),
                  pl.BlockSpec((2, None, bf, D), lambda j, i: (0, j, 0, 0))],
        out_specs=[blk, blk, blk],
        out_shape=[hid, hid, hid],
        scratch_shapes=[pltpu.VMEM((T, D), MXU)],
        compiler_params=_params(40, 2),
    )(x, g, w13)


def _ffn_down(x, A, w2, name):
    T, D = x.shape
    J, _, bf = A.shape
    tm = _tile(T, 512)

    def body(x_ref, A_ref, w2_ref, xo_ref):
        f = _mm(A_ref[0], w2_ref[0:bf, :])
        for j in range(1, J):
            f = f + _mm(A_ref[j], w2_ref[j * bf:(j + 1) * bf, :])
        xo_ref[...] = x_ref[...] + 0.5 * f

    row = pl.BlockSpec((tm, D), lambda i: (i, 0))
    return _pallas_call(
        body, name=name, grid=(T // tm,),
        in_specs=[row, pl.BlockSpec((J, tm, bf), lambda i: (0, i, 0)), pl.BlockSpec((J * bf, D), lambda i: (0, 0))],
        out_specs=row,
        out_shape=jax.ShapeDtypeStruct((T, D), F32),
        compiler_params=_params(48, 1),
    )(x, A, w2)


def _ffn_bwd_act(x, g, dy, Gs, Us, w13, w2, name, after=()):
    T, D = x.shape
    _, J, bf, _ = w13.shape
    tm = _tile(T, 512)
    I = T // tm

    def body(x_ref, g_ref, dy_ref, G_ref, U_ref, w13_ref, w2_ref, *rest):
        dx_ref, dg_ref, h_ref, dG_ref, dU_ref, dh_s, dF_s, h_s = rest[len(after):]
        j = pl.program_id(0)
        i = pl.program_id(1)
        rows = pl.ds(pl.multiple_of(i * tm, tm), tm)

        @pl.when(j == 0)
        def _():
            h, _ = _rms_fwd(x_ref[...], g_ref[...])
            hb = h.astype(MXU)
            h_s[rows, :] = hb
            h_ref[...] = hb
            dF_s[rows, :] = (0.5 * dy_ref[...]).astype(MXU)
            dh_s[rows, :] = jnp.zeros((tm, D), F32)

        chunks = [slice(r0, r0 + _FFN_CHUNK) for r0 in range(0, tm, _FFN_CHUNK)]
        crows = [pl.ds(pl.multiple_of(i * tm + rs.start, _FFN_CHUNK), _FFN_CHUNK) for rs in chunks]
        dAs = [_mm_nt(dF_s[cr, :], w2_ref[...]) for cr in crows]
        for rs, cr, dA in zip(chunks, crows, dAs):
            G = G_ref[rs, :].astype(F32)
            U = U_ref[rs, :].astype(F32)
            sg = jax.nn.sigmoid(G)
            s = G * sg
            dUb = (dA * s).astype(MXU)
            dGb = (dA * U * _silu_grad(G, sg)).astype(MXU)
            dG_ref[rs, :] = dGb
            dU_ref[rs, :] = dUb
            dh_s[cr, :] += _mm(dGb, w13_ref[0]) + _mm(dUb, w13_ref[1])

        @pl.when(j == J - 1)
        def _():
            xv = x_ref[...]
            gv = g_ref[...]
            _, r = _rms_fwd(xv, gv)
            dxn, dgp = _rms_bwd(xv, r, gv, dh_s[rows, :])
            dx_ref[...] = dy_ref[...] + dxn

            @pl.when(i == 0)
            def _():
                dg_ref[...] = dgp

            @pl.when(i > 0)
            def _():
                dg_ref[...] += dgp

    ends = lambda j, i: (jnp.where((j == 0) | (j == J - 1), i, I - 1), 0)
    blk = pl.BlockSpec((None, tm, bf), lambda j, i: (j, i, 0))
    hid = jax.ShapeDtypeStruct((J, T, bf), MXU)
    return _pallas_call(
        body, name=name, grid=(J, I),
        in_specs=[pl.BlockSpec((tm, D), ends), pl.BlockSpec((1, D), lambda j, i: (0, 0)), pl.BlockSpec((tm, D), ends),
                  blk, blk, pl.BlockSpec((2, None, bf, D), lambda j, i: (0, j, 0, 0)),
                  pl.BlockSpec((bf, D), lambda j, i: (j, 0))] + [_UNREAD] * len(after),
        out_specs=[pl.BlockSpec((tm, D), lambda j, i: (jnp.where(j == J - 1, i, 0), 0)),
                   pl.BlockSpec((1, D), lambda j, i: (0, 0)),
                   pl.BlockSpec((tm, D), lambda j, i: (jnp.where(j == 0, i, I - 1), 0)), blk, blk],
        out_shape=[jax.ShapeDtypeStruct((T, D), F32), jax.ShapeDtypeStruct((1, D), F32),
                   jax.ShapeDtypeStruct((T, D), MXU), hid, hid],
        scratch_shapes=[pltpu.VMEM((T, D), F32), pltpu.VMEM((T, D), MXU), pltpu.VMEM((T, D), MXU)],
        compiler_params=_params(58, 2),
    )(x, g, dy, Gs, Us, w13, w2, *after)


def _ffn_w13_grad(h, dG, dU, name):
    T, D = h.shape
    J, _, bf = dG.shape

    def body(h_ref, dG_ref, dU_ref, dw13_ref):
        dw13_ref[0] = _mm_tn(dG_ref[...], h_ref[...]).astype(dw13_ref.dtype)
        dw13_ref[1] = _mm_tn(dU_ref[...], h_ref[...]).astype(dw13_ref.dtype)

    blk = pl.BlockSpec((None, T, bf), lambda j: (j, 0, 0))
    return _pallas_call(
        body, name=name, grid=(J,),
        in_specs=[pl.BlockSpec((T, D), lambda j: (0, 0)), blk, blk],
        out_specs=pl.BlockSpec((2, None, bf, D), lambda j: (0, j, 0, 0)),
        out_shape=jax.ShapeDtypeStruct((2, J, bf, D), MXU),
        compiler_params=_params(48, 1),
    )(h, dG, dU)


def _ffn_w2_grad(dy, A, name, after=()):
    T, D = dy.shape
    J, _, bf = A.shape

    def body(dy_ref, A_ref, *rest):
        dw2_ref, dF_s = rest[len(after):]

        @pl.when(pl.program_id(0) == 0)
        def _():
            dF_s[...] = (0.5 * dy_ref[...]).astype(MXU)

        dw2_ref[...] = _mm_tn(A_ref[...], dF_s[...]).astype(dw2_ref.dtype)

    return _pallas_call(
        body, name=name, grid=(J,),
        in_specs=[pl.BlockSpec((T, D), lambda j: (0, 0)), pl.BlockSpec((None, T, bf), lambda j: (j, 0, 0))]
        + [_UNREAD] * len(after),
        out_specs=pl.BlockSpec((bf, D), lambda j: (j, 0)),
        out_shape=jax.ShapeDtypeStruct((J * bf, D), MXU),
        scratch_shapes=[pltpu.VMEM((T, D), MXU)],
        compiler_params=_params(48, 1),
    )(dy, A, *after)


_AG0, _Q0, _K0, _V0, _F0 = 0, 2 * D_CONV, 2 * D_CONV + D_ATTN, 2 * D_CONV + 2 * D_ATTN, 2 * D_CONV + 3 * D_ATTN
N_IN = _F0 + N_HEADS
N_IN_PAD = _F0 + LANES


def _inproj_fwd(x1, gm, winp, name):
    T, D = x1.shape
    tm = _tile(T, 256)

    def body(x_ref, g_ref, w_ref, ag_ref, k_ref, v_ref, qT_ref, kT_ref, vT_ref, fl_ref):
        h, _ = _rms_fwd(x_ref[...], g_ref[...])
        hb = h.astype(MXU)
        ag_ref[...] = _mm_nt(hb, w_ref[_AG0:_Q0, :])
        qT_ref[...] = (_mm_nt(hb, w_ref[_Q0:_K0, :]) * SCALE).T.astype(MXU)
        for c0, ref, refT in ((_K0, k_ref, kT_ref), (_V0, v_ref, vT_ref)):
            y = _mm_nt(hb, w_ref[c0:c0 + D_ATTN, :])
            ref[...] = y.astype(MXU)
            refT[...] = y.T.astype(MXU)
        fl_ref[...] = _mm_nt(hb, w_ref[_F0:N_IN_PAD, :])

    row = lambda w: pl.BlockSpec((tm, w), lambda i: (i, 0))
    col = pl.BlockSpec((D_ATTN, tm), lambda i: (0, i))
    std = jax.ShapeDtypeStruct((T, D_ATTN), MXU)
    trn = jax.ShapeDtypeStruct((D_ATTN, T), MXU)
    return _pallas_call(
        body, name=name, grid=(T // tm,),
        in_specs=[row(D), pl.BlockSpec((1, D), lambda i: (0, 0)), pl.BlockSpec((N_IN_PAD, D), lambda i: (0, 0))],
        out_specs=[row(2 * D_CONV), row(D_ATTN), row(D_ATTN), col, col, col, row(LANES)],
        out_shape=[jax.ShapeDtypeStruct((T, 2 * D_CONV), F32), std, std, trn, trn, trn,
                   jax.ShapeDtypeStruct((T, LANES), F32)],
        compiler_params=_params(40, 1),
    )(x1, gm, winp)


def _inproj_bwd_act(x1, gm, dx2, dag, dqT, dkT, dvT, dfl, winp, name):
    T, D = x1.shape
    tm = _tile(T, 256)

    def body(x_ref, g_ref, dx2_ref, dag_ref, dqT_ref, dkT_ref, dvT_ref, dfl_ref, w_ref, dx1_ref, dg_ref, h_ref):
        i = pl.program_id(0)
        xv = x_ref[...]
        gv = g_ref[...]
        h, r = _rms_fwd(xv, gv)
        h_ref[...] = h.astype(MXU)
        dh = _mm(dag_ref[...], w_ref[_AG0:_Q0, :])
        for c0, ref in ((_Q0, dqT_ref), (_K0, dkT_ref), (_V0, dvT_ref)):
            dh = dh + _mm_tn(ref[...].astype(MXU), w_ref[c0:c0 + D_ATTN, :])
        dh = dh + _mm(dfl_ref[...].astype(MXU), w_ref[_F0:N_IN_PAD, :])
        dxn, dgp = _rms_bwd(xv, r, gv, dh)
        dx1_ref[...] = dx2_ref[...] + dxn

        @pl.when(i == 0)
        def _():
            dg_ref[...] = dgp

        @pl.when(i > 0)
        def _():
            dg_ref[...] += dgp

    row = lambda w: pl.BlockSpec((tm, w), lambda i: (i, 0))
    col = pl.BlockSpec((D_ATTN, tm), lambda i: (0, i))
    full = lambda a, b: pl.BlockSpec((a, b), lambda i: (0, 0))
    return _pallas_call(
        body, name=name, grid=(T // tm,),
        in_specs=[row(D), full(1, D), row(D), row(2 * D_CONV), col, col, col, row(LANES), full(N_IN_PAD, D)],
        out_specs=[row(D), full(1, D), row(D)],
        out_shape=[jax.ShapeDtypeStruct((T, D), F32), jax.ShapeDtypeStruct((1, D), F32),
                   jax.ShapeDtypeStruct((T, D), MXU)],
        compiler_params=_params(40, 1),
    )(x1, gm, dx2, dag, dqT, dkT, dvT, dfl, winp)


def _inproj_bwd_weights(h, dag, dqT, dkT, dvT, dfl, name, after=()):
    T, D = h.shape

    def body(h_ref, dag_ref, dqT_ref, dkT_ref, dvT_ref, dfl_ref, *rest):
        dw_ref = rest[len(after)]
        hb = h_ref[...]
        dw_ref[_AG0:_Q0, :] = _mm_tn(dag_ref[...], hb).astype(dw_ref.dtype)
        for c0, ref in ((_Q0, dqT_ref), (_K0, dkT_ref), (_V0, dvT_ref)):
            dw_ref[c0:c0 + D_ATTN, :] = _mm(ref[...].astype(MXU), hb).astype(dw_ref.dtype)
        dw_ref[_F0:N_IN_PAD, :] = _mm_tn(dfl_ref[...].astype(MXU), hb).astype(dw_ref.dtype)

    vmem = pl.BlockSpec(memory_space=pltpu.VMEM)
    return _pallas_call(
        body, name=name, in_specs=[vmem] * 6 + [_UNREAD] * len(after), out_specs=vmem,
        out_shape=jax.ShapeDtypeStruct((N_IN_PAD, D), MXU),
        compiler_params=pltpu.CompilerParams(vmem_limit_bytes=56 * MIB),
    )(h, dag, dqT, dkT, dvT, dfl, *after)


def _forget_fwd(fl, fbp, name):
    T = fl.shape[0]
    tb = _tile(T, 256)

    def body(fl_ref, fb_ref, cum_ref, cumT_ref):
        ri = lax.broadcasted_iota(jnp.int32, (tb, tb), 0)
        ci = lax.broadcasted_iota(jnp.int32, (tb, tb), 1)
        tri = (ri >= ci).astype(jnp.bfloat16)
        carry = jnp.zeros((1, LANES), F32)
        for b in range(T // tb):
            z = fl_ref[b * tb:(b + 1) * tb, :] + fb_ref[...]
            lf = jnp.minimum(z, 0.0) - jnp.log1p(jnp.exp(-jnp.abs(z)))
            c = _exact_tri_dot(tri, lf) + carry
            cum_ref[b * tb:(b + 1) * tb, :] = c
            carry = c[tb - 1:tb, :]
        cumT_ref[...] = cum_ref[...].T[:N_HEADS, :]

    return _pallas_call(
        body, name=name,
        out_shape=[jax.ShapeDtypeStruct((T, LANES), F32), jax.ShapeDtypeStruct((N_HEADS, T), F32)],
        compiler_params=pltpu.CompilerParams(vmem_limit_bytes=32 * MIB),
    )(fl, fbp)


def _forget_bwd(dcum, fl, fbp, name):
    T = fl.shape[0]
    tb = _tile(T, 256)

    def body(dc_ref, fl_ref, fb_ref, dfl_ref, dfb_ref):
        ri = lax.broadcasted_iota(jnp.int32, (tb, tb), 0)
        ci = lax.broadcasted_iota(jnp.int32, (tb, tb), 1)
        tri = (ri <= ci).astype(jnp.bfloat16)
        carry = jnp.zeros((1, LANES), F32)
        dfb = jnp.zeros((1, LANES), F32)
        for b in reversed(range(T // tb)):
            sl = slice(b * tb, (b + 1) * tb)
            dl = _exact_tri_dot(tri, dc_ref[sl, :]) + carry
            carry = dl[0:1, :]
            z = fl_ref[sl, :] + fb_ref[...]
            dfl = dl * jax.nn.sigmoid(-z)
            dfl_ref[sl, :] = dfl
            dfb = dfb + jnp.sum(dfl, axis=0, keepdims=True)
        dfb_ref[...] = dfb

    return _pallas_call(
        body, name=name,
        out_shape=[jax.ShapeDtypeStruct((T, LANES), F32), jax.ShapeDtypeStruct((1, LANES), F32)],
        compiler_params=pltpu.CompilerParams(vmem_limit_bytes=32 * MIB),
    )(dcum, fl, fbp)


def _causal_keep(i, j, tq, tk):
    key = j * tk + lax.broadcasted_iota(jnp.int32, (tk, tq), 0)
    qry = i * tq + lax.broadcasted_iota(jnp.int32, (tk, tq), 1)
    return key <= qry


def _split_hi_lo(x):
    hi = x.astype(MXU)
    lo = (x - hi.astype(F32)).astype(MXU)
    return hi, lo


def _attn_fwd(qT, k, vT, cum, cumT, name):
    T = k.shape[0]
    tq = _tile(T, 256)
    tk = _tile(tq, 128)
    kpq = tq // tk
    heads = [slice(HEAD_DIM * h, HEAD_DIM * (h + 1)) for h in range(N_HEADS)]

    def body(qT_ref, k_ref, vT_ref, cum_ref, cumT_ref, o_ref, lseT_ref, acc_s, m_s, l_s):
        i = pl.program_id(0)
        acc_s[...] = jnp.zeros_like(acc_s)
        m_s[...] = jnp.full_like(m_s, NEG)
        l_s[...] = jnp.zeros_like(l_s)

        def kblock(j, masked):
            rows = pl.ds(pl.multiple_of(j * tk, tk), tk)
            keep = _causal_keep(i, j, tq, tk) if masked else None
            bias = [cumT_ref[h:h + 1, :] - cum_ref[rows, h:h + 1] for h in range(N_HEADS)]
            qk = [_mm(k_ref[rows, hs], qT_ref[hs, :]) + bias[h] for h, hs in enumerate(heads)]
            for h, hs in enumerate(heads):
                sT = qk[h]
                if masked:
                    sT = jnp.where(keep, sT, NEG)
                m_old = m_s[h:h + 1, :]
                m_new = jnp.maximum(m_old, jnp.max(sT, axis=0, keepdims=True))
                alpha = jnp.exp(m_old - m_new)
                pT = jnp.exp(sT - m_new)
                l_s[h:h + 1, :] = alpha * l_s[h:h + 1, :] + jnp.sum(pT, axis=0, keepdims=True)
                p_hi, p_lo = _split_hi_lo(pT)
                vh = vT_ref[hs, rows]
                acc_s[hs, :] = alpha * acc_s[hs, :] + (_mm(vh, p_hi) + _mm(vh, p_lo))
                m_s[h:h + 1, :] = m_new

        def unmasked(j, c):
            kblock(j, False)
            return c

        lax.fori_loop(0, kpq * i, unmasked, 0)
        for d in range(kpq):
            kblock(kpq * i + d, True)
        for h, hs in enumerate(heads):
            acc_s[hs, :] = acc_s[hs, :] / l_s[h:h + 1, :]
        o_ref[...] = acc_s[...].T
        lseT_ref[...] = m_s[...] + jnp.log(l_s[...])

    full = lambda a, b: pl.BlockSpec((a, b), lambda i: (0, 0))
    colblk = lambda r: pl.BlockSpec((r, tq), lambda i: (0, i))
    return _pallas_call(
        body, name=name, grid=(T // tq,),
        in_specs=[colblk(D_ATTN), full(T, D_ATTN), full(D_ATTN, T), full(T, LANES), colblk(N_HEADS)],
        out_specs=[pl.BlockSpec((tq, D_ATTN), lambda i: (i, 0)), colblk(N_HEADS)],
        out_shape=[jax.ShapeDtypeStruct((T, D_ATTN), F32), jax.ShapeDtypeStruct((N_HEADS, T), F32)],
        scratch_shapes=[pltpu.VMEM((D_ATTN, tq), F32), pltpu.VMEM((N_HEADS, tq), F32),
                        pltpu.VMEM((N_HEADS, tq), F32)],
        compiler_params=_params(40, 1),
    )(qT, k, vT, cum, cumT)


def _attn_bwd(qT, k, kT, v, doT, lseT, deltaT, cum, cumT, name, after=()):
    T = k.shape[0]
    tq = _tile(T, 256)
    tk = _tile(tq, 128)
    kpq = tq // tk
    heads = [slice(HEAD_DIM * h, HEAD_DIM * (h + 1)) for h in range(N_HEADS)]

    def body(qT_ref, k_ref, kT_ref, v_ref, doT_ref, lseT_ref, dlT_ref, cum_ref, cumT_ref, *rest):
        dq_ref, dk_ref, dv_ref, dcum_ref, dq_s = rest[len(after):]
        i = pl.program_id(0)

        @pl.when(i == 0)
        def _():
            dk_ref[...] = jnp.zeros_like(dk_ref)
            dv_ref[...] = jnp.zeros_like(dv_ref)
            dcum_ref[...] = jnp.zeros_like(dcum_ref)

        dq_s[...] = jnp.zeros_like(dq_s)

        def kblock(j, masked):
            rows = pl.ds(pl.multiple_of(j * tk, tk), tk)
            keep = _causal_keep(i, j, tq, tk) if masked else None
            bias = [cumT_ref[h:h + 1, :] - cum_ref[rows, h:h + 1] for h in range(N_HEADS)]
            qk = [_mm(k_ref[rows, hs], qT_ref[hs, :]) + bias[h] for h, hs in enumerate(heads)]
            dps = [_mm(v_ref[rows, hs], doT_ref[hs, :]) for hs in heads]
            for h, hs in enumerate(heads):
                sT = qk[h]
                if masked:
                    sT = jnp.where(keep, sT, NEG)
                pT = jnp.exp(sT - lseT_ref[h:h + 1, :])
                dsT = pT * (dps[h] - dlT_ref[h:h + 1, :])
                dcum_ref[rows, h:h + 1] += -jnp.sum(dsT, axis=1, keepdims=True)
                dsb = dsT.astype(MXU)
                dv_ref[hs, rows] += _mm_nt(doT_ref[hs, :], pT.astype(MXU))
                dk_ref[hs, rows] += _mm_nt(qT_ref[hs, :], dsb)
                dq_s[hs, :] += _mm(kT_ref[hs, rows], dsb)

        def unmasked(j, c):
            kblock(j, False)
            return c

        lax.fori_loop(0, kpq * i, unmasked, 0)
        for d in range(kpq):
            kblock(kpq * i + d, True)
        dq_ref[...] = (dq_s[...] * SCALE).astype(dq_ref.dtype)

    full = lambda a, b: pl.BlockSpec((a, b), lambda i: (0, 0))
    colblk = lambda r: pl.BlockSpec((r, tq), lambda i: (0, i))
    return _pallas_call(
        body, name=name, grid=(T // tq,),
        in_specs=[colblk(D_ATTN), full(T, D_ATTN), full(D_ATTN, T), full(T, D_ATTN), colblk(D_ATTN),
                  colblk(N_HEADS), colblk(N_HEADS), full(T, LANES), colblk(N_HEADS)] + [_UNREAD] * len(after),
        out_specs=[colblk(D_ATTN), full(D_ATTN, T), full(D_ATTN, T), full(T, LANES)],
        out_shape=[
            jax.ShapeDtypeStruct((D_ATTN, T), MXU),
            jax.ShapeDtypeStruct((D_ATTN, T), F32),
            jax.ShapeDtypeStruct((D_ATTN, T), F32),
            jax.ShapeDtypeStruct((T, LANES), F32),
        ],
        scratch_shapes=[pltpu.VMEM((D_ATTN, tq), F32)],
        compiler_params=_params(48, 1),
    )(qT, k, kT, v, doT, lseT, deltaT, cum, cumT, *after)


_ROWS_PER_CHUNK = 64


def _glu_halo(ag_ref, agh_ref, uext_s, tm, first):
    a = ag_ref[:, :D_CONV]
    sg = jax.nn.sigmoid(ag_ref[:, D_CONV:])
    uh = agh_ref[:, :D_CONV] * jax.nn.sigmoid(agh_ref[:, D_CONV:])
    uext_s[0:CONV_HALO, :] = jnp.where(first, 0.0, uh)
    uext_s[CONV_HALO:CONV_HALO + tm, :] = a * sg
    return a, sg


_SUBLANES = 8


def _shifted_copies(ext_s, sh_s, rows):
    for k in range(1, _SUBLANES):
        sh_s[k, 0:rows, :] = ext_s[pl.ds(k, rows), :]


def _window(ext_s, sh_s, start, rows):
    k = start % _SUBLANES
    if k == 0:
        return ext_s[pl.ds(start, rows), :]
    return sh_s[k, pl.ds(start - k, rows), :]


def _layer_norm_stats(y):
    mu = jnp.mean(y, axis=-1, keepdims=True)
    xc = y - mu
    rs = lax.rsqrt(jnp.mean(xc * xc, axis=-1, keepdims=True) + EPS)
    return xc * rs, rs


def _conv_fwd(ag, w32, cb, lg, lb, name):
    T = ag.shape[0]
    tm = _tile(T, 256)
    rc = _tile(tm, _ROWS_PER_CHUNK)
    hb = tm // CONV_HALO

    def body(ag_ref, agh_ref, w_ref, cb_ref, lg_ref, lb_ref, yc_ref, c_ref, uext_s, ush_s):
        i = pl.program_id(0)
        _glu_halo(ag_ref, agh_ref, uext_s, tm, i == 0)
        _shifted_copies(uext_s, ush_s, tm + CONV_HALO - _SUBLANES)
        for r0 in range(0, tm, rc):
            acc = jnp.zeros((rc, D_CONV), F32)
            for t in range(CONV_TAPS):
                acc = acc + _window(uext_s, ush_s, r0 + CONV_HALO - (CONV_TAPS - 1) + t, rc) * w_ref[t:t + 1, :]
            y = acc + cb_ref[...]
            yc_ref[r0:r0 + rc, :] = y
            n, _ = _layer_norm_stats(y)
            z = n * lg_ref[...] + lb_ref[...]
            c_ref[r0:r0 + rc, :] = z * jax.nn.sigmoid(z)

    row = lambda w: pl.BlockSpec((tm, w), lambda i: (i, 0))
    full = lambda a, b: pl.BlockSpec((a, b), lambda i: (0, 0))
    return _pallas_call(
        body, name=name, grid=(T // tm,),
        in_specs=[row(2 * D_CONV),
                  pl.BlockSpec((CONV_HALO, 2 * D_CONV), lambda i: (jnp.maximum(i * hb - 1, 0), 0)),
                  full(CONV_HALO, D_CONV), full(1, D_CONV), full(1, D_CONV), full(1, D_CONV)],
        out_specs=[row(D_CONV), row(D_CONV)],
        out_shape=[jax.ShapeDtypeStruct((T, D_CONV), F32), jax.ShapeDtypeStruct((T, D_CONV), F32)],
        scratch_shapes=[pltpu.VMEM((CONV_HALO + tm, D_CONV), F32),
                        pltpu.VMEM((_SUBLANES, CONV_HALO + tm, D_CONV), F32)],
        compiler_params=_params(32, 1),
    )(ag, ag, w32, cb, lg, lb)


def _conv_bwd(dc, yc, ag, w32, lg, lb, name):
    T = ag.shape[0]
    tm = _tile(T, 256)
    rc = _tile(tm, _ROWS_PER_CHUNK)
    I = T // tm
    hb = tm // CONV_HALO
    n_halo_blocks = T // CONV_HALO

    def body(dc_ref, yc_ref, dch_ref, ych_ref, ag_ref, agh_ref, w_ref, lg_ref, lb_ref,
             dag_ref, dw_ref, dcb_ref, dlg_ref, dlb_ref, uext_s, dext_s, ush_s, dsh_s):
        i = pl.program_id(0)
        lgv = lg_ref[...]
        lbv = lb_ref[...]

        def ln_bwd(dcv, ycv):
            n, rs = _layer_norm_stats(ycv)
            z = n * lgv + lbv
            dz = dcv * _silu_grad(z, jax.nn.sigmoid(z))
            dn = dz * lgv
            dy = rs * (dn - jnp.mean(dn, axis=-1, keepdims=True) - n * jnp.mean(dn * n, axis=-1, keepdims=True))
            return dy, dz, n

        dy, dz, n = ln_bwd(dc_ref[...], yc_ref[...])
        dyh, _, _ = ln_bwd(dch_ref[...], ych_ref[...])
        dext_s[0:tm, :] = dy
        dext_s[tm:tm + CONV_HALO, :] = jnp.where(i == I - 1, 0.0, dyh)
        a, sg = _glu_halo(ag_ref, agh_ref, uext_s, tm, i == 0)
        _shifted_copies(uext_s, ush_s, tm + CONV_HALO - _SUBLANES)
        _shifted_copies(dext_s, dsh_s, tm + CONV_HALO - _SUBLANES)

        @pl.when(i == 0)
        def _():
            dw_ref[...] = jnp.zeros_like(dw_ref)
            dcb_ref[...] = jnp.zeros_like(dcb_ref)
            dlg_ref[...] = jnp.zeros_like(dlg_ref)
            dlb_ref[...] = jnp.zeros_like(dlb_ref)

        dcb_ref[...] += jnp.sum(dy, axis=0, keepdims=True)
        dlg_ref[...] += jnp.sum(dz * n, axis=0, keepdims=True)
        dlb_ref[...] += jnp.sum(dz, axis=0, keepdims=True)
        for t in range(CONV_TAPS):
            u_t = _window(uext_s, ush_s, CONV_HALO - (CONV_TAPS - 1) + t, tm)
            dw_ref[t:t + 1, :] += jnp.sum(dy * u_t, axis=0, keepdims=True)
        for r0 in range(0, tm, rc):
            acc = jnp.zeros((rc, D_CONV), F32)
            for t in range(CONV_TAPS):
                acc = acc + _window(dext_s, dsh_s, r0 + (CONV_TAPS - 1) - t, rc) * w_ref[t:t + 1, :]
            a_c = a[r0:r0 + rc, :]
            sg_c = sg[r0:r0 + rc, :]
            dag_ref[r0:r0 + rc, :D_CONV] = (acc * sg_c).astype(dag_ref.dtype)
            dag_ref[r0:r0 + rc, D_CONV:] = (acc * a_c * sg_c * (1.0 - sg_c)).astype(dag_ref.dtype)

    row = lambda w: pl.BlockSpec((tm, w), lambda i: (i, 0))
    full = lambda a, b: pl.BlockSpec((a, b), lambda i: (0, 0))
    nxt = pl.BlockSpec((CONV_HALO, D_CONV), lambda i: (jnp.minimum((i + 1) * hb, n_halo_blocks - 1), 0))
    return _pallas_call(
        body, name=name, grid=(I,),
        in_specs=[row(D_CONV), row(D_CONV), nxt, nxt, row(2 * D_CONV),
                  pl.BlockSpec((CONV_HALO, 2 * D_CONV), lambda i: (jnp.maximum(i * hb - 1, 0), 0)),
                  full(CONV_HALO, D_CONV), full(1, D_CONV), full(1, D_CONV)],
        out_specs=[row(2 * D_CONV), full(CONV_HALO, D_CONV), full(1, D_CONV), full(1, D_CONV), full(1, D_CONV)],
        out_shape=[
            jax.ShapeDtypeStruct((T, 2 * D_CONV), MXU),
            jax.ShapeDtypeStruct((CONV_HALO, D_CONV), F32),
            jax.ShapeDtypeStruct((1, D_CONV), F32),
            jax.ShapeDtypeStruct((1, D_CONV), F32),
            jax.ShapeDtypeStruct((1, D_CONV), F32),
        ],
        scratch_shapes=[pltpu.VMEM((CONV_HALO + tm, D_CONV), F32), pltpu.VMEM((tm + CONV_HALO, D_CONV), F32),
                        pltpu.VMEM((_SUBLANES, CONV_HALO + tm, D_CONV), F32),
                        pltpu.VMEM((_SUBLANES, CONV_HALO + tm, D_CONV), F32)],
        compiler_params=_params(40, 1),
    )(dc, yc, dc, yc, ag, ag, w32, lg, lb)


def _outproj_fwd(x1, c, o, gc, ga, wout, name):
    T, D = x1.shape
    tm = _tile(T, 512)

    def body(x_ref, c_ref, o_ref, gc_ref, ga_ref, w_ref, x2_ref):
        yc, _ = _rms_fwd(c_ref[...], gc_ref[...])
        ya, _ = _rms_fwd(o_ref[...], ga_ref[...])
        x2_ref[...] = (x_ref[...] + _mm(yc.astype(MXU), w_ref[:D_CONV, :])
                       + _mm(ya.astype(MXU), w_ref[D_CONV:, :]))

    row = lambda w: pl.BlockSpec((tm, w), lambda i: (i, 0))
    full = lambda a, b: pl.BlockSpec((a, b), lambda i: (0, 0))
    return _pallas_call(
        body, name=name, grid=(T // tm,),
        in_specs=[row(D), row(D_CONV), row(D_ATTN), full(1, D_CONV), full(1, D_ATTN), full(D_CONV + D_ATTN, D)],
        out_specs=row(D),
        out_shape=jax.ShapeDtypeStruct((T, D), F32),
        compiler_params=_params(32, 1),
    )(x1, c, o, gc, ga, wout)


def _outproj_bwd(dx2, c, o, gc, ga, wout, name):
    T, D = dx2.shape
    tm = _tile(T, 256)
    I = T // tm

    def body(dx_ref, c_ref, o_ref, gc_ref, ga_ref, w_ref,
             dc_ref, doT_ref, dlT_ref, dw_ref, dgc_ref, dga_ref, acc_s):
        i = pl.program_id(0)
        dxb = dx_ref[...].astype(MXU)
        cv = c_ref[...]
        ov = o_ref[...]
        yc, rcn = _rms_fwd(cv, gc_ref[...])
        ya, ra = _rms_fwd(ov, ga_ref[...])
        dyc = _mm_nt(dxb, w_ref[:D_CONV, :])
        dya = _mm_nt(dxb, w_ref[D_CONV:, :])
        dwc = _mm_tn(yc.astype(MXU), dxb)
        dwa = _mm_tn(ya.astype(MXU), dxb)
        dcv, dgc = _rms_bwd(cv, rcn, gc_ref[...], dyc)
        dov, dga = _rms_bwd(ov, ra, ga_ref[...], dya)
        dc_ref[...] = dcv
        dob = dov.astype(doT_ref.dtype)
        doT_ref[...] = dov.T.astype(doT_ref.dtype)
        chan = lax.broadcasted_iota(jnp.int32, (D_ATTN, LANES), 0)
        head = lax.broadcasted_iota(jnp.int32, (D_ATTN, LANES), 1)
        in_head = ((chan >= head * HEAD_DIM) & (chan < (head + 1) * HEAD_DIM)).astype(jnp.bfloat16)
        dlT_ref[...] = _exact_dot_01(dob.astype(F32) * ov, in_head).T[:N_HEADS, :]

        @pl.when(i == 0)
        def _():
            acc_s[:D_CONV, :] = dwc
            acc_s[D_CONV:, :] = dwa
            dgc_ref[...] = dgc
            dga_ref[...] = dga

        @pl.when(i > 0)
        def _():
            acc_s[:D_CONV, :] += dwc
            acc_s[D_CONV:, :] += dwa
            dgc_ref[...] += dgc
            dga_ref[...] += dga

        @pl.when(i == I - 1)
        def _():
            dw_ref[...] = acc_s[...].astype(dw_ref.dtype)

    row = lambda w: pl.BlockSpec((tm, w), lambda i: (i, 0))
    full = lambda a, b: pl.BlockSpec((a, b), lambda i: (0, 0))
    return _pallas_call(
        body, name=name, grid=(I,),
        in_specs=[row(D), row(D_CONV), row(D_ATTN), full(1, D_CONV), full(1, D_ATTN), full(D_CONV + D_ATTN, D)],
        out_specs=[row(D_CONV), pl.BlockSpec((D_ATTN, tm), lambda i: (0, i)),
                   pl.BlockSpec((N_HEADS, tm), lambda i: (0, i)),
                   full(D_CONV + D_ATTN, D), full(1, D_CONV), full(1, D_ATTN)],
        out_shape=[
            jax.ShapeDtypeStruct((T, D_CONV), F32),
            jax.ShapeDtypeStruct((D_ATTN, T), MXU),
            jax.ShapeDtypeStruct((N_HEADS, T), F32),
            jax.ShapeDtypeStruct((D_CONV + D_ATTN, D), MXU),
            jax.ShapeDtypeStruct((1, D_CONV), F32),
            jax.ShapeDtypeStruct((1, D_ATTN), F32),
        ],
        scratch_shapes=[pltpu.VMEM((D_CONV + D_ATTN, D), F32)],
        compiler_params=_params(40, 1),
    )(dx2, c, o, gc, ga, wout)


def _loss_head(x3, gf, target, name):
    T, D = x3.shape
    tm = _tile(T, 512)

    def body(x_ref, g_ref, t_ref, loss_ref, dx_ref, dg_ref):
        i = pl.program_id(0)
        xv = x_ref[...]
        gv = g_ref[...]
        out, r = _rms_fwd(xv, gv)
        err = out - t_ref[...]
        part = jnp.full((1, LANES), 0.5 / D, F32) * jnp.sum(err * err)
        dxn, dgp = _rms_bwd(xv, r, gv, err * (1.0 / D))
        dx_ref[...] = dxn

        @pl.when(i == 0)
        def _():
            loss_ref[...] = part
            dg_ref[...] = dgp

        @pl.when(i > 0)
        def _():
            loss_ref[...] += part
            dg_ref[...] += dgp

    row = lambda w: pl.BlockSpec((tm, w), lambda i: (i, 0))
    full = lambda a, b: pl.BlockSpec((a, b), lambda i: (0, 0))
    return _pallas_call(
        body, name=name, grid=(T // tm,),
        in_specs=[row(D), full(1, D), row(D)],
        out_specs=[full(1, LANES), row(D), full(1, D)],
        out_shape=[jax.ShapeDtypeStruct((1, LANES), F32), jax.ShapeDtypeStruct((T, D), F32),
                   jax.ShapeDtypeStruct((1, D), F32)],
        compiler_params=_params(32, 1),
    )(x3, gf, target)


def _row_tile(rows):
    for cand in (256, 176, 128, 64, 32, 16):
        if rows % cand == 0:
            return cand
    return rows


def _adamw(w, m, v, parts, name):
    R, C = w.shape
    P = parts.shape[0]
    tr = _row_tile(R)
    c1 = 1.0 - ADAM_B1 ** ADAM_STEP
    c2 = 1.0 - ADAM_B2 ** ADAM_STEP

    def body(w_ref, m_ref, v_ref, p_ref, g_ref, d_ref, nm_ref, nv_ref):
        g = p_ref[0].astype(F32)
        for s in range(1, P):
            g = g + p_ref[s].astype(F32)
        wv = w_ref[...]
        mn = ADAM_B1 * m_ref[...] + (1.0 - ADAM_B1) * g
        vn = ADAM_B2 * v_ref[...] + (1.0 - ADAM_B2) * (g * g)
        g_ref[...] = g
        nm_ref[...] = mn
        nv_ref[...] = vn
        d_ref[...] = -ADAM_LR * ((mn / c1) / (jnp.sqrt(vn / c2) + ADAM_EPS) + ADAM_WD * wv)

    blk = pl.BlockSpec((tr, C), lambda i: (i, 0))
    out = jax.ShapeDtypeStruct((R, C), F32)
    return _pallas_call(
        body, name=name, grid=(R // tr,),
        in_specs=[blk, blk, blk, pl.BlockSpec((P, tr, C), lambda i: (0, i, 0))],
        out_specs=[blk, blk, blk, blk],
        out_shape=[out, out, out, out],
        compiler_params=_params(32, 1),
    )(w, m, v, parts)


def _position():
    return lax.axis_index("x"), lax.axis_index("y"), lax.axis_index("c")


def _flat(px, py, pc):
    return 4 * px + 2 * py + pc


def _gather_body(ins, outs, send_sems, recv_sems, local_sems, handshake):
    n = len(ins)
    x, y, c = _position()
    me, sibling = (x, y, c), (x, y, 1 - c)
    chips = [(1 - x, y), (x, 1 - y), (1 - x, 1 - y)]
    if handshake:
        _handshake([sibling] + [(*chip, cc) for chip in chips for cc in (c, 1 - c)])

    def copy(a, k, block, to, src=None):
        dst = outs[a].at[_flat(*block)]
        return pltpu.make_async_remote_copy(
            src_ref=dst if src is None else src, dst_ref=dst,
            send_sem=send_sems.at[a, k], recv_sem=recv_sems.at[a, k],
            device_id=to, device_id_type=MESH)

    mine = [pltpu.make_async_copy(ins[a], outs[a].at[_flat(*me)], local_sems.at[a]) for a in range(n)]
    for cp in mine:
        cp.start()
    first = []
    for a in range(n):
        first.append(copy(a, 0, me, sibling, src=ins[a]))
        first += [copy(a, 1 + j, me, (*chip, c), src=ins[a]) for j, chip in enumerate(chips)]
    for cp in first:
        cp.start()
    passed = []
    for a in range(n):
        for j, chip in enumerate(chips):
            copy(a, 1 + j, (*chip, c), me).wait_recv()
            fwd = copy(a, 4 + j, (*chip, c), sibling)
            fwd.start()
            passed.append(fwd)
    for a in range(n):
        copy(a, 0, sibling, me).wait_recv()
        for j, chip in enumerate(chips):
            copy(a, 4 + j, (*chip, 1 - c), me).wait_recv()
    for cp in first + passed:
        cp.wait_send()
    for cp in mine:
        cp.wait()


def _gather_scratch(n):
    return [pltpu.SemaphoreType.DMA((n, 7)), pltpu.SemaphoreType.DMA((n, 7)), pltpu.SemaphoreType.DMA((n,))]


def _all_gather(shards, name):
    n = len(shards)

    def body(*refs):
        _gather_body(refs[:n], refs[n:2 * n], *refs[2 * n:], handshake=False)

    hbm = pl.BlockSpec(memory_space=pltpu.HBM)
    return _pallas_call(
        body, name=name,
        in_specs=[hbm] * n, out_specs=[hbm] * n,
        out_shape=[jax.ShapeDtypeStruct((N_DEV,) + s.shape, s.dtype) for s in shards],
        scratch_shapes=_gather_scratch(n),
    )(*shards)


def _handshake(peers):
    barrier = pltpu.get_barrier_semaphore()
    for peer in peers:
        pl.semaphore_signal(barrier, inc=1, device_id=peer, device_id_type=MESH)
    pl.semaphore_wait(barrier, len(peers))


def _sequencer_call(body, name, collective_id, out_type, scratch_types, operands):
    return pl.kernel(
        body, name=name, out_type=out_type,
        mesh=plsc.ScalarSubcoreMesh(axis_name="sequencer", num_cores=1),
        scratch_types=scratch_types,
        compiler_params=pltpu.CompilerParams(collective_id=collective_id),
    )(*operands)


def _seq_all_gather(shards, name, collective_id, after):
    n = len(shards)

    def body(*refs):
        _gather_body(refs[:n], refs[n + 1:2 * n + 1], *refs[2 * n + 1:], handshake=True)

    return _sequencer_call(
        body, name, collective_id,
        [jax.ShapeDtypeStruct((N_DEV,) + s.shape, s.dtype) for s in shards],
        _gather_scratch(n), list(shards) + [after])


def _seq_to_sibling(parts, name, collective_id, after):
    n = len(parts)

    def body(*refs):
        ins, outs = refs[:n], refs[n + len(after):2 * n + len(after)]
        send_sems, recv_sems = refs[2 * n + len(after):]
        x, y, c = _position()
        sibling = (x, y, 1 - c)
        _handshake([sibling])
        sent = []
        for a in range(n):
            for q in range(N_CHIPS):
                cp = pltpu.make_async_remote_copy(
                    src_ref=ins[a].at[2 * q + (1 - c)], dst_ref=outs[a].at[q],
                    send_sem=send_sems.at[a, q], recv_sem=recv_sems.at[a, q],
                    device_id=sibling, device_id_type=MESH)
                cp.start()
                sent.append(cp)
        for cp in sent:
            cp.wait_recv()
        for cp in sent:
            cp.wait_send()

    return _sequencer_call(
        body, name, collective_id,
        [jax.ShapeDtypeStruct((N_CHIPS,) + p.shape[1:], p.dtype) for p in parts],
        [pltpu.SemaphoreType.DMA((n, N_CHIPS)), pltpu.SemaphoreType.DMA((n, N_CHIPS))],
        list(parts) + list(after))


def _seq_to_chips(partials, name, collective_id, after=()):
    n = len(partials)

    def body(*refs):
        ins, outs = refs[:n], refs[n + len(after):2 * n + len(after)]
        send_sems, recv_sems, local_sems = refs[2 * n + len(after):]
        x, y, c = _position()
        my_chip = 2 * x + y
        chips = [(1 - x, y), (x, 1 - y), (1 - x, 1 - y)]
        _handshake([(*chip, c) for chip in chips])
        mine = [pltpu.make_async_copy(ins[a].at[my_chip], outs[a].at[my_chip], local_sems.at[a]) for a in range(n)]
        for cp in mine:
            cp.start()
        sent = []
        for a in range(n):
            for j, (px, py) in enumerate(chips):
                cp = pltpu.make_async_remote_copy(
                    src_ref=ins[a].at[2 * px + py], dst_ref=outs[a].at[my_chip],
                    send_sem=send_sems.at[a, j], recv_sem=recv_sems.at[a, j],
                    device_id=(px, py, c), device_id_type=MESH)
                cp.start()
                sent.append(cp)
        for cp in sent:
            cp.wait_recv()
        for cp in sent:
            cp.wait_send()
        for cp in mine:
            cp.wait()

    return _sequencer_call(
        body, name, collective_id,
        [jax.ShapeDtypeStruct(p.shape, p.dtype) for p in partials],
        [pltpu.SemaphoreType.DMA((n, 3)), pltpu.SemaphoreType.DMA((n, 3)), pltpu.SemaphoreType.DMA((n,))],
        list(partials) + list(after))


def _pair_add(part, recv, name, after=()):
    _, R, C = part.shape
    core = lax.axis_index("c").astype(jnp.int32).reshape(1)

    def body(c_ref, p_ref, r_ref, *rest):
        o_ref = rest[len(after)]
        o_ref[...] = (p_ref[...].astype(F32) + r_ref[...].astype(F32)).astype(o_ref.dtype)

    blk = pl.BlockSpec((None, R, C), lambda q, c: (q, 0, 0))
    return pl.pallas_call(
        body, name=name,
        grid_spec=pltpu.PrefetchScalarGridSpec(
            num_scalar_prefetch=1, grid=(N_CHIPS,),
            in_specs=[pl.BlockSpec((None, R, C), lambda q, c: (2 * q + c[0], 0, 0)), blk] + [_UNREAD] * len(after),
            out_specs=blk),
        out_shape=pltpu.HBM((N_CHIPS, R, C), part.dtype),
        compiler_params=_params(32, 1),
    )(core, *[pltpu.with_memory_space_constraint(a, pltpu.HBM) for a in (part, recv, *after)])


class _Reduced(NamedTuple):
    partials: list
    reduced: list


def _blocks(g):
    return g.reshape(N_DEV, -1, g.shape[-1])


def _reduce_scatter(parts, tag, ids, after=(), between=None, add_after=()):
    from_sibling = _seq_to_sibling(parts, "rs_sibling_" + tag, ids[0], after)
    mid = between(from_sibling[0]) if between else ()
    partials = [_pair_add(p, r, "rs_add_%s_%d" % (tag, a), add_after)
                for a, (p, r) in enumerate(zip(parts, from_sibling))]
    return _Reduced(partials, _seq_to_chips(partials, "rs_chips_" + tag, ids[1], mid))


_SMALL = ("ffn1_norm", "mix_norm", "conv_b", "conv_ln_g", "conv_ln_b", "forget_b", "out_norm_conv",
          "out_norm_attn", "ffn2_norm", "final_norm")
_PACK_WIDTH = 2 * D_CONV
_SLOT = dict(ffn1_norm=(0, 0), mix_norm=(1, 0), ffn2_norm=(2, 0), final_norm=(3, 0), conv_b=(4, 0),
             conv_ln_g=(4, D_CONV), conv_ln_b=(5, 0), out_norm_conv=(5, D_CONV), out_norm_attn=(6, 0),
             forget_b=(6, D_CONV))
_CONV_ROW0 = 8
_PACK_ROWS = _CONV_ROW0 + CONV_HALO


def _pack_small(small, name):
    arrays = [small[n] for n in _SMALL] + [small["conv_w"]]

    def body(*refs):
        out = refs[-1]
        out[...] = jnp.zeros_like(out)
        for n, ref in zip(_SMALL, refs):
            row, lane = _SLOT[n]
            out[row:row + 1, lane:lane + ref.shape[1]] = ref[...]
        out[_CONV_ROW0:, :D_CONV] = refs[len(_SMALL)][...]

    return _pallas_call(body, name=name, out_shape=jax.ShapeDtypeStruct((_PACK_ROWS, _PACK_WIDTH), F32))(*arrays)


def _adamw_small(gathered, w, m, v, name):
    c1 = 1.0 - ADAM_B1 ** ADAM_STEP
    c2 = 1.0 - ADAM_B2 ** ADAM_STEP
    k = len(_SMALL)

    def body(g_ref, *refs):
        ws, ms, vs = refs[:k], refs[k:2 * k], refs[2 * k:3 * k]
        outs = refs[3 * k:]
        total = g_ref[0]
        for s in range(1, N_DEV):
            total = total + g_ref[s]
        for i, n in enumerate(_SMALL):
            row, lane = _SLOT[n]
            width = ws[i].shape[1]
            g = total[row:row + 1, lane:lane + width]
            mn = ADAM_B1 * ms[i][...] + (1.0 - ADAM_B1) * g
            vn = ADAM_B2 * vs[i][...] + (1.0 - ADAM_B2) * (g * g)
            o_g, o_d, o_m, o_v = outs[4 * i:4 * i + 4]
            o_g[...] = g
            o_m[...] = mn
            o_v[...] = vn
            o_d[...] = -ADAM_LR * ((mn / c1) / (jnp.sqrt(vn / c2) + ADAM_EPS) + ADAM_WD * ws[i][...])
        outs[4 * k][...] = total[_CONV_ROW0:, :D_CONV]

    shapes = []
    for n in _SMALL:
        shapes += [jax.ShapeDtypeStruct(w[n].shape, F32)] * 4
    shapes.append(jax.ShapeDtypeStruct((CONV_HALO, D_CONV), F32))
    res = _pallas_call(body, name=name, out_shape=shapes)(
        gathered, *[w[n] for n in _SMALL], *[m[n] for n in _SMALL], *[v[n] for n in _SMALL])
    return {n: res[4 * i:4 * i + 4] for i, n in enumerate(_SMALL)}, res[4 * k]


def _local_step(x, target, norms, shard):
    D = x.shape[1]
    J = N_DEV // 2
    as13 = lambda g: g.reshape(2, J, g.shape[1], D)

    (g13_1,) = _all_gather([shard["ffn1_w13"]], "gather_ffn1_w13")
    (g2_1,) = _seq_all_gather([shard["ffn1_w2"]], "gather_ffn1_w2", 10, after=g13_1)
    w13_1 = as13(g13_1)
    G1, U1, A1 = _ffn_up(x, norms["ffn1_norm"], w13_1, "ffn1_up")
    gin, gout, gconv = _seq_all_gather([shard["w_in"], shard["w_out"], shard["conv_w"]], "gather_mix", 1, after=G1)
    w2_1 = g2_1.reshape(-1, D)
    x1 = _ffn_down(x, A1, w2_1, "ffn1_down")
    g13_2, g2_2 = _seq_all_gather([shard["ffn2_w13"], shard["ffn2_w2"]], "gather_ffn2", 2, after=x1)
    winp = jnp.pad(gin.reshape(N_IN, D), ((0, N_IN_PAD - N_IN), (0, 0)))
    wout = gout.reshape(-1, D)
    conv_w32 = jnp.pad(gconv.transpose(1, 0, 2).reshape(CONV_TAPS, D_CONV), ((0, CONV_HALO - CONV_TAPS), (0, 0)))

    ag, k, v, qT, kT, vT, fl = _inproj_fwd(x1, norms["mix_norm"], winp, "inproj_fwd")
    cum, cumT = _forget_fwd(fl, norms["forget_b"], "forget_fwd")
    yc, c = _conv_fwd(ag, conv_w32, norms["conv_b"], norms["conv_ln_g"], norms["conv_ln_b"], "conv_fwd")
    o, lseT = _attn_fwd(qT, k, vT, cum, cumT, "attn_fwd")
    x2 = _outproj_fwd(x1, c, o, norms["out_norm_conv"], norms["out_norm_attn"], wout, "outproj_fwd")
    w13_2, w2_2 = as13(g13_2), g2_2.reshape(-1, D)
    G2, U2, A2 = _ffn_up(x2, norms["ffn2_norm"], w13_2, "ffn2_up")
    x3 = _ffn_down(x2, A2, w2_2, "ffn2_down")
    loss, dx3, d_final = _loss_head(x3, norms["final_norm"], target, "loss_head")

    dw2_2 = _ffn_w2_grad(dx3, A2, "ffn2_w2_grad")
    dx2, d_ffn2n, h3, dG2, dU2 = _ffn_bwd_act(x2, norms["ffn2_norm"], dx3, G2, U2, w13_2, w2_2, "ffn2_bwd_act")
    dw13_2 = _ffn_w13_grad(h3, dG2, dU2, "ffn2_w13_grad")
    dc, dobT, deltaT, dwout, d_onc, d_ona = _outproj_bwd(
        dx2, c, o, norms["out_norm_conv"], norms["out_norm_attn"], wout, "outproj_bwd")
    red_ffn2 = _reduce_scatter([_blocks(dw13_2), _blocks(dw2_2)], "ffn2", (3, 4), add_after=(dc,))
    dqT, dkT, dvT, dcum = _attn_bwd(qT, k, kT, v, dobT, lseT, deltaT, cum, cumT, "attn_bwd",
                                    after=red_ffn2.partials)
    dfl, d_fb = _forget_bwd(dcum, fl, norms["forget_b"], "forget_bwd")
    dag, d_convw, d_cb, d_lg, d_lb = _conv_bwd(dc, yc, ag, conv_w32, norms["conv_ln_g"], norms["conv_ln_b"], "conv_bwd")
    dx1, d_mixn, h2 = _inproj_bwd_act(x1, norms["mix_norm"], dx2, dag, dqT, dkT, dvT, dfl, winp, "inproj_bwd_act")
    dw2_1 = _ffn_w2_grad(dx1, A1, "ffn1_w2_grad")
    red_w2_1 = _reduce_scatter([_blocks(dw2_1)], "ffn1_w2", (11, 12), after=red_ffn2.reduced[:1])
    dwinp = _inproj_bwd_weights(h2, dag, dqT, dkT, dvT, dfl, "inproj_bwd_weights", after=red_w2_1.partials)
    dwin_blocks = dwinp[:N_IN].reshape(N_DEV, N_IN // N_DEV, -1)
    red_mix = _reduce_scatter([dwin_blocks, _blocks(dwout)], "mix", (5, 6), after=red_w2_1.reduced[:1])
    dx, d_ffn1n, h1, dG1, dU1 = _ffn_bwd_act(x, norms["ffn1_norm"], dx1, G1, U1, w13_1, w2_1, "ffn1_bwd_act",
                                             after=red_mix.partials)
    dw13_1 = _ffn_w13_grad(h1, dG1, dU1, "ffn1_w13_grad")

    small = dict(ffn1_norm=d_ffn1n, mix_norm=d_mixn, conv_b=d_cb, conv_ln_g=d_lg, conv_ln_b=d_lb,
                 forget_b=d_fb, out_norm_conv=d_onc, out_norm_attn=d_ona, ffn2_norm=d_ffn2n,
                 final_norm=d_final, conv_w=d_convw)
    packed_small = _pack_small(small, "pack_small_grads")
    gathered_small = []

    def gather_small(behind):
        gathered_small.extend(_seq_all_gather([packed_small], "gather_small_grads", 9, after=behind))
        return gathered_small

    red_w13_1 = _reduce_scatter([_blocks(dw13_1)], "ffn1_w13", (7, 8), after=red_mix.reduced[:1],
                                between=gather_small)
    big = dict(ffn1_w13=red_w13_1.reduced[0], ffn1_w2=red_w2_1.reduced[0], w_in=red_mix.reduced[0],
               w_out=red_mix.reduced[1], ffn2_w13=red_ffn2.reduced[0], ffn2_w2=red_ffn2.reduced[1])
    return loss[0, 0], dx, gathered_small[0], big


_BIG = ("ffn1_w13", "ffn1_w2", "w_in", "w_out", "ffn2_w13", "ffn2_w2")
_TRANSPOSED = ("ffn1_w13", "ffn2_w13", "w_in")
_ORDER = ("ffn1_norm", "ffn1_w13", "ffn1_w2", "mix_norm", "w_in", "conv_w", "conv_b", "conv_ln_g", "conv_ln_b",
          "forget_b", "out_norm_conv", "out_norm_attn", "w_out", "ffn2_norm", "ffn2_w13", "ffn2_w2", "final_norm")


def kernel(x, ffn1_norm, ffn1_w13, ffn1_w2, mix_norm, w_in, conv_w, conv_b, conv_ln_g, conv_ln_b, forget_b, out_norm_conv, out_norm_attn, w_out, ffn2_norm, ffn2_w13, ffn2_w2, final_norm, loss_target, m_ffn1_norm, m_ffn1_w13, m_ffn1_w2, m_mix_norm, m_w_in, m_conv_w, m_conv_b, m_conv_ln_g, m_conv_ln_b, m_forget_b, m_out_norm_conv, m_out_norm_attn, m_w_out, m_ffn2_norm, m_ffn2_w13, m_ffn2_w2, m_final_norm, v_ffn1_norm, v_ffn1_w13, v_ffn1_w2, v_mix_norm, v_w_in, v_conv_w, v_conv_b, v_conv_ln_g, v_conv_ln_b, v_forget_b, v_out_norm_conv, v_out_norm_attn, v_w_out, v_ffn2_norm, v_ffn2_w13, v_ffn2_w2, v_final_norm):
    w = dict(ffn1_norm=ffn1_norm, ffn1_w13=ffn1_w13, ffn1_w2=ffn1_w2, mix_norm=mix_norm, w_in=w_in, conv_w=conv_w,
             conv_b=conv_b, conv_ln_g=conv_ln_g, conv_ln_b=conv_ln_b, forget_b=forget_b, out_norm_conv=out_norm_conv,
             out_norm_attn=out_norm_attn, w_out=w_out, ffn2_norm=ffn2_norm, ffn2_w13=ffn2_w13, ffn2_w2=ffn2_w2,
             final_norm=final_norm)
    m = dict(ffn1_norm=m_ffn1_norm, ffn1_w13=m_ffn1_w13, ffn1_w2=m_ffn1_w2, mix_norm=m_mix_norm, w_in=m_w_in,
             conv_w=m_conv_w, conv_b=m_conv_b, conv_ln_g=m_conv_ln_g, conv_ln_b=m_conv_ln_b, forget_b=m_forget_b,
             out_norm_conv=m_out_norm_conv, out_norm_attn=m_out_norm_attn, w_out=m_w_out, ffn2_norm=m_ffn2_norm,
             ffn2_w13=m_ffn2_w13, ffn2_w2=m_ffn2_w2, final_norm=m_final_norm)
    v = dict(ffn1_norm=v_ffn1_norm, ffn1_w13=v_ffn1_w13, ffn1_w2=v_ffn1_w2, mix_norm=v_mix_norm, w_in=v_w_in,
             conv_w=v_conv_w, conv_b=v_conv_b, conv_ln_g=v_conv_ln_g, conv_ln_b=v_conv_ln_b, forget_b=v_forget_b,
             out_norm_conv=v_out_norm_conv, out_norm_attn=v_out_norm_attn, w_out=v_w_out, ffn2_norm=v_ffn2_norm,
             ffn2_w13=v_ffn2_w13, ffn2_w2=v_ffn2_w2, final_norm=v_final_norm)
    shapes = {n: a.shape for n, a in w.items()}
    T, D = x.shape[1], x.shape[2]
    def two(n, a):
        if a.ndim != 3:
            return a.reshape(1, -1)
        a = a.reshape(a.shape[-2], a.shape[-1])
        return a.T if n in _TRANSPOSED else a

    w2d = {n: two(n, a) for n, a in w.items()}
    m2d = {n: two(n, a) for n, a in m.items()}
    v2d = {n: two(n, a) for n, a in v.items()}

    shard = {n: w2d[n].astype(MXU) for n in _BIG}
    shard["conv_w"] = w2d["conv_w"]
    norms = {n: w2d[n] for n in _SMALL}
    norms["forget_b"] = jnp.pad(w2d["forget_b"], ((0, 0), (0, LANES - N_HEADS)))
    loss_part, dx, gathered_small, big = _local_step(x[0], loss_target[0], norms, shard)
    loss = lax.psum(loss_part, ("x", "y", "c"))

    grads, deltas, new_m, new_v = {}, {}, {}, {}
    for n in _BIG:
        g, d, nm, nv = _adamw(w2d[n], m2d[n], v2d[n], big[n], "adamw_" + n)
        grads[n], deltas[n], new_m[n], new_v[n] = g, d, nm, nv

    small_out, conv_g_full = _adamw_small(gathered_small, w2d, m2d, v2d, "adamw_small")
    for n in _SMALL:
        grads[n], deltas[n], new_m[n], new_v[n] = small_out[n]
    conv_g_full = conv_g_full[:CONV_TAPS]
    xi, yi, ci = _position()
    cw = shapes["conv_w"][-1]
    conv_g_mine = lax.dynamic_slice_in_dim(conv_g_full, _flat(xi, yi, ci) * cw, cw, axis=1)
    g, d, nm, nv = _adamw(w2d["conv_w"], m2d["conv_w"], v2d["conv_w"], conv_g_mine[None], "adamw_conv_w")
    grads["conv_w"], deltas["conv_w"], new_m["conv_w"], new_v["conv_w"] = g, d, nm, nv

    shaped = lambda dct: [(dct[n].T if n in _TRANSPOSED else dct[n]).reshape(shapes[n]) for n in _ORDER]
    return (loss, dx[None], *shaped(grads), *shaped(deltas), *shaped(new_m), *shaped(new_v))
```

```python
from typing import NamedTuple

import jax
import jax.numpy as jnp
from jax import lax
from jax.experimental import pallas as pl
from jax.experimental.pallas import tpu as pltpu
from jax.experimental.pallas import tpu_sc as plsc

F32 = jnp.float32
MXU = jnp.bfloat16
EPS = 1e-6
N_HEADS = 8
HEAD_DIM = 64
D_CONV = 512
D_ATTN = N_HEADS * HEAD_DIM
CONV_TAPS = 31
CONV_HALO = 32
SCALE = HEAD_DIM ** -0.5
NEG = -1e30
LANES = 128
N_DEV = 8
N_CHIPS = N_DEV // 2
MESH = pl.DeviceIdType.MESH
MIB = 1 << 20

ADAM_LR = 0.001
ADAM_B1 = 0.9
ADAM_B2 = 0.999
ADAM_EPS = 1e-08
ADAM_WD = 0.01
ADAM_STEP = 10


_UNREAD = pl.BlockSpec(memory_space=pl.ANY)


def _pallas_call(body, *, out_shape, **kwargs):
    in_hbm = lambda s: pltpu.HBM(s.shape, s.dtype)
    outs = [in_hbm(s) for s in out_shape] if isinstance(out_shape, (list, tuple)) else in_hbm(out_shape)
    call = pl.pallas_call(body, out_shape=outs, **kwargs)
    return lambda *operands: call(*[pltpu.with_memory_space_constraint(a, pltpu.HBM) for a in operands])


def _params(vmem_mib, n_axes):
    return pltpu.CompilerParams(dimension_semantics=("arbitrary",) * n_axes, vmem_limit_bytes=vmem_mib * MIB)


def _mm(a, b):
    return jnp.dot(a, b, preferred_element_type=F32)


def _mm_nt(a, b):
    return lax.dot_general(a, b, (((1,), (1,)), ((), ())), preferred_element_type=F32)


def _mm_tn(a, b):
    return lax.dot_general(a, b, (((0,), (0,)), ((), ())), preferred_element_type=F32)


def _rms_fwd(x, g):
    r = lax.rsqrt(jnp.mean(x * x, axis=-1, keepdims=True) + EPS)
    return x * r * g, r


def _rms_bwd(x, r, g, dy):
    gdy = dy * g
    dx = r * gdy - x * (r * r * r) * jnp.mean(x * gdy, axis=-1, keepdims=True)
    dg = jnp.sum(dy * x * r, axis=0, keepdims=True)
    return dx, dg


def _silu_grad(z, sz):
    return sz * (1.0 + z * (1.0 - sz))


def _three_terms(x):
    x1 = x.astype(jnp.bfloat16)
    r1 = x - x1.astype(F32)
    x2 = r1.astype(jnp.bfloat16)
    x3 = (r1 - x2.astype(F32)).astype(jnp.bfloat16)
    return x1, x2, x3


def _exact_tri_dot(tri, x):
    x1, x2, x3 = _three_terms(x)
    return _mm(tri, x1) + _mm(tri, x2) + _mm(tri, x3)


def _exact_dot_01(x, sel):
    x1, x2, x3 = _three_terms(x)
    return _mm(x1, sel) + _mm(x2, sel) + _mm(x3, sel)


def _tile(n, want):
    t = min(n, want)
    assert n % t == 0
    return t


_FFN_CHUNK = 256


def _ffn_up(x, g, w13, name):
    T, D = x.shape
    _, J, bf, _ = w13.shape
    tm = _tile(T, 512)
    I = T // tm

    def body(x_ref, g_ref, w13_ref, G_ref, U_ref, A_ref, h_s):
        j = pl.program_id(0)
        i = pl.program_id(1)
        rows = pl.ds(pl.multiple_of(i * tm, tm), tm)

        @pl.when(j == 0)
        def _():
            h, _ = _rms_fwd(x_ref[...], g_ref[...])
            h_s[rows, :] = h.astype(MXU)

        chunks = [slice(r0, r0 + _FFN_CHUNK) for r0 in range(0, tm, _FFN_CHUNK)]
        hbs = [h_s[pl.ds(pl.multiple_of(i * tm + rs.start, _FFN_CHUNK), _FFN_CHUNK), :] for rs in chunks]
        GU = [(_mm_nt(hb, w13_ref[0]), _mm_nt(hb, w13_ref[1])) for hb in hbs]
        for rs, (G, U) in zip(chunks, GU):
            G_ref[rs, :] = G.astype(MXU)
            U_ref[rs, :] = U.astype(MXU)
            A_ref[rs, :] = (G * jax.nn.sigmoid(G) * U).astype(MXU)

    blk = pl.BlockSpec((None, tm, bf), lambda j, i: (j, i, 0))
    hid = jax.ShapeDtypeStruct((J, T, bf), MXU)
    return _pallas_call(
        body, name=name, grid=(J, I),
        in_specs=[pl.BlockSpec((tm, D), lambda j, i: (jnp.where(j == 0, i, I - 1), 0)),
                  pl.BlockSpec((1, D), lambda j, i: (0, 0)),
                  pl.BlockSpec((2, None, bf, D), lambda j, i: (0, j, 0, 0))],
        out_specs=[blk, blk, blk],
        out_shape=[hid, hid, hid],
        scratch_shapes=[pltpu.VMEM((T, D), MXU)],
        compiler_params=_params(40, 2),
    )(x, g, w13)


def _ffn_down(x, A, w2, name):
    T, D = x.shape
    J, _, bf = A.shape
    tm = _tile(T, 512)

    def body(x_ref, A_ref, w2_ref, xo_ref):
        f = _mm(A_ref[0], w2_ref[0:bf, :])
        for j in range(1, J):
            f = f + _mm(A_ref[j], w2_ref[j * bf:(j + 1) * bf, :])
        xo_ref[...] = x_ref[...] + 0.5 * f

    row = pl.BlockSpec((tm, D), lambda i: (i, 0))
    return _pallas_call(
        body, name=name, grid=(T // tm,),
        in_specs=[row, pl.BlockSpec((J, tm, bf), lambda i: (0, i, 0)), pl.BlockSpec((J * bf, D), lambda i: (0, 0))],
        out_specs=row,
        out_shape=jax.ShapeDtypeStruct((T, D), F32),
        compiler_params=_params(48, 1),
    )(x, A, w2)


def _ffn_bwd_gates(x, g, dy, Gs, Us, w2, name, after=()):
    T, D = x.shape
    J, _, bf = Gs.shape
    tm = _tile(T, 512)
    I = T // tm

    def body(x_ref, g_ref, dy_ref, G_ref, U_ref, w2_ref, *rest):
        h_ref, dG_ref, dU_ref, dF_s = rest[len(after):]
        j = pl.program_id(0)
        i = pl.program_id(1)
        rows = pl.ds(pl.multiple_of(i * tm, tm), tm)

        @pl.when(j == 0)
        def _():
            h, _ = _rms_fwd(x_ref[...], g_ref[...])
            h_ref[...] = h.astype(MXU)
            dF_s[rows, :] = (0.5 * dy_ref[...]).astype(MXU)

        chunks = [slice(r0, r0 + _FFN_CHUNK) for r0 in range(0, tm, _FFN_CHUNK)]
        dAs = [_mm_nt(dF_s[pl.ds(pl.multiple_of(i * tm + rs.start, _FFN_CHUNK), _FFN_CHUNK), :], w2_ref[...])
               for rs in chunks]
        for rs, dA in zip(chunks, dAs):
            G = G_ref[rs, :].astype(F32)
            U = U_ref[rs, :].astype(F32)
            sg = jax.nn.sigmoid(G)
            s = G * sg
            dU_ref[rs, :] = (dA * s).astype(MXU)
            dG_ref[rs, :] = (dA * U * _silu_grad(G, sg)).astype(MXU)

    first = lambda j, i: (jnp.where(j == 0, i, I - 1), 0)
    blk = pl.BlockSpec((None, tm, bf), lambda j, i: (j, i, 0))
    hid = jax.ShapeDtypeStruct((J, T, bf), MXU)
    return _pallas_call(
        body, name=name, grid=(J, I),
        in_specs=[pl.BlockSpec((tm, D), first), pl.BlockSpec((1, D), lambda j, i: (0, 0)), pl.BlockSpec((tm, D), first),
                  blk, blk, pl.BlockSpec((bf, D), lambda j, i: (j, 0))] + [_UNREAD] * len(after),
        out_specs=[pl.BlockSpec((tm, D), first), blk, blk],
        out_shape=[jax.ShapeDtypeStruct((T, D), MXU), hid, hid],
        scratch_shapes=[pltpu.VMEM((T, D), MXU)],
        compiler_params=_params(48, 2),
    )(x, g, dy, Gs, Us, w2, *after)


def _ffn_bwd_input(x, g, dy, dG, dU, w13, name, after=()):
    T, D = x.shape
    _, J, bf, _ = w13.shape
    tm = _tile(T, 512)

    def body(x_ref, g_ref, dy_ref, dG_ref, dU_ref, w13_ref, *rest):
        dx_ref, dg_ref = rest[len(after):]
        i = pl.program_id(0)
        dh = _mm(dG_ref[0], w13_ref[0, 0]) + _mm(dU_ref[0], w13_ref[1, 0])
        for j in range(1, J):
            dh = dh + _mm(dG_ref[j], w13_ref[0, j]) + _mm(dU_ref[j], w13_ref[1, j])
        xv = x_ref[...]
        gv = g_ref[...]
        _, r = _rms_fwd(xv, gv)
        dxn, dgp = _rms_bwd(xv, r, gv, dh)
        dx_ref[...] = dy_ref[...] + dxn

        @pl.when(i == 0)
        def _():
            dg_ref[...] = dgp

        @pl.when(i > 0)
        def _():
            dg_ref[...] += dgp

    row = pl.BlockSpec((tm, D), lambda i: (i, 0))
    blk = pl.BlockSpec((J, tm, bf), lambda i: (0, i, 0))
    return _pallas_call(
        body, name=name, grid=(T // tm,),
        in_specs=[row, pl.BlockSpec((1, D), lambda i: (0, 0)), row, blk, blk,
                  pl.BlockSpec((2, J, bf, D), lambda i: (0, 0, 0, 0))] + [_UNREAD] * len(after),
        out_specs=[row, pl.BlockSpec((1, D), lambda i: (0, 0))],
        out_shape=[jax.ShapeDtypeStruct((T, D), F32), jax.ShapeDtypeStruct((1, D), F32)],
        compiler_params=_params(58, 1),
    )(x, g, dy, dG, dU, w13, *after)


def _ffn_w13_grad(h, dG, dU, name):
    T, D = h.shape
    J, _, bf = dG.shape

    def body(h_ref, dG_ref, dU_ref, dw13_ref):
        dw13_ref[0] = _mm_tn(dG_ref[...], h_ref[...]).astype(dw13_ref.dtype)
        dw13_ref[1] = _mm_tn(dU_ref[...], h_ref[...]).astype(dw13_ref.dtype)

    blk = pl.BlockSpec((None, T, bf), lambda j: (j, 0, 0))
    return _pallas_call(
        body, name=name, grid=(J,),
        in_specs=[pl.BlockSpec((T, D), lambda j: (0, 0)), blk, blk],
        out_specs=pl.BlockSpec((2, None, bf, D), lambda j: (0, j, 0, 0)),
        out_shape=jax.ShapeDtypeStruct((2, J, bf, D), MXU),
        compiler_params=_params(48, 1),
    )(h, dG, dU)


def _ffn_w2_grad(dy, A, name, after=()):
    T, D = dy.shape
    J, _, bf = A.shape

    def body(dy_ref, A_ref, *rest):
        dw2_ref, dF_s = rest[len(after):]

        @pl.when(pl.program_id(0) == 0)
        def _():
            dF_s[...] = (0.5 * dy_ref[...]).astype(MXU)

        dw2_ref[...] = _mm_tn(A_ref[...], dF_s[...]).astype(dw2_ref.dtype)

    return _pallas_call(
        body, name=name, grid=(J,),
        in_specs=[pl.BlockSpec((T, D), lambda j: (0, 0)), pl.BlockSpec((None, T, bf), lambda j: (j, 0, 0))]
        + [_UNREAD] * len(after),
        out_specs=pl.BlockSpec((bf, D), lambda j: (j, 0)),
        out_shape=jax.ShapeDtypeStruct((J * bf, D), MXU),
        scratch_shapes=[pltpu.VMEM((T, D), MXU)],
        compiler_params=_params(48, 1),
    )(dy, A, *after)


_AG0, _Q0, _K0, _V0, _F0 = 0, 2 * D_CONV, 2 * D_CONV + D_ATTN, 2 * D_CONV + 2 * D_ATTN, 2 * D_CONV + 3 * D_ATTN
N_IN = _F0 + N_HEADS
N_IN_PAD = _F0 + LANES


def _inproj_fwd(x1, gm, winp, name):
    T, D = x1.shape
    tm = _tile(T, 256)

    def body(x_ref, g_ref, w_ref, ag_ref, k_ref, v_ref, qT_ref, kT_ref, vT_ref, fl_ref):
        h, _ = _rms_fwd(x_ref[...], g_ref[...])
        hb = h.astype(MXU)
        ag_ref[...] = _mm_nt(hb, w_ref[_AG0:_Q0, :])
        qT_ref[...] = (_mm_nt(hb, w_ref[_Q0:_K0, :]) * SCALE).T.astype(MXU)
        for c0, ref, refT in ((_K0, k_ref, kT_ref), (_V0, v_ref, vT_ref)):
            y = _mm_nt(hb, w_ref[c0:c0 + D_ATTN, :])
            ref[...] = y.astype(MXU)
            refT[...] = y.T.astype(MXU)
        fl_ref[...] = _mm_nt(hb, w_ref[_F0:N_IN_PAD, :])

    row = lambda w: pl.BlockSpec((tm, w), lambda i: (i, 0))
    col = pl.BlockSpec((D_ATTN, tm), lambda i: (0, i))
    std = jax.ShapeDtypeStruct((T, D_ATTN), MXU)
    trn = jax.ShapeDtypeStruct((D_ATTN, T), MXU)
    return _pallas_call(
        body, name=name, grid=(T // tm,),
        in_specs=[row(D), pl.BlockSpec((1, D), lambda i: (0, 0)), pl.BlockSpec((N_IN_PAD, D), lambda i: (0, 0))],
        out_specs=[row(2 * D_CONV), row(D_ATTN), row(D_ATTN), col, col, col, row(LANES)],
        out_shape=[jax.ShapeDtypeStruct((T, 2 * D_CONV), F32), std, std, trn, trn, trn,
                   jax.ShapeDtypeStruct((T, LANES), F32)],
        compiler_params=_params(40, 1),
    )(x1, gm, winp)


def _inproj_bwd_act(x1, gm, dx2, dag, dqT, dkT, dvT, dfl, winp, name):
    T, D = x1.shape
    tm = _tile(T, 256)

    def body(x_ref, g_ref, dx2_ref, dag_ref, dqT_ref, dkT_ref, dvT_ref, dfl_ref, w_ref, dx1_ref, dg_ref, h_ref):
        i = pl.program_id(0)
        xv = x_ref[...]
        gv = g_ref[...]
        h, r = _rms_fwd(xv, gv)
        h_ref[...] = h.astype(MXU)
        dh = _mm(dag_ref[...], w_ref[_AG0:_Q0, :])
        for c0, ref in ((_Q0, dqT_ref), (_K0, dkT_ref), (_V0, dvT_ref)):
            dh = dh + _mm_tn(ref[...].astype(MXU), w_ref[c0:c0 + D_ATTN, :])
        dh = dh + _mm(dfl_ref[...].astype(MXU), w_ref[_F0:N_IN_PAD, :])
        dxn, dgp = _rms_bwd(xv, r, gv, dh)
        dx1_ref[...] = dx2_ref[...] + dxn

        @pl.when(i == 0)
        def _():
            dg_ref[...] = dgp

        @pl.when(i > 0)
        def _():
            dg_ref[...] += dgp

    row = lambda w: pl.BlockSpec((tm, w), lambda i: (i, 0))
    col = pl.BlockSpec((D_ATTN, tm), lambda i: (0, i))
    full = lambda a, b: pl.BlockSpec((a, b), lambda i: (0, 0))
    return _pallas_call(
        body, name=name, grid=(T // tm,),
        in_specs=[row(D), full(1, D), row(D), row(2 * D_CONV), col, col, col, row(LANES), full(N_IN_PAD, D)],
        out_specs=[row(D), full(1, D), row(D)],
        out_shape=[jax.ShapeDtypeStruct((T, D), F32), jax.ShapeDtypeStruct((1, D), F32),
                   jax.ShapeDtypeStruct((T, D), MXU)],
        compiler_params=_params(40, 1),
    )(x1, gm, dx2, dag, dqT, dkT, dvT, dfl, winp)


def _inproj_bwd_weights(h, dag, dqT, dkT, dvT, dfl, name, after=()):
    T, D = h.shape

    def body(h_ref, dag_ref, dqT_ref, dkT_ref, dvT_ref, dfl_ref, *rest):
        dw_ref = rest[len(after)]
        hb = h_ref[...]
        dw_ref[_AG0:_Q0, :] = _mm_tn(dag_ref[...], hb).astype(dw_ref.dtype)
        for c0, ref in ((_Q0, dqT_ref), (_K0, dkT_ref), (_V0, dvT_ref)):
            dw_ref[c0:c0 + D_ATTN, :] = _mm(ref[...].astype(MXU), hb).astype(dw_ref.dtype)
        dw_ref[_F0:N_IN_PAD, :] = _mm_tn(dfl_ref[...].astype(MXU), hb).astype(dw_ref.dtype)

    vmem = pl.BlockSpec(memory_space=pltpu.VMEM)
    return _pallas_call(
        body, name=name, in_specs=[vmem] * 6 + [_UNREAD] * len(after), out_specs=vmem,
        out_shape=jax.ShapeDtypeStruct((N_IN_PAD, D), MXU),
        compiler_params=pltpu.CompilerParams(vmem_limit_bytes=56 * MIB),
    )(h, dag, dqT, dkT, dvT, dfl, *after)


def _forget_fwd(fl, fbp, name):
    T = fl.shape[0]
    tb = _tile(T, 256)

    def body(fl_ref, fb_ref, cum_ref, cumT_ref):
        ri = lax.broadcasted_iota(jnp.int32, (tb, tb), 0)
        ci = lax.broadcasted_iota(jnp.int32, (tb, tb), 1)
        tri = (ri >= ci).astype(jnp.bfloat16)
        carry = jnp.zeros((1, LANES), F32)
        for b in range(T // tb):
            z = fl_ref[b * tb:(b + 1) * tb, :] + fb_ref[...]
            lf = jnp.minimum(z, 0.0) - jnp.log1p(jnp.exp(-jnp.abs(z)))
            c = _exact_tri_dot(tri, lf) + carry
            cum_ref[b * tb:(b + 1) * tb, :] = c
            carry = c[tb - 1:tb, :]
        cumT_ref[...] = cum_ref[...].T[:N_HEADS, :]

    return _pallas_call(
        body, name=name,
        out_shape=[jax.ShapeDtypeStruct((T, LANES), F32), jax.ShapeDtypeStruct((N_HEADS, T), F32)],
        compiler_params=pltpu.CompilerParams(vmem_limit_bytes=32 * MIB),
    )(fl, fbp)


def _forget_bwd(dcum, fl, fbp, name):
    T = fl.shape[0]
    tb = _tile(T, 256)

    def body(dc_ref, fl_ref, fb_ref, dfl_ref, dfb_ref):
        ri = lax.broadcasted_iota(jnp.int32, (tb, tb), 0)
        ci = lax.broadcasted_iota(jnp.int32, (tb, tb), 1)
        tri = (ri <= ci).astype(jnp.bfloat16)
        carry = jnp.zeros((1, LANES), F32)
        dfb = jnp.zeros((1, LANES), F32)
        for b in reversed(range(T // tb)):
            sl = slice(b * tb, (b + 1) * tb)
            dl = _exact_tri_dot(tri, dc_ref[sl, :]) + carry
            carry = dl[0:1, :]
            z = fl_ref[sl, :] + fb_ref[...]
            dfl = dl * jax.nn.sigmoid(-z)
            dfl_ref[sl, :] = dfl
            dfb = dfb + jnp.sum(dfl, axis=0, keepdims=True)
        dfb_ref[...] = dfb

    return _pallas_call(
        body, name=name,
        out_shape=[jax.ShapeDtypeStruct((T, LANES), F32), jax.ShapeDtypeStruct((1, LANES), F32)],
        compiler_params=pltpu.CompilerParams(vmem_limit_bytes=32 * MIB),
    )(dcum, fl, fbp)


def _causal_keep(i, j, tq, tk):
    key = j * tk + lax.broadcasted_iota(jnp.int32, (tk, tq), 0)
    qry = i * tq + lax.broadcasted_iota(jnp.int32, (tk, tq), 1)
    return key <= qry


def _split_hi_lo(x):
    hi = x.astype(MXU)
    lo = (x - hi.astype(F32)).astype(MXU)
    return hi, lo


def _attn_fwd(qT, k, vT, cum, cumT, name):
    T = k.shape[0]
    tq = _tile(T, 256)
    tk = _tile(tq, 128)
    kpq = tq // tk
    heads = [slice(HEAD_DIM * h, HEAD_DIM * (h + 1)) for h in range(N_HEADS)]

    def body(qT_ref, k_ref, vT_ref, cum_ref, cumT_ref, o_ref, lseT_ref, acc_s, m_s, l_s):
        i = pl.program_id(0)
        acc_s[...] = jnp.zeros_like(acc_s)
        m_s[...] = jnp.full_like(m_s, NEG)
        l_s[...] = jnp.zeros_like(l_s)

        def kblock(j, masked):
            rows = pl.ds(pl.multiple_of(j * tk, tk), tk)
            keep = _causal_keep(i, j, tq, tk) if masked else None
            bias = [cumT_ref[h:h + 1, :] - cum_ref[rows, h:h + 1] for h in range(N_HEADS)]
            qk = [_mm(k_ref[rows, hs], qT_ref[hs, :]) + bias[h] for h, hs in enumerate(heads)]
            for h, hs in enumerate(heads):
                sT = qk[h]
                if masked:
                    sT = jnp.where(keep, sT, NEG)
                m_old = m_s[h:h + 1, :]
                m_new = jnp.maximum(m_old, jnp.max(sT, axis=0, keepdims=True))
                alpha = jnp.exp(m_old - m_new)
                pT = jnp.exp(sT - m_new)
                l_s[h:h + 1, :] = alpha * l_s[h:h + 1, :] + jnp.sum(pT, axis=0, keepdims=True)
                p_hi, p_lo = _split_hi_lo(pT)
                vh = vT_ref[hs, rows]
                acc_s[hs, :] = alpha * acc_s[hs, :] + (_mm(vh, p_hi) + _mm(vh, p_lo))
                m_s[h:h + 1, :] = m_new

        def unmasked(j, c):
            kblock(j, False)
            return c

        lax.fori_loop(0, kpq * i, unmasked, 0)
        for d in range(kpq):
            kblock(kpq * i + d, True)
        for h, hs in enumerate(heads):
            acc_s[hs, :] = acc_s[hs, :] / l_s[h:h + 1, :]
        o_ref[...] = acc_s[...].T
        lseT_ref[...] = m_s[...] + jnp.log(l_s[...])

    full = lambda a, b: pl.BlockSpec((a, b), lambda i: (0, 0))
    colblk = lambda r: pl.BlockSpec((r, tq), lambda i: (0, i))
    return _pallas_call(
        body, name=name, grid=(T // tq,),
        in_specs=[colblk(D_ATTN), full(T, D_ATTN), full(D_ATTN, T), full(T, LANES), colblk(N_HEADS)],
        out_specs=[pl.BlockSpec((tq, D_ATTN), lambda i: (i, 0)), colblk(N_HEADS)],
        out_shape=[jax.ShapeDtypeStruct((T, D_ATTN), F32), jax.ShapeDtypeStruct((N_HEADS, T), F32)],
        scratch_shapes=[pltpu.VMEM((D_ATTN, tq), F32), pltpu.VMEM((N_HEADS, tq), F32),
                        pltpu.VMEM((N_HEADS, tq), F32)],
        compiler_params=_params(40, 1),
    )(qT, k, vT, cum, cumT)


def _attn_bwd(qT, k, kT, v, doT, lseT, deltaT, cum, cumT, name, after=()):
    T = k.shape[0]
    tq = _tile(T, 256)
    tk = _tile(tq, 128)
    kpq = tq // tk
    heads = [slice(HEAD_DIM * h, HEAD_DIM * (h + 1)) for h in range(N_HEADS)]

    def body(qT_ref, k_ref, kT_ref, v_ref, doT_ref, lseT_ref, dlT_ref, cum_ref, cumT_ref, *rest):
        dq_ref, dk_ref, dv_ref, dcum_ref, dq_s = rest[len(after):]
        i = pl.program_id(0)

        @pl.when(i == 0)
        def _():
            dk_ref[...] = jnp.zeros_like(dk_ref)
            dv_ref[...] = jnp.zeros_like(dv_ref)
            dcum_ref[...] = jnp.zeros_like(dcum_ref)

        dq_s[...] = jnp.zeros_like(dq_s)

        def kblock(j, masked):
            rows = pl.ds(pl.multiple_of(j * tk, tk), tk)
            keep = _causal_keep(i, j, tq, tk) if masked else None
            bias = [cumT_ref[h:h + 1, :] - cum_ref[rows, h:h + 1] for h in range(N_HEADS)]
            qk = [_mm(k_ref[rows, hs], qT_ref[hs, :]) + bias[h] for h, hs in enumerate(heads)]
            dps = [_mm(v_ref[rows, hs], doT_ref[hs, :]) for hs in heads]
            for h, hs in enumerate(heads):
                sT = qk[h]
                if masked:
                    sT = jnp.where(keep, sT, NEG)
                pT = jnp.exp(sT - lseT_ref[h:h + 1, :])
                dsT = pT * (dps[h] - dlT_ref[h:h + 1, :])
                dcum_ref[rows, h:h + 1] += -jnp.sum(dsT, axis=1, keepdims=True)
                dsb = dsT.astype(MXU)
                dv_ref[hs, rows] += _mm_nt(doT_ref[hs, :], pT.astype(MXU))
                dk_ref[hs, rows] += _mm_nt(qT_ref[hs, :], dsb)
                dq_s[hs, :] += _mm(kT_ref[hs, rows], dsb)

        def unmasked(j, c):
            kblock(j, False)
            return c

        lax.fori_loop(0, kpq * i, unmasked, 0)
        for d in range(kpq):
            kblock(kpq * i + d, True)
        dq_ref[...] = (dq_s[...] * SCALE).astype(dq_ref.dtype)

    full = lambda a, b: pl.BlockSpec((a, b), lambda i: (0, 0))
    colblk = lambda r: pl.BlockSpec((r, tq), lambda i: (0, i))
    return _pallas_call(
        body, name=name, grid=(T // tq,),
        in_specs=[colblk(D_ATTN), full(T, D_ATTN), full(D_ATTN, T), full(T, D_ATTN), colblk(D_ATTN),
                  colblk(N_HEADS), colblk(N_HEADS), full(T, LANES), colblk(N_HEADS)] + [_UNREAD] * len(after),
        out_specs=[colblk(D_ATTN), full(D_ATTN, T), full(D_ATTN, T), full(T, LANES)],
        out_shape=[
            jax.ShapeDtypeStruct((D_ATTN, T), MXU),
            jax.ShapeDtypeStruct((D_ATTN, T), F32),
            jax.ShapeDtypeStruct((D_ATTN, T), F32),
            jax.ShapeDtypeStruct((T, LANES), F32),
        ],
        scratch_shapes=[pltpu.VMEM((D_ATTN, tq), F32)],
        compiler_params=_params(48, 1),
    )(qT, k, kT, v, doT, lseT, deltaT, cum, cumT, *after)


_ROWS_PER_CHUNK = 64


def _glu_halo(ag_ref, agh_ref, uext_s, tm, first):
    a = ag_ref[:, :D_CONV]
    sg = jax.nn.sigmoid(ag_ref[:, D_CONV:])
    uh = agh_ref[:, :D_CONV] * jax.nn.sigmoid(agh_ref[:, D_CONV:])
    uext_s[0:CONV_HALO, :] = jnp.where(first, 0.0, uh)
    uext_s[CONV_HALO:CONV_HALO + tm, :] = a * sg
    return a, sg


_SUBLANES = 8


def _shifted_copies(ext_s, sh_s, rows):
    for k in range(1, _SUBLANES):
        sh_s[k, 0:rows, :] = ext_s[pl.ds(k, rows), :]


def _window(ext_s, sh_s, start, rows):
    k = start % _SUBLANES
    if k == 0:
        return ext_s[pl.ds(start, rows), :]
    return sh_s[k, pl.ds(start - k, rows), :]


def _layer_norm_stats(y):
    mu = jnp.mean(y, axis=-1, keepdims=True)
    xc = y - mu
    rs = lax.rsqrt(jnp.mean(xc * xc, axis=-1, keepdims=True) + EPS)
    return xc * rs, rs


def _conv_fwd(ag, w32, cb, lg, lb, name):
    T = ag.shape[0]
    tm = _tile(T, 256)
    rc = _tile(tm, _ROWS_PER_CHUNK)
    hb = tm // CONV_HALO

    def body(ag_ref, agh_ref, w_ref, cb_ref, lg_ref, lb_ref, yc_ref, c_ref, uext_s, ush_s):
        i = pl.program_id(0)
        _glu_halo(ag_ref, agh_ref, uext_s, tm, i == 0)
        _shifted_copies(uext_s, ush_s, tm + CONV_HALO - _SUBLANES)
        for r0 in range(0, tm, rc):
            acc = jnp.zeros((rc, D_CONV), F32)
            for t in range(CONV_TAPS):
                acc = acc + _window(uext_s, ush_s, r0 + CONV_HALO - (CONV_TAPS - 1) + t, rc) * w_ref[t:t + 1, :]
            y = acc + cb_ref[...]
            yc_ref[r0:r0 + rc, :] = y
            n, _ = _layer_norm_stats(y)
            z = n * lg_ref[...] + lb_ref[...]
            c_ref[r0:r0 + rc, :] = z * jax.nn.sigmoid(z)

    row = lambda w: pl.BlockSpec((tm, w), lambda i: (i, 0))
    full = lambda a, b: pl.BlockSpec((a, b), lambda i: (0, 0))
    return _pallas_call(
        body, name=name, grid=(T // tm,),
        in_specs=[row(2 * D_CONV),
                  pl.BlockSpec((CONV_HALO, 2 * D_CONV), lambda i: (jnp.maximum(i * hb - 1, 0), 0)),
                  full(CONV_HALO, D_CONV), full(1, D_CONV), full(1, D_CONV), full(1, D_CONV)],
        out_specs=[row(D_CONV), row(D_CONV)],
        out_shape=[jax.ShapeDtypeStruct((T, D_CONV), F32), jax.ShapeDtypeStruct((T, D_CONV), F32)],
        scratch_shapes=[pltpu.VMEM((CONV_HALO + tm, D_CONV), F32),
                        pltpu.VMEM((_SUBLANES, CONV_HALO + tm, D_CONV), F32)],
        compiler_params=_params(32, 1),
    )(ag, ag, w32, cb, lg, lb)


def _conv_bwd(dc, yc, ag, w32, lg, lb, name):
    T = ag.shape[0]
    tm = _tile(T, 256)
    rc = _tile(tm, _ROWS_PER_CHUNK)
    I = T // tm
    hb = tm // CONV_HALO
    n_halo_blocks = T // CONV_HALO

    def body(dc_ref, yc_ref, dch_ref, ych_ref, ag_ref, agh_ref, w_ref, lg_ref, lb_ref,
             dag_ref, dw_ref, dcb_ref, dlg_ref, dlb_ref, uext_s, dext_s, ush_s, dsh_s):
        i = pl.program_id(0)
        lgv = lg_ref[...]
        lbv = lb_ref[...]

        def ln_bwd(dcv, ycv):
            n, rs = _layer_norm_stats(ycv)
            z = n * lgv + lbv
            dz = dcv * _silu_grad(z, jax.nn.sigmoid(z))
            dn = dz * lgv
            dy = rs * (dn - jnp.mean(dn, axis=-1, keepdims=True) - n * jnp.mean(dn * n, axis=-1, keepdims=True))
            return dy, dz, n

        dy, dz, n = ln_bwd(dc_ref[...], yc_ref[...])
        dyh, _, _ = ln_bwd(dch_ref[...], ych_ref[...])
        dext_s[0:tm, :] = dy
        dext_s[tm:tm + CONV_HALO, :] = jnp.where(i == I - 1, 0.0, dyh)
        a, sg = _glu_halo(ag_ref, agh_ref, uext_s, tm, i == 0)
        _shifted_copies(uext_s, ush_s, tm + CONV_HALO - _SUBLANES)
        _shifted_copies(dext_s, dsh_s, tm + CONV_HALO - _SUBLANES)

        @pl.when(i == 0)
        def _():
            dw_ref[...] = jnp.zeros_like(dw_ref)
            dcb_ref[...] = jnp.zeros_like(dcb_ref)
            dlg_ref[...] = jnp.zeros_like(dlg_ref)
            dlb_ref[...] = jnp.zeros_like(dlb_ref)

        dcb_ref[...] += jnp.sum(dy, axis=0, keepdims=True)
        dlg_ref[...] += jnp.sum(dz * n, axis=0, keepdims=True)
        dlb_ref[...] += jnp.sum(dz, axis=0, keepdims=True)
        for t in range(CONV_TAPS):
            u_t = _window(uext_s, ush_s, CONV_HALO - (CONV_TAPS - 1) + t, tm)
            dw_ref[t:t + 1, :] += jnp.sum(dy * u_t, axis=0, keepdims=True)
        for r0 in range(0, tm, rc):
            acc = jnp.zeros((rc, D_CONV), F32)
            for t in range(CONV_TAPS):
                acc = acc + _window(dext_s, dsh_s, r0 + (CONV_TAPS - 1) - t, rc) * w_ref[t:t + 1, :]
            a_c = a[r0:r0 + rc, :]
            sg_c = sg[r0:r0 + rc, :]
            dag_ref[r0:r0 + rc, :D_CONV] = (acc * sg_c).astype(dag_ref.dtype)
            dag_ref[r0:r0 + rc, D_CONV:] = (acc * a_c * sg_c * (1.0 - sg_c)).astype(dag_ref.dtype)

    row = lambda w: pl.BlockSpec((tm, w), lambda i: (i, 0))
    full = lambda a, b: pl.BlockSpec((a, b), lambda i: (0, 0))
    nxt = pl.BlockSpec((CONV_HALO, D_CONV), lambda i: (jnp.minimum((i + 1) * hb, n_halo_blocks - 1), 0))
    return _pallas_call(
        body, name=name, grid=(I,),
        in_specs=[row(D_CONV), row(D_CONV), nxt, nxt, row(2 * D_CONV),
                  pl.BlockSpec((CONV_HALO, 2 * D_CONV), lambda i: (jnp.maximum(i * hb - 1, 0), 0)),
                  full(CONV_HALO, D_CONV), full(1, D_CONV), full(1, D_CONV)],
        out_specs=[row(2 * D_CONV), full(CONV_HALO, D_CONV), full(1, D_CONV), full(1, D_CONV), full(1, D_CONV)],
        out_shape=[
            jax.ShapeDtypeStruct((T, 2 * D_CONV), MXU),
            jax.ShapeDtypeStruct((CONV_HALO, D_CONV), F32),
            jax.ShapeDtypeStruct((1, D_CONV), F32),
            jax.ShapeDtypeStruct((1, D_CONV), F32),
            jax.ShapeDtypeStruct((1, D_CONV), F32),
        ],
        scratch_shapes=[pltpu.VMEM((CONV_HALO + tm, D_CONV), F32), pltpu.VMEM((tm + CONV_HALO, D_CONV), F32),
                        pltpu.VMEM((_SUBLANES, CONV_HALO + tm, D_CONV), F32),
                        pltpu.VMEM((_SUBLANES, CONV_HALO + tm, D_CONV), F32)],
        compiler_params=_params(40, 1),
    )(dc, yc, dc, yc, ag, ag, w32, lg, lb)


def _outproj_fwd(x1, c, o, gc, ga, wout, name):
    T, D = x1.shape
    tm = _tile(T, 512)

    def body(x_ref, c_ref, o_ref, gc_ref, ga_ref, w_ref, x2_ref):
        yc, _ = _rms_fwd(c_ref[...], gc_ref[...])
        ya, _ = _rms_fwd(o_ref[...], ga_ref[...])
        x2_ref[...] = (x_ref[...] + _mm(yc.astype(MXU), w_ref[:D_CONV, :])
                       + _mm(ya.astype(MXU), w_ref[D_CONV:, :]))

    row = lambda w: pl.BlockSpec((tm, w), lambda i: (i, 0))
    full = lambda a, b: pl.BlockSpec((a, b), lambda i: (0, 0))
    return _pallas_call(
        body, name=name, grid=(T // tm,),
        in_specs=[row(D), row(D_CONV), row(D_ATTN), full(1, D_CONV), full(1, D_ATTN), full(D_CONV + D_ATTN, D)],
        out_specs=row(D),
        out_shape=jax.ShapeDtypeStruct((T, D), F32),
        compiler_params=_params(32, 1),
    )(x1, c, o, gc, ga, wout)


def _outproj_bwd(dx2, c, o, gc, ga, wout, name):
    T, D = dx2.shape
    tm = _tile(T, 256)
    I = T // tm

    def body(dx_ref, c_ref, o_ref, gc_ref, ga_ref, w_ref,
             dc_ref, doT_ref, dlT_ref, dw_ref, dgc_ref, dga_ref, acc_s):
        i = pl.program_id(0)
        dxb = dx_ref[...].astype(MXU)
        cv = c_ref[...]
        ov = o_ref[...]
        yc, rcn = _rms_fwd(cv, gc_ref[...])
        ya, ra = _rms_fwd(ov, ga_ref[...])
        dyc = _mm_nt(dxb, w_ref[:D_CONV, :])
        dya = _mm_nt(dxb, w_ref[D_CONV:, :])
        dwc = _mm_tn(yc.astype(MXU), dxb)
        dwa = _mm_tn(ya.astype(MXU), dxb)
        dcv, dgc = _rms_bwd(cv, rcn, gc_ref[...], dyc)
        dov, dga = _rms_bwd(ov, ra, ga_ref[...], dya)
        dc_ref[...] = dcv
        dob = dov.astype(doT_ref.dtype)
        doT_ref[...] = dov.T.astype(doT_ref.dtype)
        chan = lax.broadcasted_iota(jnp.int32, (D_ATTN, LANES), 0)
        head = lax.broadcasted_iota(jnp.int32, (D_ATTN, LANES), 1)
        in_head = ((chan >= head * HEAD_DIM) & (chan < (head + 1) * HEAD_DIM)).astype(jnp.bfloat16)
        dlT_ref[...] = _exact_dot_01(dob.astype(F32) * ov, in_head).T[:N_HEADS, :]

        @pl.when(i == 0)
        def _():
            acc_s[:D_CONV, :] = dwc
            acc_s[D_CONV:, :] = dwa
            dgc_ref[...] = dgc
            dga_ref[...] = dga

        @pl.when(i > 0)
        def _():
            acc_s[:D_CONV, :] += dwc
            acc_s[D_CONV:, :] += dwa
            dgc_ref[...] += dgc
            dga_ref[...] += dga

        @pl.when(i == I - 1)
        def _():
            dw_ref[...] = acc_s[...].astype(dw_ref.dtype)

    row = lambda w: pl.BlockSpec((tm, w), lambda i: (i, 0))
    full = lambda a, b: pl.BlockSpec((a, b), lambda i: (0, 0))
    return _pallas_call(
        body, name=name, grid=(I,),
        in_specs=[row(D), row(D_CONV), row(D_ATTN), full(1, D_CONV), full(1, D_ATTN), full(D_CONV + D_ATTN, D)],
        out_specs=[row(D_CONV), pl.BlockSpec((D_ATTN, tm), lambda i: (0, i)),
                   pl.BlockSpec((N_HEADS, tm), lambda i: (0, i)),
                   full(D_CONV + D_ATTN, D), full(1, D_CONV), full(1, D_ATTN)],
        out_shape=[
            jax.ShapeDtypeStruct((T, D_CONV), F32),
            jax.ShapeDtypeStruct((D_ATTN, T), MXU),
            jax.ShapeDtypeStruct((N_HEADS, T), F32),
            jax.ShapeDtypeStruct((D_CONV + D_ATTN, D), MXU),
            jax.ShapeDtypeStruct((1, D_CONV), F32),
            jax.ShapeDtypeStruct((1, D_ATTN), F32),
        ],
        scratch_shapes=[pltpu.VMEM((D_CONV + D_ATTN, D), F32)],
        compiler_params=_params(40, 1),
    )(dx2, c, o, gc, ga, wout)


def _loss_head(x3, gf, target, name):
    T, D = x3.shape
    tm = _tile(T, 512)

    def body(x_ref, g_ref, t_ref, loss_ref, dx_ref, dg_ref):
        i = pl.program_id(0)
        xv = x_ref[...]
        gv = g_ref[...]
        out, r = _rms_fwd(xv, gv)
        err = out - t_ref[...]
        part = jnp.full((1, LANES), 0.5 / D, F32) * jnp.sum(err * err)
        dxn, dgp = _rms_bwd(xv, r, gv, err * (1.0 / D))
        dx_ref[...] = dxn

        @pl.when(i == 0)
        def _():
            loss_ref[...] = part
            dg_ref[...] = dgp

        @pl.when(i > 0)
        def _():
            loss_ref[...] += part
            dg_ref[...] += dgp

    row = lambda w: pl.BlockSpec((tm, w), lambda i: (i, 0))
    full = lambda a, b: pl.BlockSpec((a, b), lambda i: (0, 0))
    return _pallas_call(
        body, name=name, grid=(T // tm,),
        in_specs=[row(D), full(1, D), row(D)],
        out_specs=[full(1, LANES), row(D), full(1, D)],
        out_shape=[jax.ShapeDtypeStruct((1, LANES), F32), jax.ShapeDtypeStruct((T, D), F32),
                   jax.ShapeDtypeStruct((1, D), F32)],
        compiler_params=_params(32, 1),
    )(x3, gf, target)


def _row_tile(rows):
    for cand in (256, 176, 128, 64, 32, 16):
        if rows % cand == 0:
            return cand
    return rows


def _adamw(w, m, v, parts, name):
    R, C = w.shape
    P = parts.shape[0]
    tr = _row_tile(R)
    c1 = 1.0 - ADAM_B1 ** ADAM_STEP
    c2 = 1.0 - ADAM_B2 ** ADAM_STEP

    def body(w_ref, m_ref, v_ref, p_ref, g_ref, d_ref, nm_ref, nv_ref):
        g = p_ref[0].astype(F32)
        for s in range(1, P):
            g = g + p_ref[s].astype(F32)
        wv = w_ref[...]
        mn = ADAM_B1 * m_ref[...] + (1.0 - ADAM_B1) * g
        vn = ADAM_B2 * v_ref[...] + (1.0 - ADAM_B2) * (g * g)
        g_ref[...] = g
        nm_ref[...] = mn
        nv_ref[...] = vn
        d_ref[...] = -ADAM_LR * ((mn / c1) / (jnp.sqrt(vn / c2) + ADAM_EPS) + ADAM_WD * wv)

    blk = pl.BlockSpec((tr, C), lambda i: (i, 0))
    out = jax.ShapeDtypeStruct((R, C), F32)
    return _pallas_call(
        body, name=name, grid=(R // tr,),
        in_specs=[blk, blk, blk, pl.BlockSpec((P, tr, C), lambda i: (0, i, 0))],
        out_specs=[blk, blk, blk, blk],
        out_shape=[out, out, out, out],
        compiler_params=_params(32, 1),
    )(w, m, v, parts)


def _position():
    return lax.axis_index("x"), lax.axis_index("y"), lax.axis_index("c")


def _flat(px, py, pc):
    return 4 * px + 2 * py + pc


def _gather_body(ins, outs, send_sems, recv_sems, local_sems, handshake):
    n = len(ins)
    x, y, c = _position()
    me, sibling = (x, y, c), (x, y, 1 - c)
    chips = [(1 - x, y), (x, 1 - y), (1 - x, 1 - y)]
    if handshake:
        _handshake([sibling] + [(*chip, cc) for chip in chips for cc in (c, 1 - c)])

    def copy(a, k, block, to, src=None):
        dst = outs[a].at[_flat(*block)]
        return pltpu.make_async_remote_copy(
            src_ref=dst if src is None else src, dst_ref=dst,
            send_sem=send_sems.at[a, k], recv_sem=recv_sems.at[a, k],
            device_id=to, device_id_type=MESH)

    mine = [pltpu.make_async_copy(ins[a], outs[a].at[_flat(*me)], local_sems.at[a]) for a in range(n)]
    for cp in mine:
        cp.start()
    first = []
    for a in range(n):
        first.append(copy(a, 0, me, sibling, src=ins[a]))
        first += [copy(a, 1 + j, me, (*chip, c), src=ins[a]) for j, chip in enumerate(chips)]
    for cp in first:
        cp.start()
    passed = []
    for a in range(n):
        for j, chip in enumerate(chips):
            copy(a, 1 + j, (*chip, c), me).wait_recv()
            fwd = copy(a, 4 + j, (*chip, c), sibling)
            fwd.start()
            passed.append(fwd)
    for a in range(n):
        copy(a, 0, sibling, me).wait_recv()
        for j, chip in enumerate(chips):
            copy(a, 4 + j, (*chip, 1 - c), me).wait_recv()
    for cp in first + passed:
        cp.wait_send()
    for cp in mine:
        cp.wait()


def _gather_scratch(n):
    return [pltpu.SemaphoreType.DMA((n, 7)), pltpu.SemaphoreType.DMA((n, 7)), pltpu.SemaphoreType.DMA((n,))]


def _all_gather(shards, name):
    n = len(shards)

    def body(*refs):
        _gather_body(refs[:n], refs[n:2 * n], *refs[2 * n:], handshake=False)

    hbm = pl.BlockSpec(memory_space=pltpu.HBM)
    return _pallas_call(
        body, name=name,
        in_specs=[hbm] * n, out_specs=[hbm] * n,
        out_shape=[jax.ShapeDtypeStruct((N_DEV,) + s.shape, s.dtype) for s in shards],
        scratch_shapes=_gather_scratch(n),
    )(*shards)


def _handshake(peers):
    barrier = pltpu.get_barrier_semaphore()
    for peer in peers:
        pl.semaphore_signal(barrier, inc=1, device_id=peer, device_id_type=MESH)
    pl.semaphore_wait(barrier, len(peers))


def _sequencer_call(body, name, collective_id, out_type, scratch_types, operands):
    return pl.kernel(
        body, name=name, out_type=out_type,
        mesh=plsc.ScalarSubcoreMesh(axis_name="sequencer", num_cores=1),
        scratch_types=scratch_types,
        compiler_params=pltpu.CompilerParams(collective_id=collective_id),
    )(*operands)


def _seq_all_gather(shards, name, collective_id, after):
    n = len(shards)

    def body(*refs):
        _gather_body(refs[:n], refs[n + 1:2 * n + 1], *refs[2 * n + 1:], handshake=True)

    return _sequencer_call(
        body, name, collective_id,
        [jax.ShapeDtypeStruct((N_DEV,) + s.shape, s.dtype) for s in shards],
        _gather_scratch(n), list(shards) + [after])


def _seq_to_sibling(parts, name, collective_id, after):
    n = len(parts)

    def body(*refs):
        ins, outs = refs[:n], refs[n + len(after):2 * n + len(after)]
        send_sems, recv_sems = refs[2 * n + len(after):]
        x, y, c = _position()
        sibling = (x, y, 1 - c)
        _handshake([sibling])
        sent = []
        for a in range(n):
            for q in range(N_CHIPS):
                cp = pltpu.make_async_remote_copy(
                    src_ref=ins[a].at[2 * q + (1 - c)], dst_ref=outs[a].at[q],
                    send_sem=send_sems.at[a, q], recv_sem=recv_sems.at[a, q],
                    device_id=sibling, device_id_type=MESH)
                cp.start()
                sent.append(cp)
        for cp in sent:
            cp.wait_recv()
        for cp in sent:
            cp.wait_send()

    return _sequencer_call(
        body, name, collective_id,
        [jax.ShapeDtypeStruct((N_CHIPS,) + p.shape[1:], p.dtype) for p in parts],
        [pltpu.SemaphoreType.DMA((n, N_CHIPS)), pltpu.SemaphoreType.DMA((n, N_CHIPS))],
        list(parts) + list(after))


def _seq_to_chips(partials, name, collective_id, after=()):
    n = len(partials)

    def body(*refs):
        ins, outs = refs[:n], refs[n + len(after):2 * n + len(after)]
        send_sems, recv_sems, local_sems = refs[2 * n + len(after):]
        x, y, c = _position()
        my_chip = 2 * x + y
        chips = [(1 - x, y), (x, 1 - y), (1 - x, 1 - y)]
        _handshake([(*chip, c) for chip in chips])
        mine = [pltpu.make_async_copy(ins[a].at[my_chip], outs[a].at[my_chip], local_sems.at[a]) for a in range(n)]
        for cp in mine:
            cp.start()
        sent = []
        for a in range(n):
            for j, (px, py) in enumerate(chips):
                cp = pltpu.make_async_remote_copy(
                    src_ref=ins[a].at[2 * px + py], dst_ref=outs[a].at[my_chip],
                    send_sem=send_sems.at[a, j], recv_sem=recv_sems.at[a, j],
                    device_id=(px, py, c), device_id_type=MESH)
                cp.start()
                sent.append(cp)
        for cp in sent:
            cp.wait_recv()
        for cp in sent:
            cp.wait_send()
        for cp in mine:
            cp.wait()

    return _sequencer_call(
        body, name, collective_id,
        [jax.ShapeDtypeStruct(p.shape, p.dtype) for p in partials],
        [pltpu.SemaphoreType.DMA((n, 3)), pltpu.SemaphoreType.DMA((n, 3)), pltpu.SemaphoreType.DMA((n,))],
        list(partials) + list(after))


def _pair_add(part, recv, name, after=()):
    _, R, C = part.shape
    core = lax.axis_index("c").astype(jnp.int32).reshape(1)

    def body(c_ref, p_ref, r_ref, *rest):
        o_ref = rest[len(after)]
        o_ref[...] = (p_ref[...].astype(F32) + r_ref[...].astype(F32)).astype(o_ref.dtype)

    blk = pl.BlockSpec((None, R, C), lambda q, c: (q, 0, 0))
    return pl.pallas_call(
        body, name=name,
        grid_spec=pltpu.PrefetchScalarGridSpec(
            num_scalar_prefetch=1, grid=(N_CHIPS,),
            in_specs=[pl.BlockSpec((None, R, C), lambda q, c: (2 * q + c[0], 0, 0)), blk] + [_UNREAD] * len(after),
            out_specs=blk),
        out_shape=pltpu.HBM((N_CHIPS, R, C), part.dtype),
        compiler_params=_params(32, 1),
    )(core, *[pltpu.with_memory_space_constraint(a, pltpu.HBM) for a in (part, recv, *after)])


class _Reduced(NamedTuple):
    partials: list
    reduced: list


def _blocks(g):
    return g.reshape(N_DEV, -1, g.shape[-1])


def _reduce_scatter(parts, tag, ids, after=(), between=None, add_after=()):
    from_sibling = _seq_to_sibling(parts, "rs_sibling_" + tag, ids[0], after)
    mid = between(from_sibling[0]) if between else ()
    partials = [_pair_add(p, r, "rs_add_%s_%d" % (tag, a), add_after)
                for a, (p, r) in enumerate(zip(parts, from_sibling))]
    return _Reduced(partials, _seq_to_chips(partials, "rs_chips_" + tag, ids[1], mid))


_SMALL = ("ffn1_norm", "mix_norm", "conv_b", "conv_ln_g", "conv_ln_b", "forget_b", "out_norm_conv",
          "out_norm_attn", "ffn2_norm", "final_norm")
_PACK_WIDTH = 2 * D_CONV
_SLOT = dict(ffn1_norm=(0, 0), mix_norm=(1, 0), ffn2_norm=(2, 0), final_norm=(3, 0), conv_b=(4, 0),
             conv_ln_g=(4, D_CONV), conv_ln_b=(5, 0), out_norm_conv=(5, D_CONV), out_norm_attn=(6, 0),
             forget_b=(6, D_CONV))
_CONV_ROW0 = 8
_PACK_ROWS = _CONV_ROW0 + CONV_HALO


def _pack_small(small, name, after=()):
    arrays = [small[n] for n in _SMALL] + [small["conv_w"]]

    def body(*refs):
        out = refs[-1]
        out[...] = jnp.zeros_like(out)
        for n, ref in zip(_SMALL, refs):
            row, lane = _SLOT[n]
            out[row:row + 1, lane:lane + ref.shape[1]] = ref[...]
        out[_CONV_ROW0:, :D_CONV] = refs[len(_SMALL)][...]

    vmem = pl.BlockSpec(memory_space=pltpu.VMEM)
    return _pallas_call(body, name=name, in_specs=[vmem] * len(arrays) + [_UNREAD] * len(after), out_specs=vmem,
                        out_shape=jax.ShapeDtypeStruct((_PACK_ROWS, _PACK_WIDTH), F32))(*arrays, *after)


def _adamw_small(gathered, w, m, v, name):
    c1 = 1.0 - ADAM_B1 ** ADAM_STEP
    c2 = 1.0 - ADAM_B2 ** ADAM_STEP
    k = len(_SMALL)

    def body(g_ref, *refs):
        ws, ms, vs = refs[:k], refs[k:2 * k], refs[2 * k:3 * k]
        outs = refs[3 * k:]
        total = g_ref[0]
        for s in range(1, N_DEV):
            total = total + g_ref[s]
        for i, n in enumerate(_SMALL):
            row, lane = _SLOT[n]
            width = ws[i].shape[1]
            g = total[row:row + 1, lane:lane + width]
            mn = ADAM_B1 * ms[i][...] + (1.0 - ADAM_B1) * g
            vn = ADAM_B2 * vs[i][...] + (1.0 - ADAM_B2) * (g * g)
            o_g, o_d, o_m, o_v = outs[4 * i:4 * i + 4]
            o_g[...] = g
            o_m[...] = mn
            o_v[...] = vn
            o_d[...] = -ADAM_LR * ((mn / c1) / (jnp.sqrt(vn / c2) + ADAM_EPS) + ADAM_WD * ws[i][...])
        outs[4 * k][...] = total[_CONV_ROW0:, :D_CONV]

    shapes = []
    for n in _SMALL:
        shapes += [jax.ShapeDtypeStruct(w[n].shape, F32)] * 4
    shapes.append(jax.ShapeDtypeStruct((CONV_HALO, D_CONV), F32))
    res = _pallas_call(body, name=name, out_shape=shapes)(
        gathered, *[w[n] for n in _SMALL], *[m[n] for n in _SMALL], *[v[n] for n in _SMALL])
    return {n: res[4 * i:4 * i + 4] for i, n in enumerate(_SMALL)}, res[4 * k]


def _local_step(x, target, norms, shard):
    D = x.shape[1]
    J = N_DEV // 2
    as13 = lambda g: g.reshape(2, J, g.shape[1], D)

    (g13_1,) = _all_gather([shard["ffn1_w13"]], "gather_ffn1_w13")
    (g2_1,) = _seq_all_gather([shard["ffn1_w2"]], "gather_ffn1_w2", 10, after=g13_1)
    w13_1 = as13(g13_1)
    G1, U1, A1 = _ffn_up(x, norms["ffn1_norm"], w13_1, "ffn1_up")
    gin, gconv = _seq_all_gather([shard["w_in"], shard["conv_w"]], "gather_mix", 1, after=G1)
    w2_1 = g2_1.reshape(-1, D)
    x1 = _ffn_down(x, A1, w2_1, "ffn1_down")
    gout, g13_2, g2_2 = _seq_all_gather([shard["w_out"], shard["ffn2_w13"], shard["ffn2_w2"]], "gather_ffn2", 2,
                                        after=x1)
    winp = jnp.pad(gin.reshape(N_IN, D), ((0, N_IN_PAD - N_IN), (0, 0)))
    wout = gout.reshape(-1, D)
    conv_w32 = jnp.pad(gconv.transpose(1, 0, 2).reshape(CONV_TAPS, D_CONV), ((0, CONV_HALO - CONV_TAPS), (0, 0)))

    ag, k, v, qT, kT, vT, fl = _inproj_fwd(x1, norms["mix_norm"], winp, "inproj_fwd")
    cum, cumT = _forget_fwd(fl, norms["forget_b"], "forget_fwd")
    yc, c = _conv_fwd(ag, conv_w32, norms["conv_b"], norms["conv_ln_g"], norms["conv_ln_b"], "conv_fwd")
    o, lseT = _attn_fwd(qT, k, vT, cum, cumT, "attn_fwd")
    x2 = _outproj_fwd(x1, c, o, norms["out_norm_conv"], norms["out_norm_attn"], wout, "outproj_fwd")
    w13_2, w2_2 = as13(g13_2), g2_2.reshape(-1, D)
    G2, U2, A2 = _ffn_up(x2, norms["ffn2_norm"], w13_2, "ffn2_up")
    x3 = _ffn_down(x2, A2, w2_2, "ffn2_down")
    loss, dx3, d_final = _loss_head(x3, norms["final_norm"], target, "loss_head")

    dw2_2 = _ffn_w2_grad(dx3, A2, "ffn2_w2_grad")
    h3, dG2, dU2 = _ffn_bwd_gates(x2, norms["ffn2_norm"], dx3, G2, U2, w2_2, "ffn2_bwd_gates")
    dx2, d_ffn2n = _ffn_bwd_input(x2, norms["ffn2_norm"], dx3, dG2, dU2, w13_2, "ffn2_bwd_input")
    dw13_2 = _ffn_w13_grad(h3, dG2, dU2, "ffn2_w13_grad")
    dc, dobT, deltaT, dwout, d_onc, d_ona = _outproj_bwd(
        dx2, c, o, norms["out_norm_conv"], norms["out_norm_attn"], wout, "outproj_bwd")
    red_ffn2 = _reduce_scatter([_blocks(dw13_2), _blocks(dw2_2)], "ffn2", (3, 4), add_after=(dc,))
    dqT, dkT, dvT, dcum = _attn_bwd(qT, k, kT, v, dobT, lseT, deltaT, cum, cumT, "attn_bwd",
                                    after=red_ffn2.partials)
    dfl, d_fb = _forget_bwd(dcum, fl, norms["forget_b"], "forget_bwd")
    dag, d_convw, d_cb, d_lg, d_lb = _conv_bwd(dc, yc, ag, conv_w32, norms["conv_ln_g"], norms["conv_ln_b"], "conv_bwd")
    dx1, d_mixn, h2 = _inproj_bwd_act(x1, norms["mix_norm"], dx2, dag, dqT, dkT, dvT, dfl, winp, "inproj_bwd_act")
    dw2_1 = _ffn_w2_grad(dx1, A1, "ffn1_w2_grad")
    red_w2_1 = _reduce_scatter([_blocks(dw2_1)], "ffn1_w2", (11, 12), after=red_ffn2.reduced[:1])
    dwinp = _inproj_bwd_weights(h2, dag, dqT, dkT, dvT, dfl, "inproj_bwd_weights", after=red_w2_1.partials)
    dwin_blocks = dwinp[:N_IN].reshape(N_DEV, N_IN // N_DEV, -1)
    red_mix = _reduce_scatter([dwin_blocks, _blocks(dwout)], "mix", (5, 6), after=red_w2_1.reduced[:1])
    h1, dG1, dU1 = _ffn_bwd_gates(x, norms["ffn1_norm"], dx1, G1, U1, w2_1, "ffn1_bwd_gates", after=red_mix.partials)
    dw13_1 = _ffn_w13_grad(h1, dG1, dU1, "ffn1_w13_grad")
    red_w13_1 = _reduce_scatter([_blocks(dw13_1)], "ffn1_w13", (7, 8), after=red_mix.reduced[:1])
    dx, d_ffn1n = _ffn_bwd_input(x, norms["ffn1_norm"], dx1, dG1, dU1, w13_1, "ffn1_bwd_input",
                                 after=red_w13_1.partials)

    small = dict(ffn1_norm=d_ffn1n, mix_norm=d_mixn, conv_b=d_cb, conv_ln_g=d_lg, conv_ln_b=d_lb,
                 forget_b=d_fb, out_norm_conv=d_onc, out_norm_attn=d_ona, ffn2_norm=d_ffn2n,
                 final_norm=d_final, conv_w=d_convw)
    big = dict(ffn1_w13=red_w13_1.reduced[0], ffn1_w2=red_w2_1.reduced[0], w_in=red_mix.reduced[0],
               w_out=red_mix.reduced[1], ffn2_w13=red_ffn2.reduced[0], ffn2_w2=red_ffn2.reduced[1])
    return loss[0, 0], dx, small, big


_BIG = ("ffn1_w13", "ffn1_w2", "w_in", "w_out", "ffn2_w13", "ffn2_w2")
_TRANSPOSED = ("ffn1_w13", "ffn2_w13", "w_in")
_ORDER = ("ffn1_norm", "ffn1_w13", "ffn1_w2", "mix_norm", "w_in", "conv_w", "conv_b", "conv_ln_g", "conv_ln_b",
          "forget_b", "out_norm_conv", "out_norm_attn", "w_out", "ffn2_norm", "ffn2_w13", "ffn2_w2", "final_norm")


def kernel(x, ffn1_norm, ffn1_w13, ffn1_w2, mix_norm, w_in, conv_w, conv_b, conv_ln_g, conv_ln_b, forget_b, out_norm_conv, out_norm_attn, w_out, ffn2_norm, ffn2_w13, ffn2_w2, final_norm, loss_target, m_ffn1_norm, m_ffn1_w13, m_ffn1_w2, m_mix_norm, m_w_in, m_conv_w, m_conv_b, m_conv_ln_g, m_conv_ln_b, m_forget_b, m_out_norm_conv, m_out_norm_attn, m_w_out, m_ffn2_norm, m_ffn2_w13, m_ffn2_w2, m_final_norm, v_ffn1_norm, v_ffn1_w13, v_ffn1_w2, v_mix_norm, v_w_in, v_conv_w, v_conv_b, v_conv_ln_g, v_conv_ln_b, v_forget_b, v_out_norm_conv, v_out_norm_attn, v_w_out, v_ffn2_norm, v_ffn2_w13, v_ffn2_w2, v_final_norm):
    w = dict(ffn1_norm=ffn1_norm, ffn1_w13=ffn1_w13, ffn1_w2=ffn1_w2, mix_norm=mix_norm, w_in=w_in, conv_w=conv_w,
             conv_b=conv_b, conv_ln_g=conv_ln_g, conv_ln_b=conv_ln_b, forget_b=forget_b, out_norm_conv=out_norm_conv,
             out_norm_attn=out_norm_attn, w_out=w_out, ffn2_norm=ffn2_norm, ffn2_w13=ffn2_w13, ffn2_w2=ffn2_w2,
             final_norm=final_norm)
    m = dict(ffn1_norm=m_ffn1_norm, ffn1_w13=m_ffn1_w13, ffn1_w2=m_ffn1_w2, mix_norm=m_mix_norm, w_in=m_w_in,
             conv_w=m_conv_w, conv_b=m_conv_b, conv_ln_g=m_conv_ln_g, conv_ln_b=m_conv_ln_b, forget_b=m_forget_b,
             out_norm_conv=m_out_norm_conv, out_norm_attn=m_out_norm_attn, w_out=m_w_out, ffn2_norm=m_ffn2_norm,
             ffn2_w13=m_ffn2_w13, ffn2_w2=m_ffn2_w2, final_norm=m_final_norm)
    v = dict(ffn1_norm=v_ffn1_norm, ffn1_w13=v_ffn1_w13, ffn1_w2=v_ffn1_w2, mix_norm=v_mix_norm, w_in=v_w_in,
             conv_w=v_conv_w, conv_b=v_conv_b, conv_ln_g=v_conv_ln_g, conv_ln_b=v_conv_ln_b, forget_b=v_forget_b,
             out_norm_conv=v_out_norm_conv, out_norm_attn=v_out_norm_attn, w_out=v_w_out, ffn2_norm=v_ffn2_norm,
             ffn2_w13=v_ffn2_w13, ffn2_w2=v_ffn2_w2, final_norm=v_final_norm)
    shapes = {n: a.shape for n, a in w.items()}
    T, D = x.shape[1], x.shape[2]
    def two(n, a):
        if a.ndim != 3:
            return a.reshape(1, -1)
        a = a.reshape(a.shape[-2], a.shape[-1])
        return a.T if n in _TRANSPOSED else a

    w2d = {n: two(n, a) for n, a in w.items()}
    m2d = {n: two(n, a) for n, a in m.items()}
    v2d = {n: two(n, a) for n, a in v.items()}

    shard = {n: w2d[n].astype(MXU) for n in _BIG}
    shard["conv_w"] = w2d["conv_w"]
    norms = {n: w2d[n] for n in _SMALL}
    norms["forget_b"] = jnp.pad(w2d["forget_b"], ((0, 0), (0, LANES - N_HEADS)))
    loss_part, dx, small, big = _local_step(x[0], loss_target[0], norms, shard)
    loss = lax.psum(loss_part, ("x", "y", "c"))

    grads, deltas, new_m, new_v = {}, {}, {}, {}
    last = "ffn1_w13"
    for n in [n for n in _BIG if n != last] + [last]:
        if n == last:
            packed_small = _pack_small(small, "pack_small_grads", after=[deltas[e] for e in _BIG if e != last])
            (gathered_small,) = _seq_all_gather([packed_small], "gather_small_grads", 9, after=big[last])
        g, d, nm, nv = _adamw(w2d[n], m2d[n], v2d[n], big[n], "adamw_" + n)
        grads[n], deltas[n], new_m[n], new_v[n] = g, d, nm, nv

    small_out, conv_g_full = _adamw_small(gathered_small, w2d, m2d, v2d, "adamw_small")
    for n in _SMALL:
        grads[n], deltas[n], new_m[n], new_v[n] = small_out[n]
    conv_g_full = conv_g_full[:CONV_TAPS]
    xi, yi, ci = _position()
    cw = shapes["conv_w"][-1]
    conv_g_mine = lax.dynamic_slice_in_dim(conv_g_full, _flat(xi, yi, ci) * cw, cw, axis=1)
    g, d, nm, nv = _adamw(w2d["conv_w"], m2d["conv_w"], v2d["conv_w"], conv_g_mine[None], "adamw_conv_w")
    grads["conv_w"], deltas["conv_w"], new_m["conv_w"], new_v["conv_w"] = g, d, nm, nv

    shaped = lambda dct: [(dct[n].T if n in _TRANSPOSED else dct[n]).reshape(shapes[n]) for n in _ORDER]
    return (loss, dx[None], *shaped(grads), *shaped(deltas), *shaped(new_m), *shaped(new_v))
```

```python
from typing import NamedTuple

import jax
import jax.numpy as jnp
from jax import lax
from jax.experimental import pallas as pl
from jax.experimental.pallas import tpu as pltpu
from jax.experimental.pallas import tpu_sc as plsc

F32 = jnp.float32
MXU = jnp.bfloat16
EPS = 1e-6
N_HEADS = 8
HEAD_DIM = 64
D_CONV = 512
D_ATTN = N_HEADS * HEAD_DIM
CONV_TAPS = 31
CONV_HALO = 32
SCALE = HEAD_DIM ** -0.5
NEG = -1e30
LANES = 128
N_DEV = 8
N_CHIPS = N_DEV // 2
MESH = pl.DeviceIdType.MESH
MIB = 1 << 20

ADAM_LR = 0.001
ADAM_B1 = 0.9
ADAM_B2 = 0.999
ADAM_EPS = 1e-08
ADAM_WD = 0.01
ADAM_STEP = 10


_UNREAD = pl.BlockSpec(memory_space=pl.ANY)


def _pallas_call(body, *, out_shape, **kwargs):
    in_hbm = lambda s: pltpu.HBM(s.shape, s.dtype)
    outs = [in_hbm(s) for s in out_shape] if isinstance(out_shape, (list, tuple)) else in_hbm(out_shape)
    call = pl.pallas_call(body, out_shape=outs, **kwargs)
    return lambda *operands: call(*[pltpu.with_memory_space_constraint(a, pltpu.HBM) for a in operands])


def _params(vmem_mib, n_axes):
    return pltpu.CompilerParams(dimension_semantics=("arbitrary",) * n_axes, vmem_limit_bytes=vmem_mib * MIB)


def _mm(a, b):
    return jnp.dot(a, b, preferred_element_type=F32)


def _mm_nt(a, b):
    return lax.dot_general(a, b, (((1,), (1,)), ((), ())), preferred_element_type=F32)


def _mm_tn(a, b):
    return lax.dot_general(a, b, (((0,), (0,)), ((), ())), preferred_element_type=F32)


def _rms_fwd(x, g):
    r = lax.rsqrt(jnp.mean(x * x, axis=-1, keepdims=True) + EPS)
    return x * r * g, r


def _rms_bwd(x, r, g, dy):
    gdy = dy * g
    dx = r * gdy - x * (r * r * r) * jnp.mean(x * gdy, axis=-1, keepdims=True)
    dg = jnp.sum(dy * x * r, axis=0, keepdims=True)
    return dx, dg


def _silu_grad(z, sz):
    return sz * (1.0 + z * (1.0 - sz))


def _three_terms(x):
    x1 = x.astype(jnp.bfloat16)
    r1 = x - x1.astype(F32)
    x2 = r1.astype(jnp.bfloat16)
    x3 = (r1 - x2.astype(F32)).astype(jnp.bfloat16)
    return x1, x2, x3


def _exact_tri_dot(tri, x):
    x1, x2, x3 = _three_terms(x)
    return _mm(tri, x1) + _mm(tri, x2) + _mm(tri, x3)


def _exact_dot_01(x, sel):
    x1, x2, x3 = _three_terms(x)
    return _mm(x1, sel) + _mm(x2, sel) + _mm(x3, sel)


def _tile(n, want):
    t = min(n, want)
    assert n % t == 0
    return t


_FFN_CHUNK = 256


def _ffn_up(x, g, w13, name):
    T, D = x.shape
    _, J, bf, _ = w13.shape
    tm = _tile(T, 512)
    I = T // tm

    def body(x_ref, g_ref, w13_ref, G_ref, U_ref, A_ref, h_s):
        j = pl.program_id(0)
        i = pl.program_id(1)
        rows = pl.ds(pl.multiple_of(i * tm, tm), tm)

        @pl.when(j == 0)
        def _():
            h, _ = _rms_fwd(x_ref[...], g_ref[...])
            h_s[rows, :] = h.astype(MXU)

        chunks = [slice(r0, r0 + _FFN_CHUNK) for r0 in range(0, tm, _FFN_CHUNK)]
        hbs = [h_s[pl.ds(pl.multiple_of(i * tm + rs.start, _FFN_CHUNK), _FFN_CHUNK), :] for rs in chunks]
        GU = [(_mm_nt(hb, w13_ref[0]), _mm_nt(hb, w13_ref[1])) for hb in hbs]
        for rs, (G, U) in zip(chunks, GU):
            G_ref[rs, :] = G.astype(MXU)
            U_ref[rs, :] = U.astype(MXU)
            A_ref[rs, :] = (G * jax.nn.sigmoid(G) * U).astype(MXU)

    blk = pl.BlockSpec((None, tm, bf), lambda j, i: (j, i, 0))
    hid = jax.ShapeDtypeStruct((J, T, bf), MXU)
    return _pallas_call(
        body, name=name, grid=(J, I),
        in_specs=[pl.BlockSpec((tm, D), lambda j, i: (jnp.where(j == 0, i, I - 1), 0)),
                  pl.BlockSpec((1, D), lambda j, i: (0, 0)),
                  pl.BlockSpec((2, None, bf, D), lambda j, i: (0, j, 0, 0))],
        out_specs=[blk, blk, blk],
        out_shape=[hid, hid, hid],
        scratch_shapes=[pltpu.VMEM((T, D), MXU)],
        compiler_params=_params(40, 2),
    )(x, g, w13)


def _ffn_down(x, A, w2, name):
    T, D = x.shape
    J, _, bf = A.shape
    tm = _tile(T, 512)

    def body(x_ref, A_ref, w2_ref, xo_ref):
        f = _mm(A_ref[0], w2_ref[0:bf, :])
        for j in range(1, J):
            f = f + _mm(A_ref[j], w2_ref[j * bf:(j + 1) * bf, :])
        xo_ref[...] = x_ref[...] + 0.5 * f

    row = pl.BlockSpec((tm, D), lambda i: (i, 0))
    return _pallas_call(
        body, name=name, grid=(T // tm,),
        in_specs=[row, pl.BlockSpec((J, tm, bf), lambda i: (0, i, 0)), pl.BlockSpec((J * bf, D), lambda i: (0, 0))],
        out_specs=row,
        out_shape=jax.ShapeDtypeStruct((T, D), F32),
        compiler_params=_params(48, 1),
    )(x, A, w2)


def _ffn_bwd_act(x, g, dy, Gs, Us, w13, w2, name, after=()):
    T, D = x.shape
    _, J, bf, _ = w13.shape
    tm = _tile(T, 512)
    I = T // tm

    def body(x_ref, g_ref, dy_ref, G_ref, U_ref, w13_ref, w2_ref, *rest):
        dx_ref, dg_ref, h_ref, dG_ref, dU_ref, dh_s, dF_s, h_s = rest[len(after):]
        j = pl.program_id(0)
        i = pl.program_id(1)
        rows = pl.ds(pl.multiple_of(i * tm, tm), tm)

        @pl.when(j == 0)
        def _():
            h, _ = _rms_fwd(x_ref[...], g_ref[...])
            hb = h.astype(MXU)
            h_s[rows, :] = hb
            h_ref[...] = hb
            dF_s[rows, :] = (0.5 * dy_ref[...]).astype(MXU)
            dh_s[rows, :] = jnp.zeros((tm, D), F32)

        chunks = [slice(r0, r0 + _FFN_CHUNK) for r0 in range(0, tm, _FFN_CHUNK)]
        crows = [pl.ds(pl.multiple_of(i * tm + rs.start, _FFN_CHUNK), _FFN_CHUNK) for rs in chunks]
        dAs = [_mm_nt(dF_s[cr, :], w2_ref[...]) for cr in crows]
        for rs, cr, dA in zip(chunks, crows, dAs):
            G = G_ref[rs, :].astype(F32)
            U = U_ref[rs, :].astype(F32)
            sg = jax.nn.sigmoid(G)
            s = G * sg
            dUb = (dA * s).astype(MXU)
            dGb = (dA * U * _silu_grad(G, sg)).astype(MXU)
            dG_ref[rs, :] = dGb
            dU_ref[rs, :] = dUb
            dh_s[cr, :] += _mm(dGb, w13_ref[0]) + _mm(dUb, w13_ref[1])

        @pl.when(j == J - 1)
        def _():
            xv = x_ref[...]
            gv = g_ref[...]
            _, r = _rms_fwd(xv, gv)
            dxn, dgp = _rms_bwd(xv, r, gv, dh_s[rows, :])
            dx_ref[...] = dy_ref[...] + dxn

            @pl.when(i == 0)
            def _():
                dg_ref[...] = dgp

            @pl.when(i > 0)
            def _():
                dg_ref[...] += dgp

    ends = lambda j, i: (jnp.where((j == 0) | (j == J - 1), i, I - 1), 0)
    blk = pl.BlockSpec((None, tm, bf), lambda j, i: (j, i, 0))
    hid = jax.ShapeDtypeStruct((J, T, bf), MXU)
    return _pallas_call(
        body, name=name, grid=(J, I),
        in_specs=[pl.BlockSpec((tm, D), ends), pl.BlockSpec((1, D), lambda j, i: (0, 0)), pl.BlockSpec((tm, D), ends),
                  blk, blk, pl.BlockSpec((2, None, bf, D), lambda j, i: (0, j, 0, 0)),
                  pl.BlockSpec((bf, D), lambda j, i: (j, 0))] + [_UNREAD] * len(after),
        out_specs=[pl.BlockSpec((tm, D), lambda j, i: (jnp.where(j == J - 1, i, 0), 0)),
                   pl.BlockSpec((1, D), lambda j, i: (0, 0)),
                   pl.BlockSpec((tm, D), lambda j, i: (jnp.where(j == 0, i, I - 1), 0)), blk, blk],
        out_shape=[jax.ShapeDtypeStruct((T, D), F32), jax.ShapeDtypeStruct((1, D), F32),
                   jax.ShapeDtypeStruct((T, D), MXU), hid, hid],
        scratch_shapes=[pltpu.VMEM((T, D), F32), pltpu.VMEM((T, D), MXU), pltpu.VMEM((T, D), MXU)],
        compiler_params=_params(58, 2),
    )(x, g, dy, Gs, Us, w13, w2, *after)


def _ffn_w13_grad(h, dG, dU, name):
    T, D = h.shape
    J, _, bf = dG.shape

    def body(h_ref, dG_ref, dU_ref, dw13_ref):
        dw13_ref[0] = _mm_tn(dG_ref[...], h_ref[...]).astype(dw13_ref.dtype)
        dw13_ref[1] = _mm_tn(dU_ref[...], h_ref[...]).astype(dw13_ref.dtype)

    blk = pl.BlockSpec((None, T, bf), lambda j: (j, 0, 0))
    return _pallas_call(
        body, name=name, grid=(J,),
        in_specs=[pl.BlockSpec((T, D), lambda j: (0, 0)), blk, blk],
        out_specs=pl.BlockSpec((2, None, bf, D), lambda j: (0, j, 0, 0)),
        out_shape=jax.ShapeDtypeStruct((2, J, bf, D), MXU),
        compiler_params=_params(48, 1),
    )(h, dG, dU)


def _ffn_w2_grad(dy, A, name, after=()):
    T, D = dy.shape
    J, _, bf = A.shape

    def body(dy_ref, A_ref, *rest):
        dw2_ref, dF_s = rest[len(after):]

        @pl.when(pl.program_id(0) == 0)
        def _():
            dF_s[...] = (0.5 * dy_ref[...]).astype(MXU)

        dw2_ref[...] = _mm_tn(A_ref[...], dF_s[...]).astype(dw2_ref.dtype)

    return _pallas_call(
        body, name=name, grid=(J,),
        in_specs=[pl.BlockSpec((T, D), lambda j: (0, 0)), pl.BlockSpec((None, T, bf), lambda j: (j, 0, 0))]
        + [_UNREAD] * len(after),
        out_specs=pl.BlockSpec((bf, D), lambda j: (j, 0)),
        out_shape=jax.ShapeDtypeStruct((J * bf, D), MXU),
        scratch_shapes=[pltpu.VMEM((T, D), MXU)],
        compiler_params=_params(48, 1),
    )(dy, A, *after)


_AG0, _Q0, _K0, _V0, _F0 = 0, 2 * D_CONV, 2 * D_CONV + D_ATTN, 2 * D_CONV + 2 * D_ATTN, 2 * D_CONV + 3 * D_ATTN
N_IN = _F0 + N_HEADS
N_IN_PAD = _F0 + LANES


def _inproj_fwd(x1, gm, winp, name):
    T, D = x1.shape
    tm = _tile(T, 256)

    def body(x_ref, g_ref, w_ref, ag_ref, k_ref, v_ref, qT_ref, kT_ref, vT_ref, fl_ref):
        h, _ = _rms_fwd(x_ref[...], g_ref[...])
        hb = h.astype(MXU)
        ag_ref[...] = _mm_nt(hb, w_ref[_AG0:_Q0, :])
        qT_ref[...] = (_mm_nt(hb, w_ref[_Q0:_K0, :]) * SCALE).T.astype(MXU)
        for c0, ref, refT in ((_K0, k_ref, kT_ref), (_V0, v_ref, vT_ref)):
            y = _mm_nt(hb, w_ref[c0:c0 + D_ATTN, :])
            ref[...] = y.astype(MXU)
            refT[...] = y.T.astype(MXU)
        fl_ref[...] = _mm_nt(hb, w_ref[_F0:N_IN_PAD, :])

    row = lambda w: pl.BlockSpec((tm, w), lambda i: (i, 0))
    col = pl.BlockSpec((D_ATTN, tm), lambda i: (0, i))
    std = jax.ShapeDtypeStruct((T, D_ATTN), MXU)
    trn = jax.ShapeDtypeStruct((D_ATTN, T), MXU)
    return _pallas_call(
        body, name=name, grid=(T // tm,),
        in_specs=[row(D), pl.BlockSpec((1, D), lambda i: (0, 0)), pl.BlockSpec((N_IN_PAD, D), lambda i: (0, 0))],
        out_specs=[row(2 * D_CONV), row(D_ATTN), row(D_ATTN), col, col, col, row(LANES)],
        out_shape=[jax.ShapeDtypeStruct((T, 2 * D_CONV), F32), std, std, trn, trn, trn,
                   jax.ShapeDtypeStruct((T, LANES), F32)],
        compiler_params=_params(40, 1),
    )(x1, gm, winp)


def _inproj_bwd_act(x1, gm, dx2, dag, dqT, dkT, dvT, dfl, winp, name):
    T, D = x1.shape
    tm = _tile(T, 256)

    def body(x_ref, g_ref, dx2_ref, dag_ref, dqT_ref, dkT_ref, dvT_ref, dfl_ref, w_ref, dx1_ref, dg_ref, h_ref):
        i = pl.program_id(0)
        xv = x_ref[...]
        gv = g_ref[...]
        h, r = _rms_fwd(xv, gv)
        h_ref[...] = h.astype(MXU)
        dh = _mm(dag_ref[...], w_ref[_AG0:_Q0, :])
        for c0, ref in ((_Q0, dqT_ref), (_K0, dkT_ref), (_V0, dvT_ref)):
            dh = dh + _mm_tn(ref[...].astype(MXU), w_ref[c0:c0 + D_ATTN, :])
        dh = dh + _mm(dfl_ref[...].astype(MXU), w_ref[_F0:N_IN_PAD, :])
        dxn, dgp = _rms_bwd(xv, r, gv, dh)
        dx1_ref[...] = dx2_ref[...] + dxn

        @pl.when(i == 0)
        def _():
            dg_ref[...] = dgp

        @pl.when(i > 0)
        def _():
            dg_ref[...] += dgp

    row = lambda w: pl.BlockSpec((tm, w), lambda i: (i, 0))
    col = pl.BlockSpec((D_ATTN, tm), lambda i: (0, i))
    full = lambda a, b: pl.BlockSpec((a, b), lambda i: (0, 0))
    return _pallas_call(
        body, name=name, grid=(T // tm,),
        in_specs=[row(D), full(1, D), row(D), row(2 * D_CONV), col, col, col, row(LANES), full(N_IN_PAD, D)],
        out_specs=[row(D), full(1, D), row(D)],
        out_shape=[jax.ShapeDtypeStruct((T, D), F32), jax.ShapeDtypeStruct((1, D), F32),
                   jax.ShapeDtypeStruct((T, D), MXU)],
        compiler_params=_params(40, 1),
    )(x1, gm, dx2, dag, dqT, dkT, dvT, dfl, winp)


def _inproj_bwd_weights(h, dag, dqT, dkT, dvT, dfl, name, after=()):
    T, D = h.shape

    def body(h_ref, dag_ref, dqT_ref, dkT_ref, dvT_ref, dfl_ref, *rest):
        dw_ref = rest[len(after)]
        hb = h_ref[...]
        dw_ref[_AG0:_Q0, :] = _mm_tn(dag_ref[...], hb).astype(dw_ref.dtype)
        for c0, ref in ((_Q0, dqT_ref), (_K0, dkT_ref), (_V0, dvT_ref)):
            dw_ref[c0:c0 + D_ATTN, :] = _mm(ref[...].astype(MXU), hb).astype(dw_ref.dtype)
        dw_ref[_F0:N_IN_PAD, :] = _mm_tn(dfl_ref[...].astype(MXU), hb).astype(dw_ref.dtype)

    vmem = pl.BlockSpec(memory_space=pltpu.VMEM)
    return _pallas_call(
        body, name=name, in_specs=[vmem] * 6 + [_UNREAD] * len(after), out_specs=vmem,
        out_shape=jax.ShapeDtypeStruct((N_IN_PAD, D), MXU),
        compiler_params=pltpu.CompilerParams(vmem_limit_bytes=56 * MIB),
    )(h, dag, dqT, dkT, dvT, dfl, *after)


def _forget_fwd(fl, fbp, name):
    T = fl.shape[0]
    tb = _tile(T, 256)

    def body(fl_ref, fb_ref, cum_ref, cumT_ref):
        ri = lax.broadcasted_iota(jnp.int32, (tb, tb), 0)
        ci = lax.broadcasted_iota(jnp.int32, (tb, tb), 1)
        tri = (ri >= ci).astype(jnp.bfloat16)
        carry = jnp.zeros((1, LANES), F32)
        for b in range(T // tb):
            z = fl_ref[b * tb:(b + 1) * tb, :] + fb_ref[...]
            lf = jnp.minimum(z, 0.0) - jnp.log1p(jnp.exp(-jnp.abs(z)))
            c = _exact_tri_dot(tri, lf) + carry
            cum_ref[b * tb:(b + 1) * tb, :] = c
            carry = c[tb - 1:tb, :]
        cumT_ref[...] = cum_ref[...].T[:N_HEADS, :]

    return _pallas_call(
        body, name=name,
        out_shape=[jax.ShapeDtypeStruct((T, LANES), F32), jax.ShapeDtypeStruct((N_HEADS, T), F32)],
        compiler_params=pltpu.CompilerParams(vmem_limit_bytes=32 * MIB),
    )(fl, fbp)


def _forget_bwd(dcum, fl, fbp, name):
    T = fl.shape[0]
    tb = _tile(T, 256)

    def body(dc_ref, fl_ref, fb_ref, dfl_ref, dfb_ref):
        ri = lax.broadcasted_iota(jnp.int32, (tb, tb), 0)
        ci = lax.broadcasted_iota(jnp.int32, (tb, tb), 1)
        tri = (ri <= ci).astype(jnp.bfloat16)
        carry = jnp.zeros((1, LANES), F32)
        dfb = jnp.zeros((1, LANES), F32)
        for b in reversed(range(T // tb)):
            sl = slice(b * tb, (b + 1) * tb)
            dl = _exact_tri_dot(tri, dc_ref[sl, :]) + carry
            carry = dl[0:1, :]
            z = fl_ref[sl, :] + fb_ref[...]
            dfl = dl * jax.nn.sigmoid(-z)
            dfl_ref[sl, :] = dfl
            dfb = dfb + jnp.sum(dfl, axis=0, keepdims=True)
        dfb_ref[...] = dfb

    return _pallas_call(
        body, name=name,
        out_shape=[jax.ShapeDtypeStruct((T, LANES), F32), jax.ShapeDtypeStruct((1, LANES), F32)],
        compiler_params=pltpu.CompilerParams(vmem_limit_bytes=32 * MIB),
    )(dcum, fl, fbp)


def _causal_keep(i, j, tq, tk):
    key = j * tk + lax.broadcasted_iota(jnp.int32, (tk, tq), 0)
    qry = i * tq + lax.broadcasted_iota(jnp.int32, (tk, tq), 1)
    return key <= qry


def _split_hi_lo(x):
    hi = x.astype(MXU)
    lo = (x - hi.astype(F32)).astype(MXU)
    return hi, lo


def _attn_fwd(qT, k, vT, cum, cumT, name):
    T = k.shape[0]
    tq = _tile(T, 256)
    tk = _tile(tq, 128)
    kpq = tq // tk
    heads = [slice(HEAD_DIM * h, HEAD_DIM * (h + 1)) for h in range(N_HEADS)]

    def body(qT_ref, k_ref, vT_ref, cum_ref, cumT_ref, o_ref, lseT_ref, acc_s, m_s, l_s):
        i = pl.program_id(0)
        acc_s[...] = jnp.zeros_like(acc_s)
        m_s[...] = jnp.full_like(m_s, NEG)
        l_s[...] = jnp.zeros_like(l_s)

        def kblock(j, masked):
            rows = pl.ds(pl.multiple_of(j * tk, tk), tk)
            keep = _causal_keep(i, j, tq, tk) if masked else None
            bias = [cumT_ref[h:h + 1, :] - cum_ref[rows, h:h + 1] for h in range(N_HEADS)]
            qk = [_mm(k_ref[rows, hs], qT_ref[hs, :]) + bias[h] for h, hs in enumerate(heads)]
            for h, hs in enumerate(heads):
                sT = qk[h]
                if masked:
                    sT = jnp.where(keep, sT, NEG)
                m_old = m_s[h:h + 1, :]
                m_new = jnp.maximum(m_old, jnp.max(sT, axis=0, keepdims=True))
                alpha = jnp.exp(m_old - m_new)
                pT = jnp.exp(sT - m_new)
                l_s[h:h + 1, :] = alpha * l_s[h:h + 1, :] + jnp.sum(pT, axis=0, keepdims=True)
                p_hi, p_lo = _split_hi_lo(pT)
                vh = vT_ref[hs, rows]
                acc_s[hs, :] = alpha * acc_s[hs, :] + (_mm(vh, p_hi) + _mm(vh, p_lo))
                m_s[h:h + 1, :] = m_new

        def unmasked(j, c):
            kblock(j, False)
            return c

        lax.fori_loop(0, kpq * i, unmasked, 0)
        for d in range(kpq):
            kblock(kpq * i + d, True)
        for h, hs in enumerate(heads):
            acc_s[hs, :] = acc_s[hs, :] / l_s[h:h + 1, :]
        o_ref[...] = acc_s[...].T
        lseT_ref[...] = m_s[...] + jnp.log(l_s[...])

    full = lambda a, b: pl.BlockSpec((a, b), lambda i: (0, 0))
    colblk = lambda r: pl.BlockSpec((r, tq), lambda i: (0, i))
    return _pallas_call(
        body, name=name, grid=(T // tq,),
        in_specs=[colblk(D_ATTN), full(T, D_ATTN), full(D_ATTN, T), full(T, LANES), colblk(N_HEADS)],
        out_specs=[pl.BlockSpec((tq, D_ATTN), lambda i: (i, 0)), colblk(N_HEADS)],
        out_shape=[jax.ShapeDtypeStruct((T, D_ATTN), F32), jax.ShapeDtypeStruct((N_HEADS, T), F32)],
        scratch_shapes=[pltpu.VMEM((D_ATTN, tq), F32), pltpu.VMEM((N_HEADS, tq), F32),
                        pltpu.VMEM((N_HEADS, tq), F32)],
        compiler_params=_params(40, 1),
    )(qT, k, vT, cum, cumT)


def _attn_bwd(qT, k, kT, v, doT, lseT, deltaT, cum, cumT, name, after=()):
    T = k.shape[0]
    tq = _tile(T, 256)
    tk = _tile(tq, 128)
    kpq = tq // tk
    heads = [slice(HEAD_DIM * h, HEAD_DIM * (h + 1)) for h in range(N_HEADS)]

    def body(qT_ref, k_ref, kT_ref, v_ref, doT_ref, lseT_ref, dlT_ref, cum_ref, cumT_ref, *rest):
        dq_ref, dk_ref, dv_ref, dcum_ref, dq_s = rest[len(after):]
        i = pl.program_id(0)

        @pl.when(i == 0)
        def _():
            dk_ref[...] = jnp.zeros_like(dk_ref)
            dv_ref[...] = jnp.zeros_like(dv_ref)
            dcum_ref[...] = jnp.zeros_like(dcum_ref)

        dq_s[...] = jnp.zeros_like(dq_s)

        def kblock(j, masked):
            rows = pl.ds(pl.multiple_of(j * tk, tk), tk)
            keep = _causal_keep(i, j, tq, tk) if masked else None
            bias = [cumT_ref[h:h + 1, :] - cum_ref[rows, h:h + 1] for h in range(N_HEADS)]
            qk = [_mm(k_ref[rows, hs], qT_ref[hs, :]) + bias[h] for h, hs in enumerate(heads)]
            dps = [_mm(v_ref[rows, hs], doT_ref[hs, :]) for hs in heads]
            for h, hs in enumerate(heads):
                sT = qk[h]
                if masked:
                    sT = jnp.where(keep, sT, NEG)
                pT = jnp.exp(sT - lseT_ref[h:h + 1, :])
                dsT = pT * (dps[h] - dlT_ref[h:h + 1, :])
                dcum_ref[rows, h:h + 1] += -jnp.sum(dsT, axis=1, keepdims=True)
                dsb = dsT.astype(MXU)
                dv_ref[hs, rows] += _mm_nt(doT_ref[hs, :], pT.astype(MXU))
                dk_ref[hs, rows] += _mm_nt(qT_ref[hs, :], dsb)
                dq_s[hs, :] += _mm(kT_ref[hs, rows], dsb)

        def unmasked(j, c):
            kblock(j, False)
            return c

        lax.fori_loop(0, kpq * i, unmasked, 0)
        for d in range(kpq):
            kblock(kpq * i + d, True)
        dq_ref[...] = (dq_s[...] * SCALE).astype(dq_ref.dtype)

    full = lambda a, b: pl.BlockSpec((a, b), lambda i: (0, 0))
    colblk = lambda r: pl.BlockSpec((r, tq), lambda i: (0, i))
    return _pallas_call(
        body, name=name, grid=(T // tq,),
        in_specs=[colblk(D_ATTN), full(T, D_ATTN), full(D_ATTN, T), full(T, D_ATTN), colblk(D_ATTN),
                  colblk(N_HEADS), colblk(N_HEADS), full(T, LANES), colblk(N_HEADS)] + [_UNREAD] * len(after),
        out_specs=[colblk(D_ATTN), full(D_ATTN, T), full(D_ATTN, T), full(T, LANES)],
        out_shape=[
            jax.ShapeDtypeStruct((D_ATTN, T), MXU),
            jax.ShapeDtypeStruct((D_ATTN, T), F32),
            jax.ShapeDtypeStruct((D_ATTN, T), F32),
            jax.ShapeDtypeStruct((T, LANES), F32),
        ],
        scratch_shapes=[pltpu.VMEM((D_ATTN, tq), F32)],
        compiler_params=_params(48, 1),
    )(qT, k, kT, v, doT, lseT, deltaT, cum, cumT, *after)


_ROWS_PER_CHUNK = 64


def _glu_halo(ag_ref, agh_ref, uext_s, tm, first):
    a = ag_ref[:, :D_CONV]
    sg = jax.nn.sigmoid(ag_ref[:, D_CONV:])
    uh = agh_ref[:, :D_CONV] * jax.nn.sigmoid(agh_ref[:, D_CONV:])
    uext_s[0:CONV_HALO, :] = jnp.where(first, 0.0, uh)
    uext_s[CONV_HALO:CONV_HALO + tm, :] = a * sg
    return a, sg


_SUBLANES = 8


def _shifted_copies(ext_s, sh_s, rows):
    for k in range(1, _SUBLANES):
        sh_s[k, 0:rows, :] = ext_s[pl.ds(k, rows), :]


def _window(ext_s, sh_s, start, rows):
    k = start % _SUBLANES
    if k == 0:
        return ext_s[pl.ds(start, rows), :]
    return sh_s[k, pl.ds(start - k, rows), :]


def _layer_norm_stats(y):
    mu = jnp.mean(y, axis=-1, keepdims=True)
    xc = y - mu
    rs = lax.rsqrt(jnp.mean(xc * xc, axis=-1, keepdims=True) + EPS)
    return xc * rs, rs


def _conv_fwd(ag, w32, cb, lg, lb, name):
    T = ag.shape[0]
    tm = _tile(T, 256)
    rc = _tile(tm, _ROWS_PER_CHUNK)
    hb = tm // CONV_HALO

    def body(ag_ref, agh_ref, w_ref, cb_ref, lg_ref, lb_ref, yc_ref, c_ref, uext_s, ush_s):
        i = pl.program_id(0)
        _glu_halo(ag_ref, agh_ref, uext_s, tm, i == 0)
        _shifted_copies(uext_s, ush_s, tm + CONV_HALO - _SUBLANES)
        for r0 in range(0, tm, rc):
            acc = jnp.zeros((rc, D_CONV), F32)
            for t in range(CONV_TAPS):
                acc = acc + _window(uext_s, ush_s, r0 + CONV_HALO - (CONV_TAPS - 1) + t, rc) * w_ref[t:t + 1, :]
            y = acc + cb_ref[...]
            yc_ref[r0:r0 + rc, :] = y
            n, _ = _layer_norm_stats(y)
            z = n * lg_ref[...] + lb_ref[...]
            c_ref[r0:r0 + rc, :] = z * jax.nn.sigmoid(z)

    row = lambda w: pl.BlockSpec((tm, w), lambda i: (i, 0))
    full = lambda a, b: pl.BlockSpec((a, b), lambda i: (0, 0))
    return _pallas_call(
        body, name=name, grid=(T // tm,),
        in_specs=[row(2 * D_CONV),
                  pl.BlockSpec((CONV_HALO, 2 * D_CONV), lambda i: (jnp.maximum(i * hb - 1, 0), 0)),
                  full(CONV_HALO, D_CONV), full(1, D_CONV), full(1, D_CONV), full(1, D_CONV)],
        out_specs=[row(D_CONV), row(D_CONV)],
        out_shape=[jax.ShapeDtypeStruct((T, D_CONV), F32), jax.ShapeDtypeStruct((T, D_CONV), F32)],
        scratch_shapes=[pltpu.VMEM((CONV_HALO + tm, D_CONV), F32),
                        pltpu.VMEM((_SUBLANES, CONV_HALO + tm, D_CONV), F32)],
        compiler_params=_params(32, 1),
    )(ag, ag, w32, cb, lg, lb)


def _conv_bwd(dc, yc, ag, w32, lg, lb, name):
    T = ag.shape[0]
    tm = _tile(T, 256)
    rc = _tile(tm, _ROWS_PER_CHUNK)
    I = T // tm
    hb = tm // CONV_HALO
    n_halo_blocks = T // CONV_HALO

    def body(dc_ref, yc_ref, dch_ref, ych_ref, ag_ref, agh_ref, w_ref, lg_ref, lb_ref,
             dag_ref, dw_ref, dcb_ref, dlg_ref, dlb_ref, uext_s, dext_s, ush_s, dsh_s):
        i = pl.program_id(0)
        lgv = lg_ref[...]
        lbv = lb_ref[...]

        def ln_bwd(dcv, ycv):
            n, rs = _layer_norm_stats(ycv)
            z = n * lgv + lbv
            dz = dcv * _silu_grad(z, jax.nn.sigmoid(z))
            dn = dz * lgv
            dy = rs * (dn - jnp.mean(dn, axis=-1, keepdims=True) - n * jnp.mean(dn * n, axis=-1, keepdims=True))
            return dy, dz, n

        dy, dz, n = ln_bwd(dc_ref[...], yc_ref[...])
        dyh, _, _ = ln_bwd(dch_ref[...], ych_ref[...])
        dext_s[0:tm, :] = dy
        dext_s[tm:tm + CONV_HALO, :] = jnp.where(i == I - 1, 0.0, dyh)
        a, sg = _glu_halo(ag_ref, agh_ref, uext_s, tm, i == 0)
        _shifted_copies(uext_s, ush_s, tm + CONV_HALO - _SUBLANES)
        _shifted_copies(dext_s, dsh_s, tm + CONV_HALO - _SUBLANES)

        @pl.when(i == 0)
        def _():
            dw_ref[...] = jnp.zeros_like(dw_ref)
            dcb_ref[...] = jnp.zeros_like(dcb_ref)
            dlg_ref[...] = jnp.zeros_like(dlg_ref)
            dlb_ref[...] = jnp.zeros_like(dlb_ref)

        dcb_ref[...] += jnp.sum(dy, axis=0, keepdims=True)
        dlg_ref[...] += jnp.sum(dz * n, axis=0, keepdims=True)
        dlb_ref[...] += jnp.sum(dz, axis=0, keepdims=True)
        for t in range(CONV_TAPS):
            u_t = _window(uext_s, ush_s, CONV_HALO - (CONV_TAPS - 1) + t, tm)
            dw_ref[t:t + 1, :] += jnp.sum(dy * u_t, axis=0, keepdims=True)
        for r0 in range(0, tm, rc):
            acc = jnp.zeros((rc, D_CONV), F32)
            for t in range(CONV_TAPS):
                acc = acc + _window(dext_s, dsh_s, r0 + (CONV_TAPS - 1) - t, rc) * w_ref[t:t + 1, :]
            a_c = a[r0:r0 + rc, :]
            sg_c = sg[r0:r0 + rc, :]
            dag_ref[r0:r0 + rc, :D_CONV] = (acc * sg_c).astype(dag_ref.dtype)
            dag_ref[r0:r0 + rc, D_CONV:] = (acc * a_c * sg_c * (1.0 - sg_c)).astype(dag_ref.dtype)

    row = lambda w: pl.BlockSpec((tm, w), lambda i: (i, 0))
    full = lambda a, b: pl.BlockSpec((a, b), lambda i: (0, 0))
    nxt = pl.BlockSpec((CONV_HALO, D_CONV), lambda i: (jnp.minimum((i + 1) * hb, n_halo_blocks - 1), 0))
    return _pallas_call(
        body, name=name, grid=(I,),
        in_specs=[row(D_CONV), row(D_CONV), nxt, nxt, row(2 * D_CONV),
                  pl.BlockSpec((CONV_HALO, 2 * D_CONV), lambda i: (jnp.maximum(i * hb - 1, 0), 0)),
                  full(CONV_HALO, D_CONV), full(1, D_CONV), full(1, D_CONV)],
        out_specs=[row(2 * D_CONV), full(CONV_HALO, D_CONV), full(1, D_CONV), full(1, D_CONV), full(1, D_CONV)],
        out_shape=[
            jax.ShapeDtypeStruct((T, 2 * D_CONV), MXU),
            jax.ShapeDtypeStruct((CONV_HALO, D_CONV), F32),
            jax.ShapeDtypeStruct((1, D_CONV), F32),
            jax.ShapeDtypeStruct((1, D_CONV), F32),
            jax.ShapeDtypeStruct((1, D_CONV), F32),
        ],
        scratch_shapes=[pltpu.VMEM((CONV_HALO + tm, D_CONV), F32), pltpu.VMEM((tm + CONV_HALO, D_CONV), F32),
                        pltpu.VMEM((_SUBLANES, CONV_HALO + tm, D_CONV), F32),
                        pltpu.VMEM((_SUBLANES, CONV_HALO + tm, D_CONV), F32)],
        compiler_params=_params(40, 1),
    )(dc, yc, dc, yc, ag, ag, w32, lg, lb)


def _outproj_fwd(x1, c, o, gc, ga, wout, name):
    T, D = x1.shape
    tm = _tile(T, 512)

    def body(x_ref, c_ref, o_ref, gc_ref, ga_ref, w_ref, x2_ref):
        yc, _ = _rms_fwd(c_ref[...], gc_ref[...])
        ya, _ = _rms_fwd(o_ref[...], ga_ref[...])
        x2_ref[...] = (x_ref[...] + _mm(yc.astype(MXU), w_ref[:D_CONV, :])
                       + _mm(ya.astype(MXU), w_ref[D_CONV:, :]))

    row = lambda w: pl.BlockSpec((tm, w), lambda i: (i, 0))
    full = lambda a, b: pl.BlockSpec((a, b), lambda i: (0, 0))
    return _pallas_call(
        body, name=name, grid=(T // tm,),
        in_specs=[row(D), row(D_CONV), row(D_ATTN), full(1, D_CONV), full(1, D_ATTN), full(D_CONV + D_ATTN, D)],
        out_specs=row(D),
        out_shape=jax.ShapeDtypeStruct((T, D), F32),
        compiler_params=_params(32, 1),
    )(x1, c, o, gc, ga, wout)


def _outproj_bwd(dx2, c, o, gc, ga, wout, name):
    T, D = dx2.shape
    tm = _tile(T, 256)
    I = T // tm

    def body(dx_ref, c_ref, o_ref, gc_ref, ga_ref, w_ref,
             dc_ref, doT_ref, dlT_ref, dw_ref, dgc_ref, dga_ref, acc_s):
        i = pl.program_id(0)
        dxb = dx_ref[...].astype(MXU)
        cv = c_ref[...]
        ov = o_ref[...]
        yc, rcn = _rms_fwd(cv, gc_ref[...])
        ya, ra = _rms_fwd(ov, ga_ref[...])
        dyc = _mm_nt(dxb, w_ref[:D_CONV, :])
        dya = _mm_nt(dxb, w_ref[D_CONV:, :])
        dwc = _mm_tn(yc.astype(MXU), dxb)
        dwa = _mm_tn(ya.astype(MXU), dxb)
        dcv, dgc = _rms_bwd(cv, rcn, gc_ref[...], dyc)
        dov, dga = _rms_bwd(ov, ra, ga_ref[...], dya)
        dc_ref[...] = dcv
        dob = dov.astype(doT_ref.dtype)
        doT_ref[...] = dov.T.astype(doT_ref.dtype)
        chan = lax.broadcasted_iota(jnp.int32, (D_ATTN, LANES), 0)
        head = lax.broadcasted_iota(jnp.int32, (D_ATTN, LANES), 1)
        in_head = ((chan >= head * HEAD_DIM) & (chan < (head + 1) * HEAD_DIM)).astype(jnp.bfloat16)
        dlT_ref[...] = _exact_dot_01(dob.astype(F32) * ov, in_head).T[:N_HEADS, :]

        @pl.when(i == 0)
        def _():
            acc_s[:D_CONV, :] = dwc
            acc_s[D_CONV:, :] = dwa
            dgc_ref[...] = dgc
            dga_ref[...] = dga

        @pl.when(i > 0)
        def _():
            acc_s[:D_CONV, :] += dwc
            acc_s[D_CONV:, :] += dwa
            dgc_ref[...] += dgc
            dga_ref[...] += dga

        @pl.when(i == I - 1)
        def _():
            dw_ref[...] = acc_s[...].astype(dw_ref.dtype)

    row = lambda w: pl.BlockSpec((tm, w), lambda i: (i, 0))
    full = lambda a, b: pl.BlockSpec((a, b), lambda i: (0, 0))
    return _pallas_call(
        body, name=name, grid=(I,),
        in_specs=[row(D), row(D_CONV), row(D_ATTN), full(1, D_CONV), full(1, D_ATTN), full(D_CONV + D_ATTN, D)],
        out_specs=[row(D_CONV), pl.BlockSpec((D_ATTN, tm), lambda i: (0, i)),
                   pl.BlockSpec((N_HEADS, tm), lambda i: (0, i)),
                   full(D_CONV + D_ATTN, D), full(1, D_CONV), full(1, D_ATTN)],
        out_shape=[
            jax.ShapeDtypeStruct((T, D_CONV), F32),
            jax.ShapeDtypeStruct((D_ATTN, T), MXU),
            jax.ShapeDtypeStruct((N_HEADS, T), F32),
            jax.ShapeDtypeStruct((D_CONV + D_ATTN, D), MXU),
            jax.ShapeDtypeStruct((1, D_CONV), F32),
            jax.ShapeDtypeStruct((1, D_ATTN), F32),
        ],
        scratch_shapes=[pltpu.VMEM((D_CONV + D_ATTN, D), F32)],
        compiler_params=_params(40, 1),
    )(dx2, c, o, gc, ga, wout)


def _loss_head(x3, gf, target, name):
    T, D = x3.shape
    tm = _tile(T, 512)

    def body(x_ref, g_ref, t_ref, loss_ref, dx_ref, dg_ref):
        i = pl.program_id(0)
        xv = x_ref[...]
        gv = g_ref[...]
        out, r = _rms_fwd(xv, gv)
        err = out - t_ref[...]
        part = jnp.full((1, LANES), 0.5 / D, F32) * jnp.sum(err * err)
        dxn, dgp = _rms_bwd(xv, r, gv, err * (1.0 / D))
        dx_ref[...] = dxn

        @pl.when(i == 0)
        def _():
            loss_ref[...] = part
            dg_ref[...] = dgp

        @pl.when(i > 0)
        def _():
            loss_ref[...] += part
            dg_ref[...] += dgp

    row = lambda w: pl.BlockSpec((tm, w), lambda i: (i, 0))
    full = lambda a, b: pl.BlockSpec((a, b), lambda i: (0, 0))
    return _pallas_call(
        body, name=name, grid=(T // tm,),
        in_specs=[row(D), full(1, D), row(D)],
        out_specs=[full(1, LANES), row(D), full(1, D)],
        out_shape=[jax.ShapeDtypeStruct((1, LANES), F32), jax.ShapeDtypeStruct((T, D), F32),
                   jax.ShapeDtypeStruct((1, D), F32)],
        compiler_params=_params(32, 1),
    )(x3, gf, target)


def _row_tile(rows):
    for cand in (256, 176, 128, 64, 32, 16):
        if rows % cand == 0:
            return cand
    return rows


def _adamw(w, m, v, parts, name):
    R, C = w.shape
    P = parts.shape[0]
    tr = _row_tile(R)
    c1 = 1.0 - ADAM_B1 ** ADAM_STEP
    c2 = 1.0 - ADAM_B2 ** ADAM_STEP

    def body(w_ref, m_ref, v_ref, p_ref, g_ref, d_ref, nm_ref, nv_ref):
        g = p_ref[0].astype(F32)
        for s in range(1, P):
            g = g + p_ref[s].astype(F32)
        wv = w_ref[...]
        mn = ADAM_B1 * m_ref[...] + (1.0 - ADAM_B1) * g
        vn = ADAM_B2 * v_ref[...] + (1.0 - ADAM_B2) * (g * g)
        g_ref[...] = g
        nm_ref[...] = mn
        nv_ref[...] = vn
        d_ref[...] = -ADAM_LR * ((mn / c1) / (jnp.sqrt(vn / c2) + ADAM_EPS) + ADAM_WD * wv)

    blk = pl.BlockSpec((tr, C), lambda i: (i, 0))
    out = jax.ShapeDtypeStruct((R, C), F32)
    return _pallas_call(
        body, name=name, grid=(R // tr,),
        in_specs=[blk, blk, blk, pl.BlockSpec((P, tr, C), lambda i: (0, i, 0))],
        out_specs=[blk, blk, blk, blk],
        out_shape=[out, out, out, out],
        compiler_params=_params(32, 1),
    )(w, m, v, parts)


def _position():
    return lax.axis_index("x"), lax.axis_index("y"), lax.axis_index("c")


def _flat(px, py, pc):
    return 4 * px + 2 * py + pc


def _gather_body(ins, outs, send_sems, recv_sems, local_sems, handshake):
    n = len(ins)
    x, y, c = _position()
    me, sibling = (x, y, c), (x, y, 1 - c)
    chips = [(1 - x, y), (x, 1 - y), (1 - x, 1 - y)]
    if handshake:
        _handshake([sibling] + [(*chip, cc) for chip in chips for cc in (c, 1 - c)])

    def copy(a, k, block, to, src=None):
        dst = outs[a].at[_flat(*block)]
        return pltpu.make_async_remote_copy(
            src_ref=dst if src is None else src, dst_ref=dst,
            send_sem=send_sems.at[a, k], recv_sem=recv_sems.at[a, k],
            device_id=to, device_id_type=MESH)

    mine = [pltpu.make_async_copy(ins[a], outs[a].at[_flat(*me)], local_sems.at[a]) for a in range(n)]
    for cp in mine:
        cp.start()
    first = []
    for a in range(n):
        first.append(copy(a, 0, me, sibling, src=ins[a]))
        first += [copy(a, 1 + j, me, (*chip, c), src=ins[a]) for j, chip in enumerate(chips)]
    for cp in first:
        cp.start()
    passed = []
    for a in range(n):
        for j, chip in enumerate(chips):
            copy(a, 1 + j, (*chip, c), me).wait_recv()
            fwd = copy(a, 4 + j, (*chip, c), sibling)
            fwd.start()
            passed.append(fwd)
    for a in range(n):
        copy(a, 0, sibling, me).wait_recv()
        for j, chip in enumerate(chips):
            copy(a, 4 + j, (*chip, 1 - c), me).wait_recv()
    for cp in first + passed:
        cp.wait_send()
    for cp in mine:
        cp.wait()


def _gather_scratch(n):
    return [pltpu.SemaphoreType.DMA((n, 7)), pltpu.SemaphoreType.DMA((n, 7)), pltpu.SemaphoreType.DMA((n,))]


def _all_gather(shards, name):
    n = len(shards)

    def body(*refs):
        _gather_body(refs[:n], refs[n:2 * n], *refs[2 * n:], handshake=False)

    hbm = pl.BlockSpec(memory_space=pltpu.HBM)
    return _pallas_call(
        body, name=name,
        in_specs=[hbm] * n, out_specs=[hbm] * n,
        out_shape=[jax.ShapeDtypeStruct((N_DEV,) + s.shape, s.dtype) for s in shards],
        scratch_shapes=_gather_scratch(n),
    )(*shards)


def _handshake(peers):
    barrier = pltpu.get_barrier_semaphore()
    for peer in peers:
        pl.semaphore_signal(barrier, inc=1, device_id=peer, device_id_type=MESH)
    pl.semaphore_wait(barrier, len(peers))


def _sequencer_call(body, name, collective_id, out_type, scratch_types, operands):
    return pl.kernel(
        body, name=name, out_type=out_type,
        mesh=plsc.ScalarSubcoreMesh(axis_name="sequencer", num_cores=1),
        scratch_types=scratch_types,
        compiler_params=pltpu.CompilerParams(collective_id=collective_id),
    )(*operands)


def _seq_all_gather(shards, name, collective_id, after):
    n = len(shards)

    def body(*refs):
        _gather_body(refs[:n], refs[n + 1:2 * n + 1], *refs[2 * n + 1:], handshake=True)

    return _sequencer_call(
        body, name, collective_id,
        [jax.ShapeDtypeStruct((N_DEV,) + s.shape, s.dtype) for s in shards],
        _gather_scratch(n), list(shards) + [after])


def _seq_to_sibling(parts, name, collective_id, after):
    n = len(parts)

    def body(*refs):
        ins, outs = refs[:n], refs[n + len(after):2 * n + len(after)]
        send_sems, recv_sems = refs[2 * n + len(after):]
        x, y, c = _position()
        sibling = (x, y, 1 - c)
        _handshake([sibling])
        sent = []
        for a in range(n):
            for q in range(N_CHIPS):
                cp = pltpu.make_async_remote_copy(
                    src_ref=ins[a].at[2 * q + (1 - c)], dst_ref=outs[a].at[q],
                    send_sem=send_sems.at[a, q], recv_sem=recv_sems.at[a, q],
                    device_id=sibling, device_id_type=MESH)
                cp.start()
                sent.append(cp)
        for cp in sent:
            cp.wait_recv()
        for cp in sent:
            cp.wait_send()

    return _sequencer_call(
        body, name, collective_id,
        [jax.ShapeDtypeStruct((N_CHIPS,) + p.shape[1:], p.dtype) for p in parts],
        [pltpu.SemaphoreType.DMA((n, N_CHIPS)), pltpu.SemaphoreType.DMA((n, N_CHIPS))],
        list(parts) + list(after))


def _seq_to_chips(partials, name, collective_id, after=()):
    n = len(partials)

    def body(*refs):
        ins, outs = refs[:n], refs[n + len(after):2 * n + len(after)]
        send_sems, recv_sems, local_sems = refs[2 * n + len(after):]
        x, y, c = _position()
        my_chip = 2 * x + y
        chips = [(1 - x, y), (x, 1 - y), (1 - x, 1 - y)]
        _handshake([(*chip, c) for chip in chips])
        mine = [pltpu.make_async_copy(ins[a].at[my_chip], outs[a].at[my_chip], local_sems.at[a]) for a in range(n)]
        for cp in mine:
            cp.start()
        sent = []
        for a in range(n):
            for j, (px, py) in enumerate(chips):
                cp = pltpu.make_async_remote_copy(
                    src_ref=ins[a].at[2 * px + py], dst_ref=outs[a].at[my_chip],
                    send_sem=send_sems.at[a, j], recv_sem=recv_sems.at[a, j],
                    device_id=(px, py, c), device_id_type=MESH)
                cp.start()
                sent.append(cp)
        for cp in sent:
            cp.wait_recv()
        for cp in sent:
            cp.wait_send()
        for cp in mine:
            cp.wait()

    return _sequencer_call(
        body, name, collective_id,
        [jax.ShapeDtypeStruct(p.shape, p.dtype) for p in partials],
        [pltpu.SemaphoreType.DMA((n, 3)), pltpu.SemaphoreType.DMA((n, 3)), pltpu.SemaphoreType.DMA((n,))],
        list(partials) + list(after))


def _pair_add(part, recv, name, after=()):
    _, R, C = part.shape
    core = lax.axis_index("c").astype(jnp.int32).reshape(1)

    def body(c_ref, p_ref, r_ref, *rest):
        o_ref = rest[len(after)]
        o_ref[...] = (p_ref[...].astype(F32) + r_ref[...].astype(F32)).astype(o_ref.dtype)

    blk = pl.BlockSpec((None, R, C), lambda q, c: (q, 0, 0))
    return pl.pallas_call(
        body, name=name,
        grid_spec=pltpu.PrefetchScalarGridSpec(
            num_scalar_prefetch=1, grid=(N_CHIPS,),
            in_specs=[pl.BlockSpec((None, R, C), lambda q, c: (2 * q + c[0], 0, 0)), blk] + [_UNREAD] * len(after),
            out_specs=blk),
        out_shape=pltpu.HBM((N_CHIPS, R, C), part.dtype),
        compiler_params=_params(32, 1),
    )(core, *[pltpu.with_memory_space_constraint(a, pltpu.HBM) for a in (part, recv, *after)])


class _Reduced(NamedTuple):
    partials: list
    reduced: list


def _blocks(g):
    return g.reshape(N_DEV, -1, g.shape[-1])


def _reduce_scatter(parts, tag, ids, after=(), between=None, add_after=()):
    from_sibling = _seq_to_sibling(parts, "rs_sibling_" + tag, ids[0], after)
    mid = between(from_sibling[0]) if between else ()
    partials = [_pair_add(p, r, "rs_add_%s_%d" % (tag, a), add_after)
                for a, (p, r) in enumerate(zip(parts, from_sibling))]
    return _Reduced(partials, _seq_to_chips(partials, "rs_chips_" + tag, ids[1], mid))


_SMALL = ("ffn1_norm", "mix_norm", "conv_b", "conv_ln_g", "conv_ln_b", "forget_b", "out_norm_conv",
          "out_norm_attn", "ffn2_norm", "final_norm")
_PACK_WIDTH = 2 * D_CONV
_SLOT = dict(ffn1_norm=(0, 0), mix_norm=(1, 0), ffn2_norm=(2, 0), final_norm=(3, 0), conv_b=(4, 0),
             conv_ln_g=(4, D_CONV), conv_ln_b=(5, 0), out_norm_conv=(5, D_CONV), out_norm_attn=(6, 0),
             forget_b=(6, D_CONV))
_CONV_ROW0 = 8
_PACK_ROWS = _CONV_ROW0 + CONV_HALO


def _pack_small(small, name):
    arrays = [small[n] for n in _SMALL] + [small["conv_w"]]

    def body(*refs):
        out = refs[-1]
        out[...] = jnp.zeros_like(out)
        for n, ref in zip(_SMALL, refs):
            row, lane = _SLOT[n]
            out[row:row + 1, lane:lane + ref.shape[1]] = ref[...]
        out[_CONV_ROW0:, :D_CONV] = refs[len(_SMALL)][...]

    return _pallas_call(body, name=name, out_shape=jax.ShapeDtypeStruct((_PACK_ROWS, _PACK_WIDTH), F32))(*arrays)


def _adamw_small(gathered, w, m, v, name):
    c1 = 1.0 - ADAM_B1 ** ADAM_STEP
    c2 = 1.0 - ADAM_B2 ** ADAM_STEP
    k = len(_SMALL)

    def body(g_ref, *refs):
        ws, ms, vs = refs[:k], refs[k:2 * k], refs[2 * k:3 * k]
        outs = refs[3 * k:]
        total = g_ref[0]
        for s in range(1, N_DEV):
            total = total + g_ref[s]
        for i, n in enumerate(_SMALL):
            row, lane = _SLOT[n]
            width = ws[i].shape[1]
            g = total[row:row + 1, lane:lane + width]
            mn = ADAM_B1 * ms[i][...] + (1.0 - ADAM_B1) * g
            vn = ADAM_B2 * vs[i][...] + (1.0 - ADAM_B2) * (g * g)
            o_g, o_d, o_m, o_v = outs[4 * i:4 * i + 4]
            o_g[...] = g
            o_m[...] = mn
            o_v[...] = vn
            o_d[...] = -ADAM_LR * ((mn / c1) / (jnp.sqrt(vn / c2) + ADAM_EPS) + ADAM_WD * ws[i][...])
        outs[4 * k][...] = total[_CONV_ROW0:, :D_CONV]

    shapes = []
    for n in _SMALL:
        shapes += [jax.ShapeDtypeStruct(w[n].shape, F32)] * 4
    shapes.append(jax.ShapeDtypeStruct((CONV_HALO, D_CONV), F32))
    res = _pallas_call(body, name=name, out_shape=shapes)(
        gathered, *[w[n] for n in _SMALL], *[m[n] for n in _SMALL], *[v[n] for n in _SMALL])
    return {n: res[4 * i:4 * i + 4] for i, n in enumerate(_SMALL)}, res[4 * k]


def _local_step(x, target, norms, shard):
    D = x.shape[1]
    J = N_DEV // 2
    as13 = lambda g: g.reshape(2, J, g.shape[1], D)

    (g13_1,) = _all_gather([shard["ffn1_w13"]], "gather_ffn1_w13")
    (g2_1,) = _seq_all_gather([shard["ffn1_w2"]], "gather_ffn1_w2", 10, after=g13_1)
    w13_1 = as13(g13_1)
    G1, U1, A1 = _ffn_up(x, norms["ffn1_norm"], w13_1, "ffn1_up")
    gin, gconv = _seq_all_gather([shard["w_in"], shard["conv_w"]], "gather_mix", 1, after=G1)
    w2_1 = g2_1.reshape(-1, D)
    x1 = _ffn_down(x, A1, w2_1, "ffn1_down")
    gout, g13_2, g2_2 = _seq_all_gather([shard["w_out"], shard["ffn2_w13"], shard["ffn2_w2"]], "gather_ffn2", 2,
                                        after=x1)
    winp = jnp.pad(gin.reshape(N_IN, D), ((0, N_IN_PAD - N_IN), (0, 0)))
    wout = gout.reshape(-1, D)
    conv_w32 = jnp.pad(gconv.transpose(1, 0, 2).reshape(CONV_TAPS, D_CONV), ((0, CONV_HALO - CONV_TAPS), (0, 0)))

    ag, k, v, qT, kT, vT, fl = _inproj_fwd(x1, norms["mix_norm"], winp, "inproj_fwd")
    cum, cumT = _forget_fwd(fl, norms["forget_b"], "forget_fwd")
    yc, c = _conv_fwd(ag, conv_w32, norms["conv_b"], norms["conv_ln_g"], norms["conv_ln_b"], "conv_fwd")
    o, lseT = _attn_fwd(qT, k, vT, cum, cumT, "attn_fwd")
    x2 = _outproj_fwd(x1, c, o, norms["out_norm_conv"], norms["out_norm_attn"], wout, "outproj_fwd")
    w13_2, w2_2 = as13(g13_2), g2_2.reshape(-1, D)
    G2, U2, A2 = _ffn_up(x2, norms["ffn2_norm"], w13_2, "ffn2_up")
    x3 = _ffn_down(x2, A2, w2_2, "ffn2_down")
    loss, dx3, d_final = _loss_head(x3, norms["final_norm"], target, "loss_head")

    dw2_2 = _ffn_w2_grad(dx3, A2, "ffn2_w2_grad")
    dx2, d_ffn2n, h3, dG2, dU2 = _ffn_bwd_act(x2, norms["ffn2_norm"], dx3, G2, U2, w13_2, w2_2, "ffn2_bwd_act")
    dw13_2 = _ffn_w13_grad(h3, dG2, dU2, "ffn2_w13_grad")
    dc, dobT, deltaT, dwout, d_onc, d_ona = _outproj_bwd(
        dx2, c, o, norms["out_norm_conv"], norms["out_norm_attn"], wout, "outproj_bwd")
    red_ffn2 = _reduce_scatter([_blocks(dw13_2), _blocks(dw2_2)], "ffn2", (3, 4), add_after=(dc,))
    dqT, dkT, dvT, dcum = _attn_bwd(qT, k, kT, v, dobT, lseT, deltaT, cum, cumT, "attn_bwd",
                                    after=red_ffn2.partials)
    dfl, d_fb = _forget_bwd(dcum, fl, norms["forget_b"], "forget_bwd")
    dag, d_convw, d_cb, d_lg, d_lb = _conv_bwd(dc, yc, ag, conv_w32, norms["conv_ln_g"], norms["conv_ln_b"], "conv_bwd")
    dx1, d_mixn, h2 = _inproj_bwd_act(x1, norms["mix_norm"], dx2, dag, dqT, dkT, dvT, dfl, winp, "inproj_bwd_act")
    dw2_1 = _ffn_w2_grad(dx1, A1, "ffn1_w2_grad")
    dwinp = _inproj_bwd_weights(h2, dag, dqT, dkT, dvT, dfl, "inproj_bwd_weights")
    dwin_blocks = dwinp[:N_IN].reshape(N_DEV, N_IN // N_DEV, -1)
    red_mix = _reduce_scatter([dwin_blocks, _blocks(dwout), _blocks(dw2_1)], "mix", (5, 6),
                              after=red_ffn2.reduced[:1])
    dx, d_ffn1n, h1, dG1, dU1 = _ffn_bwd_act(x, norms["ffn1_norm"], dx1, G1, U1, w13_1, w2_1, "ffn1_bwd_act",
                                             after=red_mix.partials)
    dw13_1 = _ffn_w13_grad(h1, dG1, dU1, "ffn1_w13_grad")

    small = dict(ffn1_norm=d_ffn1n, mix_norm=d_mixn, conv_b=d_cb, conv_ln_g=d_lg, conv_ln_b=d_lb,
                 forget_b=d_fb, out_norm_conv=d_onc, out_norm_attn=d_ona, ffn2_norm=d_ffn2n,
                 final_norm=d_final, conv_w=d_convw)
    packed_small = _pack_small(small, "pack_small_grads")
    gathered_small = []

    def gather_small(behind):
        gathered_small.extend(_seq_all_gather([packed_small], "gather_small_grads", 9, after=behind))
        return gathered_small

    red_w13_1 = _reduce_scatter([_blocks(dw13_1)], "ffn1_w13", (7, 8), after=red_mix.reduced[:1],
                                between=gather_small)
    big = dict(ffn1_w13=red_w13_1.reduced[0], ffn1_w2=red_mix.reduced[2], w_in=red_mix.reduced[0],
               w_out=red_mix.reduced[1], ffn2_w13=red_ffn2.reduced[0], ffn2_w2=red_ffn2.reduced[1])
    return loss[0, 0], dx, gathered_small[0], big


_BIG = ("ffn1_w13", "ffn1_w2", "w_in", "w_out", "ffn2_w13", "ffn2_w2")
_TRANSPOSED = ("ffn1_w13", "ffn2_w13", "w_in")
_ORDER = ("ffn1_norm", "ffn1_w13", "ffn1_w2", "mix_norm", "w_in", "conv_w", "conv_b", "conv_ln_g", "conv_ln_b",
          "forget_b", "out_norm_conv", "out_norm_attn", "w_out", "ffn2_norm", "ffn2_w13", "ffn2_w2", "final_norm")


def kernel(x, ffn1_norm, ffn1_w13, ffn1_w2, mix_norm, w_in, conv_w, conv_b, conv_ln_g, conv_ln_b, forget_b, out_norm_conv, out_norm_attn, w_out, ffn2_norm, ffn2_w13, ffn2_w2, final_norm, loss_target, m_ffn1_norm, m_ffn1_w13, m_ffn1_w2, m_mix_norm, m_w_in, m_conv_w, m_conv_b, m_conv_ln_g, m_conv_ln_b, m_forget_b, m_out_norm_conv, m_out_norm_attn, m_w_out, m_ffn2_norm, m_ffn2_w13, m_ffn2_w2, m_final_norm, v_ffn1_norm, v_ffn1_w13, v_ffn1_w2, v_mix_norm, v_w_in, v_conv_w, v_conv_b, v_conv_ln_g, v_conv_ln_b, v_forget_b, v_out_norm_conv, v_out_norm_attn, v_w_out, v_ffn2_norm, v_ffn2_w13, v_ffn2_w2, v_final_norm):
    w = dict(ffn1_norm=ffn1_norm, ffn1_w13=ffn1_w13, ffn1_w2=ffn1_w2, mix_norm=mix_norm, w_in=w_in, conv_w=conv_w,
             conv_b=conv_b, conv_ln_g=conv_ln_g, conv_ln_b=conv_ln_b, forget_b=forget_b, out_norm_conv=out_norm_conv,
             out_norm_attn=out_norm_attn, w_out=w_out, ffn2_norm=ffn2_norm, ffn2_w13=ffn2_w13, ffn2_w2=ffn2_w2,
             final_norm=final_norm)
    m = dict(ffn1_norm=m_ffn1_norm, ffn1_w13=m_ffn1_w13, ffn1_w2=m_ffn1_w2, mix_norm=m_mix_norm, w_in=m_w_in,
             conv_w=m_conv_w, conv_b=m_conv_b, conv_ln_g=m_conv_ln_g, conv_ln_b=m_conv_ln_b, forget_b=m_forget_b,
             out_norm_conv=m_out_norm_conv, out_norm_attn=m_out_norm_attn, w_out=m_w_out, ffn2_norm=m_ffn2_norm,
             ffn2_w13=m_ffn2_w13, ffn2_w2=m_ffn2_w2, final_norm=m_final_norm)
    v = dict(ffn1_norm=v_ffn1_norm, ffn1_w13=v_ffn1_w13, ffn1_w2=v_ffn1_w2, mix_norm=v_mix_norm, w_in=v_w_in,
             conv_w=v_conv_w, conv_b=v_conv_b, conv_ln_g=v_conv_ln_g, conv_ln_b=v_conv_ln_b, forget_b=v_forget_b,
             out_norm_conv=v_out_norm_conv, out_norm_attn=v_out_norm_attn, w_out=v_w_out, ffn2_norm=v_ffn2_norm,
             ffn2_w13=v_ffn2_w13, ffn2_w2=v_ffn2_w2, final_norm=v_final_norm)
    shapes = {n: a.shape for n, a in w.items()}
    T, D = x.shape[1], x.shape[2]
    def two(n, a):
        if a.ndim != 3:
            return a.reshape(1, -1)
        a = a.reshape(a.shape[-2], a.shape[-1])
        return a.T if n in _TRANSPOSED else a

    w2d = {n: two(n, a) for n, a in w.items()}
    m2d = {n: two(n, a) for n, a in m.items()}
    v2d = {n: two(n, a) for n, a in v.items()}

    shard = {n: w2d[n].astype(MXU) for n in _BIG}
    shard["conv_w"] = w2d["conv_w"]
    norms = {n: w2d[n] for n in _SMALL}
    norms["forget_b"] = jnp.pad(w2d["forget_b"], ((0, 0), (0, LANES - N_HEADS)))
    loss_part, dx, gathered_small, big = _local_step(x[0], loss_target[0], norms, shard)
    loss = lax.psum(loss_part, ("x", "y", "c"))

    grads, deltas, new_m, new_v = {}, {}, {}, {}
    for n in _BIG:
        g, d, nm, nv = _adamw(w2d[n], m2d[n], v2d[n], big[n], "adamw_" + n)
        grads[n], deltas[n], new_m[n], new_v[n] = g, d, nm, nv

    small_out, conv_g_full = _adamw_small(gathered_small, w2d, m2d, v2d, "adamw_small")
    for n in _SMALL:
        grads[n], deltas[n], new_m[n], new_v[n] = small_out[n]
    conv_g_full = conv_g_full[:CONV_TAPS]
    xi, yi, ci = _position()
    cw = shapes["conv_w"][-1]
    conv_g_mine = lax.dynamic_slice_in_dim(conv_g_full, _flat(xi, yi, ci) * cw, cw, axis=1)
    g, d, nm, nv = _adamw(w2d["conv_w"], m2d["conv_w"], v2d["conv_w"], conv_g_mine[None], "adamw_conv_w")
    grads["conv_w"], deltas["conv_w"], new_m["conv_w"], new_v["conv_w"] = g, d, nm, nv

    shaped = lambda dct: [(dct[n].T if n in _TRANSPOSED else dct[n]).reshape(shapes[n]) for n in _ORDER]
    return (loss, dx[None], *shaped(grads), *shaped(deltas), *shaped(new_m), *shaped(new_v))
```

```python
from typing import NamedTuple

import jax
import jax.numpy as jnp
from jax import lax
from jax.experimental import pallas as pl
from jax.experimental.pallas import tpu as pltpu
from jax.experimental.pallas import tpu_sc as plsc

F32 = jnp.float32
MXU = jnp.bfloat16
EPS = 1e-6
N_HEADS = 8
HEAD_DIM = 64
D_CONV = 512
D_ATTN = N_HEADS * HEAD_DIM
CONV_TAPS = 31
CONV_HALO = 32
SCALE = HEAD_DIM ** -0.5
NEG = -1e30
LANES = 128
N_DEV = 8
N_CHIPS = N_DEV // 2
MESH = pl.DeviceIdType.MESH
MIB = 1 << 20

ADAM_LR = 0.001
ADAM_B1 = 0.9
ADAM_B2 = 0.999
ADAM_EPS = 1e-08
ADAM_WD = 0.01
ADAM_STEP = 10


_UNREAD = pl.BlockSpec(memory_space=pl.ANY)


def _pallas_call(body, *, out_shape, **kwargs):
    in_hbm = lambda s: pltpu.HBM(s.shape, s.dtype)
    outs = [in_hbm(s) for s in out_shape] if isinstance(out_shape, (list, tuple)) else in_hbm(out_shape)
    call = pl.pallas_call(body, out_shape=outs, **kwargs)
    return lambda *operands: call(*[pltpu.with_memory_space_constraint(a, pltpu.HBM) for a in operands])


def _params(vmem_mib, n_axes):
    return pltpu.CompilerParams(dimension_semantics=("arbitrary",) * n_axes, vmem_limit_bytes=vmem_mib * MIB)


def _mm(a, b):
    return jnp.dot(a, b, preferred_element_type=F32)


def _mm_nt(a, b):
    return lax.dot_general(a, b, (((1,), (1,)), ((), ())), preferred_element_type=F32)


def _mm_tn(a, b):
    return lax.dot_general(a, b, (((0,), (0,)), ((), ())), preferred_element_type=F32)


def _rms_fwd(x, g):
    r = lax.rsqrt(jnp.mean(x * x, axis=-1, keepdims=True) + EPS)
    return x * r * g, r


def _rms_bwd(x, r, g, dy):
    gdy = dy * g
    dx = r * gdy - x * (r * r * r) * jnp.mean(x * gdy, axis=-1, keepdims=True)
    dg = jnp.sum(dy * x * r, axis=0, keepdims=True)
    return dx, dg


def _silu_grad(z, sz):
    return sz * (1.0 + z * (1.0 - sz))


def _three_terms(x):
    x1 = x.astype(jnp.bfloat16)
    r1 = x - x1.astype(F32)
    x2 = r1.astype(jnp.bfloat16)
    x3 = (r1 - x2.astype(F32)).astype(jnp.bfloat16)
    return x1, x2, x3


def _exact_tri_dot(tri, x):
    x1, x2, x3 = _three_terms(x)
    return _mm(tri, x1) + _mm(tri, x2) + _mm(tri, x3)


def _exact_dot_01(x, sel):
    x1, x2, x3 = _three_terms(x)
    return _mm(x1, sel) + _mm(x2, sel) + _mm(x3, sel)


def _tile(n, want):
    t = min(n, want)
    assert n % t == 0
    return t


_FFN_CHUNK = 256


def _ffn_up(x, g, w13, name):
    T, D = x.shape
    _, J, bf, _ = w13.shape
    tm = _tile(T, 512)
    I = T // tm

    def body(x_ref, g_ref, w13_ref, G_ref, U_ref, A_ref, h_s):
        j = pl.program_id(0)
        i = pl.program_id(1)
        rows = pl.ds(pl.multiple_of(i * tm, tm), tm)

        @pl.when(j == 0)
        def _():
            h, _ = _rms_fwd(x_ref[...], g_ref[...])
            h_s[rows, :] = h.astype(MXU)

        chunks = [slice(r0, r0 + _FFN_CHUNK) for r0 in range(0, tm, _FFN_CHUNK)]
        hbs = [h_s[pl.ds(pl.multiple_of(i * tm + rs.start, _FFN_CHUNK), _FFN_CHUNK), :] for rs in chunks]
        GU = [(_mm_nt(hb, w13_ref[0]), _mm_nt(hb, w13_ref[1])) for hb in hbs]
        for rs, (G, U) in zip(chunks, GU):
            G_ref[rs, :] = G.astype(MXU)
            U_ref[rs, :] = U.astype(MXU)
            A_ref[rs, :] = (G * jax.nn.sigmoid(G) * U).astype(MXU)

    blk = pl.BlockSpec((None, tm, bf), lambda j, i: (j, i, 0))
    hid = jax.ShapeDtypeStruct((J, T, bf), MXU)
    return _pallas_call(
        body, name=name, grid=(J, I),
        in_specs=[pl.BlockSpec((tm, D), lambda j, i: (jnp.where(j == 0, i, I - 1), 0)),
                  pl.BlockSpec((1, D), lambda j, i: (0, 0)),
                  pl.BlockSpec((2, None, bf, D), lambda j, i: (0, j, 0, 0))],
        out_specs=[blk, blk, blk],
        out_shape=[hid, hid, hid],
        scratch_shapes=[pltpu.VMEM((T, D), MXU)],
        compiler_params=_params(40, 2),
    )(x, g, w13)


def _ffn_down(x, A, w2, name):
    T, D = x.shape
    J, _, bf = A.shape
    tm = _tile(T, 512)

    def body(x_ref, A_ref, w2_ref, xo_ref):
        f = _mm(A_ref[0], w2_ref[0:bf, :])
        for j in range(1, J):
            f = f + _mm(A_ref[j], w2_ref[j * bf:(j + 1) * bf, :])
        xo_ref[...] = x_ref[...] + 0.5 * f

    row = pl.BlockSpec((tm, D), lambda i: (i, 0))
    return _pallas_call(
        body, name=name, grid=(T // tm,),
        in_specs=[row, pl.BlockSpec((J, tm, bf), lambda i: (0, i, 0)), pl.BlockSpec((J * bf, D), lambda i: (0, 0))],
        out_specs=row,
        out_shape=jax.ShapeDtypeStruct((T, D), F32),
        compiler_params=_params(48, 1),
    )(x, A, w2)


def _ffn_bwd_act(x, g, dy, Gs, Us, w13, w2, name, after=()):
    T, D = x.shape
    _, J, bf, _ = w13.shape
    tm = _tile(T, 512)
    I = T // tm

    def body(x_ref, g_ref, dy_ref, G_ref, U_ref, w13_ref, w2_ref, *rest):
        dx_ref, dg_ref, h_ref, dG_ref, dU_ref, dh_s, dF_s, h_s = rest[len(after):]
        j = pl.program_id(0)
        i = pl.program_id(1)
        rows = pl.ds(pl.multiple_of(i * tm, tm), tm)

        @pl.when(j == 0)
        def _():
            h, _ = _rms_fwd(x_ref[...], g_ref[...])
            hb = h.astype(MXU)
            h_s[rows, :] = hb
            h_ref[...] = hb
            dF_s[rows, :] = (0.5 * dy_ref[...]).astype(MXU)
            dh_s[rows, :] = jnp.zeros((tm, D), F32)

        chunks = [slice(r0, r0 + _FFN_CHUNK) for r0 in range(0, tm, _FFN_CHUNK)]
        crows = [pl.ds(pl.multiple_of(i * tm + rs.start, _FFN_CHUNK), _FFN_CHUNK) for rs in chunks]
        dAs = [_mm_nt(dF_s[cr, :], w2_ref[...]) for cr in crows]
        for rs, cr, dA in zip(chunks, crows, dAs):
            G = G_ref[rs, :].astype(F32)
            U = U_ref[rs, :].astype(F32)
            sg = jax.nn.sigmoid(G)
            s = G * sg
            dUb = (dA * s).astype(MXU)
            dGb = (dA * U * _silu_grad(G, sg)).astype(MXU)
            dG_ref[rs, :] = dGb
            dU_ref[rs, :] = dUb
            dh_s[cr, :] += _mm(dGb, w13_ref[0]) + _mm(dUb, w13_ref[1])

        @pl.when(j == J - 1)
        def _():
            xv = x_ref[...]
            gv = g_ref[...]
            _, r = _rms_fwd(xv, gv)
            dxn, dgp = _rms_bwd(xv, r, gv, dh_s[rows, :])
            dx_ref[...] = dy_ref[...] + dxn

            @pl.when(i == 0)
            def _():
                dg_ref[...] = dgp

            @pl.when(i > 0)
            def _():
                dg_ref[...] += dgp

    ends = lambda j, i: (jnp.where((j == 0) | (j == J - 1), i, I - 1), 0)
    blk = pl.BlockSpec((None, tm, bf), lambda j, i: (j, i, 0))
    hid = jax.ShapeDtypeStruct((J, T, bf), MXU)
    return _pallas_call(
        body, name=name, grid=(J, I),
        in_specs=[pl.BlockSpec((tm, D), ends), pl.BlockSpec((1, D), lambda j, i: (0, 0)), pl.BlockSpec((tm, D), ends),
                  blk, blk, pl.BlockSpec((2, None, bf, D), lambda j, i: (0, j, 0, 0)),
                  pl.BlockSpec((bf, D), lambda j, i: (j, 0))] + [_UNREAD] * len(after),
        out_specs=[pl.BlockSpec((tm, D), lambda j, i: (jnp.where(j == J - 1, i, 0), 0)),
                   pl.BlockSpec((1, D), lambda j, i: (0, 0)),
                   pl.BlockSpec((tm, D), lambda j, i: (jnp.where(j == 0, i, I - 1), 0)), blk, blk],
        out_shape=[jax.ShapeDtypeStruct((T, D), F32), jax.ShapeDtypeStruct((1, D), F32),
                   jax.ShapeDtypeStruct((T, D), MXU), hid, hid],
        scratch_shapes=[pltpu.VMEM((T, D), F32), pltpu.VMEM((T, D), MXU), pltpu.VMEM((T, D), MXU)],
        compiler_params=_params(58, 2),
    )(x, g, dy, Gs, Us, w13, w2, *after)


def _ffn_w13_grad(h, dG, dU, name):
    T, D = h.shape
    J, _, bf = dG.shape

    def body(h_ref, dG_ref, dU_ref, dw13_ref):
        dw13_ref[0] = _mm_tn(dG_ref[...], h_ref[...]).astype(dw13_ref.dtype)
        dw13_ref[1] = _mm_tn(dU_ref[...], h_ref[...]).astype(dw13_ref.dtype)

    blk = pl.BlockSpec((None, T, bf), lambda j: (j, 0, 0))
    return _pallas_call(
        body, name=name, grid=(J,),
        in_specs=[pl.BlockSpec((T, D), lambda j: (0, 0)), blk, blk],
        out_specs=pl.BlockSpec((2, None, bf, D), lambda j: (0, j, 0, 0)),
        out_shape=jax.ShapeDtypeStruct((2, J, bf, D), MXU),
        compiler_params=_params(48, 1),
    )(h, dG, dU)


def _ffn_w2_grad(dy, A, name, after=()):
    T, D = dy.shape
    J, _, bf = A.shape

    def body(dy_ref, A_ref, *rest):
        dw2_ref, dF_s = rest[len(after):]

        @pl.when(pl.program_id(0) == 0)
        def _():
            dF_s[...] = (0.5 * dy_ref[...]).astype(MXU)

        dw2_ref[...] = _mm_tn(A_ref[...], dF_s[...]).astype(dw2_ref.dtype)

    return _pallas_call(
        body, name=name, grid=(J,),
        in_specs=[pl.BlockSpec((T, D), lambda j: (0, 0)), pl.BlockSpec((None, T, bf), lambda j: (j, 0, 0))]
        + [_UNREAD] * len(after),
        out_specs=pl.BlockSpec((bf, D), lambda j: (j, 0)),
        out_shape=jax.ShapeDtypeStruct((J * bf, D), MXU),
        scratch_shapes=[pltpu.VMEM((T, D), MXU)],
        compiler_params=_params(48, 1),
    )(dy, A, *after)


_AG0, _Q0, _K0, _V0, _F0 = 0, 2 * D_CONV, 2 * D_CONV + D_ATTN, 2 * D_CONV + 2 * D_ATTN, 2 * D_CONV + 3 * D_ATTN
N_IN = _F0 + N_HEADS
N_IN_PAD = _F0 + LANES


def _inproj_fwd(x1, gm, winp, name):
    T, D = x1.shape
    tm = _tile(T, 256)

    def body(x_ref, g_ref, w_ref, ag_ref, k_ref, v_ref, qT_ref, kT_ref, vT_ref, fl_ref):
        h, _ = _rms_fwd(x_ref[...], g_ref[...])
        hb = h.astype(MXU)
        ag_ref[...] = _mm_nt(hb, w_ref[_AG0:_Q0, :])
        qT_ref[...] = (_mm_nt(hb, w_ref[_Q0:_K0, :]) * SCALE).T.astype(MXU)
        for c0, ref, refT in ((_K0, k_ref, kT_ref), (_V0, v_ref, vT_ref)):
            y = _mm_nt(hb, w_ref[c0:c0 + D_ATTN, :])
            ref[...] = y.astype(MXU)
            refT[...] = y.T.astype(MXU)
        fl_ref[...] = _mm_nt(hb, w_ref[_F0:N_IN_PAD, :])

    row = lambda w: pl.BlockSpec((tm, w), lambda i: (i, 0))
    col = pl.BlockSpec((D_ATTN, tm), lambda i: (0, i))
    std = jax.ShapeDtypeStruct((T, D_ATTN), MXU)
    trn = jax.ShapeDtypeStruct((D_ATTN, T), MXU)
    return _pallas_call(
        body, name=name, grid=(T // tm,),
        in_specs=[row(D), pl.BlockSpec((1, D), lambda i: (0, 0)), pl.BlockSpec((N_IN_PAD, D), lambda i: (0, 0))],
        out_specs=[row(2 * D_CONV), row(D_ATTN), row(D_ATTN), col, col, col, row(LANES)],
        out_shape=[jax.ShapeDtypeStruct((T, 2 * D_CONV), F32), std, std, trn, trn, trn,
                   jax.ShapeDtypeStruct((T, LANES), F32)],
        compiler_params=_params(40, 1),
    )(x1, gm, winp)


def _inproj_bwd_act(x1, gm, dx2, dag, dqT, dkT, dvT, dfl, winp, name):
    T, D = x1.shape
    tm = _tile(T, 256)

    def body(x_ref, g_ref, dx2_ref, dag_ref, dqT_ref, dkT_ref, dvT_ref, dfl_ref, w_ref, dx1_ref, dg_ref, h_ref):
        i = pl.program_id(0)
        xv = x_ref[...]
        gv = g_ref[...]
        h, r = _rms_fwd(xv, gv)
        h_ref[...] = h.astype(MXU)
        dh = _mm(dag_ref[...], w_ref[_AG0:_Q0, :])
        for c0, ref in ((_Q0, dqT_ref), (_K0, dkT_ref), (_V0, dvT_ref)):
            dh = dh + _mm_tn(ref[...].astype(MXU), w_ref[c0:c0 + D_ATTN, :])
        dh = dh + _mm(dfl_ref[...].astype(MXU), w_ref[_F0:N_IN_PAD, :])
        dxn, dgp = _rms_bwd(xv, r, gv, dh)
        dx1_ref[...] = dx2_ref[...] + dxn

        @pl.when(i == 0)
        def _():
            dg_ref[...] = dgp

        @pl.when(i > 0)
        def _():
            dg_ref[...] += dgp

    row = lambda w: pl.BlockSpec((tm, w), lambda i: (i, 0))
    col = pl.BlockSpec((D_ATTN, tm), lambda i: (0, i))
    full = lambda a, b: pl.BlockSpec((a, b), lambda i: (0, 0))
    return _pallas_call(
        body, name=name, grid=(T // tm,),
        in_specs=[row(D), full(1, D), row(D), row(2 * D_CONV), col, col, col, row(LANES), full(N_IN_PAD, D)],
        out_specs=[row(D), full(1, D), row(D)],
        out_shape=[jax.ShapeDtypeStruct((T, D), F32), jax.ShapeDtypeStruct((1, D), F32),
                   jax.ShapeDtypeStruct((T, D), MXU)],
        compiler_params=_params(40, 1),
    )(x1, gm, dx2, dag, dqT, dkT, dvT, dfl, winp)


def _inproj_bwd_weights(h, dag, dqT, dkT, dvT, dfl, name, after=()):
    T, D = h.shape

    def body(h_ref, dag_ref, dqT_ref, dkT_ref, dvT_ref, dfl_ref, *rest):
        dw_ref = rest[len(after)]
        hb = h_ref[...]
        dw_ref[_AG0:_Q0, :] = _mm_tn(dag_ref[...], hb).astype(dw_ref.dtype)
        for c0, ref in ((_Q0, dqT_ref), (_K0, dkT_ref), (_V0, dvT_ref)):
            dw_ref[c0:c0 + D_ATTN, :] = _mm(ref[...].astype(MXU), hb).astype(dw_ref.dtype)
        dw_ref[_F0:N_IN_PAD, :] = _mm_tn(dfl_ref[...].astype(MXU), hb).astype(dw_ref.dtype)

    vmem = pl.BlockSpec(memory_space=pltpu.VMEM)
    return _pallas_call(
        body, name=name, in_specs=[vmem] * 6 + [_UNREAD] * len(after), out_specs=vmem,
        out_shape=jax.ShapeDtypeStruct((N_IN_PAD, D), MXU),
        compiler_params=pltpu.CompilerParams(vmem_limit_bytes=56 * MIB),
    )(h, dag, dqT, dkT, dvT, dfl, *after)


def _forget_fwd(fl, fbp, name):
    T = fl.shape[0]
    tb = _tile(T, 256)

    def body(fl_ref, fb_ref, cum_ref, cumT_ref):
        ri = lax.broadcasted_iota(jnp.int32, (tb, tb), 0)
        ci = lax.broadcasted_iota(jnp.int32, (tb, tb), 1)
        tri = (ri >= ci).astype(jnp.bfloat16)
        carry = jnp.zeros((1, LANES), F32)
        for b in range(T // tb):
            z = fl_ref[b * tb:(b + 1) * tb, :] + fb_ref[...]
            lf = jnp.minimum(z, 0.0) - jnp.log1p(jnp.exp(-jnp.abs(z)))
            c = _exact_tri_dot(tri, lf) + carry
            cum_ref[b * tb:(b + 1) * tb, :] = c
            carry = c[tb - 1:tb, :]
        cumT_ref[...] = cum_ref[...].T[:N_HEADS, :]

    return _pallas_call(
        body, name=name,
        out_shape=[jax.ShapeDtypeStruct((T, LANES), F32), jax.ShapeDtypeStruct((N_HEADS, T), F32)],
        compiler_params=pltpu.CompilerParams(vmem_limit_bytes=32 * MIB),
    )(fl, fbp)


def _forget_bwd(dcum, fl, fbp, name):
    T = fl.shape[0]
    tb = _tile(T, 256)

    def body(dc_ref, fl_ref, fb_ref, dfl_ref, dfb_ref):
        ri = lax.broadcasted_iota(jnp.int32, (tb, tb), 0)
        ci = lax.broadcasted_iota(jnp.int32, (tb, tb), 1)
        tri = (ri <= ci).astype(jnp.bfloat16)
        carry = jnp.zeros((1, LANES), F32)
        dfb = jnp.zeros((1, LANES), F32)
        for b in reversed(range(T // tb)):
            sl = slice(b * tb, (b + 1) * tb)
            dl = _exact_tri_dot(tri, dc_ref[sl, :]) + carry
            carry = dl[0:1, :]
            z = fl_ref[sl, :] + fb_ref[...]
            dfl = dl * jax.nn.sigmoid(-z)
            dfl_ref[sl, :] = dfl
            dfb = dfb + jnp.sum(dfl, axis=0, keepdims=True)
        dfb_ref[...] = dfb

    return _pallas_call(
        body, name=name,
        out_shape=[jax.ShapeDtypeStruct((T, LANES), F32), jax.ShapeDtypeStruct((1, LANES), F32)],
        compiler_params=pltpu.CompilerParams(vmem_limit_bytes=32 * MIB),
    )(dcum, fl, fbp)


def _causal_keep(i, j, tq, tk):
    key = j * tk + lax.broadcasted_iota(jnp.int32, (tk, tq), 0)
    qry = i * tq + lax.broadcasted_iota(jnp.int32, (tk, tq), 1)
    return key <= qry


def _split_hi_lo(x):
    hi = x.astype(MXU)
    lo = (x - hi.astype(F32)).astype(MXU)
    return hi, lo


def _attn_fwd(qT, k, vT, cum, cumT, name):
    T = k.shape[0]
    tq = _tile(T, 256)
    tk = _tile(tq, 256)
    kpq = tq // tk
    heads = [slice(HEAD_DIM * h, HEAD_DIM * (h + 1)) for h in range(N_HEADS)]

    def body(qT_ref, k_ref, vT_ref, cum_ref, cumT_ref, o_ref, lseT_ref, acc_s, m_s, l_s):
        i = pl.program_id(0)
        acc_s[...] = jnp.zeros_like(acc_s)
        m_s[...] = jnp.full_like(m_s, NEG)
        l_s[...] = jnp.zeros_like(l_s)

        def kblock(j, masked):
            rows = pl.ds(pl.multiple_of(j * tk, tk), tk)
            keep = _causal_keep(i, j, tq, tk) if masked else None
            bias = [cumT_ref[h:h + 1, :] - cum_ref[rows, h:h + 1] for h in range(N_HEADS)]
            qk = [_mm(k_ref[rows, hs], qT_ref[hs, :]) + bias[h] for h, hs in enumerate(heads)]
            for h, hs in enumerate(heads):
                sT = qk[h]
                if masked:
                    sT = jnp.where(keep, sT, NEG)
                m_old = m_s[h:h + 1, :]
                m_new = jnp.maximum(m_old, jnp.max(sT, axis=0, keepdims=True))
                alpha = jnp.exp(m_old - m_new)
                pT = jnp.exp(sT - m_new)
                l_s[h:h + 1, :] = alpha * l_s[h:h + 1, :] + jnp.sum(pT, axis=0, keepdims=True)
                p_hi, p_lo = _split_hi_lo(pT)
                vh = vT_ref[hs, rows]
                acc_s[hs, :] = alpha * acc_s[hs, :] + (_mm(vh, p_hi) + _mm(vh, p_lo))
                m_s[h:h + 1, :] = m_new

        def unmasked(j, c):
            kblock(j, False)
            return c

        lax.fori_loop(0, kpq * i, unmasked, 0)
        for d in range(kpq):
            kblock(kpq * i + d, True)
        for h, hs in enumerate(heads):
            acc_s[hs, :] = acc_s[hs, :] / l_s[h:h + 1, :]
        o_ref[...] = acc_s[...].T
        lseT_ref[...] = m_s[...] + jnp.log(l_s[...])

    full = lambda a, b: pl.BlockSpec((a, b), lambda i: (0, 0))
    colblk = lambda r: pl.BlockSpec((r, tq), lambda i: (0, i))
    return _pallas_call(
        body, name=name, grid=(T // tq,),
        in_specs=[colblk(D_ATTN), full(T, D_ATTN), full(D_ATTN, T), full(T, LANES), colblk(N_HEADS)],
        out_specs=[pl.BlockSpec((tq, D_ATTN), lambda i: (i, 0)), colblk(N_HEADS)],
        out_shape=[jax.ShapeDtypeStruct((T, D_ATTN), F32), jax.ShapeDtypeStruct((N_HEADS, T), F32)],
        scratch_shapes=[pltpu.VMEM((D_ATTN, tq), F32), pltpu.VMEM((N_HEADS, tq), F32),
                        pltpu.VMEM((N_HEADS, tq), F32)],
        compiler_params=_params(40, 1),
    )(qT, k, vT, cum, cumT)


def _attn_bwd(qT, k, kT, v, doT, lseT, deltaT, cum, cumT, name, after=()):
    T = k.shape[0]
    tq = _tile(T, 256)
    tk = _tile(tq, 256)
    kpq = tq // tk
    heads = [slice(HEAD_DIM * h, HEAD_DIM * (h + 1)) for h in range(N_HEADS)]

    def body(qT_ref, k_ref, kT_ref, v_ref, doT_ref, lseT_ref, dlT_ref, cum_ref, cumT_ref, *rest):
        dq_ref, dk_ref, dv_ref, dcum_ref, dq_s = rest[len(after):]
        i = pl.program_id(0)

        @pl.when(i == 0)
        def _():
            dk_ref[...] = jnp.zeros_like(dk_ref)
            dv_ref[...] = jnp.zeros_like(dv_ref)
            dcum_ref[...] = jnp.zeros_like(dcum_ref)

        dq_s[...] = jnp.zeros_like(dq_s)

        def kblock(j, masked):
            rows = pl.ds(pl.multiple_of(j * tk, tk), tk)
            keep = _causal_keep(i, j, tq, tk) if masked else None
            bias = [cumT_ref[h:h + 1, :] - cum_ref[rows, h:h + 1] for h in range(N_HEADS)]
            qk = [_mm(k_ref[rows, hs], qT_ref[hs, :]) + bias[h] for h, hs in enumerate(heads)]
            dps = [_mm(v_ref[rows, hs], doT_ref[hs, :]) for hs in heads]
            for h, hs in enumerate(heads):
                sT = qk[h]
                if masked:
                    sT = jnp.where(keep, sT, NEG)
                pT = jnp.exp(sT - lseT_ref[h:h + 1, :])
                dsT = pT * (dps[h] - dlT_ref[h:h + 1, :])
                dcum_ref[rows, h:h + 1] += -jnp.sum(dsT, axis=1, keepdims=True)
                dsb = dsT.astype(MXU)
                dv_ref[hs, rows] += _mm_nt(doT_ref[hs, :], pT.astype(MXU))
                dk_ref[hs, rows] += _mm_nt(qT_ref[hs, :], dsb)
                dq_s[hs, :] += _mm(kT_ref[hs, rows], dsb)

        def unmasked(j, c):
            kblock(j, False)
            return c

        lax.fori_loop(0, kpq * i, unmasked, 0)
        for d in range(kpq):
            kblock(kpq * i + d, True)
        dq_ref[...] = (dq_s[...] * SCALE).astype(dq_ref.dtype)

    full = lambda a, b: pl.BlockSpec((a, b), lambda i: (0, 0))
    colblk = lambda r: pl.BlockSpec((r, tq), lambda i: (0, i))
    return _pallas_call(
        body, name=name, grid=(T // tq,),
        in_specs=[colblk(D_ATTN), full(T, D_ATTN), full(D_ATTN, T), full(T, D_ATTN), colblk(D_ATTN),
                  colblk(N_HEADS), colblk(N_HEADS), full(T, LANES), colblk(N_HEADS)] + [_UNREAD] * len(after),
        out_specs=[colblk(D_ATTN), full(D_ATTN, T), full(D_ATTN, T), full(T, LANES)],
        out_shape=[
            jax.ShapeDtypeStruct((D_ATTN, T), MXU),
            jax.ShapeDtypeStruct((D_ATTN, T), F32),
            jax.ShapeDtypeStruct((D_ATTN, T), F32),
            jax.ShapeDtypeStruct((T, LANES), F32),
        ],
        scratch_shapes=[pltpu.VMEM((D_ATTN, tq), F32)],
        compiler_params=_params(48, 1),
    )(qT, k, kT, v, doT, lseT, deltaT, cum, cumT, *after)


_ROWS_PER_CHUNK = 64


def _glu_halo(ag_ref, agh_ref, uext_s, tm, first):
    a = ag_ref[:, :D_CONV]
    sg = jax.nn.sigmoid(ag_ref[:, D_CONV:])
    uh = agh_ref[:, :D_CONV] * jax.nn.sigmoid(agh_ref[:, D_CONV:])
    uext_s[0:CONV_HALO, :] = jnp.where(first, 0.0, uh)
    uext_s[CONV_HALO:CONV_HALO + tm, :] = a * sg
    return a, sg


_SUBLANES = 8


def _shifted_copies(ext_s, sh_s, rows):
    for k in range(1, _SUBLANES):
        sh_s[k, 0:rows, :] = ext_s[pl.ds(k, rows), :]


def _window(ext_s, sh_s, start, rows):
    k = start % _SUBLANES
    if k == 0:
        return ext_s[pl.ds(start, rows), :]
    return sh_s[k, pl.ds(start - k, rows), :]


def _layer_norm_stats(y):
    mu = jnp.mean(y, axis=-1, keepdims=True)
    xc = y - mu
    rs = lax.rsqrt(jnp.mean(xc * xc, axis=-1, keepdims=True) + EPS)
    return xc * rs, rs


def _conv_fwd(ag, w32, cb, lg, lb, name):
    T = ag.shape[0]
    tm = _tile(T, 256)
    rc = _tile(tm, _ROWS_PER_CHUNK)
    hb = tm // CONV_HALO

    def body(ag_ref, agh_ref, w_ref, cb_ref, lg_ref, lb_ref, yc_ref, c_ref, uext_s, ush_s):
        i = pl.program_id(0)
        _glu_halo(ag_ref, agh_ref, uext_s, tm, i == 0)
        _shifted_copies(uext_s, ush_s, tm + CONV_HALO - _SUBLANES)
        for r0 in range(0, tm, rc):
            acc = jnp.zeros((rc, D_CONV), F32)
            for t in range(CONV_TAPS):
                acc = acc + _window(uext_s, ush_s, r0 + CONV_HALO - (CONV_TAPS - 1) + t, rc) * w_ref[t:t + 1, :]
            y = acc + cb_ref[...]
            yc_ref[r0:r0 + rc, :] = y
            n, _ = _layer_norm_stats(y)
            z = n * lg_ref[...] + lb_ref[...]
            c_ref[r0:r0 + rc, :] = z * jax.nn.sigmoid(z)

    row = lambda w: pl.BlockSpec((tm, w), lambda i: (i, 0))
    full = lambda a, b: pl.BlockSpec((a, b), lambda i: (0, 0))
    return _pallas_call(
        body, name=name, grid=(T // tm,),
        in_specs=[row(2 * D_CONV),
                  pl.BlockSpec((CONV_HALO, 2 * D_CONV), lambda i: (jnp.maximum(i * hb - 1, 0), 0)),
                  full(CONV_HALO, D_CONV), full(1, D_CONV), full(1, D_CONV), full(1, D_CONV)],
        out_specs=[row(D_CONV), row(D_CONV)],
        out_shape=[jax.ShapeDtypeStruct((T, D_CONV), F32), jax.ShapeDtypeStruct((T, D_CONV), F32)],
        scratch_shapes=[pltpu.VMEM((CONV_HALO + tm, D_CONV), F32),
                        pltpu.VMEM((_SUBLANES, CONV_HALO + tm, D_CONV), F32)],
        compiler_params=_params(32, 1),
    )(ag, ag, w32, cb, lg, lb)


def _conv_bwd(dc, yc, ag, w32, lg, lb, name):
    T = ag.shape[0]
    tm = _tile(T, 256)
    rc = _tile(tm, _ROWS_PER_CHUNK)
    I = T // tm
    hb = tm // CONV_HALO
    n_halo_blocks = T // CONV_HALO

    def body(dc_ref, yc_ref, dch_ref, ych_ref, ag_ref, agh_ref, w_ref, lg_ref, lb_ref,
             dag_ref, dw_ref, dcb_ref, dlg_ref, dlb_ref, uext_s, dext_s, ush_s, dsh_s):
        i = pl.program_id(0)
        lgv = lg_ref[...]
        lbv = lb_ref[...]

        def ln_bwd(dcv, ycv):
            n, rs = _layer_norm_stats(ycv)
            z = n * lgv + lbv
            dz = dcv * _silu_grad(z, jax.nn.sigmoid(z))
            dn = dz * lgv
            dy = rs * (dn - jnp.mean(dn, axis=-1, keepdims=True) - n * jnp.mean(dn * n, axis=-1, keepdims=True))
            return dy, dz, n

        dy, dz, n = ln_bwd(dc_ref[...], yc_ref[...])
        dyh, _, _ = ln_bwd(dch_ref[...], ych_ref[...])
        dext_s[0:tm, :] = dy
        dext_s[tm:tm + CONV_HALO, :] = jnp.where(i == I - 1, 0.0, dyh)
        a, sg = _glu_halo(ag_ref, agh_ref, uext_s, tm, i == 0)
        _shifted_copies(uext_s, ush_s, tm + CONV_HALO - _SUBLANES)
        _shifted_copies(dext_s, dsh_s, tm + CONV_HALO - _SUBLANES)

        @pl.when(i == 0)
        def _():
            dw_ref[...] = jnp.zeros_like(dw_ref)
            dcb_ref[...] = jnp.zeros_like(dcb_ref)
            dlg_ref[...] = jnp.zeros_like(dlg_ref)
            dlb_ref[...] = jnp.zeros_like(dlb_ref)

        dcb_ref[...] += jnp.sum(dy, axis=0, keepdims=True)
        dlg_ref[...] += jnp.sum(dz * n, axis=0, keepdims=True)
        dlb_ref[...] += jnp.sum(dz, axis=0, keepdims=True)
        for t in range(CONV_TAPS):
            u_t = _window(uext_s, ush_s, CONV_HALO - (CONV_TAPS - 1) + t, tm)
            dw_ref[t:t + 1, :] += jnp.sum(dy * u_t, axis=0, keepdims=True)
        for r0 in range(0, tm, rc):
            acc = jnp.zeros((rc, D_CONV), F32)
            for t in range(CONV_TAPS):
                acc = acc + _window(dext_s, dsh_s, r0 + (CONV_TAPS - 1) - t, rc) * w_ref[t:t + 1, :]
            a_c = a[r0:r0 + rc, :]
            sg_c = sg[r0:r0 + rc, :]
            dag_ref[r0:r0 + rc, :D_CONV] = (acc * sg_c).astype(dag_ref.dtype)
            dag_ref[r0:r0 + rc, D_CONV:] = (acc * a_c * sg_c * (1.0 - sg_c)).astype(dag_ref.dtype)

    row = lambda w: pl.BlockSpec((tm, w), lambda i: (i, 0))
    full = lambda a, b: pl.BlockSpec((a, b), lambda i: (0, 0))
    nxt = pl.BlockSpec((CONV_HALO, D_CONV), lambda i: (jnp.minimum((i + 1) * hb, n_halo_blocks - 1), 0))
    return _pallas_call(
        body, name=name, grid=(I,),
        in_specs=[row(D_CONV), row(D_CONV), nxt, nxt, row(2 * D_CONV),
                  pl.BlockSpec((CONV_HALO, 2 * D_CONV), lambda i: (jnp.maximum(i * hb - 1, 0), 0)),
                  full(CONV_HALO, D_CONV), full(1, D_CONV), full(1, D_CONV)],
        out_specs=[row(2 * D_CONV), full(CONV_HALO, D_CONV), full(1, D_CONV), full(1, D_CONV), full(1, D_CONV)],
        out_shape=[
            jax.ShapeDtypeStruct((T, 2 * D_CONV), MXU),
            jax.ShapeDtypeStruct((CONV_HALO, D_CONV), F32),
            jax.ShapeDtypeStruct((1, D_CONV), F32),
            jax.ShapeDtypeStruct((1, D_CONV), F32),
            jax.ShapeDtypeStruct((1, D_CONV), F32),
        ],
        scratch_shapes=[pltpu.VMEM((CONV_HALO + tm, D_CONV), F32), pltpu.VMEM((tm + CONV_HALO, D_CONV), F32),
                        pltpu.VMEM((_SUBLANES, CONV_HALO + tm, D_CONV), F32),
                        pltpu.VMEM((_SUBLANES, CONV_HALO + tm, D_CONV), F32)],
        compiler_params=_params(40, 1),
    )(dc, yc, dc, yc, ag, ag, w32, lg, lb)


def _outproj_fwd(x1, c, o, gc, ga, wout, name):
    T, D = x1.shape
    tm = _tile(T, 512)

    def body(x_ref, c_ref, o_ref, gc_ref, ga_ref, w_ref, x2_ref):
        yc, _ = _rms_fwd(c_ref[...], gc_ref[...])
        ya, _ = _rms_fwd(o_ref[...], ga_ref[...])
        x2_ref[...] = (x_ref[...] + _mm(yc.astype(MXU), w_ref[:D_CONV, :])
                       + _mm(ya.astype(MXU), w_ref[D_CONV:, :]))

    row = lambda w: pl.BlockSpec((tm, w), lambda i: (i, 0))
    full = lambda a, b: pl.BlockSpec((a, b), lambda i: (0, 0))
    return _pallas_call(
        body, name=name, grid=(T // tm,),
        in_specs=[row(D), row(D_CONV), row(D_ATTN), full(1, D_CONV), full(1, D_ATTN), full(D_CONV + D_ATTN, D)],
        out_specs=row(D),
        out_shape=jax.ShapeDtypeStruct((T, D), F32),
        compiler_params=_params(32, 1),
    )(x1, c, o, gc, ga, wout)


def _outproj_bwd(dx2, c, o, gc, ga, wout, name):
    T, D = dx2.shape
    tm = _tile(T, 256)
    I = T // tm

    def body(dx_ref, c_ref, o_ref, gc_ref, ga_ref, w_ref,
             dc_ref, doT_ref, dlT_ref, dw_ref, dgc_ref, dga_ref, acc_s):
        i = pl.program_id(0)
        dxb = dx_ref[...].astype(MXU)
        cv = c_ref[...]
        ov = o_ref[...]
        yc, rcn = _rms_fwd(cv, gc_ref[...])
        ya, ra = _rms_fwd(ov, ga_ref[...])
        dyc = _mm_nt(dxb, w_ref[:D_CONV, :])
        dya = _mm_nt(dxb, w_ref[D_CONV:, :])
        dwc = _mm_tn(yc.astype(MXU), dxb)
        dwa = _mm_tn(ya.astype(MXU), dxb)
        dcv, dgc = _rms_bwd(cv, rcn, gc_ref[...], dyc)
        dov, dga = _rms_bwd(ov, ra, ga_ref[...], dya)
        dc_ref[...] = dcv
        dob = dov.astype(doT_ref.dtype)
        doT_ref[...] = dov.T.astype(doT_ref.dtype)
        chan = lax.broadcasted_iota(jnp.int32, (D_ATTN, LANES), 0)
        head = lax.broadcasted_iota(jnp.int32, (D_ATTN, LANES), 1)
        in_head = ((chan >= head * HEAD_DIM) & (chan < (head + 1) * HEAD_DIM)).astype(jnp.bfloat16)
        dlT_ref[...] = _exact_dot_01(dob.astype(F32) * ov, in_head).T[:N_HEADS, :]

        @pl.when(i == 0)
        def _():
            acc_s[:D_CONV, :] = dwc
            acc_s[D_CONV:, :] = dwa
            dgc_ref[...] = dgc
            dga_ref[...] = dga

        @pl.when(i > 0)
        def _():
            acc_s[:D_CONV, :] += dwc
            acc_s[D_CONV:, :] += dwa
            dgc_ref[...] += dgc
            dga_ref[...] += dga

        @pl.when(i == I - 1)
        def _():
            dw_ref[...] = acc_s[...].astype(dw_ref.dtype)

    row = lambda w: pl.BlockSpec((tm, w), lambda i: (i, 0))
    full = lambda a, b: pl.BlockSpec((a, b), lambda i: (0, 0))
    return _pallas_call(
        body, name=name, grid=(I,),
        in_specs=[row(D), row(D_CONV), row(D_ATTN), full(1, D_CONV), full(1, D_ATTN), full(D_CONV + D_ATTN, D)],
        out_specs=[row(D_CONV), pl.BlockSpec((D_ATTN, tm), lambda i: (0, i)),
                   pl.BlockSpec((N_HEADS, tm), lambda i: (0, i)),
                   full(D_CONV + D_ATTN, D), full(1, D_CONV), full(1, D_ATTN)],
        out_shape=[
            jax.ShapeDtypeStruct((T, D_CONV), F32),
            jax.ShapeDtypeStruct((D_ATTN, T), MXU),
            jax.ShapeDtypeStruct((N_HEADS, T), F32),
            jax.ShapeDtypeStruct((D_CONV + D_ATTN, D), MXU),
            jax.ShapeDtypeStruct((1, D_CONV), F32),
            jax.ShapeDtypeStruct((1, D_ATTN), F32),
        ],
        scratch_shapes=[pltpu.VMEM((D_CONV + D_ATTN, D), F32)],
        compiler_params=_params(40, 1),
    )(dx2, c, o, gc, ga, wout)


def _ffn_down_loss(x, A, w2, gf, target, name):
    T, D = x.shape
    J, _, bf = A.shape
    tm = _tile(T, 512)

    def body(x_ref, A_ref, w2_ref, g_ref, t_ref, loss_ref, dx_ref, dg_ref):
        i = pl.program_id(0)
        f = _mm(A_ref[0], w2_ref[0:bf, :])
        for j in range(1, J):
            f = f + _mm(A_ref[j], w2_ref[j * bf:(j + 1) * bf, :])
        xv = x_ref[...] + 0.5 * f
        gv = g_ref[...]
        out, r = _rms_fwd(xv, gv)
        err = out - t_ref[...]
        part = jnp.full((1, LANES), 0.5 / D, F32) * jnp.sum(err * err)
        dxn, dgp = _rms_bwd(xv, r, gv, err * (1.0 / D))
        dx_ref[...] = dxn

        @pl.when(i == 0)
        def _():
            loss_ref[...] = part
            dg_ref[...] = dgp

        @pl.when(i > 0)
        def _():
            loss_ref[...] += part
            dg_ref[...] += dgp

    row = lambda w: pl.BlockSpec((tm, w), lambda i: (i, 0))
    full = lambda a, b: pl.BlockSpec((a, b), lambda i: (0, 0))
    return _pallas_call(
        body, name=name, grid=(T // tm,),
        in_specs=[row(D), pl.BlockSpec((J, tm, bf), lambda i: (0, i, 0)), full(J * bf, D), full(1, D), row(D)],
        out_specs=[full(1, LANES), row(D), full(1, D)],
        out_shape=[jax.ShapeDtypeStruct((1, LANES), F32), jax.ShapeDtypeStruct((T, D), F32),
                   jax.ShapeDtypeStruct((1, D), F32)],
        compiler_params=_params(56, 1),
    )(x, A, w2, gf, target)


def _row_tile(rows):
    for cand in (256, 176, 128, 64, 32, 16):
        if rows % cand == 0:
            return cand
    return rows


def _adamw(w, m, v, parts, name):
    R, C = w.shape
    P = parts.shape[0]
    tr = _row_tile(R)
    c1 = 1.0 - ADAM_B1 ** ADAM_STEP
    c2 = 1.0 - ADAM_B2 ** ADAM_STEP

    def body(w_ref, m_ref, v_ref, p_ref, g_ref, d_ref, nm_ref, nv_ref):
        g = p_ref[0].astype(F32)
        for s in range(1, P):
            g = g + p_ref[s].astype(F32)
        wv = w_ref[...]
        mn = ADAM_B1 * m_ref[...] + (1.0 - ADAM_B1) * g
        vn = ADAM_B2 * v_ref[...] + (1.0 - ADAM_B2) * (g * g)
        g_ref[...] = g
        nm_ref[...] = mn
        nv_ref[...] = vn
        d_ref[...] = -ADAM_LR * ((mn / c1) / (jnp.sqrt(vn / c2) + ADAM_EPS) + ADAM_WD * wv)

    blk = pl.BlockSpec((tr, C), lambda i: (i, 0))
    out = jax.ShapeDtypeStruct((R, C), F32)
    return _pallas_call(
        body, name=name, grid=(R // tr,),
        in_specs=[blk, blk, blk, pl.BlockSpec((P, tr, C), lambda i: (0, i, 0))],
        out_specs=[blk, blk, blk, blk],
        out_shape=[out, out, out, out],
        compiler_params=_params(32, 1),
    )(w, m, v, parts)


def _position():
    return lax.axis_index("x"), lax.axis_index("y"), lax.axis_index("c")


def _flat(px, py, pc):
    return 4 * px + 2 * py + pc


def _gather_body(ins, outs, send_sems, recv_sems, local_sems, handshake):
    n = len(ins)
    x, y, c = _position()
    me, sibling = (x, y, c), (x, y, 1 - c)
    chips = [(1 - x, y), (x, 1 - y), (1 - x, 1 - y)]
    if handshake:
        _handshake([sibling] + [(*chip, cc) for chip in chips for cc in (c, 1 - c)])

    def copy(a, k, block, to, src=None):
        dst = outs[a].at[_flat(*block)]
        return pltpu.make_async_remote_copy(
            src_ref=dst if src is None else src, dst_ref=dst,
            send_sem=send_sems.at[a, k], recv_sem=recv_sems.at[a, k],
            device_id=to, device_id_type=MESH)

    mine = [pltpu.make_async_copy(ins[a], outs[a].at[_flat(*me)], local_sems.at[a]) for a in range(n)]
    for cp in mine:
        cp.start()
    first = []
    for a in range(n):
        first.append(copy(a, 0, me, sibling, src=ins[a]))
        first += [copy(a, 1 + j, me, (*chip, c), src=ins[a]) for j, chip in enumerate(chips)]
    for cp in first:
        cp.start()
    passed = []
    for a in range(n):
        for j, chip in enumerate(chips):
            copy(a, 1 + j, (*chip, c), me).wait_recv()
            fwd = copy(a, 4 + j, (*chip, c), sibling)
            fwd.start()
            passed.append(fwd)
    for a in range(n):
        copy(a, 0, sibling, me).wait_recv()
        for j, chip in enumerate(chips):
            copy(a, 4 + j, (*chip, 1 - c), me).wait_recv()
    for cp in first + passed:
        cp.wait_send()
    for cp in mine:
        cp.wait()


def _gather_scratch(n):
    return [pltpu.SemaphoreType.DMA((n, 7)), pltpu.SemaphoreType.DMA((n, 7)), pltpu.SemaphoreType.DMA((n,))]


def _all_gather(shards, name):
    n = len(shards)

    def body(*refs):
        _gather_body(refs[:n], refs[n:2 * n], *refs[2 * n:], handshake=False)

    hbm = pl.BlockSpec(memory_space=pltpu.HBM)
    return _pallas_call(
        body, name=name,
        in_specs=[hbm] * n, out_specs=[hbm] * n,
        out_shape=[jax.ShapeDtypeStruct((N_DEV,) + s.shape, s.dtype) for s in shards],
        scratch_shapes=_gather_scratch(n),
    )(*shards)


def _handshake(peers):
    barrier = pltpu.get_barrier_semaphore()
    for peer in peers:
        pl.semaphore_signal(barrier, inc=1, device_id=peer, device_id_type=MESH)
    pl.semaphore_wait(barrier, len(peers))


def _sequencer_call(body, name, collective_id, out_type, scratch_types, operands):
    return pl.kernel(
        body, name=name, out_type=out_type,
        mesh=plsc.ScalarSubcoreMesh(axis_name="sequencer", num_cores=1),
        scratch_types=scratch_types,
        compiler_params=pltpu.CompilerParams(collective_id=collective_id),
    )(*operands)


def _seq_all_gather(shards, name, collective_id, after):
    n = len(shards)

    def body(*refs):
        _gather_body(refs[:n], refs[n + 1:2 * n + 1], *refs[2 * n + 1:], handshake=True)

    return _sequencer_call(
        body, name, collective_id,
        [jax.ShapeDtypeStruct((N_DEV,) + s.shape, s.dtype) for s in shards],
        _gather_scratch(n), list(shards) + [after])


def _seq_to_sibling(parts, name, collective_id, after):
    n = len(parts)

    def body(*refs):
        ins, outs = refs[:n], refs[n + len(after):2 * n + len(after)]
        send_sems, recv_sems = refs[2 * n + len(after):]
        x, y, c = _position()
        sibling = (x, y, 1 - c)
        _handshake([sibling])
        sent = []
        for a in range(n):
            for q in range(N_CHIPS):
                cp = pltpu.make_async_remote_copy(
                    src_ref=ins[a].at[2 * q + (1 - c)], dst_ref=outs[a].at[q],
                    send_sem=send_sems.at[a, q], recv_sem=recv_sems.at[a, q],
                    device_id=sibling, device_id_type=MESH)
                cp.start()
                sent.append(cp)
        for cp in sent:
            cp.wait_recv()
        for cp in sent:
            cp.wait_send()

    return _sequencer_call(
        body, name, collective_id,
        [jax.ShapeDtypeStruct((N_CHIPS,) + p.shape[1:], p.dtype) for p in parts],
        [pltpu.SemaphoreType.DMA((n, N_CHIPS)), pltpu.SemaphoreType.DMA((n, N_CHIPS))],
        list(parts) + list(after))


def _seq_to_chips(partials, name, collective_id, after=()):
    n = len(partials)

    def body(*refs):
        ins, outs = refs[:n], refs[n + len(after):2 * n + len(after)]
        send_sems, recv_sems, local_sems = refs[2 * n + len(after):]
        x, y, c = _position()
        my_chip = 2 * x + y
        chips = [(1 - x, y), (x, 1 - y), (1 - x, 1 - y)]
        _handshake([(*chip, c) for chip in chips])
        mine = [pltpu.make_async_copy(ins[a].at[my_chip], outs[a].at[my_chip], local_sems.at[a]) for a in range(n)]
        for cp in mine:
            cp.start()
        sent = []
        for a in range(n):
            for j, (px, py) in enumerate(chips):
                cp = pltpu.make_async_remote_copy(
                    src_ref=ins[a].at[2 * px + py], dst_ref=outs[a].at[my_chip],
                    send_sem=send_sems.at[a, j], recv_sem=recv_sems.at[a, j],
                    device_id=(px, py, c), device_id_type=MESH)
                cp.start()
                sent.append(cp)
        for cp in sent:
            cp.wait_recv()
        for cp in sent:
            cp.wait_send()
        for cp in mine:
            cp.wait()

    return _sequencer_call(
        body, name, collective_id,
        [jax.ShapeDtypeStruct(p.shape, p.dtype) for p in partials],
        [pltpu.SemaphoreType.DMA((n, 3)), pltpu.SemaphoreType.DMA((n, 3)), pltpu.SemaphoreType.DMA((n,))],
        list(partials) + list(after))


def _pair_add(part, recv, name, after=()):
    _, R, C = part.shape
    core = lax.axis_index("c").astype(jnp.int32).reshape(1)

    def body(c_ref, p_ref, r_ref, *rest):
        o_ref = rest[len(after)]
        o_ref[...] = (p_ref[...].astype(F32) + r_ref[...].astype(F32)).astype(o_ref.dtype)

    blk = pl.BlockSpec((None, R, C), lambda q, c: (q, 0, 0))
    return pl.pallas_call(
        body, name=name,
        grid_spec=pltpu.PrefetchScalarGridSpec(
            num_scalar_prefetch=1, grid=(N_CHIPS,),
            in_specs=[pl.BlockSpec((None, R, C), lambda q, c: (2 * q + c[0], 0, 0)), blk] + [_UNREAD] * len(after),
            out_specs=blk),
        out_shape=pltpu.HBM((N_CHIPS, R, C), part.dtype),
        compiler_params=_params(32, 1),
    )(core, *[pltpu.with_memory_space_constraint(a, pltpu.HBM) for a in (part, recv, *after)])


class _Reduced(NamedTuple):
    partials: list
    reduced: list


def _blocks(g):
    return g.reshape(N_DEV, -1, g.shape[-1])


def _reduce_scatter(parts, tag, ids, after=(), between=None, add_after=()):
    from_sibling = _seq_to_sibling(parts, "rs_sibling_" + tag, ids[0], after)
    mid = between(from_sibling[0]) if between else ()
    partials = [_pair_add(p, r, "rs_add_%s_%d" % (tag, a), add_after)
                for a, (p, r) in enumerate(zip(parts, from_sibling))]
    return _Reduced(partials, _seq_to_chips(partials, "rs_chips_" + tag, ids[1], mid))


_SMALL = ("ffn1_norm", "mix_norm", "conv_b", "conv_ln_g", "conv_ln_b", "forget_b", "out_norm_conv",
          "out_norm_attn", "ffn2_norm", "final_norm")
_PACK_WIDTH = 2 * D_CONV
_SLOT = dict(ffn1_norm=(0, 0), mix_norm=(1, 0), ffn2_norm=(2, 0), final_norm=(3, 0), conv_b=(4, 0),
             conv_ln_g=(4, D_CONV), conv_ln_b=(5, 0), out_norm_conv=(5, D_CONV), out_norm_attn=(6, 0),
             forget_b=(6, D_CONV))
_CONV_ROW0 = 8
_PACK_ROWS = _CONV_ROW0 + CONV_HALO


def _pack_small(small, name):
    arrays = [small[n] for n in _SMALL] + [small["conv_w"]]

    def body(*refs):
        out = refs[-1]
        out[...] = jnp.zeros_like(out)
        for n, ref in zip(_SMALL, refs):
            row, lane = _SLOT[n]
            out[row:row + 1, lane:lane + ref.shape[1]] = ref[...]
        out[_CONV_ROW0:, :D_CONV] = refs[len(_SMALL)][...]

    return _pallas_call(body, name=name, out_shape=jax.ShapeDtypeStruct((_PACK_ROWS, _PACK_WIDTH), F32))(*arrays)


def _adamw_small(gathered, w, m, v, name):
    c1 = 1.0 - ADAM_B1 ** ADAM_STEP
    c2 = 1.0 - ADAM_B2 ** ADAM_STEP
    k = len(_SMALL)

    def body(g_ref, *refs):
        ws, ms, vs = refs[:k], refs[k:2 * k], refs[2 * k:3 * k]
        outs = refs[3 * k:]
        total = g_ref[0]
        for s in range(1, N_DEV):
            total = total + g_ref[s]
        for i, n in enumerate(_SMALL):
            row, lane = _SLOT[n]
            width = ws[i].shape[1]
            g = total[row:row + 1, lane:lane + width]
            mn = ADAM_B1 * ms[i][...] + (1.0 - ADAM_B1) * g
            vn = ADAM_B2 * vs[i][...] + (1.0 - ADAM_B2) * (g * g)
            o_g, o_d, o_m, o_v = outs[4 * i:4 * i + 4]
            o_g[...] = g
            o_m[...] = mn
            o_v[...] = vn
            o_d[...] = -ADAM_LR * ((mn / c1) / (jnp.sqrt(vn / c2) + ADAM_EPS) + ADAM_WD * ws[i][...])
        outs[4 * k][...] = total[_CONV_ROW0:, :D_CONV]

    shapes = []
    for n in _SMALL:
        shapes += [jax.ShapeDtypeStruct(w[n].shape, F32)] * 4
    shapes.append(jax.ShapeDtypeStruct((CONV_HALO, D_CONV), F32))
    res = _pallas_call(body, name=name, out_shape=shapes)(
        gathered, *[w[n] for n in _SMALL], *[m[n] for n in _SMALL], *[v[n] for n in _SMALL])
    return {n: res[4 * i:4 * i + 4] for i, n in enumerate(_SMALL)}, res[4 * k]


def _local_step(x, target, norms, shard):
    D = x.shape[1]
    J = N_DEV // 2
    as13 = lambda g: g.reshape(2, J, g.shape[1], D)

    (g13_1,) = _all_gather([shard["ffn1_w13"]], "gather_ffn1_w13")
    (g2_1,) = _seq_all_gather([shard["ffn1_w2"]], "gather_ffn1_w2", 10, after=g13_1)
    w13_1 = as13(g13_1)
    G1, U1, A1 = _ffn_up(x, norms["ffn1_norm"], w13_1, "ffn1_up")
    gin, gconv = _seq_all_gather([shard["w_in"], shard["conv_w"]], "gather_mix", 1, after=G1)
    w2_1 = g2_1.reshape(-1, D)
    x1 = _ffn_down(x, A1, w2_1, "ffn1_down")
    gout, g13_2, g2_2 = _seq_all_gather([shard["w_out"], shard["ffn2_w13"], shard["ffn2_w2"]], "gather_ffn2", 2,
                                        after=x1)
    winp = jnp.pad(gin.reshape(N_IN, D), ((0, N_IN_PAD - N_IN), (0, 0)))
    wout = gout.reshape(-1, D)
    conv_w32 = jnp.pad(gconv.transpose(1, 0, 2).reshape(CONV_TAPS, D_CONV), ((0, CONV_HALO - CONV_TAPS), (0, 0)))

    ag, k, v, qT, kT, vT, fl = _inproj_fwd(x1, norms["mix_norm"], winp, "inproj_fwd")
    cum, cumT = _forget_fwd(fl, norms["forget_b"], "forget_fwd")
    yc, c = _conv_fwd(ag, conv_w32, norms["conv_b"], norms["conv_ln_g"], norms["conv_ln_b"], "conv_fwd")
    o, lseT = _attn_fwd(qT, k, vT, cum, cumT, "attn_fwd")
    x2 = _outproj_fwd(x1, c, o, norms["out_norm_conv"], norms["out_norm_attn"], wout, "outproj_fwd")
    w13_2, w2_2 = as13(g13_2), g2_2.reshape(-1, D)
    G2, U2, A2 = _ffn_up(x2, norms["ffn2_norm"], w13_2, "ffn2_up")
    loss, dx3, d_final = _ffn_down_loss(x2, A2, w2_2, norms["final_norm"], target, "ffn2_down_loss")

    dw2_2 = _ffn_w2_grad(dx3, A2, "ffn2_w2_grad")
    dx2, d_ffn2n, h3, dG2, dU2 = _ffn_bwd_act(x2, norms["ffn2_norm"], dx3, G2, U2, w13_2, w2_2, "ffn2_bwd_act")
    dw13_2 = _ffn_w13_grad(h3, dG2, dU2, "ffn2_w13_grad")
    dc, dobT, deltaT, dwout, d_onc, d_ona = _outproj_bwd(
        dx2, c, o, norms["out_norm_conv"], norms["out_norm_attn"], wout, "outproj_bwd")
    red_ffn2 = _reduce_scatter([_blocks(dw13_2), _blocks(dw2_2)], "ffn2", (3, 4), add_after=(dc,))
    dqT, dkT, dvT, dcum = _attn_bwd(qT, k, kT, v, dobT, lseT, deltaT, cum, cumT, "attn_bwd",
                                    after=red_ffn2.partials)
    dfl, d_fb = _forget_bwd(dcum, fl, norms["forget_b"], "forget_bwd")
    dag, d_convw, d_cb, d_lg, d_lb = _conv_bwd(dc, yc, ag, conv_w32, norms["conv_ln_g"], norms["conv_ln_b"], "conv_bwd")
    dx1, d_mixn, h2 = _inproj_bwd_act(x1, norms["mix_norm"], dx2, dag, dqT, dkT, dvT, dfl, winp, "inproj_bwd_act")
    dw2_1 = _ffn_w2_grad(dx1, A1, "ffn1_w2_grad")
    dwinp = _inproj_bwd_weights(h2, dag, dqT, dkT, dvT, dfl, "inproj_bwd_weights")
    dwin_blocks = dwinp[:N_IN].reshape(N_DEV, N_IN // N_DEV, -1)
    red_mix = _reduce_scatter([dwin_blocks, _blocks(dwout), _blocks(dw2_1)], "mix", (5, 6),
                              after=red_ffn2.reduced[:1])
    dx, d_ffn1n, h1, dG1, dU1 = _ffn_bwd_act(x, norms["ffn1_norm"], dx1, G1, U1, w13_1, w2_1, "ffn1_bwd_act",
                                             after=red_mix.partials)
    dw13_1 = _ffn_w13_grad(h1, dG1, dU1, "ffn1_w13_grad")

    small = dict(ffn1_norm=d_ffn1n, mix_norm=d_mixn, conv_b=d_cb, conv_ln_g=d_lg, conv_ln_b=d_lb,
                 forget_b=d_fb, out_norm_conv=d_onc, out_norm_attn=d_ona, ffn2_norm=d_ffn2n,
                 final_norm=d_final, conv_w=d_convw)
    packed_small = _pack_small(small, "pack_small_grads")
    gathered_small = []

    def gather_small(behind):
        gathered_small.extend(_seq_all_gather([packed_small], "gather_small_grads", 9, after=behind))
        return gathered_small

    red_w13_1 = _reduce_scatter([_blocks(dw13_1)], "ffn1_w13", (7, 8), after=red_mix.reduced[:1],
                                between=gather_small)
    big = dict(ffn1_w13=red_w13_1.reduced[0], ffn1_w2=red_mix.reduced[2], w_in=red_mix.reduced[0],
               w_out=red_mix.reduced[1], ffn2_w13=red_ffn2.reduced[0], ffn2_w2=red_ffn2.reduced[1])
    return loss[0, 0], dx, gathered_small[0], big


_BIG = ("ffn1_w13", "ffn1_w2", "w_in", "w_out", "ffn2_w13", "ffn2_w2")
_TRANSPOSED = ("ffn1_w13", "ffn2_w13", "w_in")
_ORDER = ("ffn1_norm", "ffn1_w13", "ffn1_w2", "mix_norm", "w_in", "conv_w", "conv_b", "conv_ln_g", "conv_ln_b",
          "forget_b", "out_norm_conv", "out_norm_attn", "w_out", "ffn2_norm", "ffn2_w13", "ffn2_w2", "final_norm")


def kernel(x, ffn1_norm, ffn1_w13, ffn1_w2, mix_norm, w_in, conv_w, conv_b, conv_ln_g, conv_ln_b, forget_b, out_norm_conv, out_norm_attn, w_out, ffn2_norm, ffn2_w13, ffn2_w2, final_norm, loss_target, m_ffn1_norm, m_ffn1_w13, m_ffn1_w2, m_mix_norm, m_w_in, m_conv_w, m_conv_b, m_conv_ln_g, m_conv_ln_b, m_forget_b, m_out_norm_conv, m_out_norm_attn, m_w_out, m_ffn2_norm, m_ffn2_w13, m_ffn2_w2, m_final_norm, v_ffn1_norm, v_ffn1_w13, v_ffn1_w2, v_mix_norm, v_w_in, v_conv_w, v_conv_b, v_conv_ln_g, v_conv_ln_b, v_forget_b, v_out_norm_conv, v_out_norm_attn, v_w_out, v_ffn2_norm, v_ffn2_w13, v_ffn2_w2, v_final_norm):
    w = dict(ffn1_norm=ffn1_norm, ffn1_w13=ffn1_w13, ffn1_w2=ffn1_w2, mix_norm=mix_norm, w_in=w_in, conv_w=conv_w,
             conv_b=conv_b, conv_ln_g=conv_ln_g, conv_ln_b=conv_ln_b, forget_b=forget_b, out_norm_conv=out_norm_conv,
             out_norm_attn=out_norm_attn, w_out=w_out, ffn2_norm=ffn2_norm, ffn2_w13=ffn2_w13, ffn2_w2=ffn2_w2,
             final_norm=final_norm)
    m = dict(ffn1_norm=m_ffn1_norm, ffn1_w13=m_ffn1_w13, ffn1_w2=m_ffn1_w2, mix_norm=m_mix_norm, w_in=m_w_in,
             conv_w=m_conv_w, conv_b=m_conv_b, conv_ln_g=m_conv_ln_g, conv_ln_b=m_conv_ln_b, forget_b=m_forget_b,
             out_norm_conv=m_out_norm_conv, out_norm_attn=m_out_norm_attn, w_out=m_w_out, ffn2_norm=m_ffn2_norm,
             ffn2_w13=m_ffn2_w13, ffn2_w2=m_ffn2_w2, final_norm=m_final_norm)
    v = dict(ffn1_norm=v_ffn1_norm, ffn1_w13=v_ffn1_w13, ffn1_w2=v_ffn1_w2, mix_norm=v_mix_norm, w_in=v_w_in,
             conv_w=v_conv_w, conv_b=v_conv_b, conv_ln_g=v_conv_ln_g, conv_ln_b=v_conv_ln_b, forget_b=v_forget_b,
             out_norm_conv=v_out_norm_conv, out_norm_attn=v_out_norm_attn, w_out=v_w_out, ffn2_norm=v_ffn2_norm,
             ffn2_w13=v_ffn2_w13, ffn2_w2=v_ffn2_w2, final_norm=v_final_norm)
    shapes = {n: a.shape for n, a in w.items()}
    T, D = x.shape[1], x.shape[2]
    def two(n, a):
        if a.ndim != 3:
            return a.reshape(1, -1)
        a = a.reshape(a.shape[-2], a.shape[-1])
        return a.T if n in _TRANSPOSED else a

    w2d = {n: two(n, a) for n, a in w.items()}
    m2d = {n: two(n, a) for n, a in m.items()}
    v2d = {n: two(n, a) for n, a in v.items()}

    shard = {n: w2d[n].astype(MXU) for n in _BIG}
    shard["conv_w"] = w2d["conv_w"]
    norms = {n: w2d[n] for n in _SMALL}
    norms["forget_b"] = jnp.pad(w2d["forget_b"], ((0, 0), (0, LANES - N_HEADS)))
    loss_part, dx, gathered_small, big = _local_step(x[0], loss_target[0], norms, shard)
    loss = lax.psum(loss_part, ("x", "y", "c"))

    grads, deltas, new_m, new_v = {}, {}, {}, {}
    for n in _BIG:
        g, d, nm, nv = _adamw(w2d[n], m2d[n], v2d[n], big[n], "adamw_" + n)
        grads[n], deltas[n], new_m[n], new_v[n] = g, d, nm, nv

    small_out, conv_g_full = _adamw_small(gathered_small, w2d, m2d, v2d, "adamw_small")
    for n in _SMALL:
        grads[n], deltas[n], new_m[n], new_v[n] = small_out[n]
    conv_g_full = conv_g_full[:CONV_TAPS]
    xi, yi, ci = _position()
    cw = shapes["conv_w"][-1]
    conv_g_mine = lax.dynamic_slice_in_dim(conv_g_full, _flat(xi, yi, ci) * cw, cw, axis=1)
    g, d, nm, nv = _adamw(w2d["conv_w"], m2d["conv_w"], v2d["conv_w"], conv_g_mine[None], "adamw_conv_w")
    grads["conv_w"], deltas["conv_w"], new_m["conv_w"], new_v["conv_w"] = g, d, nm, nv

    shaped = lambda dct: [(dct[n].T if n in _TRANSPOSED else dct[n]).reshape(shapes[n]) for n in _ORDER]
    return (loss, dx[None], *shaped(grads), *shaped(deltas), *shaped(new_m), *shaped(new_v))
```

```python
from typing import NamedTuple

import jax
import jax.numpy as jnp
from jax import lax
from jax.experimental import pallas as pl
from jax.experimental.pallas import tpu as pltpu
from jax.experimental.pallas import tpu_sc as plsc

F32 = jnp.float32
MXU = jnp.bfloat16
EPS = 1e-6
N_HEADS = 8
HEAD_DIM = 64
D_CONV = 512
D_ATTN = N_HEADS * HEAD_DIM
CONV_TAPS = 31
CONV_HALO = 32
SCALE = HEAD_DIM ** -0.5
NEG = -1e30
LANES = 128
N_DEV = 8
N_CHIPS = N_DEV // 2
MESH = pl.DeviceIdType.MESH
MIB = 1 << 20

ADAM_LR = 0.001
ADAM_B1 = 0.9
ADAM_B2 = 0.999
ADAM_EPS = 1e-08
ADAM_WD = 0.01
ADAM_STEP = 10


_UNREAD = pl.BlockSpec(memory_space=pl.ANY)


def _pallas_call(body, *, out_shape, **kwargs):
    in_hbm = lambda s: pltpu.HBM(s.shape, s.dtype)
    outs = [in_hbm(s) for s in out_shape] if isinstance(out_shape, (list, tuple)) else in_hbm(out_shape)
    call = pl.pallas_call(body, out_shape=outs, **kwargs)
    return lambda *operands: call(*[pltpu.with_memory_space_constraint(a, pltpu.HBM) for a in operands])


def _params(vmem_mib, n_axes):
    return pltpu.CompilerParams(dimension_semantics=("arbitrary",) * n_axes, vmem_limit_bytes=vmem_mib * MIB)


def _mm(a, b):
    return jnp.dot(a, b, preferred_element_type=F32)


def _mm_nt(a, b):
    return lax.dot_general(a, b, (((1,), (1,)), ((), ())), preferred_element_type=F32)


def _mm_tn(a, b):
    return lax.dot_general(a, b, (((0,), (0,)), ((), ())), preferred_element_type=F32)


def _rms_fwd(x, g):
    r = lax.rsqrt(jnp.mean(x * x, axis=-1, keepdims=True) + EPS)
    return x * r * g, r


def _rms_bwd(x, r, g, dy):
    gdy = dy * g
    dx = r * gdy - x * (r * r * r) * jnp.mean(x * gdy, axis=-1, keepdims=True)
    dg = jnp.sum(dy * x * r, axis=0, keepdims=True)
    return dx, dg


def _silu_grad(z, sz):
    return sz * (1.0 + z * (1.0 - sz))


def _three_terms(x):
    x1 = x.astype(jnp.bfloat16)
    r1 = x - x1.astype(F32)
    x2 = r1.astype(jnp.bfloat16)
    x3 = (r1 - x2.astype(F32)).astype(jnp.bfloat16)
    return x1, x2, x3


def _exact_tri_dot(tri, x):
    x1, x2, x3 = _three_terms(x)
    return _mm(tri, x1) + _mm(tri, x2) + _mm(tri, x3)


def _exact_dot_01(x, sel):
    x1, x2, x3 = _three_terms(x)
    return _mm(x1, sel) + _mm(x2, sel) + _mm(x3, sel)


def _tile(n, want):
    t = min(n, want)
    assert n % t == 0
    return t


_FFN_CHUNK = 256


def _ffn_up(x, g, w13, name):
    T, D = x.shape
    _, J, bf, _ = w13.shape
    tm = _tile(T, 512)
    I = T // tm

    def body(x_ref, g_ref, w13_ref, G_ref, U_ref, A_ref, h_s):
        j = pl.program_id(0)
        i = pl.program_id(1)
        rows = pl.ds(pl.multiple_of(i * tm, tm), tm)

        @pl.when(j == 0)
        def _():
            h, _ = _rms_fwd(x_ref[...], g_ref[...])
            h_s[rows, :] = h.astype(MXU)

        chunks = [slice(r0, r0 + _FFN_CHUNK) for r0 in range(0, tm, _FFN_CHUNK)]
        hbs = [h_s[pl.ds(pl.multiple_of(i * tm + rs.start, _FFN_CHUNK), _FFN_CHUNK), :] for rs in chunks]
        GU = [(_mm_nt(hb, w13_ref[0]), _mm_nt(hb, w13_ref[1])) for hb in hbs]
        for rs, (G, U) in zip(chunks, GU):
            G_ref[rs, :] = G.astype(MXU)
            U_ref[rs, :] = U.astype(MXU)
            A_ref[rs, :] = (G * jax.nn.sigmoid(G) * U).astype(MXU)

    blk = pl.BlockSpec((None, tm, bf), lambda j, i: (j, i, 0))
    hid = jax.ShapeDtypeStruct((J, T, bf), MXU)
    return _pallas_call(
        body, name=name, grid=(J, I),
        in_specs=[pl.BlockSpec((tm, D), lambda j, i: (jnp.where(j == 0, i, I - 1), 0)),
                  pl.BlockSpec((1, D), lambda j, i: (0, 0)),
                  pl.BlockSpec((2, None, bf, D), lambda j, i: (0, j, 0, 0))],
        out_specs=[blk, blk, blk],
        out_shape=[hid, hid, hid],
        scratch_shapes=[pltpu.VMEM((T, D), MXU)],
        compiler_params=_params(40, 2),
    )(x, g, w13)


def _ffn_down(x, A, w2, name):
    T, D = x.shape
    J, _, bf = A.shape
    tm = _tile(T, 512)

    def body(x_ref, A_ref, w2_ref, xo_ref):
        f = _mm(A_ref[0], w2_ref[0:bf, :])
        for j in range(1, J):
            f = f + _mm(A_ref[j], w2_ref[j * bf:(j + 1) * bf, :])
        xo_ref[...] = x_ref[...] + 0.5 * f

    row = pl.BlockSpec((tm, D), lambda i: (i, 0))
    return _pallas_call(
        body, name=name, grid=(T // tm,),
        in_specs=[row, pl.BlockSpec((J, tm, bf), lambda i: (0, i, 0)), pl.BlockSpec((J * bf, D), lambda i: (0, 0))],
        out_specs=row,
        out_shape=jax.ShapeDtypeStruct((T, D), F32),
        compiler_params=_params(48, 1),
    )(x, A, w2)


def _ffn_bwd_act(x, g, dy, Gs, Us, w13, w2, name, after=()):
    T, D = x.shape
    _, J, bf, _ = w13.shape
    tm = _tile(T, 512)
    I = T // tm

    def body(x_ref, g_ref, dy_ref, G_ref, U_ref, w13_ref, w2_ref, *rest):
        dx_ref, dg_ref, h_ref, dG_ref, dU_ref, dh_s, dF_s, h_s = rest[len(after):]
        j = pl.program_id(0)
        i = pl.program_id(1)
        rows = pl.ds(pl.multiple_of(i * tm, tm), tm)

        @pl.when(j == 0)
        def _():
            h, _ = _rms_fwd(x_ref[...], g_ref[...])
            hb = h.astype(MXU)
            h_s[rows, :] = hb
            h_ref[...] = hb
            dF_s[rows, :] = (0.5 * dy_ref[...]).astype(MXU)
            dh_s[rows, :] = jnp.zeros((tm, D), F32)

        chunks = [slice(r0, r0 + _FFN_CHUNK) for r0 in range(0, tm, _FFN_CHUNK)]
        crows = [pl.ds(pl.multiple_of(i * tm + rs.start, _FFN_CHUNK), _FFN_CHUNK) for rs in chunks]
        dAs = [_mm_nt(dF_s[cr, :], w2_ref[...]) for cr in crows]
        for rs, cr, dA in zip(chunks, crows, dAs):
            G = G_ref[rs, :].astype(F32)
            U = U_ref[rs, :].astype(F32)
            sg = jax.nn.sigmoid(G)
            s = G * sg
            dUb = (dA * s).astype(MXU)
            dGb = (dA * U * _silu_grad(G, sg)).astype(MXU)
            dG_ref[rs, :] = dGb
            dU_ref[rs, :] = dUb
            dh_s[cr, :] += _mm(dGb, w13_ref[0]) + _mm(dUb, w13_ref[1])

        @pl.when(j == J - 1)
        def _():
            xv = x_ref[...]
            gv = g_ref[...]
            _, r = _rms_fwd(xv, gv)
            dxn, dgp = _rms_bwd(xv, r, gv, dh_s[rows, :])
            dx_ref[...] = dy_ref[...] + dxn

            @pl.when(i == 0)
            def _():
                dg_ref[...] = dgp

            @pl.when(i > 0)
            def _():
                dg_ref[...] += dgp

    ends = lambda j, i: (jnp.where((j == 0) | (j == J - 1), i, I - 1), 0)
    blk = pl.BlockSpec((None, tm, bf), lambda j, i: (j, i, 0))
    hid = jax.ShapeDtypeStruct((J, T, bf), MXU)
    return _pallas_call(
        body, name=name, grid=(J, I),
        in_specs=[pl.BlockSpec((tm, D), ends), pl.BlockSpec((1, D), lambda j, i: (0, 0)), pl.BlockSpec((tm, D), ends),
                  blk, blk, pl.BlockSpec((2, None, bf, D), lambda j, i: (0, j, 0, 0)),
                  pl.BlockSpec((bf, D), lambda j, i: (j, 0))] + [_UNREAD] * len(after),
        out_specs=[pl.BlockSpec((tm, D), lambda j, i: (jnp.where(j == J - 1, i, 0), 0)),
                   pl.BlockSpec((1, D), lambda j, i: (0, 0)),
                   pl.BlockSpec((tm, D), lambda j, i: (jnp.where(j == 0, i, I - 1), 0)), blk, blk],
        out_shape=[jax.ShapeDtypeStruct((T, D), F32), jax.ShapeDtypeStruct((1, D), F32),
                   jax.ShapeDtypeStruct((T, D), MXU), hid, hid],
        scratch_shapes=[pltpu.VMEM((T, D), F32), pltpu.VMEM((T, D), MXU), pltpu.VMEM((T, D), MXU)],
        compiler_params=_params(58, 2),
    )(x, g, dy, Gs, Us, w13, w2, *after)


def _ffn_w13_grad(h, dG, dU, name):
    T, D = h.shape
    J, _, bf = dG.shape

    def body(h_ref, dG_ref, dU_ref, dw13_ref):
        dw13_ref[0] = _mm_tn(dG_ref[...], h_ref[...]).astype(dw13_ref.dtype)
        dw13_ref[1] = _mm_tn(dU_ref[...], h_ref[...]).astype(dw13_ref.dtype)

    blk = pl.BlockSpec((None, T, bf), lambda j: (j, 0, 0))
    return _pallas_call(
        body, name=name, grid=(J,),
        in_specs=[pl.BlockSpec((T, D), lambda j: (0, 0)), blk, blk],
        out_specs=pl.BlockSpec((2, None, bf, D), lambda j: (0, j, 0, 0)),
        out_shape=jax.ShapeDtypeStruct((2, J, bf, D), MXU),
        compiler_params=_params(48, 1),
    )(h, dG, dU)


def _ffn_w2_grad(dy, A, name, after=()):
    T, D = dy.shape
    J, _, bf = A.shape

    def body(dy_ref, A_ref, *rest):
        dw2_ref, dF_s = rest[len(after):]

        @pl.when(pl.program_id(0) == 0)
        def _():
            dF_s[...] = (0.5 * dy_ref[...]).astype(MXU)

        dw2_ref[...] = _mm_tn(A_ref[...], dF_s[...]).astype(dw2_ref.dtype)

    return _pallas_call(
        body, name=name, grid=(J,),
        in_specs=[pl.BlockSpec((T, D), lambda j: (0, 0)), pl.BlockSpec((None, T, bf), lambda j: (j, 0, 0))]
        + [_UNREAD] * len(after),
        out_specs=pl.BlockSpec((bf, D), lambda j: (j, 0)),
        out_shape=jax.ShapeDtypeStruct((J * bf, D), MXU),
        scratch_shapes=[pltpu.VMEM((T, D), MXU)],
        compiler_params=_params(48, 1),
    )(dy, A, *after)


_AG0, _Q0, _K0, _V0, _F0 = 0, 2 * D_CONV, 2 * D_CONV + D_ATTN, 2 * D_CONV + 2 * D_ATTN, 2 * D_CONV + 3 * D_ATTN
N_IN = _F0 + N_HEADS
N_IN_PAD = _F0 + LANES


def _inproj_fwd(x1, gm, winp, name):
    T, D = x1.shape
    tm = _tile(T, 256)

    def body(x_ref, g_ref, w_ref, ag_ref, k_ref, v_ref, qT_ref, kT_ref, vT_ref, fl_ref):
        h, _ = _rms_fwd(x_ref[...], g_ref[...])
        hb = h.astype(MXU)
        ag_ref[...] = _mm_nt(hb, w_ref[_AG0:_Q0, :])
        qT_ref[...] = (_mm_nt(hb, w_ref[_Q0:_K0, :]) * SCALE).T.astype(MXU)
        for c0, ref, refT in ((_K0, k_ref, kT_ref), (_V0, v_ref, vT_ref)):
            y = _mm_nt(hb, w_ref[c0:c0 + D_ATTN, :])
            ref[...] = y.astype(MXU)
            refT[...] = y.T.astype(MXU)
        fl_ref[...] = _mm_nt(hb, w_ref[_F0:N_IN_PAD, :])

    row = lambda w: pl.BlockSpec((tm, w), lambda i: (i, 0))
    col = pl.BlockSpec((D_ATTN, tm), lambda i: (0, i))
    std = jax.ShapeDtypeStruct((T, D_ATTN), MXU)
    trn = jax.ShapeDtypeStruct((D_ATTN, T), MXU)
    return _pallas_call(
        body, name=name, grid=(T // tm,),
        in_specs=[row(D), pl.BlockSpec((1, D), lambda i: (0, 0)), pl.BlockSpec((N_IN_PAD, D), lambda i: (0, 0))],
        out_specs=[row(2 * D_CONV), row(D_ATTN), row(D_ATTN), col, col, col, row(LANES)],
        out_shape=[jax.ShapeDtypeStruct((T, 2 * D_CONV), F32), std, std, trn, trn, trn,
                   jax.ShapeDtypeStruct((T, LANES), F32)],
        compiler_params=_params(40, 1),
    )(x1, gm, winp)


def _inproj_bwd_act(x1, gm, dx2, dag, dqT, dkT, dvT, dfl, winp, name):
    T, D = x1.shape
    tm = _tile(T, 256)

    def body(x_ref, g_ref, dx2_ref, dag_ref, dqT_ref, dkT_ref, dvT_ref, dfl_ref, w_ref, dx1_ref, dg_ref, h_ref):
        i = pl.program_id(0)
        xv = x_ref[...]
        gv = g_ref[...]
        h, r = _rms_fwd(xv, gv)
        h_ref[...] = h.astype(MXU)
        dh = _mm(dag_ref[...], w_ref[_AG0:_Q0, :])
        for c0, ref in ((_Q0, dqT_ref), (_K0, dkT_ref), (_V0, dvT_ref)):
            dh = dh + _mm_tn(ref[...].astype(MXU), w_ref[c0:c0 + D_ATTN, :])
        dh = dh + _mm(dfl_ref[...].astype(MXU), w_ref[_F0:N_IN_PAD, :])
        dxn, dgp = _rms_bwd(xv, r, gv, dh)
        dx1_ref[...] = dx2_ref[...] + dxn

        @pl.when(i == 0)
        def _():
            dg_ref[...] = dgp

        @pl.when(i > 0)
        def _():
            dg_ref[...] += dgp

    row = lambda w: pl.BlockSpec((tm, w), lambda i: (i, 0))
    col = pl.BlockSpec((D_ATTN, tm), lambda i: (0, i))
    full = lambda a, b: pl.BlockSpec((a, b), lambda i: (0, 0))
    return _pallas_call(
        body, name=name, grid=(T // tm,),
        in_specs=[row(D), full(1, D), row(D), row(2 * D_CONV), col, col, col, row(LANES), full(N_IN_PAD, D)],
        out_specs=[row(D), full(1, D), row(D)],
        out_shape=[jax.ShapeDtypeStruct((T, D), F32), jax.ShapeDtypeStruct((1, D), F32),
                   jax.ShapeDtypeStruct((T, D), MXU)],
        compiler_params=_params(40, 1),
    )(x1, gm, dx2, dag, dqT, dkT, dvT, dfl, winp)


def _inproj_bwd_weights(h, dag, dqT, dkT, dvT, dfl, name, after=()):
    T, D = h.shape

    def body(h_ref, dag_ref, dqT_ref, dkT_ref, dvT_ref, dfl_ref, *rest):
        dw_ref = rest[len(after)]
        hb = h_ref[...]
        dw_ref[_AG0:_Q0, :] = _mm_tn(dag_ref[...], hb).astype(dw_ref.dtype)
        for c0, ref in ((_Q0, dqT_ref), (_K0, dkT_ref), (_V0, dvT_ref)):
            dw_ref[c0:c0 + D_ATTN, :] = _mm(ref[...].astype(MXU), hb).astype(dw_ref.dtype)
        dw_ref[_F0:N_IN_PAD, :] = _mm_tn(dfl_ref[...].astype(MXU), hb).astype(dw_ref.dtype)

    vmem = pl.BlockSpec(memory_space=pltpu.VMEM)
    return _pallas_call(
        body, name=name, in_specs=[vmem] * 6 + [_UNREAD] * len(after), out_specs=vmem,
        out_shape=jax.ShapeDtypeStruct((N_IN_PAD, D), MXU),
        compiler_params=pltpu.CompilerParams(vmem_limit_bytes=56 * MIB),
    )(h, dag, dqT, dkT, dvT, dfl, *after)


def _forget_fwd(fl, fbp, name):
    T = fl.shape[0]
    tb = _tile(T, 256)

    def body(fl_ref, fb_ref, cum_ref, cumT_ref):
        ri = lax.broadcasted_iota(jnp.int32, (tb, tb), 0)
        ci = lax.broadcasted_iota(jnp.int32, (tb, tb), 1)
        tri = (ri >= ci).astype(jnp.bfloat16)
        carry = jnp.zeros((1, LANES), F32)
        for b in range(T // tb):
            z = fl_ref[b * tb:(b + 1) * tb, :] + fb_ref[...]
            lf = jnp.minimum(z, 0.0) - jnp.log1p(jnp.exp(-jnp.abs(z)))
            c = _exact_tri_dot(tri, lf) + carry
            cum_ref[b * tb:(b + 1) * tb, :] = c
            carry = c[tb - 1:tb, :]
        cumT_ref[...] = cum_ref[...].T[:N_HEADS, :]

    return _pallas_call(
        body, name=name,
        out_shape=[jax.ShapeDtypeStruct((T, LANES), F32), jax.ShapeDtypeStruct((N_HEADS, T), F32)],
        compiler_params=pltpu.CompilerParams(vmem_limit_bytes=32 * MIB),
    )(fl, fbp)


def _forget_bwd(dcum, fl, fbp, name):
    T = fl.shape[0]
    tb = _tile(T, 256)

    def body(dc_ref, fl_ref, fb_ref, dfl_ref, dfb_ref):
        ri = lax.broadcasted_iota(jnp.int32, (tb, tb), 0)
        ci = lax.broadcasted_iota(jnp.int32, (tb, tb), 1)
        tri = (ri <= ci).astype(jnp.bfloat16)
        carry = jnp.zeros((1, LANES), F32)
        dfb = jnp.zeros((1, LANES), F32)
        for b in reversed(range(T // tb)):
            sl = slice(b * tb, (b + 1) * tb)
            dl = _exact_tri_dot(tri, dc_ref[sl, :]) + carry
            carry = dl[0:1, :]
            z = fl_ref[sl, :] + fb_ref[...]
            dfl = dl * jax.nn.sigmoid(-z)
            dfl_ref[sl, :] = dfl
            dfb = dfb + jnp.sum(dfl, axis=0, keepdims=True)
        dfb_ref[...] = dfb

    return _pallas_call(
        body, name=name,
        out_shape=[jax.ShapeDtypeStruct((T, LANES), F32), jax.ShapeDtypeStruct((1, LANES), F32)],
        compiler_params=pltpu.CompilerParams(vmem_limit_bytes=32 * MIB),
    )(dcum, fl, fbp)


def _causal_keep(i, j, tq, tk):
    key = j * tk + lax.broadcasted_iota(jnp.int32, (tk, tq), 0)
    qry = i * tq + lax.broadcasted_iota(jnp.int32, (tk, tq), 1)
    return key <= qry


def _split_hi_lo(x):
    hi = x.astype(MXU)
    lo = (x - hi.astype(F32)).astype(MXU)
    return hi, lo


def _attn_fwd(qT, k, vT, cum, cumT, name):
    T = k.shape[0]
    tq = _tile(T, 256)
    tk = _tile(tq, 256)
    kpq = tq // tk
    heads = [slice(HEAD_DIM * h, HEAD_DIM * (h + 1)) for h in range(N_HEADS)]

    def body(qT_ref, k_ref, vT_ref, cum_ref, cumT_ref, o_ref, lseT_ref, acc_s, m_s, l_s):
        i = pl.program_id(0)
        acc_s[...] = jnp.zeros_like(acc_s)
        m_s[...] = jnp.full_like(m_s, NEG)
        l_s[...] = jnp.zeros_like(l_s)

        def kblock(j, masked):
            rows = pl.ds(pl.multiple_of(j * tk, tk), tk)
            keep = _causal_keep(i, j, tq, tk) if masked else None
            bias = [cumT_ref[h:h + 1, :] - cum_ref[rows, h:h + 1] for h in range(N_HEADS)]
            qk = [_mm(k_ref[rows, hs], qT_ref[hs, :]) + bias[h] for h, hs in enumerate(heads)]
            for h, hs in enumerate(heads):
                sT = qk[h]
                if masked:
                    sT = jnp.where(keep, sT, NEG)
                m_old = m_s[h:h + 1, :]
                m_new = jnp.maximum(m_old, jnp.max(sT, axis=0, keepdims=True))
                alpha = jnp.exp(m_old - m_new)
                pT = jnp.exp(sT - m_new)
                l_s[h:h + 1, :] = alpha * l_s[h:h + 1, :] + jnp.sum(pT, axis=0, keepdims=True)
                p_hi, p_lo = _split_hi_lo(pT)
                vh = vT_ref[hs, rows]
                acc_s[hs, :] = alpha * acc_s[hs, :] + (_mm(vh, p_hi) + _mm(vh, p_lo))
                m_s[h:h + 1, :] = m_new

        def unmasked(j, c):
            kblock(j, False)
            return c

        lax.fori_loop(0, kpq * i, unmasked, 0)
        for d in range(kpq):
            kblock(kpq * i + d, True)
        for h, hs in enumerate(heads):
            acc_s[hs, :] = acc_s[hs, :] / l_s[h:h + 1, :]
        o_ref[...] = acc_s[...].T
        lseT_ref[...] = m_s[...] + jnp.log(l_s[...])

    full = lambda a, b: pl.BlockSpec((a, b), lambda i: (0, 0))
    colblk = lambda r: pl.BlockSpec((r, tq), lambda i: (0, i))
    return _pallas_call(
        body, name=name, grid=(T // tq,),
        in_specs=[colblk(D_ATTN), full(T, D_ATTN), full(D_ATTN, T), full(T, LANES), colblk(N_HEADS)],
        out_specs=[pl.BlockSpec((tq, D_ATTN), lambda i: (i, 0)), colblk(N_HEADS)],
        out_shape=[jax.ShapeDtypeStruct((T, D_ATTN), F32), jax.ShapeDtypeStruct((N_HEADS, T), F32)],
        scratch_shapes=[pltpu.VMEM((D_ATTN, tq), F32), pltpu.VMEM((N_HEADS, tq), F32),
                        pltpu.VMEM((N_HEADS, tq), F32)],
        compiler_params=_params(40, 1),
    )(qT, k, vT, cum, cumT)


def _attn_bwd(qT, k, kT, v, doT, lseT, deltaT, cum, cumT, name, after=()):
    T = k.shape[0]
    tq = _tile(T, 256)
    tk = _tile(tq, 256)
    kpq = tq // tk
    heads = [slice(HEAD_DIM * h, HEAD_DIM * (h + 1)) for h in range(N_HEADS)]

    def body(qT_ref, k_ref, kT_ref, v_ref, doT_ref, lseT_ref, dlT_ref, cum_ref, cumT_ref, *rest):
        dq_ref, dk_ref, dv_ref, dcum_ref, dq_s = rest[len(after):]
        i = pl.program_id(0)

        @pl.when(i == 0)
        def _():
            dk_ref[...] = jnp.zeros_like(dk_ref)
            dv_ref[...] = jnp.zeros_like(dv_ref)
            dcum_ref[...] = jnp.zeros_like(dcum_ref)

        dq_s[...] = jnp.zeros_like(dq_s)

        def kblock(j, masked):
            rows = pl.ds(pl.multiple_of(j * tk, tk), tk)
            keep = _causal_keep(i, j, tq, tk) if masked else None
            bias = [cumT_ref[h:h + 1, :] - cum_ref[rows, h:h + 1] for h in range(N_HEADS)]
            qk = [_mm(k_ref[rows, hs], qT_ref[hs, :]) + bias[h] for h, hs in enumerate(heads)]
            dps = [_mm(v_ref[rows, hs], doT_ref[hs, :]) for hs in heads]
            for h, hs in enumerate(heads):
                sT = qk[h]
                if masked:
                    sT = jnp.where(keep, sT, NEG)
                pT = jnp.exp(sT - lseT_ref[h:h + 1, :])
                dsT = pT * (dps[h] - dlT_ref[h:h + 1, :])
                dcum_ref[rows, h:h + 1] += -jnp.sum(dsT, axis=1, keepdims=True)
                dsb = dsT.astype(MXU)
                dv_ref[hs, rows] += _mm_nt(doT_ref[hs, :], pT.astype(MXU))
                dk_ref[hs, rows] += _mm_nt(qT_ref[hs, :], dsb)
                dq_s[hs, :] += _mm(kT_ref[hs, rows], dsb)

        def unmasked(j, c):
            kblock(j, False)
            return c

        lax.fori_loop(0, kpq * i, unmasked, 0)
        for d in range(kpq):
            kblock(kpq * i + d, True)
        dq_ref[...] = (dq_s[...] * SCALE).astype(dq_ref.dtype)

    full = lambda a, b: pl.BlockSpec((a, b), lambda i: (0, 0))
    colblk = lambda r: pl.BlockSpec((r, tq), lambda i: (0, i))
    return _pallas_call(
        body, name=name, grid=(T // tq,),
        in_specs=[colblk(D_ATTN), full(T, D_ATTN), full(D_ATTN, T), full(T, D_ATTN), colblk(D_ATTN),
                  colblk(N_HEADS), colblk(N_HEADS), full(T, LANES), colblk(N_HEADS)] + [_UNREAD] * len(after),
        out_specs=[colblk(D_ATTN), full(D_ATTN, T), full(D_ATTN, T), full(T, LANES)],
        out_shape=[
            jax.ShapeDtypeStruct((D_ATTN, T), MXU),
            jax.ShapeDtypeStruct((D_ATTN, T), F32),
            jax.ShapeDtypeStruct((D_ATTN, T), F32),
            jax.ShapeDtypeStruct((T, LANES), F32),
        ],
        scratch_shapes=[pltpu.VMEM((D_ATTN, tq), F32)],
        compiler_params=_params(48, 1),
    )(qT, k, kT, v, doT, lseT, deltaT, cum, cumT, *after)


_ROWS_PER_CHUNK = 64


def _glu_halo(ag_ref, agh_ref, uext_s, tm, first):
    a = ag_ref[:, :D_CONV]
    sg = jax.nn.sigmoid(ag_ref[:, D_CONV:])
    uh = agh_ref[:, :D_CONV] * jax.nn.sigmoid(agh_ref[:, D_CONV:])
    uext_s[0:CONV_HALO, :] = jnp.where(first, 0.0, uh)
    uext_s[CONV_HALO:CONV_HALO + tm, :] = a * sg
    return a, sg


_SUBLANES = 8


def _shifted_copies(ext_s, sh_s, rows):
    for k in range(1, _SUBLANES):
        sh_s[k, 0:rows, :] = ext_s[pl.ds(k, rows), :]


def _window(ext_s, sh_s, start, rows):
    k = start % _SUBLANES
    if k == 0:
        return ext_s[pl.ds(start, rows), :]
    return sh_s[k, pl.ds(start - k, rows), :]


def _layer_norm_stats(y):
    mu = jnp.mean(y, axis=-1, keepdims=True)
    xc = y - mu
    rs = lax.rsqrt(jnp.mean(xc * xc, axis=-1, keepdims=True) + EPS)
    return xc * rs, rs


def _conv_fwd(ag, w32, cb, lg, lb, name):
    T = ag.shape[0]
    tm = _tile(T, 256)
    rc = _tile(tm, _ROWS_PER_CHUNK)
    hb = tm // CONV_HALO

    def body(ag_ref, agh_ref, w_ref, cb_ref, lg_ref, lb_ref, yc_ref, c_ref, uext_s, ush_s):
        i = pl.program_id(0)
        _glu_halo(ag_ref, agh_ref, uext_s, tm, i == 0)
        _shifted_copies(uext_s, ush_s, tm + CONV_HALO - _SUBLANES)
        for r0 in range(0, tm, rc):
            acc = jnp.zeros((rc, D_CONV), F32)
            for t in range(CONV_TAPS):
                acc = acc + _window(uext_s, ush_s, r0 + CONV_HALO - (CONV_TAPS - 1) + t, rc) * w_ref[t:t + 1, :]
            y = acc + cb_ref[...]
            yc_ref[r0:r0 + rc, :] = y
            n, _ = _layer_norm_stats(y)
            z = n * lg_ref[...] + lb_ref[...]
            c_ref[r0:r0 + rc, :] = z * jax.nn.sigmoid(z)

    row = lambda w: pl.BlockSpec((tm, w), lambda i: (i, 0))
    full = lambda a, b: pl.BlockSpec((a, b), lambda i: (0, 0))
    return _pallas_call(
        body, name=name, grid=(T // tm,),
        in_specs=[row(2 * D_CONV),
                  pl.BlockSpec((CONV_HALO, 2 * D_CONV), lambda i: (jnp.maximum(i * hb - 1, 0), 0)),
                  full(CONV_HALO, D_CONV), full(1, D_CONV), full(1, D_CONV), full(1, D_CONV)],
        out_specs=[row(D_CONV), row(D_CONV)],
        out_shape=[jax.ShapeDtypeStruct((T, D_CONV), F32), jax.ShapeDtypeStruct((T, D_CONV), F32)],
        scratch_shapes=[pltpu.VMEM((CONV_HALO + tm, D_CONV), F32),
                        pltpu.VMEM((_SUBLANES, CONV_HALO + tm, D_CONV), F32)],
        compiler_params=_params(32, 1),
    )(ag, ag, w32, cb, lg, lb)


def _conv_bwd(dc, yc, ag, w32, lg, lb, name):
    T = ag.shape[0]
    tm = _tile(T, 256)
    rc = _tile(tm, _ROWS_PER_CHUNK)
    I = T // tm
    hb = tm // CONV_HALO
    n_halo_blocks = T // CONV_HALO

    def body(dc_ref, yc_ref, dch_ref, ych_ref, ag_ref, agh_ref, w_ref, lg_ref, lb_ref,
             dag_ref, dw_ref, dcb_ref, dlg_ref, dlb_ref, uext_s, dext_s, ush_s, dsh_s):
        i = pl.program_id(0)
        lgv = lg_ref[...]
        lbv = lb_ref[...]

        def ln_bwd(dcv, ycv):
            n, rs = _layer_norm_stats(ycv)
            z = n * lgv + lbv
            dz = dcv * _silu_grad(z, jax.nn.sigmoid(z))
            dn = dz * lgv
            dy = rs * (dn - jnp.mean(dn, axis=-1, keepdims=True) - n * jnp.mean(dn * n, axis=-1, keepdims=True))
            return dy, dz, n

        dy, dz, n = ln_bwd(dc_ref[...], yc_ref[...])
        dyh, _, _ = ln_bwd(dch_ref[...], ych_ref[...])
        dext_s[0:tm, :] = dy
        dext_s[tm:tm + CONV_HALO, :] = jnp.where(i == I - 1, 0.0, dyh)
        a, sg = _glu_halo(ag_ref, agh_ref, uext_s, tm, i == 0)
        _shifted_copies(uext_s, ush_s, tm + CONV_HALO - _SUBLANES)
        _shifted_copies(dext_s, dsh_s, tm + CONV_HALO - _SUBLANES)

        @pl.when(i == 0)
        def _():
            dw_ref[...] = jnp.zeros_like(dw_ref)
            dcb_ref[...] = jnp.zeros_like(dcb_ref)
            dlg_ref[...] = jnp.zeros_like(dlg_ref)
            dlb_ref[...] = jnp.zeros_like(dlb_ref)

        dcb_ref[...] += jnp.sum(dy, axis=0, keepdims=True)
        dlg_ref[...] += jnp.sum(dz * n, axis=0, keepdims=True)
        dlb_ref[...] += jnp.sum(dz, axis=0, keepdims=True)
        for t in range(CONV_TAPS):
            u_t = _window(uext_s, ush_s, CONV_HALO - (CONV_TAPS - 1) + t, tm)
            dw_ref[t:t + 1, :] += jnp.sum(dy * u_t, axis=0, keepdims=True)
        for r0 in range(0, tm, rc):
            acc = jnp.zeros((rc, D_CONV), F32)
            for t in range(CONV_TAPS):
                acc = acc + _window(dext_s, dsh_s, r0 + (CONV_TAPS - 1) - t, rc) * w_ref[t:t + 1, :]
            a_c = a[r0:r0 + rc, :]
            sg_c = sg[r0:r0 + rc, :]
            dag_ref[r0:r0 + rc, :D_CONV] = (acc * sg_c).astype(dag_ref.dtype)
            dag_ref[r0:r0 + rc, D_CONV:] = (acc * a_c * sg_c * (1.0 - sg_c)).astype(dag_ref.dtype)

    row = lambda w: pl.BlockSpec((tm, w), lambda i: (i, 0))
    full = lambda a, b: pl.BlockSpec((a, b), lambda i: (0, 0))
    nxt = pl.BlockSpec((CONV_HALO, D_CONV), lambda i: (jnp.minimum((i + 1) * hb, n_halo_blocks - 1), 0))
    return _pallas_call(
        body, name=name, grid=(I,),
        in_specs=[row(D_CONV), row(D_CONV), nxt, nxt, row(2 * D_CONV),
                  pl.BlockSpec((CONV_HALO, 2 * D_CONV), lambda i: (jnp.maximum(i * hb - 1, 0), 0)),
                  full(CONV_HALO, D_CONV), full(1, D_CONV), full(1, D_CONV)],
        out_specs=[row(2 * D_CONV), full(CONV_HALO, D_CONV), full(1, D_CONV), full(1, D_CONV), full(1, D_CONV)],
        out_shape=[
            jax.ShapeDtypeStruct((T, 2 * D_CONV), MXU),
            jax.ShapeDtypeStruct((CONV_HALO, D_CONV), F32),
            jax.ShapeDtypeStruct((1, D_CONV), F32),
            jax.ShapeDtypeStruct((1, D_CONV), F32),
            jax.ShapeDtypeStruct((1, D_CONV), F32),
        ],
        scratch_shapes=[pltpu.VMEM((CONV_HALO + tm, D_CONV), F32), pltpu.VMEM((tm + CONV_HALO, D_CONV), F32),
                        pltpu.VMEM((_SUBLANES, CONV_HALO + tm, D_CONV), F32),
                        pltpu.VMEM((_SUBLANES, CONV_HALO + tm, D_CONV), F32)],
        compiler_params=_params(40, 1),
    )(dc, yc, dc, yc, ag, ag, w32, lg, lb)


def _outproj_fwd(x1, c, o, gc, ga, wout, name):
    T, D = x1.shape
    tm = _tile(T, 512)

    def body(x_ref, c_ref, o_ref, gc_ref, ga_ref, w_ref, x2_ref):
        yc, _ = _rms_fwd(c_ref[...], gc_ref[...])
        ya, _ = _rms_fwd(o_ref[...], ga_ref[...])
        x2_ref[...] = (x_ref[...] + _mm(yc.astype(MXU), w_ref[:D_CONV, :])
                       + _mm(ya.astype(MXU), w_ref[D_CONV:, :]))

    row = lambda w: pl.BlockSpec((tm, w), lambda i: (i, 0))
    full = lambda a, b: pl.BlockSpec((a, b), lambda i: (0, 0))
    return _pallas_call(
        body, name=name, grid=(T // tm,),
        in_specs=[row(D), row(D_CONV), row(D_ATTN), full(1, D_CONV), full(1, D_ATTN), full(D_CONV + D_ATTN, D)],
        out_specs=row(D),
        out_shape=jax.ShapeDtypeStruct((T, D), F32),
        compiler_params=_params(32, 1),
    )(x1, c, o, gc, ga, wout)


def _outproj_bwd(dx2, c, o, gc, ga, wout, name):
    T, D = dx2.shape
    tm = _tile(T, 256)
    I = T // tm

    def body(dx_ref, c_ref, o_ref, gc_ref, ga_ref, w_ref,
             dc_ref, doT_ref, dlT_ref, dw_ref, dgc_ref, dga_ref, acc_s):
        i = pl.program_id(0)
        dxb = dx_ref[...].astype(MXU)
        cv = c_ref[...]
        ov = o_ref[...]
        yc, rcn = _rms_fwd(cv, gc_ref[...])
        ya, ra = _rms_fwd(ov, ga_ref[...])
        dyc = _mm_nt(dxb, w_ref[:D_CONV, :])
        dya = _mm_nt(dxb, w_ref[D_CONV:, :])
        dwc = _mm_tn(yc.astype(MXU), dxb)
        dwa = _mm_tn(ya.astype(MXU), dxb)
        dcv, dgc = _rms_bwd(cv, rcn, gc_ref[...], dyc)
        dov, dga = _rms_bwd(ov, ra, ga_ref[...], dya)
        dc_ref[...] = dcv
        dob = dov.astype(doT_ref.dtype)
        doT_ref[...] = dov.T.astype(doT_ref.dtype)
        chan = lax.broadcasted_iota(jnp.int32, (D_ATTN, LANES), 0)
        head = lax.broadcasted_iota(jnp.int32, (D_ATTN, LANES), 1)
        in_head = ((chan >= head * HEAD_DIM) & (chan < (head + 1) * HEAD_DIM)).astype(jnp.bfloat16)
        dlT_ref[...] = _exact_dot_01(dob.astype(F32) * ov, in_head).T[:N_HEADS, :]

        @pl.when(i == 0)
        def _():
            acc_s[:D_CONV, :] = dwc
            acc_s[D_CONV:, :] = dwa
            dgc_ref[...] = dgc
            dga_ref[...] = dga

        @pl.when(i > 0)
        def _():
            acc_s[:D_CONV, :] += dwc
            acc_s[D_CONV:, :] += dwa
            dgc_ref[...] += dgc
            dga_ref[...] += dga

        @pl.when(i == I - 1)
        def _():
            dw_ref[...] = acc_s[...].astype(dw_ref.dtype)

    row = lambda w: pl.BlockSpec((tm, w), lambda i: (i, 0))
    full = lambda a, b: pl.BlockSpec((a, b), lambda i: (0, 0))
    return _pallas_call(
        body, name=name, grid=(I,),
        in_specs=[row(D), row(D_CONV), row(D_ATTN), full(1, D_CONV), full(1, D_ATTN), full(D_CONV + D_ATTN, D)],
        out_specs=[row(D_CONV), pl.BlockSpec((D_ATTN, tm), lambda i: (0, i)),
                   pl.BlockSpec((N_HEADS, tm), lambda i: (0, i)),
                   full(D_CONV + D_ATTN, D), full(1, D_CONV), full(1, D_ATTN)],
        out_shape=[
            jax.ShapeDtypeStruct((T, D_CONV), F32),
            jax.ShapeDtypeStruct((D_ATTN, T), MXU),
            jax.ShapeDtypeStruct((N_HEADS, T), F32),
            jax.ShapeDtypeStruct((D_CONV + D_ATTN, D), MXU),
            jax.ShapeDtypeStruct((1, D_CONV), F32),
            jax.ShapeDtypeStruct((1, D_ATTN), F32),
        ],
        scratch_shapes=[pltpu.VMEM((D_CONV + D_ATTN, D), F32)],
        compiler_params=_params(40, 1),
    )(dx2, c, o, gc, ga, wout)


def _ffn_down_loss(x, A, w2, gf, target, name):
    T, D = x.shape
    J, _, bf = A.shape
    tm = _tile(T, 512)

    def body(x_ref, A_ref, w2_ref, g_ref, t_ref, loss_ref, dx_ref, dg_ref):
        i = pl.program_id(0)
        f = _mm(A_ref[0], w2_ref[0:bf, :])
        for j in range(1, J):
            f = f + _mm(A_ref[j], w2_ref[j * bf:(j + 1) * bf, :])
        xv = x_ref[...] + 0.5 * f
        gv = g_ref[...]
        out, r = _rms_fwd(xv, gv)
        err = out - t_ref[...]
        part = jnp.full((1, LANES), 0.5 / D, F32) * jnp.sum(err * err)
        dxn, dgp = _rms_bwd(xv, r, gv, err * (1.0 / D))
        dx_ref[...] = dxn

        @pl.when(i == 0)
        def _():
            loss_ref[...] = part
            dg_ref[...] = dgp

        @pl.when(i > 0)
        def _():
            loss_ref[...] += part
            dg_ref[...] += dgp

    row = lambda w: pl.BlockSpec((tm, w), lambda i: (i, 0))
    full = lambda a, b: pl.BlockSpec((a, b), lambda i: (0, 0))
    return _pallas_call(
        body, name=name, grid=(T // tm,),
        in_specs=[row(D), pl.BlockSpec((J, tm, bf), lambda i: (0, i, 0)), full(J * bf, D), full(1, D), row(D)],
        out_specs=[full(1, LANES), row(D), full(1, D)],
        out_shape=[jax.ShapeDtypeStruct((1, LANES), F32), jax.ShapeDtypeStruct((T, D), F32),
                   jax.ShapeDtypeStruct((1, D), F32)],
        compiler_params=_params(56, 1),
    )(x, A, w2, gf, target)


def _row_tile(rows):
    for cand in (256, 176, 128, 64, 32, 16):
        if rows % cand == 0:
            return cand
    return rows


def _adamw(w, m, v, parts, name):
    R, C = w.shape
    P = parts.shape[0]
    tr = _row_tile(R)
    c1 = 1.0 - ADAM_B1 ** ADAM_STEP
    c2 = 1.0 - ADAM_B2 ** ADAM_STEP

    def body(w_ref, m_ref, v_ref, p_ref, g_ref, d_ref, nm_ref, nv_ref):
        g = p_ref[0].astype(F32)
        for s in range(1, P):
            g = g + p_ref[s].astype(F32)
        wv = w_ref[...]
        mn = ADAM_B1 * m_ref[...] + (1.0 - ADAM_B1) * g
        vn = ADAM_B2 * v_ref[...] + (1.0 - ADAM_B2) * (g * g)
        g_ref[...] = g
        nm_ref[...] = mn
        nv_ref[...] = vn
        d_ref[...] = -ADAM_LR * ((mn / c1) / (jnp.sqrt(vn / c2) + ADAM_EPS) + ADAM_WD * wv)

    blk = pl.BlockSpec((tr, C), lambda i: (i, 0))
    out = jax.ShapeDtypeStruct((R, C), F32)
    return _pallas_call(
        body, name=name, grid=(R // tr,),
        in_specs=[blk, blk, blk, pl.BlockSpec((P, tr, C), lambda i: (0, i, 0))],
        out_specs=[blk, blk, blk, blk],
        out_shape=[out, out, out, out],
        compiler_params=_params(32, 1),
    )(w, m, v, parts)


def _position():
    return lax.axis_index("x"), lax.axis_index("y"), lax.axis_index("c")


def _flat(px, py, pc):
    return 4 * px + 2 * py + pc


def _gather_body(ins, outs, send_sems, recv_sems, local_sems, handshake):
    n = len(ins)
    x, y, c = _position()
    me, sibling = (x, y, c), (x, y, 1 - c)
    chips = [(1 - x, y), (x, 1 - y), (1 - x, 1 - y)]
    if handshake:
        _handshake([sibling] + [(*chip, cc) for chip in chips for cc in (c, 1 - c)])

    def copy(a, k, block, to, src=None):
        dst = outs[a].at[_flat(*block)]
        return pltpu.make_async_remote_copy(
            src_ref=dst if src is None else src, dst_ref=dst,
            send_sem=send_sems.at[a, k], recv_sem=recv_sems.at[a, k],
            device_id=to, device_id_type=MESH)

    mine = [pltpu.make_async_copy(ins[a], outs[a].at[_flat(*me)], local_sems.at[a]) for a in range(n)]
    for cp in mine:
        cp.start()
    first = []
    for a in range(n):
        first.append(copy(a, 0, me, sibling, src=ins[a]))
        first += [copy(a, 1 + j, me, (*chip, c), src=ins[a]) for j, chip in enumerate(chips)]
    for cp in first:
        cp.start()
    passed = []
    for a in range(n):
        for j, chip in enumerate(chips):
            copy(a, 1 + j, (*chip, c), me).wait_recv()
            fwd = copy(a, 4 + j, (*chip, c), sibling)
            fwd.start()
            passed.append(fwd)
    for a in range(n):
        copy(a, 0, sibling, me).wait_recv()
        for j, chip in enumerate(chips):
            copy(a, 4 + j, (*chip, 1 - c), me).wait_recv()
    for cp in first + passed:
        cp.wait_send()
    for cp in mine:
        cp.wait()


def _gather_scratch(n):
    return [pltpu.SemaphoreType.DMA((n, 7)), pltpu.SemaphoreType.DMA((n, 7)), pltpu.SemaphoreType.DMA((n,))]


def _all_gather(shards, name):
    n = len(shards)

    def body(*refs):
        _gather_body(refs[:n], refs[n:2 * n], *refs[2 * n:], handshake=False)

    hbm = pl.BlockSpec(memory_space=pltpu.HBM)
    return _pallas_call(
        body, name=name,
        in_specs=[hbm] * n, out_specs=[hbm] * n,
        out_shape=[jax.ShapeDtypeStruct((N_DEV,) + s.shape, s.dtype) for s in shards],
        scratch_shapes=_gather_scratch(n),
    )(*shards)


def _handshake(peers):
    barrier = pltpu.get_barrier_semaphore()
    for peer in peers:
        pl.semaphore_signal(barrier, inc=1, device_id=peer, device_id_type=MESH)
    pl.semaphore_wait(barrier, len(peers))


def _sequencer_call(body, name, collective_id, out_type, scratch_types, operands):
    return pl.kernel(
        body, name=name, out_type=out_type,
        mesh=plsc.ScalarSubcoreMesh(axis_name="sequencer", num_cores=1),
        scratch_types=scratch_types,
        compiler_params=pltpu.CompilerParams(collective_id=collective_id),
    )(*operands)


def _seq_all_gather(shards, name, collective_id, after):
    n = len(shards)

    def body(*refs):
        _gather_body(refs[:n], refs[n + 1:2 * n + 1], *refs[2 * n + 1:], handshake=True)

    return _sequencer_call(
        body, name, collective_id,
        [jax.ShapeDtypeStruct((N_DEV,) + s.shape, s.dtype) for s in shards],
        _gather_scratch(n), list(shards) + [after])


def _seq_to_sibling(parts, name, collective_id, after):
    n = len(parts)

    def body(*refs):
        ins, outs = refs[:n], refs[n + len(after):2 * n + len(after)]
        send_sems, recv_sems = refs[2 * n + len(after):]
        x, y, c = _position()
        sibling = (x, y, 1 - c)
        _handshake([sibling])
        sent = []
        for a in range(n):
            for q in range(N_CHIPS):
                cp = pltpu.make_async_remote_copy(
                    src_ref=ins[a].at[2 * q + (1 - c)], dst_ref=outs[a].at[q],
                    send_sem=send_sems.at[a, q], recv_sem=recv_sems.at[a, q],
                    device_id=sibling, device_id_type=MESH)
                cp.start()
                sent.append(cp)
        for cp in sent:
            cp.wait_recv()
        for cp in sent:
            cp.wait_send()

    return _sequencer_call(
        body, name, collective_id,
        [jax.ShapeDtypeStruct((N_CHIPS,) + p.shape[1:], p.dtype) for p in parts],
        [pltpu.SemaphoreType.DMA((n, N_CHIPS)), pltpu.SemaphoreType.DMA((n, N_CHIPS))],
        list(parts) + list(after))


def _seq_to_chips(partials, name, collective_id, after=()):
    n = len(partials)

    def body(*refs):
        ins, outs = refs[:n], refs[n + len(after):2 * n + len(after)]
        send_sems, recv_sems, local_sems = refs[2 * n + len(after):]
        x, y, c = _position()
        my_chip = 2 * x + y
        chips = [(1 - x, y), (x, 1 - y), (1 - x, 1 - y)]
        _handshake([(*chip, c) for chip in chips])
        mine = [pltpu.make_async_copy(ins[a].at[my_chip], outs[a].at[my_chip], local_sems.at[a]) for a in range(n)]
        for cp in mine:
            cp.start()
        sent = []
        for a in range(n):
            for j, (px, py) in enumerate(chips):
                cp = pltpu.make_async_remote_copy(
                    src_ref=ins[a].at[2 * px + py], dst_ref=outs[a].at[my_chip],
                    send_sem=send_sems.at[a, j], recv_sem=recv_sems.at[a, j],
                    device_id=(px, py, c), device_id_type=MESH)
                cp.start()
                sent.append(cp)
        for cp in sent:
            cp.wait_recv()
        for cp in sent:
            cp.wait_send()
        for cp in mine:
            cp.wait()

    return _sequencer_call(
        body, name, collective_id,
        [jax.ShapeDtypeStruct(p.shape, p.dtype) for p in partials],
        [pltpu.SemaphoreType.DMA((n, 3)), pltpu.SemaphoreType.DMA((n, 3)), pltpu.SemaphoreType.DMA((n,))],
        list(partials) + list(after))


def _pair_add(parts, recvs, name, after=()):
    n = len(parts)
    core = lax.axis_index("c").astype(jnp.int32).reshape(1)

    def body(c_ref, *refs):
        ps, rs, outs = refs[:n], refs[n:2 * n], refs[2 * n + len(after):]
        for p_ref, r_ref, o_ref in zip(ps, rs, outs):
            o_ref[...] = (p_ref[...].astype(F32) + r_ref[...].astype(F32)).astype(o_ref.dtype)

    mine = lambda p: pl.BlockSpec((None,) + p.shape[1:], lambda q, c: (2 * q + c[0], 0, 0))
    blk = lambda p: pl.BlockSpec((None,) + p.shape[1:], lambda q, c: (q, 0, 0))
    return pl.pallas_call(
        body, name=name,
        grid_spec=pltpu.PrefetchScalarGridSpec(
            num_scalar_prefetch=1, grid=(N_CHIPS,),
            in_specs=[mine(p) for p in parts] + [blk(p) for p in parts] + [_UNREAD] * len(after),
            out_specs=[blk(p) for p in parts]),
        out_shape=[pltpu.HBM((N_CHIPS,) + p.shape[1:], p.dtype) for p in parts],
        compiler_params=_params(40, 1),
    )(core, *[pltpu.with_memory_space_constraint(a, pltpu.HBM) for a in (*parts, *recvs, *after)])


class _Reduced(NamedTuple):
    partials: list
    reduced: list


def _blocks(g):
    return g.reshape(N_DEV, -1, g.shape[-1])


def _reduce_scatter(parts, tag, ids, after=(), between=None, add_after=()):
    from_sibling = _seq_to_sibling(parts, "rs_sibling_" + tag, ids[0], after)
    mid = between(from_sibling[0]) if between else ()
    partials = _pair_add(parts, from_sibling, "rs_add_" + tag, add_after)
    return _Reduced(partials, _seq_to_chips(partials, "rs_chips_" + tag, ids[1], mid))


_SMALL = ("ffn1_norm", "mix_norm", "conv_b", "conv_ln_g", "conv_ln_b", "forget_b", "out_norm_conv",
          "out_norm_attn", "ffn2_norm", "final_norm")
_PACK_WIDTH = 2 * D_CONV
_SLOT = dict(ffn1_norm=(0, 0), mix_norm=(1, 0), ffn2_norm=(2, 0), final_norm=(3, 0), conv_b=(4, 0),
             conv_ln_g=(4, D_CONV), conv_ln_b=(5, 0), out_norm_conv=(5, D_CONV), out_norm_attn=(6, 0),
             forget_b=(6, D_CONV))
_CONV_ROW0 = 8
_PACK_ROWS = _CONV_ROW0 + CONV_HALO


def _pack_small(small, name):
    arrays = [small[n] for n in _SMALL] + [small["conv_w"]]

    def body(*refs):
        out = refs[-1]
        out[...] = jnp.zeros_like(out)
        for n, ref in zip(_SMALL, refs):
            row, lane = _SLOT[n]
            out[row:row + 1, lane:lane + ref.shape[1]] = ref[...]
        out[_CONV_ROW0:, :D_CONV] = refs[len(_SMALL)][...]

    return _pallas_call(body, name=name, out_shape=jax.ShapeDtypeStruct((_PACK_ROWS, _PACK_WIDTH), F32))(*arrays)


def _adamw_small(gathered, w, m, v, name):
    c1 = 1.0 - ADAM_B1 ** ADAM_STEP
    c2 = 1.0 - ADAM_B2 ** ADAM_STEP
    k = len(_SMALL)

    def body(g_ref, *refs):
        ws, ms, vs = refs[:k], refs[k:2 * k], refs[2 * k:3 * k]
        outs = refs[3 * k:]
        total = g_ref[0]
        for s in range(1, N_DEV):
            total = total + g_ref[s]
        for i, n in enumerate(_SMALL):
            row, lane = _SLOT[n]
            width = ws[i].shape[1]
            g = total[row:row + 1, lane:lane + width]
            mn = ADAM_B1 * ms[i][...] + (1.0 - ADAM_B1) * g
            vn = ADAM_B2 * vs[i][...] + (1.0 - ADAM_B2) * (g * g)
            o_g, o_d, o_m, o_v = outs[4 * i:4 * i + 4]
            o_g[...] = g
            o_m[...] = mn
            o_v[...] = vn
            o_d[...] = -ADAM_LR * ((mn / c1) / (jnp.sqrt(vn / c2) + ADAM_EPS) + ADAM_WD * ws[i][...])
        outs[4 * k][...] = total[_CONV_ROW0:, :D_CONV]

    shapes = []
    for n in _SMALL:
        shapes += [jax.ShapeDtypeStruct(w[n].shape, F32)] * 4
    shapes.append(jax.ShapeDtypeStruct((CONV_HALO, D_CONV), F32))
    res = _pallas_call(body, name=name, out_shape=shapes)(
        gathered, *[w[n] for n in _SMALL], *[m[n] for n in _SMALL], *[v[n] for n in _SMALL])
    return {n: res[4 * i:4 * i + 4] for i, n in enumerate(_SMALL)}, res[4 * k]


def _local_step(x, target, norms, shard):
    D = x.shape[1]
    J = N_DEV // 2
    as13 = lambda g: g.reshape(2, J, g.shape[1], D)

    (g13_1,) = _all_gather([shard["ffn1_w13"]], "gather_ffn1_w13")
    (g2_1,) = _seq_all_gather([shard["ffn1_w2"]], "gather_ffn1_w2", 10, after=g13_1)
    w13_1 = as13(g13_1)
    G1, U1, A1 = _ffn_up(x, norms["ffn1_norm"], w13_1, "ffn1_up")
    gin, gconv = _seq_all_gather([shard["w_in"], shard["conv_w"]], "gather_mix", 1, after=G1)
    w2_1 = g2_1.reshape(-1, D)
    x1 = _ffn_down(x, A1, w2_1, "ffn1_down")
    gout, g13_2, g2_2 = _seq_all_gather([shard["w_out"], shard["ffn2_w13"], shard["ffn2_w2"]], "gather_ffn2", 2,
                                        after=x1)
    winp = jnp.pad(gin.reshape(N_IN, D), ((0, N_IN_PAD - N_IN), (0, 0)))
    wout = gout.reshape(-1, D)
    conv_w32 = jnp.pad(gconv.transpose(1, 0, 2).reshape(CONV_TAPS, D_CONV), ((0, CONV_HALO - CONV_TAPS), (0, 0)))

    ag, k, v, qT, kT, vT, fl = _inproj_fwd(x1, norms["mix_norm"], winp, "inproj_fwd")
    cum, cumT = _forget_fwd(fl, norms["forget_b"], "forget_fwd")
    yc, c = _conv_fwd(ag, conv_w32, norms["conv_b"], norms["conv_ln_g"], norms["conv_ln_b"], "conv_fwd")
    o, lseT = _attn_fwd(qT, k, vT, cum, cumT, "attn_fwd")
    x2 = _outproj_fwd(x1, c, o, norms["out_norm_conv"], norms["out_norm_attn"], wout, "outproj_fwd")
    w13_2, w2_2 = as13(g13_2), g2_2.reshape(-1, D)
    G2, U2, A2 = _ffn_up(x2, norms["ffn2_norm"], w13_2, "ffn2_up")
    loss, dx3, d_final = _ffn_down_loss(x2, A2, w2_2, norms["final_norm"], target, "ffn2_down_loss")

    dw2_2 = _ffn_w2_grad(dx3, A2, "ffn2_w2_grad")
    dx2, d_ffn2n, h3, dG2, dU2 = _ffn_bwd_act(x2, norms["ffn2_norm"], dx3, G2, U2, w13_2, w2_2, "ffn2_bwd_act")
    dw13_2 = _ffn_w13_grad(h3, dG2, dU2, "ffn2_w13_grad")
    dc, dobT, deltaT, dwout, d_onc, d_ona = _outproj_bwd(
        dx2, c, o, norms["out_norm_conv"], norms["out_norm_attn"], wout, "outproj_bwd")
    red_ffn2 = _reduce_scatter([_blocks(dw13_2), _blocks(dw2_2)], "ffn2", (3, 4), add_after=(dc,))
    dqT, dkT, dvT, dcum = _attn_bwd(qT, k, kT, v, dobT, lseT, deltaT, cum, cumT, "attn_bwd",
                                    after=red_ffn2.partials)
    dfl, d_fb = _forget_bwd(dcum, fl, norms["forget_b"], "forget_bwd")
    dag, d_convw, d_cb, d_lg, d_lb = _conv_bwd(dc, yc, ag, conv_w32, norms["conv_ln_g"], norms["conv_ln_b"], "conv_bwd")
    dx1, d_mixn, h2 = _inproj_bwd_act(x1, norms["mix_norm"], dx2, dag, dqT, dkT, dvT, dfl, winp, "inproj_bwd_act")
    dw2_1 = _ffn_w2_grad(dx1, A1, "ffn1_w2_grad")
    dwinp = _inproj_bwd_weights(h2, dag, dqT, dkT, dvT, dfl, "inproj_bwd_weights")
    dwin_blocks = dwinp[:N_IN].reshape(N_DEV, N_IN // N_DEV, -1)
    red_mix = _reduce_scatter([dwin_blocks, _blocks(dwout), _blocks(dw2_1)], "mix", (5, 6),
                              after=red_ffn2.reduced[:1])
    dx, d_ffn1n, h1, dG1, dU1 = _ffn_bwd_act(x, norms["ffn1_norm"], dx1, G1, U1, w13_1, w2_1, "ffn1_bwd_act",
                                             after=red_mix.partials)
    dw13_1 = _ffn_w13_grad(h1, dG1, dU1, "ffn1_w13_grad")

    small = dict(ffn1_norm=d_ffn1n, mix_norm=d_mixn, conv_b=d_cb, conv_ln_g=d_lg, conv_ln_b=d_lb,
                 forget_b=d_fb, out_norm_conv=d_onc, out_norm_attn=d_ona, ffn2_norm=d_ffn2n,
                 final_norm=d_final, conv_w=d_convw)
    packed_small = _pack_small(small, "pack_small_grads")
    gathered_small = []

    def gather_small(behind):
        gathered_small.extend(_seq_all_gather([packed_small], "gather_small_grads", 9, after=behind))
        return gathered_small

    red_w13_1 = _reduce_scatter([_blocks(dw13_1)], "ffn1_w13", (7, 8), after=red_mix.reduced[:1],
                                between=gather_small)
    big = dict(ffn1_w13=red_w13_1.reduced[0], ffn1_w2=red_mix.reduced[2], w_in=red_mix.reduced[0],
               w_out=red_mix.reduced[1], ffn2_w13=red_ffn2.reduced[0], ffn2_w2=red_ffn2.reduced[1])
    return loss[0, 0], dx, gathered_small[0], big


_BIG = ("ffn1_w13", "ffn1_w2", "w_in", "w_out", "ffn2_w13", "ffn2_w2")
_TRANSPOSED = ("ffn1_w13", "ffn2_w13", "w_in")
_ORDER = ("ffn1_norm", "ffn1_w13", "ffn1_w2", "mix_norm", "w_in", "conv_w", "conv_b", "conv_ln_g", "conv_ln_b",
          "forget_b", "out_norm_conv", "out_norm_attn", "w_out", "ffn2_norm", "ffn2_w13", "ffn2_w2", "final_norm")


def kernel(x, ffn1_norm, ffn1_w13, ffn1_w2, mix_norm, w_in, conv_w, conv_b, conv_ln_g, conv_ln_b, forget_b, out_norm_conv, out_norm_attn, w_out, ffn2_norm, ffn2_w13, ffn2_w2, final_norm, loss_target, m_ffn1_norm, m_ffn1_w13, m_ffn1_w2, m_mix_norm, m_w_in, m_conv_w, m_conv_b, m_conv_ln_g, m_conv_ln_b, m_forget_b, m_out_norm_conv, m_out_norm_attn, m_w_out, m_ffn2_norm, m_ffn2_w13, m_ffn2_w2, m_final_norm, v_ffn1_norm, v_ffn1_w13, v_ffn1_w2, v_mix_norm, v_w_in, v_conv_w, v_conv_b, v_conv_ln_g, v_conv_ln_b, v_forget_b, v_out_norm_conv, v_out_norm_attn, v_w_out, v_ffn2_norm, v_ffn2_w13, v_ffn2_w2, v_final_norm):
    w = dict(ffn1_norm=ffn1_norm, ffn1_w13=ffn1_w13, ffn1_w2=ffn1_w2, mix_norm=mix_norm, w_in=w_in, conv_w=conv_w,
             conv_b=conv_b, conv_ln_g=conv_ln_g, conv_ln_b=conv_ln_b, forget_b=forget_b, out_norm_conv=out_norm_conv,
             out_norm_attn=out_norm_attn, w_out=w_out, ffn2_norm=ffn2_norm, ffn2_w13=ffn2_w13, ffn2_w2=ffn2_w2,
             final_norm=final_norm)
    m = dict(ffn1_norm=m_ffn1_norm, ffn1_w13=m_ffn1_w13, ffn1_w2=m_ffn1_w2, mix_norm=m_mix_norm, w_in=m_w_in,
             conv_w=m_conv_w, conv_b=m_conv_b, conv_ln_g=m_conv_ln_g, conv_ln_b=m_conv_ln_b, forget_b=m_forget_b,
             out_norm_conv=m_out_norm_conv, out_norm_attn=m_out_norm_attn, w_out=m_w_out, ffn2_norm=m_ffn2_norm,
             ffn2_w13=m_ffn2_w13, ffn2_w2=m_ffn2_w2, final_norm=m_final_norm)
    v = dict(ffn1_norm=v_ffn1_norm, ffn1_w13=v_ffn1_w13, ffn1_w2=v_ffn1_w2, mix_norm=v_mix_norm, w_in=v_w_in,
             conv_w=v_conv_w, conv_b=v_conv_b, conv_ln_g=v_conv_ln_g, conv_ln_b=v_conv_ln_b, forget_b=v_forget_b,
             out_norm_conv=v_out_norm_conv, out_norm_attn=v_out_norm_attn, w_out=v_w_out, ffn2_norm=v_ffn2_norm,
             ffn2_w13=v_ffn2_w13, ffn2_w2=v_ffn2_w2, final_norm=v_final_norm)
    shapes = {n: a.shape for n, a in w.items()}
    T, D = x.shape[1], x.shape[2]
    def two(n, a):
        if a.ndim != 3:
            return a.reshape(1, -1)
        a = a.reshape(a.shape[-2], a.shape[-1])
        return a.T if n in _TRANSPOSED else a

    w2d = {n: two(n, a) for n, a in w.items()}
    m2d = {n: two(n, a) for n, a in m.items()}
    v2d = {n: two(n, a) for n, a in v.items()}

    shard = {n: w2d[n].astype(MXU) for n in _BIG}
    shard["conv_w"] = w2d["conv_w"]
    norms = {n: w2d[n] for n in _SMALL}
    norms["forget_b"] = jnp.pad(w2d["forget_b"], ((0, 0), (0, LANES - N_HEADS)))
    loss_part, dx, gathered_small, big = _local_step(x[0], loss_target[0], norms, shard)
    loss = lax.psum(loss_part, ("x", "y", "c"))

    grads, deltas, new_m, new_v = {}, {}, {}, {}
    for n in _BIG:
        g, d, nm, nv = _adamw(w2d[n], m2d[n], v2d[n], big[n], "adamw_" + n)
        grads[n], deltas[n], new_m[n], new_v[n] = g, d, nm, nv

    small_out, conv_g_full = _adamw_small(gathered_small, w2d, m2d, v2d, "adamw_small")
    for n in _SMALL:
        grads[n], deltas[n], new_m[n], new_v[n] = small_out[n]
    conv_g_full = conv_g_full[:CONV_TAPS]
    xi, yi, ci = _position()
    cw = shapes["conv_w"][-1]
    conv_g_mine = lax.dynamic_slice_in_dim(conv_g_full, _flat(xi, yi, ci) * cw, cw, axis=1)
    g, d, nm, nv = _adamw(w2d["conv_w"], m2d["conv_w"], v2d["conv_w"], conv_g_mine[None], "adamw_conv_w")
    grads["conv_w"], deltas["conv_w"], new_m["conv_w"], new_v["conv_w"] = g, d, nm, nv

    shaped = lambda dct: [(dct[n].T if n in _TRANSPOSED else dct[n]).reshape(shapes[n]) for n in _ORDER]
    return (loss, dx[None], *shaped(grads), *shaped(deltas), *shaped(new_m), *shaped(new_v))
```

```python
from typing import NamedTuple

import jax
import jax.numpy as jnp
from jax import lax
from jax.experimental import pallas as pl
from jax.experimental.pallas import tpu as pltpu
from jax.experimental.pallas import tpu_sc as plsc

F32 = jnp.float32
MXU = jnp.bfloat16
EPS = 1e-6
N_HEADS = 8
HEAD_DIM = 64
D_CONV = 512
D_ATTN = N_HEADS * HEAD_DIM
CONV_TAPS = 31
CONV_HALO = 32
SCALE = HEAD_DIM ** -0.5
NEG = -1e30
LANES = 128
N_DEV = 8
N_CHIPS = N_DEV // 2
MESH = pl.DeviceIdType.MESH
MIB = 1 << 20

ADAM_LR = 0.001
ADAM_B1 = 0.9
ADAM_B2 = 0.999
ADAM_EPS = 1e-08
ADAM_WD = 0.01
ADAM_STEP = 10


_UNREAD = pl.BlockSpec(memory_space=pl.ANY)


def _pallas_call(body, *, out_shape, **kwargs):
    in_hbm = lambda s: pltpu.HBM(s.shape, s.dtype)
    outs = [in_hbm(s) for s in out_shape] if isinstance(out_shape, (list, tuple)) else in_hbm(out_shape)
    call = pl.pallas_call(body, out_shape=outs, **kwargs)
    return lambda *operands: call(*[pltpu.with_memory_space_constraint(a, pltpu.HBM) for a in operands])


def _params(vmem_mib, n_axes):
    return pltpu.CompilerParams(dimension_semantics=("arbitrary",) * n_axes, vmem_limit_bytes=vmem_mib * MIB)


def _mm(a, b):
    return jnp.dot(a, b, preferred_element_type=F32)


def _mm_nt(a, b):
    return lax.dot_general(a, b, (((1,), (1,)), ((), ())), preferred_element_type=F32)


def _mm_tn(a, b):
    return lax.dot_general(a, b, (((0,), (0,)), ((), ())), preferred_element_type=F32)


def _rms_fwd(x, g):
    r = lax.rsqrt(jnp.mean(x * x, axis=-1, keepdims=True) + EPS)
    return x * r * g, r


def _rms_bwd(x, r, g, dy):
    gdy = dy * g
    dx = r * gdy - x * (r * r * r) * jnp.mean(x * gdy, axis=-1, keepdims=True)
    dg = jnp.sum(dy * x * r, axis=0, keepdims=True)
    return dx, dg


def _silu_grad(z, sz):
    return sz * (1.0 + z * (1.0 - sz))


def _three_terms(x):
    x1 = x.astype(jnp.bfloat16)
    r1 = x - x1.astype(F32)
    x2 = r1.astype(jnp.bfloat16)
    x3 = (r1 - x2.astype(F32)).astype(jnp.bfloat16)
    return x1, x2, x3


def _exact_tri_dot(tri, x):
    x1, x2, x3 = _three_terms(x)
    return _mm(tri, x1) + _mm(tri, x2) + _mm(tri, x3)


def _exact_dot_01(x, sel):
    x1, x2, x3 = _three_terms(x)
    return _mm(x1, sel) + _mm(x2, sel) + _mm(x3, sel)


def _tile(n, want):
    t = min(n, want)
    assert n % t == 0
    return t


_FFN_CHUNK = 256


def _ffn_up(x, g, w13, name):
    T, D = x.shape
    _, J, bf, _ = w13.shape
    tm = _tile(T, 512)
    I = T // tm

    def body(x_ref, g_ref, w13_ref, G_ref, U_ref, A_ref, h_s):
        j = pl.program_id(0)
        i = pl.program_id(1)
        rows = pl.ds(pl.multiple_of(i * tm, tm), tm)

        @pl.when(j == 0)
        def _():
            h, _ = _rms_fwd(x_ref[...], g_ref[...])
            h_s[rows, :] = h.astype(MXU)

        chunks = [slice(r0, r0 + _FFN_CHUNK) for r0 in range(0, tm, _FFN_CHUNK)]
        hbs = [h_s[pl.ds(pl.multiple_of(i * tm + rs.start, _FFN_CHUNK), _FFN_CHUNK), :] for rs in chunks]
        GU = [(_mm_nt(hb, w13_ref[0]), _mm_nt(hb, w13_ref[1])) for hb in hbs]
        for rs, (G, U) in zip(chunks, GU):
            G_ref[rs, :] = G.astype(MXU)
            U_ref[rs, :] = U.astype(MXU)
            A_ref[rs, :] = (G * jax.nn.sigmoid(G) * U).astype(MXU)

    blk = pl.BlockSpec((None, tm, bf), lambda j, i: (j, i, 0))
    hid = jax.ShapeDtypeStruct((J, T, bf), MXU)
    return _pallas_call(
        body, name=name, grid=(J, I),
        in_specs=[pl.BlockSpec((tm, D), lambda j, i: (jnp.where(j == 0, i, I - 1), 0)),
                  pl.BlockSpec((1, D), lambda j, i: (0, 0)),
                  pl.BlockSpec((2, None, bf, D), lambda j, i: (0, j, 0, 0))],
        out_specs=[blk, blk, blk],
        out_shape=[hid, hid, hid],
        scratch_shapes=[pltpu.VMEM((T, D), MXU)],
        compiler_params=_params(40, 2),
    )(x, g, w13)


def _ffn_down(x, A, w2, name):
    T, D = x.shape
    J, _, bf = A.shape
    tm = _tile(T, 512)

    def body(x_ref, A_ref, w2_ref, xo_ref):
        f = _mm(A_ref[0], w2_ref[0:bf, :])
        for j in range(1, J):
            f = f + _mm(A_ref[j], w2_ref[j * bf:(j + 1) * bf, :])
        xo_ref[...] = x_ref[...] + 0.5 * f

    row = pl.BlockSpec((tm, D), lambda i: (i, 0))
    return _pallas_call(
        body, name=name, grid=(T // tm,),
        in_specs=[row, pl.BlockSpec((J, tm, bf), lambda i: (0, i, 0)), pl.BlockSpec((J * bf, D), lambda i: (0, 0))],
        out_specs=row,
        out_shape=jax.ShapeDtypeStruct((T, D), F32),
        compiler_params=_params(48, 1),
    )(x, A, w2)


def _ffn_bwd_act(x, g, dy, Gs, Us, w13, w2, name, after=()):
    T, D = x.shape
    _, J, bf, _ = w13.shape
    tm = _tile(T, 512)
    I = T // tm

    def body(x_ref, g_ref, dy_ref, G_ref, U_ref, w13_ref, w2_ref, *rest):
        dx_ref, dg_ref, h_ref, dG_ref, dU_ref, dh_s, dF_s, h_s = rest[len(after):]
        j = pl.program_id(0)
        i = pl.program_id(1)
        rows = pl.ds(pl.multiple_of(i * tm, tm), tm)

        @pl.when(j == 0)
        def _():
            h, _ = _rms_fwd(x_ref[...], g_ref[...])
            hb = h.astype(MXU)
            h_s[rows, :] = hb
            h_ref[...] = hb
            dF_s[rows, :] = (0.5 * dy_ref[...]).astype(MXU)
            dh_s[rows, :] = jnp.zeros((tm, D), F32)

        chunks = [slice(r0, r0 + _FFN_CHUNK) for r0 in range(0, tm, _FFN_CHUNK)]
        crows = [pl.ds(pl.multiple_of(i * tm + rs.start, _FFN_CHUNK), _FFN_CHUNK) for rs in chunks]
        dAs = [_mm_nt(dF_s[cr, :], w2_ref[...]) for cr in crows]
        for rs, cr, dA in zip(chunks, crows, dAs):
            G = G_ref[rs, :].astype(F32)
            U = U_ref[rs, :].astype(F32)
            sg = jax.nn.sigmoid(G)
            s = G * sg
            dUb = (dA * s).astype(MXU)
            dGb = (dA * U * _silu_grad(G, sg)).astype(MXU)
            dG_ref[rs, :] = dGb
            dU_ref[rs, :] = dUb
            dh_s[cr, :] += _mm(dGb, w13_ref[0]) + _mm(dUb, w13_ref[1])

        @pl.when(j == J - 1)
        def _():
            xv = x_ref[...]
            gv = g_ref[...]
            _, r = _rms_fwd(xv, gv)
            dxn, dgp = _rms_bwd(xv, r, gv, dh_s[rows, :])
            dx_ref[...] = dy_ref[...] + dxn

            @pl.when(i == 0)
            def _():
                dg_ref[...] = dgp

            @pl.when(i > 0)
            def _():
                dg_ref[...] += dgp

    ends = lambda j, i: (jnp.where((j == 0) | (j == J - 1), i, I - 1), 0)
    blk = pl.BlockSpec((None, tm, bf), lambda j, i: (j, i, 0))
    hid = jax.ShapeDtypeStruct((J, T, bf), MXU)
    return _pallas_call(
        body, name=name, grid=(J, I),
        in_specs=[pl.BlockSpec((tm, D), ends), pl.BlockSpec((1, D), lambda j, i: (0, 0)), pl.BlockSpec((tm, D), ends),
                  blk, blk, pl.BlockSpec((2, None, bf, D), lambda j, i: (0, j, 0, 0)),
                  pl.BlockSpec((bf, D), lambda j, i: (j, 0))] + [_UNREAD] * len(after),
        out_specs=[pl.BlockSpec((tm, D), lambda j, i: (jnp.where(j == J - 1, i, 0), 0)),
                   pl.BlockSpec((1, D), lambda j, i: (0, 0)),
                   pl.BlockSpec((tm, D), lambda j, i: (jnp.where(j == 0, i, I - 1), 0)), blk, blk],
        out_shape=[jax.ShapeDtypeStruct((T, D), F32), jax.ShapeDtypeStruct((1, D), F32),
                   jax.ShapeDtypeStruct((T, D), MXU), hid, hid],
        scratch_shapes=[pltpu.VMEM((T, D), F32), pltpu.VMEM((T, D), MXU), pltpu.VMEM((T, D), MXU)],
        compiler_params=_params(58, 2),
    )(x, g, dy, Gs, Us, w13, w2, *after)


def _ffn_w13_grad(h, dG, dU, name):
    T, D = h.shape
    J, _, bf = dG.shape

    def body(h_ref, dG_ref, dU_ref, dw13_ref):
        dw13_ref[0] = _mm_tn(dG_ref[...], h_ref[...]).astype(dw13_ref.dtype)
        dw13_ref[1] = _mm_tn(dU_ref[...], h_ref[...]).astype(dw13_ref.dtype)

    blk = pl.BlockSpec((None, T, bf), lambda j: (j, 0, 0))
    return _pallas_call(
        body, name=name, grid=(J,),
        in_specs=[pl.BlockSpec((T, D), lambda j: (0, 0)), blk, blk],
        out_specs=pl.BlockSpec((2, None, bf, D), lambda j: (0, j, 0, 0)),
        out_shape=jax.ShapeDtypeStruct((2, J, bf, D), MXU),
        compiler_params=_params(48, 1),
    )(h, dG, dU)


def _ffn_w2_grad(dy, A, name, after=()):
    T, D = dy.shape
    J, _, bf = A.shape

    def body(dy_ref, A_ref, *rest):
        dw2_ref, dF_s = rest[len(after):]

        @pl.when(pl.program_id(0) == 0)
        def _():
            dF_s[...] = (0.5 * dy_ref[...]).astype(MXU)

        dw2_ref[...] = _mm_tn(A_ref[...], dF_s[...]).astype(dw2_ref.dtype)

    return _pallas_call(
        body, name=name, grid=(J,),
        in_specs=[pl.BlockSpec((T, D), lambda j: (0, 0)), pl.BlockSpec((None, T, bf), lambda j: (j, 0, 0))]
        + [_UNREAD] * len(after),
        out_specs=pl.BlockSpec((bf, D), lambda j: (j, 0)),
        out_shape=jax.ShapeDtypeStruct((J * bf, D), MXU),
        scratch_shapes=[pltpu.VMEM((T, D), MXU)],
        compiler_params=_params(48, 1),
    )(dy, A, *after)


_AG0, _Q0, _K0, _V0, _F0 = 0, 2 * D_CONV, 2 * D_CONV + D_ATTN, 2 * D_CONV + 2 * D_ATTN, 2 * D_CONV + 3 * D_ATTN
N_IN = _F0 + N_HEADS
N_IN_PAD = _F0 + LANES


def _inproj_fwd(x1, gm, winp, name):
    T, D = x1.shape
    tm = _tile(T, 256)

    def body(x_ref, g_ref, w_ref, ag_ref, k_ref, v_ref, qT_ref, kT_ref, vT_ref, fl_ref):
        h, _ = _rms_fwd(x_ref[...], g_ref[...])
        hb = h.astype(MXU)
        ag_ref[...] = _mm_nt(hb, w_ref[_AG0:_Q0, :])
        qT_ref[...] = (_mm_nt(hb, w_ref[_Q0:_K0, :]) * SCALE).T.astype(MXU)
        for c0, ref, refT in ((_K0, k_ref, kT_ref), (_V0, v_ref, vT_ref)):
            y = _mm_nt(hb, w_ref[c0:c0 + D_ATTN, :])
            ref[...] = y.astype(MXU)
            refT[...] = y.T.astype(MXU)
        fl_ref[...] = _mm_nt(hb, w_ref[_F0:N_IN_PAD, :])

    row = lambda w: pl.BlockSpec((tm, w), lambda i: (i, 0))
    col = pl.BlockSpec((D_ATTN, tm), lambda i: (0, i))
    std = jax.ShapeDtypeStruct((T, D_ATTN), MXU)
    trn = jax.ShapeDtypeStruct((D_ATTN, T), MXU)
    return _pallas_call(
        body, name=name, grid=(T // tm,),
        in_specs=[row(D), pl.BlockSpec((1, D), lambda i: (0, 0)), pl.BlockSpec((N_IN_PAD, D), lambda i: (0, 0))],
        out_specs=[row(2 * D_CONV), row(D_ATTN), row(D_ATTN), col, col, col, row(LANES)],
        out_shape=[jax.ShapeDtypeStruct((T, 2 * D_CONV), F32), std, std, trn, trn, trn,
                   jax.ShapeDtypeStruct((T, LANES), F32)],
        compiler_params=_params(40, 1),
    )(x1, gm, winp)


def _inproj_bwd_act(x1, gm, dx2, dag, dqT, dkT, dvT, dfl, winp, name):
    T, D = x1.shape
    tm = _tile(T, 256)

    def body(x_ref, g_ref, dx2_ref, dag_ref, dqT_ref, dkT_ref, dvT_ref, dfl_ref, w_ref, dx1_ref, dg_ref, h_ref):
        i = pl.program_id(0)
        xv = x_ref[...]
        gv = g_ref[...]
        h, r = _rms_fwd(xv, gv)
        h_ref[...] = h.astype(MXU)
        dh = _mm(dag_ref[...], w_ref[_AG0:_Q0, :])
        for c0, ref in ((_Q0, dqT_ref), (_K0, dkT_ref), (_V0, dvT_ref)):
            dh = dh + _mm_tn(ref[...].astype(MXU), w_ref[c0:c0 + D_ATTN, :])
        dh = dh + _mm(dfl_ref[...].astype(MXU), w_ref[_F0:N_IN_PAD, :])
        dxn, dgp = _rms_bwd(xv, r, gv, dh)
        dx1_ref[...] = dx2_ref[...] + dxn

        @pl.when(i == 0)
        def _():
            dg_ref[...] = dgp

        @pl.when(i > 0)
        def _():
            dg_ref[...] += dgp

    row = lambda w: pl.BlockSpec((tm, w), lambda i: (i, 0))
    col = pl.BlockSpec((D_ATTN, tm), lambda i: (0, i))
    full = lambda a, b: pl.BlockSpec((a, b), lambda i: (0, 0))
    return _pallas_call(
        body, name=name, grid=(T // tm,),
        in_specs=[row(D), full(1, D), row(D), row(2 * D_CONV), col, col, col, row(LANES), full(N_IN_PAD, D)],
        out_specs=[row(D), full(1, D), row(D)],
        out_shape=[jax.ShapeDtypeStruct((T, D), F32), jax.ShapeDtypeStruct((1, D), F32),
                   jax.ShapeDtypeStruct((T, D), MXU)],
        compiler_params=_params(40, 1),
    )(x1, gm, dx2, dag, dqT, dkT, dvT, dfl, winp)


def _inproj_bwd_weights(h, dag, dqT, dkT, dvT, dfl, name, after=()):
    T, D = h.shape

    def body(h_ref, dag_ref, dqT_ref, dkT_ref, dvT_ref, dfl_ref, *rest):
        dw_ref = rest[len(after)]
        hb = h_ref[...]
        dw_ref[_AG0:_Q0, :] = _mm_tn(dag_ref[...], hb).astype(dw_ref.dtype)
        for c0, ref in ((_Q0, dqT_ref), (_K0, dkT_ref), (_V0, dvT_ref)):
            dw_ref[c0:c0 + D_ATTN, :] = _mm(ref[...].astype(MXU), hb).astype(dw_ref.dtype)
        dw_ref[_F0:N_IN_PAD, :] = _mm_tn(dfl_ref[...].astype(MXU), hb).astype(dw_ref.dtype)

    vmem = pl.BlockSpec(memory_space=pltpu.VMEM)
    return _pallas_call(
        body, name=name, in_specs=[vmem] * 6 + [_UNREAD] * len(after), out_specs=vmem,
        out_shape=jax.ShapeDtypeStruct((N_IN_PAD, D), MXU),
        compiler_params=pltpu.CompilerParams(vmem_limit_bytes=56 * MIB),
    )(h, dag, dqT, dkT, dvT, dfl, *after)


def _forget_fwd(fl, fbp, name):
    T = fl.shape[0]
    tb = _tile(T, 256)

    def body(fl_ref, fb_ref, cum_ref, cumT_ref):
        ri = lax.broadcasted_iota(jnp.int32, (tb, tb), 0)
        ci = lax.broadcasted_iota(jnp.int32, (tb, tb), 1)
        tri = (ri >= ci).astype(jnp.bfloat16)
        carry = jnp.zeros((1, LANES), F32)
        for b in range(T // tb):
            z = fl_ref[b * tb:(b + 1) * tb, :] + fb_ref[...]
            lf = jnp.minimum(z, 0.0) - jnp.log1p(jnp.exp(-jnp.abs(z)))
            c = _exact_tri_dot(tri, lf) + carry
            cum_ref[b * tb:(b + 1) * tb, :] = c
            carry = c[tb - 1:tb, :]
        cumT_ref[...] = cum_ref[...].T[:N_HEADS, :]

    return _pallas_call(
        body, name=name,
        out_shape=[jax.ShapeDtypeStruct((T, LANES), F32), jax.ShapeDtypeStruct((N_HEADS, T), F32)],
        compiler_params=pltpu.CompilerParams(vmem_limit_bytes=32 * MIB),
    )(fl, fbp)


def _forget_bwd(dcum, fl, fbp, name):
    T = fl.shape[0]
    tb = _tile(T, 256)

    def body(dc_ref, fl_ref, fb_ref, dfl_ref, dfb_ref):
        ri = lax.broadcasted_iota(jnp.int32, (tb, tb), 0)
        ci = lax.broadcasted_iota(jnp.int32, (tb, tb), 1)
        tri = (ri <= ci).astype(jnp.bfloat16)
        carry = jnp.zeros((1, LANES), F32)
        dfb = jnp.zeros((1, LANES), F32)
        for b in reversed(range(T // tb)):
            sl = slice(b * tb, (b + 1) * tb)
            dl = _exact_tri_dot(tri, dc_ref[sl, :]) + carry
            carry = dl[0:1, :]
            z = fl_ref[sl, :] + fb_ref[...]
            dfl = dl * jax.nn.sigmoid(-z)
            dfl_ref[sl, :] = dfl
            dfb = dfb + jnp.sum(dfl, axis=0, keepdims=True)
        dfb_ref[...] = dfb

    return _pallas_call(
        body, name=name,
        out_shape=[jax.ShapeDtypeStruct((T, LANES), F32), jax.ShapeDtypeStruct((1, LANES), F32)],
        compiler_params=pltpu.CompilerParams(vmem_limit_bytes=32 * MIB),
    )(dcum, fl, fbp)


def _causal_keep(i, j, tq, tk):
    key = j * tk + lax.broadcasted_iota(jnp.int32, (tk, tq), 0)
    qry = i * tq + lax.broadcasted_iota(jnp.int32, (tk, tq), 1)
    return key <= qry


def _split_hi_lo(x):
    hi = x.astype(MXU)
    lo = (x - hi.astype(F32)).astype(MXU)
    return hi, lo


def _attn_fwd(qT, k, vT, cum, cumT, name):
    T = k.shape[0]
    tq = _tile(T, 256)
    tk = _tile(tq, 256)
    kpq = tq // tk
    heads = [slice(HEAD_DIM * h, HEAD_DIM * (h + 1)) for h in range(N_HEADS)]

    def body(qT_ref, k_ref, vT_ref, cum_ref, cumT_ref, o_ref, lseT_ref, acc_s, m_s, l_s):
        i = pl.program_id(0)
        acc_s[...] = jnp.zeros_like(acc_s)
        m_s[...] = jnp.full_like(m_s, NEG)
        l_s[...] = jnp.zeros_like(l_s)

        def kblock(j, masked):
            rows = pl.ds(pl.multiple_of(j * tk, tk), tk)
            keep = _causal_keep(i, j, tq, tk) if masked else None
            bias = [cumT_ref[h:h + 1, :] - cum_ref[rows, h:h + 1] for h in range(N_HEADS)]
            qk = [_mm(k_ref[rows, hs], qT_ref[hs, :]) + bias[h] for h, hs in enumerate(heads)]
            for h, hs in enumerate(heads):
                sT = qk[h]
                if masked:
                    sT = jnp.where(keep, sT, NEG)
                m_old = m_s[h:h + 1, :]
                m_new = jnp.maximum(m_old, jnp.max(sT, axis=0, keepdims=True))
                alpha = jnp.exp(m_old - m_new)
                pT = jnp.exp(sT - m_new)
                l_s[h:h + 1, :] = alpha * l_s[h:h + 1, :] + jnp.sum(pT, axis=0, keepdims=True)
                p_hi, p_lo = _split_hi_lo(pT)
                vh = vT_ref[hs, rows]
                acc_s[hs, :] = alpha * acc_s[hs, :] + (_mm(vh, p_hi) + _mm(vh, p_lo))
                m_s[h:h + 1, :] = m_new

        def unmasked(j, c):
            kblock(j, False)
            return c

        lax.fori_loop(0, kpq * i, unmasked, 0)
        for d in range(kpq):
            kblock(kpq * i + d, True)
        for h, hs in enumerate(heads):
            acc_s[hs, :] = acc_s[hs, :] / l_s[h:h + 1, :]
        o_ref[...] = acc_s[...].T
        lseT_ref[...] = m_s[...] + jnp.log(l_s[...])

    full = lambda a, b: pl.BlockSpec((a, b), lambda i: (0, 0))
    colblk = lambda r: pl.BlockSpec((r, tq), lambda i: (0, i))
    return _pallas_call(
        body, name=name, grid=(T // tq,),
        in_specs=[colblk(D_ATTN), full(T, D_ATTN), full(D_ATTN, T), full(T, LANES), colblk(N_HEADS)],
        out_specs=[pl.BlockSpec((tq, D_ATTN), lambda i: (i, 0)), colblk(N_HEADS)],
        out_shape=[jax.ShapeDtypeStruct((T, D_ATTN), F32), jax.ShapeDtypeStruct((N_HEADS, T), F32)],
        scratch_shapes=[pltpu.VMEM((D_ATTN, tq), F32), pltpu.VMEM((N_HEADS, tq), F32),
                        pltpu.VMEM((N_HEADS, tq), F32)],
        compiler_params=_params(40, 1),
    )(qT, k, vT, cum, cumT)


def _attn_bwd(qT, k, kT, v, doT, lseT, deltaT, cum, cumT, name, after=()):
    T = k.shape[0]
    tq = _tile(T, 256)
    tk = _tile(tq, 256)
    kpq = tq // tk
    heads = [slice(HEAD_DIM * h, HEAD_DIM * (h + 1)) for h in range(N_HEADS)]

    def body(qT_ref, k_ref, kT_ref, v_ref, doT_ref, lseT_ref, dlT_ref, cum_ref, cumT_ref, *rest):
        dq_ref, dk_ref, dv_ref, dcum_ref, dq_s = rest[len(after):]
        i = pl.program_id(0)

        @pl.when(i == 0)
        def _():
            dk_ref[...] = jnp.zeros_like(dk_ref)
            dv_ref[...] = jnp.zeros_like(dv_ref)
            dcum_ref[...] = jnp.zeros_like(dcum_ref)

        dq_s[...] = jnp.zeros_like(dq_s)

        def kblock(j, masked):
            rows = pl.ds(pl.multiple_of(j * tk, tk), tk)
            keep = _causal_keep(i, j, tq, tk) if masked else None
            bias = [cumT_ref[h:h + 1, :] - cum_ref[rows, h:h + 1] for h in range(N_HEADS)]
            qk = [_mm(k_ref[rows, hs], qT_ref[hs, :]) + bias[h] for h, hs in enumerate(heads)]
            dps = [_mm(v_ref[rows, hs], doT_ref[hs, :]) for hs in heads]
            for h, hs in enumerate(heads):
                sT = qk[h]
                if masked:
                    sT = jnp.where(keep, sT, NEG)
                pT = jnp.exp(sT - lseT_ref[h:h + 1, :])
                dsT = pT * (dps[h] - dlT_ref[h:h + 1, :])
                dcum_ref[rows, h:h + 1] += -jnp.sum(dsT, axis=1, keepdims=True)
                dsb = dsT.astype(MXU)
                dv_ref[hs, rows] += _mm_nt(doT_ref[hs, :], pT.astype(MXU))
                dk_ref[hs, rows] += _mm_nt(qT_ref[hs, :], dsb)
                dq_s[hs, :] += _mm(kT_ref[hs, rows], dsb)

        def unmasked(j, c):
            kblock(j, False)
            return c

        lax.fori_loop(0, kpq * i, unmasked, 0)
        for d in range(kpq):
            kblock(kpq * i + d, True)
        dq_ref[...] = (dq_s[...] * SCALE).astype(dq_ref.dtype)

    full = lambda a, b: pl.BlockSpec((a, b), lambda i: (0, 0))
    colblk = lambda r: pl.BlockSpec((r, tq), lambda i: (0, i))
    return _pallas_call(
        body, name=name, grid=(T // tq,),
        in_specs=[colblk(D_ATTN), full(T, D_ATTN), full(D_ATTN, T), full(T, D_ATTN), colblk(D_ATTN),
                  colblk(N_HEADS), colblk(N_HEADS), full(T, LANES), colblk(N_HEADS)] + [_UNREAD] * len(after),
        out_specs=[colblk(D_ATTN), full(D_ATTN, T), full(D_ATTN, T), full(T, LANES)],
        out_shape=[
            jax.ShapeDtypeStruct((D_ATTN, T), MXU),
            jax.ShapeDtypeStruct((D_ATTN, T), F32),
            jax.ShapeDtypeStruct((D_ATTN, T), F32),
            jax.ShapeDtypeStruct((T, LANES), F32),
        ],
        scratch_shapes=[pltpu.VMEM((D_ATTN, tq), F32)],
        compiler_params=_params(48, 1),
    )(qT, k, kT, v, doT, lseT, deltaT, cum, cumT, *after)


_ROWS_PER_CHUNK = 64


def _glu_halo(ag_ref, agh_ref, uext_s, tm, first):
    a = ag_ref[:, :D_CONV]
    sg = jax.nn.sigmoid(ag_ref[:, D_CONV:])
    uh = agh_ref[:, :D_CONV] * jax.nn.sigmoid(agh_ref[:, D_CONV:])
    uext_s[0:CONV_HALO, :] = jnp.where(first, 0.0, uh)
    uext_s[CONV_HALO:CONV_HALO + tm, :] = a * sg
    return a, sg


_SUBLANES = 8


def _shifted_copies(ext_s, sh_s, rows):
    for k in range(1, _SUBLANES):
        sh_s[k, 0:rows, :] = ext_s[pl.ds(k, rows), :]


def _window(ext_s, sh_s, start, rows):
    k = start % _SUBLANES
    if k == 0:
        return ext_s[pl.ds(start, rows), :]
    return sh_s[k, pl.ds(start - k, rows), :]


def _layer_norm_stats(y):
    mu = jnp.mean(y, axis=-1, keepdims=True)
    xc = y - mu
    rs = lax.rsqrt(jnp.mean(xc * xc, axis=-1, keepdims=True) + EPS)
    return xc * rs, rs


def _conv_fwd(ag, w32, cb, lg, lb, name):
    T = ag.shape[0]
    tm = _tile(T, 256)
    rc = _tile(tm, _ROWS_PER_CHUNK)
    hb = tm // CONV_HALO

    def body(ag_ref, agh_ref, w_ref, cb_ref, lg_ref, lb_ref, yc_ref, c_ref, uext_s, ush_s):
        i = pl.program_id(0)
        _glu_halo(ag_ref, agh_ref, uext_s, tm, i == 0)
        _shifted_copies(uext_s, ush_s, tm + CONV_HALO - _SUBLANES)
        for r0 in range(0, tm, rc):
            acc = jnp.zeros((rc, D_CONV), F32)
            for t in range(CONV_TAPS):
                acc = acc + _window(uext_s, ush_s, r0 + CONV_HALO - (CONV_TAPS - 1) + t, rc) * w_ref[t:t + 1, :]
            y = acc + cb_ref[...]
            yc_ref[r0:r0 + rc, :] = y
            n, _ = _layer_norm_stats(y)
            z = n * lg_ref[...] + lb_ref[...]
            c_ref[r0:r0 + rc, :] = z * jax.nn.sigmoid(z)

    row = lambda w: pl.BlockSpec((tm, w), lambda i: (i, 0))
    full = lambda a, b: pl.BlockSpec((a, b), lambda i: (0, 0))
    return _pallas_call(
        body, name=name, grid=(T // tm,),
        in_specs=[row(2 * D_CONV),
                  pl.BlockSpec((CONV_HALO, 2 * D_CONV), lambda i: (jnp.maximum(i * hb - 1, 0), 0)),
                  full(CONV_HALO, D_CONV), full(1, D_CONV), full(1, D_CONV), full(1, D_CONV)],
        out_specs=[row(D_CONV), row(D_CONV)],
        out_shape=[jax.ShapeDtypeStruct((T, D_CONV), F32), jax.ShapeDtypeStruct((T, D_CONV), F32)],
        scratch_shapes=[pltpu.VMEM((CONV_HALO + tm, D_CONV), F32),
                        pltpu.VMEM((_SUBLANES, CONV_HALO + tm, D_CONV), F32)],
        compiler_params=_params(32, 1),
    )(ag, ag, w32, cb, lg, lb)


def _conv_bwd(dc, yc, ag, w32, lg, lb, name):
    T = ag.shape[0]
    tm = _tile(T, 256)
    rc = _tile(tm, _ROWS_PER_CHUNK)
    I = T // tm
    hb = tm // CONV_HALO
    n_halo_blocks = T // CONV_HALO

    def body(dc_ref, yc_ref, dch_ref, ych_ref, ag_ref, agh_ref, w_ref, lg_ref, lb_ref,
             dag_ref, dw_ref, dcb_ref, dlg_ref, dlb_ref, uext_s, dext_s, ush_s, dsh_s):
        i = pl.program_id(0)
        lgv = lg_ref[...]
        lbv = lb_ref[...]

        def ln_bwd(dcv, ycv):
            n, rs = _layer_norm_stats(ycv)
            z = n * lgv + lbv
            dz = dcv * _silu_grad(z, jax.nn.sigmoid(z))
            dn = dz * lgv
            dy = rs * (dn - jnp.mean(dn, axis=-1, keepdims=True) - n * jnp.mean(dn * n, axis=-1, keepdims=True))
            return dy, dz, n

        dy, dz, n = ln_bwd(dc_ref[...], yc_ref[...])
        dyh, _, _ = ln_bwd(dch_ref[...], ych_ref[...])
        dext_s[0:tm, :] = dy
        dext_s[tm:tm + CONV_HALO, :] = jnp.where(i == I - 1, 0.0, dyh)
        a, sg = _glu_halo(ag_ref, agh_ref, uext_s, tm, i == 0)
        _shifted_copies(uext_s, ush_s, tm + CONV_HALO - _SUBLANES)
        _shifted_copies(dext_s, dsh_s, tm + CONV_HALO - _SUBLANES)

        @pl.when(i == 0)
        def _():
            dw_ref[...] = jnp.zeros_like(dw_ref)
            dcb_ref[...] = jnp.zeros_like(dcb_ref)
            dlg_ref[...] = jnp.zeros_like(dlg_ref)
            dlb_ref[...] = jnp.zeros_like(dlb_ref)

        dcb_ref[...] += jnp.sum(dy, axis=0, keepdims=True)
        dlg_ref[...] += jnp.sum(dz * n, axis=0, keepdims=True)
        dlb_ref[...] += jnp.sum(dz, axis=0, keepdims=True)
        for t in range(CONV_TAPS):
            u_t = _window(uext_s, ush_s, CONV_HALO - (CONV_TAPS - 1) + t, tm)
            dw_ref[t:t + 1, :] += jnp.sum(dy * u_t, axis=0, keepdims=True)
        for r0 in range(0, tm, rc):
            acc = jnp.zeros((rc, D_CONV), F32)
            for t in range(CONV_TAPS):
                acc = acc + _window(dext_s, dsh_s, r0 + (CONV_TAPS - 1) - t, rc) * w_ref[t:t + 1, :]
            a_c = a[r0:r0 + rc, :]
            sg_c = sg[r0:r0 + rc, :]
            dag_ref[r0:r0 + rc, :D_CONV] = (acc * sg_c).astype(dag_ref.dtype)
            dag_ref[r0:r0 + rc, D_CONV:] = (acc * a_c * sg_c * (1.0 - sg_c)).astype(dag_ref.dtype)

    row = lambda w: pl.BlockSpec((tm, w), lambda i: (i, 0))
    full = lambda a, b: pl.BlockSpec((a, b), lambda i: (0, 0))
    nxt = pl.BlockSpec((CONV_HALO, D_CONV), lambda i: (jnp.minimum((i + 1) * hb, n_halo_blocks - 1), 0))
    return _pallas_call(
        body, name=name, grid=(I,),
        in_specs=[row(D_CONV), row(D_CONV), nxt, nxt, row(2 * D_CONV),
                  pl.BlockSpec((CONV_HALO, 2 * D_CONV), lambda i: (jnp.maximum(i * hb - 1, 0), 0)),
                  full(CONV_HALO, D_CONV), full(1, D_CONV), full(1, D_CONV)],
        out_specs=[row(2 * D_CONV), full(CONV_HALO, D_CONV), full(1, D_CONV), full(1, D_CONV), full(1, D_CONV)],
        out_shape=[
            jax.ShapeDtypeStruct((T, 2 * D_CONV), MXU),
            jax.ShapeDtypeStruct((CONV_HALO, D_CONV), F32),
            jax.ShapeDtypeStruct((1, D_CONV), F32),
            jax.ShapeDtypeStruct((1, D_CONV), F32),
            jax.ShapeDtypeStruct((1, D_CONV), F32),
        ],
        scratch_shapes=[pltpu.VMEM((CONV_HALO + tm, D_CONV), F32), pltpu.VMEM((tm + CONV_HALO, D_CONV), F32),
                        pltpu.VMEM((_SUBLANES, CONV_HALO + tm, D_CONV), F32),
                        pltpu.VMEM((_SUBLANES, CONV_HALO + tm, D_CONV), F32)],
        compiler_params=_params(40, 1),
    )(dc, yc, dc, yc, ag, ag, w32, lg, lb)


def _outproj_fwd(x1, c, o, gc, ga, wout, name):
    T, D = x1.shape
    tm = _tile(T, 512)

    def body(x_ref, c_ref, o_ref, gc_ref, ga_ref, w_ref, x2_ref):
        yc, _ = _rms_fwd(c_ref[...], gc_ref[...])
        ya, _ = _rms_fwd(o_ref[...], ga_ref[...])
        x2_ref[...] = (x_ref[...] + _mm(yc.astype(MXU), w_ref[:D_CONV, :])
                       + _mm(ya.astype(MXU), w_ref[D_CONV:, :]))

    row = lambda w: pl.BlockSpec((tm, w), lambda i: (i, 0))
    full = lambda a, b: pl.BlockSpec((a, b), lambda i: (0, 0))
    return _pallas_call(
        body, name=name, grid=(T // tm,),
        in_specs=[row(D), row(D_CONV), row(D_ATTN), full(1, D_CONV), full(1, D_ATTN), full(D_CONV + D_ATTN, D)],
        out_specs=row(D),
        out_shape=jax.ShapeDtypeStruct((T, D), F32),
        compiler_params=_params(32, 1),
    )(x1, c, o, gc, ga, wout)


def _outproj_bwd(dx2, c, o, gc, ga, wout, name):
    T, D = dx2.shape
    tm = _tile(T, 256)

    def body(dx_ref, c_ref, o_ref, gc_ref, ga_ref, w_ref,
             dc_ref, doT_ref, dlT_ref, dgc_ref, dga_ref, y_ref, dxb_ref):
        i = pl.program_id(0)
        dxb = dx_ref[...].astype(MXU)
        dxb_ref[...] = dxb
        cv = c_ref[...]
        ov = o_ref[...]
        yc, rcn = _rms_fwd(cv, gc_ref[...])
        ya, ra = _rms_fwd(ov, ga_ref[...])
        dyc = _mm_nt(dxb, w_ref[:D_CONV, :])
        dya = _mm_nt(dxb, w_ref[D_CONV:, :])
        y_ref[:, :D_CONV] = yc.astype(MXU)
        y_ref[:, D_CONV:] = ya.astype(MXU)
        dcv, dgc = _rms_bwd(cv, rcn, gc_ref[...], dyc)
        dov, dga = _rms_bwd(ov, ra, ga_ref[...], dya)
        dc_ref[...] = dcv
        dob = dov.astype(doT_ref.dtype)
        doT_ref[...] = dov.T.astype(doT_ref.dtype)
        chan = lax.broadcasted_iota(jnp.int32, (D_ATTN, LANES), 0)
        head = lax.broadcasted_iota(jnp.int32, (D_ATTN, LANES), 1)
        in_head = ((chan >= head * HEAD_DIM) & (chan < (head + 1) * HEAD_DIM)).astype(jnp.bfloat16)
        dlT_ref[...] = _exact_dot_01(dob.astype(F32) * ov, in_head).T[:N_HEADS, :]

        @pl.when(i == 0)
        def _():
            dgc_ref[...] = dgc
            dga_ref[...] = dga

        @pl.when(i > 0)
        def _():
            dgc_ref[...] += dgc
            dga_ref[...] += dga

    row = lambda w: pl.BlockSpec((tm, w), lambda i: (i, 0))
    full = lambda a, b: pl.BlockSpec((a, b), lambda i: (0, 0))
    return _pallas_call(
        body, name=name, grid=(T // tm,),
        in_specs=[row(D), row(D_CONV), row(D_ATTN), full(1, D_CONV), full(1, D_ATTN), full(D_CONV + D_ATTN, D)],
        out_specs=[row(D_CONV), pl.BlockSpec((D_ATTN, tm), lambda i: (0, i)),
                   pl.BlockSpec((N_HEADS, tm), lambda i: (0, i)),
                   full(1, D_CONV), full(1, D_ATTN), row(D_CONV + D_ATTN), row(D)],
        out_shape=[
            jax.ShapeDtypeStruct((T, D_CONV), F32),
            jax.ShapeDtypeStruct((D_ATTN, T), MXU),
            jax.ShapeDtypeStruct((N_HEADS, T), F32),
            jax.ShapeDtypeStruct((1, D_CONV), F32),
            jax.ShapeDtypeStruct((1, D_ATTN), F32),
            jax.ShapeDtypeStruct((T, D_CONV + D_ATTN), MXU),
            jax.ShapeDtypeStruct((T, D), MXU),
        ],
        compiler_params=_params(40, 1),
    )(dx2, c, o, gc, ga, wout)


def _outproj_w_grad(y, dxb, name):
    def body(y_ref, dx_ref, dw_ref):
        dw_ref[...] = _mm_tn(y_ref[...], dx_ref[...]).astype(dw_ref.dtype)

    return _pallas_call(
        body, name=name, out_shape=jax.ShapeDtypeStruct((y.shape[1], dxb.shape[1]), MXU),
        compiler_params=pltpu.CompilerParams(vmem_limit_bytes=40 * MIB),
    )(y, dxb)


def _ffn_down_loss(x, A, w2, gf, target, name):
    T, D = x.shape
    J, _, bf = A.shape
    tm = _tile(T, 512)

    def body(x_ref, A_ref, w2_ref, g_ref, t_ref, loss_ref, dx_ref, dg_ref):
        i = pl.program_id(0)
        f = _mm(A_ref[0], w2_ref[0:bf, :])
        for j in range(1, J):
            f = f + _mm(A_ref[j], w2_ref[j * bf:(j + 1) * bf, :])
        xv = x_ref[...] + 0.5 * f
        gv = g_ref[...]
        out, r = _rms_fwd(xv, gv)
        err = out - t_ref[...]
        part = jnp.full((1, LANES), 0.5 / D, F32) * jnp.sum(err * err)
        dxn, dgp = _rms_bwd(xv, r, gv, err * (1.0 / D))
        dx_ref[...] = dxn

        @pl.when(i == 0)
        def _():
            loss_ref[...] = part
            dg_ref[...] = dgp

        @pl.when(i > 0)
        def _():
            loss_ref[...] += part
            dg_ref[...] += dgp

    row = lambda w: pl.BlockSpec((tm, w), lambda i: (i, 0))
    full = lambda a, b: pl.BlockSpec((a, b), lambda i: (0, 0))
    return _pallas_call(
        body, name=name, grid=(T // tm,),
        in_specs=[row(D), pl.BlockSpec((J, tm, bf), lambda i: (0, i, 0)), full(J * bf, D), full(1, D), row(D)],
        out_specs=[full(1, LANES), row(D), full(1, D)],
        out_shape=[jax.ShapeDtypeStruct((1, LANES), F32), jax.ShapeDtypeStruct((T, D), F32),
                   jax.ShapeDtypeStruct((1, D), F32)],
        compiler_params=_params(56, 1),
    )(x, A, w2, gf, target)


def _row_tile(rows):
    for cand in (256, 176, 128, 64, 32, 16):
        if rows % cand == 0:
            return cand
    return rows


def _adamw(w, m, v, parts, name):
    R, C = w.shape
    P = parts.shape[0]
    tr = _row_tile(R)
    c1 = 1.0 - ADAM_B1 ** ADAM_STEP
    c2 = 1.0 - ADAM_B2 ** ADAM_STEP

    def body(w_ref, m_ref, v_ref, p_ref, g_ref, d_ref, nm_ref, nv_ref):
        g = p_ref[0].astype(F32)
        for s in range(1, P):
            g = g + p_ref[s].astype(F32)
        wv = w_ref[...]
        mn = ADAM_B1 * m_ref[...] + (1.0 - ADAM_B1) * g
        vn = ADAM_B2 * v_ref[...] + (1.0 - ADAM_B2) * (g * g)
        g_ref[...] = g
        nm_ref[...] = mn
        nv_ref[...] = vn
        d_ref[...] = -ADAM_LR * ((mn / c1) / (jnp.sqrt(vn / c2) + ADAM_EPS) + ADAM_WD * wv)

    blk = pl.BlockSpec((tr, C), lambda i: (i, 0))
    out = jax.ShapeDtypeStruct((R, C), F32)
    return _pallas_call(
        body, name=name, grid=(R // tr,),
        in_specs=[blk, blk, blk, pl.BlockSpec((P, tr, C), lambda i: (0, i, 0))],
        out_specs=[blk, blk, blk, blk],
        out_shape=[out, out, out, out],
        compiler_params=_params(32, 1),
    )(w, m, v, parts)


def _position():
    return lax.axis_index("x"), lax.axis_index("y"), lax.axis_index("c")


def _flat(px, py, pc):
    return 4 * px + 2 * py + pc


def _gather_body(ins, outs, send_sems, recv_sems, local_sems, handshake):
    n = len(ins)
    x, y, c = _position()
    me, sibling = (x, y, c), (x, y, 1 - c)
    chips = [(1 - x, y), (x, 1 - y), (1 - x, 1 - y)]
    if handshake:
        _handshake([sibling] + [(*chip, cc) for chip in chips for cc in (c, 1 - c)])

    def copy(a, k, block, to, src=None):
        dst = outs[a].at[_flat(*block)]
        return pltpu.make_async_remote_copy(
            src_ref=dst if src is None else src, dst_ref=dst,
            send_sem=send_sems.at[a, k], recv_sem=recv_sems.at[a, k],
            device_id=to, device_id_type=MESH)

    mine = [pltpu.make_async_copy(ins[a], outs[a].at[_flat(*me)], local_sems.at[a]) for a in range(n)]
    for cp in mine:
        cp.start()
    first = []
    for a in range(n):
        first.append(copy(a, 0, me, sibling, src=ins[a]))
        first += [copy(a, 1 + j, me, (*chip, c), src=ins[a]) for j, chip in enumerate(chips)]
    for cp in first:
        cp.start()
    passed = []
    for a in range(n):
        for j, chip in enumerate(chips):
            copy(a, 1 + j, (*chip, c), me).wait_recv()
            fwd = copy(a, 4 + j, (*chip, c), sibling)
            fwd.start()
            passed.append(fwd)
    for a in range(n):
        copy(a, 0, sibling, me).wait_recv()
        for j, chip in enumerate(chips):
            copy(a, 4 + j, (*chip, 1 - c), me).wait_recv()
    for cp in first + passed:
        cp.wait_send()
    for cp in mine:
        cp.wait()


def _gather_scratch(n):
    return [pltpu.SemaphoreType.DMA((n, 7)), pltpu.SemaphoreType.DMA((n, 7)), pltpu.SemaphoreType.DMA((n,))]


def _all_gather(shards, name):
    n = len(shards)

    def body(*refs):
        _gather_body(refs[:n], refs[n:2 * n], *refs[2 * n:], handshake=False)

    hbm = pl.BlockSpec(memory_space=pltpu.HBM)
    return _pallas_call(
        body, name=name,
        in_specs=[hbm] * n, out_specs=[hbm] * n,
        out_shape=[jax.ShapeDtypeStruct((N_DEV,) + s.shape, s.dtype) for s in shards],
        scratch_shapes=_gather_scratch(n),
    )(*shards)


def _handshake(peers):
    barrier = pltpu.get_barrier_semaphore()
    for peer in peers:
        pl.semaphore_signal(barrier, inc=1, device_id=peer, device_id_type=MESH)
    pl.semaphore_wait(barrier, len(peers))


def _sequencer_call(body, name, collective_id, out_type, scratch_types, operands):
    return pl.kernel(
        body, name=name, out_type=out_type,
        mesh=plsc.ScalarSubcoreMesh(axis_name="sequencer", num_cores=1),
        scratch_types=scratch_types,
        compiler_params=pltpu.CompilerParams(collective_id=collective_id),
    )(*operands)


def _seq_all_gather(shards, name, collective_id, after):
    n = len(shards)

    def body(*refs):
        _gather_body(refs[:n], refs[n + 1:2 * n + 1], *refs[2 * n + 1:], handshake=True)

    return _sequencer_call(
        body, name, collective_id,
        [jax.ShapeDtypeStruct((N_DEV,) + s.shape, s.dtype) for s in shards],
        _gather_scratch(n), list(shards) + [after])


def _seq_to_sibling(parts, name, collective_id, after):
    n = len(parts)

    def body(*refs):
        ins, outs = refs[:n], refs[n + len(after):2 * n + len(after)]
        send_sems, recv_sems = refs[2 * n + len(after):]
        x, y, c = _position()
        sibling = (x, y, 1 - c)
        _handshake([sibling])
        sent = []
        for a in range(n):
            for q in range(N_CHIPS):
                cp = pltpu.make_async_remote_copy(
                    src_ref=ins[a].at[2 * q + (1 - c)], dst_ref=outs[a].at[q],
                    send_sem=send_sems.at[a, q], recv_sem=recv_sems.at[a, q],
                    device_id=sibling, device_id_type=MESH)
                cp.start()
                sent.append(cp)
        for cp in sent:
            cp.wait_recv()
        for cp in sent:
            cp.wait_send()

    return _sequencer_call(
        body, name, collective_id,
        [jax.ShapeDtypeStruct((N_CHIPS,) + p.shape[1:], p.dtype) for p in parts],
        [pltpu.SemaphoreType.DMA((n, N_CHIPS)), pltpu.SemaphoreType.DMA((n, N_CHIPS))],
        list(parts) + list(after))


def _seq_to_chips(partials, name, collective_id, after=()):
    n = len(partials)

    def body(*refs):
        ins, outs = refs[:n], refs[n + len(after):2 * n + len(after)]
        send_sems, recv_sems, local_sems = refs[2 * n + len(after):]
        x, y, c = _position()
        my_chip = 2 * x + y
        chips = [(1 - x, y), (x, 1 - y), (1 - x, 1 - y)]
        _handshake([(*chip, c) for chip in chips])
        mine = [pltpu.make_async_copy(ins[a].at[my_chip], outs[a].at[my_chip], local_sems.at[a]) for a in range(n)]
        for cp in mine:
            cp.start()
        sent = []
        for a in range(n):
            for j, (px, py) in enumerate(chips):
                cp = pltpu.make_async_remote_copy(
                    src_ref=ins[a].at[2 * px + py], dst_ref=outs[a].at[my_chip],
                    send_sem=send_sems.at[a, j], recv_sem=recv_sems.at[a, j],
                    device_id=(px, py, c), device_id_type=MESH)
                cp.start()
                sent.append(cp)
        for cp in sent:
            cp.wait_recv()
        for cp in sent:
            cp.wait_send()
        for cp in mine:
            cp.wait()

    return _sequencer_call(
        body, name, collective_id,
        [jax.ShapeDtypeStruct(p.shape, p.dtype) for p in partials],
        [pltpu.SemaphoreType.DMA((n, 3)), pltpu.SemaphoreType.DMA((n, 3)), pltpu.SemaphoreType.DMA((n,))],
        list(partials) + list(after))


def _pair_add(parts, recvs, name, after=()):
    n = len(parts)
    core = lax.axis_index("c").astype(jnp.int32).reshape(1)

    def body(c_ref, *refs):
        ps, rs, outs = refs[:n], refs[n:2 * n], refs[2 * n + len(after):]
        for p_ref, r_ref, o_ref in zip(ps, rs, outs):
            o_ref[...] = (p_ref[...].astype(F32) + r_ref[...].astype(F32)).astype(o_ref.dtype)

    mine = lambda p: pl.BlockSpec((None,) + p.shape[1:], lambda q, c: (2 * q + c[0], 0, 0))
    blk = lambda p: pl.BlockSpec((None,) + p.shape[1:], lambda q, c: (q, 0, 0))
    return pl.pallas_call(
        body, name=name,
        grid_spec=pltpu.PrefetchScalarGridSpec(
            num_scalar_prefetch=1, grid=(N_CHIPS,),
            in_specs=[mine(p) for p in parts] + [blk(p) for p in parts] + [_UNREAD] * len(after),
            out_specs=[blk(p) for p in parts]),
        out_shape=[pltpu.HBM((N_CHIPS,) + p.shape[1:], p.dtype) for p in parts],
        compiler_params=_params(40, 1),
    )(core, *[pltpu.with_memory_space_constraint(a, pltpu.HBM) for a in (*parts, *recvs, *after)])


class _Reduced(NamedTuple):
    partials: list
    reduced: list


def _blocks(g):
    return g.reshape(N_DEV, -1, g.shape[-1])


def _reduce_scatter(parts, tag, ids, after=(), between=None, add_after=()):
    from_sibling = _seq_to_sibling(parts, "rs_sibling_" + tag, ids[0], after)
    mid = between(from_sibling[0]) if between else ()
    partials = _pair_add(parts, from_sibling, "rs_add_" + tag, add_after)
    return _Reduced(partials, _seq_to_chips(partials, "rs_chips_" + tag, ids[1], mid))


_SMALL = ("ffn1_norm", "mix_norm", "conv_b", "conv_ln_g", "conv_ln_b", "forget_b", "out_norm_conv",
          "out_norm_attn", "ffn2_norm", "final_norm")
_PACK_WIDTH = 2 * D_CONV
_SLOT = dict(ffn1_norm=(0, 0), mix_norm=(1, 0), ffn2_norm=(2, 0), final_norm=(3, 0), conv_b=(4, 0),
             conv_ln_g=(4, D_CONV), conv_ln_b=(5, 0), out_norm_conv=(5, D_CONV), out_norm_attn=(6, 0),
             forget_b=(6, D_CONV))
_CONV_ROW0 = 8
_PACK_ROWS = _CONV_ROW0 + CONV_HALO


def _pack_small(small, name):
    arrays = [small[n] for n in _SMALL] + [small["conv_w"]]

    def body(*refs):
        out = refs[-1]
        out[...] = jnp.zeros_like(out)
        for n, ref in zip(_SMALL, refs):
            row, lane = _SLOT[n]
            out[row:row + 1, lane:lane + ref.shape[1]] = ref[...]
        out[_CONV_ROW0:, :D_CONV] = refs[len(_SMALL)][...]

    return _pallas_call(body, name=name, out_shape=jax.ShapeDtypeStruct((_PACK_ROWS, _PACK_WIDTH), F32))(*arrays)


def _adamw_small(gathered, w, m, v, name):
    c1 = 1.0 - ADAM_B1 ** ADAM_STEP
    c2 = 1.0 - ADAM_B2 ** ADAM_STEP
    k = len(_SMALL)

    def body(g_ref, *refs):
        ws, ms, vs = refs[:k], refs[k:2 * k], refs[2 * k:3 * k]
        outs = refs[3 * k:]
        total = g_ref[0]
        for s in range(1, N_DEV):
            total = total + g_ref[s]
        for i, n in enumerate(_SMALL):
            row, lane = _SLOT[n]
            width = ws[i].shape[1]
            g = total[row:row + 1, lane:lane + width]
            mn = ADAM_B1 * ms[i][...] + (1.0 - ADAM_B1) * g
            vn = ADAM_B2 * vs[i][...] + (1.0 - ADAM_B2) * (g * g)
            o_g, o_d, o_m, o_v = outs[4 * i:4 * i + 4]
            o_g[...] = g
            o_m[...] = mn
            o_v[...] = vn
            o_d[...] = -ADAM_LR * ((mn / c1) / (jnp.sqrt(vn / c2) + ADAM_EPS) + ADAM_WD * ws[i][...])
        outs[4 * k][...] = total[_CONV_ROW0:, :D_CONV]

    shapes = []
    for n in _SMALL:
        shapes += [jax.ShapeDtypeStruct(w[n].shape, F32)] * 4
    shapes.append(jax.ShapeDtypeStruct((CONV_HALO, D_CONV), F32))
    res = _pallas_call(body, name=name, out_shape=shapes)(
        gathered, *[w[n] for n in _SMALL], *[m[n] for n in _SMALL], *[v[n] for n in _SMALL])
    return {n: res[4 * i:4 * i + 4] for i, n in enumerate(_SMALL)}, res[4 * k]


def _local_step(x, target, norms, shard):
    D = x.shape[1]
    J = N_DEV // 2
    as13 = lambda g: g.reshape(2, J, g.shape[1], D)

    (g13_1,) = _all_gather([shard["ffn1_w13"]], "gather_ffn1_w13")
    (g2_1,) = _seq_all_gather([shard["ffn1_w2"]], "gather_ffn1_w2", 10, after=g13_1)
    w13_1 = as13(g13_1)
    G1, U1, A1 = _ffn_up(x, norms["ffn1_norm"], w13_1, "ffn1_up")
    gin, gconv = _seq_all_gather([shard["w_in"], shard["conv_w"]], "gather_mix", 1, after=G1)
    w2_1 = g2_1.reshape(-1, D)
    x1 = _ffn_down(x, A1, w2_1, "ffn1_down")
    gout, g13_2, g2_2 = _seq_all_gather([shard["w_out"], shard["ffn2_w13"], shard["ffn2_w2"]], "gather_ffn2", 2,
                                        after=x1)
    winp = jnp.pad(gin.reshape(N_IN, D), ((0, N_IN_PAD - N_IN), (0, 0)))
    wout = gout.reshape(-1, D)
    conv_w32 = jnp.pad(gconv.transpose(1, 0, 2).reshape(CONV_TAPS, D_CONV), ((0, CONV_HALO - CONV_TAPS), (0, 0)))

    ag, k, v, qT, kT, vT, fl = _inproj_fwd(x1, norms["mix_norm"], winp, "inproj_fwd")
    cum, cumT = _forget_fwd(fl, norms["forget_b"], "forget_fwd")
    yc, c = _conv_fwd(ag, conv_w32, norms["conv_b"], norms["conv_ln_g"], norms["conv_ln_b"], "conv_fwd")
    o, lseT = _attn_fwd(qT, k, vT, cum, cumT, "attn_fwd")
    x2 = _outproj_fwd(x1, c, o, norms["out_norm_conv"], norms["out_norm_attn"], wout, "outproj_fwd")
    w13_2, w2_2 = as13(g13_2), g2_2.reshape(-1, D)
    G2, U2, A2 = _ffn_up(x2, norms["ffn2_norm"], w13_2, "ffn2_up")
    loss, dx3, d_final = _ffn_down_loss(x2, A2, w2_2, norms["final_norm"], target, "ffn2_down_loss")

    dw2_2 = _ffn_w2_grad(dx3, A2, "ffn2_w2_grad")
    dx2, d_ffn2n, h3, dG2, dU2 = _ffn_bwd_act(x2, norms["ffn2_norm"], dx3, G2, U2, w13_2, w2_2, "ffn2_bwd_act")
    dw13_2 = _ffn_w13_grad(h3, dG2, dU2, "ffn2_w13_grad")
    dc, dobT, deltaT, d_onc, d_ona, y_mix, dx2b = _outproj_bwd(
        dx2, c, o, norms["out_norm_conv"], norms["out_norm_attn"], wout, "outproj_bwd")
    dwout = _outproj_w_grad(y_mix, dx2b, "outproj_w_grad")
    red_ffn2 = _reduce_scatter([_blocks(dw13_2), _blocks(dw2_2)], "ffn2", (3, 4), add_after=(dc,))
    dqT, dkT, dvT, dcum = _attn_bwd(qT, k, kT, v, dobT, lseT, deltaT, cum, cumT, "attn_bwd",
                                    after=red_ffn2.partials)
    dfl, d_fb = _forget_bwd(dcum, fl, norms["forget_b"], "forget_bwd")
    dag, d_convw, d_cb, d_lg, d_lb = _conv_bwd(dc, yc, ag, conv_w32, norms["conv_ln_g"], norms["conv_ln_b"], "conv_bwd")
    dx1, d_mixn, h2 = _inproj_bwd_act(x1, norms["mix_norm"], dx2, dag, dqT, dkT, dvT, dfl, winp, "inproj_bwd_act")
    dw2_1 = _ffn_w2_grad(dx1, A1, "ffn1_w2_grad")
    early = [_blocks(dwout), _blocks(dw2_1)]
    sib_early = _seq_to_sibling(early, "rs_sibling_mix_early", 11, red_ffn2.reduced[:1])
    dwinp = _inproj_bwd_weights(h2, dag, dqT, dkT, dvT, dfl, "inproj_bwd_weights")
    dwin_blocks = dwinp[:N_IN].reshape(N_DEV, N_IN // N_DEV, -1)
    sib_w_in = _seq_to_sibling([dwin_blocks], "rs_sibling_mix", 5, sib_early[:1])
    mix_partials = _pair_add([dwin_blocks] + early, sib_w_in + sib_early, "rs_add_mix")
    red_mix = _Reduced(mix_partials, _seq_to_chips(mix_partials, "rs_chips_mix", 6))
    dx, d_ffn1n, h1, dG1, dU1 = _ffn_bwd_act(x, norms["ffn1_norm"], dx1, G1, U1, w13_1, w2_1, "ffn1_bwd_act",
                                             after=red_mix.partials)
    dw13_1 = _ffn_w13_grad(h1, dG1, dU1, "ffn1_w13_grad")

    small = dict(ffn1_norm=d_ffn1n, mix_norm=d_mixn, conv_b=d_cb, conv_ln_g=d_lg, conv_ln_b=d_lb,
                 forget_b=d_fb, out_norm_conv=d_onc, out_norm_attn=d_ona, ffn2_norm=d_ffn2n,
                 final_norm=d_final, conv_w=d_convw)
    packed_small = _pack_small(small, "pack_small_grads")
    gathered_small = []

    def gather_small(behind):
        gathered_small.extend(_seq_all_gather([packed_small], "gather_small_grads", 9, after=behind))
        return gathered_small

    red_w13_1 = _reduce_scatter([_blocks(dw13_1)], "ffn1_w13", (7, 8), after=red_mix.reduced[:1],
                                between=gather_small)
    big = dict(ffn1_w13=red_w13_1.reduced[0], ffn1_w2=red_mix.reduced[2], w_in=red_mix.reduced[0],
               w_out=red_mix.reduced[1], ffn2_w13=red_ffn2.reduced[0], ffn2_w2=red_ffn2.reduced[1])
    return loss[0, 0], dx, gathered_small[0], big


_BIG = ("ffn1_w13", "ffn1_w2", "w_in", "w_out", "ffn2_w13", "ffn2_w2")
_TRANSPOSED = ("ffn1_w13", "ffn2_w13", "w_in")
_ORDER = ("ffn1_norm", "ffn1_w13", "ffn1_w2", "mix_norm", "w_in", "conv_w", "conv_b", "conv_ln_g", "conv_ln_b",
          "forget_b", "out_norm_conv", "out_norm_attn", "w_out", "ffn2_norm", "ffn2_w13", "ffn2_w2", "final_norm")


def kernel(x, ffn1_norm, ffn1_w13, ffn1_w2, mix_norm, w_in, conv_w, conv_b, conv_ln_g, conv_ln_b, forget_b, out_norm_conv, out_norm_attn, w_out, ffn2_norm, ffn2_w13, ffn2_w2, final_norm, loss_target, m_ffn1_norm, m_ffn1_w13, m_ffn1_w2, m_mix_norm, m_w_in, m_conv_w, m_conv_b, m_conv_ln_g, m_conv_ln_b, m_forget_b, m_out_norm_conv, m_out_norm_attn, m_w_out, m_ffn2_norm, m_ffn2_w13, m_ffn2_w2, m_final_norm, v_ffn1_norm, v_ffn1_w13, v_ffn1_w2, v_mix_norm, v_w_in, v_conv_w, v_conv_b, v_conv_ln_g, v_conv_ln_b, v_forget_b, v_out_norm_conv, v_out_norm_attn, v_w_out, v_ffn2_norm, v_ffn2_w13, v_ffn2_w2, v_final_norm):
    w = dict(ffn1_norm=ffn1_norm, ffn1_w13=ffn1_w13, ffn1_w2=ffn1_w2, mix_norm=mix_norm, w_in=w_in, conv_w=conv_w,
             conv_b=conv_b, conv_ln_g=conv_ln_g, conv_ln_b=conv_ln_b, forget_b=forget_b, out_norm_conv=out_norm_conv,
             out_norm_attn=out_norm_attn, w_out=w_out, ffn2_norm=ffn2_norm, ffn2_w13=ffn2_w13, ffn2_w2=ffn2_w2,
             final_norm=final_norm)
    m = dict(ffn1_norm=m_ffn1_norm, ffn1_w13=m_ffn1_w13, ffn1_w2=m_ffn1_w2, mix_norm=m_mix_norm, w_in=m_w_in,
             conv_w=m_conv_w, conv_b=m_conv_b, conv_ln_g=m_conv_ln_g, conv_ln_b=m_conv_ln_b, forget_b=m_forget_b,
             out_norm_conv=m_out_norm_conv, out_norm_attn=m_out_norm_attn, w_out=m_w_out, ffn2_norm=m_ffn2_norm,
             ffn2_w13=m_ffn2_w13, ffn2_w2=m_ffn2_w2, final_norm=m_final_norm)
    v = dict(ffn1_norm=v_ffn1_norm, ffn1_w13=v_ffn1_w13, ffn1_w2=v_ffn1_w2, mix_norm=v_mix_norm, w_in=v_w_in,
             conv_w=v_conv_w, conv_b=v_conv_b, conv_ln_g=v_conv_ln_g, conv_ln_b=v_conv_ln_b, forget_b=v_forget_b,
             out_norm_conv=v_out_norm_conv, out_norm_attn=v_out_norm_attn, w_out=v_w_out, ffn2_norm=v_ffn2_norm,
             ffn2_w13=v_ffn2_w13, ffn2_w2=v_ffn2_w2, final_norm=v_final_norm)
    shapes = {n: a.shape for n, a in w.items()}
    T, D = x.shape[1], x.shape[2]
    def two(n, a):
        if a.ndim != 3:
            return a.reshape(1, -1)
        a = a.reshape(a.shape[-2], a.shape[-1])
        return a.T if n in _TRANSPOSED else a

    w2d = {n: two(n, a) for n, a in w.items()}
    m2d = {n: two(n, a) for n, a in m.items()}
    v2d = {n: two(n, a) for n, a in v.items()}

    shard = {n: w2d[n].astype(MXU) for n in _BIG}
    shard["conv_w"] = w2d["conv_w"]
    norms = {n: w2d[n] for n in _SMALL}
    norms["forget_b"] = jnp.pad(w2d["forget_b"], ((0, 0), (0, LANES - N_HEADS)))
    loss_part, dx, gathered_small, big = _local_step(x[0], loss_target[0], norms, shard)
    loss = lax.psum(loss_part, ("x", "y", "c"))

    grads, deltas, new_m, new_v = {}, {}, {}, {}
    for n in _BIG:
        g, d, nm, nv = _adamw(w2d[n], m2d[n], v2d[n], big[n], "adamw_" + n)
        grads[n], deltas[n], new_m[n], new_v[n] = g, d, nm, nv

    small_out, conv_g_full = _adamw_small(gathered_small, w2d, m2d, v2d, "adamw_small")
    for n in _SMALL:
        grads[n], deltas[n], new_m[n], new_v[n] = small_out[n]
    conv_g_full = conv_g_full[:CONV_TAPS]
    xi, yi, ci = _position()
    cw = shapes["conv_w"][-1]
    conv_g_mine = lax.dynamic_slice_in_dim(conv_g_full, _flat(xi, yi, ci) * cw, cw, axis=1)
    g, d, nm, nv = _adamw(w2d["conv_w"], m2d["conv_w"], v2d["conv_w"], conv_g_mine[None], "adamw_conv_w")
    grads["conv_w"], deltas["conv_w"], new_m["conv_w"], new_v["conv_w"] = g, d, nm, nv

    shaped = lambda dct: [(dct[n].T if n in _TRANSPOSED else dct[n]).reshape(shapes[n]) for n in _ORDER]
    return (loss, dx[None], *shaped(grads), *shaped(deltas), *shaped(new_m), *shaped(new_v))
```

```python
from typing import NamedTuple

import jax
import jax.numpy as jnp
from jax import lax
from jax.experimental import pallas as pl
from jax.experimental.pallas import tpu as pltpu
from jax.experimental.pallas import tpu_sc as plsc

F32 = jnp.float32
MXU = jnp.bfloat16
EPS = 1e-6
N_HEADS = 8
HEAD_DIM = 64
D_CONV = 512
D_ATTN = N_HEADS * HEAD_DIM
CONV_TAPS = 31
CONV_HALO = 32
SCALE = HEAD_DIM ** -0.5
NEG = -1e30
LANES = 128
N_DEV = 8
N_CHIPS = N_DEV // 2
MESH = pl.DeviceIdType.MESH
MIB = 1 << 20

ADAM_LR = 0.001
ADAM_B1 = 0.9
ADAM_B2 = 0.999
ADAM_EPS = 1e-08
ADAM_WD = 0.01
ADAM_STEP = 10


_UNREAD = pl.BlockSpec(memory_space=pl.ANY)


def _pallas_call(body, *, out_shape, **kwargs):
    in_hbm = lambda s: pltpu.HBM(s.shape, s.dtype)
    outs = [in_hbm(s) for s in out_shape] if isinstance(out_shape, (list, tuple)) else in_hbm(out_shape)
    call = pl.pallas_call(body, out_shape=outs, **kwargs)
    return lambda *operands: call(*[pltpu.with_memory_space_constraint(a, pltpu.HBM) for a in operands])


def _params(vmem_mib, n_axes):
    return pltpu.CompilerParams(dimension_semantics=("arbitrary",) * n_axes, vmem_limit_bytes=vmem_mib * MIB)


def _mm(a, b):
    return jnp.dot(a, b, preferred_element_type=F32)


def _mm_nt(a, b):
    return lax.dot_general(a, b, (((1,), (1,)), ((), ())), preferred_element_type=F32)


def _mm_tn(a, b):
    return lax.dot_general(a, b, (((0,), (0,)), ((), ())), preferred_element_type=F32)


def _rms_fwd(x, g):
    r = lax.rsqrt(jnp.mean(x * x, axis=-1, keepdims=True) + EPS)
    return x * r * g, r


def _rms_bwd(x, r, g, dy):
    gdy = dy * g
    dx = r * gdy - x * (r * r * r) * jnp.mean(x * gdy, axis=-1, keepdims=True)
    dg = jnp.sum(dy * x * r, axis=0, keepdims=True)
    return dx, dg


def _silu_grad(z, sz):
    return sz * (1.0 + z * (1.0 - sz))


def _three_terms(x):
    x1 = x.astype(jnp.bfloat16)
    r1 = x - x1.astype(F32)
    x2 = r1.astype(jnp.bfloat16)
    x3 = (r1 - x2.astype(F32)).astype(jnp.bfloat16)
    return x1, x2, x3


def _exact_tri_dot(tri, x):
    x1, x2, x3 = _three_terms(x)
    return _mm(tri, x1) + _mm(tri, x2) + _mm(tri, x3)


def _exact_dot_01(x, sel):
    x1, x2, x3 = _three_terms(x)
    return _mm(x1, sel) + _mm(x2, sel) + _mm(x3, sel)


def _tile(n, want):
    t = min(n, want)
    assert n % t == 0
    return t


_FFN_CHUNK = 256


def _ffn_up(x, g, w13, name):
    T, D = x.shape
    _, J, bf, _ = w13.shape
    tm = _tile(T, 512)
    I = T // tm

    def body(x_ref, g_ref, w13_ref, G_ref, U_ref, A_ref, h_s):
        j = pl.program_id(0)
        i = pl.program_id(1)
        rows = pl.ds(pl.multiple_of(i * tm, tm), tm)

        @pl.when(j == 0)
        def _():
            h, _ = _rms_fwd(x_ref[...], g_ref[...])
            h_s[rows, :] = h.astype(MXU)

        chunks = [slice(r0, r0 + _FFN_CHUNK) for r0 in range(0, tm, _FFN_CHUNK)]
        hbs = [h_s[pl.ds(pl.multiple_of(i * tm + rs.start, _FFN_CHUNK), _FFN_CHUNK), :] for rs in chunks]
        GU = [(_mm_nt(hb, w13_ref[0]), _mm_nt(hb, w13_ref[1])) for hb in hbs]
        for rs, (G, U) in zip(chunks, GU):
            G_ref[rs, :] = G.astype(MXU)
            U_ref[rs, :] = U.astype(MXU)
            A_ref[rs, :] = (G * jax.nn.sigmoid(G) * U).astype(MXU)

    blk = pl.BlockSpec((None, tm, bf), lambda j, i: (j, i, 0))
    hid = jax.ShapeDtypeStruct((J, T, bf), MXU)
    return _pallas_call(
        body, name=name, grid=(J, I),
        in_specs=[pl.BlockSpec((tm, D), lambda j, i: (jnp.where(j == 0, i, I - 1), 0)),
                  pl.BlockSpec((1, D), lambda j, i: (0, 0)),
                  pl.BlockSpec((2, None, bf, D), lambda j, i: (0, j, 0, 0))],
        out_specs=[blk, blk, blk],
        out_shape=[hid, hid, hid],
        scratch_shapes=[pltpu.VMEM((T, D), MXU)],
        compiler_params=_params(40, 2),
    )(x, g, w13)


def _ffn_down(x, A, w2, name):
    T, D = x.shape
    J, _, bf = A.shape
    tm = _tile(T, 512)

    def body(x_ref, A_ref, w2_ref, xo_ref):
        f = _mm(A_ref[0], w2_ref[0:bf, :])
        for j in range(1, J):
            f = f + _mm(A_ref[j], w2_ref[j * bf:(j + 1) * bf, :])
        xo_ref[...] = x_ref[...] + 0.5 * f

    row = pl.BlockSpec((tm, D), lambda i: (i, 0))
    return _pallas_call(
        body, name=name, grid=(T // tm,),
        in_specs=[row, pl.BlockSpec((J, tm, bf), lambda i: (0, i, 0)), pl.BlockSpec((J * bf, D), lambda i: (0, 0))],
        out_specs=row,
        out_shape=jax.ShapeDtypeStruct((T, D), F32),
        compiler_params=_params(48, 1),
    )(x, A, w2)


def _ffn_bwd_act(x, g, dy, Gs, Us, w13, w2, name, after=()):
    T, D = x.shape
    _, J, bf, _ = w13.shape
    tm = _tile(T, 512)
    I = T // tm

    def body(x_ref, g_ref, dy_ref, G_ref, U_ref, w13_ref, w2_ref, *rest):
        dx_ref, dg_ref, h_ref, dG_ref, dU_ref, dh_s, dF_s, h_s = rest[len(after):]
        j = pl.program_id(0)
        i = pl.program_id(1)
        rows = pl.ds(pl.multiple_of(i * tm, tm), tm)

        @pl.when(j == 0)
        def _():
            h, _ = _rms_fwd(x_ref[...], g_ref[...])
            hb = h.astype(MXU)
            h_s[rows, :] = hb
            h_ref[...] = hb
            dF_s[rows, :] = (0.5 * dy_ref[...]).astype(MXU)
            dh_s[rows, :] = jnp.zeros((tm, D), F32)

        chunks = [slice(r0, r0 + _FFN_CHUNK) for r0 in range(0, tm, _FFN_CHUNK)]
        crows = [pl.ds(pl.multiple_of(i * tm + rs.start, _FFN_CHUNK), _FFN_CHUNK) for rs in chunks]
        dAs = [_mm_nt(dF_s[cr, :], w2_ref[...]) for cr in crows]
        for rs, cr, dA in zip(chunks, crows, dAs):
            G = G_ref[rs, :].astype(F32)
            U = U_ref[rs, :].astype(F32)
            sg = jax.nn.sigmoid(G)
            s = G * sg
            dUb = (dA * s).astype(MXU)
            dGb = (dA * U * _silu_grad(G, sg)).astype(MXU)
            dG_ref[rs, :] = dGb
            dU_ref[rs, :] = dUb
            dh_s[cr, :] += _mm(dGb, w13_ref[0]) + _mm(dUb, w13_ref[1])

        @pl.when(j == J - 1)
        def _():
            xv = x_ref[...]
            gv = g_ref[...]
            _, r = _rms_fwd(xv, gv)
            dxn, dgp = _rms_bwd(xv, r, gv, dh_s[rows, :])
            dx_ref[...] = dy_ref[...] + dxn

            @pl.when(i == 0)
            def _():
                dg_ref[...] = dgp

            @pl.when(i > 0)
            def _():
                dg_ref[...] += dgp

    ends = lambda j, i: (jnp.where((j == 0) | (j == J - 1), i, I - 1), 0)
    blk = pl.BlockSpec((None, tm, bf), lambda j, i: (j, i, 0))
    hid = jax.ShapeDtypeStruct((J, T, bf), MXU)
    return _pallas_call(
        body, name=name, grid=(J, I),
        in_specs=[pl.BlockSpec((tm, D), ends), pl.BlockSpec((1, D), lambda j, i: (0, 0)), pl.BlockSpec((tm, D), ends),
                  blk, blk, pl.BlockSpec((2, None, bf, D), lambda j, i: (0, j, 0, 0)),
                  pl.BlockSpec((bf, D), lambda j, i: (j, 0))] + [_UNREAD] * len(after),
        out_specs=[pl.BlockSpec((tm, D), lambda j, i: (jnp.where(j == J - 1, i, 0), 0)),
                   pl.BlockSpec((1, D), lambda j, i: (0, 0)),
                   pl.BlockSpec((tm, D), lambda j, i: (jnp.where(j == 0, i, I - 1), 0)), blk, blk],
        out_shape=[jax.ShapeDtypeStruct((T, D), F32), jax.ShapeDtypeStruct((1, D), F32),
                   jax.ShapeDtypeStruct((T, D), MXU), hid, hid],
        scratch_shapes=[pltpu.VMEM((T, D), F32), pltpu.VMEM((T, D), MXU), pltpu.VMEM((T, D), MXU)],
        compiler_params=_params(58, 2),
    )(x, g, dy, Gs, Us, w13, w2, *after)


def _ffn_w13_grad(h, dG, dU, name):
    T, D = h.shape
    J, _, bf = dG.shape

    def body(h_ref, dG_ref, dU_ref, dw13_ref):
        dw13_ref[0] = _mm_tn(dG_ref[...], h_ref[...]).astype(dw13_ref.dtype)
        dw13_ref[1] = _mm_tn(dU_ref[...], h_ref[...]).astype(dw13_ref.dtype)

    blk = pl.BlockSpec((None, T, bf), lambda j: (j, 0, 0))
    return _pallas_call(
        body, name=name, grid=(J,),
        in_specs=[pl.BlockSpec((T, D), lambda j: (0, 0)), blk, blk],
        out_specs=pl.BlockSpec((2, None, bf, D), lambda j: (0, j, 0, 0)),
        out_shape=jax.ShapeDtypeStruct((2, J, bf, D), MXU),
        compiler_params=_params(48, 1),
    )(h, dG, dU)


def _ffn_w2_grad(dy, A, name, after=()):
    T, D = dy.shape
    J, _, bf = A.shape

    def body(dy_ref, A_ref, *rest):
        dw2_ref, dF_s = rest[len(after):]

        @pl.when(pl.program_id(0) == 0)
        def _():
            dF_s[...] = (0.5 * dy_ref[...]).astype(MXU)

        dw2_ref[...] = _mm_tn(A_ref[...], dF_s[...]).astype(dw2_ref.dtype)

    return _pallas_call(
        body, name=name, grid=(J,),
        in_specs=[pl.BlockSpec((T, D), lambda j: (0, 0)), pl.BlockSpec((None, T, bf), lambda j: (j, 0, 0))]
        + [_UNREAD] * len(after),
        out_specs=pl.BlockSpec((bf, D), lambda j: (j, 0)),
        out_shape=jax.ShapeDtypeStruct((J * bf, D), MXU),
        scratch_shapes=[pltpu.VMEM((T, D), MXU)],
        compiler_params=_params(48, 1),
    )(dy, A, *after)


_AG0, _Q0, _K0, _V0, _F0 = 0, 2 * D_CONV, 2 * D_CONV + D_ATTN, 2 * D_CONV + 2 * D_ATTN, 2 * D_CONV + 3 * D_ATTN
N_IN = _F0 + N_HEADS
N_IN_PAD = _F0 + LANES


def _inproj_fwd(x1, gm, winp, name):
    T, D = x1.shape
    tm = _tile(T, 256)

    def body(x_ref, g_ref, w_ref, ag_ref, k_ref, v_ref, qT_ref, kT_ref, vT_ref, fl_ref):
        h, _ = _rms_fwd(x_ref[...], g_ref[...])
        hb = h.astype(MXU)
        ag_ref[...] = _mm_nt(hb, w_ref[_AG0:_Q0, :])
        qT_ref[...] = (_mm_nt(hb, w_ref[_Q0:_K0, :]) * SCALE).T.astype(MXU)
        for c0, ref, refT in ((_K0, k_ref, kT_ref), (_V0, v_ref, vT_ref)):
            y = _mm_nt(hb, w_ref[c0:c0 + D_ATTN, :])
            ref[...] = y.astype(MXU)
            refT[...] = y.T.astype(MXU)
        fl_ref[...] = _mm_nt(hb, w_ref[_F0:N_IN_PAD, :])

    row = lambda w: pl.BlockSpec((tm, w), lambda i: (i, 0))
    col = pl.BlockSpec((D_ATTN, tm), lambda i: (0, i))
    std = jax.ShapeDtypeStruct((T, D_ATTN), MXU)
    trn = jax.ShapeDtypeStruct((D_ATTN, T), MXU)
    return _pallas_call(
        body, name=name, grid=(T // tm,),
        in_specs=[row(D), pl.BlockSpec((1, D), lambda i: (0, 0)), pl.BlockSpec((N_IN_PAD, D), lambda i: (0, 0))],
        out_specs=[row(2 * D_CONV), row(D_ATTN), row(D_ATTN), col, col, col, row(LANES)],
        out_shape=[jax.ShapeDtypeStruct((T, 2 * D_CONV), F32), std, std, trn, trn, trn,
                   jax.ShapeDtypeStruct((T, LANES), F32)],
        compiler_params=_params(40, 1),
    )(x1, gm, winp)


def _inproj_bwd_act(x1, gm, dx2, dag, dqT, dkT, dvT, dfl, winp, name):
    T, D = x1.shape
    tm = _tile(T, 256)

    def body(x_ref, g_ref, dx2_ref, dag_ref, dqT_ref, dkT_ref, dvT_ref, dfl_ref, w_ref, dx1_ref, dg_ref, h_ref):
        i = pl.program_id(0)
        xv = x_ref[...]
        gv = g_ref[...]
        h, r = _rms_fwd(xv, gv)
        h_ref[...] = h.astype(MXU)
        dh = _mm(dag_ref[...], w_ref[_AG0:_Q0, :])
        for c0, ref in ((_Q0, dqT_ref), (_K0, dkT_ref), (_V0, dvT_ref)):
            dh = dh + _mm_tn(ref[...].astype(MXU), w_ref[c0:c0 + D_ATTN, :])
        dh = dh + _mm(dfl_ref[...].astype(MXU), w_ref[_F0:N_IN_PAD, :])
        dxn, dgp = _rms_bwd(xv, r, gv, dh)
        dx1_ref[...] = dx2_ref[...] + dxn

        @pl.when(i == 0)
        def _():
            dg_ref[...] = dgp

        @pl.when(i > 0)
        def _():
            dg_ref[...] += dgp

    row = lambda w: pl.BlockSpec((tm, w), lambda i: (i, 0))
    col = pl.BlockSpec((D_ATTN, tm), lambda i: (0, i))
    full = lambda a, b: pl.BlockSpec((a, b), lambda i: (0, 0))
    return _pallas_call(
        body, name=name, grid=(T // tm,),
        in_specs=[row(D), full(1, D), row(D), row(2 * D_CONV), col, col, col, row(LANES), full(N_IN_PAD, D)],
        out_specs=[row(D), full(1, D), row(D)],
        out_shape=[jax.ShapeDtypeStruct((T, D), F32), jax.ShapeDtypeStruct((1, D), F32),
                   jax.ShapeDtypeStruct((T, D), MXU)],
        compiler_params=_params(40, 1),
    )(x1, gm, dx2, dag, dqT, dkT, dvT, dfl, winp)


def _inproj_bwd_weights(h, dag, dqT, dkT, dvT, dfl, name, after=()):
    T, D = h.shape

    def body(h_ref, dag_ref, dqT_ref, dkT_ref, dvT_ref, dfl_ref, *rest):
        dw_ref = rest[len(after)]
        hb = h_ref[...]
        dw_ref[_AG0:_Q0, :] = _mm_tn(dag_ref[...], hb).astype(dw_ref.dtype)
        for c0, ref in ((_Q0, dqT_ref), (_K0, dkT_ref), (_V0, dvT_ref)):
            dw_ref[c0:c0 + D_ATTN, :] = _mm(ref[...].astype(MXU), hb).astype(dw_ref.dtype)
        dw_ref[_F0:N_IN_PAD, :] = _mm_tn(dfl_ref[...].astype(MXU), hb).astype(dw_ref.dtype)

    vmem = pl.BlockSpec(memory_space=pltpu.VMEM)
    return _pallas_call(
        body, name=name, in_specs=[vmem] * 6 + [_UNREAD] * len(after), out_specs=vmem,
        out_shape=jax.ShapeDtypeStruct((N_IN_PAD, D), MXU),
        compiler_params=pltpu.CompilerParams(vmem_limit_bytes=56 * MIB),
    )(h, dag, dqT, dkT, dvT, dfl, *after)


def _forget_fwd(fl, fbp, name):
    T = fl.shape[0]
    tb = _tile(T, 256)

    def body(fl_ref, fb_ref, cum_ref, cumT_ref):
        ri = lax.broadcasted_iota(jnp.int32, (tb, tb), 0)
        ci = lax.broadcasted_iota(jnp.int32, (tb, tb), 1)
        tri = (ri >= ci).astype(jnp.bfloat16)
        carry = jnp.zeros((1, LANES), F32)
        for b in range(T // tb):
            z = fl_ref[b * tb:(b + 1) * tb, :] + fb_ref[...]
            lf = jnp.minimum(z, 0.0) - jnp.log1p(jnp.exp(-jnp.abs(z)))
            c = _exact_tri_dot(tri, lf) + carry
            cum_ref[b * tb:(b + 1) * tb, :] = c
            carry = c[tb - 1:tb, :]
        cumT_ref[...] = cum_ref[...].T[:N_HEADS, :]

    return _pallas_call(
        body, name=name,
        out_shape=[jax.ShapeDtypeStruct((T, LANES), F32), jax.ShapeDtypeStruct((N_HEADS, T), F32)],
        compiler_params=pltpu.CompilerParams(vmem_limit_bytes=32 * MIB),
    )(fl, fbp)


def _forget_bwd(dcum, fl, fbp, name):
    T = fl.shape[0]
    tb = _tile(T, 256)

    def body(dc_ref, fl_ref, fb_ref, dfl_ref, dfb_ref):
        ri = lax.broadcasted_iota(jnp.int32, (tb, tb), 0)
        ci = lax.broadcasted_iota(jnp.int32, (tb, tb), 1)
        tri = (ri <= ci).astype(jnp.bfloat16)
        carry = jnp.zeros((1, LANES), F32)
        dfb = jnp.zeros((1, LANES), F32)
        for b in reversed(range(T // tb)):
            sl = slice(b * tb, (b + 1) * tb)
            dl = _exact_tri_dot(tri, dc_ref[sl, :]) + carry
            carry = dl[0:1, :]
            z = fl_ref[sl, :] + fb_ref[...]
            dfl = dl * jax.nn.sigmoid(-z)
            dfl_ref[sl, :] = dfl
            dfb = dfb + jnp.sum(dfl, axis=0, keepdims=True)
        dfb_ref[...] = dfb

    return _pallas_call(
        body, name=name,
        out_shape=[jax.ShapeDtypeStruct((T, LANES), F32), jax.ShapeDtypeStruct((1, LANES), F32)],
        compiler_params=pltpu.CompilerParams(vmem_limit_bytes=32 * MIB),
    )(dcum, fl, fbp)


def _causal_keep(i, j, tq, tk):
    key = j * tk + lax.broadcasted_iota(jnp.int32, (tk, tq), 0)
    qry = i * tq + lax.broadcasted_iota(jnp.int32, (tk, tq), 1)
    return key <= qry


def _split_hi_lo(x):
    hi = x.astype(MXU)
    lo = (x - hi.astype(F32)).astype(MXU)
    return hi, lo


def _attn_fwd(qT, k, vT, cum, cumT, name):
    T = k.shape[0]
    tq = _tile(T, 256)
    tk = _tile(tq, 256)
    kpq = tq // tk
    heads = [slice(HEAD_DIM * h, HEAD_DIM * (h + 1)) for h in range(N_HEADS)]

    def body(qT_ref, k_ref, vT_ref, cum_ref, cumT_ref, o_ref, lseT_ref, acc_s, m_s, l_s):
        i = pl.program_id(0)
        acc_s[...] = jnp.zeros_like(acc_s)
        m_s[...] = jnp.full_like(m_s, NEG)
        l_s[...] = jnp.zeros_like(l_s)

        def kblock(j, masked):
            rows = pl.ds(pl.multiple_of(j * tk, tk), tk)
            keep = _causal_keep(i, j, tq, tk) if masked else None
            bias = [cumT_ref[h:h + 1, :] - cum_ref[rows, h:h + 1] for h in range(N_HEADS)]
            qk = [_mm(k_ref[rows, hs], qT_ref[hs, :]) + bias[h] for h, hs in enumerate(heads)]
            for h, hs in enumerate(heads):
                sT = qk[h]
                if masked:
                    sT = jnp.where(keep, sT, NEG)
                m_old = m_s[h:h + 1, :]
                m_new = jnp.maximum(m_old, jnp.max(sT, axis=0, keepdims=True))
                alpha = jnp.exp(m_old - m_new)
                pT = jnp.exp(sT - m_new)
                l_s[h:h + 1, :] = alpha * l_s[h:h + 1, :] + jnp.sum(pT, axis=0, keepdims=True)
                p_hi, p_lo = _split_hi_lo(pT)
                vh = vT_ref[hs, rows]
                acc_s[hs, :] = alpha * acc_s[hs, :] + (_mm(vh, p_hi) + _mm(vh, p_lo))
                m_s[h:h + 1, :] = m_new

        def unmasked(j, c):
            kblock(j, False)
            return c

        lax.fori_loop(0, kpq * i, unmasked, 0)
        for d in range(kpq):
            kblock(kpq * i + d, True)
        for h, hs in enumerate(heads):
            acc_s[hs, :] = acc_s[hs, :] / l_s[h:h + 1, :]
        o_ref[...] = acc_s[...].T
        lseT_ref[...] = m_s[...] + jnp.log(l_s[...])

    full = lambda a, b: pl.BlockSpec((a, b), lambda i: (0, 0))
    colblk = lambda r: pl.BlockSpec((r, tq), lambda i: (0, i))
    return _pallas_call(
        body, name=name, grid=(T // tq,),
        in_specs=[colblk(D_ATTN), full(T, D_ATTN), full(D_ATTN, T), full(T, LANES), colblk(N_HEADS)],
        out_specs=[pl.BlockSpec((tq, D_ATTN), lambda i: (i, 0)), colblk(N_HEADS)],
        out_shape=[jax.ShapeDtypeStruct((T, D_ATTN), F32), jax.ShapeDtypeStruct((N_HEADS, T), F32)],
        scratch_shapes=[pltpu.VMEM((D_ATTN, tq), F32), pltpu.VMEM((N_HEADS, tq), F32),
                        pltpu.VMEM((N_HEADS, tq), F32)],
        compiler_params=_params(40, 1),
    )(qT, k, vT, cum, cumT)


def _attn_bwd(qT, k, kT, v, doT, lseT, deltaT, cum, cumT, name, after=()):
    T = k.shape[0]
    tq = _tile(T, 256)
    tk = _tile(tq, 256)
    kpq = tq // tk
    heads = [slice(HEAD_DIM * h, HEAD_DIM * (h + 1)) for h in range(N_HEADS)]

    def body(qT_ref, k_ref, kT_ref, v_ref, doT_ref, lseT_ref, dlT_ref, cum_ref, cumT_ref, *rest):
        dq_ref, dk_ref, dv_ref, dcum_ref, dq_s = rest[len(after):]
        i = pl.program_id(0)

        @pl.when(i == 0)
        def _():
            dk_ref[...] = jnp.zeros_like(dk_ref)
            dv_ref[...] = jnp.zeros_like(dv_ref)
            dcum_ref[...] = jnp.zeros_like(dcum_ref)

        dq_s[...] = jnp.zeros_like(dq_s)

        def kblock(j, masked):
            rows = pl.ds(pl.multiple_of(j * tk, tk), tk)
            keep = _causal_keep(i, j, tq, tk) if masked else None
            bias = [cumT_ref[h:h + 1, :] - cum_ref[rows, h:h + 1] for h in range(N_HEADS)]
            qk = [_mm(k_ref[rows, hs], qT_ref[hs, :]) + bias[h] for h, hs in enumerate(heads)]
            dps = [_mm(v_ref[rows, hs], doT_ref[hs, :]) for hs in heads]
            for h, hs in enumerate(heads):
                sT = qk[h]
                if masked:
                    sT = jnp.where(keep, sT, NEG)
                pT = jnp.exp(sT - lseT_ref[h:h + 1, :])
                dsT = pT * (dps[h] - dlT_ref[h:h + 1, :])
                dcum_ref[rows, h:h + 1] += -jnp.sum(dsT, axis=1, keepdims=True)
                dsb = dsT.astype(MXU)
                dv_ref[hs, rows] += _mm_nt(doT_ref[hs, :], pT.astype(MXU))
                dk_ref[hs, rows] += _mm_nt(qT_ref[hs, :], dsb)
                dq_s[hs, :] += _mm(kT_ref[hs, rows], dsb)

        def unmasked(j, c):
            kblock(j, False)
            return c

        lax.fori_loop(0, kpq * i, unmasked, 0)
        for d in range(kpq):
            kblock(kpq * i + d, True)
        dq_ref[...] = (dq_s[...] * SCALE).astype(dq_ref.dtype)

    full = lambda a, b: pl.BlockSpec((a, b), lambda i: (0, 0))
    colblk = lambda r: pl.BlockSpec((r, tq), lambda i: (0, i))
    return _pallas_call(
        body, name=name, grid=(T // tq,),
        in_specs=[colblk(D_ATTN), full(T, D_ATTN), full(D_ATTN, T), full(T, D_ATTN), colblk(D_ATTN),
                  colblk(N_HEADS), colblk(N_HEADS), full(T, LANES), colblk(N_HEADS)] + [_UNREAD] * len(after),
        out_specs=[colblk(D_ATTN), full(D_ATTN, T), full(D_ATTN, T), full(T, LANES)],
        out_shape=[
            jax.ShapeDtypeStruct((D_ATTN, T), MXU),
            jax.ShapeDtypeStruct((D_ATTN, T), F32),
            jax.ShapeDtypeStruct((D_ATTN, T), F32),
            jax.ShapeDtypeStruct((T, LANES), F32),
        ],
        scratch_shapes=[pltpu.VMEM((D_ATTN, tq), F32)],
        compiler_params=_params(48, 1),
    )(qT, k, kT, v, doT, lseT, deltaT, cum, cumT, *after)


_ROWS_PER_CHUNK = 64


def _glu_halo(ag_ref, agh_ref, uext_s, tm, first):
    a = ag_ref[:, :D_CONV]
    sg = jax.nn.sigmoid(ag_ref[:, D_CONV:])
    uh = agh_ref[:, :D_CONV] * jax.nn.sigmoid(agh_ref[:, D_CONV:])
    uext_s[0:CONV_HALO, :] = jnp.where(first, 0.0, uh)
    uext_s[CONV_HALO:CONV_HALO + tm, :] = a * sg
    return a, sg


_SUBLANES = 8


def _shifted_copies(ext_s, sh_s, rows):
    for k in range(1, _SUBLANES):
        sh_s[k, 0:rows, :] = ext_s[pl.ds(k, rows), :]


def _window(ext_s, sh_s, start, rows):
    k = start % _SUBLANES
    if k == 0:
        return ext_s[pl.ds(start, rows), :]
    return sh_s[k, pl.ds(start - k, rows), :]


def _layer_norm_stats(y):
    mu = jnp.mean(y, axis=-1, keepdims=True)
    xc = y - mu
    rs = lax.rsqrt(jnp.mean(xc * xc, axis=-1, keepdims=True) + EPS)
    return xc * rs, rs


def _conv_fwd(ag, w32, cb, lg, lb, name):
    T = ag.shape[0]
    tm = _tile(T, 256)
    rc = _tile(tm, _ROWS_PER_CHUNK)
    hb = tm // CONV_HALO

    def body(ag_ref, agh_ref, w_ref, cb_ref, lg_ref, lb_ref, yc_ref, c_ref, uext_s, ush_s):
        i = pl.program_id(0)
        _glu_halo(ag_ref, agh_ref, uext_s, tm, i == 0)
        _shifted_copies(uext_s, ush_s, tm + CONV_HALO - _SUBLANES)
        for r0 in range(0, tm, rc):
            acc = jnp.zeros((rc, D_CONV), F32)
            for t in range(CONV_TAPS):
                acc = acc + _window(uext_s, ush_s, r0 + CONV_HALO - (CONV_TAPS - 1) + t, rc) * w_ref[t:t + 1, :]
            y = acc + cb_ref[...]
            yc_ref[r0:r0 + rc, :] = y
            n, _ = _layer_norm_stats(y)
            z = n * lg_ref[...] + lb_ref[...]
            c_ref[r0:r0 + rc, :] = z * jax.nn.sigmoid(z)

    row = lambda w: pl.BlockSpec((tm, w), lambda i: (i, 0))
    full = lambda a, b: pl.BlockSpec((a, b), lambda i: (0, 0))
    return _pallas_call(
        body, name=name, grid=(T // tm,),
        in_specs=[row(2 * D_CONV),
                  pl.BlockSpec((CONV_HALO, 2 * D_CONV), lambda i: (jnp.maximum(i * hb - 1, 0), 0)),
                  full(CONV_HALO, D_CONV), full(1, D_CONV), full(1, D_CONV), full(1, D_CONV)],
        out_specs=[row(D_CONV), row(D_CONV)],
        out_shape=[jax.ShapeDtypeStruct((T, D_CONV), F32), jax.ShapeDtypeStruct((T, D_CONV), F32)],
        scratch_shapes=[pltpu.VMEM((CONV_HALO + tm, D_CONV), F32),
                        pltpu.VMEM((_SUBLANES, CONV_HALO + tm, D_CONV), F32)],
        compiler_params=_params(32, 1),
    )(ag, ag, w32, cb, lg, lb)


def _conv_bwd(dc, yc, ag, w32, lg, lb, name):
    T = ag.shape[0]
    tm = _tile(T, 256)
    rc = _tile(tm, _ROWS_PER_CHUNK)
    I = T // tm
    hb = tm // CONV_HALO
    n_halo_blocks = T // CONV_HALO

    def body(dc_ref, yc_ref, dch_ref, ych_ref, ag_ref, agh_ref, w_ref, lg_ref, lb_ref,
             dag_ref, dw_ref, dcb_ref, dlg_ref, dlb_ref, uext_s, dext_s, ush_s, dsh_s):
        i = pl.program_id(0)
        lgv = lg_ref[...]
        lbv = lb_ref[...]

        def ln_bwd(dcv, ycv):
            n, rs = _layer_norm_stats(ycv)
            z = n * lgv + lbv
            dz = dcv * _silu_grad(z, jax.nn.sigmoid(z))
            dn = dz * lgv
            dy = rs * (dn - jnp.mean(dn, axis=-1, keepdims=True) - n * jnp.mean(dn * n, axis=-1, keepdims=True))
            return dy, dz, n

        dy, dz, n = ln_bwd(dc_ref[...], yc_ref[...])
        dyh, _, _ = ln_bwd(dch_ref[...], ych_ref[...])
        dext_s[0:tm, :] = dy
        dext_s[tm:tm + CONV_HALO, :] = jnp.where(i == I - 1, 0.0, dyh)
        a, sg = _glu_halo(ag_ref, agh_ref, uext_s, tm, i == 0)
        _shifted_copies(uext_s, ush_s, tm + CONV_HALO - _SUBLANES)
        _shifted_copies(dext_s, dsh_s, tm + CONV_HALO - _SUBLANES)

        @pl.when(i == 0)
        def _():
            dw_ref[...] = jnp.zeros_like(dw_ref)
            dcb_ref[...] = jnp.zeros_like(dcb_ref)
            dlg_ref[...] = jnp.zeros_like(dlg_ref)
            dlb_ref[...] = jnp.zeros_like(dlb_ref)

        dcb_ref[...] += jnp.sum(dy, axis=0, keepdims=True)
        dlg_ref[...] += jnp.sum(dz * n, axis=0, keepdims=True)
        dlb_ref[...] += jnp.sum(dz, axis=0, keepdims=True)
        for t in range(CONV_TAPS):
            u_t = _window(uext_s, ush_s, CONV_HALO - (CONV_TAPS - 1) + t, tm)
            dw_ref[t:t + 1, :] += jnp.sum(dy * u_t, axis=0, keepdims=True)
        for r0 in range(0, tm, rc):
            acc = jnp.zeros((rc, D_CONV), F32)
            for t in range(CONV_TAPS):
                acc = acc + _window(dext_s, dsh_s, r0 + (CONV_TAPS - 1) - t, rc) * w_ref[t:t + 1, :]
            a_c = a[r0:r0 + rc, :]
            sg_c = sg[r0:r0 + rc, :]
            dag_ref[r0:r0 + rc, :D_CONV] = (acc * sg_c).astype(dag_ref.dtype)
            dag_ref[r0:r0 + rc, D_CONV:] = (acc * a_c * sg_c * (1.0 - sg_c)).astype(dag_ref.dtype)

    row = lambda w: pl.BlockSpec((tm, w), lambda i: (i, 0))
    full = lambda a, b: pl.BlockSpec((a, b), lambda i: (0, 0))
    nxt = pl.BlockSpec((CONV_HALO, D_CONV), lambda i: (jnp.minimum((i + 1) * hb, n_halo_blocks - 1), 0))
    return _pallas_call(
        body, name=name, grid=(I,),
        in_specs=[row(D_CONV), row(D_CONV), nxt, nxt, row(2 * D_CONV),
                  pl.BlockSpec((CONV_HALO, 2 * D_CONV), lambda i: (jnp.maximum(i * hb - 1, 0), 0)),
                  full(CONV_HALO, D_CONV), full(1, D_CONV), full(1, D_CONV)],
        out_specs=[row(2 * D_CONV), full(CONV_HALO, D_CONV), full(1, D_CONV), full(1, D_CONV), full(1, D_CONV)],
        out_shape=[
            jax.ShapeDtypeStruct((T, 2 * D_CONV), MXU),
            jax.ShapeDtypeStruct((CONV_HALO, D_CONV), F32),
            jax.ShapeDtypeStruct((1, D_CONV), F32),
            jax.ShapeDtypeStruct((1, D_CONV), F32),
            jax.ShapeDtypeStruct((1, D_CONV), F32),
        ],
        scratch_shapes=[pltpu.VMEM((CONV_HALO + tm, D_CONV), F32), pltpu.VMEM((tm + CONV_HALO, D_CONV), F32),
                        pltpu.VMEM((_SUBLANES, CONV_HALO + tm, D_CONV), F32),
                        pltpu.VMEM((_SUBLANES, CONV_HALO + tm, D_CONV), F32)],
        compiler_params=_params(40, 1),
    )(dc, yc, dc, yc, ag, ag, w32, lg, lb)


def _outproj_fwd(x1, c, o, gc, ga, wout, name):
    T, D = x1.shape
    tm = _tile(T, 512)

    def body(x_ref, c_ref, o_ref, gc_ref, ga_ref, w_ref, x2_ref):
        yc, _ = _rms_fwd(c_ref[...], gc_ref[...])
        ya, _ = _rms_fwd(o_ref[...], ga_ref[...])
        x2_ref[...] = (x_ref[...] + _mm(yc.astype(MXU), w_ref[:D_CONV, :])
                       + _mm(ya.astype(MXU), w_ref[D_CONV:, :]))

    row = lambda w: pl.BlockSpec((tm, w), lambda i: (i, 0))
    full = lambda a, b: pl.BlockSpec((a, b), lambda i: (0, 0))
    return _pallas_call(
        body, name=name, grid=(T // tm,),
        in_specs=[row(D), row(D_CONV), row(D_ATTN), full(1, D_CONV), full(1, D_ATTN), full(D_CONV + D_ATTN, D)],
        out_specs=row(D),
        out_shape=jax.ShapeDtypeStruct((T, D), F32),
        compiler_params=_params(32, 1),
    )(x1, c, o, gc, ga, wout)


def _outproj_bwd(dx2, c, o, gc, ga, wout, name):
    T, D = dx2.shape
    tm = _tile(T, 256)
    I = T // tm

    def body(dx_ref, c_ref, o_ref, gc_ref, ga_ref, w_ref,
             dc_ref, doT_ref, dlT_ref, dw_ref, dgc_ref, dga_ref, acc_s):
        i = pl.program_id(0)
        dxb = dx_ref[...].astype(MXU)
        cv = c_ref[...]
        ov = o_ref[...]
        yc, rcn = _rms_fwd(cv, gc_ref[...])
        ya, ra = _rms_fwd(ov, ga_ref[...])
        dyc = _mm_nt(dxb, w_ref[:D_CONV, :])
        dya = _mm_nt(dxb, w_ref[D_CONV:, :])
        dwc = _mm_tn(yc.astype(MXU), dxb)
        dwa = _mm_tn(ya.astype(MXU), dxb)
        dcv, dgc = _rms_bwd(cv, rcn, gc_ref[...], dyc)
        dov, dga = _rms_bwd(ov, ra, ga_ref[...], dya)
        dc_ref[...] = dcv
        dob = dov.astype(doT_ref.dtype)
        doT_ref[...] = dov.T.astype(doT_ref.dtype)
        chan = lax.broadcasted_iota(jnp.int32, (D_ATTN, LANES), 0)
        head = lax.broadcasted_iota(jnp.int32, (D_ATTN, LANES), 1)
        in_head = ((chan >= head * HEAD_DIM) & (chan < (head + 1) * HEAD_DIM)).astype(jnp.bfloat16)
        dlT_ref[...] = _exact_dot_01(dob.astype(F32) * ov, in_head).T[:N_HEADS, :]

        @pl.when(i == 0)
        def _():
            acc_s[:D_CONV, :] = dwc
            acc_s[D_CONV:, :] = dwa
            dgc_ref[...] = dgc
            dga_ref[...] = dga

        @pl.when(i > 0)
        def _():
            acc_s[:D_CONV, :] += dwc
            acc_s[D_CONV:, :] += dwa
            dgc_ref[...] += dgc
            dga_ref[...] += dga

        @pl.when(i == I - 1)
        def _():
            dw_ref[...] = acc_s[...].astype(dw_ref.dtype)

    row = lambda w: pl.BlockSpec((tm, w), lambda i: (i, 0))
    full = lambda a, b: pl.BlockSpec((a, b), lambda i: (0, 0))
    return _pallas_call(
        body, name=name, grid=(I,),
        in_specs=[row(D), row(D_CONV), row(D_ATTN), full(1, D_CONV), full(1, D_ATTN), full(D_CONV + D_ATTN, D)],
        out_specs=[row(D_CONV), pl.BlockSpec((D_ATTN, tm), lambda i: (0, i)),
                   pl.BlockSpec((N_HEADS, tm), lambda i: (0, i)),
                   full(D_CONV + D_ATTN, D), full(1, D_CONV), full(1, D_ATTN)],
        out_shape=[
            jax.ShapeDtypeStruct((T, D_CONV), F32),
            jax.ShapeDtypeStruct((D_ATTN, T), MXU),
            jax.ShapeDtypeStruct((N_HEADS, T), F32),
            jax.ShapeDtypeStruct((D_CONV + D_ATTN, D), MXU),
            jax.ShapeDtypeStruct((1, D_CONV), F32),
            jax.ShapeDtypeStruct((1, D_ATTN), F32),
        ],
        scratch_shapes=[pltpu.VMEM((D_CONV + D_ATTN, D), F32)],
        compiler_params=_params(40, 1),
    )(dx2, c, o, gc, ga, wout)


def _ffn_down_loss(x, A, w2, gf, target, name):
    T, D = x.shape
    J, _, bf = A.shape
    tm = _tile(T, 512)

    def body(x_ref, A_ref, w2_ref, g_ref, t_ref, loss_ref, dx_ref, dg_ref):
        i = pl.program_id(0)
        f = _mm(A_ref[0], w2_ref[0:bf, :])
        for j in range(1, J):
            f = f + _mm(A_ref[j], w2_ref[j * bf:(j + 1) * bf, :])
        xv = x_ref[...] + 0.5 * f
        gv = g_ref[...]
        out, r = _rms_fwd(xv, gv)
        err = out - t_ref[...]
        part = jnp.full((1, LANES), 0.5 / D, F32) * jnp.sum(err * err)
        dxn, dgp = _rms_bwd(xv, r, gv, err * (1.0 / D))
        dx_ref[...] = dxn

        @pl.when(i == 0)
        def _():
            loss_ref[...] = part
            dg_ref[...] = dgp

        @pl.when(i > 0)
        def _():
            loss_ref[...] += part
            dg_ref[...] += dgp

    row = lambda w: pl.BlockSpec((tm, w), lambda i: (i, 0))
    full = lambda a, b: pl.BlockSpec((a, b), lambda i: (0, 0))
    return _pallas_call(
        body, name=name, grid=(T // tm,),
        in_specs=[row(D), pl.BlockSpec((J, tm, bf), lambda i: (0, i, 0)), full(J * bf, D), full(1, D), row(D)],
        out_specs=[full(1, LANES), row(D), full(1, D)],
        out_shape=[jax.ShapeDtypeStruct((1, LANES), F32), jax.ShapeDtypeStruct((T, D), F32),
                   jax.ShapeDtypeStruct((1, D), F32)],
        compiler_params=_params(56, 1),
    )(x, A, w2, gf, target)


def _row_tile(rows):
    for cand in (256, 176, 128, 64, 32, 16):
        if rows % cand == 0:
            return cand
    return rows


def _adamw(w, m, v, parts, name):
    R, C = w.shape
    P = parts.shape[0]
    tr = _row_tile(R)
    c1 = 1.0 - ADAM_B1 ** ADAM_STEP
    c2 = 1.0 - ADAM_B2 ** ADAM_STEP

    def body(w_ref, m_ref, v_ref, p_ref, g_ref, d_ref, nm_ref, nv_ref):
        g = p_ref[0].astype(F32)
        for s in range(1, P):
            g = g + p_ref[s].astype(F32)
        wv = w_ref[...]
        mn = ADAM_B1 * m_ref[...] + (1.0 - ADAM_B1) * g
        vn = ADAM_B2 * v_ref[...] + (1.0 - ADAM_B2) * (g * g)
        g_ref[...] = g
        nm_ref[...] = mn
        nv_ref[...] = vn
        d_ref[...] = -ADAM_LR * ((mn / c1) / (jnp.sqrt(vn / c2) + ADAM_EPS) + ADAM_WD * wv)

    blk = pl.BlockSpec((tr, C), lambda i: (i, 0))
    out = jax.ShapeDtypeStruct((R, C), F32)
    return _pallas_call(
        body, name=name, grid=(R // tr,),
        in_specs=[blk, blk, blk, pl.BlockSpec((P, tr, C), lambda i: (0, i, 0))],
        out_specs=[blk, blk, blk, blk],
        out_shape=[out, out, out, out],
        compiler_params=_params(32, 1),
    )(w, m, v, parts)


def _position():
    return lax.axis_index("x"), lax.axis_index("y"), lax.axis_index("c")


def _flat(px, py, pc):
    return 4 * px + 2 * py + pc


def _gather_body(ins, outs, send_sems, recv_sems, local_sems, handshake):
    n = len(ins)
    x, y, c = _position()
    me, sibling = (x, y, c), (x, y, 1 - c)
    chips = [(1 - x, y), (x, 1 - y), (1 - x, 1 - y)]
    if handshake:
        _handshake([sibling] + [(*chip, cc) for chip in chips for cc in (c, 1 - c)])

    def copy(a, k, block, to, src=None):
        dst = outs[a].at[_flat(*block)]
        return pltpu.make_async_remote_copy(
            src_ref=dst if src is None else src, dst_ref=dst,
            send_sem=send_sems.at[a, k], recv_sem=recv_sems.at[a, k],
            device_id=to, device_id_type=MESH)

    mine = [pltpu.make_async_copy(ins[a], outs[a].at[_flat(*me)], local_sems.at[a]) for a in range(n)]
    for cp in mine:
        cp.start()
    first = []
    for a in range(n):
        first.append(copy(a, 0, me, sibling, src=ins[a]))
        first += [copy(a, 1 + j, me, (*chip, c), src=ins[a]) for j, chip in enumerate(chips)]
    for cp in first:
        cp.start()
    passed = []
    for a in range(n):
        for j, chip in enumerate(chips):
            copy(a, 1 + j, (*chip, c), me).wait_recv()
            fwd = copy(a, 4 + j, (*chip, c), sibling)
            fwd.start()
            passed.append(fwd)
    for a in range(n):
        copy(a, 0, sibling, me).wait_recv()
        for j, chip in enumerate(chips):
            copy(a, 4 + j, (*chip, 1 - c), me).wait_recv()
    for cp in first + passed:
        cp.wait_send()
    for cp in mine:
        cp.wait()


def _gather_scratch(n):
    return [pltpu.SemaphoreType.DMA((n, 7)), pltpu.SemaphoreType.DMA((n, 7)), pltpu.SemaphoreType.DMA((n,))]


def _all_gather(shards, name):
    n = len(shards)

    def body(*refs):
        _gather_body(refs[:n], refs[n:2 * n], *refs[2 * n:], handshake=False)

    hbm = pl.BlockSpec(memory_space=pltpu.HBM)
    return _pallas_call(
        body, name=name,
        in_specs=[hbm] * n, out_specs=[hbm] * n,
        out_shape=[jax.ShapeDtypeStruct((N_DEV,) + s.shape, s.dtype) for s in shards],
        scratch_shapes=_gather_scratch(n),
    )(*shards)


def _handshake(peers):
    barrier = pltpu.get_barrier_semaphore()
    for peer in peers:
        pl.semaphore_signal(barrier, inc=1, device_id=peer, device_id_type=MESH)
    pl.semaphore_wait(barrier, len(peers))


def _sequencer_call(body, name, collective_id, out_type, scratch_types, operands):
    return pl.kernel(
        body, name=name, out_type=out_type,
        mesh=plsc.ScalarSubcoreMesh(axis_name="sequencer", num_cores=1),
        scratch_types=scratch_types,
        compiler_params=pltpu.CompilerParams(collective_id=collective_id),
    )(*operands)


def _seq_all_gather(shards, name, collective_id, after):
    n = len(shards)

    def body(*refs):
        _gather_body(refs[:n], refs[n + 1:2 * n + 1], *refs[2 * n + 1:], handshake=True)

    return _sequencer_call(
        body, name, collective_id,
        [jax.ShapeDtypeStruct((N_DEV,) + s.shape, s.dtype) for s in shards],
        _gather_scratch(n), list(shards) + [after])


def _seq_to_sibling(parts, name, collective_id, after):
    n = len(parts)

    def body(*refs):
        ins, outs = refs[:n], refs[n + len(after):2 * n + len(after)]
        send_sems, recv_sems = refs[2 * n + len(after):]
        x, y, c = _position()
        sibling = (x, y, 1 - c)
        _handshake([sibling])
        sent = []
        for a in range(n):
            for q in range(N_CHIPS):
                cp = pltpu.make_async_remote_copy(
                    src_ref=ins[a].at[2 * q + (1 - c)], dst_ref=outs[a].at[q],
                    send_sem=send_sems.at[a, q], recv_sem=recv_sems.at[a, q],
                    device_id=sibling, device_id_type=MESH)
                cp.start()
                sent.append(cp)
        for cp in sent:
            cp.wait_recv()
        for cp in sent:
            cp.wait_send()

    return _sequencer_call(
        body, name, collective_id,
        [jax.ShapeDtypeStruct((N_CHIPS,) + p.shape[1:], p.dtype) for p in parts],
        [pltpu.SemaphoreType.DMA((n, N_CHIPS)), pltpu.SemaphoreType.DMA((n, N_CHIPS))],
        list(parts) + list(after))


def _seq_to_chips(partials, name, collective_id, after=()):
    n = len(partials)

    def body(*refs):
        ins, outs = refs[:n], refs[n + len(after):2 * n + len(after)]
        send_sems, recv_sems, local_sems = refs[2 * n + len(after):]
        x, y, c = _position()
        my_chip = 2 * x + y
        chips = [(1 - x, y), (x, 1 - y), (1 - x, 1 - y)]
        _handshake([(*chip, c) for chip in chips])
        mine = [pltpu.make_async_copy(ins[a].at[my_chip], outs[a].at[my_chip], local_sems.at[a]) for a in range(n)]
        for cp in mine:
            cp.start()
        sent = []
        for a in range(n):
            for j, (px, py) in enumerate(chips):
                cp = pltpu.make_async_remote_copy(
                    src_ref=ins[a].at[2 * px + py], dst_ref=outs[a].at[my_chip],
                    send_sem=send_sems.at[a, j], recv_sem=recv_sems.at[a, j],
                    device_id=(px, py, c), device_id_type=MESH)
                cp.start()
                sent.append(cp)
        for cp in sent:
            cp.wait_recv()
        for cp in sent:
            cp.wait_send()
        for cp in mine:
            cp.wait()

    return _sequencer_call(
        body, name, collective_id,
        [jax.ShapeDtypeStruct(p.shape, p.dtype) for p in partials],
        [pltpu.SemaphoreType.DMA((n, 3)), pltpu.SemaphoreType.DMA((n, 3)), pltpu.SemaphoreType.DMA((n,))],
        list(partials) + list(after))


def _pair_add(parts, recvs, name, after=()):
    n = len(parts)
    core = lax.axis_index("c").astype(jnp.int32).reshape(1)

    def body(c_ref, *refs):
        ps, rs, outs = refs[:n], refs[n:2 * n], refs[2 * n + len(after):]
        for p_ref, r_ref, o_ref in zip(ps, rs, outs):
            o_ref[...] = (p_ref[...].astype(F32) + r_ref[...].astype(F32)).astype(o_ref.dtype)

    mine = lambda p: pl.BlockSpec((None,) + p.shape[1:], lambda q, c: (2 * q + c[0], 0, 0))
    blk = lambda p: pl.BlockSpec((None,) + p.shape[1:], lambda q, c: (q, 0, 0))
    return pl.pallas_call(
        body, name=name,
        grid_spec=pltpu.PrefetchScalarGridSpec(
            num_scalar_prefetch=1, grid=(N_CHIPS,),
            in_specs=[mine(p) for p in parts] + [blk(p) for p in parts] + [_UNREAD] * len(after),
            out_specs=[blk(p) for p in parts]),
        out_shape=[pltpu.HBM((N_CHIPS,) + p.shape[1:], p.dtype) for p in parts],
        compiler_params=_params(40, 1),
    )(core, *[pltpu.with_memory_space_constraint(a, pltpu.HBM) for a in (*parts, *recvs, *after)])


class _Reduced(NamedTuple):
    partials: list
    reduced: list


def _blocks(g):
    return g.reshape(N_DEV, -1, g.shape[-1])


def _reduce_scatter(parts, tag, ids, after=(), between=None, add_after=()):
    from_sibling = _seq_to_sibling(parts, "rs_sibling_" + tag, ids[0], after)
    mid = between(from_sibling[0]) if between else ()
    partials = _pair_add(parts, from_sibling, "rs_add_" + tag, add_after)
    return _Reduced(partials, _seq_to_chips(partials, "rs_chips_" + tag, ids[1], mid))


_SMALL = ("ffn1_norm", "mix_norm", "conv_b", "conv_ln_g", "conv_ln_b", "forget_b", "out_norm_conv",
          "out_norm_attn", "ffn2_norm", "final_norm")
_PACK_WIDTH = 2 * D_CONV
_SLOT = dict(ffn1_norm=(0, 0), mix_norm=(1, 0), ffn2_norm=(2, 0), final_norm=(3, 0), conv_b=(4, 0),
             conv_ln_g=(4, D_CONV), conv_ln_b=(5, 0), out_norm_conv=(5, D_CONV), out_norm_attn=(6, 0),
             forget_b=(6, D_CONV))
_CONV_ROW0 = 8
_PACK_ROWS = _CONV_ROW0 + CONV_HALO


def _pack_small(small, name):
    arrays = [small[n] for n in _SMALL] + [small["conv_w"]]

    def body(*refs):
        out = refs[-1]
        out[...] = jnp.zeros_like(out)
        for n, ref in zip(_SMALL, refs):
            row, lane = _SLOT[n]
            out[row:row + 1, lane:lane + ref.shape[1]] = ref[...]
        out[_CONV_ROW0:, :D_CONV] = refs[len(_SMALL)][...]

    return _pallas_call(body, name=name, out_shape=jax.ShapeDtypeStruct((_PACK_ROWS, _PACK_WIDTH), F32))(*arrays)


def _adamw_small(gathered, w, m, v, name):
    c1 = 1.0 - ADAM_B1 ** ADAM_STEP
    c2 = 1.0 - ADAM_B2 ** ADAM_STEP
    k = len(_SMALL)

    def body(g_ref, *refs):
        ws, ms, vs = refs[:k], refs[k:2 * k], refs[2 * k:3 * k]
        outs = refs[3 * k:]
        total = g_ref[0]
        for s in range(1, N_DEV):
            total = total + g_ref[s]
        for i, n in enumerate(_SMALL):
            row, lane = _SLOT[n]
            width = ws[i].shape[1]
            g = total[row:row + 1, lane:lane + width]
            mn = ADAM_B1 * ms[i][...] + (1.0 - ADAM_B1) * g
            vn = ADAM_B2 * vs[i][...] + (1.0 - ADAM_B2) * (g * g)
            o_g, o_d, o_m, o_v = outs[4 * i:4 * i + 4]
            o_g[...] = g
            o_m[...] = mn
            o_v[...] = vn
            o_d[...] = -ADAM_LR * ((mn / c1) / (jnp.sqrt(vn / c2) + ADAM_EPS) + ADAM_WD * ws[i][...])
        outs[4 * k][...] = total[_CONV_ROW0:, :D_CONV]

    shapes = []
    for n in _SMALL:
        shapes += [jax.ShapeDtypeStruct(w[n].shape, F32)] * 4
    shapes.append(jax.ShapeDtypeStruct((CONV_HALO, D_CONV), F32))
    res = _pallas_call(body, name=name, out_shape=shapes)(
        gathered, *[w[n] for n in _SMALL], *[m[n] for n in _SMALL], *[v[n] for n in _SMALL])
    return {n: res[4 * i:4 * i + 4] for i, n in enumerate(_SMALL)}, res[4 * k]


def _local_step(x, target, norms, shard):
    D = x.shape[1]
    J = N_DEV // 2
    as13 = lambda g: g.reshape(2, J, g.shape[1], D)

    (g13_1,) = _all_gather([shard["ffn1_w13"]], "gather_ffn1_w13")
    (g2_1,) = _seq_all_gather([shard["ffn1_w2"]], "gather_ffn1_w2", 10, after=g13_1)
    w13_1 = as13(g13_1)
    G1, U1, A1 = _ffn_up(x, norms["ffn1_norm"], w13_1, "ffn1_up")
    gin, gconv = _seq_all_gather([shard["w_in"], shard["conv_w"]], "gather_mix", 1, after=G1)
    w2_1 = g2_1.reshape(-1, D)
    x1 = _ffn_down(x, A1, w2_1, "ffn1_down")
    gout, g13_2, g2_2 = _seq_all_gather([shard["w_out"], shard["ffn2_w13"], shard["ffn2_w2"]], "gather_ffn2", 2,
                                        after=x1)
    winp = jnp.pad(gin.reshape(N_IN, D), ((0, N_IN_PAD - N_IN), (0, 0)))
    wout = gout.reshape(-1, D)
    conv_w32 = jnp.pad(gconv.transpose(1, 0, 2).reshape(CONV_TAPS, D_CONV), ((0, CONV_HALO - CONV_TAPS), (0, 0)))

    ag, k, v, qT, kT, vT, fl = _inproj_fwd(x1, norms["mix_norm"], winp, "inproj_fwd")
    cum, cumT = _forget_fwd(fl, norms["forget_b"], "forget_fwd")
    yc, c = _conv_fwd(ag, conv_w32, norms["conv_b"], norms["conv_ln_g"], norms["conv_ln_b"], "conv_fwd")
    o, lseT = _attn_fwd(qT, k, vT, cum, cumT, "attn_fwd")
    x2 = _outproj_fwd(x1, c, o, norms["out_norm_conv"], norms["out_norm_attn"], wout, "outproj_fwd")
    w13_2, w2_2 = as13(g13_2), g2_2.reshape(-1, D)
    G2, U2, A2 = _ffn_up(x2, norms["ffn2_norm"], w13_2, "ffn2_up")
    loss, dx3, d_final = _ffn_down_loss(x2, A2, w2_2, norms["final_norm"], target, "ffn2_down_loss")

    dw2_2 = _ffn_w2_grad(dx3, A2, "ffn2_w2_grad")
    dx2, d_ffn2n, h3, dG2, dU2 = _ffn_bwd_act(x2, norms["ffn2_norm"], dx3, G2, U2, w13_2, w2_2, "ffn2_bwd_act")
    dw13_2 = _ffn_w13_grad(h3, dG2, dU2, "ffn2_w13_grad")
    dc, dobT, deltaT, dwout, d_onc, d_ona = _outproj_bwd(
        dx2, c, o, norms["out_norm_conv"], norms["out_norm_attn"], wout, "outproj_bwd")
    red_ffn2 = _reduce_scatter([_blocks(dw13_2), _blocks(dw2_2)], "ffn2", (3, 4), add_after=(dc,))
    dqT, dkT, dvT, dcum = _attn_bwd(qT, k, kT, v, dobT, lseT, deltaT, cum, cumT, "attn_bwd",
                                    after=red_ffn2.partials)
    dfl, d_fb = _forget_bwd(dcum, fl, norms["forget_b"], "forget_bwd")
    dag, d_convw, d_cb, d_lg, d_lb = _conv_bwd(dc, yc, ag, conv_w32, norms["conv_ln_g"], norms["conv_ln_b"], "conv_bwd")
    dx1, d_mixn, h2 = _inproj_bwd_act(x1, norms["mix_norm"], dx2, dag, dqT, dkT, dvT, dfl, winp, "inproj_bwd_act")
    dw2_1 = _ffn_w2_grad(dx1, A1, "ffn1_w2_grad")
    early = [_blocks(dwout), _blocks(dw2_1)]
    sib_early = _seq_to_sibling(early, "rs_sibling_mix_early", 11, red_ffn2.reduced[:1])
    dwinp = _inproj_bwd_weights(h2, dag, dqT, dkT, dvT, dfl, "inproj_bwd_weights")
    dwin_blocks = dwinp[:N_IN].reshape(N_DEV, N_IN // N_DEV, -1)
    sib_w_in = _seq_to_sibling([dwin_blocks], "rs_sibling_mix", 5, sib_early[:1])
    mix_partials = _pair_add([dwin_blocks] + early, sib_w_in + sib_early, "rs_add_mix")
    red_mix = _Reduced(mix_partials, _seq_to_chips(mix_partials, "rs_chips_mix", 6))
    dx, d_ffn1n, h1, dG1, dU1 = _ffn_bwd_act(x, norms["ffn1_norm"], dx1, G1, U1, w13_1, w2_1, "ffn1_bwd_act",
                                             after=red_mix.partials)
    dw13_1 = _ffn_w13_grad(h1, dG1, dU1, "ffn1_w13_grad")

    small = dict(ffn1_norm=d_ffn1n, mix_norm=d_mixn, conv_b=d_cb, conv_ln_g=d_lg, conv_ln_b=d_lb,
                 forget_b=d_fb, out_norm_conv=d_onc, out_norm_attn=d_ona, ffn2_norm=d_ffn2n,
                 final_norm=d_final, conv_w=d_convw)
    packed_small = _pack_small(small, "pack_small_grads")
    gathered_small = []

    def gather_small(behind):
        gathered_small.extend(_seq_all_gather([packed_small], "gather_small_grads", 9, after=behind))
        return gathered_small

    red_w13_1 = _reduce_scatter([_blocks(dw13_1)], "ffn1_w13", (7, 8), after=red_mix.reduced[:1],
                                between=gather_small)
    big = dict(ffn1_w13=red_w13_1.reduced[0], ffn1_w2=red_mix.reduced[2], w_in=red_mix.reduced[0],
               w_out=red_mix.reduced[1], ffn2_w13=red_ffn2.reduced[0], ffn2_w2=red_ffn2.reduced[1])
    return loss[0, 0], dx, gathered_small[0], big


_BIG = ("ffn1_w13", "ffn1_w2", "w_in", "w_out", "ffn2_w13", "ffn2_w2")
_TRANSPOSED = ("ffn1_w13", "ffn2_w13", "w_in")
_ORDER = ("ffn1_norm", "ffn1_w13", "ffn1_w2", "mix_norm", "w_in", "conv_w", "conv_b", "conv_ln_g", "conv_ln_b",
          "forget_b", "out_norm_conv", "out_norm_attn", "w_out", "ffn2_norm", "ffn2_w13", "ffn2_w2", "final_norm")


def kernel(x, ffn1_norm, ffn1_w13, ffn1_w2, mix_norm, w_in, conv_w, conv_b, conv_ln_g, conv_ln_b, forget_b, out_norm_conv, out_norm_attn, w_out, ffn2_norm, ffn2_w13, ffn2_w2, final_norm, loss_target, m_ffn1_norm, m_ffn1_w13, m_ffn1_w2, m_mix_norm, m_w_in, m_conv_w, m_conv_b, m_conv_ln_g, m_conv_ln_b, m_forget_b, m_out_norm_conv, m_out_norm_attn, m_w_out, m_ffn2_norm, m_ffn2_w13, m_ffn2_w2, m_final_norm, v_ffn1_norm, v_ffn1_w13, v_ffn1_w2, v_mix_norm, v_w_in, v_conv_w, v_conv_b, v_conv_ln_g, v_conv_ln_b, v_forget_b, v_out_norm_conv, v_out_norm_attn, v_w_out, v_ffn2_norm, v_ffn2_w13, v_ffn2_w2, v_final_norm):
    w = dict(ffn1_norm=ffn1_norm, ffn1_w13=ffn1_w13, ffn1_w2=ffn1_w2, mix_norm=mix_norm, w_in=w_in, conv_w=conv_w,
             conv_b=conv_b, conv_ln_g=conv_ln_g, conv_ln_b=conv_ln_b, forget_b=forget_b, out_norm_conv=out_norm_conv,
             out_norm_attn=out_norm_attn, w_out=w_out, ffn2_norm=ffn2_norm, ffn2_w13=ffn2_w13, ffn2_w2=ffn2_w2,
             final_norm=final_norm)
    m = dict(ffn1_norm=m_ffn1_norm, ffn1_w13=m_ffn1_w13, ffn1_w2=m_ffn1_w2, mix_norm=m_mix_norm, w_in=m_w_in,
             conv_w=m_conv_w, conv_b=m_conv_b, conv_ln_g=m_conv_ln_g, conv_ln_b=m_conv_ln_b, forget_b=m_forget_b,
             out_norm_conv=m_out_norm_conv, out_norm_attn=m_out_norm_attn, w_out=m_w_out, ffn2_norm=m_ffn2_norm,
             ffn2_w13=m_ffn2_w13, ffn2_w2=m_ffn2_w2, final_norm=m_final_norm)
    v = dict(ffn1_norm=v_ffn1_norm, ffn1_w13=v_ffn1_w13, ffn1_w2=v_ffn1_w2, mix_norm=v_mix_norm, w_in=v_w_in,
             conv_w=v_conv_w, conv_b=v_conv_b, conv_ln_g=v_conv_ln_g, conv_ln_b=v_conv_ln_b, forget_b=v_forget_b,
             out_norm_conv=v_out_norm_conv, out_norm_attn=v_out_norm_attn, w_out=v_w_out, ffn2_norm=v_ffn2_norm,
             ffn2_w13=v_ffn2_w13, ffn2_w2=v_ffn2_w2, final_norm=v_final_norm)
    shapes = {n: a.shape for n, a in w.items()}
    T, D = x.shape[1], x.shape[2]
    def two(n, a):
        if a.ndim != 3:
            return a.reshape(1, -1)
        a = a.reshape(a.shape[-2], a.shape[-1])
        return a.T if n in _TRANSPOSED else a

    w2d = {n: two(n, a) for n, a in w.items()}
    m2d = {n: two(n, a) for n, a in m.items()}
    v2d = {n: two(n, a) for n, a in v.items()}

    shard = {n: w2d[n].astype(MXU) for n in _BIG}
    shard["conv_w"] = w2d["conv_w"]
    norms = {n: w2d[n] for n in _SMALL}
    norms["forget_b"] = jnp.pad(w2d["forget_b"], ((0, 0), (0, LANES - N_HEADS)))
    loss_part, dx, gathered_small, big = _local_step(x[0], loss_target[0], norms, shard)
    loss = lax.psum(loss_part, ("x", "y", "c"))

    grads, deltas, new_m, new_v = {}, {}, {}, {}
    for n in _BIG:
        g, d, nm, nv = _adamw(w2d[n], m2d[n], v2d[n], big[n], "adamw_" + n)
        grads[n], deltas[n], new_m[n], new_v[n] = g, d, nm, nv

    small_out, conv_g_full = _adamw_small(gathered_small, w2d, m2d, v2d, "adamw_small")
    for n in _SMALL:
        grads[n], deltas[n], new_m[n], new_v[n] = small_out[n]
    conv_g_full = conv_g_full[:CONV_TAPS]
    xi, yi, ci = _position()
    cw = shapes["conv_w"][-1]
    conv_g_mine = lax.dynamic_slice_in_dim(conv_g_full, _flat(xi, yi, ci) * cw, cw, axis=1)
    g, d, nm, nv = _adamw(w2d["conv_w"], m2d["conv_w"], v2d["conv_w"], conv_g_mine[None], "adamw_conv_w")
    grads["conv_w"], deltas["conv_w"], new_m["conv_w"], new_v["conv_w"] = g, d, nm, nv

    shaped = lambda dct: [(dct[n].T if n in _TRANSPOSED else dct[n]).reshape(shapes[n]) for n in _ORDER]
    return (loss, dx[None], *shaped(grads), *shaped(deltas), *shaped(new_m), *shaped(new_v))
```

```python
from typing import NamedTuple

import jax
import jax.numpy as jnp
from jax import lax
from jax.experimental import pallas as pl
from jax.experimental.pallas import tpu as pltpu
from jax.experimental.pallas import tpu_sc as plsc

F32 = jnp.float32
MXU = jnp.bfloat16
EPS = 1e-6
N_HEADS = 8
HEAD_DIM = 64
D_CONV = 512
D_ATTN = N_HEADS * HEAD_DIM
CONV_TAPS = 31
CONV_HALO = 32
SCALE = HEAD_DIM ** -0.5
NEG = -1e30
LANES = 128
N_DEV = 8
N_CHIPS = N_DEV // 2
MESH = pl.DeviceIdType.MESH
MIB = 1 << 20

ADAM_LR = 0.001
ADAM_B1 = 0.9
ADAM_B2 = 0.999
ADAM_EPS = 1e-08
ADAM_WD = 0.01
ADAM_STEP = 10


_UNREAD = pl.BlockSpec(memory_space=pl.ANY)


def _pallas_call(body, *, out_shape, **kwargs):
    in_hbm = lambda s: pltpu.HBM(s.shape, s.dtype)
    outs = [in_hbm(s) for s in out_shape] if isinstance(out_shape, (list, tuple)) else in_hbm(out_shape)
    call = pl.pallas_call(body, out_shape=outs, **kwargs)
    return lambda *operands: call(*[pltpu.with_memory_space_constraint(a, pltpu.HBM) for a in operands])


def _params(vmem_mib, n_axes):
    return pltpu.CompilerParams(dimension_semantics=("arbitrary",) * n_axes, vmem_limit_bytes=vmem_mib * MIB)


def _mm(a, b):
    return jnp.dot(a, b, preferred_element_type=F32)


def _mm_nt(a, b):
    return lax.dot_general(a, b, (((1,), (1,)), ((), ())), preferred_element_type=F32)


def _mm_tn(a, b):
    return lax.dot_general(a, b, (((0,), (0,)), ((), ())), preferred_element_type=F32)


def _rms_fwd(x, g):
    r = lax.rsqrt(jnp.mean(x * x, axis=-1, keepdims=True) + EPS)
    return x * r * g, r


def _rms_bwd(x, r, g, dy):
    gdy = dy * g
    dx = r * gdy - x * (r * r * r) * jnp.mean(x * gdy, axis=-1, keepdims=True)
    dg = jnp.sum(dy * x * r, axis=0, keepdims=True)
    return dx, dg


def _silu_grad(z, sz):
    return sz * (1.0 + z * (1.0 - sz))


def _three_terms(x):
    x1 = x.astype(jnp.bfloat16)
    r1 = x - x1.astype(F32)
    x2 = r1.astype(jnp.bfloat16)
    x3 = (r1 - x2.astype(F32)).astype(jnp.bfloat16)
    return x1, x2, x3


def _exact_tri_dot(tri, x):
    x1, x2, x3 = _three_terms(x)
    return _mm(tri, x1) + _mm(tri, x2) + _mm(tri, x3)


def _exact_dot_01(x, sel):
    x1, x2, x3 = _three_terms(x)
    return _mm(x1, sel) + _mm(x2, sel) + _mm(x3, sel)


def _tile(n, want):
    t = min(n, want)
    assert n % t == 0
    return t


_FFN_CHUNK = 256


def _ffn_up(x, g, w13, name):
    T, D = x.shape
    _, J, bf, _ = w13.shape
    tm = _tile(T, 512)
    I = T // tm

    def body(x_ref, g_ref, w13_ref, G_ref, U_ref, A_ref, h_s):
        j = pl.program_id(0)
        i = pl.program_id(1)
        rows = pl.ds(pl.multiple_of(i * tm, tm), tm)

        @pl.when(j == 0)
        def _():
            h, _ = _rms_fwd(x_ref[...], g_ref[...])
            h_s[rows, :] = h.astype(MXU)

        chunks = [slice(r0, r0 + _FFN_CHUNK) for r0 in range(0, tm, _FFN_CHUNK)]
        hbs = [h_s[pl.ds(pl.multiple_of(i * tm + rs.start, _FFN_CHUNK), _FFN_CHUNK), :] for rs in chunks]
        GU = [(_mm_nt(hb, w13_ref[0]), _mm_nt(hb, w13_ref[1])) for hb in hbs]
        for rs, (G, U) in zip(chunks, GU):
            G_ref[rs, :] = G.astype(MXU)
            U_ref[rs, :] = U.astype(MXU)
            A_ref[rs, :] = (G * jax.nn.sigmoid(G) * U).astype(MXU)

    blk = pl.BlockSpec((None, tm, bf), lambda j, i: (j, i, 0))
    hid = jax.ShapeDtypeStruct((J, T, bf), MXU)
    return _pallas_call(
        body, name=name, grid=(J, I),
        in_specs=[pl.BlockSpec((tm, D), lambda j, i: (jnp.where(j == 0, i, I - 1), 0)),
                  pl.BlockSpec((1, D), lambda j, i: (0, 0)),
                  pl.BlockSpec((2, None, bf, D), lambda j, i: (0, j, 0, 0))],
        out_specs=[blk, blk, blk],
        out_shape=[hid, hid, hid],
        scratch_shapes=[pltpu.VMEM((T, D), MXU)],
        compiler_params=_params(40, 2),
    )(x, g, w13)


def _ffn_down(x, A, w2, name):
    T, D = x.shape
    J, _, bf = A.shape
    tm = _tile(T, 512)

    def body(x_ref, A_ref, w2_ref, xo_ref):
        f = _mm(A_ref[0], w2_ref[0:bf, :])
        for j in range(1, J):
            f = f + _mm(A_ref[j], w2_ref[j * bf:(j + 1) * bf, :])
        xo_ref[...] = x_ref[...] + 0.5 * f

    row = pl.BlockSpec((tm, D), lambda i: (i, 0))
    return _pallas_call(
        body, name=name, grid=(T // tm,),
        in_specs=[row, pl.BlockSpec((J, tm, bf), lambda i: (0, i, 0)), pl.BlockSpec((J * bf, D), lambda i: (0, 0))],
        out_specs=row,
        out_shape=jax.ShapeDtypeStruct((T, D), F32),
        compiler_params=_params(48, 1),
    )(x, A, w2)


def _ffn_bwd_act(x, g, dy, Gs, Us, w13, w2, name, after=()):
    T, D = x.shape
    _, J, bf, _ = w13.shape
    tm = _tile(T, 512)
    I = T // tm

    def body(x_ref, g_ref, dy_ref, G_ref, U_ref, w13_ref, w2_ref, *rest):
        dx_ref, dg_ref, h_ref, dG_ref, dU_ref, dh_s, dF_s, h_s = rest[len(after):]
        j = pl.program_id(0)
        i = pl.program_id(1)
        rows = pl.ds(pl.multiple_of(i * tm, tm), tm)

        @pl.when(j == 0)
        def _():
            h, _ = _rms_fwd(x_ref[...], g_ref[...])
            hb = h.astype(MXU)
            h_s[rows, :] = hb
            h_ref[...] = hb
            dF_s[rows, :] = (0.5 * dy_ref[...]).astype(MXU)
            dh_s[rows, :] = jnp.zeros((tm, D), F32)

        chunks = [slice(r0, r0 + _FFN_CHUNK) for r0 in range(0, tm, _FFN_CHUNK)]
        crows = [pl.ds(pl.multiple_of(i * tm + rs.start, _FFN_CHUNK), _FFN_CHUNK) for rs in chunks]
        dAs = [_mm_nt(dF_s[cr, :], w2_ref[...]) for cr in crows]
        for rs, cr, dA in zip(chunks, crows, dAs):
            G = G_ref[rs, :].astype(F32)
            U = U_ref[rs, :].astype(F32)
            sg = jax.nn.sigmoid(G)
            s = G * sg
            dUb = (dA * s).astype(MXU)
            dGb = (dA * U * _silu_grad(G, sg)).astype(MXU)
            dG_ref[rs, :] = dGb
            dU_ref[rs, :] = dUb
            dh_s[cr, :] += _mm(dGb, w13_ref[0]) + _mm(dUb, w13_ref[1])

        @pl.when(j == J - 1)
        def _():
            xv = x_ref[...]
            gv = g_ref[...]
            _, r = _rms_fwd(xv, gv)
            dxn, dgp = _rms_bwd(xv, r, gv, dh_s[rows, :])
            dx_ref[...] = dy_ref[...] + dxn

            @pl.when(i == 0)
            def _():
                dg_ref[...] = dgp

            @pl.when(i > 0)
            def _():
                dg_ref[...] += dgp

    ends = lambda j, i: (jnp.where((j == 0) | (j == J - 1), i, I - 1), 0)
    blk = pl.BlockSpec((None, tm, bf), lambda j, i: (j, i, 0))
    hid = jax.ShapeDtypeStruct((J, T, bf), MXU)
    return _pallas_call(
        body, name=name, grid=(J, I),
        in_specs=[pl.BlockSpec((tm, D), ends), pl.BlockSpec((1, D), lambda j, i: (0, 0)), pl.BlockSpec((tm, D), ends),
                  blk, blk, pl.BlockSpec((2, None, bf, D), lambda j, i: (0, j, 0, 0)),
                  pl.BlockSpec((bf, D), lambda j, i: (j, 0))] + [_UNREAD] * len(after),
        out_specs=[pl.BlockSpec((tm, D), lambda j, i: (jnp.where(j == J - 1, i, 0), 0)),
                   pl.BlockSpec((1, D), lambda j, i: (0, 0)),
                   pl.BlockSpec((tm, D), lambda j, i: (jnp.where(j == 0, i, I - 1), 0)), blk, blk],
        out_shape=[jax.ShapeDtypeStruct((T, D), F32), jax.ShapeDtypeStruct((1, D), F32),
                   jax.ShapeDtypeStruct((T, D), MXU), hid, hid],
        scratch_shapes=[pltpu.VMEM((T, D), F32), pltpu.VMEM((T, D), MXU), pltpu.VMEM((T, D), MXU)],
        compiler_params=_params(58, 2),
    )(x, g, dy, Gs, Us, w13, w2, *after)


def _ffn_w13_grad(h, dG, dU, name):
    T, D = h.shape
    J, _, bf = dG.shape

    def body(h_ref, dG_ref, dU_ref, dw13_ref):
        dw13_ref[0] = _mm_tn(dG_ref[...], h_ref[...]).astype(dw13_ref.dtype)
        dw13_ref[1] = _mm_tn(dU_ref[...], h_ref[...]).astype(dw13_ref.dtype)

    blk = pl.BlockSpec((None, T, bf), lambda j: (j, 0, 0))
    return _pallas_call(
        body, name=name, grid=(J,),
        in_specs=[pl.BlockSpec((T, D), lambda j: (0, 0)), blk, blk],
        out_specs=pl.BlockSpec((2, None, bf, D), lambda j: (0, j, 0, 0)),
        out_shape=jax.ShapeDtypeStruct((2, J, bf, D), MXU),
        compiler_params=_params(48, 1),
    )(h, dG, dU)


def _ffn_w2_grad(dy, A, name, after=()):
    T, D = dy.shape
    J, _, bf = A.shape

    def body(dy_ref, A_ref, *rest):
        dw2_ref, dF_s = rest[len(after):]

        @pl.when(pl.program_id(0) == 0)
        def _():
            dF_s[...] = (0.5 * dy_ref[...]).astype(MXU)

        dw2_ref[...] = _mm_tn(A_ref[...], dF_s[...]).astype(dw2_ref.dtype)

    return _pallas_call(
        body, name=name, grid=(J,),
        in_specs=[pl.BlockSpec((T, D), lambda j: (0, 0)), pl.BlockSpec((None, T, bf), lambda j: (j, 0, 0))]
        + [_UNREAD] * len(after),
        out_specs=pl.BlockSpec((bf, D), lambda j: (j, 0)),
        out_shape=jax.ShapeDtypeStruct((J * bf, D), MXU),
        scratch_shapes=[pltpu.VMEM((T, D), MXU)],
        compiler_params=_params(48, 1),
    )(dy, A, *after)


_AG0, _Q0, _K0, _V0, _F0 = 0, 2 * D_CONV, 2 * D_CONV + D_ATTN, 2 * D_CONV + 2 * D_ATTN, 2 * D_CONV + 3 * D_ATTN
N_IN = _F0 + N_HEADS
N_IN_PAD = _F0 + LANES
_IN_BLOCK = N_IN // N_DEV


def _rows_from_blocks(blocks_ref, rows_ref):
    for p in range(N_DEV):
        rows_ref[_IN_BLOCK * p:_IN_BLOCK * (p + 1), :] = blocks_ref[p]
    rows_ref[N_IN:, :] = jnp.zeros((N_IN_PAD - N_IN, rows_ref.shape[1]), rows_ref.dtype)


def _inproj_fwd(x1, gm, win_blocks, name):
    T, D = x1.shape
    tm = _tile(T, 256)

    def body(x_ref, g_ref, wb_ref, ag_ref, k_ref, v_ref, qT_ref, kT_ref, vT_ref, fl_ref, w_ref):
        @pl.when(pl.program_id(0) == 0)
        def _():
            _rows_from_blocks(wb_ref, w_ref)

        h, _ = _rms_fwd(x_ref[...], g_ref[...])
        hb = h.astype(MXU)
        ag_ref[...] = _mm_nt(hb, w_ref[_AG0:_Q0, :])
        qT_ref[...] = (_mm_nt(hb, w_ref[_Q0:_K0, :]) * SCALE).T.astype(MXU)
        for c0, ref, refT in ((_K0, k_ref, kT_ref), (_V0, v_ref, vT_ref)):
            y = _mm_nt(hb, w_ref[c0:c0 + D_ATTN, :])
            ref[...] = y.astype(MXU)
            refT[...] = y.T.astype(MXU)
        fl_ref[...] = _mm_nt(hb, w_ref[_F0:N_IN_PAD, :])

    row = lambda w: pl.BlockSpec((tm, w), lambda i: (i, 0))
    col = pl.BlockSpec((D_ATTN, tm), lambda i: (0, i))
    std = jax.ShapeDtypeStruct((T, D_ATTN), MXU)
    trn = jax.ShapeDtypeStruct((D_ATTN, T), MXU)
    return _pallas_call(
        body, name=name, grid=(T // tm,),
        in_specs=[row(D), pl.BlockSpec((1, D), lambda i: (0, 0)),
                  pl.BlockSpec((N_DEV, _IN_BLOCK, D), lambda i: (0, 0, 0))],
        out_specs=[row(2 * D_CONV), row(D_ATTN), row(D_ATTN), col, col, col, row(LANES)],
        out_shape=[jax.ShapeDtypeStruct((T, 2 * D_CONV), F32), std, std, trn, trn, trn,
                   jax.ShapeDtypeStruct((T, LANES), F32)],
        scratch_shapes=[pltpu.VMEM((N_IN_PAD, D), MXU)],
        compiler_params=_params(40, 1),
    )(x1, gm, win_blocks)


def _inproj_bwd_act(x1, gm, dx2, dag, dqT, dkT, dvT, dfl, win_blocks, name):
    T, D = x1.shape
    tm = _tile(T, 256)

    def body(x_ref, g_ref, dx2_ref, dag_ref, dqT_ref, dkT_ref, dvT_ref, dfl_ref, wb_ref, dx1_ref, dg_ref, h_ref,
             w_ref):
        i = pl.program_id(0)

        @pl.when(i == 0)
        def _():
            _rows_from_blocks(wb_ref, w_ref)

        xv = x_ref[...]
        gv = g_ref[...]
        h, r = _rms_fwd(xv, gv)
        h_ref[...] = h.astype(MXU)
        dh = _mm(dag_ref[...], w_ref[_AG0:_Q0, :])
        for c0, ref in ((_Q0, dqT_ref), (_K0, dkT_ref), (_V0, dvT_ref)):
            dh = dh + _mm_tn(ref[...].astype(MXU), w_ref[c0:c0 + D_ATTN, :])
        dh = dh + _mm(dfl_ref[...].astype(MXU), w_ref[_F0:N_IN_PAD, :])
        dxn, dgp = _rms_bwd(xv, r, gv, dh)
        dx1_ref[...] = dx2_ref[...] + dxn

        @pl.when(i == 0)
        def _():
            dg_ref[...] = dgp

        @pl.when(i > 0)
        def _():
            dg_ref[...] += dgp

    row = lambda w: pl.BlockSpec((tm, w), lambda i: (i, 0))
    col = pl.BlockSpec((D_ATTN, tm), lambda i: (0, i))
    full = lambda a, b: pl.BlockSpec((a, b), lambda i: (0, 0))
    return _pallas_call(
        body, name=name, grid=(T // tm,),
        in_specs=[row(D), full(1, D), row(D), row(2 * D_CONV), col, col, col, row(LANES),
                  pl.BlockSpec((N_DEV, _IN_BLOCK, D), lambda i: (0, 0, 0))],
        out_specs=[row(D), full(1, D), row(D)],
        out_shape=[jax.ShapeDtypeStruct((T, D), F32), jax.ShapeDtypeStruct((1, D), F32),
                   jax.ShapeDtypeStruct((T, D), MXU)],
        scratch_shapes=[pltpu.VMEM((N_IN_PAD, D), MXU)],
        compiler_params=_params(40, 1),
    )(x1, gm, dx2, dag, dqT, dkT, dvT, dfl, win_blocks)


def _inproj_bwd_weights(h, dag, dqT, dkT, dvT, dfl, name, after=()):
    T, D = h.shape

    def body(h_ref, dag_ref, dqT_ref, dkT_ref, dvT_ref, dfl_ref, *rest):
        blocks_ref, dw_ref = rest[len(after):]
        hb = h_ref[...]
        dw_ref[_AG0:_Q0, :] = _mm_tn(dag_ref[...], hb).astype(dw_ref.dtype)
        for c0, ref in ((_Q0, dqT_ref), (_K0, dkT_ref), (_V0, dvT_ref)):
            dw_ref[c0:c0 + D_ATTN, :] = _mm(ref[...].astype(MXU), hb).astype(dw_ref.dtype)
        dw_ref[_F0:N_IN_PAD, :] = _mm_tn(dfl_ref[...].astype(MXU), hb).astype(dw_ref.dtype)
        for p in range(N_DEV):
            blocks_ref[p] = dw_ref[_IN_BLOCK * p:_IN_BLOCK * (p + 1), :]

    vmem = pl.BlockSpec(memory_space=pltpu.VMEM)
    return _pallas_call(
        body, name=name, in_specs=[vmem] * 6 + [_UNREAD] * len(after), out_specs=vmem,
        out_shape=jax.ShapeDtypeStruct((N_DEV, _IN_BLOCK, D), MXU),
        scratch_shapes=[pltpu.VMEM((N_IN_PAD, D), MXU)],
        compiler_params=pltpu.CompilerParams(vmem_limit_bytes=56 * MIB),
    )(h, dag, dqT, dkT, dvT, dfl, *after)


def _forget_fwd(fl, fbp, name):
    T = fl.shape[0]
    tb = _tile(T, 256)

    def body(fl_ref, fb_ref, cum_ref, cumT_ref):
        ri = lax.broadcasted_iota(jnp.int32, (tb, tb), 0)
        ci = lax.broadcasted_iota(jnp.int32, (tb, tb), 1)
        tri = (ri >= ci).astype(jnp.bfloat16)
        carry = jnp.zeros((1, LANES), F32)
        for b in range(T // tb):
            z = fl_ref[b * tb:(b + 1) * tb, :] + fb_ref[...]
            lf = jnp.minimum(z, 0.0) - jnp.log1p(jnp.exp(-jnp.abs(z)))
            c = _exact_tri_dot(tri, lf) + carry
            cum_ref[b * tb:(b + 1) * tb, :] = c
            carry = c[tb - 1:tb, :]
        cumT_ref[...] = cum_ref[...].T[:N_HEADS, :]

    return _pallas_call(
        body, name=name,
        out_shape=[jax.ShapeDtypeStruct((T, LANES), F32), jax.ShapeDtypeStruct((N_HEADS, T), F32)],
        compiler_params=pltpu.CompilerParams(vmem_limit_bytes=32 * MIB),
    )(fl, fbp)


def _forget_bwd(dcum, fl, fbp, name):
    T = fl.shape[0]
    tb = _tile(T, 256)

    def body(dc_ref, fl_ref, fb_ref, dfl_ref, dfb_ref):
        ri = lax.broadcasted_iota(jnp.int32, (tb, tb), 0)
        ci = lax.broadcasted_iota(jnp.int32, (tb, tb), 1)
        tri = (ri <= ci).astype(jnp.bfloat16)
        carry = jnp.zeros((1, LANES), F32)
        dfb = jnp.zeros((1, LANES), F32)
        for b in reversed(range(T // tb)):
            sl = slice(b * tb, (b + 1) * tb)
            dl = _exact_tri_dot(tri, dc_ref[sl, :]) + carry
            carry = dl[0:1, :]
            z = fl_ref[sl, :] + fb_ref[...]
            dfl = dl * jax.nn.sigmoid(-z)
            dfl_ref[sl, :] = dfl
            dfb = dfb + jnp.sum(dfl, axis=0, keepdims=True)
        dfb_ref[...] = dfb

    return _pallas_call(
        body, name=name,
        out_shape=[jax.ShapeDtypeStruct((T, LANES), F32), jax.ShapeDtypeStruct((1, LANES), F32)],
        compiler_params=pltpu.CompilerParams(vmem_limit_bytes=32 * MIB),
    )(dcum, fl, fbp)


def _causal_keep(i, j, tq, tk):
    key = j * tk + lax.broadcasted_iota(jnp.int32, (tk, tq), 0)
    qry = i * tq + lax.broadcasted_iota(jnp.int32, (tk, tq), 1)
    return key <= qry


def _split_hi_lo(x):
    hi = x.astype(MXU)
    lo = (x - hi.astype(F32)).astype(MXU)
    return hi, lo


def _attn_fwd(qT, k, vT, cum, cumT, name):
    T = k.shape[0]
    tq = _tile(T, 256)
    tk = _tile(tq, 256)
    kpq = tq // tk
    heads = [slice(HEAD_DIM * h, HEAD_DIM * (h + 1)) for h in range(N_HEADS)]

    def body(qT_ref, k_ref, vT_ref, cum_ref, cumT_ref, o_ref, lseT_ref, acc_s, m_s, l_s):
        i = pl.program_id(0)
        acc_s[...] = jnp.zeros_like(acc_s)
        m_s[...] = jnp.full_like(m_s, NEG)
        l_s[...] = jnp.zeros_like(l_s)

        def kblock(j, masked):
            rows = pl.ds(pl.multiple_of(j * tk, tk), tk)
            keep = _causal_keep(i, j, tq, tk) if masked else None
            bias = [cumT_ref[h:h + 1, :] - cum_ref[rows, h:h + 1] for h in range(N_HEADS)]
            qk = [_mm(k_ref[rows, hs], qT_ref[hs, :]) + bias[h] for h, hs in enumerate(heads)]
            for h, hs in enumerate(heads):
                sT = qk[h]
                if masked:
                    sT = jnp.where(keep, sT, NEG)
                m_old = m_s[h:h + 1, :]
                m_new = jnp.maximum(m_old, jnp.max(sT, axis=0, keepdims=True))
                alpha = jnp.exp(m_old - m_new)
                pT = jnp.exp(sT - m_new)
                l_s[h:h + 1, :] = alpha * l_s[h:h + 1, :] + jnp.sum(pT, axis=0, keepdims=True)
                p_hi, p_lo = _split_hi_lo(pT)
                vh = vT_ref[hs, rows]
                acc_s[hs, :] = alpha * acc_s[hs, :] + (_mm(vh, p_hi) + _mm(vh, p_lo))
                m_s[h:h + 1, :] = m_new

        def unmasked(j, c):
            kblock(j, False)
            return c

        lax.fori_loop(0, kpq * i, unmasked, 0)
        for d in range(kpq):
            kblock(kpq * i + d, True)
        for h, hs in enumerate(heads):
            acc_s[hs, :] = acc_s[hs, :] / l_s[h:h + 1, :]
        o_ref[...] = acc_s[...].T
        lseT_ref[...] = m_s[...] + jnp.log(l_s[...])

    full = lambda a, b: pl.BlockSpec((a, b), lambda i: (0, 0))
    colblk = lambda r: pl.BlockSpec((r, tq), lambda i: (0, i))
    return _pallas_call(
        body, name=name, grid=(T // tq,),
        in_specs=[colblk(D_ATTN), full(T, D_ATTN), full(D_ATTN, T), full(T, LANES), colblk(N_HEADS)],
        out_specs=[pl.BlockSpec((tq, D_ATTN), lambda i: (i, 0)), colblk(N_HEADS)],
        out_shape=[jax.ShapeDtypeStruct((T, D_ATTN), F32), jax.ShapeDtypeStruct((N_HEADS, T), F32)],
        scratch_shapes=[pltpu.VMEM((D_ATTN, tq), F32), pltpu.VMEM((N_HEADS, tq), F32),
                        pltpu.VMEM((N_HEADS, tq), F32)],
        compiler_params=_params(40, 1),
    )(qT, k, vT, cum, cumT)


def _attn_bwd(qT, k, kT, v, doT, lseT, deltaT, cum, cumT, name, after=()):
    T = k.shape[0]
    tq = _tile(T, 256)
    tk = _tile(tq, 256)
    kpq = tq // tk
    heads = [slice(HEAD_DIM * h, HEAD_DIM * (h + 1)) for h in range(N_HEADS)]

    def body(qT_ref, k_ref, kT_ref, v_ref, doT_ref, lseT_ref, dlT_ref, cum_ref, cumT_ref, *rest):
        dq_ref, dk_ref, dv_ref, dcum_ref, dq_s = rest[len(after):]
        i = pl.program_id(0)

        @pl.when(i == 0)
        def _():
            dk_ref[...] = jnp.zeros_like(dk_ref)
            dv_ref[...] = jnp.zeros_like(dv_ref)
            dcum_ref[...] = jnp.zeros_like(dcum_ref)

        dq_s[...] = jnp.zeros_like(dq_s)

        def kblock(j, masked):
            rows = pl.ds(pl.multiple_of(j * tk, tk), tk)
            keep = _causal_keep(i, j, tq, tk) if masked else None
            bias = [cumT_ref[h:h + 1, :] - cum_ref[rows, h:h + 1] for h in range(N_HEADS)]
            qk = [_mm(k_ref[rows, hs], qT_ref[hs, :]) + bias[h] for h, hs in enumerate(heads)]
            dps = [_mm(v_ref[rows, hs], doT_ref[hs, :]) for hs in heads]
            for h, hs in enumerate(heads):
                sT = qk[h]
                if masked:
                    sT = jnp.where(keep, sT, NEG)
                pT = jnp.exp(sT - lseT_ref[h:h + 1, :])
                dsT = pT * (dps[h] - dlT_ref[h:h + 1, :])
                dcum_ref[rows, h:h + 1] += -jnp.sum(dsT, axis=1, keepdims=True)
                dsb = dsT.astype(MXU)
                dv_ref[hs, rows] += _mm_nt(doT_ref[hs, :], pT.astype(MXU))
                dk_ref[hs, rows] += _mm_nt(qT_ref[hs, :], dsb)
                dq_s[hs, :] += _mm(kT_ref[hs, rows], dsb)

        def unmasked(j, c):
            kblock(j, False)
            return c

        lax.fori_loop(0, kpq * i, unmasked, 0)
        for d in range(kpq):
            kblock(kpq * i + d, True)
        dq_ref[...] = (dq_s[...] * SCALE).astype(dq_ref.dtype)

    full = lambda a, b: pl.BlockSpec((a, b), lambda i: (0, 0))
    colblk = lambda r: pl.BlockSpec((r, tq), lambda i: (0, i))
    return _pallas_call(
        body, name=name, grid=(T // tq,),
        in_specs=[colblk(D_ATTN), full(T, D_ATTN), full(D_ATTN, T), full(T, D_ATTN), colblk(D_ATTN),
                  colblk(N_HEADS), colblk(N_HEADS), full(T, LANES), colblk(N_HEADS)] + [_UNREAD] * len(after),
        out_specs=[colblk(D_ATTN), full(D_ATTN, T), full(D_ATTN, T), full(T, LANES)],
        out_shape=[
            jax.ShapeDtypeStruct((D_ATTN, T), MXU),
            jax.ShapeDtypeStruct((D_ATTN, T), F32),
            jax.ShapeDtypeStruct((D_ATTN, T), F32),
            jax.ShapeDtypeStruct((T, LANES), F32),
        ],
        scratch_shapes=[pltpu.VMEM((D_ATTN, tq), F32)],
        compiler_params=_params(48, 1),
    )(qT, k, kT, v, doT, lseT, deltaT, cum, cumT, *after)


_ROWS_PER_CHUNK = 64


def _glu_halo(ag_ref, agh_ref, uext_s, tm, first):
    a = ag_ref[:, :D_CONV]
    sg = jax.nn.sigmoid(ag_ref[:, D_CONV:])
    uh = agh_ref[:, :D_CONV] * jax.nn.sigmoid(agh_ref[:, D_CONV:])
    uext_s[0:CONV_HALO, :] = jnp.where(first, 0.0, uh)
    uext_s[CONV_HALO:CONV_HALO + tm, :] = a * sg
    return a, sg


_SUBLANES = 8


def _shifted_copies(ext_s, sh_s, rows):
    for k in range(1, _SUBLANES):
        sh_s[k, 0:rows, :] = ext_s[pl.ds(k, rows), :]


def _window(ext_s, sh_s, start, rows):
    k = start % _SUBLANES
    if k == 0:
        return ext_s[pl.ds(start, rows), :]
    return sh_s[k, pl.ds(start - k, rows), :]


def _layer_norm_stats(y):
    mu = jnp.mean(y, axis=-1, keepdims=True)
    xc = y - mu
    rs = lax.rsqrt(jnp.mean(xc * xc, axis=-1, keepdims=True) + EPS)
    return xc * rs, rs


def _conv_fwd(ag, w32, cb, lg, lb, name):
    T = ag.shape[0]
    tm = _tile(T, 256)
    rc = _tile(tm, _ROWS_PER_CHUNK)
    hb = tm // CONV_HALO

    def body(ag_ref, agh_ref, w_ref, cb_ref, lg_ref, lb_ref, yc_ref, c_ref, uext_s, ush_s):
        i = pl.program_id(0)
        _glu_halo(ag_ref, agh_ref, uext_s, tm, i == 0)
        _shifted_copies(uext_s, ush_s, tm + CONV_HALO - _SUBLANES)
        for r0 in range(0, tm, rc):
            acc = jnp.zeros((rc, D_CONV), F32)
            for t in range(CONV_TAPS):
                acc = acc + _window(uext_s, ush_s, r0 + CONV_HALO - (CONV_TAPS - 1) + t, rc) * w_ref[t:t + 1, :]
            y = acc + cb_ref[...]
            yc_ref[r0:r0 + rc, :] = y
            n, _ = _layer_norm_stats(y)
            z = n * lg_ref[...] + lb_ref[...]
            c_ref[r0:r0 + rc, :] = z * jax.nn.sigmoid(z)

    row = lambda w: pl.BlockSpec((tm, w), lambda i: (i, 0))
    full = lambda a, b: pl.BlockSpec((a, b), lambda i: (0, 0))
    return _pallas_call(
        body, name=name, grid=(T // tm,),
        in_specs=[row(2 * D_CONV),
                  pl.BlockSpec((CONV_HALO, 2 * D_CONV), lambda i: (jnp.maximum(i * hb - 1, 0), 0)),
                  full(CONV_HALO, D_CONV), full(1, D_CONV), full(1, D_CONV), full(1, D_CONV)],
        out_specs=[row(D_CONV), row(D_CONV)],
        out_shape=[jax.ShapeDtypeStruct((T, D_CONV), F32), jax.ShapeDtypeStruct((T, D_CONV), F32)],
        scratch_shapes=[pltpu.VMEM((CONV_HALO + tm, D_CONV), F32),
                        pltpu.VMEM((_SUBLANES, CONV_HALO + tm, D_CONV), F32)],
        compiler_params=_params(32, 1),
    )(ag, ag, w32, cb, lg, lb)


def _conv_bwd(dc, yc, ag, w32, lg, lb, name):
    T = ag.shape[0]
    tm = _tile(T, 256)
    rc = _tile(tm, _ROWS_PER_CHUNK)
    I = T // tm
    hb = tm // CONV_HALO
    n_halo_blocks = T // CONV_HALO

    def body(dc_ref, yc_ref, dch_ref, ych_ref, ag_ref, agh_ref, w_ref, lg_ref, lb_ref,
             dag_ref, dw_ref, dcb_ref, dlg_ref, dlb_ref, uext_s, dext_s, ush_s, dsh_s):
        i = pl.program_id(0)
        lgv = lg_ref[...]
        lbv = lb_ref[...]

        def ln_bwd(dcv, ycv):
            n, rs = _layer_norm_stats(ycv)
            z = n * lgv + lbv
            dz = dcv * _silu_grad(z, jax.nn.sigmoid(z))
            dn = dz * lgv
            dy = rs * (dn - jnp.mean(dn, axis=-1, keepdims=True) - n * jnp.mean(dn * n, axis=-1, keepdims=True))
            return dy, dz, n

        dy, dz, n = ln_bwd(dc_ref[...], yc_ref[...])
        dyh, _, _ = ln_bwd(dch_ref[...], ych_ref[...])
        dext_s[0:tm, :] = dy
        dext_s[tm:tm + CONV_HALO, :] = jnp.where(i == I - 1, 0.0, dyh)
        a, sg = _glu_halo(ag_ref, agh_ref, uext_s, tm, i == 0)
        _shifted_copies(uext_s, ush_s, tm + CONV_HALO - _SUBLANES)
        _shifted_copies(dext_s, dsh_s, tm + CONV_HALO - _SUBLANES)

        @pl.when(i == 0)
        def _():
            dw_ref[...] = jnp.zeros_like(dw_ref)
            dcb_ref[...] = jnp.zeros_like(dcb_ref)
            dlg_ref[...] = jnp.zeros_like(dlg_ref)
            dlb_ref[...] = jnp.zeros_like(dlb_ref)

        dcb_ref[...] += jnp.sum(dy, axis=0, keepdims=True)
        dlg_ref[...] += jnp.sum(dz * n, axis=0, keepdims=True)
        dlb_ref[...] += jnp.sum(dz, axis=0, keepdims=True)
        for t in range(CONV_TAPS):
            u_t = _window(uext_s, ush_s, CONV_HALO - (CONV_TAPS - 1) + t, tm)
            dw_ref[t:t + 1, :] += jnp.sum(dy * u_t, axis=0, keepdims=True)
        for r0 in range(0, tm, rc):
            acc = jnp.zeros((rc, D_CONV), F32)
            for t in range(CONV_TAPS):
                acc = acc + _window(dext_s, dsh_s, r0 + (CONV_TAPS - 1) - t, rc) * w_ref[t:t + 1, :]
            a_c = a[r0:r0 + rc, :]
            sg_c = sg[r0:r0 + rc, :]
            dag_ref[r0:r0 + rc, :D_CONV] = (acc * sg_c).astype(dag_ref.dtype)
            dag_ref[r0:r0 + rc, D_CONV:] = (acc * a_c * sg_c * (1.0 - sg_c)).astype(dag_ref.dtype)

    row = lambda w: pl.BlockSpec((tm, w), lambda i: (i, 0))
    full = lambda a, b: pl.BlockSpec((a, b), lambda i: (0, 0))
    nxt = pl.BlockSpec((CONV_HALO, D_CONV), lambda i: (jnp.minimum((i + 1) * hb, n_halo_blocks - 1), 0))
    return _pallas_call(
        body, name=name, grid=(I,),
        in_specs=[row(D_CONV), row(D_CONV), nxt, nxt, row(2 * D_CONV),
                  pl.BlockSpec((CONV_HALO, 2 * D_CONV), lambda i: (jnp.maximum(i * hb - 1, 0), 0)),
                  full(CONV_HALO, D_CONV), full(1, D_CONV), full(1, D_CONV)],
        out_specs=[row(2 * D_CONV), full(CONV_HALO, D_CONV), full(1, D_CONV), full(1, D_CONV), full(1, D_CONV)],
        out_shape=[
            jax.ShapeDtypeStruct((T, 2 * D_CONV), MXU),
            jax.ShapeDtypeStruct((CONV_HALO, D_CONV), F32),
            jax.ShapeDtypeStruct((1, D_CONV), F32),
            jax.ShapeDtypeStruct((1, D_CONV), F32),
            jax.ShapeDtypeStruct((1, D_CONV), F32),
        ],
        scratch_shapes=[pltpu.VMEM((CONV_HALO + tm, D_CONV), F32), pltpu.VMEM((tm + CONV_HALO, D_CONV), F32),
                        pltpu.VMEM((_SUBLANES, CONV_HALO + tm, D_CONV), F32),
                        pltpu.VMEM((_SUBLANES, CONV_HALO + tm, D_CONV), F32)],
        compiler_params=_params(40, 1),
    )(dc, yc, dc, yc, ag, ag, w32, lg, lb)


def _outproj_fwd(x1, c, o, gc, ga, wout, name):
    T, D = x1.shape
    tm = _tile(T, 512)

    def body(x_ref, c_ref, o_ref, gc_ref, ga_ref, w_ref, x2_ref):
        yc, _ = _rms_fwd(c_ref[...], gc_ref[...])
        ya, _ = _rms_fwd(o_ref[...], ga_ref[...])
        x2_ref[...] = (x_ref[...] + _mm(yc.astype(MXU), w_ref[:D_CONV, :])
                       + _mm(ya.astype(MXU), w_ref[D_CONV:, :]))

    row = lambda w: pl.BlockSpec((tm, w), lambda i: (i, 0))
    full = lambda a, b: pl.BlockSpec((a, b), lambda i: (0, 0))
    return _pallas_call(
        body, name=name, grid=(T // tm,),
        in_specs=[row(D), row(D_CONV), row(D_ATTN), full(1, D_CONV), full(1, D_ATTN), full(D_CONV + D_ATTN, D)],
        out_specs=row(D),
        out_shape=jax.ShapeDtypeStruct((T, D), F32),
        compiler_params=_params(32, 1),
    )(x1, c, o, gc, ga, wout)


def _outproj_bwd(dx2, c, o, gc, ga, wout, name):
    T, D = dx2.shape
    tm = _tile(T, 256)
    I = T // tm

    def body(dx_ref, c_ref, o_ref, gc_ref, ga_ref, w_ref,
             dc_ref, doT_ref, dlT_ref, dw_ref, dgc_ref, dga_ref, acc_s):
        i = pl.program_id(0)
        dxb = dx_ref[...].astype(MXU)
        cv = c_ref[...]
        ov = o_ref[...]
        yc, rcn = _rms_fwd(cv, gc_ref[...])
        ya, ra = _rms_fwd(ov, ga_ref[...])
        dyc = _mm_nt(dxb, w_ref[:D_CONV, :])
        dya = _mm_nt(dxb, w_ref[D_CONV:, :])
        dwc = _mm_tn(yc.astype(MXU), dxb)
        dwa = _mm_tn(ya.astype(MXU), dxb)
        dcv, dgc = _rms_bwd(cv, rcn, gc_ref[...], dyc)
        dov, dga = _rms_bwd(ov, ra, ga_ref[...], dya)
        dc_ref[...] = dcv
        dob = dov.astype(doT_ref.dtype)
        doT_ref[...] = dov.T.astype(doT_ref.dtype)
        chan = lax.broadcasted_iota(jnp.int32, (D_ATTN, LANES), 0)
        head = lax.broadcasted_iota(jnp.int32, (D_ATTN, LANES), 1)
        in_head = ((chan >= head * HEAD_DIM) & (chan < (head + 1) * HEAD_DIM)).astype(jnp.bfloat16)
        dlT_ref[...] = _exact_dot_01(dob.astype(F32) * ov, in_head).T[:N_HEADS, :]

        @pl.when(i == 0)
        def _():
            acc_s[:D_CONV, :] = dwc
            acc_s[D_CONV:, :] = dwa
            dgc_ref[...] = dgc
            dga_ref[...] = dga

        @pl.when(i > 0)
        def _():
            acc_s[:D_CONV, :] += dwc
            acc_s[D_CONV:, :] += dwa
            dgc_ref[...] += dgc
            dga_ref[...] += dga

        @pl.when(i == I - 1)
        def _():
            dw_ref[...] = acc_s[...].astype(dw_ref.dtype)

    row = lambda w: pl.BlockSpec((tm, w), lambda i: (i, 0))
    full = lambda a, b: pl.BlockSpec((a, b), lambda i: (0, 0))
    return _pallas_call(
        body, name=name, grid=(I,),
        in_specs=[row(D), row(D_CONV), row(D_ATTN), full(1, D_CONV), full(1, D_ATTN), full(D_CONV + D_ATTN, D)],
        out_specs=[row(D_CONV), pl.BlockSpec((D_ATTN, tm), lambda i: (0, i)),
                   pl.BlockSpec((N_HEADS, tm), lambda i: (0, i)),
                   full(D_CONV + D_ATTN, D), full(1, D_CONV), full(1, D_ATTN)],
        out_shape=[
            jax.ShapeDtypeStruct((T, D_CONV), F32),
            jax.ShapeDtypeStruct((D_ATTN, T), MXU),
            jax.ShapeDtypeStruct((N_HEADS, T), F32),
            jax.ShapeDtypeStruct((D_CONV + D_ATTN, D), MXU),
            jax.ShapeDtypeStruct((1, D_CONV), F32),
            jax.ShapeDtypeStruct((1, D_ATTN), F32),
        ],
        scratch_shapes=[pltpu.VMEM((D_CONV + D_ATTN, D), F32)],
        compiler_params=_params(40, 1),
    )(dx2, c, o, gc, ga, wout)


def _ffn_down_loss(x, A, w2, gf, target, name):
    T, D = x.shape
    J, _, bf = A.shape
    tm = _tile(T, 512)

    def body(x_ref, A_ref, w2_ref, g_ref, t_ref, loss_ref, dx_ref, dg_ref):
        i = pl.program_id(0)
        f = _mm(A_ref[0], w2_ref[0:bf, :])
        for j in range(1, J):
            f = f + _mm(A_ref[j], w2_ref[j * bf:(j + 1) * bf, :])
        xv = x_ref[...] + 0.5 * f
        gv = g_ref[...]
        out, r = _rms_fwd(xv, gv)
        err = out - t_ref[...]
        part = jnp.full((1, LANES), 0.5 / D, F32) * jnp.sum(err * err)
        dxn, dgp = _rms_bwd(xv, r, gv, err * (1.0 / D))
        dx_ref[...] = dxn

        @pl.when(i == 0)
        def _():
            loss_ref[...] = part
            dg_ref[...] = dgp

        @pl.when(i > 0)
        def _():
            loss_ref[...] += part
            dg_ref[...] += dgp

    row = lambda w: pl.BlockSpec((tm, w), lambda i: (i, 0))
    full = lambda a, b: pl.BlockSpec((a, b), lambda i: (0, 0))
    return _pallas_call(
        body, name=name, grid=(T // tm,),
        in_specs=[row(D), pl.BlockSpec((J, tm, bf), lambda i: (0, i, 0)), full(J * bf, D), full(1, D), row(D)],
        out_specs=[full(1, LANES), row(D), full(1, D)],
        out_shape=[jax.ShapeDtypeStruct((1, LANES), F32), jax.ShapeDtypeStruct((T, D), F32),
                   jax.ShapeDtypeStruct((1, D), F32)],
        compiler_params=_params(56, 1),
    )(x, A, w2, gf, target)


def _row_tile(rows):
    for cand in (256, 176, 128, 64, 32, 16):
        if rows % cand == 0:
            return cand
    return rows


def _adamw(w, m, v, parts, name):
    R, C = w.shape
    P = parts.shape[0]
    tr = _row_tile(R)
    c1 = 1.0 - ADAM_B1 ** ADAM_STEP
    c2 = 1.0 - ADAM_B2 ** ADAM_STEP

    def body(w_ref, m_ref, v_ref, p_ref, g_ref, d_ref, nm_ref, nv_ref):
        g = p_ref[0].astype(F32)
        for s in range(1, P):
            g = g + p_ref[s].astype(F32)
        wv = w_ref[...]
        mn = ADAM_B1 * m_ref[...] + (1.0 - ADAM_B1) * g
        vn = ADAM_B2 * v_ref[...] + (1.0 - ADAM_B2) * (g * g)
        g_ref[...] = g
        nm_ref[...] = mn
        nv_ref[...] = vn
        d_ref[...] = -ADAM_LR * ((mn / c1) / (jnp.sqrt(vn / c2) + ADAM_EPS) + ADAM_WD * wv)

    blk = pl.BlockSpec((tr, C), lambda i: (i, 0))
    out = jax.ShapeDtypeStruct((R, C), F32)
    return _pallas_call(
        body, name=name, grid=(R // tr,),
        in_specs=[blk, blk, blk, pl.BlockSpec((P, tr, C), lambda i: (0, i, 0))],
        out_specs=[blk, blk, blk, blk],
        out_shape=[out, out, out, out],
        compiler_params=_params(32, 1),
    )(w, m, v, parts)


def _position():
    return lax.axis_index("x"), lax.axis_index("y"), lax.axis_index("c")


def _flat(px, py, pc):
    return 4 * px + 2 * py + pc


def _gather_body(ins, outs, send_sems, recv_sems, local_sems, handshake):
    n = len(ins)
    x, y, c = _position()
    me, sibling = (x, y, c), (x, y, 1 - c)
    chips = [(1 - x, y), (x, 1 - y), (1 - x, 1 - y)]
    if handshake:
        _handshake([sibling] + [(*chip, cc) for chip in chips for cc in (c, 1 - c)])

    def copy(a, k, block, to, src=None):
        dst = outs[a].at[_flat(*block)]
        return pltpu.make_async_remote_copy(
            src_ref=dst if src is None else src, dst_ref=dst,
            send_sem=send_sems.at[a, k], recv_sem=recv_sems.at[a, k],
            device_id=to, device_id_type=MESH)

    mine = [pltpu.make_async_copy(ins[a], outs[a].at[_flat(*me)], local_sems.at[a]) for a in range(n)]
    for cp in mine:
        cp.start()
    first = []
    for a in range(n):
        first.append(copy(a, 0, me, sibling, src=ins[a]))
        first += [copy(a, 1 + j, me, (*chip, c), src=ins[a]) for j, chip in enumerate(chips)]
    for cp in first:
        cp.start()
    passed = []
    for a in range(n):
        for j, chip in enumerate(chips):
            copy(a, 1 + j, (*chip, c), me).wait_recv()
            fwd = copy(a, 4 + j, (*chip, c), sibling)
            fwd.start()
            passed.append(fwd)
    for a in range(n):
        copy(a, 0, sibling, me).wait_recv()
        for j, chip in enumerate(chips):
            copy(a, 4 + j, (*chip, 1 - c), me).wait_recv()
    for cp in first + passed:
        cp.wait_send()
    for cp in mine:
        cp.wait()


def _gather_scratch(n):
    return [pltpu.SemaphoreType.DMA((n, 7)), pltpu.SemaphoreType.DMA((n, 7)), pltpu.SemaphoreType.DMA((n,))]


def _all_gather(shards, name):
    n = len(shards)

    def body(*refs):
        _gather_body(refs[:n], refs[n:2 * n], *refs[2 * n:], handshake=False)

    hbm = pl.BlockSpec(memory_space=pltpu.HBM)
    return _pallas_call(
        body, name=name,
        in_specs=[hbm] * n, out_specs=[hbm] * n,
        out_shape=[jax.ShapeDtypeStruct((N_DEV,) + s.shape, s.dtype) for s in shards],
        scratch_shapes=_gather_scratch(n),
    )(*shards)


def _handshake(peers):
    barrier = pltpu.get_barrier_semaphore()
    for peer in peers:
        pl.semaphore_signal(barrier, inc=1, device_id=peer, device_id_type=MESH)
    pl.semaphore_wait(barrier, len(peers))


def _sequencer_call(body, name, collective_id, out_type, scratch_types, operands):
    return pl.kernel(
        body, name=name, out_type=out_type,
        mesh=plsc.ScalarSubcoreMesh(axis_name="sequencer", num_cores=1),
        scratch_types=scratch_types,
        compiler_params=pltpu.CompilerParams(collective_id=collective_id),
    )(*operands)


def _seq_all_gather(shards, name, collective_id, after):
    n = len(shards)

    def body(*refs):
        _gather_body(refs[:n], refs[n + 1:2 * n + 1], *refs[2 * n + 1:], handshake=True)

    return _sequencer_call(
        body, name, collective_id,
        [jax.ShapeDtypeStruct((N_DEV,) + s.shape, s.dtype) for s in shards],
        _gather_scratch(n), list(shards) + [after])


def _seq_to_sibling(parts, name, collective_id, after):
    n = len(parts)

    def body(*refs):
        ins, outs = refs[:n], refs[n + len(after):2 * n + len(after)]
        send_sems, recv_sems = refs[2 * n + len(after):]
        x, y, c = _position()
        sibling = (x, y, 1 - c)
        _handshake([sibling])
        sent = []
        for a in range(n):
            for q in range(N_CHIPS):
                cp = pltpu.make_async_remote_copy(
                    src_ref=ins[a].at[2 * q + (1 - c)], dst_ref=outs[a].at[q],
                    send_sem=send_sems.at[a, q], recv_sem=recv_sems.at[a, q],
                    device_id=sibling, device_id_type=MESH)
                cp.start()
                sent.append(cp)
        for cp in sent:
            cp.wait_recv()
        for cp in sent:
            cp.wait_send()

    return _sequencer_call(
        body, name, collective_id,
        [jax.ShapeDtypeStruct((N_CHIPS,) + p.shape[1:], p.dtype) for p in parts],
        [pltpu.SemaphoreType.DMA((n, N_CHIPS)), pltpu.SemaphoreType.DMA((n, N_CHIPS))],
        list(parts) + list(after))


def _seq_to_chips(partials, name, collective_id, after=()):
    n = len(partials)

    def body(*refs):
        ins, outs = refs[:n], refs[n + len(after):2 * n + len(after)]
        send_sems, recv_sems, local_sems = refs[2 * n + len(after):]
        x, y, c = _position()
        my_chip = 2 * x + y
        chips = [(1 - x, y), (x, 1 - y), (1 - x, 1 - y)]
        _handshake([(*chip, c) for chip in chips])
        mine = [pltpu.make_async_copy(ins[a].at[my_chip], outs[a].at[my_chip], local_sems.at[a]) for a in range(n)]
        for cp in mine:
            cp.start()
        sent = []
        for a in range(n):
            for j, (px, py) in enumerate(chips):
                cp = pltpu.make_async_remote_copy(
                    src_ref=ins[a].at[2 * px + py], dst_ref=outs[a].at[my_chip],
                    send_sem=send_sems.at[a, j], recv_sem=recv_sems.at[a, j],
                    device_id=(px, py, c), device_id_type=MESH)
                cp.start()
                sent.append(cp)
        for cp in sent:
            cp.wait_recv()
        for cp in sent:
            cp.wait_send()
        for cp in mine:
            cp.wait()

    return _sequencer_call(
        body, name, collective_id,
        [jax.ShapeDtypeStruct(p.shape, p.dtype) for p in partials],
        [pltpu.SemaphoreType.DMA((n, 3)), pltpu.SemaphoreType.DMA((n, 3)), pltpu.SemaphoreType.DMA((n,))],
        list(partials) + list(after))


def _pair_add(parts, recvs, name, after=()):
    n = len(parts)
    core = lax.axis_index("c").astype(jnp.int32).reshape(1)

    def body(c_ref, *refs):
        ps, rs, outs = refs[:n], refs[n:2 * n], refs[2 * n + len(after):]
        for p_ref, r_ref, o_ref in zip(ps, rs, outs):
            o_ref[...] = (p_ref[...].astype(F32) + r_ref[...].astype(F32)).astype(o_ref.dtype)

    mine = lambda p: pl.BlockSpec((None,) + p.shape[1:], lambda q, c: (2 * q + c[0], 0, 0))
    blk = lambda p: pl.BlockSpec((None,) + p.shape[1:], lambda q, c: (q, 0, 0))
    return pl.pallas_call(
        body, name=name,
        grid_spec=pltpu.PrefetchScalarGridSpec(
            num_scalar_prefetch=1, grid=(N_CHIPS,),
            in_specs=[mine(p) for p in parts] + [blk(p) for p in parts] + [_UNREAD] * len(after),
            out_specs=[blk(p) for p in parts]),
        out_shape=[pltpu.HBM((N_CHIPS,) + p.shape[1:], p.dtype) for p in parts],
        compiler_params=_params(40, 1),
    )(core, *[pltpu.with_memory_space_constraint(a, pltpu.HBM) for a in (*parts, *recvs, *after)])


class _Reduced(NamedTuple):
    partials: list
    reduced: list


def _blocks(g):
    return g.reshape(N_DEV, -1, g.shape[-1])


def _reduce_scatter(parts, tag, ids, after=(), between=None, add_after=()):
    from_sibling = _seq_to_sibling(parts, "rs_sibling_" + tag, ids[0], after)
    mid = between(from_sibling[0]) if between else ()
    partials = _pair_add(parts, from_sibling, "rs_add_" + tag, add_after)
    return _Reduced(partials, _seq_to_chips(partials, "rs_chips_" + tag, ids[1], mid))


_SMALL = ("ffn1_norm", "mix_norm", "conv_b", "conv_ln_g", "conv_ln_b", "forget_b", "out_norm_conv",
          "out_norm_attn", "ffn2_norm", "final_norm")
_PACK_WIDTH = 2 * D_CONV
_SLOT = dict(ffn1_norm=(0, 0), mix_norm=(1, 0), ffn2_norm=(2, 0), final_norm=(3, 0), conv_b=(4, 0),
             conv_ln_g=(4, D_CONV), conv_ln_b=(5, 0), out_norm_conv=(5, D_CONV), out_norm_attn=(6, 0),
             forget_b=(6, D_CONV))
_CONV_ROW0 = 8
_PACK_ROWS = _CONV_ROW0 + CONV_HALO


def _pack_small(small, name):
    arrays = [small[n] for n in _SMALL] + [small["conv_w"]]

    def body(*refs):
        out = refs[-1]
        out[...] = jnp.zeros_like(out)
        for n, ref in zip(_SMALL, refs):
            row, lane = _SLOT[n]
            out[row:row + 1, lane:lane + ref.shape[1]] = ref[...]
        out[_CONV_ROW0:, :D_CONV] = refs[len(_SMALL)][...]

    return _pallas_call(body, name=name, out_shape=jax.ShapeDtypeStruct((_PACK_ROWS, _PACK_WIDTH), F32))(*arrays)


def _adamw_small(gathered, w, m, v, name):
    c1 = 1.0 - ADAM_B1 ** ADAM_STEP
    c2 = 1.0 - ADAM_B2 ** ADAM_STEP
    k = len(_SMALL)

    def body(g_ref, *refs):
        ws, ms, vs = refs[:k], refs[k:2 * k], refs[2 * k:3 * k]
        outs = refs[3 * k:]
        total = g_ref[0]
        for s in range(1, N_DEV):
            total = total + g_ref[s]
        for i, n in enumerate(_SMALL):
            row, lane = _SLOT[n]
            width = ws[i].shape[1]
            g = total[row:row + 1, lane:lane + width]
            mn = ADAM_B1 * ms[i][...] + (1.0 - ADAM_B1) * g
            vn = ADAM_B2 * vs[i][...] + (1.0 - ADAM_B2) * (g * g)
            o_g, o_d, o_m, o_v = outs[4 * i:4 * i + 4]
            o_g[...] = g
            o_m[...] = mn
            o_v[...] = vn
            o_d[...] = -ADAM_LR * ((mn / c1) / (jnp.sqrt(vn / c2) + ADAM_EPS) + ADAM_WD * ws[i][...])
        outs[4 * k][...] = total[_CONV_ROW0:, :D_CONV]

    shapes = []
    for n in _SMALL:
        shapes += [jax.ShapeDtypeStruct(w[n].shape, F32)] * 4
    shapes.append(jax.ShapeDtypeStruct((CONV_HALO, D_CONV), F32))
    res = _pallas_call(body, name=name, out_shape=shapes)(
        gathered, *[w[n] for n in _SMALL], *[m[n] for n in _SMALL], *[v[n] for n in _SMALL])
    return {n: res[4 * i:4 * i + 4] for i, n in enumerate(_SMALL)}, res[4 * k]


def _local_step(x, target, norms, shard):
    D = x.shape[1]
    J = N_DEV // 2
    as13 = lambda g: g.reshape(2, J, g.shape[1], D)

    (g13_1,) = _all_gather([shard["ffn1_w13"]], "gather_ffn1_w13")
    (g2_1,) = _seq_all_gather([shard["ffn1_w2"]], "gather_ffn1_w2", 10, after=g13_1)
    w13_1 = as13(g13_1)
    G1, U1, A1 = _ffn_up(x, norms["ffn1_norm"], w13_1, "ffn1_up")
    gin, gconv = _seq_all_gather([shard["w_in"], shard["conv_w"]], "gather_mix", 1, after=G1)
    w2_1 = g2_1.reshape(-1, D)
    x1 = _ffn_down(x, A1, w2_1, "ffn1_down")
    gout, g13_2, g2_2 = _seq_all_gather([shard["w_out"], shard["ffn2_w13"], shard["ffn2_w2"]], "gather_ffn2", 2,
                                        after=x1)
    wout = gout.reshape(-1, D)
    conv_w32 = jnp.pad(gconv.transpose(1, 0, 2).reshape(CONV_TAPS, D_CONV), ((0, CONV_HALO - CONV_TAPS), (0, 0)))

    ag, k, v, qT, kT, vT, fl = _inproj_fwd(x1, norms["mix_norm"], gin, "inproj_fwd")
    cum, cumT = _forget_fwd(fl, norms["forget_b"], "forget_fwd")
    yc, c = _conv_fwd(ag, conv_w32, norms["conv_b"], norms["conv_ln_g"], norms["conv_ln_b"], "conv_fwd")
    o, lseT = _attn_fwd(qT, k, vT, cum, cumT, "attn_fwd")
    x2 = _outproj_fwd(x1, c, o, norms["out_norm_conv"], norms["out_norm_attn"], wout, "outproj_fwd")
    w13_2, w2_2 = as13(g13_2), g2_2.reshape(-1, D)
    G2, U2, A2 = _ffn_up(x2, norms["ffn2_norm"], w13_2, "ffn2_up")
    loss, dx3, d_final = _ffn_down_loss(x2, A2, w2_2, norms["final_norm"], target, "ffn2_down_loss")

    dw2_2 = _ffn_w2_grad(dx3, A2, "ffn2_w2_grad")
    dx2, d_ffn2n, h3, dG2, dU2 = _ffn_bwd_act(x2, norms["ffn2_norm"], dx3, G2, U2, w13_2, w2_2, "ffn2_bwd_act")
    dw13_2 = _ffn_w13_grad(h3, dG2, dU2, "ffn2_w13_grad")
    dc, dobT, deltaT, dwout, d_onc, d_ona = _outproj_bwd(
        dx2, c, o, norms["out_norm_conv"], norms["out_norm_attn"], wout, "outproj_bwd")
    red_ffn2 = _reduce_scatter([_blocks(dw13_2), _blocks(dw2_2)], "ffn2", (3, 4), add_after=(dc,))
    dqT, dkT, dvT, dcum = _attn_bwd(qT, k, kT, v, dobT, lseT, deltaT, cum, cumT, "attn_bwd",
                                    after=red_ffn2.partials)
    dfl, d_fb = _forget_bwd(dcum, fl, norms["forget_b"], "forget_bwd")
    dag, d_convw, d_cb, d_lg, d_lb = _conv_bwd(dc, yc, ag, conv_w32, norms["conv_ln_g"], norms["conv_ln_b"], "conv_bwd")
    dx1, d_mixn, h2 = _inproj_bwd_act(x1, norms["mix_norm"], dx2, dag, dqT, dkT, dvT, dfl, gin, "inproj_bwd_act")
    dw2_1 = _ffn_w2_grad(dx1, A1, "ffn1_w2_grad")
    early = [_blocks(dwout), _blocks(dw2_1)]
    sib_early = _seq_to_sibling(early, "rs_sibling_mix_early", 11, red_ffn2.reduced[:1])
    dwin_blocks = _inproj_bwd_weights(h2, dag, dqT, dkT, dvT, dfl, "inproj_bwd_weights")
    sib_w_in = _seq_to_sibling([dwin_blocks], "rs_sibling_mix", 5, sib_early[:1])
    mix_partials = _pair_add([dwin_blocks] + early, sib_w_in + sib_early, "rs_add_mix")
    red_mix = _Reduced(mix_partials, _seq_to_chips(mix_partials, "rs_chips_mix", 6))
    dx, d_ffn1n, h1, dG1, dU1 = _ffn_bwd_act(x, norms["ffn1_norm"], dx1, G1, U1, w13_1, w2_1, "ffn1_bwd_act",
                                             after=red_mix.partials)
    dw13_1 = _ffn_w13_grad(h1, dG1, dU1, "ffn1_w13_grad")

    small = dict(ffn1_norm=d_ffn1n, mix_norm=d_mixn, conv_b=d_cb, conv_ln_g=d_lg, conv_ln_b=d_lb,
                 forget_b=d_fb, out_norm_conv=d_onc, out_norm_attn=d_ona, ffn2_norm=d_ffn2n,
                 final_norm=d_final, conv_w=d_convw)
    packed_small = _pack_small(small, "pack_small_grads")
    gathered_small = []

    def gather_small(behind):
        gathered_small.extend(_seq_all_gather([packed_small], "gather_small_grads", 9, after=behind))
        return gathered_small

    red_w13_1 = _reduce_scatter([_blocks(dw13_1)], "ffn1_w13", (7, 8), after=red_mix.reduced[:1],
                                between=gather_small)
    big = dict(ffn1_w13=red_w13_1.reduced[0], ffn1_w2=red_mix.reduced[2], w_in=red_mix.reduced[0],
               w_out=red_mix.reduced[1], ffn2_w13=red_ffn2.reduced[0], ffn2_w2=red_ffn2.reduced[1])
    return loss[0, 0], dx, gathered_small[0], big


_BIG = ("ffn1_w13", "ffn1_w2", "w_in", "w_out", "ffn2_w13", "ffn2_w2")
_TRANSPOSED = ("ffn1_w13", "ffn2_w13", "w_in")
_ORDER = ("ffn1_norm", "ffn1_w13", "ffn1_w2", "mix_norm", "w_in", "conv_w", "conv_b", "conv_ln_g", "conv_ln_b",
          "forget_b", "out_norm_conv", "out_norm_attn", "w_out", "ffn2_norm", "ffn2_w13", "ffn2_w2", "final_norm")


def kernel(x, ffn1_norm, ffn1_w13, ffn1_w2, mix_norm, w_in, conv_w, conv_b, conv_ln_g, conv_ln_b, forget_b, out_norm_conv, out_norm_attn, w_out, ffn2_norm, ffn2_w13, ffn2_w2, final_norm, loss_target, m_ffn1_norm, m_ffn1_w13, m_ffn1_w2, m_mix_norm, m_w_in, m_conv_w, m_conv_b, m_conv_ln_g, m_conv_ln_b, m_forget_b, m_out_norm_conv, m_out_norm_attn, m_w_out, m_ffn2_norm, m_ffn2_w13, m_ffn2_w2, m_final_norm, v_ffn1_norm, v_ffn1_w13, v_ffn1_w2, v_mix_norm, v_w_in, v_conv_w, v_conv_b, v_conv_ln_g, v_conv_ln_b, v_forget_b, v_out_norm_conv, v_out_norm_attn, v_w_out, v_ffn2_norm, v_ffn2_w13, v_ffn2_w2, v_final_norm):
    w = dict(ffn1_norm=ffn1_norm, ffn1_w13=ffn1_w13, ffn1_w2=ffn1_w2, mix_norm=mix_norm, w_in=w_in, conv_w=conv_w,
             conv_b=conv_b, conv_ln_g=conv_ln_g, conv_ln_b=conv_ln_b, forget_b=forget_b, out_norm_conv=out_norm_conv,
             out_norm_attn=out_norm_attn, w_out=w_out, ffn2_norm=ffn2_norm, ffn2_w13=ffn2_w13, ffn2_w2=ffn2_w2,
             final_norm=final_norm)
    m = dict(ffn1_norm=m_ffn1_norm, ffn1_w13=m_ffn1_w13, ffn1_w2=m_ffn1_w2, mix_norm=m_mix_norm, w_in=m_w_in,
             conv_w=m_conv_w, conv_b=m_conv_b, conv_ln_g=m_conv_ln_g, conv_ln_b=m_conv_ln_b, forget_b=m_forget_b,
             out_norm_conv=m_out_norm_conv, out_norm_attn=m_out_norm_attn, w_out=m_w_out, ffn2_norm=m_ffn2_norm,
             ffn2_w13=m_ffn2_w13, ffn2_w2=m_ffn2_w2, final_norm=m_final_norm)
    v = dict(ffn1_norm=v_ffn1_norm, ffn1_w13=v_ffn1_w13, ffn1_w2=v_ffn1_w2, mix_norm=v_mix_norm, w_in=v_w_in,
             conv_w=v_conv_w, conv_b=v_conv_b, conv_ln_g=v_conv_ln_g, conv_ln_b=v_conv_ln_b, forget_b=v_forget_b,
             out_norm_conv=v_out_norm_conv, out_norm_attn=v_out_norm_attn, w_out=v_w_out, ffn2_norm=v_ffn2_norm,
             ffn2_w13=v_ffn2_w13, ffn2_w2=v_ffn2_w2, final_norm=v_final_norm)
    shapes = {n: a.shape for n, a in w.items()}
    T, D = x.shape[1], x.shape[2]
    def two(n, a):
        if a.ndim != 3:
            return a.reshape(1, -1)
        a = a.reshape(a.shape[-2], a.shape[-1])
        return a.T if n in _TRANSPOSED else a

    w2d = {n: two(n, a) for n, a in w.items()}
    m2d = {n: two(n, a) for n, a in m.items()}
    v2d = {n: two(n, a) for n, a in v.items()}

    shard = {n: w2d[n].astype(MXU) for n in _BIG}
    shard["conv_w"] = w2d["conv_w"]
    norms = {n: w2d[n] for n in _SMALL}
    norms["forget_b"] = jnp.pad(w2d["forget_b"], ((0, 0), (0, LANES - N_HEADS)))
    loss_part, dx, gathered_small, big = _local_step(x[0], loss_target[0], norms, shard)
    loss = lax.psum(loss_part, ("x", "y", "c"))

    grads, deltas, new_m, new_v = {}, {}, {}, {}
    for n in _BIG:
        g, d, nm, nv = _adamw(w2d[n], m2d[n], v2d[n], big[n], "adamw_" + n)
        grads[n], deltas[n], new_m[n], new_v[n] = g, d, nm, nv

    small_out, conv_g_full = _adamw_small(gathered_small, w2d, m2d, v2d, "adamw_small")
    for n in _SMALL:
        grads[n], deltas[n], new_m[n], new_v[n] = small_out[n]
    conv_g_full = conv_g_full[:CONV_TAPS]
    xi, yi, ci = _position()
    cw = shapes["conv_w"][-1]
    conv_g_mine = lax.dynamic_slice_in_dim(conv_g_full, _flat(xi, yi, ci) * cw, cw, axis=1)
    g, d, nm, nv = _adamw(w2d["conv_w"], m2d["conv_w"], v2d["conv_w"], conv_g_mine[None], "adamw_conv_w")
    grads["conv_w"], deltas["conv_w"], new_m["conv_w"], new_v["conv_w"] = g, d, nm, nv

    shaped = lambda dct: [(dct[n].T if n in _TRANSPOSED else dct[n]).reshape(shapes[n]) for n in _ORDER]
    return (loss, dx[None], *shaped(grads), *shaped(deltas), *shaped(new_m), *shaped(new_v))
```

```python
from typing import NamedTuple

import jax
import jax.numpy as jnp
from jax import lax
from jax.experimental import pallas as pl
from jax.experimental.pallas import tpu as pltpu
from jax.experimental.pallas import tpu_sc as plsc

F32 = jnp.float32
MXU = jnp.bfloat16
EPS = 1e-6
N_HEADS = 8
HEAD_DIM = 64
D_CONV = 512
D_ATTN = N_HEADS * HEAD_DIM
CONV_TAPS = 31
CONV_HALO = 32
SCALE = HEAD_DIM ** -0.5
NEG = -1e30
LANES = 128
N_DEV = 8
N_CHIPS = N_DEV // 2
MESH = pl.DeviceIdType.MESH
MIB = 1 << 20

ADAM_LR = 0.001
ADAM_B1 = 0.9
ADAM_B2 = 0.999
ADAM_EPS = 1e-08
ADAM_WD = 0.01
ADAM_STEP = 10


_UNREAD = pl.BlockSpec(memory_space=pl.ANY)


def _pallas_call(body, *, out_shape, **kwargs):
    in_hbm = lambda s: pltpu.HBM(s.shape, s.dtype)
    outs = [in_hbm(s) for s in out_shape] if isinstance(out_shape, (list, tuple)) else in_hbm(out_shape)
    call = pl.pallas_call(body, out_shape=outs, **kwargs)
    return lambda *operands: call(*[pltpu.with_memory_space_constraint(a, pltpu.HBM) for a in operands])


def _params(vmem_mib, n_axes):
    return pltpu.CompilerParams(dimension_semantics=("arbitrary",) * n_axes, vmem_limit_bytes=vmem_mib * MIB)


def _mm(a, b):
    return jnp.dot(a, b, preferred_element_type=F32)


def _mm_nt(a, b):
    return lax.dot_general(a, b, (((1,), (1,)), ((), ())), preferred_element_type=F32)


def _mm_tn(a, b):
    return lax.dot_general(a, b, (((0,), (0,)), ((), ())), preferred_element_type=F32)


def _rms_fwd(x, g):
    r = lax.rsqrt(jnp.mean(x * x, axis=-1, keepdims=True) + EPS)
    return x * r * g, r


def _rms_bwd(x, r, g, dy):
    gdy = dy * g
    dx = r * gdy - x * (r * r * r) * jnp.mean(x * gdy, axis=-1, keepdims=True)
    dg = jnp.sum(dy * x * r, axis=0, keepdims=True)
    return dx, dg


def _silu_grad(z, sz):
    return sz * (1.0 + z * (1.0 - sz))


def _three_terms(x):
    x1 = x.astype(jnp.bfloat16)
    r1 = x - x1.astype(F32)
    x2 = r1.astype(jnp.bfloat16)
    x3 = (r1 - x2.astype(F32)).astype(jnp.bfloat16)
    return x1, x2, x3


def _exact_tri_dot(tri, x):
    x1, x2, x3 = _three_terms(x)
    return _mm(tri, x1) + _mm(tri, x2) + _mm(tri, x3)


def _exact_dot_01(x, sel):
    x1, x2, x3 = _three_terms(x)
    return _mm(x1, sel) + _mm(x2, sel) + _mm(x3, sel)


def _tile(n, want):
    t = min(n, want)
    assert n % t == 0
    return t


_FFN_CHUNK = 256


def _ffn_up(x, g, w13, name):
    T, D = x.shape
    _, J, bf, _ = w13.shape
    tm = _tile(T, 512)
    I = T // tm

    def body(x_ref, g_ref, w13_ref, G_ref, U_ref, A_ref, h_s):
        j = pl.program_id(0)
        i = pl.program_id(1)
        rows = pl.ds(pl.multiple_of(i * tm, tm), tm)

        @pl.when(j == 0)
        def _():
            h, _ = _rms_fwd(x_ref[...], g_ref[...])
            h_s[rows, :] = h.astype(MXU)

        chunks = [slice(r0, r0 + _FFN_CHUNK) for r0 in range(0, tm, _FFN_CHUNK)]
        hbs = [h_s[pl.ds(pl.multiple_of(i * tm + rs.start, _FFN_CHUNK), _FFN_CHUNK), :] for rs in chunks]
        GU = [(_mm_nt(hb, w13_ref[0]), _mm_nt(hb, w13_ref[1])) for hb in hbs]
        for rs, (G, U) in zip(chunks, GU):
            G_ref[rs, :] = G.astype(MXU)
            U_ref[rs, :] = U.astype(MXU)
            A_ref[rs, :] = (G * jax.nn.sigmoid(G) * U).astype(MXU)

    blk = pl.BlockSpec((None, tm, bf), lambda j, i: (j, i, 0))
    hid = jax.ShapeDtypeStruct((J, T, bf), MXU)
    return _pallas_call(
        body, name=name, grid=(J, I),
        in_specs=[pl.BlockSpec((tm, D), lambda j, i: (jnp.where(j == 0, i, I - 1), 0)),
                  pl.BlockSpec((1, D), lambda j, i: (0, 0)),
                  pl.BlockSpec((2, None, bf, D), lambda j, i: (0, j, 0, 0))],
        out_specs=[blk, blk, blk],
        out_shape=[hid, hid, hid],
        scratch_shapes=[pltpu.VMEM((T, D), MXU)],
        compiler_params=_params(40, 2),
    )(x, g, w13)


def _ffn_down(x, A, w2, name):
    T, D = x.shape
    J, _, bf = A.shape
    tm = _tile(T, 512)

    def body(x_ref, A_ref, w2_ref, xo_ref):
        f = _mm(A_ref[0], w2_ref[0:bf, :])
        for j in range(1, J):
            f = f + _mm(A_ref[j], w2_ref[j * bf:(j + 1) * bf, :])
        xo_ref[...] = x_ref[...] + 0.5 * f

    row = pl.BlockSpec((tm, D), lambda i: (i, 0))
    return _pallas_call(
        body, name=name, grid=(T // tm,),
        in_specs=[row, pl.BlockSpec((J, tm, bf), lambda i: (0, i, 0)), pl.BlockSpec((J * bf, D), lambda i: (0, 0))],
        out_specs=row,
        out_shape=jax.ShapeDtypeStruct((T, D), F32),
        compiler_params=_params(48, 1),
    )(x, A, w2)


def _ffn_bwd_act(x, g, dy, Gs, Us, w13, w2, name, after=()):
    T, D = x.shape
    _, J, bf, _ = w13.shape
    tm = _tile(T, 512)
    I = T // tm

    def body(x_ref, g_ref, dy_ref, G_ref, U_ref, w13_ref, w2_ref, *rest):
        dx_ref, dg_ref, h_ref, dG_ref, dU_ref, dh_s, dF_s, h_s = rest[len(after):]
        j = pl.program_id(0)
        i = pl.program_id(1)
        rows = pl.ds(pl.multiple_of(i * tm, tm), tm)

        @pl.when(j == 0)
        def _():
            h, _ = _rms_fwd(x_ref[...], g_ref[...])
            hb = h.astype(MXU)
            h_s[rows, :] = hb
            h_ref[...] = hb
            dF_s[rows, :] = (0.5 * dy_ref[...]).astype(MXU)
            dh_s[rows, :] = jnp.zeros((tm, D), F32)

        chunks = [slice(r0, r0 + _FFN_CHUNK) for r0 in range(0, tm, _FFN_CHUNK)]
        crows = [pl.ds(pl.multiple_of(i * tm + rs.start, _FFN_CHUNK), _FFN_CHUNK) for rs in chunks]
        dAs = [_mm_nt(dF_s[cr, :], w2_ref[...]) for cr in crows]
        for rs, cr, dA in zip(chunks, crows, dAs):
            G = G_ref[rs, :].astype(F32)
            U = U_ref[rs, :].astype(F32)
            sg = jax.nn.sigmoid(G)
            s = G * sg
            dUb = (dA * s).astype(MXU)
            dGb = (dA * U * _silu_grad(G, sg)).astype(MXU)
            dG_ref[rs, :] = dGb
            dU_ref[rs, :] = dUb
            dh_s[cr, :] += _mm(dGb, w13_ref[0]) + _mm(dUb, w13_ref[1])

        @pl.when(j == J - 1)
        def _():
            xv = x_ref[...]
            gv = g_ref[...]
            _, r = _rms_fwd(xv, gv)
            dxn, dgp = _rms_bwd(xv, r, gv, dh_s[rows, :])
            dx_ref[...] = dy_ref[...] + dxn

            @pl.when(i == 0)
            def _():
                dg_ref[...] = dgp

            @pl.when(i > 0)
            def _():
                dg_ref[...] += dgp

    ends = lambda j, i: (jnp.where((j == 0) | (j == J - 1), i, I - 1), 0)
    blk = pl.BlockSpec((None, tm, bf), lambda j, i: (j, i, 0))
    hid = jax.ShapeDtypeStruct((J, T, bf), MXU)
    return _pallas_call(
        body, name=name, grid=(J, I),
        in_specs=[pl.BlockSpec((tm, D), ends), pl.BlockSpec((1, D), lambda j, i: (0, 0)), pl.BlockSpec((tm, D), ends),
                  blk, blk, pl.BlockSpec((2, None, bf, D), lambda j, i: (0, j, 0, 0)),
                  pl.BlockSpec((bf, D), lambda j, i: (j, 0))] + [_UNREAD] * len(after),
        out_specs=[pl.BlockSpec((tm, D), lambda j, i: (jnp.where(j == J - 1, i, 0), 0)),
                   pl.BlockSpec((1, D), lambda j, i: (0, 0)),
                   pl.BlockSpec((tm, D), lambda j, i: (jnp.where(j == 0, i, I - 1), 0)), blk, blk],
        out_shape=[jax.ShapeDtypeStruct((T, D), F32), jax.ShapeDtypeStruct((1, D), F32),
                   jax.ShapeDtypeStruct((T, D), MXU), hid, hid],
        scratch_shapes=[pltpu.VMEM((T, D), F32), pltpu.VMEM((T, D), MXU), pltpu.VMEM((T, D), MXU)],
        compiler_params=_params(58, 2),
    )(x, g, dy, Gs, Us, w13, w2, *after)


def _ffn_w13_grad(h, dG, dU, name):
    T, D = h.shape
    J, _, bf = dG.shape

    def body(h_ref, dG_ref, dU_ref, dw13_ref):
        dw13_ref[0] = _mm_tn(dG_ref[...], h_ref[...]).astype(dw13_ref.dtype)
        dw13_ref[1] = _mm_tn(dU_ref[...], h_ref[...]).astype(dw13_ref.dtype)

    blk = pl.BlockSpec((None, T, bf), lambda j: (j, 0, 0))
    return _pallas_call(
        body, name=name, grid=(J,),
        in_specs=[pl.BlockSpec((T, D), lambda j: (0, 0)), blk, blk],
        out_specs=pl.BlockSpec((2, None, bf, D), lambda j: (0, j, 0, 0)),
        out_shape=jax.ShapeDtypeStruct((2, J, bf, D), MXU),
        compiler_params=_params(48, 1),
    )(h, dG, dU)


def _ffn_w2_grad(dy, A, name, after=()):
    T, D = dy.shape
    J, _, bf = A.shape

    def body(dy_ref, A_ref, *rest):
        dw2_ref, dF_s = rest[len(after):]

        @pl.when(pl.program_id(0) == 0)
        def _():
            dF_s[...] = (0.5 * dy_ref[...]).astype(MXU)

        dw2_ref[...] = _mm_tn(A_ref[...], dF_s[...]).astype(dw2_ref.dtype)

    return _pallas_call(
        body, name=name, grid=(J,),
        in_specs=[pl.BlockSpec((T, D), lambda j: (0, 0)), pl.BlockSpec((None, T, bf), lambda j: (j, 0, 0))]
        + [_UNREAD] * len(after),
        out_specs=pl.BlockSpec((bf, D), lambda j: (j, 0)),
        out_shape=jax.ShapeDtypeStruct((J * bf, D), MXU),
        scratch_shapes=[pltpu.VMEM((T, D), MXU)],
        compiler_params=_params(48, 1),
    )(dy, A, *after)


_AG0, _Q0, _K0, _V0, _F0 = 0, 2 * D_CONV, 2 * D_CONV + D_ATTN, 2 * D_CONV + 2 * D_ATTN, 2 * D_CONV + 3 * D_ATTN
N_IN = _F0 + N_HEADS
N_IN_PAD = _F0 + LANES
_IN_BLOCK = N_IN // N_DEV


def _rows_from_blocks(blocks_ref, rows_ref):
    for p in range(N_DEV):
        rows_ref[_IN_BLOCK * p:_IN_BLOCK * (p + 1), :] = blocks_ref[p]
    rows_ref[N_IN:, :] = jnp.zeros((N_IN_PAD - N_IN, rows_ref.shape[1]), rows_ref.dtype)


def _inproj_fwd(x1, gm, win_blocks, name):
    T, D = x1.shape
    tm = _tile(T, 256)

    def body(x_ref, g_ref, wb_ref, ag_ref, k_ref, v_ref, qT_ref, kT_ref, vT_ref, fl_ref, w_ref):
        @pl.when(pl.program_id(0) == 0)
        def _():
            _rows_from_blocks(wb_ref, w_ref)

        h, _ = _rms_fwd(x_ref[...], g_ref[...])
        hb = h.astype(MXU)
        ag_ref[...] = _mm_nt(hb, w_ref[_AG0:_Q0, :])
        qT_ref[...] = (_mm_nt(hb, w_ref[_Q0:_K0, :]) * SCALE).T.astype(MXU)
        for c0, ref, refT in ((_K0, k_ref, kT_ref), (_V0, v_ref, vT_ref)):
            y = _mm_nt(hb, w_ref[c0:c0 + D_ATTN, :])
            ref[...] = y.astype(MXU)
            refT[...] = y.T.astype(MXU)
        fl_ref[...] = _mm_nt(hb, w_ref[_F0:N_IN_PAD, :])

    row = lambda w: pl.BlockSpec((tm, w), lambda i: (i, 0))
    col = pl.BlockSpec((D_ATTN, tm), lambda i: (0, i))
    std = jax.ShapeDtypeStruct((T, D_ATTN), MXU)
    trn = jax.ShapeDtypeStruct((D_ATTN, T), MXU)
    return _pallas_call(
        body, name=name, grid=(T // tm,),
        in_specs=[row(D), pl.BlockSpec((1, D), lambda i: (0, 0)),
                  pl.BlockSpec((N_DEV, _IN_BLOCK, D), lambda i: (0, 0, 0))],
        out_specs=[row(2 * D_CONV), row(D_ATTN), row(D_ATTN), col, col, col, row(LANES)],
        out_shape=[jax.ShapeDtypeStruct((T, 2 * D_CONV), F32), std, std, trn, trn, trn,
                   jax.ShapeDtypeStruct((T, LANES), F32)],
        scratch_shapes=[pltpu.VMEM((N_IN_PAD, D), MXU)],
        compiler_params=_params(40, 1),
    )(x1, gm, win_blocks)


def _inproj_bwd_act(x1, gm, dx2, dag, dqT, dkT, dvT, dfl, win_blocks, name):
    T, D = x1.shape
    tm = _tile(T, 256)

    def body(x_ref, g_ref, dx2_ref, dag_ref, dqT_ref, dkT_ref, dvT_ref, dfl_ref, wb_ref, dx1_ref, dg_ref, h_ref,
             w_ref):
        i = pl.program_id(0)

        @pl.when(i == 0)
        def _():
            _rows_from_blocks(wb_ref, w_ref)

        xv = x_ref[...]
        gv = g_ref[...]
        h, r = _rms_fwd(xv, gv)
        h_ref[...] = h.astype(MXU)
        dh = _mm(dag_ref[...], w_ref[_AG0:_Q0, :])
        for c0, ref in ((_Q0, dqT_ref), (_K0, dkT_ref), (_V0, dvT_ref)):
            dh = dh + _mm_tn(ref[...].astype(MXU), w_ref[c0:c0 + D_ATTN, :])
        dh = dh + _mm(dfl_ref[...].astype(MXU), w_ref[_F0:N_IN_PAD, :])
        dxn, dgp = _rms_bwd(xv, r, gv, dh)
        dx1_ref[...] = dx2_ref[...] + dxn

        @pl.when(i == 0)
        def _():
            dg_ref[...] = dgp

        @pl.when(i > 0)
        def _():
            dg_ref[...] += dgp

    row = lambda w: pl.BlockSpec((tm, w), lambda i: (i, 0))
    col = pl.BlockSpec((D_ATTN, tm), lambda i: (0, i))
    full = lambda a, b: pl.BlockSpec((a, b), lambda i: (0, 0))
    return _pallas_call(
        body, name=name, grid=(T // tm,),
        in_specs=[row(D), full(1, D), row(D), row(2 * D_CONV), col, col, col, row(LANES),
                  pl.BlockSpec((N_DEV, _IN_BLOCK, D), lambda i: (0, 0, 0))],
        out_specs=[row(D), full(1, D), row(D)],
        out_shape=[jax.ShapeDtypeStruct((T, D), F32), jax.ShapeDtypeStruct((1, D), F32),
                   jax.ShapeDtypeStruct((T, D), MXU)],
        scratch_shapes=[pltpu.VMEM((N_IN_PAD, D), MXU)],
        compiler_params=_params(40, 1),
    )(x1, gm, dx2, dag, dqT, dkT, dvT, dfl, win_blocks)


def _inproj_bwd_weights(h, dag, dqT, dkT, dvT, dfl, name, after=()):
    T, D = h.shape

    def body(h_ref, dag_ref, dqT_ref, dkT_ref, dvT_ref, dfl_ref, *rest):
        blocks_ref, dw_ref = rest[len(after):]
        hb = h_ref[...]
        dw_ref[_AG0:_Q0, :] = _mm_tn(dag_ref[...], hb).astype(dw_ref.dtype)
        for c0, ref in ((_Q0, dqT_ref), (_K0, dkT_ref), (_V0, dvT_ref)):
            dw_ref[c0:c0 + D_ATTN, :] = _mm(ref[...].astype(MXU), hb).astype(dw_ref.dtype)
        dw_ref[_F0:N_IN_PAD, :] = _mm_tn(dfl_ref[...].astype(MXU), hb).astype(dw_ref.dtype)
        for p in range(N_DEV):
            blocks_ref[p] = dw_ref[_IN_BLOCK * p:_IN_BLOCK * (p + 1), :]

    vmem = pl.BlockSpec(memory_space=pltpu.VMEM)
    return _pallas_call(
        body, name=name, in_specs=[vmem] * 6 + [_UNREAD] * len(after), out_specs=vmem,
        out_shape=jax.ShapeDtypeStruct((N_DEV, _IN_BLOCK, D), MXU),
        scratch_shapes=[pltpu.VMEM((N_IN_PAD, D), MXU)],
        compiler_params=pltpu.CompilerParams(vmem_limit_bytes=56 * MIB),
    )(h, dag, dqT, dkT, dvT, dfl, *after)


def _forget_fwd(fl, fbp, name):
    T = fl.shape[0]
    tb = _tile(T, 256)

    def body(fl_ref, fb_ref, cum_ref, cumT_ref):
        ri = lax.broadcasted_iota(jnp.int32, (tb, tb), 0)
        ci = lax.broadcasted_iota(jnp.int32, (tb, tb), 1)
        tri = (ri >= ci).astype(jnp.bfloat16)
        carry = jnp.zeros((1, LANES), F32)
        for b in range(T // tb):
            z = fl_ref[b * tb:(b + 1) * tb, :] + fb_ref[...]
            lf = jnp.minimum(z, 0.0) - jnp.log1p(jnp.exp(-jnp.abs(z)))
            c = _exact_tri_dot(tri, lf) + carry
            cum_ref[b * tb:(b + 1) * tb, :] = c
            carry = c[tb - 1:tb, :]
        cumT_ref[...] = cum_ref[...].T[:N_HEADS, :]

    return _pallas_call(
        body, name=name,
        out_shape=[jax.ShapeDtypeStruct((T, LANES), F32), jax.ShapeDtypeStruct((N_HEADS, T), F32)],
        compiler_params=pltpu.CompilerParams(vmem_limit_bytes=32 * MIB),
    )(fl, fbp)


def _forget_bwd(dcum, fl, fbp, name):
    T = fl.shape[0]
    tb = _tile(T, 256)

    def body(dc_ref, fl_ref, fb_ref, dfl_ref, dfb_ref):
        ri = lax.broadcasted_iota(jnp.int32, (tb, tb), 0)
        ci = lax.broadcasted_iota(jnp.int32, (tb, tb), 1)
        tri = (ri <= ci).astype(jnp.bfloat16)
        carry = jnp.zeros((1, LANES), F32)
        dfb = jnp.zeros((1, LANES), F32)
        for b in reversed(range(T // tb)):
            sl = slice(b * tb, (b + 1) * tb)
            dl = _exact_tri_dot(tri, dc_ref[sl, :]) + carry
            carry = dl[0:1, :]
            z = fl_ref[sl, :] + fb_ref[...]
            dfl = dl * jax.nn.sigmoid(-z)
            dfl_ref[sl, :] = dfl
            dfb = dfb + jnp.sum(dfl, axis=0, keepdims=True)
        dfb_ref[...] = dfb

    return _pallas_call(
        body, name=name,
        out_shape=[jax.ShapeDtypeStruct((T, LANES), F32), jax.ShapeDtypeStruct((1, LANES), F32)],
        compiler_params=pltpu.CompilerParams(vmem_limit_bytes=32 * MIB),
    )(dcum, fl, fbp)


def _causal_keep(i, j, tq, tk):
    key = j * tk + lax.broadcasted_iota(jnp.int32, (tk, tq), 0)
    qry = i * tq + lax.broadcasted_iota(jnp.int32, (tk, tq), 1)
    return key <= qry


def _split_hi_lo(x):
    hi = x.astype(MXU)
    lo = (x - hi.astype(F32)).astype(MXU)
    return hi, lo


def _attn_fwd(qT, k, vT, cum, cumT, name):
    T = k.shape[0]
    tq = _tile(T, 256)
    tk = _tile(tq, 256)
    kpq = tq // tk
    heads = [slice(HEAD_DIM * h, HEAD_DIM * (h + 1)) for h in range(N_HEADS)]

    def body(qT_ref, k_ref, vT_ref, cum_ref, cumT_ref, o_ref, lseT_ref, acc_s, m_s, l_s):
        i = pl.program_id(0)
        acc_s[...] = jnp.zeros_like(acc_s)
        m_s[...] = jnp.full_like(m_s, NEG)
        l_s[...] = jnp.zeros_like(l_s)

        def kblock(j, masked):
            rows = pl.ds(pl.multiple_of(j * tk, tk), tk)
            keep = _causal_keep(i, j, tq, tk) if masked else None
            bias = [cumT_ref[h:h + 1, :] - cum_ref[rows, h:h + 1] for h in range(N_HEADS)]
            qk = [_mm(k_ref[rows, hs], qT_ref[hs, :]) + bias[h] for h, hs in enumerate(heads)]
            for h, hs in enumerate(heads):
                sT = qk[h]
                if masked:
                    sT = jnp.where(keep, sT, NEG)
                m_old = m_s[h:h + 1, :]
                m_new = jnp.maximum(m_old, jnp.max(sT, axis=0, keepdims=True))
                alpha = jnp.exp(m_old - m_new)
                pT = jnp.exp(sT - m_new)
                l_s[h:h + 1, :] = alpha * l_s[h:h + 1, :] + jnp.sum(pT, axis=0, keepdims=True)
                p_hi, p_lo = _split_hi_lo(pT)
                vh = vT_ref[hs, rows]
                acc_s[hs, :] = alpha * acc_s[hs, :] + (_mm(vh, p_hi) + _mm(vh, p_lo))
                m_s[h:h + 1, :] = m_new

        def unmasked(j, c):
            kblock(j, False)
            return c

        lax.fori_loop(0, kpq * i, unmasked, 0)
        for d in range(kpq):
            kblock(kpq * i + d, True)
        for h, hs in enumerate(heads):
            acc_s[hs, :] = acc_s[hs, :] / l_s[h:h + 1, :]
        o_ref[...] = acc_s[...].T
        lseT_ref[...] = m_s[...] + jnp.log(l_s[...])

    full = lambda a, b: pl.BlockSpec((a, b), lambda i: (0, 0))
    colblk = lambda r: pl.BlockSpec((r, tq), lambda i: (0, i))
    return _pallas_call(
        body, name=name, grid=(T // tq,),
        in_specs=[colblk(D_ATTN), full(T, D_ATTN), full(D_ATTN, T), full(T, LANES), colblk(N_HEADS)],
        out_specs=[pl.BlockSpec((tq, D_ATTN), lambda i: (i, 0)), colblk(N_HEADS)],
        out_shape=[jax.ShapeDtypeStruct((T, D_ATTN), F32), jax.ShapeDtypeStruct((N_HEADS, T), F32)],
        scratch_shapes=[pltpu.VMEM((D_ATTN, tq), F32), pltpu.VMEM((N_HEADS, tq), F32),
                        pltpu.VMEM((N_HEADS, tq), F32)],
        compiler_params=_params(40, 1),
    )(qT, k, vT, cum, cumT)


def _attn_bwd(qT, k, kT, v, doT, lseT, deltaT, cum, cumT, name, after=()):
    T = k.shape[0]
    tq = _tile(T, 256)
    tk = _tile(tq, 256)
    kpq = tq // tk
    heads = [slice(HEAD_DIM * h, HEAD_DIM * (h + 1)) for h in range(N_HEADS)]

    def body(qT_ref, k_ref, kT_ref, v_ref, doT_ref, lseT_ref, dlT_ref, cum_ref, cumT_ref, *rest):
        dq_ref, dk_ref, dv_ref, dcum_ref, dq_s = rest[len(after):]
        i = pl.program_id(0)

        @pl.when(i == 0)
        def _():
            dk_ref[...] = jnp.zeros_like(dk_ref)
            dv_ref[...] = jnp.zeros_like(dv_ref)
            dcum_ref[...] = jnp.zeros_like(dcum_ref)

        dq_s[...] = jnp.zeros_like(dq_s)

        def kblock(j, masked):
            rows = pl.ds(pl.multiple_of(j * tk, tk), tk)
            keep = _causal_keep(i, j, tq, tk) if masked else None
            bias = [cumT_ref[h:h + 1, :] - cum_ref[rows, h:h + 1] for h in range(N_HEADS)]
            qk = [_mm(k_ref[rows, hs], qT_ref[hs, :]) + bias[h] for h, hs in enumerate(heads)]
            dps = [_mm(v_ref[rows, hs], doT_ref[hs, :]) for hs in heads]
            for h, hs in enumerate(heads):
                sT = qk[h]
                if masked:
                    sT = jnp.where(keep, sT, NEG)
                pT = jnp.exp(sT - lseT_ref[h:h + 1, :])
                dsT = pT * (dps[h] - dlT_ref[h:h + 1, :])
                dcum_ref[rows, h:h + 1] += -jnp.sum(dsT, axis=1, keepdims=True)
                dsb = dsT.astype(MXU)
                dv_ref[hs, rows] += _mm_nt(doT_ref[hs, :], pT.astype(MXU))
                dk_ref[hs, rows] += _mm_nt(qT_ref[hs, :], dsb)
                dq_s[hs, :] += _mm(kT_ref[hs, rows], dsb)

        def unmasked(j, c):
            kblock(j, False)
            return c

        lax.fori_loop(0, kpq * i, unmasked, 0)
        for d in range(kpq):
            kblock(kpq * i + d, True)
        dq_ref[...] = (dq_s[...] * SCALE).astype(dq_ref.dtype)

    full = lambda a, b: pl.BlockSpec((a, b), lambda i: (0, 0))
    colblk = lambda r: pl.BlockSpec((r, tq), lambda i: (0, i))
    return _pallas_call(
        body, name=name, grid=(T // tq,),
        in_specs=[colblk(D_ATTN), full(T, D_ATTN), full(D_ATTN, T), full(T, D_ATTN), colblk(D_ATTN),
                  colblk(N_HEADS), colblk(N_HEADS), full(T, LANES), colblk(N_HEADS)] + [_UNREAD] * len(after),
        out_specs=[colblk(D_ATTN), full(D_ATTN, T), full(D_ATTN, T), full(T, LANES)],
        out_shape=[
            jax.ShapeDtypeStruct((D_ATTN, T), MXU),
            jax.ShapeDtypeStruct((D_ATTN, T), F32),
            jax.ShapeDtypeStruct((D_ATTN, T), F32),
            jax.ShapeDtypeStruct((T, LANES), F32),
        ],
        scratch_shapes=[pltpu.VMEM((D_ATTN, tq), F32)],
        compiler_params=_params(48, 1),
    )(qT, k, kT, v, doT, lseT, deltaT, cum, cumT, *after)


_ROWS_PER_CHUNK = 64


def _glu_halo(ag_ref, agh_ref, uext_s, tm, first):
    a = ag_ref[:, :D_CONV]
    sg = jax.nn.sigmoid(ag_ref[:, D_CONV:])
    uh = agh_ref[:, :D_CONV] * jax.nn.sigmoid(agh_ref[:, D_CONV:])
    uext_s[0:CONV_HALO, :] = jnp.where(first, 0.0, uh)
    uext_s[CONV_HALO:CONV_HALO + tm, :] = a * sg
    return a, sg


_SUBLANES = 8


def _shifted_copies(ext_s, sh_s, rows):
    for k in range(1, _SUBLANES):
        sh_s[k, 0:rows, :] = ext_s[pl.ds(k, rows), :]


def _window(ext_s, sh_s, start, rows):
    k = start % _SUBLANES
    if k == 0:
        return ext_s[pl.ds(start, rows), :]
    return sh_s[k, pl.ds(start - k, rows), :]


def _layer_norm_stats(y):
    mu = jnp.mean(y, axis=-1, keepdims=True)
    xc = y - mu
    rs = lax.rsqrt(jnp.mean(xc * xc, axis=-1, keepdims=True) + EPS)
    return xc * rs, rs


def _conv_fwd(ag, w32, cb, lg, lb, name):
    T = ag.shape[0]
    tm = _tile(T, 256)
    rc = _tile(tm, _ROWS_PER_CHUNK)
    hb = tm // CONV_HALO

    def body(ag_ref, agh_ref, w_ref, cb_ref, lg_ref, lb_ref, yc_ref, c_ref, uext_s, ush_s):
        i = pl.program_id(0)
        _glu_halo(ag_ref, agh_ref, uext_s, tm, i == 0)
        _shifted_copies(uext_s, ush_s, tm + CONV_HALO - _SUBLANES)
        for r0 in range(0, tm, rc):
            acc = jnp.zeros((rc, D_CONV), F32)
            for t in range(CONV_TAPS):
                acc = acc + _window(uext_s, ush_s, r0 + CONV_HALO - (CONV_TAPS - 1) + t, rc) * w_ref[t:t + 1, :]
            y = acc + cb_ref[...]
            yc_ref[r0:r0 + rc, :] = y
            n, _ = _layer_norm_stats(y)
            z = n * lg_ref[...] + lb_ref[...]
            c_ref[r0:r0 + rc, :] = z * jax.nn.sigmoid(z)

    row = lambda w: pl.BlockSpec((tm, w), lambda i: (i, 0))
    full = lambda a, b: pl.BlockSpec((a, b), lambda i: (0, 0))
    return _pallas_call(
        body, name=name, grid=(T // tm,),
        in_specs=[row(2 * D_CONV),
                  pl.BlockSpec((CONV_HALO, 2 * D_CONV), lambda i: (jnp.maximum(i * hb - 1, 0), 0)),
                  full(CONV_HALO, D_CONV), full(1, D_CONV), full(1, D_CONV), full(1, D_CONV)],
        out_specs=[row(D_CONV), row(D_CONV)],
        out_shape=[jax.ShapeDtypeStruct((T, D_CONV), F32), jax.ShapeDtypeStruct((T, D_CONV), F32)],
        scratch_shapes=[pltpu.VMEM((CONV_HALO + tm, D_CONV), F32),
                        pltpu.VMEM((_SUBLANES, CONV_HALO + tm, D_CONV), F32)],
        compiler_params=_params(32, 1),
    )(ag, ag, w32, cb, lg, lb)


def _conv_bwd(dc, yc, ag, w32, lg, lb, name):
    T = ag.shape[0]
    tm = _tile(T, 256)
    rc = _tile(tm, _ROWS_PER_CHUNK)
    I = T // tm
    hb = tm // CONV_HALO
    n_halo_blocks = T // CONV_HALO

    def body(dc_ref, yc_ref, dch_ref, ych_ref, ag_ref, agh_ref, w_ref, lg_ref, lb_ref,
             dag_ref, dw_ref, dcb_ref, dlg_ref, dlb_ref, uext_s, dext_s, ush_s, dsh_s):
        i = pl.program_id(0)
        lgv = lg_ref[...]
        lbv = lb_ref[...]

        def ln_bwd(dcv, ycv):
            n, rs = _layer_norm_stats(ycv)
            z = n * lgv + lbv
            dz = dcv * _silu_grad(z, jax.nn.sigmoid(z))
            dn = dz * lgv
            dy = rs * (dn - jnp.mean(dn, axis=-1, keepdims=True) - n * jnp.mean(dn * n, axis=-1, keepdims=True))
            return dy, dz, n

        dy, dz, n = ln_bwd(dc_ref[...], yc_ref[...])
        dyh, _, _ = ln_bwd(dch_ref[...], ych_ref[...])
        dext_s[0:tm, :] = dy
        dext_s[tm:tm + CONV_HALO, :] = jnp.where(i == I - 1, 0.0, dyh)
        a, sg = _glu_halo(ag_ref, agh_ref, uext_s, tm, i == 0)
        _shifted_copies(uext_s, ush_s, tm + CONV_HALO - _SUBLANES)
        _shifted_copies(dext_s, dsh_s, tm + CONV_HALO - _SUBLANES)

        @pl.when(i == 0)
        def _():
            dw_ref[...] = jnp.zeros_like(dw_ref)
            dcb_ref[...] = jnp.zeros_like(dcb_ref)
            dlg_ref[...] = jnp.zeros_like(dlg_ref)
            dlb_ref[...] = jnp.zeros_like(dlb_ref)

        dcb_ref[...] += jnp.sum(dy, axis=0, keepdims=True)
        dlg_ref[...] += jnp.sum(dz * n, axis=0, keepdims=True)
        dlb_ref[...] += jnp.sum(dz, axis=0, keepdims=True)
        for t in range(CONV_TAPS):
            u_t = _window(uext_s, ush_s, CONV_HALO - (CONV_TAPS - 1) + t, tm)
            dw_ref[t:t + 1, :] += jnp.sum(dy * u_t, axis=0, keepdims=True)
        for r0 in range(0, tm, rc):
            acc = jnp.zeros((rc, D_CONV), F32)
            for t in range(CONV_TAPS):
                acc = acc + _window(dext_s, dsh_s, r0 + (CONV_TAPS - 1) - t, rc) * w_ref[t:t + 1, :]
            a_c = a[r0:r0 + rc, :]
            sg_c = sg[r0:r0 + rc, :]
            dag_ref[r0:r0 + rc, :D_CONV] = (acc * sg_c).astype(dag_ref.dtype)
            dag_ref[r0:r0 + rc, D_CONV:] = (acc * a_c * sg_c * (1.0 - sg_c)).astype(dag_ref.dtype)

    row = lambda w: pl.BlockSpec((tm, w), lambda i: (i, 0))
    full = lambda a, b: pl.BlockSpec((a, b), lambda i: (0, 0))
    nxt = pl.BlockSpec((CONV_HALO, D_CONV), lambda i: (jnp.minimum((i + 1) * hb, n_halo_blocks - 1), 0))
    return _pallas_call(
        body, name=name, grid=(I,),
        in_specs=[row(D_CONV), row(D_CONV), nxt, nxt, row(2 * D_CONV),
                  pl.BlockSpec((CONV_HALO, 2 * D_CONV), lambda i: (jnp.maximum(i * hb - 1, 0), 0)),
                  full(CONV_HALO, D_CONV), full(1, D_CONV), full(1, D_CONV)],
        out_specs=[row(2 * D_CONV), full(CONV_HALO, D_CONV), full(1, D_CONV), full(1, D_CONV), full(1, D_CONV)],
        out_shape=[
            jax.ShapeDtypeStruct((T, 2 * D_CONV), MXU),
            jax.ShapeDtypeStruct((CONV_HALO, D_CONV), F32),
            jax.ShapeDtypeStruct((1, D_CONV), F32),
            jax.ShapeDtypeStruct((1, D_CONV), F32),
            jax.ShapeDtypeStruct((1, D_CONV), F32),
        ],
        scratch_shapes=[pltpu.VMEM((CONV_HALO + tm, D_CONV), F32), pltpu.VMEM((tm + CONV_HALO, D_CONV), F32),
                        pltpu.VMEM((_SUBLANES, CONV_HALO + tm, D_CONV), F32),
                        pltpu.VMEM((_SUBLANES, CONV_HALO + tm, D_CONV), F32)],
        compiler_params=_params(40, 1),
    )(dc, yc, dc, yc, ag, ag, w32, lg, lb)


def _outproj_fwd(x1, c, o, gc, ga, wout, name):
    T, D = x1.shape
    tm = _tile(T, 512)

    def body(x_ref, c_ref, o_ref, gc_ref, ga_ref, w_ref, x2_ref):
        yc, _ = _rms_fwd(c_ref[...], gc_ref[...])
        ya, _ = _rms_fwd(o_ref[...], ga_ref[...])
        x2_ref[...] = (x_ref[...] + _mm(yc.astype(MXU), w_ref[:D_CONV, :])
                       + _mm(ya.astype(MXU), w_ref[D_CONV:, :]))

    row = lambda w: pl.BlockSpec((tm, w), lambda i: (i, 0))
    full = lambda a, b: pl.BlockSpec((a, b), lambda i: (0, 0))
    return _pallas_call(
        body, name=name, grid=(T // tm,),
        in_specs=[row(D), row(D_CONV), row(D_ATTN), full(1, D_CONV), full(1, D_ATTN), full(D_CONV + D_ATTN, D)],
        out_specs=row(D),
        out_shape=jax.ShapeDtypeStruct((T, D), F32),
        compiler_params=_params(32, 1),
    )(x1, c, o, gc, ga, wout)


def _outproj_bwd(dx2, c, o, gc, ga, wout, name):
    T, D = dx2.shape
    tm = _tile(T, 256)
    I = T // tm

    def body(dx_ref, c_ref, o_ref, gc_ref, ga_ref, w_ref,
             dc_ref, doT_ref, dlT_ref, dw_ref, dgc_ref, dga_ref, acc_s):
        i = pl.program_id(0)
        dxb = dx_ref[...].astype(MXU)
        cv = c_ref[...]
        ov = o_ref[...]
        yc, rcn = _rms_fwd(cv, gc_ref[...])
        ya, ra = _rms_fwd(ov, ga_ref[...])
        dyc = _mm_nt(dxb, w_ref[:D_CONV, :])
        dya = _mm_nt(dxb, w_ref[D_CONV:, :])
        dwc = _mm_tn(yc.astype(MXU), dxb)
        dwa = _mm_tn(ya.astype(MXU), dxb)
        dcv, dgc = _rms_bwd(cv, rcn, gc_ref[...], dyc)
        dov, dga = _rms_bwd(ov, ra, ga_ref[...], dya)
        dc_ref[...] = dcv
        dob = dov.astype(doT_ref.dtype)
        doT_ref[...] = dov.T.astype(doT_ref.dtype)
        chan = lax.broadcasted_iota(jnp.int32, (D_ATTN, LANES), 0)
        head = lax.broadcasted_iota(jnp.int32, (D_ATTN, LANES), 1)
        in_head = ((chan >= head * HEAD_DIM) & (chan < (head + 1) * HEAD_DIM)).astype(jnp.bfloat16)
        dlT_ref[...] = _exact_dot_01(dob.astype(F32) * ov, in_head).T[:N_HEADS, :]

        @pl.when(i == 0)
        def _():
            acc_s[:D_CONV, :] = dwc
            acc_s[D_CONV:, :] = dwa
            dgc_ref[...] = dgc
            dga_ref[...] = dga

        @pl.when(i > 0)
        def _():
            acc_s[:D_CONV, :] += dwc
            acc_s[D_CONV:, :] += dwa
            dgc_ref[...] += dgc
            dga_ref[...] += dga

        @pl.when(i == I - 1)
        def _():
            dw_ref[...] = acc_s[...].astype(dw_ref.dtype)

    row = lambda w: pl.BlockSpec((tm, w), lambda i: (i, 0))
    full = lambda a, b: pl.BlockSpec((a, b), lambda i: (0, 0))
    return _pallas_call(
        body, name=name, grid=(I,),
        in_specs=[row(D), row(D_CONV), row(D_ATTN), full(1, D_CONV), full(1, D_ATTN), full(D_CONV + D_ATTN, D)],
        out_specs=[row(D_CONV), pl.BlockSpec((D_ATTN, tm), lambda i: (0, i)),
                   pl.BlockSpec((N_HEADS, tm), lambda i: (0, i)),
                   full(D_CONV + D_ATTN, D), full(1, D_CONV), full(1, D_ATTN)],
        out_shape=[
            jax.ShapeDtypeStruct((T, D_CONV), F32),
            jax.ShapeDtypeStruct((D_ATTN, T), MXU),
            jax.ShapeDtypeStruct((N_HEADS, T), F32),
            jax.ShapeDtypeStruct((D_CONV + D_ATTN, D), MXU),
            jax.ShapeDtypeStruct((1, D_CONV), F32),
            jax.ShapeDtypeStruct((1, D_ATTN), F32),
        ],
        scratch_shapes=[pltpu.VMEM((D_CONV + D_ATTN, D), F32)],
        compiler_params=_params(40, 1),
    )(dx2, c, o, gc, ga, wout)


def _ffn_down_loss(x, A, w2, gf, target, name):
    T, D = x.shape
    J, _, bf = A.shape
    tm = _tile(T, 512)

    def body(x_ref, A_ref, w2_ref, g_ref, t_ref, loss_ref, dx_ref, dg_ref):
        i = pl.program_id(0)
        f = _mm(A_ref[0], w2_ref[0:bf, :])
        for j in range(1, J):
            f = f + _mm(A_ref[j], w2_ref[j * bf:(j + 1) * bf, :])
        xv = x_ref[...] + 0.5 * f
        gv = g_ref[...]
        out, r = _rms_fwd(xv, gv)
        err = out - t_ref[...]
        part = jnp.full((1, LANES), 0.5 / D, F32) * jnp.sum(err * err)
        dxn, dgp = _rms_bwd(xv, r, gv, err * (1.0 / D))
        dx_ref[...] = dxn

        @pl.when(i == 0)
        def _():
            loss_ref[...] = part
            dg_ref[...] = dgp

        @pl.when(i > 0)
        def _():
            loss_ref[...] += part
            dg_ref[...] += dgp

    row = lambda w: pl.BlockSpec((tm, w), lambda i: (i, 0))
    full = lambda a, b: pl.BlockSpec((a, b), lambda i: (0, 0))
    return _pallas_call(
        body, name=name, grid=(T // tm,),
        in_specs=[row(D), pl.BlockSpec((J, tm, bf), lambda i: (0, i, 0)), full(J * bf, D), full(1, D), row(D)],
        out_specs=[full(1, LANES), row(D), full(1, D)],
        out_shape=[jax.ShapeDtypeStruct((1, LANES), F32), jax.ShapeDtypeStruct((T, D), F32),
                   jax.ShapeDtypeStruct((1, D), F32)],
        compiler_params=_params(56, 1),
    )(x, A, w2, gf, target)


def _row_tile(rows):
    for cand in (256, 176, 128, 64, 32, 16):
        if rows % cand == 0:
            return cand
    return rows


def _adamw(w, m, v, parts, name):
    R, C = w.shape
    P = parts.shape[0]
    tr = _row_tile(R)
    c1 = 1.0 - ADAM_B1 ** ADAM_STEP
    c2 = 1.0 - ADAM_B2 ** ADAM_STEP

    def body(w_ref, m_ref, v_ref, p_ref, g_ref, d_ref, nm_ref, nv_ref):
        g = p_ref[0].astype(F32)
        for s in range(1, P):
            g = g + p_ref[s].astype(F32)
        wv = w_ref[...]
        mn = ADAM_B1 * m_ref[...] + (1.0 - ADAM_B1) * g
        vn = ADAM_B2 * v_ref[...] + (1.0 - ADAM_B2) * (g * g)
        g_ref[...] = g
        nm_ref[...] = mn
        nv_ref[...] = vn
        d_ref[...] = -ADAM_LR * ((mn / c1) / (jnp.sqrt(vn / c2) + ADAM_EPS) + ADAM_WD * wv)

    blk = pl.BlockSpec((tr, C), lambda i: (i, 0))
    out = jax.ShapeDtypeStruct((R, C), F32)
    return _pallas_call(
        body, name=name, grid=(R // tr,),
        in_specs=[blk, blk, blk, pl.BlockSpec((P, tr, C), lambda i: (0, i, 0))],
        out_specs=[blk, blk, blk, blk],
        out_shape=[out, out, out, out],
        compiler_params=_params(32, 1),
    )(w, m, v, parts)


def _position():
    return lax.axis_index("x"), lax.axis_index("y"), lax.axis_index("c")


def _flat(px, py, pc):
    return 4 * px + 2 * py + pc


def _row_halves(rows, dtype):
    tile = _SUBLANES * (4 // jnp.dtype(dtype).itemsize)
    half = rows // 2 // tile * tile
    assert half > 0
    return (0, half), (half, rows - half)


def _gather_body(ins, outs, send_sems, recv_sems, local_sems, handshake):
    n = len(ins)
    x, y, c = _position()
    me, sibling = (x, y, c), (x, y, 1 - c)
    across_x, across_y, diagonal = (1 - x, y), (x, 1 - y), (1 - x, 1 - y)
    if handshake:
        _handshake([sibling] + [(*chip, cc) for chip in (across_x, across_y, diagonal) for cc in (c, 1 - c)])

    def copy(a, k, block, to, rows=None, src=None):
        dst = outs[a].at[_flat(*block)]
        if rows is not None:
            dst = dst.at[pl.ds(*rows)]
        return pltpu.make_async_remote_copy(
            src_ref=dst if src is None else src, dst_ref=dst,
            send_sem=send_sems.at[a, k], recv_sem=recv_sems.at[a, k],
            device_id=to, device_id_type=MESH)

    halves = [_row_halves(ins[a].shape[0], ins[a].dtype) for a in range(n)]
    mine = [pltpu.make_async_copy(ins[a], outs[a].at[_flat(*me)], local_sems.at[a]) for a in range(n)]
    for cp in mine:
        cp.start()
    sent = []

    def start(cp):
        cp.start()
        sent.append(cp)

    for a in range(n):
        start(copy(a, 0, me, sibling, src=ins[a]))
        start(copy(a, 1, me, (*across_x, c), src=ins[a]))
        start(copy(a, 2, me, (*across_y, c), src=ins[a]))
    for a in range(n):
        low, high = halves[a]
        copy(a, 1, (*across_x, c), me).wait_recv()
        start(copy(a, 3, (*across_x, c), (*across_y, c), rows=low))
        start(copy(a, 5, (*across_x, c), sibling))
        copy(a, 2, (*across_y, c), me).wait_recv()
        start(copy(a, 4, (*across_y, c), (*across_x, c), rows=high))
        start(copy(a, 6, (*across_y, c), sibling))
    for a in range(n):
        low, high = halves[a]
        copy(a, 3, (*diagonal, c), me, rows=low).wait_recv()
        start(copy(a, 7, (*diagonal, c), sibling, rows=low))
        copy(a, 4, (*diagonal, c), me, rows=high).wait_recv()
        start(copy(a, 8, (*diagonal, c), sibling, rows=high))
    for a in range(n):
        low, high = halves[a]
        copy(a, 0, sibling, me).wait_recv()
        copy(a, 5, (*across_x, 1 - c), me).wait_recv()
        copy(a, 6, (*across_y, 1 - c), me).wait_recv()
        copy(a, 7, (*diagonal, 1 - c), me, rows=low).wait_recv()
        copy(a, 8, (*diagonal, 1 - c), me, rows=high).wait_recv()
    for cp in sent:
        cp.wait_send()
    for cp in mine:
        cp.wait()


_GATHER_SLOTS = 9


def _gather_scratch(n):
    return [pltpu.SemaphoreType.DMA((n, _GATHER_SLOTS)), pltpu.SemaphoreType.DMA((n, _GATHER_SLOTS)),
            pltpu.SemaphoreType.DMA((n,))]


def _all_gather(shards, name):
    n = len(shards)

    def body(*refs):
        _gather_body(refs[:n], refs[n:2 * n], *refs[2 * n:], handshake=False)

    hbm = pl.BlockSpec(memory_space=pltpu.HBM)
    return _pallas_call(
        body, name=name,
        in_specs=[hbm] * n, out_specs=[hbm] * n,
        out_shape=[jax.ShapeDtypeStruct((N_DEV,) + s.shape, s.dtype) for s in shards],
        scratch_shapes=_gather_scratch(n),
    )(*shards)


def _handshake(peers):
    barrier = pltpu.get_barrier_semaphore()
    for peer in peers:
        pl.semaphore_signal(barrier, inc=1, device_id=peer, device_id_type=MESH)
    pl.semaphore_wait(barrier, len(peers))


def _sequencer_call(body, name, collective_id, out_type, scratch_types, operands):
    return pl.kernel(
        body, name=name, out_type=out_type,
        mesh=plsc.ScalarSubcoreMesh(axis_name="sequencer", num_cores=1),
        scratch_types=scratch_types,
        compiler_params=pltpu.CompilerParams(collective_id=collective_id),
    )(*operands)


def _seq_all_gather(shards, name, collective_id, after):
    n = len(shards)

    def body(*refs):
        _gather_body(refs[:n], refs[n + 1:2 * n + 1], *refs[2 * n + 1:], handshake=True)

    return _sequencer_call(
        body, name, collective_id,
        [jax.ShapeDtypeStruct((N_DEV,) + s.shape, s.dtype) for s in shards],
        _gather_scratch(n), list(shards) + [after])


def _seq_to_sibling(parts, name, collective_id, after):
    n = len(parts)

    def body(*refs):
        ins, outs = refs[:n], refs[n + len(after):2 * n + len(after)]
        send_sems, recv_sems = refs[2 * n + len(after):]
        x, y, c = _position()
        sibling = (x, y, 1 - c)
        _handshake([sibling])
        sent = []
        for a in range(n):
            for q in range(N_CHIPS):
                cp = pltpu.make_async_remote_copy(
                    src_ref=ins[a].at[2 * q + (1 - c)], dst_ref=outs[a].at[q],
                    send_sem=send_sems.at[a, q], recv_sem=recv_sems.at[a, q],
                    device_id=sibling, device_id_type=MESH)
                cp.start()
                sent.append(cp)
        for cp in sent:
            cp.wait_recv()
        for cp in sent:
            cp.wait_send()

    return _sequencer_call(
        body, name, collective_id,
        [jax.ShapeDtypeStruct((N_CHIPS,) + p.shape[1:], p.dtype) for p in parts],
        [pltpu.SemaphoreType.DMA((n, N_CHIPS)), pltpu.SemaphoreType.DMA((n, N_CHIPS))],
        list(parts) + list(after))


def _seq_to_chips(partials, name, collective_id, after=()):
    n = len(partials)

    def body(*refs):
        ins, outs = refs[:n], refs[n + len(after):2 * n + len(after)]
        send_sems, recv_sems, local_sems = refs[2 * n + len(after):]
        x, y, c = _position()
        my_chip = 2 * x + y
        chips = [(1 - x, y), (x, 1 - y), (1 - x, 1 - y)]
        _handshake([(*chip, c) for chip in chips])
        mine = [pltpu.make_async_copy(ins[a].at[my_chip], outs[a].at[my_chip], local_sems.at[a]) for a in range(n)]
        for cp in mine:
            cp.start()
        sent = []
        for a in range(n):
            for j, (px, py) in enumerate(chips):
                cp = pltpu.make_async_remote_copy(
                    src_ref=ins[a].at[2 * px + py], dst_ref=outs[a].at[my_chip],
                    send_sem=send_sems.at[a, j], recv_sem=recv_sems.at[a, j],
                    device_id=(px, py, c), device_id_type=MESH)
                cp.start()
                sent.append(cp)
        for cp in sent:
            cp.wait_recv()
        for cp in sent:
            cp.wait_send()
        for cp in mine:
            cp.wait()

    return _sequencer_call(
        body, name, collective_id,
        [jax.ShapeDtypeStruct(p.shape, p.dtype) for p in partials],
        [pltpu.SemaphoreType.DMA((n, 3)), pltpu.SemaphoreType.DMA((n, 3)), pltpu.SemaphoreType.DMA((n,))],
        list(partials) + list(after))


def _pair_add(parts, recvs, name, after=()):
    n = len(parts)
    core = lax.axis_index("c").astype(jnp.int32).reshape(1)

    def body(c_ref, *refs):
        ps, rs, outs = refs[:n], refs[n:2 * n], refs[2 * n + len(after):]
        for p_ref, r_ref, o_ref in zip(ps, rs, outs):
            o_ref[...] = (p_ref[...].astype(F32) + r_ref[...].astype(F32)).astype(o_ref.dtype)

    mine = lambda p: pl.BlockSpec((None,) + p.shape[1:], lambda q, c: (2 * q + c[0], 0, 0))
    blk = lambda p: pl.BlockSpec((None,) + p.shape[1:], lambda q, c: (q, 0, 0))
    return pl.pallas_call(
        body, name=name,
        grid_spec=pltpu.PrefetchScalarGridSpec(
            num_scalar_prefetch=1, grid=(N_CHIPS,),
            in_specs=[mine(p) for p in parts] + [blk(p) for p in parts] + [_UNREAD] * len(after),
            out_specs=[blk(p) for p in parts]),
        out_shape=[pltpu.HBM((N_CHIPS,) + p.shape[1:], p.dtype) for p in parts],
        compiler_params=_params(40, 1),
    )(core, *[pltpu.with_memory_space_constraint(a, pltpu.HBM) for a in (*parts, *recvs, *after)])


class _Reduced(NamedTuple):
    partials: list
    reduced: list


def _blocks(g):
    return g.reshape(N_DEV, -1, g.shape[-1])


def _reduce_scatter(parts, tag, ids, after=(), between=None, add_after=()):
    from_sibling = _seq_to_sibling(parts, "rs_sibling_" + tag, ids[0], after)
    mid = between(from_sibling[0]) if between else ()
    partials = _pair_add(parts, from_sibling, "rs_add_" + tag, add_after)
    return _Reduced(partials, _seq_to_chips(partials, "rs_chips_" + tag, ids[1], mid))


_SMALL = ("ffn1_norm", "mix_norm", "conv_b", "conv_ln_g", "conv_ln_b", "forget_b", "out_norm_conv",
          "out_norm_attn", "ffn2_norm", "final_norm")
_PACK_WIDTH = 2 * D_CONV
_SLOT = dict(ffn1_norm=(0, 0), mix_norm=(1, 0), ffn2_norm=(2, 0), final_norm=(3, 0), conv_b=(4, 0),
             conv_ln_g=(4, D_CONV), conv_ln_b=(5, 0), out_norm_conv=(5, D_CONV), out_norm_attn=(6, 0),
             forget_b=(6, D_CONV))
_CONV_ROW0 = 8
_PACK_ROWS = _CONV_ROW0 + CONV_HALO


def _pack_small(small, name):
    arrays = [small[n] for n in _SMALL] + [small["conv_w"]]

    def body(*refs):
        out = refs[-1]
        out[...] = jnp.zeros_like(out)
        for n, ref in zip(_SMALL, refs):
            row, lane = _SLOT[n]
            out[row:row + 1, lane:lane + ref.shape[1]] = ref[...]
        out[_CONV_ROW0:, :D_CONV] = refs[len(_SMALL)][...]

    return _pallas_call(body, name=name, out_shape=jax.ShapeDtypeStruct((_PACK_ROWS, _PACK_WIDTH), F32))(*arrays)


def _adamw_small(gathered, w, m, v, name):
    c1 = 1.0 - ADAM_B1 ** ADAM_STEP
    c2 = 1.0 - ADAM_B2 ** ADAM_STEP
    k = len(_SMALL)

    def body(g_ref, *refs):
        ws, ms, vs = refs[:k], refs[k:2 * k], refs[2 * k:3 * k]
        outs = refs[3 * k:]
        total = g_ref[0]
        for s in range(1, N_DEV):
            total = total + g_ref[s]
        for i, n in enumerate(_SMALL):
            row, lane = _SLOT[n]
            width = ws[i].shape[1]
            g = total[row:row + 1, lane:lane + width]
            mn = ADAM_B1 * ms[i][...] + (1.0 - ADAM_B1) * g
            vn = ADAM_B2 * vs[i][...] + (1.0 - ADAM_B2) * (g * g)
            o_g, o_d, o_m, o_v = outs[4 * i:4 * i + 4]
            o_g[...] = g
            o_m[...] = mn
            o_v[...] = vn
            o_d[...] = -ADAM_LR * ((mn / c1) / (jnp.sqrt(vn / c2) + ADAM_EPS) + ADAM_WD * ws[i][...])
        outs[4 * k][...] = total[_CONV_ROW0:, :D_CONV]

    shapes = []
    for n in _SMALL:
        shapes += [jax.ShapeDtypeStruct(w[n].shape, F32)] * 4
    shapes.append(jax.ShapeDtypeStruct((CONV_HALO, D_CONV), F32))
    res = _pallas_call(body, name=name, out_shape=shapes)(
        gathered, *[w[n] for n in _SMALL], *[m[n] for n in _SMALL], *[v[n] for n in _SMALL])
    return {n: res[4 * i:4 * i + 4] for i, n in enumerate(_SMALL)}, res[4 * k]


def _local_step(x, target, norms, shard):
    D = x.shape[1]
    J = N_DEV // 2
    as13 = lambda g: g.reshape(2, J, g.shape[1], D)

    (g13_1,) = _all_gather([shard["ffn1_w13"]], "gather_ffn1_w13")
    (g2_1,) = _seq_all_gather([shard["ffn1_w2"]], "gather_ffn1_w2", 10, after=g13_1)
    w13_1 = as13(g13_1)
    G1, U1, A1 = _ffn_up(x, norms["ffn1_norm"], w13_1, "ffn1_up")
    gin, gconv = _seq_all_gather([shard["w_in"], shard["conv_w"]], "gather_mix", 1, after=G1)
    w2_1 = g2_1.reshape(-1, D)
    x1 = _ffn_down(x, A1, w2_1, "ffn1_down")
    gout, g13_2, g2_2 = _seq_all_gather([shard["w_out"], shard["ffn2_w13"], shard["ffn2_w2"]], "gather_ffn2", 2,
                                        after=x1)
    wout = gout.reshape(-1, D)
    conv_w32 = jnp.pad(gconv.transpose(1, 0, 2).reshape(CONV_TAPS, D_CONV), ((0, CONV_HALO - CONV_TAPS), (0, 0)))

    ag, k, v, qT, kT, vT, fl = _inproj_fwd(x1, norms["mix_norm"], gin, "inproj_fwd")
    cum, cumT = _forget_fwd(fl, norms["forget_b"], "forget_fwd")
    yc, c = _conv_fwd(ag, conv_w32, norms["conv_b"], norms["conv_ln_g"], norms["conv_ln_b"], "conv_fwd")
    o, lseT = _attn_fwd(qT, k, vT, cum, cumT, "attn_fwd")
    x2 = _outproj_fwd(x1, c, o, norms["out_norm_conv"], norms["out_norm_attn"], wout, "outproj_fwd")
    w13_2, w2_2 = as13(g13_2), g2_2.reshape(-1, D)
    G2, U2, A2 = _ffn_up(x2, norms["ffn2_norm"], w13_2, "ffn2_up")
    loss, dx3, d_final = _ffn_down_loss(x2, A2, w2_2, norms["final_norm"], target, "ffn2_down_loss")

    dw2_2 = _ffn_w2_grad(dx3, A2, "ffn2_w2_grad")
    dx2, d_ffn2n, h3, dG2, dU2 = _ffn_bwd_act(x2, norms["ffn2_norm"], dx3, G2, U2, w13_2, w2_2, "ffn2_bwd_act")
    dw13_2 = _ffn_w13_grad(h3, dG2, dU2, "ffn2_w13_grad")
    dc, dobT, deltaT, dwout, d_onc, d_ona = _outproj_bwd(
        dx2, c, o, norms["out_norm_conv"], norms["out_norm_attn"], wout, "outproj_bwd")
    red_ffn2 = _reduce_scatter([_blocks(dw13_2), _blocks(dw2_2)], "ffn2", (3, 4), add_after=(dc,))
    dqT, dkT, dvT, dcum = _attn_bwd(qT, k, kT, v, dobT, lseT, deltaT, cum, cumT, "attn_bwd",
                                    after=red_ffn2.partials)
    dfl, d_fb = _forget_bwd(dcum, fl, norms["forget_b"], "forget_bwd")
    dag, d_convw, d_cb, d_lg, d_lb = _conv_bwd(dc, yc, ag, conv_w32, norms["conv_ln_g"], norms["conv_ln_b"], "conv_bwd")
    dx1, d_mixn, h2 = _inproj_bwd_act(x1, norms["mix_norm"], dx2, dag, dqT, dkT, dvT, dfl, gin, "inproj_bwd_act")
    dw2_1 = _ffn_w2_grad(dx1, A1, "ffn1_w2_grad")
    early = [_blocks(dwout), _blocks(dw2_1)]
    sib_early = _seq_to_sibling(early, "rs_sibling_mix_early", 11, red_ffn2.reduced[:1])
    dwin_blocks = _inproj_bwd_weights(h2, dag, dqT, dkT, dvT, dfl, "inproj_bwd_weights")
    sib_w_in = _seq_to_sibling([dwin_blocks], "rs_sibling_mix", 5, sib_early[:1])
    mix_partials = _pair_add([dwin_blocks] + early, sib_w_in + sib_early, "rs_add_mix")
    red_mix = _Reduced(mix_partials, _seq_to_chips(mix_partials, "rs_chips_mix", 6))
    dx, d_ffn1n, h1, dG1, dU1 = _ffn_bwd_act(x, norms["ffn1_norm"], dx1, G1, U1, w13_1, w2_1, "ffn1_bwd_act",
                                             after=red_mix.partials)
    dw13_1 = _ffn_w13_grad(h1, dG1, dU1, "ffn1_w13_grad")

    small = dict(ffn1_norm=d_ffn1n, mix_norm=d_mixn, conv_b=d_cb, conv_ln_g=d_lg, conv_ln_b=d_lb,
                 forget_b=d_fb, out_norm_conv=d_onc, out_norm_attn=d_ona, ffn2_norm=d_ffn2n,
                 final_norm=d_final, conv_w=d_convw)
    packed_small = _pack_small(small, "pack_small_grads")
    gathered_small = []

    def gather_small(behind):
        gathered_small.extend(_seq_all_gather([packed_small], "gather_small_grads", 9, after=behind))
        return gathered_small

    red_w13_1 = _reduce_scatter([_blocks(dw13_1)], "ffn1_w13", (7, 8), after=red_mix.reduced[:1],
                                between=gather_small)
    big = dict(ffn1_w13=red_w13_1.reduced[0], ffn1_w2=red_mix.reduced[2], w_in=red_mix.reduced[0],
               w_out=red_mix.reduced[1], ffn2_w13=red_ffn2.reduced[0], ffn2_w2=red_ffn2.reduced[1])
    return loss[0, 0], dx, gathered_small[0], big


_BIG = ("ffn1_w13", "ffn1_w2", "w_in", "w_out", "ffn2_w13", "ffn2_w2")
_TRANSPOSED = ("ffn1_w13", "ffn2_w13", "w_in")
_ORDER = ("ffn1_norm", "ffn1_w13", "ffn1_w2", "mix_norm", "w_in", "conv_w", "conv_b", "conv_ln_g", "conv_ln_b",
          "forget_b", "out_norm_conv", "out_norm_attn", "w_out", "ffn2_norm", "ffn2_w13", "ffn2_w2", "final_norm")


def kernel(x, ffn1_norm, ffn1_w13, ffn1_w2, mix_norm, w_in, conv_w, conv_b, conv_ln_g, conv_ln_b, forget_b, out_norm_conv, out_norm_attn, w_out, ffn2_norm, ffn2_w13, ffn2_w2, final_norm, loss_target, m_ffn1_norm, m_ffn1_w13, m_ffn1_w2, m_mix_norm, m_w_in, m_conv_w, m_conv_b, m_conv_ln_g, m_conv_ln_b, m_forget_b, m_out_norm_conv, m_out_norm_attn, m_w_out, m_ffn2_norm, m_ffn2_w13, m_ffn2_w2, m_final_norm, v_ffn1_norm, v_ffn1_w13, v_ffn1_w2, v_mix_norm, v_w_in, v_conv_w, v_conv_b, v_conv_ln_g, v_conv_ln_b, v_forget_b, v_out_norm_conv, v_out_norm_attn, v_w_out, v_ffn2_norm, v_ffn2_w13, v_ffn2_w2, v_final_norm):
    w = dict(ffn1_norm=ffn1_norm, ffn1_w13=ffn1_w13, ffn1_w2=ffn1_w2, mix_norm=mix_norm, w_in=w_in, conv_w=conv_w,
             conv_b=conv_b, conv_ln_g=conv_ln_g, conv_ln_b=conv_ln_b, forget_b=forget_b, out_norm_conv=out_norm_conv,
             out_norm_attn=out_norm_attn, w_out=w_out, ffn2_norm=ffn2_norm, ffn2_w13=ffn2_w13, ffn2_w2=ffn2_w2,
             final_norm=final_norm)
    m = dict(ffn1_norm=m_ffn1_norm, ffn1_w13=m_ffn1_w13, ffn1_w2=m_ffn1_w2, mix_norm=m_mix_norm, w_in=m_w_in,
             conv_w=m_conv_w, conv_b=m_conv_b, conv_ln_g=m_conv_ln_g, conv_ln_b=m_conv_ln_b, forget_b=m_forget_b,
             out_norm_conv=m_out_norm_conv, out_norm_attn=m_out_norm_attn, w_out=m_w_out, ffn2_norm=m_ffn2_norm,
             ffn2_w13=m_ffn2_w13, ffn2_w2=m_ffn2_w2, final_norm=m_final_norm)
    v = dict(ffn1_norm=v_ffn1_norm, ffn1_w13=v_ffn1_w13, ffn1_w2=v_ffn1_w2, mix_norm=v_mix_norm, w_in=v_w_in,
             conv_w=v_conv_w, conv_b=v_conv_b, conv_ln_g=v_conv_ln_g, conv_ln_b=v_conv_ln_b, forget_b=v_forget_b,
             out_norm_conv=v_out_norm_conv, out_norm_attn=v_out_norm_attn, w_out=v_w_out, ffn2_norm=v_ffn2_norm,
             ffn2_w13=v_ffn2_w13, ffn2_w2=v_ffn2_w2, final_norm=v_final_norm)
    shapes = {n: a.shape for n, a in w.items()}
    T, D = x.shape[1], x.shape[2]
    def two(n, a):
        if a.ndim != 3:
            return a.reshape(1, -1)
        a = a.reshape(a.shape[-2], a.shape[-1])
        return a.T if n in _TRANSPOSED else a

    w2d = {n: two(n, a) for n, a in w.items()}
    m2d = {n: two(n, a) for n, a in m.items()}
    v2d = {n: two(n, a) for n, a in v.items()}

    shard = {n: w2d[n].astype(MXU) for n in _BIG}
    shard["conv_w"] = w2d["conv_w"]
    norms = {n: w2d[n] for n in _SMALL}
    norms["forget_b"] = jnp.pad(w2d["forget_b"], ((0, 0), (0, LANES - N_HEADS)))
    loss_part, dx, gathered_small, big = _local_step(x[0], loss_target[0], norms, shard)
    loss = lax.psum(loss_part, ("x", "y", "c"))

    grads, deltas, new_m, new_v = {}, {}, {}, {}
    for n in _BIG:
        g, d, nm, nv = _adamw(w2d[n], m2d[n], v2d[n], big[n], "adamw_" + n)
        grads[n], deltas[n], new_m[n], new_v[n] = g, d, nm, nv

    small_out, conv_g_full = _adamw_small(gathered_small, w2d, m2d, v2d, "adamw_small")
    for n in _SMALL:
        grads[n], deltas[n], new_m[n], new_v[n] = small_out[n]
    conv_g_full = conv_g_full[:CONV_TAPS]
    xi, yi, ci = _position()
    cw = shapes["conv_w"][-1]
    conv_g_mine = lax.dynamic_slice_in_dim(conv_g_full, _flat(xi, yi, ci) * cw, cw, axis=1)
    g, d, nm, nv = _adamw(w2d["conv_w"], m2d["conv_w"], v2d["conv_w"], conv_g_mine[None], "adamw_conv_w")
    grads["conv_w"], deltas["conv_w"], new_m["conv_w"], new_v["conv_w"] = g, d, nm, nv

    shaped = lambda dct: [(dct[n].T if n in _TRANSPOSED else dct[n]).reshape(shapes[n]) for n in _ORDER]
    return (loss, dx[None], *shaped(grads), *shaped(deltas), *shaped(new_m), *shaped(new_v))
```

```python
from typing import NamedTuple

import jax
import jax.numpy as jnp
from jax import lax
from jax.experimental import pallas as pl
from jax.experimental.pallas import tpu as pltpu
from jax.experimental.pallas import tpu_sc as plsc

F32 = jnp.float32
MXU = jnp.bfloat16
EPS = 1e-6
N_HEADS = 8
HEAD_DIM = 64
D_CONV = 512
D_ATTN = N_HEADS * HEAD_DIM
CONV_TAPS = 31
CONV_HALO = 32
SCALE = HEAD_DIM ** -0.5
NEG = -1e30
LANES = 128
N_DEV = 8
N_CHIPS = N_DEV // 2
MESH = pl.DeviceIdType.MESH
MIB = 1 << 20

ADAM_LR = 0.001
ADAM_B1 = 0.9
ADAM_B2 = 0.999
ADAM_EPS = 1e-08
ADAM_WD = 0.01
ADAM_STEP = 10


_UNREAD = pl.BlockSpec(memory_space=pl.ANY)


def _pallas_call(body, *, out_shape, **kwargs):
    in_hbm = lambda s: pltpu.HBM(s.shape, s.dtype)
    outs = [in_hbm(s) for s in out_shape] if isinstance(out_shape, (list, tuple)) else in_hbm(out_shape)
    call = pl.pallas_call(body, out_shape=outs, **kwargs)
    return lambda *operands: call(*[pltpu.with_memory_space_constraint(a, pltpu.HBM) for a in operands])


def _params(vmem_mib, n_axes):
    return pltpu.CompilerParams(dimension_semantics=("arbitrary",) * n_axes, vmem_limit_bytes=vmem_mib * MIB)


def _mm(a, b):
    return jnp.dot(a, b, preferred_element_type=F32)


def _mm_nt(a, b):
    return lax.dot_general(a, b, (((1,), (1,)), ((), ())), preferred_element_type=F32)


def _mm_tn(a, b):
    return lax.dot_general(a, b, (((0,), (0,)), ((), ())), preferred_element_type=F32)


def _rms_fwd(x, g):
    r = lax.rsqrt(jnp.mean(x * x, axis=-1, keepdims=True) + EPS)
    return x * r * g, r


def _rms_bwd(x, r, g, dy):
    gdy = dy * g
    dx = r * gdy - x * (r * r * r) * jnp.mean(x * gdy, axis=-1, keepdims=True)
    dg = jnp.sum(dy * x * r, axis=0, keepdims=True)
    return dx, dg


def _silu_grad(z, sz):
    return sz * (1.0 + z * (1.0 - sz))


def _three_terms(x):
    x1 = x.astype(jnp.bfloat16)
    r1 = x - x1.astype(F32)
    x2 = r1.astype(jnp.bfloat16)
    x3 = (r1 - x2.astype(F32)).astype(jnp.bfloat16)
    return x1, x2, x3


def _exact_tri_dot(tri, x):
    x1, x2, x3 = _three_terms(x)
    return _mm(tri, x1) + _mm(tri, x2) + _mm(tri, x3)


def _exact_dot_01(x, sel):
    x1, x2, x3 = _three_terms(x)
    return _mm(x1, sel) + _mm(x2, sel) + _mm(x3, sel)


def _tile(n, want):
    t = min(n, want)
    assert n % t == 0
    return t


_FFN_CHUNK = 256


def _ffn_up(x, g, w13, name):
    T, D = x.shape
    _, J, bf, _ = w13.shape
    tm = _tile(T, 512)
    I = T // tm

    def body(x_ref, g_ref, w13_ref, G_ref, U_ref, A_ref, h_s):
        j = pl.program_id(0)
        i = pl.program_id(1)
        rows = pl.ds(pl.multiple_of(i * tm, tm), tm)

        @pl.when(j == 0)
        def _():
            h, _ = _rms_fwd(x_ref[...], g_ref[...])
            h_s[rows, :] = h.astype(MXU)

        chunks = [slice(r0, r0 + _FFN_CHUNK) for r0 in range(0, tm, _FFN_CHUNK)]
        hbs = [h_s[pl.ds(pl.multiple_of(i * tm + rs.start, _FFN_CHUNK), _FFN_CHUNK), :] for rs in chunks]
        GU = [(_mm_nt(hb, w13_ref[0]), _mm_nt(hb, w13_ref[1])) for hb in hbs]
        for rs, (G, U) in zip(chunks, GU):
            G_ref[rs, :] = G.astype(MXU)
            U_ref[rs, :] = U.astype(MXU)
            A_ref[rs, :] = (G * jax.nn.sigmoid(G) * U).astype(MXU)

    blk = pl.BlockSpec((None, tm, bf), lambda j, i: (j, i, 0))
    hid = jax.ShapeDtypeStruct((J, T, bf), MXU)
    return _pallas_call(
        body, name=name, grid=(J, I),
        in_specs=[pl.BlockSpec((tm, D), lambda j, i: (jnp.where(j == 0, i, I - 1), 0)),
                  pl.BlockSpec((1, D), lambda j, i: (0, 0)),
                  pl.BlockSpec((2, None, bf, D), lambda j, i: (0, j, 0, 0))],
        out_specs=[blk, blk, blk],
        out_shape=[hid, hid, hid],
        scratch_shapes=[pltpu.VMEM((T, D), MXU)],
        compiler_params=_params(40, 2),
    )(x, g, w13)


def _ffn_down(x, A, w2, name):
    T, D = x.shape
    J, _, bf = A.shape
    tm = _tile(T, 512)

    def body(x_ref, A_ref, w2_ref, xo_ref):
        f = _mm(A_ref[0], w2_ref[0:bf, :])
        for j in range(1, J):
            f = f + _mm(A_ref[j], w2_ref[j * bf:(j + 1) * bf, :])
        xo_ref[...] = x_ref[...] + 0.5 * f

    row = pl.BlockSpec((tm, D), lambda i: (i, 0))
    return _pallas_call(
        body, name=name, grid=(T // tm,),
        in_specs=[row, pl.BlockSpec((J, tm, bf), lambda i: (0, i, 0)), pl.BlockSpec((J * bf, D), lambda i: (0, 0))],
        out_specs=row,
        out_shape=jax.ShapeDtypeStruct((T, D), F32),
        compiler_params=_params(48, 1),
    )(x, A, w2)


def _ffn_bwd_act(x, g, dy, Gs, Us, w13, w2, name, after=()):
    T, D = x.shape
    _, J, bf, _ = w13.shape
    tm = _tile(T, 512)
    I = T // tm

    def body(x_ref, g_ref, dy_ref, G_ref, U_ref, w13_ref, w2_ref, *rest):
        dx_ref, dg_ref, h_ref, dG_ref, dU_ref, dh_s, dF_s, h_s = rest[len(after):]
        j = pl.program_id(0)
        i = pl.program_id(1)
        rows = pl.ds(pl.multiple_of(i * tm, tm), tm)

        @pl.when(j == 0)
        def _():
            h, _ = _rms_fwd(x_ref[...], g_ref[...])
            hb = h.astype(MXU)
            h_s[rows, :] = hb
            h_ref[...] = hb
            dF_s[rows, :] = (0.5 * dy_ref[...]).astype(MXU)
            dh_s[rows, :] = jnp.zeros((tm, D), F32)

        chunks = [slice(r0, r0 + _FFN_CHUNK) for r0 in range(0, tm, _FFN_CHUNK)]
        crows = [pl.ds(pl.multiple_of(i * tm + rs.start, _FFN_CHUNK), _FFN_CHUNK) for rs in chunks]
        dAs = [_mm_nt(dF_s[cr, :], w2_ref[...]) for cr in crows]
        for rs, cr, dA in zip(chunks, crows, dAs):
            G = G_ref[rs, :].astype(F32)
            U = U_ref[rs, :].astype(F32)
            sg = jax.nn.sigmoid(G)
            s = G * sg
            dUb = (dA * s).astype(MXU)
            dGb = (dA * U * _silu_grad(G, sg)).astype(MXU)
            dG_ref[rs, :] = dGb
            dU_ref[rs, :] = dUb
            dh_s[cr, :] += _mm(dGb, w13_ref[0]) + _mm(dUb, w13_ref[1])

        @pl.when(j == J - 1)
        def _():
            xv = x_ref[...]
            gv = g_ref[...]
            _, r = _rms_fwd(xv, gv)
            dxn, dgp = _rms_bwd(xv, r, gv, dh_s[rows, :])
            dx_ref[...] = dy_ref[...] + dxn

            @pl.when(i == 0)
            def _():
                dg_ref[...] = dgp

            @pl.when(i > 0)
            def _():
                dg_ref[...] += dgp

    ends = lambda j, i: (jnp.where((j == 0) | (j == J - 1), i, I - 1), 0)
    blk = pl.BlockSpec((None, tm, bf), lambda j, i: (j, i, 0))
    hid = jax.ShapeDtypeStruct((J, T, bf), MXU)
    return _pallas_call(
        body, name=name, grid=(J, I),
        in_specs=[pl.BlockSpec((tm, D), ends), pl.BlockSpec((1, D), lambda j, i: (0, 0)), pl.BlockSpec((tm, D), ends),
                  blk, blk, pl.BlockSpec((2, None, bf, D), lambda j, i: (0, j, 0, 0)),
                  pl.BlockSpec((bf, D), lambda j, i: (j, 0))] + [_UNREAD] * len(after),
        out_specs=[pl.BlockSpec((tm, D), lambda j, i: (jnp.where(j == J - 1, i, 0), 0)),
                   pl.BlockSpec((1, D), lambda j, i: (0, 0)),
                   pl.BlockSpec((tm, D), lambda j, i: (jnp.where(j == 0, i, I - 1), 0)), blk, blk],
        out_shape=[jax.ShapeDtypeStruct((T, D), F32), jax.ShapeDtypeStruct((1, D), F32),
                   jax.ShapeDtypeStruct((T, D), MXU), hid, hid],
        scratch_shapes=[pltpu.VMEM((T, D), F32), pltpu.VMEM((T, D), MXU), pltpu.VMEM((T, D), MXU)],
        compiler_params=_params(58, 2),
    )(x, g, dy, Gs, Us, w13, w2, *after)


def _ffn_w13_grad(h, dG, dU, name):
    T, D = h.shape
    J, _, bf = dG.shape

    def body(h_ref, dG_ref, dU_ref, dw13_ref):
        dw13_ref[0] = _mm_tn(dG_ref[...], h_ref[...]).astype(dw13_ref.dtype)
        dw13_ref[1] = _mm_tn(dU_ref[...], h_ref[...]).astype(dw13_ref.dtype)

    blk = pl.BlockSpec((None, T, bf), lambda j: (j, 0, 0))
    return _pallas_call(
        body, name=name, grid=(J,),
        in_specs=[pl.BlockSpec((T, D), lambda j: (0, 0)), blk, blk],
        out_specs=pl.BlockSpec((2, None, bf, D), lambda j: (0, j, 0, 0)),
        out_shape=jax.ShapeDtypeStruct((2, J, bf, D), MXU),
        compiler_params=_params(48, 1),
    )(h, dG, dU)


def _ffn_w2_grad(dy, A, name, after=()):
    T, D = dy.shape
    J, _, bf = A.shape

    def body(dy_ref, A_ref, *rest):
        dw2_ref, dF_s = rest[len(after):]

        @pl.when(pl.program_id(0) == 0)
        def _():
            dF_s[...] = (0.5 * dy_ref[...]).astype(MXU)

        dw2_ref[...] = _mm_tn(A_ref[...], dF_s[...]).astype(dw2_ref.dtype)

    return _pallas_call(
        body, name=name, grid=(J,),
        in_specs=[pl.BlockSpec((T, D), lambda j: (0, 0)), pl.BlockSpec((None, T, bf), lambda j: (j, 0, 0))]
        + [_UNREAD] * len(after),
        out_specs=pl.BlockSpec((bf, D), lambda j: (j, 0)),
        out_shape=jax.ShapeDtypeStruct((J * bf, D), MXU),
        scratch_shapes=[pltpu.VMEM((T, D), MXU)],
        compiler_params=_params(48, 1),
    )(dy, A, *after)


_AG0, _Q0, _K0, _V0, _F0 = 0, 2 * D_CONV, 2 * D_CONV + D_ATTN, 2 * D_CONV + 2 * D_ATTN, 2 * D_CONV + 3 * D_ATTN
N_IN = _F0 + N_HEADS
N_IN_PAD = _F0 + LANES
_IN_BLOCK = N_IN // N_DEV


def _rows_from_blocks(blocks_ref, rows_ref):
    for p in range(N_DEV):
        rows_ref[_IN_BLOCK * p:_IN_BLOCK * (p + 1), :] = blocks_ref[p]
    rows_ref[N_IN:, :] = jnp.zeros((N_IN_PAD - N_IN, rows_ref.shape[1]), rows_ref.dtype)


def _inproj_fwd(x1, gm, win_blocks, name):
    T, D = x1.shape
    tm = _tile(T, 256)

    def body(x_ref, g_ref, wb_ref, ag_ref, k_ref, v_ref, qT_ref, kT_ref, vT_ref, fl_ref, w_ref):
        @pl.when(pl.program_id(0) == 0)
        def _():
            _rows_from_blocks(wb_ref, w_ref)

        h, _ = _rms_fwd(x_ref[...], g_ref[...])
        hb = h.astype(MXU)
        ag_ref[...] = _mm_nt(hb, w_ref[_AG0:_Q0, :])
        qT_ref[...] = (_mm_nt(hb, w_ref[_Q0:_K0, :]) * SCALE).T.astype(MXU)
        for c0, ref, refT in ((_K0, k_ref, kT_ref), (_V0, v_ref, vT_ref)):
            y = _mm_nt(hb, w_ref[c0:c0 + D_ATTN, :])
            ref[...] = y.astype(MXU)
            refT[...] = y.T.astype(MXU)
        fl_ref[...] = _mm_nt(hb, w_ref[_F0:N_IN_PAD, :])

    row = lambda w: pl.BlockSpec((tm, w), lambda i: (i, 0))
    col = pl.BlockSpec((D_ATTN, tm), lambda i: (0, i))
    std = jax.ShapeDtypeStruct((T, D_ATTN), MXU)
    trn = jax.ShapeDtypeStruct((D_ATTN, T), MXU)
    return _pallas_call(
        body, name=name, grid=(T // tm,),
        in_specs=[row(D), pl.BlockSpec((1, D), lambda i: (0, 0)),
                  pl.BlockSpec((N_DEV, _IN_BLOCK, D), lambda i: (0, 0, 0))],
        out_specs=[row(2 * D_CONV), row(D_ATTN), row(D_ATTN), col, col, col, row(LANES)],
        out_shape=[jax.ShapeDtypeStruct((T, 2 * D_CONV), F32), std, std, trn, trn, trn,
                   jax.ShapeDtypeStruct((T, LANES), F32)],
        scratch_shapes=[pltpu.VMEM((N_IN_PAD, D), MXU)],
        compiler_params=_params(40, 1),
    )(x1, gm, win_blocks)


def _inproj_bwd_act(x1, gm, dx2, dag, dqT, dkT, dvT, dfl, win_blocks, name):
    T, D = x1.shape
    tm = _tile(T, 256)

    def body(x_ref, g_ref, dx2_ref, dag_ref, dqT_ref, dkT_ref, dvT_ref, dfl_ref, wb_ref, dx1_ref, dg_ref, h_ref,
             w_ref):
        i = pl.program_id(0)

        @pl.when(i == 0)
        def _():
            _rows_from_blocks(wb_ref, w_ref)

        xv = x_ref[...]
        gv = g_ref[...]
        h, r = _rms_fwd(xv, gv)
        h_ref[...] = h.astype(MXU)
        dh = _mm(dag_ref[...], w_ref[_AG0:_Q0, :])
        for c0, ref in ((_Q0, dqT_ref), (_K0, dkT_ref), (_V0, dvT_ref)):
            dh = dh + _mm_tn(ref[...].astype(MXU), w_ref[c0:c0 + D_ATTN, :])
        dh = dh + _mm(dfl_ref[...].astype(MXU), w_ref[_F0:N_IN_PAD, :])
        dxn, dgp = _rms_bwd(xv, r, gv, dh)
        dx1_ref[...] = dx2_ref[...] + dxn

        @pl.when(i == 0)
        def _():
            dg_ref[...] = dgp

        @pl.when(i > 0)
        def _():
            dg_ref[...] += dgp

    row = lambda w: pl.BlockSpec((tm, w), lambda i: (i, 0))
    col = pl.BlockSpec((D_ATTN, tm), lambda i: (0, i))
    full = lambda a, b: pl.BlockSpec((a, b), lambda i: (0, 0))
    return _pallas_call(
        body, name=name, grid=(T // tm,),
        in_specs=[row(D), full(1, D), row(D), row(2 * D_CONV), col, col, col, row(LANES),
                  pl.BlockSpec((N_DEV, _IN_BLOCK, D), lambda i: (0, 0, 0))],
        out_specs=[row(D), full(1, D), row(D)],
        out_shape=[jax.ShapeDtypeStruct((T, D), F32), jax.ShapeDtypeStruct((1, D), F32),
                   jax.ShapeDtypeStruct((T, D), MXU)],
        scratch_shapes=[pltpu.VMEM((N_IN_PAD, D), MXU)],
        compiler_params=_params(40, 1),
    )(x1, gm, dx2, dag, dqT, dkT, dvT, dfl, win_blocks)


def _inproj_bwd_weights(h, dag, dqT, dkT, dvT, dfl, name, after=()):
    T, D = h.shape

    def body(h_ref, dag_ref, dqT_ref, dkT_ref, dvT_ref, dfl_ref, *rest):
        blocks_ref, dw_ref = rest[len(after):]
        hb = h_ref[...]
        dw_ref[_AG0:_Q0, :] = _mm_tn(dag_ref[...], hb).astype(dw_ref.dtype)
        for c0, ref in ((_Q0, dqT_ref), (_K0, dkT_ref), (_V0, dvT_ref)):
            dw_ref[c0:c0 + D_ATTN, :] = _mm(ref[...].astype(MXU), hb).astype(dw_ref.dtype)
        dw_ref[_F0:N_IN_PAD, :] = _mm_tn(dfl_ref[...].astype(MXU), hb).astype(dw_ref.dtype)
        for p in range(N_DEV):
            blocks_ref[p] = dw_ref[_IN_BLOCK * p:_IN_BLOCK * (p + 1), :]

    vmem = pl.BlockSpec(memory_space=pltpu.VMEM)
    return _pallas_call(
        body, name=name, in_specs=[vmem] * 6 + [_UNREAD] * len(after), out_specs=vmem,
        out_shape=jax.ShapeDtypeStruct((N_DEV, _IN_BLOCK, D), MXU),
        scratch_shapes=[pltpu.VMEM((N_IN_PAD, D), MXU)],
        compiler_params=pltpu.CompilerParams(vmem_limit_bytes=56 * MIB),
    )(h, dag, dqT, dkT, dvT, dfl, *after)


def _forget_fwd(fl, fbp, name):
    T = fl.shape[0]
    tb = _tile(T, 256)

    def body(fl_ref, fb_ref, cum_ref, cumT_ref):
        ri = lax.broadcasted_iota(jnp.int32, (tb, tb), 0)
        ci = lax.broadcasted_iota(jnp.int32, (tb, tb), 1)
        tri = (ri >= ci).astype(jnp.bfloat16)
        carry = jnp.zeros((1, LANES), F32)
        for b in range(T // tb):
            z = fl_ref[b * tb:(b + 1) * tb, :] + fb_ref[...]
            lf = jnp.minimum(z, 0.0) - jnp.log1p(jnp.exp(-jnp.abs(z)))
            c = _exact_tri_dot(tri, lf) + carry
            cum_ref[b * tb:(b + 1) * tb, :] = c
            carry = c[tb - 1:tb, :]
        cumT_ref[...] = cum_ref[...].T[:N_HEADS, :]

    return _pallas_call(
        body, name=name,
        out_shape=[jax.ShapeDtypeStruct((T, LANES), F32), jax.ShapeDtypeStruct((N_HEADS, T), F32)],
        compiler_params=pltpu.CompilerParams(vmem_limit_bytes=32 * MIB),
    )(fl, fbp)


def _forget_bwd(dcum, fl, fbp, name):
    T = fl.shape[0]
    tb = _tile(T, 256)

    def body(dc_ref, fl_ref, fb_ref, dfl_ref, dfb_ref):
        ri = lax.broadcasted_iota(jnp.int32, (tb, tb), 0)
        ci = lax.broadcasted_iota(jnp.int32, (tb, tb), 1)
        tri = (ri <= ci).astype(jnp.bfloat16)
        carry = jnp.zeros((1, LANES), F32)
        dfb = jnp.zeros((1, LANES), F32)
        for b in reversed(range(T // tb)):
            sl = slice(b * tb, (b + 1) * tb)
            dl = _exact_tri_dot(tri, dc_ref[sl, :]) + carry
            carry = dl[0:1, :]
            z = fl_ref[sl, :] + fb_ref[...]
            dfl = dl * jax.nn.sigmoid(-z)
            dfl_ref[sl, :] = dfl
            dfb = dfb + jnp.sum(dfl, axis=0, keepdims=True)
        dfb_ref[...] = dfb

    return _pallas_call(
        body, name=name,
        out_shape=[jax.ShapeDtypeStruct((T, LANES), F32), jax.ShapeDtypeStruct((1, LANES), F32)],
        compiler_params=pltpu.CompilerParams(vmem_limit_bytes=32 * MIB),
    )(dcum, fl, fbp)


def _causal_keep(i, j, tq, tk):
    key = j * tk + lax.broadcasted_iota(jnp.int32, (tk, tq), 0)
    qry = i * tq + lax.broadcasted_iota(jnp.int32, (tk, tq), 1)
    return key <= qry


def _split_hi_lo(x):
    hi = x.astype(MXU)
    lo = (x - hi.astype(F32)).astype(MXU)
    return hi, lo


def _attn_fwd(qT, k, vT, cum, cumT, name):
    T = k.shape[0]
    tq = _tile(T, 256)
    tk = _tile(tq, 256)
    kpq = tq // tk
    heads = [slice(HEAD_DIM * h, HEAD_DIM * (h + 1)) for h in range(N_HEADS)]

    def body(qT_ref, k_ref, vT_ref, cum_ref, cumT_ref, o_ref, lseT_ref, acc_s, m_s, l_s):
        i = pl.program_id(0)
        acc_s[...] = jnp.zeros_like(acc_s)
        m_s[...] = jnp.full_like(m_s, NEG)
        l_s[...] = jnp.zeros_like(l_s)

        def kblock(j, masked):
            rows = pl.ds(pl.multiple_of(j * tk, tk), tk)
            keep = _causal_keep(i, j, tq, tk) if masked else None
            bias = [cumT_ref[h:h + 1, :] - cum_ref[rows, h:h + 1] for h in range(N_HEADS)]
            qk = [_mm(k_ref[rows, hs], qT_ref[hs, :]) + bias[h] for h, hs in enumerate(heads)]
            for h, hs in enumerate(heads):
                sT = qk[h]
                if masked:
                    sT = jnp.where(keep, sT, NEG)
                m_old = m_s[h:h + 1, :]
                m_new = jnp.maximum(m_old, jnp.max(sT, axis=0, keepdims=True))
                alpha = jnp.exp(m_old - m_new)
                pT = jnp.exp(sT - m_new)
                l_s[h:h + 1, :] = alpha * l_s[h:h + 1, :] + jnp.sum(pT, axis=0, keepdims=True)
                p_hi, p_lo = _split_hi_lo(pT)
                vh = vT_ref[hs, rows]
                acc_s[hs, :] = alpha * acc_s[hs, :] + (_mm(vh, p_hi) + _mm(vh, p_lo))
                m_s[h:h + 1, :] = m_new

        def unmasked(j, c):
            kblock(j, False)
            return c

        lax.fori_loop(0, kpq * i, unmasked, 0)
        for d in range(kpq):
            kblock(kpq * i + d, True)
        for h, hs in enumerate(heads):
            acc_s[hs, :] = acc_s[hs, :] / l_s[h:h + 1, :]
        o_ref[...] = acc_s[...].T
        lseT_ref[...] = m_s[...] + jnp.log(l_s[...])

    full = lambda a, b: pl.BlockSpec((a, b), lambda i: (0, 0))
    colblk = lambda r: pl.BlockSpec((r, tq), lambda i: (0, i))
    return _pallas_call(
        body, name=name, grid=(T // tq,),
        in_specs=[colblk(D_ATTN), full(T, D_ATTN), full(D_ATTN, T), full(T, LANES), colblk(N_HEADS)],
        out_specs=[pl.BlockSpec((tq, D_ATTN), lambda i: (i, 0)), colblk(N_HEADS)],
        out_shape=[jax.ShapeDtypeStruct((T, D_ATTN), F32), jax.ShapeDtypeStruct((N_HEADS, T), F32)],
        scratch_shapes=[pltpu.VMEM((D_ATTN, tq), F32), pltpu.VMEM((N_HEADS, tq), F32),
                        pltpu.VMEM((N_HEADS, tq), F32)],
        compiler_params=_params(40, 1),
    )(qT, k, vT, cum, cumT)


def _attn_bwd(qT, k, kT, v, doT, lseT, deltaT, cum, cumT, name, after=()):
    T = k.shape[0]
    tq = _tile(T, 256)
    tk = _tile(tq, 256)
    kpq = tq // tk
    heads = [slice(HEAD_DIM * h, HEAD_DIM * (h + 1)) for h in range(N_HEADS)]

    def body(qT_ref, k_ref, kT_ref, v_ref, doT_ref, lseT_ref, dlT_ref, cum_ref, cumT_ref, *rest):
        dq_ref, dk_ref, dv_ref, dcum_ref, dq_s = rest[len(after):]
        i = pl.program_id(0)

        @pl.when(i == 0)
        def _():
            dk_ref[...] = jnp.zeros_like(dk_ref)
            dv_ref[...] = jnp.zeros_like(dv_ref)
            dcum_ref[...] = jnp.zeros_like(dcum_ref)

        dq_s[...] = jnp.zeros_like(dq_s)

        def kblock(j, masked):
            rows = pl.ds(pl.multiple_of(j * tk, tk), tk)
            keep = _causal_keep(i, j, tq, tk) if masked else None
            bias = [cumT_ref[h:h + 1, :] - cum_ref[rows, h:h + 1] for h in range(N_HEADS)]
            qk = [_mm(k_ref[rows, hs], qT_ref[hs, :]) + bias[h] for h, hs in enumerate(heads)]
            dps = [_mm(v_ref[rows, hs], doT_ref[hs, :]) for hs in heads]
            for h, hs in enumerate(heads):
                sT = qk[h]
                if masked:
                    sT = jnp.where(keep, sT, NEG)
                pT = jnp.exp(sT - lseT_ref[h:h + 1, :])
                dsT = pT * (dps[h] - dlT_ref[h:h + 1, :])
                dcum_ref[rows, h:h + 1] += -jnp.sum(dsT, axis=1, keepdims=True)
                dsb = dsT.astype(MXU)
                dv_ref[hs, rows] += _mm_nt(doT_ref[hs, :], pT.astype(MXU))
                dk_ref[hs, rows] += _mm_nt(qT_ref[hs, :], dsb)
                dq_s[hs, :] += _mm(kT_ref[hs, rows], dsb)

        def unmasked(j, c):
            kblock(j, False)
            return c

        lax.fori_loop(0, kpq * i, unmasked, 0)
        for d in range(kpq):
            kblock(kpq * i + d, True)
        dq_ref[...] = (dq_s[...] * SCALE).astype(dq_ref.dtype)

    full = lambda a, b: pl.BlockSpec((a, b), lambda i: (0, 0))
    colblk = lambda r: pl.BlockSpec((r, tq), lambda i: (0, i))
    return _pallas_call(
        body, name=name, grid=(T // tq,),
        in_specs=[colblk(D_ATTN), full(T, D_ATTN), full(D_ATTN, T), full(T, D_ATTN), colblk(D_ATTN),
                  colblk(N_HEADS), colblk(N_HEADS), full(T, LANES), colblk(N_HEADS)] + [_UNREAD] * len(after),
        out_specs=[colblk(D_ATTN), full(D_ATTN, T), full(D_ATTN, T), full(T, LANES)],
        out_shape=[
            jax.ShapeDtypeStruct((D_ATTN, T), MXU),
            jax.ShapeDtypeStruct((D_ATTN, T), F32),
            jax.ShapeDtypeStruct((D_ATTN, T), F32),
            jax.ShapeDtypeStruct((T, LANES), F32),
        ],
        scratch_shapes=[pltpu.VMEM((D_ATTN, tq), F32)],
        compiler_params=_params(48, 1),
    )(qT, k, kT, v, doT, lseT, deltaT, cum, cumT, *after)


_ROWS_PER_CHUNK = 64


def _glu_halo(ag_ref, agh_ref, uext_s, tm, first):
    a = ag_ref[:, :D_CONV]
    sg = jax.nn.sigmoid(ag_ref[:, D_CONV:])
    uh = agh_ref[:, :D_CONV] * jax.nn.sigmoid(agh_ref[:, D_CONV:])
    uext_s[0:CONV_HALO, :] = jnp.where(first, 0.0, uh)
    uext_s[CONV_HALO:CONV_HALO + tm, :] = a * sg
    return a, sg


_SUBLANES = 8


def _shifted_copies(ext_s, sh_s, rows):
    for k in range(1, _SUBLANES):
        sh_s[k, 0:rows, :] = ext_s[pl.ds(k, rows), :]


def _window(ext_s, sh_s, start, rows):
    k = start % _SUBLANES
    if k == 0:
        return ext_s[pl.ds(start, rows), :]
    return sh_s[k, pl.ds(start - k, rows), :]


def _layer_norm_stats(y):
    mu = jnp.mean(y, axis=-1, keepdims=True)
    xc = y - mu
    rs = lax.rsqrt(jnp.mean(xc * xc, axis=-1, keepdims=True) + EPS)
    return xc * rs, rs


def _conv_fwd(ag, w32, cb, lg, lb, name):
    T = ag.shape[0]
    tm = _tile(T, 256)
    rc = _tile(tm, _ROWS_PER_CHUNK)
    hb = tm // CONV_HALO

    def body(ag_ref, agh_ref, w_ref, cb_ref, lg_ref, lb_ref, yc_ref, c_ref, uext_s, ush_s):
        i = pl.program_id(0)
        _glu_halo(ag_ref, agh_ref, uext_s, tm, i == 0)
        _shifted_copies(uext_s, ush_s, tm + CONV_HALO - _SUBLANES)
        for r0 in range(0, tm, rc):
            acc = jnp.zeros((rc, D_CONV), F32)
            for t in range(CONV_TAPS):
                acc = acc + _window(uext_s, ush_s, r0 + CONV_HALO - (CONV_TAPS - 1) + t, rc) * w_ref[t:t + 1, :]
            y = acc + cb_ref[...]
            yc_ref[r0:r0 + rc, :] = y
            n, _ = _layer_norm_stats(y)
            z = n * lg_ref[...] + lb_ref[...]
            c_ref[r0:r0 + rc, :] = z * jax.nn.sigmoid(z)

    row = lambda w: pl.BlockSpec((tm, w), lambda i: (i, 0))
    full = lambda a, b: pl.BlockSpec((a, b), lambda i: (0, 0))
    return _pallas_call(
        body, name=name, grid=(T // tm,),
        in_specs=[row(2 * D_CONV),
                  pl.BlockSpec((CONV_HALO, 2 * D_CONV), lambda i: (jnp.maximum(i * hb - 1, 0), 0)),
                  full(CONV_HALO, D_CONV), full(1, D_CONV), full(1, D_CONV), full(1, D_CONV)],
        out_specs=[row(D_CONV), row(D_CONV)],
        out_shape=[jax.ShapeDtypeStruct((T, D_CONV), F32), jax.ShapeDtypeStruct((T, D_CONV), F32)],
        scratch_shapes=[pltpu.VMEM((CONV_HALO + tm, D_CONV), F32),
                        pltpu.VMEM((_SUBLANES, CONV_HALO + tm, D_CONV), F32)],
        compiler_params=_params(32, 1),
    )(ag, ag, w32, cb, lg, lb)


def _conv_bwd(dc, yc, ag, w32, lg, lb, name):
    T = ag.shape[0]
    tm = _tile(T, 256)
    rc = _tile(tm, _ROWS_PER_CHUNK)
    I = T // tm
    hb = tm // CONV_HALO
    n_halo_blocks = T // CONV_HALO

    def body(dc_ref, yc_ref, dch_ref, ych_ref, ag_ref, agh_ref, w_ref, lg_ref, lb_ref,
             dag_ref, dw_ref, dcb_ref, dlg_ref, dlb_ref, uext_s, dext_s, ush_s, dsh_s):
        i = pl.program_id(0)
        lgv = lg_ref[...]
        lbv = lb_ref[...]

        def ln_bwd(dcv, ycv):
            n, rs = _layer_norm_stats(ycv)
            z = n * lgv + lbv
            dz = dcv * _silu_grad(z, jax.nn.sigmoid(z))
            dn = dz * lgv
            dy = rs * (dn - jnp.mean(dn, axis=-1, keepdims=True) - n * jnp.mean(dn * n, axis=-1, keepdims=True))
            return dy, dz, n

        dy, dz, n = ln_bwd(dc_ref[...], yc_ref[...])
        dyh, _, _ = ln_bwd(dch_ref[...], ych_ref[...])
        dext_s[0:tm, :] = dy
        dext_s[tm:tm + CONV_HALO, :] = jnp.where(i == I - 1, 0.0, dyh)
        a, sg = _glu_halo(ag_ref, agh_ref, uext_s, tm, i == 0)
        _shifted_copies(uext_s, ush_s, tm + CONV_HALO - _SUBLANES)
        _shifted_copies(dext_s, dsh_s, tm + CONV_HALO - _SUBLANES)

        @pl.when(i == 0)
        def _():
            dw_ref[...] = jnp.zeros_like(dw_ref)
            dcb_ref[...] = jnp.zeros_like(dcb_ref)
            dlg_ref[...] = jnp.zeros_like(dlg_ref)
            dlb_ref[...] = jnp.zeros_like(dlb_ref)

        dcb_ref[...] += jnp.sum(dy, axis=0, keepdims=True)
        dlg_ref[...] += jnp.sum(dz * n, axis=0, keepdims=True)
        dlb_ref[...] += jnp.sum(dz, axis=0, keepdims=True)
        for t in range(CONV_TAPS):
            u_t = _window(uext_s, ush_s, CONV_HALO - (CONV_TAPS - 1) + t, tm)
            dw_ref[t:t + 1, :] += jnp.sum(dy * u_t, axis=0, keepdims=True)
        for r0 in range(0, tm, rc):
            acc = jnp.zeros((rc, D_CONV), F32)
            for t in range(CONV_TAPS):
                acc = acc + _window(dext_s, dsh_s, r0 + (CONV_TAPS - 1) - t, rc) * w_ref[t:t + 1, :]
            a_c = a[r0:r0 + rc, :]
            sg_c = sg[r0:r0 + rc, :]
            dag_ref[r0:r0 + rc, :D_CONV] = (acc * sg_c).astype(dag_ref.dtype)
            dag_ref[r0:r0 + rc, D_CONV:] = (acc * a_c * sg_c * (1.0 - sg_c)).astype(dag_ref.dtype)

    row = lambda w: pl.BlockSpec((tm, w), lambda i: (i, 0))
    full = lambda a, b: pl.BlockSpec((a, b), lambda i: (0, 0))
    nxt = pl.BlockSpec((CONV_HALO, D_CONV), lambda i: (jnp.minimum((i + 1) * hb, n_halo_blocks - 1), 0))
    return _pallas_call(
        body, name=name, grid=(I,),
        in_specs=[row(D_CONV), row(D_CONV), nxt, nxt, row(2 * D_CONV),
                  pl.BlockSpec((CONV_HALO, 2 * D_CONV), lambda i: (jnp.maximum(i * hb - 1, 0), 0)),
                  full(CONV_HALO, D_CONV), full(1, D_CONV), full(1, D_CONV)],
        out_specs=[row(2 * D_CONV), full(CONV_HALO, D_CONV), full(1, D_CONV), full(1, D_CONV), full(1, D_CONV)],
        out_shape=[
            jax.ShapeDtypeStruct((T, 2 * D_CONV), MXU),
            jax.ShapeDtypeStruct((CONV_HALO, D_CONV), F32),
            jax.ShapeDtypeStruct((1, D_CONV), F32),
            jax.ShapeDtypeStruct((1, D_CONV), F32),
            jax.ShapeDtypeStruct((1, D_CONV), F32),
        ],
        scratch_shapes=[pltpu.VMEM((CONV_HALO + tm, D_CONV), F32), pltpu.VMEM((tm + CONV_HALO, D_CONV), F32),
                        pltpu.VMEM((_SUBLANES, CONV_HALO + tm, D_CONV), F32),
                        pltpu.VMEM((_SUBLANES, CONV_HALO + tm, D_CONV), F32)],
        compiler_params=_params(40, 1),
    )(dc, yc, dc, yc, ag, ag, w32, lg, lb)


def _outproj_fwd(x1, c, o, gc, ga, wout, name):
    T, D = x1.shape
    tm = _tile(T, 512)

    def body(x_ref, c_ref, o_ref, gc_ref, ga_ref, w_ref, x2_ref):
        yc, _ = _rms_fwd(c_ref[...], gc_ref[...])
        ya, _ = _rms_fwd(o_ref[...], ga_ref[...])
        x2_ref[...] = (x_ref[...] + _mm(yc.astype(MXU), w_ref[:D_CONV, :])
                       + _mm(ya.astype(MXU), w_ref[D_CONV:, :]))

    row = lambda w: pl.BlockSpec((tm, w), lambda i: (i, 0))
    full = lambda a, b: pl.BlockSpec((a, b), lambda i: (0, 0))
    return _pallas_call(
        body, name=name, grid=(T // tm,),
        in_specs=[row(D), row(D_CONV), row(D_ATTN), full(1, D_CONV), full(1, D_ATTN), full(D_CONV + D_ATTN, D)],
        out_specs=row(D),
        out_shape=jax.ShapeDtypeStruct((T, D), F32),
        compiler_params=_params(32, 1),
    )(x1, c, o, gc, ga, wout)


def _outproj_bwd(dx2, c, o, gc, ga, wout, name):
    T, D = dx2.shape
    tm = _tile(T, 256)
    I = T // tm

    def body(dx_ref, c_ref, o_ref, gc_ref, ga_ref, w_ref,
             dc_ref, doT_ref, dlT_ref, dw_ref, dgc_ref, dga_ref, acc_s):
        i = pl.program_id(0)
        dxb = dx_ref[...].astype(MXU)
        cv = c_ref[...]
        ov = o_ref[...]
        yc, rcn = _rms_fwd(cv, gc_ref[...])
        ya, ra = _rms_fwd(ov, ga_ref[...])
        dyc = _mm_nt(dxb, w_ref[:D_CONV, :])
        dya = _mm_nt(dxb, w_ref[D_CONV:, :])
        dwc = _mm_tn(yc.astype(MXU), dxb)
        dwa = _mm_tn(ya.astype(MXU), dxb)
        dcv, dgc = _rms_bwd(cv, rcn, gc_ref[...], dyc)
        dov, dga = _rms_bwd(ov, ra, ga_ref[...], dya)
        dc_ref[...] = dcv
        dob = dov.astype(doT_ref.dtype)
        doT_ref[...] = dov.T.astype(doT_ref.dtype)
        chan = lax.broadcasted_iota(jnp.int32, (D_ATTN, LANES), 0)
        head = lax.broadcasted_iota(jnp.int32, (D_ATTN, LANES), 1)
        in_head = ((chan >= head * HEAD_DIM) & (chan < (head + 1) * HEAD_DIM)).astype(jnp.bfloat16)
        dlT_ref[...] = _exact_dot_01(dob.astype(F32) * ov, in_head).T[:N_HEADS, :]

        @pl.when(i == 0)
        def _():
            acc_s[:D_CONV, :] = dwc
            acc_s[D_CONV:, :] = dwa
            dgc_ref[...] = dgc
            dga_ref[...] = dga

        @pl.when(i > 0)
        def _():
            acc_s[:D_CONV, :] += dwc
            acc_s[D_CONV:, :] += dwa
            dgc_ref[...] += dgc
            dga_ref[...] += dga

        @pl.when(i == I - 1)
        def _():
            dw_ref[...] = acc_s[...].astype(dw_ref.dtype)

    row = lambda w: pl.BlockSpec((tm, w), lambda i: (i, 0))
    full = lambda a, b: pl.BlockSpec((a, b), lambda i: (0, 0))
    return _pallas_call(
        body, name=name, grid=(I,),
        in_specs=[row(D), row(D_CONV), row(D_ATTN), full(1, D_CONV), full(1, D_ATTN), full(D_CONV + D_ATTN, D)],
        out_specs=[row(D_CONV), pl.BlockSpec((D_ATTN, tm), lambda i: (0, i)),
                   pl.BlockSpec((N_HEADS, tm), lambda i: (0, i)),
                   full(D_CONV + D_ATTN, D), full(1, D_CONV), full(1, D_ATTN)],
        out_shape=[
            jax.ShapeDtypeStruct((T, D_CONV), F32),
            jax.ShapeDtypeStruct((D_ATTN, T), MXU),
            jax.ShapeDtypeStruct((N_HEADS, T), F32),
            jax.ShapeDtypeStruct((D_CONV + D_ATTN, D), MXU),
            jax.ShapeDtypeStruct((1, D_CONV), F32),
            jax.ShapeDtypeStruct((1, D_ATTN), F32),
        ],
        scratch_shapes=[pltpu.VMEM((D_CONV + D_ATTN, D), F32)],
        compiler_params=_params(40, 1),
    )(dx2, c, o, gc, ga, wout)


def _ffn_down_loss(x, A, w2, gf, target, name):
    T, D = x.shape
    J, _, bf = A.shape
    tm = _tile(T, 512)

    def body(x_ref, A_ref, w2_ref, g_ref, t_ref, loss_ref, dx_ref, dg_ref):
        i = pl.program_id(0)
        f = _mm(A_ref[0], w2_ref[0:bf, :])
        for j in range(1, J):
            f = f + _mm(A_ref[j], w2_ref[j * bf:(j + 1) * bf, :])
        xv = x_ref[...] + 0.5 * f
        gv = g_ref[...]
        out, r = _rms_fwd(xv, gv)
        err = out - t_ref[...]
        part = jnp.full((1, LANES), 0.5 / D, F32) * jnp.sum(err * err)
        dxn, dgp = _rms_bwd(xv, r, gv, err * (1.0 / D))
        dx_ref[...] = dxn

        @pl.when(i == 0)
        def _():
            loss_ref[...] = part
            dg_ref[...] = dgp

        @pl.when(i > 0)
        def _():
            loss_ref[...] += part
            dg_ref[...] += dgp

    row = lambda w: pl.BlockSpec((tm, w), lambda i: (i, 0))
    full = lambda a, b: pl.BlockSpec((a, b), lambda i: (0, 0))
    return _pallas_call(
        body, name=name, grid=(T // tm,),
        in_specs=[row(D), pl.BlockSpec((J, tm, bf), lambda i: (0, i, 0)), full(J * bf, D), full(1, D), row(D)],
        out_specs=[full(1, LANES), row(D), full(1, D)],
        out_shape=[jax.ShapeDtypeStruct((1, LANES), F32), jax.ShapeDtypeStruct((T, D), F32),
                   jax.ShapeDtypeStruct((1, D), F32)],
        compiler_params=_params(56, 1),
    )(x, A, w2, gf, target)


def _row_tile(rows):
    for cand in (256, 176, 128, 64, 32, 16):
        if rows % cand == 0:
            return cand
    return rows


def _adamw(w, m, v, parts, name):
    R, C = w.shape
    P = parts.shape[0]
    tr = _row_tile(R)
    c1 = 1.0 - ADAM_B1 ** ADAM_STEP
    c2 = 1.0 - ADAM_B2 ** ADAM_STEP

    def body(w_ref, m_ref, v_ref, p_ref, g_ref, d_ref, nm_ref, nv_ref):
        g = p_ref[0].astype(F32)
        for s in range(1, P):
            g = g + p_ref[s].astype(F32)
        wv = w_ref[...]
        mn = ADAM_B1 * m_ref[...] + (1.0 - ADAM_B1) * g
        vn = ADAM_B2 * v_ref[...] + (1.0 - ADAM_B2) * (g * g)
        g_ref[...] = g
        nm_ref[...] = mn
        nv_ref[...] = vn
        d_ref[...] = -ADAM_LR * ((mn / c1) / (jnp.sqrt(vn / c2) + ADAM_EPS) + ADAM_WD * wv)

    blk = pl.BlockSpec((tr, C), lambda i: (i, 0))
    out = jax.ShapeDtypeStruct((R, C), F32)
    return _pallas_call(
        body, name=name, grid=(R // tr,),
        in_specs=[blk, blk, blk, pl.BlockSpec((P, tr, C), lambda i: (0, i, 0))],
        out_specs=[blk, blk, blk, blk],
        out_shape=[out, out, out, out],
        compiler_params=_params(32, 1),
    )(w, m, v, parts)


def _position():
    return lax.axis_index("x"), lax.axis_index("y"), lax.axis_index("c")


def _flat(px, py, pc):
    return 4 * px + 2 * py + pc


def _row_halves(rows, dtype):
    tile = _SUBLANES * (4 // jnp.dtype(dtype).itemsize)
    half = rows // 2 // tile * tile
    assert half > 0
    return (0, half), (half, rows - half)


def _gather_body(ins, outs, send_sems, recv_sems, local_sems, handshake):
    n = len(ins)
    x, y, c = _position()
    me, sibling = (x, y, c), (x, y, 1 - c)
    across_x, across_y, diagonal = (1 - x, y), (x, 1 - y), (1 - x, 1 - y)
    if handshake:
        _handshake([sibling] + [(*chip, cc) for chip in (across_x, across_y, diagonal) for cc in (c, 1 - c)])

    def copy(a, k, block, to, rows=None, src=None):
        dst = outs[a].at[_flat(*block)]
        if rows is not None:
            dst = dst.at[pl.ds(*rows)]
        return pltpu.make_async_remote_copy(
            src_ref=dst if src is None else src, dst_ref=dst,
            send_sem=send_sems.at[a, k], recv_sem=recv_sems.at[a, k],
            device_id=to, device_id_type=MESH)

    halves = [_row_halves(ins[a].shape[0], ins[a].dtype) for a in range(n)]
    mine = [pltpu.make_async_copy(ins[a], outs[a].at[_flat(*me)], local_sems.at[a]) for a in range(n)]
    for cp in mine:
        cp.start()
    sent = []

    def start(cp):
        cp.start()
        sent.append(cp)

    for a in range(n):
        start(copy(a, 0, me, sibling, src=ins[a]))
        start(copy(a, 1, me, (*across_x, c), src=ins[a]))
        start(copy(a, 2, me, (*across_y, c), src=ins[a]))
    for a in range(n):
        low, high = halves[a]
        copy(a, 1, (*across_x, c), me).wait_recv()
        start(copy(a, 3, (*across_x, c), (*across_y, c), rows=low))
        start(copy(a, 5, (*across_x, c), sibling))
        copy(a, 2, (*across_y, c), me).wait_recv()
        start(copy(a, 4, (*across_y, c), (*across_x, c), rows=high))
        start(copy(a, 6, (*across_y, c), sibling))
    for a in range(n):
        low, high = halves[a]
        copy(a, 3, (*diagonal, c), me, rows=low).wait_recv()
        start(copy(a, 7, (*diagonal, c), sibling, rows=low))
        copy(a, 4, (*diagonal, c), me, rows=high).wait_recv()
        start(copy(a, 8, (*diagonal, c), sibling, rows=high))
    for a in range(n):
        low, high = halves[a]
        copy(a, 0, sibling, me).wait_recv()
        copy(a, 5, (*across_x, 1 - c), me).wait_recv()
        copy(a, 6, (*across_y, 1 - c), me).wait_recv()
        copy(a, 7, (*diagonal, 1 - c), me, rows=low).wait_recv()
        copy(a, 8, (*diagonal, 1 - c), me, rows=high).wait_recv()
    for cp in sent:
        cp.wait_send()
    for cp in mine:
        cp.wait()


_GATHER_SLOTS = 9


def _gather_scratch(n):
    return [pltpu.SemaphoreType.DMA((n, _GATHER_SLOTS)), pltpu.SemaphoreType.DMA((n, _GATHER_SLOTS)),
            pltpu.SemaphoreType.DMA((n,))]


def _all_gather(shards, name):
    n = len(shards)

    def body(*refs):
        _gather_body(refs[:n], refs[n:2 * n], *refs[2 * n:], handshake=False)

    hbm = pl.BlockSpec(memory_space=pltpu.HBM)
    return _pallas_call(
        body, name=name,
        in_specs=[hbm] * n, out_specs=[hbm] * n,
        out_shape=[jax.ShapeDtypeStruct((N_DEV,) + s.shape, s.dtype) for s in shards],
        scratch_shapes=_gather_scratch(n),
    )(*shards)


def _handshake(peers):
    barrier = pltpu.get_barrier_semaphore()
    for peer in peers:
        pl.semaphore_signal(barrier, inc=1, device_id=peer, device_id_type=MESH)
    pl.semaphore_wait(barrier, len(peers))


def _sequencer_call(body, name, collective_id, out_type, scratch_types, operands):
    return pl.kernel(
        body, name=name, out_type=out_type,
        mesh=plsc.ScalarSubcoreMesh(axis_name="sequencer", num_cores=1),
        scratch_types=scratch_types,
        compiler_params=pltpu.CompilerParams(collective_id=collective_id),
    )(*operands)


def _seq_all_gather(shards, name, collective_id, after):
    n = len(shards)

    def body(*refs):
        _gather_body(refs[:n], refs[n + 1:2 * n + 1], *refs[2 * n + 1:], handshake=True)

    return _sequencer_call(
        body, name, collective_id,
        [jax.ShapeDtypeStruct((N_DEV,) + s.shape, s.dtype) for s in shards],
        _gather_scratch(n), list(shards) + [after])


def _seq_to_sibling(parts, name, collective_id, after):
    n = len(parts)

    def body(*refs):
        ins, outs = refs[:n], refs[n + len(after):2 * n + len(after)]
        send_sems, recv_sems = refs[2 * n + len(after):]
        x, y, c = _position()
        sibling = (x, y, 1 - c)
        _handshake([sibling])
        sent = []
        for a in range(n):
            for q in range(N_CHIPS):
                cp = pltpu.make_async_remote_copy(
                    src_ref=ins[a].at[2 * q + (1 - c)], dst_ref=outs[a].at[q],
                    send_sem=send_sems.at[a, q], recv_sem=recv_sems.at[a, q],
                    device_id=sibling, device_id_type=MESH)
                cp.start()
                sent.append(cp)
        for cp in sent:
            cp.wait_recv()
        for cp in sent:
            cp.wait_send()

    return _sequencer_call(
        body, name, collective_id,
        [jax.ShapeDtypeStruct((N_CHIPS,) + p.shape[1:], p.dtype) for p in parts],
        [pltpu.SemaphoreType.DMA((n, N_CHIPS)), pltpu.SemaphoreType.DMA((n, N_CHIPS))],
        list(parts) + list(after))


def _seq_to_chips(partials, name, collective_id):
    n = len(partials)

    def body(*refs):
        ins, outs = refs[:n], refs[n:2 * n]
        send_sems, recv_sems, local_sems = refs[2 * n:]
        x, y, c = _position()
        my_chip = 2 * x + y
        chips = [(1 - x, y), (x, 1 - y), (1 - x, 1 - y)]
        _handshake([(*chip, c) for chip in chips])
        mine = [pltpu.make_async_copy(ins[a].at[my_chip], outs[a].at[my_chip], local_sems.at[a]) for a in range(n)]
        for cp in mine:
            cp.start()
        sent = []
        for a in range(n):
            for j, (px, py) in enumerate(chips):
                cp = pltpu.make_async_remote_copy(
                    src_ref=ins[a].at[2 * px + py], dst_ref=outs[a].at[my_chip],
                    send_sem=send_sems.at[a, j], recv_sem=recv_sems.at[a, j],
                    device_id=(px, py, c), device_id_type=MESH)
                cp.start()
                sent.append(cp)
        for cp in sent:
            cp.wait_recv()
        for cp in sent:
            cp.wait_send()
        for cp in mine:
            cp.wait()

    return _sequencer_call(
        body, name, collective_id,
        [jax.ShapeDtypeStruct(p.shape, p.dtype) for p in partials],
        [pltpu.SemaphoreType.DMA((n, 3)), pltpu.SemaphoreType.DMA((n, 3)), pltpu.SemaphoreType.DMA((n,))],
        list(partials))


def _pair_add(parts, recvs, name, after=()):
    n = len(parts)
    core = lax.axis_index("c").astype(jnp.int32).reshape(1)

    def body(c_ref, *refs):
        ps, rs, outs = refs[:n], refs[n:2 * n], refs[2 * n + len(after):]
        for p_ref, r_ref, o_ref in zip(ps, rs, outs):
            o_ref[...] = (p_ref[...].astype(F32) + r_ref[...].astype(F32)).astype(o_ref.dtype)

    mine = lambda p: pl.BlockSpec((None,) + p.shape[1:], lambda q, c: (2 * q + c[0], 0, 0))
    blk = lambda p: pl.BlockSpec((None,) + p.shape[1:], lambda q, c: (q, 0, 0))
    return pl.pallas_call(
        body, name=name,
        grid_spec=pltpu.PrefetchScalarGridSpec(
            num_scalar_prefetch=1, grid=(N_CHIPS,),
            in_specs=[mine(p) for p in parts] + [blk(p) for p in parts] + [_UNREAD] * len(after),
            out_specs=[blk(p) for p in parts]),
        out_shape=[pltpu.HBM((N_CHIPS,) + p.shape[1:], p.dtype) for p in parts],
        compiler_params=_params(40, 1),
    )(core, *[pltpu.with_memory_space_constraint(a, pltpu.HBM) for a in (*parts, *recvs, *after)])


class _Reduced(NamedTuple):
    partials: list
    reduced: list


def _blocks(g):
    return g.reshape(N_DEV, -1, g.shape[-1])


def _reduce_scatter(parts, tag, ids, after=(), add_after=()):
    from_sibling = _seq_to_sibling(parts, "rs_sibling_" + tag, ids[0], after)
    partials = _pair_add(parts, from_sibling, "rs_add_" + tag, add_after)
    return _Reduced(partials, _seq_to_chips(partials, "rs_chips_" + tag, ids[1]))


_SMALL = ("ffn1_norm", "mix_norm", "conv_b", "conv_ln_g", "conv_ln_b", "forget_b", "out_norm_conv",
          "out_norm_attn", "ffn2_norm", "final_norm")
_PACK_WIDTH = 2 * D_CONV
_SLOT = dict(ffn1_norm=(0, 0), mix_norm=(1, 0), ffn2_norm=(2, 0), final_norm=(3, 0), conv_b=(4, 0),
             conv_ln_g=(4, D_CONV), conv_ln_b=(5, 0), out_norm_conv=(5, D_CONV), out_norm_attn=(6, 0),
             forget_b=(6, D_CONV))
_LOSS_ROW = 7
_CONV_ROW0 = 8
_PACK_ROWS = _CONV_ROW0 + CONV_HALO


def _pack_small(small, name):
    arrays = [small[n] for n in _SMALL] + [small["conv_w"], small["loss"]]

    def body(*refs):
        out = refs[-1]
        out[...] = jnp.zeros_like(out)
        for n, ref in zip(_SMALL, refs):
            row, lane = _SLOT[n]
            out[row:row + 1, lane:lane + ref.shape[1]] = ref[...]
        out[_CONV_ROW0:, :D_CONV] = refs[len(_SMALL)][...]
        out[_LOSS_ROW:_LOSS_ROW + 1, :LANES] = refs[len(_SMALL) + 1][...]

    return _pallas_call(body, name=name, out_shape=jax.ShapeDtypeStruct((_PACK_ROWS, _PACK_WIDTH), F32))(*arrays)


def _adamw_small(gathered, w, m, v, name):
    c1 = 1.0 - ADAM_B1 ** ADAM_STEP
    c2 = 1.0 - ADAM_B2 ** ADAM_STEP
    k = len(_SMALL)

    def body(g_ref, *refs):
        ws, ms, vs = refs[:k], refs[k:2 * k], refs[2 * k:3 * k]
        outs = refs[3 * k:]
        total = g_ref[0]
        for s in range(1, N_DEV):
            total = total + g_ref[s]
        for i, n in enumerate(_SMALL):
            row, lane = _SLOT[n]
            width = ws[i].shape[1]
            g = total[row:row + 1, lane:lane + width]
            mn = ADAM_B1 * ms[i][...] + (1.0 - ADAM_B1) * g
            vn = ADAM_B2 * vs[i][...] + (1.0 - ADAM_B2) * (g * g)
            o_g, o_d, o_m, o_v = outs[4 * i:4 * i + 4]
            o_g[...] = g
            o_m[...] = mn
            o_v[...] = vn
            o_d[...] = -ADAM_LR * ((mn / c1) / (jnp.sqrt(vn / c2) + ADAM_EPS) + ADAM_WD * ws[i][...])
        outs[4 * k][...] = total[_CONV_ROW0:, :D_CONV]
        outs[4 * k + 1][...] = total[_LOSS_ROW:_LOSS_ROW + 1, :LANES]

    shapes = []
    for n in _SMALL:
        shapes += [jax.ShapeDtypeStruct(w[n].shape, F32)] * 4
    shapes.append(jax.ShapeDtypeStruct((CONV_HALO, D_CONV), F32))
    shapes.append(jax.ShapeDtypeStruct((1, LANES), F32))
    res = _pallas_call(body, name=name, out_shape=shapes)(
        gathered, *[w[n] for n in _SMALL], *[m[n] for n in _SMALL], *[v[n] for n in _SMALL])
    return {n: res[4 * i:4 * i + 4] for i, n in enumerate(_SMALL)}, res[4 * k], res[4 * k + 1]


def _local_step(x, target, norms, shard):
    D = x.shape[1]
    J = N_DEV // 2
    as13 = lambda g: g.reshape(2, J, g.shape[1], D)

    (g13_1,) = _all_gather([shard["ffn1_w13"]], "gather_ffn1_w13")
    (g2_1,) = _seq_all_gather([shard["ffn1_w2"]], "gather_ffn1_w2", 10, after=g13_1)
    w13_1 = as13(g13_1)
    G1, U1, A1 = _ffn_up(x, norms["ffn1_norm"], w13_1, "ffn1_up")
    gin, gconv = _seq_all_gather([shard["w_in"], shard["conv_w"]], "gather_mix", 1, after=G1)
    w2_1 = g2_1.reshape(-1, D)
    x1 = _ffn_down(x, A1, w2_1, "ffn1_down")
    gout, g13_2, g2_2 = _seq_all_gather([shard["w_out"], shard["ffn2_w13"], shard["ffn2_w2"]], "gather_ffn2", 2,
                                        after=x1)
    wout = gout.reshape(-1, D)
    conv_w32 = jnp.pad(gconv.transpose(1, 0, 2).reshape(CONV_TAPS, D_CONV), ((0, CONV_HALO - CONV_TAPS), (0, 0)))

    ag, k, v, qT, kT, vT, fl = _inproj_fwd(x1, norms["mix_norm"], gin, "inproj_fwd")
    cum, cumT = _forget_fwd(fl, norms["forget_b"], "forget_fwd")
    yc, c = _conv_fwd(ag, conv_w32, norms["conv_b"], norms["conv_ln_g"], norms["conv_ln_b"], "conv_fwd")
    o, lseT = _attn_fwd(qT, k, vT, cum, cumT, "attn_fwd")
    x2 = _outproj_fwd(x1, c, o, norms["out_norm_conv"], norms["out_norm_attn"], wout, "outproj_fwd")
    w13_2, w2_2 = as13(g13_2), g2_2.reshape(-1, D)
    G2, U2, A2 = _ffn_up(x2, norms["ffn2_norm"], w13_2, "ffn2_up")
    loss, dx3, d_final = _ffn_down_loss(x2, A2, w2_2, norms["final_norm"], target, "ffn2_down_loss")

    dw2_2 = _ffn_w2_grad(dx3, A2, "ffn2_w2_grad")
    dx2, d_ffn2n, h3, dG2, dU2 = _ffn_bwd_act(x2, norms["ffn2_norm"], dx3, G2, U2, w13_2, w2_2, "ffn2_bwd_act")
    dw13_2 = _ffn_w13_grad(h3, dG2, dU2, "ffn2_w13_grad")
    dc, dobT, deltaT, dwout, d_onc, d_ona = _outproj_bwd(
        dx2, c, o, norms["out_norm_conv"], norms["out_norm_attn"], wout, "outproj_bwd")
    red_ffn2 = _reduce_scatter([_blocks(dw13_2), _blocks(dw2_2)], "ffn2", (3, 4), add_after=(dc,))
    dqT, dkT, dvT, dcum = _attn_bwd(qT, k, kT, v, dobT, lseT, deltaT, cum, cumT, "attn_bwd",
                                    after=red_ffn2.partials)
    dfl, d_fb = _forget_bwd(dcum, fl, norms["forget_b"], "forget_bwd")
    dag, d_convw, d_cb, d_lg, d_lb = _conv_bwd(dc, yc, ag, conv_w32, norms["conv_ln_g"], norms["conv_ln_b"], "conv_bwd")
    dx1, d_mixn, h2 = _inproj_bwd_act(x1, norms["mix_norm"], dx2, dag, dqT, dkT, dvT, dfl, gin, "inproj_bwd_act")
    dw2_1 = _ffn_w2_grad(dx1, A1, "ffn1_w2_grad")
    early = [_blocks(dwout), _blocks(dw2_1)]
    sib_early = _seq_to_sibling(early, "rs_sibling_mix_early", 11, red_ffn2.reduced[:1])
    dwin_blocks = _inproj_bwd_weights(h2, dag, dqT, dkT, dvT, dfl, "inproj_bwd_weights")
    sib_w_in = _seq_to_sibling([dwin_blocks], "rs_sibling_mix", 5, sib_early[:1])
    mix_partials = _pair_add([dwin_blocks] + early, sib_w_in + sib_early, "rs_add_mix")
    red_mix = _Reduced(mix_partials, _seq_to_chips(mix_partials, "rs_chips_mix", 6))
    dx, d_ffn1n, h1, dG1, dU1 = _ffn_bwd_act(x, norms["ffn1_norm"], dx1, G1, U1, w13_1, w2_1, "ffn1_bwd_act",
                                             after=red_mix.partials)
    small = dict(ffn1_norm=d_ffn1n, mix_norm=d_mixn, conv_b=d_cb, conv_ln_g=d_lg, conv_ln_b=d_lb,
                 forget_b=d_fb, out_norm_conv=d_onc, out_norm_attn=d_ona, ffn2_norm=d_ffn2n,
                 final_norm=d_final, conv_w=d_convw, loss=loss)
    packed_small = _pack_small(small, "pack_small_grads")
    (gathered_small,) = _seq_all_gather([packed_small], "gather_small_grads", 9, after=red_mix.reduced[0])
    dw13_1 = _ffn_w13_grad(h1, dG1, dU1, "ffn1_w13_grad")
    red_w13_1 = _reduce_scatter([_blocks(dw13_1)], "ffn1_w13", (7, 8), after=[gathered_small])
    big = dict(ffn1_w13=red_w13_1.reduced[0], ffn1_w2=red_mix.reduced[2], w_in=red_mix.reduced[0],
               w_out=red_mix.reduced[1], ffn2_w13=red_ffn2.reduced[0], ffn2_w2=red_ffn2.reduced[1])
    return dx, gathered_small, big


_BIG = ("ffn1_w13", "ffn1_w2", "w_in", "w_out", "ffn2_w13", "ffn2_w2")
_TRANSPOSED = ("ffn1_w13", "ffn2_w13", "w_in")
_ORDER = ("ffn1_norm", "ffn1_w13", "ffn1_w2", "mix_norm", "w_in", "conv_w", "conv_b", "conv_ln_g", "conv_ln_b",
          "forget_b", "out_norm_conv", "out_norm_attn", "w_out", "ffn2_norm", "ffn2_w13", "ffn2_w2", "final_norm")


def kernel(x, ffn1_norm, ffn1_w13, ffn1_w2, mix_norm, w_in, conv_w, conv_b, conv_ln_g, conv_ln_b, forget_b, out_norm_conv, out_norm_attn, w_out, ffn2_norm, ffn2_w13, ffn2_w2, final_norm, loss_target, m_ffn1_norm, m_ffn1_w13, m_ffn1_w2, m_mix_norm, m_w_in, m_conv_w, m_conv_b, m_conv_ln_g, m_conv_ln_b, m_forget_b, m_out_norm_conv, m_out_norm_attn, m_w_out, m_ffn2_norm, m_ffn2_w13, m_ffn2_w2, m_final_norm, v_ffn1_norm, v_ffn1_w13, v_ffn1_w2, v_mix_norm, v_w_in, v_conv_w, v_conv_b, v_conv_ln_g, v_conv_ln_b, v_forget_b, v_out_norm_conv, v_out_norm_attn, v_w_out, v_ffn2_norm, v_ffn2_w13, v_ffn2_w2, v_final_norm):
    w = dict(ffn1_norm=ffn1_norm, ffn1_w13=ffn1_w13, ffn1_w2=ffn1_w2, mix_norm=mix_norm, w_in=w_in, conv_w=conv_w,
             conv_b=conv_b, conv_ln_g=conv_ln_g, conv_ln_b=conv_ln_b, forget_b=forget_b, out_norm_conv=out_norm_conv,
             out_norm_attn=out_norm_attn, w_out=w_out, ffn2_norm=ffn2_norm, ffn2_w13=ffn2_w13, ffn2_w2=ffn2_w2,
             final_norm=final_norm)
    m = dict(ffn1_norm=m_ffn1_norm, ffn1_w13=m_ffn1_w13, ffn1_w2=m_ffn1_w2, mix_norm=m_mix_norm, w_in=m_w_in,
             conv_w=m_conv_w, conv_b=m_conv_b, conv_ln_g=m_conv_ln_g, conv_ln_b=m_conv_ln_b, forget_b=m_forget_b,
             out_norm_conv=m_out_norm_conv, out_norm_attn=m_out_norm_attn, w_out=m_w_out, ffn2_norm=m_ffn2_norm,
             ffn2_w13=m_ffn2_w13, ffn2_w2=m_ffn2_w2, final_norm=m_final_norm)
    v = dict(ffn1_norm=v_ffn1_norm, ffn1_w13=v_ffn1_w13, ffn1_w2=v_ffn1_w2, mix_norm=v_mix_norm, w_in=v_w_in,
             conv_w=v_conv_w, conv_b=v_conv_b, conv_ln_g=v_conv_ln_g, conv_ln_b=v_conv_ln_b, forget_b=v_forget_b,
             out_norm_conv=v_out_norm_conv, out_norm_attn=v_out_norm_attn, w_out=v_w_out, ffn2_norm=v_ffn2_norm,
             ffn2_w13=v_ffn2_w13, ffn2_w2=v_ffn2_w2, final_norm=v_final_norm)
    shapes = {n: a.shape for n, a in w.items()}
    T, D = x.shape[1], x.shape[2]
    def two(n, a):
        if a.ndim != 3:
            return a.reshape(1, -1)
        a = a.reshape(a.shape[-2], a.shape[-1])
        return a.T if n in _TRANSPOSED else a

    w2d = {n: two(n, a) for n, a in w.items()}
    m2d = {n: two(n, a) for n, a in m.items()}
    v2d = {n: two(n, a) for n, a in v.items()}

    shard = {n: w2d[n].astype(MXU) for n in _BIG}
    shard["conv_w"] = w2d["conv_w"]
    norms = {n: w2d[n] for n in _SMALL}
    norms["forget_b"] = jnp.pad(w2d["forget_b"], ((0, 0), (0, LANES - N_HEADS)))
    dx, gathered_small, big = _local_step(x[0], loss_target[0], norms, shard)

    grads, deltas, new_m, new_v = {}, {}, {}, {}
    for n in _BIG:
        g, d, nm, nv = _adamw(w2d[n], m2d[n], v2d[n], big[n], "adamw_" + n)
        grads[n], deltas[n], new_m[n], new_v[n] = g, d, nm, nv

    small_out, conv_g_full, loss = _adamw_small(gathered_small, w2d, m2d, v2d, "adamw_small")
    for n in _SMALL:
        grads[n], deltas[n], new_m[n], new_v[n] = small_out[n]
    conv_g_full = conv_g_full[:CONV_TAPS]
    xi, yi, ci = _position()
    cw = shapes["conv_w"][-1]
    conv_g_mine = lax.dynamic_slice_in_dim(conv_g_full, _flat(xi, yi, ci) * cw, cw, axis=1)
    g, d, nm, nv = _adamw(w2d["conv_w"], m2d["conv_w"], v2d["conv_w"], conv_g_mine[None], "adamw_conv_w")
    grads["conv_w"], deltas["conv_w"], new_m["conv_w"], new_v["conv_w"] = g, d, nm, nv

    shaped = lambda dct: [(dct[n].T if n in _TRANSPOSED else dct[n]).reshape(shapes[n]) for n in _ORDER]
    return (loss[0, 0], dx[None], *shaped(grads), *shaped(deltas), *shaped(new_m), *shaped(new_v))
```

```python
from typing import NamedTuple

import jax
import jax.numpy as jnp
from jax import lax
from jax.experimental import pallas as pl
from jax.experimental.pallas import tpu as pltpu
from jax.experimental.pallas import tpu_sc as plsc

F32 = jnp.float32
MXU = jnp.bfloat16
EPS = 1e-6
N_HEADS = 8
HEAD_DIM = 64
D_CONV = 512
D_ATTN = N_HEADS * HEAD_DIM
CONV_TAPS = 31
CONV_HALO = 32
SCALE = HEAD_DIM ** -0.5
NEG = -1e30
LANES = 128
N_DEV = 8
N_CHIPS = N_DEV // 2
MESH = pl.DeviceIdType.MESH
MIB = 1 << 20

ADAM_LR = 0.001
ADAM_B1 = 0.9
ADAM_B2 = 0.999
ADAM_EPS = 1e-08
ADAM_WD = 0.01
ADAM_STEP = 10


_UNREAD = pl.BlockSpec(memory_space=pl.ANY)


def _pallas_call(body, *, out_shape, **kwargs):
    in_hbm = lambda s: pltpu.HBM(s.shape, s.dtype)
    outs = [in_hbm(s) for s in out_shape] if isinstance(out_shape, (list, tuple)) else in_hbm(out_shape)
    call = pl.pallas_call(body, out_shape=outs, **kwargs)
    return lambda *operands: call(*[pltpu.with_memory_space_constraint(a, pltpu.HBM) for a in operands])


def _params(vmem_mib, n_axes):
    return pltpu.CompilerParams(dimension_semantics=("arbitrary",) * n_axes, vmem_limit_bytes=vmem_mib * MIB)


def _mm(a, b):
    return jnp.dot(a, b, preferred_element_type=F32)


def _mm_nt(a, b):
    return lax.dot_general(a, b, (((1,), (1,)), ((), ())), preferred_element_type=F32)


def _mm_tn(a, b):
    return lax.dot_general(a, b, (((0,), (0,)), ((), ())), preferred_element_type=F32)


def _rms_fwd(x, g):
    r = lax.rsqrt(jnp.mean(x * x, axis=-1, keepdims=True) + EPS)
    return x * r * g, r


def _rms_bwd(x, r, g, dy):
    gdy = dy * g
    dx = r * gdy - x * (r * r * r) * jnp.mean(x * gdy, axis=-1, keepdims=True)
    dg = jnp.sum(dy * x * r, axis=0, keepdims=True)
    return dx, dg


def _silu_grad(z, sz):
    return sz * (1.0 + z * (1.0 - sz))


def _three_terms(x):
    x1 = x.astype(jnp.bfloat16)
    r1 = x - x1.astype(F32)
    x2 = r1.astype(jnp.bfloat16)
    x3 = (r1 - x2.astype(F32)).astype(jnp.bfloat16)
    return x1, x2, x3


def _exact_tri_dot(tri, x):
    x1, x2, x3 = _three_terms(x)
    return _mm(tri, x1) + _mm(tri, x2) + _mm(tri, x3)


def _exact_dot_01(x, sel):
    x1, x2, x3 = _three_terms(x)
    return _mm(x1, sel) + _mm(x2, sel) + _mm(x3, sel)


def _tile(n, want):
    t = min(n, want)
    assert n % t == 0
    return t


_FFN_CHUNK = 256


def _ffn_up(x, g, w13, name):
    T, D = x.shape
    _, J, bf, _ = w13.shape
    tm = _tile(T, 512)
    I = T // tm

    def body(x_ref, g_ref, w13_ref, G_ref, U_ref, A_ref, h_s):
        j = pl.program_id(0)
        i = pl.program_id(1)
        rows = pl.ds(pl.multiple_of(i * tm, tm), tm)

        @pl.when(j == 0)
        def _():
            h, _ = _rms_fwd(x_ref[...], g_ref[...])
            h_s[rows, :] = h.astype(MXU)

        chunks = [slice(r0, r0 + _FFN_CHUNK) for r0 in range(0, tm, _FFN_CHUNK)]
        hbs = [h_s[pl.ds(pl.multiple_of(i * tm + rs.start, _FFN_CHUNK), _FFN_CHUNK), :] for rs in chunks]
        GU = [(_mm_nt(hb, w13_ref[0]), _mm_nt(hb, w13_ref[1])) for hb in hbs]
        for rs, (G, U) in zip(chunks, GU):
            G_ref[rs, :] = G.astype(MXU)
            U_ref[rs, :] = U.astype(MXU)
            A_ref[rs, :] = (G * jax.nn.sigmoid(G) * U).astype(MXU)

    blk = pl.BlockSpec((None, tm, bf), lambda j, i: (j, i, 0))
    hid = jax.ShapeDtypeStruct((J, T, bf), MXU)
    return _pallas_call(
        body, name=name, grid=(J, I),
        in_specs=[pl.BlockSpec((tm, D), lambda j, i: (jnp.where(j == 0, i, I - 1), 0)),
                  pl.BlockSpec((1, D), lambda j, i: (0, 0)),
                  pl.BlockSpec((2, None, bf, D), lambda j, i: (0, j, 0, 0))],
        out_specs=[blk, blk, blk],
        out_shape=[hid, hid, hid],
        scratch_shapes=[pltpu.VMEM((T, D), MXU)],
        compiler_params=_params(40, 2),
    )(x, g, w13)


def _ffn_down(x, A, w2, name):
    T, D = x.shape
    J, _, bf = A.shape
    tm = _tile(T, 512)

    def body(x_ref, A_ref, w2_ref, xo_ref):
        f = _mm(A_ref[0], w2_ref[0:bf, :])
        for j in range(1, J):
            f = f + _mm(A_ref[j], w2_ref[j * bf:(j + 1) * bf, :])
        xo_ref[...] = x_ref[...] + 0.5 * f

    row = pl.BlockSpec((tm, D), lambda i: (i, 0))
    return _pallas_call(
        body, name=name, grid=(T // tm,),
        in_specs=[row, pl.BlockSpec((J, tm, bf), lambda i: (0, i, 0)), pl.BlockSpec((J * bf, D), lambda i: (0, 0))],
        out_specs=row,
        out_shape=jax.ShapeDtypeStruct((T, D), F32),
        compiler_params=_params(48, 1),
    )(x, A, w2)


def _ffn_bwd_act(x, g, dy, Gs, Us, w13, w2, name, after=()):
    T, D = x.shape
    _, J, bf, _ = w13.shape
    tm = _tile(T, 512)
    I = T // tm

    def body(x_ref, g_ref, dy_ref, G_ref, U_ref, w13_ref, w2_ref, *rest):
        dx_ref, dg_ref, h_ref, dG_ref, dU_ref, dh_s, dF_s, h_s = rest[len(after):]
        j = pl.program_id(0)
        i = pl.program_id(1)
        rows = pl.ds(pl.multiple_of(i * tm, tm), tm)

        @pl.when(j == 0)
        def _():
            h, _ = _rms_fwd(x_ref[...], g_ref[...])
            hb = h.astype(MXU)
            h_s[rows, :] = hb
            h_ref[...] = hb
            dF_s[rows, :] = (0.5 * dy_ref[...]).astype(MXU)
            dh_s[rows, :] = jnp.zeros((tm, D), F32)

        chunks = [slice(r0, r0 + _FFN_CHUNK) for r0 in range(0, tm, _FFN_CHUNK)]
        crows = [pl.ds(pl.multiple_of(i * tm + rs.start, _FFN_CHUNK), _FFN_CHUNK) for rs in chunks]
        dAs = [_mm_nt(dF_s[cr, :], w2_ref[...]) for cr in crows]
        for rs, cr, dA in zip(chunks, crows, dAs):
            G = G_ref[rs, :].astype(F32)
            U = U_ref[rs, :].astype(F32)
            sg = jax.nn.sigmoid(G)
            s = G * sg
            dUb = (dA * s).astype(MXU)
            dGb = (dA * U * _silu_grad(G, sg)).astype(MXU)
            dG_ref[rs, :] = dGb
            dU_ref[rs, :] = dUb
            dh_s[cr, :] += _mm(dGb, w13_ref[0]) + _mm(dUb, w13_ref[1])

        @pl.when(j == J - 1)
        def _():
            xv = x_ref[...]
            gv = g_ref[...]
            _, r = _rms_fwd(xv, gv)
            dxn, dgp = _rms_bwd(xv, r, gv, dh_s[rows, :])
            dx_ref[...] = dy_ref[...] + dxn

            @pl.when(i == 0)
            def _():
                dg_ref[...] = dgp

            @pl.when(i > 0)
            def _():
                dg_ref[...] += dgp

    ends = lambda j, i: (jnp.where((j == 0) | (j == J - 1), i, I - 1), 0)
    blk = pl.BlockSpec((None, tm, bf), lambda j, i: (j, i, 0))
    hid = jax.ShapeDtypeStruct((J, T, bf), MXU)
    return _pallas_call(
        body, name=name, grid=(J, I),
        in_specs=[pl.BlockSpec((tm, D), ends), pl.BlockSpec((1, D), lambda j, i: (0, 0)), pl.BlockSpec((tm, D), ends),
                  blk, blk, pl.BlockSpec((2, None, bf, D), lambda j, i: (0, j, 0, 0)),
                  pl.BlockSpec((bf, D), lambda j, i: (j, 0))] + [_UNREAD] * len(after),
        out_specs=[pl.BlockSpec((tm, D), lambda j, i: (jnp.where(j == J - 1, i, 0), 0)),
                   pl.BlockSpec((1, D), lambda j, i: (0, 0)),
                   pl.BlockSpec((tm, D), lambda j, i: (jnp.where(j == 0, i, I - 1), 0)), blk, blk],
        out_shape=[jax.ShapeDtypeStruct((T, D), F32), jax.ShapeDtypeStruct((1, D), F32),
                   jax.ShapeDtypeStruct((T, D), MXU), hid, hid],
        scratch_shapes=[pltpu.VMEM((T, D), F32), pltpu.VMEM((T, D), MXU), pltpu.VMEM((T, D), MXU)],
        compiler_params=_params(58, 2),
    )(x, g, dy, Gs, Us, w13, w2, *after)


def _ffn_w13_grad(h, dG, dU, name):
    T, D = h.shape
    J, _, bf = dG.shape

    def body(h_ref, dG_ref, dU_ref, dw13_ref):
        dw13_ref[0] = _mm_tn(dG_ref[...], h_ref[...]).astype(dw13_ref.dtype)
        dw13_ref[1] = _mm_tn(dU_ref[...], h_ref[...]).astype(dw13_ref.dtype)

    blk = pl.BlockSpec((None, T, bf), lambda j: (j, 0, 0))
    return _pallas_call(
        body, name=name, grid=(J,),
        in_specs=[pl.BlockSpec((T, D), lambda j: (0, 0)), blk, blk],
        out_specs=pl.BlockSpec((2, None, bf, D), lambda j: (0, j, 0, 0)),
        out_shape=jax.ShapeDtypeStruct((2, J, bf, D), MXU),
        compiler_params=_params(48, 1),
    )(h, dG, dU)


def _ffn_w2_grad(dy, A, name, after=()):
    T, D = dy.shape
    J, _, bf = A.shape

    def body(dy_ref, A_ref, *rest):
        dw2_ref, dF_s = rest[len(after):]

        @pl.when(pl.program_id(0) == 0)
        def _():
            dF_s[...] = (0.5 * dy_ref[...]).astype(MXU)

        dw2_ref[...] = _mm_tn(A_ref[...], dF_s[...]).astype(dw2_ref.dtype)

    return _pallas_call(
        body, name=name, grid=(J,),
        in_specs=[pl.BlockSpec((T, D), lambda j: (0, 0)), pl.BlockSpec((None, T, bf), lambda j: (j, 0, 0))]
        + [_UNREAD] * len(after),
        out_specs=pl.BlockSpec((bf, D), lambda j: (j, 0)),
        out_shape=jax.ShapeDtypeStruct((J * bf, D), MXU),
        scratch_shapes=[pltpu.VMEM((T, D), MXU)],
        compiler_params=_params(48, 1),
    )(dy, A, *after)


_AG0, _Q0, _K0, _V0, _F0 = 0, 2 * D_CONV, 2 * D_CONV + D_ATTN, 2 * D_CONV + 2 * D_ATTN, 2 * D_CONV + 3 * D_ATTN
N_IN = _F0 + N_HEADS
N_IN_PAD = _F0 + LANES
_IN_BLOCK = N_IN // N_DEV


def _rows_from_blocks(blocks_ref, rows_ref):
    for p in range(N_DEV):
        rows_ref[_IN_BLOCK * p:_IN_BLOCK * (p + 1), :] = blocks_ref[p]
    rows_ref[N_IN:, :] = jnp.zeros((N_IN_PAD - N_IN, rows_ref.shape[1]), rows_ref.dtype)


def _inproj_fwd(x1, gm, win_blocks, name):
    T, D = x1.shape
    tm = _tile(T, 256)

    def body(x_ref, g_ref, wb_ref, ag_ref, k_ref, v_ref, qT_ref, kT_ref, vT_ref, fl_ref, w_ref):
        @pl.when(pl.program_id(0) == 0)
        def _():
            _rows_from_blocks(wb_ref, w_ref)

        h, _ = _rms_fwd(x_ref[...], g_ref[...])
        hb = h.astype(MXU)
        ag_ref[...] = _mm_nt(hb, w_ref[_AG0:_Q0, :])
        qT_ref[...] = (_mm_nt(hb, w_ref[_Q0:_K0, :]) * SCALE).T.astype(MXU)
        for c0, ref, refT in ((_K0, k_ref, kT_ref), (_V0, v_ref, vT_ref)):
            y = _mm_nt(hb, w_ref[c0:c0 + D_ATTN, :])
            ref[...] = y.astype(MXU)
            refT[...] = y.T.astype(MXU)
        fl_ref[...] = _mm_nt(hb, w_ref[_F0:N_IN_PAD, :])

    row = lambda w: pl.BlockSpec((tm, w), lambda i: (i, 0))
    col = pl.BlockSpec((D_ATTN, tm), lambda i: (0, i))
    std = jax.ShapeDtypeStruct((T, D_ATTN), MXU)
    trn = jax.ShapeDtypeStruct((D_ATTN, T), MXU)
    return _pallas_call(
        body, name=name, grid=(T // tm,),
        in_specs=[row(D), pl.BlockSpec((1, D), lambda i: (0, 0)),
                  pl.BlockSpec((N_DEV, _IN_BLOCK, D), lambda i: (0, 0, 0))],
        out_specs=[row(2 * D_CONV), row(D_ATTN), row(D_ATTN), col, col, col, row(LANES)],
        out_shape=[jax.ShapeDtypeStruct((T, 2 * D_CONV), F32), std, std, trn, trn, trn,
                   jax.ShapeDtypeStruct((T, LANES), F32)],
        scratch_shapes=[pltpu.VMEM((N_IN_PAD, D), MXU)],
        compiler_params=_params(40, 1),
    )(x1, gm, win_blocks)


def _inproj_bwd_act(x1, gm, dx2, dag, dqT, dkT, dvT, dfl, win_blocks, name):
    T, D = x1.shape
    tm = _tile(T, 256)

    def body(x_ref, g_ref, dx2_ref, dag_ref, dqT_ref, dkT_ref, dvT_ref, dfl_ref, wb_ref, dx1_ref, dg_ref, h_ref,
             w_ref):
        i = pl.program_id(0)

        @pl.when(i == 0)
        def _():
            _rows_from_blocks(wb_ref, w_ref)

        xv = x_ref[...]
        gv = g_ref[...]
        h, r = _rms_fwd(xv, gv)
        h_ref[...] = h.astype(MXU)
        dh = _mm(dag_ref[...], w_ref[_AG0:_Q0, :])
        for c0, ref in ((_Q0, dqT_ref), (_K0, dkT_ref), (_V0, dvT_ref)):
            dh = dh + _mm_tn(ref[...].astype(MXU), w_ref[c0:c0 + D_ATTN, :])
        dh = dh + _mm(dfl_ref[...].astype(MXU), w_ref[_F0:N_IN_PAD, :])
        dxn, dgp = _rms_bwd(xv, r, gv, dh)
        dx1_ref[...] = dx2_ref[...] + dxn

        @pl.when(i == 0)
        def _():
            dg_ref[...] = dgp

        @pl.when(i > 0)
        def _():
            dg_ref[...] += dgp

    row = lambda w: pl.BlockSpec((tm, w), lambda i: (i, 0))
    col = pl.BlockSpec((D_ATTN, tm), lambda i: (0, i))
    full = lambda a, b: pl.BlockSpec((a, b), lambda i: (0, 0))
    return _pallas_call(
        body, name=name, grid=(T // tm,),
        in_specs=[row(D), full(1, D), row(D), row(2 * D_CONV), col, col, col, row(LANES),
                  pl.BlockSpec((N_DEV, _IN_BLOCK, D), lambda i: (0, 0, 0))],
        out_specs=[row(D), full(1, D), row(D)],
        out_shape=[jax.ShapeDtypeStruct((T, D), F32), jax.ShapeDtypeStruct((1, D), F32),
                   jax.ShapeDtypeStruct((T, D), MXU)],
        scratch_shapes=[pltpu.VMEM((N_IN_PAD, D), MXU)],
        compiler_params=_params(40, 1),
    )(x1, gm, dx2, dag, dqT, dkT, dvT, dfl, win_blocks)


def _inproj_bwd_weights(h, dag, dqT, dkT, dvT, dfl, name, after=()):
    T, D = h.shape

    def body(h_ref, dag_ref, dqT_ref, dkT_ref, dvT_ref, dfl_ref, *rest):
        blocks_ref, dw_ref = rest[len(after):]
        hb = h_ref[...]
        dw_ref[_AG0:_Q0, :] = _mm_tn(dag_ref[...], hb).astype(dw_ref.dtype)
        for c0, ref in ((_Q0, dqT_ref), (_K0, dkT_ref), (_V0, dvT_ref)):
            dw_ref[c0:c0 + D_ATTN, :] = _mm(ref[...].astype(MXU), hb).astype(dw_ref.dtype)
        dw_ref[_F0:N_IN_PAD, :] = _mm_tn(dfl_ref[...].astype(MXU), hb).astype(dw_ref.dtype)
        for p in range(N_DEV):
            blocks_ref[p] = dw_ref[_IN_BLOCK * p:_IN_BLOCK * (p + 1), :]

    vmem = pl.BlockSpec(memory_space=pltpu.VMEM)
    return _pallas_call(
        body, name=name, in_specs=[vmem] * 6 + [_UNREAD] * len(after), out_specs=vmem,
        out_shape=jax.ShapeDtypeStruct((N_DEV, _IN_BLOCK, D), MXU),
        scratch_shapes=[pltpu.VMEM((N_IN_PAD, D), MXU)],
        compiler_params=pltpu.CompilerParams(vmem_limit_bytes=56 * MIB),
    )(h, dag, dqT, dkT, dvT, dfl, *after)


def _forget_fwd(fl, fbp, name):
    T = fl.shape[0]
    tb = _tile(T, 256)

    def body(fl_ref, fb_ref, cum_ref, cumT_ref):
        ri = lax.broadcasted_iota(jnp.int32, (tb, tb), 0)
        ci = lax.broadcasted_iota(jnp.int32, (tb, tb), 1)
        tri = (ri >= ci).astype(jnp.bfloat16)
        carry = jnp.zeros((1, LANES), F32)
        for b in range(T // tb):
            z = fl_ref[b * tb:(b + 1) * tb, :] + fb_ref[...]
            lf = jnp.minimum(z, 0.0) - jnp.log1p(jnp.exp(-jnp.abs(z)))
            c = _exact_tri_dot(tri, lf) + carry
            cum_ref[b * tb:(b + 1) * tb, :] = c
            carry = c[tb - 1:tb, :]
        cumT_ref[...] = cum_ref[...].T[:N_HEADS, :]

    return _pallas_call(
        body, name=name,
        out_shape=[jax.ShapeDtypeStruct((T, LANES), F32), jax.ShapeDtypeStruct((N_HEADS, T), F32)],
        compiler_params=pltpu.CompilerParams(vmem_limit_bytes=32 * MIB),
    )(fl, fbp)


def _forget_bwd(dcum, fl, fbp, name):
    T = fl.shape[0]
    tb = _tile(T, 256)

    def body(dc_ref, fl_ref, fb_ref, dfl_ref, dfb_ref):
        ri = lax.broadcasted_iota(jnp.int32, (tb, tb), 0)
        ci = lax.broadcasted_iota(jnp.int32, (tb, tb), 1)
        tri = (ri <= ci).astype(jnp.bfloat16)
        carry = jnp.zeros((1, LANES), F32)
        dfb = jnp.zeros((1, LANES), F32)
        for b in reversed(range(T // tb)):
            sl = slice(b * tb, (b + 1) * tb)
            dl = _exact_tri_dot(tri, dc_ref[sl, :]) + carry
            carry = dl[0:1, :]
            z = fl_ref[sl, :] + fb_ref[...]
            dfl = dl * jax.nn.sigmoid(-z)
            dfl_ref[sl, :] = dfl
            dfb = dfb + jnp.sum(dfl, axis=0, keepdims=True)
        dfb_ref[...] = dfb

    return _pallas_call(
        body, name=name,
        out_shape=[jax.ShapeDtypeStruct((T, LANES), F32), jax.ShapeDtypeStruct((1, LANES), F32)],
        compiler_params=pltpu.CompilerParams(vmem_limit_bytes=32 * MIB),
    )(dcum, fl, fbp)


def _causal_keep(i, j, tq, tk):
    key = j * tk + lax.broadcasted_iota(jnp.int32, (tk, tq), 0)
    qry = i * tq + lax.broadcasted_iota(jnp.int32, (tk, tq), 1)
    return key <= qry


def _split_hi_lo(x):
    hi = x.astype(MXU)
    lo = (x - hi.astype(F32)).astype(MXU)
    return hi, lo


def _attn_fwd(qT, k, vT, cum, cumT, name):
    T = k.shape[0]
    tq = _tile(T, 256)
    tk = _tile(tq, 256)
    kpq = tq // tk
    heads = [slice(HEAD_DIM * h, HEAD_DIM * (h + 1)) for h in range(N_HEADS)]

    def body(qT_ref, k_ref, vT_ref, cum_ref, cumT_ref, o_ref, lseT_ref, acc_s, m_s, l_s):
        i = pl.program_id(0)
        acc_s[...] = jnp.zeros_like(acc_s)
        m_s[...] = jnp.full_like(m_s, NEG)
        l_s[...] = jnp.zeros_like(l_s)

        def kblock(j, masked):
            rows = pl.ds(pl.multiple_of(j * tk, tk), tk)
            keep = _causal_keep(i, j, tq, tk) if masked else None
            bias = [cumT_ref[h:h + 1, :] - cum_ref[rows, h:h + 1] for h in range(N_HEADS)]
            qk = [_mm(k_ref[rows, hs], qT_ref[hs, :]) + bias[h] for h, hs in enumerate(heads)]
            for h, hs in enumerate(heads):
                sT = qk[h]
                if masked:
                    sT = jnp.where(keep, sT, NEG)
                m_old = m_s[h:h + 1, :]
                m_new = jnp.maximum(m_old, jnp.max(sT, axis=0, keepdims=True))
                alpha = jnp.exp(m_old - m_new)
                pT = jnp.exp(sT - m_new)
                l_s[h:h + 1, :] = alpha * l_s[h:h + 1, :] + jnp.sum(pT, axis=0, keepdims=True)
                p_hi, p_lo = _split_hi_lo(pT)
                vh = vT_ref[hs, rows]
                acc_s[hs, :] = alpha * acc_s[hs, :] + (_mm(vh, p_hi) + _mm(vh, p_lo))
                m_s[h:h + 1, :] = m_new

        def unmasked(j, c):
            kblock(j, False)
            return c

        lax.fori_loop(0, kpq * i, unmasked, 0)
        for d in range(kpq):
            kblock(kpq * i + d, True)
        for h, hs in enumerate(heads):
            acc_s[hs, :] = acc_s[hs, :] / l_s[h:h + 1, :]
        o_ref[...] = acc_s[...].T
        lseT_ref[...] = m_s[...] + jnp.log(l_s[...])

    full = lambda a, b: pl.BlockSpec((a, b), lambda i: (0, 0))
    colblk = lambda r: pl.BlockSpec((r, tq), lambda i: (0, i))
    return _pallas_call(
        body, name=name, grid=(T // tq,),
        in_specs=[colblk(D_ATTN), full(T, D_ATTN), full(D_ATTN, T), full(T, LANES), colblk(N_HEADS)],
        out_specs=[pl.BlockSpec((tq, D_ATTN), lambda i: (i, 0)), colblk(N_HEADS)],
        out_shape=[jax.ShapeDtypeStruct((T, D_ATTN), F32), jax.ShapeDtypeStruct((N_HEADS, T), F32)],
        scratch_shapes=[pltpu.VMEM((D_ATTN, tq), F32), pltpu.VMEM((N_HEADS, tq), F32),
                        pltpu.VMEM((N_HEADS, tq), F32)],
        compiler_params=_params(40, 1),
    )(qT, k, vT, cum, cumT)


def _attn_bwd(qT, k, kT, v, doT, lseT, deltaT, cum, cumT, name, after=()):
    T = k.shape[0]
    tq = _tile(T, 256)
    tk = _tile(tq, 256)
    kpq = tq // tk
    heads = [slice(HEAD_DIM * h, HEAD_DIM * (h + 1)) for h in range(N_HEADS)]

    def body(qT_ref, k_ref, kT_ref, v_ref, doT_ref, lseT_ref, dlT_ref, cum_ref, cumT_ref, *rest):
        dq_ref, dk_ref, dv_ref, dcum_ref, dq_s = rest[len(after):]
        i = pl.program_id(0)

        @pl.when(i == 0)
        def _():
            dk_ref[...] = jnp.zeros_like(dk_ref)
            dv_ref[...] = jnp.zeros_like(dv_ref)
            dcum_ref[...] = jnp.zeros_like(dcum_ref)

        dq_s[...] = jnp.zeros_like(dq_s)

        def kblock(j, masked):
            rows = pl.ds(pl.multiple_of(j * tk, tk), tk)
            keep = _causal_keep(i, j, tq, tk) if masked else None
            bias = [cumT_ref[h:h + 1, :] - cum_ref[rows, h:h + 1] for h in range(N_HEADS)]
            qk = [_mm(k_ref[rows, hs], qT_ref[hs, :]) + bias[h] for h, hs in enumerate(heads)]
            dps = [_mm(v_ref[rows, hs], doT_ref[hs, :]) for hs in heads]
            for h, hs in enumerate(heads):
                sT = qk[h]
                if masked:
                    sT = jnp.where(keep, sT, NEG)
                pT = jnp.exp(sT - lseT_ref[h:h + 1, :])
                dsT = pT * (dps[h] - dlT_ref[h:h + 1, :])
                dcum_ref[rows, h:h + 1] += -jnp.sum(dsT, axis=1, keepdims=True)
                dsb = dsT.astype(MXU)
                dv_ref[hs, rows] += _mm_nt(doT_ref[hs, :], pT.astype(MXU))
                dk_ref[hs, rows] += _mm_nt(qT_ref[hs, :], dsb)
                dq_s[hs, :] += _mm(kT_ref[hs, rows], dsb)

        def unmasked(j, c):
            kblock(j, False)
            return c

        lax.fori_loop(0, kpq * i, unmasked, 0)
        for d in range(kpq):
            kblock(kpq * i + d, True)
        dq_ref[...] = (dq_s[...] * SCALE).astype(dq_ref.dtype)

    full = lambda a, b: pl.BlockSpec((a, b), lambda i: (0, 0))
    colblk = lambda r: pl.BlockSpec((r, tq), lambda i: (0, i))
    return _pallas_call(
        body, name=name, grid=(T // tq,),
        in_specs=[colblk(D_ATTN), full(T, D_ATTN), full(D_ATTN, T), full(T, D_ATTN), colblk(D_ATTN),
                  colblk(N_HEADS), colblk(N_HEADS), full(T, LANES), colblk(N_HEADS)] + [_UNREAD] * len(after),
        out_specs=[colblk(D_ATTN), full(D_ATTN, T), full(D_ATTN, T), full(T, LANES)],
        out_shape=[
            jax.ShapeDtypeStruct((D_ATTN, T), MXU),
            jax.ShapeDtypeStruct((D_ATTN, T), F32),
            jax.ShapeDtypeStruct((D_ATTN, T), F32),
            jax.ShapeDtypeStruct((T, LANES), F32),
        ],
        scratch_shapes=[pltpu.VMEM((D_ATTN, tq), F32)],
        compiler_params=_params(48, 1),
    )(qT, k, kT, v, doT, lseT, deltaT, cum, cumT, *after)


_ROWS_PER_CHUNK = 64


def _glu_halo(ag_ref, agh_ref, uext_s, tm, first):
    a = ag_ref[:, :D_CONV]
    sg = jax.nn.sigmoid(ag_ref[:, D_CONV:])
    uh = agh_ref[:, :D_CONV] * jax.nn.sigmoid(agh_ref[:, D_CONV:])
    uext_s[0:CONV_HALO, :] = jnp.where(first, 0.0, uh)
    uext_s[CONV_HALO:CONV_HALO + tm, :] = a * sg
    return a, sg


_SUBLANES = 8


def _shifted_copies(ext_s, sh_s, rows):
    for k in range(1, _SUBLANES):
        sh_s[k, 0:rows, :] = ext_s[pl.ds(k, rows), :]


def _window(ext_s, sh_s, start, rows):
    k = start % _SUBLANES
    if k == 0:
        return ext_s[pl.ds(start, rows), :]
    return sh_s[k, pl.ds(start - k, rows), :]


def _layer_norm_stats(y):
    mu = jnp.mean(y, axis=-1, keepdims=True)
    xc = y - mu
    rs = lax.rsqrt(jnp.mean(xc * xc, axis=-1, keepdims=True) + EPS)
    return xc * rs, rs


def _conv_fwd(ag, w32, cb, lg, lb, name):
    T = ag.shape[0]
    tm = _tile(T, 256)
    rc = _tile(tm, _ROWS_PER_CHUNK)
    hb = tm // CONV_HALO

    def body(ag_ref, agh_ref, w_ref, cb_ref, lg_ref, lb_ref, yc_ref, c_ref, uext_s, ush_s):
        i = pl.program_id(0)
        _glu_halo(ag_ref, agh_ref, uext_s, tm, i == 0)
        _shifted_copies(uext_s, ush_s, tm + CONV_HALO - _SUBLANES)
        for r0 in range(0, tm, rc):
            acc = jnp.zeros((rc, D_CONV), F32)
            for t in range(CONV_TAPS):
                acc = acc + _window(uext_s, ush_s, r0 + CONV_HALO - (CONV_TAPS - 1) + t, rc) * w_ref[t:t + 1, :]
            y = acc + cb_ref[...]
            yc_ref[r0:r0 + rc, :] = y
            n, _ = _layer_norm_stats(y)
            z = n * lg_ref[...] + lb_ref[...]
            c_ref[r0:r0 + rc, :] = z * jax.nn.sigmoid(z)

    row = lambda w: pl.BlockSpec((tm, w), lambda i: (i, 0))
    full = lambda a, b: pl.BlockSpec((a, b), lambda i: (0, 0))
    return _pallas_call(
        body, name=name, grid=(T // tm,),
        in_specs=[row(2 * D_CONV),
                  pl.BlockSpec((CONV_HALO, 2 * D_CONV), lambda i: (jnp.maximum(i * hb - 1, 0), 0)),
                  full(CONV_HALO, D_CONV), full(1, D_CONV), full(1, D_CONV), full(1, D_CONV)],
        out_specs=[row(D_CONV), row(D_CONV)],
        out_shape=[jax.ShapeDtypeStruct((T, D_CONV), F32), jax.ShapeDtypeStruct((T, D_CONV), F32)],
        scratch_shapes=[pltpu.VMEM((CONV_HALO + tm, D_CONV), F32),
                        pltpu.VMEM((_SUBLANES, CONV_HALO + tm, D_CONV), F32)],
        compiler_params=_params(32, 1),
    )(ag, ag, w32, cb, lg, lb)


def _conv_bwd(dc, yc, ag, w32, lg, lb, name):
    T = ag.shape[0]
    tm = _tile(T, 256)
    rc = _tile(tm, _ROWS_PER_CHUNK)
    I = T // tm
    hb = tm // CONV_HALO
    n_halo_blocks = T // CONV_HALO

    def body(dc_ref, yc_ref, dch_ref, ych_ref, ag_ref, agh_ref, w_ref, lg_ref, lb_ref,
             dag_ref, dw_ref, dcb_ref, dlg_ref, dlb_ref, uext_s, dext_s, ush_s, dsh_s):
        i = pl.program_id(0)
        lgv = lg_ref[...]
        lbv = lb_ref[...]

        def ln_bwd(dcv, ycv):
            n, rs = _layer_norm_stats(ycv)
            z = n * lgv + lbv
            dz = dcv * _silu_grad(z, jax.nn.sigmoid(z))
            dn = dz * lgv
            dy = rs * (dn - jnp.mean(dn, axis=-1, keepdims=True) - n * jnp.mean(dn * n, axis=-1, keepdims=True))
            return dy, dz, n

        dy, dz, n = ln_bwd(dc_ref[...], yc_ref[...])
        dyh, _, _ = ln_bwd(dch_ref[...], ych_ref[...])
        dext_s[0:tm, :] = dy
        dext_s[tm:tm + CONV_HALO, :] = jnp.where(i == I - 1, 0.0, dyh)
        a, sg = _glu_halo(ag_ref, agh_ref, uext_s, tm, i == 0)
        _shifted_copies(uext_s, ush_s, tm + CONV_HALO - _SUBLANES)
        _shifted_copies(dext_s, dsh_s, tm + CONV_HALO - _SUBLANES)

        @pl.when(i == 0)
        def _():
            dw_ref[...] = jnp.zeros_like(dw_ref)
            dcb_ref[...] = jnp.zeros_like(dcb_ref)
            dlg_ref[...] = jnp.zeros_like(dlg_ref)
            dlb_ref[...] = jnp.zeros_like(dlb_ref)

        dcb_ref[...] += jnp.sum(dy, axis=0, keepdims=True)
        dlg_ref[...] += jnp.sum(dz * n, axis=0, keepdims=True)
        dlb_ref[...] += jnp.sum(dz, axis=0, keepdims=True)
        for t in range(CONV_TAPS):
            u_t = _window(uext_s, ush_s, CONV_HALO - (CONV_TAPS - 1) + t, tm)
            dw_ref[t:t + 1, :] += jnp.sum(dy * u_t, axis=0, keepdims=True)
        for r0 in range(0, tm, rc):
            acc = jnp.zeros((rc, D_CONV), F32)
            for t in range(CONV_TAPS):
                acc = acc + _window(dext_s, dsh_s, r0 + (CONV_TAPS - 1) - t, rc) * w_ref[t:t + 1, :]
            a_c = a[r0:r0 + rc, :]
            sg_c = sg[r0:r0 + rc, :]
            dag_ref[r0:r0 + rc, :D_CONV] = (acc * sg_c).astype(dag_ref.dtype)
            dag_ref[r0:r0 + rc, D_CONV:] = (acc * a_c * sg_c * (1.0 - sg_c)).astype(dag_ref.dtype)

    row = lambda w: pl.BlockSpec((tm, w), lambda i: (i, 0))
    full = lambda a, b: pl.BlockSpec((a, b), lambda i: (0, 0))
    nxt = pl.BlockSpec((CONV_HALO, D_CONV), lambda i: (jnp.minimum((i + 1) * hb, n_halo_blocks - 1), 0))
    return _pallas_call(
        body, name=name, grid=(I,),
        in_specs=[row(D_CONV), row(D_CONV), nxt, nxt, row(2 * D_CONV),
                  pl.BlockSpec((CONV_HALO, 2 * D_CONV), lambda i: (jnp.maximum(i * hb - 1, 0), 0)),
                  full(CONV_HALO, D_CONV), full(1, D_CONV), full(1, D_CONV)],
        out_specs=[row(2 * D_CONV), full(CONV_HALO, D_CONV), full(1, D_CONV), full(1, D_CONV), full(1, D_CONV)],
        out_shape=[
            jax.ShapeDtypeStruct((T, 2 * D_CONV), MXU),
            jax.ShapeDtypeStruct((CONV_HALO, D_CONV), F32),
            jax.ShapeDtypeStruct((1, D_CONV), F32),
            jax.ShapeDtypeStruct((1, D_CONV), F32),
            jax.ShapeDtypeStruct((1, D_CONV), F32),
        ],
        scratch_shapes=[pltpu.VMEM((CONV_HALO + tm, D_CONV), F32), pltpu.VMEM((tm + CONV_HALO, D_CONV), F32),
                        pltpu.VMEM((_SUBLANES, CONV_HALO + tm, D_CONV), F32),
                        pltpu.VMEM((_SUBLANES, CONV_HALO + tm, D_CONV), F32)],
        compiler_params=_params(40, 1),
    )(dc, yc, dc, yc, ag, ag, w32, lg, lb)


def _outproj_fwd(x1, c, o, gc, ga, wout, name):
    T, D = x1.shape
    tm = _tile(T, 512)

    def body(x_ref, c_ref, o_ref, gc_ref, ga_ref, w_ref, x2_ref):
        yc, _ = _rms_fwd(c_ref[...], gc_ref[...])
        ya, _ = _rms_fwd(o_ref[...], ga_ref[...])
        x2_ref[...] = (x_ref[...] + _mm(yc.astype(MXU), w_ref[:D_CONV, :])
                       + _mm(ya.astype(MXU), w_ref[D_CONV:, :]))

    row = lambda w: pl.BlockSpec((tm, w), lambda i: (i, 0))
    full = lambda a, b: pl.BlockSpec((a, b), lambda i: (0, 0))
    return _pallas_call(
        body, name=name, grid=(T // tm,),
        in_specs=[row(D), row(D_CONV), row(D_ATTN), full(1, D_CONV), full(1, D_ATTN), full(D_CONV + D_ATTN, D)],
        out_specs=row(D),
        out_shape=jax.ShapeDtypeStruct((T, D), F32),
        compiler_params=_params(32, 1),
    )(x1, c, o, gc, ga, wout)


def _outproj_bwd(dx2, c, o, gc, ga, wout, name):
    T, D = dx2.shape
    tm = _tile(T, 256)
    I = T // tm

    def body(dx_ref, c_ref, o_ref, gc_ref, ga_ref, w_ref,
             dc_ref, doT_ref, dlT_ref, dw_ref, dgc_ref, dga_ref, acc_s):
        i = pl.program_id(0)
        dxb = dx_ref[...].astype(MXU)
        cv = c_ref[...]
        ov = o_ref[...]
        yc, rcn = _rms_fwd(cv, gc_ref[...])
        ya, ra = _rms_fwd(ov, ga_ref[...])
        dyc = _mm_nt(dxb, w_ref[:D_CONV, :])
        dya = _mm_nt(dxb, w_ref[D_CONV:, :])
        dwc = _mm_tn(yc.astype(MXU), dxb)
        dwa = _mm_tn(ya.astype(MXU), dxb)
        dcv, dgc = _rms_bwd(cv, rcn, gc_ref[...], dyc)
        dov, dga = _rms_bwd(ov, ra, ga_ref[...], dya)
        dc_ref[...] = dcv
        dob = dov.astype(doT_ref.dtype)
        doT_ref[...] = dov.T.astype(doT_ref.dtype)
        chan = lax.broadcasted_iota(jnp.int32, (D_ATTN, LANES), 0)
        head = lax.broadcasted_iota(jnp.int32, (D_ATTN, LANES), 1)
        in_head = ((chan >= head * HEAD_DIM) & (chan < (head + 1) * HEAD_DIM)).astype(jnp.bfloat16)
        dlT_ref[...] = _exact_dot_01(dob.astype(F32) * ov, in_head).T[:N_HEADS, :]

        @pl.when(i == 0)
        def _():
            acc_s[:D_CONV, :] = dwc
            acc_s[D_CONV:, :] = dwa
            dgc_ref[...] = dgc
            dga_ref[...] = dga

        @pl.when(i > 0)
        def _():
            acc_s[:D_CONV, :] += dwc
            acc_s[D_CONV:, :] += dwa
            dgc_ref[...] += dgc
            dga_ref[...] += dga

        @pl.when(i == I - 1)
        def _():
            dw_ref[...] = acc_s[...].astype(dw_ref.dtype)

    row = lambda w: pl.BlockSpec((tm, w), lambda i: (i, 0))
    full = lambda a, b: pl.BlockSpec((a, b), lambda i: (0, 0))
    return _pallas_call(
        body, name=name, grid=(I,),
        in_specs=[row(D), row(D_CONV), row(D_ATTN), full(1, D_CONV), full(1, D_ATTN), full(D_CONV + D_ATTN, D)],
        out_specs=[row(D_CONV), pl.BlockSpec((D_ATTN, tm), lambda i: (0, i)),
                   pl.BlockSpec((N_HEADS, tm), lambda i: (0, i)),
                   full(D_CONV + D_ATTN, D), full(1, D_CONV), full(1, D_ATTN)],
        out_shape=[
            jax.ShapeDtypeStruct((T, D_CONV), F32),
            jax.ShapeDtypeStruct((D_ATTN, T), MXU),
            jax.ShapeDtypeStruct((N_HEADS, T), F32),
            jax.ShapeDtypeStruct((D_CONV + D_ATTN, D), MXU),
            jax.ShapeDtypeStruct((1, D_CONV), F32),
            jax.ShapeDtypeStruct((1, D_ATTN), F32),
        ],
        scratch_shapes=[pltpu.VMEM((D_CONV + D_ATTN, D), F32)],
        compiler_params=_params(40, 1),
    )(dx2, c, o, gc, ga, wout)


def _ffn_down_loss(x, A, w2, gf, target, name):
    T, D = x.shape
    J, _, bf = A.shape
    tm = _tile(T, 512)

    def body(x_ref, A_ref, w2_ref, g_ref, t_ref, loss_ref, dx_ref, dg_ref):
        i = pl.program_id(0)
        f = _mm(A_ref[0], w2_ref[0:bf, :])
        for j in range(1, J):
            f = f + _mm(A_ref[j], w2_ref[j * bf:(j + 1) * bf, :])
        xv = x_ref[...] + 0.5 * f
        gv = g_ref[...]
        out, r = _rms_fwd(xv, gv)
        err = out - t_ref[...]
        part = jnp.full((1, LANES), 0.5 / D, F32) * jnp.sum(err * err)
        dxn, dgp = _rms_bwd(xv, r, gv, err * (1.0 / D))
        dx_ref[...] = dxn

        @pl.when(i == 0)
        def _():
            loss_ref[...] = part
            dg_ref[...] = dgp

        @pl.when(i > 0)
        def _():
            loss_ref[...] += part
            dg_ref[...] += dgp

    row = lambda w: pl.BlockSpec((tm, w), lambda i: (i, 0))
    full = lambda a, b: pl.BlockSpec((a, b), lambda i: (0, 0))
    return _pallas_call(
        body, name=name, grid=(T // tm,),
        in_specs=[row(D), pl.BlockSpec((J, tm, bf), lambda i: (0, i, 0)), full(J * bf, D), full(1, D), row(D)],
        out_specs=[full(1, LANES), row(D), full(1, D)],
        out_shape=[jax.ShapeDtypeStruct((1, LANES), F32), jax.ShapeDtypeStruct((T, D), F32),
                   jax.ShapeDtypeStruct((1, D), F32)],
        compiler_params=_params(56, 1),
    )(x, A, w2, gf, target)


def _row_tile(rows):
    for cand in (256, 176, 128, 64, 32, 16):
        if rows % cand == 0:
            return cand
    return rows


def _adamw(w, m, v, parts, name):
    R, C = w.shape
    P = parts.shape[0]
    tr = _row_tile(R)
    c1 = 1.0 - ADAM_B1 ** ADAM_STEP
    c2 = 1.0 - ADAM_B2 ** ADAM_STEP

    def body(w_ref, m_ref, v_ref, p_ref, g_ref, d_ref, nm_ref, nv_ref):
        g = p_ref[0].astype(F32)
        for s in range(1, P):
            g = g + p_ref[s].astype(F32)
        wv = w_ref[...]
        mn = ADAM_B1 * m_ref[...] + (1.0 - ADAM_B1) * g
        vn = ADAM_B2 * v_ref[...] + (1.0 - ADAM_B2) * (g * g)
        g_ref[...] = g
        nm_ref[...] = mn
        nv_ref[...] = vn
        d_ref[...] = -ADAM_LR * ((mn / c1) / (jnp.sqrt(vn / c2) + ADAM_EPS) + ADAM_WD * wv)

    blk = pl.BlockSpec((tr, C), lambda i: (i, 0))
    out = jax.ShapeDtypeStruct((R, C), F32)
    return _pallas_call(
        body, name=name, grid=(R // tr,),
        in_specs=[blk, blk, blk, pl.BlockSpec((P, tr, C), lambda i: (0, i, 0))],
        out_specs=[blk, blk, blk, blk],
        out_shape=[out, out, out, out],
        compiler_params=_params(32, 1),
    )(w, m, v, parts)


def _position():
    return lax.axis_index("x"), lax.axis_index("y"), lax.axis_index("c")


def _flat(px, py, pc):
    return 4 * px + 2 * py + pc


def _row_halves(rows, dtype):
    tile = _SUBLANES * (4 // jnp.dtype(dtype).itemsize)
    half = rows // 2 // tile * tile
    assert half > 0
    return (0, half), (half, rows - half)


def _gather_body(ins, outs, send_sems, recv_sems, local_sems, handshake):
    n = len(ins)
    x, y, c = _position()
    me, sibling = (x, y, c), (x, y, 1 - c)
    across_x, across_y, diagonal = (1 - x, y), (x, 1 - y), (1 - x, 1 - y)
    if handshake:
        _handshake([sibling] + [(*chip, cc) for chip in (across_x, across_y, diagonal) for cc in (c, 1 - c)])

    def copy(a, k, block, to, rows=None, src=None):
        dst = outs[a].at[_flat(*block)]
        if rows is not None:
            dst = dst.at[pl.ds(*rows)]
        return pltpu.make_async_remote_copy(
            src_ref=dst if src is None else src, dst_ref=dst,
            send_sem=send_sems.at[a, k], recv_sem=recv_sems.at[a, k],
            device_id=to, device_id_type=MESH)

    halves = [_row_halves(ins[a].shape[0], ins[a].dtype) for a in range(n)]
    mine = [pltpu.make_async_copy(ins[a], outs[a].at[_flat(*me)], local_sems.at[a]) for a in range(n)]
    for cp in mine:
        cp.start()
    sent = []

    def start(cp):
        cp.start()
        sent.append(cp)

    for a in range(n):
        start(copy(a, 0, me, sibling, src=ins[a]))
        start(copy(a, 1, me, (*across_x, c), src=ins[a]))
        start(copy(a, 2, me, (*across_y, c), src=ins[a]))
    for a in range(n):
        low, high = halves[a]
        copy(a, 1, (*across_x, c), me).wait_recv()
        start(copy(a, 3, (*across_x, c), (*across_y, c), rows=low))
        start(copy(a, 5, (*across_x, c), sibling))
        copy(a, 2, (*across_y, c), me).wait_recv()
        start(copy(a, 4, (*across_y, c), (*across_x, c), rows=high))
        start(copy(a, 6, (*across_y, c), sibling))
    for a in range(n):
        low, high = halves[a]
        copy(a, 3, (*diagonal, c), me, rows=low).wait_recv()
        start(copy(a, 7, (*diagonal, c), sibling, rows=low))
        copy(a, 4, (*diagonal, c), me, rows=high).wait_recv()
        start(copy(a, 8, (*diagonal, c), sibling, rows=high))
    for a in range(n):
        low, high = halves[a]
        copy(a, 0, sibling, me).wait_recv()
        copy(a, 5, (*across_x, 1 - c), me).wait_recv()
        copy(a, 6, (*across_y, 1 - c), me).wait_recv()
        copy(a, 7, (*diagonal, 1 - c), me, rows=low).wait_recv()
        copy(a, 8, (*diagonal, 1 - c), me, rows=high).wait_recv()
    for cp in sent:
        cp.wait_send()
    for cp in mine:
        cp.wait()


_GATHER_SLOTS = 9


def _gather_scratch(n):
    return [pltpu.SemaphoreType.DMA((n, _GATHER_SLOTS)), pltpu.SemaphoreType.DMA((n, _GATHER_SLOTS)),
            pltpu.SemaphoreType.DMA((n,))]


def _all_gather(shards, name):
    n = len(shards)

    def body(*refs):
        _gather_body(refs[:n], refs[n:2 * n], *refs[2 * n:], handshake=False)

    hbm = pl.BlockSpec(memory_space=pltpu.HBM)
    return _pallas_call(
        body, name=name,
        in_specs=[hbm] * n, out_specs=[hbm] * n,
        out_shape=[jax.ShapeDtypeStruct((N_DEV,) + s.shape, s.dtype) for s in shards],
        scratch_shapes=_gather_scratch(n),
    )(*shards)


def _handshake(peers):
    barrier = pltpu.get_barrier_semaphore()
    for peer in peers:
        pl.semaphore_signal(barrier, inc=1, device_id=peer, device_id_type=MESH)
    pl.semaphore_wait(barrier, len(peers))


def _sequencer_call(body, name, collective_id, out_type, scratch_types, operands):
    return pl.kernel(
        body, name=name, out_type=out_type,
        mesh=plsc.ScalarSubcoreMesh(axis_name="sequencer", num_cores=1),
        scratch_types=scratch_types,
        compiler_params=pltpu.CompilerParams(collective_id=collective_id),
    )(*operands)


def _seq_all_gather(shards, name, collective_id, after):
    n = len(shards)

    def body(*refs):
        _gather_body(refs[:n], refs[n + 1:2 * n + 1], *refs[2 * n + 1:], handshake=True)

    return _sequencer_call(
        body, name, collective_id,
        [jax.ShapeDtypeStruct((N_DEV,) + s.shape, s.dtype) for s in shards],
        _gather_scratch(n), list(shards) + [after])


def _seq_to_sibling(parts, name, collective_id, after):
    n = len(parts)

    def body(*refs):
        ins, outs = refs[:n], refs[n + len(after):2 * n + len(after)]
        send_sems, recv_sems = refs[2 * n + len(after):]
        x, y, c = _position()
        sibling = (x, y, 1 - c)
        _handshake([sibling])
        sent = []
        for a in range(n):
            for q in range(N_CHIPS):
                cp = pltpu.make_async_remote_copy(
                    src_ref=ins[a].at[2 * q + (1 - c)], dst_ref=outs[a].at[q],
                    send_sem=send_sems.at[a, q], recv_sem=recv_sems.at[a, q],
                    device_id=sibling, device_id_type=MESH)
                cp.start()
                sent.append(cp)
        for cp in sent:
            cp.wait_recv()
        for cp in sent:
            cp.wait_send()

    return _sequencer_call(
        body, name, collective_id,
        [jax.ShapeDtypeStruct((N_CHIPS,) + p.shape[1:], p.dtype) for p in parts],
        [pltpu.SemaphoreType.DMA((n, N_CHIPS)), pltpu.SemaphoreType.DMA((n, N_CHIPS))],
        list(parts) + list(after))


def _seq_to_chips(partials, name, collective_id):
    n = len(partials)

    def body(*refs):
        ins, outs = refs[:n], refs[n:2 * n]
        send_sems, recv_sems, local_sems = refs[2 * n:]
        x, y, c = _position()
        my_chip = 2 * x + y
        chips = [(1 - x, y), (x, 1 - y), (1 - x, 1 - y)]
        _handshake([(*chip, c) for chip in chips])
        mine = [pltpu.make_async_copy(ins[a].at[my_chip], outs[a].at[my_chip], local_sems.at[a]) for a in range(n)]
        for cp in mine:
            cp.start()
        sent = []
        for a in range(n):
            for j, (px, py) in enumerate(chips):
                cp = pltpu.make_async_remote_copy(
                    src_ref=ins[a].at[2 * px + py], dst_ref=outs[a].at[my_chip],
                    send_sem=send_sems.at[a, j], recv_sem=recv_sems.at[a, j],
                    device_id=(px, py, c), device_id_type=MESH)
                cp.start()
                sent.append(cp)
        for cp in sent:
            cp.wait_recv()
        for cp in sent:
            cp.wait_send()
        for cp in mine:
            cp.wait()

    return _sequencer_call(
        body, name, collective_id,
        [jax.ShapeDtypeStruct(p.shape, p.dtype) for p in partials],
        [pltpu.SemaphoreType.DMA((n, 3)), pltpu.SemaphoreType.DMA((n, 3)), pltpu.SemaphoreType.DMA((n,))],
        list(partials))


def _pair_add(parts, recvs, name, after=()):
    n = len(parts)
    core = lax.axis_index("c").astype(jnp.int32).reshape(1)

    def body(c_ref, *refs):
        ps, rs, outs = refs[:n], refs[n:2 * n], refs[2 * n + len(after):]
        for p_ref, r_ref, o_ref in zip(ps, rs, outs):
            o_ref[...] = (p_ref[...].astype(F32) + r_ref[...].astype(F32)).astype(o_ref.dtype)

    mine = lambda p: pl.BlockSpec((None,) + p.shape[1:], lambda q, c: (2 * q + c[0], 0, 0))
    blk = lambda p: pl.BlockSpec((None,) + p.shape[1:], lambda q, c: (q, 0, 0))
    return pl.pallas_call(
        body, name=name,
        grid_spec=pltpu.PrefetchScalarGridSpec(
            num_scalar_prefetch=1, grid=(N_CHIPS,),
            in_specs=[mine(p) for p in parts] + [blk(p) for p in parts] + [_UNREAD] * len(after),
            out_specs=[blk(p) for p in parts]),
        out_shape=[pltpu.HBM((N_CHIPS,) + p.shape[1:], p.dtype) for p in parts],
        compiler_params=_params(40, 1),
    )(core, *[pltpu.with_memory_space_constraint(a, pltpu.HBM) for a in (*parts, *recvs, *after)])


class _Reduced(NamedTuple):
    partials: list
    reduced: list


def _blocks(g):
    return g.reshape(N_DEV, -1, g.shape[-1])


def _reduce_scatter(parts, tag, ids, after=(), add_after=()):
    from_sibling = _seq_to_sibling(parts, "rs_sibling_" + tag, ids[0], after)
    partials = _pair_add(parts, from_sibling, "rs_add_" + tag, add_after)
    return _Reduced(partials, _seq_to_chips(partials, "rs_chips_" + tag, ids[1]))


_SMALL = ("ffn1_norm", "mix_norm", "conv_b", "conv_ln_g", "conv_ln_b", "forget_b", "out_norm_conv",
          "out_norm_attn", "ffn2_norm", "final_norm")
_PACK_WIDTH = 2 * D_CONV
_SLOT = dict(ffn1_norm=(0, 0), mix_norm=(1, 0), ffn2_norm=(2, 0), final_norm=(3, 0), conv_b=(4, 0),
             conv_ln_g=(4, D_CONV), conv_ln_b=(5, 0), out_norm_conv=(5, D_CONV), out_norm_attn=(6, 0),
             forget_b=(6, D_CONV))
_LOSS_ROW = 7
_CONV_ROW0 = 8
_PACK_ROWS = _CONV_ROW0 + CONV_HALO


def _pack_small(small, name):
    arrays = [small[n] for n in _SMALL] + [small["conv_w"], small["loss"]]

    def body(*refs):
        out = refs[-1]
        out[...] = jnp.zeros_like(out)
        for n, ref in zip(_SMALL, refs):
            row, lane = _SLOT[n]
            out[row:row + 1, lane:lane + ref.shape[1]] = ref[...]
        out[_CONV_ROW0:, :D_CONV] = refs[len(_SMALL)][...]
        out[_LOSS_ROW:_LOSS_ROW + 1, :LANES] = refs[len(_SMALL) + 1][...]

    return _pallas_call(body, name=name, out_shape=jax.ShapeDtypeStruct((_PACK_ROWS, _PACK_WIDTH), F32))(*arrays)


def _adamw_small(gathered, w, m, v, name):
    c1 = 1.0 - ADAM_B1 ** ADAM_STEP
    c2 = 1.0 - ADAM_B2 ** ADAM_STEP
    k = len(_SMALL)

    def body(g_ref, *refs):
        ws, ms, vs = refs[:k], refs[k:2 * k], refs[2 * k:3 * k]
        outs = refs[3 * k:]
        total = g_ref[0]
        for s in range(1, N_DEV):
            total = total + g_ref[s]
        for i, n in enumerate(_SMALL):
            row, lane = _SLOT[n]
            width = ws[i].shape[1]
            g = total[row:row + 1, lane:lane + width]
            mn = ADAM_B1 * ms[i][...] + (1.0 - ADAM_B1) * g
            vn = ADAM_B2 * vs[i][...] + (1.0 - ADAM_B2) * (g * g)
            o_g, o_d, o_m, o_v = outs[4 * i:4 * i + 4]
            o_g[...] = g
            o_m[...] = mn
            o_v[...] = vn
            o_d[...] = -ADAM_LR * ((mn / c1) / (jnp.sqrt(vn / c2) + ADAM_EPS) + ADAM_WD * ws[i][...])
        outs[4 * k][...] = total[_CONV_ROW0:, :D_CONV]
        outs[4 * k + 1][...] = total[_LOSS_ROW:_LOSS_ROW + 1, :LANES]

    shapes = []
    for n in _SMALL:
        shapes += [jax.ShapeDtypeStruct(w[n].shape, F32)] * 4
    shapes.append(jax.ShapeDtypeStruct((CONV_HALO, D_CONV), F32))
    shapes.append(jax.ShapeDtypeStruct((1, LANES), F32))
    res = _pallas_call(body, name=name, out_shape=shapes)(
        gathered, *[w[n] for n in _SMALL], *[m[n] for n in _SMALL], *[v[n] for n in _SMALL])
    return {n: res[4 * i:4 * i + 4] for i, n in enumerate(_SMALL)}, res[4 * k], res[4 * k + 1]


def _local_step(x, target, norms, shard):
    D = x.shape[1]
    J = N_DEV // 2
    as13 = lambda g: g.reshape(2, J, g.shape[1], D)

    (g13_1,) = _all_gather([shard["ffn1_w13"]], "gather_ffn1_w13")
    (g2_1,) = _seq_all_gather([shard["ffn1_w2"]], "gather_ffn1_w2", 10, after=g13_1)
    w13_1 = as13(g13_1)
    G1, U1, A1 = _ffn_up(x, norms["ffn1_norm"], w13_1, "ffn1_up")
    gin, gconv = _seq_all_gather([shard["w_in"], shard["conv_w"]], "gather_mix", 1, after=G1)
    w2_1 = g2_1.reshape(-1, D)
    x1 = _ffn_down(x, A1, w2_1, "ffn1_down")
    gout, g13_2, g2_2 = _seq_all_gather([shard["w_out"], shard["ffn2_w13"], shard["ffn2_w2"]], "gather_ffn2", 2,
                                        after=x1)
    wout = gout.reshape(-1, D)
    conv_w32 = jnp.pad(gconv.transpose(1, 0, 2).reshape(CONV_TAPS, D_CONV), ((0, CONV_HALO - CONV_TAPS), (0, 0)))

    ag, k, v, qT, kT, vT, fl = _inproj_fwd(x1, norms["mix_norm"], gin, "inproj_fwd")
    cum, cumT = _forget_fwd(fl, norms["forget_b"], "forget_fwd")
    yc, c = _conv_fwd(ag, conv_w32, norms["conv_b"], norms["conv_ln_g"], norms["conv_ln_b"], "conv_fwd")
    o, lseT = _attn_fwd(qT, k, vT, cum, cumT, "attn_fwd")
    x2 = _outproj_fwd(x1, c, o, norms["out_norm_conv"], norms["out_norm_attn"], wout, "outproj_fwd")
    w13_2, w2_2 = as13(g13_2), g2_2.reshape(-1, D)
    G2, U2, A2 = _ffn_up(x2, norms["ffn2_norm"], w13_2, "ffn2_up")
    loss, dx3, d_final = _ffn_down_loss(x2, A2, w2_2, norms["final_norm"], target, "ffn2_down_loss")

    dw2_2 = _ffn_w2_grad(dx3, A2, "ffn2_w2_grad")
    dx2, d_ffn2n, h3, dG2, dU2 = _ffn_bwd_act(x2, norms["ffn2_norm"], dx3, G2, U2, w13_2, w2_2, "ffn2_bwd_act")
    dw13_2 = _ffn_w13_grad(h3, dG2, dU2, "ffn2_w13_grad")
    dc, dobT, deltaT, dwout, d_onc, d_ona = _outproj_bwd(
        dx2, c, o, norms["out_norm_conv"], norms["out_norm_attn"], wout, "outproj_bwd")
    red_ffn2 = _reduce_scatter([_blocks(dw13_2), _blocks(dw2_2)], "ffn2", (3, 4), add_after=(dc,))
    dqT, dkT, dvT, dcum = _attn_bwd(qT, k, kT, v, dobT, lseT, deltaT, cum, cumT, "attn_bwd",
                                    after=red_ffn2.partials)
    dfl, d_fb = _forget_bwd(dcum, fl, norms["forget_b"], "forget_bwd")
    dag, d_convw, d_cb, d_lg, d_lb = _conv_bwd(dc, yc, ag, conv_w32, norms["conv_ln_g"], norms["conv_ln_b"], "conv_bwd")
    dx1, d_mixn, h2 = _inproj_bwd_act(x1, norms["mix_norm"], dx2, dag, dqT, dkT, dvT, dfl, gin, "inproj_bwd_act")
    dw2_1 = _ffn_w2_grad(dx1, A1, "ffn1_w2_grad")
    early = [_blocks(dwout), _blocks(dw2_1)]
    sib_early = _seq_to_sibling(early, "rs_sibling_mix_early", 11, red_ffn2.reduced[:1])
    dwin_blocks = _inproj_bwd_weights(h2, dag, dqT, dkT, dvT, dfl, "inproj_bwd_weights")
    sib_w_in = _seq_to_sibling([dwin_blocks], "rs_sibling_mix", 5, sib_early[:1])
    mix_partials = _pair_add([dwin_blocks] + early, sib_w_in + sib_early, "rs_add_mix")
    red_mix = _Reduced(mix_partials, _seq_to_chips(mix_partials, "rs_chips_mix", 6))
    dx, d_ffn1n, h1, dG1, dU1 = _ffn_bwd_act(x, norms["ffn1_norm"], dx1, G1, U1, w13_1, w2_1, "ffn1_bwd_act",
                                             after=red_mix.partials)
    small = dict(ffn1_norm=d_ffn1n, mix_norm=d_mixn, conv_b=d_cb, conv_ln_g=d_lg, conv_ln_b=d_lb,
                 forget_b=d_fb, out_norm_conv=d_onc, out_norm_attn=d_ona, ffn2_norm=d_ffn2n,
                 final_norm=d_final, conv_w=d_convw, loss=loss)
    packed_small = _pack_small(small, "pack_small_grads")
    (gathered_small,) = _seq_all_gather([packed_small], "gather_small_grads", 9, after=red_mix.partials[0])
    dw13_1 = _ffn_w13_grad(h1, dG1, dU1, "ffn1_w13_grad")
    red_w13_1 = _reduce_scatter([_blocks(dw13_1)], "ffn1_w13", (7, 8), after=[red_mix.reduced[0], gathered_small])
    big = dict(ffn1_w13=red_w13_1.reduced[0], ffn1_w2=red_mix.reduced[2], w_in=red_mix.reduced[0],
               w_out=red_mix.reduced[1], ffn2_w13=red_ffn2.reduced[0], ffn2_w2=red_ffn2.reduced[1])
    return dx, gathered_small, big


_BIG = ("ffn1_w13", "ffn1_w2", "w_in", "w_out", "ffn2_w13", "ffn2_w2")
_TRANSPOSED = ("ffn1_w13", "ffn2_w13", "w_in")
_ORDER = ("ffn1_norm", "ffn1_w13", "ffn1_w2", "mix_norm", "w_in", "conv_w", "conv_b", "conv_ln_g", "conv_ln_b",
          "forget_b", "out_norm_conv", "out_norm_attn", "w_out", "ffn2_norm", "ffn2_w13", "ffn2_w2", "final_norm")


def kernel(x, ffn1_norm, ffn1_w13, ffn1_w2, mix_norm, w_in, conv_w, conv_b, conv_ln_g, conv_ln_b, forget_b, out_norm_conv, out_norm_attn, w_out, ffn2_norm, ffn2_w13, ffn2_w2, final_norm, loss_target, m_ffn1_norm, m_ffn1_w13, m_ffn1_w2, m_mix_norm, m_w_in, m_conv_w, m_conv_b, m_conv_ln_g, m_conv_ln_b, m_forget_b, m_out_norm_conv, m_out_norm_attn, m_w_out, m_ffn2_norm, m_ffn2_w13, m_ffn2_w2, m_final_norm, v_ffn1_norm, v_ffn1_w13, v_ffn1_w2, v_mix_norm, v_w_in, v_conv_w, v_conv_b, v_conv_ln_g, v_conv_ln_b, v_forget_b, v_out_norm_conv, v_out_norm_attn, v_w_out, v_ffn2_norm, v_ffn2_w13, v_ffn2_w2, v_final_norm):
    w = dict(ffn1_norm=ffn1_norm, ffn1_w13=ffn1_w13, ffn1_w2=ffn1_w2, mix_norm=mix_norm, w_in=w_in, conv_w=conv_w,
             conv_b=conv_b, conv_ln_g=conv_ln_g, conv_ln_b=conv_ln_b, forget_b=forget_b, out_norm_conv=out_norm_conv,
             out_norm_attn=out_norm_attn, w_out=w_out, ffn2_norm=ffn2_norm, ffn2_w13=ffn2_w13, ffn2_w2=ffn2_w2,
             final_norm=final_norm)
    m = dict(ffn1_norm=m_ffn1_norm, ffn1_w13=m_ffn1_w13, ffn1_w2=m_ffn1_w2, mix_norm=m_mix_norm, w_in=m_w_in,
             conv_w=m_conv_w, conv_b=m_conv_b, conv_ln_g=m_conv_ln_g, conv_ln_b=m_conv_ln_b, forget_b=m_forget_b,
             out_norm_conv=m_out_norm_conv, out_norm_attn=m_out_norm_attn, w_out=m_w_out, ffn2_norm=m_ffn2_norm,
             ffn2_w13=m_ffn2_w13, ffn2_w2=m_ffn2_w2, final_norm=m_final_norm)
    v = dict(ffn1_norm=v_ffn1_norm, ffn1_w13=v_ffn1_w13, ffn1_w2=v_ffn1_w2, mix_norm=v_mix_norm, w_in=v_w_in,
             conv_w=v_conv_w, conv_b=v_conv_b, conv_ln_g=v_conv_ln_g, conv_ln_b=v_conv_ln_b, forget_b=v_forget_b,
             out_norm_conv=v_out_norm_conv, out_norm_attn=v_out_norm_attn, w_out=v_w_out, ffn2_norm=v_ffn2_norm,
             ffn2_w13=v_ffn2_w13, ffn2_w2=v_ffn2_w2, final_norm=v_final_norm)
    shapes = {n: a.shape for n, a in w.items()}
    T, D = x.shape[1], x.shape[2]
    def two(n, a):
        if a.ndim != 3:
            return a.reshape(1, -1)
        a = a.reshape(a.shape[-2], a.shape[-1])
        return a.T if n in _TRANSPOSED else a

    w2d = {n: two(n, a) for n, a in w.items()}
    m2d = {n: two(n, a) for n, a in m.items()}
    v2d = {n: two(n, a) for n, a in v.items()}

    shard = {n: w2d[n].astype(MXU) for n in _BIG}
    shard["conv_w"] = w2d["conv_w"]
    norms = {n: w2d[n] for n in _SMALL}
    norms["forget_b"] = jnp.pad(w2d["forget_b"], ((0, 0), (0, LANES - N_HEADS)))
    dx, gathered_small, big = _local_step(x[0], loss_target[0], norms, shard)

    grads, deltas, new_m, new_v = {}, {}, {}, {}
    for n in _BIG:
        g, d, nm, nv = _adamw(w2d[n], m2d[n], v2d[n], big[n], "adamw_" + n)
        grads[n], deltas[n], new_m[n], new_v[n] = g, d, nm, nv

    small_out, conv_g_full, loss = _adamw_small(gathered_small, w2d, m2d, v2d, "adamw_small")
    for n in _SMALL:
        grads[n], deltas[n], new_m[n], new_v[n] = small_out[n]
    conv_g_full = conv_g_full[:CONV_TAPS]
    xi, yi, ci = _position()
    cw = shapes["conv_w"][-1]
    conv_g_mine = lax.dynamic_slice_in_dim(conv_g_full, _flat(xi, yi, ci) * cw, cw, axis=1)
    g, d, nm, nv = _adamw(w2d["conv_w"], m2d["conv_w"], v2d["conv_w"], conv_g_mine[None], "adamw_conv_w")
    grads["conv_w"], deltas["conv_w"], new_m["conv_w"], new_v["conv_w"] = g, d, nm, nv

    shaped = lambda dct: [(dct[n].T if n in _TRANSPOSED else dct[n]).reshape(shapes[n]) for n in _ORDER]
    return (loss[0, 0], dx[None], *shaped(grads), *shaped(deltas), *shaped(new_m), *shaped(new_v))
```

```python
from typing import NamedTuple

import jax
import jax.numpy as jnp
from jax import lax
from jax.experimental import pallas as pl
from jax.experimental.pallas import tpu as pltpu
from jax.experimental.pallas import tpu_sc as plsc

F32 = jnp.float32
MXU = jnp.bfloat16
EPS = 1e-6
N_HEADS = 8
HEAD_DIM = 64
D_CONV = 512
D_ATTN = N_HEADS * HEAD_DIM
CONV_TAPS = 31
CONV_HALO = 32
SCALE = HEAD_DIM ** -0.5
NEG = -1e30
LANES = 128
N_DEV = 8
N_CHIPS = N_DEV // 2
MESH = pl.DeviceIdType.MESH
MIB = 1 << 20

ADAM_LR = 0.001
ADAM_B1 = 0.9
ADAM_B2 = 0.999
ADAM_EPS = 1e-08
ADAM_WD = 0.01
ADAM_STEP = 10


_UNREAD = pl.BlockSpec(memory_space=pl.ANY)


def _pallas_call(body, *, out_shape, scalars=0, **kwargs):
    in_hbm = lambda s: pltpu.HBM(s.shape, s.dtype)
    outs = [in_hbm(s) for s in out_shape] if isinstance(out_shape, (list, tuple)) else in_hbm(out_shape)
    call = pl.pallas_call(body, out_shape=outs, **kwargs)
    return lambda *operands: call(
        *operands[:scalars], *[pltpu.with_memory_space_constraint(a, pltpu.HBM) for a in operands[scalars:]])


def _params(vmem_mib, n_axes):
    return pltpu.CompilerParams(dimension_semantics=("arbitrary",) * n_axes, vmem_limit_bytes=vmem_mib * MIB)


def _mm(a, b):
    return jnp.dot(a, b, preferred_element_type=F32)


def _mm_nt(a, b):
    return lax.dot_general(a, b, (((1,), (1,)), ((), ())), preferred_element_type=F32)


def _mm_tn(a, b):
    return lax.dot_general(a, b, (((0,), (0,)), ((), ())), preferred_element_type=F32)


def _rms_fwd(x, g):
    r = lax.rsqrt(jnp.mean(x * x, axis=-1, keepdims=True) + EPS)
    return x * r * g, r


def _rms_bwd(x, r, g, dy):
    gdy = dy * g
    dx = r * gdy - x * (r * r * r) * jnp.mean(x * gdy, axis=-1, keepdims=True)
    dg = jnp.sum(dy * x * r, axis=0, keepdims=True)
    return dx, dg


def _silu_grad(z, sz):
    return sz * (1.0 + z * (1.0 - sz))


def _three_terms(x):
    x1 = x.astype(jnp.bfloat16)
    r1 = x - x1.astype(F32)
    x2 = r1.astype(jnp.bfloat16)
    x3 = (r1 - x2.astype(F32)).astype(jnp.bfloat16)
    return x1, x2, x3


def _exact_tri_dot(tri, x):
    x1, x2, x3 = _three_terms(x)
    return _mm(tri, x1) + _mm(tri, x2) + _mm(tri, x3)


def _exact_dot_01(x, sel):
    x1, x2, x3 = _three_terms(x)
    return _mm(x1, sel) + _mm(x2, sel) + _mm(x3, sel)


def _tile(n, want):
    t = min(n, want)
    assert n % t == 0
    return t


_FFN_CHUNK = 256


def _ffn_up(x, g, w13, name):
    T, D = x.shape
    _, J, bf, _ = w13.shape
    tm = _tile(T, 512)
    I = T // tm

    def body(x_ref, g_ref, w13_ref, G_ref, U_ref, A_ref, h_s):
        j = pl.program_id(0)
        i = pl.program_id(1)
        rows = pl.ds(pl.multiple_of(i * tm, tm), tm)

        @pl.when(j == 0)
        def _():
            h, _ = _rms_fwd(x_ref[...], g_ref[...])
            h_s[rows, :] = h.astype(MXU)

        chunks = [slice(r0, r0 + _FFN_CHUNK) for r0 in range(0, tm, _FFN_CHUNK)]
        hbs = [h_s[pl.ds(pl.multiple_of(i * tm + rs.start, _FFN_CHUNK), _FFN_CHUNK), :] for rs in chunks]
        GU = [(_mm_nt(hb, w13_ref[0]), _mm_nt(hb, w13_ref[1])) for hb in hbs]
        for rs, (G, U) in zip(chunks, GU):
            G_ref[rs, :] = G.astype(MXU)
            U_ref[rs, :] = U.astype(MXU)
            A_ref[rs, :] = (G * jax.nn.sigmoid(G) * U).astype(MXU)

    blk = pl.BlockSpec((None, tm, bf), lambda j, i: (j, i, 0))
    hid = jax.ShapeDtypeStruct((J, T, bf), MXU)
    return _pallas_call(
        body, name=name, grid=(J, I),
        in_specs=[pl.BlockSpec((tm, D), lambda j, i: (jnp.where(j == 0, i, I - 1), 0)),
                  pl.BlockSpec((1, D), lambda j, i: (0, 0)),
                  pl.BlockSpec((2, None, bf, D), lambda j, i: (0, j, 0, 0))],
        out_specs=[blk, blk, blk],
        out_shape=[hid, hid, hid],
        scratch_shapes=[pltpu.VMEM((T, D), MXU)],
        compiler_params=_params(40, 2),
    )(x, g, w13)


def _ffn_down(x, A, w2, name):
    T, D = x.shape
    J, _, bf = A.shape
    tm = _tile(T, 512)

    def body(x_ref, A_ref, w2_ref, xo_ref):
        f = _mm(A_ref[0], w2_ref[0:bf, :])
        for j in range(1, J):
            f = f + _mm(A_ref[j], w2_ref[j * bf:(j + 1) * bf, :])
        xo_ref[...] = x_ref[...] + 0.5 * f

    row = pl.BlockSpec((tm, D), lambda i: (i, 0))
    return _pallas_call(
        body, name=name, grid=(T // tm,),
        in_specs=[row, pl.BlockSpec((J, tm, bf), lambda i: (0, i, 0)), pl.BlockSpec((J * bf, D), lambda i: (0, 0))],
        out_specs=row,
        out_shape=jax.ShapeDtypeStruct((T, D), F32),
        compiler_params=_params(48, 1),
    )(x, A, w2)


def _ffn_bwd_act(x, g, dy, Gs, Us, w13, w2, name, after=()):
    T, D = x.shape
    _, J, bf, _ = w13.shape
    tm = _tile(T, 512)
    I = T // tm

    def body(x_ref, g_ref, dy_ref, G_ref, U_ref, w13_ref, w2_ref, *rest):
        dx_ref, dg_ref, h_ref, dG_ref, dU_ref, dh_s, dF_s, h_s = rest[len(after):]
        j = pl.program_id(0)
        i = pl.program_id(1)
        rows = pl.ds(pl.multiple_of(i * tm, tm), tm)

        @pl.when(j == 0)
        def _():
            h, _ = _rms_fwd(x_ref[...], g_ref[...])
            hb = h.astype(MXU)
            h_s[rows, :] = hb
            h_ref[...] = hb
            dF_s[rows, :] = (0.5 * dy_ref[...]).astype(MXU)
            dh_s[rows, :] = jnp.zeros((tm, D), F32)

        chunks = [slice(r0, r0 + _FFN_CHUNK) for r0 in range(0, tm, _FFN_CHUNK)]
        crows = [pl.ds(pl.multiple_of(i * tm + rs.start, _FFN_CHUNK), _FFN_CHUNK) for rs in chunks]
        dAs = [_mm_nt(dF_s[cr, :], w2_ref[...]) for cr in crows]
        for rs, cr, dA in zip(chunks, crows, dAs):
            G = G_ref[rs, :].astype(F32)
            U = U_ref[rs, :].astype(F32)
            sg = jax.nn.sigmoid(G)
            s = G * sg
            dUb = (dA * s).astype(MXU)
            dGb = (dA * U * _silu_grad(G, sg)).astype(MXU)
            dG_ref[rs, :] = dGb
            dU_ref[rs, :] = dUb
            dh_s[cr, :] += _mm(dGb, w13_ref[0]) + _mm(dUb, w13_ref[1])

        @pl.when(j == J - 1)
        def _():
            xv = x_ref[...]
            gv = g_ref[...]
            _, r = _rms_fwd(xv, gv)
            dxn, dgp = _rms_bwd(xv, r, gv, dh_s[rows, :])
            dx_ref[...] = dy_ref[...] + dxn

            @pl.when(i == 0)
            def _():
                dg_ref[...] = dgp

            @pl.when(i > 0)
            def _():
                dg_ref[...] += dgp

    ends = lambda j, i: (jnp.where((j == 0) | (j == J - 1), i, I - 1), 0)
    blk = pl.BlockSpec((None, tm, bf), lambda j, i: (j, i, 0))
    hid = jax.ShapeDtypeStruct((J, T, bf), MXU)
    return _pallas_call(
        body, name=name, grid=(J, I),
        in_specs=[pl.BlockSpec((tm, D), ends), pl.BlockSpec((1, D), lambda j, i: (0, 0)), pl.BlockSpec((tm, D), ends),
                  blk, blk, pl.BlockSpec((2, None, bf, D), lambda j, i: (0, j, 0, 0)),
                  pl.BlockSpec((bf, D), lambda j, i: (j, 0))] + [_UNREAD] * len(after),
        out_specs=[pl.BlockSpec((tm, D), lambda j, i: (jnp.where(j == J - 1, i, 0), 0)),
                   pl.BlockSpec((1, D), lambda j, i: (0, 0)),
                   pl.BlockSpec((tm, D), lambda j, i: (jnp.where(j == 0, i, I - 1), 0)), blk, blk],
        out_shape=[jax.ShapeDtypeStruct((T, D), F32), jax.ShapeDtypeStruct((1, D), F32),
                   jax.ShapeDtypeStruct((T, D), MXU), hid, hid],
        scratch_shapes=[pltpu.VMEM((T, D), F32), pltpu.VMEM((T, D), MXU), pltpu.VMEM((T, D), MXU)],
        compiler_params=_params(58, 2),
    )(x, g, dy, Gs, Us, w13, w2, *after)


def _ffn_w13_grad(h, dG, dU, parity, name):
    T, D = h.shape
    J, _, bf = dG.shape
    half = J // 2

    def body(par_ref, h_ref, dG_ref, dU_ref, dw_ref):
        q = pl.program_id(0)

        @pl.when(q < half)
        def _():
            dw_ref[...] = _mm_tn(dG_ref[...], h_ref[...]).astype(dw_ref.dtype)

        @pl.when(q >= half)
        def _():
            dw_ref[...] = _mm_tn(dU_ref[...], h_ref[...]).astype(dw_ref.dtype)

    gate = pl.BlockSpec((None, T, bf), lambda q, par: (2 * jnp.minimum(q, half - 1) + par[0], 0, 0))
    up = pl.BlockSpec((None, T, bf), lambda q, par: (2 * jnp.maximum(q - half, 0) + par[0], 0, 0))
    return _pallas_call(
        body, name=name, scalars=1,
        grid_spec=pltpu.PrefetchScalarGridSpec(
            num_scalar_prefetch=1, grid=(J,),
            in_specs=[pl.BlockSpec((T, D), lambda q, par: (0, 0)), gate, up],
            out_specs=pl.BlockSpec((None, bf, D), lambda q, par: (q, 0, 0))),
        out_shape=jax.ShapeDtypeStruct((J, bf, D), MXU),
        compiler_params=_params(48, 1),
    )(parity.astype(jnp.int32).reshape(1), h, dG, dU)


def _ffn_w2_grad(dy, A, name, after=()):
    T, D = dy.shape
    J, _, bf = A.shape

    def body(dy_ref, A_ref, *rest):
        dw2_ref, dF_s = rest[len(after):]

        @pl.when(pl.program_id(0) == 0)
        def _():
            dF_s[...] = (0.5 * dy_ref[...]).astype(MXU)

        dw2_ref[...] = _mm_tn(A_ref[...], dF_s[...]).astype(dw2_ref.dtype)

    return _pallas_call(
        body, name=name, grid=(J,),
        in_specs=[pl.BlockSpec((T, D), lambda j: (0, 0)), pl.BlockSpec((None, T, bf), lambda j: (j, 0, 0))]
        + [_UNREAD] * len(after),
        out_specs=pl.BlockSpec((bf, D), lambda j: (j, 0)),
        out_shape=jax.ShapeDtypeStruct((J * bf, D), MXU),
        scratch_shapes=[pltpu.VMEM((T, D), MXU)],
        compiler_params=_params(48, 1),
    )(dy, A, *after)


_AG0, _Q0, _K0, _V0, _F0 = 0, 2 * D_CONV, 2 * D_CONV + D_ATTN, 2 * D_CONV + 2 * D_ATTN, 2 * D_CONV + 3 * D_ATTN
N_IN = _F0 + N_HEADS
N_IN_PAD = _F0 + LANES
_IN_BLOCK = N_IN // N_DEV


def _rows_from_blocks(blocks_ref, rows_ref):
    for p in range(N_DEV):
        rows_ref[_IN_BLOCK * p:_IN_BLOCK * (p + 1), :] = blocks_ref[p]
    rows_ref[N_IN:, :] = jnp.zeros((N_IN_PAD - N_IN, rows_ref.shape[1]), rows_ref.dtype)


def _inproj_fwd(x1, gm, win_blocks, name):
    T, D = x1.shape
    tm = _tile(T, 256)

    def body(x_ref, g_ref, wb_ref, ag_ref, k_ref, v_ref, qT_ref, kT_ref, vT_ref, fl_ref, w_ref):
        @pl.when(pl.program_id(0) == 0)
        def _():
            _rows_from_blocks(wb_ref, w_ref)

        h, _ = _rms_fwd(x_ref[...], g_ref[...])
        hb = h.astype(MXU)
        ag_ref[...] = _mm_nt(hb, w_ref[_AG0:_Q0, :])
        qT_ref[...] = (_mm_nt(hb, w_ref[_Q0:_K0, :]) * SCALE).T.astype(MXU)
        for c0, ref, refT in ((_K0, k_ref, kT_ref), (_V0, v_ref, vT_ref)):
            y = _mm_nt(hb, w_ref[c0:c0 + D_ATTN, :])
            ref[...] = y.astype(MXU)
            refT[...] = y.T.astype(MXU)
        fl_ref[...] = _mm_nt(hb, w_ref[_F0:N_IN_PAD, :])

    row = lambda w: pl.BlockSpec((tm, w), lambda i: (i, 0))
    col = pl.BlockSpec((D_ATTN, tm), lambda i: (0, i))
    std = jax.ShapeDtypeStruct((T, D_ATTN), MXU)
    trn = jax.ShapeDtypeStruct((D_ATTN, T), MXU)
    return _pallas_call(
        body, name=name, grid=(T // tm,),
        in_specs=[row(D), pl.BlockSpec((1, D), lambda i: (0, 0)),
                  pl.BlockSpec((N_DEV, _IN_BLOCK, D), lambda i: (0, 0, 0))],
        out_specs=[row(2 * D_CONV), row(D_ATTN), row(D_ATTN), col, col, col, row(LANES)],
        out_shape=[jax.ShapeDtypeStruct((T, 2 * D_CONV), F32), std, std, trn, trn, trn,
                   jax.ShapeDtypeStruct((T, LANES), F32)],
        scratch_shapes=[pltpu.VMEM((N_IN_PAD, D), MXU)],
        compiler_params=_params(40, 1),
    )(x1, gm, win_blocks)


def _inproj_bwd_act(x1, gm, dx2, dag, dqT, dkT, dvT, dfl, win_blocks, name):
    T, D = x1.shape
    tm = _tile(T, 256)

    def body(x_ref, g_ref, dx2_ref, dag_ref, dqT_ref, dkT_ref, dvT_ref, dfl_ref, wb_ref, dx1_ref, dg_ref, h_ref,
             w_ref):
        i = pl.program_id(0)

        @pl.when(i == 0)
        def _():
            _rows_from_blocks(wb_ref, w_ref)

        xv = x_ref[...]
        gv = g_ref[...]
        h, r = _rms_fwd(xv, gv)
        h_ref[...] = h.astype(MXU)
        dh = _mm(dag_ref[...], w_ref[_AG0:_Q0, :])
        for c0, ref in ((_Q0, dqT_ref), (_K0, dkT_ref), (_V0, dvT_ref)):
            dh = dh + _mm_tn(ref[...].astype(MXU), w_ref[c0:c0 + D_ATTN, :])
        dh = dh + _mm(dfl_ref[...].astype(MXU), w_ref[_F0:N_IN_PAD, :])
        dxn, dgp = _rms_bwd(xv, r, gv, dh)
        dx1_ref[...] = dx2_ref[...] + dxn

        @pl.when(i == 0)
        def _():
            dg_ref[...] = dgp

        @pl.when(i > 0)
        def _():
            dg_ref[...] += dgp

    row = lambda w: pl.BlockSpec((tm, w), lambda i: (i, 0))
    col = pl.BlockSpec((D_ATTN, tm), lambda i: (0, i))
    full = lambda a, b: pl.BlockSpec((a, b), lambda i: (0, 0))
    return _pallas_call(
        body, name=name, grid=(T // tm,),
        in_specs=[row(D), full(1, D), row(D), row(2 * D_CONV), col, col, col, row(LANES),
                  pl.BlockSpec((N_DEV, _IN_BLOCK, D), lambda i: (0, 0, 0))],
        out_specs=[row(D), full(1, D), row(D)],
        out_shape=[jax.ShapeDtypeStruct((T, D), F32), jax.ShapeDtypeStruct((1, D), F32),
                   jax.ShapeDtypeStruct((T, D), MXU)],
        scratch_shapes=[pltpu.VMEM((N_IN_PAD, D), MXU)],
        compiler_params=_params(40, 1),
    )(x1, gm, dx2, dag, dqT, dkT, dvT, dfl, win_blocks)


def _inproj_bwd_weights(h, dag, dqT, dkT, dvT, dfl, name, after=()):
    T, D = h.shape

    def body(h_ref, dag_ref, dqT_ref, dkT_ref, dvT_ref, dfl_ref, *rest):
        blocks_ref, dw_ref = rest[len(after):]
        hb = h_ref[...]
        dw_ref[_AG0:_Q0, :] = _mm_tn(dag_ref[...], hb).astype(dw_ref.dtype)
        for c0, ref in ((_Q0, dqT_ref), (_K0, dkT_ref), (_V0, dvT_ref)):
            dw_ref[c0:c0 + D_ATTN, :] = _mm(ref[...].astype(MXU), hb).astype(dw_ref.dtype)
        dw_ref[_F0:N_IN_PAD, :] = _mm_tn(dfl_ref[...].astype(MXU), hb).astype(dw_ref.dtype)
        for p in range(N_DEV):
            blocks_ref[p] = dw_ref[_IN_BLOCK * p:_IN_BLOCK * (p + 1), :]

    vmem = pl.BlockSpec(memory_space=pltpu.VMEM)
    return _pallas_call(
        body, name=name, in_specs=[vmem] * 6 + [_UNREAD] * len(after), out_specs=vmem,
        out_shape=jax.ShapeDtypeStruct((N_DEV, _IN_BLOCK, D), MXU),
        scratch_shapes=[pltpu.VMEM((N_IN_PAD, D), MXU)],
        compiler_params=pltpu.CompilerParams(vmem_limit_bytes=56 * MIB),
    )(h, dag, dqT, dkT, dvT, dfl, *after)


def _forget_fwd(fl, fbp, name):
    T = fl.shape[0]
    tb = _tile(T, 256)

    def body(fl_ref, fb_ref, cum_ref, cumT_ref):
        ri = lax.broadcasted_iota(jnp.int32, (tb, tb), 0)
        ci = lax.broadcasted_iota(jnp.int32, (tb, tb), 1)
        tri = (ri >= ci).astype(jnp.bfloat16)
        carry = jnp.zeros((1, LANES), F32)
        for b in range(T // tb):
            z = fl_ref[b * tb:(b + 1) * tb, :] + fb_ref[...]
            lf = jnp.minimum(z, 0.0) - jnp.log1p(jnp.exp(-jnp.abs(z)))
            c = _exact_tri_dot(tri, lf) + carry
            cum_ref[b * tb:(b + 1) * tb, :] = c
            carry = c[tb - 1:tb, :]
        cumT_ref[...] = cum_ref[...].T[:N_HEADS, :]

    return _pallas_call(
        body, name=name,
        out_shape=[jax.ShapeDtypeStruct((T, LANES), F32), jax.ShapeDtypeStruct((N_HEADS, T), F32)],
        compiler_params=pltpu.CompilerParams(vmem_limit_bytes=32 * MIB),
    )(fl, fbp)


def _forget_bwd(dcum, fl, fbp, name):
    T = fl.shape[0]
    tb = _tile(T, 256)

    def body(dc_ref, fl_ref, fb_ref, dfl_ref, dfb_ref):
        ri = lax.broadcasted_iota(jnp.int32, (tb, tb), 0)
        ci = lax.broadcasted_iota(jnp.int32, (tb, tb), 1)
        tri = (ri <= ci).astype(jnp.bfloat16)
        carry = jnp.zeros((1, LANES), F32)
        dfb = jnp.zeros((1, LANES), F32)
        for b in reversed(range(T // tb)):
            sl = slice(b * tb, (b + 1) * tb)
            dl = _exact_tri_dot(tri, dc_ref[sl, :]) + carry
            carry = dl[0:1, :]
            z = fl_ref[sl, :] + fb_ref[...]
            dfl = dl * jax.nn.sigmoid(-z)
            dfl_ref[sl, :] = dfl
            dfb = dfb + jnp.sum(dfl, axis=0, keepdims=True)
        dfb_ref[...] = dfb

    return _pallas_call(
        body, name=name,
        out_shape=[jax.ShapeDtypeStruct((T, LANES), F32), jax.ShapeDtypeStruct((1, LANES), F32)],
        compiler_params=pltpu.CompilerParams(vmem_limit_bytes=32 * MIB),
    )(dcum, fl, fbp)


def _causal_keep(i, j, tq, tk):
    key = j * tk + lax.broadcasted_iota(jnp.int32, (tk, tq), 0)
    qry = i * tq + lax.broadcasted_iota(jnp.int32, (tk, tq), 1)
    return key <= qry


def _split_hi_lo(x):
    hi = x.astype(MXU)
    lo = (x - hi.astype(F32)).astype(MXU)
    return hi, lo


def _attn_fwd(qT, k, vT, cum, cumT, name):
    T = k.shape[0]
    tq = _tile(T, 256)
    tk = _tile(tq, 256)
    kpq = tq // tk
    heads = [slice(HEAD_DIM * h, HEAD_DIM * (h + 1)) for h in range(N_HEADS)]

    def body(qT_ref, k_ref, vT_ref, cum_ref, cumT_ref, o_ref, lseT_ref, acc_s, m_s, l_s):
        i = pl.program_id(0)
        acc_s[...] = jnp.zeros_like(acc_s)
        m_s[...] = jnp.full_like(m_s, NEG)
        l_s[...] = jnp.zeros_like(l_s)

        def kblock(j, masked):
            rows = pl.ds(pl.multiple_of(j * tk, tk), tk)
            keep = _causal_keep(i, j, tq, tk) if masked else None
            bias = [cumT_ref[h:h + 1, :] - cum_ref[rows, h:h + 1] for h in range(N_HEADS)]
            qk = [_mm(k_ref[rows, hs], qT_ref[hs, :]) + bias[h] for h, hs in enumerate(heads)]
            for h, hs in enumerate(heads):
                sT = qk[h]
                if masked:
                    sT = jnp.where(keep, sT, NEG)
                m_old = m_s[h:h + 1, :]
                m_new = jnp.maximum(m_old, jnp.max(sT, axis=0, keepdims=True))
                alpha = jnp.exp(m_old - m_new)
                pT = jnp.exp(sT - m_new)
                l_s[h:h + 1, :] = alpha * l_s[h:h + 1, :] + jnp.sum(pT, axis=0, keepdims=True)
                p_hi, p_lo = _split_hi_lo(pT)
                vh = vT_ref[hs, rows]
                acc_s[hs, :] = alpha * acc_s[hs, :] + (_mm(vh, p_hi) + _mm(vh, p_lo))
                m_s[h:h + 1, :] = m_new

        def unmasked(j, c):
            kblock(j, False)
            return c

        lax.fori_loop(0, kpq * i, unmasked, 0)
        for d in range(kpq):
            kblock(kpq * i + d, True)
        for h, hs in enumerate(heads):
            acc_s[hs, :] = acc_s[hs, :] / l_s[h:h + 1, :]
        o_ref[...] = acc_s[...].T
        lseT_ref[...] = m_s[...] + jnp.log(l_s[...])

    full = lambda a, b: pl.BlockSpec((a, b), lambda i: (0, 0))
    colblk = lambda r: pl.BlockSpec((r, tq), lambda i: (0, i))
    return _pallas_call(
        body, name=name, grid=(T // tq,),
        in_specs=[colblk(D_ATTN), full(T, D_ATTN), full(D_ATTN, T), full(T, LANES), colblk(N_HEADS)],
        out_specs=[pl.BlockSpec((tq, D_ATTN), lambda i: (i, 0)), colblk(N_HEADS)],
        out_shape=[jax.ShapeDtypeStruct((T, D_ATTN), F32), jax.ShapeDtypeStruct((N_HEADS, T), F32)],
        scratch_shapes=[pltpu.VMEM((D_ATTN, tq), F32), pltpu.VMEM((N_HEADS, tq), F32),
                        pltpu.VMEM((N_HEADS, tq), F32)],
        compiler_params=_params(40, 1),
    )(qT, k, vT, cum, cumT)


def _attn_bwd(qT, k, kT, v, doT, lseT, deltaT, cum, cumT, name, after=()):
    T = k.shape[0]
    tq = _tile(T, 256)
    tk = _tile(tq, 256)
    kpq = tq // tk
    heads = [slice(HEAD_DIM * h, HEAD_DIM * (h + 1)) for h in range(N_HEADS)]

    def body(qT_ref, k_ref, kT_ref, v_ref, doT_ref, lseT_ref, dlT_ref, cum_ref, cumT_ref, *rest):
        dq_ref, dk_ref, dv_ref, dcum_ref, dq_s = rest[len(after):]
        i = pl.program_id(0)

        @pl.when(i == 0)
        def _():
            dk_ref[...] = jnp.zeros_like(dk_ref)
            dv_ref[...] = jnp.zeros_like(dv_ref)
            dcum_ref[...] = jnp.zeros_like(dcum_ref)

        dq_s[...] = jnp.zeros_like(dq_s)

        def kblock(j, masked):
            rows = pl.ds(pl.multiple_of(j * tk, tk), tk)
            keep = _causal_keep(i, j, tq, tk) if masked else None
            bias = [cumT_ref[h:h + 1, :] - cum_ref[rows, h:h + 1] for h in range(N_HEADS)]
            qk = [_mm(k_ref[rows, hs], qT_ref[hs, :]) + bias[h] for h, hs in enumerate(heads)]
            dps = [_mm(v_ref[rows, hs], doT_ref[hs, :]) for hs in heads]
            for h, hs in enumerate(heads):
                sT = qk[h]
                if masked:
                    sT = jnp.where(keep, sT, NEG)
                pT = jnp.exp(sT - lseT_ref[h:h + 1, :])
                dsT = pT * (dps[h] - dlT_ref[h:h + 1, :])
                dcum_ref[rows, h:h + 1] += -jnp.sum(dsT, axis=1, keepdims=True)
                dsb = dsT.astype(MXU)
                dv_ref[hs, rows] += _mm_nt(doT_ref[hs, :], pT.astype(MXU))
                dk_ref[hs, rows] += _mm_nt(qT_ref[hs, :], dsb)
                dq_s[hs, :] += _mm(kT_ref[hs, rows], dsb)

        def unmasked(j, c):
            kblock(j, False)
            return c

        lax.fori_loop(0, kpq * i, unmasked, 0)
        for d in range(kpq):
            kblock(kpq * i + d, True)
        dq_ref[...] = (dq_s[...] * SCALE).astype(dq_ref.dtype)

    full = lambda a, b: pl.BlockSpec((a, b), lambda i: (0, 0))
    colblk = lambda r: pl.BlockSpec((r, tq), lambda i: (0, i))
    return _pallas_call(
        body, name=name, grid=(T // tq,),
        in_specs=[colblk(D_ATTN), full(T, D_ATTN), full(D_ATTN, T), full(T, D_ATTN), colblk(D_ATTN),
                  colblk(N_HEADS), colblk(N_HEADS), full(T, LANES), colblk(N_HEADS)] + [_UNREAD] * len(after),
        out_specs=[colblk(D_ATTN), full(D_ATTN, T), full(D_ATTN, T), full(T, LANES)],
        out_shape=[
            jax.ShapeDtypeStruct((D_ATTN, T), MXU),
            jax.ShapeDtypeStruct((D_ATTN, T), F32),
            jax.ShapeDtypeStruct((D_ATTN, T), F32),
            jax.ShapeDtypeStruct((T, LANES), F32),
        ],
        scratch_shapes=[pltpu.VMEM((D_ATTN, tq), F32)],
        compiler_params=_params(48, 1),
    )(qT, k, kT, v, doT, lseT, deltaT, cum, cumT, *after)


_ROWS_PER_CHUNK = 64


def _glu_halo(ag_ref, agh_ref, uext_s, tm, first):
    a = ag_ref[:, :D_CONV]
    sg = jax.nn.sigmoid(ag_ref[:, D_CONV:])
    uh = agh_ref[:, :D_CONV] * jax.nn.sigmoid(agh_ref[:, D_CONV:])
    uext_s[0:CONV_HALO, :] = jnp.where(first, 0.0, uh)
    uext_s[CONV_HALO:CONV_HALO + tm, :] = a * sg
    return a, sg


_SUBLANES = 8


def _shifted_copies(ext_s, sh_s, rows):
    for k in range(1, _SUBLANES):
        sh_s[k, 0:rows, :] = ext_s[pl.ds(k, rows), :]


def _window(ext_s, sh_s, start, rows):
    k = start % _SUBLANES
    if k == 0:
        return ext_s[pl.ds(start, rows), :]
    return sh_s[k, pl.ds(start - k, rows), :]


def _layer_norm_stats(y):
    mu = jnp.mean(y, axis=-1, keepdims=True)
    xc = y - mu
    rs = lax.rsqrt(jnp.mean(xc * xc, axis=-1, keepdims=True) + EPS)
    return xc * rs, rs


def _conv_fwd(ag, w32, cb, lg, lb, name):
    T = ag.shape[0]
    tm = _tile(T, 256)
    rc = _tile(tm, _ROWS_PER_CHUNK)
    hb = tm // CONV_HALO

    def body(ag_ref, agh_ref, w_ref, cb_ref, lg_ref, lb_ref, yc_ref, c_ref, uext_s, ush_s):
        i = pl.program_id(0)
        _glu_halo(ag_ref, agh_ref, uext_s, tm, i == 0)
        _shifted_copies(uext_s, ush_s, tm + CONV_HALO - _SUBLANES)
        for r0 in range(0, tm, rc):
            acc = jnp.zeros((rc, D_CONV), F32)
            for t in range(CONV_TAPS):
                acc = acc + _window(uext_s, ush_s, r0 + CONV_HALO - (CONV_TAPS - 1) + t, rc) * w_ref[t:t + 1, :]
            y = acc + cb_ref[...]
            yc_ref[r0:r0 + rc, :] = y
            n, _ = _layer_norm_stats(y)
            z = n * lg_ref[...] + lb_ref[...]
            c_ref[r0:r0 + rc, :] = z * jax.nn.sigmoid(z)

    row = lambda w: pl.BlockSpec((tm, w), lambda i: (i, 0))
    full = lambda a, b: pl.BlockSpec((a, b), lambda i: (0, 0))
    return _pallas_call(
        body, name=name, grid=(T // tm,),
        in_specs=[row(2 * D_CONV),
                  pl.BlockSpec((CONV_HALO, 2 * D_CONV), lambda i: (jnp.maximum(i * hb - 1, 0), 0)),
                  full(CONV_HALO, D_CONV), full(1, D_CONV), full(1, D_CONV), full(1, D_CONV)],
        out_specs=[row(D_CONV), row(D_CONV)],
        out_shape=[jax.ShapeDtypeStruct((T, D_CONV), F32), jax.ShapeDtypeStruct((T, D_CONV), F32)],
        scratch_shapes=[pltpu.VMEM((CONV_HALO + tm, D_CONV), F32),
                        pltpu.VMEM((_SUBLANES, CONV_HALO + tm, D_CONV), F32)],
        compiler_params=_params(32, 1),
    )(ag, ag, w32, cb, lg, lb)


def _conv_bwd(dc, yc, ag, w32, lg, lb, name):
    T = ag.shape[0]
    tm = _tile(T, 256)
    rc = _tile(tm, _ROWS_PER_CHUNK)
    I = T // tm
    hb = tm // CONV_HALO
    n_halo_blocks = T // CONV_HALO

    def body(dc_ref, yc_ref, dch_ref, ych_ref, ag_ref, agh_ref, w_ref, lg_ref, lb_ref,
             dag_ref, dw_ref, dcb_ref, dlg_ref, dlb_ref, uext_s, dext_s, ush_s, dsh_s):
        i = pl.program_id(0)
        lgv = lg_ref[...]
        lbv = lb_ref[...]

        def ln_bwd(dcv, ycv):
            n, rs = _layer_norm_stats(ycv)
            z = n * lgv + lbv
            dz = dcv * _silu_grad(z, jax.nn.sigmoid(z))
            dn = dz * lgv
            dy = rs * (dn - jnp.mean(dn, axis=-1, keepdims=True) - n * jnp.mean(dn * n, axis=-1, keepdims=True))
            return dy, dz, n

        dy, dz, n = ln_bwd(dc_ref[...], yc_ref[...])
        dyh, _, _ = ln_bwd(dch_ref[...], ych_ref[...])
        dext_s[0:tm, :] = dy
        dext_s[tm:tm + CONV_HALO, :] = jnp.where(i == I - 1, 0.0, dyh)
        a, sg = _glu_halo(ag_ref, agh_ref, uext_s, tm, i == 0)
        _shifted_copies(uext_s, ush_s, tm + CONV_HALO - _SUBLANES)
        _shifted_copies(dext_s, dsh_s, tm + CONV_HALO - _SUBLANES)

        @pl.when(i == 0)
        def _():
            dw_ref[...] = jnp.zeros_like(dw_ref)
            dcb_ref[...] = jnp.zeros_like(dcb_ref)
            dlg_ref[...] = jnp.zeros_like(dlg_ref)
            dlb_ref[...] = jnp.zeros_like(dlb_ref)

        dcb_ref[...] += jnp.sum(dy, axis=0, keepdims=True)
        dlg_ref[...] += jnp.sum(dz * n, axis=0, keepdims=True)
        dlb_ref[...] += jnp.sum(dz, axis=0, keepdims=True)
        for t in range(CONV_TAPS):
            u_t = _window(uext_s, ush_s, CONV_HALO - (CONV_TAPS - 1) + t, tm)
            dw_ref[t:t + 1, :] += jnp.sum(dy * u_t, axis=0, keepdims=True)
        for r0 in range(0, tm, rc):
            acc = jnp.zeros((rc, D_CONV), F32)
            for t in range(CONV_TAPS):
                acc = acc + _window(dext_s, dsh_s, r0 + (CONV_TAPS - 1) - t, rc) * w_ref[t:t + 1, :]
            a_c = a[r0:r0 + rc, :]
            sg_c = sg[r0:r0 + rc, :]
            dag_ref[r0:r0 + rc, :D_CONV] = (acc * sg_c).astype(dag_ref.dtype)
            dag_ref[r0:r0 + rc, D_CONV:] = (acc * a_c * sg_c * (1.0 - sg_c)).astype(dag_ref.dtype)

    row = lambda w: pl.BlockSpec((tm, w), lambda i: (i, 0))
    full = lambda a, b: pl.BlockSpec((a, b), lambda i: (0, 0))
    nxt = pl.BlockSpec((CONV_HALO, D_CONV), lambda i: (jnp.minimum((i + 1) * hb, n_halo_blocks - 1), 0))
    return _pallas_call(
        body, name=name, grid=(I,),
        in_specs=[row(D_CONV), row(D_CONV), nxt, nxt, row(2 * D_CONV),
                  pl.BlockSpec((CONV_HALO, 2 * D_CONV), lambda i: (jnp.maximum(i * hb - 1, 0), 0)),
                  full(CONV_HALO, D_CONV), full(1, D_CONV), full(1, D_CONV)],
        out_specs=[row(2 * D_CONV), full(CONV_HALO, D_CONV), full(1, D_CONV), full(1, D_CONV), full(1, D_CONV)],
        out_shape=[
            jax.ShapeDtypeStruct((T, 2 * D_CONV), MXU),
            jax.ShapeDtypeStruct((CONV_HALO, D_CONV), F32),
            jax.ShapeDtypeStruct((1, D_CONV), F32),
            jax.ShapeDtypeStruct((1, D_CONV), F32),
            jax.ShapeDtypeStruct((1, D_CONV), F32),
        ],
        scratch_shapes=[pltpu.VMEM((CONV_HALO + tm, D_CONV), F32), pltpu.VMEM((tm + CONV_HALO, D_CONV), F32),
                        pltpu.VMEM((_SUBLANES, CONV_HALO + tm, D_CONV), F32),
                        pltpu.VMEM((_SUBLANES, CONV_HALO + tm, D_CONV), F32)],
        compiler_params=_params(40, 1),
    )(dc, yc, dc, yc, ag, ag, w32, lg, lb)


def _outproj_fwd(x1, c, o, gc, ga, wout, name):
    T, D = x1.shape
    tm = _tile(T, 512)

    def body(x_ref, c_ref, o_ref, gc_ref, ga_ref, w_ref, x2_ref):
        yc, _ = _rms_fwd(c_ref[...], gc_ref[...])
        ya, _ = _rms_fwd(o_ref[...], ga_ref[...])
        x2_ref[...] = (x_ref[...] + _mm(yc.astype(MXU), w_ref[:D_CONV, :])
                       + _mm(ya.astype(MXU), w_ref[D_CONV:, :]))

    row = lambda w: pl.BlockSpec((tm, w), lambda i: (i, 0))
    full = lambda a, b: pl.BlockSpec((a, b), lambda i: (0, 0))
    return _pallas_call(
        body, name=name, grid=(T // tm,),
        in_specs=[row(D), row(D_CONV), row(D_ATTN), full(1, D_CONV), full(1, D_ATTN), full(D_CONV + D_ATTN, D)],
        out_specs=row(D),
        out_shape=jax.ShapeDtypeStruct((T, D), F32),
        compiler_params=_params(32, 1),
    )(x1, c, o, gc, ga, wout)


def _outproj_bwd(dx2, c, o, gc, ga, wout, name):
    T, D = dx2.shape
    tm = _tile(T, 256)
    I = T // tm

    def body(dx_ref, c_ref, o_ref, gc_ref, ga_ref, w_ref,
             dc_ref, doT_ref, dlT_ref, dw_ref, dgc_ref, dga_ref, acc_s):
        i = pl.program_id(0)
        dxb = dx_ref[...].astype(MXU)
        cv = c_ref[...]
        ov = o_ref[...]
        yc, rcn = _rms_fwd(cv, gc_ref[...])
        ya, ra = _rms_fwd(ov, ga_ref[...])
        dyc = _mm_nt(dxb, w_ref[:D_CONV, :])
        dya = _mm_nt(dxb, w_ref[D_CONV:, :])
        dwc = _mm_tn(yc.astype(MXU), dxb)
        dwa = _mm_tn(ya.astype(MXU), dxb)
        dcv, dgc = _rms_bwd(cv, rcn, gc_ref[...], dyc)
        dov, dga = _rms_bwd(ov, ra, ga_ref[...], dya)
        dc_ref[...] = dcv
        dob = dov.astype(doT_ref.dtype)
        doT_ref[...] = dov.T.astype(doT_ref.dtype)
        chan = lax.broadcasted_iota(jnp.int32, (D_ATTN, LANES), 0)
        head = lax.broadcasted_iota(jnp.int32, (D_ATTN, LANES), 1)
        in_head = ((chan >= head * HEAD_DIM) & (chan < (head + 1) * HEAD_DIM)).astype(jnp.bfloat16)
        dlT_ref[...] = _exact_dot_01(dob.astype(F32) * ov, in_head).T[:N_HEADS, :]

        @pl.when(i == 0)
        def _():
            acc_s[:D_CONV, :] = dwc
            acc_s[D_CONV:, :] = dwa
            dgc_ref[...] = dgc
            dga_ref[...] = dga

        @pl.when(i > 0)
        def _():
            acc_s[:D_CONV, :] += dwc
            acc_s[D_CONV:, :] += dwa
            dgc_ref[...] += dgc
            dga_ref[...] += dga

        @pl.when(i == I - 1)
        def _():
            dw_ref[...] = acc_s[...].astype(dw_ref.dtype)

    row = lambda w: pl.BlockSpec((tm, w), lambda i: (i, 0))
    full = lambda a, b: pl.BlockSpec((a, b), lambda i: (0, 0))
    return _pallas_call(
        body, name=name, grid=(I,),
        in_specs=[row(D), row(D_CONV), row(D_ATTN), full(1, D_CONV), full(1, D_ATTN), full(D_CONV + D_ATTN, D)],
        out_specs=[row(D_CONV), pl.BlockSpec((D_ATTN, tm), lambda i: (0, i)),
                   pl.BlockSpec((N_HEADS, tm), lambda i: (0, i)),
                   full(D_CONV + D_ATTN, D), full(1, D_CONV), full(1, D_ATTN)],
        out_shape=[
            jax.ShapeDtypeStruct((T, D_CONV), F32),
            jax.ShapeDtypeStruct((D_ATTN, T), MXU),
            jax.ShapeDtypeStruct((N_HEADS, T), F32),
            jax.ShapeDtypeStruct((D_CONV + D_ATTN, D), MXU),
            jax.ShapeDtypeStruct((1, D_CONV), F32),
            jax.ShapeDtypeStruct((1, D_ATTN), F32),
        ],
        scratch_shapes=[pltpu.VMEM((D_CONV + D_ATTN, D), F32)],
        compiler_params=_params(40, 1),
    )(dx2, c, o, gc, ga, wout)


def _ffn_down_loss(x, A, w2, gf, target, name):
    T, D = x.shape
    J, _, bf = A.shape
    tm = _tile(T, 512)

    def body(x_ref, A_ref, w2_ref, g_ref, t_ref, loss_ref, dx_ref, dg_ref):
        i = pl.program_id(0)
        f = _mm(A_ref[0], w2_ref[0:bf, :])
        for j in range(1, J):
            f = f + _mm(A_ref[j], w2_ref[j * bf:(j + 1) * bf, :])
        xv = x_ref[...] + 0.5 * f
        gv = g_ref[...]
        out, r = _rms_fwd(xv, gv)
        err = out - t_ref[...]
        part = jnp.full((1, LANES), 0.5 / D, F32) * jnp.sum(err * err)
        dxn, dgp = _rms_bwd(xv, r, gv, err * (1.0 / D))
        dx_ref[...] = dxn

        @pl.when(i == 0)
        def _():
            loss_ref[...] = part
            dg_ref[...] = dgp

        @pl.when(i > 0)
        def _():
            loss_ref[...] += part
            dg_ref[...] += dgp

    row = lambda w: pl.BlockSpec((tm, w), lambda i: (i, 0))
    full = lambda a, b: pl.BlockSpec((a, b), lambda i: (0, 0))
    return _pallas_call(
        body, name=name, grid=(T // tm,),
        in_specs=[row(D), pl.BlockSpec((J, tm, bf), lambda i: (0, i, 0)), full(J * bf, D), full(1, D), row(D)],
        out_specs=[full(1, LANES), row(D), full(1, D)],
        out_shape=[jax.ShapeDtypeStruct((1, LANES), F32), jax.ShapeDtypeStruct((T, D), F32),
                   jax.ShapeDtypeStruct((1, D), F32)],
        compiler_params=_params(56, 1),
    )(x, A, w2, gf, target)


def _row_tile(rows):
    for cand in (256, 176, 128, 64, 32, 16):
        if rows % cand == 0:
            return cand
    return rows


def _adamw(w, m, v, parts, name):
    R, C = w.shape
    P = parts.shape[0]
    tr = _row_tile(R)
    c1 = 1.0 - ADAM_B1 ** ADAM_STEP
    c2 = 1.0 - ADAM_B2 ** ADAM_STEP

    def body(w_ref, m_ref, v_ref, p_ref, g_ref, d_ref, nm_ref, nv_ref):
        g = p_ref[0].astype(F32)
        for s in range(1, P):
            g = g + p_ref[s].astype(F32)
        wv = w_ref[...]
        mn = ADAM_B1 * m_ref[...] + (1.0 - ADAM_B1) * g
        vn = ADAM_B2 * v_ref[...] + (1.0 - ADAM_B2) * (g * g)
        g_ref[...] = g
        nm_ref[...] = mn
        nv_ref[...] = vn
        d_ref[...] = -ADAM_LR * ((mn / c1) / (jnp.sqrt(vn / c2) + ADAM_EPS) + ADAM_WD * wv)

    blk = pl.BlockSpec((tr, C), lambda i: (i, 0))
    out = jax.ShapeDtypeStruct((R, C), F32)
    return _pallas_call(
        body, name=name, grid=(R // tr,),
        in_specs=[blk, blk, blk, pl.BlockSpec((P, tr, C), lambda i: (0, i, 0))],
        out_specs=[blk, blk, blk, blk],
        out_shape=[out, out, out, out],
        compiler_params=_params(32, 1),
    )(w, m, v, parts)


def _position():
    return lax.axis_index("x"), lax.axis_index("y"), lax.axis_index("c")


def _core_index():
    return lax.axis_index("c").astype(jnp.int32)


def _flat(px, py, pc):
    return 4 * px + 2 * py + pc


def _row_halves(rows, dtype):
    tile = _SUBLANES * (4 // jnp.dtype(dtype).itemsize)
    half = rows // 2 // tile * tile
    assert half > 0
    return (0, half), (half, rows - half)


def _gather_body(ins, outs, send_sems, recv_sems, local_sems, handshake):
    n = len(ins)
    x, y, c = _position()
    me, sibling = (x, y, c), (x, y, 1 - c)
    across_x, across_y, diagonal = (1 - x, y), (x, 1 - y), (1 - x, 1 - y)
    if handshake:
        _handshake([sibling] + [(*chip, cc) for chip in (across_x, across_y, diagonal) for cc in (c, 1 - c)])

    def copy(a, k, block, to, rows=None, src=None):
        dst = outs[a].at[_flat(*block)]
        if rows is not None:
            dst = dst.at[pl.ds(*rows)]
        return pltpu.make_async_remote_copy(
            src_ref=dst if src is None else src, dst_ref=dst,
            send_sem=send_sems.at[a, k], recv_sem=recv_sems.at[a, k],
            device_id=to, device_id_type=MESH)

    halves = [_row_halves(ins[a].shape[0], ins[a].dtype) for a in range(n)]
    mine = [pltpu.make_async_copy(ins[a], outs[a].at[_flat(*me)], local_sems.at[a]) for a in range(n)]
    for cp in mine:
        cp.start()
    sent = []

    def start(cp):
        cp.start()
        sent.append(cp)

    for a in range(n):
        start(copy(a, 0, me, sibling, src=ins[a]))
        start(copy(a, 1, me, (*across_x, c), src=ins[a]))
        start(copy(a, 2, me, (*across_y, c), src=ins[a]))
    for a in range(n):
        low, high = halves[a]
        copy(a, 1, (*across_x, c), me).wait_recv()
        start(copy(a, 3, (*across_x, c), (*across_y, c), rows=low))
        start(copy(a, 5, (*across_x, c), sibling))
        copy(a, 2, (*across_y, c), me).wait_recv()
        start(copy(a, 4, (*across_y, c), (*across_x, c), rows=high))
        start(copy(a, 6, (*across_y, c), sibling))
    for a in range(n):
        low, high = halves[a]
        copy(a, 3, (*diagonal, c), me, rows=low).wait_recv()
        start(copy(a, 7, (*diagonal, c), sibling, rows=low))
        copy(a, 4, (*diagonal, c), me, rows=high).wait_recv()
        start(copy(a, 8, (*diagonal, c), sibling, rows=high))
    for a in range(n):
        low, high = halves[a]
        copy(a, 0, sibling, me).wait_recv()
        copy(a, 5, (*across_x, 1 - c), me).wait_recv()
        copy(a, 6, (*across_y, 1 - c), me).wait_recv()
        copy(a, 7, (*diagonal, 1 - c), me, rows=low).wait_recv()
        copy(a, 8, (*diagonal, 1 - c), me, rows=high).wait_recv()
    for cp in sent:
        cp.wait_send()
    for cp in mine:
        cp.wait()


_GATHER_SLOTS = 9


def _gather_scratch(n):
    return [pltpu.SemaphoreType.DMA((n, _GATHER_SLOTS)), pltpu.SemaphoreType.DMA((n, _GATHER_SLOTS)),
            pltpu.SemaphoreType.DMA((n,))]


def _all_gather(shards, name):
    n = len(shards)

    def body(*refs):
        _gather_body(refs[:n], refs[n:2 * n], *refs[2 * n:], handshake=False)

    hbm = pl.BlockSpec(memory_space=pltpu.HBM)
    return _pallas_call(
        body, name=name,
        in_specs=[hbm] * n, out_specs=[hbm] * n,
        out_shape=[jax.ShapeDtypeStruct((N_DEV,) + s.shape, s.dtype) for s in shards],
        scratch_shapes=_gather_scratch(n),
    )(*shards)


def _handshake(peers):
    barrier = pltpu.get_barrier_semaphore()
    for peer in peers:
        pl.semaphore_signal(barrier, inc=1, device_id=peer, device_id_type=MESH)
    pl.semaphore_wait(barrier, len(peers))


def _sequencer_call(body, name, collective_id, out_type, scratch_types, operands):
    return pl.kernel(
        body, name=name, out_type=out_type,
        mesh=plsc.ScalarSubcoreMesh(axis_name="sequencer", num_cores=1),
        scratch_types=scratch_types,
        compiler_params=pltpu.CompilerParams(collective_id=collective_id),
    )(*operands)


def _seq_all_gather(shards, name, collective_id, after):
    n = len(shards)

    def body(*refs):
        _gather_body(refs[:n], refs[n + 1:2 * n + 1], *refs[2 * n + 1:], handshake=True)

    return _sequencer_call(
        body, name, collective_id,
        [jax.ShapeDtypeStruct((N_DEV,) + s.shape, s.dtype) for s in shards],
        _gather_scratch(n), list(shards) + [after])


class _Halves(NamedTuple):
    to_sibling: jax.Array
    kept: jax.Array


def _seq_to_sibling(parts, name, collective_id, after):
    n = len(parts)
    split = [isinstance(p, _Halves) for p in parts]
    parts = [p.to_sibling if isinstance(p, _Halves) else p for p in parts]

    def body(*refs):
        ins, outs = refs[:n], refs[n + len(after):2 * n + len(after)]
        send_sems, recv_sems = refs[2 * n + len(after):]
        x, y, c = _position()
        sibling = (x, y, 1 - c)
        _handshake([sibling])
        sent = []
        for a in range(n):
            for q in range(N_CHIPS):
                cp = pltpu.make_async_remote_copy(
                    src_ref=ins[a].at[q if split[a] else 2 * q + (1 - c)], dst_ref=outs[a].at[q],
                    send_sem=send_sems.at[a, q], recv_sem=recv_sems.at[a, q],
                    device_id=sibling, device_id_type=MESH)
                cp.start()
                sent.append(cp)
        for cp in sent:
            cp.wait_recv()
        for cp in sent:
            cp.wait_send()

    return _sequencer_call(
        body, name, collective_id,
        [jax.ShapeDtypeStruct((N_CHIPS,) + p.shape[1:], p.dtype) for p in parts],
        [pltpu.SemaphoreType.DMA((n, N_CHIPS)), pltpu.SemaphoreType.DMA((n, N_CHIPS))],
        list(parts) + list(after))


def _seq_to_chips(partials, name, collective_id):
    n = len(partials)

    def body(*refs):
        ins, outs = refs[:n], refs[n:2 * n]
        send_sems, recv_sems, local_sems = refs[2 * n:]
        x, y, c = _position()
        my_chip = 2 * x + y
        chips = [(1 - x, y), (x, 1 - y), (1 - x, 1 - y)]
        _handshake([(*chip, c) for chip in chips])
        mine = [pltpu.make_async_copy(ins[a].at[my_chip], outs[a].at[my_chip], local_sems.at[a]) for a in range(n)]
        for cp in mine:
            cp.start()
        sent = []
        for a in range(n):
            for j, (px, py) in enumerate(chips):
                cp = pltpu.make_async_remote_copy(
                    src_ref=ins[a].at[2 * px + py], dst_ref=outs[a].at[my_chip],
                    send_sem=send_sems.at[a, j], recv_sem=recv_sems.at[a, j],
                    device_id=(px, py, c), device_id_type=MESH)
                cp.start()
                sent.append(cp)
        for cp in sent:
            cp.wait_recv()
        for cp in sent:
            cp.wait_send()
        for cp in mine:
            cp.wait()

    return _sequencer_call(
        body, name, collective_id,
        [jax.ShapeDtypeStruct(p.shape, p.dtype) for p in partials],
        [pltpu.SemaphoreType.DMA((n, 3)), pltpu.SemaphoreType.DMA((n, 3)), pltpu.SemaphoreType.DMA((n,))],
        list(partials))


def _pair_add(parts, recvs, name, after=()):
    n = len(parts)
    core = _core_index().reshape(1)
    split = [isinstance(p, _Halves) for p in parts]
    parts = [p.kept if isinstance(p, _Halves) else p for p in parts]

    def body(c_ref, *refs):
        ps, rs, outs = refs[:n], refs[n:2 * n], refs[2 * n + len(after):]
        for p_ref, r_ref, o_ref in zip(ps, rs, outs):
            o_ref[...] = (p_ref[...].astype(F32) + r_ref[...].astype(F32)).astype(o_ref.dtype)

    mine = lambda p: pl.BlockSpec((None,) + p.shape[1:], lambda q, c: (2 * q + c[0], 0, 0))
    blk = lambda p: pl.BlockSpec((None,) + p.shape[1:], lambda q, c: (q, 0, 0))
    return pl.pallas_call(
        body, name=name,
        grid_spec=pltpu.PrefetchScalarGridSpec(
            num_scalar_prefetch=1, grid=(N_CHIPS,),
            in_specs=[blk(p) if s else mine(p) for p, s in zip(parts, split)] + [blk(p) for p in parts]
            + [_UNREAD] * len(after),
            out_specs=[blk(p) for p in parts]),
        out_shape=[pltpu.HBM((N_CHIPS,) + p.shape[1:], p.dtype) for p in parts],
        compiler_params=_params(40, 1),
    )(core, *[pltpu.with_memory_space_constraint(a, pltpu.HBM) for a in (*parts, *recvs, *after)])


class _Reduced(NamedTuple):
    partials: list
    reduced: list


def _blocks(g):
    return g.reshape(N_DEV, -1, g.shape[-1])


def _reduce_scatter(parts, tag, ids, after=(), add_after=()):
    from_sibling = _seq_to_sibling(parts, "rs_sibling_" + tag, ids[0], after)
    partials = _pair_add(parts, from_sibling, "rs_add_" + tag, add_after)
    return _Reduced(partials, _seq_to_chips(partials, "rs_chips_" + tag, ids[1]))


_SMALL = ("ffn1_norm", "mix_norm", "conv_b", "conv_ln_g", "conv_ln_b", "forget_b", "out_norm_conv",
          "out_norm_attn", "ffn2_norm", "final_norm")
_PACK_WIDTH = 2 * D_CONV
_SLOT = dict(ffn1_norm=(0, 0), mix_norm=(1, 0), ffn2_norm=(2, 0), final_norm=(3, 0), conv_b=(4, 0),
             conv_ln_g=(4, D_CONV), conv_ln_b=(5, 0), out_norm_conv=(5, D_CONV), out_norm_attn=(6, 0),
             forget_b=(6, D_CONV))
_LOSS_ROW = 7
_CONV_ROW0 = 8
_PACK_ROWS = _CONV_ROW0 + CONV_HALO


def _pack_small(small, name):
    arrays = [small[n] for n in _SMALL] + [small["conv_w"], small["loss"]]

    def body(*refs):
        out = refs[-1]
        out[...] = jnp.zeros_like(out)
        for n, ref in zip(_SMALL, refs):
            row, lane = _SLOT[n]
            out[row:row + 1, lane:lane + ref.shape[1]] = ref[...]
        out[_CONV_ROW0:, :D_CONV] = refs[len(_SMALL)][...]
        out[_LOSS_ROW:_LOSS_ROW + 1, :LANES] = refs[len(_SMALL) + 1][...]

    return _pallas_call(body, name=name, out_shape=jax.ShapeDtypeStruct((_PACK_ROWS, _PACK_WIDTH), F32))(*arrays)


def _adamw_small(gathered, w, m, v, name):
    c1 = 1.0 - ADAM_B1 ** ADAM_STEP
    c2 = 1.0 - ADAM_B2 ** ADAM_STEP
    k = len(_SMALL)

    def body(g_ref, *refs):
        ws, ms, vs = refs[:k], refs[k:2 * k], refs[2 * k:3 * k]
        outs = refs[3 * k:]
        total = g_ref[0]
        for s in range(1, N_DEV):
            total = total + g_ref[s]
        for i, n in enumerate(_SMALL):
            row, lane = _SLOT[n]
            width = ws[i].shape[1]
            g = total[row:row + 1, lane:lane + width]
            mn = ADAM_B1 * ms[i][...] + (1.0 - ADAM_B1) * g
            vn = ADAM_B2 * vs[i][...] + (1.0 - ADAM_B2) * (g * g)
            o_g, o_d, o_m, o_v = outs[4 * i:4 * i + 4]
            o_g[...] = g
            o_m[...] = mn
            o_v[...] = vn
            o_d[...] = -ADAM_LR * ((mn / c1) / (jnp.sqrt(vn / c2) + ADAM_EPS) + ADAM_WD * ws[i][...])
        outs[4 * k][...] = total[_CONV_ROW0:, :D_CONV]
        outs[4 * k + 1][...] = total[_LOSS_ROW:_LOSS_ROW + 1, :LANES]

    shapes = []
    for n in _SMALL:
        shapes += [jax.ShapeDtypeStruct(w[n].shape, F32)] * 4
    shapes.append(jax.ShapeDtypeStruct((CONV_HALO, D_CONV), F32))
    shapes.append(jax.ShapeDtypeStruct((1, LANES), F32))
    res = _pallas_call(body, name=name, out_shape=shapes)(
        gathered, *[w[n] for n in _SMALL], *[m[n] for n in _SMALL], *[v[n] for n in _SMALL])
    return {n: res[4 * i:4 * i + 4] for i, n in enumerate(_SMALL)}, res[4 * k], res[4 * k + 1]


def _w13_grad_halves(h, dG, dU, name):
    core = _core_index()
    return _Halves(_ffn_w13_grad(h, dG, dU, 1 - core, name + "_sibling"), _ffn_w13_grad(h, dG, dU, core, name + "_kept"))

def _local_step(x, target, norms, shard):
    D = x.shape[1]
    J = N_DEV // 2
    as13 = lambda g: g.reshape(2, J, g.shape[1], D)

    (g13_1,) = _all_gather([shard["ffn1_w13"]], "gather_ffn1_w13")
    (g2_1,) = _seq_all_gather([shard["ffn1_w2"]], "gather_ffn1_w2", 10, after=g13_1)
    w13_1 = as13(g13_1)
    G1, U1, A1 = _ffn_up(x, norms["ffn1_norm"], w13_1, "ffn1_up")
    gin, gconv = _seq_all_gather([shard["w_in"], shard["conv_w"]], "gather_mix", 1, after=G1)
    w2_1 = g2_1.reshape(-1, D)
    x1 = _ffn_down(x, A1, w2_1, "ffn1_down")
    gout, g13_2, g2_2 = _seq_all_gather([shard["w_out"], shard["ffn2_w13"], shard["ffn2_w2"]], "gather_ffn2", 2,
                                        after=x1)
    wout = gout.reshape(-1, D)
    conv_w32 = jnp.pad(gconv.transpose(1, 0, 2).reshape(CONV_TAPS, D_CONV), ((0, CONV_HALO - CONV_TAPS), (0, 0)))

    ag, k, v, qT, kT, vT, fl = _inproj_fwd(x1, norms["mix_norm"], gin, "inproj_fwd")
    cum, cumT = _forget_fwd(fl, norms["forget_b"], "forget_fwd")
    yc, c = _conv_fwd(ag, conv_w32, norms["conv_b"], norms["conv_ln_g"], norms["conv_ln_b"], "conv_fwd")
    o, lseT = _attn_fwd(qT, k, vT, cum, cumT, "attn_fwd")
    x2 = _outproj_fwd(x1, c, o, norms["out_norm_conv"], norms["out_norm_attn"], wout, "outproj_fwd")
    w13_2, w2_2 = as13(g13_2), g2_2.reshape(-1, D)
    G2, U2, A2 = _ffn_up(x2, norms["ffn2_norm"], w13_2, "ffn2_up")
    loss, dx3, d_final = _ffn_down_loss(x2, A2, w2_2, norms["final_norm"], target, "ffn2_down_loss")

    dw2_2 = _ffn_w2_grad(dx3, A2, "ffn2_w2_grad")
    dx2, d_ffn2n, h3, dG2, dU2 = _ffn_bwd_act(x2, norms["ffn2_norm"], dx3, G2, U2, w13_2, w2_2, "ffn2_bwd_act")
    dw13_2 = _w13_grad_halves(h3, dG2, dU2, "ffn2_w13_grad")
    dc, dobT, deltaT, dwout, d_onc, d_ona = _outproj_bwd(
        dx2, c, o, norms["out_norm_conv"], norms["out_norm_attn"], wout, "outproj_bwd")
    red_ffn2 = _reduce_scatter([dw13_2, _blocks(dw2_2)], "ffn2", (3, 4), add_after=(dc,))
    dqT, dkT, dvT, dcum = _attn_bwd(qT, k, kT, v, dobT, lseT, deltaT, cum, cumT, "attn_bwd",
                                    after=red_ffn2.partials)
    dfl, d_fb = _forget_bwd(dcum, fl, norms["forget_b"], "forget_bwd")
    dag, d_convw, d_cb, d_lg, d_lb = _conv_bwd(dc, yc, ag, conv_w32, norms["conv_ln_g"], norms["conv_ln_b"], "conv_bwd")
    dx1, d_mixn, h2 = _inproj_bwd_act(x1, norms["mix_norm"], dx2, dag, dqT, dkT, dvT, dfl, gin, "inproj_bwd_act")
    dw2_1 = _ffn_w2_grad(dx1, A1, "ffn1_w2_grad")
    early = [_blocks(dwout), _blocks(dw2_1)]
    sib_early = _seq_to_sibling(early, "rs_sibling_mix_early", 11, red_ffn2.reduced[:1])
    dwin_blocks = _inproj_bwd_weights(h2, dag, dqT, dkT, dvT, dfl, "inproj_bwd_weights")
    sib_w_in = _seq_to_sibling([dwin_blocks], "rs_sibling_mix", 5, sib_early[:1])
    mix_partials = _pair_add([dwin_blocks] + early, sib_w_in + sib_early, "rs_add_mix")
    red_mix = _Reduced(mix_partials, _seq_to_chips(mix_partials, "rs_chips_mix", 6))
    dx, d_ffn1n, h1, dG1, dU1 = _ffn_bwd_act(x, norms["ffn1_norm"], dx1, G1, U1, w13_1, w2_1, "ffn1_bwd_act",
                                             after=red_mix.partials)
    small = dict(ffn1_norm=d_ffn1n, mix_norm=d_mixn, conv_b=d_cb, conv_ln_g=d_lg, conv_ln_b=d_lb,
                 forget_b=d_fb, out_norm_conv=d_onc, out_norm_attn=d_ona, ffn2_norm=d_ffn2n,
                 final_norm=d_final, conv_w=d_convw, loss=loss)
    packed_small = _pack_small(small, "pack_small_grads")
    (gathered_small,) = _seq_all_gather([packed_small], "gather_small_grads", 9, after=red_mix.partials[0])
    dw13_1 = _w13_grad_halves(h1, dG1, dU1, "ffn1_w13_grad")
    red_w13_1 = _reduce_scatter([dw13_1], "ffn1_w13", (7, 8), after=[red_mix.reduced[0], gathered_small])
    big = dict(ffn1_w13=red_w13_1.reduced[0], ffn1_w2=red_mix.reduced[2], w_in=red_mix.reduced[0],
               w_out=red_mix.reduced[1], ffn2_w13=red_ffn2.reduced[0], ffn2_w2=red_ffn2.reduced[1])
    return dx, gathered_small, big


_BIG = ("ffn1_w13", "ffn1_w2", "w_in", "w_out", "ffn2_w13", "ffn2_w2")
_TRANSPOSED = ("ffn1_w13", "ffn2_w13", "w_in")
_ORDER = ("ffn1_norm", "ffn1_w13", "ffn1_w2", "mix_norm", "w_in", "conv_w", "conv_b", "conv_ln_g", "conv_ln_b",
          "forget_b", "out_norm_conv", "out_norm_attn", "w_out", "ffn2_norm", "ffn2_w13", "ffn2_w2", "final_norm")


def kernel(x, ffn1_norm, ffn1_w13, ffn1_w2, mix_norm, w_in, conv_w, conv_b, conv_ln_g, conv_ln_b, forget_b, out_norm_conv, out_norm_attn, w_out, ffn2_norm, ffn2_w13, ffn2_w2, final_norm, loss_target, m_ffn1_norm, m_ffn1_w13, m_ffn1_w2, m_mix_norm, m_w_in, m_conv_w, m_conv_b, m_conv_ln_g, m_conv_ln_b, m_forget_b, m_out_norm_conv, m_out_norm_attn, m_w_out, m_ffn2_norm, m_ffn2_w13, m_ffn2_w2, m_final_norm, v_ffn1_norm, v_ffn1_w13, v_ffn1_w2, v_mix_norm, v_w_in, v_conv_w, v_conv_b, v_conv_ln_g, v_conv_ln_b, v_forget_b, v_out_norm_conv, v_out_norm_attn, v_w_out, v_ffn2_norm, v_ffn2_w13, v_ffn2_w2, v_final_norm):
    w = dict(ffn1_norm=ffn1_norm, ffn1_w13=ffn1_w13, ffn1_w2=ffn1_w2, mix_norm=mix_norm, w_in=w_in, conv_w=conv_w,
             conv_b=conv_b, conv_ln_g=conv_ln_g, conv_ln_b=conv_ln_b, forget_b=forget_b, out_norm_conv=out_norm_conv,
             out_norm_attn=out_norm_attn, w_out=w_out, ffn2_norm=ffn2_norm, ffn2_w13=ffn2_w13, ffn2_w2=ffn2_w2,
             final_norm=final_norm)
    m = dict(ffn1_norm=m_ffn1_norm, ffn1_w13=m_ffn1_w13, ffn1_w2=m_ffn1_w2, mix_norm=m_mix_norm, w_in=m_w_in,
             conv_w=m_conv_w, conv_b=m_conv_b, conv_ln_g=m_conv_ln_g, conv_ln_b=m_conv_ln_b, forget_b=m_forget_b,
             out_norm_conv=m_out_norm_conv, out_norm_attn=m_out_norm_attn, w_out=m_w_out, ffn2_norm=m_ffn2_norm,
             ffn2_w13=m_ffn2_w13, ffn2_w2=m_ffn2_w2, final_norm=m_final_norm)
    v = dict(ffn1_norm=v_ffn1_norm, ffn1_w13=v_ffn1_w13, ffn1_w2=v_ffn1_w2, mix_norm=v_mix_norm, w_in=v_w_in,
             conv_w=v_conv_w, conv_b=v_conv_b, conv_ln_g=v_conv_ln_g, conv_ln_b=v_conv_ln_b, forget_b=v_forget_b,
             out_norm_conv=v_out_norm_conv, out_norm_attn=v_out_norm_attn, w_out=v_w_out, ffn2_norm=v_ffn2_norm,
             ffn2_w13=v_ffn2_w13, ffn2_w2=v_ffn2_w2, final_norm=v_final_norm)
    shapes = {n: a.shape for n, a in w.items()}
    T, D = x.shape[1], x.shape[2]
    def two(n, a):
        if a.ndim != 3:
            return a.reshape(1, -1)
        a = a.reshape(a.shape[-2], a.shape[-1])
        return a.T if n in _TRANSPOSED else a

    w2d = {n: two(n, a) for n, a in w.items()}
    m2d = {n: two(n, a) for n, a in m.items()}
    v2d = {n: two(n, a) for n, a in v.items()}

    shard = {n: w2d[n].astype(MXU) for n in _BIG}
    shard["conv_w"] = w2d["conv_w"]
    norms = {n: w2d[n] for n in _SMALL}
    norms["forget_b"] = jnp.pad(w2d["forget_b"], ((0, 0), (0, LANES - N_HEADS)))
    dx, gathered_small, big = _local_step(x[0], loss_target[0], norms, shard)

    grads, deltas, new_m, new_v = {}, {}, {}, {}
    for n in _BIG:
        g, d, nm, nv = _adamw(w2d[n], m2d[n], v2d[n], big[n], "adamw_" + n)
        grads[n], deltas[n], new_m[n], new_v[n] = g, d, nm, nv

    small_out, conv_g_full, loss = _adamw_small(gathered_small, w2d, m2d, v2d, "adamw_small")
    for n in _SMALL:
        grads[n], deltas[n], new_m[n], new_v[n] = small_out[n]
    conv_g_full = conv_g_full[:CONV_TAPS]
    xi, yi, ci = _position()
    cw = shapes["conv_w"][-1]
    conv_g_mine = lax.dynamic_slice_in_dim(conv_g_full, _flat(xi, yi, ci) * cw, cw, axis=1)
    g, d, nm, nv = _adamw(w2d["conv_w"], m2d["conv_w"], v2d["conv_w"], conv_g_mine[None], "adamw_conv_w")
    grads["conv_w"], deltas["conv_w"], new_m["conv_w"], new_v["conv_w"] = g, d, nm, nv

    shaped = lambda dct: [(dct[n].T if n in _TRANSPOSED else dct[n]).reshape(shapes[n]) for n in _ORDER]
    return (loss[0, 0], dx[None], *shaped(grads), *shaped(deltas), *shaped(new_m), *shaped(new_v))
```

```python
from typing import NamedTuple

import jax
import jax.numpy as jnp
from jax import lax
from jax.experimental import pallas as pl
from jax.experimental.pallas import tpu as pltpu
from jax.experimental.pallas import tpu_sc as plsc

F32 = jnp.float32
MXU = jnp.bfloat16
EPS = 1e-6
N_HEADS = 8
HEAD_DIM = 64
D_CONV = 512
D_ATTN = N_HEADS * HEAD_DIM
CONV_TAPS = 31
CONV_HALO = 32
SCALE = HEAD_DIM ** -0.5
NEG = -1e30
LANES = 128
N_DEV = 8
N_CHIPS = N_DEV // 2
MESH = pl.DeviceIdType.MESH
MIB = 1 << 20

ADAM_LR = 0.001
ADAM_B1 = 0.9
ADAM_B2 = 0.999
ADAM_EPS = 1e-08
ADAM_WD = 0.01
ADAM_STEP = 10


_UNREAD = pl.BlockSpec(memory_space=pl.ANY)


def _pallas_call(body, *, out_shape, **kwargs):
    in_hbm = lambda s: pltpu.HBM(s.shape, s.dtype)
    outs = [in_hbm(s) for s in out_shape] if isinstance(out_shape, (list, tuple)) else in_hbm(out_shape)
    call = pl.pallas_call(body, out_shape=outs, **kwargs)
    return lambda *operands: call(*[pltpu.with_memory_space_constraint(a, pltpu.HBM) for a in operands])


def _params(vmem_mib, n_axes):
    return pltpu.CompilerParams(dimension_semantics=("arbitrary",) * n_axes, vmem_limit_bytes=vmem_mib * MIB)


def _mm(a, b):
    return jnp.dot(a, b, preferred_element_type=F32)


def _mm_nt(a, b):
    return lax.dot_general(a, b, (((1,), (1,)), ((), ())), preferred_element_type=F32)


def _mm_tn(a, b):
    return lax.dot_general(a, b, (((0,), (0,)), ((), ())), preferred_element_type=F32)


def _rms_fwd(x, g):
    r = lax.rsqrt(jnp.mean(x * x, axis=-1, keepdims=True) + EPS)
    return x * r * g, r


def _rms_bwd(x, r, g, dy):
    gdy = dy * g
    dx = r * gdy - x * (r * r * r) * jnp.mean(x * gdy, axis=-1, keepdims=True)
    dg = jnp.sum(dy * x * r, axis=0, keepdims=True)
    return dx, dg


def _silu_grad(z, sz):
    return sz * (1.0 + z * (1.0 - sz))


def _three_terms(x):
    x1 = x.astype(jnp.bfloat16)
    r1 = x - x1.astype(F32)
    x2 = r1.astype(jnp.bfloat16)
    x3 = (r1 - x2.astype(F32)).astype(jnp.bfloat16)
    return x1, x2, x3


def _exact_tri_dot(tri, x):
    x1, x2, x3 = _three_terms(x)
    return _mm(tri, x1) + _mm(tri, x2) + _mm(tri, x3)


def _exact_dot_01(x, sel):
    x1, x2, x3 = _three_terms(x)
    return _mm(x1, sel) + _mm(x2, sel) + _mm(x3, sel)


def _tile(n, want):
    t = min(n, want)
    assert n % t == 0
    return t


_FFN_CHUNK = 256


def _ffn_up(x, g, w13, name):
    T, D = x.shape
    _, J, bf, _ = w13.shape
    tm = _tile(T, 512)
    I = T // tm

    def body(x_ref, g_ref, w13_ref, G_ref, U_ref, A_ref, h_s):
        j = pl.program_id(0)
        i = pl.program_id(1)
        rows = pl.ds(pl.multiple_of(i * tm, tm), tm)

        @pl.when(j == 0)
        def _():
            h, _ = _rms_fwd(x_ref[...], g_ref[...])
            h_s[rows, :] = h.astype(MXU)

        chunks = [slice(r0, r0 + _FFN_CHUNK) for r0 in range(0, tm, _FFN_CHUNK)]
        hbs = [h_s[pl.ds(pl.multiple_of(i * tm + rs.start, _FFN_CHUNK), _FFN_CHUNK), :] for rs in chunks]
        GU = [(_mm_nt(hb, w13_ref[0]), _mm_nt(hb, w13_ref[1])) for hb in hbs]
        for rs, (G, U) in zip(chunks, GU):
            G_ref[rs, :] = G.astype(MXU)
            U_ref[rs, :] = U.astype(MXU)
            A_ref[rs, :] = (G * jax.nn.sigmoid(G) * U).astype(MXU)

    blk = pl.BlockSpec((None, tm, bf), lambda j, i: (j, i, 0))
    hid = jax.ShapeDtypeStruct((J, T, bf), MXU)
    return _pallas_call(
        body, name=name, grid=(J, I),
        in_specs=[pl.BlockSpec((tm, D), lambda j, i: (jnp.where(j == 0, i, I - 1), 0)),
                  pl.BlockSpec((1, D), lambda j, i: (0, 0)),
                  pl.BlockSpec((2, None, bf, D), lambda j, i: (0, j, 0, 0))],
        out_specs=[blk, blk, blk],
        out_shape=[hid, hid, hid],
        scratch_shapes=[pltpu.VMEM((T, D), MXU)],
        compiler_params=_params(40, 2),
    )(x, g, w13)


def _ffn_down(x, A, w2, name):
    T, D = x.shape
    J, _, bf = A.shape
    tm = _tile(T, 512)

    def body(x_ref, A_ref, w2_ref, xo_ref):
        f = _mm(A_ref[0], w2_ref[0:bf, :])
        for j in range(1, J):
            f = f + _mm(A_ref[j], w2_ref[j * bf:(j + 1) * bf, :])
        xo_ref[...] = x_ref[...] + 0.5 * f

    row = pl.BlockSpec((tm, D), lambda i: (i, 0))
    return _pallas_call(
        body, name=name, grid=(T // tm,),
        in_specs=[row, pl.BlockSpec((J, tm, bf), lambda i: (0, i, 0)), pl.BlockSpec((J * bf, D), lambda i: (0, 0))],
        out_specs=row,
        out_shape=jax.ShapeDtypeStruct((T, D), F32),
        compiler_params=_params(48, 1),
    )(x, A, w2)


def _ffn_bwd_act(x, g, dy, Gs, Us, w13, w2, name, after=()):
    T, D = x.shape
    _, J, bf, _ = w13.shape
    tm = _tile(T, 512)
    I = T // tm

    def body(x_ref, g_ref, dy_ref, G_ref, U_ref, w13_ref, w2_ref, *rest):
        dx_ref, dg_ref, h_ref, dG_ref, dU_ref, dh_s, dF_s, h_s = rest[len(after):]
        j = pl.program_id(0)
        i = pl.program_id(1)
        rows = pl.ds(pl.multiple_of(i * tm, tm), tm)

        @pl.when(j == 0)
        def _():
            h, _ = _rms_fwd(x_ref[...], g_ref[...])
            hb = h.astype(MXU)
            h_s[rows, :] = hb
            h_ref[...] = hb
            dF_s[rows, :] = (0.5 * dy_ref[...]).astype(MXU)
            dh_s[rows, :] = jnp.zeros((tm, D), F32)

        chunks = [slice(r0, r0 + _FFN_CHUNK) for r0 in range(0, tm, _FFN_CHUNK)]
        crows = [pl.ds(pl.multiple_of(i * tm + rs.start, _FFN_CHUNK), _FFN_CHUNK) for rs in chunks]
        dAs = [_mm_nt(dF_s[cr, :], w2_ref[...]) for cr in crows]
        for rs, cr, dA in zip(chunks, crows, dAs):
            G = G_ref[rs, :].astype(F32)
            U = U_ref[rs, :].astype(F32)
            sg = jax.nn.sigmoid(G)
            s = G * sg
            dUb = (dA * s).astype(MXU)
            dGb = (dA * U * _silu_grad(G, sg)).astype(MXU)
            dG_ref[rs, :] = dGb
            dU_ref[rs, :] = dUb
            dh_s[cr, :] += _mm(dGb, w13_ref[0]) + _mm(dUb, w13_ref[1])

        @pl.when(j == J - 1)
        def _():
            xv = x_ref[...]
            gv = g_ref[...]
            _, r = _rms_fwd(xv, gv)
            dxn, dgp = _rms_bwd(xv, r, gv, dh_s[rows, :])
            dx_ref[...] = dy_ref[...] + dxn

            @pl.when(i == 0)
            def _():
                dg_ref[...] = dgp

            @pl.when(i > 0)
            def _():
                dg_ref[...] += dgp

    ends = lambda j, i: (jnp.where((j == 0) | (j == J - 1), i, I - 1), 0)
    blk = pl.BlockSpec((None, tm, bf), lambda j, i: (j, i, 0))
    hid = jax.ShapeDtypeStruct((J, T, bf), MXU)
    return _pallas_call(
        body, name=name, grid=(J, I),
        in_specs=[pl.BlockSpec((tm, D), ends), pl.BlockSpec((1, D), lambda j, i: (0, 0)), pl.BlockSpec((tm, D), ends),
                  blk, blk, pl.BlockSpec((2, None, bf, D), lambda j, i: (0, j, 0, 0)),
                  pl.BlockSpec((bf, D), lambda j, i: (j, 0))] + [_UNREAD] * len(after),
        out_specs=[pl.BlockSpec((tm, D), lambda j, i: (jnp.where(j == J - 1, i, 0), 0)),
                   pl.BlockSpec((1, D), lambda j, i: (0, 0)),
                   pl.BlockSpec((tm, D), lambda j, i: (jnp.where(j == 0, i, I - 1), 0)), blk, blk],
        out_shape=[jax.ShapeDtypeStruct((T, D), F32), jax.ShapeDtypeStruct((1, D), F32),
                   jax.ShapeDtypeStruct((T, D), MXU), hid, hid],
        scratch_shapes=[pltpu.VMEM((T, D), F32), pltpu.VMEM((T, D), MXU), pltpu.VMEM((T, D), MXU)],
        compiler_params=_params(58, 2),
    )(x, g, dy, Gs, Us, w13, w2, *after)


def _ffn_w13_grad(h, dG, dU, name):
    T, D = h.shape
    J, _, bf = dG.shape

    def body(h_ref, dG_ref, dU_ref, dw13_ref):
        dw13_ref[0] = _mm_tn(dG_ref[...], h_ref[...]).astype(dw13_ref.dtype)
        dw13_ref[1] = _mm_tn(dU_ref[...], h_ref[...]).astype(dw13_ref.dtype)

    blk = pl.BlockSpec((None, T, bf), lambda j: (j, 0, 0))
    return _pallas_call(
        body, name=name, grid=(J,),
        in_specs=[pl.BlockSpec((T, D), lambda j: (0, 0)), blk, blk],
        out_specs=pl.BlockSpec((2, None, bf, D), lambda j: (0, j, 0, 0)),
        out_shape=jax.ShapeDtypeStruct((2, J, bf, D), MXU),
        compiler_params=_params(48, 1),
    )(h, dG, dU)


def _ffn_w2_grad(dy, A, name, after=()):
    T, D = dy.shape
    J, _, bf = A.shape

    def body(dy_ref, A_ref, *rest):
        dw2_ref, dF_s = rest[len(after):]

        @pl.when(pl.program_id(0) == 0)
        def _():
            dF_s[...] = (0.5 * dy_ref[...]).astype(MXU)

        dw2_ref[...] = _mm_tn(A_ref[...], dF_s[...]).astype(dw2_ref.dtype)

    return _pallas_call(
        body, name=name, grid=(J,),
        in_specs=[pl.BlockSpec((T, D), lambda j: (0, 0)), pl.BlockSpec((None, T, bf), lambda j: (j, 0, 0))]
        + [_UNREAD] * len(after),
        out_specs=pl.BlockSpec((bf, D), lambda j: (j, 0)),
        out_shape=jax.ShapeDtypeStruct((J * bf, D), MXU),
        scratch_shapes=[pltpu.VMEM((T, D), MXU)],
        compiler_params=_params(48, 1),
    )(dy, A, *after)


_AG0, _Q0, _K0, _V0, _F0 = 0, 2 * D_CONV, 2 * D_CONV + D_ATTN, 2 * D_CONV + 2 * D_ATTN, 2 * D_CONV + 3 * D_ATTN
N_IN = _F0 + N_HEADS
N_IN_PAD = _F0 + LANES
_IN_BLOCK = N_IN // N_DEV


def _rows_from_blocks(blocks_ref, rows_ref):
    for p in range(N_DEV):
        rows_ref[_IN_BLOCK * p:_IN_BLOCK * (p + 1), :] = blocks_ref[p]
    rows_ref[N_IN:, :] = jnp.zeros((N_IN_PAD - N_IN, rows_ref.shape[1]), rows_ref.dtype)


def _inproj_fwd(x1, gm, win_blocks, name):
    T, D = x1.shape
    tm = _tile(T, 256)

    def body(x_ref, g_ref, wb_ref, ag_ref, k_ref, v_ref, qT_ref, kT_ref, vT_ref, fl_ref, w_ref):
        @pl.when(pl.program_id(0) == 0)
        def _():
            _rows_from_blocks(wb_ref, w_ref)

        h, _ = _rms_fwd(x_ref[...], g_ref[...])
        hb = h.astype(MXU)
        ag_ref[...] = _mm_nt(hb, w_ref[_AG0:_Q0, :])
        qT_ref[...] = (_mm_nt(hb, w_ref[_Q0:_K0, :]) * SCALE).T.astype(MXU)
        for c0, ref, refT in ((_K0, k_ref, kT_ref), (_V0, v_ref, vT_ref)):
            y = _mm_nt(hb, w_ref[c0:c0 + D_ATTN, :])
            ref[...] = y.astype(MXU)
            refT[...] = y.T.astype(MXU)
        fl_ref[...] = _mm_nt(hb, w_ref[_F0:N_IN_PAD, :])

    row = lambda w: pl.BlockSpec((tm, w), lambda i: (i, 0))
    col = pl.BlockSpec((D_ATTN, tm), lambda i: (0, i))
    std = jax.ShapeDtypeStruct((T, D_ATTN), MXU)
    trn = jax.ShapeDtypeStruct((D_ATTN, T), MXU)
    return _pallas_call(
        body, name=name, grid=(T // tm,),
        in_specs=[row(D), pl.BlockSpec((1, D), lambda i: (0, 0)),
                  pl.BlockSpec((N_DEV, _IN_BLOCK, D), lambda i: (0, 0, 0))],
        out_specs=[row(2 * D_CONV), row(D_ATTN), row(D_ATTN), col, col, col, row(LANES)],
        out_shape=[jax.ShapeDtypeStruct((T, 2 * D_CONV), F32), std, std, trn, trn, trn,
                   jax.ShapeDtypeStruct((T, LANES), F32)],
        scratch_shapes=[pltpu.VMEM((N_IN_PAD, D), MXU)],
        compiler_params=_params(40, 1),
    )(x1, gm, win_blocks)


def _inproj_bwd_act(x1, gm, dx2, dag, dqT, dkT, dvT, dfl, win_blocks, name):
    T, D = x1.shape
    tm = _tile(T, 256)

    def body(x_ref, g_ref, dx2_ref, dag_ref, dqT_ref, dkT_ref, dvT_ref, dfl_ref, wb_ref, dx1_ref, dg_ref, h_ref,
             w_ref):
        i = pl.program_id(0)

        @pl.when(i == 0)
        def _():
            _rows_from_blocks(wb_ref, w_ref)

        xv = x_ref[...]
        gv = g_ref[...]
        h, r = _rms_fwd(xv, gv)
        h_ref[...] = h.astype(MXU)
        dh = _mm(dag_ref[...], w_ref[_AG0:_Q0, :])
        for c0, ref in ((_Q0, dqT_ref), (_K0, dkT_ref), (_V0, dvT_ref)):
            dh = dh + _mm_tn(ref[...].astype(MXU), w_ref[c0:c0 + D_ATTN, :])
        dh = dh + _mm(dfl_ref[...].astype(MXU), w_ref[_F0:N_IN_PAD, :])
        dxn, dgp = _rms_bwd(xv, r, gv, dh)
        dx1_ref[...] = dx2_ref[...] + dxn

        @pl.when(i == 0)
        def _():
            dg_ref[...] = dgp

        @pl.when(i > 0)
        def _():
            dg_ref[...] += dgp

    row = lambda w: pl.BlockSpec((tm, w), lambda i: (i, 0))
    col = pl.BlockSpec((D_ATTN, tm), lambda i: (0, i))
    full = lambda a, b: pl.BlockSpec((a, b), lambda i: (0, 0))
    return _pallas_call(
        body, name=name, grid=(T // tm,),
        in_specs=[row(D), full(1, D), row(D), row(2 * D_CONV), col, col, col, row(LANES),
                  pl.BlockSpec((N_DEV, _IN_BLOCK, D), lambda i: (0, 0, 0))],
        out_specs=[row(D), full(1, D), row(D)],
        out_shape=[jax.ShapeDtypeStruct((T, D), F32), jax.ShapeDtypeStruct((1, D), F32),
                   jax.ShapeDtypeStruct((T, D), MXU)],
        scratch_shapes=[pltpu.VMEM((N_IN_PAD, D), MXU)],
        compiler_params=_params(40, 1),
    )(x1, gm, dx2, dag, dqT, dkT, dvT, dfl, win_blocks)


def _inproj_bwd_weights(h, dag, dqT, dkT, dvT, dfl, name, after=()):
    T, D = h.shape

    def body(h_ref, dag_ref, dqT_ref, dkT_ref, dvT_ref, dfl_ref, *rest):
        blocks_ref, dw_ref = rest[len(after):]
        hb = h_ref[...]
        dw_ref[_AG0:_Q0, :] = _mm_tn(dag_ref[...], hb).astype(dw_ref.dtype)
        for c0, ref in ((_Q0, dqT_ref), (_K0, dkT_ref), (_V0, dvT_ref)):
            dw_ref[c0:c0 + D_ATTN, :] = _mm(ref[...].astype(MXU), hb).astype(dw_ref.dtype)
        dw_ref[_F0:N_IN_PAD, :] = _mm_tn(dfl_ref[...].astype(MXU), hb).astype(dw_ref.dtype)
        for p in range(N_DEV):
            blocks_ref[p] = dw_ref[_IN_BLOCK * p:_IN_BLOCK * (p + 1), :]

    vmem = pl.BlockSpec(memory_space=pltpu.VMEM)
    return _pallas_call(
        body, name=name, in_specs=[vmem] * 6 + [_UNREAD] * len(after), out_specs=vmem,
        out_shape=jax.ShapeDtypeStruct((N_DEV, _IN_BLOCK, D), MXU),
        scratch_shapes=[pltpu.VMEM((N_IN_PAD, D), MXU)],
        compiler_params=pltpu.CompilerParams(vmem_limit_bytes=56 * MIB),
    )(h, dag, dqT, dkT, dvT, dfl, *after)


def _forget_fwd(fl, fbp, name):
    T = fl.shape[0]
    tb = _tile(T, 256)

    def body(fl_ref, fb_ref, cum_ref, cumT_ref):
        ri = lax.broadcasted_iota(jnp.int32, (tb, tb), 0)
        ci = lax.broadcasted_iota(jnp.int32, (tb, tb), 1)
        tri = (ri >= ci).astype(jnp.bfloat16)
        carry = jnp.zeros((1, LANES), F32)
        for b in range(T // tb):
            z = fl_ref[b * tb:(b + 1) * tb, :] + fb_ref[...]
            lf = jnp.minimum(z, 0.0) - jnp.log1p(jnp.exp(-jnp.abs(z)))
            c = _exact_tri_dot(tri, lf) + carry
            cum_ref[b * tb:(b + 1) * tb, :] = c
            carry = c[tb - 1:tb, :]
        cumT_ref[...] = cum_ref[...].T[:N_HEADS, :]

    return _pallas_call(
        body, name=name,
        out_shape=[jax.ShapeDtypeStruct((T, LANES), F32), jax.ShapeDtypeStruct((N_HEADS, T), F32)],
        compiler_params=pltpu.CompilerParams(vmem_limit_bytes=32 * MIB),
    )(fl, fbp)


def _forget_bwd(dcum, fl, fbp, name):
    T = fl.shape[0]
    tb = _tile(T, 256)

    def body(dc_ref, fl_ref, fb_ref, dfl_ref, dfb_ref):
        ri = lax.broadcasted_iota(jnp.int32, (tb, tb), 0)
        ci = lax.broadcasted_iota(jnp.int32, (tb, tb), 1)
        tri = (ri <= ci).astype(jnp.bfloat16)
        carry = jnp.zeros((1, LANES), F32)
        dfb = jnp.zeros((1, LANES), F32)
        for b in reversed(range(T // tb)):
            sl = slice(b * tb, (b + 1) * tb)
            dl = _exact_tri_dot(tri, dc_ref[sl, :]) + carry
            carry = dl[0:1, :]
            z = fl_ref[sl, :] + fb_ref[...]
            dfl = dl * jax.nn.sigmoid(-z)
            dfl_ref[sl, :] = dfl
            dfb = dfb + jnp.sum(dfl, axis=0, keepdims=True)
        dfb_ref[...] = dfb

    return _pallas_call(
        body, name=name,
        out_shape=[jax.ShapeDtypeStruct((T, LANES), F32), jax.ShapeDtypeStruct((1, LANES), F32)],
        compiler_params=pltpu.CompilerParams(vmem_limit_bytes=32 * MIB),
    )(dcum, fl, fbp)


def _causal_keep(i, j, tq, tk):
    key = j * tk + lax.broadcasted_iota(jnp.int32, (tk, tq), 0)
    qry = i * tq + lax.broadcasted_iota(jnp.int32, (tk, tq), 1)
    return key <= qry


def _split_hi_lo(x):
    hi = x.astype(MXU)
    lo = (x - hi.astype(F32)).astype(MXU)
    return hi, lo


def _attn_fwd(qT, k, vT, cum, cumT, name):
    T = k.shape[0]
    tq = _tile(T, 256)
    tk = _tile(tq, 256)
    kpq = tq // tk
    heads = [slice(HEAD_DIM * h, HEAD_DIM * (h + 1)) for h in range(N_HEADS)]

    def body(qT_ref, k_ref, vT_ref, cum_ref, cumT_ref, o_ref, lseT_ref, acc_s, m_s, l_s):
        i = pl.program_id(0)
        acc_s[...] = jnp.zeros_like(acc_s)
        m_s[...] = jnp.full_like(m_s, NEG)
        l_s[...] = jnp.zeros_like(l_s)

        def kblock(j, masked):
            rows = pl.ds(pl.multiple_of(j * tk, tk), tk)
            keep = _causal_keep(i, j, tq, tk) if masked else None
            bias = [cumT_ref[h:h + 1, :] - cum_ref[rows, h:h + 1] for h in range(N_HEADS)]
            qk = [_mm(k_ref[rows, hs], qT_ref[hs, :]) + bias[h] for h, hs in enumerate(heads)]
            for h, hs in enumerate(heads):
                sT = qk[h]
                if masked:
                    sT = jnp.where(keep, sT, NEG)
                m_old = m_s[h:h + 1, :]
                m_new = jnp.maximum(m_old, jnp.max(sT, axis=0, keepdims=True))
                alpha = jnp.exp(m_old - m_new)
                pT = jnp.exp(sT - m_new)
                l_s[h:h + 1, :] = alpha * l_s[h:h + 1, :] + jnp.sum(pT, axis=0, keepdims=True)
                p_hi, p_lo = _split_hi_lo(pT)
                vh = vT_ref[hs, rows]
                acc_s[hs, :] = alpha * acc_s[hs, :] + (_mm(vh, p_hi) + _mm(vh, p_lo))
                m_s[h:h + 1, :] = m_new

        def unmasked(j, c):
            kblock(j, False)
            return c

        lax.fori_loop(0, kpq * i, unmasked, 0)
        for d in range(kpq):
            kblock(kpq * i + d, True)
        for h, hs in enumerate(heads):
            acc_s[hs, :] = acc_s[hs, :] / l_s[h:h + 1, :]
        o_ref[...] = acc_s[...].T
        lseT_ref[...] = m_s[...] + jnp.log(l_s[...])

    full = lambda a, b: pl.BlockSpec((a, b), lambda i: (0, 0))
    colblk = lambda r: pl.BlockSpec((r, tq), lambda i: (0, i))
    return _pallas_call(
        body, name=name, grid=(T // tq,),
        in_specs=[colblk(D_ATTN), full(T, D_ATTN), full(D_ATTN, T), full(T, LANES), colblk(N_HEADS)],
        out_specs=[pl.BlockSpec((tq, D_ATTN), lambda i: (i, 0)), colblk(N_HEADS)],
        out_shape=[jax.ShapeDtypeStruct((T, D_ATTN), F32), jax.ShapeDtypeStruct((N_HEADS, T), F32)],
        scratch_shapes=[pltpu.VMEM((D_ATTN, tq), F32), pltpu.VMEM((N_HEADS, tq), F32),
                        pltpu.VMEM((N_HEADS, tq), F32)],
        compiler_params=_params(40, 1),
    )(qT, k, vT, cum, cumT)


def _attn_bwd(qT, k, kT, v, doT, lseT, deltaT, cum, cumT, name, after=()):
    T = k.shape[0]
    tq = _tile(T, 256)
    tk = _tile(tq, 256)
    kpq = tq // tk
    heads = [slice(HEAD_DIM * h, HEAD_DIM * (h + 1)) for h in range(N_HEADS)]

    def body(qT_ref, k_ref, kT_ref, v_ref, doT_ref, lseT_ref, dlT_ref, cum_ref, cumT_ref, *rest):
        dq_ref, dk_ref, dv_ref, dcum_ref, dq_s = rest[len(after):]
        i = pl.program_id(0)

        @pl.when(i == 0)
        def _():
            dk_ref[...] = jnp.zeros_like(dk_ref)
            dv_ref[...] = jnp.zeros_like(dv_ref)
            dcum_ref[...] = jnp.zeros_like(dcum_ref)

        dq_s[...] = jnp.zeros_like(dq_s)

        def kblock(j, masked):
            rows = pl.ds(pl.multiple_of(j * tk, tk), tk)
            keep = _causal_keep(i, j, tq, tk) if masked else None
            bias = [cumT_ref[h:h + 1, :] - cum_ref[rows, h:h + 1] for h in range(N_HEADS)]
            qk = [_mm(k_ref[rows, hs], qT_ref[hs, :]) + bias[h] for h, hs in enumerate(heads)]
            dps = [_mm(v_ref[rows, hs], doT_ref[hs, :]) for hs in heads]
            for h, hs in enumerate(heads):
                sT = qk[h]
                if masked:
                    sT = jnp.where(keep, sT, NEG)
                pT = jnp.exp(sT - lseT_ref[h:h + 1, :])
                dsT = pT * (dps[h] - dlT_ref[h:h + 1, :])
                dcum_ref[rows, h:h + 1] += -jnp.sum(dsT, axis=1, keepdims=True)
                dsb = dsT.astype(MXU)
                dv_ref[hs, rows] += _mm_nt(doT_ref[hs, :], pT.astype(MXU))
                dk_ref[hs, rows] += _mm_nt(qT_ref[hs, :], dsb)
                dq_s[hs, :] += _mm(kT_ref[hs, rows], dsb)

        def unmasked(j, c):
            kblock(j, False)
            return c

        lax.fori_loop(0, kpq * i, unmasked, 0)
        for d in range(kpq):
            kblock(kpq * i + d, True)
        dq_ref[...] = (dq_s[...] * SCALE).astype(dq_ref.dtype)

    full = lambda a, b: pl.BlockSpec((a, b), lambda i: (0, 0))
    colblk = lambda r: pl.BlockSpec((r, tq), lambda i: (0, i))
    return _pallas_call(
        body, name=name, grid=(T // tq,),
        in_specs=[colblk(D_ATTN), full(T, D_ATTN), full(D_ATTN, T), full(T, D_ATTN), colblk(D_ATTN),
                  colblk(N_HEADS), colblk(N_HEADS), full(T, LANES), colblk(N_HEADS)] + [_UNREAD] * len(after),
        out_specs=[colblk(D_ATTN), full(D_ATTN, T), full(D_ATTN, T), full(T, LANES)],
        out_shape=[
            jax.ShapeDtypeStruct((D_ATTN, T), MXU),
            jax.ShapeDtypeStruct((D_ATTN, T), F32),
            jax.ShapeDtypeStruct((D_ATTN, T), F32),
            jax.ShapeDtypeStruct((T, LANES), F32),
        ],
        scratch_shapes=[pltpu.VMEM((D_ATTN, tq), F32)],
        compiler_params=_params(48, 1),
    )(qT, k, kT, v, doT, lseT, deltaT, cum, cumT, *after)


_ROWS_PER_CHUNK = 64


def _glu_halo(ag_ref, agh_ref, uext_s, tm, first):
    a = ag_ref[:, :D_CONV]
    sg = jax.nn.sigmoid(ag_ref[:, D_CONV:])
    uh = agh_ref[:, :D_CONV] * jax.nn.sigmoid(agh_ref[:, D_CONV:])
    uext_s[0:CONV_HALO, :] = jnp.where(first, 0.0, uh)
    uext_s[CONV_HALO:CONV_HALO + tm, :] = a * sg
    return a, sg


_SUBLANES = 8


def _shifted_copies(ext_s, sh_s, rows):
    for k in range(1, _SUBLANES):
        sh_s[k, 0:rows, :] = ext_s[pl.ds(k, rows), :]


def _window(ext_s, sh_s, start, rows):
    k = start % _SUBLANES
    if k == 0:
        return ext_s[pl.ds(start, rows), :]
    return sh_s[k, pl.ds(start - k, rows), :]


def _layer_norm_stats(y):
    mu = jnp.mean(y, axis=-1, keepdims=True)
    xc = y - mu
    rs = lax.rsqrt(jnp.mean(xc * xc, axis=-1, keepdims=True) + EPS)
    return xc * rs, rs


def _conv_fwd(ag, w32, cb, lg, lb, name):
    T = ag.shape[0]
    tm = _tile(T, 256)
    rc = _tile(tm, _ROWS_PER_CHUNK)
    hb = tm // CONV_HALO

    def body(ag_ref, agh_ref, w_ref, cb_ref, lg_ref, lb_ref, yc_ref, c_ref, uext_s, ush_s):
        i = pl.program_id(0)
        _glu_halo(ag_ref, agh_ref, uext_s, tm, i == 0)
        _shifted_copies(uext_s, ush_s, tm + CONV_HALO - _SUBLANES)
        for r0 in range(0, tm, rc):
            acc = jnp.zeros((rc, D_CONV), F32)
            for t in range(CONV_TAPS):
                acc = acc + _window(uext_s, ush_s, r0 + CONV_HALO - (CONV_TAPS - 1) + t, rc) * w_ref[t:t + 1, :]
            y = acc + cb_ref[...]
            yc_ref[r0:r0 + rc, :] = y
            n, _ = _layer_norm_stats(y)
            z = n * lg_ref[...] + lb_ref[...]
            c_ref[r0:r0 + rc, :] = z * jax.nn.sigmoid(z)

    row = lambda w: pl.BlockSpec((tm, w), lambda i: (i, 0))
    full = lambda a, b: pl.BlockSpec((a, b), lambda i: (0, 0))
    return _pallas_call(
        body, name=name, grid=(T // tm,),
        in_specs=[row(2 * D_CONV),
                  pl.BlockSpec((CONV_HALO, 2 * D_CONV), lambda i: (jnp.maximum(i * hb - 1, 0), 0)),
                  full(CONV_HALO, D_CONV), full(1, D_CONV), full(1, D_CONV), full(1, D_CONV)],
        out_specs=[row(D_CONV), row(D_CONV)],
        out_shape=[jax.ShapeDtypeStruct((T, D_CONV), F32), jax.ShapeDtypeStruct((T, D_CONV), F32)],
        scratch_shapes=[pltpu.VMEM((CONV_HALO + tm, D_CONV), F32),
                        pltpu.VMEM((_SUBLANES, CONV_HALO + tm, D_CONV), F32)],
        compiler_params=_params(32, 1),
    )(ag, ag, w32, cb, lg, lb)


def _conv_bwd(dc, yc, ag, w32, lg, lb, name):
    T = ag.shape[0]
    tm = _tile(T, 256)
    rc = _tile(tm, _ROWS_PER_CHUNK)
    I = T // tm
    hb = tm // CONV_HALO
    n_halo_blocks = T // CONV_HALO

    def body(dc_ref, yc_ref, dch_ref, ych_ref, ag_ref, agh_ref, w_ref, lg_ref, lb_ref,
             dag_ref, dw_ref, dcb_ref, dlg_ref, dlb_ref, uext_s, dext_s, ush_s, dsh_s):
        i = pl.program_id(0)
        lgv = lg_ref[...]
        lbv = lb_ref[...]

        def ln_bwd(dcv, ycv):
            n, rs = _layer_norm_stats(ycv)
            z = n * lgv + lbv
            dz = dcv * _silu_grad(z, jax.nn.sigmoid(z))
            dn = dz * lgv
            dy = rs * (dn - jnp.mean(dn, axis=-1, keepdims=True) - n * jnp.mean(dn * n, axis=-1, keepdims=True))
            return dy, dz, n

        dy, dz, n = ln_bwd(dc_ref[...], yc_ref[...])
        dyh, _, _ = ln_bwd(dch_ref[...], ych_ref[...])
        dext_s[0:tm, :] = dy
        dext_s[tm:tm + CONV_HALO, :] = jnp.where(i == I - 1, 0.0, dyh)
        a, sg = _glu_halo(ag_ref, agh_ref, uext_s, tm, i == 0)
        _shifted_copies(uext_s, ush_s, tm + CONV_HALO - _SUBLANES)
        _shifted_copies(dext_s, dsh_s, tm + CONV_HALO - _SUBLANES)

        @pl.when(i == 0)
        def _():
            dw_ref[...] = jnp.zeros_like(dw_ref)
            dcb_ref[...] = jnp.zeros_like(dcb_ref)
            dlg_ref[...] = jnp.zeros_like(dlg_ref)
            dlb_ref[...] = jnp.zeros_like(dlb_ref)

        dcb_ref[...] += jnp.sum(dy, axis=0, keepdims=True)
        dlg_ref[...] += jnp.sum(dz * n, axis=0, keepdims=True)
        dlb_ref[...] += jnp.sum(dz, axis=0, keepdims=True)
        for t in range(CONV_TAPS):
            u_t = _window(uext_s, ush_s, CONV_HALO - (CONV_TAPS - 1) + t, tm)
            dw_ref[t:t + 1, :] += jnp.sum(dy * u_t, axis=0, keepdims=True)
        for r0 in range(0, tm, rc):
            acc = jnp.zeros((rc, D_CONV), F32)
            for t in range(CONV_TAPS):
                acc = acc + _window(dext_s, dsh_s, r0 + (CONV_TAPS - 1) - t, rc) * w_ref[t:t + 1, :]
            a_c = a[r0:r0 + rc, :]
            sg_c = sg[r0:r0 + rc, :]
            dag_ref[r0:r0 + rc, :D_CONV] = (acc * sg_c).astype(dag_ref.dtype)
            dag_ref[r0:r0 + rc, D_CONV:] = (acc * a_c * sg_c * (1.0 - sg_c)).astype(dag_ref.dtype)

    row = lambda w: pl.BlockSpec((tm, w), lambda i: (i, 0))
    full = lambda a, b: pl.BlockSpec((a, b), lambda i: (0, 0))
    nxt = pl.BlockSpec((CONV_HALO, D_CONV), lambda i: (jnp.minimum((i + 1) * hb, n_halo_blocks - 1), 0))
    return _pallas_call(
        body, name=name, grid=(I,),
        in_specs=[row(D_CONV), row(D_CONV), nxt, nxt, row(2 * D_CONV),
                  pl.BlockSpec((CONV_HALO, 2 * D_CONV), lambda i: (jnp.maximum(i * hb - 1, 0), 0)),
                  full(CONV_HALO, D_CONV), full(1, D_CONV), full(1, D_CONV)],
        out_specs=[row(2 * D_CONV), full(CONV_HALO, D_CONV), full(1, D_CONV), full(1, D_CONV), full(1, D_CONV)],
        out_shape=[
            jax.ShapeDtypeStruct((T, 2 * D_CONV), MXU),
            jax.ShapeDtypeStruct((CONV_HALO, D_CONV), F32),
            jax.ShapeDtypeStruct((1, D_CONV), F32),
            jax.ShapeDtypeStruct((1, D_CONV), F32),
            jax.ShapeDtypeStruct((1, D_CONV), F32),
        ],
        scratch_shapes=[pltpu.VMEM((CONV_HALO + tm, D_CONV), F32), pltpu.VMEM((tm + CONV_HALO, D_CONV), F32),
                        pltpu.VMEM((_SUBLANES, CONV_HALO + tm, D_CONV), F32),
                        pltpu.VMEM((_SUBLANES, CONV_HALO + tm, D_CONV), F32)],
        compiler_params=_params(40, 1),
    )(dc, yc, dc, yc, ag, ag, w32, lg, lb)


def _outproj_fwd(x1, c, o, gc, ga, wout, name):
    T, D = x1.shape
    tm = _tile(T, 512)

    def body(x_ref, c_ref, o_ref, gc_ref, ga_ref, w_ref, x2_ref):
        yc, _ = _rms_fwd(c_ref[...], gc_ref[...])
        ya, _ = _rms_fwd(o_ref[...], ga_ref[...])
        x2_ref[...] = (x_ref[...] + _mm(yc.astype(MXU), w_ref[:D_CONV, :])
                       + _mm(ya.astype(MXU), w_ref[D_CONV:, :]))

    row = lambda w: pl.BlockSpec((tm, w), lambda i: (i, 0))
    full = lambda a, b: pl.BlockSpec((a, b), lambda i: (0, 0))
    return _pallas_call(
        body, name=name, grid=(T // tm,),
        in_specs=[row(D), row(D_CONV), row(D_ATTN), full(1, D_CONV), full(1, D_ATTN), full(D_CONV + D_ATTN, D)],
        out_specs=row(D),
        out_shape=jax.ShapeDtypeStruct((T, D), F32),
        compiler_params=_params(32, 1),
    )(x1, c, o, gc, ga, wout)


def _outproj_bwd(dx2, c, o, gc, ga, wout, name):
    T, D = dx2.shape
    tm = _tile(T, 256)
    I = T // tm

    def body(dx_ref, c_ref, o_ref, gc_ref, ga_ref, w_ref,
             dc_ref, doT_ref, dlT_ref, dw_ref, dgc_ref, dga_ref, acc_s):
        i = pl.program_id(0)
        dxb = dx_ref[...].astype(MXU)
        cv = c_ref[...]
        ov = o_ref[...]
        yc, rcn = _rms_fwd(cv, gc_ref[...])
        ya, ra = _rms_fwd(ov, ga_ref[...])
        dyc = _mm_nt(dxb, w_ref[:D_CONV, :])
        dya = _mm_nt(dxb, w_ref[D_CONV:, :])
        dwc = _mm_tn(yc.astype(MXU), dxb)
        dwa = _mm_tn(ya.astype(MXU), dxb)
        dcv, dgc = _rms_bwd(cv, rcn, gc_ref[...], dyc)
        dov, dga = _rms_bwd(ov, ra, ga_ref[...], dya)
        dc_ref[...] = dcv
        dob = dov.astype(doT_ref.dtype)
        doT_ref[...] = dov.T.astype(doT_ref.dtype)
        chan = lax.broadcasted_iota(jnp.int32, (D_ATTN, LANES), 0)
        head = lax.broadcasted_iota(jnp.int32, (D_ATTN, LANES), 1)
        in_head = ((chan >= head * HEAD_DIM) & (chan < (head + 1) * HEAD_DIM)).astype(jnp.bfloat16)
        dlT_ref[...] = _exact_dot_01(dob.astype(F32) * ov, in_head).T[:N_HEADS, :]

        @pl.when(i == 0)
        def _():
            acc_s[:D_CONV, :] = dwc
            acc_s[D_CONV:, :] = dwa
            dgc_ref[...] = dgc
            dga_ref[...] = dga

        @pl.when(i > 0)
        def _():
            acc_s[:D_CONV, :] += dwc
            acc_s[D_CONV:, :] += dwa
            dgc_ref[...] += dgc
            dga_ref[...] += dga

        @pl.when(i == I - 1)
        def _():
            dw_ref[...] = acc_s[...].astype(dw_ref.dtype)

    row = lambda w: pl.BlockSpec((tm, w), lambda i: (i, 0))
    full = lambda a, b: pl.BlockSpec((a, b), lambda i: (0, 0))
    return _pallas_call(
        body, name=name, grid=(I,),
        in_specs=[row(D), row(D_CONV), row(D_ATTN), full(1, D_CONV), full(1, D_ATTN), full(D_CONV + D_ATTN, D)],
        out_specs=[row(D_CONV), pl.BlockSpec((D_ATTN, tm), lambda i: (0, i)),
                   pl.BlockSpec((N_HEADS, tm), lambda i: (0, i)),
                   full(D_CONV + D_ATTN, D), full(1, D_CONV), full(1, D_ATTN)],
        out_shape=[
            jax.ShapeDtypeStruct((T, D_CONV), F32),
            jax.ShapeDtypeStruct((D_ATTN, T), MXU),
            jax.ShapeDtypeStruct((N_HEADS, T), F32),
            jax.ShapeDtypeStruct((D_CONV + D_ATTN, D), MXU),
            jax.ShapeDtypeStruct((1, D_CONV), F32),
            jax.ShapeDtypeStruct((1, D_ATTN), F32),
        ],
        scratch_shapes=[pltpu.VMEM((D_CONV + D_ATTN, D), F32)],
        compiler_params=_params(40, 1),
    )(dx2, c, o, gc, ga, wout)


def _ffn_down_loss(x, A, w2, gf, target, name):
    T, D = x.shape
    J, _, bf = A.shape
    tm = _tile(T, 512)

    def body(x_ref, A_ref, w2_ref, g_ref, t_ref, loss_ref, dx_ref, dg_ref):
        i = pl.program_id(0)
        f = _mm(A_ref[0], w2_ref[0:bf, :])
        for j in range(1, J):
            f = f + _mm(A_ref[j], w2_ref[j * bf:(j + 1) * bf, :])
        xv = x_ref[...] + 0.5 * f
        gv = g_ref[...]
        out, r = _rms_fwd(xv, gv)
        err = out - t_ref[...]
        part = jnp.full((1, LANES), 0.5 / D, F32) * jnp.sum(err * err)
        dxn, dgp = _rms_bwd(xv, r, gv, err * (1.0 / D))
        dx_ref[...] = dxn

        @pl.when(i == 0)
        def _():
            loss_ref[...] = part
            dg_ref[...] = dgp

        @pl.when(i > 0)
        def _():
            loss_ref[...] += part
            dg_ref[...] += dgp

    row = lambda w: pl.BlockSpec((tm, w), lambda i: (i, 0))
    full = lambda a, b: pl.BlockSpec((a, b), lambda i: (0, 0))
    return _pallas_call(
        body, name=name, grid=(T // tm,),
        in_specs=[row(D), pl.BlockSpec((J, tm, bf), lambda i: (0, i, 0)), full(J * bf, D), full(1, D), row(D)],
        out_specs=[full(1, LANES), row(D), full(1, D)],
        out_shape=[jax.ShapeDtypeStruct((1, LANES), F32), jax.ShapeDtypeStruct((T, D), F32),
                   jax.ShapeDtypeStruct((1, D), F32)],
        compiler_params=_params(56, 1),
    )(x, A, w2, gf, target)


def _cast_shards(arrays, name):
    n = len(arrays)

    def body(*refs):
        for src, dst in zip(refs[:n], refs[n:]):
            dst[...] = src[...].astype(dst.dtype)

    return _pallas_call(
        body, name=name, out_shape=[jax.ShapeDtypeStruct(a.shape, MXU) for a in arrays],
        compiler_params=pltpu.CompilerParams(vmem_limit_bytes=32 * MIB),
    )(*arrays)


def _row_tile(rows):
    for cand in (256, 176, 128, 64, 32, 16):
        if rows % cand == 0:
            return cand
    return rows


def _adamw(w, m, v, parts, name):
    R, C = w.shape
    P = parts.shape[0]
    tr = _row_tile(R)
    c1 = 1.0 - ADAM_B1 ** ADAM_STEP
    c2 = 1.0 - ADAM_B2 ** ADAM_STEP

    def body(w_ref, m_ref, v_ref, p_ref, g_ref, d_ref, nm_ref, nv_ref):
        g = p_ref[0].astype(F32)
        for s in range(1, P):
            g = g + p_ref[s].astype(F32)
        wv = w_ref[...]
        mn = ADAM_B1 * m_ref[...] + (1.0 - ADAM_B1) * g
        vn = ADAM_B2 * v_ref[...] + (1.0 - ADAM_B2) * (g * g)
        g_ref[...] = g
        nm_ref[...] = mn
        nv_ref[...] = vn
        d_ref[...] = -ADAM_LR * ((mn / c1) / (jnp.sqrt(vn / c2) + ADAM_EPS) + ADAM_WD * wv)

    blk = pl.BlockSpec((tr, C), lambda i: (i, 0))
    out = jax.ShapeDtypeStruct((R, C), F32)
    return _pallas_call(
        body, name=name, grid=(R // tr,),
        in_specs=[blk, blk, blk, pl.BlockSpec((P, tr, C), lambda i: (0, i, 0))],
        out_specs=[blk, blk, blk, blk],
        out_shape=[out, out, out, out],
        compiler_params=_params(32, 1),
    )(w, m, v, parts)


def _position():
    return lax.axis_index("x"), lax.axis_index("y"), lax.axis_index("c")


def _flat(px, py, pc):
    return 4 * px + 2 * py + pc


def _row_halves(rows, dtype):
    tile = _SUBLANES * (4 // jnp.dtype(dtype).itemsize)
    half = rows // 2 // tile * tile
    assert half > 0
    return (0, half), (half, rows - half)


def _gather_body(ins, outs, send_sems, recv_sems, local_sems, handshake):
    n = len(ins)
    x, y, c = _position()
    me, sibling = (x, y, c), (x, y, 1 - c)
    across_x, across_y, diagonal = (1 - x, y), (x, 1 - y), (1 - x, 1 - y)
    if handshake:
        _handshake([sibling] + [(*chip, cc) for chip in (across_x, across_y, diagonal) for cc in (c, 1 - c)])

    def copy(a, k, block, to, rows=None, src=None):
        dst = outs[a].at[_flat(*block)]
        if rows is not None:
            dst = dst.at[pl.ds(*rows)]
        return pltpu.make_async_remote_copy(
            src_ref=dst if src is None else src, dst_ref=dst,
            send_sem=send_sems.at[a, k], recv_sem=recv_sems.at[a, k],
            device_id=to, device_id_type=MESH)

    halves = [_row_halves(ins[a].shape[0], ins[a].dtype) for a in range(n)]
    mine = [pltpu.make_async_copy(ins[a], outs[a].at[_flat(*me)], local_sems.at[a]) for a in range(n)]
    for cp in mine:
        cp.start()
    sent = []

    def start(cp):
        cp.start()
        sent.append(cp)

    for a in range(n):
        start(copy(a, 0, me, sibling, src=ins[a]))
        start(copy(a, 1, me, (*across_x, c), src=ins[a]))
        start(copy(a, 2, me, (*across_y, c), src=ins[a]))
    for a in range(n):
        low, high = halves[a]
        copy(a, 1, (*across_x, c), me).wait_recv()
        start(copy(a, 3, (*across_x, c), (*across_y, c), rows=low))
        start(copy(a, 5, (*across_x, c), sibling))
        copy(a, 2, (*across_y, c), me).wait_recv()
        start(copy(a, 4, (*across_y, c), (*across_x, c), rows=high))
        start(copy(a, 6, (*across_y, c), sibling))
    for a in range(n):
        low, high = halves[a]
        copy(a, 3, (*diagonal, c), me, rows=low).wait_recv()
        start(copy(a, 7, (*diagonal, c), sibling, rows=low))
        copy(a, 4, (*diagonal, c), me, rows=high).wait_recv()
        start(copy(a, 8, (*diagonal, c), sibling, rows=high))
    for a in range(n):
        low, high = halves[a]
        copy(a, 0, sibling, me).wait_recv()
        copy(a, 5, (*across_x, 1 - c), me).wait_recv()
        copy(a, 6, (*across_y, 1 - c), me).wait_recv()
        copy(a, 7, (*diagonal, 1 - c), me, rows=low).wait_recv()
        copy(a, 8, (*diagonal, 1 - c), me, rows=high).wait_recv()
    for cp in sent:
        cp.wait_send()
    for cp in mine:
        cp.wait()


_GATHER_SLOTS = 9


def _gather_scratch(n):
    return [pltpu.SemaphoreType.DMA((n, _GATHER_SLOTS)), pltpu.SemaphoreType.DMA((n, _GATHER_SLOTS)),
            pltpu.SemaphoreType.DMA((n,))]


def _all_gather(shards, name):
    n = len(shards)

    def body(*refs):
        _gather_body(refs[:n], refs[n:2 * n], *refs[2 * n:], handshake=False)

    hbm = pl.BlockSpec(memory_space=pltpu.HBM)
    return _pallas_call(
        body, name=name,
        in_specs=[hbm] * n, out_specs=[hbm] * n,
        out_shape=[jax.ShapeDtypeStruct((N_DEV,) + s.shape, s.dtype) for s in shards],
        scratch_shapes=_gather_scratch(n),
    )(*shards)


def _handshake(peers):
    barrier = pltpu.get_barrier_semaphore()
    for peer in peers:
        pl.semaphore_signal(barrier, inc=1, device_id=peer, device_id_type=MESH)
    pl.semaphore_wait(barrier, len(peers))


def _sequencer_call(body, name, collective_id, out_type, scratch_types, operands):
    return pl.kernel(
        body, name=name, out_type=out_type,
        mesh=plsc.ScalarSubcoreMesh(axis_name="sequencer", num_cores=1),
        scratch_types=scratch_types,
        compiler_params=pltpu.CompilerParams(collective_id=collective_id),
    )(*operands)


def _seq_all_gather(shards, name, collective_id, after):
    n = len(shards)

    def body(*refs):
        _gather_body(refs[:n], refs[n + 1:2 * n + 1], *refs[2 * n + 1:], handshake=True)

    return _sequencer_call(
        body, name, collective_id,
        [jax.ShapeDtypeStruct((N_DEV,) + s.shape, s.dtype) for s in shards],
        _gather_scratch(n), list(shards) + [after])


def _seq_to_sibling(parts, name, collective_id, after):
    n = len(parts)

    def body(*refs):
        ins, outs = refs[:n], refs[n + len(after):2 * n + len(after)]
        send_sems, recv_sems = refs[2 * n + len(after):]
        x, y, c = _position()
        sibling = (x, y, 1 - c)
        _handshake([sibling])
        sent = []
        for a in range(n):
            for q in range(N_CHIPS):
                cp = pltpu.make_async_remote_copy(
                    src_ref=ins[a].at[2 * q + (1 - c)], dst_ref=outs[a].at[q],
                    send_sem=send_sems.at[a, q], recv_sem=recv_sems.at[a, q],
                    device_id=sibling, device_id_type=MESH)
                cp.start()
                sent.append(cp)
        for cp in sent:
            cp.wait_recv()
        for cp in sent:
            cp.wait_send()

    return _sequencer_call(
        body, name, collective_id,
        [jax.ShapeDtypeStruct((N_CHIPS,) + p.shape[1:], p.dtype) for p in parts],
        [pltpu.SemaphoreType.DMA((n, N_CHIPS)), pltpu.SemaphoreType.DMA((n, N_CHIPS))],
        list(parts) + list(after))


def _seq_to_chips(partials, name, collective_id):
    n = len(partials)

    def body(*refs):
        ins, outs = refs[:n], refs[n:2 * n]
        send_sems, recv_sems, local_sems = refs[2 * n:]
        x, y, c = _position()
        my_chip = 2 * x + y
        chips = [(1 - x, y), (x, 1 - y), (1 - x, 1 - y)]
        _handshake([(*chip, c) for chip in chips])
        mine = [pltpu.make_async_copy(ins[a].at[my_chip], outs[a].at[my_chip], local_sems.at[a]) for a in range(n)]
        for cp in mine:
            cp.start()
        sent = []
        for a in range(n):
            for j, (px, py) in enumerate(chips):
                cp = pltpu.make_async_remote_copy(
                    src_ref=ins[a].at[2 * px + py], dst_ref=outs[a].at[my_chip],
                    send_sem=send_sems.at[a, j], recv_sem=recv_sems.at[a, j],
                    device_id=(px, py, c), device_id_type=MESH)
                cp.start()
                sent.append(cp)
        for cp in sent:
            cp.wait_recv()
        for cp in sent:
            cp.wait_send()
        for cp in mine:
            cp.wait()

    return _sequencer_call(
        body, name, collective_id,
        [jax.ShapeDtypeStruct(p.shape, p.dtype) for p in partials],
        [pltpu.SemaphoreType.DMA((n, 3)), pltpu.SemaphoreType.DMA((n, 3)), pltpu.SemaphoreType.DMA((n,))],
        list(partials))


def _pair_add(parts, recvs, name, after=()):
    n = len(parts)
    core = lax.axis_index("c").astype(jnp.int32).reshape(1)

    def body(c_ref, *refs):
        ps, rs, outs = refs[:n], refs[n:2 * n], refs[2 * n + len(after):]
        for p_ref, r_ref, o_ref in zip(ps, rs, outs):
            o_ref[...] = (p_ref[...].astype(F32) + r_ref[...].astype(F32)).astype(o_ref.dtype)

    mine = lambda p: pl.BlockSpec((None,) + p.shape[1:], lambda q, c: (2 * q + c[0], 0, 0))
    blk = lambda p: pl.BlockSpec((None,) + p.shape[1:], lambda q, c: (q, 0, 0))
    return pl.pallas_call(
        body, name=name,
        grid_spec=pltpu.PrefetchScalarGridSpec(
            num_scalar_prefetch=1, grid=(N_CHIPS,),
            in_specs=[mine(p) for p in parts] + [blk(p) for p in parts] + [_UNREAD] * len(after),
            out_specs=[blk(p) for p in parts]),
        out_shape=[pltpu.HBM((N_CHIPS,) + p.shape[1:], p.dtype) for p in parts],
        compiler_params=_params(40, 1),
    )(core, *[pltpu.with_memory_space_constraint(a, pltpu.HBM) for a in (*parts, *recvs, *after)])


class _Reduced(NamedTuple):
    partials: list
    reduced: list


def _blocks(g):
    return g.reshape(N_DEV, -1, g.shape[-1])


def _reduce_scatter(parts, tag, ids, after=(), add_after=()):
    from_sibling = _seq_to_sibling(parts, "rs_sibling_" + tag, ids[0], after)
    partials = _pair_add(parts, from_sibling, "rs_add_" + tag, add_after)
    return _Reduced(partials, _seq_to_chips(partials, "rs_chips_" + tag, ids[1]))


_SMALL = ("ffn1_norm", "mix_norm", "conv_b", "conv_ln_g", "conv_ln_b", "forget_b", "out_norm_conv",
          "out_norm_attn", "ffn2_norm", "final_norm")
_PACK_WIDTH = 2 * D_CONV
_SLOT = dict(ffn1_norm=(0, 0), mix_norm=(1, 0), ffn2_norm=(2, 0), final_norm=(3, 0), conv_b=(4, 0),
             conv_ln_g=(4, D_CONV), conv_ln_b=(5, 0), out_norm_conv=(5, D_CONV), out_norm_attn=(6, 0),
             forget_b=(6, D_CONV))
_LOSS_ROW = 7
_CONV_ROW0 = 8
_PACK_ROWS = _CONV_ROW0 + CONV_HALO


def _pack_small(small, name):
    arrays = [small[n] for n in _SMALL] + [small["conv_w"], small["loss"]]

    def body(*refs):
        out = refs[-1]
        out[...] = jnp.zeros_like(out)
        for n, ref in zip(_SMALL, refs):
            row, lane = _SLOT[n]
            out[row:row + 1, lane:lane + ref.shape[1]] = ref[...]
        out[_CONV_ROW0:, :D_CONV] = refs[len(_SMALL)][...]
        out[_LOSS_ROW:_LOSS_ROW + 1, :LANES] = refs[len(_SMALL) + 1][...]

    return _pallas_call(body, name=name, out_shape=jax.ShapeDtypeStruct((_PACK_ROWS, _PACK_WIDTH), F32))(*arrays)


def _adamw_small(gathered, w, m, v, name):
    c1 = 1.0 - ADAM_B1 ** ADAM_STEP
    c2 = 1.0 - ADAM_B2 ** ADAM_STEP
    k = len(_SMALL)

    def body(g_ref, *refs):
        ws, ms, vs = refs[:k], refs[k:2 * k], refs[2 * k:3 * k]
        outs = refs[3 * k:]
        total = g_ref[0]
        for s in range(1, N_DEV):
            total = total + g_ref[s]
        for i, n in enumerate(_SMALL):
            row, lane = _SLOT[n]
            width = ws[i].shape[1]
            g = total[row:row + 1, lane:lane + width]
            mn = ADAM_B1 * ms[i][...] + (1.0 - ADAM_B1) * g
            vn = ADAM_B2 * vs[i][...] + (1.0 - ADAM_B2) * (g * g)
            o_g, o_d, o_m, o_v = outs[4 * i:4 * i + 4]
            o_g[...] = g
            o_m[...] = mn
            o_v[...] = vn
            o_d[...] = -ADAM_LR * ((mn / c1) / (jnp.sqrt(vn / c2) + ADAM_EPS) + ADAM_WD * ws[i][...])
        outs[4 * k][...] = total[_CONV_ROW0:, :D_CONV]
        outs[4 * k + 1][...] = total[_LOSS_ROW:_LOSS_ROW + 1, :LANES]

    shapes = []
    for n in _SMALL:
        shapes += [jax.ShapeDtypeStruct(w[n].shape, F32)] * 4
    shapes.append(jax.ShapeDtypeStruct((CONV_HALO, D_CONV), F32))
    shapes.append(jax.ShapeDtypeStruct((1, LANES), F32))
    res = _pallas_call(body, name=name, out_shape=shapes)(
        gathered, *[w[n] for n in _SMALL], *[m[n] for n in _SMALL], *[v[n] for n in _SMALL])
    return {n: res[4 * i:4 * i + 4] for i, n in enumerate(_SMALL)}, res[4 * k], res[4 * k + 1]


def _local_step(x, target, norms, shard):
    D = x.shape[1]
    J = N_DEV // 2
    as13 = lambda g: g.reshape(2, J, g.shape[1], D)

    (g13_1,) = _seq_all_gather([shard["ffn1_w13"]], "gather_ffn1_w13", 12, after=x)
    (g2_1,) = _seq_all_gather([shard["ffn1_w2"]], "gather_ffn1_w2", 10, after=g13_1)
    w13_1 = as13(g13_1)
    G1, U1, A1 = _ffn_up(x, norms["ffn1_norm"], w13_1, "ffn1_up")
    gin, gconv = _seq_all_gather([shard["w_in"], shard["conv_w"]], "gather_mix", 1, after=G1)
    w2_1 = g2_1.reshape(-1, D)
    x1 = _ffn_down(x, A1, w2_1, "ffn1_down")
    gout, g13_2, g2_2 = _seq_all_gather([shard["w_out"], shard["ffn2_w13"], shard["ffn2_w2"]], "gather_ffn2", 2,
                                        after=x1)
    wout = gout.reshape(-1, D)
    conv_w32 = jnp.pad(gconv.transpose(1, 0, 2).reshape(CONV_TAPS, D_CONV), ((0, CONV_HALO - CONV_TAPS), (0, 0)))

    ag, k, v, qT, kT, vT, fl = _inproj_fwd(x1, norms["mix_norm"], gin, "inproj_fwd")
    cum, cumT = _forget_fwd(fl, norms["forget_b"], "forget_fwd")
    yc, c = _conv_fwd(ag, conv_w32, norms["conv_b"], norms["conv_ln_g"], norms["conv_ln_b"], "conv_fwd")
    o, lseT = _attn_fwd(qT, k, vT, cum, cumT, "attn_fwd")
    x2 = _outproj_fwd(x1, c, o, norms["out_norm_conv"], norms["out_norm_attn"], wout, "outproj_fwd")
    w13_2, w2_2 = as13(g13_2), g2_2.reshape(-1, D)
    G2, U2, A2 = _ffn_up(x2, norms["ffn2_norm"], w13_2, "ffn2_up")
    loss, dx3, d_final = _ffn_down_loss(x2, A2, w2_2, norms["final_norm"], target, "ffn2_down_loss")

    dw2_2 = _ffn_w2_grad(dx3, A2, "ffn2_w2_grad")
    dx2, d_ffn2n, h3, dG2, dU2 = _ffn_bwd_act(x2, norms["ffn2_norm"], dx3, G2, U2, w13_2, w2_2, "ffn2_bwd_act")
    dw13_2 = _ffn_w13_grad(h3, dG2, dU2, "ffn2_w13_grad")
    dc, dobT, deltaT, dwout, d_onc, d_ona = _outproj_bwd(
        dx2, c, o, norms["out_norm_conv"], norms["out_norm_attn"], wout, "outproj_bwd")
    red_ffn2 = _reduce_scatter([_blocks(dw13_2), _blocks(dw2_2)], "ffn2", (3, 4), add_after=(dc,))
    dqT, dkT, dvT, dcum = _attn_bwd(qT, k, kT, v, dobT, lseT, deltaT, cum, cumT, "attn_bwd",
                                    after=red_ffn2.partials)
    dfl, d_fb = _forget_bwd(dcum, fl, norms["forget_b"], "forget_bwd")
    dag, d_convw, d_cb, d_lg, d_lb = _conv_bwd(dc, yc, ag, conv_w32, norms["conv_ln_g"], norms["conv_ln_b"], "conv_bwd")
    dx1, d_mixn, h2 = _inproj_bwd_act(x1, norms["mix_norm"], dx2, dag, dqT, dkT, dvT, dfl, gin, "inproj_bwd_act")
    dw2_1 = _ffn_w2_grad(dx1, A1, "ffn1_w2_grad")
    early = [_blocks(dwout), _blocks(dw2_1)]
    sib_early = _seq_to_sibling(early, "rs_sibling_mix_early", 11, red_ffn2.reduced[:1])
    dwin_blocks = _inproj_bwd_weights(h2, dag, dqT, dkT, dvT, dfl, "inproj_bwd_weights")
    sib_w_in = _seq_to_sibling([dwin_blocks], "rs_sibling_mix", 5, sib_early[:1])
    mix_partials = _pair_add([dwin_blocks] + early, sib_w_in + sib_early, "rs_add_mix")
    red_mix = _Reduced(mix_partials, _seq_to_chips(mix_partials, "rs_chips_mix", 6))
    dx, d_ffn1n, h1, dG1, dU1 = _ffn_bwd_act(x, norms["ffn1_norm"], dx1, G1, U1, w13_1, w2_1, "ffn1_bwd_act",
                                             after=red_mix.partials)
    small = dict(ffn1_norm=d_ffn1n, mix_norm=d_mixn, conv_b=d_cb, conv_ln_g=d_lg, conv_ln_b=d_lb,
                 forget_b=d_fb, out_norm_conv=d_onc, out_norm_attn=d_ona, ffn2_norm=d_ffn2n,
                 final_norm=d_final, conv_w=d_convw, loss=loss)
    packed_small = _pack_small(small, "pack_small_grads")
    (gathered_small,) = _seq_all_gather([packed_small], "gather_small_grads", 9, after=red_mix.partials[0])
    dw13_1 = _ffn_w13_grad(h1, dG1, dU1, "ffn1_w13_grad")
    red_w13_1 = _reduce_scatter([_blocks(dw13_1)], "ffn1_w13", (7, 8), after=[red_mix.reduced[0], gathered_small])
    big = dict(ffn1_w13=red_w13_1.reduced[0], ffn1_w2=red_mix.reduced[2], w_in=red_mix.reduced[0],
               w_out=red_mix.reduced[1], ffn2_w13=red_ffn2.reduced[0], ffn2_w2=red_ffn2.reduced[1])
    return dx, gathered_small, big


_BIG = ("ffn1_w13", "ffn1_w2", "w_in", "w_out", "ffn2_w13", "ffn2_w2")
_TRANSPOSED = ("ffn1_w13", "ffn2_w13", "w_in")
_ORDER = ("ffn1_norm", "ffn1_w13", "ffn1_w2", "mix_norm", "w_in", "conv_w", "conv_b", "conv_ln_g", "conv_ln_b",
          "forget_b", "out_norm_conv", "out_norm_attn", "w_out", "ffn2_norm", "ffn2_w13", "ffn2_w2", "final_norm")


def kernel(x, ffn1_norm, ffn1_w13, ffn1_w2, mix_norm, w_in, conv_w, conv_b, conv_ln_g, conv_ln_b, forget_b, out_norm_conv, out_norm_attn, w_out, ffn2_norm, ffn2_w13, ffn2_w2, final_norm, loss_target, m_ffn1_norm, m_ffn1_w13, m_ffn1_w2, m_mix_norm, m_w_in, m_conv_w, m_conv_b, m_conv_ln_g, m_conv_ln_b, m_forget_b, m_out_norm_conv, m_out_norm_attn, m_w_out, m_ffn2_norm, m_ffn2_w13, m_ffn2_w2, m_final_norm, v_ffn1_norm, v_ffn1_w13, v_ffn1_w2, v_mix_norm, v_w_in, v_conv_w, v_conv_b, v_conv_ln_g, v_conv_ln_b, v_forget_b, v_out_norm_conv, v_out_norm_attn, v_w_out, v_ffn2_norm, v_ffn2_w13, v_ffn2_w2, v_final_norm):
    w = dict(ffn1_norm=ffn1_norm, ffn1_w13=ffn1_w13, ffn1_w2=ffn1_w2, mix_norm=mix_norm, w_in=w_in, conv_w=conv_w,
             conv_b=conv_b, conv_ln_g=conv_ln_g, conv_ln_b=conv_ln_b, forget_b=forget_b, out_norm_conv=out_norm_conv,
             out_norm_attn=out_norm_attn, w_out=w_out, ffn2_norm=ffn2_norm, ffn2_w13=ffn2_w13, ffn2_w2=ffn2_w2,
             final_norm=final_norm)
    m = dict(ffn1_norm=m_ffn1_norm, ffn1_w13=m_ffn1_w13, ffn1_w2=m_ffn1_w2, mix_norm=m_mix_norm, w_in=m_w_in,
             conv_w=m_conv_w, conv_b=m_conv_b, conv_ln_g=m_conv_ln_g, conv_ln_b=m_conv_ln_b, forget_b=m_forget_b,
             out_norm_conv=m_out_norm_conv, out_norm_attn=m_out_norm_attn, w_out=m_w_out, ffn2_norm=m_ffn2_norm,
             ffn2_w13=m_ffn2_w13, ffn2_w2=m_ffn2_w2, final_norm=m_final_norm)
    v = dict(ffn1_norm=v_ffn1_norm, ffn1_w13=v_ffn1_w13, ffn1_w2=v_ffn1_w2, mix_norm=v_mix_norm, w_in=v_w_in,
             conv_w=v_conv_w, conv_b=v_conv_b, conv_ln_g=v_conv_ln_g, conv_ln_b=v_conv_ln_b, forget_b=v_forget_b,
             out_norm_conv=v_out_norm_conv, out_norm_attn=v_out_norm_attn, w_out=v_w_out, ffn2_norm=v_ffn2_norm,
             ffn2_w13=v_ffn2_w13, ffn2_w2=v_ffn2_w2, final_norm=v_final_norm)
    shapes = {n: a.shape for n, a in w.items()}
    T, D = x.shape[1], x.shape[2]
    def two(n, a):
        if a.ndim != 3:
            return a.reshape(1, -1)
        a = a.reshape(a.shape[-2], a.shape[-1])
        return a.T if n in _TRANSPOSED else a

    w2d = {n: two(n, a) for n, a in w.items()}
    m2d = {n: two(n, a) for n, a in m.items()}
    v2d = {n: two(n, a) for n, a in v.items()}

    first, later = _BIG[0], _BIG[1:]
    shard = {first: w2d[first].astype(MXU)}
    shard.update(zip(later, _cast_shards([w2d[n] for n in later], "cast_shards")))
    shard["conv_w"] = w2d["conv_w"]
    norms = {n: w2d[n] for n in _SMALL}
    norms["forget_b"] = jnp.pad(w2d["forget_b"], ((0, 0), (0, LANES - N_HEADS)))
    dx, gathered_small, big = _local_step(x[0], loss_target[0], norms, shard)

    grads, deltas, new_m, new_v = {}, {}, {}, {}
    for n in _BIG:
        g, d, nm, nv = _adamw(w2d[n], m2d[n], v2d[n], big[n], "adamw_" + n)
        grads[n], deltas[n], new_m[n], new_v[n] = g, d, nm, nv

    small_out, conv_g_full, loss = _adamw_small(gathered_small, w2d, m2d, v2d, "adamw_small")
    for n in _SMALL:
        grads[n], deltas[n], new_m[n], new_v[n] = small_out[n]
    conv_g_full = conv_g_full[:CONV_TAPS]
    xi, yi, ci = _position()
    cw = shapes["conv_w"][-1]
    conv_g_mine = lax.dynamic_slice_in_dim(conv_g_full, _flat(xi, yi, ci) * cw, cw, axis=1)
    g, d, nm, nv = _adamw(w2d["conv_w"], m2d["conv_w"], v2d["conv_w"], conv_g_mine[None], "adamw_conv_w")
    grads["conv_w"], deltas["conv_w"], new_m["conv_w"], new_v["conv_w"] = g, d, nm, nv

    shaped = lambda dct: [(dct[n].T if n in _TRANSPOSED else dct[n]).reshape(shapes[n]) for n in _ORDER]
    return (loss[0, 0], dx[None], *shaped(grads), *shaped(deltas), *shaped(new_m), *shaped(new_v))
```

```python
from typing import NamedTuple

import jax
import jax.numpy as jnp
from jax import lax
from jax.experimental import pallas as pl
from jax.experimental.pallas import tpu as pltpu
from jax.experimental.pallas import tpu_sc as plsc

F32 = jnp.float32
MXU = jnp.bfloat16
EPS = 1e-6
N_HEADS = 8
HEAD_DIM = 64
D_CONV = 512
D_ATTN = N_HEADS * HEAD_DIM
CONV_TAPS = 31
CONV_HALO = 32
SCALE = HEAD_DIM ** -0.5
NEG = -1e30
LANES = 128
N_DEV = 8
N_CHIPS = N_DEV // 2
MESH = pl.DeviceIdType.MESH
MIB = 1 << 20

ADAM_LR = 0.001
ADAM_B1 = 0.9
ADAM_B2 = 0.999
ADAM_EPS = 1e-08
ADAM_WD = 0.01
ADAM_STEP = 10


_UNREAD = pl.BlockSpec(memory_space=pl.ANY)


def _pallas_call(body, *, out_shape, **kwargs):
    in_hbm = lambda s: pltpu.HBM(s.shape, s.dtype)
    outs = [in_hbm(s) for s in out_shape] if isinstance(out_shape, (list, tuple)) else in_hbm(out_shape)
    call = pl.pallas_call(body, out_shape=outs, **kwargs)
    return lambda *operands: call(*[pltpu.with_memory_space_constraint(a, pltpu.HBM) for a in operands])


def _params(vmem_mib, n_axes):
    return pltpu.CompilerParams(dimension_semantics=("arbitrary",) * n_axes, vmem_limit_bytes=vmem_mib * MIB)


def _mm(a, b):
    return jnp.dot(a, b, preferred_element_type=F32)


def _mm_nt(a, b):
    return lax.dot_general(a, b, (((1,), (1,)), ((), ())), preferred_element_type=F32)


def _mm_tn(a, b):
    return lax.dot_general(a, b, (((0,), (0,)), ((), ())), preferred_element_type=F32)


def _rms_fwd(x, g):
    r = lax.rsqrt(jnp.mean(x * x, axis=-1, keepdims=True) + EPS)
    return x * r * g, r


def _rms_bwd(x, r, g, dy):
    gdy = dy * g
    dx = r * gdy - x * (r * r * r) * jnp.mean(x * gdy, axis=-1, keepdims=True)
    dg = jnp.sum(dy * x * r, axis=0, keepdims=True)
    return dx, dg


def _silu_grad(z, sz):
    return sz * (1.0 + z * (1.0 - sz))


def _three_terms(x):
    x1 = x.astype(jnp.bfloat16)
    r1 = x - x1.astype(F32)
    x2 = r1.astype(jnp.bfloat16)
    x3 = (r1 - x2.astype(F32)).astype(jnp.bfloat16)
    return x1, x2, x3


def _exact_tri_dot(tri, x):
    x1, x2, x3 = _three_terms(x)
    return _mm(tri, x1) + _mm(tri, x2) + _mm(tri, x3)


def _exact_dot_01(x, sel):
    x1, x2, x3 = _three_terms(x)
    return _mm(x1, sel) + _mm(x2, sel) + _mm(x3, sel)


def _tile(n, want):
    t = min(n, want)
    assert n % t == 0
    return t


_FFN_CHUNK = 256


def _ffn_up(x, g, w13, name):
    T, D = x.shape
    _, J, bf, _ = w13.shape
    tm = _tile(T, 1024)
    I = T // tm

    def body(x_ref, g_ref, w13_ref, G_ref, U_ref, A_ref, h_s):
        j = pl.program_id(0)
        i = pl.program_id(1)
        rows = pl.ds(pl.multiple_of(i * tm, tm), tm)

        @pl.when(j == 0)
        def _():
            h, _ = _rms_fwd(x_ref[...], g_ref[...])
            h_s[rows, :] = h.astype(MXU)

        chunks = [slice(r0, r0 + _FFN_CHUNK) for r0 in range(0, tm, _FFN_CHUNK)]
        hbs = [h_s[pl.ds(pl.multiple_of(i * tm + rs.start, _FFN_CHUNK), _FFN_CHUNK), :] for rs in chunks]
        GU = [(_mm_nt(hb, w13_ref[0]), _mm_nt(hb, w13_ref[1])) for hb in hbs]
        for rs, (G, U) in zip(chunks, GU):
            G_ref[rs, :] = G.astype(MXU)
            U_ref[rs, :] = U.astype(MXU)
            A_ref[rs, :] = (G * jax.nn.sigmoid(G) * U).astype(MXU)

    blk = pl.BlockSpec((None, tm, bf), lambda j, i: (j, i, 0))
    hid = jax.ShapeDtypeStruct((J, T, bf), MXU)
    return _pallas_call(
        body, name=name, grid=(J, I),
        in_specs=[pl.BlockSpec((tm, D), lambda j, i: (jnp.where(j == 0, i, I - 1), 0)),
                  pl.BlockSpec((1, D), lambda j, i: (0, 0)),
                  pl.BlockSpec((2, None, bf, D), lambda j, i: (0, j, 0, 0))],
        out_specs=[blk, blk, blk],
        out_shape=[hid, hid, hid],
        scratch_shapes=[pltpu.VMEM((T, D), MXU)],
        compiler_params=_params(48, 2),
    )(x, g, w13)


def _ffn_down(x, A, w2, name):
    T, D = x.shape
    J, _, bf = A.shape
    tm = _tile(T, 512)

    def body(x_ref, A_ref, w2_ref, xo_ref):
        f = _mm(A_ref[0], w2_ref[0:bf, :])
        for j in range(1, J):
            f = f + _mm(A_ref[j], w2_ref[j * bf:(j + 1) * bf, :])
        xo_ref[...] = x_ref[...] + 0.5 * f

    row = pl.BlockSpec((tm, D), lambda i: (i, 0))
    return _pallas_call(
        body, name=name, grid=(T // tm,),
        in_specs=[row, pl.BlockSpec((J, tm, bf), lambda i: (0, i, 0)), pl.BlockSpec((J * bf, D), lambda i: (0, 0))],
        out_specs=row,
        out_shape=jax.ShapeDtypeStruct((T, D), F32),
        compiler_params=_params(48, 1),
    )(x, A, w2)


def _ffn_bwd_act(x, g, dy, Gs, Us, w13, w2, name, after=()):
    T, D = x.shape
    _, J, bf, _ = w13.shape
    tm = _tile(T, 512)
    I = T // tm

    def body(x_ref, g_ref, dy_ref, G_ref, U_ref, w13_ref, w2_ref, *rest):
        dx_ref, dg_ref, h_ref, dG_ref, dU_ref, dh_s, dF_s, h_s = rest[len(after):]
        j = pl.program_id(0)
        i = pl.program_id(1)
        rows = pl.ds(pl.multiple_of(i * tm, tm), tm)

        @pl.when(j == 0)
        def _():
            h, _ = _rms_fwd(x_ref[...], g_ref[...])
            hb = h.astype(MXU)
            h_s[rows, :] = hb
            h_ref[...] = hb
            dF_s[rows, :] = (0.5 * dy_ref[...]).astype(MXU)
            dh_s[rows, :] = jnp.zeros((tm, D), F32)

        chunks = [slice(r0, r0 + _FFN_CHUNK) for r0 in range(0, tm, _FFN_CHUNK)]
        crows = [pl.ds(pl.multiple_of(i * tm + rs.start, _FFN_CHUNK), _FFN_CHUNK) for rs in chunks]
        dAs = [_mm_nt(dF_s[cr, :], w2_ref[...]) for cr in crows]
        for rs, cr, dA in zip(chunks, crows, dAs):
            G = G_ref[rs, :].astype(F32)
            U = U_ref[rs, :].astype(F32)
            sg = jax.nn.sigmoid(G)
            s = G * sg
            dUb = (dA * s).astype(MXU)
            dGb = (dA * U * _silu_grad(G, sg)).astype(MXU)
            dG_ref[rs, :] = dGb
            dU_ref[rs, :] = dUb
            dh_s[cr, :] += _mm(dGb, w13_ref[0]) + _mm(dUb, w13_ref[1])

        @pl.when(j == J - 1)
        def _():
            xv = x_ref[...]
            gv = g_ref[...]
            _, r = _rms_fwd(xv, gv)
            dxn, dgp = _rms_bwd(xv, r, gv, dh_s[rows, :])
            dx_ref[...] = dy_ref[...] + dxn

            @pl.when(i == 0)
            def _():
                dg_ref[...] = dgp

            @pl.when(i > 0)
            def _():
                dg_ref[...] += dgp

    ends = lambda j, i: (jnp.where((j == 0) | (j == J - 1), i, I - 1), 0)
    blk = pl.BlockSpec((None, tm, bf), lambda j, i: (j, i, 0))
    hid = jax.ShapeDtypeStruct((J, T, bf), MXU)
    return _pallas_call(
        body, name=name, grid=(J, I),
        in_specs=[pl.BlockSpec((tm, D), ends), pl.BlockSpec((1, D), lambda j, i: (0, 0)), pl.BlockSpec((tm, D), ends),
                  blk, blk, pl.BlockSpec((2, None, bf, D), lambda j, i: (0, j, 0, 0)),
                  pl.BlockSpec((bf, D), lambda j, i: (j, 0))] + [_UNREAD] * len(after),
        out_specs=[pl.BlockSpec((tm, D), lambda j, i: (jnp.where(j == J - 1, i, 0), 0)),
                   pl.BlockSpec((1, D), lambda j, i: (0, 0)),
                   pl.BlockSpec((tm, D), lambda j, i: (jnp.where(j == 0, i, I - 1), 0)), blk, blk],
        out_shape=[jax.ShapeDtypeStruct((T, D), F32), jax.ShapeDtypeStruct((1, D), F32),
                   jax.ShapeDtypeStruct((T, D), MXU), hid, hid],
        scratch_shapes=[pltpu.VMEM((T, D), F32), pltpu.VMEM((T, D), MXU), pltpu.VMEM((T, D), MXU)],
        compiler_params=_params(58, 2),
    )(x, g, dy, Gs, Us, w13, w2, *after)


def _ffn_w13_grad(h, dG, dU, name):
    T, D = h.shape
    J, _, bf = dG.shape

    def body(h_ref, dG_ref, dU_ref, dw13_ref):
        dw13_ref[0] = _mm_tn(dG_ref[...], h_ref[...]).astype(dw13_ref.dtype)
        dw13_ref[1] = _mm_tn(dU_ref[...], h_ref[...]).astype(dw13_ref.dtype)

    blk = pl.BlockSpec((None, T, bf), lambda j: (j, 0, 0))
    return _pallas_call(
        body, name=name, grid=(J,),
        in_specs=[pl.BlockSpec((T, D), lambda j: (0, 0)), blk, blk],
        out_specs=pl.BlockSpec((2, None, bf, D), lambda j: (0, j, 0, 0)),
        out_shape=jax.ShapeDtypeStruct((2, J, bf, D), MXU),
        compiler_params=_params(48, 1),
    )(h, dG, dU)


def _ffn_w2_grad(dy, A, name, after=()):
    T, D = dy.shape
    J, _, bf = A.shape

    def body(dy_ref, A_ref, *rest):
        dw2_ref, dF_s = rest[len(after):]

        @pl.when(pl.program_id(0) == 0)
        def _():
            dF_s[...] = (0.5 * dy_ref[...]).astype(MXU)

        dw2_ref[...] = _mm_tn(A_ref[...], dF_s[...]).astype(dw2_ref.dtype)

    return _pallas_call(
        body, name=name, grid=(J,),
        in_specs=[pl.BlockSpec((T, D), lambda j: (0, 0)), pl.BlockSpec((None, T, bf), lambda j: (j, 0, 0))]
        + [_UNREAD] * len(after),
        out_specs=pl.BlockSpec((bf, D), lambda j: (j, 0)),
        out_shape=jax.ShapeDtypeStruct((J * bf, D), MXU),
        scratch_shapes=[pltpu.VMEM((T, D), MXU)],
        compiler_params=_params(48, 1),
    )(dy, A, *after)


_AG0, _Q0, _K0, _V0, _F0 = 0, 2 * D_CONV, 2 * D_CONV + D_ATTN, 2 * D_CONV + 2 * D_ATTN, 2 * D_CONV + 3 * D_ATTN
N_IN = _F0 + N_HEADS
N_IN_PAD = _F0 + LANES
_IN_BLOCK = N_IN // N_DEV


def _rows_from_blocks(blocks_ref, rows_ref):
    for p in range(N_DEV):
        rows_ref[_IN_BLOCK * p:_IN_BLOCK * (p + 1), :] = blocks_ref[p]
    rows_ref[N_IN:, :] = jnp.zeros((N_IN_PAD - N_IN, rows_ref.shape[1]), rows_ref.dtype)


def _inproj_fwd(x1, gm, win_blocks, name):
    T, D = x1.shape
    tm = _tile(T, 512)

    def body(x_ref, g_ref, wb_ref, ag_ref, k_ref, v_ref, qT_ref, kT_ref, vT_ref, fl_ref, w_ref):
        @pl.when(pl.program_id(0) == 0)
        def _():
            _rows_from_blocks(wb_ref, w_ref)

        h, _ = _rms_fwd(x_ref[...], g_ref[...])
        hb = h.astype(MXU)
        ag_ref[...] = _mm_nt(hb, w_ref[_AG0:_Q0, :])
        qT_ref[...] = (_mm_nt(hb, w_ref[_Q0:_K0, :]) * SCALE).T.astype(MXU)
        for c0, ref, refT in ((_K0, k_ref, kT_ref), (_V0, v_ref, vT_ref)):
            y = _mm_nt(hb, w_ref[c0:c0 + D_ATTN, :])
            ref[...] = y.astype(MXU)
            refT[...] = y.T.astype(MXU)
        fl_ref[...] = _mm_nt(hb, w_ref[_F0:N_IN_PAD, :])

    row = lambda w: pl.BlockSpec((tm, w), lambda i: (i, 0))
    col = pl.BlockSpec((D_ATTN, tm), lambda i: (0, i))
    std = jax.ShapeDtypeStruct((T, D_ATTN), MXU)
    trn = jax.ShapeDtypeStruct((D_ATTN, T), MXU)
    return _pallas_call(
        body, name=name, grid=(T // tm,),
        in_specs=[row(D), pl.BlockSpec((1, D), lambda i: (0, 0)),
                  pl.BlockSpec((N_DEV, _IN_BLOCK, D), lambda i: (0, 0, 0))],
        out_specs=[row(2 * D_CONV), row(D_ATTN), row(D_ATTN), col, col, col, row(LANES)],
        out_shape=[jax.ShapeDtypeStruct((T, 2 * D_CONV), F32), std, std, trn, trn, trn,
                   jax.ShapeDtypeStruct((T, LANES), F32)],
        scratch_shapes=[pltpu.VMEM((N_IN_PAD, D), MXU)],
        compiler_params=_params(40, 1),
    )(x1, gm, win_blocks)


def _inproj_bwd_act(x1, gm, dx2, dag, dqT, dkT, dvT, dfl, win_blocks, name):
    T, D = x1.shape
    tm = _tile(T, 512)

    def body(x_ref, g_ref, dx2_ref, dag_ref, dqT_ref, dkT_ref, dvT_ref, dfl_ref, wb_ref, dx1_ref, dg_ref, h_ref,
             w_ref):
        i = pl.program_id(0)

        @pl.when(i == 0)
        def _():
            _rows_from_blocks(wb_ref, w_ref)

        xv = x_ref[...]
        gv = g_ref[...]
        h, r = _rms_fwd(xv, gv)
        h_ref[...] = h.astype(MXU)
        dh = _mm(dag_ref[...], w_ref[_AG0:_Q0, :])
        for c0, ref in ((_Q0, dqT_ref), (_K0, dkT_ref), (_V0, dvT_ref)):
            dh = dh + _mm_tn(ref[...].astype(MXU), w_ref[c0:c0 + D_ATTN, :])
        dh = dh + _mm(dfl_ref[...].astype(MXU), w_ref[_F0:N_IN_PAD, :])
        dxn, dgp = _rms_bwd(xv, r, gv, dh)
        dx1_ref[...] = dx2_ref[...] + dxn

        @pl.when(i == 0)
        def _():
            dg_ref[...] = dgp

        @pl.when(i > 0)
        def _():
            dg_ref[...] += dgp

    row = lambda w: pl.BlockSpec((tm, w), lambda i: (i, 0))
    col = pl.BlockSpec((D_ATTN, tm), lambda i: (0, i))
    full = lambda a, b: pl.BlockSpec((a, b), lambda i: (0, 0))
    return _pallas_call(
        body, name=name, grid=(T // tm,),
        in_specs=[row(D), full(1, D), row(D), row(2 * D_CONV), col, col, col, row(LANES),
                  pl.BlockSpec((N_DEV, _IN_BLOCK, D), lambda i: (0, 0, 0))],
        out_specs=[row(D), full(1, D), row(D)],
        out_shape=[jax.ShapeDtypeStruct((T, D), F32), jax.ShapeDtypeStruct((1, D), F32),
                   jax.ShapeDtypeStruct((T, D), MXU)],
        scratch_shapes=[pltpu.VMEM((N_IN_PAD, D), MXU)],
        compiler_params=_params(40, 1),
    )(x1, gm, dx2, dag, dqT, dkT, dvT, dfl, win_blocks)


def _inproj_bwd_weights(h, dag, dqT, dkT, dvT, dfl, name, after=()):
    T, D = h.shape

    def body(h_ref, dag_ref, dqT_ref, dkT_ref, dvT_ref, dfl_ref, *rest):
        blocks_ref, dw_ref = rest[len(after):]
        hb = h_ref[...]
        dw_ref[_AG0:_Q0, :] = _mm_tn(dag_ref[...], hb).astype(dw_ref.dtype)
        for c0, ref in ((_Q0, dqT_ref), (_K0, dkT_ref), (_V0, dvT_ref)):
            dw_ref[c0:c0 + D_ATTN, :] = _mm(ref[...].astype(MXU), hb).astype(dw_ref.dtype)
        dw_ref[_F0:N_IN_PAD, :] = _mm_tn(dfl_ref[...].astype(MXU), hb).astype(dw_ref.dtype)
        for p in range(N_DEV):
            blocks_ref[p] = dw_ref[_IN_BLOCK * p:_IN_BLOCK * (p + 1), :]

    vmem = pl.BlockSpec(memory_space=pltpu.VMEM)
    return _pallas_call(
        body, name=name, in_specs=[vmem] * 6 + [_UNREAD] * len(after), out_specs=vmem,
        out_shape=jax.ShapeDtypeStruct((N_DEV, _IN_BLOCK, D), MXU),
        scratch_shapes=[pltpu.VMEM((N_IN_PAD, D), MXU)],
        compiler_params=pltpu.CompilerParams(vmem_limit_bytes=56 * MIB),
    )(h, dag, dqT, dkT, dvT, dfl, *after)


def _forget_fwd(fl, fbp, name):
    T = fl.shape[0]
    tb = _tile(T, 256)

    def body(fl_ref, fb_ref, cum_ref, cumT_ref):
        ri = lax.broadcasted_iota(jnp.int32, (tb, tb), 0)
        ci = lax.broadcasted_iota(jnp.int32, (tb, tb), 1)
        tri = (ri >= ci).astype(jnp.bfloat16)
        carry = jnp.zeros((1, LANES), F32)
        for b in range(T // tb):
            z = fl_ref[b * tb:(b + 1) * tb, :] + fb_ref[...]
            lf = jnp.minimum(z, 0.0) - jnp.log1p(jnp.exp(-jnp.abs(z)))
            c = _exact_tri_dot(tri, lf) + carry
            cum_ref[b * tb:(b + 1) * tb, :] = c
            carry = c[tb - 1:tb, :]
        cumT_ref[...] = cum_ref[...].T[:N_HEADS, :]

    return _pallas_call(
        body, name=name,
        out_shape=[jax.ShapeDtypeStruct((T, LANES), F32), jax.ShapeDtypeStruct((N_HEADS, T), F32)],
        compiler_params=pltpu.CompilerParams(vmem_limit_bytes=32 * MIB),
    )(fl, fbp)


def _forget_bwd(dcum, fl, fbp, name):
    T = fl.shape[0]
    tb = _tile(T, 256)

    def body(dc_ref, fl_ref, fb_ref, dfl_ref, dfb_ref):
        ri = lax.broadcasted_iota(jnp.int32, (tb, tb), 0)
        ci = lax.broadcasted_iota(jnp.int32, (tb, tb), 1)
        tri = (ri <= ci).astype(jnp.bfloat16)
        carry = jnp.zeros((1, LANES), F32)
        dfb = jnp.zeros((1, LANES), F32)
        for b in reversed(range(T // tb)):
            sl = slice(b * tb, (b + 1) * tb)
            dl = _exact_tri_dot(tri, dc_ref[sl, :]) + carry
            carry = dl[0:1, :]
            z = fl_ref[sl, :] + fb_ref[...]
            dfl = dl * jax.nn.sigmoid(-z)
            dfl_ref[sl, :] = dfl
            dfb = dfb + jnp.sum(dfl, axis=0, keepdims=True)
        dfb_ref[...] = dfb

    return _pallas_call(
        body, name=name,
        out_shape=[jax.ShapeDtypeStruct((T, LANES), F32), jax.ShapeDtypeStruct((1, LANES), F32)],
        compiler_params=pltpu.CompilerParams(vmem_limit_bytes=32 * MIB),
    )(dcum, fl, fbp)


def _causal_keep(i, j, tq, tk):
    key = j * tk + lax.broadcasted_iota(jnp.int32, (tk, tq), 0)
    qry = i * tq + lax.broadcasted_iota(jnp.int32, (tk, tq), 1)
    return key <= qry


def _split_hi_lo(x):
    hi = x.astype(MXU)
    lo = (x - hi.astype(F32)).astype(MXU)
    return hi, lo


def _attn_fwd(qT, k, vT, cum, cumT, name):
    T = k.shape[0]
    tq = _tile(T, 256)
    tk = _tile(tq, 256)
    kpq = tq // tk
    heads = [slice(HEAD_DIM * h, HEAD_DIM * (h + 1)) for h in range(N_HEADS)]

    def body(qT_ref, k_ref, vT_ref, cum_ref, cumT_ref, o_ref, lseT_ref, acc_s, m_s, l_s):
        i = pl.program_id(0)
        acc_s[...] = jnp.zeros_like(acc_s)
        m_s[...] = jnp.full_like(m_s, NEG)
        l_s[...] = jnp.zeros_like(l_s)

        def kblock(j, masked):
            rows = pl.ds(pl.multiple_of(j * tk, tk), tk)
            keep = _causal_keep(i, j, tq, tk) if masked else None
            bias = [cumT_ref[h:h + 1, :] - cum_ref[rows, h:h + 1] for h in range(N_HEADS)]
            qk = [_mm(k_ref[rows, hs], qT_ref[hs, :]) + bias[h] for h, hs in enumerate(heads)]
            for h, hs in enumerate(heads):
                sT = qk[h]
                if masked:
                    sT = jnp.where(keep, sT, NEG)
                m_old = m_s[h:h + 1, :]
                m_new = jnp.maximum(m_old, jnp.max(sT, axis=0, keepdims=True))
                alpha = jnp.exp(m_old - m_new)
                pT = jnp.exp(sT - m_new)
                l_s[h:h + 1, :] = alpha * l_s[h:h + 1, :] + jnp.sum(pT, axis=0, keepdims=True)
                p_hi, p_lo = _split_hi_lo(pT)
                vh = vT_ref[hs, rows]
                acc_s[hs, :] = alpha * acc_s[hs, :] + (_mm(vh, p_hi) + _mm(vh, p_lo))
                m_s[h:h + 1, :] = m_new

        def unmasked(j, c):
            kblock(j, False)
            return c

        lax.fori_loop(0, kpq * i, unmasked, 0)
        for d in range(kpq):
            kblock(kpq * i + d, True)
        for h, hs in enumerate(heads):
            acc_s[hs, :] = acc_s[hs, :] / l_s[h:h + 1, :]
        o_ref[...] = acc_s[...].T
        lseT_ref[...] = m_s[...] + jnp.log(l_s[...])

    full = lambda a, b: pl.BlockSpec((a, b), lambda i: (0, 0))
    colblk = lambda r: pl.BlockSpec((r, tq), lambda i: (0, i))
    return _pallas_call(
        body, name=name, grid=(T // tq,),
        in_specs=[colblk(D_ATTN), full(T, D_ATTN), full(D_ATTN, T), full(T, LANES), colblk(N_HEADS)],
        out_specs=[pl.BlockSpec((tq, D_ATTN), lambda i: (i, 0)), colblk(N_HEADS)],
        out_shape=[jax.ShapeDtypeStruct((T, D_ATTN), F32), jax.ShapeDtypeStruct((N_HEADS, T), F32)],
        scratch_shapes=[pltpu.VMEM((D_ATTN, tq), F32), pltpu.VMEM((N_HEADS, tq), F32),
                        pltpu.VMEM((N_HEADS, tq), F32)],
        compiler_params=_params(40, 1),
    )(qT, k, vT, cum, cumT)


def _attn_bwd(qT, k, kT, v, doT, lseT, deltaT, cum, cumT, name, after=()):
    T = k.shape[0]
    tq = _tile(T, 256)
    tk = _tile(tq, 256)
    kpq = tq // tk
    heads = [slice(HEAD_DIM * h, HEAD_DIM * (h + 1)) for h in range(N_HEADS)]

    def body(qT_ref, k_ref, kT_ref, v_ref, doT_ref, lseT_ref, dlT_ref, cum_ref, cumT_ref, *rest):
        dq_ref, dk_ref, dv_ref, dcum_ref, dq_s = rest[len(after):]
        i = pl.program_id(0)

        @pl.when(i == 0)
        def _():
            dk_ref[...] = jnp.zeros_like(dk_ref)
            dv_ref[...] = jnp.zeros_like(dv_ref)
            dcum_ref[...] = jnp.zeros_like(dcum_ref)

        dq_s[...] = jnp.zeros_like(dq_s)

        def kblock(j, masked):
            rows = pl.ds(pl.multiple_of(j * tk, tk), tk)
            keep = _causal_keep(i, j, tq, tk) if masked else None
            bias = [cumT_ref[h:h + 1, :] - cum_ref[rows, h:h + 1] for h in range(N_HEADS)]
            qk = [_mm(k_ref[rows, hs], qT_ref[hs, :]) + bias[h] for h, hs in enumerate(heads)]
            dps = [_mm(v_ref[rows, hs], doT_ref[hs, :]) for hs in heads]
            for h, hs in enumerate(heads):
                sT = qk[h]
                if masked:
                    sT = jnp.where(keep, sT, NEG)
                pT = jnp.exp(sT - lseT_ref[h:h + 1, :])
                dsT = pT * (dps[h] - dlT_ref[h:h + 1, :])
                dcum_ref[rows, h:h + 1] += -jnp.sum(dsT, axis=1, keepdims=True)
                dsb = dsT.astype(MXU)
                dv_ref[hs, rows] += _mm_nt(doT_ref[hs, :], pT.astype(MXU))
                dk_ref[hs, rows] += _mm_nt(qT_ref[hs, :], dsb)
                dq_s[hs, :] += _mm(kT_ref[hs, rows], dsb)

        def unmasked(j, c):
            kblock(j, False)
            return c

        lax.fori_loop(0, kpq * i, unmasked, 0)
        for d in range(kpq):
            kblock(kpq * i + d, True)
        dq_ref[...] = (dq_s[...] * SCALE).astype(dq_ref.dtype)

    full = lambda a, b: pl.BlockSpec((a, b), lambda i: (0, 0))
    colblk = lambda r: pl.BlockSpec((r, tq), lambda i: (0, i))
    return _pallas_call(
        body, name=name, grid=(T // tq,),
        in_specs=[colblk(D_ATTN), full(T, D_ATTN), full(D_ATTN, T), full(T, D_ATTN), colblk(D_ATTN),
                  colblk(N_HEADS), colblk(N_HEADS), full(T, LANES), colblk(N_HEADS)] + [_UNREAD] * len(after),
        out_specs=[colblk(D_ATTN), full(D_ATTN, T), full(D_ATTN, T), full(T, LANES)],
        out_shape=[
            jax.ShapeDtypeStruct((D_ATTN, T), MXU),
            jax.ShapeDtypeStruct((D_ATTN, T), F32),
            jax.ShapeDtypeStruct((D_ATTN, T), F32),
            jax.ShapeDtypeStruct((T, LANES), F32),
        ],
        scratch_shapes=[pltpu.VMEM((D_ATTN, tq), F32)],
        compiler_params=_params(48, 1),
    )(qT, k, kT, v, doT, lseT, deltaT, cum, cumT, *after)


_ROWS_PER_CHUNK = 64


def _glu_halo(ag_ref, agh_ref, uext_s, tm, first):
    a = ag_ref[:, :D_CONV]
    sg = jax.nn.sigmoid(ag_ref[:, D_CONV:])
    uh = agh_ref[:, :D_CONV] * jax.nn.sigmoid(agh_ref[:, D_CONV:])
    uext_s[0:CONV_HALO, :] = jnp.where(first, 0.0, uh)
    uext_s[CONV_HALO:CONV_HALO + tm, :] = a * sg
    return a, sg


_SUBLANES = 8


def _shifted_copies(ext_s, sh_s, rows):
    for k in range(1, _SUBLANES):
        sh_s[k, 0:rows, :] = ext_s[pl.ds(k, rows), :]


def _window(ext_s, sh_s, start, rows):
    k = start % _SUBLANES
    if k == 0:
        return ext_s[pl.ds(start, rows), :]
    return sh_s[k, pl.ds(start - k, rows), :]


def _layer_norm_stats(y):
    mu = jnp.mean(y, axis=-1, keepdims=True)
    xc = y - mu
    rs = lax.rsqrt(jnp.mean(xc * xc, axis=-1, keepdims=True) + EPS)
    return xc * rs, rs


def _conv_fwd(ag, w32, cb, lg, lb, name):
    T = ag.shape[0]
    tm = _tile(T, 512)
    rc = _tile(tm, _ROWS_PER_CHUNK)
    hb = tm // CONV_HALO

    def body(ag_ref, agh_ref, w_ref, cb_ref, lg_ref, lb_ref, yc_ref, c_ref, uext_s, ush_s):
        i = pl.program_id(0)
        _glu_halo(ag_ref, agh_ref, uext_s, tm, i == 0)
        _shifted_copies(uext_s, ush_s, tm + CONV_HALO - _SUBLANES)
        for r0 in range(0, tm, rc):
            acc = jnp.zeros((rc, D_CONV), F32)
            for t in range(CONV_TAPS):
                acc = acc + _window(uext_s, ush_s, r0 + CONV_HALO - (CONV_TAPS - 1) + t, rc) * w_ref[t:t + 1, :]
            y = acc + cb_ref[...]
            yc_ref[r0:r0 + rc, :] = y
            n, _ = _layer_norm_stats(y)
            z = n * lg_ref[...] + lb_ref[...]
            c_ref[r0:r0 + rc, :] = z * jax.nn.sigmoid(z)

    row = lambda w: pl.BlockSpec((tm, w), lambda i: (i, 0))
    full = lambda a, b: pl.BlockSpec((a, b), lambda i: (0, 0))
    return _pallas_call(
        body, name=name, grid=(T // tm,),
        in_specs=[row(2 * D_CONV),
                  pl.BlockSpec((CONV_HALO, 2 * D_CONV), lambda i: (jnp.maximum(i * hb - 1, 0), 0)),
                  full(CONV_HALO, D_CONV), full(1, D_CONV), full(1, D_CONV), full(1, D_CONV)],
        out_specs=[row(D_CONV), row(D_CONV)],
        out_shape=[jax.ShapeDtypeStruct((T, D_CONV), F32), jax.ShapeDtypeStruct((T, D_CONV), F32)],
        scratch_shapes=[pltpu.VMEM((CONV_HALO + tm, D_CONV), F32),
                        pltpu.VMEM((_SUBLANES, CONV_HALO + tm, D_CONV), F32)],
        compiler_params=_params(32, 1),
    )(ag, ag, w32, cb, lg, lb)


def _conv_bwd(dc, yc, ag, w32, lg, lb, name):
    T = ag.shape[0]
    tm = _tile(T, 512)
    rc = _tile(tm, _ROWS_PER_CHUNK)
    I = T // tm
    hb = tm // CONV_HALO
    n_halo_blocks = T // CONV_HALO

    def body(dc_ref, yc_ref, dch_ref, ych_ref, ag_ref, agh_ref, w_ref, lg_ref, lb_ref,
             dag_ref, dw_ref, dcb_ref, dlg_ref, dlb_ref, uext_s, dext_s, ush_s, dsh_s):
        i = pl.program_id(0)
        lgv = lg_ref[...]
        lbv = lb_ref[...]

        def ln_bwd(dcv, ycv):
            n, rs = _layer_norm_stats(ycv)
            z = n * lgv + lbv
            dz = dcv * _silu_grad(z, jax.nn.sigmoid(z))
            dn = dz * lgv
            dy = rs * (dn - jnp.mean(dn, axis=-1, keepdims=True) - n * jnp.mean(dn * n, axis=-1, keepdims=True))
            return dy, dz, n

        dy, dz, n = ln_bwd(dc_ref[...], yc_ref[...])
        dyh, _, _ = ln_bwd(dch_ref[...], ych_ref[...])
        dext_s[0:tm, :] = dy
        dext_s[tm:tm + CONV_HALO, :] = jnp.where(i == I - 1, 0.0, dyh)
        a, sg = _glu_halo(ag_ref, agh_ref, uext_s, tm, i == 0)
        _shifted_copies(uext_s, ush_s, tm + CONV_HALO - _SUBLANES)
        _shifted_copies(dext_s, dsh_s, tm + CONV_HALO - _SUBLANES)

        @pl.when(i == 0)
        def _():
            dw_ref[...] = jnp.zeros_like(dw_ref)
            dcb_ref[...] = jnp.zeros_like(dcb_ref)
            dlg_ref[...] = jnp.zeros_like(dlg_ref)
            dlb_ref[...] = jnp.zeros_like(dlb_ref)

        dcb_ref[...] += jnp.sum(dy, axis=0, keepdims=True)
        dlg_ref[...] += jnp.sum(dz * n, axis=0, keepdims=True)
        dlb_ref[...] += jnp.sum(dz, axis=0, keepdims=True)
        for t in range(CONV_TAPS):
            u_t = _window(uext_s, ush_s, CONV_HALO - (CONV_TAPS - 1) + t, tm)
            dw_ref[t:t + 1, :] += jnp.sum(dy * u_t, axis=0, keepdims=True)
        for r0 in range(0, tm, rc):
            acc = jnp.zeros((rc, D_CONV), F32)
            for t in range(CONV_TAPS):
                acc = acc + _window(dext_s, dsh_s, r0 + (CONV_TAPS - 1) - t, rc) * w_ref[t:t + 1, :]
            a_c = a[r0:r0 + rc, :]
            sg_c = sg[r0:r0 + rc, :]
            dag_ref[r0:r0 + rc, :D_CONV] = (acc * sg_c).astype(dag_ref.dtype)
            dag_ref[r0:r0 + rc, D_CONV:] = (acc * a_c * sg_c * (1.0 - sg_c)).astype(dag_ref.dtype)

    row = lambda w: pl.BlockSpec((tm, w), lambda i: (i, 0))
    full = lambda a, b: pl.BlockSpec((a, b), lambda i: (0, 0))
    nxt = pl.BlockSpec((CONV_HALO, D_CONV), lambda i: (jnp.minimum((i + 1) * hb, n_halo_blocks - 1), 0))
    return _pallas_call(
        body, name=name, grid=(I,),
        in_specs=[row(D_CONV), row(D_CONV), nxt, nxt, row(2 * D_CONV),
                  pl.BlockSpec((CONV_HALO, 2 * D_CONV), lambda i: (jnp.maximum(i * hb - 1, 0), 0)),
                  full(CONV_HALO, D_CONV), full(1, D_CONV), full(1, D_CONV)],
        out_specs=[row(2 * D_CONV), full(CONV_HALO, D_CONV), full(1, D_CONV), full(1, D_CONV), full(1, D_CONV)],
        out_shape=[
            jax.ShapeDtypeStruct((T, 2 * D_CONV), MXU),
            jax.ShapeDtypeStruct((CONV_HALO, D_CONV), F32),
            jax.ShapeDtypeStruct((1, D_CONV), F32),
            jax.ShapeDtypeStruct((1, D_CONV), F32),
            jax.ShapeDtypeStruct((1, D_CONV), F32),
        ],
        scratch_shapes=[pltpu.VMEM((CONV_HALO + tm, D_CONV), F32), pltpu.VMEM((tm + CONV_HALO, D_CONV), F32),
                        pltpu.VMEM((_SUBLANES, CONV_HALO + tm, D_CONV), F32),
                        pltpu.VMEM((_SUBLANES, CONV_HALO + tm, D_CONV), F32)],
        compiler_params=_params(40, 1),
    )(dc, yc, dc, yc, ag, ag, w32, lg, lb)


def _outproj_fwd(x1, c, o, gc, ga, wout, name):
    T, D = x1.shape
    tm = _tile(T, 512)

    def body(x_ref, c_ref, o_ref, gc_ref, ga_ref, w_ref, x2_ref):
        yc, _ = _rms_fwd(c_ref[...], gc_ref[...])
        ya, _ = _rms_fwd(o_ref[...], ga_ref[...])
        x2_ref[...] = (x_ref[...] + _mm(yc.astype(MXU), w_ref[:D_CONV, :])
                       + _mm(ya.astype(MXU), w_ref[D_CONV:, :]))

    row = lambda w: pl.BlockSpec((tm, w), lambda i: (i, 0))
    full = lambda a, b: pl.BlockSpec((a, b), lambda i: (0, 0))
    return _pallas_call(
        body, name=name, grid=(T // tm,),
        in_specs=[row(D), row(D_CONV), row(D_ATTN), full(1, D_CONV), full(1, D_ATTN), full(D_CONV + D_ATTN, D)],
        out_specs=row(D),
        out_shape=jax.ShapeDtypeStruct((T, D), F32),
        compiler_params=_params(32, 1),
    )(x1, c, o, gc, ga, wout)


def _outproj_bwd(dx2, c, o, gc, ga, wout, name):
    T, D = dx2.shape
    tm = _tile(T, 512)
    I = T // tm

    def body(dx_ref, c_ref, o_ref, gc_ref, ga_ref, w_ref,
             dc_ref, doT_ref, dlT_ref, dw_ref, dgc_ref, dga_ref, acc_s):
        i = pl.program_id(0)
        dxb = dx_ref[...].astype(MXU)
        cv = c_ref[...]
        ov = o_ref[...]
        yc, rcn = _rms_fwd(cv, gc_ref[...])
        ya, ra = _rms_fwd(ov, ga_ref[...])
        dyc = _mm_nt(dxb, w_ref[:D_CONV, :])
        dya = _mm_nt(dxb, w_ref[D_CONV:, :])
        dwc = _mm_tn(yc.astype(MXU), dxb)
        dwa = _mm_tn(ya.astype(MXU), dxb)
        dcv, dgc = _rms_bwd(cv, rcn, gc_ref[...], dyc)
        dov, dga = _rms_bwd(ov, ra, ga_ref[...], dya)
        dc_ref[...] = dcv
        dob = dov.astype(doT_ref.dtype)
        doT_ref[...] = dov.T.astype(doT_ref.dtype)
        chan = lax.broadcasted_iota(jnp.int32, (D_ATTN, LANES), 0)
        head = lax.broadcasted_iota(jnp.int32, (D_ATTN, LANES), 1)
        in_head = ((chan >= head * HEAD_DIM) & (chan < (head + 1) * HEAD_DIM)).astype(jnp.bfloat16)
        dlT_ref[...] = _exact_dot_01(dob.astype(F32) * ov, in_head).T[:N_HEADS, :]

        @pl.when(i == 0)
        def _():
            acc_s[:D_CONV, :] = dwc
            acc_s[D_CONV:, :] = dwa
            dgc_ref[...] = dgc
            dga_ref[...] = dga

        @pl.when(i > 0)
        def _():
            acc_s[:D_CONV, :] += dwc
            acc_s[D_CONV:, :] += dwa
            dgc_ref[...] += dgc
            dga_ref[...] += dga

        @pl.when(i == I - 1)
        def _():
            dw_ref[...] = acc_s[...].astype(dw_ref.dtype)

    row = lambda w: pl.BlockSpec((tm, w), lambda i: (i, 0))
    full = lambda a, b: pl.BlockSpec((a, b), lambda i: (0, 0))
    return _pallas_call(
        body, name=name, grid=(I,),
        in_specs=[row(D), row(D_CONV), row(D_ATTN), full(1, D_CONV), full(1, D_ATTN), full(D_CONV + D_ATTN, D)],
        out_specs=[row(D_CONV), pl.BlockSpec((D_ATTN, tm), lambda i: (0, i)),
                   pl.BlockSpec((N_HEADS, tm), lambda i: (0, i)),
                   full(D_CONV + D_ATTN, D), full(1, D_CONV), full(1, D_ATTN)],
        out_shape=[
            jax.ShapeDtypeStruct((T, D_CONV), F32),
            jax.ShapeDtypeStruct((D_ATTN, T), MXU),
            jax.ShapeDtypeStruct((N_HEADS, T), F32),
            jax.ShapeDtypeStruct((D_CONV + D_ATTN, D), MXU),
            jax.ShapeDtypeStruct((1, D_CONV), F32),
            jax.ShapeDtypeStruct((1, D_ATTN), F32),
        ],
        scratch_shapes=[pltpu.VMEM((D_CONV + D_ATTN, D), F32)],
        compiler_params=_params(40, 1),
    )(dx2, c, o, gc, ga, wout)


def _ffn_down_loss(x, A, w2, gf, target, name):
    T, D = x.shape
    J, _, bf = A.shape
    tm = _tile(T, 512)

    def body(x_ref, A_ref, w2_ref, g_ref, t_ref, loss_ref, dx_ref, dg_ref):
        i = pl.program_id(0)
        f = _mm(A_ref[0], w2_ref[0:bf, :])
        for j in range(1, J):
            f = f + _mm(A_ref[j], w2_ref[j * bf:(j + 1) * bf, :])
        xv = x_ref[...] + 0.5 * f
        gv = g_ref[...]
        out, r = _rms_fwd(xv, gv)
        err = out - t_ref[...]
        part = jnp.full((1, LANES), 0.5 / D, F32) * jnp.sum(err * err)
        dxn, dgp = _rms_bwd(xv, r, gv, err * (1.0 / D))
        dx_ref[...] = dxn

        @pl.when(i == 0)
        def _():
            loss_ref[...] = part
            dg_ref[...] = dgp

        @pl.when(i > 0)
        def _():
            loss_ref[...] += part
            dg_ref[...] += dgp

    row = lambda w: pl.BlockSpec((tm, w), lambda i: (i, 0))
    full = lambda a, b: pl.BlockSpec((a, b), lambda i: (0, 0))
    return _pallas_call(
        body, name=name, grid=(T // tm,),
        in_specs=[row(D), pl.BlockSpec((J, tm, bf), lambda i: (0, i, 0)), full(J * bf, D), full(1, D), row(D)],
        out_specs=[full(1, LANES), row(D), full(1, D)],
        out_shape=[jax.ShapeDtypeStruct((1, LANES), F32), jax.ShapeDtypeStruct((T, D), F32),
                   jax.ShapeDtypeStruct((1, D), F32)],
        compiler_params=_params(56, 1),
    )(x, A, w2, gf, target)


def _row_tile(rows):
    for cand in (256, 176, 128, 64, 32, 16):
        if rows % cand == 0:
            return cand
    return rows


def _adamw(w, m, v, parts, name):
    R, C = w.shape
    P = parts.shape[0]
    tr = _row_tile(R)
    c1 = 1.0 - ADAM_B1 ** ADAM_STEP
    c2 = 1.0 - ADAM_B2 ** ADAM_STEP

    def body(w_ref, m_ref, v_ref, p_ref, g_ref, d_ref, nm_ref, nv_ref):
        g = p_ref[0].astype(F32)
        for s in range(1, P):
            g = g + p_ref[s].astype(F32)
        wv = w_ref[...]
        mn = ADAM_B1 * m_ref[...] + (1.0 - ADAM_B1) * g
        vn = ADAM_B2 * v_ref[...] + (1.0 - ADAM_B2) * (g * g)
        g_ref[...] = g
        nm_ref[...] = mn
        nv_ref[...] = vn
        d_ref[...] = -ADAM_LR * ((mn / c1) / (jnp.sqrt(vn / c2) + ADAM_EPS) + ADAM_WD * wv)

    blk = pl.BlockSpec((tr, C), lambda i: (i, 0))
    out = jax.ShapeDtypeStruct((R, C), F32)
    return _pallas_call(
        body, name=name, grid=(R // tr,),
        in_specs=[blk, blk, blk, pl.BlockSpec((P, tr, C), lambda i: (0, i, 0))],
        out_specs=[blk, blk, blk, blk],
        out_shape=[out, out, out, out],
        compiler_params=_params(32, 1),
    )(w, m, v, parts)


def _position():
    return lax.axis_index("x"), lax.axis_index("y"), lax.axis_index("c")


def _flat(px, py, pc):
    return 4 * px + 2 * py + pc


def _row_halves(rows, dtype):
    tile = _SUBLANES * (4 // jnp.dtype(dtype).itemsize)
    half = rows // 2 // tile * tile
    assert half > 0
    return (0, half), (half, rows - half)


def _gather_body(ins, outs, send_sems, recv_sems, local_sems, handshake):
    n = len(ins)
    x, y, c = _position()
    me, sibling = (x, y, c), (x, y, 1 - c)
    across_x, across_y, diagonal = (1 - x, y), (x, 1 - y), (1 - x, 1 - y)
    if handshake:
        _handshake([sibling] + [(*chip, cc) for chip in (across_x, across_y, diagonal) for cc in (c, 1 - c)])

    def copy(a, k, block, to, rows=None, src=None):
        dst = outs[a].at[_flat(*block)]
        if rows is not None:
            dst = dst.at[pl.ds(*rows)]
        return pltpu.make_async_remote_copy(
            src_ref=dst if src is None else src, dst_ref=dst,
            send_sem=send_sems.at[a, k], recv_sem=recv_sems.at[a, k],
            device_id=to, device_id_type=MESH)

    halves = [_row_halves(ins[a].shape[0], ins[a].dtype) for a in range(n)]
    mine = [pltpu.make_async_copy(ins[a], outs[a].at[_flat(*me)], local_sems.at[a]) for a in range(n)]
    for cp in mine:
        cp.start()
    sent = []

    def start(cp):
        cp.start()
        sent.append(cp)

    for a in range(n):
        start(copy(a, 0, me, sibling, src=ins[a]))
        start(copy(a, 1, me, (*across_x, c), src=ins[a]))
        start(copy(a, 2, me, (*across_y, c), src=ins[a]))
    for a in range(n):
        low, high = halves[a]
        copy(a, 1, (*across_x, c), me).wait_recv()
        start(copy(a, 3, (*across_x, c), (*across_y, c), rows=low))
        start(copy(a, 5, (*across_x, c), sibling))
        copy(a, 2, (*across_y, c), me).wait_recv()
        start(copy(a, 4, (*across_y, c), (*across_x, c), rows=high))
        start(copy(a, 6, (*across_y, c), sibling))
    for a in range(n):
        low, high = halves[a]
        copy(a, 3, (*diagonal, c), me, rows=low).wait_recv()
        start(copy(a, 7, (*diagonal, c), sibling, rows=low))
        copy(a, 4, (*diagonal, c), me, rows=high).wait_recv()
        start(copy(a, 8, (*diagonal, c), sibling, rows=high))
    for a in range(n):
        low, high = halves[a]
        copy(a, 0, sibling, me).wait_recv()
        copy(a, 5, (*across_x, 1 - c), me).wait_recv()
        copy(a, 6, (*across_y, 1 - c), me).wait_recv()
        copy(a, 7, (*diagonal, 1 - c), me, rows=low).wait_recv()
        copy(a, 8, (*diagonal, 1 - c), me, rows=high).wait_recv()
    for cp in sent:
        cp.wait_send()
    for cp in mine:
        cp.wait()


_GATHER_SLOTS = 9


def _gather_scratch(n):
    return [pltpu.SemaphoreType.DMA((n, _GATHER_SLOTS)), pltpu.SemaphoreType.DMA((n, _GATHER_SLOTS)),
            pltpu.SemaphoreType.DMA((n,))]


def _all_gather(shards, name):
    n = len(shards)

    def body(*refs):
        _gather_body(refs[:n], refs[n:2 * n], *refs[2 * n:], handshake=False)

    hbm = pl.BlockSpec(memory_space=pltpu.HBM)
    return _pallas_call(
        body, name=name,
        in_specs=[hbm] * n, out_specs=[hbm] * n,
        out_shape=[jax.ShapeDtypeStruct((N_DEV,) + s.shape, s.dtype) for s in shards],
        scratch_shapes=_gather_scratch(n),
    )(*shards)


def _handshake(peers):
    barrier = pltpu.get_barrier_semaphore()
    for peer in peers:
        pl.semaphore_signal(barrier, inc=1, device_id=peer, device_id_type=MESH)
    pl.semaphore_wait(barrier, len(peers))


def _sequencer_call(body, name, collective_id, out_type, scratch_types, operands):
    return pl.kernel(
        body, name=name, out_type=out_type,
        mesh=plsc.ScalarSubcoreMesh(axis_name="sequencer", num_cores=1),
        scratch_types=scratch_types,
        compiler_params=pltpu.CompilerParams(collective_id=collective_id),
    )(*operands)


def _seq_all_gather(shards, name, collective_id, after):
    n = len(shards)

    def body(*refs):
        _gather_body(refs[:n], refs[n + 1:2 * n + 1], *refs[2 * n + 1:], handshake=True)

    return _sequencer_call(
        body, name, collective_id,
        [jax.ShapeDtypeStruct((N_DEV,) + s.shape, s.dtype) for s in shards],
        _gather_scratch(n), list(shards) + [after])


def _seq_to_sibling(parts, name, collective_id, after):
    n = len(parts)

    def body(*refs):
        ins, outs = refs[:n], refs[n + len(after):2 * n + len(after)]
        send_sems, recv_sems = refs[2 * n + len(after):]
        x, y, c = _position()
        sibling = (x, y, 1 - c)
        _handshake([sibling])
        sent = []
        for a in range(n):
            for q in range(N_CHIPS):
                cp = pltpu.make_async_remote_copy(
                    src_ref=ins[a].at[2 * q + (1 - c)], dst_ref=outs[a].at[q],
                    send_sem=send_sems.at[a, q], recv_sem=recv_sems.at[a, q],
                    device_id=sibling, device_id_type=MESH)
                cp.start()
                sent.append(cp)
        for cp in sent:
            cp.wait_recv()
        for cp in sent:
            cp.wait_send()

    return _sequencer_call(
        body, name, collective_id,
        [jax.ShapeDtypeStruct((N_CHIPS,) + p.shape[1:], p.dtype) for p in parts],
        [pltpu.SemaphoreType.DMA((n, N_CHIPS)), pltpu.SemaphoreType.DMA((n, N_CHIPS))],
        list(parts) + list(after))


def _seq_to_chips(partials, name, collective_id):
    n = len(partials)

    def body(*refs):
        ins, outs = refs[:n], refs[n:2 * n]
        send_sems, recv_sems, local_sems = refs[2 * n:]
        x, y, c = _position()
        my_chip = 2 * x + y
        chips = [(1 - x, y), (x, 1 - y), (1 - x, 1 - y)]
        _handshake([(*chip, c) for chip in chips])
        mine = [pltpu.make_async_copy(ins[a].at[my_chip], outs[a].at[my_chip], local_sems.at[a]) for a in range(n)]
        for cp in mine:
            cp.start()
        sent = []
        for a in range(n):
            for j, (px, py) in enumerate(chips):
                cp = pltpu.make_async_remote_copy(
                    src_ref=ins[a].at[2 * px + py], dst_ref=outs[a].at[my_chip],
                    send_sem=send_sems.at[a, j], recv_sem=recv_sems.at[a, j],
                    device_id=(px, py, c), device_id_type=MESH)
                cp.start()
                sent.append(cp)
        for cp in sent:
            cp.wait_recv()
        for cp in sent:
            cp.wait_send()
        for cp in mine:
            cp.wait()

    return _sequencer_call(
        body, name, collective_id,
        [jax.ShapeDtypeStruct(p.shape, p.dtype) for p in partials],
        [pltpu.SemaphoreType.DMA((n, 3)), pltpu.SemaphoreType.DMA((n, 3)), pltpu.SemaphoreType.DMA((n,))],
        list(partials))


def _pair_add(parts, recvs, name, after=()):
    n = len(parts)
    core = lax.axis_index("c").astype(jnp.int32).reshape(1)

    def body(c_ref, *refs):
        ps, rs, outs = refs[:n], refs[n:2 * n], refs[2 * n + len(after):]
        for p_ref, r_ref, o_ref in zip(ps, rs, outs):
            o_ref[...] = (p_ref[...].astype(F32) + r_ref[...].astype(F32)).astype(o_ref.dtype)

    mine = lambda p: pl.BlockSpec((None,) + p.shape[1:], lambda q, c: (2 * q + c[0], 0, 0))
    blk = lambda p: pl.BlockSpec((None,) + p.shape[1:], lambda q, c: (q, 0, 0))
    return pl.pallas_call(
        body, name=name,
        grid_spec=pltpu.PrefetchScalarGridSpec(
            num_scalar_prefetch=1, grid=(N_CHIPS,),
            in_specs=[mine(p) for p in parts] + [blk(p) for p in parts] + [_UNREAD] * len(after),
            out_specs=[blk(p) for p in parts]),
        out_shape=[pltpu.HBM((N_CHIPS,) + p.shape[1:], p.dtype) for p in parts],
        compiler_params=_params(40, 1),
    )(core, *[pltpu.with_memory_space_constraint(a, pltpu.HBM) for a in (*parts, *recvs, *after)])


class _Reduced(NamedTuple):
    partials: list
    reduced: list


def _blocks(g):
    return g.reshape(N_DEV, -1, g.shape[-1])


def _reduce_scatter(parts, tag, ids, after=(), add_after=()):
    from_sibling = _seq_to_sibling(parts, "rs_sibling_" + tag, ids[0], after)
    partials = _pair_add(parts, from_sibling, "rs_add_" + tag, add_after)
    return _Reduced(partials, _seq_to_chips(partials, "rs_chips_" + tag, ids[1]))


_SMALL = ("ffn1_norm", "mix_norm", "conv_b", "conv_ln_g", "conv_ln_b", "forget_b", "out_norm_conv",
          "out_norm_attn", "ffn2_norm", "final_norm")
_PACK_WIDTH = 2 * D_CONV
_SLOT = dict(ffn1_norm=(0, 0), mix_norm=(1, 0), ffn2_norm=(2, 0), final_norm=(3, 0), conv_b=(4, 0),
             conv_ln_g=(4, D_CONV), conv_ln_b=(5, 0), out_norm_conv=(5, D_CONV), out_norm_attn=(6, 0),
             forget_b=(6, D_CONV))
_LOSS_ROW = 7
_CONV_ROW0 = 8
_PACK_ROWS = _CONV_ROW0 + CONV_HALO


def _pack_small(small, name):
    arrays = [small[n] for n in _SMALL] + [small["conv_w"], small["loss"]]

    def body(*refs):
        out = refs[-1]
        out[...] = jnp.zeros_like(out)
        for n, ref in zip(_SMALL, refs):
            row, lane = _SLOT[n]
            out[row:row + 1, lane:lane + ref.shape[1]] = ref[...]
        out[_CONV_ROW0:, :D_CONV] = refs[len(_SMALL)][...]
        out[_LOSS_ROW:_LOSS_ROW + 1, :LANES] = refs[len(_SMALL) + 1][...]

    return _pallas_call(body, name=name, out_shape=jax.ShapeDtypeStruct((_PACK_ROWS, _PACK_WIDTH), F32))(*arrays)


def _adamw_small(gathered, w, m, v, name):
    c1 = 1.0 - ADAM_B1 ** ADAM_STEP
    c2 = 1.0 - ADAM_B2 ** ADAM_STEP
    k = len(_SMALL)

    def body(g_ref, *refs):
        ws, ms, vs = refs[:k], refs[k:2 * k], refs[2 * k:3 * k]
        outs = refs[3 * k:]
        total = g_ref[0]
        for s in range(1, N_DEV):
            total = total + g_ref[s]
        for i, n in enumerate(_SMALL):
            row, lane = _SLOT[n]
            width = ws[i].shape[1]
            g = total[row:row + 1, lane:lane + width]
            mn = ADAM_B1 * ms[i][...] + (1.0 - ADAM_B1) * g
            vn = ADAM_B2 * vs[i][...] + (1.0 - ADAM_B2) * (g * g)
            o_g, o_d, o_m, o_v = outs[4 * i:4 * i + 4]
            o_g[...] = g
            o_m[...] = mn
            o_v[...] = vn
            o_d[...] = -ADAM_LR * ((mn / c1) / (jnp.sqrt(vn / c2) + ADAM_EPS) + ADAM_WD * ws[i][...])
        outs[4 * k][...] = total[_CONV_ROW0:, :D_CONV]
        outs[4 * k + 1][...] = total[_LOSS_ROW:_LOSS_ROW + 1, :LANES]

    shapes = []
    for n in _SMALL:
        shapes += [jax.ShapeDtypeStruct(w[n].shape, F32)] * 4
    shapes.append(jax.ShapeDtypeStruct((CONV_HALO, D_CONV), F32))
    shapes.append(jax.ShapeDtypeStruct((1, LANES), F32))
    res = _pallas_call(body, name=name, out_shape=shapes)(
        gathered, *[w[n] for n in _SMALL], *[m[n] for n in _SMALL], *[v[n] for n in _SMALL])
    return {n: res[4 * i:4 * i + 4] for i, n in enumerate(_SMALL)}, res[4 * k], res[4 * k + 1]


def _local_step(x, target, norms, shard):
    D = x.shape[1]
    J = N_DEV // 2
    as13 = lambda g: g.reshape(2, J, g.shape[1], D)

    (g13_1,) = _all_gather([shard["ffn1_w13"]], "gather_ffn1_w13")
    (g2_1,) = _seq_all_gather([shard["ffn1_w2"]], "gather_ffn1_w2", 10, after=g13_1)
    w13_1 = as13(g13_1)
    G1, U1, A1 = _ffn_up(x, norms["ffn1_norm"], w13_1, "ffn1_up")
    gin, gconv = _seq_all_gather([shard["w_in"], shard["conv_w"]], "gather_mix", 1, after=G1)
    w2_1 = g2_1.reshape(-1, D)
    x1 = _ffn_down(x, A1, w2_1, "ffn1_down")
    gout, g13_2, g2_2 = _seq_all_gather([shard["w_out"], shard["ffn2_w13"], shard["ffn2_w2"]], "gather_ffn2", 2,
                                        after=x1)
    wout = gout.reshape(-1, D)
    conv_w32 = jnp.pad(gconv.transpose(1, 0, 2).reshape(CONV_TAPS, D_CONV), ((0, CONV_HALO - CONV_TAPS), (0, 0)))

    ag, k, v, qT, kT, vT, fl = _inproj_fwd(x1, norms["mix_norm"], gin, "inproj_fwd")
    cum, cumT = _forget_fwd(fl, norms["forget_b"], "forget_fwd")
    yc, c = _conv_fwd(ag, conv_w32, norms["conv_b"], norms["conv_ln_g"], norms["conv_ln_b"], "conv_fwd")
    o, lseT = _attn_fwd(qT, k, vT, cum, cumT, "attn_fwd")
    x2 = _outproj_fwd(x1, c, o, norms["out_norm_conv"], norms["out_norm_attn"], wout, "outproj_fwd")
    w13_2, w2_2 = as13(g13_2), g2_2.reshape(-1, D)
    G2, U2, A2 = _ffn_up(x2, norms["ffn2_norm"], w13_2, "ffn2_up")
    loss, dx3, d_final = _ffn_down_loss(x2, A2, w2_2, norms["final_norm"], target, "ffn2_down_loss")

    dw2_2 = _ffn_w2_grad(dx3, A2, "ffn2_w2_grad")
    dx2, d_ffn2n, h3, dG2, dU2 = _ffn_bwd_act(x2, norms["ffn2_norm"], dx3, G2, U2, w13_2, w2_2, "ffn2_bwd_act")
    dw13_2 = _ffn_w13_grad(h3, dG2, dU2, "ffn2_w13_grad")
    dc, dobT, deltaT, dwout, d_onc, d_ona = _outproj_bwd(
        dx2, c, o, norms["out_norm_conv"], norms["out_norm_attn"], wout, "outproj_bwd")
    red_ffn2 = _reduce_scatter([_blocks(dw13_2), _blocks(dw2_2)], "ffn2", (3, 4), add_after=(dc,))
    dqT, dkT, dvT, dcum = _attn_bwd(qT, k, kT, v, dobT, lseT, deltaT, cum, cumT, "attn_bwd",
                                    after=red_ffn2.partials)
    dfl, d_fb = _forget_bwd(dcum, fl, norms["forget_b"], "forget_bwd")
    dag, d_convw, d_cb, d_lg, d_lb = _conv_bwd(dc, yc, ag, conv_w32, norms["conv_ln_g"], norms["conv_ln_b"], "conv_bwd")
    dx1, d_mixn, h2 = _inproj_bwd_act(x1, norms["mix_norm"], dx2, dag, dqT, dkT, dvT, dfl, gin, "inproj_bwd_act")
    dw2_1 = _ffn_w2_grad(dx1, A1, "ffn1_w2_grad")
    early = [_blocks(dwout), _blocks(dw2_1)]
    sib_early = _seq_to_sibling(early, "rs_sibling_mix_early", 11, red_ffn2.reduced[:1])
    dwin_blocks = _inproj_bwd_weights(h2, dag, dqT, dkT, dvT, dfl, "inproj_bwd_weights")
    sib_w_in = _seq_to_sibling([dwin_blocks], "rs_sibling_mix", 5, sib_early[:1])
    mix_partials = _pair_add([dwin_blocks] + early, sib_w_in + sib_early, "rs_add_mix")
    red_mix = _Reduced(mix_partials, _seq_to_chips(mix_partials, "rs_chips_mix", 6))
    dx, d_ffn1n, h1, dG1, dU1 = _ffn_bwd_act(x, norms["ffn1_norm"], dx1, G1, U1, w13_1, w2_1, "ffn1_bwd_act",
                                             after=red_mix.partials)
    small = dict(ffn1_norm=d_ffn1n, mix_norm=d_mixn, conv_b=d_cb, conv_ln_g=d_lg, conv_ln_b=d_lb,
                 forget_b=d_fb, out_norm_conv=d_onc, out_norm_attn=d_ona, ffn2_norm=d_ffn2n,
                 final_norm=d_final, conv_w=d_convw, loss=loss)
    packed_small = _pack_small(small, "pack_small_grads")
    (gathered_small,) = _seq_all_gather([packed_small], "gather_small_grads", 9, after=red_mix.partials[0])
    dw13_1 = _ffn_w13_grad(h1, dG1, dU1, "ffn1_w13_grad")
    red_w13_1 = _reduce_scatter([_blocks(dw13_1)], "ffn1_w13", (7, 8), after=[red_mix.reduced[0], gathered_small])
    big = dict(ffn1_w13=red_w13_1.reduced[0], ffn1_w2=red_mix.reduced[2], w_in=red_mix.reduced[0],
               w_out=red_mix.reduced[1], ffn2_w13=red_ffn2.reduced[0], ffn2_w2=red_ffn2.reduced[1])
    return dx, gathered_small, big


_BIG = ("ffn1_w13", "ffn1_w2", "w_in", "w_out", "ffn2_w13", "ffn2_w2")
_TRANSPOSED = ("ffn1_w13", "ffn2_w13", "w_in")
_ORDER = ("ffn1_norm", "ffn1_w13", "ffn1_w2", "mix_norm", "w_in", "conv_w", "conv_b", "conv_ln_g", "conv_ln_b",
          "forget_b", "out_norm_conv", "out_norm_attn", "w_out", "ffn2_norm", "ffn2_w13", "ffn2_w2", "final_norm")


def kernel(x, ffn1_norm, ffn1_w13, ffn1_w2, mix_norm, w_in, conv_w, conv_b, conv_ln_g, conv_ln_b, forget_b, out_norm_conv, out_norm_attn, w_out, ffn2_norm, ffn2_w13, ffn2_w2, final_norm, loss_target, m_ffn1_norm, m_ffn1_w13, m_ffn1_w2, m_mix_norm, m_w_in, m_conv_w, m_conv_b, m_conv_ln_g, m_conv_ln_b, m_forget_b, m_out_norm_conv, m_out_norm_attn, m_w_out, m_ffn2_norm, m_ffn2_w13, m_ffn2_w2, m_final_norm, v_ffn1_norm, v_ffn1_w13, v_ffn1_w2, v_mix_norm, v_w_in, v_conv_w, v_conv_b, v_conv_ln_g, v_conv_ln_b, v_forget_b, v_out_norm_conv, v_out_norm_attn, v_w_out, v_ffn2_norm, v_ffn2_w13, v_ffn2_w2, v_final_norm):
    w = dict(ffn1_norm=ffn1_norm, ffn1_w13=ffn1_w13, ffn1_w2=ffn1_w2, mix_norm=mix_norm, w_in=w_in, conv_w=conv_w,
             conv_b=conv_b, conv_ln_g=conv_ln_g, conv_ln_b=conv_ln_b, forget_b=forget_b, out_norm_conv=out_norm_conv,
             out_norm_attn=out_norm_attn, w_out=w_out, ffn2_norm=ffn2_norm, ffn2_w13=ffn2_w13, ffn2_w2=ffn2_w2,
             final_norm=final_norm)
    m = dict(ffn1_norm=m_ffn1_norm, ffn1_w13=m_ffn1_w13, ffn1_w2=m_ffn1_w2, mix_norm=m_mix_norm, w_in=m_w_in,
             conv_w=m_conv_w, conv_b=m_conv_b, conv_ln_g=m_conv_ln_g, conv_ln_b=m_conv_ln_b, forget_b=m_forget_b,
             out_norm_conv=m_out_norm_conv, out_norm_attn=m_out_norm_attn, w_out=m_w_out, ffn2_norm=m_ffn2_norm,
             ffn2_w13=m_ffn2_w13, ffn2_w2=m_ffn2_w2, final_norm=m_final_norm)
    v = dict(ffn1_norm=v_ffn1_norm, ffn1_w13=v_ffn1_w13, ffn1_w2=v_ffn1_w2, mix_norm=v_mix_norm, w_in=v_w_in,
             conv_w=v_conv_w, conv_b=v_conv_b, conv_ln_g=v_conv_ln_g, conv_ln_b=v_conv_ln_b, forget_b=v_forget_b,
             out_norm_conv=v_out_norm_conv, out_norm_attn=v_out_norm_attn, w_out=v_w_out, ffn2_norm=v_ffn2_norm,
             ffn2_w13=v_ffn2_w13, ffn2_w2=v_ffn2_w2, final_norm=v_final_norm)
    shapes = {n: a.shape for n, a in w.items()}
    T, D = x.shape[1], x.shape[2]
    def two(n, a):
        if a.ndim != 3:
            return a.reshape(1, -1)
        a = a.reshape(a.shape[-2], a.shape[-1])
        return a.T if n in _TRANSPOSED else a

    w2d = {n: two(n, a) for n, a in w.items()}
    m2d = {n: two(n, a) for n, a in m.items()}
    v2d = {n: two(n, a) for n, a in v.items()}

    shard = {n: w2d[n].astype(MXU) for n in _BIG}
    shard["conv_w"] = w2d["conv_w"]
    norms = {n: w2d[n] for n in _SMALL}
    norms["forget_b"] = jnp.pad(w2d["forget_b"], ((0, 0), (0, LANES - N_HEADS)))
    dx, gathered_small, big = _local_step(x[0], loss_target[0], norms, shard)

    grads, deltas, new_m, new_v = {}, {}, {}, {}
    for n in _BIG:
        g, d, nm, nv = _adamw(w2d[n], m2d[n], v2d[n], big[n], "adamw_" + n)
        grads[n], deltas[n], new_m[n], new_v[n] = g, d, nm, nv

    small_out, conv_g_full, loss = _adamw_small(gathered_small, w2d, m2d, v2d, "adamw_small")
    for n in _SMALL:
        grads[n], deltas[n], new_m[n], new_v[n] = small_out[n]
    conv_g_full = conv_g_full[:CONV_TAPS]
    xi, yi, ci = _position()
    cw = shapes["conv_w"][-1]
    conv_g_mine = lax.dynamic_slice_in_dim(conv_g_full, _flat(xi, yi, ci) * cw, cw, axis=1)
    g, d, nm, nv = _adamw(w2d["conv_w"], m2d["conv_w"], v2d["conv_w"], conv_g_mine[None], "adamw_conv_w")
    grads["conv_w"], deltas["conv_w"], new_m["conv_w"], new_v["conv_w"] = g, d, nm, nv

    shaped = lambda dct: [(dct[n].T if n in _TRANSPOSED else dct[n]).reshape(shapes[n]) for n in _ORDER]
    return (loss[0, 0], dx[None], *shaped(grads), *shaped(deltas), *shaped(new_m), *shaped(new_v))
```

```python
from typing import NamedTuple

import jax
import jax.numpy as jnp
from jax import lax
from jax.experimental import pallas as pl
from jax.experimental.pallas import tpu as pltpu
from jax.experimental.pallas import tpu_sc as plsc

F32 = jnp.float32
MXU = jnp.bfloat16
EPS = 1e-6
N_HEADS = 8
HEAD_DIM = 64
D_CONV = 512
D_ATTN = N_HEADS * HEAD_DIM
CONV_TAPS = 31
CONV_HALO = 32
SCALE = HEAD_DIM ** -0.5
NEG = -1e30
LANES = 128
N_DEV = 8
N_CHIPS = N_DEV // 2
MESH = pl.DeviceIdType.MESH
MIB = 1 << 20

ADAM_LR = 0.001
ADAM_B1 = 0.9
ADAM_B2 = 0.999
ADAM_EPS = 1e-08
ADAM_WD = 0.01
ADAM_STEP = 10


_UNREAD = pl.BlockSpec(memory_space=pl.ANY)


def _pallas_call(body, *, out_shape, **kwargs):
    in_hbm = lambda s: pltpu.HBM(s.shape, s.dtype)
    outs = [in_hbm(s) for s in out_shape] if isinstance(out_shape, (list, tuple)) else in_hbm(out_shape)
    call = pl.pallas_call(body, out_shape=outs, **kwargs)
    return lambda *operands: call(*[pltpu.with_memory_space_constraint(a, pltpu.HBM) for a in operands])


def _params(vmem_mib, n_axes):
    return pltpu.CompilerParams(dimension_semantics=("arbitrary",) * n_axes, vmem_limit_bytes=vmem_mib * MIB)


def _mm(a, b):
    return jnp.dot(a, b, preferred_element_type=F32)


def _mm_nt(a, b):
    return lax.dot_general(a, b, (((1,), (1,)), ((), ())), preferred_element_type=F32)


def _mm_tn(a, b):
    return lax.dot_general(a, b, (((0,), (0,)), ((), ())), preferred_element_type=F32)


def _rms_fwd(x, g):
    r = lax.rsqrt(jnp.mean(x * x, axis=-1, keepdims=True) + EPS)
    return x * r * g, r


def _rms_bwd(x, r, g, dy):
    gdy = dy * g
    dx = r * gdy - x * (r * r * r) * jnp.mean(x * gdy, axis=-1, keepdims=True)
    dg = jnp.sum(dy * x * r, axis=0, keepdims=True)
    return dx, dg


def _silu_grad(z, sz):
    return sz * (1.0 + z * (1.0 - sz))


def _three_terms(x):
    x1 = x.astype(jnp.bfloat16)
    r1 = x - x1.astype(F32)
    x2 = r1.astype(jnp.bfloat16)
    x3 = (r1 - x2.astype(F32)).astype(jnp.bfloat16)
    return x1, x2, x3


def _exact_tri_dot(tri, x):
    x1, x2, x3 = _three_terms(x)
    return _mm(tri, x1) + _mm(tri, x2) + _mm(tri, x3)


def _exact_dot_01(x, sel):
    x1, x2, x3 = _three_terms(x)
    return _mm(x1, sel) + _mm(x2, sel) + _mm(x3, sel)


def _tile(n, want):
    t = min(n, want)
    assert n % t == 0
    return t


_FFN_CHUNK = 256


def _ffn_up(x, g, w13, name):
    T, D = x.shape
    _, J, bf, _ = w13.shape
    tm = _tile(T, 1024)
    I = T // tm

    def body(x_ref, g_ref, w13_ref, G_ref, U_ref, A_ref, h_s):
        j = pl.program_id(0)
        i = pl.program_id(1)
        rows = pl.ds(pl.multiple_of(i * tm, tm), tm)

        @pl.when(j == 0)
        def _():
            h, _ = _rms_fwd(x_ref[...], g_ref[...])
            h_s[rows, :] = h.astype(MXU)

        chunks = [slice(r0, r0 + _FFN_CHUNK) for r0 in range(0, tm, _FFN_CHUNK)]
        hbs = [h_s[pl.ds(pl.multiple_of(i * tm + rs.start, _FFN_CHUNK), _FFN_CHUNK), :] for rs in chunks]
        GU = [(_mm_nt(hb, w13_ref[0]), _mm_nt(hb, w13_ref[1])) for hb in hbs]
        for rs, (G, U) in zip(chunks, GU):
            G_ref[rs, :] = G.astype(MXU)
            U_ref[rs, :] = U.astype(MXU)
            A_ref[rs, :] = (G * jax.nn.sigmoid(G) * U).astype(MXU)

    blk = pl.BlockSpec((None, tm, bf), lambda j, i: (j, i, 0))
    hid = jax.ShapeDtypeStruct((J, T, bf), MXU)
    return _pallas_call(
        body, name=name, grid=(J, I),
        in_specs=[pl.BlockSpec((tm, D), lambda j, i: (jnp.where(j == 0, i, I - 1), 0)),
                  pl.BlockSpec((1, D), lambda j, i: (0, 0)),
                  pl.BlockSpec((2, None, bf, D), lambda j, i: (0, j, 0, 0))],
        out_specs=[blk, blk, blk],
        out_shape=[hid, hid, hid],
        scratch_shapes=[pltpu.VMEM((T, D), MXU)],
        compiler_params=_params(48, 2),
    )(x, g, w13)


def _ffn_down(x, A, w2, name):
    T, D = x.shape
    J, _, bf = A.shape
    tm = _tile(T, 512)

    def body(x_ref, A_ref, w2_ref, xo_ref):
        f = _mm(A_ref[0], w2_ref[0:bf, :])
        for j in range(1, J):
            f = f + _mm(A_ref[j], w2_ref[j * bf:(j + 1) * bf, :])
        xo_ref[...] = x_ref[...] + 0.5 * f

    row = pl.BlockSpec((tm, D), lambda i: (i, 0))
    return _pallas_call(
        body, name=name, grid=(T // tm,),
        in_specs=[row, pl.BlockSpec((J, tm, bf), lambda i: (0, i, 0)), pl.BlockSpec((J * bf, D), lambda i: (0, 0))],
        out_specs=row,
        out_shape=jax.ShapeDtypeStruct((T, D), F32),
        compiler_params=_params(48, 1),
    )(x, A, w2)


def _ffn_bwd_act(x, g, dy, Gs, Us, w13, w2, name, after=()):
    T, D = x.shape
    _, J, bf, _ = w13.shape
    tm = _tile(T, 512)
    I = T // tm

    def body(x_ref, g_ref, dy_ref, G_ref, U_ref, w13_ref, w2_ref, *rest):
        dx_ref, dg_ref, h_ref, dG_ref, dU_ref, dh_s, dF_s, h_s = rest[len(after):]
        j = pl.program_id(0)
        i = pl.program_id(1)
        rows = pl.ds(pl.multiple_of(i * tm, tm), tm)

        @pl.when(j == 0)
        def _():
            h, _ = _rms_fwd(x_ref[...], g_ref[...])
            hb = h.astype(MXU)
            h_s[rows, :] = hb
            h_ref[...] = hb
            dF_s[rows, :] = (0.5 * dy_ref[...]).astype(MXU)
            dh_s[rows, :] = jnp.zeros((tm, D), F32)

        chunks = [slice(r0, r0 + _FFN_CHUNK) for r0 in range(0, tm, _FFN_CHUNK)]
        crows = [pl.ds(pl.multiple_of(i * tm + rs.start, _FFN_CHUNK), _FFN_CHUNK) for rs in chunks]
        dAs = [_mm_nt(dF_s[cr, :], w2_ref[...]) for cr in crows]
        for rs, cr, dA in zip(chunks, crows, dAs):
            G = G_ref[rs, :].astype(F32)
            U = U_ref[rs, :].astype(F32)
            sg = jax.nn.sigmoid(G)
            s = G * sg
            dUb = (dA * s).astype(MXU)
            dGb = (dA * U * _silu_grad(G, sg)).astype(MXU)
            dG_ref[rs, :] = dGb
            dU_ref[rs, :] = dUb
            dh_s[cr, :] += _mm(dGb, w13_ref[0]) + _mm(dUb, w13_ref[1])

        @pl.when(j == J - 1)
        def _():
            xv = x_ref[...]
            gv = g_ref[...]
            _, r = _rms_fwd(xv, gv)
            dxn, dgp = _rms_bwd(xv, r, gv, dh_s[rows, :])
            dx_ref[...] = dy_ref[...] + dxn

            @pl.when(i == 0)
            def _():
                dg_ref[...] = dgp

            @pl.when(i > 0)
            def _():
                dg_ref[...] += dgp

    ends = lambda j, i: (jnp.where((j == 0) | (j == J - 1), i, I - 1), 0)
    blk = pl.BlockSpec((None, tm, bf), lambda j, i: (j, i, 0))
    hid = jax.ShapeDtypeStruct((J, T, bf), MXU)
    return _pallas_call(
        body, name=name, grid=(J, I),
        in_specs=[pl.BlockSpec((tm, D), ends), pl.BlockSpec((1, D), lambda j, i: (0, 0)), pl.BlockSpec((tm, D), ends),
                  blk, blk, pl.BlockSpec((2, None, bf, D), lambda j, i: (0, j, 0, 0)),
                  pl.BlockSpec((bf, D), lambda j, i: (j, 0))] + [_UNREAD] * len(after),
        out_specs=[pl.BlockSpec((tm, D), lambda j, i: (jnp.where(j == J - 1, i, 0), 0)),
                   pl.BlockSpec((1, D), lambda j, i: (0, 0)),
                   pl.BlockSpec((tm, D), lambda j, i: (jnp.where(j == 0, i, I - 1), 0)), blk, blk],
        out_shape=[jax.ShapeDtypeStruct((T, D), F32), jax.ShapeDtypeStruct((1, D), F32),
                   jax.ShapeDtypeStruct((T, D), MXU), hid, hid],
        scratch_shapes=[pltpu.VMEM((T, D), F32), pltpu.VMEM((T, D), MXU), pltpu.VMEM((T, D), MXU)],
        compiler_params=_params(58, 2),
    )(x, g, dy, Gs, Us, w13, w2, *after)


def _ffn_w13_grad(h, dG, dU, name):
    T, D = h.shape
    J, _, bf = dG.shape

    def body(h_ref, dG_ref, dU_ref, dw13_ref):
        dw13_ref[0] = _mm_tn(dG_ref[...], h_ref[...]).astype(dw13_ref.dtype)
        dw13_ref[1] = _mm_tn(dU_ref[...], h_ref[...]).astype(dw13_ref.dtype)

    blk = pl.BlockSpec((None, T, bf), lambda j: (j, 0, 0))
    return _pallas_call(
        body, name=name, grid=(J,),
        in_specs=[pl.BlockSpec((T, D), lambda j: (0, 0)), blk, blk],
        out_specs=pl.BlockSpec((2, None, bf, D), lambda j: (0, j, 0, 0)),
        out_shape=jax.ShapeDtypeStruct((2, J, bf, D), MXU),
        compiler_params=_params(48, 1),
    )(h, dG, dU)


def _ffn_w2_grad(dy, A, name, after=()):
    T, D = dy.shape
    J, _, bf = A.shape

    def body(dy_ref, A_ref, *rest):
        dw2_ref, dF_s = rest[len(after):]

        @pl.when(pl.program_id(0) == 0)
        def _():
            dF_s[...] = (0.5 * dy_ref[...]).astype(MXU)

        dw2_ref[...] = _mm_tn(A_ref[...], dF_s[...]).astype(dw2_ref.dtype)

    return _pallas_call(
        body, name=name, grid=(J,),
        in_specs=[pl.BlockSpec((T, D), lambda j: (0, 0)), pl.BlockSpec((None, T, bf), lambda j: (j, 0, 0))]
        + [_UNREAD] * len(after),
        out_specs=pl.BlockSpec((bf, D), lambda j: (j, 0)),
        out_shape=jax.ShapeDtypeStruct((J * bf, D), MXU),
        scratch_shapes=[pltpu.VMEM((T, D), MXU)],
        compiler_params=_params(48, 1),
    )(dy, A, *after)


_AG0, _Q0, _K0, _V0, _F0 = 0, 2 * D_CONV, 2 * D_CONV + D_ATTN, 2 * D_CONV + 2 * D_ATTN, 2 * D_CONV + 3 * D_ATTN
N_IN = _F0 + N_HEADS
N_IN_PAD = _F0 + LANES
_IN_BLOCK = N_IN // N_DEV


def _rows_from_blocks(blocks_ref, rows_ref):
    for p in range(N_DEV):
        rows_ref[_IN_BLOCK * p:_IN_BLOCK * (p + 1), :] = blocks_ref[p]
    rows_ref[N_IN:, :] = jnp.zeros((N_IN_PAD - N_IN, rows_ref.shape[1]), rows_ref.dtype)


def _inproj_fwd(x1, gm, win_blocks, name):
    T, D = x1.shape
    tm = _tile(T, 512)

    def body(x_ref, g_ref, wb_ref, ag_ref, k_ref, v_ref, qT_ref, kT_ref, vT_ref, fl_ref, w_ref):
        @pl.when(pl.program_id(0) == 0)
        def _():
            _rows_from_blocks(wb_ref, w_ref)

        h, _ = _rms_fwd(x_ref[...], g_ref[...])
        hb = h.astype(MXU)
        ag_ref[...] = _mm_nt(hb, w_ref[_AG0:_Q0, :])
        qT_ref[...] = (_mm_nt(hb, w_ref[_Q0:_K0, :]) * SCALE).T.astype(MXU)
        for c0, ref, refT in ((_K0, k_ref, kT_ref), (_V0, v_ref, vT_ref)):
            y = _mm_nt(hb, w_ref[c0:c0 + D_ATTN, :])
            ref[...] = y.astype(MXU)
            refT[...] = y.T.astype(MXU)
        fl_ref[...] = _mm_nt(hb, w_ref[_F0:N_IN_PAD, :])

    row = lambda w: pl.BlockSpec((tm, w), lambda i: (i, 0))
    col = pl.BlockSpec((D_ATTN, tm), lambda i: (0, i))
    std = jax.ShapeDtypeStruct((T, D_ATTN), MXU)
    trn = jax.ShapeDtypeStruct((D_ATTN, T), MXU)
    return _pallas_call(
        body, name=name, grid=(T // tm,),
        in_specs=[row(D), pl.BlockSpec((1, D), lambda i: (0, 0)),
                  pl.BlockSpec((N_DEV, _IN_BLOCK, D), lambda i: (0, 0, 0))],
        out_specs=[row(2 * D_CONV), row(D_ATTN), row(D_ATTN), col, col, col, row(LANES)],
        out_shape=[jax.ShapeDtypeStruct((T, 2 * D_CONV), F32), std, std, trn, trn, trn,
                   jax.ShapeDtypeStruct((T, LANES), F32)],
        scratch_shapes=[pltpu.VMEM((N_IN_PAD, D), MXU)],
        compiler_params=_params(40, 1),
    )(x1, gm, win_blocks)


def _inproj_bwd_act(x1, gm, dx2, dag, dqT, dkT, dvT, dfl, win_blocks, name):
    T, D = x1.shape
    tm = _tile(T, 512)

    def body(x_ref, g_ref, dx2_ref, dag_ref, dqT_ref, dkT_ref, dvT_ref, dfl_ref, wb_ref, dx1_ref, dg_ref, h_ref,
             w_ref):
        i = pl.program_id(0)

        @pl.when(i == 0)
        def _():
            _rows_from_blocks(wb_ref, w_ref)

        xv = x_ref[...]
        gv = g_ref[...]
        h, r = _rms_fwd(xv, gv)
        h_ref[...] = h.astype(MXU)
        dh = _mm(dag_ref[...], w_ref[_AG0:_Q0, :])
        for c0, ref in ((_Q0, dqT_ref), (_K0, dkT_ref), (_V0, dvT_ref)):
            dh = dh + _mm_tn(ref[...].astype(MXU), w_ref[c0:c0 + D_ATTN, :])
        dh = dh + _mm(dfl_ref[...].astype(MXU), w_ref[_F0:N_IN_PAD, :])
        dxn, dgp = _rms_bwd(xv, r, gv, dh)
        dx1_ref[...] = dx2_ref[...] + dxn

        @pl.when(i == 0)
        def _():
            dg_ref[...] = dgp

        @pl.when(i > 0)
        def _():
            dg_ref[...] += dgp

    row = lambda w: pl.BlockSpec((tm, w), lambda i: (i, 0))
    col = pl.BlockSpec((D_ATTN, tm), lambda i: (0, i))
    full = lambda a, b: pl.BlockSpec((a, b), lambda i: (0, 0))
    return _pallas_call(
        body, name=name, grid=(T // tm,),
        in_specs=[row(D), full(1, D), row(D), row(2 * D_CONV), col, col, col, row(LANES),
                  pl.BlockSpec((N_DEV, _IN_BLOCK, D), lambda i: (0, 0, 0))],
        out_specs=[row(D), full(1, D), row(D)],
        out_shape=[jax.ShapeDtypeStruct((T, D), F32), jax.ShapeDtypeStruct((1, D), F32),
                   jax.ShapeDtypeStruct((T, D), MXU)],
        scratch_shapes=[pltpu.VMEM((N_IN_PAD, D), MXU)],
        compiler_params=_params(40, 1),
    )(x1, gm, dx2, dag, dqT, dkT, dvT, dfl, win_blocks)


def _inproj_bwd_weights(h, dag, dqT, dkT, dvT, dfl, name, after=()):
    T, D = h.shape
    operands = (h, dag, dqT, dkT, dvT, dfl)
    n = len(operands)

    def body(*refs):
        sources = refs[:n]
        blocks_ref = refs[n + len(after)]
        h_v, dag_v, dqT_v, dkT_v, dvT_v, dfl_v = buffers = refs[n + len(after) + 1:2 * n + len(after) + 1]
        dw_ref, sems = refs[2 * n + len(after) + 1:]
        copies = [pltpu.make_async_copy(src, dst, sems.at[k]) for k, (src, dst) in enumerate(zip(sources, buffers))]
        for cp in copies:
            cp.start()
        copies[0].wait()
        hb = h_v[...]
        copies[1].wait()
        dw_ref[_AG0:_Q0, :] = _mm_tn(dag_v[...], hb).astype(dw_ref.dtype)
        for k, (c0, ref) in enumerate(((_Q0, dqT_v), (_K0, dkT_v), (_V0, dvT_v))):
            copies[2 + k].wait()
            dw_ref[c0:c0 + D_ATTN, :] = _mm(ref[...].astype(MXU), hb).astype(dw_ref.dtype)
        copies[5].wait()
        dw_ref[_F0:N_IN_PAD, :] = _mm_tn(dfl_v[...].astype(MXU), hb).astype(dw_ref.dtype)
        for p in range(N_DEV):
            blocks_ref[p] = dw_ref[_IN_BLOCK * p:_IN_BLOCK * (p + 1), :]

    return _pallas_call(
        body, name=name, in_specs=[_UNREAD] * (n + len(after)), out_specs=pl.BlockSpec(memory_space=pltpu.VMEM),
        out_shape=jax.ShapeDtypeStruct((N_DEV, _IN_BLOCK, D), MXU),
        scratch_shapes=[pltpu.VMEM(a.shape, a.dtype) for a in operands]
        + [pltpu.VMEM((N_IN_PAD, D), MXU), pltpu.SemaphoreType.DMA((n,))],
        compiler_params=pltpu.CompilerParams(vmem_limit_bytes=56 * MIB),
    )(*operands, *after)


def _forget_fwd(fl, fbp, name):
    T = fl.shape[0]
    tb = _tile(T, 256)

    def body(fl_ref, fb_ref, cum_ref, cumT_ref):
        ri = lax.broadcasted_iota(jnp.int32, (tb, tb), 0)
        ci = lax.broadcasted_iota(jnp.int32, (tb, tb), 1)
        tri = (ri >= ci).astype(jnp.bfloat16)
        carry = jnp.zeros((1, LANES), F32)
        for b in range(T // tb):
            z = fl_ref[b * tb:(b + 1) * tb, :] + fb_ref[...]
            lf = jnp.minimum(z, 0.0) - jnp.log1p(jnp.exp(-jnp.abs(z)))
            c = _exact_tri_dot(tri, lf) + carry
            cum_ref[b * tb:(b + 1) * tb, :] = c
            carry = c[tb - 1:tb, :]
        cumT_ref[...] = cum_ref[...].T[:N_HEADS, :]

    return _pallas_call(
        body, name=name,
        out_shape=[jax.ShapeDtypeStruct((T, LANES), F32), jax.ShapeDtypeStruct((N_HEADS, T), F32)],
        compiler_params=pltpu.CompilerParams(vmem_limit_bytes=32 * MIB),
    )(fl, fbp)


def _forget_bwd(dcum, fl, fbp, name):
    T = fl.shape[0]
    tb = _tile(T, 256)

    def body(dc_ref, fl_ref, fb_ref, dfl_ref, dfb_ref):
        ri = lax.broadcasted_iota(jnp.int32, (tb, tb), 0)
        ci = lax.broadcasted_iota(jnp.int32, (tb, tb), 1)
        tri = (ri <= ci).astype(jnp.bfloat16)
        carry = jnp.zeros((1, LANES), F32)
        dfb = jnp.zeros((1, LANES), F32)
        for b in reversed(range(T // tb)):
            sl = slice(b * tb, (b + 1) * tb)
            dl = _exact_tri_dot(tri, dc_ref[sl, :]) + carry
            carry = dl[0:1, :]
            z = fl_ref[sl, :] + fb_ref[...]
            dfl = dl * jax.nn.sigmoid(-z)
            dfl_ref[sl, :] = dfl
            dfb = dfb + jnp.sum(dfl, axis=0, keepdims=True)
        dfb_ref[...] = dfb

    return _pallas_call(
        body, name=name,
        out_shape=[jax.ShapeDtypeStruct((T, LANES), F32), jax.ShapeDtypeStruct((1, LANES), F32)],
        compiler_params=pltpu.CompilerParams(vmem_limit_bytes=32 * MIB),
    )(dcum, fl, fbp)


def _causal_keep(i, j, tq, tk):
    key = j * tk + lax.broadcasted_iota(jnp.int32, (tk, tq), 0)
    qry = i * tq + lax.broadcasted_iota(jnp.int32, (tk, tq), 1)
    return key <= qry


def _split_hi_lo(x):
    hi = x.astype(MXU)
    lo = (x - hi.astype(F32)).astype(MXU)
    return hi, lo


def _attn_fwd(qT, k, vT, cum, cumT, name):
    T = k.shape[0]
    tq = _tile(T, 256)
    tk = _tile(tq, 256)
    kpq = tq // tk
    heads = [slice(HEAD_DIM * h, HEAD_DIM * (h + 1)) for h in range(N_HEADS)]

    def body(qT_ref, k_ref, vT_ref, cum_ref, cumT_ref, o_ref, lseT_ref, acc_s, m_s, l_s):
        i = pl.program_id(0)
        acc_s[...] = jnp.zeros_like(acc_s)
        m_s[...] = jnp.full_like(m_s, NEG)
        l_s[...] = jnp.zeros_like(l_s)

        def kblock(j, masked):
            rows = pl.ds(pl.multiple_of(j * tk, tk), tk)
            keep = _causal_keep(i, j, tq, tk) if masked else None
            bias = [cumT_ref[h:h + 1, :] - cum_ref[rows, h:h + 1] for h in range(N_HEADS)]
            qk = [_mm(k_ref[rows, hs], qT_ref[hs, :]) + bias[h] for h, hs in enumerate(heads)]
            for h, hs in enumerate(heads):
                sT = qk[h]
                if masked:
                    sT = jnp.where(keep, sT, NEG)
                m_old = m_s[h:h + 1, :]
                m_new = jnp.maximum(m_old, jnp.max(sT, axis=0, keepdims=True))
                alpha = jnp.exp(m_old - m_new)
                pT = jnp.exp(sT - m_new)
                l_s[h:h + 1, :] = alpha * l_s[h:h + 1, :] + jnp.sum(pT, axis=0, keepdims=True)
                p_hi, p_lo = _split_hi_lo(pT)
                vh = vT_ref[hs, rows]
                acc_s[hs, :] = alpha * acc_s[hs, :] + (_mm(vh, p_hi) + _mm(vh, p_lo))
                m_s[h:h + 1, :] = m_new

        def unmasked(j, c):
            kblock(j, False)
            return c

        lax.fori_loop(0, kpq * i, unmasked, 0)
        for d in range(kpq):
            kblock(kpq * i + d, True)
        for h, hs in enumerate(heads):
            acc_s[hs, :] = acc_s[hs, :] / l_s[h:h + 1, :]
        o_ref[...] = acc_s[...].T
        lseT_ref[...] = m_s[...] + jnp.log(l_s[...])

    full = lambda a, b: pl.BlockSpec((a, b), lambda i: (0, 0))
    colblk = lambda r: pl.BlockSpec((r, tq), lambda i: (0, i))
    return _pallas_call(
        body, name=name, grid=(T // tq,),
        in_specs=[colblk(D_ATTN), full(T, D_ATTN), full(D_ATTN, T), full(T, LANES), colblk(N_HEADS)],
        out_specs=[pl.BlockSpec((tq, D_ATTN), lambda i: (i, 0)), colblk(N_HEADS)],
        out_shape=[jax.ShapeDtypeStruct((T, D_ATTN), F32), jax.ShapeDtypeStruct((N_HEADS, T), F32)],
        scratch_shapes=[pltpu.VMEM((D_ATTN, tq), F32), pltpu.VMEM((N_HEADS, tq), F32),
                        pltpu.VMEM((N_HEADS, tq), F32)],
        compiler_params=_params(40, 1),
    )(qT, k, vT, cum, cumT)


def _attn_bwd(qT, k, kT, v, doT, lseT, deltaT, cum, cumT, name, after=()):
    T = k.shape[0]
    tq = _tile(T, 256)
    tk = _tile(tq, 256)
    kpq = tq // tk
    heads = [slice(HEAD_DIM * h, HEAD_DIM * (h + 1)) for h in range(N_HEADS)]

    def body(qT_ref, k_ref, kT_ref, v_ref, doT_ref, lseT_ref, dlT_ref, cum_ref, cumT_ref, *rest):
        dq_ref, dk_ref, dv_ref, dcum_ref, dq_s = rest[len(after):]
        i = pl.program_id(0)

        @pl.when(i == 0)
        def _():
            dk_ref[...] = jnp.zeros_like(dk_ref)
            dv_ref[...] = jnp.zeros_like(dv_ref)
            dcum_ref[...] = jnp.zeros_like(dcum_ref)

        dq_s[...] = jnp.zeros_like(dq_s)

        def kblock(j, masked):
            rows = pl.ds(pl.multiple_of(j * tk, tk), tk)
            keep = _causal_keep(i, j, tq, tk) if masked else None
            bias = [cumT_ref[h:h + 1, :] - cum_ref[rows, h:h + 1] for h in range(N_HEADS)]
            qk = [_mm(k_ref[rows, hs], qT_ref[hs, :]) + bias[h] for h, hs in enumerate(heads)]
            dps = [_mm(v_ref[rows, hs], doT_ref[hs, :]) for hs in heads]
            for h, hs in enumerate(heads):
                sT = qk[h]
                if masked:
                    sT = jnp.where(keep, sT, NEG)
                pT = jnp.exp(sT - lseT_ref[h:h + 1, :])
                dsT = pT * (dps[h] - dlT_ref[h:h + 1, :])
                dcum_ref[rows, h:h + 1] += -jnp.sum(dsT, axis=1, keepdims=True)
                dsb = dsT.astype(MXU)
                dv_ref[hs, rows] += _mm_nt(doT_ref[hs, :], pT.astype(MXU))
                dk_ref[hs, rows] += _mm_nt(qT_ref[hs, :], dsb)
                dq_s[hs, :] += _mm(kT_ref[hs, rows], dsb)

        def unmasked(j, c):
            kblock(j, False)
            return c

        lax.fori_loop(0, kpq * i, unmasked, 0)
        for d in range(kpq):
            kblock(kpq * i + d, True)
        dq_ref[...] = (dq_s[...] * SCALE).astype(dq_ref.dtype)

    full = lambda a, b: pl.BlockSpec((a, b), lambda i: (0, 0))
    colblk = lambda r: pl.BlockSpec((r, tq), lambda i: (0, i))
    return _pallas_call(
        body, name=name, grid=(T // tq,),
        in_specs=[colblk(D_ATTN), full(T, D_ATTN), full(D_ATTN, T), full(T, D_ATTN), colblk(D_ATTN),
                  colblk(N_HEADS), colblk(N_HEADS), full(T, LANES), colblk(N_HEADS)] + [_UNREAD] * len(after),
        out_specs=[colblk(D_ATTN), full(D_ATTN, T), full(D_ATTN, T), full(T, LANES)],
        out_shape=[
            jax.ShapeDtypeStruct((D_ATTN, T), MXU),
            jax.ShapeDtypeStruct((D_ATTN, T), F32),
            jax.ShapeDtypeStruct((D_ATTN, T), F32),
            jax.ShapeDtypeStruct((T, LANES), F32),
        ],
        scratch_shapes=[pltpu.VMEM((D_ATTN, tq), F32)],
        compiler_params=_params(48, 1),
    )(qT, k, kT, v, doT, lseT, deltaT, cum, cumT, *after)


_ROWS_PER_CHUNK = 64


def _glu_halo(ag_ref, agh_ref, uext_s, tm, first):
    a = ag_ref[:, :D_CONV]
    sg = jax.nn.sigmoid(ag_ref[:, D_CONV:])
    uh = agh_ref[:, :D_CONV] * jax.nn.sigmoid(agh_ref[:, D_CONV:])
    uext_s[0:CONV_HALO, :] = jnp.where(first, 0.0, uh)
    uext_s[CONV_HALO:CONV_HALO + tm, :] = a * sg
    return a, sg


_SUBLANES = 8


def _shifted_copies(ext_s, sh_s, rows):
    for k in range(1, _SUBLANES):
        sh_s[k, 0:rows, :] = ext_s[pl.ds(k, rows), :]


def _window(ext_s, sh_s, start, rows):
    k = start % _SUBLANES
    if k == 0:
        return ext_s[pl.ds(start, rows), :]
    return sh_s[k, pl.ds(start - k, rows), :]


def _layer_norm_stats(y):
    mu = jnp.mean(y, axis=-1, keepdims=True)
    xc = y - mu
    rs = lax.rsqrt(jnp.mean(xc * xc, axis=-1, keepdims=True) + EPS)
    return xc * rs, rs


def _conv_fwd(ag, w32, cb, lg, lb, name):
    T = ag.shape[0]
    tm = _tile(T, 512)
    rc = _tile(tm, _ROWS_PER_CHUNK)
    hb = tm // CONV_HALO

    def body(ag_ref, agh_ref, w_ref, cb_ref, lg_ref, lb_ref, yc_ref, c_ref, uext_s, ush_s):
        i = pl.program_id(0)
        _glu_halo(ag_ref, agh_ref, uext_s, tm, i == 0)
        _shifted_copies(uext_s, ush_s, tm + CONV_HALO - _SUBLANES)
        for r0 in range(0, tm, rc):
            acc = jnp.zeros((rc, D_CONV), F32)
            for t in range(CONV_TAPS):
                acc = acc + _window(uext_s, ush_s, r0 + CONV_HALO - (CONV_TAPS - 1) + t, rc) * w_ref[t:t + 1, :]
            y = acc + cb_ref[...]
            yc_ref[r0:r0 + rc, :] = y
            n, _ = _layer_norm_stats(y)
            z = n * lg_ref[...] + lb_ref[...]
            c_ref[r0:r0 + rc, :] = z * jax.nn.sigmoid(z)

    row = lambda w: pl.BlockSpec((tm, w), lambda i: (i, 0))
    full = lambda a, b: pl.BlockSpec((a, b), lambda i: (0, 0))
    return _pallas_call(
        body, name=name, grid=(T // tm,),
        in_specs=[row(2 * D_CONV),
                  pl.BlockSpec((CONV_HALO, 2 * D_CONV), lambda i: (jnp.maximum(i * hb - 1, 0), 0)),
                  full(CONV_HALO, D_CONV), full(1, D_CONV), full(1, D_CONV), full(1, D_CONV)],
        out_specs=[row(D_CONV), row(D_CONV)],
        out_shape=[jax.ShapeDtypeStruct((T, D_CONV), F32), jax.ShapeDtypeStruct((T, D_CONV), F32)],
        scratch_shapes=[pltpu.VMEM((CONV_HALO + tm, D_CONV), F32),
                        pltpu.VMEM((_SUBLANES, CONV_HALO + tm, D_CONV), F32)],
        compiler_params=_params(32, 1),
    )(ag, ag, w32, cb, lg, lb)


def _conv_bwd(dc, yc, ag, w32, lg, lb, name):
    T = ag.shape[0]
    tm = _tile(T, 512)
    rc = _tile(tm, _ROWS_PER_CHUNK)
    I = T // tm
    hb = tm // CONV_HALO
    n_halo_blocks = T // CONV_HALO

    def body(dc_ref, yc_ref, dch_ref, ych_ref, ag_ref, agh_ref, w_ref, lg_ref, lb_ref,
             dag_ref, dw_ref, dcb_ref, dlg_ref, dlb_ref, uext_s, dext_s, ush_s, dsh_s):
        i = pl.program_id(0)
        lgv = lg_ref[...]
        lbv = lb_ref[...]

        def ln_bwd(dcv, ycv):
            n, rs = _layer_norm_stats(ycv)
            z = n * lgv + lbv
            dz = dcv * _silu_grad(z, jax.nn.sigmoid(z))
            dn = dz * lgv
            dy = rs * (dn - jnp.mean(dn, axis=-1, keepdims=True) - n * jnp.mean(dn * n, axis=-1, keepdims=True))
            return dy, dz, n

        dy, dz, n = ln_bwd(dc_ref[...], yc_ref[...])
        dyh, _, _ = ln_bwd(dch_ref[...], ych_ref[...])
        dext_s[0:tm, :] = dy
        dext_s[tm:tm + CONV_HALO, :] = jnp.where(i == I - 1, 0.0, dyh)
        a, sg = _glu_halo(ag_ref, agh_ref, uext_s, tm, i == 0)
        _shifted_copies(uext_s, ush_s, tm + CONV_HALO - _SUBLANES)
        _shifted_copies(dext_s, dsh_s, tm + CONV_HALO - _SUBLANES)

        @pl.when(i == 0)
        def _():
            dw_ref[...] = jnp.zeros_like(dw_ref)
            dcb_ref[...] = jnp.zeros_like(dcb_ref)
            dlg_ref[...] = jnp.zeros_like(dlg_ref)
            dlb_ref[...] = jnp.zeros_like(dlb_ref)

        dcb_ref[...] += jnp.sum(dy, axis=0, keepdims=True)
        dlg_ref[...] += jnp.sum(dz * n, axis=0, keepdims=True)
        dlb_ref[...] += jnp.sum(dz, axis=0, keepdims=True)
        for t in range(CONV_TAPS):
            u_t = _window(uext_s, ush_s, CONV_HALO - (CONV_TAPS - 1) + t, tm)
            dw_ref[t:t + 1, :] += jnp.sum(dy * u_t, axis=0, keepdims=True)
        for r0 in range(0, tm, rc):
            acc = jnp.zeros((rc, D_CONV), F32)
            for t in range(CONV_TAPS):
                acc = acc + _window(dext_s, dsh_s, r0 + (CONV_TAPS - 1) - t, rc) * w_ref[t:t + 1, :]
            a_c = a[r0:r0 + rc, :]
            sg_c = sg[r0:r0 + rc, :]
            dag_ref[r0:r0 + rc, :D_CONV] = (acc * sg_c).astype(dag_ref.dtype)
            dag_ref[r0:r0 + rc, D_CONV:] = (acc * a_c * sg_c * (1.0 - sg_c)).astype(dag_ref.dtype)

    row = lambda w: pl.BlockSpec((tm, w), lambda i: (i, 0))
    full = lambda a, b: pl.BlockSpec((a, b), lambda i: (0, 0))
    nxt = pl.BlockSpec((CONV_HALO, D_CONV), lambda i: (jnp.minimum((i + 1) * hb, n_halo_blocks - 1), 0))
    return _pallas_call(
        body, name=name, grid=(I,),
        in_specs=[row(D_CONV), row(D_CONV), nxt, nxt, row(2 * D_CONV),
                  pl.BlockSpec((CONV_HALO, 2 * D_CONV), lambda i: (jnp.maximum(i * hb - 1, 0), 0)),
                  full(CONV_HALO, D_CONV), full(1, D_CONV), full(1, D_CONV)],
        out_specs=[row(2 * D_CONV), full(CONV_HALO, D_CONV), full(1, D_CONV), full(1, D_CONV), full(1, D_CONV)],
        out_shape=[
            jax.ShapeDtypeStruct((T, 2 * D_CONV), MXU),
            jax.ShapeDtypeStruct((CONV_HALO, D_CONV), F32),
            jax.ShapeDtypeStruct((1, D_CONV), F32),
            jax.ShapeDtypeStruct((1, D_CONV), F32),
            jax.ShapeDtypeStruct((1, D_CONV), F32),
        ],
        scratch_shapes=[pltpu.VMEM((CONV_HALO + tm, D_CONV), F32), pltpu.VMEM((tm + CONV_HALO, D_CONV), F32),
                        pltpu.VMEM((_SUBLANES, CONV_HALO + tm, D_CONV), F32),
                        pltpu.VMEM((_SUBLANES, CONV_HALO + tm, D_CONV), F32)],
        compiler_params=_params(40, 1),
    )(dc, yc, dc, yc, ag, ag, w32, lg, lb)


def _outproj_fwd(x1, c, o, gc, ga, wout, name):
    T, D = x1.shape
    tm = _tile(T, 512)

    def body(x_ref, c_ref, o_ref, gc_ref, ga_ref, w_ref, x2_ref):
        yc, _ = _rms_fwd(c_ref[...], gc_ref[...])
        ya, _ = _rms_fwd(o_ref[...], ga_ref[...])
        x2_ref[...] = (x_ref[...] + _mm(yc.astype(MXU), w_ref[:D_CONV, :])
                       + _mm(ya.astype(MXU), w_ref[D_CONV:, :]))

    row = lambda w: pl.BlockSpec((tm, w), lambda i: (i, 0))
    full = lambda a, b: pl.BlockSpec((a, b), lambda i: (0, 0))
    return _pallas_call(
        body, name=name, grid=(T // tm,),
        in_specs=[row(D), row(D_CONV), row(D_ATTN), full(1, D_CONV), full(1, D_ATTN), full(D_CONV + D_ATTN, D)],
        out_specs=row(D),
        out_shape=jax.ShapeDtypeStruct((T, D), F32),
        compiler_params=_params(32, 1),
    )(x1, c, o, gc, ga, wout)


def _outproj_bwd(dx2, c, o, gc, ga, wout, name):
    T, D = dx2.shape
    tm = _tile(T, 512)
    I = T // tm

    def body(dx_ref, c_ref, o_ref, gc_ref, ga_ref, w_ref,
             dc_ref, doT_ref, dlT_ref, dw_ref, dgc_ref, dga_ref, acc_s):
        i = pl.program_id(0)
        dxb = dx_ref[...].astype(MXU)
        cv = c_ref[...]
        ov = o_ref[...]
        yc, rcn = _rms_fwd(cv, gc_ref[...])
        ya, ra = _rms_fwd(ov, ga_ref[...])
        dyc = _mm_nt(dxb, w_ref[:D_CONV, :])
        dya = _mm_nt(dxb, w_ref[D_CONV:, :])
        dwc = _mm_tn(yc.astype(MXU), dxb)
        dwa = _mm_tn(ya.astype(MXU), dxb)
        dcv, dgc = _rms_bwd(cv, rcn, gc_ref[...], dyc)
        dov, dga = _rms_bwd(ov, ra, ga_ref[...], dya)
        dc_ref[...] = dcv
        dob = dov.astype(doT_ref.dtype)
        doT_ref[...] = dov.T.astype(doT_ref.dtype)
        chan = lax.broadcasted_iota(jnp.int32, (D_ATTN, LANES), 0)
        head = lax.broadcasted_iota(jnp.int32, (D_ATTN, LANES), 1)
        in_head = ((chan >= head * HEAD_DIM) & (chan < (head + 1) * HEAD_DIM)).astype(jnp.bfloat16)
        dlT_ref[...] = _exact_dot_01(dob.astype(F32) * ov, in_head).T[:N_HEADS, :]

        @pl.when(i == 0)
        def _():
            acc_s[:D_CONV, :] = dwc
            acc_s[D_CONV:, :] = dwa
            dgc_ref[...] = dgc
            dga_ref[...] = dga

        @pl.when(i > 0)
        def _():
            acc_s[:D_CONV, :] += dwc
            acc_s[D_CONV:, :] += dwa
            dgc_ref[...] += dgc
            dga_ref[...] += dga

        @pl.when(i == I - 1)
        def _():
            dw_ref[...] = acc_s[...].astype(dw_ref.dtype)

    row = lambda w: pl.BlockSpec((tm, w), lambda i: (i, 0))
    full = lambda a, b: pl.BlockSpec((a, b), lambda i: (0, 0))
    return _pallas_call(
        body, name=name, grid=(I,),
        in_specs=[row(D), row(D_CONV), row(D_ATTN), full(1, D_CONV), full(1, D_ATTN), full(D_CONV + D_ATTN, D)],
        out_specs=[row(D_CONV), pl.BlockSpec((D_ATTN, tm), lambda i: (0, i)),
                   pl.BlockSpec((N_HEADS, tm), lambda i: (0, i)),
                   full(D_CONV + D_ATTN, D), full(1, D_CONV), full(1, D_ATTN)],
        out_shape=[
            jax.ShapeDtypeStruct((T, D_CONV), F32),
            jax.ShapeDtypeStruct((D_ATTN, T), MXU),
            jax.ShapeDtypeStruct((N_HEADS, T), F32),
            jax.ShapeDtypeStruct((D_CONV + D_ATTN, D), MXU),
            jax.ShapeDtypeStruct((1, D_CONV), F32),
            jax.ShapeDtypeStruct((1, D_ATTN), F32),
        ],
        scratch_shapes=[pltpu.VMEM((D_CONV + D_ATTN, D), F32)],
        compiler_params=_params(40, 1),
    )(dx2, c, o, gc, ga, wout)


def _ffn_down_loss(x, A, w2, gf, target, name):
    T, D = x.shape
    J, _, bf = A.shape
    tm = _tile(T, 512)

    def body(x_ref, A_ref, w2_ref, g_ref, t_ref, loss_ref, dx_ref, dg_ref):
        i = pl.program_id(0)
        f = _mm(A_ref[0], w2_ref[0:bf, :])
        for j in range(1, J):
            f = f + _mm(A_ref[j], w2_ref[j * bf:(j + 1) * bf, :])
        xv = x_ref[...] + 0.5 * f
        gv = g_ref[...]
        out, r = _rms_fwd(xv, gv)
        err = out - t_ref[...]
        part = jnp.full((1, LANES), 0.5 / D, F32) * jnp.sum(err * err)
        dxn, dgp = _rms_bwd(xv, r, gv, err * (1.0 / D))
        dx_ref[...] = dxn

        @pl.when(i == 0)
        def _():
            loss_ref[...] = part
            dg_ref[...] = dgp

        @pl.when(i > 0)
        def _():
            loss_ref[...] += part
            dg_ref[...] += dgp

    row = lambda w: pl.BlockSpec((tm, w), lambda i: (i, 0))
    full = lambda a, b: pl.BlockSpec((a, b), lambda i: (0, 0))
    return _pallas_call(
        body, name=name, grid=(T // tm,),
        in_specs=[row(D), pl.BlockSpec((J, tm, bf), lambda i: (0, i, 0)), full(J * bf, D), full(1, D), row(D)],
        out_specs=[full(1, LANES), row(D), full(1, D)],
        out_shape=[jax.ShapeDtypeStruct((1, LANES), F32), jax.ShapeDtypeStruct((T, D), F32),
                   jax.ShapeDtypeStruct((1, D), F32)],
        compiler_params=_params(56, 1),
    )(x, A, w2, gf, target)


def _row_tile(rows):
    for cand in (256, 176, 128, 64, 32, 16):
        if rows % cand == 0:
            return cand
    return rows


def _adamw(w, m, v, parts, name):
    R, C = w.shape
    P = parts.shape[0]
    tr = _row_tile(R)
    c1 = 1.0 - ADAM_B1 ** ADAM_STEP
    c2 = 1.0 - ADAM_B2 ** ADAM_STEP

    def body(w_ref, m_ref, v_ref, p_ref, g_ref, d_ref, nm_ref, nv_ref):
        g = p_ref[0].astype(F32)
        for s in range(1, P):
            g = g + p_ref[s].astype(F32)
        wv = w_ref[...]
        mn = ADAM_B1 * m_ref[...] + (1.0 - ADAM_B1) * g
        vn = ADAM_B2 * v_ref[...] + (1.0 - ADAM_B2) * (g * g)
        g_ref[...] = g
        nm_ref[...] = mn
        nv_ref[...] = vn
        d_ref[...] = -ADAM_LR * ((mn / c1) / (jnp.sqrt(vn / c2) + ADAM_EPS) + ADAM_WD * wv)

    blk = pl.BlockSpec((tr, C), lambda i: (i, 0))
    out = jax.ShapeDtypeStruct((R, C), F32)
    return _pallas_call(
        body, name=name, grid=(R // tr,),
        in_specs=[blk, blk, blk, pl.BlockSpec((P, tr, C), lambda i: (0, i, 0))],
        out_specs=[blk, blk, blk, blk],
        out_shape=[out, out, out, out],
        compiler_params=_params(32, 1),
    )(w, m, v, parts)


def _position():
    return lax.axis_index("x"), lax.axis_index("y"), lax.axis_index("c")


def _flat(px, py, pc):
    return 4 * px + 2 * py + pc


def _row_halves(rows, dtype):
    tile = _SUBLANES * (4 // jnp.dtype(dtype).itemsize)
    half = rows // 2 // tile * tile
    assert half > 0
    return (0, half), (half, rows - half)


def _gather_body(ins, outs, send_sems, recv_sems, local_sems, handshake):
    n = len(ins)
    x, y, c = _position()
    me, sibling = (x, y, c), (x, y, 1 - c)
    across_x, across_y, diagonal = (1 - x, y), (x, 1 - y), (1 - x, 1 - y)
    if handshake:
        _handshake([sibling] + [(*chip, cc) for chip in (across_x, across_y, diagonal) for cc in (c, 1 - c)])

    def copy(a, k, block, to, rows=None, src=None):
        dst = outs[a].at[_flat(*block)]
        if rows is not None:
            dst = dst.at[pl.ds(*rows)]
        return pltpu.make_async_remote_copy(
            src_ref=dst if src is None else src, dst_ref=dst,
            send_sem=send_sems.at[a, k], recv_sem=recv_sems.at[a, k],
            device_id=to, device_id_type=MESH)

    halves = [_row_halves(ins[a].shape[0], ins[a].dtype) for a in range(n)]
    mine = [pltpu.make_async_copy(ins[a], outs[a].at[_flat(*me)], local_sems.at[a]) for a in range(n)]
    for cp in mine:
        cp.start()
    sent = []

    def start(cp):
        cp.start()
        sent.append(cp)

    for a in range(n):
        start(copy(a, 0, me, sibling, src=ins[a]))
        start(copy(a, 1, me, (*across_x, c), src=ins[a]))
        start(copy(a, 2, me, (*across_y, c), src=ins[a]))
    for a in range(n):
        low, high = halves[a]
        copy(a, 1, (*across_x, c), me).wait_recv()
        start(copy(a, 3, (*across_x, c), (*across_y, c), rows=low))
        start(copy(a, 5, (*across_x, c), sibling))
        copy(a, 2, (*across_y, c), me).wait_recv()
        start(copy(a, 4, (*across_y, c), (*across_x, c), rows=high))
        start(copy(a, 6, (*across_y, c), sibling))
    for a in range(n):
        low, high = halves[a]
        copy(a, 3, (*diagonal, c), me, rows=low).wait_recv()
        start(copy(a, 7, (*diagonal, c), sibling, rows=low))
        copy(a, 4, (*diagonal, c), me, rows=high).wait_recv()
        start(copy(a, 8, (*diagonal, c), sibling, rows=high))
    for a in range(n):
        low, high = halves[a]
        copy(a, 0, sibling, me).wait_recv()
        copy(a, 5, (*across_x, 1 - c), me).wait_recv()
        copy(a, 6, (*across_y, 1 - c), me).wait_recv()
        copy(a, 7, (*diagonal, 1 - c), me, rows=low).wait_recv()
        copy(a, 8, (*diagonal, 1 - c), me, rows=high).wait_recv()
    for cp in sent:
        cp.wait_send()
    for cp in mine:
        cp.wait()


_GATHER_SLOTS = 9


def _gather_scratch(n):
    return [pltpu.SemaphoreType.DMA((n, _GATHER_SLOTS)), pltpu.SemaphoreType.DMA((n, _GATHER_SLOTS)),
            pltpu.SemaphoreType.DMA((n,))]


def _all_gather(shards, name):
    n = len(shards)

    def body(*refs):
        _gather_body(refs[:n], refs[n:2 * n], *refs[2 * n:], handshake=False)

    hbm = pl.BlockSpec(memory_space=pltpu.HBM)
    return _pallas_call(
        body, name=name,
        in_specs=[hbm] * n, out_specs=[hbm] * n,
        out_shape=[jax.ShapeDtypeStruct((N_DEV,) + s.shape, s.dtype) for s in shards],
        scratch_shapes=_gather_scratch(n),
    )(*shards)


def _handshake(peers):
    barrier = pltpu.get_barrier_semaphore()
    for peer in peers:
        pl.semaphore_signal(barrier, inc=1, device_id=peer, device_id_type=MESH)
    pl.semaphore_wait(barrier, len(peers))


def _sequencer_call(body, name, collective_id, out_type, scratch_types, operands):
    return pl.kernel(
        body, name=name, out_type=out_type,
        mesh=plsc.ScalarSubcoreMesh(axis_name="sequencer", num_cores=1),
        scratch_types=scratch_types,
        compiler_params=pltpu.CompilerParams(collective_id=collective_id),
    )(*operands)


def _seq_all_gather(shards, name, collective_id, after):
    n = len(shards)

    def body(*refs):
        _gather_body(refs[:n], refs[n + 1:2 * n + 1], *refs[2 * n + 1:], handshake=True)

    return _sequencer_call(
        body, name, collective_id,
        [jax.ShapeDtypeStruct((N_DEV,) + s.shape, s.dtype) for s in shards],
        _gather_scratch(n), list(shards) + [after])


def _seq_to_sibling(parts, name, collective_id, after):
    n = len(parts)

    def body(*refs):
        ins, outs = refs[:n], refs[n + len(after):2 * n + len(after)]
        send_sems, recv_sems = refs[2 * n + len(after):]
        x, y, c = _position()
        sibling = (x, y, 1 - c)
        _handshake([sibling])
        sent = []
        for a in range(n):
            for q in range(N_CHIPS):
                cp = pltpu.make_async_remote_copy(
                    src_ref=ins[a].at[2 * q + (1 - c)], dst_ref=outs[a].at[q],
                    send_sem=send_sems.at[a, q], recv_sem=recv_sems.at[a, q],
                    device_id=sibling, device_id_type=MESH)
                cp.start()
                sent.append(cp)
        for cp in sent:
            cp.wait_recv()
        for cp in sent:
            cp.wait_send()

    return _sequencer_call(
        body, name, collective_id,
        [jax.ShapeDtypeStruct((N_CHIPS,) + p.shape[1:], p.dtype) for p in parts],
        [pltpu.SemaphoreType.DMA((n, N_CHIPS)), pltpu.SemaphoreType.DMA((n, N_CHIPS))],
        list(parts) + list(after))


def _seq_to_chips(partials, name, collective_id):
    n = len(partials)

    def body(*refs):
        ins, outs = refs[:n], refs[n:2 * n]
        send_sems, recv_sems, local_sems = refs[2 * n:]
        x, y, c = _position()
        my_chip = 2 * x + y
        chips = [(1 - x, y), (x, 1 - y), (1 - x, 1 - y)]
        _handshake([(*chip, c) for chip in chips])
        mine = [pltpu.make_async_copy(ins[a].at[my_chip], outs[a].at[my_chip], local_sems.at[a]) for a in range(n)]
        for cp in mine:
            cp.start()
        sent = []
        for a in range(n):
            for j, (px, py) in enumerate(chips):
                cp = pltpu.make_async_remote_copy(
                    src_ref=ins[a].at[2 * px + py], dst_ref=outs[a].at[my_chip],
                    send_sem=send_sems.at[a, j], recv_sem=recv_sems.at[a, j],
                    device_id=(px, py, c), device_id_type=MESH)
                cp.start()
                sent.append(cp)
        for cp in sent:
            cp.wait_recv()
        for cp in sent:
            cp.wait_send()
        for cp in mine:
            cp.wait()

    return _sequencer_call(
        body, name, collective_id,
        [jax.ShapeDtypeStruct(p.shape, p.dtype) for p in partials],
        [pltpu.SemaphoreType.DMA((n, 3)), pltpu.SemaphoreType.DMA((n, 3)), pltpu.SemaphoreType.DMA((n,))],
        list(partials))


def _pair_add(parts, recvs, name, after=()):
    n = len(parts)
    core = lax.axis_index("c").astype(jnp.int32).reshape(1)

    def body(c_ref, *refs):
        ps, rs, outs = refs[:n], refs[n:2 * n], refs[2 * n + len(after):]
        for p_ref, r_ref, o_ref in zip(ps, rs, outs):
            o_ref[...] = (p_ref[...].astype(F32) + r_ref[...].astype(F32)).astype(o_ref.dtype)

    mine = lambda p: pl.BlockSpec((None,) + p.shape[1:], lambda q, c: (2 * q + c[0], 0, 0))
    blk = lambda p: pl.BlockSpec((None,) + p.shape[1:], lambda q, c: (q, 0, 0))
    return pl.pallas_call(
        body, name=name,
        grid_spec=pltpu.PrefetchScalarGridSpec(
            num_scalar_prefetch=1, grid=(N_CHIPS,),
            in_specs=[mine(p) for p in parts] + [blk(p) for p in parts] + [_UNREAD] * len(after),
            out_specs=[blk(p) for p in parts]),
        out_shape=[pltpu.HBM((N_CHIPS,) + p.shape[1:], p.dtype) for p in parts],
        compiler_params=_params(40, 1),
    )(core, *[pltpu.with_memory_space_constraint(a, pltpu.HBM) for a in (*parts, *recvs, *after)])


class _Reduced(NamedTuple):
    partials: list
    reduced: list


def _blocks(g):
    return g.reshape(N_DEV, -1, g.shape[-1])


def _reduce_scatter(parts, tag, ids, after=(), add_after=()):
    from_sibling = _seq_to_sibling(parts, "rs_sibling_" + tag, ids[0], after)
    partials = _pair_add(parts, from_sibling, "rs_add_" + tag, add_after)
    return _Reduced(partials, _seq_to_chips(partials, "rs_chips_" + tag, ids[1]))


_SMALL = ("ffn1_norm", "mix_norm", "conv_b", "conv_ln_g", "conv_ln_b", "forget_b", "out_norm_conv",
          "out_norm_attn", "ffn2_norm", "final_norm")
_PACK_WIDTH = 2 * D_CONV
_SLOT = dict(ffn1_norm=(0, 0), mix_norm=(1, 0), ffn2_norm=(2, 0), final_norm=(3, 0), conv_b=(4, 0),
             conv_ln_g=(4, D_CONV), conv_ln_b=(5, 0), out_norm_conv=(5, D_CONV), out_norm_attn=(6, 0),
             forget_b=(6, D_CONV))
_LOSS_ROW = 7
_CONV_ROW0 = 8
_PACK_ROWS = _CONV_ROW0 + CONV_HALO


def _pack_small(small, name):
    arrays = [small[n] for n in _SMALL] + [small["conv_w"], small["loss"]]

    def body(*refs):
        out = refs[-1]
        out[...] = jnp.zeros_like(out)
        for n, ref in zip(_SMALL, refs):
            row, lane = _SLOT[n]
            out[row:row + 1, lane:lane + ref.shape[1]] = ref[...]
        out[_CONV_ROW0:, :D_CONV] = refs[len(_SMALL)][...]
        out[_LOSS_ROW:_LOSS_ROW + 1, :LANES] = refs[len(_SMALL) + 1][...]

    return _pallas_call(body, name=name, out_shape=jax.ShapeDtypeStruct((_PACK_ROWS, _PACK_WIDTH), F32))(*arrays)


def _adamw_small(gathered, w, m, v, name):
    c1 = 1.0 - ADAM_B1 ** ADAM_STEP
    c2 = 1.0 - ADAM_B2 ** ADAM_STEP
    k = len(_SMALL)

    def body(g_ref, *refs):
        ws, ms, vs = refs[:k], refs[k:2 * k], refs[2 * k:3 * k]
        outs = refs[3 * k:]
        total = g_ref[0]
        for s in range(1, N_DEV):
            total = total + g_ref[s]
        for i, n in enumerate(_SMALL):
            row, lane = _SLOT[n]
            width = ws[i].shape[1]
            g = total[row:row + 1, lane:lane + width]
            mn = ADAM_B1 * ms[i][...] + (1.0 - ADAM_B1) * g
            vn = ADAM_B2 * vs[i][...] + (1.0 - ADAM_B2) * (g * g)
            o_g, o_d, o_m, o_v = outs[4 * i:4 * i + 4]
            o_g[...] = g
            o_m[...] = mn
            o_v[...] = vn
            o_d[...] = -ADAM_LR * ((mn / c1) / (jnp.sqrt(vn / c2) + ADAM_EPS) + ADAM_WD * ws[i][...])
        outs[4 * k][...] = total[_CONV_ROW0:, :D_CONV]
        outs[4 * k + 1][...] = total[_LOSS_ROW:_LOSS_ROW + 1, :LANES]

    shapes = []
    for n in _SMALL:
        shapes += [jax.ShapeDtypeStruct(w[n].shape, F32)] * 4
    shapes.append(jax.ShapeDtypeStruct((CONV_HALO, D_CONV), F32))
    shapes.append(jax.ShapeDtypeStruct((1, LANES), F32))
    res = _pallas_call(body, name=name, out_shape=shapes)(
        gathered, *[w[n] for n in _SMALL], *[m[n] for n in _SMALL], *[v[n] for n in _SMALL])
    return {n: res[4 * i:4 * i + 4] for i, n in enumerate(_SMALL)}, res[4 * k], res[4 * k + 1]


def _local_step(x, target, norms, shard):
    D = x.shape[1]
    J = N_DEV // 2
    as13 = lambda g: g.reshape(2, J, g.shape[1], D)

    (g13_1,) = _all_gather([shard["ffn1_w13"]], "gather_ffn1_w13")
    (g2_1,) = _seq_all_gather([shard["ffn1_w2"]], "gather_ffn1_w2", 10, after=g13_1)
    w13_1 = as13(g13_1)
    G1, U1, A1 = _ffn_up(x, norms["ffn1_norm"], w13_1, "ffn1_up")
    gin, gconv = _seq_all_gather([shard["w_in"], shard["conv_w"]], "gather_mix", 1, after=G1)
    w2_1 = g2_1.reshape(-1, D)
    x1 = _ffn_down(x, A1, w2_1, "ffn1_down")
    gout, g13_2, g2_2 = _seq_all_gather([shard["w_out"], shard["ffn2_w13"], shard["ffn2_w2"]], "gather_ffn2", 2,
                                        after=x1)
    wout = gout.reshape(-1, D)
    conv_w32 = jnp.pad(gconv.transpose(1, 0, 2).reshape(CONV_TAPS, D_CONV), ((0, CONV_HALO - CONV_TAPS), (0, 0)))

    ag, k, v, qT, kT, vT, fl = _inproj_fwd(x1, norms["mix_norm"], gin, "inproj_fwd")
    cum, cumT = _forget_fwd(fl, norms["forget_b"], "forget_fwd")
    yc, c = _conv_fwd(ag, conv_w32, norms["conv_b"], norms["conv_ln_g"], norms["conv_ln_b"], "conv_fwd")
    o, lseT = _attn_fwd(qT, k, vT, cum, cumT, "attn_fwd")
    x2 = _outproj_fwd(x1, c, o, norms["out_norm_conv"], norms["out_norm_attn"], wout, "outproj_fwd")
    w13_2, w2_2 = as13(g13_2), g2_2.reshape(-1, D)
    G2, U2, A2 = _ffn_up(x2, norms["ffn2_norm"], w13_2, "ffn2_up")
    loss, dx3, d_final = _ffn_down_loss(x2, A2, w2_2, norms["final_norm"], target, "ffn2_down_loss")

    dw2_2 = _ffn_w2_grad(dx3, A2, "ffn2_w2_grad")
    dx2, d_ffn2n, h3, dG2, dU2 = _ffn_bwd_act(x2, norms["ffn2_norm"], dx3, G2, U2, w13_2, w2_2, "ffn2_bwd_act")
    dw13_2 = _ffn_w13_grad(h3, dG2, dU2, "ffn2_w13_grad")
    dc, dobT, deltaT, dwout, d_onc, d_ona = _outproj_bwd(
        dx2, c, o, norms["out_norm_conv"], norms["out_norm_attn"], wout, "outproj_bwd")
    red_ffn2 = _reduce_scatter([_blocks(dw13_2), _blocks(dw2_2)], "ffn2", (3, 4), add_after=(dc,))
    dqT, dkT, dvT, dcum = _attn_bwd(qT, k, kT, v, dobT, lseT, deltaT, cum, cumT, "attn_bwd",
                                    after=red_ffn2.partials)
    dfl, d_fb = _forget_bwd(dcum, fl, norms["forget_b"], "forget_bwd")
    dag, d_convw, d_cb, d_lg, d_lb = _conv_bwd(dc, yc, ag, conv_w32, norms["conv_ln_g"], norms["conv_ln_b"], "conv_bwd")
    dx1, d_mixn, h2 = _inproj_bwd_act(x1, norms["mix_norm"], dx2, dag, dqT, dkT, dvT, dfl, gin, "inproj_bwd_act")
    dw2_1 = _ffn_w2_grad(dx1, A1, "ffn1_w2_grad")
    early = [_blocks(dwout), _blocks(dw2_1)]
    sib_early = _seq_to_sibling(early, "rs_sibling_mix_early", 11, red_ffn2.reduced[:1])
    dwin_blocks = _inproj_bwd_weights(h2, dag, dqT, dkT, dvT, dfl, "inproj_bwd_weights")
    sib_w_in = _seq_to_sibling([dwin_blocks], "rs_sibling_mix", 5, sib_early[:1])
    mix_partials = _pair_add([dwin_blocks] + early, sib_w_in + sib_early, "rs_add_mix")
    red_mix = _Reduced(mix_partials, _seq_to_chips(mix_partials, "rs_chips_mix", 6))
    dx, d_ffn1n, h1, dG1, dU1 = _ffn_bwd_act(x, norms["ffn1_norm"], dx1, G1, U1, w13_1, w2_1, "ffn1_bwd_act",
                                             after=red_mix.partials)
    small = dict(ffn1_norm=d_ffn1n, mix_norm=d_mixn, conv_b=d_cb, conv_ln_g=d_lg, conv_ln_b=d_lb,
                 forget_b=d_fb, out_norm_conv=d_onc, out_norm_attn=d_ona, ffn2_norm=d_ffn2n,
                 final_norm=d_final, conv_w=d_convw, loss=loss)
    packed_small = _pack_small(small, "pack_small_grads")
    (gathered_small,) = _seq_all_gather([packed_small], "gather_small_grads", 9, after=red_mix.partials[0])
    dw13_1 = _ffn_w13_grad(h1, dG1, dU1, "ffn1_w13_grad")
    red_w13_1 = _reduce_scatter([_blocks(dw13_1)], "ffn1_w13", (7, 8), after=[red_mix.reduced[0], gathered_small])
    big = dict(ffn1_w13=red_w13_1.reduced[0], ffn1_w2=red_mix.reduced[2], w_in=red_mix.reduced[0],
               w_out=red_mix.reduced[1], ffn2_w13=red_ffn2.reduced[0], ffn2_w2=red_ffn2.reduced[1])
    return dx, gathered_small, big


_BIG = ("ffn1_w13", "ffn1_w2", "w_in", "w_out", "ffn2_w13", "ffn2_w2")
_TRANSPOSED = ("ffn1_w13", "ffn2_w13", "w_in")
_ORDER = ("ffn1_norm", "ffn1_w13", "ffn1_w2", "mix_norm", "w_in", "conv_w", "conv_b", "conv_ln_g", "conv_ln_b",
          "forget_b", "out_norm_conv", "out_norm_attn", "w_out", "ffn2_norm", "ffn2_w13", "ffn2_w2", "final_norm")


def kernel(x, ffn1_norm, ffn1_w13, ffn1_w2, mix_norm, w_in, conv_w, conv_b, conv_ln_g, conv_ln_b, forget_b, out_norm_conv, out_norm_attn, w_out, ffn2_norm, ffn2_w13, ffn2_w2, final_norm, loss_target, m_ffn1_norm, m_ffn1_w13, m_ffn1_w2, m_mix_norm, m_w_in, m_conv_w, m_conv_b, m_conv_ln_g, m_conv_ln_b, m_forget_b, m_out_norm_conv, m_out_norm_attn, m_w_out, m_ffn2_norm, m_ffn2_w13, m_ffn2_w2, m_final_norm, v_ffn1_norm, v_ffn1_w13, v_ffn1_w2, v_mix_norm, v_w_in, v_conv_w, v_conv_b, v_conv_ln_g, v_conv_ln_b, v_forget_b, v_out_norm_conv, v_out_norm_attn, v_w_out, v_ffn2_norm, v_ffn2_w13, v_ffn2_w2, v_final_norm):
    w = dict(ffn1_norm=ffn1_norm, ffn1_w13=ffn1_w13, ffn1_w2=ffn1_w2, mix_norm=mix_norm, w_in=w_in, conv_w=conv_w,
             conv_b=conv_b, conv_ln_g=conv_ln_g, conv_ln_b=conv_ln_b, forget_b=forget_b, out_norm_conv=out_norm_conv,
             out_norm_attn=out_norm_attn, w_out=w_out, ffn2_norm=ffn2_norm, ffn2_w13=ffn2_w13, ffn2_w2=ffn2_w2,
             final_norm=final_norm)
    m = dict(ffn1_norm=m_ffn1_norm, ffn1_w13=m_ffn1_w13, ffn1_w2=m_ffn1_w2, mix_norm=m_mix_norm, w_in=m_w_in,
             conv_w=m_conv_w, conv_b=m_conv_b, conv_ln_g=m_conv_ln_g, conv_ln_b=m_conv_ln_b, forget_b=m_forget_b,
             out_norm_conv=m_out_norm_conv, out_norm_attn=m_out_norm_attn, w_out=m_w_out, ffn2_norm=m_ffn2_norm,
             ffn2_w13=m_ffn2_w13, ffn2_w2=m_ffn2_w2, final_norm=m_final_norm)
    v = dict(ffn1_norm=v_ffn1_norm, ffn1_w13=v_ffn1_w13, ffn1_w2=v_ffn1_w2, mix_norm=v_mix_norm, w_in=v_w_in,
             conv_w=v_conv_w, conv_b=v_conv_b, conv_ln_g=v_conv_ln_g, conv_ln_b=v_conv_ln_b, forget_b=v_forget_b,
             out_norm_conv=v_out_norm_conv, out_norm_attn=v_out_norm_attn, w_out=v_w_out, ffn2_norm=v_ffn2_norm,
             ffn2_w13=v_ffn2_w13, ffn2_w2=v_ffn2_w2, final_norm=v_final_norm)
    shapes = {n: a.shape for n, a in w.items()}
    T, D = x.shape[1], x.shape[2]
    def two(n, a):
        if a.ndim != 3:
            return a.reshape(1, -1)
        a = a.reshape(a.shape[-2], a.shape[-1])
        return a.T if n in _TRANSPOSED else a

    w2d = {n: two(n, a) for n, a in w.items()}
    m2d = {n: two(n, a) for n, a in m.items()}
    v2d = {n: two(n, a) for n, a in v.items()}

    shard = {n: w2d[n].astype(MXU) for n in _BIG}
    shard["conv_w"] = w2d["conv_w"]
    norms = {n: w2d[n] for n in _SMALL}
    norms["forget_b"] = jnp.pad(w2d["forget_b"], ((0, 0), (0, LANES - N_HEADS)))
    dx, gathered_small, big = _local_step(x[0], loss_target[0], norms, shard)

    grads, deltas, new_m, new_v = {}, {}, {}, {}
    for n in _BIG:
        g, d, nm, nv = _adamw(w2d[n], m2d[n], v2d[n], big[n], "adamw_" + n)
        grads[n], deltas[n], new_m[n], new_v[n] = g, d, nm, nv

    small_out, conv_g_full, loss = _adamw_small(gathered_small, w2d, m2d, v2d, "adamw_small")
    for n in _SMALL:
        grads[n], deltas[n], new_m[n], new_v[n] = small_out[n]
    conv_g_full = conv_g_full[:CONV_TAPS]
    xi, yi, ci = _position()
    cw = shapes["conv_w"][-1]
    conv_g_mine = lax.dynamic_slice_in_dim(conv_g_full, _flat(xi, yi, ci) * cw, cw, axis=1)
    g, d, nm, nv = _adamw(w2d["conv_w"], m2d["conv_w"], v2d["conv_w"], conv_g_mine[None], "adamw_conv_w")
    grads["conv_w"], deltas["conv_w"], new_m["conv_w"], new_v["conv_w"] = g, d, nm, nv

    shaped = lambda dct: [(dct[n].T if n in _TRANSPOSED else dct[n]).reshape(shapes[n]) for n in _ORDER]
    return (loss[0, 0], dx[None], *shaped(grads), *shaped(deltas), *shaped(new_m), *shaped(new_v))
```

```python
from typing import NamedTuple

import jax
import jax.numpy as jnp
from jax import lax
from jax.experimental import pallas as pl
from jax.experimental.pallas import tpu as pltpu
from jax.experimental.pallas import tpu_sc as plsc

F32 = jnp.float32
MXU = jnp.bfloat16
EPS = 1e-6
N_HEADS = 8
HEAD_DIM = 64
D_CONV = 512
D_ATTN = N_HEADS * HEAD_DIM
CONV_TAPS = 31
CONV_HALO = 32
SCALE = HEAD_DIM ** -0.5
NEG = -1e30
LANES = 128
N_DEV = 8
N_CHIPS = N_DEV // 2
MESH = pl.DeviceIdType.MESH
MIB = 1 << 20

ADAM_LR = 0.001
ADAM_B1 = 0.9
ADAM_B2 = 0.999
ADAM_EPS = 1e-08
ADAM_WD = 0.01
ADAM_STEP = 10


_UNREAD = pl.BlockSpec(memory_space=pl.ANY)


def _pallas_call(body, *, out_shape, **kwargs):
    in_hbm = lambda s: pltpu.HBM(s.shape, s.dtype)
    outs = [in_hbm(s) for s in out_shape] if isinstance(out_shape, (list, tuple)) else in_hbm(out_shape)
    call = pl.pallas_call(body, out_shape=outs, **kwargs)
    return lambda *operands: call(*[pltpu.with_memory_space_constraint(a, pltpu.HBM) for a in operands])


def _params(vmem_mib, n_axes):
    return pltpu.CompilerParams(dimension_semantics=("arbitrary",) * n_axes, vmem_limit_bytes=vmem_mib * MIB)


def _mm(a, b):
    return jnp.dot(a, b, preferred_element_type=F32)


def _mm_nt(a, b):
    return lax.dot_general(a, b, (((1,), (1,)), ((), ())), preferred_element_type=F32)


def _mm_tn(a, b):
    return lax.dot_general(a, b, (((0,), (0,)), ((), ())), preferred_element_type=F32)


def _rms_fwd(x, g):
    r = lax.rsqrt(jnp.mean(x * x, axis=-1, keepdims=True) + EPS)
    return x * r * g, r


def _rms_bwd(x, r, g, dy):
    gdy = dy * g
    dx = r * gdy - x * (r * r * r) * jnp.mean(x * gdy, axis=-1, keepdims=True)
    dg = jnp.sum(dy * x * r, axis=0, keepdims=True)
    return dx, dg


def _silu_grad(z, sz):
    return sz * (1.0 + z * (1.0 - sz))


def _three_terms(x):
    x1 = x.astype(jnp.bfloat16)
    r1 = x - x1.astype(F32)
    x2 = r1.astype(jnp.bfloat16)
    x3 = (r1 - x2.astype(F32)).astype(jnp.bfloat16)
    return x1, x2, x3


def _exact_tri_dot(tri, x):
    x1, x2, x3 = _three_terms(x)
    return _mm(tri, x1) + _mm(tri, x2) + _mm(tri, x3)


def _exact_dot_01(x, sel):
    x1, x2, x3 = _three_terms(x)
    return _mm(x1, sel) + _mm(x2, sel) + _mm(x3, sel)


def _tile(n, want):
    t = min(n, want)
    assert n % t == 0
    return t


_FFN_CHUNK = 256


def _ffn_up(x, g, w13, name):
    T, D = x.shape
    _, J, bf, _ = w13.shape
    tm = _tile(T, 1024)
    I = T // tm

    def body(x_ref, g_ref, w13_ref, G_ref, U_ref, A_ref, h_s):
        j = pl.program_id(0)
        i = pl.program_id(1)
        rows = pl.ds(pl.multiple_of(i * tm, tm), tm)

        @pl.when(j == 0)
        def _():
            h, _ = _rms_fwd(x_ref[...], g_ref[...])
            h_s[rows, :] = h.astype(MXU)

        chunks = [slice(r0, r0 + _FFN_CHUNK) for r0 in range(0, tm, _FFN_CHUNK)]
        hbs = [h_s[pl.ds(pl.multiple_of(i * tm + rs.start, _FFN_CHUNK), _FFN_CHUNK), :] for rs in chunks]
        GU = [(_mm_nt(hb, w13_ref[0]), _mm_nt(hb, w13_ref[1])) for hb in hbs]
        for rs, (G, U) in zip(chunks, GU):
            G_ref[rs, :] = G.astype(MXU)
            U_ref[rs, :] = U.astype(MXU)
            A_ref[rs, :] = (G * jax.nn.sigmoid(G) * U).astype(MXU)

    blk = pl.BlockSpec((None, tm, bf), lambda j, i: (j, i, 0))
    hid = jax.ShapeDtypeStruct((J, T, bf), MXU)
    return _pallas_call(
        body, name=name, grid=(J, I),
        in_specs=[pl.BlockSpec((tm, D), lambda j, i: (jnp.where(j == 0, i, I - 1), 0)),
                  pl.BlockSpec((1, D), lambda j, i: (0, 0)),
                  pl.BlockSpec((2, None, bf, D), lambda j, i: (0, j, 0, 0))],
        out_specs=[blk, blk, blk],
        out_shape=[hid, hid, hid],
        scratch_shapes=[pltpu.VMEM((T, D), MXU)],
        compiler_params=_params(48, 2),
    )(x, g, w13)


def _ffn_down(x, A, w2, name):
    T, D = x.shape
    J, _, bf = A.shape
    tm = _tile(T, 512)

    def body(x_ref, A_ref, w2_ref, xo_ref):
        f = _mm(A_ref[0], w2_ref[0:bf, :])
        for j in range(1, J):
            f = f + _mm(A_ref[j], w2_ref[j * bf:(j + 1) * bf, :])
        xo_ref[...] = x_ref[...] + 0.5 * f

    row = pl.BlockSpec((tm, D), lambda i: (i, 0))
    return _pallas_call(
        body, name=name, grid=(T // tm,),
        in_specs=[row, pl.BlockSpec((J, tm, bf), lambda i: (0, i, 0)), pl.BlockSpec((J * bf, D), lambda i: (0, 0))],
        out_specs=row,
        out_shape=jax.ShapeDtypeStruct((T, D), F32),
        compiler_params=_params(48, 1),
    )(x, A, w2)


def _ffn_bwd_act(x, g, dy, Gs, Us, w13, w2, name, after=()):
    T, D = x.shape
    _, J, bf, _ = w13.shape
    tm = _tile(T, 512)
    I = T // tm

    def body(x_ref, g_ref, dy_ref, G_ref, U_ref, w13_ref, w2_ref, *rest):
        dx_ref, dg_ref, h_ref, dG_ref, dU_ref, dh_s, dF_s, h_s = rest[len(after):]
        j = pl.program_id(0)
        i = pl.program_id(1)
        rows = pl.ds(pl.multiple_of(i * tm, tm), tm)

        @pl.when(j == 0)
        def _():
            h, _ = _rms_fwd(x_ref[...], g_ref[...])
            hb = h.astype(MXU)
            h_s[rows, :] = hb
            h_ref[...] = hb
            dF_s[rows, :] = (0.5 * dy_ref[...]).astype(MXU)
            dh_s[rows, :] = jnp.zeros((tm, D), F32)

        chunks = [slice(r0, r0 + _FFN_CHUNK) for r0 in range(0, tm, _FFN_CHUNK)]
        crows = [pl.ds(pl.multiple_of(i * tm + rs.start, _FFN_CHUNK), _FFN_CHUNK) for rs in chunks]
        dAs = [_mm_nt(dF_s[cr, :], w2_ref[...]) for cr in crows]
        for rs, cr, dA in zip(chunks, crows, dAs):
            G = G_ref[rs, :].astype(F32)
            U = U_ref[rs, :].astype(F32)
            sg = jax.nn.sigmoid(G)
            s = G * sg
            dUb = (dA * s).astype(MXU)
            dGb = (dA * U * _silu_grad(G, sg)).astype(MXU)
            dG_ref[rs, :] = dGb
            dU_ref[rs, :] = dUb
            dh_s[cr, :] += _mm(dGb, w13_ref[0]) + _mm(dUb, w13_ref[1])

        @pl.when(j == J - 1)
        def _():
            xv = x_ref[...]
            gv = g_ref[...]
            _, r = _rms_fwd(xv, gv)
            dxn, dgp = _rms_bwd(xv, r, gv, dh_s[rows, :])
            dx_ref[...] = dy_ref[...] + dxn

            @pl.when(i == 0)
            def _():
                dg_ref[...] = dgp

            @pl.when(i > 0)
            def _():
                dg_ref[...] += dgp

    ends = lambda j, i: (jnp.where((j == 0) | (j == J - 1), i, I - 1), 0)
    blk = pl.BlockSpec((None, tm, bf), lambda j, i: (j, i, 0))
    hid = jax.ShapeDtypeStruct((J, T, bf), MXU)
    return _pallas_call(
        body, name=name, grid=(J, I),
        in_specs=[pl.BlockSpec((tm, D), ends), pl.BlockSpec((1, D), lambda j, i: (0, 0)), pl.BlockSpec((tm, D), ends),
                  blk, blk, pl.BlockSpec((2, None, bf, D), lambda j, i: (0, j, 0, 0)),
                  pl.BlockSpec((bf, D), lambda j, i: (j, 0))] + [_UNREAD] * len(after),
        out_specs=[pl.BlockSpec((tm, D), lambda j, i: (jnp.where(j == J - 1, i, 0), 0)),
                   pl.BlockSpec((1, D), lambda j, i: (0, 0)),
                   pl.BlockSpec((tm, D), lambda j, i: (jnp.where(j == 0, i, I - 1), 0)), blk, blk],
        out_shape=[jax.ShapeDtypeStruct((T, D), F32), jax.ShapeDtypeStruct((1, D), F32),
                   jax.ShapeDtypeStruct((T, D), MXU), hid, hid],
        scratch_shapes=[pltpu.VMEM((T, D), F32), pltpu.VMEM((T, D), MXU), pltpu.VMEM((T, D), MXU)],
        compiler_params=_params(58, 2),
    )(x, g, dy, Gs, Us, w13, w2, *after)


def _ffn_w13_grad(h, dG, dU, name):
    T, D = h.shape
    J, _, bf = dG.shape

    def body(h_ref, dG_ref, dU_ref, dw13_ref):
        dw13_ref[0] = _mm_tn(dG_ref[...], h_ref[...]).astype(dw13_ref.dtype)
        dw13_ref[1] = _mm_tn(dU_ref[...], h_ref[...]).astype(dw13_ref.dtype)

    blk = pl.BlockSpec((None, T, bf), lambda j: (j, 0, 0))
    return _pallas_call(
        body, name=name, grid=(J,),
        in_specs=[pl.BlockSpec((T, D), lambda j: (0, 0)), blk, blk],
        out_specs=pl.BlockSpec((2, None, bf, D), lambda j: (0, j, 0, 0)),
        out_shape=jax.ShapeDtypeStruct((2, J, bf, D), MXU),
        compiler_params=_params(48, 1),
    )(h, dG, dU)


def _ffn_w2_grad(dy, A, name, after=()):
    T, D = dy.shape
    J, _, bf = A.shape

    def body(dy_ref, A_ref, *rest):
        dw2_ref, dF_s = rest[len(after):]

        @pl.when(pl.program_id(0) == 0)
        def _():
            dF_s[...] = (0.5 * dy_ref[...]).astype(MXU)

        dw2_ref[...] = _mm_tn(A_ref[...], dF_s[...]).astype(dw2_ref.dtype)

    return _pallas_call(
        body, name=name, grid=(J,),
        in_specs=[pl.BlockSpec((T, D), lambda j: (0, 0)), pl.BlockSpec((None, T, bf), lambda j: (j, 0, 0))]
        + [_UNREAD] * len(after),
        out_specs=pl.BlockSpec((bf, D), lambda j: (j, 0)),
        out_shape=jax.ShapeDtypeStruct((J * bf, D), MXU),
        scratch_shapes=[pltpu.VMEM((T, D), MXU)],
        compiler_params=_params(48, 1),
    )(dy, A, *after)


_AG0, _Q0, _K0, _V0, _F0 = 0, 2 * D_CONV, 2 * D_CONV + D_ATTN, 2 * D_CONV + 2 * D_ATTN, 2 * D_CONV + 3 * D_ATTN
N_IN = _F0 + N_HEADS
N_IN_PAD = _F0 + LANES
_IN_BLOCK = N_IN // N_DEV


def _rows_from_blocks(blocks_ref, rows_ref):
    for p in range(N_DEV):
        rows_ref[_IN_BLOCK * p:_IN_BLOCK * (p + 1), :] = blocks_ref[p]
    rows_ref[N_IN:, :] = jnp.zeros((N_IN_PAD - N_IN, rows_ref.shape[1]), rows_ref.dtype)


def _inproj_fwd(x1, gm, win_blocks, name):
    T, D = x1.shape
    tm = _tile(T, 512)

    def body(x_ref, g_ref, wb_ref, ag_ref, k_ref, v_ref, qT_ref, kT_ref, vT_ref, fl_ref, w_ref):
        @pl.when(pl.program_id(0) == 0)
        def _():
            _rows_from_blocks(wb_ref, w_ref)

        h, _ = _rms_fwd(x_ref[...], g_ref[...])
        hb = h.astype(MXU)
        ag_ref[...] = _mm_nt(hb, w_ref[_AG0:_Q0, :])
        qT_ref[...] = (_mm_nt(hb, w_ref[_Q0:_K0, :]) * SCALE).T.astype(MXU)
        for c0, ref, refT in ((_K0, k_ref, kT_ref), (_V0, v_ref, vT_ref)):
            y = _mm_nt(hb, w_ref[c0:c0 + D_ATTN, :])
            ref[...] = y.astype(MXU)
            refT[...] = y.T.astype(MXU)
        fl_ref[...] = _mm_nt(hb, w_ref[_F0:N_IN_PAD, :])

    row = lambda w: pl.BlockSpec((tm, w), lambda i: (i, 0))
    col = pl.BlockSpec((D_ATTN, tm), lambda i: (0, i))
    std = jax.ShapeDtypeStruct((T, D_ATTN), MXU)
    trn = jax.ShapeDtypeStruct((D_ATTN, T), MXU)
    return _pallas_call(
        body, name=name, grid=(T // tm,),
        in_specs=[row(D), pl.BlockSpec((1, D), lambda i: (0, 0)),
                  pl.BlockSpec((N_DEV, _IN_BLOCK, D), lambda i: (0, 0, 0))],
        out_specs=[row(2 * D_CONV), row(D_ATTN), row(D_ATTN), col, col, col, row(LANES)],
        out_shape=[jax.ShapeDtypeStruct((T, 2 * D_CONV), F32), std, std, trn, trn, trn,
                   jax.ShapeDtypeStruct((T, LANES), F32)],
        scratch_shapes=[pltpu.VMEM((N_IN_PAD, D), MXU)],
        compiler_params=_params(40, 1),
    )(x1, gm, win_blocks)


def _inproj_bwd_act(x1, gm, dx2, dag, dqT, dkT, dvT, dfl, win_blocks, name):
    T, D = x1.shape
    tm = _tile(T, 512)

    def body(x_ref, g_ref, dx2_ref, dag_ref, dqT_ref, dkT_ref, dvT_ref, dfl_ref, wb_ref, dx1_ref, dg_ref, h_ref,
             w_ref):
        i = pl.program_id(0)

        @pl.when(i == 0)
        def _():
            _rows_from_blocks(wb_ref, w_ref)

        xv = x_ref[...]
        gv = g_ref[...]
        h, r = _rms_fwd(xv, gv)
        h_ref[...] = h.astype(MXU)
        dh = _mm(dag_ref[...], w_ref[_AG0:_Q0, :])
        for c0, ref in ((_Q0, dqT_ref), (_K0, dkT_ref), (_V0, dvT_ref)):
            dh = dh + _mm_tn(ref[...].astype(MXU), w_ref[c0:c0 + D_ATTN, :])
        dh = dh + _mm(dfl_ref[...].astype(MXU), w_ref[_F0:N_IN_PAD, :])
        dxn, dgp = _rms_bwd(xv, r, gv, dh)
        dx1_ref[...] = dx2_ref[...] + dxn

        @pl.when(i == 0)
        def _():
            dg_ref[...] = dgp

        @pl.when(i > 0)
        def _():
            dg_ref[...] += dgp

    row = lambda w: pl.BlockSpec((tm, w), lambda i: (i, 0))
    col = pl.BlockSpec((D_ATTN, tm), lambda i: (0, i))
    full = lambda a, b: pl.BlockSpec((a, b), lambda i: (0, 0))
    return _pallas_call(
        body, name=name, grid=(T // tm,),
        in_specs=[row(D), full(1, D), row(D), row(2 * D_CONV), col, col, col, row(LANES),
                  pl.BlockSpec((N_DEV, _IN_BLOCK, D), lambda i: (0, 0, 0))],
        out_specs=[row(D), full(1, D), row(D)],
        out_shape=[jax.ShapeDtypeStruct((T, D), F32), jax.ShapeDtypeStruct((1, D), F32),
                   jax.ShapeDtypeStruct((T, D), MXU)],
        scratch_shapes=[pltpu.VMEM((N_IN_PAD, D), MXU)],
        compiler_params=_params(40, 1),
    )(x1, gm, dx2, dag, dqT, dkT, dvT, dfl, win_blocks)


def _inproj_bwd_weights(h, dag, dqT, dkT, dvT, dfl, name, after=()):
    T, D = h.shape
    operands = (h, dag, dqT, dkT, dvT, dfl)
    n = len(operands)

    def body(*refs):
        sources = refs[:n]
        blocks_ref = refs[n + len(after)]
        h_v, dag_v, dqT_v, dkT_v, dvT_v, dfl_v = buffers = refs[n + len(after) + 1:2 * n + len(after) + 1]
        dw_ref, sems = refs[2 * n + len(after) + 1:]
        copies = [pltpu.make_async_copy(src, dst, sems.at[k]) for k, (src, dst) in enumerate(zip(sources, buffers))]
        for cp in copies:
            cp.start()
        copies[0].wait()
        hb = h_v[...]
        copies[1].wait()
        dw_ref[_AG0:_Q0, :] = _mm_tn(dag_v[...], hb).astype(dw_ref.dtype)
        for k, (c0, ref) in enumerate(((_Q0, dqT_v), (_K0, dkT_v), (_V0, dvT_v))):
            copies[2 + k].wait()
            dw_ref[c0:c0 + D_ATTN, :] = _mm(ref[...].astype(MXU), hb).astype(dw_ref.dtype)
        copies[5].wait()
        dw_ref[_F0:N_IN_PAD, :] = _mm_tn(dfl_v[...].astype(MXU), hb).astype(dw_ref.dtype)
        for p in range(N_DEV):
            blocks_ref[p] = dw_ref[_IN_BLOCK * p:_IN_BLOCK * (p + 1), :]

    return _pallas_call(
        body, name=name, in_specs=[_UNREAD] * (n + len(after)), out_specs=pl.BlockSpec(memory_space=pltpu.VMEM),
        out_shape=jax.ShapeDtypeStruct((N_DEV, _IN_BLOCK, D), MXU),
        scratch_shapes=[pltpu.VMEM(a.shape, a.dtype) for a in operands]
        + [pltpu.VMEM((N_IN_PAD, D), MXU), pltpu.SemaphoreType.DMA((n,))],
        compiler_params=pltpu.CompilerParams(vmem_limit_bytes=56 * MIB),
    )(*operands, *after)


def _forget_fwd(fl, fbp, name):
    T = fl.shape[0]
    tb = _tile(T, 256)

    def body(fl_ref, fb_ref, cum_ref, cumT_ref):
        ri = lax.broadcasted_iota(jnp.int32, (tb, tb), 0)
        ci = lax.broadcasted_iota(jnp.int32, (tb, tb), 1)
        tri = (ri >= ci).astype(jnp.bfloat16)
        carry = jnp.zeros((1, LANES), F32)
        for b in range(T // tb):
            z = fl_ref[b * tb:(b + 1) * tb, :] + fb_ref[...]
            lf = jnp.minimum(z, 0.0) - jnp.log1p(jnp.exp(-jnp.abs(z)))
            c = _exact_tri_dot(tri, lf) + carry
            cum_ref[b * tb:(b + 1) * tb, :] = c
            carry = c[tb - 1:tb, :]
        cumT_ref[...] = cum_ref[...].T[:N_HEADS, :]

    return _pallas_call(
        body, name=name,
        out_shape=[jax.ShapeDtypeStruct((T, LANES), F32), jax.ShapeDtypeStruct((N_HEADS, T), F32)],
        compiler_params=pltpu.CompilerParams(vmem_limit_bytes=32 * MIB),
    )(fl, fbp)


def _forget_bwd(dcum, fl, fbp, name):
    T = fl.shape[0]
    tb = _tile(T, 256)

    def body(dc_ref, fl_ref, fb_ref, dfl_ref, dfb_ref):
        ri = lax.broadcasted_iota(jnp.int32, (tb, tb), 0)
        ci = lax.broadcasted_iota(jnp.int32, (tb, tb), 1)
        tri = (ri <= ci).astype(jnp.bfloat16)
        carry = jnp.zeros((1, LANES), F32)
        dfb = jnp.zeros((1, LANES), F32)
        for b in reversed(range(T // tb)):
            sl = slice(b * tb, (b + 1) * tb)
            dl = _exact_tri_dot(tri, dc_ref[sl, :]) + carry
            carry = dl[0:1, :]
            z = fl_ref[sl, :] + fb_ref[...]
            dfl = dl * jax.nn.sigmoid(-z)
            dfl_ref[sl, :] = dfl
            dfb = dfb + jnp.sum(dfl, axis=0, keepdims=True)
        dfb_ref[...] = dfb

    return _pallas_call(
        body, name=name,
        out_shape=[jax.ShapeDtypeStruct((T, LANES), F32), jax.ShapeDtypeStruct((1, LANES), F32)],
        compiler_params=pltpu.CompilerParams(vmem_limit_bytes=32 * MIB),
    )(dcum, fl, fbp)


def _causal_keep(i, j, tq, tk):
    key = j * tk + lax.broadcasted_iota(jnp.int32, (tk, tq), 0)
    qry = i * tq + lax.broadcasted_iota(jnp.int32, (tk, tq), 1)
    return key <= qry


def _split_hi_lo(x):
    hi = x.astype(MXU)
    lo = (x - hi.astype(F32)).astype(MXU)
    return hi, lo


def _attn_fwd(qT, k, vT, cum, cumT, name):
    T = k.shape[0]
    tq = _tile(T, 256)
    tk = _tile(tq, 256)
    kpq = tq // tk
    heads = [slice(HEAD_DIM * h, HEAD_DIM * (h + 1)) for h in range(N_HEADS)]

    def body(qT_ref, k_ref, vT_ref, cum_ref, cumT_ref, o_ref, lseT_ref, acc_s, m_s, l_s):
        i = pl.program_id(0)
        acc_s[...] = jnp.zeros_like(acc_s)
        m_s[...] = jnp.full_like(m_s, NEG)
        l_s[...] = jnp.zeros_like(l_s)

        def kblock(j, masked):
            rows = pl.ds(pl.multiple_of(j * tk, tk), tk)
            keep = _causal_keep(i, j, tq, tk) if masked else None
            bias = [cumT_ref[h:h + 1, :] - cum_ref[rows, h:h + 1] for h in range(N_HEADS)]
            qk = [_mm(k_ref[rows, hs], qT_ref[hs, :]) + bias[h] for h, hs in enumerate(heads)]
            for h, hs in enumerate(heads):
                sT = qk[h]
                if masked:
                    sT = jnp.where(keep, sT, NEG)
                m_old = m_s[h:h + 1, :]
                m_new = jnp.maximum(m_old, jnp.max(sT, axis=0, keepdims=True))
                alpha = jnp.exp(m_old - m_new)
                pT = jnp.exp(sT - m_new)
                l_s[h:h + 1, :] = alpha * l_s[h:h + 1, :] + jnp.sum(pT, axis=0, keepdims=True)
                p_hi, p_lo = _split_hi_lo(pT)
                vh = vT_ref[hs, rows]
                acc_s[hs, :] = alpha * acc_s[hs, :] + (_mm(vh, p_hi) + _mm(vh, p_lo))
                m_s[h:h + 1, :] = m_new

        def unmasked(j, c):
            kblock(j, False)
            return c

        lax.fori_loop(0, kpq * i, unmasked, 0)
        for d in range(kpq):
            kblock(kpq * i + d, True)
        for h, hs in enumerate(heads):
            acc_s[hs, :] = acc_s[hs, :] / l_s[h:h + 1, :]
        o_ref[...] = acc_s[...].T
        lseT_ref[...] = m_s[...] + jnp.log(l_s[...])

    full = lambda a, b: pl.BlockSpec((a, b), lambda i: (0, 0))
    colblk = lambda r: pl.BlockSpec((r, tq), lambda i: (0, i))
    return _pallas_call(
        body, name=name, grid=(T // tq,),
        in_specs=[colblk(D_ATTN), full(T, D_ATTN), full(D_ATTN, T), full(T, LANES), colblk(N_HEADS)],
        out_specs=[pl.BlockSpec((tq, D_ATTN), lambda i: (i, 0)), colblk(N_HEADS)],
        out_shape=[jax.ShapeDtypeStruct((T, D_ATTN), F32), jax.ShapeDtypeStruct((N_HEADS, T), F32)],
        scratch_shapes=[pltpu.VMEM((D_ATTN, tq), F32), pltpu.VMEM((N_HEADS, tq), F32),
                        pltpu.VMEM((N_HEADS, tq), F32)],
        compiler_params=_params(40, 1),
    )(qT, k, vT, cum, cumT)


def _attn_bwd(qT, k, kT, v, doT, lseT, deltaT, cum, cumT, name, after=()):
    T = k.shape[0]
    tq = _tile(T, 256)
    tk = _tile(tq, 256)
    kpq = tq // tk
    heads = [slice(HEAD_DIM * h, HEAD_DIM * (h + 1)) for h in range(N_HEADS)]

    def body(qT_ref, k_ref, kT_ref, v_ref, doT_ref, lseT_ref, dlT_ref, cum_ref, cumT_ref, *rest):
        dq_ref, dk_ref, dv_ref, dcum_ref, dq_s = rest[len(after):]
        i = pl.program_id(0)

        @pl.when(i == 0)
        def _():
            dk_ref[...] = jnp.zeros_like(dk_ref)
            dv_ref[...] = jnp.zeros_like(dv_ref)
            dcum_ref[...] = jnp.zeros_like(dcum_ref)

        dq_s[...] = jnp.zeros_like(dq_s)

        def kblock(j, masked):
            rows = pl.ds(pl.multiple_of(j * tk, tk), tk)
            keep = _causal_keep(i, j, tq, tk) if masked else None
            bias = [cumT_ref[h:h + 1, :] - cum_ref[rows, h:h + 1] for h in range(N_HEADS)]
            qk = [_mm(k_ref[rows, hs], qT_ref[hs, :]) + bias[h] for h, hs in enumerate(heads)]
            dps = [_mm(v_ref[rows, hs], doT_ref[hs, :]) for hs in heads]
            for h, hs in enumerate(heads):
                sT = qk[h]
                if masked:
                    sT = jnp.where(keep, sT, NEG)
                pT = jnp.exp(sT - lseT_ref[h:h + 1, :])
                dsT = pT * (dps[h] - dlT_ref[h:h + 1, :])
                dcum_ref[rows, h:h + 1] += -jnp.sum(dsT, axis=1, keepdims=True)
                dsb = dsT.astype(MXU)
                dv_ref[hs, rows] += _mm_nt(doT_ref[hs, :], pT.astype(MXU))
                dk_ref[hs, rows] += _mm_nt(qT_ref[hs, :], dsb)
                dq_s[hs, :] += _mm(kT_ref[hs, rows], dsb)

        def unmasked(j, c):
            kblock(j, False)
            return c

        lax.fori_loop(0, kpq * i, unmasked, 0)
        for d in range(kpq):
            kblock(kpq * i + d, True)
        dq_ref[...] = (dq_s[...] * SCALE).astype(dq_ref.dtype)

    full = lambda a, b: pl.BlockSpec((a, b), lambda i: (0, 0))
    colblk = lambda r: pl.BlockSpec((r, tq), lambda i: (0, i))
    return _pallas_call(
        body, name=name, grid=(T // tq,),
        in_specs=[colblk(D_ATTN), full(T, D_ATTN), full(D_ATTN, T), full(T, D_ATTN), colblk(D_ATTN),
                  colblk(N_HEADS), colblk(N_HEADS), full(T, LANES), colblk(N_HEADS)] + [_UNREAD] * len(after),
        out_specs=[colblk(D_ATTN), full(D_ATTN, T), full(D_ATTN, T), full(T, LANES)],
        out_shape=[
            jax.ShapeDtypeStruct((D_ATTN, T), MXU),
            jax.ShapeDtypeStruct((D_ATTN, T), F32),
            jax.ShapeDtypeStruct((D_ATTN, T), F32),
            jax.ShapeDtypeStruct((T, LANES), F32),
        ],
        scratch_shapes=[pltpu.VMEM((D_ATTN, tq), F32)],
        compiler_params=_params(48, 1),
    )(qT, k, kT, v, doT, lseT, deltaT, cum, cumT, *after)


_ROWS_PER_CHUNK = 64


def _glu_halo(ag_ref, agh_ref, uext_s, tm, first):
    a = ag_ref[:, :D_CONV]
    sg = jax.nn.sigmoid(ag_ref[:, D_CONV:])
    uh = agh_ref[:, :D_CONV] * jax.nn.sigmoid(agh_ref[:, D_CONV:])
    uext_s[0:CONV_HALO, :] = jnp.where(first, 0.0, uh)
    uext_s[CONV_HALO:CONV_HALO + tm, :] = a * sg
    return a, sg


_SUBLANES = 8


def _shifted_copies(ext_s, sh_s, rows):
    for k in range(1, _SUBLANES):
        sh_s[k, 0:rows, :] = ext_s[pl.ds(k, rows), :]


def _window(ext_s, sh_s, start, rows):
    k = start % _SUBLANES
    if k == 0:
        return ext_s[pl.ds(start, rows), :]
    return sh_s[k, pl.ds(start - k, rows), :]


def _layer_norm_stats(y):
    mu = jnp.mean(y, axis=-1, keepdims=True)
    xc = y - mu
    rs = lax.rsqrt(jnp.mean(xc * xc, axis=-1, keepdims=True) + EPS)
    return xc * rs, rs


def _conv_fwd(ag, w32, cb, lg, lb, name):
    T = ag.shape[0]
    tm = _tile(T, 512)
    rc = _tile(tm, _ROWS_PER_CHUNK)
    hb = tm // CONV_HALO

    def body(ag_ref, agh_ref, w_ref, cb_ref, lg_ref, lb_ref, yc_ref, c_ref, uext_s, ush_s):
        i = pl.program_id(0)
        _glu_halo(ag_ref, agh_ref, uext_s, tm, i == 0)
        _shifted_copies(uext_s, ush_s, tm + CONV_HALO - _SUBLANES)
        for r0 in range(0, tm, rc):
            acc = jnp.zeros((rc, D_CONV), F32)
            for t in range(CONV_TAPS):
                acc = acc + _window(uext_s, ush_s, r0 + CONV_HALO - (CONV_TAPS - 1) + t, rc) * w_ref[t:t + 1, :]
            y = acc + cb_ref[...]
            yc_ref[r0:r0 + rc, :] = y
            n, _ = _layer_norm_stats(y)
            z = n * lg_ref[...] + lb_ref[...]
            c_ref[r0:r0 + rc, :] = z * jax.nn.sigmoid(z)

    row = lambda w: pl.BlockSpec((tm, w), lambda i: (i, 0))
    full = lambda a, b: pl.BlockSpec((a, b), lambda i: (0, 0))
    return _pallas_call(
        body, name=name, grid=(T // tm,),
        in_specs=[row(2 * D_CONV),
                  pl.BlockSpec((CONV_HALO, 2 * D_CONV), lambda i: (jnp.maximum(i * hb - 1, 0), 0)),
                  full(CONV_HALO, D_CONV), full(1, D_CONV), full(1, D_CONV), full(1, D_CONV)],
        out_specs=[row(D_CONV), row(D_CONV)],
        out_shape=[jax.ShapeDtypeStruct((T, D_CONV), F32), jax.ShapeDtypeStruct((T, D_CONV), F32)],
        scratch_shapes=[pltpu.VMEM((CONV_HALO + tm, D_CONV), F32),
                        pltpu.VMEM((_SUBLANES, CONV_HALO + tm, D_CONV), F32)],
        compiler_params=_params(32, 1),
    )(ag, ag, w32, cb, lg, lb)


def _conv_bwd(dc, yc, ag, w32, lg, lb, name):
    T = ag.shape[0]
    tm = _tile(T, 512)
    rc = _tile(tm, _ROWS_PER_CHUNK)
    I = T // tm
    hb = tm // CONV_HALO
    n_halo_blocks = T // CONV_HALO

    def body(dc_ref, yc_ref, dch_ref, ych_ref, ag_ref, agh_ref, w_ref, lg_ref, lb_ref,
             dag_ref, dw_ref, dcb_ref, dlg_ref, dlb_ref, uext_s, dext_s, ush_s, dsh_s):
        i = pl.program_id(0)
        lgv = lg_ref[...]
        lbv = lb_ref[...]

        def ln_bwd(dcv, ycv):
            n, rs = _layer_norm_stats(ycv)
            z = n * lgv + lbv
            dz = dcv * _silu_grad(z, jax.nn.sigmoid(z))
            dn = dz * lgv
            dy = rs * (dn - jnp.mean(dn, axis=-1, keepdims=True) - n * jnp.mean(dn * n, axis=-1, keepdims=True))
            return dy, dz, n

        dy, dz, n = ln_bwd(dc_ref[...], yc_ref[...])
        dyh, _, _ = ln_bwd(dch_ref[...], ych_ref[...])
        dext_s[0:tm, :] = dy
        dext_s[tm:tm + CONV_HALO, :] = jnp.where(i == I - 1, 0.0, dyh)
        a, sg = _glu_halo(ag_ref, agh_ref, uext_s, tm, i == 0)
        _shifted_copies(uext_s, ush_s, tm + CONV_HALO - _SUBLANES)
        _shifted_copies(dext_s, dsh_s, tm + CONV_HALO - _SUBLANES)

        @pl.when(i == 0)
        def _():
            dw_ref[...] = jnp.zeros_like(dw_ref)
            dcb_ref[...] = jnp.zeros_like(dcb_ref)
            dlg_ref[...] = jnp.zeros_like(dlg_ref)
            dlb_ref[...] = jnp.zeros_like(dlb_ref)

        dcb_ref[...] += jnp.sum(dy, axis=0, keepdims=True)
        dlg_ref[...] += jnp.sum(dz * n, axis=0, keepdims=True)
        dlb_ref[...] += jnp.sum(dz, axis=0, keepdims=True)
        for t in range(CONV_TAPS):
            u_t = _window(uext_s, ush_s, CONV_HALO - (CONV_TAPS - 1) + t, tm)
            dw_ref[t:t + 1, :] += jnp.sum(dy * u_t, axis=0, keepdims=True)
        for r0 in range(0, tm, rc):
            acc = jnp.zeros((rc, D_CONV), F32)
            for t in range(CONV_TAPS):
                acc = acc + _window(dext_s, dsh_s, r0 + (CONV_TAPS - 1) - t, rc) * w_ref[t:t + 1, :]
            a_c = a[r0:r0 + rc, :]
            sg_c = sg[r0:r0 + rc, :]
            dag_ref[r0:r0 + rc, :D_CONV] = (acc * sg_c).astype(dag_ref.dtype)
            dag_ref[r0:r0 + rc, D_CONV:] = (acc * a_c * sg_c * (1.0 - sg_c)).astype(dag_ref.dtype)

    row = lambda w: pl.BlockSpec((tm, w), lambda i: (i, 0))
    full = lambda a, b: pl.BlockSpec((a, b), lambda i: (0, 0))
    nxt = pl.BlockSpec((CONV_HALO, D_CONV), lambda i: (jnp.minimum((i + 1) * hb, n_halo_blocks - 1), 0))
    return _pallas_call(
        body, name=name, grid=(I,),
        in_specs=[row(D_CONV), row(D_CONV), nxt, nxt, row(2 * D_CONV),
                  pl.BlockSpec((CONV_HALO, 2 * D_CONV), lambda i: (jnp.maximum(i * hb - 1, 0), 0)),
                  full(CONV_HALO, D_CONV), full(1, D_CONV), full(1, D_CONV)],
        out_specs=[row(2 * D_CONV), full(CONV_HALO, D_CONV), full(1, D_CONV), full(1, D_CONV), full(1, D_CONV)],
        out_shape=[
            jax.ShapeDtypeStruct((T, 2 * D_CONV), MXU),
            jax.ShapeDtypeStruct((CONV_HALO, D_CONV), F32),
            jax.ShapeDtypeStruct((1, D_CONV), F32),
            jax.ShapeDtypeStruct((1, D_CONV), F32),
            jax.ShapeDtypeStruct((1, D_CONV), F32),
        ],
        scratch_shapes=[pltpu.VMEM((CONV_HALO + tm, D_CONV), F32), pltpu.VMEM((tm + CONV_HALO, D_CONV), F32),
                        pltpu.VMEM((_SUBLANES, CONV_HALO + tm, D_CONV), F32),
                        pltpu.VMEM((_SUBLANES, CONV_HALO + tm, D_CONV), F32)],
        compiler_params=_params(40, 1),
    )(dc, yc, dc, yc, ag, ag, w32, lg, lb)


def _outproj_fwd(x1, c, o, gc, ga, wout, name):
    T, D = x1.shape
    tm = _tile(T, 512)

    def body(x_ref, c_ref, o_ref, gc_ref, ga_ref, w_ref, x2_ref):
        yc, _ = _rms_fwd(c_ref[...], gc_ref[...])
        ya, _ = _rms_fwd(o_ref[...], ga_ref[...])
        x2_ref[...] = (x_ref[...] + _mm(yc.astype(MXU), w_ref[:D_CONV, :])
                       + _mm(ya.astype(MXU), w_ref[D_CONV:, :]))

    row = lambda w: pl.BlockSpec((tm, w), lambda i: (i, 0))
    full = lambda a, b: pl.BlockSpec((a, b), lambda i: (0, 0))
    return _pallas_call(
        body, name=name, grid=(T // tm,),
        in_specs=[row(D), row(D_CONV), row(D_ATTN), full(1, D_CONV), full(1, D_ATTN), full(D_CONV + D_ATTN, D)],
        out_specs=row(D),
        out_shape=jax.ShapeDtypeStruct((T, D), F32),
        compiler_params=_params(32, 1),
    )(x1, c, o, gc, ga, wout)


def _outproj_bwd(dx2, c, o, gc, ga, wout, name):
    T, D = dx2.shape
    tm = _tile(T, 512)
    I = T // tm

    def body(dx_ref, c_ref, o_ref, gc_ref, ga_ref, w_ref,
             dc_ref, doT_ref, dlT_ref, dw_ref, dgc_ref, dga_ref, acc_s):
        i = pl.program_id(0)
        dxb = dx_ref[...].astype(MXU)
        cv = c_ref[...]
        ov = o_ref[...]
        yc, rcn = _rms_fwd(cv, gc_ref[...])
        ya, ra = _rms_fwd(ov, ga_ref[...])
        dyc = _mm_nt(dxb, w_ref[:D_CONV, :])
        dya = _mm_nt(dxb, w_ref[D_CONV:, :])
        dwc = _mm_tn(yc.astype(MXU), dxb)
        dwa = _mm_tn(ya.astype(MXU), dxb)
        dcv, dgc = _rms_bwd(cv, rcn, gc_ref[...], dyc)
        dov, dga = _rms_bwd(ov, ra, ga_ref[...], dya)
        dc_ref[...] = dcv
        dob = dov.astype(doT_ref.dtype)
        doT_ref[...] = dov.T.astype(doT_ref.dtype)
        chan = lax.broadcasted_iota(jnp.int32, (D_ATTN, LANES), 0)
        head = lax.broadcasted_iota(jnp.int32, (D_ATTN, LANES), 1)
        in_head = ((chan >= head * HEAD_DIM) & (chan < (head + 1) * HEAD_DIM)).astype(jnp.bfloat16)
        dlT_ref[...] = _exact_dot_01(dob.astype(F32) * ov, in_head).T[:N_HEADS, :]

        @pl.when(i == 0)
        def _():
            acc_s[:D_CONV, :] = dwc
            acc_s[D_CONV:, :] = dwa
            dgc_ref[...] = dgc
            dga_ref[...] = dga

        @pl.when(i > 0)
        def _():
            acc_s[:D_CONV, :] += dwc
            acc_s[D_CONV:, :] += dwa
            dgc_ref[...] += dgc
            dga_ref[...] += dga

        @pl.when(i == I - 1)
        def _():
            dw_ref[...] = acc_s[...].astype(dw_ref.dtype)

    row = lambda w: pl.BlockSpec((tm, w), lambda i: (i, 0))
    full = lambda a, b: pl.BlockSpec((a, b), lambda i: (0, 0))
    return _pallas_call(
        body, name=name, grid=(I,),
        in_specs=[row(D), row(D_CONV), row(D_ATTN), full(1, D_CONV), full(1, D_ATTN), full(D_CONV + D_ATTN, D)],
        out_specs=[row(D_CONV), pl.BlockSpec((D_ATTN, tm), lambda i: (0, i)),
                   pl.BlockSpec((N_HEADS, tm), lambda i: (0, i)),
                   full(D_CONV + D_ATTN, D), full(1, D_CONV), full(1, D_ATTN)],
        out_shape=[
            jax.ShapeDtypeStruct((T, D_CONV), F32),
            jax.ShapeDtypeStruct((D_ATTN, T), MXU),
            jax.ShapeDtypeStruct((N_HEADS, T), F32),
            jax.ShapeDtypeStruct((D_CONV + D_ATTN, D), MXU),
            jax.ShapeDtypeStruct((1, D_CONV), F32),
            jax.ShapeDtypeStruct((1, D_ATTN), F32),
        ],
        scratch_shapes=[pltpu.VMEM((D_CONV + D_ATTN, D), F32)],
        compiler_params=_params(40, 1),
    )(dx2, c, o, gc, ga, wout)


def _ffn_down_loss(x, A, w2, gf, target, name):
    T, D = x.shape
    J, _, bf = A.shape
    tm = _tile(T, 512)

    def body(x_ref, A_ref, w2_ref, g_ref, t_ref, loss_ref, dx_ref, dg_ref):
        i = pl.program_id(0)
        f = _mm(A_ref[0], w2_ref[0:bf, :])
        for j in range(1, J):
            f = f + _mm(A_ref[j], w2_ref[j * bf:(j + 1) * bf, :])
        xv = x_ref[...] + 0.5 * f
        gv = g_ref[...]
        out, r = _rms_fwd(xv, gv)
        err = out - t_ref[...]
        part = jnp.full((1, LANES), 0.5 / D, F32) * jnp.sum(err * err)
        dxn, dgp = _rms_bwd(xv, r, gv, err * (1.0 / D))
        dx_ref[...] = dxn

        @pl.when(i == 0)
        def _():
            loss_ref[...] = part
            dg_ref[...] = dgp

        @pl.when(i > 0)
        def _():
            loss_ref[...] += part
            dg_ref[...] += dgp

    row = lambda w: pl.BlockSpec((tm, w), lambda i: (i, 0))
    full = lambda a, b: pl.BlockSpec((a, b), lambda i: (0, 0))
    return _pallas_call(
        body, name=name, grid=(T // tm,),
        in_specs=[row(D), pl.BlockSpec((J, tm, bf), lambda i: (0, i, 0)), full(J * bf, D), full(1, D), row(D)],
        out_specs=[full(1, LANES), row(D), full(1, D)],
        out_shape=[jax.ShapeDtypeStruct((1, LANES), F32), jax.ShapeDtypeStruct((T, D), F32),
                   jax.ShapeDtypeStruct((1, D), F32)],
        compiler_params=_params(56, 1),
    )(x, A, w2, gf, target)


def _row_tile(rows):
    for cand in (256, 176, 128, 64, 32, 16):
        if rows % cand == 0:
            return cand
    return rows


def _adamw(w, m, v, parts, name):
    R, C = w.shape
    P = parts.shape[0]
    tr = _row_tile(R)
    c1 = 1.0 - ADAM_B1 ** ADAM_STEP
    c2 = 1.0 - ADAM_B2 ** ADAM_STEP

    def body(w_ref, m_ref, v_ref, p_ref, g_ref, d_ref, nm_ref, nv_ref):
        g = p_ref[0].astype(F32)
        for s in range(1, P):
            g = g + p_ref[s].astype(F32)
        wv = w_ref[...]
        mn = ADAM_B1 * m_ref[...] + (1.0 - ADAM_B1) * g
        vn = ADAM_B2 * v_ref[...] + (1.0 - ADAM_B2) * (g * g)
        g_ref[...] = g
        nm_ref[...] = mn
        nv_ref[...] = vn
        d_ref[...] = -ADAM_LR * ((mn / c1) / (jnp.sqrt(vn / c2) + ADAM_EPS) + ADAM_WD * wv)

    blk = pl.BlockSpec((tr, C), lambda i: (i, 0))
    out = jax.ShapeDtypeStruct((R, C), F32)
    return _pallas_call(
        body, name=name, grid=(R // tr,),
        in_specs=[blk, blk, blk, pl.BlockSpec((P, tr, C), lambda i: (0, i, 0))],
        out_specs=[blk, blk, blk, blk],
        out_shape=[out, out, out, out],
        compiler_params=_params(32, 1),
    )(w, m, v, parts)


def _position():
    return lax.axis_index("x"), lax.axis_index("y"), lax.axis_index("c")


def _flat(px, py, pc):
    return 4 * px + 2 * py + pc


def _row_halves(rows, dtype):
    tile = _SUBLANES * (4 // jnp.dtype(dtype).itemsize)
    half = rows // 2 // tile * tile
    assert half > 0
    return (0, half), (half, rows - half)


def _gather_body(ins, outs, send_sems, recv_sems, local_sems, handshake):
    n = len(ins)
    x, y, c = _position()
    me, sibling = (x, y, c), (x, y, 1 - c)
    across_x, across_y, diagonal = (1 - x, y), (x, 1 - y), (1 - x, 1 - y)
    if handshake:
        _handshake([sibling] + [(*chip, cc) for chip in (across_x, across_y, diagonal) for cc in (c, 1 - c)])

    def copy(a, k, block, to, rows=None, src=None):
        dst = outs[a].at[_flat(*block)]
        if rows is not None:
            dst = dst.at[pl.ds(*rows)]
        return pltpu.make_async_remote_copy(
            src_ref=dst if src is None else src, dst_ref=dst,
            send_sem=send_sems.at[a, k], recv_sem=recv_sems.at[a, k],
            device_id=to, device_id_type=MESH)

    halves = [_row_halves(ins[a].shape[0], ins[a].dtype) for a in range(n)]
    mine = [pltpu.make_async_copy(ins[a], outs[a].at[_flat(*me)], local_sems.at[a]) for a in range(n)]
    for cp in mine:
        cp.start()
    sent = []

    def start(cp):
        cp.start()
        sent.append(cp)

    for a in range(n):
        start(copy(a, 0, me, sibling, src=ins[a]))
        start(copy(a, 1, me, (*across_x, c), src=ins[a]))
        start(copy(a, 2, me, (*across_y, c), src=ins[a]))
    for a in range(n):
        low, high = halves[a]
        copy(a, 1, (*across_x, c), me).wait_recv()
        start(copy(a, 3, (*across_x, c), (*across_y, c), rows=low))
        start(copy(a, 5, (*across_x, c), sibling))
        copy(a, 2, (*across_y, c), me).wait_recv()
        start(copy(a, 4, (*across_y, c), (*across_x, c), rows=high))
        start(copy(a, 6, (*across_y, c), sibling))
    for a in range(n):
        low, high = halves[a]
        copy(a, 3, (*diagonal, c), me, rows=low).wait_recv()
        start(copy(a, 7, (*diagonal, c), sibling, rows=low))
        copy(a, 4, (*diagonal, c), me, rows=high).wait_recv()
        start(copy(a, 8, (*diagonal, c), sibling, rows=high))
    for a in range(n):
        low, high = halves[a]
        copy(a, 0, sibling, me).wait_recv()
        copy(a, 5, (*across_x, 1 - c), me).wait_recv()
        copy(a, 6, (*across_y, 1 - c), me).wait_recv()
        copy(a, 7, (*diagonal, 1 - c), me, rows=low).wait_recv()
        copy(a, 8, (*diagonal, 1 - c), me, rows=high).wait_recv()
    for cp in sent:
        cp.wait_send()
    for cp in mine:
        cp.wait()


_GATHER_SLOTS = 9


def _gather_scratch(n):
    return [pltpu.SemaphoreType.DMA((n, _GATHER_SLOTS)), pltpu.SemaphoreType.DMA((n, _GATHER_SLOTS)),
            pltpu.SemaphoreType.DMA((n,))]


def _all_gather(shards, name):
    n = len(shards)

    def body(*refs):
        _gather_body(refs[:n], refs[n:2 * n], *refs[2 * n:], handshake=False)

    hbm = pl.BlockSpec(memory_space=pltpu.HBM)
    return _pallas_call(
        body, name=name,
        in_specs=[hbm] * n, out_specs=[hbm] * n,
        out_shape=[jax.ShapeDtypeStruct((N_DEV,) + s.shape, s.dtype) for s in shards],
        scratch_shapes=_gather_scratch(n),
    )(*shards)


def _handshake(peers):
    barrier = pltpu.get_barrier_semaphore()
    for peer in peers:
        pl.semaphore_signal(barrier, inc=1, device_id=peer, device_id_type=MESH)
    pl.semaphore_wait(barrier, len(peers))


def _sequencer_call(body, name, collective_id, out_type, scratch_types, operands):
    return pl.kernel(
        body, name=name, out_type=out_type,
        mesh=plsc.ScalarSubcoreMesh(axis_name="sequencer", num_cores=1),
        scratch_types=scratch_types,
        compiler_params=pltpu.CompilerParams(collective_id=collective_id),
    )(*operands)


def _seq_all_gather(shards, name, collective_id, after):
    n = len(shards)

    def body(*refs):
        _gather_body(refs[:n], refs[n + 1:2 * n + 1], *refs[2 * n + 1:], handshake=True)

    return _sequencer_call(
        body, name, collective_id,
        [jax.ShapeDtypeStruct((N_DEV,) + s.shape, s.dtype) for s in shards],
        _gather_scratch(n), list(shards) + [after])


def _seq_to_sibling(parts, name, collective_id, after):
    n = len(parts)

    def body(*refs):
        ins, outs = refs[:n], refs[n + len(after):2 * n + len(after)]
        send_sems, recv_sems = refs[2 * n + len(after):]
        x, y, c = _position()
        sibling = (x, y, 1 - c)
        _handshake([sibling])
        sent = []
        for a in range(n):
            for q in range(N_CHIPS):
                cp = pltpu.make_async_remote_copy(
                    src_ref=ins[a].at[2 * q + (1 - c)], dst_ref=outs[a].at[q],
                    send_sem=send_sems.at[a, q], recv_sem=recv_sems.at[a, q],
                    device_id=sibling, device_id_type=MESH)
                cp.start()
                sent.append(cp)
        for cp in sent:
            cp.wait_recv()
        for cp in sent:
            cp.wait_send()

    return _sequencer_call(
        body, name, collective_id,
        [jax.ShapeDtypeStruct((N_CHIPS,) + p.shape[1:], p.dtype) for p in parts],
        [pltpu.SemaphoreType.DMA((n, N_CHIPS)), pltpu.SemaphoreType.DMA((n, N_CHIPS))],
        list(parts) + list(after))


def _seq_to_chips(partials, name, collective_id):
    n = len(partials)

    def body(*refs):
        ins, outs = refs[:n], refs[n:2 * n]
        send_sems, recv_sems, local_sems = refs[2 * n:]
        x, y, c = _position()
        my_chip = 2 * x + y
        chips = [(1 - x, y), (x, 1 - y), (1 - x, 1 - y)]
        _handshake([(*chip, c) for chip in chips])
        mine = [pltpu.make_async_copy(ins[a].at[my_chip], outs[a].at[my_chip], local_sems.at[a]) for a in range(n)]
        for cp in mine:
            cp.start()
        sent = []
        for a in range(n):
            for j, (px, py) in enumerate(chips):
                cp = pltpu.make_async_remote_copy(
                    src_ref=ins[a].at[2 * px + py], dst_ref=outs[a].at[my_chip],
                    send_sem=send_sems.at[a, j], recv_sem=recv_sems.at[a, j],
                    device_id=(px, py, c), device_id_type=MESH)
                cp.start()
                sent.append(cp)
        for cp in sent:
            cp.wait_recv()
        for cp in sent:
            cp.wait_send()
        for cp in mine:
            cp.wait()

    return _sequencer_call(
        body, name, collective_id,
        [jax.ShapeDtypeStruct(p.shape, p.dtype) for p in partials],
        [pltpu.SemaphoreType.DMA((n, 3)), pltpu.SemaphoreType.DMA((n, 3)), pltpu.SemaphoreType.DMA((n,))],
        list(partials))


def _pair_add(parts, recvs, name, after=()):
    n = len(parts)
    core = lax.axis_index("c").astype(jnp.int32).reshape(1)

    def body(c_ref, *refs):
        ps, rs, outs = refs[:n], refs[n:2 * n], refs[2 * n + len(after):]
        for p_ref, r_ref, o_ref in zip(ps, rs, outs):
            o_ref[...] = (p_ref[...].astype(F32) + r_ref[...].astype(F32)).astype(o_ref.dtype)

    mine = lambda p: pl.BlockSpec((None,) + p.shape[1:], lambda q, c: (2 * q + c[0], 0, 0))
    blk = lambda p: pl.BlockSpec((None,) + p.shape[1:], lambda q, c: (q, 0, 0))
    return pl.pallas_call(
        body, name=name,
        grid_spec=pltpu.PrefetchScalarGridSpec(
            num_scalar_prefetch=1, grid=(N_CHIPS,),
            in_specs=[mine(p) for p in parts] + [blk(p) for p in parts] + [_UNREAD] * len(after),
            out_specs=[blk(p) for p in parts]),
        out_shape=[pltpu.HBM((N_CHIPS,) + p.shape[1:], p.dtype) for p in parts],
        compiler_params=_params(40, 1),
    )(core, *[pltpu.with_memory_space_constraint(a, pltpu.HBM) for a in (*parts, *recvs, *after)])


class _Reduced(NamedTuple):
    partials: list
    reduced: list


def _blocks(g):
    return g.reshape(N_DEV, -1, g.shape[-1])


def _reduce_scatter(parts, tag, ids, after=(), add_after=()):
    from_sibling = _seq_to_sibling(parts, "rs_sibling_" + tag, ids[0], after)
    partials = _pair_add(parts, from_sibling, "rs_add_" + tag, add_after)
    return _Reduced(partials, _seq_to_chips(partials, "rs_chips_" + tag, ids[1]))


_SMALL = ("ffn1_norm", "mix_norm", "conv_b", "conv_ln_g", "conv_ln_b", "forget_b", "out_norm_conv",
          "out_norm_attn", "ffn2_norm", "final_norm")
_PACK_WIDTH = 2 * D_CONV
_SLOT = dict(ffn1_norm=(0, 0), mix_norm=(1, 0), ffn2_norm=(2, 0), final_norm=(3, 0), conv_b=(4, 0),
             conv_ln_g=(4, D_CONV), conv_ln_b=(5, 0), out_norm_conv=(5, D_CONV), out_norm_attn=(6, 0),
             forget_b=(6, D_CONV))
_LOSS_ROW = 7
_CONV_ROW0 = 8
_PACK_ROWS = _CONV_ROW0 + CONV_HALO


def _pack_small(small, name):
    arrays = [small[n] for n in _SMALL] + [small["conv_w"], small["loss"]]

    def body(*refs):
        out = refs[-1]
        out[...] = jnp.zeros_like(out)
        for n, ref in zip(_SMALL, refs):
            row, lane = _SLOT[n]
            out[row:row + 1, lane:lane + ref.shape[1]] = ref[...]
        out[_CONV_ROW0:, :D_CONV] = refs[len(_SMALL)][...]
        out[_LOSS_ROW:_LOSS_ROW + 1, :LANES] = refs[len(_SMALL) + 1][...]

    return _pallas_call(body, name=name, out_shape=jax.ShapeDtypeStruct((_PACK_ROWS, _PACK_WIDTH), F32))(*arrays)


def _adamw_small(gathered, w, m, v, name):
    c1 = 1.0 - ADAM_B1 ** ADAM_STEP
    c2 = 1.0 - ADAM_B2 ** ADAM_STEP
    k = len(_SMALL)

    def body(g_ref, *refs):
        ws, ms, vs = refs[:k], refs[k:2 * k], refs[2 * k:3 * k]
        outs = refs[3 * k:]
        total = g_ref[0]
        for s in range(1, N_DEV):
            total = total + g_ref[s]
        for i, n in enumerate(_SMALL):
            row, lane = _SLOT[n]
            width = ws[i].shape[1]
            g = total[row:row + 1, lane:lane + width]
            mn = ADAM_B1 * ms[i][...] + (1.0 - ADAM_B1) * g
            vn = ADAM_B2 * vs[i][...] + (1.0 - ADAM_B2) * (g * g)
            o_g, o_d, o_m, o_v = outs[4 * i:4 * i + 4]
            o_g[...] = g
            o_m[...] = mn
            o_v[...] = vn
            o_d[...] = -ADAM_LR * ((mn / c1) / (jnp.sqrt(vn / c2) + ADAM_EPS) + ADAM_WD * ws[i][...])
        outs[4 * k][...] = total[_CONV_ROW0:, :D_CONV]
        outs[4 * k + 1][...] = total[_LOSS_ROW:_LOSS_ROW + 1, :LANES]

    shapes = []
    for n in _SMALL:
        shapes += [jax.ShapeDtypeStruct(w[n].shape, F32)] * 4
    shapes.append(jax.ShapeDtypeStruct((CONV_HALO, D_CONV), F32))
    shapes.append(jax.ShapeDtypeStruct((1, LANES), F32))
    res = _pallas_call(body, name=name, out_shape=shapes)(
        gathered, *[w[n] for n in _SMALL], *[m[n] for n in _SMALL], *[v[n] for n in _SMALL])
    return {n: res[4 * i:4 * i + 4] for i, n in enumerate(_SMALL)}, res[4 * k], res[4 * k + 1]


def _local_step(x, target, norms, shard):
    D = x.shape[1]
    J = N_DEV // 2
    as13 = lambda g: g.reshape(2, J, g.shape[1], D)

    (g13_1,) = _all_gather([shard["ffn1_w13"]], "gather_ffn1_w13")
    (g2_1,) = _seq_all_gather([shard["ffn1_w2"]], "gather_ffn1_w2", 10, after=g13_1)
    w13_1 = as13(g13_1)
    G1, U1, A1 = _ffn_up(x, norms["ffn1_norm"], w13_1, "ffn1_up")
    gin, gconv = _seq_all_gather([shard["w_in"], shard["conv_w"]], "gather_mix", 1, after=G1)
    w2_1 = g2_1.reshape(-1, D)
    x1 = _ffn_down(x, A1, w2_1, "ffn1_down")
    gout, g13_2, g2_2 = _seq_all_gather([shard["w_out"], shard["ffn2_w13"], shard["ffn2_w2"]], "gather_ffn2", 2,
                                        after=x1)
    wout = gout.reshape(-1, D)
    conv_w32 = jnp.pad(gconv.transpose(1, 0, 2).reshape(CONV_TAPS, D_CONV), ((0, CONV_HALO - CONV_TAPS), (0, 0)))

    ag, k, v, qT, kT, vT, fl = _inproj_fwd(x1, norms["mix_norm"], gin, "inproj_fwd")
    cum, cumT = _forget_fwd(fl, norms["forget_b"], "forget_fwd")
    yc, c = _conv_fwd(ag, conv_w32, norms["conv_b"], norms["conv_ln_g"], norms["conv_ln_b"], "conv_fwd")
    o, lseT = _attn_fwd(qT, k, vT, cum, cumT, "attn_fwd")
    x2 = _outproj_fwd(x1, c, o, norms["out_norm_conv"], norms["out_norm_attn"], wout, "outproj_fwd")
    w13_2, w2_2 = as13(g13_2), g2_2.reshape(-1, D)
    G2, U2, A2 = _ffn_up(x2, norms["ffn2_norm"], w13_2, "ffn2_up")
    loss, dx3, d_final = _ffn_down_loss(x2, A2, w2_2, norms["final_norm"], target, "ffn2_down_loss")

    dw2_2 = _ffn_w2_grad(dx3, A2, "ffn2_w2_grad")
    dx2, d_ffn2n, h3, dG2, dU2 = _ffn_bwd_act(x2, norms["ffn2_norm"], dx3, G2, U2, w13_2, w2_2, "ffn2_bwd_act")
    dw13_2 = _ffn_w13_grad(h3, dG2, dU2, "ffn2_w13_grad")
    dc, dobT, deltaT, dwout, d_onc, d_ona = _outproj_bwd(
        dx2, c, o, norms["out_norm_conv"], norms["out_norm_attn"], wout, "outproj_bwd")
    red_ffn2 = _reduce_scatter([_blocks(dw13_2), _blocks(dw2_2)], "ffn2", (3, 4), add_after=(dc,))
    dqT, dkT, dvT, dcum = _attn_bwd(qT, k, kT, v, dobT, lseT, deltaT, cum, cumT, "attn_bwd",
                                    after=red_ffn2.partials)
    dfl, d_fb = _forget_bwd(dcum, fl, norms["forget_b"], "forget_bwd")
    dag, d_convw, d_cb, d_lg, d_lb = _conv_bwd(dc, yc, ag, conv_w32, norms["conv_ln_g"], norms["conv_ln_b"], "conv_bwd")
    dx1, d_mixn, h2 = _inproj_bwd_act(x1, norms["mix_norm"], dx2, dag, dqT, dkT, dvT, dfl, gin, "inproj_bwd_act")
    dwin_blocks = _inproj_bwd_weights(h2, dag, dqT, dkT, dvT, dfl, "inproj_bwd_weights")
    early = [dwin_blocks, _blocks(dwout)]
    sib_early = _seq_to_sibling(early, "rs_sibling_mix_early", 11, red_ffn2.reduced[:1])
    dw2_1 = _ffn_w2_grad(dx1, A1, "ffn1_w2_grad", after=(dwin_blocks,))
    late = [_blocks(dw2_1)]
    sib_late = _seq_to_sibling(late, "rs_sibling_mix", 5, sib_early[:1])
    mix_partials = _pair_add(early + late, sib_early + sib_late, "rs_add_mix")
    red_mix = _Reduced(mix_partials, _seq_to_chips(mix_partials, "rs_chips_mix", 6))
    dx, d_ffn1n, h1, dG1, dU1 = _ffn_bwd_act(x, norms["ffn1_norm"], dx1, G1, U1, w13_1, w2_1, "ffn1_bwd_act",
                                             after=red_mix.partials)
    small = dict(ffn1_norm=d_ffn1n, mix_norm=d_mixn, conv_b=d_cb, conv_ln_g=d_lg, conv_ln_b=d_lb,
                 forget_b=d_fb, out_norm_conv=d_onc, out_norm_attn=d_ona, ffn2_norm=d_ffn2n,
                 final_norm=d_final, conv_w=d_convw, loss=loss)
    packed_small = _pack_small(small, "pack_small_grads")
    (gathered_small,) = _seq_all_gather([packed_small], "gather_small_grads", 9, after=red_mix.partials[0])
    dw13_1 = _ffn_w13_grad(h1, dG1, dU1, "ffn1_w13_grad")
    red_w13_1 = _reduce_scatter([_blocks(dw13_1)], "ffn1_w13", (7, 8), after=[red_mix.reduced[0], gathered_small])
    big = dict(ffn1_w13=red_w13_1.reduced[0], ffn1_w2=red_mix.reduced[2], w_in=red_mix.reduced[0],
               w_out=red_mix.reduced[1], ffn2_w13=red_ffn2.reduced[0], ffn2_w2=red_ffn2.reduced[1])
    return dx, gathered_small, big


_BIG = ("ffn1_w13", "ffn1_w2", "w_in", "w_out", "ffn2_w13", "ffn2_w2")
_TRANSPOSED = ("ffn1_w13", "ffn2_w13", "w_in")
_ORDER = ("ffn1_norm", "ffn1_w13", "ffn1_w2", "mix_norm", "w_in", "conv_w", "conv_b", "conv_ln_g", "conv_ln_b",
          "forget_b", "out_norm_conv", "out_norm_attn", "w_out", "ffn2_norm", "ffn2_w13", "ffn2_w2", "final_norm")


def kernel(x, ffn1_norm, ffn1_w13, ffn1_w2, mix_norm, w_in, conv_w, conv_b, conv_ln_g, conv_ln_b, forget_b, out_norm_conv, out_norm_attn, w_out, ffn2_norm, ffn2_w13, ffn2_w2, final_norm, loss_target, m_ffn1_norm, m_ffn1_w13, m_ffn1_w2, m_mix_norm, m_w_in, m_conv_w, m_conv_b, m_conv_ln_g, m_conv_ln_b, m_forget_b, m_out_norm_conv, m_out_norm_attn, m_w_out, m_ffn2_norm, m_ffn2_w13, m_ffn2_w2, m_final_norm, v_ffn1_norm, v_ffn1_w13, v_ffn1_w2, v_mix_norm, v_w_in, v_conv_w, v_conv_b, v_conv_ln_g, v_conv_ln_b, v_forget_b, v_out_norm_conv, v_out_norm_attn, v_w_out, v_ffn2_norm, v_ffn2_w13, v_ffn2_w2, v_final_norm):
    w = dict(ffn1_norm=ffn1_norm, ffn1_w13=ffn1_w13, ffn1_w2=ffn1_w2, mix_norm=mix_norm, w_in=w_in, conv_w=conv_w,
             conv_b=conv_b, conv_ln_g=conv_ln_g, conv_ln_b=conv_ln_b, forget_b=forget_b, out_norm_conv=out_norm_conv,
             out_norm_attn=out_norm_attn, w_out=w_out, ffn2_norm=ffn2_norm, ffn2_w13=ffn2_w13, ffn2_w2=ffn2_w2,
             final_norm=final_norm)
    m = dict(ffn1_norm=m_ffn1_norm, ffn1_w13=m_ffn1_w13, ffn1_w2=m_ffn1_w2, mix_norm=m_mix_norm, w_in=m_w_in,
             conv_w=m_conv_w, conv_b=m_conv_b, conv_ln_g=m_conv_ln_g, conv_ln_b=m_conv_ln_b, forget_b=m_forget_b,
             out_norm_conv=m_out_norm_conv, out_norm_attn=m_out_norm_attn, w_out=m_w_out, ffn2_norm=m_ffn2_norm,
             ffn2_w13=m_ffn2_w13, ffn2_w2=m_ffn2_w2, final_norm=m_final_norm)
    v = dict(ffn1_norm=v_ffn1_norm, ffn1_w13=v_ffn1_w13, ffn1_w2=v_ffn1_w2, mix_norm=v_mix_norm, w_in=v_w_in,
             conv_w=v_conv_w, conv_b=v_conv_b, conv_ln_g=v_conv_ln_g, conv_ln_b=v_conv_ln_b, forget_b=v_forget_b,
             out_norm_conv=v_out_norm_conv, out_norm_attn=v_out_norm_attn, w_out=v_w_out, ffn2_norm=v_ffn2_norm,
             ffn2_w13=v_ffn2_w13, ffn2_w2=v_ffn2_w2, final_norm=v_final_norm)
    shapes = {n: a.shape for n, a in w.items()}
    T, D = x.shape[1], x.shape[2]
    def two(n, a):
        if a.ndim != 3:
            return a.reshape(1, -1)
        a = a.reshape(a.shape[-2], a.shape[-1])
        return a.T if n in _TRANSPOSED else a

    w2d = {n: two(n, a) for n, a in w.items()}
    m2d = {n: two(n, a) for n, a in m.items()}
    v2d = {n: two(n, a) for n, a in v.items()}

    shard = {n: w2d[n].astype(MXU) for n in _BIG}
    shard["conv_w"] = w2d["conv_w"]
    norms = {n: w2d[n] for n in _SMALL}
    norms["forget_b"] = jnp.pad(w2d["forget_b"], ((0, 0), (0, LANES - N_HEADS)))
    dx, gathered_small, big = _local_step(x[0], loss_target[0], norms, shard)

    grads, deltas, new_m, new_v = {}, {}, {}, {}
    for n in _BIG:
        g, d, nm, nv = _adamw(w2d[n], m2d[n], v2d[n], big[n], "adamw_" + n)
        grads[n], deltas[n], new_m[n], new_v[n] = g, d, nm, nv

    small_out, conv_g_full, loss = _adamw_small(gathered_small, w2d, m2d, v2d, "adamw_small")
    for n in _SMALL:
        grads[n], deltas[n], new_m[n], new_v[n] = small_out[n]
    conv_g_full = conv_g_full[:CONV_TAPS]
    xi, yi, ci = _position()
    cw = shapes["conv_w"][-1]
    conv_g_mine = lax.dynamic_slice_in_dim(conv_g_full, _flat(xi, yi, ci) * cw, cw, axis=1)
    g, d, nm, nv = _adamw(w2d["conv_w"], m2d["conv_w"], v2d["conv_w"], conv_g_mine[None], "adamw_conv_w")
    grads["conv_w"], deltas["conv_w"], new_m["conv_w"], new_v["conv_w"] = g, d, nm, nv

    shaped = lambda dct: [(dct[n].T if n in _TRANSPOSED else dct[n]).reshape(shapes[n]) for n in _ORDER]
    return (loss[0, 0], dx[None], *shaped(grads), *shaped(deltas), *shaped(new_m), *shaped(new_v))
```

```python
from typing import NamedTuple

import jax
import jax.numpy as jnp
from jax import lax
from jax.experimental import pallas as pl
from jax.experimental.pallas import tpu as pltpu
from jax.experimental.pallas import tpu_sc as plsc

F32 = jnp.float32
MXU = jnp.bfloat16
EPS = 1e-6
N_HEADS = 8
HEAD_DIM = 64
D_CONV = 512
D_ATTN = N_HEADS * HEAD_DIM
CONV_TAPS = 31
CONV_HALO = 32
SCALE = HEAD_DIM ** -0.5
NEG = -1e30
LANES = 128
N_DEV = 8
N_CHIPS = N_DEV // 2
MESH = pl.DeviceIdType.MESH
MIB = 1 << 20

ADAM_LR = 0.001
ADAM_B1 = 0.9
ADAM_B2 = 0.999
ADAM_EPS = 1e-08
ADAM_WD = 0.01
ADAM_STEP = 10


_UNREAD = pl.BlockSpec(memory_space=pl.ANY)


def _pallas_call(body, *, out_shape, **kwargs):
    in_hbm = lambda s: pltpu.HBM(s.shape, s.dtype)
    outs = [in_hbm(s) for s in out_shape] if isinstance(out_shape, (list, tuple)) else in_hbm(out_shape)
    call = pl.pallas_call(body, out_shape=outs, **kwargs)
    return lambda *operands: call(*[pltpu.with_memory_space_constraint(a, pltpu.HBM) for a in operands])


def _params(vmem_mib, n_axes):
    return pltpu.CompilerParams(dimension_semantics=("arbitrary",) * n_axes, vmem_limit_bytes=vmem_mib * MIB)


def _mm(a, b):
    return jnp.dot(a, b, preferred_element_type=F32)


def _mm_nt(a, b):
    return lax.dot_general(a, b, (((1,), (1,)), ((), ())), preferred_element_type=F32)


def _mm_tn(a, b):
    return lax.dot_general(a, b, (((0,), (0,)), ((), ())), preferred_element_type=F32)


def _rms_fwd(x, g):
    r = lax.rsqrt(jnp.mean(x * x, axis=-1, keepdims=True) + EPS)
    return x * r * g, r


def _rms_bwd(x, r, g, dy):
    gdy = dy * g
    dx = r * gdy - x * (r * r * r) * jnp.mean(x * gdy, axis=-1, keepdims=True)
    dg = jnp.sum(dy * x * r, axis=0, keepdims=True)
    return dx, dg


def _silu_grad(z, sz):
    return sz * (1.0 + z * (1.0 - sz))


def _three_terms(x):
    x1 = x.astype(jnp.bfloat16)
    r1 = x - x1.astype(F32)
    x2 = r1.astype(jnp.bfloat16)
    x3 = (r1 - x2.astype(F32)).astype(jnp.bfloat16)
    return x1, x2, x3


def _exact_tri_dot(tri, x):
    x1, x2, x3 = _three_terms(x)
    return _mm(tri, x1) + _mm(tri, x2) + _mm(tri, x3)


def _exact_dot_01(x, sel):
    x1, x2, x3 = _three_terms(x)
    return _mm(x1, sel) + _mm(x2, sel) + _mm(x3, sel)


def _tile(n, want):
    t = min(n, want)
    assert n % t == 0
    return t


_FFN_CHUNK = 256


def _ffn_up(x, g, w13, name):
    T, D = x.shape
    _, J, bf, _ = w13.shape
    tm = _tile(T, 1024)
    I = T // tm

    def body(x_ref, g_ref, w13_ref, G_ref, U_ref, A_ref, h_s):
        j = pl.program_id(0)
        i = pl.program_id(1)
        rows = pl.ds(pl.multiple_of(i * tm, tm), tm)

        @pl.when(j == 0)
        def _():
            h, _ = _rms_fwd(x_ref[...], g_ref[...])
            h_s[rows, :] = h.astype(MXU)

        chunks = [slice(r0, r0 + _FFN_CHUNK) for r0 in range(0, tm, _FFN_CHUNK)]
        hbs = [h_s[pl.ds(pl.multiple_of(i * tm + rs.start, _FFN_CHUNK), _FFN_CHUNK), :] for rs in chunks]
        GU = [(_mm_nt(hb, w13_ref[0]), _mm_nt(hb, w13_ref[1])) for hb in hbs]
        for rs, (G, U) in zip(chunks, GU):
            G_ref[rs, :] = G.astype(MXU)
            U_ref[rs, :] = U.astype(MXU)
            A_ref[rs, :] = (G * jax.nn.sigmoid(G) * U).astype(MXU)

    blk = pl.BlockSpec((None, tm, bf), lambda j, i: (j, i, 0))
    hid = jax.ShapeDtypeStruct((J, T, bf), MXU)
    return _pallas_call(
        body, name=name, grid=(J, I),
        in_specs=[pl.BlockSpec((tm, D), lambda j, i: (jnp.where(j == 0, i, I - 1), 0)),
                  pl.BlockSpec((1, D), lambda j, i: (0, 0)),
                  pl.BlockSpec((2, None, bf, D), lambda j, i: (0, j, 0, 0))],
        out_specs=[blk, blk, blk],
        out_shape=[hid, hid, hid],
        scratch_shapes=[pltpu.VMEM((T, D), MXU)],
        compiler_params=_params(48, 2),
    )(x, g, w13)


def _ffn_down(x, A, w2, name):
    T, D = x.shape
    J, _, bf = A.shape
    tm = _tile(T, 512)

    def body(x_ref, A_ref, w2_ref, xo_ref):
        f = _mm(A_ref[0], w2_ref[0:bf, :])
        for j in range(1, J):
            f = f + _mm(A_ref[j], w2_ref[j * bf:(j + 1) * bf, :])
        xo_ref[...] = x_ref[...] + 0.5 * f

    row = pl.BlockSpec((tm, D), lambda i: (i, 0))
    return _pallas_call(
        body, name=name, grid=(T // tm,),
        in_specs=[row, pl.BlockSpec((J, tm, bf), lambda i: (0, i, 0)), pl.BlockSpec((J * bf, D), lambda i: (0, 0))],
        out_specs=row,
        out_shape=jax.ShapeDtypeStruct((T, D), F32),
        compiler_params=_params(48, 1),
    )(x, A, w2)


def _ffn_bwd_act(x, g, dy, Gs, Us, w13, w2, name, after=()):
    T, D = x.shape
    _, J, bf, _ = w13.shape
    tm = _tile(T, 512)
    I = T // tm

    def body(x_ref, g_ref, dy_ref, G_ref, U_ref, w13_ref, w2_ref, *rest):
        dx_ref, dg_ref, h_ref, dG_ref, dU_ref, dh_s, dF_s, h_s = rest[len(after):]
        j = pl.program_id(0)
        i = pl.program_id(1)
        rows = pl.ds(pl.multiple_of(i * tm, tm), tm)

        @pl.when(j == 0)
        def _():
            h, _ = _rms_fwd(x_ref[...], g_ref[...])
            hb = h.astype(MXU)
            h_s[rows, :] = hb
            h_ref[...] = hb
            dF_s[rows, :] = (0.5 * dy_ref[...]).astype(MXU)
            dh_s[rows, :] = jnp.zeros((tm, D), F32)

        chunks = [slice(r0, r0 + _FFN_CHUNK) for r0 in range(0, tm, _FFN_CHUNK)]
        crows = [pl.ds(pl.multiple_of(i * tm + rs.start, _FFN_CHUNK), _FFN_CHUNK) for rs in chunks]
        dAs = [_mm_nt(dF_s[cr, :], w2_ref[...]) for cr in crows]
        for rs, cr, dA in zip(chunks, crows, dAs):
            G = G_ref[rs, :].astype(F32)
            U = U_ref[rs, :].astype(F32)
            sg = jax.nn.sigmoid(G)
            s = G * sg
            dUb = (dA * s).astype(MXU)
            dGb = (dA * U * _silu_grad(G, sg)).astype(MXU)
            dG_ref[rs, :] = dGb
            dU_ref[rs, :] = dUb
            dh_s[cr, :] += _mm(dGb, w13_ref[0]) + _mm(dUb, w13_ref[1])

        @pl.when(j == J - 1)
        def _():
            xv = x_ref[...]
            gv = g_ref[...]
            _, r = _rms_fwd(xv, gv)
            dxn, dgp = _rms_bwd(xv, r, gv, dh_s[rows, :])
            dx_ref[...] = dy_ref[...] + dxn

            @pl.when(i == 0)
            def _():
                dg_ref[...] = dgp

            @pl.when(i > 0)
            def _():
                dg_ref[...] += dgp

    ends = lambda j, i: (jnp.where((j == 0) | (j == J - 1), i, I - 1), 0)
    blk = pl.BlockSpec((None, tm, bf), lambda j, i: (j, i, 0))
    hid = jax.ShapeDtypeStruct((J, T, bf), MXU)
    return _pallas_call(
        body, name=name, grid=(J, I),
        in_specs=[pl.BlockSpec((tm, D), ends), pl.BlockSpec((1, D), lambda j, i: (0, 0)), pl.BlockSpec((tm, D), ends),
                  blk, blk, pl.BlockSpec((2, None, bf, D), lambda j, i: (0, j, 0, 0)),
                  pl.BlockSpec((bf, D), lambda j, i: (j, 0))] + [_UNREAD] * len(after),
        out_specs=[pl.BlockSpec((tm, D), lambda j, i: (jnp.where(j == J - 1, i, 0), 0)),
                   pl.BlockSpec((1, D), lambda j, i: (0, 0)),
                   pl.BlockSpec((tm, D), lambda j, i: (jnp.where(j == 0, i, I - 1), 0)), blk, blk],
        out_shape=[jax.ShapeDtypeStruct((T, D), F32), jax.ShapeDtypeStruct((1, D), F32),
                   jax.ShapeDtypeStruct((T, D), MXU), hid, hid],
        scratch_shapes=[pltpu.VMEM((T, D), F32), pltpu.VMEM((T, D), MXU), pltpu.VMEM((T, D), MXU)],
        compiler_params=_params(58, 2),
    )(x, g, dy, Gs, Us, w13, w2, *after)


def _ffn_w13_grad(h, dG, dU, name):
    T, D = h.shape
    J, _, bf = dG.shape

    def body(h_ref, dG_ref, dU_ref, dw13_ref):
        dw13_ref[0] = _mm_tn(dG_ref[...], h_ref[...]).astype(dw13_ref.dtype)
        dw13_ref[1] = _mm_tn(dU_ref[...], h_ref[...]).astype(dw13_ref.dtype)

    blk = pl.BlockSpec((None, T, bf), lambda j: (j, 0, 0))
    return _pallas_call(
        body, name=name, grid=(J,),
        in_specs=[pl.BlockSpec((T, D), lambda j: (0, 0)), blk, blk],
        out_specs=pl.BlockSpec((2, None, bf, D), lambda j: (0, j, 0, 0)),
        out_shape=jax.ShapeDtypeStruct((2, J, bf, D), MXU),
        compiler_params=_params(48, 1),
    )(h, dG, dU)


def _ffn_w2_grad(dy, A, name, after=()):
    T, D = dy.shape
    J, _, bf = A.shape

    def body(dy_ref, A_ref, *rest):
        dw2_ref, dF_s = rest[len(after):]

        @pl.when(pl.program_id(0) == 0)
        def _():
            dF_s[...] = (0.5 * dy_ref[...]).astype(MXU)

        dw2_ref[...] = _mm_tn(A_ref[...], dF_s[...]).astype(dw2_ref.dtype)

    return _pallas_call(
        body, name=name, grid=(J,),
        in_specs=[pl.BlockSpec((T, D), lambda j: (0, 0)), pl.BlockSpec((None, T, bf), lambda j: (j, 0, 0))]
        + [_UNREAD] * len(after),
        out_specs=pl.BlockSpec((bf, D), lambda j: (j, 0)),
        out_shape=jax.ShapeDtypeStruct((J * bf, D), MXU),
        scratch_shapes=[pltpu.VMEM((T, D), MXU)],
        compiler_params=_params(48, 1),
    )(dy, A, *after)


_AG0, _Q0, _K0, _V0, _F0 = 0, 2 * D_CONV, 2 * D_CONV + D_ATTN, 2 * D_CONV + 2 * D_ATTN, 2 * D_CONV + 3 * D_ATTN
N_IN = _F0 + N_HEADS
N_IN_PAD = _F0 + LANES
_IN_BLOCK = N_IN // N_DEV


def _rows_from_blocks(blocks_ref, rows_ref):
    for p in range(N_DEV):
        rows_ref[_IN_BLOCK * p:_IN_BLOCK * (p + 1), :] = blocks_ref[p]
    rows_ref[N_IN:, :] = jnp.zeros((N_IN_PAD - N_IN, rows_ref.shape[1]), rows_ref.dtype)


def _inproj_fwd(x1, gm, win_blocks, name):
    T, D = x1.shape
    tm = _tile(T, 512)

    def body(x_ref, g_ref, wb_ref, ag_ref, k_ref, v_ref, qT_ref, kT_ref, vT_ref, fl_ref, w_ref):
        @pl.when(pl.program_id(0) == 0)
        def _():
            _rows_from_blocks(wb_ref, w_ref)

        h, _ = _rms_fwd(x_ref[...], g_ref[...])
        hb = h.astype(MXU)
        ag_ref[...] = _mm_nt(hb, w_ref[_AG0:_Q0, :])
        qT_ref[...] = (_mm_nt(hb, w_ref[_Q0:_K0, :]) * SCALE).T.astype(MXU)
        for c0, ref, refT in ((_K0, k_ref, kT_ref), (_V0, v_ref, vT_ref)):
            y = _mm_nt(hb, w_ref[c0:c0 + D_ATTN, :])
            ref[...] = y.astype(MXU)
            refT[...] = y.T.astype(MXU)
        fl_ref[...] = _mm_nt(hb, w_ref[_F0:N_IN_PAD, :])

    row = lambda w: pl.BlockSpec((tm, w), lambda i: (i, 0))
    col = pl.BlockSpec((D_ATTN, tm), lambda i: (0, i))
    std = jax.ShapeDtypeStruct((T, D_ATTN), MXU)
    trn = jax.ShapeDtypeStruct((D_ATTN, T), MXU)
    return _pallas_call(
        body, name=name, grid=(T // tm,),
        in_specs=[row(D), pl.BlockSpec((1, D), lambda i: (0, 0)),
                  pl.BlockSpec((N_DEV, _IN_BLOCK, D), lambda i: (0, 0, 0))],
        out_specs=[row(2 * D_CONV), row(D_ATTN), row(D_ATTN), col, col, col, row(LANES)],
        out_shape=[jax.ShapeDtypeStruct((T, 2 * D_CONV), F32), std, std, trn, trn, trn,
                   jax.ShapeDtypeStruct((T, LANES), F32)],
        scratch_shapes=[pltpu.VMEM((N_IN_PAD, D), MXU)],
        compiler_params=_params(40, 1),
    )(x1, gm, win_blocks)


def _inproj_bwd_act(x1, gm, dx2, dag, dqT, dkT, dvT, dfl, win_blocks, name):
    T, D = x1.shape
    tm = _tile(T, 512)

    def body(x_ref, g_ref, dx2_ref, dag_ref, dqT_ref, dkT_ref, dvT_ref, dfl_ref, wb_ref, dx1_ref, dg_ref, h_ref,
             w_ref):
        i = pl.program_id(0)

        @pl.when(i == 0)
        def _():
            _rows_from_blocks(wb_ref, w_ref)

        xv = x_ref[...]
        gv = g_ref[...]
        h, r = _rms_fwd(xv, gv)
        h_ref[...] = h.astype(MXU)
        dh = _mm(dag_ref[...], w_ref[_AG0:_Q0, :])
        for c0, ref in ((_Q0, dqT_ref), (_K0, dkT_ref), (_V0, dvT_ref)):
            dh = dh + _mm_tn(ref[...].astype(MXU), w_ref[c0:c0 + D_ATTN, :])
        dh = dh + _mm(dfl_ref[...].astype(MXU), w_ref[_F0:N_IN_PAD, :])
        dxn, dgp = _rms_bwd(xv, r, gv, dh)
        dx1_ref[...] = dx2_ref[...] + dxn

        @pl.when(i == 0)
        def _():
            dg_ref[...] = dgp

        @pl.when(i > 0)
        def _():
            dg_ref[...] += dgp

    row = lambda w: pl.BlockSpec((tm, w), lambda i: (i, 0))
    col = pl.BlockSpec((D_ATTN, tm), lambda i: (0, i))
    full = lambda a, b: pl.BlockSpec((a, b), lambda i: (0, 0))
    return _pallas_call(
        body, name=name, grid=(T // tm,),
        in_specs=[row(D), full(1, D), row(D), row(2 * D_CONV), col, col, col, row(LANES),
                  pl.BlockSpec((N_DEV, _IN_BLOCK, D), lambda i: (0, 0, 0))],
        out_specs=[row(D), full(1, D), row(D)],
        out_shape=[jax.ShapeDtypeStruct((T, D), F32), jax.ShapeDtypeStruct((1, D), F32),
                   jax.ShapeDtypeStruct((T, D), MXU)],
        scratch_shapes=[pltpu.VMEM((N_IN_PAD, D), MXU)],
        compiler_params=_params(40, 1),
    )(x1, gm, dx2, dag, dqT, dkT, dvT, dfl, win_blocks)


def _inproj_bwd_weights(h, dag, dqT, dkT, dvT, dfl, name, after=()):
    T, D = h.shape
    operands = (h, dag, dqT, dkT, dvT, dfl)
    n = len(operands)

    def body(*refs):
        sources = refs[:n]
        blocks_ref = refs[n + len(after)]
        h_v, dag_v, dqT_v, dkT_v, dvT_v, dfl_v = buffers = refs[n + len(after) + 1:2 * n + len(after) + 1]
        dw_ref, sems = refs[2 * n + len(after) + 1:]
        copies = [pltpu.make_async_copy(src, dst, sems.at[k]) for k, (src, dst) in enumerate(zip(sources, buffers))]
        for cp in copies:
            cp.start()
        copies[0].wait()
        hb = h_v[...]
        copies[1].wait()
        dw_ref[_AG0:_Q0, :] = _mm_tn(dag_v[...], hb).astype(dw_ref.dtype)
        for k, (c0, ref) in enumerate(((_Q0, dqT_v), (_K0, dkT_v), (_V0, dvT_v))):
            copies[2 + k].wait()
            dw_ref[c0:c0 + D_ATTN, :] = _mm(ref[...].astype(MXU), hb).astype(dw_ref.dtype)
        copies[5].wait()
        dw_ref[_F0:N_IN_PAD, :] = _mm_tn(dfl_v[...].astype(MXU), hb).astype(dw_ref.dtype)
        for p in range(N_DEV):
            blocks_ref[p] = dw_ref[_IN_BLOCK * p:_IN_BLOCK * (p + 1), :]

    return _pallas_call(
        body, name=name, in_specs=[_UNREAD] * (n + len(after)), out_specs=pl.BlockSpec(memory_space=pltpu.VMEM),
        out_shape=jax.ShapeDtypeStruct((N_DEV, _IN_BLOCK, D), MXU),
        scratch_shapes=[pltpu.VMEM(a.shape, a.dtype) for a in operands]
        + [pltpu.VMEM((N_IN_PAD, D), MXU), pltpu.SemaphoreType.DMA((n,))],
        compiler_params=pltpu.CompilerParams(vmem_limit_bytes=56 * MIB),
    )(*operands, *after)


def _forget_fwd(fl, fbp, name):
    T = fl.shape[0]
    tb = _tile(T, 256)

    def body(fl_ref, fb_ref, cum_ref, cumT_ref):
        ri = lax.broadcasted_iota(jnp.int32, (tb, tb), 0)
        ci = lax.broadcasted_iota(jnp.int32, (tb, tb), 1)
        tri = (ri >= ci).astype(jnp.bfloat16)
        carry = jnp.zeros((1, LANES), F32)
        for b in range(T // tb):
            z = fl_ref[b * tb:(b + 1) * tb, :] + fb_ref[...]
            lf = jnp.minimum(z, 0.0) - jnp.log1p(jnp.exp(-jnp.abs(z)))
            c = _exact_tri_dot(tri, lf) + carry
            cum_ref[b * tb:(b + 1) * tb, :] = c
            carry = c[tb - 1:tb, :]
        cumT_ref[...] = cum_ref[...].T[:N_HEADS, :]

    return _pallas_call(
        body, name=name,
        out_shape=[jax.ShapeDtypeStruct((T, LANES), F32), jax.ShapeDtypeStruct((N_HEADS, T), F32)],
        compiler_params=pltpu.CompilerParams(vmem_limit_bytes=32 * MIB),
    )(fl, fbp)


def _forget_bwd(dcum, fl, fbp, name):
    T = fl.shape[0]
    tb = _tile(T, 256)

    def body(dc_ref, fl_ref, fb_ref, dfl_ref, dfb_ref):
        ri = lax.broadcasted_iota(jnp.int32, (tb, tb), 0)
        ci = lax.broadcasted_iota(jnp.int32, (tb, tb), 1)
        tri = (ri <= ci).astype(jnp.bfloat16)
        carry = jnp.zeros((1, LANES), F32)
        dfb = jnp.zeros((1, LANES), F32)
        for b in reversed(range(T // tb)):
            sl = slice(b * tb, (b + 1) * tb)
            dl = _exact_tri_dot(tri, dc_ref[sl, :]) + carry
            carry = dl[0:1, :]
            z = fl_ref[sl, :] + fb_ref[...]
            dfl = dl * jax.nn.sigmoid(-z)
            dfl_ref[sl, :] = dfl
            dfb = dfb + jnp.sum(dfl, axis=0, keepdims=True)
        dfb_ref[...] = dfb

    return _pallas_call(
        body, name=name,
        out_shape=[jax.ShapeDtypeStruct((T, LANES), F32), jax.ShapeDtypeStruct((1, LANES), F32)],
        compiler_params=pltpu.CompilerParams(vmem_limit_bytes=32 * MIB),
    )(dcum, fl, fbp)


def _causal_keep(i, j, tq, tk):
    key = j * tk + lax.broadcasted_iota(jnp.int32, (tk, tq), 0)
    qry = i * tq + lax.broadcasted_iota(jnp.int32, (tk, tq), 1)
    return key <= qry


def _split_hi_lo(x):
    hi = x.astype(MXU)
    lo = (x - hi.astype(F32)).astype(MXU)
    return hi, lo


def _attn_fwd(qT, k, vT, cum, cumT, name):
    T = k.shape[0]
    tq = _tile(T, 256)
    tk = _tile(tq, 256)
    kpq = tq // tk
    heads = [slice(HEAD_DIM * h, HEAD_DIM * (h + 1)) for h in range(N_HEADS)]

    def body(qT_ref, k_ref, vT_ref, cum_ref, cumT_ref, o_ref, lseT_ref, acc_s, m_s, l_s):
        i = pl.program_id(0)
        acc_s[...] = jnp.zeros_like(acc_s)
        m_s[...] = jnp.full_like(m_s, NEG)
        l_s[...] = jnp.zeros_like(l_s)

        def kblock(j, masked):
            rows = pl.ds(pl.multiple_of(j * tk, tk), tk)
            keep = _causal_keep(i, j, tq, tk) if masked else None
            bias = [cumT_ref[h:h + 1, :] - cum_ref[rows, h:h + 1] for h in range(N_HEADS)]
            qk = [_mm(k_ref[rows, hs], qT_ref[hs, :]) + bias[h] for h, hs in enumerate(heads)]
            for h, hs in enumerate(heads):
                sT = qk[h]
                if masked:
                    sT = jnp.where(keep, sT, NEG)
                m_old = m_s[h:h + 1, :]
                m_new = jnp.maximum(m_old, jnp.max(sT, axis=0, keepdims=True))
                alpha = jnp.exp(m_old - m_new)
                pT = jnp.exp(sT - m_new)
                l_s[h:h + 1, :] = alpha * l_s[h:h + 1, :] + jnp.sum(pT, axis=0, keepdims=True)
                p_hi, p_lo = _split_hi_lo(pT)
                vh = vT_ref[hs, rows]
                acc_s[hs, :] = alpha * acc_s[hs, :] + (_mm(vh, p_hi) + _mm(vh, p_lo))
                m_s[h:h + 1, :] = m_new

        def unmasked(j, c):
            kblock(j, False)
            return c

        lax.fori_loop(0, kpq * i, unmasked, 0)
        for d in range(kpq):
            kblock(kpq * i + d, True)
        for h, hs in enumerate(heads):
            acc_s[hs, :] = acc_s[hs, :] / l_s[h:h + 1, :]
        o_ref[...] = acc_s[...].T
        lseT_ref[...] = m_s[...] + jnp.log(l_s[...])

    full = lambda a, b: pl.BlockSpec((a, b), lambda i: (0, 0))
    colblk = lambda r: pl.BlockSpec((r, tq), lambda i: (0, i))
    return _pallas_call(
        body, name=name, grid=(T // tq,),
        in_specs=[colblk(D_ATTN), full(T, D_ATTN), full(D_ATTN, T), full(T, LANES), colblk(N_HEADS)],
        out_specs=[pl.BlockSpec((tq, D_ATTN), lambda i: (i, 0)), colblk(N_HEADS)],
        out_shape=[jax.ShapeDtypeStruct((T, D_ATTN), F32), jax.ShapeDtypeStruct((N_HEADS, T), F32)],
        scratch_shapes=[pltpu.VMEM((D_ATTN, tq), F32), pltpu.VMEM((N_HEADS, tq), F32),
                        pltpu.VMEM((N_HEADS, tq), F32)],
        compiler_params=_params(40, 1),
    )(qT, k, vT, cum, cumT)


def _attn_bwd(qT, k, kT, v, doT, lseT, deltaT, cum, cumT, name, after=()):
    T = k.shape[0]
    tq = _tile(T, 256)
    tk = _tile(tq, 256)
    kpq = tq // tk
    heads = [slice(HEAD_DIM * h, HEAD_DIM * (h + 1)) for h in range(N_HEADS)]

    def body(qT_ref, k_ref, kT_ref, v_ref, doT_ref, lseT_ref, dlT_ref, cum_ref, cumT_ref, *rest):
        dq_ref, dk_ref, dv_ref, dcum_ref, dq_s = rest[len(after):]
        i = pl.program_id(0)

        @pl.when(i == 0)
        def _():
            dk_ref[...] = jnp.zeros_like(dk_ref)
            dv_ref[...] = jnp.zeros_like(dv_ref)
            dcum_ref[...] = jnp.zeros_like(dcum_ref)

        dq_s[...] = jnp.zeros_like(dq_s)

        def kblock(j, masked):
            rows = pl.ds(pl.multiple_of(j * tk, tk), tk)
            keep = _causal_keep(i, j, tq, tk) if masked else None
            bias = [cumT_ref[h:h + 1, :] - cum_ref[rows, h:h + 1] for h in range(N_HEADS)]
            qk = [_mm(k_ref[rows, hs], qT_ref[hs, :]) + bias[h] for h, hs in enumerate(heads)]
            dps = [_mm(v_ref[rows, hs], doT_ref[hs, :]) for hs in heads]
            for h, hs in enumerate(heads):
                sT = qk[h]
                if masked:
                    sT = jnp.where(keep, sT, NEG)
                pT = jnp.exp(sT - lseT_ref[h:h + 1, :])
                dsT = pT * (dps[h] - dlT_ref[h:h + 1, :])
                dcum_ref[rows, h:h + 1] += -jnp.sum(dsT, axis=1, keepdims=True)
                dsb = dsT.astype(MXU)
                dv_ref[hs, rows] += _mm_nt(doT_ref[hs, :], pT.astype(MXU))
                dk_ref[hs, rows] += _mm_nt(qT_ref[hs, :], dsb)
                dq_s[hs, :] += _mm(kT_ref[hs, rows], dsb)

        def unmasked(j, c):
            kblock(j, False)
            return c

        lax.fori_loop(0, kpq * i, unmasked, 0)
        for d in range(kpq):
            kblock(kpq * i + d, True)
        dq_ref[...] = (dq_s[...] * SCALE).astype(dq_ref.dtype)

    full = lambda a, b: pl.BlockSpec((a, b), lambda i: (0, 0))
    colblk = lambda r: pl.BlockSpec((r, tq), lambda i: (0, i))
    return _pallas_call(
        body, name=name, grid=(T // tq,),
        in_specs=[colblk(D_ATTN), full(T, D_ATTN), full(D_ATTN, T), full(T, D_ATTN), colblk(D_ATTN),
                  colblk(N_HEADS), colblk(N_HEADS), full(T, LANES), colblk(N_HEADS)] + [_UNREAD] * len(after),
        out_specs=[colblk(D_ATTN), full(D_ATTN, T), full(D_ATTN, T), full(T, LANES)],
        out_shape=[
            jax.ShapeDtypeStruct((D_ATTN, T), MXU),
            jax.ShapeDtypeStruct((D_ATTN, T), F32),
            jax.ShapeDtypeStruct((D_ATTN, T), F32),
            jax.ShapeDtypeStruct((T, LANES), F32),
        ],
        scratch_shapes=[pltpu.VMEM((D_ATTN, tq), F32)],
        compiler_params=_params(48, 1),
    )(qT, k, kT, v, doT, lseT, deltaT, cum, cumT, *after)


_ROWS_PER_CHUNK = 64


def _glu_halo(ag_ref, agh_ref, uext_s, tm, first):
    a = ag_ref[:, :D_CONV]
    sg = jax.nn.sigmoid(ag_ref[:, D_CONV:])
    uh = agh_ref[:, :D_CONV] * jax.nn.sigmoid(agh_ref[:, D_CONV:])
    uext_s[0:CONV_HALO, :] = jnp.where(first, 0.0, uh)
    uext_s[CONV_HALO:CONV_HALO + tm, :] = a * sg
    return a, sg


_SUBLANES = 8


def _shifted_copies(ext_s, sh_s, rows):
    for k in range(1, _SUBLANES):
        sh_s[k, 0:rows, :] = ext_s[pl.ds(k, rows), :]


def _window(ext_s, sh_s, start, rows):
    k = start % _SUBLANES
    if k == 0:
        return ext_s[pl.ds(start, rows), :]
    return sh_s[k, pl.ds(start - k, rows), :]


def _layer_norm_stats(y):
    mu = jnp.mean(y, axis=-1, keepdims=True)
    xc = y - mu
    rs = lax.rsqrt(jnp.mean(xc * xc, axis=-1, keepdims=True) + EPS)
    return xc * rs, rs


def _conv_fwd(ag, w32, cb, lg, lb, name):
    T = ag.shape[0]
    tm = _tile(T, 512)
    rc = _tile(tm, _ROWS_PER_CHUNK)
    hb = tm // CONV_HALO

    def body(ag_ref, agh_ref, w_ref, cb_ref, lg_ref, lb_ref, yc_ref, c_ref, uext_s, ush_s):
        i = pl.program_id(0)
        _glu_halo(ag_ref, agh_ref, uext_s, tm, i == 0)
        _shifted_copies(uext_s, ush_s, tm + CONV_HALO - _SUBLANES)
        for r0 in range(0, tm, rc):
            acc = jnp.zeros((rc, D_CONV), F32)
            for t in range(CONV_TAPS):
                acc = acc + _window(uext_s, ush_s, r0 + CONV_HALO - (CONV_TAPS - 1) + t, rc) * w_ref[t:t + 1, :]
            y = acc + cb_ref[...]
            yc_ref[r0:r0 + rc, :] = y
            n, _ = _layer_norm_stats(y)
            z = n * lg_ref[...] + lb_ref[...]
            c_ref[r0:r0 + rc, :] = z * jax.nn.sigmoid(z)

    row = lambda w: pl.BlockSpec((tm, w), lambda i: (i, 0))
    full = lambda a, b: pl.BlockSpec((a, b), lambda i: (0, 0))
    return _pallas_call(
        body, name=name, grid=(T // tm,),
        in_specs=[row(2 * D_CONV),
                  pl.BlockSpec((CONV_HALO, 2 * D_CONV), lambda i: (jnp.maximum(i * hb - 1, 0), 0)),
                  full(CONV_HALO, D_CONV), full(1, D_CONV), full(1, D_CONV), full(1, D_CONV)],
        out_specs=[row(D_CONV), row(D_CONV)],
        out_shape=[jax.ShapeDtypeStruct((T, D_CONV), F32), jax.ShapeDtypeStruct((T, D_CONV), F32)],
        scratch_shapes=[pltpu.VMEM((CONV_HALO + tm, D_CONV), F32),
                        pltpu.VMEM((_SUBLANES, CONV_HALO + tm, D_CONV), F32)],
        compiler_params=_params(32, 1),
    )(ag, ag, w32, cb, lg, lb)


def _conv_bwd(dc, yc, ag, w32, lg, lb, name):
    T = ag.shape[0]
    tm = _tile(T, 512)
    rc = _tile(tm, _ROWS_PER_CHUNK)
    I = T // tm
    hb = tm // CONV_HALO
    n_halo_blocks = T // CONV_HALO

    def body(dc_ref, yc_ref, dch_ref, ych_ref, ag_ref, agh_ref, w_ref, lg_ref, lb_ref,
             dag_ref, dw_ref, dcb_ref, dlg_ref, dlb_ref, uext_s, dext_s, ush_s, dsh_s):
        i = pl.program_id(0)
        lgv = lg_ref[...]
        lbv = lb_ref[...]

        def ln_bwd(dcv, ycv):
            n, rs = _layer_norm_stats(ycv)
            z = n * lgv + lbv
            dz = dcv * _silu_grad(z, jax.nn.sigmoid(z))
            dn = dz * lgv
            dy = rs * (dn - jnp.mean(dn, axis=-1, keepdims=True) - n * jnp.mean(dn * n, axis=-1, keepdims=True))
            return dy, dz, n

        dy, dz, n = ln_bwd(dc_ref[...], yc_ref[...])
        dyh, _, _ = ln_bwd(dch_ref[...], ych_ref[...])
        dext_s[0:tm, :] = dy
        dext_s[tm:tm + CONV_HALO, :] = jnp.where(i == I - 1, 0.0, dyh)
        a, sg = _glu_halo(ag_ref, agh_ref, uext_s, tm, i == 0)
        _shifted_copies(uext_s, ush_s, tm + CONV_HALO - _SUBLANES)
        _shifted_copies(dext_s, dsh_s, tm + CONV_HALO - _SUBLANES)

        @pl.when(i == 0)
        def _():
            dw_ref[...] = jnp.zeros_like(dw_ref)
            dcb_ref[...] = jnp.zeros_like(dcb_ref)
            dlg_ref[...] = jnp.zeros_like(dlg_ref)
            dlb_ref[...] = jnp.zeros_like(dlb_ref)

        dcb_ref[...] += jnp.sum(dy, axis=0, keepdims=True)
        dlg_ref[...] += jnp.sum(dz * n, axis=0, keepdims=True)
        dlb_ref[...] += jnp.sum(dz, axis=0, keepdims=True)
        for t in range(CONV_TAPS):
            u_t = _window(uext_s, ush_s, CONV_HALO - (CONV_TAPS - 1) + t, tm)
            dw_ref[t:t + 1, :] += jnp.sum(dy * u_t, axis=0, keepdims=True)
        for r0 in range(0, tm, rc):
            acc = jnp.zeros((rc, D_CONV), F32)
            for t in range(CONV_TAPS):
                acc = acc + _window(dext_s, dsh_s, r0 + (CONV_TAPS - 1) - t, rc) * w_ref[t:t + 1, :]
            a_c = a[r0:r0 + rc, :]
            sg_c = sg[r0:r0 + rc, :]
            dag_ref[r0:r0 + rc, :D_CONV] = (acc * sg_c).astype(dag_ref.dtype)
            dag_ref[r0:r0 + rc, D_CONV:] = (acc * a_c * sg_c * (1.0 - sg_c)).astype(dag_ref.dtype)

    row = lambda w: pl.BlockSpec((tm, w), lambda i: (i, 0))
    full = lambda a, b: pl.BlockSpec((a, b), lambda i: (0, 0))
    nxt = pl.BlockSpec((CONV_HALO, D_CONV), lambda i: (jnp.minimum((i + 1) * hb, n_halo_blocks - 1), 0))
    return _pallas_call(
        body, name=name, grid=(I,),
        in_specs=[row(D_CONV), row(D_CONV), nxt, nxt, row(2 * D_CONV),
                  pl.BlockSpec((CONV_HALO, 2 * D_CONV), lambda i: (jnp.maximum(i * hb - 1, 0), 0)),
                  full(CONV_HALO, D_CONV), full(1, D_CONV), full(1, D_CONV)],
        out_specs=[row(2 * D_CONV), full(CONV_HALO, D_CONV), full(1, D_CONV), full(1, D_CONV), full(1, D_CONV)],
        out_shape=[
            jax.ShapeDtypeStruct((T, 2 * D_CONV), MXU),
            jax.ShapeDtypeStruct((CONV_HALO, D_CONV), F32),
            jax.ShapeDtypeStruct((1, D_CONV), F32),
            jax.ShapeDtypeStruct((1, D_CONV), F32),
            jax.ShapeDtypeStruct((1, D_CONV), F32),
        ],
        scratch_shapes=[pltpu.VMEM((CONV_HALO + tm, D_CONV), F32), pltpu.VMEM((tm + CONV_HALO, D_CONV), F32),
                        pltpu.VMEM((_SUBLANES, CONV_HALO + tm, D_CONV), F32),
                        pltpu.VMEM((_SUBLANES, CONV_HALO + tm, D_CONV), F32)],
        compiler_params=_params(40, 1),
    )(dc, yc, dc, yc, ag, ag, w32, lg, lb)


def _outproj_fwd(x1, c, o, gc, ga, wout, name):
    T, D = x1.shape
    tm = _tile(T, 1024)

    def body(x_ref, c_ref, o_ref, gc_ref, ga_ref, w_ref, x2_ref):
        yc, _ = _rms_fwd(c_ref[...], gc_ref[...])
        ya, _ = _rms_fwd(o_ref[...], ga_ref[...])
        x2_ref[...] = (x_ref[...] + _mm(yc.astype(MXU), w_ref[:D_CONV, :])
                       + _mm(ya.astype(MXU), w_ref[D_CONV:, :]))

    row = lambda w: pl.BlockSpec((tm, w), lambda i: (i, 0))
    full = lambda a, b: pl.BlockSpec((a, b), lambda i: (0, 0))
    return _pallas_call(
        body, name=name, grid=(T // tm,),
        in_specs=[row(D), row(D_CONV), row(D_ATTN), full(1, D_CONV), full(1, D_ATTN), full(D_CONV + D_ATTN, D)],
        out_specs=row(D),
        out_shape=jax.ShapeDtypeStruct((T, D), F32),
        compiler_params=_params(56, 1),
    )(x1, c, o, gc, ga, wout)


def _outproj_bwd(dx2, c, o, gc, ga, wout, name):
    T, D = dx2.shape
    tm = _tile(T, 1024)
    I = T // tm

    def body(dx_ref, c_ref, o_ref, gc_ref, ga_ref, w_ref,
             dc_ref, doT_ref, dlT_ref, dw_ref, dgc_ref, dga_ref, acc_s):
        i = pl.program_id(0)
        dxb = dx_ref[...].astype(MXU)
        cv = c_ref[...]
        ov = o_ref[...]
        yc, rcn = _rms_fwd(cv, gc_ref[...])
        ya, ra = _rms_fwd(ov, ga_ref[...])
        dyc = _mm_nt(dxb, w_ref[:D_CONV, :])
        dya = _mm_nt(dxb, w_ref[D_CONV:, :])
        dwc = _mm_tn(yc.astype(MXU), dxb)
        dwa = _mm_tn(ya.astype(MXU), dxb)
        dcv, dgc = _rms_bwd(cv, rcn, gc_ref[...], dyc)
        dov, dga = _rms_bwd(ov, ra, ga_ref[...], dya)
        dc_ref[...] = dcv
        dob = dov.astype(doT_ref.dtype)
        doT_ref[...] = dov.T.astype(doT_ref.dtype)
        chan = lax.broadcasted_iota(jnp.int32, (D_ATTN, LANES), 0)
        head = lax.broadcasted_iota(jnp.int32, (D_ATTN, LANES), 1)
        in_head = ((chan >= head * HEAD_DIM) & (chan < (head + 1) * HEAD_DIM)).astype(jnp.bfloat16)
        dlT_ref[...] = _exact_dot_01(dob.astype(F32) * ov, in_head).T[:N_HEADS, :]

        @pl.when(i == 0)
        def _():
            acc_s[:D_CONV, :] = dwc
            acc_s[D_CONV:, :] = dwa
            dgc_ref[...] = dgc
            dga_ref[...] = dga

        @pl.when(i > 0)
        def _():
            acc_s[:D_CONV, :] += dwc
            acc_s[D_CONV:, :] += dwa
            dgc_ref[...] += dgc
            dga_ref[...] += dga

        @pl.when(i == I - 1)
        def _():
            dw_ref[...] = acc_s[...].astype(dw_ref.dtype)

    row = lambda w: pl.BlockSpec((tm, w), lambda i: (i, 0))
    full = lambda a, b: pl.BlockSpec((a, b), lambda i: (0, 0))
    return _pallas_call(
        body, name=name, grid=(I,),
        in_specs=[row(D), row(D_CONV), row(D_ATTN), full(1, D_CONV), full(1, D_ATTN), full(D_CONV + D_ATTN, D)],
        out_specs=[row(D_CONV), pl.BlockSpec((D_ATTN, tm), lambda i: (0, i)),
                   pl.BlockSpec((N_HEADS, tm), lambda i: (0, i)),
                   full(D_CONV + D_ATTN, D), full(1, D_CONV), full(1, D_ATTN)],
        out_shape=[
            jax.ShapeDtypeStruct((T, D_CONV), F32),
            jax.ShapeDtypeStruct((D_ATTN, T), MXU),
            jax.ShapeDtypeStruct((N_HEADS, T), F32),
            jax.ShapeDtypeStruct((D_CONV + D_ATTN, D), MXU),
            jax.ShapeDtypeStruct((1, D_CONV), F32),
            jax.ShapeDtypeStruct((1, D_ATTN), F32),
        ],
        scratch_shapes=[pltpu.VMEM((D_CONV + D_ATTN, D), F32)],
        compiler_params=_params(56, 1),
    )(dx2, c, o, gc, ga, wout)


def _ffn_down_loss(x, A, w2, gf, target, name):
    T, D = x.shape
    J, _, bf = A.shape
    tm = _tile(T, 512)

    def body(x_ref, A_ref, w2_ref, g_ref, t_ref, loss_ref, dx_ref, dg_ref):
        i = pl.program_id(0)
        f = _mm(A_ref[0], w2_ref[0:bf, :])
        for j in range(1, J):
            f = f + _mm(A_ref[j], w2_ref[j * bf:(j + 1) * bf, :])
        xv = x_ref[...] + 0.5 * f
        gv = g_ref[...]
        out, r = _rms_fwd(xv, gv)
        err = out - t_ref[...]
        part = jnp.full((1, LANES), 0.5 / D, F32) * jnp.sum(err * err)
        dxn, dgp = _rms_bwd(xv, r, gv, err * (1.0 / D))
        dx_ref[...] = dxn

        @pl.when(i == 0)
        def _():
            loss_ref[...] = part
            dg_ref[...] = dgp

        @pl.when(i > 0)
        def _():
            loss_ref[...] += part
            dg_ref[...] += dgp

    row = lambda w: pl.BlockSpec((tm, w), lambda i: (i, 0))
    full = lambda a, b: pl.BlockSpec((a, b), lambda i: (0, 0))
    return _pallas_call(
        body, name=name, grid=(T // tm,),
        in_specs=[row(D), pl.BlockSpec((J, tm, bf), lambda i: (0, i, 0)), full(J * bf, D), full(1, D), row(D)],
        out_specs=[full(1, LANES), row(D), full(1, D)],
        out_shape=[jax.ShapeDtypeStruct((1, LANES), F32), jax.ShapeDtypeStruct((T, D), F32),
                   jax.ShapeDtypeStruct((1, D), F32)],
        compiler_params=_params(56, 1),
    )(x, A, w2, gf, target)


def _row_tile(rows):
    for cand in (256, 176, 128, 64, 32, 16):
        if rows % cand == 0:
            return cand
    return rows


def _adamw(w, m, v, parts, name):
    R, C = w.shape
    P = parts.shape[0]
    tr = _row_tile(R)
    c1 = 1.0 - ADAM_B1 ** ADAM_STEP
    c2 = 1.0 - ADAM_B2 ** ADAM_STEP

    def body(w_ref, m_ref, v_ref, p_ref, g_ref, d_ref, nm_ref, nv_ref):
        g = p_ref[0].astype(F32)
        for s in range(1, P):
            g = g + p_ref[s].astype(F32)
        wv = w_ref[...]
        mn = ADAM_B1 * m_ref[...] + (1.0 - ADAM_B1) * g
        vn = ADAM_B2 * v_ref[...] + (1.0 - ADAM_B2) * (g * g)
        g_ref[...] = g
        nm_ref[...] = mn
        nv_ref[...] = vn
        d_ref[...] = -ADAM_LR * ((mn / c1) / (jnp.sqrt(vn / c2) + ADAM_EPS) + ADAM_WD * wv)

    blk = pl.BlockSpec((tr, C), lambda i: (i, 0))
    out = jax.ShapeDtypeStruct((R, C), F32)
    return _pallas_call(
        body, name=name, grid=(R // tr,),
        in_specs=[blk, blk, blk, pl.BlockSpec((P, tr, C), lambda i: (0, i, 0))],
        out_specs=[blk, blk, blk, blk],
        out_shape=[out, out, out, out],
        compiler_params=_params(32, 1),
    )(w, m, v, parts)


def _position():
    return lax.axis_index("x"), lax.axis_index("y"), lax.axis_index("c")


def _flat(px, py, pc):
    return 4 * px + 2 * py + pc


def _row_halves(rows, dtype):
    tile = _SUBLANES * (4 // jnp.dtype(dtype).itemsize)
    half = rows // 2 // tile * tile
    assert half > 0
    return (0, half), (half, rows - half)


def _gather_body(ins, outs, send_sems, recv_sems, local_sems, handshake):
    n = len(ins)
    x, y, c = _position()
    me, sibling = (x, y, c), (x, y, 1 - c)
    across_x, across_y, diagonal = (1 - x, y), (x, 1 - y), (1 - x, 1 - y)
    if handshake:
        _handshake([sibling] + [(*chip, cc) for chip in (across_x, across_y, diagonal) for cc in (c, 1 - c)])

    def copy(a, k, block, to, rows=None, src=None):
        dst = outs[a].at[_flat(*block)]
        if rows is not None:
            dst = dst.at[pl.ds(*rows)]
        return pltpu.make_async_remote_copy(
            src_ref=dst if src is None else src, dst_ref=dst,
            send_sem=send_sems.at[a, k], recv_sem=recv_sems.at[a, k],
            device_id=to, device_id_type=MESH)

    halves = [_row_halves(ins[a].shape[0], ins[a].dtype) for a in range(n)]
    mine = [pltpu.make_async_copy(ins[a], outs[a].at[_flat(*me)], local_sems.at[a]) for a in range(n)]
    for cp in mine:
        cp.start()
    sent = []

    def start(cp):
        cp.start()
        sent.append(cp)

    for a in range(n):
        start(copy(a, 0, me, sibling, src=ins[a]))
        start(copy(a, 1, me, (*across_x, c), src=ins[a]))
        start(copy(a, 2, me, (*across_y, c), src=ins[a]))
    for a in range(n):
        low, high = halves[a]
        copy(a, 1, (*across_x, c), me).wait_recv()
        start(copy(a, 3, (*across_x, c), (*across_y, c), rows=low))
        start(copy(a, 5, (*across_x, c), sibling))
        copy(a, 2, (*across_y, c), me).wait_recv()
        start(copy(a, 4, (*across_y, c), (*across_x, c), rows=high))
        start(copy(a, 6, (*across_y, c), sibling))
    for a in range(n):
        low, high = halves[a]
        copy(a, 3, (*diagonal, c), me, rows=low).wait_recv()
        start(copy(a, 7, (*diagonal, c), sibling, rows=low))
        copy(a, 4, (*diagonal, c), me, rows=high).wait_recv()
        start(copy(a, 8, (*diagonal, c), sibling, rows=high))
    for a in range(n):
        low, high = halves[a]
        copy(a, 0, sibling, me).wait_recv()
        copy(a, 5, (*across_x, 1 - c), me).wait_recv()
        copy(a, 6, (*across_y, 1 - c), me).wait_recv()
        copy(a, 7, (*diagonal, 1 - c), me, rows=low).wait_recv()
        copy(a, 8, (*diagonal, 1 - c), me, rows=high).wait_recv()
    for cp in sent:
        cp.wait_send()
    for cp in mine:
        cp.wait()


_GATHER_SLOTS = 9


def _gather_scratch(n):
    return [pltpu.SemaphoreType.DMA((n, _GATHER_SLOTS)), pltpu.SemaphoreType.DMA((n, _GATHER_SLOTS)),
            pltpu.SemaphoreType.DMA((n,))]


def _all_gather(shards, name):
    n = len(shards)

    def body(*refs):
        _gather_body(refs[:n], refs[n:2 * n], *refs[2 * n:], handshake=False)

    hbm = pl.BlockSpec(memory_space=pltpu.HBM)
    return _pallas_call(
        body, name=name,
        in_specs=[hbm] * n, out_specs=[hbm] * n,
        out_shape=[jax.ShapeDtypeStruct((N_DEV,) + s.shape, s.dtype) for s in shards],
        scratch_shapes=_gather_scratch(n),
    )(*shards)


def _handshake(peers):
    barrier = pltpu.get_barrier_semaphore()
    for peer in peers:
        pl.semaphore_signal(barrier, inc=1, device_id=peer, device_id_type=MESH)
    pl.semaphore_wait(barrier, len(peers))


def _sequencer_call(body, name, collective_id, out_type, scratch_types, operands):
    return pl.kernel(
        body, name=name, out_type=out_type,
        mesh=plsc.ScalarSubcoreMesh(axis_name="sequencer", num_cores=1),
        scratch_types=scratch_types,
        compiler_params=pltpu.CompilerParams(collective_id=collective_id),
    )(*operands)


def _seq_all_gather(shards, name, collective_id, after):
    n = len(shards)

    def body(*refs):
        _gather_body(refs[:n], refs[n + 1:2 * n + 1], *refs[2 * n + 1:], handshake=True)

    return _sequencer_call(
        body, name, collective_id,
        [jax.ShapeDtypeStruct((N_DEV,) + s.shape, s.dtype) for s in shards],
        _gather_scratch(n), list(shards) + [after])


def _seq_to_sibling(parts, name, collective_id, after):
    n = len(parts)

    def body(*refs):
        ins, outs = refs[:n], refs[n + len(after):2 * n + len(after)]
        send_sems, recv_sems = refs[2 * n + len(after):]
        x, y, c = _position()
        sibling = (x, y, 1 - c)
        _handshake([sibling])
        sent = []
        for a in range(n):
            for q in range(N_CHIPS):
                cp = pltpu.make_async_remote_copy(
                    src_ref=ins[a].at[2 * q + (1 - c)], dst_ref=outs[a].at[q],
                    send_sem=send_sems.at[a, q], recv_sem=recv_sems.at[a, q],
                    device_id=sibling, device_id_type=MESH)
                cp.start()
                sent.append(cp)
        for cp in sent:
            cp.wait_recv()
        for cp in sent:
            cp.wait_send()

    return _sequencer_call(
        body, name, collective_id,
        [jax.ShapeDtypeStruct((N_CHIPS,) + p.shape[1:], p.dtype) for p in parts],
        [pltpu.SemaphoreType.DMA((n, N_CHIPS)), pltpu.SemaphoreType.DMA((n, N_CHIPS))],
        list(parts) + list(after))


def _seq_to_chips(partials, name, collective_id):
    n = len(partials)

    def body(*refs):
        ins, outs = refs[:n], refs[n:2 * n]
        send_sems, recv_sems, local_sems = refs[2 * n:]
        x, y, c = _position()
        my_chip = 2 * x + y
        chips = [(1 - x, y), (x, 1 - y), (1 - x, 1 - y)]
        _handshake([(*chip, c) for chip in chips])
        mine = [pltpu.make_async_copy(ins[a].at[my_chip], outs[a].at[my_chip], local_sems.at[a]) for a in range(n)]
        for cp in mine:
            cp.start()
        sent = []
        for a in range(n):
            for j, (px, py) in enumerate(chips):
                cp = pltpu.make_async_remote_copy(
                    src_ref=ins[a].at[2 * px + py], dst_ref=outs[a].at[my_chip],
                    send_sem=send_sems.at[a, j], recv_sem=recv_sems.at[a, j],
                    device_id=(px, py, c), device_id_type=MESH)
                cp.start()
                sent.append(cp)
        for cp in sent:
            cp.wait_recv()
        for cp in sent:
            cp.wait_send()
        for cp in mine:
            cp.wait()

    return _sequencer_call(
        body, name, collective_id,
        [jax.ShapeDtypeStruct(p.shape, p.dtype) for p in partials],
        [pltpu.SemaphoreType.DMA((n, 3)), pltpu.SemaphoreType.DMA((n, 3)), pltpu.SemaphoreType.DMA((n,))],
        list(partials))


def _pair_add(parts, recvs, name, after=()):
    n = len(parts)
    core = lax.axis_index("c").astype(jnp.int32).reshape(1)

    def body(c_ref, *refs):
        ps, rs, outs = refs[:n], refs[n:2 * n], refs[2 * n + len(after):]
        for p_ref, r_ref, o_ref in zip(ps, rs, outs):
            o_ref[...] = (p_ref[...].astype(F32) + r_ref[...].astype(F32)).astype(o_ref.dtype)

    mine = lambda p: pl.BlockSpec((None,) + p.shape[1:], lambda q, c: (2 * q + c[0], 0, 0))
    blk = lambda p: pl.BlockSpec((None,) + p.shape[1:], lambda q, c: (q, 0, 0))
    return pl.pallas_call(
        body, name=name,
        grid_spec=pltpu.PrefetchScalarGridSpec(
            num_scalar_prefetch=1, grid=(N_CHIPS,),
            in_specs=[mine(p) for p in parts] + [blk(p) for p in parts] + [_UNREAD] * len(after),
            out_specs=[blk(p) for p in parts]),
        out_shape=[pltpu.HBM((N_CHIPS,) + p.shape[1:], p.dtype) for p in parts],
        compiler_params=_params(40, 1),
    )(core, *[pltpu.with_memory_space_constraint(a, pltpu.HBM) for a in (*parts, *recvs, *after)])


class _Reduced(NamedTuple):
    partials: list
    reduced: list


def _blocks(g):
    return g.reshape(N_DEV, -1, g.shape[-1])


def _reduce_scatter(parts, tag, ids, after=(), add_after=()):
    from_sibling = _seq_to_sibling(parts, "rs_sibling_" + tag, ids[0], after)
    partials = _pair_add(parts, from_sibling, "rs_add_" + tag, add_after)
    return _Reduced(partials, _seq_to_chips(partials, "rs_chips_" + tag, ids[1]))


_SMALL = ("ffn1_norm", "mix_norm", "conv_b", "conv_ln_g", "conv_ln_b", "forget_b", "out_norm_conv",
          "out_norm_attn", "ffn2_norm", "final_norm")
_PACK_WIDTH = 2 * D_CONV
_SLOT = dict(ffn1_norm=(0, 0), mix_norm=(1, 0), ffn2_norm=(2, 0), final_norm=(3, 0), conv_b=(4, 0),
             conv_ln_g=(4, D_CONV), conv_ln_b=(5, 0), out_norm_conv=(5, D_CONV), out_norm_attn=(6, 0),
             forget_b=(6, D_CONV))
_LOSS_ROW = 7
_CONV_ROW0 = 8
_PACK_ROWS = _CONV_ROW0 + CONV_HALO


def _pack_small(small, name):
    arrays = [small[n] for n in _SMALL] + [small["conv_w"], small["loss"]]

    def body(*refs):
        out = refs[-1]
        out[...] = jnp.zeros_like(out)
        for n, ref in zip(_SMALL, refs):
            row, lane = _SLOT[n]
            out[row:row + 1, lane:lane + ref.shape[1]] = ref[...]
        out[_CONV_ROW0:, :D_CONV] = refs[len(_SMALL)][...]
        out[_LOSS_ROW:_LOSS_ROW + 1, :LANES] = refs[len(_SMALL) + 1][...]

    return _pallas_call(body, name=name, out_shape=jax.ShapeDtypeStruct((_PACK_ROWS, _PACK_WIDTH), F32))(*arrays)


def _adamw_small(gathered, w, m, v, name):
    c1 = 1.0 - ADAM_B1 ** ADAM_STEP
    c2 = 1.0 - ADAM_B2 ** ADAM_STEP
    k = len(_SMALL)

    def body(g_ref, *refs):
        ws, ms, vs = refs[:k], refs[k:2 * k], refs[2 * k:3 * k]
        outs = refs[3 * k:]
        total = g_ref[0]
        for s in range(1, N_DEV):
            total = total + g_ref[s]
        for i, n in enumerate(_SMALL):
            row, lane = _SLOT[n]
            width = ws[i].shape[1]
            g = total[row:row + 1, lane:lane + width]
            mn = ADAM_B1 * ms[i][...] + (1.0 - ADAM_B1) * g
            vn = ADAM_B2 * vs[i][...] + (1.0 - ADAM_B2) * (g * g)
            o_g, o_d, o_m, o_v = outs[4 * i:4 * i + 4]
            o_g[...] = g
            o_m[...] = mn
            o_v[...] = vn
            o_d[...] = -ADAM_LR * ((mn / c1) / (jnp.sqrt(vn / c2) + ADAM_EPS) + ADAM_WD * ws[i][...])
        outs[4 * k][...] = total[_CONV_ROW0:, :D_CONV]
        outs[4 * k + 1][...] = total[_LOSS_ROW:_LOSS_ROW + 1, :LANES]

    shapes = []
    for n in _SMALL:
        shapes += [jax.ShapeDtypeStruct(w[n].shape, F32)] * 4
    shapes.append(jax.ShapeDtypeStruct((CONV_HALO, D_CONV), F32))
    shapes.append(jax.ShapeDtypeStruct((1, LANES), F32))
    res = _pallas_call(body, name=name, out_shape=shapes)(
        gathered, *[w[n] for n in _SMALL], *[m[n] for n in _SMALL], *[v[n] for n in _SMALL])
    return {n: res[4 * i:4 * i + 4] for i, n in enumerate(_SMALL)}, res[4 * k], res[4 * k + 1]


def _local_step(x, target, norms, shard):
    D = x.shape[1]
    J = N_DEV // 2
    as13 = lambda g: g.reshape(2, J, g.shape[1], D)

    (g13_1,) = _all_gather([shard["ffn1_w13"]], "gather_ffn1_w13")
    (g2_1,) = _seq_all_gather([shard["ffn1_w2"]], "gather_ffn1_w2", 10, after=g13_1)
    w13_1 = as13(g13_1)
    G1, U1, A1 = _ffn_up(x, norms["ffn1_norm"], w13_1, "ffn1_up")
    gin, gconv = _seq_all_gather([shard["w_in"], shard["conv_w"]], "gather_mix", 1, after=G1)
    w2_1 = g2_1.reshape(-1, D)
    x1 = _ffn_down(x, A1, w2_1, "ffn1_down")
    gout, g13_2, g2_2 = _seq_all_gather([shard["w_out"], shard["ffn2_w13"], shard["ffn2_w2"]], "gather_ffn2", 2,
                                        after=x1)
    wout = gout.reshape(-1, D)
    conv_w32 = jnp.pad(gconv.transpose(1, 0, 2).reshape(CONV_TAPS, D_CONV), ((0, CONV_HALO - CONV_TAPS), (0, 0)))

    ag, k, v, qT, kT, vT, fl = _inproj_fwd(x1, norms["mix_norm"], gin, "inproj_fwd")
    cum, cumT = _forget_fwd(fl, norms["forget_b"], "forget_fwd")
    yc, c = _conv_fwd(ag, conv_w32, norms["conv_b"], norms["conv_ln_g"], norms["conv_ln_b"], "conv_fwd")
    o, lseT = _attn_fwd(qT, k, vT, cum, cumT, "attn_fwd")
    x2 = _outproj_fwd(x1, c, o, norms["out_norm_conv"], norms["out_norm_attn"], wout, "outproj_fwd")
    w13_2, w2_2 = as13(g13_2), g2_2.reshape(-1, D)
    G2, U2, A2 = _ffn_up(x2, norms["ffn2_norm"], w13_2, "ffn2_up")
    loss, dx3, d_final = _ffn_down_loss(x2, A2, w2_2, norms["final_norm"], target, "ffn2_down_loss")

    dw2_2 = _ffn_w2_grad(dx3, A2, "ffn2_w2_grad")
    dx2, d_ffn2n, h3, dG2, dU2 = _ffn_bwd_act(x2, norms["ffn2_norm"], dx3, G2, U2, w13_2, w2_2, "ffn2_bwd_act")
    dw13_2 = _ffn_w13_grad(h3, dG2, dU2, "ffn2_w13_grad")
    dc, dobT, deltaT, dwout, d_onc, d_ona = _outproj_bwd(
        dx2, c, o, norms["out_norm_conv"], norms["out_norm_attn"], wout, "outproj_bwd")
    red_ffn2 = _reduce_scatter([_blocks(dw13_2), _blocks(dw2_2)], "ffn2", (3, 4), add_after=(dc,))
    dqT, dkT, dvT, dcum = _attn_bwd(qT, k, kT, v, dobT, lseT, deltaT, cum, cumT, "attn_bwd",
                                    after=red_ffn2.partials)
    dfl, d_fb = _forget_bwd(dcum, fl, norms["forget_b"], "forget_bwd")
    dag, d_convw, d_cb, d_lg, d_lb = _conv_bwd(dc, yc, ag, conv_w32, norms["conv_ln_g"], norms["conv_ln_b"], "conv_bwd")
    dx1, d_mixn, h2 = _inproj_bwd_act(x1, norms["mix_norm"], dx2, dag, dqT, dkT, dvT, dfl, gin, "inproj_bwd_act")
    dw2_1 = _ffn_w2_grad(dx1, A1, "ffn1_w2_grad")
    early = [_blocks(dwout), _blocks(dw2_1)]
    sib_early = _seq_to_sibling(early, "rs_sibling_mix_early", 11, red_ffn2.reduced[:1])
    dwin_blocks = _inproj_bwd_weights(h2, dag, dqT, dkT, dvT, dfl, "inproj_bwd_weights")
    sib_w_in = _seq_to_sibling([dwin_blocks], "rs_sibling_mix", 5, sib_early[:1])
    mix_partials = _pair_add([dwin_blocks] + early, sib_w_in + sib_early, "rs_add_mix")
    red_mix = _Reduced(mix_partials, _seq_to_chips(mix_partials, "rs_chips_mix", 6))
    dx, d_ffn1n, h1, dG1, dU1 = _ffn_bwd_act(x, norms["ffn1_norm"], dx1, G1, U1, w13_1, w2_1, "ffn1_bwd_act",
                                             after=red_mix.partials)
    small = dict(ffn1_norm=d_ffn1n, mix_norm=d_mixn, conv_b=d_cb, conv_ln_g=d_lg, conv_ln_b=d_lb,
                 forget_b=d_fb, out_norm_conv=d_onc, out_norm_attn=d_ona, ffn2_norm=d_ffn2n,
                 final_norm=d_final, conv_w=d_convw, loss=loss)
    packed_small = _pack_small(small, "pack_small_grads")
    (gathered_small,) = _seq_all_gather([packed_small], "gather_small_grads", 9, after=red_mix.partials[0])
    dw13_1 = _ffn_w13_grad(h1, dG1, dU1, "ffn1_w13_grad")
    red_w13_1 = _reduce_scatter([_blocks(dw13_1)], "ffn1_w13", (7, 8), after=[red_mix.reduced[0], gathered_small])
    big = dict(ffn1_w13=red_w13_1.reduced[0], ffn1_w2=red_mix.reduced[2], w_in=red_mix.reduced[0],
               w_out=red_mix.reduced[1], ffn2_w13=red_ffn2.reduced[0], ffn2_w2=red_ffn2.reduced[1])
    return dx, gathered_small, big


_BIG = ("ffn1_w13", "ffn1_w2", "w_in", "w_out", "ffn2_w13", "ffn2_w2")
_TRANSPOSED = ("ffn1_w13", "ffn2_w13", "w_in")
_ORDER = ("ffn1_norm", "ffn1_w13", "ffn1_w2", "mix_norm", "w_in", "conv_w", "conv_b", "conv_ln_g", "conv_ln_b",
          "forget_b", "out_norm_conv", "out_norm_attn", "w_out", "ffn2_norm", "ffn2_w13", "ffn2_w2", "final_norm")


def kernel(x, ffn1_norm, ffn1_w13, ffn1_w2, mix_norm, w_in, conv_w, conv_b, conv_ln_g, conv_ln_b, forget_b, out_norm_conv, out_norm_attn, w_out, ffn2_norm, ffn2_w13, ffn2_w2, final_norm, loss_target, m_ffn1_norm, m_ffn1_w13, m_ffn1_w2, m_mix_norm, m_w_in, m_conv_w, m_conv_b, m_conv_ln_g, m_conv_ln_b, m_forget_b, m_out_norm_conv, m_out_norm_attn, m_w_out, m_ffn2_norm, m_ffn2_w13, m_ffn2_w2, m_final_norm, v_ffn1_norm, v_ffn1_w13, v_ffn1_w2, v_mix_norm, v_w_in, v_conv_w, v_conv_b, v_conv_ln_g, v_conv_ln_b, v_forget_b, v_out_norm_conv, v_out_norm_attn, v_w_out, v_ffn2_norm, v_ffn2_w13, v_ffn2_w2, v_final_norm):
    w = dict(ffn1_norm=ffn1_norm, ffn1_w13=ffn1_w13, ffn1_w2=ffn1_w2, mix_norm=mix_norm, w_in=w_in, conv_w=conv_w,
             conv_b=conv_b, conv_ln_g=conv_ln_g, conv_ln_b=conv_ln_b, forget_b=forget_b, out_norm_conv=out_norm_conv,
             out_norm_attn=out_norm_attn, w_out=w_out, ffn2_norm=ffn2_norm, ffn2_w13=ffn2_w13, ffn2_w2=ffn2_w2,
             final_norm=final_norm)
    m = dict(ffn1_norm=m_ffn1_norm, ffn1_w13=m_ffn1_w13, ffn1_w2=m_ffn1_w2, mix_norm=m_mix_norm, w_in=m_w_in,
             conv_w=m_conv_w, conv_b=m_conv_b, conv_ln_g=m_conv_ln_g, conv_ln_b=m_conv_ln_b, forget_b=m_forget_b,
             out_norm_conv=m_out_norm_conv, out_norm_attn=m_out_norm_attn, w_out=m_w_out, ffn2_norm=m_ffn2_norm,
             ffn2_w13=m_ffn2_w13, ffn2_w2=m_ffn2_w2, final_norm=m_final_norm)
    v = dict(ffn1_norm=v_ffn1_norm, ffn1_w13=v_ffn1_w13, ffn1_w2=v_ffn1_w2, mix_norm=v_mix_norm, w_in=v_w_in,
             conv_w=v_conv_w, conv_b=v_conv_b, conv_ln_g=v_conv_ln_g, conv_ln_b=v_conv_ln_b, forget_b=v_forget_b,
             out_norm_conv=v_out_norm_conv, out_norm_attn=v_out_norm_attn, w_out=v_w_out, ffn2_norm=v_ffn2_norm,
             ffn2_w13=v_ffn2_w13, ffn2_w2=v_ffn2_w2, final_norm=v_final_norm)
    shapes = {n: a.shape for n, a in w.items()}
    T, D = x.shape[1], x.shape[2]
    def two(n, a):
        if a.ndim != 3:
            return a.reshape(1, -1)
        a = a.reshape(a.shape[-2], a.shape[-1])
        return a.T if n in _TRANSPOSED else a

    w2d = {n: two(n, a) for n, a in w.items()}
    m2d = {n: two(n, a) for n, a in m.items()}
    v2d = {n: two(n, a) for n, a in v.items()}

    shard = {n: w2d[n].astype(MXU) for n in _BIG}
    shard["conv_w"] = w2d["conv_w"]
    norms = {n: w2d[n] for n in _SMALL}
    norms["forget_b"] = jnp.pad(w2d["forget_b"], ((0, 0), (0, LANES - N_HEADS)))
    dx, gathered_small, big = _local_step(x[0], loss_target[0], norms, shard)

    grads, deltas, new_m, new_v = {}, {}, {}, {}
    for n in _BIG:
        g, d, nm, nv = _adamw(w2d[n], m2d[n], v2d[n], big[n], "adamw_" + n)
        grads[n], deltas[n], new_m[n], new_v[n] = g, d, nm, nv

    small_out, conv_g_full, loss = _adamw_small(gathered_small, w2d, m2d, v2d, "adamw_small")
    for n in _SMALL:
        grads[n], deltas[n], new_m[n], new_v[n] = small_out[n]
    conv_g_full = conv_g_full[:CONV_TAPS]
    xi, yi, ci = _position()
    cw = shapes["conv_w"][-1]
    conv_g_mine = lax.dynamic_slice_in_dim(conv_g_full, _flat(xi, yi, ci) * cw, cw, axis=1)
    g, d, nm, nv = _adamw(w2d["conv_w"], m2d["conv_w"], v2d["conv_w"], conv_g_mine[None], "adamw_conv_w")
    grads["conv_w"], deltas["conv_w"], new_m["conv_w"], new_v["conv_w"] = g, d, nm, nv

    shaped = lambda dct: [(dct[n].T if n in _TRANSPOSED else dct[n]).reshape(shapes[n]) for n in _ORDER]
    return (loss[0, 0], dx[None], *shaped(grads), *shaped(deltas), *shaped(new_m), *shaped(new_v))
```

```python
from typing import NamedTuple

import jax
import jax.numpy as jnp
from jax import lax
from jax.experimental import pallas as pl
from jax.experimental.pallas import tpu as pltpu
from jax.experimental.pallas import tpu_sc as plsc

F32 = jnp.float32
MXU = jnp.bfloat16
EPS = 1e-6
N_HEADS = 8
HEAD_DIM = 64
D_CONV = 512
D_ATTN = N_HEADS * HEAD_DIM
CONV_TAPS = 31
CONV_HALO = 32
SCALE = HEAD_DIM ** -0.5
NEG = -1e30
LANES = 128
N_DEV = 8
N_CHIPS = N_DEV // 2
MESH = pl.DeviceIdType.MESH
MIB = 1 << 20

ADAM_LR = 0.001
ADAM_B1 = 0.9
ADAM_B2 = 0.999
ADAM_EPS = 1e-08
ADAM_WD = 0.01
ADAM_STEP = 10


_UNREAD = pl.BlockSpec(memory_space=pl.ANY)


def _pallas_call(body, *, out_shape, **kwargs):
    in_hbm = lambda s: pltpu.HBM(s.shape, s.dtype)
    outs = [in_hbm(s) for s in out_shape] if isinstance(out_shape, (list, tuple)) else in_hbm(out_shape)
    call = pl.pallas_call(body, out_shape=outs, **kwargs)
    return lambda *operands: call(*[pltpu.with_memory_space_constraint(a, pltpu.HBM) for a in operands])


def _params(vmem_mib, n_axes):
    return pltpu.CompilerParams(dimension_semantics=("arbitrary",) * n_axes, vmem_limit_bytes=vmem_mib * MIB)


def _mm(a, b):
    return jnp.dot(a, b, preferred_element_type=F32)


def _mm_nt(a, b):
    return lax.dot_general(a, b, (((1,), (1,)), ((), ())), preferred_element_type=F32)


def _mm_tn(a, b):
    return lax.dot_general(a, b, (((0,), (0,)), ((), ())), preferred_element_type=F32)


def _rms_fwd(x, g):
    r = lax.rsqrt(jnp.mean(x * x, axis=-1, keepdims=True) + EPS)
    return x * r * g, r


def _rms_bwd(x, r, g, dy):
    gdy = dy * g
    dx = r * gdy - x * (r * r * r) * jnp.mean(x * gdy, axis=-1, keepdims=True)
    dg = jnp.sum(dy * x * r, axis=0, keepdims=True)
    return dx, dg


def _silu_grad(z, sz):
    return sz * (1.0 + z * (1.0 - sz))


def _three_terms(x):
    x1 = x.astype(jnp.bfloat16)
    r1 = x - x1.astype(F32)
    x2 = r1.astype(jnp.bfloat16)
    x3 = (r1 - x2.astype(F32)).astype(jnp.bfloat16)
    return x1, x2, x3


def _exact_tri_dot(tri, x):
    x1, x2, x3 = _three_terms(x)
    return _mm(tri, x1) + _mm(tri, x2) + _mm(tri, x3)


def _exact_dot_01(x, sel):
    x1, x2, x3 = _three_terms(x)
    return _mm(x1, sel) + _mm(x2, sel) + _mm(x3, sel)


def _tile(n, want):
    t = min(n, want)
    assert n % t == 0
    return t


_FFN_CHUNK = 256


def _ffn_up(x, g, w13, name):
    T, D = x.shape
    _, J, bf, _ = w13.shape
    tm = _tile(T, 1024)
    I = T // tm

    def body(x_ref, g_ref, w13_ref, G_ref, U_ref, A_ref, h_s):
        j = pl.program_id(0)
        i = pl.program_id(1)
        rows = pl.ds(pl.multiple_of(i * tm, tm), tm)

        @pl.when(j == 0)
        def _():
            h, _ = _rms_fwd(x_ref[...], g_ref[...])
            h_s[rows, :] = h.astype(MXU)

        chunks = [slice(r0, r0 + _FFN_CHUNK) for r0 in range(0, tm, _FFN_CHUNK)]
        hbs = [h_s[pl.ds(pl.multiple_of(i * tm + rs.start, _FFN_CHUNK), _FFN_CHUNK), :] for rs in chunks]
        GU = [(_mm_nt(hb, w13_ref[0]), _mm_nt(hb, w13_ref[1])) for hb in hbs]
        for rs, (G, U) in zip(chunks, GU):
            G_ref[rs, :] = G.astype(MXU)
            U_ref[rs, :] = U.astype(MXU)
            A_ref[rs, :] = (G * jax.nn.sigmoid(G) * U).astype(MXU)

    blk = pl.BlockSpec((None, tm, bf), lambda j, i: (j, i, 0))
    hid = jax.ShapeDtypeStruct((J, T, bf), MXU)
    return _pallas_call(
        body, name=name, grid=(J, I),
        in_specs=[pl.BlockSpec((tm, D), lambda j, i: (jnp.where(j == 0, i, I - 1), 0)),
                  pl.BlockSpec((1, D), lambda j, i: (0, 0)),
                  pl.BlockSpec((2, None, bf, D), lambda j, i: (0, j, 0, 0))],
        out_specs=[blk, blk, blk],
        out_shape=[hid, hid, hid],
        scratch_shapes=[pltpu.VMEM((T, D), MXU)],
        compiler_params=_params(48, 2),
    )(x, g, w13)


def _ffn_down(x, A, w2, name):
    T, D = x.shape
    J, _, bf = A.shape
    tm = _tile(T, 512)

    def body(x_ref, A_ref, w2_ref, xo_ref):
        f = _mm(A_ref[0], w2_ref[0:bf, :])
        for j in range(1, J):
            f = f + _mm(A_ref[j], w2_ref[j * bf:(j + 1) * bf, :])
        xo_ref[...] = x_ref[...] + 0.5 * f

    row = pl.BlockSpec((tm, D), lambda i: (i, 0))
    return _pallas_call(
        body, name=name, grid=(T // tm,),
        in_specs=[row, pl.BlockSpec((J, tm, bf), lambda i: (0, i, 0)), pl.BlockSpec((J * bf, D), lambda i: (0, 0))],
        out_specs=row,
        out_shape=jax.ShapeDtypeStruct((T, D), F32),
        compiler_params=_params(48, 1),
    )(x, A, w2)


def _ffn_bwd_act(x, g, dy, Gs, Us, w13, w2, name, after=()):
    T, D = x.shape
    _, J, bf, _ = w13.shape
    tm = _tile(T, 512)
    I = T // tm

    def body(x_ref, g_ref, dy_ref, G_ref, U_ref, w13_ref, w2_ref, *rest):
        dx_ref, dg_ref, h_ref, dG_ref, dU_ref, dh_s, dF_s, h_s = rest[len(after):]
        j = pl.program_id(0)
        i = pl.program_id(1)
        rows = pl.ds(pl.multiple_of(i * tm, tm), tm)

        @pl.when(j == 0)
        def _():
            h, _ = _rms_fwd(x_ref[...], g_ref[...])
            hb = h.astype(MXU)
            h_s[rows, :] = hb
            h_ref[...] = hb
            dF_s[rows, :] = (0.5 * dy_ref[...]).astype(MXU)
            dh_s[rows, :] = jnp.zeros((tm, D), F32)

        chunks = [slice(r0, r0 + _FFN_CHUNK) for r0 in range(0, tm, _FFN_CHUNK)]
        crows = [pl.ds(pl.multiple_of(i * tm + rs.start, _FFN_CHUNK), _FFN_CHUNK) for rs in chunks]
        dAs = [_mm_nt(dF_s[cr, :], w2_ref[...]) for cr in crows]
        for rs, cr, dA in zip(chunks, crows, dAs):
            G = G_ref[rs, :].astype(F32)
            U = U_ref[rs, :].astype(F32)
            sg = jax.nn.sigmoid(G)
            s = G * sg
            dUb = (dA * s).astype(MXU)
            dGb = (dA * U * _silu_grad(G, sg)).astype(MXU)
            dG_ref[rs, :] = dGb
            dU_ref[rs, :] = dUb
            dh_s[cr, :] += _mm(dGb, w13_ref[0]) + _mm(dUb, w13_ref[1])

        @pl.when(j == J - 1)
        def _():
            xv = x_ref[...]
            gv = g_ref[...]
            _, r = _rms_fwd(xv, gv)
            dxn, dgp = _rms_bwd(xv, r, gv, dh_s[rows, :])
            dx_ref[...] = dy_ref[...] + dxn

            @pl.when(i == 0)
            def _():
                dg_ref[...] = dgp

            @pl.when(i > 0)
            def _():
                dg_ref[...] += dgp

    ends = lambda j, i: (jnp.where((j == 0) | (j == J - 1), i, I - 1), 0)
    blk = pl.BlockSpec((None, tm, bf), lambda j, i: (j, i, 0))
    hid = jax.ShapeDtypeStruct((J, T, bf), MXU)
    return _pallas_call(
        body, name=name, grid=(J, I),
        in_specs=[pl.BlockSpec((tm, D), ends), pl.BlockSpec((1, D), lambda j, i: (0, 0)), pl.BlockSpec((tm, D), ends),
                  blk, blk, pl.BlockSpec((2, None, bf, D), lambda j, i: (0, j, 0, 0)),
                  pl.BlockSpec((bf, D), lambda j, i: (j, 0))] + [_UNREAD] * len(after),
        out_specs=[pl.BlockSpec((tm, D), lambda j, i: (jnp.where(j == J - 1, i, 0), 0)),
                   pl.BlockSpec((1, D), lambda j, i: (0, 0)),
                   pl.BlockSpec((tm, D), lambda j, i: (jnp.where(j == 0, i, I - 1), 0)), blk, blk],
        out_shape=[jax.ShapeDtypeStruct((T, D), F32), jax.ShapeDtypeStruct((1, D), F32),
                   jax.ShapeDtypeStruct((T, D), MXU), hid, hid],
        scratch_shapes=[pltpu.VMEM((T, D), F32), pltpu.VMEM((T, D), MXU), pltpu.VMEM((T, D), MXU)],
        compiler_params=_params(58, 2),
    )(x, g, dy, Gs, Us, w13, w2, *after)


def _ffn_w13_grad(h, dG, dU, name):
    T, D = h.shape
    J, _, bf = dG.shape

    def body(h_ref, dG_ref, dU_ref, dw13_ref):
        dw13_ref[0] = _mm_tn(dG_ref[...], h_ref[...]).astype(dw13_ref.dtype)
        dw13_ref[1] = _mm_tn(dU_ref[...], h_ref[...]).astype(dw13_ref.dtype)

    blk = pl.BlockSpec((None, T, bf), lambda j: (j, 0, 0))
    return _pallas_call(
        body, name=name, grid=(J,),
        in_specs=[pl.BlockSpec((T, D), lambda j: (0, 0)), blk, blk],
        out_specs=pl.BlockSpec((2, None, bf, D), lambda j: (0, j, 0, 0)),
        out_shape=jax.ShapeDtypeStruct((2, J, bf, D), MXU),
        compiler_params=_params(48, 1),
    )(h, dG, dU)


def _ffn_w2_grad(dy, A, name, after=()):
    T, D = dy.shape
    J, _, bf = A.shape
    n_chunks = 4
    rc = T // n_chunks

    def body(dy_hbm, A_ref, *rest):
        dw2_ref, dF_s, dy_s, sems = rest[len(after):]

        @pl.when(pl.program_id(0) == 0)
        def _():
            copies = [pltpu.make_async_copy(dy_hbm.at[pl.ds(c * rc, rc)], dy_s.at[pl.ds(c * rc, rc)], sems.at[c])
                      for c in range(n_chunks)]
            for cp in copies:
                cp.start()
            for c, cp in enumerate(copies):
                cp.wait()
                dF_s[c * rc:(c + 1) * rc, :] = (0.5 * dy_s[c * rc:(c + 1) * rc, :]).astype(MXU)

        dw2_ref[...] = _mm_tn(A_ref[...], dF_s[...]).astype(dw2_ref.dtype)

    return _pallas_call(
        body, name=name, grid=(J,),
        in_specs=[_UNREAD, pl.BlockSpec((None, T, bf), lambda j: (j, 0, 0))] + [_UNREAD] * len(after),
        out_specs=pl.BlockSpec((bf, D), lambda j: (j, 0)),
        out_shape=jax.ShapeDtypeStruct((J * bf, D), MXU),
        scratch_shapes=[pltpu.VMEM((T, D), MXU), pltpu.VMEM((T, D), F32), pltpu.SemaphoreType.DMA((n_chunks,))],
        compiler_params=_params(48, 1),
    )(dy, A, *after)


_AG0, _Q0, _K0, _V0, _F0 = 0, 2 * D_CONV, 2 * D_CONV + D_ATTN, 2 * D_CONV + 2 * D_ATTN, 2 * D_CONV + 3 * D_ATTN
N_IN = _F0 + N_HEADS
N_IN_PAD = _F0 + LANES
_IN_BLOCK = N_IN // N_DEV


def _rows_from_blocks(blocks_ref, rows_ref):
    for p in range(N_DEV):
        rows_ref[_IN_BLOCK * p:_IN_BLOCK * (p + 1), :] = blocks_ref[p]
    rows_ref[N_IN:, :] = jnp.zeros((N_IN_PAD - N_IN, rows_ref.shape[1]), rows_ref.dtype)


def _inproj_fwd(x1, gm, win_blocks, name):
    T, D = x1.shape
    tm = _tile(T, 512)

    def body(x_ref, g_ref, wb_ref, ag_ref, k_ref, v_ref, qT_ref, kT_ref, vT_ref, fl_ref, w_ref):
        @pl.when(pl.program_id(0) == 0)
        def _():
            _rows_from_blocks(wb_ref, w_ref)

        h, _ = _rms_fwd(x_ref[...], g_ref[...])
        hb = h.astype(MXU)
        ag_ref[...] = _mm_nt(hb, w_ref[_AG0:_Q0, :])
        qT_ref[...] = (_mm_nt(hb, w_ref[_Q0:_K0, :]) * SCALE).T.astype(MXU)
        for c0, ref, refT in ((_K0, k_ref, kT_ref), (_V0, v_ref, vT_ref)):
            y = _mm_nt(hb, w_ref[c0:c0 + D_ATTN, :])
            ref[...] = y.astype(MXU)
            refT[...] = y.T.astype(MXU)
        fl_ref[...] = _mm_nt(hb, w_ref[_F0:N_IN_PAD, :])

    row = lambda w: pl.BlockSpec((tm, w), lambda i: (i, 0))
    col = pl.BlockSpec((D_ATTN, tm), lambda i: (0, i))
    std = jax.ShapeDtypeStruct((T, D_ATTN), MXU)
    trn = jax.ShapeDtypeStruct((D_ATTN, T), MXU)
    return _pallas_call(
        body, name=name, grid=(T // tm,),
        in_specs=[row(D), pl.BlockSpec((1, D), lambda i: (0, 0)),
                  pl.BlockSpec((N_DEV, _IN_BLOCK, D), lambda i: (0, 0, 0))],
        out_specs=[row(2 * D_CONV), row(D_ATTN), row(D_ATTN), col, col, col, row(LANES)],
        out_shape=[jax.ShapeDtypeStruct((T, 2 * D_CONV), F32), std, std, trn, trn, trn,
                   jax.ShapeDtypeStruct((T, LANES), F32)],
        scratch_shapes=[pltpu.VMEM((N_IN_PAD, D), MXU)],
        compiler_params=_params(40, 1),
    )(x1, gm, win_blocks)


def _inproj_bwd_act(x1, gm, dx2, dag, dqT, dkT, dvT, dfl, win_blocks, name):
    T, D = x1.shape
    tm = _tile(T, 512)

    def body(x_ref, g_ref, dx2_ref, dag_ref, dqT_ref, dkT_ref, dvT_ref, dfl_ref, wb_ref, dx1_ref, dg_ref, h_ref,
             w_ref):
        i = pl.program_id(0)

        @pl.when(i == 0)
        def _():
            _rows_from_blocks(wb_ref, w_ref)

        xv = x_ref[...]
        gv = g_ref[...]
        h, r = _rms_fwd(xv, gv)
        h_ref[...] = h.astype(MXU)
        dh = _mm(dag_ref[...], w_ref[_AG0:_Q0, :])
        for c0, ref in ((_Q0, dqT_ref), (_K0, dkT_ref), (_V0, dvT_ref)):
            dh = dh + _mm_tn(ref[...].astype(MXU), w_ref[c0:c0 + D_ATTN, :])
        dh = dh + _mm(dfl_ref[...].astype(MXU), w_ref[_F0:N_IN_PAD, :])
        dxn, dgp = _rms_bwd(xv, r, gv, dh)
        dx1_ref[...] = dx2_ref[...] + dxn

        @pl.when(i == 0)
        def _():
            dg_ref[...] = dgp

        @pl.when(i > 0)
        def _():
            dg_ref[...] += dgp

    row = lambda w: pl.BlockSpec((tm, w), lambda i: (i, 0))
    col = pl.BlockSpec((D_ATTN, tm), lambda i: (0, i))
    full = lambda a, b: pl.BlockSpec((a, b), lambda i: (0, 0))
    return _pallas_call(
        body, name=name, grid=(T // tm,),
        in_specs=[row(D), full(1, D), row(D), row(2 * D_CONV), col, col, col, row(LANES),
                  pl.BlockSpec((N_DEV, _IN_BLOCK, D), lambda i: (0, 0, 0))],
        out_specs=[row(D), full(1, D), row(D)],
        out_shape=[jax.ShapeDtypeStruct((T, D), F32), jax.ShapeDtypeStruct((1, D), F32),
                   jax.ShapeDtypeStruct((T, D), MXU)],
        scratch_shapes=[pltpu.VMEM((N_IN_PAD, D), MXU)],
        compiler_params=_params(40, 1),
    )(x1, gm, dx2, dag, dqT, dkT, dvT, dfl, win_blocks)


def _inproj_bwd_weights(h, dag, dqT, dkT, dvT, dfl, name, after=()):
    T, D = h.shape
    operands = (h, dag, dqT, dkT, dvT, dfl)
    n = len(operands)

    def body(*refs):
        sources = refs[:n]
        blocks_ref = refs[n + len(after)]
        h_v, dag_v, dqT_v, dkT_v, dvT_v, dfl_v = buffers = refs[n + len(after) + 1:2 * n + len(after) + 1]
        dw_ref, sems = refs[2 * n + len(after) + 1:]
        copies = [pltpu.make_async_copy(src, dst, sems.at[k]) for k, (src, dst) in enumerate(zip(sources, buffers))]
        for cp in copies:
            cp.start()
        copies[0].wait()
        hb = h_v[...]
        copies[1].wait()
        dw_ref[_AG0:_Q0, :] = _mm_tn(dag_v[...], hb).astype(dw_ref.dtype)
        for k, (c0, ref) in enumerate(((_Q0, dqT_v), (_K0, dkT_v), (_V0, dvT_v))):
            copies[2 + k].wait()
            dw_ref[c0:c0 + D_ATTN, :] = _mm(ref[...].astype(MXU), hb).astype(dw_ref.dtype)
        copies[5].wait()
        dw_ref[_F0:N_IN_PAD, :] = _mm_tn(dfl_v[...].astype(MXU), hb).astype(dw_ref.dtype)
        for p in range(N_DEV):
            blocks_ref[p] = dw_ref[_IN_BLOCK * p:_IN_BLOCK * (p + 1), :]

    return _pallas_call(
        body, name=name, in_specs=[_UNREAD] * (n + len(after)), out_specs=pl.BlockSpec(memory_space=pltpu.VMEM),
        out_shape=jax.ShapeDtypeStruct((N_DEV, _IN_BLOCK, D), MXU),
        scratch_shapes=[pltpu.VMEM(a.shape, a.dtype) for a in operands]
        + [pltpu.VMEM((N_IN_PAD, D), MXU), pltpu.SemaphoreType.DMA((n,))],
        compiler_params=pltpu.CompilerParams(vmem_limit_bytes=56 * MIB),
    )(*operands, *after)


def _forget_fwd(fl, fbp, name):
    T = fl.shape[0]
    tb = _tile(T, 256)

    def body(fl_ref, fb_ref, cum_ref, cumT_ref):
        ri = lax.broadcasted_iota(jnp.int32, (tb, tb), 0)
        ci = lax.broadcasted_iota(jnp.int32, (tb, tb), 1)
        tri = (ri >= ci).astype(jnp.bfloat16)
        carry = jnp.zeros((1, LANES), F32)
        for b in range(T // tb):
            z = fl_ref[b * tb:(b + 1) * tb, :] + fb_ref[...]
            lf = jnp.minimum(z, 0.0) - jnp.log1p(jnp.exp(-jnp.abs(z)))
            c = _exact_tri_dot(tri, lf) + carry
            cum_ref[b * tb:(b + 1) * tb, :] = c
            carry = c[tb - 1:tb, :]
        cumT_ref[...] = cum_ref[...].T[:N_HEADS, :]

    return _pallas_call(
        body, name=name,
        out_shape=[jax.ShapeDtypeStruct((T, LANES), F32), jax.ShapeDtypeStruct((N_HEADS, T), F32)],
        compiler_params=pltpu.CompilerParams(vmem_limit_bytes=32 * MIB),
    )(fl, fbp)


def _forget_bwd(dcum, fl, fbp, name):
    T = fl.shape[0]
    tb = _tile(T, 256)

    def body(dc_ref, fl_ref, fb_ref, dfl_ref, dfb_ref):
        ri = lax.broadcasted_iota(jnp.int32, (tb, tb), 0)
        ci = lax.broadcasted_iota(jnp.int32, (tb, tb), 1)
        tri = (ri <= ci).astype(jnp.bfloat16)
        carry = jnp.zeros((1, LANES), F32)
        dfb = jnp.zeros((1, LANES), F32)
        for b in reversed(range(T // tb)):
            sl = slice(b * tb, (b + 1) * tb)
            dl = _exact_tri_dot(tri, dc_ref[sl, :]) + carry
            carry = dl[0:1, :]
            z = fl_ref[sl, :] + fb_ref[...]
            dfl = dl * jax.nn.sigmoid(-z)
            dfl_ref[sl, :] = dfl
            dfb = dfb + jnp.sum(dfl, axis=0, keepdims=True)
        dfb_ref[...] = dfb

    return _pallas_call(
        body, name=name,
        out_shape=[jax.ShapeDtypeStruct((T, LANES), F32), jax.ShapeDtypeStruct((1, LANES), F32)],
        compiler_params=pltpu.CompilerParams(vmem_limit_bytes=32 * MIB),
    )(dcum, fl, fbp)


def _causal_keep(i, j, tq, tk):
    key = j * tk + lax.broadcasted_iota(jnp.int32, (tk, tq), 0)
    qry = i * tq + lax.broadcasted_iota(jnp.int32, (tk, tq), 1)
    return key <= qry


def _split_hi_lo(x):
    hi = x.astype(MXU)
    lo = (x - hi.astype(F32)).astype(MXU)
    return hi, lo


def _attn_fwd(qT, k, vT, cum, cumT, name):
    T = k.shape[0]
    tq = _tile(T, 256)
    tk = _tile(tq, 256)
    kpq = tq // tk
    heads = [slice(HEAD_DIM * h, HEAD_DIM * (h + 1)) for h in range(N_HEADS)]

    def body(qT_ref, k_ref, vT_ref, cum_ref, cumT_ref, o_ref, lseT_ref, acc_s, m_s, l_s):
        i = pl.program_id(0)
        acc_s[...] = jnp.zeros_like(acc_s)
        m_s[...] = jnp.full_like(m_s, NEG)
        l_s[...] = jnp.zeros_like(l_s)

        def kblock(j, masked):
            rows = pl.ds(pl.multiple_of(j * tk, tk), tk)
            keep = _causal_keep(i, j, tq, tk) if masked else None
            bias = [cumT_ref[h:h + 1, :] - cum_ref[rows, h:h + 1] for h in range(N_HEADS)]
            qk = [_mm(k_ref[rows, hs], qT_ref[hs, :]) + bias[h] for h, hs in enumerate(heads)]
            for h, hs in enumerate(heads):
                sT = qk[h]
                if masked:
                    sT = jnp.where(keep, sT, NEG)
                m_old = m_s[h:h + 1, :]
                m_new = jnp.maximum(m_old, jnp.max(sT, axis=0, keepdims=True))
                alpha = jnp.exp(m_old - m_new)
                pT = jnp.exp(sT - m_new)
                l_s[h:h + 1, :] = alpha * l_s[h:h + 1, :] + jnp.sum(pT, axis=0, keepdims=True)
                p_hi, p_lo = _split_hi_lo(pT)
                vh = vT_ref[hs, rows]
                acc_s[hs, :] = alpha * acc_s[hs, :] + (_mm(vh, p_hi) + _mm(vh, p_lo))
                m_s[h:h + 1, :] = m_new

        def unmasked(j, c):
            kblock(j, False)
            return c

        lax.fori_loop(0, kpq * i, unmasked, 0)
        for d in range(kpq):
            kblock(kpq * i + d, True)
        for h, hs in enumerate(heads):
            acc_s[hs, :] = acc_s[hs, :] / l_s[h:h + 1, :]
        o_ref[...] = acc_s[...].T
        lseT_ref[...] = m_s[...] + jnp.log(l_s[...])

    full = lambda a, b: pl.BlockSpec((a, b), lambda i: (0, 0))
    colblk = lambda r: pl.BlockSpec((r, tq), lambda i: (0, i))
    return _pallas_call(
        body, name=name, grid=(T // tq,),
        in_specs=[colblk(D_ATTN), full(T, D_ATTN), full(D_ATTN, T), full(T, LANES), colblk(N_HEADS)],
        out_specs=[pl.BlockSpec((tq, D_ATTN), lambda i: (i, 0)), colblk(N_HEADS)],
        out_shape=[jax.ShapeDtypeStruct((T, D_ATTN), F32), jax.ShapeDtypeStruct((N_HEADS, T), F32)],
        scratch_shapes=[pltpu.VMEM((D_ATTN, tq), F32), pltpu.VMEM((N_HEADS, tq), F32),
                        pltpu.VMEM((N_HEADS, tq), F32)],
        compiler_params=_params(40, 1),
    )(qT, k, vT, cum, cumT)


def _attn_bwd(qT, k, kT, v, doT, lseT, deltaT, cum, cumT, name, after=()):
    T = k.shape[0]
    tq = _tile(T, 256)
    tk = _tile(tq, 256)
    kpq = tq // tk
    heads = [slice(HEAD_DIM * h, HEAD_DIM * (h + 1)) for h in range(N_HEADS)]

    def body(qT_ref, k_ref, kT_ref, v_ref, doT_ref, lseT_ref, dlT_ref, cum_ref, cumT_ref, *rest):
        dq_ref, dk_ref, dv_ref, dcum_ref, dq_s = rest[len(after):]
        i = pl.program_id(0)

        @pl.when(i == 0)
        def _():
            dk_ref[...] = jnp.zeros_like(dk_ref)
            dv_ref[...] = jnp.zeros_like(dv_ref)
            dcum_ref[...] = jnp.zeros_like(dcum_ref)

        dq_s[...] = jnp.zeros_like(dq_s)

        def kblock(j, masked):
            rows = pl.ds(pl.multiple_of(j * tk, tk), tk)
            keep = _causal_keep(i, j, tq, tk) if masked else None
            bias = [cumT_ref[h:h + 1, :] - cum_ref[rows, h:h + 1] for h in range(N_HEADS)]
            qk = [_mm(k_ref[rows, hs], qT_ref[hs, :]) + bias[h] for h, hs in enumerate(heads)]
            dps = [_mm(v_ref[rows, hs], doT_ref[hs, :]) for hs in heads]
            for h, hs in enumerate(heads):
                sT = qk[h]
                if masked:
                    sT = jnp.where(keep, sT, NEG)
                pT = jnp.exp(sT - lseT_ref[h:h + 1, :])
                dsT = pT * (dps[h] - dlT_ref[h:h + 1, :])
                dcum_ref[rows, h:h + 1] += -jnp.sum(dsT, axis=1, keepdims=True)
                dsb = dsT.astype(MXU)
                dv_ref[hs, rows] += _mm_nt(doT_ref[hs, :], pT.astype(MXU))
                dk_ref[hs, rows] += _mm_nt(qT_ref[hs, :], dsb)
                dq_s[hs, :] += _mm(kT_ref[hs, rows], dsb)

        def unmasked(j, c):
            kblock(j, False)
            return c

        lax.fori_loop(0, kpq * i, unmasked, 0)
        for d in range(kpq):
            kblock(kpq * i + d, True)
        dq_ref[...] = (dq_s[...] * SCALE).astype(dq_ref.dtype)

    full = lambda a, b: pl.BlockSpec((a, b), lambda i: (0, 0))
    colblk = lambda r: pl.BlockSpec((r, tq), lambda i: (0, i))
    return _pallas_call(
        body, name=name, grid=(T // tq,),
        in_specs=[colblk(D_ATTN), full(T, D_ATTN), full(D_ATTN, T), full(T, D_ATTN), colblk(D_ATTN),
                  colblk(N_HEADS), colblk(N_HEADS), full(T, LANES), colblk(N_HEADS)] + [_UNREAD] * len(after),
        out_specs=[colblk(D_ATTN), full(D_ATTN, T), full(D_ATTN, T), full(T, LANES)],
        out_shape=[
            jax.ShapeDtypeStruct((D_ATTN, T), MXU),
            jax.ShapeDtypeStruct((D_ATTN, T), F32),
            jax.ShapeDtypeStruct((D_ATTN, T), F32),
            jax.ShapeDtypeStruct((T, LANES), F32),
        ],
        scratch_shapes=[pltpu.VMEM((D_ATTN, tq), F32)],
        compiler_params=_params(48, 1),
    )(qT, k, kT, v, doT, lseT, deltaT, cum, cumT, *after)


_ROWS_PER_CHUNK = 64


def _glu_halo(ag_ref, agh_ref, uext_s, tm, first):
    a = ag_ref[:, :D_CONV]
    sg = jax.nn.sigmoid(ag_ref[:, D_CONV:])
    uh = agh_ref[:, :D_CONV] * jax.nn.sigmoid(agh_ref[:, D_CONV:])
    uext_s[0:CONV_HALO, :] = jnp.where(first, 0.0, uh)
    uext_s[CONV_HALO:CONV_HALO + tm, :] = a * sg
    return a, sg


_SUBLANES = 8


def _shifted_copies(ext_s, sh_s, rows):
    for k in range(1, _SUBLANES):
        sh_s[k, 0:rows, :] = ext_s[pl.ds(k, rows), :]


def _window(ext_s, sh_s, start, rows):
    k = start % _SUBLANES
    if k == 0:
        return ext_s[pl.ds(start, rows), :]
    return sh_s[k, pl.ds(start - k, rows), :]


def _layer_norm_stats(y):
    mu = jnp.mean(y, axis=-1, keepdims=True)
    xc = y - mu
    rs = lax.rsqrt(jnp.mean(xc * xc, axis=-1, keepdims=True) + EPS)
    return xc * rs, rs


def _conv_fwd(ag, w32, cb, lg, lb, name):
    T = ag.shape[0]
    tm = _tile(T, 512)
    rc = _tile(tm, _ROWS_PER_CHUNK)
    hb = tm // CONV_HALO

    def body(ag_ref, agh_ref, w_ref, cb_ref, lg_ref, lb_ref, yc_ref, c_ref, uext_s, ush_s):
        i = pl.program_id(0)
        _glu_halo(ag_ref, agh_ref, uext_s, tm, i == 0)
        _shifted_copies(uext_s, ush_s, tm + CONV_HALO - _SUBLANES)
        for r0 in range(0, tm, rc):
            acc = jnp.zeros((rc, D_CONV), F32)
            for t in range(CONV_TAPS):
                acc = acc + _window(uext_s, ush_s, r0 + CONV_HALO - (CONV_TAPS - 1) + t, rc) * w_ref[t:t + 1, :]
            y = acc + cb_ref[...]
            yc_ref[r0:r0 + rc, :] = y
            n, _ = _layer_norm_stats(y)
            z = n * lg_ref[...] + lb_ref[...]
            c_ref[r0:r0 + rc, :] = z * jax.nn.sigmoid(z)

    row = lambda w: pl.BlockSpec((tm, w), lambda i: (i, 0))
    full = lambda a, b: pl.BlockSpec((a, b), lambda i: (0, 0))
    return _pallas_call(
        body, name=name, grid=(T // tm,),
        in_specs=[row(2 * D_CONV),
                  pl.BlockSpec((CONV_HALO, 2 * D_CONV), lambda i: (jnp.maximum(i * hb - 1, 0), 0)),
                  full(CONV_HALO, D_CONV), full(1, D_CONV), full(1, D_CONV), full(1, D_CONV)],
        out_specs=[row(D_CONV), row(D_CONV)],
        out_shape=[jax.ShapeDtypeStruct((T, D_CONV), F32), jax.ShapeDtypeStruct((T, D_CONV), F32)],
        scratch_shapes=[pltpu.VMEM((CONV_HALO + tm, D_CONV), F32),
                        pltpu.VMEM((_SUBLANES, CONV_HALO + tm, D_CONV), F32)],
        compiler_params=_params(32, 1),
    )(ag, ag, w32, cb, lg, lb)


def _conv_bwd(dc, yc, ag, w32, lg, lb, name):
    T = ag.shape[0]
    tm = _tile(T, 512)
    rc = _tile(tm, _ROWS_PER_CHUNK)
    I = T // tm
    hb = tm // CONV_HALO
    n_halo_blocks = T // CONV_HALO

    def body(dc_ref, yc_ref, dch_ref, ych_ref, ag_ref, agh_ref, w_ref, lg_ref, lb_ref,
             dag_ref, dw_ref, dcb_ref, dlg_ref, dlb_ref, uext_s, dext_s, ush_s, dsh_s):
        i = pl.program_id(0)
        lgv = lg_ref[...]
        lbv = lb_ref[...]

        def ln_bwd(dcv, ycv):
            n, rs = _layer_norm_stats(ycv)
            z = n * lgv + lbv
            dz = dcv * _silu_grad(z, jax.nn.sigmoid(z))
            dn = dz * lgv
            dy = rs * (dn - jnp.mean(dn, axis=-1, keepdims=True) - n * jnp.mean(dn * n, axis=-1, keepdims=True))
            return dy, dz, n

        dy, dz, n = ln_bwd(dc_ref[...], yc_ref[...])
        dyh, _, _ = ln_bwd(dch_ref[...], ych_ref[...])
        dext_s[0:tm, :] = dy
        dext_s[tm:tm + CONV_HALO, :] = jnp.where(i == I - 1, 0.0, dyh)
        a, sg = _glu_halo(ag_ref, agh_ref, uext_s, tm, i == 0)
        _shifted_copies(uext_s, ush_s, tm + CONV_HALO - _SUBLANES)
        _shifted_copies(dext_s, dsh_s, tm + CONV_HALO - _SUBLANES)

        @pl.when(i == 0)
        def _():
            dw_ref[...] = jnp.zeros_like(dw_ref)
            dcb_ref[...] = jnp.zeros_like(dcb_ref)
            dlg_ref[...] = jnp.zeros_like(dlg_ref)
            dlb_ref[...] = jnp.zeros_like(dlb_ref)

        dcb_ref[...] += jnp.sum(dy, axis=0, keepdims=True)
        dlg_ref[...] += jnp.sum(dz * n, axis=0, keepdims=True)
        dlb_ref[...] += jnp.sum(dz, axis=0, keepdims=True)
        for t in range(CONV_TAPS):
            u_t = _window(uext_s, ush_s, CONV_HALO - (CONV_TAPS - 1) + t, tm)
            dw_ref[t:t + 1, :] += jnp.sum(dy * u_t, axis=0, keepdims=True)
        for r0 in range(0, tm, rc):
            acc = jnp.zeros((rc, D_CONV), F32)
            for t in range(CONV_TAPS):
                acc = acc + _window(dext_s, dsh_s, r0 + (CONV_TAPS - 1) - t, rc) * w_ref[t:t + 1, :]
            a_c = a[r0:r0 + rc, :]
            sg_c = sg[r0:r0 + rc, :]
            dag_ref[r0:r0 + rc, :D_CONV] = (acc * sg_c).astype(dag_ref.dtype)
            dag_ref[r0:r0 + rc, D_CONV:] = (acc * a_c * sg_c * (1.0 - sg_c)).astype(dag_ref.dtype)

    row = lambda w: pl.BlockSpec((tm, w), lambda i: (i, 0))
    full = lambda a, b: pl.BlockSpec((a, b), lambda i: (0, 0))
    nxt = pl.BlockSpec((CONV_HALO, D_CONV), lambda i: (jnp.minimum((i + 1) * hb, n_halo_blocks - 1), 0))
    return _pallas_call(
        body, name=name, grid=(I,),
        in_specs=[row(D_CONV), row(D_CONV), nxt, nxt, row(2 * D_CONV),
                  pl.BlockSpec((CONV_HALO, 2 * D_CONV), lambda i: (jnp.maximum(i * hb - 1, 0), 0)),
                  full(CONV_HALO, D_CONV), full(1, D_CONV), full(1, D_CONV)],
        out_specs=[row(2 * D_CONV), full(CONV_HALO, D_CONV), full(1, D_CONV), full(1, D_CONV), full(1, D_CONV)],
        out_shape=[
            jax.ShapeDtypeStruct((T, 2 * D_CONV), MXU),
            jax.ShapeDtypeStruct((CONV_HALO, D_CONV), F32),
            jax.ShapeDtypeStruct((1, D_CONV), F32),
            jax.ShapeDtypeStruct((1, D_CONV), F32),
            jax.ShapeDtypeStruct((1, D_CONV), F32),
        ],
        scratch_shapes=[pltpu.VMEM((CONV_HALO + tm, D_CONV), F32), pltpu.VMEM((tm + CONV_HALO, D_CONV), F32),
                        pltpu.VMEM((_SUBLANES, CONV_HALO + tm, D_CONV), F32),
                        pltpu.VMEM((_SUBLANES, CONV_HALO + tm, D_CONV), F32)],
        compiler_params=_params(40, 1),
    )(dc, yc, dc, yc, ag, ag, w32, lg, lb)


def _outproj_fwd(x1, c, o, gc, ga, wout, name):
    T, D = x1.shape
    tm = _tile(T, 512)

    def body(x_ref, c_ref, o_ref, gc_ref, ga_ref, w_ref, x2_ref):
        yc, _ = _rms_fwd(c_ref[...], gc_ref[...])
        ya, _ = _rms_fwd(o_ref[...], ga_ref[...])
        x2_ref[...] = (x_ref[...] + _mm(yc.astype(MXU), w_ref[:D_CONV, :])
                       + _mm(ya.astype(MXU), w_ref[D_CONV:, :]))

    row = lambda w: pl.BlockSpec((tm, w), lambda i: (i, 0))
    full = lambda a, b: pl.BlockSpec((a, b), lambda i: (0, 0))
    return _pallas_call(
        body, name=name, grid=(T // tm,),
        in_specs=[row(D), row(D_CONV), row(D_ATTN), full(1, D_CONV), full(1, D_ATTN), full(D_CONV + D_ATTN, D)],
        out_specs=row(D),
        out_shape=jax.ShapeDtypeStruct((T, D), F32),
        compiler_params=_params(32, 1),
    )(x1, c, o, gc, ga, wout)


def _outproj_bwd(dx2, c, o, gc, ga, wout, name):
    T, D = dx2.shape
    tm = _tile(T, 512)
    I = T // tm

    def body(dx_ref, c_ref, o_ref, gc_ref, ga_ref, w_ref,
             dc_ref, doT_ref, dlT_ref, dw_ref, dgc_ref, dga_ref, acc_s):
        i = pl.program_id(0)
        dxb = dx_ref[...].astype(MXU)
        cv = c_ref[...]
        ov = o_ref[...]
        yc, rcn = _rms_fwd(cv, gc_ref[...])
        ya, ra = _rms_fwd(ov, ga_ref[...])
        dyc = _mm_nt(dxb, w_ref[:D_CONV, :])
        dya = _mm_nt(dxb, w_ref[D_CONV:, :])
        dwc = _mm_tn(yc.astype(MXU), dxb)
        dwa = _mm_tn(ya.astype(MXU), dxb)
        dcv, dgc = _rms_bwd(cv, rcn, gc_ref[...], dyc)
        dov, dga = _rms_bwd(ov, ra, ga_ref[...], dya)
        dc_ref[...] = dcv
        dob = dov.astype(doT_ref.dtype)
        doT_ref[...] = dov.T.astype(doT_ref.dtype)
        chan = lax.broadcasted_iota(jnp.int32, (D_ATTN, LANES), 0)
        head = lax.broadcasted_iota(jnp.int32, (D_ATTN, LANES), 1)
        in_head = ((chan >= head * HEAD_DIM) & (chan < (head + 1) * HEAD_DIM)).astype(jnp.bfloat16)
        dlT_ref[...] = _exact_dot_01(dob.astype(F32) * ov, in_head).T[:N_HEADS, :]

        @pl.when(i == 0)
        def _():
            acc_s[:D_CONV, :] = dwc
            acc_s[D_CONV:, :] = dwa
            dgc_ref[...] = dgc
            dga_ref[...] = dga

        @pl.when(i > 0)
        def _():
            acc_s[:D_CONV, :] += dwc
            acc_s[D_CONV:, :] += dwa
            dgc_ref[...] += dgc
            dga_ref[...] += dga

        @pl.when(i == I - 1)
        def _():
            dw_ref[...] = acc_s[...].astype(dw_ref.dtype)

    row = lambda w: pl.BlockSpec((tm, w), lambda i: (i, 0))
    full = lambda a, b: pl.BlockSpec((a, b), lambda i: (0, 0))
    return _pallas_call(
        body, name=name, grid=(I,),
        in_specs=[row(D), row(D_CONV), row(D_ATTN), full(1, D_CONV), full(1, D_ATTN), full(D_CONV + D_ATTN, D)],
        out_specs=[row(D_CONV), pl.BlockSpec((D_ATTN, tm), lambda i: (0, i)),
                   pl.BlockSpec((N_HEADS, tm), lambda i: (0, i)),
                   full(D_CONV + D_ATTN, D), full(1, D_CONV), full(1, D_ATTN)],
        out_shape=[
            jax.ShapeDtypeStruct((T, D_CONV), F32),
            jax.ShapeDtypeStruct((D_ATTN, T), MXU),
            jax.ShapeDtypeStruct((N_HEADS, T), F32),
            jax.ShapeDtypeStruct((D_CONV + D_ATTN, D), MXU),
            jax.ShapeDtypeStruct((1, D_CONV), F32),
            jax.ShapeDtypeStruct((1, D_ATTN), F32),
        ],
        scratch_shapes=[pltpu.VMEM((D_CONV + D_ATTN, D), F32)],
        compiler_params=_params(40, 1),
    )(dx2, c, o, gc, ga, wout)


def _ffn_down_loss(x, A, w2, gf, target, name):
    T, D = x.shape
    J, _, bf = A.shape
    tm = _tile(T, 512)

    def body(x_ref, A_ref, w2_ref, g_ref, t_ref, loss_ref, dx_ref, dg_ref):
        i = pl.program_id(0)
        f = _mm(A_ref[0], w2_ref[0:bf, :])
        for j in range(1, J):
            f = f + _mm(A_ref[j], w2_ref[j * bf:(j + 1) * bf, :])
        xv = x_ref[...] + 0.5 * f
        gv = g_ref[...]
        out, r = _rms_fwd(xv, gv)
        err = out - t_ref[...]
        part = jnp.full((1, LANES), 0.5 / D, F32) * jnp.sum(err * err)
        dxn, dgp = _rms_bwd(xv, r, gv, err * (1.0 / D))
        dx_ref[...] = dxn

        @pl.when(i == 0)
        def _():
            loss_ref[...] = part
            dg_ref[...] = dgp

        @pl.when(i > 0)
        def _():
            loss_ref[...] += part
            dg_ref[...] += dgp

    row = lambda w: pl.BlockSpec((tm, w), lambda i: (i, 0))
    full = lambda a, b: pl.BlockSpec((a, b), lambda i: (0, 0))
    return _pallas_call(
        body, name=name, grid=(T // tm,),
        in_specs=[row(D), pl.BlockSpec((J, tm, bf), lambda i: (0, i, 0)), full(J * bf, D), full(1, D), row(D)],
        out_specs=[full(1, LANES), row(D), full(1, D)],
        out_shape=[jax.ShapeDtypeStruct((1, LANES), F32), jax.ShapeDtypeStruct((T, D), F32),
                   jax.ShapeDtypeStruct((1, D), F32)],
        compiler_params=_params(56, 1),
    )(x, A, w2, gf, target)


def _row_tile(rows):
    for cand in (256, 176, 128, 64, 32, 16):
        if rows % cand == 0:
            return cand
    return rows


def _adamw(w, m, v, parts, name):
    R, C = w.shape
    P = parts.shape[0]
    tr = _row_tile(R)
    c1 = 1.0 - ADAM_B1 ** ADAM_STEP
    c2 = 1.0 - ADAM_B2 ** ADAM_STEP

    def body(w_ref, m_ref, v_ref, p_ref, g_ref, d_ref, nm_ref, nv_ref):
        g = p_ref[0].astype(F32)
        for s in range(1, P):
            g = g + p_ref[s].astype(F32)
        wv = w_ref[...]
        mn = ADAM_B1 * m_ref[...] + (1.0 - ADAM_B1) * g
        vn = ADAM_B2 * v_ref[...] + (1.0 - ADAM_B2) * (g * g)
        g_ref[...] = g
        nm_ref[...] = mn
        nv_ref[...] = vn
        d_ref[...] = -ADAM_LR * ((mn / c1) / (jnp.sqrt(vn / c2) + ADAM_EPS) + ADAM_WD * wv)

    blk = pl.BlockSpec((tr, C), lambda i: (i, 0))
    out = jax.ShapeDtypeStruct((R, C), F32)
    return _pallas_call(
        body, name=name, grid=(R // tr,),
        in_specs=[blk, blk, blk, pl.BlockSpec((P, tr, C), lambda i: (0, i, 0))],
        out_specs=[blk, blk, blk, blk],
        out_shape=[out, out, out, out],
        compiler_params=_params(32, 1),
    )(w, m, v, parts)


def _position():
    return lax.axis_index("x"), lax.axis_index("y"), lax.axis_index("c")


def _flat(px, py, pc):
    return 4 * px + 2 * py + pc


def _row_halves(rows, dtype):
    tile = _SUBLANES * (4 // jnp.dtype(dtype).itemsize)
    half = rows // 2 // tile * tile
    assert half > 0
    return (0, half), (half, rows - half)


def _gather_body(ins, outs, send_sems, recv_sems, local_sems, handshake):
    n = len(ins)
    x, y, c = _position()
    me, sibling = (x, y, c), (x, y, 1 - c)
    across_x, across_y, diagonal = (1 - x, y), (x, 1 - y), (1 - x, 1 - y)
    if handshake:
        _handshake([sibling] + [(*chip, cc) for chip in (across_x, across_y, diagonal) for cc in (c, 1 - c)])

    def copy(a, k, block, to, rows=None, src=None):
        dst = outs[a].at[_flat(*block)]
        if rows is not None:
            dst = dst.at[pl.ds(*rows)]
        return pltpu.make_async_remote_copy(
            src_ref=dst if src is None else src, dst_ref=dst,
            send_sem=send_sems.at[a, k], recv_sem=recv_sems.at[a, k],
            device_id=to, device_id_type=MESH)

    halves = [_row_halves(ins[a].shape[0], ins[a].dtype) for a in range(n)]
    mine = [pltpu.make_async_copy(ins[a], outs[a].at[_flat(*me)], local_sems.at[a]) for a in range(n)]
    for cp in mine:
        cp.start()
    sent = []

    def start(cp):
        cp.start()
        sent.append(cp)

    for a in range(n):
        start(copy(a, 0, me, sibling, src=ins[a]))
        start(copy(a, 1, me, (*across_x, c), src=ins[a]))
        start(copy(a, 2, me, (*across_y, c), src=ins[a]))
    for a in range(n):
        low, high = halves[a]
        copy(a, 1, (*across_x, c), me).wait_recv()
        start(copy(a, 3, (*across_x, c), (*across_y, c), rows=low))
        start(copy(a, 5, (*across_x, c), sibling))
        copy(a, 2, (*across_y, c), me).wait_recv()
        start(copy(a, 4, (*across_y, c), (*across_x, c), rows=high))
        start(copy(a, 6, (*across_y, c), sibling))
    for a in range(n):
        low, high = halves[a]
        copy(a, 3, (*diagonal, c), me, rows=low).wait_recv()
        start(copy(a, 7, (*diagonal, c), sibling, rows=low))
        copy(a, 4, (*diagonal, c), me, rows=high).wait_recv()
        start(copy(a, 8, (*diagonal, c), sibling, rows=high))
    for a in range(n):
        low, high = halves[a]
        copy(a, 0, sibling, me).wait_recv()
        copy(a, 5, (*across_x, 1 - c), me).wait_recv()
        copy(a, 6, (*across_y, 1 - c), me).wait_recv()
        copy(a, 7, (*diagonal, 1 - c), me, rows=low).wait_recv()
        copy(a, 8, (*diagonal, 1 - c), me, rows=high).wait_recv()
    for cp in sent:
        cp.wait_send()
    for cp in mine:
        cp.wait()


_GATHER_SLOTS = 9


def _gather_scratch(n):
    return [pltpu.SemaphoreType.DMA((n, _GATHER_SLOTS)), pltpu.SemaphoreType.DMA((n, _GATHER_SLOTS)),
            pltpu.SemaphoreType.DMA((n,))]


def _all_gather(shards, name):
    n = len(shards)

    def body(*refs):
        _gather_body(refs[:n], refs[n:2 * n], *refs[2 * n:], handshake=False)

    hbm = pl.BlockSpec(memory_space=pltpu.HBM)
    return _pallas_call(
        body, name=name,
        in_specs=[hbm] * n, out_specs=[hbm] * n,
        out_shape=[jax.ShapeDtypeStruct((N_DEV,) + s.shape, s.dtype) for s in shards],
        scratch_shapes=_gather_scratch(n),
    )(*shards)


def _handshake(peers):
    barrier = pltpu.get_barrier_semaphore()
    for peer in peers:
        pl.semaphore_signal(barrier, inc=1, device_id=peer, device_id_type=MESH)
    pl.semaphore_wait(barrier, len(peers))


def _sequencer_call(body, name, collective_id, out_type, scratch_types, operands):
    return pl.kernel(
        body, name=name, out_type=out_type,
        mesh=plsc.ScalarSubcoreMesh(axis_name="sequencer", num_cores=1),
        scratch_types=scratch_types,
        compiler_params=pltpu.CompilerParams(collective_id=collective_id),
    )(*operands)


def _seq_all_gather(shards, name, collective_id, after):
    n = len(shards)

    def body(*refs):
        _gather_body(refs[:n], refs[n + 1:2 * n + 1], *refs[2 * n + 1:], handshake=True)

    return _sequencer_call(
        body, name, collective_id,
        [jax.ShapeDtypeStruct((N_DEV,) + s.shape, s.dtype) for s in shards],
        _gather_scratch(n), list(shards) + [after])


def _seq_to_sibling(parts, name, collective_id, after):
    n = len(parts)

    def body(*refs):
        ins, outs = refs[:n], refs[n + len(after):2 * n + len(after)]
        send_sems, recv_sems = refs[2 * n + len(after):]
        x, y, c = _position()
        sibling = (x, y, 1 - c)
        _handshake([sibling])
        sent = []
        for a in range(n):
            for q in range(N_CHIPS):
                cp = pltpu.make_async_remote_copy(
                    src_ref=ins[a].at[2 * q + (1 - c)], dst_ref=outs[a].at[q],
                    send_sem=send_sems.at[a, q], recv_sem=recv_sems.at[a, q],
                    device_id=sibling, device_id_type=MESH)
                cp.start()
                sent.append(cp)
        for cp in sent:
            cp.wait_recv()
        for cp in sent:
            cp.wait_send()

    return _sequencer_call(
        body, name, collective_id,
        [jax.ShapeDtypeStruct((N_CHIPS,) + p.shape[1:], p.dtype) for p in parts],
        [pltpu.SemaphoreType.DMA((n, N_CHIPS)), pltpu.SemaphoreType.DMA((n, N_CHIPS))],
        list(parts) + list(after))


def _seq_to_chips(partials, name, collective_id):
    n = len(partials)

    def body(*refs):
        ins, outs = refs[:n], refs[n:2 * n]
        send_sems, recv_sems, local_sems = refs[2 * n:]
        x, y, c = _position()
        my_chip = 2 * x + y
        chips = [(1 - x, y), (x, 1 - y), (1 - x, 1 - y)]
        _handshake([(*chip, c) for chip in chips])
        mine = [pltpu.make_async_copy(ins[a].at[my_chip], outs[a].at[my_chip], local_sems.at[a]) for a in range(n)]
        for cp in mine:
            cp.start()
        sent = []
        for a in range(n):
            for j, (px, py) in enumerate(chips):
                cp = pltpu.make_async_remote_copy(
                    src_ref=ins[a].at[2 * px + py], dst_ref=outs[a].at[my_chip],
                    send_sem=send_sems.at[a, j], recv_sem=recv_sems.at[a, j],
                    device_id=(px, py, c), device_id_type=MESH)
                cp.start()
                sent.append(cp)
        for cp in sent:
            cp.wait_recv()
        for cp in sent:
            cp.wait_send()
        for cp in mine:
            cp.wait()

    return _sequencer_call(
        body, name, collective_id,
        [jax.ShapeDtypeStruct(p.shape, p.dtype) for p in partials],
        [pltpu.SemaphoreType.DMA((n, 3)), pltpu.SemaphoreType.DMA((n, 3)), pltpu.SemaphoreType.DMA((n,))],
        list(partials))


def _pair_add(parts, recvs, name, after=()):
    n = len(parts)
    core = lax.axis_index("c").astype(jnp.int32).reshape(1)

    def body(c_ref, *refs):
        ps, rs, outs = refs[:n], refs[n:2 * n], refs[2 * n + len(after):]
        for p_ref, r_ref, o_ref in zip(ps, rs, outs):
            o_ref[...] = (p_ref[...].astype(F32) + r_ref[...].astype(F32)).astype(o_ref.dtype)

    mine = lambda p: pl.BlockSpec((None,) + p.shape[1:], lambda q, c: (2 * q + c[0], 0, 0))
    blk = lambda p: pl.BlockSpec((None,) + p.shape[1:], lambda q, c: (q, 0, 0))
    return pl.pallas_call(
        body, name=name,
        grid_spec=pltpu.PrefetchScalarGridSpec(
            num_scalar_prefetch=1, grid=(N_CHIPS,),
            in_specs=[mine(p) for p in parts] + [blk(p) for p in parts] + [_UNREAD] * len(after),
            out_specs=[blk(p) for p in parts]),
        out_shape=[pltpu.HBM((N_CHIPS,) + p.shape[1:], p.dtype) for p in parts],
        compiler_params=_params(40, 1),
    )(core, *[pltpu.with_memory_space_constraint(a, pltpu.HBM) for a in (*parts, *recvs, *after)])


class _Reduced(NamedTuple):
    partials: list
    reduced: list


def _blocks(g):
    return g.reshape(N_DEV, -1, g.shape[-1])


def _reduce_scatter(parts, tag, ids, after=(), add_after=()):
    from_sibling = _seq_to_sibling(parts, "rs_sibling_" + tag, ids[0], after)
    partials = _pair_add(parts, from_sibling, "rs_add_" + tag, add_after)
    return _Reduced(partials, _seq_to_chips(partials, "rs_chips_" + tag, ids[1]))


_SMALL = ("ffn1_norm", "mix_norm", "conv_b", "conv_ln_g", "conv_ln_b", "forget_b", "out_norm_conv",
          "out_norm_attn", "ffn2_norm", "final_norm")
_PACK_WIDTH = 2 * D_CONV
_SLOT = dict(ffn1_norm=(0, 0), mix_norm=(1, 0), ffn2_norm=(2, 0), final_norm=(3, 0), conv_b=(4, 0),
             conv_ln_g=(4, D_CONV), conv_ln_b=(5, 0), out_norm_conv=(5, D_CONV), out_norm_attn=(6, 0),
             forget_b=(6, D_CONV))
_LOSS_ROW = 7
_CONV_ROW0 = 8
_PACK_ROWS = _CONV_ROW0 + CONV_HALO


def _pack_small(small, name):
    arrays = [small[n] for n in _SMALL] + [small["conv_w"], small["loss"]]

    def body(*refs):
        out = refs[-1]
        out[...] = jnp.zeros_like(out)
        for n, ref in zip(_SMALL, refs):
            row, lane = _SLOT[n]
            out[row:row + 1, lane:lane + ref.shape[1]] = ref[...]
        out[_CONV_ROW0:, :D_CONV] = refs[len(_SMALL)][...]
        out[_LOSS_ROW:_LOSS_ROW + 1, :LANES] = refs[len(_SMALL) + 1][...]

    return _pallas_call(body, name=name, out_shape=jax.ShapeDtypeStruct((_PACK_ROWS, _PACK_WIDTH), F32))(*arrays)


def _adamw_small(gathered, w, m, v, name):
    c1 = 1.0 - ADAM_B1 ** ADAM_STEP
    c2 = 1.0 - ADAM_B2 ** ADAM_STEP
    k = len(_SMALL)

    def body(g_ref, *refs):
        ws, ms, vs = refs[:k], refs[k:2 * k], refs[2 * k:3 * k]
        outs = refs[3 * k:]
        total = g_ref[0]
        for s in range(1, N_DEV):
            total = total + g_ref[s]
        for i, n in enumerate(_SMALL):
            row, lane = _SLOT[n]
            width = ws[i].shape[1]
            g = total[row:row + 1, lane:lane + width]
            mn = ADAM_B1 * ms[i][...] + (1.0 - ADAM_B1) * g
            vn = ADAM_B2 * vs[i][...] + (1.0 - ADAM_B2) * (g * g)
            o_g, o_d, o_m, o_v = outs[4 * i:4 * i + 4]
            o_g[...] = g
            o_m[...] = mn
            o_v[...] = vn
            o_d[...] = -ADAM_LR * ((mn / c1) / (jnp.sqrt(vn / c2) + ADAM_EPS) + ADAM_WD * ws[i][...])
        outs[4 * k][...] = total[_CONV_ROW0:, :D_CONV]
        outs[4 * k + 1][...] = total[_LOSS_ROW:_LOSS_ROW + 1, :LANES]

    shapes = []
    for n in _SMALL:
        shapes += [jax.ShapeDtypeStruct(w[n].shape, F32)] * 4
    shapes.append(jax.ShapeDtypeStruct((CONV_HALO, D_CONV), F32))
    shapes.append(jax.ShapeDtypeStruct((1, LANES), F32))
    res = _pallas_call(body, name=name, out_shape=shapes)(
        gathered, *[w[n] for n in _SMALL], *[m[n] for n in _SMALL], *[v[n] for n in _SMALL])
    return {n: res[4 * i:4 * i + 4] for i, n in enumerate(_SMALL)}, res[4 * k], res[4 * k + 1]


def _local_step(x, target, norms, shard):
    D = x.shape[1]
    J = N_DEV // 2
    as13 = lambda g: g.reshape(2, J, g.shape[1], D)

    (g13_1,) = _all_gather([shard["ffn1_w13"]], "gather_ffn1_w13")
    (g2_1,) = _seq_all_gather([shard["ffn1_w2"]], "gather_ffn1_w2", 10, after=g13_1)
    w13_1 = as13(g13_1)
    G1, U1, A1 = _ffn_up(x, norms["ffn1_norm"], w13_1, "ffn1_up")
    gin, gconv = _seq_all_gather([shard["w_in"], shard["conv_w"]], "gather_mix", 1, after=G1)
    w2_1 = g2_1.reshape(-1, D)
    x1 = _ffn_down(x, A1, w2_1, "ffn1_down")
    gout, g13_2, g2_2 = _seq_all_gather([shard["w_out"], shard["ffn2_w13"], shard["ffn2_w2"]], "gather_ffn2", 2,
                                        after=x1)
    wout = gout.reshape(-1, D)
    conv_w32 = jnp.pad(gconv.transpose(1, 0, 2).reshape(CONV_TAPS, D_CONV), ((0, CONV_HALO - CONV_TAPS), (0, 0)))

    ag, k, v, qT, kT, vT, fl = _inproj_fwd(x1, norms["mix_norm"], gin, "inproj_fwd")
    cum, cumT = _forget_fwd(fl, norms["forget_b"], "forget_fwd")
    yc, c = _conv_fwd(ag, conv_w32, norms["conv_b"], norms["conv_ln_g"], norms["conv_ln_b"], "conv_fwd")
    o, lseT = _attn_fwd(qT, k, vT, cum, cumT, "attn_fwd")
    x2 = _outproj_fwd(x1, c, o, norms["out_norm_conv"], norms["out_norm_attn"], wout, "outproj_fwd")
    w13_2, w2_2 = as13(g13_2), g2_2.reshape(-1, D)
    G2, U2, A2 = _ffn_up(x2, norms["ffn2_norm"], w13_2, "ffn2_up")
    loss, dx3, d_final = _ffn_down_loss(x2, A2, w2_2, norms["final_norm"], target, "ffn2_down_loss")

    dw2_2 = _ffn_w2_grad(dx3, A2, "ffn2_w2_grad")
    dx2, d_ffn2n, h3, dG2, dU2 = _ffn_bwd_act(x2, norms["ffn2_norm"], dx3, G2, U2, w13_2, w2_2, "ffn2_bwd_act")
    dw13_2 = _ffn_w13_grad(h3, dG2, dU2, "ffn2_w13_grad")
    dc, dobT, deltaT, dwout, d_onc, d_ona = _outproj_bwd(
        dx2, c, o, norms["out_norm_conv"], norms["out_norm_attn"], wout, "outproj_bwd")
    red_ffn2 = _reduce_scatter([_blocks(dw13_2), _blocks(dw2_2)], "ffn2", (3, 4), add_after=(dc,))
    dqT, dkT, dvT, dcum = _attn_bwd(qT, k, kT, v, dobT, lseT, deltaT, cum, cumT, "attn_bwd",
                                    after=red_ffn2.partials)
    dfl, d_fb = _forget_bwd(dcum, fl, norms["forget_b"], "forget_bwd")
    dag, d_convw, d_cb, d_lg, d_lb = _conv_bwd(dc, yc, ag, conv_w32, norms["conv_ln_g"], norms["conv_ln_b"], "conv_bwd")
    dx1, d_mixn, h2 = _inproj_bwd_act(x1, norms["mix_norm"], dx2, dag, dqT, dkT, dvT, dfl, gin, "inproj_bwd_act")
    dw2_1 = _ffn_w2_grad(dx1, A1, "ffn1_w2_grad")
    early = [_blocks(dwout), _blocks(dw2_1)]
    sib_early = _seq_to_sibling(early, "rs_sibling_mix_early", 11, red_ffn2.reduced[:1])
    dwin_blocks = _inproj_bwd_weights(h2, dag, dqT, dkT, dvT, dfl, "inproj_bwd_weights")
    sib_w_in = _seq_to_sibling([dwin_blocks], "rs_sibling_mix", 5, sib_early[:1])
    mix_partials = _pair_add([dwin_blocks] + early, sib_w_in + sib_early, "rs_add_mix")
    red_mix = _Reduced(mix_partials, _seq_to_chips(mix_partials, "rs_chips_mix", 6))
    dx, d_ffn1n, h1, dG1, dU1 = _ffn_bwd_act(x, norms["ffn1_norm"], dx1, G1, U1, w13_1, w2_1, "ffn1_bwd_act",
                                             after=red_mix.partials)
    small = dict(ffn1_norm=d_ffn1n, mix_norm=d_mixn, conv_b=d_cb, conv_ln_g=d_lg, conv_ln_b=d_lb,
                 forget_b=d_fb, out_norm_conv=d_onc, out_norm_attn=d_ona, ffn2_norm=d_ffn2n,
                 final_norm=d_final, conv_w=d_convw, loss=loss)
    packed_small = _pack_small(small, "pack_small_grads")
    (gathered_small,) = _seq_all_gather([packed_small], "gather_small_grads", 9, after=red_mix.partials[0])
    dw13_1 = _ffn_w13_grad(h1, dG1, dU1, "ffn1_w13_grad")
    red_w13_1 = _reduce_scatter([_blocks(dw13_1)], "ffn1_w13", (7, 8), after=[red_mix.reduced[0], gathered_small])
    big = dict(ffn1_w13=red_w13_1.reduced[0], ffn1_w2=red_mix.reduced[2], w_in=red_mix.reduced[0],
               w_out=red_mix.reduced[1], ffn2_w13=red_ffn2.reduced[0], ffn2_w2=red_ffn2.reduced[1])
    return dx, gathered_small, big


_BIG = ("ffn1_w13", "ffn1_w2", "w_in", "w_out", "ffn2_w13", "ffn2_w2")
_TRANSPOSED = ("ffn1_w13", "ffn2_w13", "w_in")
_ORDER = ("ffn1_norm", "ffn1_w13", "ffn1_w2", "mix_norm", "w_in", "conv_w", "conv_b", "conv_ln_g", "conv_ln_b",
          "forget_b", "out_norm_conv", "out_norm_attn", "w_out", "ffn2_norm", "ffn2_w13", "ffn2_w2", "final_norm")


def kernel(x, ffn1_norm, ffn1_w13, ffn1_w2, mix_norm, w_in, conv_w, conv_b, conv_ln_g, conv_ln_b, forget_b, out_norm_conv, out_norm_attn, w_out, ffn2_norm, ffn2_w13, ffn2_w2, final_norm, loss_target, m_ffn1_norm, m_ffn1_w13, m_ffn1_w2, m_mix_norm, m_w_in, m_conv_w, m_conv_b, m_conv_ln_g, m_conv_ln_b, m_forget_b, m_out_norm_conv, m_out_norm_attn, m_w_out, m_ffn2_norm, m_ffn2_w13, m_ffn2_w2, m_final_norm, v_ffn1_norm, v_ffn1_w13, v_ffn1_w2, v_mix_norm, v_w_in, v_conv_w, v_conv_b, v_conv_ln_g, v_conv_ln_b, v_forget_b, v_out_norm_conv, v_out_norm_attn, v_w_out, v_ffn2_norm, v_ffn2_w13, v_ffn2_w2, v_final_norm):
    w = dict(ffn1_norm=ffn1_norm, ffn1_w13=ffn1_w13, ffn1_w2=ffn1_w2, mix_norm=mix_norm, w_in=w_in, conv_w=conv_w,
             conv_b=conv_b, conv_ln_g=conv_ln_g, conv_ln_b=conv_ln_b, forget_b=forget_b, out_norm_conv=out_norm_conv,
             out_norm_attn=out_norm_attn, w_out=w_out, ffn2_norm=ffn2_norm, ffn2_w13=ffn2_w13, ffn2_w2=ffn2_w2,
             final_norm=final_norm)
    m = dict(ffn1_norm=m_ffn1_norm, ffn1_w13=m_ffn1_w13, ffn1_w2=m_ffn1_w2, mix_norm=m_mix_norm, w_in=m_w_in,
             conv_w=m_conv_w, conv_b=m_conv_b, conv_ln_g=m_conv_ln_g, conv_ln_b=m_conv_ln_b, forget_b=m_forget_b,
             out_norm_conv=m_out_norm_conv, out_norm_attn=m_out_norm_attn, w_out=m_w_out, ffn2_norm=m_ffn2_norm,
             ffn2_w13=m_ffn2_w13, ffn2_w2=m_ffn2_w2, final_norm=m_final_norm)
    v = dict(ffn1_norm=v_ffn1_norm, ffn1_w13=v_ffn1_w13, ffn1_w2=v_ffn1_w2, mix_norm=v_mix_norm, w_in=v_w_in,
             conv_w=v_conv_w, conv_b=v_conv_b, conv_ln_g=v_conv_ln_g, conv_ln_b=v_conv_ln_b, forget_b=v_forget_b,
             out_norm_conv=v_out_norm_conv, out_norm_attn=v_out_norm_attn, w_out=v_w_out, ffn2_norm=v_ffn2_norm,
             ffn2_w13=v_ffn2_w13, ffn2_w2=v_ffn2_w2, final_norm=v_final_norm)
    shapes = {n: a.shape for n, a in w.items()}
    T, D = x.shape[1], x.shape[2]
    def two(n, a):
        if a.ndim != 3:
            return a.reshape(1, -1)
        a = a.reshape(a.shape[-2], a.shape[-1])
        return a.T if n in _TRANSPOSED else a

    w2d = {n: two(n, a) for n, a in w.items()}
    m2d = {n: two(n, a) for n, a in m.items()}
    v2d = {n: two(n, a) for n, a in v.items()}

    shard = {n: w2d[n].astype(MXU) for n in _BIG}
    shard["conv_w"] = w2d["conv_w"]
    norms = {n: w2d[n] for n in _SMALL}
    norms["forget_b"] = jnp.pad(w2d["forget_b"], ((0, 0), (0, LANES - N_HEADS)))
    dx, gathered_small, big = _local_step(x[0], loss_target[0], norms, shard)

    grads, deltas, new_m, new_v = {}, {}, {}, {}
    for n in _BIG:
        g, d, nm, nv = _adamw(w2d[n], m2d[n], v2d[n], big[n], "adamw_" + n)
        grads[n], deltas[n], new_m[n], new_v[n] = g, d, nm, nv

    small_out, conv_g_full, loss = _adamw_small(gathered_small, w2d, m2d, v2d, "adamw_small")
    for n in _SMALL:
        grads[n], deltas[n], new_m[n], new_v[n] = small_out[n]
    conv_g_full = conv_g_full[:CONV_TAPS]
    xi, yi, ci = _position()
    cw = shapes["conv_w"][-1]
    conv_g_mine = lax.dynamic_slice_in_dim(conv_g_full, _flat(xi, yi, ci) * cw, cw, axis=1)
    g, d, nm, nv = _adamw(w2d["conv_w"], m2d["conv_w"], v2d["conv_w"], conv_g_mine[None], "adamw_conv_w")
    grads["conv_w"], deltas["conv_w"], new_m["conv_w"], new_v["conv_w"] = g, d, nm, nv

    shaped = lambda dct: [(dct[n].T if n in _TRANSPOSED else dct[n]).reshape(shapes[n]) for n in _ORDER]
    return (loss[0, 0], dx[None], *shaped(grads), *shaped(deltas), *shaped(new_m), *shaped(new_v))
```
